```python
import math
import jax, jax.numpy as jnp
from jax import lax
import numpy as np

D_MODEL = 1024
BATCH = 8
SEQ = 8192
DEPTH = 1

D_POOL = D_MODEL
POOL_WINDOWS = (2, 4, 8, 16)
POOL_GROUPS = len(POOL_WINDOWS)
POOL_GROUP_WIDTH = D_POOL // POOL_GROUPS
D_SSM = D_MODEL
SSM_GROUP = 16
SSM_GROUPS = D_SSM // SSM_GROUP
SSM_STATE = 64
DT_MIN = 1e-3
DT_MAX = 1e-1
RMS_EPS = 1e-6
IN_SPLITS = (D_POOL, 2 * D_POOL, 2 * D_POOL + D_SSM, 2 * D_POOL + 2 * D_SSM,
             2 * D_POOL + 2 * D_SSM + D_MODEL)
IN_COLS = 2 * D_POOL + 2 * D_SSM + 2 * D_MODEL

kernel_name = 'hybrid_pool_s5_gated_block'


def rmsnorm(x, gain):
    xf = x.astype(jnp.float32)
    y = xf * lax.rsqrt(jnp.mean(xf * xf, axis=-1, keepdims=True) + RMS_EPS)
    return (y * gain.astype(jnp.float32)).astype(x.dtype)


def causal_pool_mixer(u, pool_w, pool_scale):
    bsz, seq, _ = u.shape
    uf = u.astype(jnp.float32)
    csum = jnp.cumsum(uf, axis=1)
    pos = jnp.arange(1, seq + 1, dtype=jnp.float32)[None, :, None]
    pooled = []
    for g, w in enumerate(POOL_WINDOWS):
        cs = csum[..., g * POOL_GROUP_WIDTH:(g + 1) * POOL_GROUP_WIDTH]
        lagged = jnp.pad(cs, ((0, 0), (w, 0), (0, 0)))[:, :seq]
        count = jnp.minimum(pos, float(w))
        pooled.append((cs - lagged) / count)
    pooled = (jnp.concatenate(pooled, axis=-1) - uf).astype(u.dtype)
    pooled = pooled.reshape(bsz, seq, POOL_GROUPS, POOL_GROUP_WIDTH)
    mixed = jnp.einsum('blgc,gcd->blgd', pooled, pool_w).reshape(bsz, seq, D_POOL)
    return mixed * pool_scale


def _complex_linear_combine(e1, e2):
    ar1, ai1, br1, bi1 = e1
    ar2, ai2, br2, bi2 = e2
    ar = ar2 * ar1 - ai2 * ai1
    ai = ar2 * ai1 + ai2 * ar1
    br = ar2 * br1 - ai2 * bi1 + br2
    bi = ar2 * bi1 + ai2 * br1 + bi2
    return (ar, ai, br, bi)


def s5_mixer(u, a_re, a_im, log_dt, b_re, b_im, c_re, c_im, d_skip, glu_w, glu_b):
    bsz, seq, _ = u.shape
    uf = u.astype(jnp.float32).reshape(bsz, seq, SSM_GROUPS, SSM_GROUP)
    dt = jnp.exp(log_dt.astype(jnp.float32))[:, None]
    lam_re = jnp.minimum(a_re.astype(jnp.float32), -1e-4)
    lam_im = a_im.astype(jnp.float32)
    mag = jnp.exp(lam_re * dt)
    abar_re = mag * jnp.cos(lam_im * dt)
    abar_im = mag * jnp.sin(lam_im * dt)
    den = lam_re * lam_re + lam_im * lam_im
    num_re = abar_re - 1.0
    f_re = (num_re * lam_re + abar_im * lam_im) / den
    f_im = (abar_im * lam_re - num_re * lam_im) / den
    f_re, f_im = f_re[:, :, None], f_im[:, :, None]
    bb_re = f_re * b_re - f_im * b_im
    bb_im = f_re * b_im + f_im * b_re
    bu_re = jnp.einsum('blgh,gph->blgp', uf, bb_re)
    bu_im = jnp.einsum('blgh,gph->blgp', uf, bb_im)
    a_seq_re = jnp.broadcast_to(abar_re[None, None], (1, seq, SSM_GROUPS, SSM_STATE))
    a_seq_im = jnp.broadcast_to(abar_im[None, None], (1, seq, SSM_GROUPS, SSM_STATE))
    _, _, s_re, s_im = lax.associative_scan(
        _complex_linear_combine, (a_seq_re, a_seq_im, bu_re, bu_im), axis=1)
    y = (jnp.einsum('blgp,ghp->blgh', s_re, c_re)
         - jnp.einsum('blgp,ghp->blgh', s_im, c_im)
         + d_skip * uf)
    y = jax.nn.gelu(y.reshape(bsz, seq, D_SSM).astype(u.dtype))
    return y * jax.nn.sigmoid(y @ glu_w + glu_b)


def hybrid_layer(x, c, w_ada, b_ada, norm_pre, norm_post, w_in, pool_w, pool_scale,
                 a_re, a_im, log_dt, b_re, b_im, c_re, c_im, d_skip, glu_w, glu_b,
                 w_branch_pool, w_branch_ssm, w_out):
    mod = jax.nn.silu(c) @ w_ada + b_ada
    shift, scale, gate = jnp.split(mod, 3, axis=-1)
    h = rmsnorm(x, norm_pre) * (1.0 + scale[:, None, :]) + shift[:, None, :]
    proj = h @ w_in
    u_pool, z_pool, u_ssm, z_ssm, g_pool, g_ssm = jnp.split(proj, list(IN_SPLITS), axis=-1)
    y_pool = causal_pool_mixer(u_pool, pool_w, pool_scale) * jax.nn.silu(z_pool)
    y_ssm = s5_mixer(u_ssm, a_re, a_im, log_dt, b_re, b_im, c_re, c_im, d_skip,
                     glu_w, glu_b) * jax.nn.silu(z_ssm)
    merged = (jax.nn.sigmoid(g_pool) * (y_pool @ w_branch_pool)
              + jax.nn.sigmoid(g_ssm) * (y_ssm @ w_branch_ssm))
    out = merged @ w_out
    return x + gate[:, None, :] * rmsnorm(out, norm_post)


def _fwd_setup_inputs(seed: int = 0) -> dict:
    key = jax.random.key(seed)
    ks = jax.random.split(key, 24)
    f32 = jnp.float32

    def nrm(k, shape, s):
        return jax.random.normal(k, shape, f32) * s

    G, P, H = SSM_GROUPS, SSM_STATE, SSM_GROUP
    n_idx = jnp.arange(P, dtype=f32)
    return {
        'x': nrm(ks[0], (BATCH, SEQ, D_MODEL), 1.0),
        'c': nrm(ks[1], (BATCH, D_MODEL), 1.0),
        'w_ada': nrm(ks[2], (DEPTH, D_MODEL, 3 * D_MODEL), 0.5 * D_MODEL ** -0.5),
        'b_ada': nrm(ks[3], (DEPTH, 3 * D_MODEL), 0.02),
        'norm_pre': 1.0 + nrm(ks[4], (DEPTH, D_MODEL), 0.02),
        'norm_post': 1.0 + nrm(ks[5], (DEPTH, D_MODEL), 0.02),
        'w_in': nrm(ks[6], (DEPTH, D_MODEL, IN_COLS), D_MODEL ** -0.5),
        'pool_w': nrm(ks[7], (DEPTH, POOL_GROUPS, POOL_GROUP_WIDTH, POOL_GROUP_WIDTH),
                      POOL_GROUP_WIDTH ** -0.5),
        'pool_scale': 1.0 + nrm(ks[8], (DEPTH, D_POOL), 0.02),
        'ssm_a_re': -0.5 + nrm(ks[9], (DEPTH, G, P), 0.01),
        'ssm_a_im': math.pi * n_idx + nrm(ks[10], (DEPTH, G, P), 0.01),
        'ssm_log_dt': jax.random.uniform(ks[11], (DEPTH, G), f32,
                                         math.log(DT_MIN), math.log(DT_MAX)),
        'ssm_b_re': nrm(ks[12], (DEPTH, G, P, H), (2 * H) ** -0.5),
        'ssm_b_im': nrm(ks[13], (DEPTH, G, P, H), (2 * H) ** -0.5),
        'ssm_c_re': nrm(ks[14], (DEPTH, G, H, P), P ** -0.5),
        'ssm_c_im': nrm(ks[15], (DEPTH, G, H, P), P ** -0.5),
        'ssm_d': nrm(ks[16], (DEPTH, G, H), 0.5),
        'glu_w': nrm(ks[17], (DEPTH, D_SSM, D_SSM), D_SSM ** -0.5),
        'glu_b': nrm(ks[18], (DEPTH, D_SSM), 0.02),
        'w_branch_pool': nrm(ks[19], (DEPTH, D_POOL, D_MODEL), D_POOL ** -0.5),
        'w_branch_ssm': nrm(ks[20], (DEPTH, D_SSM, D_MODEL), D_SSM ** -0.5),
        'w_out': nrm(ks[21], (DEPTH, D_MODEL, D_MODEL), D_MODEL ** -0.5),
    }


def _fwd_reference(x, c, w_ada, b_ada, norm_pre, norm_post, w_in, pool_w, pool_scale,
              ssm_a_re, ssm_a_im, ssm_log_dt, ssm_b_re, ssm_b_im, ssm_c_re, ssm_c_im,
              ssm_d, glu_w, glu_b, w_branch_pool, w_branch_ssm, w_out):
    for layer in range(DEPTH):
        x = hybrid_layer(x, c, w_ada[layer], b_ada[layer], norm_pre[layer], norm_post[layer],
                         w_in[layer], pool_w[layer], pool_scale[layer],
                         ssm_a_re[layer], ssm_a_im[layer], ssm_log_dt[layer],
                         ssm_b_re[layer], ssm_b_im[layer], ssm_c_re[layer], ssm_c_im[layer],
                         ssm_d[layer], glu_w[layer], glu_b[layer],
                         w_branch_pool[layer], w_branch_ssm[layer], w_out[layer])
    return x


import jax as _jax
import jax.numpy as _jnp

TWIN_FORMAT = 'train_step'
FWD_PARAMS = ['x', 'c', 'w_ada', 'b_ada', 'norm_pre', 'norm_post', 'w_in', 'pool_w', 'pool_scale', 'ssm_a_re', 'ssm_a_im', 'ssm_log_dt', 'ssm_b_re', 'ssm_b_im', 'ssm_c_re', 'ssm_c_im', 'ssm_d', 'glu_w', 'glu_b', 'w_branch_pool', 'w_branch_ssm', 'w_out']
TWIN_WEIGHTS = ['w_ada', 'b_ada', 'norm_pre', 'norm_post', 'w_in', 'pool_w', 'pool_scale', 'ssm_a_re', 'ssm_a_im', 'ssm_log_dt', 'ssm_b_re', 'ssm_b_im', 'ssm_c_re', 'ssm_c_im', 'ssm_d', 'glu_w', 'glu_b', 'w_branch_pool', 'w_branch_ssm', 'w_out']
TWIN_DIFF_INPUT = 'x'
TWIN_INPUTS = ['x', 'c', 'w_ada', 'b_ada', 'norm_pre', 'norm_post', 'w_in', 'pool_w', 'pool_scale', 'ssm_a_re', 'ssm_a_im', 'ssm_log_dt', 'ssm_b_re', 'ssm_b_im', 'ssm_c_re', 'ssm_c_im', 'ssm_d', 'glu_w', 'glu_b', 'w_branch_pool', 'w_branch_ssm', 'w_out', 'loss_target', 'm_w_ada', 'm_b_ada', 'm_norm_pre', 'm_norm_post', 'm_w_in', 'm_pool_w', 'm_pool_scale', 'm_ssm_a_re', 'm_ssm_a_im', 'm_ssm_log_dt', 'm_ssm_b_re', 'm_ssm_b_im', 'm_ssm_c_re', 'm_ssm_c_im', 'm_ssm_d', 'm_glu_w', 'm_glu_b', 'm_w_branch_pool', 'm_w_branch_ssm', 'm_w_out', 'v_w_ada', 'v_b_ada', 'v_norm_pre', 'v_norm_post', 'v_w_in', 'v_pool_w', 'v_pool_scale', 'v_ssm_a_re', 'v_ssm_a_im', 'v_ssm_log_dt', 'v_ssm_b_re', 'v_ssm_b_im', 'v_ssm_c_re', 'v_ssm_c_im', 'v_ssm_d', 'v_glu_w', 'v_glu_b', 'v_w_branch_pool', 'v_w_branch_ssm', 'v_w_out']
TWIN_OUTPUTS = ['loss', 'grad_x', 'grad_w_ada', 'grad_b_ada', 'grad_norm_pre', 'grad_norm_post', 'grad_w_in', 'grad_pool_w', 'grad_pool_scale', 'grad_ssm_a_re', 'grad_ssm_a_im', 'grad_ssm_log_dt', 'grad_ssm_b_re', 'grad_ssm_b_im', 'grad_ssm_c_re', 'grad_ssm_c_im', 'grad_ssm_d', 'grad_glu_w', 'grad_glu_b', 'grad_w_branch_pool', 'grad_w_branch_ssm', 'grad_w_out', 'delta_w_ada', 'delta_b_ada', 'delta_norm_pre', 'delta_norm_post', 'delta_w_in', 'delta_pool_w', 'delta_pool_scale', 'delta_ssm_a_re', 'delta_ssm_a_im', 'delta_ssm_log_dt', 'delta_ssm_b_re', 'delta_ssm_b_im', 'delta_ssm_c_re', 'delta_ssm_c_im', 'delta_ssm_d', 'delta_glu_w', 'delta_glu_b', 'delta_w_branch_pool', 'delta_w_branch_ssm', 'delta_w_out', 'new_m_w_ada', 'new_m_b_ada', 'new_m_norm_pre', 'new_m_norm_post', 'new_m_w_in', 'new_m_pool_w', 'new_m_pool_scale', 'new_m_ssm_a_re', 'new_m_ssm_a_im', 'new_m_ssm_log_dt', 'new_m_ssm_b_re', 'new_m_ssm_b_im', 'new_m_ssm_c_re', 'new_m_ssm_c_im', 'new_m_ssm_d', 'new_m_glu_w', 'new_m_glu_b', 'new_m_w_branch_pool', 'new_m_w_branch_ssm', 'new_m_w_out', 'new_v_w_ada', 'new_v_b_ada', 'new_v_norm_pre', 'new_v_norm_post', 'new_v_w_in', 'new_v_pool_w', 'new_v_pool_scale', 'new_v_ssm_a_re', 'new_v_ssm_a_im', 'new_v_ssm_log_dt', 'new_v_ssm_b_re', 'new_v_ssm_b_im', 'new_v_ssm_c_re', 'new_v_ssm_c_im', 'new_v_ssm_d', 'new_v_glu_w', 'new_v_glu_b', 'new_v_w_branch_pool', 'new_v_w_branch_ssm', 'new_v_w_out']
TWIN_LEAF_KINDS = {'loss': 'loss', 'grad_x': 'grad_x', 'grad_w_ada': 'grad_w', 'grad_b_ada': 'grad_w', 'grad_norm_pre': 'grad_w', 'grad_norm_post': 'grad_w', 'grad_w_in': 'grad_w', 'grad_pool_w': 'grad_w', 'grad_pool_scale': 'grad_w', 'grad_ssm_a_re': 'grad_w', 'grad_ssm_a_im': 'grad_w', 'grad_ssm_log_dt': 'grad_w', 'grad_ssm_b_re': 'grad_w', 'grad_ssm_b_im': 'grad_w', 'grad_ssm_c_re': 'grad_w', 'grad_ssm_c_im': 'grad_w', 'grad_ssm_d': 'grad_w', 'grad_glu_w': 'grad_w', 'grad_glu_b': 'grad_w', 'grad_w_branch_pool': 'grad_w', 'grad_w_branch_ssm': 'grad_w', 'grad_w_out': 'grad_w', 'delta_w_ada': 'delta_w', 'delta_b_ada': 'delta_w', 'delta_norm_pre': 'delta_w', 'delta_norm_post': 'delta_w', 'delta_w_in': 'delta_w', 'delta_pool_w': 'delta_w', 'delta_pool_scale': 'delta_w', 'delta_ssm_a_re': 'delta_w', 'delta_ssm_a_im': 'delta_w', 'delta_ssm_log_dt': 'delta_w', 'delta_ssm_b_re': 'delta_w', 'delta_ssm_b_im': 'delta_w', 'delta_ssm_c_re': 'delta_w', 'delta_ssm_c_im': 'delta_w', 'delta_ssm_d': 'delta_w', 'delta_glu_w': 'delta_w', 'delta_glu_b': 'delta_w', 'delta_w_branch_pool': 'delta_w', 'delta_w_branch_ssm': 'delta_w', 'delta_w_out': 'delta_w', 'new_m_w_ada': 'new_m', 'new_m_b_ada': 'new_m', 'new_m_norm_pre': 'new_m', 'new_m_norm_post': 'new_m', 'new_m_w_in': 'new_m', 'new_m_pool_w': 'new_m', 'new_m_pool_scale': 'new_m', 'new_m_ssm_a_re': 'new_m', 'new_m_ssm_a_im': 'new_m', 'new_m_ssm_log_dt': 'new_m', 'new_m_ssm_b_re': 'new_m', 'new_m_ssm_b_im': 'new_m', 'new_m_ssm_c_re': 'new_m', 'new_m_ssm_c_im': 'new_m', 'new_m_ssm_d': 'new_m', 'new_m_glu_w': 'new_m', 'new_m_glu_b': 'new_m', 'new_m_w_branch_pool': 'new_m', 'new_m_w_branch_ssm': 'new_m', 'new_m_w_out': 'new_m', 'new_v_w_ada': 'new_v', 'new_v_b_ada': 'new_v', 'new_v_norm_pre': 'new_v', 'new_v_norm_post': 'new_v', 'new_v_w_in': 'new_v', 'new_v_pool_w': 'new_v', 'new_v_pool_scale': 'new_v', 'new_v_ssm_a_re': 'new_v', 'new_v_ssm_a_im': 'new_v', 'new_v_ssm_log_dt': 'new_v', 'new_v_ssm_b_re': 'new_v', 'new_v_ssm_b_im': 'new_v', 'new_v_ssm_c_re': 'new_v', 'new_v_ssm_c_im': 'new_v', 'new_v_ssm_d': 'new_v', 'new_v_glu_w': 'new_v', 'new_v_glu_b': 'new_v', 'new_v_w_branch_pool': 'new_v', 'new_v_w_branch_ssm': 'new_v', 'new_v_w_out': 'new_v'}


def _forward(args):
    return _fwd_reference(*[args[k] for k in FWD_PARAMS])


def _output_shape():
    def fwd():
        inp = _fwd_setup_inputs(0)
        return _fwd_reference(*[inp[k] for k in FWD_PARAMS])
    out = _jax.eval_shape(fwd)
    return out.shape, out.dtype

N_MICROBATCH = 1
ADAM_LR = 0.001
ADAM_B1 = 0.9
ADAM_B2 = 0.999
ADAM_EPS = 1e-08
ADAM_WD = 0.01
ADAM_STEP = 10
PER_EXAMPLE_BATCH_AXIS = {'x': 0, 'c': 0, 'loss_target': 0}
SHARED_INPUTS = []
_WEIGHT_DTYPES = {'w_ada': _jnp.float32, 'b_ada': _jnp.float32, 'norm_pre': _jnp.float32, 'norm_post': _jnp.float32, 'w_in': _jnp.float32, 'pool_w': _jnp.float32, 'pool_scale': _jnp.float32, 'ssm_a_re': _jnp.float32, 'ssm_a_im': _jnp.float32, 'ssm_log_dt': _jnp.float32, 'ssm_b_re': _jnp.float32, 'ssm_b_im': _jnp.float32, 'ssm_c_re': _jnp.float32, 'ssm_c_im': _jnp.float32, 'ssm_d': _jnp.float32, 'glu_w': _jnp.float32, 'glu_b': _jnp.float32, 'w_branch_pool': _jnp.float32, 'w_branch_ssm': _jnp.float32, 'w_out': _jnp.float32}
MOMENT_SCALE = {'w_ada': 2.402460e+00, 'b_ada': 5.049299e+00, 'norm_pre': 1.737805e-01, 'norm_post': 6.663476e+00, 'w_in': 7.227452e-02, 'pool_w': 1.201788e-01, 'pool_scale': 1.239828e-01, 'ssm_a_re': 6.197236e-03, 'ssm_a_im': 5.079426e-03, 'ssm_log_dt': 3.932016e+00, 'ssm_b_re': 2.806893e-03, 'ssm_b_im': 2.589648e-03, 'ssm_c_re': 3.682445e-03, 'ssm_c_im': 3.739392e-03, 'ssm_d': 4.853146e-02, 'glu_w': 3.701372e-03, 'glu_b': 1.244258e-02, 'w_branch_pool': 1.301209e-01, 'w_branch_ssm': 2.405557e-02, 'w_out': 1.452312e-01}


def _to_microbatches(a, axis):
    t = _jnp.moveaxis(a, axis, 0)
    t = t.reshape((N_MICROBATCH, t.shape[0] // N_MICROBATCH) + t.shape[1:])
    return _jnp.moveaxis(t, 1, axis + 1)


def setup_inputs(seed: int = 0) -> dict:
    inp = _fwd_setup_inputs(seed)
    key = _jax.random.fold_in(_jax.random.key(seed), 7919)
    shape, _ = _output_shape()
    out = dict(inp)
    out["loss_target"] = _jax.random.normal(_jax.random.fold_in(key, 0), shape, _jnp.float32)
    for i, name in enumerate(TWIN_WEIGHTS):
        w = inp[name].astype(_jnp.float32)
        if MOMENT_SCALE is None:
            s = _jnp.sqrt(_jnp.mean(_jnp.square(w)) + 1e-30)
        else:
            s = MOMENT_SCALE[name]
        km, kv = _jax.random.split(_jax.random.fold_in(key, i + 1))
        out[name] = w
        out["m_" + name] = s * _jax.random.normal(km, w.shape, _jnp.float32)
        out["v_" + name] = (s * s) * _jax.random.uniform(kv, w.shape, _jnp.float32, 0.5, 1.5)
    if N_MICROBATCH > 1:
        for name, axis in PER_EXAMPLE_BATCH_AXIS.items():
            out[name] = _to_microbatches(out[name], axis)
    return {'x': out['x'], 'c': out['c'], 'w_ada': out['w_ada'], 'b_ada': out['b_ada'], 'norm_pre': out['norm_pre'], 'norm_post': out['norm_post'], 'w_in': out['w_in'], 'pool_w': out['pool_w'], 'pool_scale': out['pool_scale'], 'ssm_a_re': out['ssm_a_re'], 'ssm_a_im': out['ssm_a_im'], 'ssm_log_dt': out['ssm_log_dt'], 'ssm_b_re': out['ssm_b_re'], 'ssm_b_im': out['ssm_b_im'], 'ssm_c_re': out['ssm_c_re'], 'ssm_c_im': out['ssm_c_im'], 'ssm_d': out['ssm_d'], 'glu_w': out['glu_w'], 'glu_b': out['glu_b'], 'w_branch_pool': out['w_branch_pool'], 'w_branch_ssm': out['w_branch_ssm'], 'w_out': out['w_out'], 'loss_target': out['loss_target'], 'm_w_ada': out['m_w_ada'], 'm_b_ada': out['m_b_ada'], 'm_norm_pre': out['m_norm_pre'], 'm_norm_post': out['m_norm_post'], 'm_w_in': out['m_w_in'], 'm_pool_w': out['m_pool_w'], 'm_pool_scale': out['m_pool_scale'], 'm_ssm_a_re': out['m_ssm_a_re'], 'm_ssm_a_im': out['m_ssm_a_im'], 'm_ssm_log_dt': out['m_ssm_log_dt'], 'm_ssm_b_re': out['m_ssm_b_re'], 'm_ssm_b_im': out['m_ssm_b_im'], 'm_ssm_c_re': out['m_ssm_c_re'], 'm_ssm_c_im': out['m_ssm_c_im'], 'm_ssm_d': out['m_ssm_d'], 'm_glu_w': out['m_glu_w'], 'm_glu_b': out['m_glu_b'], 'm_w_branch_pool': out['m_w_branch_pool'], 'm_w_branch_ssm': out['m_w_branch_ssm'], 'm_w_out': out['m_w_out'], 'v_w_ada': out['v_w_ada'], 'v_b_ada': out['v_b_ada'], 'v_norm_pre': out['v_norm_pre'], 'v_norm_post': out['v_norm_post'], 'v_w_in': out['v_w_in'], 'v_pool_w': out['v_pool_w'], 'v_pool_scale': out['v_pool_scale'], 'v_ssm_a_re': out['v_ssm_a_re'], 'v_ssm_a_im': out['v_ssm_a_im'], 'v_ssm_log_dt': out['v_ssm_log_dt'], 'v_ssm_b_re': out['v_ssm_b_re'], 'v_ssm_b_im': out['v_ssm_b_im'], 'v_ssm_c_re': out['v_ssm_c_re'], 'v_ssm_c_im': out['v_ssm_c_im'], 'v_ssm_d': out['v_ssm_d'], 'v_glu_w': out['v_glu_w'], 'v_glu_b': out['v_glu_b'], 'v_w_branch_pool': out['v_w_branch_pool'], 'v_w_branch_ssm': out['v_w_branch_ssm'], 'v_w_out': out['v_w_out']}


def _loss(weights, diff, rest, loss_target):
    with _jax.named_scope("forward"):
        args = {**rest, TWIN_DIFF_INPUT: diff, **{k: w.astype(_WEIGHT_DTYPES[k]) for k, w in weights.items()}}
        y = _forward(args)
    with _jax.named_scope("loss_head"):
        err = _jnp.square(y.astype(_jnp.float32) - loss_target)
        return 0.5 * _jnp.sum(_jnp.mean(err, axis=-1)) if err.ndim else 0.5 * err


def _adamw(w, g, m, v):
    m = ADAM_B1 * m + (1.0 - ADAM_B1) * g
    v = ADAM_B2 * v + (1.0 - ADAM_B2) * _jnp.square(g)
    m_hat = m / (1.0 - ADAM_B1 ** ADAM_STEP)
    v_hat = v / (1.0 - ADAM_B2 ** ADAM_STEP)
    delta = -ADAM_LR * (m_hat / (_jnp.sqrt(v_hat) + ADAM_EPS) + ADAM_WD * w)
    return delta, m, v


def reference(x, c, w_ada, b_ada, norm_pre, norm_post, w_in, pool_w, pool_scale, ssm_a_re, ssm_a_im, ssm_log_dt, ssm_b_re, ssm_b_im, ssm_c_re, ssm_c_im, ssm_d, glu_w, glu_b, w_branch_pool, w_branch_ssm, w_out, loss_target, m_w_ada, m_b_ada, m_norm_pre, m_norm_post, m_w_in, m_pool_w, m_pool_scale, m_ssm_a_re, m_ssm_a_im, m_ssm_log_dt, m_ssm_b_re, m_ssm_b_im, m_ssm_c_re, m_ssm_c_im, m_ssm_d, m_glu_w, m_glu_b, m_w_branch_pool, m_w_branch_ssm, m_w_out, v_w_ada, v_b_ada, v_norm_pre, v_norm_post, v_w_in, v_pool_w, v_pool_scale, v_ssm_a_re, v_ssm_a_im, v_ssm_log_dt, v_ssm_b_re, v_ssm_b_im, v_ssm_c_re, v_ssm_c_im, v_ssm_d, v_glu_w, v_glu_b, v_w_branch_pool, v_w_branch_ssm, v_w_out):
    given = dict(x=x, c=c, w_ada=w_ada, b_ada=b_ada, norm_pre=norm_pre, norm_post=norm_post, w_in=w_in, pool_w=pool_w, pool_scale=pool_scale, ssm_a_re=ssm_a_re, ssm_a_im=ssm_a_im, ssm_log_dt=ssm_log_dt, ssm_b_re=ssm_b_re, ssm_b_im=ssm_b_im, ssm_c_re=ssm_c_re, ssm_c_im=ssm_c_im, ssm_d=ssm_d, glu_w=glu_w, glu_b=glu_b, w_branch_pool=w_branch_pool, w_branch_ssm=w_branch_ssm, w_out=w_out, loss_target=loss_target, m_w_ada=m_w_ada, m_b_ada=m_b_ada, m_norm_pre=m_norm_pre, m_norm_post=m_norm_post, m_w_in=m_w_in, m_pool_w=m_pool_w, m_pool_scale=m_pool_scale, m_ssm_a_re=m_ssm_a_re, m_ssm_a_im=m_ssm_a_im, m_ssm_log_dt=m_ssm_log_dt, m_ssm_b_re=m_ssm_b_re, m_ssm_b_im=m_ssm_b_im, m_ssm_c_re=m_ssm_c_re, m_ssm_c_im=m_ssm_c_im, m_ssm_d=m_ssm_d, m_glu_w=m_glu_w, m_glu_b=m_glu_b, m_w_branch_pool=m_w_branch_pool, m_w_branch_ssm=m_w_branch_ssm, m_w_out=m_w_out, v_w_ada=v_w_ada, v_b_ada=v_b_ada, v_norm_pre=v_norm_pre, v_norm_post=v_norm_post, v_w_in=v_w_in, v_pool_w=v_pool_w, v_pool_scale=v_pool_scale, v_ssm_a_re=v_ssm_a_re, v_ssm_a_im=v_ssm_a_im, v_ssm_log_dt=v_ssm_log_dt, v_ssm_b_re=v_ssm_b_re, v_ssm_b_im=v_ssm_b_im, v_ssm_c_re=v_ssm_c_re, v_ssm_c_im=v_ssm_c_im, v_ssm_d=v_ssm_d, v_glu_w=v_glu_w, v_glu_b=v_glu_b, v_w_branch_pool=v_w_branch_pool, v_w_branch_ssm=v_w_branch_ssm, v_w_out=v_w_out)
    weights = {n: given[n] for n in TWIN_WEIGHTS}
    shared = {n: given[n] for n in SHARED_INPUTS}
    per_example = {n: given[n] for n in ['x', 'c']}
    grad_fn = _jax.value_and_grad(_loss, argnums=(0, 1))

    def one_microbatch(ex, loss_target):
        ex = dict(ex)
        diff = ex.pop(TWIN_DIFF_INPUT)
        return grad_fn(weights, diff, {**shared, **ex}, loss_target)

    if N_MICROBATCH == 1:
        loss, (grad_w, grad_x) = one_microbatch(per_example, given["loss_target"])
    else:
        def body(carry, xs):
            loss_sum, grad_sum = carry
            l_k, (gw_k, gx_k) = one_microbatch(xs[0], xs[1])
            with _jax.named_scope("update"):
                return (loss_sum + l_k, _jax.tree.map(_jnp.add, grad_sum, gw_k)), gx_k

        init = (_jnp.zeros((), _jnp.float32), _jax.tree.map(_jnp.zeros_like, weights))
        (loss, grad_w), grad_x = _jax.lax.scan(body, init, (per_example, given["loss_target"]))
    with _jax.named_scope("update"):
        delta_w, new_m, new_v = {}, {}, {}
        for n in TWIN_WEIGHTS:
            delta_w[n], new_m[n], new_v[n] = _adamw(weights[n], grad_w[n], given["m_" + n], given["v_" + n])
    return (loss, grad_x, *[grad_w[n] for n in TWIN_WEIGHTS], *[delta_w[n] for n in TWIN_WEIGHTS],
            *[new_m[n] for n in TWIN_WEIGHTS], *[new_v[n] for n in TWIN_WEIGHTS])
```

```python
import functools
import math

import numpy as np
import jax
import jax.numpy as jnp
from jax import lax
from jax.experimental import pallas as pl
from jax.experimental.pallas import tpu as pltpu

F32 = jnp.float32
BF16 = jnp.bfloat16
MESH_ID = pl.DeviceIdType.MESH

D_MODEL = 1024
LANES = 128
SUBLANES = 8
SSM_G, SSM_P, SSM_H = 64, 64, 16
LANE_BLOCKS = D_MODEL // LANES
GROUPS_PER_BLOCK = LANES // SSM_H
STATE_W = GROUPS_PER_BLOCK * SSM_P
STATE_ALL = SSM_G * SSM_P
POOL_WINDOWS = (2, 4, 8, 16)
POOL_GW = D_MODEL // len(POOL_WINDOWS)
HALO = 16
RMS_EPS = 1e-6
N_CHIPS = 4
N_DEV = 8

SCAN_CHUNK = 512
ROW_CHUNK = 256
VMEM_LIMIT_BYTES = 56 * 1024 * 1024

ADAM_BLOCK_BYTES = 1 << 20
ADAM_LR, ADAM_B1, ADAM_B2, ADAM_EPS, ADAM_WD, ADAM_STEP = 0.001, 0.9, 0.999, 1e-08, 0.01, 10

_GELU_C0 = math.sqrt(2.0 / math.pi)
_GELU_C1 = 0.044715


def _cparams(*sem):
    if sem:
        return pltpu.CompilerParams(dimension_semantics=sem, vmem_limit_bytes=VMEM_LIMIT_BYTES)
    return pltpu.CompilerParams(vmem_limit_bytes=VMEM_LIMIT_BYTES)


def _sigmoid(v):
    return jax.nn.sigmoid(v)


def _silu(v):
    return v * _sigmoid(v)


def _dsilu(v):
    s = _sigmoid(v)
    return s * (1.0 + v * (1.0 - s))


def _gelu(v):
    return 0.5 * v * (1.0 + jnp.tanh(_GELU_C0 * (v + _GELU_C1 * v * v * v)))


def _dgelu(v):
    t = jnp.tanh(_GELU_C0 * (v + _GELU_C1 * v * v * v))
    return 0.5 * (1.0 + t) + 0.5 * v * (1.0 - t * t) * _GELU_C0 * (1.0 + 3.0 * _GELU_C1 * v * v)


def _dot(a, b):
    return lax.dot_general(a, b, (((1,), (0,)), ((), ())), preferred_element_type=F32)


def _dot_nt(a, b):
    return lax.dot_general(a, b, (((1,), (1,)), ((), ())), preferred_element_type=F32)


def _dot_tn(a, b):
    return lax.dot_general(a, b, (((0,), (0,)), ((), ())), preferred_element_type=F32)


def _acc8(v):
    return v.reshape(v.shape[0] // SUBLANES, SUBLANES, v.shape[1]).sum(axis=0)


def _mm(a_parts, b_parts, *, name, ta=False, tb=False, out_dtype=F32, bm=512, bn=512, bk=512):
    a_parts, b_parts = list(a_parts), list(b_parts)
    if ta:
        assert len(a_parts) == 1
        k_dim, m_dim = a_parts[0].shape
    else:
        m_dim = a_parts[0].shape[0]
        k_dim = sum(a.shape[1] for a in a_parts)
    if tb:
        assert len(b_parts) == 1
        n_dim = b_parts[0].shape[0]
    else:
        n_dim = sum(b.shape[1] for b in b_parts)
    bm, bn, bk = min(bm, m_dim), min(bn, n_dim), min(bk, k_dim)
    nm, nn, nk = m_dim // bm, n_dim // bn, k_dim // bk
    a_ranges, off = [], 0
    for a in a_parts:
        cnt = (a.shape[0] if ta else a.shape[1]) // bk
        a_ranges.append((off, cnt))
        off += cnt
    b_ranges, off = [], 0
    for b in b_parts:
        cnt = (b.shape[0] if tb else b.shape[1]) // bn
        b_ranges.append((off, cnt))
        off += cnt

    def a_spec(off, cnt):
        if ta:
            return pl.BlockSpec((bk, bm), lambda i, n, k: (k, i))
        return pl.BlockSpec((bm, bk), lambda i, n, k: (i, jnp.clip(k - off, 0, cnt - 1)))

    def b_spec(off, cnt):
        if tb:
            return pl.BlockSpec((bn, bk), lambda i, n, k: (n, k))
        return pl.BlockSpec((bk, bn), lambda i, n, k: (k, jnp.clip(n - off, 0, cnt - 1)))

    na, nb = len(a_parts), len(b_parts)
    dims = (((0 if ta else 1,), (1 if tb else 0,)), ((), ()))

    def kern(*refs):
        a_refs, b_refs = refs[:na], refs[na:na + nb]
        o_ref, acc = refs[na + nb], refs[na + nb + 1]
        n, k = pl.program_id(1), pl.program_id(2)

        @pl.when(k == 0)
        def _():
            acc[...] = jnp.zeros_like(acc)

        for ja, (koff, kcnt) in enumerate(a_ranges):
            for jb, (noff, ncnt) in enumerate(b_ranges):
                def step(ja=ja, jb=jb):
                    a = a_refs[ja][...].astype(BF16)
                    b = b_refs[jb][...].astype(BF16)
                    acc[...] += lax.dot_general(a, b, dims, preferred_element_type=F32)

                if na == 1 and nb == 1:
                    step()
                else:
                    cond = (k >= koff) & (k < koff + kcnt) & (n >= noff) & (n < noff + ncnt)
                    pl.when(cond)(step)

        @pl.when(k == nk - 1)
        def _():
            o_ref[...] = acc[...].astype(out_dtype)

    return pl.pallas_call(
        kern,
        name=name,
        grid=(nm, nn, nk),
        in_specs=[a_spec(*r) for r in a_ranges] + [b_spec(*r) for r in b_ranges],
        out_specs=pl.BlockSpec((bm, bn), lambda i, n, k: (i, n)),
        out_shape=jax.ShapeDtypeStruct((m_dim, n_dim), out_dtype),
        scratch_shapes=[pltpu.VMEM((bm, bn), F32)],
        compiler_params=_cparams("parallel", "parallel", "arbitrary"),
    )(*a_parts, *b_parts)


def _ssm_param_fn(a_re, a_im, log_dt, b_re, b_im):
    dt = jnp.exp(log_dt)
    lam_re = jnp.minimum(a_re, -1e-4)
    lam_im = a_im
    mag = jnp.exp(lam_re * dt)
    abar_re = mag * jnp.cos(lam_im * dt)
    abar_im = mag * jnp.sin(lam_im * dt)
    den = lam_re * lam_re + lam_im * lam_im
    num_re = abar_re - 1.0
    f_re = (num_re * lam_re + abar_im * lam_im) / den
    f_im = (abar_im * lam_re - num_re * lam_im) / den
    bb_re = f_re * b_re - f_im * b_im
    bb_im = f_re * b_im + f_im * b_re
    return abar_re, abar_im, bb_re, bb_im


def _ssm_params(a_re, a_im, log_dt, b_re_t, b_im_t):
    def kern(are, aim, ldt, bre, bim, o_ar, o_ai, o_br, o_bi):
        ar, ai, br, bi = _ssm_param_fn(are[...], aim[...], ldt[...], bre[...], bim[...])
        o_ar[...] = ar
        o_ai[...] = ai
        o_br[...] = br
        o_bi[...] = bi

    gp = jax.ShapeDtypeStruct((SSM_G, SSM_P), F32)
    hgp = jax.ShapeDtypeStruct((SSM_H, SSM_G, SSM_P), F32)
    return pl.pallas_call(kern, name="ssm_params", out_shape=(gp, gp, hgp, hgp), compiler_params=_cparams())(
        a_re, a_im, log_dt, b_re_t, b_im_t)


def _ssm_params_bwd(a_re, a_im, log_dt, b_re_t, b_im_t, d_ar, d_ai, d_bbr, d_bbi):
    def kern(are, aim, ldt, bre, bim, dar, dai, dbr, dbi, o_are, o_aim, o_ldt, o_bre, o_bim):
        prim = (are[...], aim[...], ldt[...], bre[...], bim[...])
        _, vjp = jax.vjp(_ssm_param_fn, *prim)
        g = vjp((dar[...], dai[...], dbr[...], dbi[...]))
        o_are[...] = g[0]
        o_aim[...] = g[1]
        o_ldt[...] = g[2]
        o_bre[...] = g[3]
        o_bim[...] = g[4]

    gp = jax.ShapeDtypeStruct((SSM_G, SSM_P), F32)
    g1 = jax.ShapeDtypeStruct((SSM_G, 1), F32)
    hgp = jax.ShapeDtypeStruct((SSM_H, SSM_G, SSM_P), F32)
    return pl.pallas_call(kern, name="ssm_params_bwd", out_shape=(gp, gp, g1, hgp, hgp), compiler_params=_cparams())(
        a_re, a_im, log_dt, b_re_t, b_im_t, d_ar, d_ai, d_bbr, d_bbi)


def _pow_tables(abar_re, abar_im, tc):
    ls = tc // SUBLANES

    def kern(ar_ref, ai_ref, fr_ref, fi_ref, rr_ref, ri_ref):
        a_re = jnp.broadcast_to(ar_ref[...], (SUBLANES, STATE_W))
        a_im = jnp.broadcast_to(ai_ref[...], (SUBLANES, STATE_W))
        p_re, p_im = a_re, a_im
        for i in range(ls):
            fwd = pl.ds(SUBLANES * i, SUBLANES)
            rev = pl.ds(SUBLANES * (ls - 1 - i), SUBLANES)
            fr_ref[fwd, :] = p_re
            fi_ref[fwd, :] = p_im
            rr_ref[rev, :] = p_re
            ri_ref[rev, :] = p_im
            p_re, p_im = p_re * a_re - p_im * a_im, p_re * a_im + p_im * a_re

    vec = pl.BlockSpec((1, STATE_W), lambda b: (0, b))
    tab = pl.BlockSpec((tc, STATE_W), lambda b: (0, b))
    shp = jax.ShapeDtypeStruct((tc, STATE_ALL), F32)
    return pl.pallas_call(
        kern, name="pow_tables", grid=(LANE_BLOCKS,), in_specs=[vec, vec], out_specs=(tab, tab, tab, tab),
        out_shape=(shp, shp, shp, shp), compiler_params=_cparams("parallel"))(abar_re, abar_im)


def _mod_kernel(c_row, w_ada_bf, b_ada):
    def kern(c_ref, w_ref, b_ref, m_ref, s_ref):
        cv = c_ref[...]
        sc = _silu(cv)
        s_ref[...] = sc
        lhs = jnp.broadcast_to(sc, (SUBLANES, D_MODEL)).astype(BF16)
        m_ref[...] = _dot(lhs, w_ref[...]) + b_ref[...]

    return pl.pallas_call(
        kern, name="ada_mod",
        out_shape=(jax.ShapeDtypeStruct((SUBLANES, 3 * D_MODEL), F32), jax.ShapeDtypeStruct((1, D_MODEL), F32)),
        compiler_params=_cparams())(c_row, w_ada_bf, b_ada)


def _row_spec(tr, width=D_MODEL, col=0):
    return pl.BlockSpec((tr, width), lambda c: (c, col))


def _vec_spec(width=D_MODEL):
    return pl.BlockSpec((1, width), lambda c: (0, 0))


def _in_norm(x, g1, scale, shift):
    seq = x.shape[0]
    tr = min(ROW_CHUNK, seq)

    def kern(x_ref, g_ref, sc_ref, sh_ref, h_ref):
        xv = x_ref[...]
        r = lax.rsqrt(jnp.mean(xv * xv, axis=-1, keepdims=True) + RMS_EPS)
        h_ref[...] = (((xv * r) * g_ref[...]) * (1.0 + sc_ref[...]) + sh_ref[...]).astype(BF16)

    return pl.pallas_call(
        kern, name="in_norm", grid=(seq // tr,),
        in_specs=[_row_spec(tr), _vec_spec(), _vec_spec(), _vec_spec()], out_specs=_row_spec(tr),
        out_shape=jax.ShapeDtypeStruct((seq, D_MODEL), BF16), compiler_params=_cparams("parallel"))(x, g1, scale, shift)


def _pool_windows(ext, pos, g, w, tr):
    cols = pl.ds(g * POOL_GW, POOL_GW)
    cur = ext[pl.ds(HALO, tr), cols]
    acc = cur
    for k in range(1, w):
        acc = acc + ext[pl.ds(HALO - k, tr), cols]
    cnt = jnp.minimum(pos + 1, w).astype(F32)
    return acc / cnt - cur


def _pool_fwd(proj, pool_w_bf, pscale):
    seq = proj.shape[0]
    tr = min(ROW_CHUNK, seq)
    hb = tr // HALO

    def kern(up_ref, halo_ref, zp_ref, pw_ref, ps_ref, y_ref, ext):
        c = pl.program_id(0)
        ext[0:HALO, :] = jnp.where(c > 0, halo_ref[...], 0.0)
        ext[HALO:, :] = up_ref[...]
        pos = c * tr + lax.broadcasted_iota(jnp.int32, (tr, POOL_GW), 0)
        for g, w in enumerate(POOL_WINDOWS):
            cols = pl.ds(g * POOL_GW, POOL_GW)
            pooled = _pool_windows(ext, pos, g, w, tr)
            mixed = _dot(pooled.astype(BF16), pw_ref[g])
            y_ref[:, cols] = (mixed * ps_ref[:, cols] * _silu(zp_ref[:, cols])).astype(BF16)

    return pl.pallas_call(
        kern, name="pool_fwd", grid=(seq // tr,),
        in_specs=[_row_spec(tr, col=0),
                  pl.BlockSpec((HALO, D_MODEL), lambda c: (jnp.maximum(c * hb - 1, 0), 0)),
                  _row_spec(tr, col=1),
                  pl.BlockSpec((len(POOL_WINDOWS), POOL_GW, POOL_GW), lambda c: (0, 0, 0)),
                  _vec_spec()],
        out_specs=_row_spec(tr), out_shape=jax.ShapeDtypeStruct((seq, D_MODEL), BF16),
        scratch_shapes=[pltpu.VMEM((tr + HALO, D_MODEL), F32)],
        compiler_params=_cparams("parallel"))(proj, proj, proj, pool_w_bf, pscale)


def _pool_bwd(proj, dyp, pool_w_bf, pscale):
    seq = proj.shape[0]
    tr = min(ROW_CHUNK, seq)
    hb = tr // HALO
    nc = seq // tr
    n_halo = seq // HALO

    def kern(up_ref, halo_ref, zp_ref, zpn_ref, dyp_ref, dypn_ref, pw_ref, ps_ref,
             d01_ref, dpw_ref, dps_ref, ext, dpn, acc_pw, acc_ps):
        c = pl.program_id(0)

        @pl.when(c == 0)
        def _():
            acc_pw[...] = jnp.zeros_like(acc_pw)
            acc_ps[...] = jnp.zeros_like(acc_ps)

        ext[0:HALO, :] = jnp.where(c > 0, halo_ref[...], 0.0)
        ext[HALO:, :] = up_ref[...]
        pos = c * tr + lax.broadcasted_iota(jnp.int32, (tr, POOL_GW), 0)
        pos_n = (c + 1) * tr + lax.broadcasted_iota(jnp.int32, (HALO, POOL_GW), 0)
        has_next = c < nc - 1
        for g, w in enumerate(POOL_WINDOWS):
            cols = pl.ds(g * POOL_GW, POOL_GW)
            pooled_bf = _pool_windows(ext, pos, g, w, tr).astype(BF16)
            wg = pw_ref[g]
            mixed = _dot(pooled_bf, wg)
            zp = zp_ref[:, cols]
            sz = _silu(zp)
            dyp_g = dyp_ref[:, cols]
            ps = ps_ref[:, cols]
            dmixed = (dyp_g * ps * sz).astype(BF16)
            acc_ps[:, cols] += _acc8(dyp_g * mixed * sz)
            d01_ref[:, pl.ds(D_MODEL + g * POOL_GW, POOL_GW)] = (dyp_g * mixed * ps * _dsilu(zp)).astype(BF16)
            acc_pw[g] += _dot_tn(pooled_bf, dmixed)
            dpooled = _dot_nt(dmixed, wg)
            dmixed_n = (jnp.where(has_next, dypn_ref[:, cols], 0.0) * ps * _silu(zpn_ref[:, cols])).astype(BF16)
            dpooled_n = _dot_nt(dmixed_n, wg)
            dpn[0:tr, :] = dpooled / jnp.minimum(pos + 1, w).astype(F32)
            dpn[tr:, :] = dpooled_n / jnp.minimum(pos_n + 1, w).astype(F32)
            acc = dpn[0:tr, :]
            for k in range(1, w):
                acc = acc + dpn[pl.ds(k, tr), :]
            d01_ref[:, cols] = (acc - dpooled).astype(BF16)

        @pl.when(c == nc - 1)
        def _():
            dpw_ref[...] = acc_pw[...]
            dps_ref[...] = jnp.sum(acc_ps[...], axis=0, keepdims=True)

    nxt = lambda c: (jnp.minimum((c + 1) * hb, n_halo - 1), 0)
    nxt1 = lambda c: (jnp.minimum((c + 1) * hb, n_halo - 1), 1)
    return pl.pallas_call(
        kern, name="pool_bwd", grid=(nc,),
        in_specs=[_row_spec(tr, col=0),
                  pl.BlockSpec((HALO, D_MODEL), lambda c: (jnp.maximum(c * hb - 1, 0), 0)),
                  _row_spec(tr, col=1),
                  pl.BlockSpec((HALO, D_MODEL), nxt1),
                  _row_spec(tr),
                  pl.BlockSpec((HALO, D_MODEL), nxt),
                  pl.BlockSpec((len(POOL_WINDOWS), POOL_GW, POOL_GW), lambda c: (0, 0, 0)),
                  _vec_spec()],
        out_specs=(pl.BlockSpec((tr, 2 * D_MODEL), lambda c: (c, 0)),
                   pl.BlockSpec((len(POOL_WINDOWS), POOL_GW, POOL_GW), lambda c: (0, 0, 0)),
                   _vec_spec()),
        out_shape=(jax.ShapeDtypeStruct((seq, 2 * D_MODEL), BF16),
                   jax.ShapeDtypeStruct((len(POOL_WINDOWS), POOL_GW, POOL_GW), F32),
                   jax.ShapeDtypeStruct((1, D_MODEL), F32)),
        scratch_shapes=[pltpu.VMEM((tr + HALO, D_MODEL), F32), pltpu.VMEM((tr + HALO, POOL_GW), F32),
                        pltpu.VMEM((len(POOL_WINDOWS), POOL_GW, POOL_GW), F32), pltpu.VMEM((SUBLANES, D_MODEL), F32)],
        compiler_params=_cparams("arbitrary"))(proj, proj, proj, proj, dyp, dyp, pool_w_bf, pscale)


def _glu_fwd(ys, proj, glu_w_bf, glu_b):
    seq = ys.shape[0]
    tr = min(ROW_CHUNK, seq)

    def kern(ys_ref, zs_ref, w_ref, b_ref, o_ref):
        yg = _gelu(ys_ref[...])
        q = _dot(yg.astype(BF16), w_ref[...]) + b_ref[...]
        o_ref[...] = (yg * _sigmoid(q) * _silu(zs_ref[...])).astype(BF16)

    return pl.pallas_call(
        kern, name="glu_fwd", grid=(seq // tr,),
        in_specs=[_row_spec(tr), _row_spec(tr, col=3), pl.BlockSpec((D_MODEL, D_MODEL), lambda c: (0, 0)), _vec_spec()],
        out_specs=_row_spec(tr), out_shape=jax.ShapeDtypeStruct((seq, D_MODEL), BF16),
        compiler_params=_cparams("parallel"))(ys, proj, glu_w_bf, glu_b)


def _glu_bwd(ys, proj, dyssm, glu_w_bf, glu_b):
    seq = ys.shape[0]
    tr = min(ROW_CHUNK, seq)
    nc = seq // tr

    def kern(ys_ref, zs_ref, dy_ref, w_ref, b_ref, dys_ref, dzs_ref, dq_ref, yg_ref, db_ref, acc_b):
        c = pl.program_id(0)

        @pl.when(c == 0)
        def _():
            acc_b[...] = jnp.zeros_like(acc_b)

        ysv = ys_ref[...]
        yg = _gelu(ysv)
        yg_bf = yg.astype(BF16)
        q = _dot(yg_bf, w_ref[...]) + b_ref[...]
        sg = _sigmoid(q)
        zs = zs_ref[...]
        dyv = dy_ref[...]
        dyglu = dyv * _silu(zs)
        dzs_ref[...] = (dyv * (yg * sg) * _dsilu(zs)).astype(BF16)
        dq = dyglu * yg * sg * (1.0 - sg)
        dq_bf = dq.astype(BF16)
        acc_b[...] += _acc8(dq)
        dyg = dyglu * sg + _dot_nt(dq_bf, w_ref[...])
        dys_ref[...] = dyg * _dgelu(ysv)
        dq_ref[...] = dq_bf
        yg_ref[...] = yg_bf

        @pl.when(c == nc - 1)
        def _():
            db_ref[...] = jnp.sum(acc_b[...], axis=0, keepdims=True)

    bf = jax.ShapeDtypeStruct((seq, D_MODEL), BF16)
    return pl.pallas_call(
        kern, name="glu_bwd", grid=(nc,),
        in_specs=[_row_spec(tr), _row_spec(tr, col=3), _row_spec(tr),
                  pl.BlockSpec((D_MODEL, D_MODEL), lambda c: (0, 0)), _vec_spec()],
        out_specs=(_row_spec(tr), _row_spec(tr), _row_spec(tr), _row_spec(tr), _vec_spec()),
        out_shape=(jax.ShapeDtypeStruct((seq, D_MODEL), F32), bf, bf, bf, jax.ShapeDtypeStruct((1, D_MODEL), F32)),
        scratch_shapes=[pltpu.VMEM((SUBLANES, D_MODEL), F32)],
        compiler_params=_cparams("arbitrary"))(ys, proj, dyssm, glu_w_bf, glu_b)


def _out_fwd_bwd(ypool, yssm, proj, x, tgt, gate, g2, wbp_bf, wbs_bf, wout_bf):
    seq = x.shape[0]
    tr = min(ROW_CHUNK, seq)
    nc = seq // tr

    def kern(yp_ref, ysm_ref, gp_ref, gs_ref, x_ref, t_ref, gate_ref, g2_ref, wbp_ref, wbs_ref, wo_ref,
             dy_ref, dyp_ref, dys_ref, d45_ref, mb_ref, dob_ref, dbp_ref, dbs_ref, loss_ref, dgate_ref, dg2_ref,
             acc_l, acc_gate, acc_g2):
        c = pl.program_id(0)

        @pl.when(c == 0)
        def _():
            acc_l[...] = jnp.zeros_like(acc_l)
            acc_gate[...] = jnp.zeros_like(acc_gate)
            acc_g2[...] = jnp.zeros_like(acc_g2)

        bp = _dot(yp_ref[...], wbp_ref[...])
        bs = _dot(ysm_ref[...], wbs_ref[...])
        sp = _sigmoid(gp_ref[...])
        ss = _sigmoid(gs_ref[...])
        mb = (sp * bp + ss * bs).astype(BF16)
        out = _dot(mb, wo_ref[...])
        r2 = lax.rsqrt(jnp.mean(out * out, axis=-1, keepdims=True) + RMS_EPS)
        oh = out * r2
        gate_v, g2_v = gate_ref[...], g2_ref[...]
        ohg = oh * g2_v
        diff = (x_ref[...] + gate_v * ohg) - t_ref[...]
        acc_l[...] += _acc8(diff * diff)
        dyv = diff * (1.0 / D_MODEL)
        dy_ref[...] = dyv
        acc_gate[...] += _acc8(dyv * ohg)
        t = dyv * gate_v
        acc_g2[...] += _acc8(t * oh)
        doh = t * g2_v
        dout = r2 * (doh - oh * jnp.mean(doh * oh, axis=-1, keepdims=True))
        dob = dout.astype(BF16)
        dmerged = _dot_nt(dob, wo_ref[...])
        dbp = (dmerged * sp).astype(BF16)
        dbs = (dmerged * ss).astype(BF16)
        d45_ref[:, 0:D_MODEL] = (dmerged * bp * sp * (1.0 - sp)).astype(BF16)
        d45_ref[:, D_MODEL:] = (dmerged * bs * ss * (1.0 - ss)).astype(BF16)
        dyp_ref[...] = _dot_nt(dbp, wbp_ref[...])
        dys_ref[...] = _dot_nt(dbs, wbs_ref[...])
        mb_ref[...] = mb
        dob_ref[...] = dob
        dbp_ref[...] = dbp
        dbs_ref[...] = dbs

        @pl.when(c == nc - 1)
        def _():
            tot = jnp.sum(acc_l[...], axis=0, keepdims=True)
            loss_ref[...] = jnp.sum(tot, axis=1, keepdims=True) * (0.5 / D_MODEL)
            dgate_ref[...] = jnp.sum(acc_gate[...], axis=0, keepdims=True)
            dg2_ref[...] = jnp.sum(acc_g2[...], axis=0, keepdims=True)

    wspec = pl.BlockSpec((D_MODEL, D_MODEL), lambda c: (0, 0))
    f32 = jax.ShapeDtypeStruct((seq, D_MODEL), F32)
    bf = jax.ShapeDtypeStruct((seq, D_MODEL), BF16)
    vec = jax.ShapeDtypeStruct((1, D_MODEL), F32)
    acc = pltpu.VMEM((SUBLANES, D_MODEL), F32)
    return pl.pallas_call(
        kern, name="out_fwd_bwd", grid=(nc,),
        in_specs=[_row_spec(tr), _row_spec(tr), _row_spec(tr, col=4), _row_spec(tr, col=5), _row_spec(tr), _row_spec(tr),
                  _vec_spec(), _vec_spec(), wspec, wspec, wspec],
        out_specs=(_row_spec(tr), _row_spec(tr), _row_spec(tr), pl.BlockSpec((tr, 2 * D_MODEL), lambda c: (c, 0)),
                   _row_spec(tr), _row_spec(tr), _row_spec(tr), _row_spec(tr),
                   pl.BlockSpec((1, 1), lambda c: (0, 0)), _vec_spec(), _vec_spec()),
        out_shape=(f32, f32, f32, jax.ShapeDtypeStruct((seq, 2 * D_MODEL), BF16), bf, bf, bf, bf,
                   jax.ShapeDtypeStruct((1, 1), F32), vec, vec),
        scratch_shapes=[acc, acc, acc],
        compiler_params=_cparams("arbitrary"))(ypool, yssm, proj, proj, x, tgt, gate, g2, wbp_bf, wbs_bf, wout_bf)


def _in_bwd(dh, x, dy, g1, scale):
    seq = x.shape[0]
    tr = min(ROW_CHUNK, seq)
    nc = seq // tr

    def kern(dh_ref, x_ref, dy_ref, g_ref, sc_ref, dx_ref, dsh_ref, dsc_ref, dg_ref, a_sh, a_sc, a_g):
        c = pl.program_id(0)

        @pl.when(c == 0)
        def _():
            a_sh[...] = jnp.zeros_like(a_sh)
            a_sc[...] = jnp.zeros_like(a_sc)
            a_g[...] = jnp.zeros_like(a_g)

        xv = x_ref[...]
        r = lax.rsqrt(jnp.mean(xv * xv, axis=-1, keepdims=True) + RMS_EPS)
        xh = xv * r
        g = g_ref[...]
        dhv = dh_ref[...]
        a_sh[...] += _acc8(dhv)
        a_sc[...] += _acc8(dhv * (xh * g))
        dn = dhv * (1.0 + sc_ref[...])
        a_g[...] += _acc8(dn * xh)
        dxh = dn * g
        dx_ref[...] = dy_ref[...] + r * (dxh - xh * jnp.mean(dxh * xh, axis=-1, keepdims=True))

        @pl.when(c == nc - 1)
        def _():
            dsh_ref[...] = jnp.sum(a_sh[...], axis=0, keepdims=True)
            dsc_ref[...] = jnp.sum(a_sc[...], axis=0, keepdims=True)
            dg_ref[...] = jnp.sum(a_g[...], axis=0, keepdims=True)

    vec = jax.ShapeDtypeStruct((1, D_MODEL), F32)
    acc = pltpu.VMEM((SUBLANES, D_MODEL), F32)
    return pl.pallas_call(
        kern, name="in_bwd", grid=(nc,),
        in_specs=[_row_spec(tr), _row_spec(tr), _row_spec(tr), _vec_spec(), _vec_spec()],
        out_specs=(_row_spec(tr), _vec_spec(), _vec_spec(), _vec_spec()),
        out_shape=(jax.ShapeDtypeStruct((seq, D_MODEL), F32), vec, vec, vec),
        scratch_shapes=[acc, acc, acc],
        compiler_params=_cparams("arbitrary"))(dh, x, dy, g1, scale)


def _perm_matrix(tc):
    ls = tc // SUBLANES
    r = np.arange(tc)
    m = np.zeros((tc, tc), np.float32)
    m[r, (r % SUBLANES) * ls + r // SUBLANES] = 1.0
    return m


def _local_scan(a_re, a_im, br, bi, xr, xi, row0, ls, reverse):
    x_re = jnp.zeros((SUBLANES, STATE_W), F32)
    x_im = jnp.zeros((SUBLANES, STATE_W), F32)
    for i in (range(ls - 1, -1, -1) if reverse else range(ls)):
        src = pl.ds(SUBLANES * i, SUBLANES)
        dst = pl.ds(row0 + SUBLANES * i, SUBLANES)
        n_re = a_re * x_re - a_im * x_im + br[src, :]
        n_im = a_re * x_im + a_im * x_re + bi[src, :]
        x_re, x_im = n_re, n_im
        xr[dst, :] = x_re
        xi[dst, :] = x_im
    return x_re, x_im


def _unpermute(pmt, v):
    hi = v.astype(BF16)
    lo = (v - hi.astype(F32)).astype(BF16)
    return _dot(pmt, hi) + _dot(pmt, lo)


def _ssm_scan_fwd(proj, pm, pmt, bb_re, bb_im, cm_re, cm_im, abar_re, abar_im, pw_re, pw_im, d_skip, tc):
    seq = proj.shape[0]
    nc = seq // tc
    ls = tc // SUBLANES
    us_col0 = 2 * D_MODEL // LANES

    def kern(us_ref, pm_ref, pmt_ref, bbr_ref, bbi_ref, cmr_ref, cmi_ref, ar_ref, ai_ref, pwr_ref, pwi_ref, d_ref,
             ys_ref, ecr_ref, eci_ref, bur, bui, car_r, car_i, end_r, end_i):
        c = pl.program_id(1)

        @pl.when(c == 0)
        def _():
            car_r[...] = jnp.zeros_like(car_r)
            car_i[...] = jnp.zeros_like(car_i)

        u = us_ref[...]
        up = _dot(pm_ref[...], u.astype(BF16)).astype(BF16)
        bur[...] = _dot(up, bbr_ref[0])
        bui[...] = _dot(up, bbi_ref[0])
        a_re = jnp.broadcast_to(ar_ref[...], (SUBLANES, STATE_W))
        a_im = jnp.broadcast_to(ai_ref[...], (SUBLANES, STATE_W))
        x_re, x_im = _local_scan(a_re, a_im, bur, bui, bur, bui, 0, ls, False)
        end_r[...] = x_re
        end_i[...] = x_im
        big_re = pwr_ref[tc - 1:tc, :]
        big_im = pwi_ref[tc - 1:tc, :]
        e_re = car_r[0:1, :]
        e_im = car_i[0:1, :]
        for s in range(SUBLANES):
            n_re = end_r[s:s + 1, :] + big_re * e_re - big_im * e_im
            n_im = end_i[s:s + 1, :] + big_re * e_im + big_im * e_re
            e_re, e_im = n_re, n_im
            if s < SUBLANES - 1:
                car_r[s + 1:s + 2, :] = e_re
                car_i[s + 1:s + 2, :] = e_im
        ec_re = car_r[...]
        ec_im = car_i[...]
        ecr_ref[...] = ec_re
        eci_ref[...] = ec_im
        p_re = pwr_ref[...].reshape(ls, SUBLANES, STATE_W)
        p_im = pwi_ref[...].reshape(ls, SUBLANES, STATE_W)
        xf_re = bur[...].reshape(ls, SUBLANES, STATE_W) + p_re * ec_re[None] - p_im * ec_im[None]
        xf_im = bui[...].reshape(ls, SUBLANES, STATE_W) + p_re * ec_im[None] + p_im * ec_re[None]
        xb_re = xf_re.reshape(tc, STATE_W).astype(BF16)
        xb_im = xf_im.reshape(tc, STATE_W).astype(BF16)
        y_perm = _dot(xb_re, cmr_ref[0]) - _dot(xb_im, cmi_ref[0])
        ys_ref[...] = _unpermute(pmt_ref[...], y_perm) + d_ref[...] * u
        car_r[0:1, :] = e_re
        car_i[0:1, :] = e_im

    sq = pl.BlockSpec((tc, tc), lambda b, c: (0, 0))
    vec = pl.BlockSpec((1, STATE_W), lambda b, c: (0, b))
    tab = pl.BlockSpec((tc, STATE_W), lambda b, c: (0, b))
    car = pl.BlockSpec((SUBLANES, STATE_W), lambda b, c: (c, b))
    carry_shape = jax.ShapeDtypeStruct((nc * SUBLANES, STATE_ALL), F32)
    small = pltpu.VMEM((SUBLANES, STATE_W), F32)
    big = pltpu.VMEM((tc, STATE_W), F32)
    return pl.pallas_call(
        kern, name="ssm_scan_fwd", grid=(LANE_BLOCKS, nc),
        in_specs=[pl.BlockSpec((tc, LANES), lambda b, c: (c, us_col0 + b)), sq, sq,
                  pl.BlockSpec((1, LANES, STATE_W), lambda b, c: (b, 0, 0)),
                  pl.BlockSpec((1, LANES, STATE_W), lambda b, c: (b, 0, 0)),
                  pl.BlockSpec((1, STATE_W, LANES), lambda b, c: (b, 0, 0)),
                  pl.BlockSpec((1, STATE_W, LANES), lambda b, c: (b, 0, 0)),
                  vec, vec, tab, tab, pl.BlockSpec((1, LANES), lambda b, c: (0, b))],
        out_specs=(pl.BlockSpec((tc, LANES), lambda b, c: (c, b)), car, car),
        out_shape=(jax.ShapeDtypeStruct((seq, D_MODEL), F32), carry_shape, carry_shape),
        scratch_shapes=[big, big, small, small, small, small],
        compiler_params=_cparams("parallel", "arbitrary"),
    )(proj, pm, pmt, bb_re, bb_im, cm_re, cm_im, abar_re, abar_im, pw_re, pw_im, d_skip)


def _ssm_scan_bwd(proj, dys, ec_re, ec_im, pm, pmt, bb_re, bb_im, cm_re, cm_im, abar_re, abar_im,
                  pw_re, pw_im, pv_re, pv_im, d_skip, tc):
    seq = proj.shape[0]
    nc = seq // tc
    ls = tc // SUBLANES
    us_col0 = 2 * D_MODEL // LANES

    def kern(us_ref, dys_ref, ecr_ref, eci_ref, pm_ref, pmt_ref, bbr_ref, bbi_ref, cmr_ref, cmi_ref, ar_ref, ai_ref,
             pwr_ref, pwi_ref, pvr_ref, pvi_ref, d_ref,
             dus_ref, dbbr_ref, dbbi_ref, dcmr_ref, dcmi_ref, dar_ref, dai_ref, dd_ref,
             bur, bui, xr, xi, gr, gi, fc_r, fc_i, a_bbr, a_bbi, a_cmr, a_cmi, a_ar, a_ai, a_dd):
        c = pl.program_id(1)

        @pl.when(c == 0)
        def _():
            for ref in (fc_r, fc_i, a_bbr, a_bbi, a_cmr, a_cmi, a_ar, a_ai, a_dd):
                ref[...] = jnp.zeros_like(ref)

        u = us_ref[...]
        dysv = dys_ref[...]
        a_dd[...] += _acc8(dysv * u)
        up = _dot(pm_ref[...], u.astype(BF16)).astype(BF16)
        bur[...] = _dot(up, bbr_ref[0])
        bui[...] = _dot(up, bbi_ref[0])
        a_re = jnp.broadcast_to(ar_ref[...], (SUBLANES, STATE_W))
        a_im = jnp.broadcast_to(ai_ref[...], (SUBLANES, STATE_W))
        _local_scan(a_re, a_im, bur, bui, xr, xi, SUBLANES, ls, False)
        ec_r = ecr_ref[...]
        ec_i = eci_ref[...]
        xr[0:SUBLANES, :] = ec_r
        xi[0:SUBLANES, :] = ec_i
        p_re = pwr_ref[...].reshape(ls, SUBLANES, STATE_W)
        p_im = pwi_ref[...].reshape(ls, SUBLANES, STATE_W)
        xl_re = xr[SUBLANES:, :].reshape(ls, SUBLANES, STATE_W)
        xl_im = xi[SUBLANES:, :].reshape(ls, SUBLANES, STATE_W)
        xf_re = (xl_re + p_re * ec_r[None] - p_im * ec_i[None]).reshape(tc, STATE_W)
        xf_im = (xl_im + p_re * ec_i[None] + p_im * ec_r[None]).reshape(tc, STATE_W)
        xr[SUBLANES:, :] = xf_re
        xi[SUBLANES:, :] = xf_im
        dysp = _dot(pm_ref[...], dysv.astype(BF16)).astype(BF16)
        a_cmr[...] += _dot_tn(xf_re.astype(BF16), dysp)
        a_cmi[...] -= _dot_tn(xf_im.astype(BF16), dysp)
        gr[...] = _dot_nt(dysp, cmr_ref[0])
        gi[...] = -_dot_nt(dysp, cmi_ref[0])
        _local_scan(a_re, -a_im, gr, gi, gr, gi, 0, ls, True)
        big_re = pwr_ref[tc - 1:tc, :]
        big_im = -pwi_ref[tc - 1:tc, :]
        f_re = fc_r[SUBLANES - 1:SUBLANES, :]
        f_im = fc_i[SUBLANES - 1:SUBLANES, :]
        for s in range(SUBLANES - 1, -1, -1):
            n_re = gr[s:s + 1, :] + big_re * f_re - big_im * f_im
            n_im = gi[s:s + 1, :] + big_re * f_im + big_im * f_re
            f_re, f_im = n_re, n_im
            if s > 0:
                fc_r[s - 1:s, :] = f_re
                fc_i[s - 1:s, :] = f_im
        fcv_r = fc_r[...]
        fcv_i = fc_i[...]
        q_re = pvr_ref[...].reshape(ls, SUBLANES, STATE_W)
        q_im = -pvi_ref[...].reshape(ls, SUBLANES, STATE_W)
        lam_re = (gr[...].reshape(ls, SUBLANES, STATE_W) + q_re * fcv_r[None] - q_im * fcv_i[None]).reshape(tc, STATE_W)
        lam_im = (gi[...].reshape(ls, SUBLANES, STATE_W) + q_re * fcv_i[None] + q_im * fcv_r[None]).reshape(tc, STATE_W)
        fc_r[SUBLANES - 1:SUBLANES, :] = f_re
        fc_i[SUBLANES - 1:SUBLANES, :] = f_im
        xp_re = xr[0:tc, :]
        xp_im = xi[0:tc, :]
        a_ar[...] += _acc8(lam_re * xp_re + lam_im * xp_im)
        a_ai[...] += _acc8(lam_im * xp_re - lam_re * xp_im)
        lb_re = lam_re.astype(BF16)
        lb_im = lam_im.astype(BF16)
        a_bbr[...] += _dot_tn(up, lb_re)
        a_bbi[...] += _dot_tn(up, lb_im)
        dus_perm = _dot_nt(lb_re, bbr_ref[0]) + _dot_nt(lb_im, bbi_ref[0])
        dus_ref[...] = _unpermute(pmt_ref[...], dus_perm) + dysv * d_ref[...]

        @pl.when(c == nc - 1)
        def _():
            dbbr_ref[0] = a_bbr[...]
            dbbi_ref[0] = a_bbi[...]
            dcmr_ref[0] = a_cmr[...]
            dcmi_ref[0] = a_cmi[...]
            dar_ref[...] = jnp.sum(a_ar[...], axis=0, keepdims=True)
            dai_ref[...] = jnp.sum(a_ai[...], axis=0, keepdims=True)
            dd_ref[...] = jnp.sum(a_dd[...], axis=0, keepdims=True)

    rc = lambda c: nc - 1 - c
    sq = pl.BlockSpec((tc, tc), lambda b, c: (0, 0))
    vec = pl.BlockSpec((1, STATE_W), lambda b, c: (0, b))
    tab = pl.BlockSpec((tc, STATE_W), lambda b, c: (0, b))
    car = pl.BlockSpec((SUBLANES, STATE_W), lambda b, c: (rc(c), b))
    bblk = pl.BlockSpec((1, LANES, STATE_W), lambda b, c: (b, 0, 0))
    cblk = pl.BlockSpec((1, STATE_W, LANES), lambda b, c: (b, 0, 0))
    dvec = pl.BlockSpec((1, LANES), lambda b, c: (0, b))
    small = pltpu.VMEM((SUBLANES, STATE_W), F32)
    big = pltpu.VMEM((tc, STATE_W), F32)
    bigp = pltpu.VMEM((tc + SUBLANES, STATE_W), F32)
    return pl.pallas_call(
        kern, name="ssm_scan_bwd", grid=(LANE_BLOCKS, nc),
        in_specs=[pl.BlockSpec((tc, LANES), lambda b, c: (rc(c), us_col0 + b)),
                  pl.BlockSpec((tc, LANES), lambda b, c: (rc(c), b)),
                  car, car, sq, sq, bblk, bblk, cblk, cblk, vec, vec, tab, tab, tab, tab, dvec],
        out_specs=(pl.BlockSpec((tc, LANES), lambda b, c: (rc(c), b)), bblk, bblk, cblk, cblk, vec, vec, dvec),
        out_shape=(jax.ShapeDtypeStruct((seq, D_MODEL), F32),
                   jax.ShapeDtypeStruct((LANE_BLOCKS, LANES, STATE_W), F32),
                   jax.ShapeDtypeStruct((LANE_BLOCKS, LANES, STATE_W), F32),
                   jax.ShapeDtypeStruct((LANE_BLOCKS, STATE_W, LANES), F32),
                   jax.ShapeDtypeStruct((LANE_BLOCKS, STATE_W, LANES), F32),
                   jax.ShapeDtypeStruct((1, STATE_ALL), F32), jax.ShapeDtypeStruct((1, STATE_ALL), F32),
                   jax.ShapeDtypeStruct((1, D_MODEL), F32)),
        scratch_shapes=[big, big, bigp, bigp, big, big, small, small,
                        pltpu.VMEM((LANES, STATE_W), F32), pltpu.VMEM((LANES, STATE_W), F32),
                        pltpu.VMEM((STATE_W, LANES), F32), pltpu.VMEM((STATE_W, LANES), F32),
                        small, small, pltpu.VMEM((SUBLANES, LANES), F32)],
        compiler_params=_cparams("parallel", "arbitrary"),
    )(proj, dys, ec_re, ec_im, pm, pmt, bb_re, bb_im, cm_re, cm_im, abar_re, abar_im, pw_re, pw_im, pv_re, pv_im, d_skip)


def _eye5():
    return jnp.eye(GROUPS_PER_BLOCK, dtype=F32)[None, :, None, :, None]


def _embed_b(bb_t):
    t = bb_t.transpose(1, 0, 2).reshape(LANE_BLOCKS, GROUPS_PER_BLOCK, SSM_H, 1, SSM_P)
    return (t * _eye5()).reshape(LANE_BLOCKS, LANES, STATE_W)


def _extract_b(blk):
    t = blk.reshape(LANE_BLOCKS, GROUPS_PER_BLOCK, SSM_H, GROUPS_PER_BLOCK, SSM_P)
    return (t * _eye5()).sum(axis=3).reshape(SSM_G, SSM_H, SSM_P).transpose(1, 0, 2)


def _embed_c(c_ghp):
    t = c_ghp.transpose(0, 2, 1).reshape(LANE_BLOCKS, GROUPS_PER_BLOCK, SSM_P, 1, SSM_H)
    return (t * _eye5()).reshape(LANE_BLOCKS, STATE_W, LANES)


def _extract_c(blk):
    t = blk.reshape(LANE_BLOCKS, GROUPS_PER_BLOCK, SSM_P, GROUPS_PER_BLOCK, SSM_H)
    return (t * _eye5()).sum(axis=3).reshape(SSM_G, SSM_P, SSM_H).transpose(0, 2, 1)


def _local_step(x, c_row, tgt, w_ada_bf, b_ada, g1, g2, w_in_bf, pool_w_bf, pscale, a_re, a_im, log_dt,
                b_re_t, b_im_t, c_re, c_im, d_skip, glu_w_bf, glu_b, wbp_bf, wbs_bf, wout_bf):
    seq = x.shape[0]
    tc = min(SCAN_CHUNK, seq)
    mod8, silu_c = _mod_kernel(c_row, w_ada_bf, b_ada)
    mod = mod8[0:1]
    shift, scale, gate = mod[:, 0:D_MODEL], mod[:, D_MODEL:2 * D_MODEL], mod[:, 2 * D_MODEL:]

    abar_re, abar_im, bb_re_t, bb_im_t = _ssm_params(a_re, a_im, log_dt, b_re_t, b_im_t)
    abar_re_f, abar_im_f = abar_re.reshape(1, STATE_ALL), abar_im.reshape(1, STATE_ALL)
    pw_re, pw_im, pv_re, pv_im = _pow_tables(abar_re_f, abar_im_f, tc)
    bbe_re, bbe_im = _embed_b(bb_re_t).astype(BF16), _embed_b(bb_im_t).astype(BF16)
    cme_re, cme_im = _embed_c(c_re).astype(BF16), _embed_c(c_im).astype(BF16)
    pm_np = _perm_matrix(tc)
    pm, pmt = jnp.asarray(pm_np, BF16), jnp.asarray(pm_np.T, BF16)
    d_row = d_skip.reshape(1, D_MODEL)

    h = _in_norm(x, g1, scale, shift)
    proj = _mm([h], [w_in_bf], name="proj", bn=1024, bk=1024)
    ypool = _pool_fwd(proj, pool_w_bf, pscale)
    ys, ec_re, ec_im = _ssm_scan_fwd(proj, pm, pmt, bbe_re, bbe_im, cme_re, cme_im, abar_re_f, abar_im_f,
                                      pw_re, pw_im, d_row, tc)
    yssm = _glu_fwd(ys, proj, glu_w_bf, glu_b)
    (dy, dypool, dyssm, d45, mb, dob, dbp, dbs, loss, dgate, dg2) = _out_fwd_bwd(
        ypool, yssm, proj, x, tgt, gate, g2, wbp_bf, wbs_bf, wout_bf)

    d_wout = _mm([mb], [dob], ta=True, name="dw_out", bm=1024, bk=512)
    d_wbp = _mm([ypool], [dbp], ta=True, name="dw_bp", bm=1024, bk=512)
    d_wbs = _mm([yssm], [dbs], ta=True, name="dw_bs", bm=1024, bk=512)
    dys, dzs, dq, yg, d_glu_b = _glu_bwd(ys, proj, dyssm, glu_w_bf, glu_b)
    d_glu_w = _mm([yg], [dq], ta=True, name="dw_glu", bm=1024, bk=512)
    (dus, dbbe_re, dbbe_im, dcme_re, dcme_im, d_abar_re, d_abar_im, d_dskip) = _ssm_scan_bwd(
        proj, dys, ec_re, ec_im, pm, pmt, bbe_re, bbe_im, cme_re, cme_im, abar_re_f, abar_im_f,
        pw_re, pw_im, pv_re, pv_im, d_row, tc)
    d01, d_pool_w, d_pscale = _pool_bwd(proj, dypool, pool_w_bf, pscale)
    dparts = [d01, dus, dzs, d45]
    dh = _mm(dparts, [w_in_bf], tb=True, name="dh", bn=1024, bk=512)
    d_win = _mm([h], dparts, ta=True, name="dw_in", bm=1024, bk=512)
    grad_x, dshift, dscale, dg1 = _in_bwd(dh, x, dy, g1, scale)
    dmod = jnp.concatenate([dshift, dscale, dgate], axis=1)
    return dict(
        loss=loss[0, 0], grad_x=grad_x, dmod=dmod, silu_c=silu_c, dg1=dg1, dg2=dg2, d_pscale=d_pscale,
        d_glu_b=d_glu_b, d_dskip=d_dskip, d_abar_re=d_abar_re, d_abar_im=d_abar_im,
        d_bb_re_t=_extract_b(dbbe_re), d_bb_im_t=_extract_b(dbbe_im),
        d_c_re=_extract_c(dcme_re), d_c_im=_extract_c(dcme_im),
        d_win=d_win, d_glu_w=d_glu_w, d_wbp=d_wbp, d_wbs=d_wbs, d_wout=d_wout, d_pool_w=d_pool_w)


def _position():
    x, y, c = lax.axis_index("x"), lax.axis_index("y"), lax.axis_index("c")
    chips = [(1 - x, y), (x, 1 - y), (1 - x, 1 - y)]
    return x, y, c, chips


_ANY = pl.BlockSpec(memory_space=pl.ANY)


def _ag_weights(packed):
    rows, width = packed.shape
    half = rows // 2

    def body(p_ref, out_ref, send_sems, recv_sems, local_sem):
        x, y, c, chips = _position()
        sibling = (x, y, 1 - c)

        def copy(k, chip, h, to, src=None):
            dst = out_ref.at[2 * chip[0] + chip[1], pl.ds(h * half, half), :]
            return pltpu.make_async_remote_copy(
                src_ref=dst if src is None else src, dst_ref=dst, send_sem=send_sems.at[k], recv_sem=recv_sems.at[k],
                device_id=to, device_id_type=MESH_ID)

        mine = pltpu.make_async_copy(p_ref, out_ref.at[2 * x + y], local_sem)
        mine.start()
        first = [copy(j, (x, y), c, (*chip, c), src=p_ref.at[pl.ds(c * half, half), :]) for j, chip in enumerate(chips)]
        for cp in first:
            cp.start()
        passed = [copy(3 + j, chip, c, sibling) for j, chip in enumerate(chips)]
        for j, chip in enumerate(chips):
            copy(j, chip, c, (x, y, c)).wait_recv()
            passed[j].start()
        for j, chip in enumerate(chips):
            copy(3 + j, chip, 1 - c, (x, y, c)).wait_recv()
        for cp in first + passed:
            cp.wait_send()
        mine.wait()

    return pl.pallas_call(
        body, name="ag_weights", in_specs=[_ANY], out_specs=_ANY,
        out_shape=jax.ShapeDtypeStruct((N_CHIPS, rows, width), packed.dtype),
        scratch_shapes=[pltpu.SemaphoreType.DMA((6,)), pltpu.SemaphoreType.DMA((6,)), pltpu.SemaphoreType.DMA],
    )(packed)


def _small_allgather_sum(buf):
    rows, width = buf.shape

    def body(b_ref, all_ref, sum_ref, send_sems, recv_sems, local_sem):
        x, y, c, chips = _position()
        me, sibling = (x, y, c), (x, y, 1 - c)

        def slot(px, py, pc):
            return all_ref.at[4 * px + 2 * py + pc]

        def copy(k, block, to, src=None):
            return pltpu.make_async_remote_copy(
                src_ref=slot(*block) if src is None else src, dst_ref=slot(*block), send_sem=send_sems.at[k],
                recv_sem=recv_sems.at[k], device_id=to, device_id_type=MESH_ID)

        mine = pltpu.make_async_copy(b_ref, slot(*me), local_sem)
        mine.start()
        first = [copy(0, me, sibling, src=b_ref)]
        first += [copy(1 + j, me, (*chip, c), src=b_ref) for j, chip in enumerate(chips)]
        for cp in first:
            cp.start()
        passed = [copy(4 + j, (*chip, c), sibling) for j, chip in enumerate(chips)]
        for j, chip in enumerate(chips):
            copy(1 + j, (*chip, c), me).wait_recv()
            passed[j].start()
        copy(0, sibling, me).wait_recv()
        for j, chip in enumerate(chips):
            copy(4 + j, (*chip, 1 - c), me).wait_recv()
        for cp in first + passed:
            cp.wait_send()
        mine.wait()
        total = all_ref[0]
        for d in range(1, N_DEV):
            total = total + all_ref[d]
        sum_ref[...] = total

    vm = pl.BlockSpec(memory_space=pltpu.VMEM)
    return pl.pallas_call(
        body, name="small_allgather_sum", in_specs=[vm], out_specs=(vm, vm),
        out_shape=(jax.ShapeDtypeStruct((N_DEV, rows, width), F32), jax.ShapeDtypeStruct((rows, width), F32)),
        scratch_shapes=[pltpu.SemaphoreType.DMA((7,)), pltpu.SemaphoreType.DMA((7,)), pltpu.SemaphoreType.DMA],
        compiler_params=_cparams(),
    )(buf)


def _rs_pair(g):
    n, rows, width = g.shape
    half = rows // 2

    def body(g_ref, own_ref, got_ref, send_sem, recv_sem, local_sem):
        x, y, c, _ = _position()
        keep = pltpu.make_async_copy(g_ref.at[:, pl.ds(c * half, half), :], own_ref, local_sem)
        keep.start()
        swap = pltpu.make_async_remote_copy(
            src_ref=g_ref.at[:, pl.ds((1 - c) * half, half), :], dst_ref=got_ref, send_sem=send_sem, recv_sem=recv_sem,
            device_id=(x, y, 1 - c), device_id_type=MESH_ID)
        swap.start()
        swap.wait()
        keep.wait()

    shp = jax.ShapeDtypeStruct((n, half, width), g.dtype)
    return pl.pallas_call(
        body, name="rs_pair", in_specs=[_ANY], out_specs=(_ANY, _ANY), out_shape=(shp, shp),
        scratch_shapes=[pltpu.SemaphoreType.DMA, pltpu.SemaphoreType.DMA, pltpu.SemaphoreType.DMA],
    )(g)


def _rs_chips(part_f32, part_bf):
    n, rows, width = part_f32.shape

    def body(pf_ref, pb_ref, own_ref, got_ref, send_sems, recv_sems, local_sem):
        x, y, c, chips = _position()
        keep = pltpu.make_async_copy(pf_ref.at[2 * x + y], own_ref, local_sem)
        keep.start()
        sends = [pltpu.make_async_remote_copy(
            src_ref=pb_ref.at[2 * chip[0] + chip[1]], dst_ref=got_ref.at[j], send_sem=send_sems.at[j],
            recv_sem=recv_sems.at[j], device_id=(*chip, c), device_id_type=MESH_ID) for j, chip in enumerate(chips)]
        for cp in sends:
            cp.start()
        for cp in sends:
            cp.wait()
        keep.wait()

    return pl.pallas_call(
        body, name="rs_chips", in_specs=[_ANY, _ANY], out_specs=(_ANY, _ANY),
        out_shape=(jax.ShapeDtypeStruct((rows, width), F32), jax.ShapeDtypeStruct((N_CHIPS - 1, rows, width), BF16)),
        scratch_shapes=[pltpu.SemaphoreType.DMA((3,)), pltpu.SemaphoreType.DMA((3,)), pltpu.SemaphoreType.DMA],
    )(part_f32, part_bf)


def _rs_join(fin_half):
    half, width = fin_half.shape

    def body(h_ref, out_ref, send_sem, recv_sem, local_sem):
        x, y, c, _ = _position()
        keep = pltpu.make_async_copy(h_ref, out_ref.at[pl.ds(c * half, half), :], local_sem)
        keep.start()
        swap = pltpu.make_async_remote_copy(
            src_ref=h_ref, dst_ref=out_ref.at[pl.ds(c * half, half), :], send_sem=send_sem, recv_sem=recv_sem,
            device_id=(x, y, 1 - c), device_id_type=MESH_ID)
        swap.start()
        swap.wait()
        keep.wait()

    return pl.pallas_call(
        body, name="rs_join", in_specs=[_ANY], out_specs=_ANY,
        out_shape=jax.ShapeDtypeStruct((2 * half, width), fin_half.dtype),
        scratch_shapes=[pltpu.SemaphoreType.DMA, pltpu.SemaphoreType.DMA, pltpu.SemaphoreType.DMA],
    )(fin_half)


def _pair_add(own, got):
    n, rows, width = own.shape
    rb = rows // 2 if (rows // 2) % 16 == 0 else rows
    spec = pl.BlockSpec((1, rb, width), lambda k, i: (k, i, 0))

    def kern(a_ref, b_ref, f_ref, h_ref):
        s = a_ref[...] + b_ref[...]
        f_ref[...] = s
        h_ref[...] = s.astype(BF16)

    return pl.pallas_call(
        kern, name="rs_pair_add", grid=(n, rows // rb), in_specs=[spec, spec], out_specs=(spec, spec),
        out_shape=(jax.ShapeDtypeStruct(own.shape, F32), jax.ShapeDtypeStruct(own.shape, BF16)),
        compiler_params=_cparams("parallel", "parallel"))(own, got)


def _chip_add(own, got):
    rows, width = own.shape
    rb = rows // 2 if (rows // 2) % 16 == 0 else rows

    def kern(a_ref, b_ref, o_ref):
        o_ref[...] = ((a_ref[...] + b_ref[0].astype(F32)) + b_ref[1].astype(F32)) + b_ref[2].astype(F32)

    return pl.pallas_call(
        kern, name="rs_chip_add", grid=(rows // rb,),
        in_specs=[pl.BlockSpec((rb, width), lambda i: (i, 0)), pl.BlockSpec((N_CHIPS - 1, rb, width), lambda i: (0, i, 0))],
        out_specs=pl.BlockSpec((rb, width), lambda i: (i, 0)), out_shape=jax.ShapeDtypeStruct((rows, width), F32),
        compiler_params=_cparams("parallel"))(own, got)


def _adamw(w, g, m, v, name):
    rows, width = w.shape
    rb = rows
    for cand in (512, 256, 128, 64, 32, 16, 8):
        if rows % cand == 0 and cand * width * 4 <= ADAM_BLOCK_BYTES:
            rb = cand
            break
    spec = pl.BlockSpec((rb, width), lambda i: (i, 0))

    def kern(w_ref, g_ref, m_ref, v_ref, d_ref, nm_ref, nv_ref):
        gv = g_ref[...]
        nm = ADAM_B1 * m_ref[...] + (1.0 - ADAM_B1) * gv
        nv = ADAM_B2 * v_ref[...] + (1.0 - ADAM_B2) * (gv * gv)
        m_hat = nm / (1.0 - ADAM_B1 ** ADAM_STEP)
        v_hat = nv / (1.0 - ADAM_B2 ** ADAM_STEP)
        d_ref[...] = -ADAM_LR * (m_hat / (jnp.sqrt(v_hat) + ADAM_EPS) + ADAM_WD * w_ref[...])
        nm_ref[...] = nm
        nv_ref[...] = nv

    shp = jax.ShapeDtypeStruct(w.shape, F32)
    return pl.pallas_call(
        kern, name=name, grid=(rows // rb,), in_specs=[spec] * 4, out_specs=(spec, spec, spec),
        out_shape=(shp, shp, shp), compiler_params=_cparams("parallel"))(w, g, m, v)


def _wada_grad(silu_t, dmod_cols):
    n = dmod_cols.shape[1]

    def kern(s_ref, d_ref, o_ref):
        acc = s_ref[:, 0:1] * d_ref[0:1, :]
        for b in range(1, N_DEV):
            acc = acc + s_ref[:, b:b + 1] * d_ref[b:b + 1, :]
        o_ref[...] = acc

    return pl.pallas_call(kern, name="wada_grad", out_shape=jax.ShapeDtypeStruct((D_MODEL, n), F32),
                          compiler_params=_cparams())(silu_t, dmod_cols)


def _rows(a, multiple):
    flat = a.reshape(-1)
    pad = (-flat.shape[0]) % (D_MODEL * multiple)
    if pad:
        flat = jnp.concatenate([flat, jnp.zeros((pad,), flat.dtype)])
    return flat.reshape(-1, D_MODEL)


def _part_rows(shape, multiple):
    return -(-int(np.prod(shape)) // (D_MODEL * multiple)) * multiple


def _pack_rows(parts, multiple):
    return jnp.concatenate([_rows(p, multiple) for p in parts], axis=0)


def _unpack_rows(buf, shapes, multiple):
    out, r = [], 0
    for shp in shapes:
        n = int(np.prod(shp))
        nr = _part_rows(shp, multiple)
        out.append(buf[r:r + nr].reshape(-1)[:n].reshape(shp))
        r += nr
    return out


def kernel(x, c, w_ada, b_ada, norm_pre, norm_post, w_in, pool_w, pool_scale, ssm_a_re, ssm_a_im, ssm_log_dt, ssm_b_re, ssm_b_im, ssm_c_re, ssm_c_im, ssm_d, glu_w, glu_b, w_branch_pool, w_branch_ssm, w_out, loss_target, m_w_ada, m_b_ada, m_norm_pre, m_norm_post, m_w_in, m_pool_w, m_pool_scale, m_ssm_a_re, m_ssm_a_im, m_ssm_log_dt, m_ssm_b_re, m_ssm_b_im, m_ssm_c_re, m_ssm_c_im, m_ssm_d, m_glu_w, m_glu_b, m_w_branch_pool, m_w_branch_ssm, m_w_out, v_w_ada, v_b_ada, v_norm_pre, v_norm_post, v_w_in, v_pool_w, v_pool_scale, v_ssm_a_re, v_ssm_a_im, v_ssm_log_dt, v_ssm_b_re, v_ssm_b_im, v_ssm_c_re, v_ssm_c_im, v_ssm_d, v_glu_w, v_glu_b, v_w_branch_pool, v_w_branch_ssm, v_w_out):
    n_ada = w_ada.shape[2]
    n_in = w_in.shape[2]
    n_row = glu_w.shape[1]
    n_pool = pool_w.shape[2]
    n_groups = pool_w.shape[1]

    big_shards = [w_ada[0], w_in[0], pool_w[0], glu_w[0], w_branch_pool[0], w_branch_ssm[0], w_out[0]]
    packed = _pack_rows([s.astype(BF16) for s in big_shards], 2 * SUBLANES)
    gathered = _ag_weights(packed)
    r = 0
    w_ada_bf = gathered[:, r:r + n_ada].reshape(N_CHIPS, D_MODEL, n_ada).transpose(1, 0, 2).reshape(D_MODEL, 3 * D_MODEL)
    r += n_ada
    w_in_bf = gathered[:, r:r + n_in].reshape(N_CHIPS, D_MODEL, n_in).transpose(1, 0, 2).reshape(D_MODEL, N_CHIPS * n_in)
    r += n_in
    pool_rows = n_groups * n_pool * POOL_GW // D_MODEL
    pool_w_bf = gathered[:, r:r + pool_rows].reshape(N_CHIPS, n_groups, n_pool, POOL_GW).transpose(1, 0, 2, 3)
    pool_w_bf = pool_w_bf.reshape(n_groups, POOL_GW, POOL_GW)
    r += pool_rows
    squares = []
    for _ in range(4):
        squares.append(gathered[:, r:r + n_row].reshape(D_MODEL, D_MODEL))
        r += n_row
    glu_w_bf, wbp_bf, wbs_bf, wout_bf = squares

    a_re, a_im, log_dt = ssm_a_re[0], ssm_a_im[0], ssm_log_dt[0].reshape(SSM_G, 1)
    b_re_t, b_im_t = ssm_b_re[0].transpose(2, 0, 1), ssm_b_im[0].transpose(2, 0, 1)
    res = _local_step(x[0], c, loss_target[0], w_ada_bf, b_ada, norm_pre, norm_post, w_in_bf, pool_w_bf, pool_scale,
                      a_re, a_im, log_dt, b_re_t, b_im_t, ssm_c_re[0], ssm_c_im[0], ssm_d[0], glu_w_bf, glu_b[0:1],
                      wbp_bf, wbs_bf, wout_bf)
    loss = lax.psum(res["loss"], ("x", "y", "c"))

    small_parts = [res["dmod"], res["silu_c"], res["dg1"], res["dg2"], res["d_pscale"], res["d_glu_b"], res["d_dskip"],
                   res["d_abar_re"], res["d_abar_im"], res["d_bb_re_t"], res["d_bb_im_t"], res["d_c_re"], res["d_c_im"]]
    small_shapes = [p.shape for p in small_parts]
    all_small, sum_small = _small_allgather_sum(_pack_rows(small_parts, SUBLANES))
    (g_b_ada, _, g_norm_pre, g_norm_post, g_pscale, g_glu_b, g_dskip, s_abar_re, s_abar_im, s_bb_re, s_bb_im,
     g_c_re, g_c_im) = _unpack_rows(sum_small, small_shapes, SUBLANES)
    g_a_re, g_a_im, g_log_dt, g_b_re_t, g_b_im_t = _ssm_params_bwd(
        a_re, a_im, log_dt, b_re_t, b_im_t, s_abar_re.reshape(SSM_G, SSM_P), s_abar_im.reshape(SSM_G, SSM_P),
        s_bb_re, s_bb_im)
    chip = 2 * lax.axis_index("x") + lax.axis_index("y")
    dmod_all = all_small[:, 0:3].reshape(N_DEV, 3 * D_MODEL)
    dmod_cols = lax.dynamic_slice_in_dim(dmod_all, chip * n_ada, n_ada, axis=1)
    silu_t = all_small[:, _part_rows(small_shapes[0], SUBLANES)].transpose(1, 0)
    g_w_ada = _wada_grad(silu_t, dmod_cols)

    def by_cols(a, n):
        return a.reshape(D_MODEL, N_CHIPS, n).transpose(1, 0, 2).reshape(N_CHIPS, -1, D_MODEL)

    def by_rows(a):
        return a.reshape(N_CHIPS, n_row, D_MODEL)

    pool_by_chip = res["d_pool_w"].reshape(n_groups, N_CHIPS, n_pool, POOL_GW).transpose(1, 0, 2, 3)
    g_packed = jnp.concatenate(
        [by_cols(res["d_win"], n_in), by_rows(res["d_glu_w"]), by_rows(res["d_wbp"]), by_rows(res["d_wbs"]),
         by_rows(res["d_wout"]), pool_by_chip.reshape(N_CHIPS, pool_rows, D_MODEL)], axis=1)
    own, got = _rs_pair(g_packed)
    part_f32, part_bf = _pair_add(own, got)
    own_chip, got_chips = _rs_chips(part_f32, part_bf)
    shard = _rs_join(_chip_add(own_chip, got_chips))
    r = 0
    g_w_in = shard[r:r + n_in].reshape(D_MODEL, n_in)
    r += n_in
    g_squares = []
    for _ in range(4):
        g_squares.append(shard[r:r + n_row])
        r += n_row
    g_glu_w, g_wbp, g_wbs, g_wout = g_squares
    g_pool_w = shard[r:r + pool_rows].reshape(n_groups * n_pool, POOL_GW)

    big = [("w_ada", w_ada[0], g_w_ada, m_w_ada[0], v_w_ada[0]),
           ("w_in", w_in[0], g_w_in, m_w_in[0], v_w_in[0]),
           ("pool_w", pool_w[0].reshape(n_groups * n_pool, POOL_GW), g_pool_w,
            m_pool_w[0].reshape(n_groups * n_pool, POOL_GW), v_pool_w[0].reshape(n_groups * n_pool, POOL_GW)),
           ("glu_w", glu_w[0], g_glu_w, m_glu_w[0], v_glu_w[0]),
           ("w_branch_pool", w_branch_pool[0], g_wbp, m_w_branch_pool[0], v_w_branch_pool[0]),
           ("w_branch_ssm", w_branch_ssm[0], g_wbs, m_w_branch_ssm[0], v_w_branch_ssm[0]),
           ("w_out", w_out[0], g_wout, m_w_out[0], v_w_out[0])]
    out = {}
    for name, w_, g_, m_, v_ in big:
        d_, nm_, nv_ = _adamw(w_, g_, m_, v_, "adamw_" + name)
        out[name] = (g_, d_, nm_, nv_)

    g_b_re = g_b_re_t.transpose(1, 2, 0)
    g_b_im = g_b_im_t.transpose(1, 2, 0)
    small = [("b_ada", b_ada, g_b_ada, m_b_ada, v_b_ada),
             ("norm_pre", norm_pre, g_norm_pre, m_norm_pre, v_norm_pre),
             ("norm_post", norm_post, g_norm_post, m_norm_post, v_norm_post),
             ("pool_scale", pool_scale, g_pscale, m_pool_scale, v_pool_scale),
             ("ssm_a_re", ssm_a_re, g_a_re, m_ssm_a_re, v_ssm_a_re),
             ("ssm_a_im", ssm_a_im, g_a_im, m_ssm_a_im, v_ssm_a_im),
             ("ssm_log_dt", ssm_log_dt, g_log_dt, m_ssm_log_dt, v_ssm_log_dt),
             ("ssm_b_re", ssm_b_re, g_b_re, m_ssm_b_re, v_ssm_b_re),
             ("ssm_b_im", ssm_b_im, g_b_im, m_ssm_b_im, v_ssm_b_im),
             ("ssm_c_re", ssm_c_re, g_c_re, m_ssm_c_re, v_ssm_c_re),
             ("ssm_c_im", ssm_c_im, g_c_im, m_ssm_c_im, v_ssm_c_im),
             ("ssm_d", ssm_d, g_dskip, m_ssm_d, v_ssm_d),
             ("glu_b", glu_b, g_glu_b, m_glu_b, v_glu_b)]
    shapes = [w_.shape for _, w_, _, _, _ in small]
    pw_, pg_, pm_, pv_ = (_pack_rows([t[i] for t in small], SUBLANES) for i in (1, 2, 3, 4))
    pd_, pnm_, pnv_ = _adamw(pw_, pg_, pm_, pv_, "adamw_small")
    unpacked = [_unpack_rows(p, shapes, SUBLANES) for p in (pg_, pd_, pnm_, pnv_)]
    for (name, _, _, _, _), g_, d_, nm_, nv_ in zip(small, *unpacked):
        out[name] = (g_, d_, nm_, nv_)

    order = ["w_ada", "b_ada", "norm_pre", "norm_post", "w_in", "pool_w", "pool_scale", "ssm_a_re", "ssm_a_im",
             "ssm_log_dt", "ssm_b_re", "ssm_b_im", "ssm_c_re", "ssm_c_im", "ssm_d", "glu_w", "glu_b", "w_branch_pool",
             "w_branch_ssm", "w_out"]
    ref_shape = dict(w_ada=w_ada.shape, w_in=w_in.shape, pool_w=pool_w.shape, glu_w=glu_w.shape,
                     w_branch_pool=w_branch_pool.shape, w_branch_ssm=w_branch_ssm.shape, w_out=w_out.shape)
    for name, w_, _, _, _ in small:
        ref_shape[name] = w_.shape
    results = [loss, res["grad_x"][None]]
    for k in range(4):
        results += [out[name][k].reshape(ref_shape[name]) for name in order]
    return tuple(results)
```

```python
import functools
import math

import numpy as np
import jax
import jax.numpy as jnp
from jax import lax
from jax.experimental import pallas as pl
from jax.experimental.pallas import tpu as pltpu

F32 = jnp.float32
BF16 = jnp.bfloat16
MESH_ID = pl.DeviceIdType.MESH

D_MODEL = 1024
LANES = 128
SUBLANES = 8
SSM_G, SSM_P, SSM_H = 64, 64, 16
LANE_BLOCKS = D_MODEL // LANES
GROUPS_PER_BLOCK = LANES // SSM_H
STATE_W = GROUPS_PER_BLOCK * SSM_P
STATE_ALL = SSM_G * SSM_P
POOL_WINDOWS = (2, 4, 8, 16)
POOL_GW = D_MODEL // len(POOL_WINDOWS)
HALO = 16
RMS_EPS = 1e-6
N_CHIPS = 4
N_DEV = 8

SCAN_CHUNK = 512
ROW_CHUNK = 256
VMEM_LIMIT_BYTES = 56 * 1024 * 1024

ADAM_BLOCK_BYTES = 1 << 20
ADAM_LR, ADAM_B1, ADAM_B2, ADAM_EPS, ADAM_WD, ADAM_STEP = 0.001, 0.9, 0.999, 1e-08, 0.01, 10

_GELU_C0 = math.sqrt(2.0 / math.pi)
_GELU_C1 = 0.044715


def _cparams(*sem):
    if sem:
        return pltpu.CompilerParams(dimension_semantics=sem, vmem_limit_bytes=VMEM_LIMIT_BYTES)
    return pltpu.CompilerParams(vmem_limit_bytes=VMEM_LIMIT_BYTES)


def _sigmoid(v):
    return jax.nn.sigmoid(v)


def _silu(v):
    return v * _sigmoid(v)


def _dsilu(v):
    s = _sigmoid(v)
    return s * (1.0 + v * (1.0 - s))


def _gelu(v):
    return 0.5 * v * (1.0 + jnp.tanh(_GELU_C0 * (v + _GELU_C1 * v * v * v)))


def _dgelu(v):
    t = jnp.tanh(_GELU_C0 * (v + _GELU_C1 * v * v * v))
    return 0.5 * (1.0 + t) + 0.5 * v * (1.0 - t * t) * _GELU_C0 * (1.0 + 3.0 * _GELU_C1 * v * v)


def _dot(a, b):
    return lax.dot_general(a, b, (((1,), (0,)), ((), ())), preferred_element_type=F32)


def _dot_nt(a, b):
    return lax.dot_general(a, b, (((1,), (1,)), ((), ())), preferred_element_type=F32)


def _dot_tn(a, b):
    return lax.dot_general(a, b, (((0,), (0,)), ((), ())), preferred_element_type=F32)


def _acc8(v):
    return v.reshape(v.shape[0] // SUBLANES, SUBLANES, v.shape[1]).sum(axis=0)


def _mm(a_parts, b_parts, *, name, ta=False, tb=False, out_dtype=F32, bm=512, bn=512, bk=512):
    a_parts, b_parts = list(a_parts), list(b_parts)
    if ta:
        assert len(a_parts) == 1
        k_dim, m_dim = a_parts[0].shape
    else:
        m_dim = a_parts[0].shape[0]
        k_dim = sum(a.shape[1] for a in a_parts)
    if tb:
        assert len(b_parts) == 1
        n_dim = b_parts[0].shape[0]
    else:
        n_dim = sum(b.shape[1] for b in b_parts)
    bm, bn, bk = min(bm, m_dim), min(bn, n_dim), min(bk, k_dim)
    nm, nn, nk = m_dim // bm, n_dim // bn, k_dim // bk
    a_ranges, off = [], 0
    for a in a_parts:
        cnt = (a.shape[0] if ta else a.shape[1]) // bk
        a_ranges.append((off, cnt))
        off += cnt
    b_ranges, off = [], 0
    for b in b_parts:
        cnt = (b.shape[0] if tb else b.shape[1]) // bn
        b_ranges.append((off, cnt))
        off += cnt

    def a_spec(off, cnt):
        if ta:
            return pl.BlockSpec((bk, bm), lambda i, n, k: (k, i))
        return pl.BlockSpec((bm, bk), lambda i, n, k: (i, jnp.clip(k - off, 0, cnt - 1)))

    def b_spec(off, cnt):
        if tb:
            return pl.BlockSpec((bn, bk), lambda i, n, k: (n, k))
        return pl.BlockSpec((bk, bn), lambda i, n, k: (k, jnp.clip(n - off, 0, cnt - 1)))

    na, nb = len(a_parts), len(b_parts)
    dims = (((0 if ta else 1,), (1 if tb else 0,)), ((), ()))

    def kern(*refs):
        a_refs, b_refs = refs[:na], refs[na:na + nb]
        o_ref, acc = refs[na + nb], refs[na + nb + 1]
        n, k = pl.program_id(1), pl.program_id(2)

        @pl.when(k == 0)
        def _():
            acc[...] = jnp.zeros_like(acc)

        for ja, (koff, kcnt) in enumerate(a_ranges):
            for jb, (noff, ncnt) in enumerate(b_ranges):
                def step(ja=ja, jb=jb):
                    a = a_refs[ja][...].astype(BF16)
                    b = b_refs[jb][...].astype(BF16)
                    acc[...] += lax.dot_general(a, b, dims, preferred_element_type=F32)

                if na == 1 and nb == 1:
                    step()
                else:
                    cond = (k >= koff) & (k < koff + kcnt) & (n >= noff) & (n < noff + ncnt)
                    pl.when(cond)(step)

        @pl.when(k == nk - 1)
        def _():
            o_ref[...] = acc[...].astype(out_dtype)

    return pl.pallas_call(
        kern,
        name=name,
        grid=(nm, nn, nk),
        in_specs=[a_spec(*r) for r in a_ranges] + [b_spec(*r) for r in b_ranges],
        out_specs=pl.BlockSpec((bm, bn), lambda i, n, k: (i, n)),
        out_shape=jax.ShapeDtypeStruct((m_dim, n_dim), out_dtype),
        scratch_shapes=[pltpu.VMEM((bm, bn), F32)],
        compiler_params=_cparams("parallel", "parallel", "arbitrary"),
    )(*a_parts, *b_parts)


def _ssm_param_fn(a_re, a_im, log_dt, b_re, b_im):
    dt = jnp.exp(log_dt)
    lam_re = jnp.minimum(a_re, -1e-4)
    lam_im = a_im
    mag = jnp.exp(lam_re * dt)
    abar_re = mag * jnp.cos(lam_im * dt)
    abar_im = mag * jnp.sin(lam_im * dt)
    den = lam_re * lam_re + lam_im * lam_im
    num_re = abar_re - 1.0
    f_re = (num_re * lam_re + abar_im * lam_im) / den
    f_im = (abar_im * lam_re - num_re * lam_im) / den
    bb_re = f_re * b_re - f_im * b_im
    bb_im = f_re * b_im + f_im * b_re
    return abar_re, abar_im, bb_re, bb_im


def _ssm_params(a_re, a_im, log_dt, b_re_t, b_im_t):
    def kern(are, aim, ldt, bre, bim, o_ar, o_ai, o_br, o_bi):
        ar, ai, br, bi = _ssm_param_fn(are[...], aim[...], ldt[...], bre[...], bim[...])
        o_ar[...] = ar
        o_ai[...] = ai
        o_br[...] = br
        o_bi[...] = bi

    gp = jax.ShapeDtypeStruct((SSM_G, SSM_P), F32)
    hgp = jax.ShapeDtypeStruct((SSM_H, SSM_G, SSM_P), F32)
    return pl.pallas_call(kern, name="ssm_params", out_shape=(gp, gp, hgp, hgp), compiler_params=_cparams())(
        a_re, a_im, log_dt, b_re_t, b_im_t)


def _ssm_params_bwd(a_re, a_im, log_dt, b_re_t, b_im_t, d_ar, d_ai, d_bbr, d_bbi):
    def kern(are, aim, ldt, bre, bim, dar, dai, dbr, dbi, o_are, o_aim, o_ldt, o_bre, o_bim):
        prim = (are[...], aim[...], ldt[...], bre[...], bim[...])
        _, vjp = jax.vjp(_ssm_param_fn, *prim)
        g = vjp((dar[...], dai[...], dbr[...], dbi[...]))
        o_are[...] = g[0]
        o_aim[...] = g[1]
        o_ldt[...] = g[2]
        o_bre[...] = g[3]
        o_bim[...] = g[4]

    gp = jax.ShapeDtypeStruct((SSM_G, SSM_P), F32)
    g1 = jax.ShapeDtypeStruct((SSM_G, 1), F32)
    hgp = jax.ShapeDtypeStruct((SSM_H, SSM_G, SSM_P), F32)
    return pl.pallas_call(kern, name="ssm_params_bwd", out_shape=(gp, gp, g1, hgp, hgp), compiler_params=_cparams())(
        a_re, a_im, log_dt, b_re_t, b_im_t, d_ar, d_ai, d_bbr, d_bbi)


def _pow_tables(abar_re, abar_im, tc):
    ls = tc // SUBLANES

    def kern(ar_ref, ai_ref, fr_ref, fi_ref, rr_ref, ri_ref):
        a_re = jnp.broadcast_to(ar_ref[...], (SUBLANES, STATE_W))
        a_im = jnp.broadcast_to(ai_ref[...], (SUBLANES, STATE_W))
        p_re, p_im = a_re, a_im
        for i in range(ls):
            fwd = pl.ds(SUBLANES * i, SUBLANES)
            rev = pl.ds(SUBLANES * (ls - 1 - i), SUBLANES)
            fr_ref[fwd, :] = p_re
            fi_ref[fwd, :] = p_im
            rr_ref[rev, :] = p_re
            ri_ref[rev, :] = p_im
            p_re, p_im = p_re * a_re - p_im * a_im, p_re * a_im + p_im * a_re

    vec = pl.BlockSpec((1, STATE_W), lambda b: (0, b))
    tab = pl.BlockSpec((tc, STATE_W), lambda b: (0, b))
    shp = jax.ShapeDtypeStruct((tc, STATE_ALL), F32)
    return pl.pallas_call(
        kern, name="pow_tables", grid=(LANE_BLOCKS,), in_specs=[vec, vec], out_specs=(tab, tab, tab, tab),
        out_shape=(shp, shp, shp, shp), compiler_params=_cparams("parallel"))(abar_re, abar_im)


def _mod_kernel(c_row, w_ada_bf, b_ada):
    def kern(c_ref, w_ref, b_ref, m_ref, s_ref):
        cv = c_ref[...]
        sc = _silu(cv)
        s_ref[...] = sc
        lhs = jnp.broadcast_to(sc, (SUBLANES, D_MODEL)).astype(BF16)
        m_ref[...] = _dot(lhs, w_ref[...]) + b_ref[...]

    return pl.pallas_call(
        kern, name="ada_mod",
        out_shape=(jax.ShapeDtypeStruct((SUBLANES, 3 * D_MODEL), F32), jax.ShapeDtypeStruct((1, D_MODEL), F32)),
        compiler_params=_cparams())(c_row, w_ada_bf, b_ada)


def _row_spec(tr, width=D_MODEL, col=0):
    return pl.BlockSpec((tr, width), lambda c: (c, col))


def _vec_spec(width=D_MODEL):
    return pl.BlockSpec((1, width), lambda c: (0, 0))


def _in_norm(x, g1, scale, shift):
    seq = x.shape[0]
    tr = min(ROW_CHUNK, seq)

    def kern(x_ref, g_ref, sc_ref, sh_ref, h_ref):
        xv = x_ref[...]
        r = lax.rsqrt(jnp.mean(xv * xv, axis=-1, keepdims=True) + RMS_EPS)
        h_ref[...] = (((xv * r) * g_ref[...]) * (1.0 + sc_ref[...]) + sh_ref[...]).astype(BF16)

    return pl.pallas_call(
        kern, name="in_norm", grid=(seq // tr,),
        in_specs=[_row_spec(tr), _vec_spec(), _vec_spec(), _vec_spec()], out_specs=_row_spec(tr),
        out_shape=jax.ShapeDtypeStruct((seq, D_MODEL), BF16), compiler_params=_cparams("parallel"))(x, g1, scale, shift)


def _pool_windows(ext, pos, g, w, tr):
    cols = pl.ds(g * POOL_GW, POOL_GW)
    cur = ext[pl.ds(HALO, tr), cols]
    acc = cur
    for k in range(1, w):
        acc = acc + ext[pl.ds(HALO - k, tr), cols]
    cnt = jnp.minimum(pos + 1, w).astype(F32)
    return acc / cnt - cur


def _pool_fwd(proj, pool_w_bf, pscale):
    seq = proj.shape[0]
    tr = min(ROW_CHUNK, seq)
    hb = tr // HALO

    def kern(up_ref, halo_ref, zp_ref, pw_ref, ps_ref, y_ref, ext):
        c = pl.program_id(0)
        ext[0:HALO, :] = jnp.where(c > 0, halo_ref[...], 0.0)
        ext[HALO:, :] = up_ref[...]
        pos = c * tr + lax.broadcasted_iota(jnp.int32, (tr, POOL_GW), 0)
        for g, w in enumerate(POOL_WINDOWS):
            cols = pl.ds(g * POOL_GW, POOL_GW)
            pooled = _pool_windows(ext, pos, g, w, tr)
            mixed = _dot(pooled.astype(BF16), pw_ref[g])
            y_ref[:, cols] = (mixed * ps_ref[:, cols] * _silu(zp_ref[:, cols])).astype(BF16)

    return pl.pallas_call(
        kern, name="pool_fwd", grid=(seq // tr,),
        in_specs=[_row_spec(tr, col=0),
                  pl.BlockSpec((HALO, D_MODEL), lambda c: (jnp.maximum(c * hb - 1, 0), 0)),
                  _row_spec(tr, col=1),
                  pl.BlockSpec((len(POOL_WINDOWS), POOL_GW, POOL_GW), lambda c: (0, 0, 0)),
                  _vec_spec()],
        out_specs=_row_spec(tr), out_shape=jax.ShapeDtypeStruct((seq, D_MODEL), BF16),
        scratch_shapes=[pltpu.VMEM((tr + HALO, D_MODEL), F32)],
        compiler_params=_cparams("parallel"))(proj, proj, proj, pool_w_bf, pscale)


def _pool_bwd(proj, dyp, pool_w_bf, pscale):
    seq = proj.shape[0]
    tr = min(ROW_CHUNK, seq)
    hb = tr // HALO
    nc = seq // tr
    n_halo = seq // HALO

    def kern(up_ref, halo_ref, zp_ref, zpn_ref, dyp_ref, dypn_ref, pw_ref, ps_ref,
             d01_ref, dpw_ref, dps_ref, ext, dpn, acc_pw, acc_ps):
        c = pl.program_id(0)

        @pl.when(c == 0)
        def _():
            acc_pw[...] = jnp.zeros_like(acc_pw)
            acc_ps[...] = jnp.zeros_like(acc_ps)

        ext[0:HALO, :] = jnp.where(c > 0, halo_ref[...], 0.0)
        ext[HALO:, :] = up_ref[...]
        pos = c * tr + lax.broadcasted_iota(jnp.int32, (tr, POOL_GW), 0)
        pos_n = (c + 1) * tr + lax.broadcasted_iota(jnp.int32, (HALO, POOL_GW), 0)
        has_next = c < nc - 1
        for g, w in enumerate(POOL_WINDOWS):
            cols = pl.ds(g * POOL_GW, POOL_GW)
            pooled_bf = _pool_windows(ext, pos, g, w, tr).astype(BF16)
            wg = pw_ref[g]
            mixed = _dot(pooled_bf, wg)
            zp = zp_ref[:, cols]
            sz = _silu(zp)
            dyp_g = dyp_ref[:, cols]
            ps = ps_ref[:, cols]
            dmixed = (dyp_g * ps * sz).astype(BF16)
            acc_ps[:, cols] += _acc8(dyp_g * mixed * sz)
            d01_ref[:, pl.ds(D_MODEL + g * POOL_GW, POOL_GW)] = (dyp_g * mixed * ps * _dsilu(zp)).astype(BF16)
            acc_pw[g] += _dot_tn(pooled_bf, dmixed)
            dpooled = _dot_nt(dmixed, wg)
            dmixed_n = (jnp.where(has_next, dypn_ref[:, cols], 0.0) * ps * _silu(zpn_ref[:, cols])).astype(BF16)
            dpooled_n = _dot_nt(dmixed_n, wg)
            dpn[0:tr, :] = dpooled / jnp.minimum(pos + 1, w).astype(F32)
            dpn[tr:, :] = dpooled_n / jnp.minimum(pos_n + 1, w).astype(F32)
            acc = dpn[0:tr, :]
            for k in range(1, w):
                acc = acc + dpn[pl.ds(k, tr), :]
            d01_ref[:, cols] = (acc - dpooled).astype(BF16)

        @pl.when(c == nc - 1)
        def _():
            dpw_ref[...] = acc_pw[...]
            dps_ref[...] = jnp.sum(acc_ps[...], axis=0, keepdims=True)

    nxt = lambda c: (jnp.minimum((c + 1) * hb, n_halo - 1), 0)
    nxt1 = lambda c: (jnp.minimum((c + 1) * hb, n_halo - 1), 1)
    return pl.pallas_call(
        kern, name="pool_bwd", grid=(nc,),
        in_specs=[_row_spec(tr, col=0),
                  pl.BlockSpec((HALO, D_MODEL), lambda c: (jnp.maximum(c * hb - 1, 0), 0)),
                  _row_spec(tr, col=1),
                  pl.BlockSpec((HALO, D_MODEL), nxt1),
                  _row_spec(tr),
                  pl.BlockSpec((HALO, D_MODEL), nxt),
                  pl.BlockSpec((len(POOL_WINDOWS), POOL_GW, POOL_GW), lambda c: (0, 0, 0)),
                  _vec_spec()],
        out_specs=(pl.BlockSpec((tr, 2 * D_MODEL), lambda c: (c, 0)),
                   pl.BlockSpec((len(POOL_WINDOWS), POOL_GW, POOL_GW), lambda c: (0, 0, 0)),
                   _vec_spec()),
        out_shape=(jax.ShapeDtypeStruct((seq, 2 * D_MODEL), BF16),
                   jax.ShapeDtypeStruct((len(POOL_WINDOWS), POOL_GW, POOL_GW), F32),
                   jax.ShapeDtypeStruct((1, D_MODEL), F32)),
        scratch_shapes=[pltpu.VMEM((tr + HALO, D_MODEL), F32), pltpu.VMEM((tr + HALO, POOL_GW), F32),
                        pltpu.VMEM((len(POOL_WINDOWS), POOL_GW, POOL_GW), F32), pltpu.VMEM((SUBLANES, D_MODEL), F32)],
        compiler_params=_cparams("arbitrary"))(proj, proj, proj, proj, dyp, dyp, pool_w_bf, pscale)


def _glu_fwd(ys, proj, glu_w_bf, glu_b):
    seq = ys.shape[0]
    tr = min(ROW_CHUNK, seq)

    def kern(ys_ref, zs_ref, w_ref, b_ref, o_ref):
        yg = _gelu(ys_ref[...])
        q = _dot(yg.astype(BF16), w_ref[...]) + b_ref[...]
        o_ref[...] = (yg * _sigmoid(q) * _silu(zs_ref[...])).astype(BF16)

    return pl.pallas_call(
        kern, name="glu_fwd", grid=(seq // tr,),
        in_specs=[_row_spec(tr), _row_spec(tr, col=3), pl.BlockSpec((D_MODEL, D_MODEL), lambda c: (0, 0)), _vec_spec()],
        out_specs=_row_spec(tr), out_shape=jax.ShapeDtypeStruct((seq, D_MODEL), BF16),
        compiler_params=_cparams("parallel"))(ys, proj, glu_w_bf, glu_b)


def _glu_bwd(ys, proj, dyssm, glu_w_bf, glu_b):
    seq = ys.shape[0]
    tr = min(ROW_CHUNK, seq)
    nc = seq // tr

    def kern(ys_ref, zs_ref, dy_ref, w_ref, b_ref, dys_ref, dzs_ref, dq_ref, yg_ref, db_ref, acc_b):
        c = pl.program_id(0)

        @pl.when(c == 0)
        def _():
            acc_b[...] = jnp.zeros_like(acc_b)

        ysv = ys_ref[...]
        yg = _gelu(ysv)
        yg_bf = yg.astype(BF16)
        q = _dot(yg_bf, w_ref[...]) + b_ref[...]
        sg = _sigmoid(q)
        zs = zs_ref[...]
        dyv = dy_ref[...]
        dyglu = dyv * _silu(zs)
        dzs_ref[...] = (dyv * (yg * sg) * _dsilu(zs)).astype(BF16)
        dq = dyglu * yg * sg * (1.0 - sg)
        dq_bf = dq.astype(BF16)
        acc_b[...] += _acc8(dq)
        dyg = dyglu * sg + _dot_nt(dq_bf, w_ref[...])
        dys_ref[...] = dyg * _dgelu(ysv)
        dq_ref[...] = dq_bf
        yg_ref[...] = yg_bf

        @pl.when(c == nc - 1)
        def _():
            db_ref[...] = jnp.sum(acc_b[...], axis=0, keepdims=True)

    bf = jax.ShapeDtypeStruct((seq, D_MODEL), BF16)
    return pl.pallas_call(
        kern, name="glu_bwd", grid=(nc,),
        in_specs=[_row_spec(tr), _row_spec(tr, col=3), _row_spec(tr),
                  pl.BlockSpec((D_MODEL, D_MODEL), lambda c: (0, 0)), _vec_spec()],
        out_specs=(_row_spec(tr), _row_spec(tr), _row_spec(tr), _row_spec(tr), _vec_spec()),
        out_shape=(jax.ShapeDtypeStruct((seq, D_MODEL), F32), bf, bf, bf, jax.ShapeDtypeStruct((1, D_MODEL), F32)),
        scratch_shapes=[pltpu.VMEM((SUBLANES, D_MODEL), F32)],
        compiler_params=_cparams("arbitrary"))(ys, proj, dyssm, glu_w_bf, glu_b)


def _out_fwd_bwd(ypool, yssm, proj, x, tgt, gate, g2, wbp_bf, wbs_bf, wout_bf):
    seq = x.shape[0]
    tr = min(ROW_CHUNK, seq)
    nc = seq // tr

    def kern(yp_ref, ysm_ref, gp_ref, gs_ref, x_ref, t_ref, gate_ref, g2_ref, wbp_ref, wbs_ref, wo_ref,
             dy_ref, dyp_ref, dys_ref, d45_ref, mb_ref, dob_ref, dbp_ref, dbs_ref, loss_ref, dgate_ref, dg2_ref,
             acc_l, acc_gate, acc_g2):
        c = pl.program_id(0)

        @pl.when(c == 0)
        def _():
            acc_l[...] = jnp.zeros_like(acc_l)
            acc_gate[...] = jnp.zeros_like(acc_gate)
            acc_g2[...] = jnp.zeros_like(acc_g2)

        bp = _dot(yp_ref[...], wbp_ref[...])
        bs = _dot(ysm_ref[...], wbs_ref[...])
        sp = _sigmoid(gp_ref[...])
        ss = _sigmoid(gs_ref[...])
        mb = (sp * bp + ss * bs).astype(BF16)
        out = _dot(mb, wo_ref[...])
        r2 = lax.rsqrt(jnp.mean(out * out, axis=-1, keepdims=True) + RMS_EPS)
        oh = out * r2
        gate_v, g2_v = gate_ref[...], g2_ref[...]
        ohg = oh * g2_v
        diff = (x_ref[...] + gate_v * ohg) - t_ref[...]
        acc_l[...] += _acc8(diff * diff)
        dyv = diff * (1.0 / D_MODEL)
        dy_ref[...] = dyv
        acc_gate[...] += _acc8(dyv * ohg)
        t = dyv * gate_v
        acc_g2[...] += _acc8(t * oh)
        doh = t * g2_v
        dout = r2 * (doh - oh * jnp.mean(doh * oh, axis=-1, keepdims=True))
        dob = dout.astype(BF16)
        dmerged = _dot_nt(dob, wo_ref[...])
        dbp = (dmerged * sp).astype(BF16)
        dbs = (dmerged * ss).astype(BF16)
        d45_ref[:, 0:D_MODEL] = (dmerged * bp * sp * (1.0 - sp)).astype(BF16)
        d45_ref[:, D_MODEL:] = (dmerged * bs * ss * (1.0 - ss)).astype(BF16)
        dyp_ref[...] = _dot_nt(dbp, wbp_ref[...])
        dys_ref[...] = _dot_nt(dbs, wbs_ref[...])
        mb_ref[...] = mb
        dob_ref[...] = dob
        dbp_ref[...] = dbp
        dbs_ref[...] = dbs

        @pl.when(c == nc - 1)
        def _():
            tot = jnp.sum(acc_l[...], axis=0, keepdims=True)
            loss_ref[...] = jnp.sum(tot, axis=1, keepdims=True) * (0.5 / D_MODEL)
            dgate_ref[...] = jnp.sum(acc_gate[...], axis=0, keepdims=True)
            dg2_ref[...] = jnp.sum(acc_g2[...], axis=0, keepdims=True)

    wspec = pl.BlockSpec((D_MODEL, D_MODEL), lambda c: (0, 0))
    f32 = jax.ShapeDtypeStruct((seq, D_MODEL), F32)
    bf = jax.ShapeDtypeStruct((seq, D_MODEL), BF16)
    vec = jax.ShapeDtypeStruct((1, D_MODEL), F32)
    acc = pltpu.VMEM((SUBLANES, D_MODEL), F32)
    return pl.pallas_call(
        kern, name="out_fwd_bwd", grid=(nc,),
        in_specs=[_row_spec(tr), _row_spec(tr), _row_spec(tr, col=4), _row_spec(tr, col=5), _row_spec(tr), _row_spec(tr),
                  _vec_spec(), _vec_spec(), wspec, wspec, wspec],
        out_specs=(_row_spec(tr), _row_spec(tr), _row_spec(tr), pl.BlockSpec((tr, 2 * D_MODEL), lambda c: (c, 0)),
                   _row_spec(tr), _row_spec(tr), _row_spec(tr), _row_spec(tr),
                   pl.BlockSpec((1, 1), lambda c: (0, 0)), _vec_spec(), _vec_spec()),
        out_shape=(f32, f32, f32, jax.ShapeDtypeStruct((seq, 2 * D_MODEL), BF16), bf, bf, bf, bf,
                   jax.ShapeDtypeStruct((1, 1), F32), vec, vec),
        scratch_shapes=[acc, acc, acc],
        compiler_params=_cparams("arbitrary"))(ypool, yssm, proj, proj, x, tgt, gate, g2, wbp_bf, wbs_bf, wout_bf)


def _in_bwd(dh, x, dy, g1, scale):
    seq = x.shape[0]
    tr = min(ROW_CHUNK, seq)
    nc = seq // tr

    def kern(dh_ref, x_ref, dy_ref, g_ref, sc_ref, dx_ref, dsh_ref, dsc_ref, dg_ref, a_sh, a_sc, a_g):
        c = pl.program_id(0)

        @pl.when(c == 0)
        def _():
            a_sh[...] = jnp.zeros_like(a_sh)
            a_sc[...] = jnp.zeros_like(a_sc)
            a_g[...] = jnp.zeros_like(a_g)

        xv = x_ref[...]
        r = lax.rsqrt(jnp.mean(xv * xv, axis=-1, keepdims=True) + RMS_EPS)
        xh = xv * r
        g = g_ref[...]
        dhv = dh_ref[...]
        a_sh[...] += _acc8(dhv)
        a_sc[...] += _acc8(dhv * (xh * g))
        dn = dhv * (1.0 + sc_ref[...])
        a_g[...] += _acc8(dn * xh)
        dxh = dn * g
        dx_ref[...] = dy_ref[...] + r * (dxh - xh * jnp.mean(dxh * xh, axis=-1, keepdims=True))

        @pl.when(c == nc - 1)
        def _():
            dsh_ref[...] = jnp.sum(a_sh[...], axis=0, keepdims=True)
            dsc_ref[...] = jnp.sum(a_sc[...], axis=0, keepdims=True)
            dg_ref[...] = jnp.sum(a_g[...], axis=0, keepdims=True)

    vec = jax.ShapeDtypeStruct((1, D_MODEL), F32)
    acc = pltpu.VMEM((SUBLANES, D_MODEL), F32)
    return pl.pallas_call(
        kern, name="in_bwd", grid=(nc,),
        in_specs=[_row_spec(tr), _row_spec(tr), _row_spec(tr), _vec_spec(), _vec_spec()],
        out_specs=(_row_spec(tr), _vec_spec(), _vec_spec(), _vec_spec()),
        out_shape=(jax.ShapeDtypeStruct((seq, D_MODEL), F32), vec, vec, vec),
        scratch_shapes=[acc, acc, acc],
        compiler_params=_cparams("arbitrary"))(dh, x, dy, g1, scale)


def _perm_matrix(tc):
    ls = tc // SUBLANES
    r = np.arange(tc)
    m = np.zeros((tc, tc), np.float32)
    m[r, (r % SUBLANES) * ls + r // SUBLANES] = 1.0
    return m


def _local_scan(a_re, a_im, br, bi, xr, xi, row0, ls, reverse):
    x_re = jnp.zeros((SUBLANES, STATE_W), F32)
    x_im = jnp.zeros((SUBLANES, STATE_W), F32)
    for i in (range(ls - 1, -1, -1) if reverse else range(ls)):
        src = pl.ds(SUBLANES * i, SUBLANES)
        dst = pl.ds(row0 + SUBLANES * i, SUBLANES)
        n_re = a_re * x_re - a_im * x_im + br[src, :]
        n_im = a_re * x_im + a_im * x_re + bi[src, :]
        x_re, x_im = n_re, n_im
        xr[dst, :] = x_re
        xi[dst, :] = x_im
    return x_re, x_im


def _unpermute(pmt, v):
    hi = v.astype(BF16)
    lo = (v - hi.astype(F32)).astype(BF16)
    return _dot(pmt, hi) + _dot(pmt, lo)


def _ssm_scan_fwd(proj, pm, pmt, bb_re, bb_im, cm_re, cm_im, abar_re, abar_im, pw_re, pw_im, d_skip, tc):
    seq = proj.shape[0]
    nc = seq // tc
    ls = tc // SUBLANES
    us_col0 = 2 * D_MODEL // LANES

    def kern(us_ref, pm_ref, pmt_ref, bbr_ref, bbi_ref, cmr_ref, cmi_ref, ar_ref, ai_ref, pwr_ref, pwi_ref, d_ref,
             ys_ref, ecr_ref, eci_ref, bur, bui, car_r, car_i, end_r, end_i):
        c = pl.program_id(1)

        @pl.when(c == 0)
        def _():
            car_r[...] = jnp.zeros_like(car_r)
            car_i[...] = jnp.zeros_like(car_i)

        u = us_ref[...]
        up = _dot(pm_ref[...], u.astype(BF16)).astype(BF16)
        bur[...] = _dot(up, bbr_ref[0])
        bui[...] = _dot(up, bbi_ref[0])
        a_re = jnp.broadcast_to(ar_ref[...], (SUBLANES, STATE_W))
        a_im = jnp.broadcast_to(ai_ref[...], (SUBLANES, STATE_W))
        x_re, x_im = _local_scan(a_re, a_im, bur, bui, bur, bui, 0, ls, False)
        end_r[...] = x_re
        end_i[...] = x_im
        big_re = pwr_ref[tc - 1:tc, :]
        big_im = pwi_ref[tc - 1:tc, :]
        e_re = car_r[0:1, :]
        e_im = car_i[0:1, :]
        for s in range(SUBLANES):
            n_re = end_r[s:s + 1, :] + big_re * e_re - big_im * e_im
            n_im = end_i[s:s + 1, :] + big_re * e_im + big_im * e_re
            e_re, e_im = n_re, n_im
            if s < SUBLANES - 1:
                car_r[s + 1:s + 2, :] = e_re
                car_i[s + 1:s + 2, :] = e_im
        ec_re = car_r[...]
        ec_im = car_i[...]
        ecr_ref[...] = ec_re
        eci_ref[...] = ec_im
        p_re = pwr_ref[...].reshape(ls, SUBLANES, STATE_W)
        p_im = pwi_ref[...].reshape(ls, SUBLANES, STATE_W)
        xf_re = bur[...].reshape(ls, SUBLANES, STATE_W) + p_re * ec_re[None] - p_im * ec_im[None]
        xf_im = bui[...].reshape(ls, SUBLANES, STATE_W) + p_re * ec_im[None] + p_im * ec_re[None]
        xb_re = xf_re.reshape(tc, STATE_W).astype(BF16)
        xb_im = xf_im.reshape(tc, STATE_W).astype(BF16)
        y_perm = _dot(xb_re, cmr_ref[0]) - _dot(xb_im, cmi_ref[0])
        ys_ref[...] = _unpermute(pmt_ref[...], y_perm) + d_ref[...] * u
        car_r[0:1, :] = e_re
        car_i[0:1, :] = e_im

    sq = pl.BlockSpec((tc, tc), lambda b, c: (0, 0))
    vec = pl.BlockSpec((1, STATE_W), lambda b, c: (0, b))
    tab = pl.BlockSpec((tc, STATE_W), lambda b, c: (0, b))
    car = pl.BlockSpec((SUBLANES, STATE_W), lambda b, c: (c, b))
    carry_shape = jax.ShapeDtypeStruct((nc * SUBLANES, STATE_ALL), F32)
    small = pltpu.VMEM((SUBLANES, STATE_W), F32)
    big = pltpu.VMEM((tc, STATE_W), F32)
    return pl.pallas_call(
        kern, name="ssm_scan_fwd", grid=(LANE_BLOCKS, nc),
        in_specs=[pl.BlockSpec((tc, LANES), lambda b, c: (c, us_col0 + b)), sq, sq,
                  pl.BlockSpec((1, LANES, STATE_W), lambda b, c: (b, 0, 0)),
                  pl.BlockSpec((1, LANES, STATE_W), lambda b, c: (b, 0, 0)),
                  pl.BlockSpec((1, STATE_W, LANES), lambda b, c: (b, 0, 0)),
                  pl.BlockSpec((1, STATE_W, LANES), lambda b, c: (b, 0, 0)),
                  vec, vec, tab, tab, pl.BlockSpec((1, LANES), lambda b, c: (0, b))],
        out_specs=(pl.BlockSpec((tc, LANES), lambda b, c: (c, b)), car, car),
        out_shape=(jax.ShapeDtypeStruct((seq, D_MODEL), F32), carry_shape, carry_shape),
        scratch_shapes=[big, big, small, small, small, small],
        compiler_params=_cparams("parallel", "arbitrary"),
    )(proj, pm, pmt, bb_re, bb_im, cm_re, cm_im, abar_re, abar_im, pw_re, pw_im, d_skip)


def _ssm_scan_bwd(proj, dys, ec_re, ec_im, pm, pmt, bb_re, bb_im, cm_re, cm_im, abar_re, abar_im,
                  pw_re, pw_im, pv_re, pv_im, d_skip, tc):
    seq = proj.shape[0]
    nc = seq // tc
    ls = tc // SUBLANES
    us_col0 = 2 * D_MODEL // LANES

    def kern(us_ref, dys_ref, ecr_ref, eci_ref, pm_ref, pmt_ref, bbr_ref, bbi_ref, cmr_ref, cmi_ref, ar_ref, ai_ref,
             pwr_ref, pwi_ref, pvr_ref, pvi_ref, d_ref,
             dus_ref, dbbr_ref, dbbi_ref, dcmr_ref, dcmi_ref, dar_ref, dai_ref, dd_ref,
             bur, bui, xr, xi, gr, gi, fc_r, fc_i, a_bbr, a_bbi, a_cmr, a_cmi, a_ar, a_ai, a_dd):
        c = pl.program_id(1)

        @pl.when(c == 0)
        def _():
            for ref in (fc_r, fc_i, a_bbr, a_bbi, a_cmr, a_cmi, a_ar, a_ai, a_dd):
                ref[...] = jnp.zeros_like(ref)

        u = us_ref[...]
        dysv = dys_ref[...]
        a_dd[...] += _acc8(dysv * u)
        up = _dot(pm_ref[...], u.astype(BF16)).astype(BF16)
        bur[...] = _dot(up, bbr_ref[0])
        bui[...] = _dot(up, bbi_ref[0])
        a_re = jnp.broadcast_to(ar_ref[...], (SUBLANES, STATE_W))
        a_im = jnp.broadcast_to(ai_ref[...], (SUBLANES, STATE_W))
        _local_scan(a_re, a_im, bur, bui, xr, xi, SUBLANES, ls, False)
        ec_r = ecr_ref[...]
        ec_i = eci_ref[...]
        xr[0:SUBLANES, :] = ec_r
        xi[0:SUBLANES, :] = ec_i
        p_re = pwr_ref[...].reshape(ls, SUBLANES, STATE_W)
        p_im = pwi_ref[...].reshape(ls, SUBLANES, STATE_W)
        xl_re = xr[SUBLANES:, :].reshape(ls, SUBLANES, STATE_W)
        xl_im = xi[SUBLANES:, :].reshape(ls, SUBLANES, STATE_W)
        xf_re = (xl_re + p_re * ec_r[None] - p_im * ec_i[None]).reshape(tc, STATE_W)
        xf_im = (xl_im + p_re * ec_i[None] + p_im * ec_r[None]).reshape(tc, STATE_W)
        xr[SUBLANES:, :] = xf_re
        xi[SUBLANES:, :] = xf_im
        dysp = _dot(pm_ref[...], dysv.astype(BF16)).astype(BF16)
        a_cmr[...] += _dot_tn(xf_re.astype(BF16), dysp)
        a_cmi[...] -= _dot_tn(xf_im.astype(BF16), dysp)
        gr[...] = _dot_nt(dysp, cmr_ref[0])
        gi[...] = -_dot_nt(dysp, cmi_ref[0])
        _local_scan(a_re, -a_im, gr, gi, gr, gi, 0, ls, True)
        big_re = pwr_ref[tc - 1:tc, :]
        big_im = -pwi_ref[tc - 1:tc, :]
        f_re = fc_r[SUBLANES - 1:SUBLANES, :]
        f_im = fc_i[SUBLANES - 1:SUBLANES, :]
        for s in range(SUBLANES - 1, -1, -1):
            n_re = gr[s:s + 1, :] + big_re * f_re - big_im * f_im
            n_im = gi[s:s + 1, :] + big_re * f_im + big_im * f_re
            f_re, f_im = n_re, n_im
            if s > 0:
                fc_r[s - 1:s, :] = f_re
                fc_i[s - 1:s, :] = f_im
        fcv_r = fc_r[...]
        fcv_i = fc_i[...]
        q_re = pvr_ref[...].reshape(ls, SUBLANES, STATE_W)
        q_im = -pvi_ref[...].reshape(ls, SUBLANES, STATE_W)
        lam_re = (gr[...].reshape(ls, SUBLANES, STATE_W) + q_re * fcv_r[None] - q_im * fcv_i[None]).reshape(tc, STATE_W)
        lam_im = (gi[...].reshape(ls, SUBLANES, STATE_W) + q_re * fcv_i[None] + q_im * fcv_r[None]).reshape(tc, STATE_W)
        fc_r[SUBLANES - 1:SUBLANES, :] = f_re
        fc_i[SUBLANES - 1:SUBLANES, :] = f_im
        xp_re = xr[0:tc, :]
        xp_im = xi[0:tc, :]
        a_ar[...] += _acc8(lam_re * xp_re + lam_im * xp_im)
        a_ai[...] += _acc8(lam_im * xp_re - lam_re * xp_im)
        lb_re = lam_re.astype(BF16)
        lb_im = lam_im.astype(BF16)
        a_bbr[...] += _dot_tn(up, lb_re)
        a_bbi[...] += _dot_tn(up, lb_im)
        dus_perm = _dot_nt(lb_re, bbr_ref[0]) + _dot_nt(lb_im, bbi_ref[0])
        dus_ref[...] = _unpermute(pmt_ref[...], dus_perm) + dysv * d_ref[...]

        @pl.when(c == nc - 1)
        def _():
            dbbr_ref[0] = a_bbr[...]
            dbbi_ref[0] = a_bbi[...]
            dcmr_ref[0] = a_cmr[...]
            dcmi_ref[0] = a_cmi[...]
            dar_ref[...] = jnp.sum(a_ar[...], axis=0, keepdims=True)
            dai_ref[...] = jnp.sum(a_ai[...], axis=0, keepdims=True)
            dd_ref[...] = jnp.sum(a_dd[...], axis=0, keepdims=True)

    rc = lambda c: nc - 1 - c
    sq = pl.BlockSpec((tc, tc), lambda b, c: (0, 0))
    vec = pl.BlockSpec((1, STATE_W), lambda b, c: (0, b))
    tab = pl.BlockSpec((tc, STATE_W), lambda b, c: (0, b))
    car = pl.BlockSpec((SUBLANES, STATE_W), lambda b, c: (rc(c), b))
    bblk = pl.BlockSpec((1, LANES, STATE_W), lambda b, c: (b, 0, 0))
    cblk = pl.BlockSpec((1, STATE_W, LANES), lambda b, c: (b, 0, 0))
    dvec = pl.BlockSpec((1, LANES), lambda b, c: (0, b))
    small = pltpu.VMEM((SUBLANES, STATE_W), F32)
    big = pltpu.VMEM((tc, STATE_W), F32)
    bigp = pltpu.VMEM((tc + SUBLANES, STATE_W), F32)
    return pl.pallas_call(
        kern, name="ssm_scan_bwd", grid=(LANE_BLOCKS, nc),
        in_specs=[pl.BlockSpec((tc, LANES), lambda b, c: (rc(c), us_col0 + b)),
                  pl.BlockSpec((tc, LANES), lambda b, c: (rc(c), b)),
                  car, car, sq, sq, bblk, bblk, cblk, cblk, vec, vec, tab, tab, tab, tab, dvec],
        out_specs=(pl.BlockSpec((tc, LANES), lambda b, c: (rc(c), b)), bblk, bblk, cblk, cblk, vec, vec, dvec),
        out_shape=(jax.ShapeDtypeStruct((seq, D_MODEL), F32),
                   jax.ShapeDtypeStruct((LANE_BLOCKS, LANES, STATE_W), F32),
                   jax.ShapeDtypeStruct((LANE_BLOCKS, LANES, STATE_W), F32),
                   jax.ShapeDtypeStruct((LANE_BLOCKS, STATE_W, LANES), F32),
                   jax.ShapeDtypeStruct((LANE_BLOCKS, STATE_W, LANES), F32),
                   jax.ShapeDtypeStruct((1, STATE_ALL), F32), jax.ShapeDtypeStruct((1, STATE_ALL), F32),
                   jax.ShapeDtypeStruct((1, D_MODEL), F32)),
        scratch_shapes=[big, big, bigp, bigp, big, big, small, small,
                        pltpu.VMEM((LANES, STATE_W), F32), pltpu.VMEM((LANES, STATE_W), F32),
                        pltpu.VMEM((STATE_W, LANES), F32), pltpu.VMEM((STATE_W, LANES), F32),
                        small, small, pltpu.VMEM((SUBLANES, LANES), F32)],
        compiler_params=_cparams("parallel", "arbitrary"),
    )(proj, dys, ec_re, ec_im, pm, pmt, bb_re, bb_im, cm_re, cm_im, abar_re, abar_im, pw_re, pw_im, pv_re, pv_im, d_skip)


def _eye5():
    return jnp.eye(GROUPS_PER_BLOCK, dtype=F32)[None, :, None, :, None]


def _embed_b(bb_t):
    t = bb_t.transpose(1, 0, 2).reshape(LANE_BLOCKS, GROUPS_PER_BLOCK, SSM_H, 1, SSM_P)
    return (t * _eye5()).reshape(LANE_BLOCKS, LANES, STATE_W)


def _extract_b(blk):
    t = blk.reshape(LANE_BLOCKS, GROUPS_PER_BLOCK, SSM_H, GROUPS_PER_BLOCK, SSM_P)
    return (t * _eye5()).sum(axis=3).reshape(SSM_G, SSM_H, SSM_P).transpose(1, 0, 2)


def _embed_c(c_ghp):
    t = c_ghp.transpose(0, 2, 1).reshape(LANE_BLOCKS, GROUPS_PER_BLOCK, SSM_P, 1, SSM_H)
    return (t * _eye5()).reshape(LANE_BLOCKS, STATE_W, LANES)


def _extract_c(blk):
    t = blk.reshape(LANE_BLOCKS, GROUPS_PER_BLOCK, SSM_P, GROUPS_PER_BLOCK, SSM_H)
    return (t * _eye5()).sum(axis=3).reshape(SSM_G, SSM_P, SSM_H).transpose(0, 2, 1)


def _local_step(x, c_row, tgt, w_ada_bf, b_ada, g1, g2, w_in_bf, pool_w_bf, pscale, a_re, a_im, log_dt,
                b_re_t, b_im_t, c_re, c_im, d_skip, glu_w_bf, glu_b, wbp_bf, wbs_bf, wout_bf):
    seq = x.shape[0]
    tc = min(SCAN_CHUNK, seq)
    mod8, silu_c = _mod_kernel(c_row, w_ada_bf, b_ada)
    mod = mod8[0:1]
    shift, scale, gate = mod[:, 0:D_MODEL], mod[:, D_MODEL:2 * D_MODEL], mod[:, 2 * D_MODEL:]

    abar_re, abar_im, bb_re_t, bb_im_t = _ssm_params(a_re, a_im, log_dt, b_re_t, b_im_t)
    abar_re_f, abar_im_f = abar_re.reshape(1, STATE_ALL), abar_im.reshape(1, STATE_ALL)
    pw_re, pw_im, pv_re, pv_im = _pow_tables(abar_re_f, abar_im_f, tc)
    bbe_re, bbe_im = _embed_b(bb_re_t).astype(BF16), _embed_b(bb_im_t).astype(BF16)
    cme_re, cme_im = _embed_c(c_re).astype(BF16), _embed_c(c_im).astype(BF16)
    pm_np = _perm_matrix(tc)
    pm, pmt = jnp.asarray(pm_np, BF16), jnp.asarray(pm_np.T, BF16)
    d_row = d_skip.reshape(1, D_MODEL)

    h = _in_norm(x, g1, scale, shift)
    proj = _mm([h], [w_in_bf], name="proj", bn=1024, bk=1024)
    ypool = _pool_fwd(proj, pool_w_bf, pscale)
    ys, ec_re, ec_im = _ssm_scan_fwd(proj, pm, pmt, bbe_re, bbe_im, cme_re, cme_im, abar_re_f, abar_im_f,
                                      pw_re, pw_im, d_row, tc)
    yssm = _glu_fwd(ys, proj, glu_w_bf, glu_b)
    (dy, dypool, dyssm, d45, mb, dob, dbp, dbs, loss, dgate, dg2) = _out_fwd_bwd(
        ypool, yssm, proj, x, tgt, gate, g2, wbp_bf, wbs_bf, wout_bf)

    d_wout = _mm([mb], [dob], ta=True, name="dw_out", bm=1024, bk=512)
    d_wbp = _mm([ypool], [dbp], ta=True, name="dw_bp", bm=1024, bk=512)
    d_wbs = _mm([yssm], [dbs], ta=True, name="dw_bs", bm=1024, bk=512)
    dys, dzs, dq, yg, d_glu_b = _glu_bwd(ys, proj, dyssm, glu_w_bf, glu_b)
    d_glu_w = _mm([yg], [dq], ta=True, name="dw_glu", bm=1024, bk=512)
    (dus, dbbe_re, dbbe_im, dcme_re, dcme_im, d_abar_re, d_abar_im, d_dskip) = _ssm_scan_bwd(
        proj, dys, ec_re, ec_im, pm, pmt, bbe_re, bbe_im, cme_re, cme_im, abar_re_f, abar_im_f,
        pw_re, pw_im, pv_re, pv_im, d_row, tc)
    d01, d_pool_w, d_pscale = _pool_bwd(proj, dypool, pool_w_bf, pscale)
    dparts = [d01, dus, dzs, d45]
    dh = _mm(dparts, [w_in_bf], tb=True, name="dh", bn=1024, bk=512)
    d_win = _mm([h], dparts, ta=True, name="dw_in", bm=1024, bk=512)
    grad_x, dshift, dscale, dg1 = _in_bwd(dh, x, dy, g1, scale)
    dmod = jnp.concatenate([dshift, dscale, dgate], axis=1)
    return dict(
        loss=loss[0, 0], grad_x=grad_x, dmod=dmod, silu_c=silu_c, dg1=dg1, dg2=dg2, d_pscale=d_pscale,
        d_glu_b=d_glu_b, d_dskip=d_dskip, d_abar_re=d_abar_re, d_abar_im=d_abar_im,
        d_bb_re_t=_extract_b(dbbe_re), d_bb_im_t=_extract_b(dbbe_im),
        d_c_re=_extract_c(dcme_re), d_c_im=_extract_c(dcme_im),
        d_win=d_win, d_glu_w=d_glu_w, d_wbp=d_wbp, d_wbs=d_wbs, d_wout=d_wout, d_pool_w=d_pool_w)


def _position():
    x, y, c = lax.axis_index("x"), lax.axis_index("y"), lax.axis_index("c")
    chips = [(1 - x, y), (x, 1 - y), (1 - x, 1 - y)]
    return x, y, c, chips


_ANY = pl.BlockSpec(memory_space=pl.ANY)
COMM_CHUNKS = 4
COMM_ROW_ALIGN = 16


def _row_chunks(rows, k):
    assert rows % (k * COMM_ROW_ALIGN) == 0, (rows, k)
    step = rows // k
    return [(q * step, step) for q in range(k)]


def _pad_rows(buf, multiple, axis=0):
    pad = (-buf.shape[axis]) % multiple
    if not pad:
        return buf
    shape = list(buf.shape)
    shape[axis] = pad
    return jnp.concatenate([buf, jnp.zeros(shape, buf.dtype)], axis=axis)


def _ag_weights(packed):
    rows, width = packed.shape
    half = rows // 2
    chunks = _row_chunks(half, COMM_CHUNKS)
    nq = len(chunks)

    def body(p_ref, out_ref, send_sems, recv_sems, local_sems):
        x, y, c, chips = _position()
        sibling = (x, y, 1 - c)

        def copy(k, chip, h, q, to, src=None):
            start, size = chunks[q]
            rows_q = pl.ds(h * half + start, size)
            dst = out_ref.at[2 * chip[0] + chip[1], rows_q, :]
            return pltpu.make_async_remote_copy(
                src_ref=dst if src is None else src.at[rows_q, :], dst_ref=dst, send_sem=send_sems.at[k * nq + q],
                recv_sem=recv_sems.at[k * nq + q], device_id=to, device_id_type=MESH_ID)

        mine = []
        for h in range(2):
            for q, (start, size) in enumerate(chunks):
                rows_q = pl.ds(h * half + start, size)
                mine.append(pltpu.make_async_copy(p_ref.at[rows_q, :], out_ref.at[2 * x + y, rows_q, :],
                                                  local_sems.at[h * nq + q]))
        first = [copy(j, (x, y), c, q, (*chip, c), src=p_ref) for q in range(nq) for j, chip in enumerate(chips)]
        for cp in first:
            cp.start()
        for cp in mine:
            cp.start()
        passed = []
        for q in range(nq):
            for j, chip in enumerate(chips):
                copy(j, chip, c, q, (x, y, c)).wait_recv()
                fwd = copy(3 + j, chip, c, q, sibling)
                fwd.start()
                passed.append(fwd)
        for q in range(nq):
            for j, chip in enumerate(chips):
                copy(3 + j, chip, 1 - c, q, (x, y, c)).wait_recv()
        for cp in first + passed:
            cp.wait_send()
        for cp in mine:
            cp.wait()

    return pl.pallas_call(
        body, name="ag_weights", in_specs=[_ANY], out_specs=_ANY,
        out_shape=jax.ShapeDtypeStruct((N_CHIPS, rows, width), packed.dtype),
        scratch_shapes=[pltpu.SemaphoreType.DMA((6 * nq,)), pltpu.SemaphoreType.DMA((6 * nq,)),
                        pltpu.SemaphoreType.DMA((2 * nq,))],
    )(packed)


def _small_allgather_sum(buf):
    rows, width = buf.shape
    chunks = _row_chunks(rows, COMM_CHUNKS)
    nq = len(chunks)

    def body(b_ref, all_ref, sum_ref, send_sems, recv_sems, local_sem):
        x, y, c, chips = _position()
        me, sibling = (x, y, c), (x, y, 1 - c)

        def slot(px, py, pc):
            return all_ref.at[4 * px + 2 * py + pc]

        def copy(k, block, q, to, src=None):
            rows_q = pl.ds(chunks[q][0], chunks[q][1])
            dst = slot(*block).at[rows_q, :]
            return pltpu.make_async_remote_copy(
                src_ref=dst if src is None else src.at[rows_q, :], dst_ref=dst, send_sem=send_sems.at[k * nq + q],
                recv_sem=recv_sems.at[k * nq + q], device_id=to, device_id_type=MESH_ID)

        mine = pltpu.make_async_copy(b_ref, slot(*me), local_sem)
        mine.start()
        first = []
        for q in range(nq):
            first += [copy(1 + j, me, q, (*chip, c), src=b_ref) for j, chip in enumerate(chips)]
            first.append(copy(0, me, q, sibling, src=b_ref))
        for cp in first:
            cp.start()
        passed = []
        for q in range(nq):
            for j, chip in enumerate(chips):
                copy(1 + j, (*chip, c), q, me).wait_recv()
                fwd = copy(4 + j, (*chip, c), q, sibling)
                fwd.start()
                passed.append(fwd)
        for q in range(nq):
            copy(0, sibling, q, me).wait_recv()
            for j, chip in enumerate(chips):
                copy(4 + j, (*chip, 1 - c), q, me).wait_recv()
        for cp in first + passed:
            cp.wait_send()
        mine.wait()
        total = all_ref[0]
        for d in range(1, N_DEV):
            total = total + all_ref[d]
        sum_ref[...] = total

    vm = pl.BlockSpec(memory_space=pltpu.VMEM)
    return pl.pallas_call(
        body, name="small_allgather_sum", in_specs=[vm], out_specs=(vm, vm),
        out_shape=(jax.ShapeDtypeStruct((N_DEV, rows, width), F32), jax.ShapeDtypeStruct((rows, width), F32)),
        scratch_shapes=[pltpu.SemaphoreType.DMA((7 * nq,)), pltpu.SemaphoreType.DMA((7 * nq,)), pltpu.SemaphoreType.DMA],
        compiler_params=_cparams(),
    )(buf)


def _rs_pair(g):
    n, rows, width = g.shape
    half = rows // 2
    chunks = _row_chunks(half, COMM_CHUNKS)
    nq = len(chunks)

    def body(g_ref, got_ref, send_sems, recv_sems):
        x, y, c, _ = _position()
        swaps = []
        for k in range(n):
            for q, (start, size) in enumerate(chunks):
                swaps.append(pltpu.make_async_remote_copy(
                    src_ref=g_ref.at[k, pl.ds((1 - c) * half + start, size), :], dst_ref=got_ref.at[k, pl.ds(start, size), :],
                    send_sem=send_sems.at[k * nq + q], recv_sem=recv_sems.at[k * nq + q],
                    device_id=(x, y, 1 - c), device_id_type=MESH_ID))
        for cp in swaps:
            cp.start()
        for cp in swaps:
            cp.wait()

    return pl.pallas_call(
        body, name="rs_pair", in_specs=[_ANY], out_specs=_ANY, out_shape=jax.ShapeDtypeStruct((n, half, width), g.dtype),
        scratch_shapes=[pltpu.SemaphoreType.DMA((n * nq,)), pltpu.SemaphoreType.DMA((n * nq,))],
    )(g)


def _rs_chips(part_bf):
    n, rows, width = part_bf.shape
    chunks = _row_chunks(rows, COMM_CHUNKS)
    nq = len(chunks)

    def body(pb_ref, got_ref, send_sems, recv_sems):
        x, y, c, chips = _position()
        sends = []
        for q, (start, size) in enumerate(chunks):
            for j, chip in enumerate(chips):
                sends.append(pltpu.make_async_remote_copy(
                    src_ref=pb_ref.at[2 * chip[0] + chip[1], pl.ds(start, size), :], dst_ref=got_ref.at[j, pl.ds(start, size), :],
                    send_sem=send_sems.at[j * nq + q], recv_sem=recv_sems.at[j * nq + q],
                    device_id=(*chip, c), device_id_type=MESH_ID))
        for cp in sends:
            cp.start()
        for cp in sends:
            cp.wait()

    return pl.pallas_call(
        body, name="rs_chips", in_specs=[_ANY], out_specs=_ANY,
        out_shape=jax.ShapeDtypeStruct((N_CHIPS - 1, rows, width), BF16),
        scratch_shapes=[pltpu.SemaphoreType.DMA((3 * nq,)), pltpu.SemaphoreType.DMA((3 * nq,))],
    )(part_bf)


def _rs_join(fin_half):
    half, width = fin_half.shape
    chunks = _row_chunks(half, COMM_CHUNKS)
    nq = len(chunks)

    def body(h_ref, out_ref, send_sems, recv_sems, local_sems):
        x, y, c, _ = _position()
        keeps, swaps = [], []
        for q, (start, size) in enumerate(chunks):
            src = h_ref.at[pl.ds(start, size), :]
            dst = out_ref.at[pl.ds(c * half + start, size), :]
            keeps.append(pltpu.make_async_copy(src, dst, local_sems.at[q]))
            swaps.append(pltpu.make_async_remote_copy(
                src_ref=src, dst_ref=dst, send_sem=send_sems.at[q], recv_sem=recv_sems.at[q],
                device_id=(x, y, 1 - c), device_id_type=MESH_ID))
        for cp in swaps + keeps:
            cp.start()
        for cp in swaps + keeps:
            cp.wait()

    return pl.pallas_call(
        body, name="rs_join", in_specs=[_ANY], out_specs=_ANY,
        out_shape=jax.ShapeDtypeStruct((2 * half, width), fin_half.dtype),
        scratch_shapes=[pltpu.SemaphoreType.DMA((nq,)), pltpu.SemaphoreType.DMA((nq,)), pltpu.SemaphoreType.DMA((nq,))],
    )(fin_half)


def _pair_add(g, got, core):
    n, half, width = got.shape
    nb = 2
    rb = half // nb

    def kern(c_ref, a_ref, b_ref, f_ref, h_ref):
        s = a_ref[...] + b_ref[...]
        f_ref[...] = s
        h_ref[...] = s.astype(BF16)

    spec = pl.BlockSpec((1, rb, width), lambda k, i, c_ref: (k, i, 0))
    return pl.pallas_call(
        kern, name="rs_pair_add",
        grid_spec=pltpu.PrefetchScalarGridSpec(
            num_scalar_prefetch=1, grid=(n, nb),
            in_specs=[pl.BlockSpec((1, rb, width), lambda k, i, c_ref: (k, c_ref[0] * nb + i, 0)), spec],
            out_specs=(spec, spec)),
        out_shape=(jax.ShapeDtypeStruct(got.shape, F32), jax.ShapeDtypeStruct(got.shape, BF16)),
        compiler_params=_cparams("parallel", "parallel"))(core, g, got)


def _chip_add(part_f32, got, chip):
    _, rows, width = part_f32.shape
    nb = 2
    rb = rows // nb

    def kern(k_ref, a_ref, b_ref, o_ref):
        o_ref[...] = ((a_ref[0] + b_ref[0].astype(F32)) + b_ref[1].astype(F32)) + b_ref[2].astype(F32)

    return pl.pallas_call(
        kern, name="rs_chip_add",
        grid_spec=pltpu.PrefetchScalarGridSpec(
            num_scalar_prefetch=1, grid=(nb,),
            in_specs=[pl.BlockSpec((1, rb, width), lambda i, k_ref: (k_ref[0], i, 0)),
                      pl.BlockSpec((N_CHIPS - 1, rb, width), lambda i, k_ref: (0, i, 0))],
            out_specs=pl.BlockSpec((rb, width), lambda i, k_ref: (i, 0))),
        out_shape=jax.ShapeDtypeStruct((rows, width), F32),
        compiler_params=_cparams("parallel"))(chip, part_f32, got)


def _adamw(w, g, m, v, name):
    rows, width = w.shape
    rb = rows
    for cand in (512, 256, 128, 64, 32, 16, 8):
        if rows % cand == 0 and cand * width * 4 <= ADAM_BLOCK_BYTES:
            rb = cand
            break
    spec = pl.BlockSpec((rb, width), lambda i: (i, 0))

    def kern(w_ref, g_ref, m_ref, v_ref, d_ref, nm_ref, nv_ref):
        gv = g_ref[...]
        nm = ADAM_B1 * m_ref[...] + (1.0 - ADAM_B1) * gv
        nv = ADAM_B2 * v_ref[...] + (1.0 - ADAM_B2) * (gv * gv)
        m_hat = nm / (1.0 - ADAM_B1 ** ADAM_STEP)
        v_hat = nv / (1.0 - ADAM_B2 ** ADAM_STEP)
        d_ref[...] = -ADAM_LR * (m_hat / (jnp.sqrt(v_hat) + ADAM_EPS) + ADAM_WD * w_ref[...])
        nm_ref[...] = nm
        nv_ref[...] = nv

    shp = jax.ShapeDtypeStruct(w.shape, F32)
    return pl.pallas_call(
        kern, name=name, grid=(rows // rb,), in_specs=[spec] * 4, out_specs=(spec, spec, spec),
        out_shape=(shp, shp, shp), compiler_params=_cparams("parallel"))(w, g, m, v)


def _wada_grad(silu_t, dmod_cols):
    n = dmod_cols.shape[1]

    def kern(s_ref, d_ref, o_ref):
        acc = s_ref[:, 0:1] * d_ref[0:1, :]
        for b in range(1, N_DEV):
            acc = acc + s_ref[:, b:b + 1] * d_ref[b:b + 1, :]
        o_ref[...] = acc

    return pl.pallas_call(kern, name="wada_grad", out_shape=jax.ShapeDtypeStruct((D_MODEL, n), F32),
                          compiler_params=_cparams())(silu_t, dmod_cols)


def _rows(a, multiple):
    flat = a.reshape(-1)
    pad = (-flat.shape[0]) % (D_MODEL * multiple)
    if pad:
        flat = jnp.concatenate([flat, jnp.zeros((pad,), flat.dtype)])
    return flat.reshape(-1, D_MODEL)


def _part_rows(shape, multiple):
    return -(-int(np.prod(shape)) // (D_MODEL * multiple)) * multiple


def _pack_rows(parts, multiple):
    return jnp.concatenate([_rows(p, multiple) for p in parts], axis=0)


def _unpack_rows(buf, shapes, multiple):
    out, r = [], 0
    for shp in shapes:
        n = int(np.prod(shp))
        nr = _part_rows(shp, multiple)
        out.append(buf[r:r + nr].reshape(-1)[:n].reshape(shp))
        r += nr
    return out


def kernel(x, c, w_ada, b_ada, norm_pre, norm_post, w_in, pool_w, pool_scale, ssm_a_re, ssm_a_im, ssm_log_dt, ssm_b_re, ssm_b_im, ssm_c_re, ssm_c_im, ssm_d, glu_w, glu_b, w_branch_pool, w_branch_ssm, w_out, loss_target, m_w_ada, m_b_ada, m_norm_pre, m_norm_post, m_w_in, m_pool_w, m_pool_scale, m_ssm_a_re, m_ssm_a_im, m_ssm_log_dt, m_ssm_b_re, m_ssm_b_im, m_ssm_c_re, m_ssm_c_im, m_ssm_d, m_glu_w, m_glu_b, m_w_branch_pool, m_w_branch_ssm, m_w_out, v_w_ada, v_b_ada, v_norm_pre, v_norm_post, v_w_in, v_pool_w, v_pool_scale, v_ssm_a_re, v_ssm_a_im, v_ssm_log_dt, v_ssm_b_re, v_ssm_b_im, v_ssm_c_re, v_ssm_c_im, v_ssm_d, v_glu_w, v_glu_b, v_w_branch_pool, v_w_branch_ssm, v_w_out):
    n_ada = w_ada.shape[2]
    n_in = w_in.shape[2]
    n_row = glu_w.shape[1]
    n_pool = pool_w.shape[2]
    n_groups = pool_w.shape[1]

    big_shards = [w_ada[0], w_in[0], pool_w[0], glu_w[0], w_branch_pool[0], w_branch_ssm[0], w_out[0]]
    packed = _pad_rows(_pack_rows([s.astype(BF16) for s in big_shards], 2 * SUBLANES), 2 * COMM_CHUNKS * COMM_ROW_ALIGN)
    gathered = _ag_weights(packed)
    r = 0
    w_ada_bf = gathered[:, r:r + n_ada].reshape(N_CHIPS, D_MODEL, n_ada).transpose(1, 0, 2).reshape(D_MODEL, 3 * D_MODEL)
    r += n_ada
    w_in_bf = gathered[:, r:r + n_in].reshape(N_CHIPS, D_MODEL, n_in).transpose(1, 0, 2).reshape(D_MODEL, N_CHIPS * n_in)
    r += n_in
    pool_rows = n_groups * n_pool * POOL_GW // D_MODEL
    pool_w_bf = gathered[:, r:r + pool_rows].reshape(N_CHIPS, n_groups, n_pool, POOL_GW).transpose(1, 0, 2, 3)
    pool_w_bf = pool_w_bf.reshape(n_groups, POOL_GW, POOL_GW)
    r += pool_rows
    squares = []
    for _ in range(4):
        squares.append(gathered[:, r:r + n_row].reshape(D_MODEL, D_MODEL))
        r += n_row
    glu_w_bf, wbp_bf, wbs_bf, wout_bf = squares

    a_re, a_im, log_dt = ssm_a_re[0], ssm_a_im[0], ssm_log_dt[0].reshape(SSM_G, 1)
    b_re_t, b_im_t = ssm_b_re[0].transpose(2, 0, 1), ssm_b_im[0].transpose(2, 0, 1)
    res = _local_step(x[0], c, loss_target[0], w_ada_bf, b_ada, norm_pre, norm_post, w_in_bf, pool_w_bf, pool_scale,
                      a_re, a_im, log_dt, b_re_t, b_im_t, ssm_c_re[0], ssm_c_im[0], ssm_d[0], glu_w_bf, glu_b[0:1],
                      wbp_bf, wbs_bf, wout_bf)
    loss = lax.psum(res["loss"], ("x", "y", "c"))

    small_parts = [res["dmod"], res["silu_c"], res["dg1"], res["dg2"], res["d_pscale"], res["d_glu_b"], res["d_dskip"],
                   res["d_abar_re"], res["d_abar_im"], res["d_bb_re_t"], res["d_bb_im_t"], res["d_c_re"], res["d_c_im"]]
    small_shapes = [p.shape for p in small_parts]
    all_small, sum_small = _small_allgather_sum(
        _pad_rows(_pack_rows(small_parts, SUBLANES), COMM_CHUNKS * COMM_ROW_ALIGN))
    (g_b_ada, _, g_norm_pre, g_norm_post, g_pscale, g_glu_b, g_dskip, s_abar_re, s_abar_im, s_bb_re, s_bb_im,
     g_c_re, g_c_im) = _unpack_rows(sum_small, small_shapes, SUBLANES)
    g_a_re, g_a_im, g_log_dt, g_b_re_t, g_b_im_t = _ssm_params_bwd(
        a_re, a_im, log_dt, b_re_t, b_im_t, s_abar_re.reshape(SSM_G, SSM_P), s_abar_im.reshape(SSM_G, SSM_P),
        s_bb_re, s_bb_im)
    chip = 2 * lax.axis_index("x") + lax.axis_index("y")
    dmod_all = all_small[:, 0:3].reshape(N_DEV, 3 * D_MODEL)
    dmod_cols = lax.dynamic_slice_in_dim(dmod_all, chip * n_ada, n_ada, axis=1)
    silu_t = all_small[:, _part_rows(small_shapes[0], SUBLANES)].transpose(1, 0)
    g_w_ada = _wada_grad(silu_t, dmod_cols)

    def by_cols(a, n):
        return a.reshape(D_MODEL, N_CHIPS, n).transpose(1, 0, 2).reshape(N_CHIPS, -1, D_MODEL)

    def by_rows(a):
        return a.reshape(N_CHIPS, n_row, D_MODEL)

    pool_by_chip = res["d_pool_w"].reshape(n_groups, N_CHIPS, n_pool, POOL_GW).transpose(1, 0, 2, 3)
    g_packed = jnp.concatenate(
        [by_cols(res["d_win"], n_in), by_rows(res["d_glu_w"]), by_rows(res["d_wbp"]), by_rows(res["d_wbs"]),
         by_rows(res["d_wout"]), pool_by_chip.reshape(N_CHIPS, pool_rows, D_MODEL)], axis=1)
    g_packed = _pad_rows(g_packed, 2 * COMM_CHUNKS * COMM_ROW_ALIGN, axis=1)
    core = lax.axis_index("c").astype(jnp.int32).reshape(1)
    part_f32, part_bf = _pair_add(g_packed, _rs_pair(g_packed), core)
    shard = _rs_join(_chip_add(part_f32, _rs_chips(part_bf), chip.astype(jnp.int32).reshape(1)))
    r = 0
    g_w_in = shard[r:r + n_in].reshape(D_MODEL, n_in)
    r += n_in
    g_squares = []
    for _ in range(4):
        g_squares.append(shard[r:r + n_row])
        r += n_row
    g_glu_w, g_wbp, g_wbs, g_wout = g_squares
    g_pool_w = shard[r:r + pool_rows].reshape(n_groups * n_pool, POOL_GW)

    big = [("w_ada", w_ada[0], g_w_ada, m_w_ada[0], v_w_ada[0]),
           ("w_in", w_in[0], g_w_in, m_w_in[0], v_w_in[0]),
           ("pool_w", pool_w[0].reshape(n_groups * n_pool, POOL_GW), g_pool_w,
            m_pool_w[0].reshape(n_groups * n_pool, POOL_GW), v_pool_w[0].reshape(n_groups * n_pool, POOL_GW)),
           ("glu_w", glu_w[0], g_glu_w, m_glu_w[0], v_glu_w[0]),
           ("w_branch_pool", w_branch_pool[0], g_wbp, m_w_branch_pool[0], v_w_branch_pool[0]),
           ("w_branch_ssm", w_branch_ssm[0], g_wbs, m_w_branch_ssm[0], v_w_branch_ssm[0]),
           ("w_out", w_out[0], g_wout, m_w_out[0], v_w_out[0])]
    out = {}
    for name, w_, g_, m_, v_ in big:
        d_, nm_, nv_ = _adamw(w_, g_, m_, v_, "adamw_" + name)
        out[name] = (g_, d_, nm_, nv_)

    g_b_re = g_b_re_t.transpose(1, 2, 0)
    g_b_im = g_b_im_t.transpose(1, 2, 0)
    small = [("b_ada", b_ada, g_b_ada, m_b_ada, v_b_ada),
             ("norm_pre", norm_pre, g_norm_pre, m_norm_pre, v_norm_pre),
             ("norm_post", norm_post, g_norm_post, m_norm_post, v_norm_post),
             ("pool_scale", pool_scale, g_pscale, m_pool_scale, v_pool_scale),
             ("ssm_a_re", ssm_a_re, g_a_re, m_ssm_a_re, v_ssm_a_re),
             ("ssm_a_im", ssm_a_im, g_a_im, m_ssm_a_im, v_ssm_a_im),
             ("ssm_log_dt", ssm_log_dt, g_log_dt, m_ssm_log_dt, v_ssm_log_dt),
             ("ssm_b_re", ssm_b_re, g_b_re, m_ssm_b_re, v_ssm_b_re),
             ("ssm_b_im", ssm_b_im, g_b_im, m_ssm_b_im, v_ssm_b_im),
             ("ssm_c_re", ssm_c_re, g_c_re, m_ssm_c_re, v_ssm_c_re),
             ("ssm_c_im", ssm_c_im, g_c_im, m_ssm_c_im, v_ssm_c_im),
             ("ssm_d", ssm_d, g_dskip, m_ssm_d, v_ssm_d),
             ("glu_b", glu_b, g_glu_b, m_glu_b, v_glu_b)]
    shapes = [w_.shape for _, w_, _, _, _ in small]
    pw_, pg_, pm_, pv_ = (_pack_rows([t[i] for t in small], SUBLANES) for i in (1, 2, 3, 4))
    pd_, pnm_, pnv_ = _adamw(pw_, pg_, pm_, pv_, "adamw_small")
    unpacked = [_unpack_rows(p, shapes, SUBLANES) for p in (pg_, pd_, pnm_, pnv_)]
    for (name, _, _, _, _), g_, d_, nm_, nv_ in zip(small, *unpacked):
        out[name] = (g_, d_, nm_, nv_)

    order = ["w_ada", "b_ada", "norm_pre", "norm_post", "w_in", "pool_w", "pool_scale", "ssm_a_re", "ssm_a_im",
             "ssm_log_dt", "ssm_b_re", "ssm_b_im", "ssm_c_re", "ssm_c_im", "ssm_d", "glu_w", "glu_b", "w_branch_pool",
             "w_branch_ssm", "w_out"]
    ref_shape = dict(w_ada=w_ada.shape, w_in=w_in.shape, pool_w=pool_w.shape, glu_w=glu_w.shape,
                     w_branch_pool=w_branch_pool.shape, w_branch_ssm=w_branch_ssm.shape, w_out=w_out.shape)
    for name, w_, _, _, _ in small:
        ref_shape[name] = w_.shape
    results = [loss, res["grad_x"][None]]
    for k in range(4):
        results += [out[name][k].reshape(ref_shape[name]) for name in order]
    return tuple(results)
```

```python
import functools
import math

import numpy as np
import jax
import jax.numpy as jnp
from jax import lax
from jax.experimental import pallas as pl
from jax.experimental.pallas import tpu as pltpu

F32 = jnp.float32
BF16 = jnp.bfloat16
MESH_ID = pl.DeviceIdType.MESH

D_MODEL = 1024
LANES = 128
SUBLANES = 8
SSM_G, SSM_P, SSM_H = 64, 64, 16
LANE_BLOCKS = D_MODEL // LANES
GROUPS_PER_BLOCK = LANES // SSM_H
STATE_W = GROUPS_PER_BLOCK * SSM_P
STATE_ALL = SSM_G * SSM_P
POOL_WINDOWS = (2, 4, 8, 16)
POOL_GW = D_MODEL // len(POOL_WINDOWS)
HALO = 16
RMS_EPS = 1e-6
N_CHIPS = 4
N_DEV = 8

SCAN_CHUNK = 512
ROW_CHUNK = 256
VMEM_LIMIT_BYTES = 56 * 1024 * 1024

ADAM_BLOCK_BYTES = 1 << 20
ADAM_LR, ADAM_B1, ADAM_B2, ADAM_EPS, ADAM_WD, ADAM_STEP = 0.001, 0.9, 0.999, 1e-08, 0.01, 10

_GELU_C0 = math.sqrt(2.0 / math.pi)
_GELU_C1 = 0.044715


def _cparams(*sem):
    if sem:
        return pltpu.CompilerParams(dimension_semantics=sem, vmem_limit_bytes=VMEM_LIMIT_BYTES)
    return pltpu.CompilerParams(vmem_limit_bytes=VMEM_LIMIT_BYTES)


def _sigmoid(v):
    return jax.nn.sigmoid(v)


def _silu(v):
    return v * _sigmoid(v)


def _dsilu(v):
    s = _sigmoid(v)
    return s * (1.0 + v * (1.0 - s))


def _gelu(v):
    return 0.5 * v * (1.0 + jnp.tanh(_GELU_C0 * (v + _GELU_C1 * v * v * v)))


def _dgelu(v):
    t = jnp.tanh(_GELU_C0 * (v + _GELU_C1 * v * v * v))
    return 0.5 * (1.0 + t) + 0.5 * v * (1.0 - t * t) * _GELU_C0 * (1.0 + 3.0 * _GELU_C1 * v * v)


def _dot(a, b):
    return lax.dot_general(a, b, (((1,), (0,)), ((), ())), preferred_element_type=F32)


def _dot_nt(a, b):
    return lax.dot_general(a, b, (((1,), (1,)), ((), ())), preferred_element_type=F32)


def _dot_tn(a, b):
    return lax.dot_general(a, b, (((0,), (0,)), ((), ())), preferred_element_type=F32)


def _acc8(v):
    return v.reshape(v.shape[0] // SUBLANES, SUBLANES, v.shape[1]).sum(axis=0)


def _mm(a_parts, b_parts, *, name, ta=False, tb=False, out_dtype=F32, bm=512, bn=512, bk=512):
    a_parts, b_parts = list(a_parts), list(b_parts)
    if ta:
        assert len(a_parts) == 1
        k_dim, m_dim = a_parts[0].shape
    else:
        m_dim = a_parts[0].shape[0]
        k_dim = sum(a.shape[1] for a in a_parts)
    if tb:
        assert len(b_parts) == 1
        n_dim = b_parts[0].shape[0]
    else:
        n_dim = sum(b.shape[1] for b in b_parts)
    bm, bn, bk = min(bm, m_dim), min(bn, n_dim), min(bk, k_dim)
    nm, nn, nk = m_dim // bm, n_dim // bn, k_dim // bk
    a_ranges, off = [], 0
    for a in a_parts:
        cnt = (a.shape[0] if ta else a.shape[1]) // bk
        a_ranges.append((off, cnt))
        off += cnt
    b_ranges, off = [], 0
    for b in b_parts:
        cnt = (b.shape[0] if tb else b.shape[1]) // bn
        b_ranges.append((off, cnt))
        off += cnt

    def a_spec(off, cnt):
        if ta:
            return pl.BlockSpec((bk, bm), lambda i, n, k: (k, i))
        return pl.BlockSpec((bm, bk), lambda i, n, k: (i, jnp.clip(k - off, 0, cnt - 1)))

    def b_spec(off, cnt):
        if tb:
            return pl.BlockSpec((bn, bk), lambda i, n, k: (n, k))
        return pl.BlockSpec((bk, bn), lambda i, n, k: (k, jnp.clip(n - off, 0, cnt - 1)))

    na, nb = len(a_parts), len(b_parts)
    dims = (((0 if ta else 1,), (1 if tb else 0,)), ((), ()))

    def kern_single(a_ref, b_ref, o_ref):
        o_ref[...] = lax.dot_general(a_ref[...].astype(BF16), b_ref[...].astype(BF16), dims,
                                     preferred_element_type=F32).astype(out_dtype)

    if na == 1 and nb == 1 and nk == 1:
        return pl.pallas_call(
            kern_single, name=name, grid=(nm, nn),
            in_specs=[pl.BlockSpec((bk, bm), lambda i, n: (0, i)) if ta else pl.BlockSpec((bm, bk), lambda i, n: (i, 0)),
                      pl.BlockSpec((bn, bk), lambda i, n: (n, 0)) if tb else pl.BlockSpec((bk, bn), lambda i, n: (0, n))],
            out_specs=pl.BlockSpec((bm, bn), lambda i, n: (i, n)),
            out_shape=jax.ShapeDtypeStruct((m_dim, n_dim), out_dtype),
            compiler_params=_cparams("parallel", "parallel"),
        )(a_parts[0], b_parts[0])

    def kern(*refs):
        a_refs, b_refs = refs[:na], refs[na:na + nb]
        o_ref, acc = refs[na + nb], refs[na + nb + 1]
        n, k = pl.program_id(1), pl.program_id(2)

        @pl.when(k == 0)
        def _():
            acc[...] = jnp.zeros_like(acc)

        for ja, (koff, kcnt) in enumerate(a_ranges):
            for jb, (noff, ncnt) in enumerate(b_ranges):
                def step(ja=ja, jb=jb):
                    a = a_refs[ja][...].astype(BF16)
                    b = b_refs[jb][...].astype(BF16)
                    acc[...] += lax.dot_general(a, b, dims, preferred_element_type=F32)

                if na == 1 and nb == 1:
                    step()
                else:
                    cond = (k >= koff) & (k < koff + kcnt) & (n >= noff) & (n < noff + ncnt)
                    pl.when(cond)(step)

        @pl.when(k == nk - 1)
        def _():
            o_ref[...] = acc[...].astype(out_dtype)

    return pl.pallas_call(
        kern,
        name=name,
        grid=(nm, nn, nk),
        in_specs=[a_spec(*r) for r in a_ranges] + [b_spec(*r) for r in b_ranges],
        out_specs=pl.BlockSpec((bm, bn), lambda i, n, k: (i, n)),
        out_shape=jax.ShapeDtypeStruct((m_dim, n_dim), out_dtype),
        scratch_shapes=[pltpu.VMEM((bm, bn), F32)],
        compiler_params=_cparams("parallel", "parallel", "arbitrary"),
    )(*a_parts, *b_parts)


def _ssm_param_fn(a_re, a_im, log_dt, b_re, b_im):
    dt = jnp.exp(log_dt)
    lam_re = jnp.minimum(a_re, -1e-4)
    lam_im = a_im
    mag = jnp.exp(lam_re * dt)
    abar_re = mag * jnp.cos(lam_im * dt)
    abar_im = mag * jnp.sin(lam_im * dt)
    den = lam_re * lam_re + lam_im * lam_im
    num_re = abar_re - 1.0
    f_re = (num_re * lam_re + abar_im * lam_im) / den
    f_im = (abar_im * lam_re - num_re * lam_im) / den
    bb_re = f_re * b_re - f_im * b_im
    bb_im = f_re * b_im + f_im * b_re
    return abar_re, abar_im, bb_re, bb_im


def _ssm_params(a_re, a_im, log_dt, b_re_t, b_im_t):
    def kern(are, aim, ldt, bre, bim, o_ar, o_ai, o_br, o_bi):
        ar, ai, br, bi = _ssm_param_fn(are[...], aim[...], ldt[...], bre[...], bim[...])
        o_ar[...] = ar
        o_ai[...] = ai
        o_br[...] = br
        o_bi[...] = bi

    gp = jax.ShapeDtypeStruct((SSM_G, SSM_P), F32)
    hgp = jax.ShapeDtypeStruct((SSM_H, SSM_G, SSM_P), F32)
    return pl.pallas_call(kern, name="ssm_params", out_shape=(gp, gp, hgp, hgp), compiler_params=_cparams())(
        a_re, a_im, log_dt, b_re_t, b_im_t)


def _ssm_params_bwd(a_re, a_im, log_dt, b_re_t, b_im_t, d_ar, d_ai, d_bbr, d_bbi):
    def kern(are, aim, ldt, bre, bim, dar, dai, dbr, dbi, o_are, o_aim, o_ldt, o_bre, o_bim):
        prim = (are[...], aim[...], ldt[...], bre[...], bim[...])
        _, vjp = jax.vjp(_ssm_param_fn, *prim)
        g = vjp((dar[...], dai[...], dbr[...], dbi[...]))
        o_are[...] = g[0]
        o_aim[...] = g[1]
        o_ldt[...] = g[2]
        o_bre[...] = g[3]
        o_bim[...] = g[4]

    gp = jax.ShapeDtypeStruct((SSM_G, SSM_P), F32)
    g1 = jax.ShapeDtypeStruct((SSM_G, 1), F32)
    hgp = jax.ShapeDtypeStruct((SSM_H, SSM_G, SSM_P), F32)
    return pl.pallas_call(kern, name="ssm_params_bwd", out_shape=(gp, gp, g1, hgp, hgp), compiler_params=_cparams())(
        a_re, a_im, log_dt, b_re_t, b_im_t, d_ar, d_ai, d_bbr, d_bbi)


def _pow_tables(abar_re, abar_im, tc):
    ls = tc // SUBLANES

    def kern(ar_ref, ai_ref, fr_ref, fi_ref, rr_ref, ri_ref):
        a_re = jnp.broadcast_to(ar_ref[...], (SUBLANES, STATE_W))
        a_im = jnp.broadcast_to(ai_ref[...], (SUBLANES, STATE_W))
        p_re, p_im = a_re, a_im
        for i in range(ls):
            fwd = pl.ds(SUBLANES * i, SUBLANES)
            rev = pl.ds(SUBLANES * (ls - 1 - i), SUBLANES)
            fr_ref[fwd, :] = p_re
            fi_ref[fwd, :] = p_im
            rr_ref[rev, :] = p_re
            ri_ref[rev, :] = p_im
            p_re, p_im = p_re * a_re - p_im * a_im, p_re * a_im + p_im * a_re

    vec = pl.BlockSpec((1, STATE_W), lambda b: (0, b))
    tab = pl.BlockSpec((tc, STATE_W), lambda b: (0, b))
    shp = jax.ShapeDtypeStruct((tc, STATE_ALL), F32)
    return pl.pallas_call(
        kern, name="pow_tables", grid=(LANE_BLOCKS,), in_specs=[vec, vec], out_specs=(tab, tab, tab, tab),
        out_shape=(shp, shp, shp, shp), compiler_params=_cparams("parallel"))(abar_re, abar_im)


def _mod_kernel(c_row, w_ada_bf, b_ada):
    def kern(c_ref, w_ref, b_ref, m_ref, s_ref):
        cv = c_ref[...]
        sc = _silu(cv)
        s_ref[...] = sc
        lhs = jnp.broadcast_to(sc, (SUBLANES, D_MODEL)).astype(BF16)
        m_ref[...] = _dot(lhs, w_ref[...]) + b_ref[...]

    return pl.pallas_call(
        kern, name="ada_mod",
        out_shape=(jax.ShapeDtypeStruct((SUBLANES, 3 * D_MODEL), F32), jax.ShapeDtypeStruct((1, D_MODEL), F32)),
        compiler_params=_cparams())(c_row, w_ada_bf, b_ada)


def _row_spec(tr, width=D_MODEL, col=0):
    return pl.BlockSpec((tr, width), lambda c: (c, col))


def _vec_spec(width=D_MODEL):
    return pl.BlockSpec((1, width), lambda c: (0, 0))


def _in_norm(x, g1, scale, shift):
    seq = x.shape[0]
    tr = min(ROW_CHUNK, seq)

    def kern(x_ref, g_ref, sc_ref, sh_ref, h_ref):
        xv = x_ref[...]
        r = lax.rsqrt(jnp.mean(xv * xv, axis=-1, keepdims=True) + RMS_EPS)
        h_ref[...] = (((xv * r) * g_ref[...]) * (1.0 + sc_ref[...]) + sh_ref[...]).astype(BF16)

    return pl.pallas_call(
        kern, name="in_norm", grid=(seq // tr,),
        in_specs=[_row_spec(tr), _vec_spec(), _vec_spec(), _vec_spec()], out_specs=_row_spec(tr),
        out_shape=jax.ShapeDtypeStruct((seq, D_MODEL), BF16), compiler_params=_cparams("parallel"))(x, g1, scale, shift)


def _pool_windows(ext, pos, g, w, tr):
    cols = pl.ds(g * POOL_GW, POOL_GW)
    cur = ext[pl.ds(HALO, tr), cols]
    acc = cur
    for k in range(1, w):
        acc = acc + ext[pl.ds(HALO - k, tr), cols]
    cnt = jnp.minimum(pos + 1, w).astype(F32)
    return acc / cnt - cur


def _pool_fwd(proj, pool_w_bf, pscale):
    seq = proj.shape[0]
    tr = min(ROW_CHUNK, seq)
    hb = tr // HALO

    def kern(up_ref, halo_ref, zp_ref, pw_ref, ps_ref, y_ref, ext):
        c = pl.program_id(0)
        ext[0:HALO, :] = jnp.where(c > 0, halo_ref[...], 0.0)
        ext[HALO:, :] = up_ref[...]
        pos = c * tr + lax.broadcasted_iota(jnp.int32, (tr, POOL_GW), 0)
        for g, w in enumerate(POOL_WINDOWS):
            cols = pl.ds(g * POOL_GW, POOL_GW)
            pooled = _pool_windows(ext, pos, g, w, tr)
            mixed = _dot(pooled.astype(BF16), pw_ref[g])
            y_ref[:, cols] = (mixed * ps_ref[:, cols] * _silu(zp_ref[:, cols])).astype(BF16)

    return pl.pallas_call(
        kern, name="pool_fwd", grid=(seq // tr,),
        in_specs=[_row_spec(tr, col=0),
                  pl.BlockSpec((HALO, D_MODEL), lambda c: (jnp.maximum(c * hb - 1, 0), 0)),
                  _row_spec(tr, col=1),
                  pl.BlockSpec((len(POOL_WINDOWS), POOL_GW, POOL_GW), lambda c: (0, 0, 0)),
                  _vec_spec()],
        out_specs=_row_spec(tr), out_shape=jax.ShapeDtypeStruct((seq, D_MODEL), BF16),
        scratch_shapes=[pltpu.VMEM((tr + HALO, D_MODEL), F32)],
        compiler_params=_cparams("parallel"))(proj, proj, proj, pool_w_bf, pscale)


def _pool_bwd(proj, dyp, pool_w_bf, pscale):
    seq = proj.shape[0]
    tr = min(ROW_CHUNK, seq)
    hb = tr // HALO
    nc = seq // tr
    n_halo = seq // HALO

    def kern(up_ref, halo_ref, zp_ref, zpn_ref, dyp_ref, dypn_ref, pw_ref, ps_ref,
             d01_ref, dpw_ref, dps_ref, ext, dpn, acc_pw, acc_ps):
        c = pl.program_id(0)

        @pl.when(c == 0)
        def _():
            acc_pw[...] = jnp.zeros_like(acc_pw)
            acc_ps[...] = jnp.zeros_like(acc_ps)

        ext[0:HALO, :] = jnp.where(c > 0, halo_ref[...], 0.0)
        ext[HALO:, :] = up_ref[...]
        pos = c * tr + lax.broadcasted_iota(jnp.int32, (tr, POOL_GW), 0)
        pos_n = (c + 1) * tr + lax.broadcasted_iota(jnp.int32, (HALO, POOL_GW), 0)
        has_next = c < nc - 1
        for g, w in enumerate(POOL_WINDOWS):
            cols = pl.ds(g * POOL_GW, POOL_GW)
            pooled_bf = _pool_windows(ext, pos, g, w, tr).astype(BF16)
            wg = pw_ref[g]
            mixed = _dot(pooled_bf, wg)
            zp = zp_ref[:, cols]
            sz = _silu(zp)
            dyp_g = dyp_ref[:, cols]
            ps = ps_ref[:, cols]
            dmixed = (dyp_g * ps * sz).astype(BF16)
            acc_ps[:, cols] += _acc8(dyp_g * mixed * sz)
            d01_ref[:, pl.ds(D_MODEL + g * POOL_GW, POOL_GW)] = (dyp_g * mixed * ps * _dsilu(zp)).astype(BF16)
            acc_pw[g] += _dot_tn(pooled_bf, dmixed)
            dpooled = _dot_nt(dmixed, wg)
            dmixed_n = (jnp.where(has_next, dypn_ref[:, cols], 0.0) * ps * _silu(zpn_ref[:, cols])).astype(BF16)
            dpooled_n = _dot_nt(dmixed_n, wg)
            dpn[0:tr, :] = dpooled / jnp.minimum(pos + 1, w).astype(F32)
            dpn[tr:, :] = dpooled_n / jnp.minimum(pos_n + 1, w).astype(F32)
            acc = dpn[0:tr, :]
            for k in range(1, w):
                acc = acc + dpn[pl.ds(k, tr), :]
            d01_ref[:, cols] = (acc - dpooled).astype(BF16)

        @pl.when(c == nc - 1)
        def _():
            dpw_ref[...] = acc_pw[...]
            dps_ref[...] = jnp.sum(acc_ps[...], axis=0, keepdims=True)

    nxt = lambda c: (jnp.minimum((c + 1) * hb, n_halo - 1), 0)
    nxt1 = lambda c: (jnp.minimum((c + 1) * hb, n_halo - 1), 1)
    return pl.pallas_call(
        kern, name="pool_bwd", grid=(nc,),
        in_specs=[_row_spec(tr, col=0),
                  pl.BlockSpec((HALO, D_MODEL), lambda c: (jnp.maximum(c * hb - 1, 0), 0)),
                  _row_spec(tr, col=1),
                  pl.BlockSpec((HALO, D_MODEL), nxt1),
                  _row_spec(tr),
                  pl.BlockSpec((HALO, D_MODEL), nxt),
                  pl.BlockSpec((len(POOL_WINDOWS), POOL_GW, POOL_GW), lambda c: (0, 0, 0)),
                  _vec_spec()],
        out_specs=(pl.BlockSpec((tr, 2 * D_MODEL), lambda c: (c, 0)),
                   pl.BlockSpec((len(POOL_WINDOWS), POOL_GW, POOL_GW), lambda c: (0, 0, 0)),
                   _vec_spec()),
        out_shape=(jax.ShapeDtypeStruct((seq, 2 * D_MODEL), BF16),
                   jax.ShapeDtypeStruct((len(POOL_WINDOWS), POOL_GW, POOL_GW), F32),
                   jax.ShapeDtypeStruct((1, D_MODEL), F32)),
        scratch_shapes=[pltpu.VMEM((tr + HALO, D_MODEL), F32), pltpu.VMEM((tr + HALO, POOL_GW), F32),
                        pltpu.VMEM((len(POOL_WINDOWS), POOL_GW, POOL_GW), F32), pltpu.VMEM((SUBLANES, D_MODEL), F32)],
        compiler_params=_cparams("arbitrary"))(proj, proj, proj, proj, dyp, dyp, pool_w_bf, pscale)


def _glu_fwd(ys, proj, glu_w_bf, glu_b):
    seq = ys.shape[0]
    tr = min(ROW_CHUNK, seq)

    def kern(ys_ref, zs_ref, w_ref, b_ref, o_ref):
        yg = _gelu(ys_ref[...])
        q = _dot(yg.astype(BF16), w_ref[...]) + b_ref[...]
        o_ref[...] = (yg * _sigmoid(q) * _silu(zs_ref[...])).astype(BF16)

    return pl.pallas_call(
        kern, name="glu_fwd", grid=(seq // tr,),
        in_specs=[_row_spec(tr), _row_spec(tr, col=3), pl.BlockSpec((D_MODEL, D_MODEL), lambda c: (0, 0)), _vec_spec()],
        out_specs=_row_spec(tr), out_shape=jax.ShapeDtypeStruct((seq, D_MODEL), BF16),
        compiler_params=_cparams("parallel"))(ys, proj, glu_w_bf, glu_b)


def _glu_bwd(ys, proj, dyssm, glu_w_bf, glu_b):
    seq = ys.shape[0]
    tr = min(ROW_CHUNK, seq)
    nc = seq // tr

    def kern(ys_ref, zs_ref, dy_ref, w_ref, b_ref, dys_ref, dzs_ref, dq_ref, yg_ref, db_ref, acc_b):
        c = pl.program_id(0)

        @pl.when(c == 0)
        def _():
            acc_b[...] = jnp.zeros_like(acc_b)

        ysv = ys_ref[...]
        yg = _gelu(ysv)
        yg_bf = yg.astype(BF16)
        q = _dot(yg_bf, w_ref[...]) + b_ref[...]
        sg = _sigmoid(q)
        zs = zs_ref[...]
        dyv = dy_ref[...]
        dyglu = dyv * _silu(zs)
        dzs_ref[...] = (dyv * (yg * sg) * _dsilu(zs)).astype(BF16)
        dq = dyglu * yg * sg * (1.0 - sg)
        dq_bf = dq.astype(BF16)
        acc_b[...] += _acc8(dq)
        dyg = dyglu * sg + _dot_nt(dq_bf, w_ref[...])
        dys_ref[...] = dyg * _dgelu(ysv)
        dq_ref[...] = dq_bf
        yg_ref[...] = yg_bf

        @pl.when(c == nc - 1)
        def _():
            db_ref[...] = jnp.sum(acc_b[...], axis=0, keepdims=True)

    bf = jax.ShapeDtypeStruct((seq, D_MODEL), BF16)
    return pl.pallas_call(
        kern, name="glu_bwd", grid=(nc,),
        in_specs=[_row_spec(tr), _row_spec(tr, col=3), _row_spec(tr),
                  pl.BlockSpec((D_MODEL, D_MODEL), lambda c: (0, 0)), _vec_spec()],
        out_specs=(_row_spec(tr), _row_spec(tr), _row_spec(tr), _row_spec(tr), _vec_spec()),
        out_shape=(jax.ShapeDtypeStruct((seq, D_MODEL), F32), bf, bf, bf, jax.ShapeDtypeStruct((1, D_MODEL), F32)),
        scratch_shapes=[pltpu.VMEM((SUBLANES, D_MODEL), F32)],
        compiler_params=_cparams("arbitrary"))(ys, proj, dyssm, glu_w_bf, glu_b)


def _out_fwd_bwd(ypool, yssm, proj, x, tgt, gate, g2, wbp_bf, wbs_bf, wout_bf):
    seq = x.shape[0]
    tr = min(ROW_CHUNK, seq)
    nc = seq // tr

    def kern(yp_ref, ysm_ref, gp_ref, gs_ref, x_ref, t_ref, gate_ref, g2_ref, wbp_ref, wbs_ref, wo_ref,
             dy_ref, dyp_ref, dys_ref, d45_ref, mb_ref, dob_ref, dbp_ref, dbs_ref, loss_ref, dgate_ref, dg2_ref,
             acc_l, acc_gate, acc_g2):
        c = pl.program_id(0)

        @pl.when(c == 0)
        def _():
            acc_l[...] = jnp.zeros_like(acc_l)
            acc_gate[...] = jnp.zeros_like(acc_gate)
            acc_g2[...] = jnp.zeros_like(acc_g2)

        bp = _dot(yp_ref[...], wbp_ref[...])
        bs = _dot(ysm_ref[...], wbs_ref[...])
        sp = _sigmoid(gp_ref[...])
        ss = _sigmoid(gs_ref[...])
        mb = (sp * bp + ss * bs).astype(BF16)
        out = _dot(mb, wo_ref[...])
        r2 = lax.rsqrt(jnp.mean(out * out, axis=-1, keepdims=True) + RMS_EPS)
        oh = out * r2
        gate_v, g2_v = gate_ref[...], g2_ref[...]
        ohg = oh * g2_v
        diff = (x_ref[...] + gate_v * ohg) - t_ref[...]
        acc_l[...] += _acc8(diff * diff)
        dyv = diff * (1.0 / D_MODEL)
        dy_ref[...] = dyv
        acc_gate[...] += _acc8(dyv * ohg)
        t = dyv * gate_v
        acc_g2[...] += _acc8(t * oh)
        doh = t * g2_v
        dout = r2 * (doh - oh * jnp.mean(doh * oh, axis=-1, keepdims=True))
        dob = dout.astype(BF16)
        dmerged = _dot_nt(dob, wo_ref[...])
        dbp = (dmerged * sp).astype(BF16)
        dbs = (dmerged * ss).astype(BF16)
        d45_ref[:, 0:D_MODEL] = (dmerged * bp * sp * (1.0 - sp)).astype(BF16)
        d45_ref[:, D_MODEL:] = (dmerged * bs * ss * (1.0 - ss)).astype(BF16)
        dyp_ref[...] = _dot_nt(dbp, wbp_ref[...])
        dys_ref[...] = _dot_nt(dbs, wbs_ref[...])
        mb_ref[...] = mb
        dob_ref[...] = dob
        dbp_ref[...] = dbp
        dbs_ref[...] = dbs

        @pl.when(c == nc - 1)
        def _():
            tot = jnp.sum(acc_l[...], axis=0, keepdims=True)
            loss_ref[...] = jnp.sum(tot, axis=1, keepdims=True) * (0.5 / D_MODEL)
            dgate_ref[...] = jnp.sum(acc_gate[...], axis=0, keepdims=True)
            dg2_ref[...] = jnp.sum(acc_g2[...], axis=0, keepdims=True)

    wspec = pl.BlockSpec((D_MODEL, D_MODEL), lambda c: (0, 0))
    f32 = jax.ShapeDtypeStruct((seq, D_MODEL), F32)
    bf = jax.ShapeDtypeStruct((seq, D_MODEL), BF16)
    vec = jax.ShapeDtypeStruct((1, D_MODEL), F32)
    acc = pltpu.VMEM((SUBLANES, D_MODEL), F32)
    return pl.pallas_call(
        kern, name="out_fwd_bwd", grid=(nc,),
        in_specs=[_row_spec(tr), _row_spec(tr), _row_spec(tr, col=4), _row_spec(tr, col=5), _row_spec(tr), _row_spec(tr),
                  _vec_spec(), _vec_spec(), wspec, wspec, wspec],
        out_specs=(_row_spec(tr), _row_spec(tr), _row_spec(tr), pl.BlockSpec((tr, 2 * D_MODEL), lambda c: (c, 0)),
                   _row_spec(tr), _row_spec(tr), _row_spec(tr), _row_spec(tr),
                   pl.BlockSpec((1, 1), lambda c: (0, 0)), _vec_spec(), _vec_spec()),
        out_shape=(f32, f32, f32, jax.ShapeDtypeStruct((seq, 2 * D_MODEL), BF16), bf, bf, bf, bf,
                   jax.ShapeDtypeStruct((1, 1), F32), vec, vec),
        scratch_shapes=[acc, acc, acc],
        compiler_params=_cparams("arbitrary"))(ypool, yssm, proj, proj, x, tgt, gate, g2, wbp_bf, wbs_bf, wout_bf)


def _in_bwd(dh, x, dy, g1, scale):
    seq = x.shape[0]
    tr = min(ROW_CHUNK, seq)
    nc = seq // tr

    def kern(dh_ref, x_ref, dy_ref, g_ref, sc_ref, dx_ref, dsh_ref, dsc_ref, dg_ref, a_sh, a_sc, a_g):
        c = pl.program_id(0)

        @pl.when(c == 0)
        def _():
            a_sh[...] = jnp.zeros_like(a_sh)
            a_sc[...] = jnp.zeros_like(a_sc)
            a_g[...] = jnp.zeros_like(a_g)

        xv = x_ref[...]
        r = lax.rsqrt(jnp.mean(xv * xv, axis=-1, keepdims=True) + RMS_EPS)
        xh = xv * r
        g = g_ref[...]
        dhv = dh_ref[...]
        a_sh[...] += _acc8(dhv)
        a_sc[...] += _acc8(dhv * (xh * g))
        dn = dhv * (1.0 + sc_ref[...])
        a_g[...] += _acc8(dn * xh)
        dxh = dn * g
        dx_ref[...] = dy_ref[...] + r * (dxh - xh * jnp.mean(dxh * xh, axis=-1, keepdims=True))

        @pl.when(c == nc - 1)
        def _():
            dsh_ref[...] = jnp.sum(a_sh[...], axis=0, keepdims=True)
            dsc_ref[...] = jnp.sum(a_sc[...], axis=0, keepdims=True)
            dg_ref[...] = jnp.sum(a_g[...], axis=0, keepdims=True)

    vec = jax.ShapeDtypeStruct((1, D_MODEL), F32)
    acc = pltpu.VMEM((SUBLANES, D_MODEL), F32)
    return pl.pallas_call(
        kern, name="in_bwd", grid=(nc,),
        in_specs=[_row_spec(tr), _row_spec(tr), _row_spec(tr), _vec_spec(), _vec_spec()],
        out_specs=(_row_spec(tr), _vec_spec(), _vec_spec(), _vec_spec()),
        out_shape=(jax.ShapeDtypeStruct((seq, D_MODEL), F32), vec, vec, vec),
        scratch_shapes=[acc, acc, acc],
        compiler_params=_cparams("arbitrary"))(dh, x, dy, g1, scale)


def _perm_matrix(tc):
    ls = tc // SUBLANES
    r = np.arange(tc)
    m = np.zeros((tc, tc), np.float32)
    m[r, (r % SUBLANES) * ls + r // SUBLANES] = 1.0
    return m


def _local_scan(a_re, a_im, br, bi, xr, xi, row0, ls, reverse):
    x_re = jnp.zeros((SUBLANES, STATE_W), F32)
    x_im = jnp.zeros((SUBLANES, STATE_W), F32)
    for i in (range(ls - 1, -1, -1) if reverse else range(ls)):
        src = pl.ds(SUBLANES * i, SUBLANES)
        dst = pl.ds(row0 + SUBLANES * i, SUBLANES)
        n_re = a_re * x_re - a_im * x_im + br[src, :]
        n_im = a_re * x_im + a_im * x_re + bi[src, :]
        x_re, x_im = n_re, n_im
        xr[dst, :] = x_re
        xi[dst, :] = x_im
    return x_re, x_im


def _unpermute(pmt, v):
    hi = v.astype(BF16)
    lo = (v - hi.astype(F32)).astype(BF16)
    return _dot(pmt, hi) + _dot(pmt, lo)


def _unpermute_rhs(v, sel):
    hi = v.astype(BF16)
    r1 = v - hi.astype(F32)
    mid = r1.astype(BF16)
    lo = (r1 - mid.astype(F32)).astype(BF16)
    return _dot(hi, sel) + _dot(mid, sel) + _dot(lo, sel)


def _ssm_scan_fwd(proj, pm, pmt, bb_re, bb_im, cm_re, cm_im, abar_re, abar_im, pw_re, pw_im, d_skip, tc):
    seq = proj.shape[0]
    nc = seq // tc
    ls = tc // SUBLANES
    us_col0 = 2 * D_MODEL // LANES

    def kern(us_ref, pm_ref, pmt_ref, bbr_ref, bbi_ref, cmr_ref, cmi_ref, ar_ref, ai_ref, pwr_ref, pwi_ref, d_ref,
             ys_ref, ecr_ref, eci_ref, bur, bui, car_r, car_i, end_r, end_i):
        c = pl.program_id(1)

        @pl.when(c == 0)
        def _():
            car_r[...] = jnp.zeros_like(car_r)
            car_i[...] = jnp.zeros_like(car_i)

        u = us_ref[...]
        up = _dot(pm_ref[...], u.astype(BF16)).astype(BF16)
        bur[...] = _dot(up, bbr_ref[0])
        bui[...] = _dot(up, bbi_ref[0])
        a_re = jnp.broadcast_to(ar_ref[...], (SUBLANES, STATE_W))
        a_im = jnp.broadcast_to(ai_ref[...], (SUBLANES, STATE_W))
        x_re, x_im = _local_scan(a_re, a_im, bur, bui, bur, bui, 0, ls, False)
        end_r[...] = x_re
        end_i[...] = x_im
        big_re = pwr_ref[tc - 1:tc, :]
        big_im = pwi_ref[tc - 1:tc, :]
        e_re = car_r[0:1, :]
        e_im = car_i[0:1, :]
        for s in range(SUBLANES):
            n_re = end_r[s:s + 1, :] + big_re * e_re - big_im * e_im
            n_im = end_i[s:s + 1, :] + big_re * e_im + big_im * e_re
            e_re, e_im = n_re, n_im
            if s < SUBLANES - 1:
                car_r[s + 1:s + 2, :] = e_re
                car_i[s + 1:s + 2, :] = e_im
        ec_re = car_r[...]
        ec_im = car_i[...]
        ecr_ref[...] = ec_re
        eci_ref[...] = ec_im
        p_re = pwr_ref[...].reshape(ls, SUBLANES, STATE_W)
        p_im = pwi_ref[...].reshape(ls, SUBLANES, STATE_W)
        xf_re = bur[...].reshape(ls, SUBLANES, STATE_W) + p_re * ec_re[None] - p_im * ec_im[None]
        xf_im = bui[...].reshape(ls, SUBLANES, STATE_W) + p_re * ec_im[None] + p_im * ec_re[None]
        xb_re = xf_re.reshape(tc, STATE_W).astype(BF16)
        xb_im = xf_im.reshape(tc, STATE_W).astype(BF16)
        y_perm = _dot(xb_re, cmr_ref[0]) - _dot(xb_im, cmi_ref[0])
        ys_ref[...] = _unpermute(pmt_ref[...], y_perm) + d_ref[...] * u
        car_r[0:1, :] = e_re
        car_i[0:1, :] = e_im

    sq = pl.BlockSpec((tc, tc), lambda b, c: (0, 0))
    vec = pl.BlockSpec((1, STATE_W), lambda b, c: (0, b))
    tab = pl.BlockSpec((tc, STATE_W), lambda b, c: (0, b))
    car = pl.BlockSpec((SUBLANES, STATE_W), lambda b, c: (c, b))
    carry_shape = jax.ShapeDtypeStruct((nc * SUBLANES, STATE_ALL), F32)
    small = pltpu.VMEM((SUBLANES, STATE_W), F32)
    big = pltpu.VMEM((tc, STATE_W), F32)
    return pl.pallas_call(
        kern, name="ssm_scan_fwd", grid=(LANE_BLOCKS, nc),
        in_specs=[pl.BlockSpec((tc, LANES), lambda b, c: (c, us_col0 + b)), sq, sq,
                  pl.BlockSpec((1, LANES, STATE_W), lambda b, c: (b, 0, 0)),
                  pl.BlockSpec((1, LANES, STATE_W), lambda b, c: (b, 0, 0)),
                  pl.BlockSpec((1, STATE_W, LANES), lambda b, c: (b, 0, 0)),
                  pl.BlockSpec((1, STATE_W, LANES), lambda b, c: (b, 0, 0)),
                  vec, vec, tab, tab, pl.BlockSpec((1, LANES), lambda b, c: (0, b))],
        out_specs=(pl.BlockSpec((tc, LANES), lambda b, c: (c, b)), car, car),
        out_shape=(jax.ShapeDtypeStruct((seq, D_MODEL), F32), carry_shape, carry_shape),
        scratch_shapes=[big, big, small, small, small, small],
        compiler_params=_cparams("parallel", "arbitrary"),
    )(proj, pm, pmt, bb_re, bb_im, cm_re, cm_im, abar_re, abar_im, pw_re, pw_im, d_skip)


def _ssm_scan_bwd(proj, dys, ec_re, ec_im, pm, pmt, bb_re, bb_im, cm_re, cm_im, abar_re, abar_im,
                  pw_re, pw_im, pv_re, pv_im, d_skip, tc):
    seq = proj.shape[0]
    nc = seq // tc
    ls = tc // SUBLANES
    us_col0 = 2 * D_MODEL // LANES

    def kern(us_ref, dys_ref, ecr_ref, eci_ref, pm_ref, pmt_ref, bbr_ref, bbi_ref, cmr_ref, cmi_ref, ar_ref, ai_ref,
             pwr_ref, pwi_ref, pvr_ref, pvi_ref, d_ref,
             dus_ref, dbbr_ref, dbbi_ref, dcmr_ref, dcmi_ref, dar_ref, dai_ref, dd_ref,
             bur, bui, xr, xi, gr, gi, fc_r, fc_i, a_bbr, a_bbi, a_cmr, a_cmi, a_ar, a_ai, a_dd):
        c = pl.program_id(1)

        @pl.when(c == 0)
        def _():
            for ref in (fc_r, fc_i, a_bbr, a_bbi, a_cmr, a_cmi, a_ar, a_ai, a_dd):
                ref[...] = jnp.zeros_like(ref)

        u = us_ref[...]
        dysv = dys_ref[...]
        a_dd[...] += _acc8(dysv * u)
        up = _dot(pm_ref[...], u.astype(BF16)).astype(BF16)
        bur[...] = _dot(up, bbr_ref[0])
        bui[...] = _dot(up, bbi_ref[0])
        a_re = jnp.broadcast_to(ar_ref[...], (SUBLANES, STATE_W))
        a_im = jnp.broadcast_to(ai_ref[...], (SUBLANES, STATE_W))
        _local_scan(a_re, a_im, bur, bui, xr, xi, SUBLANES, ls, False)
        ec_r = ecr_ref[...]
        ec_i = eci_ref[...]
        xr[0:SUBLANES, :] = ec_r
        xi[0:SUBLANES, :] = ec_i
        p_re = pwr_ref[...].reshape(ls, SUBLANES, STATE_W)
        p_im = pwi_ref[...].reshape(ls, SUBLANES, STATE_W)
        xl_re = xr[SUBLANES:, :].reshape(ls, SUBLANES, STATE_W)
        xl_im = xi[SUBLANES:, :].reshape(ls, SUBLANES, STATE_W)
        xf_re = (xl_re + p_re * ec_r[None] - p_im * ec_i[None]).reshape(tc, STATE_W)
        xf_im = (xl_im + p_re * ec_i[None] + p_im * ec_r[None]).reshape(tc, STATE_W)
        xr[SUBLANES:, :] = xf_re
        xi[SUBLANES:, :] = xf_im
        dysp = _dot(pm_ref[...], dysv.astype(BF16)).astype(BF16)
        a_cmr[...] += _dot_tn(dysp, xf_re.astype(BF16))
        a_cmi[...] -= _dot_tn(dysp, xf_im.astype(BF16))
        gr[...] = _dot_nt(dysp, cmr_ref[0])
        gi[...] = -_dot_nt(dysp, cmi_ref[0])
        _local_scan(a_re, -a_im, gr, gi, gr, gi, 0, ls, True)
        big_re = pwr_ref[tc - 1:tc, :]
        big_im = -pwi_ref[tc - 1:tc, :]
        f_re = fc_r[SUBLANES - 1:SUBLANES, :]
        f_im = fc_i[SUBLANES - 1:SUBLANES, :]
        for s in range(SUBLANES - 1, -1, -1):
            n_re = gr[s:s + 1, :] + big_re * f_re - big_im * f_im
            n_im = gi[s:s + 1, :] + big_re * f_im + big_im * f_re
            f_re, f_im = n_re, n_im
            if s > 0:
                fc_r[s - 1:s, :] = f_re
                fc_i[s - 1:s, :] = f_im
        fcv_r = fc_r[...]
        fcv_i = fc_i[...]
        q_re = pvr_ref[...].reshape(ls, SUBLANES, STATE_W)
        q_im = -pvi_ref[...].reshape(ls, SUBLANES, STATE_W)
        lam_re = (gr[...].reshape(ls, SUBLANES, STATE_W) + q_re * fcv_r[None] - q_im * fcv_i[None]).reshape(tc, STATE_W)
        lam_im = (gi[...].reshape(ls, SUBLANES, STATE_W) + q_re * fcv_i[None] + q_im * fcv_r[None]).reshape(tc, STATE_W)
        fc_r[SUBLANES - 1:SUBLANES, :] = f_re
        fc_i[SUBLANES - 1:SUBLANES, :] = f_im
        xp_re = xr[0:tc, :]
        xp_im = xi[0:tc, :]
        a_ar[...] += _acc8(lam_re * xp_re + lam_im * xp_im)
        a_ai[...] += _acc8(lam_im * xp_re - lam_re * xp_im)
        lb_re = lam_re.astype(BF16)
        lb_im = lam_im.astype(BF16)
        a_bbr[...] += _dot_tn(up, lb_re)
        a_bbi[...] += _dot_tn(up, lb_im)
        dus_perm = _dot_nt(lb_re, bbr_ref[0]) + _dot_nt(lb_im, bbi_ref[0])
        dus_ref[...] = _unpermute(pmt_ref[...], dus_perm) + dysv * d_ref[...]

        @pl.when(c == nc - 1)
        def _():
            row_g = lax.broadcasted_iota(jnp.int32, (LANES, STATE_W), 0) // SSM_H
            col_g = lax.broadcasted_iota(jnp.int32, (LANES, STATE_W), 1) // SSM_P
            fold = (lax.broadcasted_iota(jnp.int32, (STATE_W, SSM_P), 0) % SSM_P
                    == lax.broadcasted_iota(jnp.int32, (STATE_W, SSM_P), 1)).astype(BF16)
            for acc, out in ((a_bbr, dbbr_ref), (a_bbi, dbbi_ref), (a_cmr, dcmr_ref), (a_cmi, dcmi_ref)):
                out[...] = _unpermute_rhs(jnp.where(row_g == col_g, acc[...], 0.0), fold)
            dar_ref[...] = jnp.sum(a_ar[...], axis=0, keepdims=True)
            dai_ref[...] = jnp.sum(a_ai[...], axis=0, keepdims=True)
            dd_ref[...] = jnp.sum(a_dd[...], axis=0, keepdims=True)

    rc = lambda c: nc - 1 - c
    sq = pl.BlockSpec((tc, tc), lambda b, c: (0, 0))
    vec = pl.BlockSpec((1, STATE_W), lambda b, c: (0, b))
    tab = pl.BlockSpec((tc, STATE_W), lambda b, c: (0, b))
    car = pl.BlockSpec((SUBLANES, STATE_W), lambda b, c: (rc(c), b))
    bblk = pl.BlockSpec((1, LANES, STATE_W), lambda b, c: (b, 0, 0))
    cblk = pl.BlockSpec((1, STATE_W, LANES), lambda b, c: (b, 0, 0))
    ghp = pl.BlockSpec((LANES, SSM_P), lambda b, c: (b, 0))
    ghp_shape = jax.ShapeDtypeStruct((SSM_G * SSM_H, SSM_P), F32)
    dvec = pl.BlockSpec((1, LANES), lambda b, c: (0, b))
    small = pltpu.VMEM((SUBLANES, STATE_W), F32)
    big = pltpu.VMEM((tc, STATE_W), F32)
    bigp = pltpu.VMEM((tc + SUBLANES, STATE_W), F32)
    blk = pltpu.VMEM((LANES, STATE_W), F32)
    return pl.pallas_call(
        kern, name="ssm_scan_bwd", grid=(LANE_BLOCKS, nc),
        in_specs=[pl.BlockSpec((tc, LANES), lambda b, c: (rc(c), us_col0 + b)),
                  pl.BlockSpec((tc, LANES), lambda b, c: (rc(c), b)),
                  car, car, sq, sq, bblk, bblk, cblk, cblk, vec, vec, tab, tab, tab, tab, dvec],
        out_specs=(pl.BlockSpec((tc, LANES), lambda b, c: (rc(c), b)), ghp, ghp, ghp, ghp, vec, vec, dvec),
        out_shape=(jax.ShapeDtypeStruct((seq, D_MODEL), F32), ghp_shape, ghp_shape, ghp_shape, ghp_shape,
                   jax.ShapeDtypeStruct((1, STATE_ALL), F32), jax.ShapeDtypeStruct((1, STATE_ALL), F32),
                   jax.ShapeDtypeStruct((1, D_MODEL), F32)),
        scratch_shapes=[big, big, bigp, bigp, big, big, small, small, blk, blk, blk, blk,
                        small, small, pltpu.VMEM((SUBLANES, LANES), F32)],
        compiler_params=_cparams("parallel", "arbitrary"),
    )(proj, dys, ec_re, ec_im, pm, pmt, bb_re, bb_im, cm_re, cm_im, abar_re, abar_im, pw_re, pw_im, pv_re, pv_im, d_skip)


def _eye5():
    return jnp.eye(GROUPS_PER_BLOCK, dtype=F32)[None, :, None, :, None]


def _embed_b(bb_t):
    t = bb_t.transpose(1, 0, 2).reshape(LANE_BLOCKS, GROUPS_PER_BLOCK, SSM_H, 1, SSM_P)
    return (t * _eye5()).reshape(LANE_BLOCKS, LANES, STATE_W)


def _embed_c(c_ghp):
    t = c_ghp.transpose(0, 2, 1).reshape(LANE_BLOCKS, GROUPS_PER_BLOCK, SSM_P, 1, SSM_H)
    return (t * _eye5()).reshape(LANE_BLOCKS, STATE_W, LANES)


def _local_step(x, c_row, tgt, w_ada_bf, b_ada, g1, g2, w_in_bf, pool_w_bf, pscale, a_re, a_im, log_dt,
                b_re_t, b_im_t, c_re, c_im, d_skip, glu_w_bf, glu_b, wbp_bf, wbs_bf, wout_bf):
    seq = x.shape[0]
    tc = min(SCAN_CHUNK, seq)
    mod8, silu_c = _mod_kernel(c_row, w_ada_bf, b_ada)
    mod = mod8[0:1]
    shift, scale, gate = mod[:, 0:D_MODEL], mod[:, D_MODEL:2 * D_MODEL], mod[:, 2 * D_MODEL:]

    abar_re, abar_im, bb_re_t, bb_im_t = _ssm_params(a_re, a_im, log_dt, b_re_t, b_im_t)
    abar_re_f, abar_im_f = abar_re.reshape(1, STATE_ALL), abar_im.reshape(1, STATE_ALL)
    pw_re, pw_im, pv_re, pv_im = _pow_tables(abar_re_f, abar_im_f, tc)
    bbe_re, bbe_im = _embed_b(bb_re_t).astype(BF16), _embed_b(bb_im_t).astype(BF16)
    cme_re, cme_im = _embed_c(c_re).astype(BF16), _embed_c(c_im).astype(BF16)
    pm_np = _perm_matrix(tc)
    pm, pmt = jnp.asarray(pm_np, BF16), jnp.asarray(pm_np.T, BF16)
    d_row = d_skip.reshape(1, D_MODEL)

    h = _in_norm(x, g1, scale, shift)
    proj = _mm([h], [w_in_bf], name="proj", bm=1024, bn=1024, bk=1024)
    ypool = _pool_fwd(proj, pool_w_bf, pscale)
    ys, ec_re, ec_im = _ssm_scan_fwd(proj, pm, pmt, bbe_re, bbe_im, cme_re, cme_im, abar_re_f, abar_im_f,
                                      pw_re, pw_im, d_row, tc)
    yssm = _glu_fwd(ys, proj, glu_w_bf, glu_b)
    (dy, dypool, dyssm, d45, mb, dob, dbp, dbs, loss, dgate, dg2) = _out_fwd_bwd(
        ypool, yssm, proj, x, tgt, gate, g2, wbp_bf, wbs_bf, wout_bf)

    d_wout = _mm([mb], [dob], ta=True, name="dw_out", bm=1024, bk=1024)
    d_wbp = _mm([ypool], [dbp], ta=True, name="dw_bp", bm=1024, bk=1024)
    d_wbs = _mm([yssm], [dbs], ta=True, name="dw_bs", bm=1024, bk=1024)
    dys, dzs, dq, yg, d_glu_b = _glu_bwd(ys, proj, dyssm, glu_w_bf, glu_b)
    d_glu_w = _mm([yg], [dq], ta=True, name="dw_glu", bm=1024, bk=1024)
    (dus, dbbe_re, dbbe_im, dcme_re, dcme_im, d_abar_re, d_abar_im, d_dskip) = _ssm_scan_bwd(
        proj, dys, ec_re, ec_im, pm, pmt, bbe_re, bbe_im, cme_re, cme_im, abar_re_f, abar_im_f,
        pw_re, pw_im, pv_re, pv_im, d_row, tc)
    d01, d_pool_w, d_pscale = _pool_bwd(proj, dypool, pool_w_bf, pscale)
    dparts = [d01, dus, dzs, d45]
    dh = _mm(dparts, [w_in_bf], tb=True, name="dh", bm=1024, bn=1024, bk=512)
    d_win = _mm([h], dparts, ta=True, name="dw_in", bm=1024, bk=1024)
    grad_x, dshift, dscale, dg1 = _in_bwd(dh, x, dy, g1, scale)
    dmod = jnp.concatenate([dshift, dscale, dgate], axis=1)
    return dict(
        loss=loss[0, 0], grad_x=grad_x, dmod=dmod, silu_c=silu_c, dg1=dg1, dg2=dg2, d_pscale=d_pscale,
        d_glu_b=d_glu_b, d_dskip=d_dskip, d_abar_re=d_abar_re, d_abar_im=d_abar_im,
        d_bb_re_t=dbbe_re.reshape(SSM_G, SSM_H, SSM_P).transpose(1, 0, 2),
        d_bb_im_t=dbbe_im.reshape(SSM_G, SSM_H, SSM_P).transpose(1, 0, 2),
        d_c_re=dcme_re.reshape(SSM_G, SSM_H, SSM_P), d_c_im=dcme_im.reshape(SSM_G, SSM_H, SSM_P),
        d_win=d_win, d_glu_w=d_glu_w, d_wbp=d_wbp, d_wbs=d_wbs, d_wout=d_wout, d_pool_w=d_pool_w)


def _position():
    x, y, c = lax.axis_index("x"), lax.axis_index("y"), lax.axis_index("c")
    chips = [(1 - x, y), (x, 1 - y), (1 - x, 1 - y)]
    return x, y, c, chips


_ANY = pl.BlockSpec(memory_space=pl.ANY)
COMM_CHUNKS = 4
COMM_ROW_ALIGN = 16


def _row_chunks(rows, k):
    assert rows % (k * COMM_ROW_ALIGN) == 0, (rows, k)
    step = rows // k
    return [(q * step, step) for q in range(k)]


def _pad_rows(buf, multiple, axis=0):
    pad = (-buf.shape[axis]) % multiple
    if not pad:
        return buf
    shape = list(buf.shape)
    shape[axis] = pad
    return jnp.concatenate([buf, jnp.zeros(shape, buf.dtype)], axis=axis)


def _ag_weights(packed):
    rows, width = packed.shape
    half = rows // 2
    chunks = _row_chunks(half, COMM_CHUNKS)
    nq = len(chunks)

    def body(p_ref, out_ref, send_sems, recv_sems):
        x, y, c, chips = _position()
        sibling = (x, y, 1 - c)

        def copy(k, chip, h, q, to, src=None):
            start, size = chunks[q]
            rows_q = pl.ds(h * half + start, size)
            dst = out_ref.at[2 * chip[0] + chip[1], rows_q, :]
            return pltpu.make_async_remote_copy(
                src_ref=dst if src is None else src.at[rows_q, :], dst_ref=dst, send_sem=send_sems.at[k * nq + q],
                recv_sem=recv_sems.at[k * nq + q], device_id=to, device_id_type=MESH_ID)

        mine = [copy(6 + h, (x, y), h, q, sibling, src=p_ref) for h in range(2) for q in range(nq)]
        first = [copy(j, (x, y), c, q, (*chip, c), src=p_ref) for q in range(nq) for j, chip in enumerate(chips)]
        for cp in first + mine:
            cp.start()
        passed = []
        for q in range(nq):
            for j, chip in enumerate(chips):
                copy(j, chip, c, q, (x, y, c)).wait_recv()
                fwd = copy(3 + j, chip, c, q, sibling)
                fwd.start()
                passed.append(fwd)
        for q in range(nq):
            for j, chip in enumerate(chips):
                copy(3 + j, chip, 1 - c, q, (x, y, c)).wait_recv()
        for cp in mine:
            cp.wait_recv()
        for cp in first + passed + mine:
            cp.wait_send()

    return pl.pallas_call(
        body, name="ag_weights", in_specs=[_ANY], out_specs=_ANY,
        out_shape=jax.ShapeDtypeStruct((N_CHIPS, rows, width), packed.dtype),
        scratch_shapes=[pltpu.SemaphoreType.DMA((8 * nq,)), pltpu.SemaphoreType.DMA((8 * nq,))],
    )(packed)


def _small_allgather_sum(buf):
    rows, width = buf.shape
    chunks = _row_chunks(rows, COMM_CHUNKS)
    nq = len(chunks)

    def body(b_ref, all_ref, sum_ref, send_sems, recv_sems, local_sem):
        x, y, c, chips = _position()
        me, sibling = (x, y, c), (x, y, 1 - c)

        def slot(px, py, pc):
            return all_ref.at[4 * px + 2 * py + pc]

        def copy(k, block, q, to, src=None):
            rows_q = pl.ds(chunks[q][0], chunks[q][1])
            dst = slot(*block).at[rows_q, :]
            return pltpu.make_async_remote_copy(
                src_ref=dst if src is None else src.at[rows_q, :], dst_ref=dst, send_sem=send_sems.at[k * nq + q],
                recv_sem=recv_sems.at[k * nq + q], device_id=to, device_id_type=MESH_ID)

        mine = pltpu.make_async_copy(b_ref, slot(*me), local_sem)
        mine.start()
        first = []
        for q in range(nq):
            first += [copy(1 + j, me, q, (*chip, c), src=b_ref) for j, chip in enumerate(chips)]
            first.append(copy(0, me, q, sibling, src=b_ref))
        for cp in first:
            cp.start()
        passed = []
        for q in range(nq):
            for j, chip in enumerate(chips):
                copy(1 + j, (*chip, c), q, me).wait_recv()
                fwd = copy(4 + j, (*chip, c), q, sibling)
                fwd.start()
                passed.append(fwd)
        for q in range(nq):
            copy(0, sibling, q, me).wait_recv()
            for j, chip in enumerate(chips):
                copy(4 + j, (*chip, 1 - c), q, me).wait_recv()
        for cp in first + passed:
            cp.wait_send()
        mine.wait()
        total = all_ref[0]
        for d in range(1, N_DEV):
            total = total + all_ref[d]
        sum_ref[...] = total

    vm = pl.BlockSpec(memory_space=pltpu.VMEM)
    return pl.pallas_call(
        body, name="small_allgather_sum", in_specs=[vm], out_specs=(vm, vm),
        out_shape=(jax.ShapeDtypeStruct((N_DEV, rows, width), F32), jax.ShapeDtypeStruct((rows, width), F32)),
        scratch_shapes=[pltpu.SemaphoreType.DMA((7 * nq,)), pltpu.SemaphoreType.DMA((7 * nq,)), pltpu.SemaphoreType.DMA],
        compiler_params=_cparams(),
    )(buf)


def _rs_pair(g):
    n, rows, width = g.shape
    half = rows // 2
    chunks = _row_chunks(half, COMM_CHUNKS)
    nq = len(chunks)

    def body(g_ref, got_ref, send_sems, recv_sems):
        x, y, c, _ = _position()
        swaps = []
        for k in range(n):
            for q, (start, size) in enumerate(chunks):
                swaps.append(pltpu.make_async_remote_copy(
                    src_ref=g_ref.at[k, pl.ds((1 - c) * half + start, size), :], dst_ref=got_ref.at[k, pl.ds(start, size), :],
                    send_sem=send_sems.at[k * nq + q], recv_sem=recv_sems.at[k * nq + q],
                    device_id=(x, y, 1 - c), device_id_type=MESH_ID))
        for cp in swaps:
            cp.start()
        for cp in swaps:
            cp.wait()

    return pl.pallas_call(
        body, name="rs_pair", in_specs=[_ANY], out_specs=_ANY, out_shape=jax.ShapeDtypeStruct((n, half, width), g.dtype),
        scratch_shapes=[pltpu.SemaphoreType.DMA((n * nq,)), pltpu.SemaphoreType.DMA((n * nq,))],
    )(g)


def _rs_chips(part_bf):
    n, rows, width = part_bf.shape
    chunks = _row_chunks(rows, COMM_CHUNKS)
    nq = len(chunks)

    def body(pb_ref, got_ref, send_sems, recv_sems):
        x, y, c, chips = _position()
        sends = []
        for q, (start, size) in enumerate(chunks):
            for j, chip in enumerate(chips):
                sends.append(pltpu.make_async_remote_copy(
                    src_ref=pb_ref.at[2 * chip[0] + chip[1], pl.ds(start, size), :], dst_ref=got_ref.at[j, pl.ds(start, size), :],
                    send_sem=send_sems.at[j * nq + q], recv_sem=recv_sems.at[j * nq + q],
                    device_id=(*chip, c), device_id_type=MESH_ID))
        for cp in sends:
            cp.start()
        for cp in sends:
            cp.wait()

    return pl.pallas_call(
        body, name="rs_chips", in_specs=[_ANY], out_specs=_ANY,
        out_shape=jax.ShapeDtypeStruct((N_CHIPS - 1, rows, width), BF16),
        scratch_shapes=[pltpu.SemaphoreType.DMA((3 * nq,)), pltpu.SemaphoreType.DMA((3 * nq,))],
    )(part_bf)


def _rs_join(shard):
    rows, width = shard.shape
    half = rows // 2
    chunks = _row_chunks(half, COMM_CHUNKS)
    nq = len(chunks)

    def body(in_ref, out_ref, send_sems, recv_sems):
        x, y, c, _ = _position()
        def swap(q, h):
            rows_q = pl.ds(h * half + chunks[q][0], chunks[q][1])
            return pltpu.make_async_remote_copy(
                src_ref=in_ref.at[rows_q, :], dst_ref=out_ref.at[rows_q, :], send_sem=send_sems.at[q],
                recv_sem=recv_sems.at[q], device_id=(x, y, 1 - c), device_id_type=MESH_ID)

        for q in range(nq):
            swap(q, c).start()
        for q in range(nq):
            swap(q, 1 - c).wait_recv()
        for q in range(nq):
            swap(q, c).wait_send()

    return pl.pallas_call(
        body, name="rs_join", in_specs=[_ANY], out_specs=_ANY, input_output_aliases={0: 0},
        out_shape=jax.ShapeDtypeStruct(shard.shape, shard.dtype),
        scratch_shapes=[pltpu.SemaphoreType.DMA((nq,)), pltpu.SemaphoreType.DMA((nq,))],
    )(shard)


def _pair_add(g, got, core):
    n, half, width = got.shape
    nb = 2
    rb = half // nb

    def kern(c_ref, a_ref, b_ref, f_ref, h_ref):
        s = a_ref[...] + b_ref[...]
        f_ref[...] = s
        h_ref[...] = s.astype(BF16)

    spec = pl.BlockSpec((1, rb, width), lambda k, i, c_ref: (k, i, 0))
    return pl.pallas_call(
        kern, name="rs_pair_add",
        grid_spec=pltpu.PrefetchScalarGridSpec(
            num_scalar_prefetch=1, grid=(n, nb),
            in_specs=[pl.BlockSpec((1, rb, width), lambda k, i, c_ref: (k, c_ref[0] * nb + i, 0)), spec],
            out_specs=(spec, spec)),
        out_shape=(jax.ShapeDtypeStruct(got.shape, F32), jax.ShapeDtypeStruct(got.shape, BF16)),
        compiler_params=_cparams("parallel", "parallel"))(core, g, got)


def _chip_add(part_f32, got, where):
    _, rows, width = part_f32.shape
    nb = 2
    rb = rows // nb

    def kern(w_ref, a_ref, b_ref, o_ref):
        o_ref[...] = ((a_ref[0] + b_ref[0].astype(F32)) + b_ref[1].astype(F32)) + b_ref[2].astype(F32)

    return pl.pallas_call(
        kern, name="rs_chip_add",
        grid_spec=pltpu.PrefetchScalarGridSpec(
            num_scalar_prefetch=1, grid=(nb,),
            in_specs=[pl.BlockSpec((1, rb, width), lambda i, w_ref: (w_ref[0], i, 0)),
                      pl.BlockSpec((N_CHIPS - 1, rb, width), lambda i, w_ref: (0, i, 0))],
            out_specs=pl.BlockSpec((rb, width), lambda i, w_ref: (w_ref[1] * nb + i, 0))),
        out_shape=jax.ShapeDtypeStruct((2 * rows, width), F32),
        compiler_params=_cparams("parallel"))(where, part_f32, got)


def _adamw(w, g, m, v, name):
    rows, width = w.shape
    rb = rows
    for cand in (512, 256, 128, 64, 32, 16, 8):
        if rows % cand == 0 and cand * width * 4 <= ADAM_BLOCK_BYTES:
            rb = cand
            break
    spec = pl.BlockSpec((rb, width), lambda i: (i, 0))

    def kern(w_ref, g_ref, m_ref, v_ref, d_ref, nm_ref, nv_ref):
        gv = g_ref[...]
        nm = ADAM_B1 * m_ref[...] + (1.0 - ADAM_B1) * gv
        nv = ADAM_B2 * v_ref[...] + (1.0 - ADAM_B2) * (gv * gv)
        m_hat = nm / (1.0 - ADAM_B1 ** ADAM_STEP)
        v_hat = nv / (1.0 - ADAM_B2 ** ADAM_STEP)
        d_ref[...] = -ADAM_LR * (m_hat / (jnp.sqrt(v_hat) + ADAM_EPS) + ADAM_WD * w_ref[...])
        nm_ref[...] = nm
        nv_ref[...] = nv

    shp = jax.ShapeDtypeStruct(w.shape, F32)
    return pl.pallas_call(
        kern, name=name, grid=(rows // rb,), in_specs=[spec] * 4, out_specs=(spec, spec, spec),
        out_shape=(shp, shp, shp), compiler_params=_cparams("parallel"))(w, g, m, v)


def _wada_grad(silu_t, dmod_cols):
    n = dmod_cols.shape[1]

    def kern(s_ref, d_ref, o_ref):
        acc = s_ref[:, 0:1] * d_ref[0:1, :]
        for b in range(1, N_DEV):
            acc = acc + s_ref[:, b:b + 1] * d_ref[b:b + 1, :]
        o_ref[...] = acc

    return pl.pallas_call(kern, name="wada_grad", out_shape=jax.ShapeDtypeStruct((D_MODEL, n), F32),
                          compiler_params=_cparams())(silu_t, dmod_cols)


def _rows(a, multiple):
    flat = a.reshape(-1)
    pad = (-flat.shape[0]) % (D_MODEL * multiple)
    if pad:
        flat = jnp.concatenate([flat, jnp.zeros((pad,), flat.dtype)])
    return flat.reshape(-1, D_MODEL)


def _part_rows(shape, multiple):
    return -(-int(np.prod(shape)) // (D_MODEL * multiple)) * multiple


def _pack_rows(parts, multiple):
    return jnp.concatenate([_rows(p, multiple) for p in parts], axis=0)


def _unpack_rows(buf, shapes, multiple):
    out, r = [], 0
    for shp in shapes:
        n = int(np.prod(shp))
        nr = _part_rows(shp, multiple)
        out.append(buf[r:r + nr].reshape(-1)[:n].reshape(shp))
        r += nr
    return out


def kernel(x, c, w_ada, b_ada, norm_pre, norm_post, w_in, pool_w, pool_scale, ssm_a_re, ssm_a_im, ssm_log_dt, ssm_b_re, ssm_b_im, ssm_c_re, ssm_c_im, ssm_d, glu_w, glu_b, w_branch_pool, w_branch_ssm, w_out, loss_target, m_w_ada, m_b_ada, m_norm_pre, m_norm_post, m_w_in, m_pool_w, m_pool_scale, m_ssm_a_re, m_ssm_a_im, m_ssm_log_dt, m_ssm_b_re, m_ssm_b_im, m_ssm_c_re, m_ssm_c_im, m_ssm_d, m_glu_w, m_glu_b, m_w_branch_pool, m_w_branch_ssm, m_w_out, v_w_ada, v_b_ada, v_norm_pre, v_norm_post, v_w_in, v_pool_w, v_pool_scale, v_ssm_a_re, v_ssm_a_im, v_ssm_log_dt, v_ssm_b_re, v_ssm_b_im, v_ssm_c_re, v_ssm_c_im, v_ssm_d, v_glu_w, v_glu_b, v_w_branch_pool, v_w_branch_ssm, v_w_out):
    n_ada = w_ada.shape[2]
    n_in = w_in.shape[2]
    n_row = glu_w.shape[1]
    n_pool = pool_w.shape[2]
    n_groups = pool_w.shape[1]

    big_shards = [w_ada[0], w_in[0], pool_w[0], glu_w[0], w_branch_pool[0], w_branch_ssm[0], w_out[0]]
    packed = _pad_rows(_pack_rows([s.astype(BF16) for s in big_shards], 2 * SUBLANES), 2 * COMM_CHUNKS * COMM_ROW_ALIGN)
    gathered = _ag_weights(packed)
    r = 0
    w_ada_bf = gathered[:, r:r + n_ada].reshape(N_CHIPS, D_MODEL, n_ada).transpose(1, 0, 2).reshape(D_MODEL, 3 * D_MODEL)
    r += n_ada
    w_in_bf = gathered[:, r:r + n_in].reshape(N_CHIPS, D_MODEL, n_in).transpose(1, 0, 2).reshape(D_MODEL, N_CHIPS * n_in)
    r += n_in
    pool_rows = n_groups * n_pool * POOL_GW // D_MODEL
    pool_w_bf = gathered[:, r:r + pool_rows].reshape(N_CHIPS, n_groups, n_pool, POOL_GW).transpose(1, 0, 2, 3)
    pool_w_bf = pool_w_bf.reshape(n_groups, POOL_GW, POOL_GW)
    r += pool_rows
    squares = []
    for _ in range(4):
        squares.append(gathered[:, r:r + n_row].reshape(D_MODEL, D_MODEL))
        r += n_row
    glu_w_bf, wbp_bf, wbs_bf, wout_bf = squares

    a_re, a_im, log_dt = ssm_a_re[0], ssm_a_im[0], ssm_log_dt[0].reshape(SSM_G, 1)
    b_re_t, b_im_t = ssm_b_re[0].transpose(2, 0, 1), ssm_b_im[0].transpose(2, 0, 1)
    res = _local_step(x[0], c, loss_target[0], w_ada_bf, b_ada, norm_pre, norm_post, w_in_bf, pool_w_bf, pool_scale,
                      a_re, a_im, log_dt, b_re_t, b_im_t, ssm_c_re[0], ssm_c_im[0], ssm_d[0], glu_w_bf, glu_b[0:1],
                      wbp_bf, wbs_bf, wout_bf)
    loss = lax.psum(res["loss"], ("x", "y", "c"))

    small_parts = [res["dmod"], res["silu_c"], res["dg1"], res["dg2"], res["d_pscale"], res["d_glu_b"], res["d_dskip"],
                   res["d_abar_re"], res["d_abar_im"], res["d_bb_re_t"], res["d_bb_im_t"], res["d_c_re"], res["d_c_im"]]
    small_shapes = [p.shape for p in small_parts]
    all_small, sum_small = _small_allgather_sum(
        _pad_rows(_pack_rows(small_parts, SUBLANES), COMM_CHUNKS * COMM_ROW_ALIGN))
    (g_b_ada, _, g_norm_pre, g_norm_post, g_pscale, g_glu_b, g_dskip, s_abar_re, s_abar_im, s_bb_re, s_bb_im,
     g_c_re, g_c_im) = _unpack_rows(sum_small, small_shapes, SUBLANES)
    g_a_re, g_a_im, g_log_dt, g_b_re_t, g_b_im_t = _ssm_params_bwd(
        a_re, a_im, log_dt, b_re_t, b_im_t, s_abar_re.reshape(SSM_G, SSM_P), s_abar_im.reshape(SSM_G, SSM_P),
        s_bb_re, s_bb_im)
    chip = 2 * lax.axis_index("x") + lax.axis_index("y")
    dmod_all = all_small[:, 0:3].reshape(N_DEV, 3 * D_MODEL)
    dmod_cols = lax.dynamic_slice_in_dim(dmod_all, chip * n_ada, n_ada, axis=1)
    silu_t = all_small[:, _part_rows(small_shapes[0], SUBLANES)].transpose(1, 0)
    g_w_ada = _wada_grad(silu_t, dmod_cols)

    def by_cols(a, n):
        return a.reshape(D_MODEL, N_CHIPS, n).transpose(1, 0, 2).reshape(N_CHIPS, -1, D_MODEL)

    def by_rows(a):
        return a.reshape(N_CHIPS, n_row, D_MODEL)

    pool_by_chip = res["d_pool_w"].reshape(n_groups, N_CHIPS, n_pool, POOL_GW).transpose(1, 0, 2, 3)
    g_packed = jnp.concatenate(
        [by_cols(res["d_win"], n_in), by_rows(res["d_glu_w"]), by_rows(res["d_wbp"]), by_rows(res["d_wbs"]),
         by_rows(res["d_wout"]), pool_by_chip.reshape(N_CHIPS, pool_rows, D_MODEL)], axis=1)
    g_packed = _pad_rows(g_packed, 2 * COMM_CHUNKS * COMM_ROW_ALIGN, axis=1)
    core = lax.axis_index("c").astype(jnp.int32)
    part_f32, part_bf = _pair_add(g_packed, _rs_pair(g_packed), core.reshape(1))
    shard = _rs_join(_chip_add(part_f32, _rs_chips(part_bf), jnp.stack([chip.astype(jnp.int32), core])))
    r = 0
    g_w_in = shard[r:r + n_in].reshape(D_MODEL, n_in)
    r += n_in
    g_squares = []
    for _ in range(4):
        g_squares.append(shard[r:r + n_row])
        r += n_row
    g_glu_w, g_wbp, g_wbs, g_wout = g_squares
    g_pool_w = shard[r:r + pool_rows].reshape(n_groups * n_pool, POOL_GW)

    big = [("w_ada", w_ada[0], g_w_ada, m_w_ada[0], v_w_ada[0]),
           ("w_in", w_in[0], g_w_in, m_w_in[0], v_w_in[0]),
           ("pool_w", pool_w[0].reshape(n_groups * n_pool, POOL_GW), g_pool_w,
            m_pool_w[0].reshape(n_groups * n_pool, POOL_GW), v_pool_w[0].reshape(n_groups * n_pool, POOL_GW)),
           ("glu_w", glu_w[0], g_glu_w, m_glu_w[0], v_glu_w[0]),
           ("w_branch_pool", w_branch_pool[0], g_wbp, m_w_branch_pool[0], v_w_branch_pool[0]),
           ("w_branch_ssm", w_branch_ssm[0], g_wbs, m_w_branch_ssm[0], v_w_branch_ssm[0]),
           ("w_out", w_out[0], g_wout, m_w_out[0], v_w_out[0])]
    out = {}
    for name, w_, g_, m_, v_ in big:
        d_, nm_, nv_ = _adamw(w_, g_, m_, v_, "adamw_" + name)
        out[name] = (g_, d_, nm_, nv_)

    g_b_re = g_b_re_t.transpose(1, 2, 0)
    g_b_im = g_b_im_t.transpose(1, 2, 0)
    small = [("b_ada", b_ada, g_b_ada, m_b_ada, v_b_ada),
             ("norm_pre", norm_pre, g_norm_pre, m_norm_pre, v_norm_pre),
             ("norm_post", norm_post, g_norm_post, m_norm_post, v_norm_post),
             ("pool_scale", pool_scale, g_pscale, m_pool_scale, v_pool_scale),
             ("ssm_a_re", ssm_a_re, g_a_re, m_ssm_a_re, v_ssm_a_re),
             ("ssm_a_im", ssm_a_im, g_a_im, m_ssm_a_im, v_ssm_a_im),
             ("ssm_log_dt", ssm_log_dt, g_log_dt, m_ssm_log_dt, v_ssm_log_dt),
             ("ssm_b_re", ssm_b_re, g_b_re, m_ssm_b_re, v_ssm_b_re),
             ("ssm_b_im", ssm_b_im, g_b_im, m_ssm_b_im, v_ssm_b_im),
             ("ssm_c_re", ssm_c_re, g_c_re, m_ssm_c_re, v_ssm_c_re),
             ("ssm_c_im", ssm_c_im, g_c_im, m_ssm_c_im, v_ssm_c_im),
             ("ssm_d", ssm_d, g_dskip, m_ssm_d, v_ssm_d),
             ("glu_b", glu_b, g_glu_b, m_glu_b, v_glu_b)]
    shapes = [w_.shape for _, w_, _, _, _ in small]
    pw_, pg_, pm_, pv_ = (_pack_rows([t[i] for t in small], SUBLANES) for i in (1, 2, 3, 4))
    pd_, pnm_, pnv_ = _adamw(pw_, pg_, pm_, pv_, "adamw_small")
    unpacked = [_unpack_rows(p, shapes, SUBLANES) for p in (pg_, pd_, pnm_, pnv_)]
    for (name, _, _, _, _), g_, d_, nm_, nv_ in zip(small, *unpacked):
        out[name] = (g_, d_, nm_, nv_)

    order = ["w_ada", "b_ada", "norm_pre", "norm_post", "w_in", "pool_w", "pool_scale", "ssm_a_re", "ssm_a_im",
             "ssm_log_dt", "ssm_b_re", "ssm_b_im", "ssm_c_re", "ssm_c_im", "ssm_d", "glu_w", "glu_b", "w_branch_pool",
             "w_branch_ssm", "w_out"]
    ref_shape = dict(w_ada=w_ada.shape, w_in=w_in.shape, pool_w=pool_w.shape, glu_w=glu_w.shape,
                     w_branch_pool=w_branch_pool.shape, w_branch_ssm=w_branch_ssm.shape, w_out=w_out.shape)
    for name, w_, _, _, _ in small:
        ref_shape[name] = w_.shape
    results = [loss, res["grad_x"][None]]
    for k in range(4):
        results += [out[name][k].reshape(ref_shape[name]) for name in order]
    return tuple(results)
```

```python
import functools
import math

import numpy as np
import jax
import jax.numpy as jnp
from jax import lax
from jax.experimental import pallas as pl
from jax.experimental.pallas import tpu as pltpu

F32 = jnp.float32
BF16 = jnp.bfloat16
MESH_ID = pl.DeviceIdType.MESH

D_MODEL = 1024
LANES = 128
SUBLANES = 8
SSM_G, SSM_P, SSM_H = 64, 64, 16
LANE_BLOCKS = D_MODEL // LANES
GROUPS_PER_BLOCK = LANES // SSM_H
STATE_W = GROUPS_PER_BLOCK * SSM_P
STATE_ALL = SSM_G * SSM_P
POOL_WINDOWS = (2, 4, 8, 16)
POOL_GW = D_MODEL // len(POOL_WINDOWS)
HALO = 16
RMS_EPS = 1e-6
N_CHIPS = 4
N_DEV = 8

SCAN_CHUNK = 512
ROW_CHUNK = 256
VMEM_LIMIT_BYTES = 56 * 1024 * 1024

ADAM_BLOCK_BYTES = 1 << 20
ADAM_LR, ADAM_B1, ADAM_B2, ADAM_EPS, ADAM_WD, ADAM_STEP = 0.001, 0.9, 0.999, 1e-08, 0.01, 10

_GELU_C0 = math.sqrt(2.0 / math.pi)
_GELU_C1 = 0.044715


def _cparams(*sem):
    if sem:
        return pltpu.CompilerParams(dimension_semantics=sem, vmem_limit_bytes=VMEM_LIMIT_BYTES)
    return pltpu.CompilerParams(vmem_limit_bytes=VMEM_LIMIT_BYTES)


def _sigmoid(v):
    return jax.nn.sigmoid(v)


def _silu(v):
    return v * _sigmoid(v)


def _dsilu(v):
    s = _sigmoid(v)
    return s * (1.0 + v * (1.0 - s))


def _gelu(v):
    return 0.5 * v * (1.0 + jnp.tanh(_GELU_C0 * (v + _GELU_C1 * v * v * v)))


def _dgelu(v):
    t = jnp.tanh(_GELU_C0 * (v + _GELU_C1 * v * v * v))
    return 0.5 * (1.0 + t) + 0.5 * v * (1.0 - t * t) * _GELU_C0 * (1.0 + 3.0 * _GELU_C1 * v * v)


def _dot(a, b):
    return lax.dot_general(a, b, (((1,), (0,)), ((), ())), preferred_element_type=F32)


def _dot_nt(a, b):
    return lax.dot_general(a, b, (((1,), (1,)), ((), ())), preferred_element_type=F32)


def _dot_tn(a, b):
    return lax.dot_general(a, b, (((0,), (0,)), ((), ())), preferred_element_type=F32)


def _acc8(v):
    return v.reshape(v.shape[0] // SUBLANES, SUBLANES, v.shape[1]).sum(axis=0)


def _mm(a_parts, b_parts, *, name, ta=False, tb=False, out_dtype=F32, bm=512, bn=512, bk=512):
    a_parts, b_parts = list(a_parts), list(b_parts)
    if ta:
        assert len(a_parts) == 1
        k_dim, m_dim = a_parts[0].shape
    else:
        m_dim = a_parts[0].shape[0]
        k_dim = sum(a.shape[1] for a in a_parts)
    if tb:
        assert len(b_parts) == 1
        n_dim = b_parts[0].shape[0]
    else:
        n_dim = sum(b.shape[1] for b in b_parts)
    bm, bn, bk = min(bm, m_dim), min(bn, n_dim), min(bk, k_dim)
    nm, nn, nk = m_dim // bm, n_dim // bn, k_dim // bk
    a_ranges, off = [], 0
    for a in a_parts:
        cnt = (a.shape[0] if ta else a.shape[1]) // bk
        a_ranges.append((off, cnt))
        off += cnt
    b_ranges, off = [], 0
    for b in b_parts:
        cnt = (b.shape[0] if tb else b.shape[1]) // bn
        b_ranges.append((off, cnt))
        off += cnt

    def a_spec(off, cnt):
        if ta:
            return pl.BlockSpec((bk, bm), lambda i, n, k: (k, i))
        return pl.BlockSpec((bm, bk), lambda i, n, k: (i, jnp.clip(k - off, 0, cnt - 1)))

    def b_spec(off, cnt):
        if tb:
            return pl.BlockSpec((bn, bk), lambda i, n, k: (n, k))
        return pl.BlockSpec((bk, bn), lambda i, n, k: (k, jnp.clip(n - off, 0, cnt - 1)))

    na, nb = len(a_parts), len(b_parts)
    dims = (((0 if ta else 1,), (1 if tb else 0,)), ((), ()))

    def kern_single(a_ref, b_ref, o_ref):
        o_ref[...] = lax.dot_general(a_ref[...].astype(BF16), b_ref[...].astype(BF16), dims,
                                     preferred_element_type=F32).astype(out_dtype)

    if na == 1 and nb == 1 and nk == 1:
        return pl.pallas_call(
            kern_single, name=name, grid=(nm, nn),
            in_specs=[pl.BlockSpec((bk, bm), lambda i, n: (0, i)) if ta else pl.BlockSpec((bm, bk), lambda i, n: (i, 0)),
                      pl.BlockSpec((bn, bk), lambda i, n: (n, 0)) if tb else pl.BlockSpec((bk, bn), lambda i, n: (0, n))],
            out_specs=pl.BlockSpec((bm, bn), lambda i, n: (i, n)),
            out_shape=jax.ShapeDtypeStruct((m_dim, n_dim), out_dtype),
            compiler_params=_cparams("parallel", "parallel"),
        )(a_parts[0], b_parts[0])

    def kern(*refs):
        a_refs, b_refs = refs[:na], refs[na:na + nb]
        o_ref, acc = refs[na + nb], refs[na + nb + 1]
        n, k = pl.program_id(1), pl.program_id(2)

        @pl.when(k == 0)
        def _():
            acc[...] = jnp.zeros_like(acc)

        for ja, (koff, kcnt) in enumerate(a_ranges):
            for jb, (noff, ncnt) in enumerate(b_ranges):
                def step(ja=ja, jb=jb):
                    a = a_refs[ja][...].astype(BF16)
                    b = b_refs[jb][...].astype(BF16)
                    acc[...] += lax.dot_general(a, b, dims, preferred_element_type=F32)

                if na == 1 and nb == 1:
                    step()
                else:
                    cond = (k >= koff) & (k < koff + kcnt) & (n >= noff) & (n < noff + ncnt)
                    pl.when(cond)(step)

        @pl.when(k == nk - 1)
        def _():
            o_ref[...] = acc[...].astype(out_dtype)

    return pl.pallas_call(
        kern,
        name=name,
        grid=(nm, nn, nk),
        in_specs=[a_spec(*r) for r in a_ranges] + [b_spec(*r) for r in b_ranges],
        out_specs=pl.BlockSpec((bm, bn), lambda i, n, k: (i, n)),
        out_shape=jax.ShapeDtypeStruct((m_dim, n_dim), out_dtype),
        scratch_shapes=[pltpu.VMEM((bm, bn), F32)],
        compiler_params=_cparams("parallel", "parallel", "arbitrary"),
    )(*a_parts, *b_parts)


def _ssm_param_fn(a_re, a_im, log_dt, b_re, b_im):
    dt = jnp.exp(log_dt)
    lam_re = jnp.minimum(a_re, -1e-4)
    lam_im = a_im
    mag = jnp.exp(lam_re * dt)
    abar_re = mag * jnp.cos(lam_im * dt)
    abar_im = mag * jnp.sin(lam_im * dt)
    den = lam_re * lam_re + lam_im * lam_im
    num_re = abar_re - 1.0
    f_re = (num_re * lam_re + abar_im * lam_im) / den
    f_im = (abar_im * lam_re - num_re * lam_im) / den
    bb_re = f_re * b_re - f_im * b_im
    bb_im = f_re * b_im + f_im * b_re
    return abar_re, abar_im, bb_re, bb_im


def _ssm_params(a_re, a_im, log_dt, b_re_t, b_im_t):
    def kern(are, aim, ldt, bre, bim, o_ar, o_ai, o_br, o_bi):
        ar, ai, br, bi = _ssm_param_fn(are[...], aim[...], ldt[...], bre[...], bim[...])
        o_ar[...] = ar
        o_ai[...] = ai
        o_br[...] = br
        o_bi[...] = bi

    gp = jax.ShapeDtypeStruct((SSM_G, SSM_P), F32)
    hgp = jax.ShapeDtypeStruct((SSM_H, SSM_G, SSM_P), F32)
    return pl.pallas_call(kern, name="ssm_params", out_shape=(gp, gp, hgp, hgp), compiler_params=_cparams())(
        a_re, a_im, log_dt, b_re_t, b_im_t)


def _ssm_params_bwd(a_re, a_im, log_dt, b_re_t, b_im_t, d_ar, d_ai, d_bbr, d_bbi):
    def kern(are, aim, ldt, bre, bim, dar, dai, dbr, dbi, o_are, o_aim, o_ldt, o_bre, o_bim):
        prim = (are[...], aim[...], ldt[...], bre[...], bim[...])
        _, vjp = jax.vjp(_ssm_param_fn, *prim)
        g = vjp((dar[...], dai[...], dbr[...], dbi[...]))
        o_are[...] = g[0]
        o_aim[...] = g[1]
        o_ldt[...] = g[2]
        o_bre[...] = g[3]
        o_bim[...] = g[4]

    gp = jax.ShapeDtypeStruct((SSM_G, SSM_P), F32)
    g1 = jax.ShapeDtypeStruct((SSM_G, 1), F32)
    hgp = jax.ShapeDtypeStruct((SSM_H, SSM_G, SSM_P), F32)
    return pl.pallas_call(kern, name="ssm_params_bwd", out_shape=(gp, gp, g1, hgp, hgp), compiler_params=_cparams())(
        a_re, a_im, log_dt, b_re_t, b_im_t, d_ar, d_ai, d_bbr, d_bbi)


def _pow_tables(abar_re, abar_im, tc):
    ls = tc // SUBLANES

    def kern(ar_ref, ai_ref, fr_ref, fi_ref, rr_ref, ri_ref):
        a_re = jnp.broadcast_to(ar_ref[...], (SUBLANES, STATE_W))
        a_im = jnp.broadcast_to(ai_ref[...], (SUBLANES, STATE_W))
        p_re, p_im = a_re, a_im
        for i in range(ls):
            fwd = pl.ds(SUBLANES * i, SUBLANES)
            rev = pl.ds(SUBLANES * (ls - 1 - i), SUBLANES)
            fr_ref[fwd, :] = p_re
            fi_ref[fwd, :] = p_im
            rr_ref[rev, :] = p_re
            ri_ref[rev, :] = p_im
            p_re, p_im = p_re * a_re - p_im * a_im, p_re * a_im + p_im * a_re

    vec = pl.BlockSpec((1, STATE_W), lambda b: (0, b))
    tab = pl.BlockSpec((tc, STATE_W), lambda b: (0, b))
    shp = jax.ShapeDtypeStruct((tc, STATE_ALL), F32)
    return pl.pallas_call(
        kern, name="pow_tables", grid=(LANE_BLOCKS,), in_specs=[vec, vec], out_specs=(tab, tab, tab, tab),
        out_shape=(shp, shp, shp, shp), compiler_params=_cparams("parallel"))(abar_re, abar_im)


def _mod_kernel(c_row, w_ada_bf, b_ada):
    def kern(c_ref, w_ref, b_ref, m_ref, s_ref):
        cv = c_ref[...]
        sc = _silu(cv)
        s_ref[...] = sc
        lhs = jnp.broadcast_to(sc, (SUBLANES, D_MODEL)).astype(BF16)
        m_ref[...] = _dot(lhs, w_ref[...]) + b_ref[...]

    return pl.pallas_call(
        kern, name="ada_mod",
        out_shape=(jax.ShapeDtypeStruct((SUBLANES, 3 * D_MODEL), F32), jax.ShapeDtypeStruct((1, D_MODEL), F32)),
        compiler_params=_cparams())(c_row, w_ada_bf, b_ada)


def _row_spec(tr, width=D_MODEL, col=0):
    return pl.BlockSpec((tr, width), lambda c: (c, col))


def _vec_spec(width=D_MODEL):
    return pl.BlockSpec((1, width), lambda c: (0, 0))


def _in_norm(x, g1, scale, shift):
    seq = x.shape[0]
    tr = min(ROW_CHUNK, seq)

    def kern(x_ref, g_ref, sc_ref, sh_ref, h_ref):
        xv = x_ref[...]
        r = lax.rsqrt(jnp.mean(xv * xv, axis=-1, keepdims=True) + RMS_EPS)
        h_ref[...] = (((xv * r) * g_ref[...]) * (1.0 + sc_ref[...]) + sh_ref[...]).astype(BF16)

    return pl.pallas_call(
        kern, name="in_norm", grid=(seq // tr,),
        in_specs=[_row_spec(tr), _vec_spec(), _vec_spec(), _vec_spec()], out_specs=_row_spec(tr),
        out_shape=jax.ShapeDtypeStruct((seq, D_MODEL), BF16), compiler_params=_cparams("parallel"))(x, g1, scale, shift)


def _pool_windows(ext, pos, g, w, tr):
    cols = pl.ds(g * POOL_GW, POOL_GW)
    cur = ext[pl.ds(HALO, tr), cols]
    acc = cur
    for k in range(1, w):
        acc = acc + ext[pl.ds(HALO - k, tr), cols]
    cnt = jnp.minimum(pos + 1, w).astype(F32)
    return acc / cnt - cur


def _pool_fwd(proj, pool_w_bf, pscale):
    seq = proj.shape[0]
    tr = min(ROW_CHUNK, seq)
    hb = tr // HALO

    def kern(up_ref, halo_ref, zp_ref, pw_ref, ps_ref, y_ref, ext):
        c = pl.program_id(0)
        ext[0:HALO, :] = jnp.where(c > 0, halo_ref[...], 0.0)
        ext[HALO:, :] = up_ref[...]
        pos = c * tr + lax.broadcasted_iota(jnp.int32, (tr, POOL_GW), 0)
        for g, w in enumerate(POOL_WINDOWS):
            cols = pl.ds(g * POOL_GW, POOL_GW)
            pooled = _pool_windows(ext, pos, g, w, tr)
            mixed = _dot(pooled.astype(BF16), pw_ref[g])
            y_ref[:, cols] = (mixed * ps_ref[:, cols] * _silu(zp_ref[:, cols])).astype(BF16)

    return pl.pallas_call(
        kern, name="pool_fwd", grid=(seq // tr,),
        in_specs=[_row_spec(tr, col=0),
                  pl.BlockSpec((HALO, D_MODEL), lambda c: (jnp.maximum(c * hb - 1, 0), 0)),
                  _row_spec(tr, col=1),
                  pl.BlockSpec((len(POOL_WINDOWS), POOL_GW, POOL_GW), lambda c: (0, 0, 0)),
                  _vec_spec()],
        out_specs=_row_spec(tr), out_shape=jax.ShapeDtypeStruct((seq, D_MODEL), BF16),
        scratch_shapes=[pltpu.VMEM((tr + HALO, D_MODEL), F32)],
        compiler_params=_cparams("parallel"))(proj, proj, proj, pool_w_bf, pscale)


def _pool_bwd(proj, dyp, pool_w_bf, pscale):
    seq = proj.shape[0]
    tr = min(ROW_CHUNK, seq)
    hb = tr // HALO
    nc = seq // tr
    n_halo = seq // HALO

    def kern(up_ref, halo_ref, zp_ref, zpn_ref, dyp_ref, dypn_ref, pw_ref, ps_ref,
             d01_ref, dpw_ref, dps_ref, ext, dpn, acc_pw, acc_ps):
        c = pl.program_id(0)

        @pl.when(c == 0)
        def _():
            acc_pw[...] = jnp.zeros_like(acc_pw)
            acc_ps[...] = jnp.zeros_like(acc_ps)

        ext[0:HALO, :] = jnp.where(c > 0, halo_ref[...], 0.0)
        ext[HALO:, :] = up_ref[...]
        pos = c * tr + lax.broadcasted_iota(jnp.int32, (tr, POOL_GW), 0)
        pos_n = (c + 1) * tr + lax.broadcasted_iota(jnp.int32, (HALO, POOL_GW), 0)
        has_next = c < nc - 1
        for g, w in enumerate(POOL_WINDOWS):
            cols = pl.ds(g * POOL_GW, POOL_GW)
            pooled_bf = _pool_windows(ext, pos, g, w, tr).astype(BF16)
            wg = pw_ref[g]
            mixed = _dot(pooled_bf, wg)
            zp = zp_ref[:, cols]
            sz = _silu(zp)
            dyp_g = dyp_ref[:, cols]
            ps = ps_ref[:, cols]
            dmixed = (dyp_g * ps * sz).astype(BF16)
            acc_ps[:, cols] += _acc8(dyp_g * mixed * sz)
            d01_ref[:, pl.ds(D_MODEL + g * POOL_GW, POOL_GW)] = (dyp_g * mixed * ps * _dsilu(zp)).astype(BF16)
            acc_pw[g] += _dot_tn(pooled_bf, dmixed)
            dpooled = _dot_nt(dmixed, wg)
            dmixed_n = (jnp.where(has_next, dypn_ref[:, cols], 0.0) * ps * _silu(zpn_ref[:, cols])).astype(BF16)
            dpooled_n = _dot_nt(dmixed_n, wg)
            dpn[0:tr, :] = dpooled / jnp.minimum(pos + 1, w).astype(F32)
            dpn[tr:, :] = dpooled_n / jnp.minimum(pos_n + 1, w).astype(F32)
            acc = dpn[0:tr, :]
            for k in range(1, w):
                acc = acc + dpn[pl.ds(k, tr), :]
            d01_ref[:, cols] = (acc - dpooled).astype(BF16)

        @pl.when(c == nc - 1)
        def _():
            dpw_ref[...] = acc_pw[...]
            dps_ref[...] = jnp.sum(acc_ps[...], axis=0, keepdims=True)

    nxt = lambda c: (jnp.minimum((c + 1) * hb, n_halo - 1), 0)
    nxt1 = lambda c: (jnp.minimum((c + 1) * hb, n_halo - 1), 1)
    return pl.pallas_call(
        kern, name="pool_bwd", grid=(nc,),
        in_specs=[_row_spec(tr, col=0),
                  pl.BlockSpec((HALO, D_MODEL), lambda c: (jnp.maximum(c * hb - 1, 0), 0)),
                  _row_spec(tr, col=1),
                  pl.BlockSpec((HALO, D_MODEL), nxt1),
                  _row_spec(tr),
                  pl.BlockSpec((HALO, D_MODEL), nxt),
                  pl.BlockSpec((len(POOL_WINDOWS), POOL_GW, POOL_GW), lambda c: (0, 0, 0)),
                  _vec_spec()],
        out_specs=(pl.BlockSpec((tr, 2 * D_MODEL), lambda c: (c, 0)),
                   pl.BlockSpec((len(POOL_WINDOWS), POOL_GW, POOL_GW), lambda c: (0, 0, 0)),
                   _vec_spec()),
        out_shape=(jax.ShapeDtypeStruct((seq, 2 * D_MODEL), BF16),
                   jax.ShapeDtypeStruct((len(POOL_WINDOWS), POOL_GW, POOL_GW), F32),
                   jax.ShapeDtypeStruct((1, D_MODEL), F32)),
        scratch_shapes=[pltpu.VMEM((tr + HALO, D_MODEL), F32), pltpu.VMEM((tr + HALO, POOL_GW), F32),
                        pltpu.VMEM((len(POOL_WINDOWS), POOL_GW, POOL_GW), F32), pltpu.VMEM((SUBLANES, D_MODEL), F32)],
        compiler_params=_cparams("arbitrary"))(proj, proj, proj, proj, dyp, dyp, pool_w_bf, pscale)


def _glu_fwd(ys, proj, glu_w_bf, glu_b):
    seq = ys.shape[0]
    tr = min(ROW_CHUNK, seq)

    def kern(ys_ref, zs_ref, w_ref, b_ref, o_ref):
        yg = _gelu(ys_ref[...])
        q = _dot(yg.astype(BF16), w_ref[...]) + b_ref[...]
        o_ref[...] = (yg * _sigmoid(q) * _silu(zs_ref[...])).astype(BF16)

    return pl.pallas_call(
        kern, name="glu_fwd", grid=(seq // tr,),
        in_specs=[_row_spec(tr), _row_spec(tr, col=3), pl.BlockSpec((D_MODEL, D_MODEL), lambda c: (0, 0)), _vec_spec()],
        out_specs=_row_spec(tr), out_shape=jax.ShapeDtypeStruct((seq, D_MODEL), BF16),
        compiler_params=_cparams("parallel"))(ys, proj, glu_w_bf, glu_b)


def _glu_bwd(ys, proj, dyssm, glu_w_bf, glu_b):
    seq = ys.shape[0]
    tr = min(ROW_CHUNK, seq)
    nc = seq // tr

    def kern(ys_ref, zs_ref, dy_ref, w_ref, b_ref, dys_ref, dzs_ref, dq_ref, yg_ref, db_ref, acc_b):
        c = pl.program_id(0)

        @pl.when(c == 0)
        def _():
            acc_b[...] = jnp.zeros_like(acc_b)

        ysv = ys_ref[...]
        yg = _gelu(ysv)
        yg_bf = yg.astype(BF16)
        q = _dot(yg_bf, w_ref[...]) + b_ref[...]
        sg = _sigmoid(q)
        zs = zs_ref[...]
        dyv = dy_ref[...]
        dyglu = dyv * _silu(zs)
        dzs_ref[...] = (dyv * (yg * sg) * _dsilu(zs)).astype(BF16)
        dq = dyglu * yg * sg * (1.0 - sg)
        dq_bf = dq.astype(BF16)
        acc_b[...] += _acc8(dq)
        dyg = dyglu * sg + _dot_nt(dq_bf, w_ref[...])
        dys_ref[...] = dyg * _dgelu(ysv)
        dq_ref[...] = dq_bf
        yg_ref[...] = yg_bf

        @pl.when(c == nc - 1)
        def _():
            db_ref[...] = jnp.sum(acc_b[...], axis=0, keepdims=True)

    bf = jax.ShapeDtypeStruct((seq, D_MODEL), BF16)
    return pl.pallas_call(
        kern, name="glu_bwd", grid=(nc,),
        in_specs=[_row_spec(tr), _row_spec(tr, col=3), _row_spec(tr),
                  pl.BlockSpec((D_MODEL, D_MODEL), lambda c: (0, 0)), _vec_spec()],
        out_specs=(_row_spec(tr), _row_spec(tr), _row_spec(tr), _row_spec(tr), _vec_spec()),
        out_shape=(jax.ShapeDtypeStruct((seq, D_MODEL), F32), bf, bf, bf, jax.ShapeDtypeStruct((1, D_MODEL), F32)),
        scratch_shapes=[pltpu.VMEM((SUBLANES, D_MODEL), F32)],
        compiler_params=_cparams("arbitrary"))(ys, proj, dyssm, glu_w_bf, glu_b)


def _out_fwd_bwd(ypool, yssm, proj, x, tgt, gate, g2, wbp_bf, wbs_bf, wout_bf):
    seq = x.shape[0]
    tr = min(ROW_CHUNK, seq)
    nc = seq // tr

    def kern(yp_ref, ysm_ref, gp_ref, gs_ref, x_ref, t_ref, gate_ref, g2_ref, wbp_ref, wbs_ref, wo_ref,
             dy_ref, dyp_ref, dys_ref, d45_ref, mb_ref, dob_ref, dbp_ref, dbs_ref, loss_ref, dgate_ref, dg2_ref,
             acc_l, acc_gate, acc_g2):
        c = pl.program_id(0)

        @pl.when(c == 0)
        def _():
            acc_l[...] = jnp.zeros_like(acc_l)
            acc_gate[...] = jnp.zeros_like(acc_gate)
            acc_g2[...] = jnp.zeros_like(acc_g2)

        bp = _dot(yp_ref[...], wbp_ref[...])
        bs = _dot(ysm_ref[...], wbs_ref[...])
        sp = _sigmoid(gp_ref[...])
        ss = _sigmoid(gs_ref[...])
        mb = (sp * bp + ss * bs).astype(BF16)
        out = _dot(mb, wo_ref[...])
        r2 = lax.rsqrt(jnp.mean(out * out, axis=-1, keepdims=True) + RMS_EPS)
        oh = out * r2
        gate_v, g2_v = gate_ref[...], g2_ref[...]
        ohg = oh * g2_v
        diff = (x_ref[...] + gate_v * ohg) - t_ref[...]
        acc_l[...] += _acc8(diff * diff)
        dyv = diff * (1.0 / D_MODEL)
        dy_ref[...] = dyv
        acc_gate[...] += _acc8(dyv * ohg)
        t = dyv * gate_v
        acc_g2[...] += _acc8(t * oh)
        doh = t * g2_v
        dout = r2 * (doh - oh * jnp.mean(doh * oh, axis=-1, keepdims=True))
        dob = dout.astype(BF16)
        dmerged = _dot_nt(dob, wo_ref[...])
        dbp = (dmerged * sp).astype(BF16)
        dbs = (dmerged * ss).astype(BF16)
        d45_ref[:, 0:D_MODEL] = (dmerged * bp * sp * (1.0 - sp)).astype(BF16)
        d45_ref[:, D_MODEL:] = (dmerged * bs * ss * (1.0 - ss)).astype(BF16)
        dyp_ref[...] = _dot_nt(dbp, wbp_ref[...])
        dys_ref[...] = _dot_nt(dbs, wbs_ref[...])
        mb_ref[...] = mb
        dob_ref[...] = dob
        dbp_ref[...] = dbp
        dbs_ref[...] = dbs

        @pl.when(c == nc - 1)
        def _():
            tot = jnp.sum(acc_l[...], axis=0, keepdims=True)
            loss_ref[...] = jnp.sum(tot, axis=1, keepdims=True) * (0.5 / D_MODEL)
            dgate_ref[...] = jnp.sum(acc_gate[...], axis=0, keepdims=True)
            dg2_ref[...] = jnp.sum(acc_g2[...], axis=0, keepdims=True)

    wspec = pl.BlockSpec((D_MODEL, D_MODEL), lambda c: (0, 0))
    f32 = jax.ShapeDtypeStruct((seq, D_MODEL), F32)
    bf = jax.ShapeDtypeStruct((seq, D_MODEL), BF16)
    vec = jax.ShapeDtypeStruct((1, D_MODEL), F32)
    acc = pltpu.VMEM((SUBLANES, D_MODEL), F32)
    return pl.pallas_call(
        kern, name="out_fwd_bwd", grid=(nc,),
        in_specs=[_row_spec(tr), _row_spec(tr), _row_spec(tr, col=4), _row_spec(tr, col=5), _row_spec(tr), _row_spec(tr),
                  _vec_spec(), _vec_spec(), wspec, wspec, wspec],
        out_specs=(_row_spec(tr), _row_spec(tr), _row_spec(tr), pl.BlockSpec((tr, 2 * D_MODEL), lambda c: (c, 0)),
                   _row_spec(tr), _row_spec(tr), _row_spec(tr), _row_spec(tr),
                   pl.BlockSpec((1, 1), lambda c: (0, 0)), _vec_spec(), _vec_spec()),
        out_shape=(f32, f32, f32, jax.ShapeDtypeStruct((seq, 2 * D_MODEL), BF16), bf, bf, bf, bf,
                   jax.ShapeDtypeStruct((1, 1), F32), vec, vec),
        scratch_shapes=[acc, acc, acc],
        compiler_params=_cparams("arbitrary"))(ypool, yssm, proj, proj, x, tgt, gate, g2, wbp_bf, wbs_bf, wout_bf)


def _in_bwd(dh, x, dy, g1, scale):
    seq = x.shape[0]
    tr = min(ROW_CHUNK, seq)
    nc = seq // tr

    def kern(dh_ref, x_ref, dy_ref, g_ref, sc_ref, dx_ref, dsh_ref, dsc_ref, dg_ref, a_sh, a_sc, a_g):
        c = pl.program_id(0)

        @pl.when(c == 0)
        def _():
            a_sh[...] = jnp.zeros_like(a_sh)
            a_sc[...] = jnp.zeros_like(a_sc)
            a_g[...] = jnp.zeros_like(a_g)

        xv = x_ref[...]
        r = lax.rsqrt(jnp.mean(xv * xv, axis=-1, keepdims=True) + RMS_EPS)
        xh = xv * r
        g = g_ref[...]
        dhv = dh_ref[...]
        a_sh[...] += _acc8(dhv)
        a_sc[...] += _acc8(dhv * (xh * g))
        dn = dhv * (1.0 + sc_ref[...])
        a_g[...] += _acc8(dn * xh)
        dxh = dn * g
        dx_ref[...] = dy_ref[...] + r * (dxh - xh * jnp.mean(dxh * xh, axis=-1, keepdims=True))

        @pl.when(c == nc - 1)
        def _():
            dsh_ref[...] = jnp.sum(a_sh[...], axis=0, keepdims=True)
            dsc_ref[...] = jnp.sum(a_sc[...], axis=0, keepdims=True)
            dg_ref[...] = jnp.sum(a_g[...], axis=0, keepdims=True)

    vec = jax.ShapeDtypeStruct((1, D_MODEL), F32)
    acc = pltpu.VMEM((SUBLANES, D_MODEL), F32)
    return pl.pallas_call(
        kern, name="in_bwd", grid=(nc,),
        in_specs=[_row_spec(tr), _row_spec(tr), _row_spec(tr), _vec_spec(), _vec_spec()],
        out_specs=(_row_spec(tr), _vec_spec(), _vec_spec(), _vec_spec()),
        out_shape=(jax.ShapeDtypeStruct((seq, D_MODEL), F32), vec, vec, vec),
        scratch_shapes=[acc, acc, acc],
        compiler_params=_cparams("arbitrary"))(dh, x, dy, g1, scale)


def _local_scan(a_re, a_im, br, bi, xr, xi, row0, ls, reverse):
    x_re = jnp.zeros((SUBLANES, STATE_W), F32)
    x_im = jnp.zeros((SUBLANES, STATE_W), F32)
    for i in (range(ls - 1, -1, -1) if reverse else range(ls)):
        src = pl.ds(SUBLANES * i, SUBLANES)
        dst = pl.ds(row0 + SUBLANES * i, SUBLANES)
        n_re = a_re * x_re - a_im * x_im + br[src, :]
        n_im = a_re * x_im + a_im * x_re + bi[src, :]
        x_re, x_im = n_re, n_im
        xr[dst, :] = x_re
        xi[dst, :] = x_im
    return x_re, x_im


def _load_permuted(src_ref, dst_ref, ls):
    for i in range(ls):
        dst_ref[pl.ds(SUBLANES * i, SUBLANES), :] = src_ref[pl.ds(i, SUBLANES, stride=ls), :]


def _store_unpermuted(src_ref, dst_ref, ls):
    for i in range(ls):
        dst_ref[pl.ds(i, SUBLANES, stride=ls), :] = src_ref[pl.ds(SUBLANES * i, SUBLANES), :]


def _unpermute_rhs(v, sel):
    hi = v.astype(BF16)
    r1 = v - hi.astype(F32)
    mid = r1.astype(BF16)
    lo = (r1 - mid.astype(F32)).astype(BF16)
    return _dot(hi, sel) + _dot(mid, sel) + _dot(lo, sel)


def _ssm_scan_fwd(proj, bb_re, bb_im, cm_re, cm_im, abar_re, abar_im, pw_re, pw_im, d_skip, tc):
    seq = proj.shape[0]
    nc = seq // tc
    ls = tc // SUBLANES
    us_col0 = 2 * D_MODEL // LANES

    def kern(us_ref, bbr_ref, bbi_ref, cmr_ref, cmi_ref, ar_ref, ai_ref, pwr_ref, pwi_ref, d_ref,
             ys_ref, ecr_ref, eci_ref, bur, bui, car_r, car_i, end_r, end_i, upb):
        c = pl.program_id(1)

        @pl.when(c == 0)
        def _():
            car_r[...] = jnp.zeros_like(car_r)
            car_i[...] = jnp.zeros_like(car_i)

        _load_permuted(us_ref, upb, ls)
        u = upb[...]
        up = u.astype(BF16)
        bur[...] = _dot(up, bbr_ref[0])
        bui[...] = _dot(up, bbi_ref[0])
        a_re = jnp.broadcast_to(ar_ref[...], (SUBLANES, STATE_W))
        a_im = jnp.broadcast_to(ai_ref[...], (SUBLANES, STATE_W))
        x_re, x_im = _local_scan(a_re, a_im, bur, bui, bur, bui, 0, ls, False)
        end_r[...] = x_re
        end_i[...] = x_im
        big_re = pwr_ref[tc - 1:tc, :]
        big_im = pwi_ref[tc - 1:tc, :]
        e_re = car_r[0:1, :]
        e_im = car_i[0:1, :]
        for s in range(SUBLANES):
            n_re = end_r[s:s + 1, :] + big_re * e_re - big_im * e_im
            n_im = end_i[s:s + 1, :] + big_re * e_im + big_im * e_re
            e_re, e_im = n_re, n_im
            if s < SUBLANES - 1:
                car_r[s + 1:s + 2, :] = e_re
                car_i[s + 1:s + 2, :] = e_im
        ec_re = car_r[...]
        ec_im = car_i[...]
        ecr_ref[...] = ec_re
        eci_ref[...] = ec_im
        p_re = pwr_ref[...].reshape(ls, SUBLANES, STATE_W)
        p_im = pwi_ref[...].reshape(ls, SUBLANES, STATE_W)
        xf_re = bur[...].reshape(ls, SUBLANES, STATE_W) + p_re * ec_re[None] - p_im * ec_im[None]
        xf_im = bui[...].reshape(ls, SUBLANES, STATE_W) + p_re * ec_im[None] + p_im * ec_re[None]
        xb_re = xf_re.reshape(tc, STATE_W).astype(BF16)
        xb_im = xf_im.reshape(tc, STATE_W).astype(BF16)
        upb[...] = _dot(xb_re, cmr_ref[0]) - _dot(xb_im, cmi_ref[0]) + d_ref[...] * u
        _store_unpermuted(upb, ys_ref, ls)
        car_r[0:1, :] = e_re
        car_i[0:1, :] = e_im

    vec = pl.BlockSpec((1, STATE_W), lambda b, c: (0, b))
    tab = pl.BlockSpec((tc, STATE_W), lambda b, c: (0, b))
    car = pl.BlockSpec((SUBLANES, STATE_W), lambda b, c: (c, b))
    carry_shape = jax.ShapeDtypeStruct((nc * SUBLANES, STATE_ALL), F32)
    small = pltpu.VMEM((SUBLANES, STATE_W), F32)
    big = pltpu.VMEM((tc, STATE_W), F32)
    return pl.pallas_call(
        kern, name="ssm_scan_fwd", grid=(LANE_BLOCKS, nc),
        in_specs=[pl.BlockSpec((tc, LANES), lambda b, c: (c, us_col0 + b)),
                  pl.BlockSpec((1, LANES, STATE_W), lambda b, c: (b, 0, 0)),
                  pl.BlockSpec((1, LANES, STATE_W), lambda b, c: (b, 0, 0)),
                  pl.BlockSpec((1, STATE_W, LANES), lambda b, c: (b, 0, 0)),
                  pl.BlockSpec((1, STATE_W, LANES), lambda b, c: (b, 0, 0)),
                  vec, vec, tab, tab, pl.BlockSpec((1, LANES), lambda b, c: (0, b))],
        out_specs=(pl.BlockSpec((tc, LANES), lambda b, c: (c, b)), car, car),
        out_shape=(jax.ShapeDtypeStruct((seq, D_MODEL), F32), carry_shape, carry_shape),
        scratch_shapes=[big, big, small, small, small, small, pltpu.VMEM((tc, LANES), F32)],
        compiler_params=_cparams("parallel", "arbitrary"),
    )(proj, bb_re, bb_im, cm_re, cm_im, abar_re, abar_im, pw_re, pw_im, d_skip)


def _ssm_scan_bwd(proj, dys, ec_re, ec_im, bb_re, bb_im, cm_re, cm_im, abar_re, abar_im,
                  pw_re, pw_im, pv_re, pv_im, d_skip, tc):
    seq = proj.shape[0]
    nc = seq // tc
    ls = tc // SUBLANES
    us_col0 = 2 * D_MODEL // LANES

    def kern(us_ref, dys_ref, ecr_ref, eci_ref, bbr_ref, bbi_ref, cmr_ref, cmi_ref, ar_ref, ai_ref,
             pwr_ref, pwi_ref, pvr_ref, pvi_ref, d_ref,
             dus_ref, dbbr_ref, dbbi_ref, dcmr_ref, dcmi_ref, dar_ref, dai_ref, dd_ref,
             bur, bui, xr, xi, gr, gi, fc_r, fc_i, a_bbr, a_bbi, a_cmr, a_cmi, a_ar, a_ai, a_dd, upb, dpb):
        c = pl.program_id(1)

        @pl.when(c == 0)
        def _():
            for ref in (fc_r, fc_i, a_bbr, a_bbi, a_cmr, a_cmi, a_ar, a_ai, a_dd):
                ref[...] = jnp.zeros_like(ref)

        _load_permuted(us_ref, upb, ls)
        _load_permuted(dys_ref, dpb, ls)
        u = upb[...]
        dysv = dpb[...]
        a_dd[...] += _acc8(dysv * u)
        up = u.astype(BF16)
        bur[...] = _dot(up, bbr_ref[0])
        bui[...] = _dot(up, bbi_ref[0])
        a_re = jnp.broadcast_to(ar_ref[...], (SUBLANES, STATE_W))
        a_im = jnp.broadcast_to(ai_ref[...], (SUBLANES, STATE_W))
        _local_scan(a_re, a_im, bur, bui, xr, xi, SUBLANES, ls, False)
        ec_r = ecr_ref[...]
        ec_i = eci_ref[...]
        xr[0:SUBLANES, :] = ec_r
        xi[0:SUBLANES, :] = ec_i
        p_re = pwr_ref[...].reshape(ls, SUBLANES, STATE_W)
        p_im = pwi_ref[...].reshape(ls, SUBLANES, STATE_W)
        xl_re = xr[SUBLANES:, :].reshape(ls, SUBLANES, STATE_W)
        xl_im = xi[SUBLANES:, :].reshape(ls, SUBLANES, STATE_W)
        xf_re = (xl_re + p_re * ec_r[None] - p_im * ec_i[None]).reshape(tc, STATE_W)
        xf_im = (xl_im + p_re * ec_i[None] + p_im * ec_r[None]).reshape(tc, STATE_W)
        xr[SUBLANES:, :] = xf_re
        xi[SUBLANES:, :] = xf_im
        dysp = dysv.astype(BF16)
        a_cmr[...] += _dot_tn(dysp, xf_re.astype(BF16))
        a_cmi[...] -= _dot_tn(dysp, xf_im.astype(BF16))
        gr[...] = _dot_nt(dysp, cmr_ref[0])
        gi[...] = -_dot_nt(dysp, cmi_ref[0])
        _local_scan(a_re, -a_im, gr, gi, gr, gi, 0, ls, True)
        big_re = pwr_ref[tc - 1:tc, :]
        big_im = -pwi_ref[tc - 1:tc, :]
        f_re = fc_r[SUBLANES - 1:SUBLANES, :]
        f_im = fc_i[SUBLANES - 1:SUBLANES, :]
        for s in range(SUBLANES - 1, -1, -1):
            n_re = gr[s:s + 1, :] + big_re * f_re - big_im * f_im
            n_im = gi[s:s + 1, :] + big_re * f_im + big_im * f_re
            f_re, f_im = n_re, n_im
            if s > 0:
                fc_r[s - 1:s, :] = f_re
                fc_i[s - 1:s, :] = f_im
        fcv_r = fc_r[...]
        fcv_i = fc_i[...]
        q_re = pvr_ref[...].reshape(ls, SUBLANES, STATE_W)
        q_im = -pvi_ref[...].reshape(ls, SUBLANES, STATE_W)
        lam_re = (gr[...].reshape(ls, SUBLANES, STATE_W) + q_re * fcv_r[None] - q_im * fcv_i[None]).reshape(tc, STATE_W)
        lam_im = (gi[...].reshape(ls, SUBLANES, STATE_W) + q_re * fcv_i[None] + q_im * fcv_r[None]).reshape(tc, STATE_W)
        fc_r[SUBLANES - 1:SUBLANES, :] = f_re
        fc_i[SUBLANES - 1:SUBLANES, :] = f_im
        xp_re = xr[0:tc, :]
        xp_im = xi[0:tc, :]
        a_ar[...] += _acc8(lam_re * xp_re + lam_im * xp_im)
        a_ai[...] += _acc8(lam_im * xp_re - lam_re * xp_im)
        lb_re = lam_re.astype(BF16)
        lb_im = lam_im.astype(BF16)
        a_bbr[...] += _dot_tn(up, lb_re)
        a_bbi[...] += _dot_tn(up, lb_im)
        dpb[...] = _dot_nt(lb_re, bbr_ref[0]) + _dot_nt(lb_im, bbi_ref[0]) + dysv * d_ref[...]
        _store_unpermuted(dpb, dus_ref, ls)

        @pl.when(c == nc - 1)
        def _():
            row_g = lax.broadcasted_iota(jnp.int32, (LANES, STATE_W), 0) // SSM_H
            col_g = lax.broadcasted_iota(jnp.int32, (LANES, STATE_W), 1) // SSM_P
            fold = (lax.broadcasted_iota(jnp.int32, (STATE_W, SSM_P), 0) % SSM_P
                    == lax.broadcasted_iota(jnp.int32, (STATE_W, SSM_P), 1)).astype(BF16)
            for acc, out in ((a_bbr, dbbr_ref), (a_bbi, dbbi_ref), (a_cmr, dcmr_ref), (a_cmi, dcmi_ref)):
                out[...] = _unpermute_rhs(jnp.where(row_g == col_g, acc[...], 0.0), fold)
            dar_ref[...] = jnp.sum(a_ar[...], axis=0, keepdims=True)
            dai_ref[...] = jnp.sum(a_ai[...], axis=0, keepdims=True)
            dd_ref[...] = jnp.sum(a_dd[...], axis=0, keepdims=True)

    rc = lambda c: nc - 1 - c
    vec = pl.BlockSpec((1, STATE_W), lambda b, c: (0, b))
    tab = pl.BlockSpec((tc, STATE_W), lambda b, c: (0, b))
    car = pl.BlockSpec((SUBLANES, STATE_W), lambda b, c: (rc(c), b))
    bblk = pl.BlockSpec((1, LANES, STATE_W), lambda b, c: (b, 0, 0))
    cblk = pl.BlockSpec((1, STATE_W, LANES), lambda b, c: (b, 0, 0))
    ghp = pl.BlockSpec((LANES, SSM_P), lambda b, c: (b, 0))
    ghp_shape = jax.ShapeDtypeStruct((SSM_G * SSM_H, SSM_P), F32)
    dvec = pl.BlockSpec((1, LANES), lambda b, c: (0, b))
    small = pltpu.VMEM((SUBLANES, STATE_W), F32)
    big = pltpu.VMEM((tc, STATE_W), F32)
    bigp = pltpu.VMEM((tc + SUBLANES, STATE_W), F32)
    blk = pltpu.VMEM((LANES, STATE_W), F32)
    return pl.pallas_call(
        kern, name="ssm_scan_bwd", grid=(LANE_BLOCKS, nc),
        in_specs=[pl.BlockSpec((tc, LANES), lambda b, c: (rc(c), us_col0 + b)),
                  pl.BlockSpec((tc, LANES), lambda b, c: (rc(c), b)),
                  car, car, bblk, bblk, cblk, cblk, vec, vec, tab, tab, tab, tab, dvec],
        out_specs=(pl.BlockSpec((tc, LANES), lambda b, c: (rc(c), b)), ghp, ghp, ghp, ghp, vec, vec, dvec),
        out_shape=(jax.ShapeDtypeStruct((seq, D_MODEL), F32), ghp_shape, ghp_shape, ghp_shape, ghp_shape,
                   jax.ShapeDtypeStruct((1, STATE_ALL), F32), jax.ShapeDtypeStruct((1, STATE_ALL), F32),
                   jax.ShapeDtypeStruct((1, D_MODEL), F32)),
        scratch_shapes=[big, big, bigp, bigp, big, big, small, small, blk, blk, blk, blk,
                        small, small, pltpu.VMEM((SUBLANES, LANES), F32),
                        pltpu.VMEM((tc, LANES), F32), pltpu.VMEM((tc, LANES), F32)],
        compiler_params=_cparams("parallel", "arbitrary"),
    )(proj, dys, ec_re, ec_im, bb_re, bb_im, cm_re, cm_im, abar_re, abar_im, pw_re, pw_im, pv_re, pv_im, d_skip)


def _eye5():
    return jnp.eye(GROUPS_PER_BLOCK, dtype=F32)[None, :, None, :, None]


def _embed_b(bb_t):
    t = bb_t.transpose(1, 0, 2).reshape(LANE_BLOCKS, GROUPS_PER_BLOCK, SSM_H, 1, SSM_P)
    return (t * _eye5()).reshape(LANE_BLOCKS, LANES, STATE_W)


def _embed_c(c_ghp):
    t = c_ghp.transpose(0, 2, 1).reshape(LANE_BLOCKS, GROUPS_PER_BLOCK, SSM_P, 1, SSM_H)
    return (t * _eye5()).reshape(LANE_BLOCKS, STATE_W, LANES)


def _local_step(x, c_row, tgt, w_ada_bf, b_ada, g1, g2, w_in_bf, pool_w_bf, pscale, a_re, a_im, log_dt,
                b_re_t, b_im_t, c_re, c_im, d_skip, glu_w_bf, glu_b, wbp_bf, wbs_bf, wout_bf):
    seq = x.shape[0]
    tc = min(SCAN_CHUNK, seq)
    mod8, silu_c = _mod_kernel(c_row, w_ada_bf, b_ada)
    mod = mod8[0:1]
    shift, scale, gate = mod[:, 0:D_MODEL], mod[:, D_MODEL:2 * D_MODEL], mod[:, 2 * D_MODEL:]

    abar_re, abar_im, bb_re_t, bb_im_t = _ssm_params(a_re, a_im, log_dt, b_re_t, b_im_t)
    abar_re_f, abar_im_f = abar_re.reshape(1, STATE_ALL), abar_im.reshape(1, STATE_ALL)
    pw_re, pw_im, pv_re, pv_im = _pow_tables(abar_re_f, abar_im_f, tc)
    bbe_re, bbe_im = _embed_b(bb_re_t).astype(BF16), _embed_b(bb_im_t).astype(BF16)
    cme_re, cme_im = _embed_c(c_re).astype(BF16), _embed_c(c_im).astype(BF16)
    d_row = d_skip.reshape(1, D_MODEL)

    h = _in_norm(x, g1, scale, shift)
    proj = _mm([h], [w_in_bf], name="proj", bm=1024, bn=1024, bk=1024)
    ypool = _pool_fwd(proj, pool_w_bf, pscale)
    ys, ec_re, ec_im = _ssm_scan_fwd(proj, bbe_re, bbe_im, cme_re, cme_im, abar_re_f, abar_im_f,
                                      pw_re, pw_im, d_row, tc)
    yssm = _glu_fwd(ys, proj, glu_w_bf, glu_b)
    (dy, dypool, dyssm, d45, mb, dob, dbp, dbs, loss, dgate, dg2) = _out_fwd_bwd(
        ypool, yssm, proj, x, tgt, gate, g2, wbp_bf, wbs_bf, wout_bf)

    d_wout = _mm([mb], [dob], ta=True, name="dw_out", bm=1024, bk=1024)
    d_wbp = _mm([ypool], [dbp], ta=True, name="dw_bp", bm=1024, bk=1024)
    d_wbs = _mm([yssm], [dbs], ta=True, name="dw_bs", bm=1024, bk=1024)
    dys, dzs, dq, yg, d_glu_b = _glu_bwd(ys, proj, dyssm, glu_w_bf, glu_b)
    d_glu_w = _mm([yg], [dq], ta=True, name="dw_glu", bm=1024, bk=1024)
    (dus, dbbe_re, dbbe_im, dcme_re, dcme_im, d_abar_re, d_abar_im, d_dskip) = _ssm_scan_bwd(
        proj, dys, ec_re, ec_im, bbe_re, bbe_im, cme_re, cme_im, abar_re_f, abar_im_f,
        pw_re, pw_im, pv_re, pv_im, d_row, tc)
    d01, d_pool_w, d_pscale = _pool_bwd(proj, dypool, pool_w_bf, pscale)
    dparts = [d01, dus, dzs, d45]
    dh = _mm(dparts, [w_in_bf], tb=True, name="dh", bm=1024, bn=1024, bk=512)
    d_win = _mm([h], dparts, ta=True, name="dw_in", bm=1024, bk=1024)
    grad_x, dshift, dscale, dg1 = _in_bwd(dh, x, dy, g1, scale)
    dmod = jnp.concatenate([dshift, dscale, dgate], axis=1)
    return dict(
        loss=loss[0, 0], grad_x=grad_x, dmod=dmod, silu_c=silu_c, dg1=dg1, dg2=dg2, d_pscale=d_pscale,
        d_glu_b=d_glu_b, d_dskip=d_dskip, d_abar_re=d_abar_re, d_abar_im=d_abar_im,
        d_bb_re_t=dbbe_re.reshape(SSM_G, SSM_H, SSM_P).transpose(1, 0, 2),
        d_bb_im_t=dbbe_im.reshape(SSM_G, SSM_H, SSM_P).transpose(1, 0, 2),
        d_c_re=dcme_re.reshape(SSM_G, SSM_H, SSM_P), d_c_im=dcme_im.reshape(SSM_G, SSM_H, SSM_P),
        d_win=d_win, d_glu_w=d_glu_w, d_wbp=d_wbp, d_wbs=d_wbs, d_wout=d_wout, d_pool_w=d_pool_w)


def _position():
    x, y, c = lax.axis_index("x"), lax.axis_index("y"), lax.axis_index("c")
    chips = [(1 - x, y), (x, 1 - y), (1 - x, 1 - y)]
    return x, y, c, chips


_ANY = pl.BlockSpec(memory_space=pl.ANY)
COMM_CHUNKS = 4
COMM_ROW_ALIGN = 16


def _row_chunks(rows, k):
    assert rows % (k * COMM_ROW_ALIGN) == 0, (rows, k)
    step = rows // k
    return [(q * step, step) for q in range(k)]


def _pad_rows(buf, multiple, axis=0):
    pad = (-buf.shape[axis]) % multiple
    if not pad:
        return buf
    shape = list(buf.shape)
    shape[axis] = pad
    return jnp.concatenate([buf, jnp.zeros(shape, buf.dtype)], axis=axis)


def _ag_weights(packed):
    rows, width = packed.shape
    half = rows // 2
    chunks = _row_chunks(half, COMM_CHUNKS)
    nq = len(chunks)

    def body(p_ref, out_ref, send_sems, recv_sems):
        x, y, c, chips = _position()
        sibling = (x, y, 1 - c)

        def copy(k, chip, h, q, to, src=None):
            start, size = chunks[q]
            rows_q = pl.ds(h * half + start, size)
            dst = out_ref.at[2 * chip[0] + chip[1], rows_q, :]
            return pltpu.make_async_remote_copy(
                src_ref=dst if src is None else src.at[rows_q, :], dst_ref=dst, send_sem=send_sems.at[k * nq + q],
                recv_sem=recv_sems.at[k * nq + q], device_id=to, device_id_type=MESH_ID)

        mine = [copy(6 + h, (x, y), h, q, sibling, src=p_ref) for h in range(2) for q in range(nq)]
        first = [copy(j, (x, y), c, q, (*chip, c), src=p_ref) for q in range(nq) for j, chip in enumerate(chips)]
        for cp in first + mine:
            cp.start()
        passed = []
        for q in range(nq):
            for j, chip in enumerate(chips):
                copy(j, chip, c, q, (x, y, c)).wait_recv()
                fwd = copy(3 + j, chip, c, q, sibling)
                fwd.start()
                passed.append(fwd)
        for q in range(nq):
            for j, chip in enumerate(chips):
                copy(3 + j, chip, 1 - c, q, (x, y, c)).wait_recv()
        for cp in mine:
            cp.wait_recv()
        for cp in first + passed + mine:
            cp.wait_send()

    return pl.pallas_call(
        body, name="ag_weights", in_specs=[_ANY], out_specs=_ANY,
        out_shape=jax.ShapeDtypeStruct((N_CHIPS, rows, width), packed.dtype),
        scratch_shapes=[pltpu.SemaphoreType.DMA((8 * nq,)), pltpu.SemaphoreType.DMA((8 * nq,))],
    )(packed)


def _small_allgather_sum(buf, head_rows):
    rows, width = buf.shape
    chunks = _row_chunks(rows, COMM_CHUNKS)
    nq = len(chunks)

    def body(b_ref, head_ref, sum_ref, all_ref, send_sems, recv_sems, local_sem):
        x, y, c, chips = _position()
        me, sibling = (x, y, c), (x, y, 1 - c)

        def slot(px, py, pc):
            return all_ref.at[4 * px + 2 * py + pc]

        def copy(k, block, q, to, src=None):
            rows_q = pl.ds(chunks[q][0], chunks[q][1])
            dst = slot(*block).at[rows_q, :]
            return pltpu.make_async_remote_copy(
                src_ref=dst if src is None else src.at[rows_q, :], dst_ref=dst, send_sem=send_sems.at[k * nq + q],
                recv_sem=recv_sems.at[k * nq + q], device_id=to, device_id_type=MESH_ID)

        mine = pltpu.make_async_copy(b_ref, slot(*me), local_sem)
        mine.start()
        first = []
        for q in range(nq):
            first += [copy(1 + j, me, q, (*chip, c), src=b_ref) for j, chip in enumerate(chips)]
            first.append(copy(0, me, q, sibling, src=b_ref))
        for cp in first:
            cp.start()
        passed = []
        for q in range(nq):
            for j, chip in enumerate(chips):
                copy(1 + j, (*chip, c), q, me).wait_recv()
                fwd = copy(4 + j, (*chip, c), q, sibling)
                fwd.start()
                passed.append(fwd)
        for q in range(nq):
            copy(0, sibling, q, me).wait_recv()
            for j, chip in enumerate(chips):
                copy(4 + j, (*chip, 1 - c), q, me).wait_recv()
        for cp in first + passed:
            cp.wait_send()
        mine.wait()
        total = all_ref[0]
        for d in range(1, N_DEV):
            total = total + all_ref[d]
        sum_ref[...] = total
        head_ref[...] = all_ref[:, 0:head_rows, :]

    vm = pl.BlockSpec(memory_space=pltpu.VMEM)
    return pl.pallas_call(
        body, name="small_allgather_sum", in_specs=[vm], out_specs=(vm, vm),
        out_shape=(jax.ShapeDtypeStruct((N_DEV, head_rows, width), F32), jax.ShapeDtypeStruct((rows, width), F32)),
        scratch_shapes=[pltpu.VMEM((N_DEV, rows, width), F32), pltpu.SemaphoreType.DMA((7 * nq,)),
                        pltpu.SemaphoreType.DMA((7 * nq,)), pltpu.SemaphoreType.DMA],
        compiler_params=_cparams(),
    )(buf)


def _rs_pair(g):
    n, rows, width = g.shape
    half = rows // 2
    chunks = _row_chunks(half, COMM_CHUNKS)
    nq = len(chunks)

    def body(g_ref, got_ref, send_sems, recv_sems):
        x, y, c, _ = _position()
        swaps = []
        for k in range(n):
            for q, (start, size) in enumerate(chunks):
                swaps.append(pltpu.make_async_remote_copy(
                    src_ref=g_ref.at[k, pl.ds((1 - c) * half + start, size), :], dst_ref=got_ref.at[k, pl.ds(start, size), :],
                    send_sem=send_sems.at[k * nq + q], recv_sem=recv_sems.at[k * nq + q],
                    device_id=(x, y, 1 - c), device_id_type=MESH_ID))
        for cp in swaps:
            cp.start()
        for cp in swaps:
            cp.wait()

    return pl.pallas_call(
        body, name="rs_pair", in_specs=[_ANY], out_specs=_ANY, out_shape=jax.ShapeDtypeStruct((n, half, width), g.dtype),
        scratch_shapes=[pltpu.SemaphoreType.DMA((n * nq,)), pltpu.SemaphoreType.DMA((n * nq,))],
    )(g)


def _rs_chips(part_bf):
    n, rows, width = part_bf.shape
    chunks = _row_chunks(rows, COMM_CHUNKS)
    nq = len(chunks)

    def body(pb_ref, got_ref, send_sems, recv_sems):
        x, y, c, chips = _position()
        sends = []
        for q, (start, size) in enumerate(chunks):
            for j, chip in enumerate(chips):
                sends.append(pltpu.make_async_remote_copy(
                    src_ref=pb_ref.at[2 * chip[0] + chip[1], pl.ds(start, size), :], dst_ref=got_ref.at[j, pl.ds(start, size), :],
                    send_sem=send_sems.at[j * nq + q], recv_sem=recv_sems.at[j * nq + q],
                    device_id=(*chip, c), device_id_type=MESH_ID))
        for cp in sends:
            cp.start()
        for cp in sends:
            cp.wait()

    return pl.pallas_call(
        body, name="rs_chips", in_specs=[_ANY], out_specs=_ANY,
        out_shape=jax.ShapeDtypeStruct((N_CHIPS - 1, rows, width), BF16),
        scratch_shapes=[pltpu.SemaphoreType.DMA((3 * nq,)), pltpu.SemaphoreType.DMA((3 * nq,))],
    )(part_bf)


def _rs_join(shard):
    rows, width = shard.shape
    half = rows // 2
    chunks = _row_chunks(half, COMM_CHUNKS)
    nq = len(chunks)

    def body(in_ref, out_ref, send_sems, recv_sems):
        x, y, c, _ = _position()
        def swap(q, h):
            rows_q = pl.ds(h * half + chunks[q][0], chunks[q][1])
            return pltpu.make_async_remote_copy(
                src_ref=in_ref.at[rows_q, :], dst_ref=out_ref.at[rows_q, :], send_sem=send_sems.at[q],
                recv_sem=recv_sems.at[q], device_id=(x, y, 1 - c), device_id_type=MESH_ID)

        for q in range(nq):
            swap(q, c).start()
        for q in range(nq):
            swap(q, 1 - c).wait_recv()
        for q in range(nq):
            swap(q, c).wait_send()

    return pl.pallas_call(
        body, name="rs_join", in_specs=[_ANY], out_specs=_ANY, input_output_aliases={0: 0},
        out_shape=jax.ShapeDtypeStruct(shard.shape, shard.dtype),
        scratch_shapes=[pltpu.SemaphoreType.DMA((nq,)), pltpu.SemaphoreType.DMA((nq,))],
    )(shard)


def _pair_add(g, got, core):
    n, half, width = got.shape
    nb = 2
    rb = half // nb

    def kern(c_ref, a_ref, b_ref, f_ref, h_ref):
        s = a_ref[...] + b_ref[...]
        f_ref[...] = s
        h_ref[...] = s.astype(BF16)

    spec = pl.BlockSpec((1, rb, width), lambda k, i, c_ref: (k, i, 0))
    return pl.pallas_call(
        kern, name="rs_pair_add",
        grid_spec=pltpu.PrefetchScalarGridSpec(
            num_scalar_prefetch=1, grid=(n, nb),
            in_specs=[pl.BlockSpec((1, rb, width), lambda k, i, c_ref: (k, c_ref[0] * nb + i, 0)), spec],
            out_specs=(spec, spec)),
        out_shape=(jax.ShapeDtypeStruct(got.shape, F32), jax.ShapeDtypeStruct(got.shape, BF16)),
        compiler_params=_cparams("parallel", "parallel"))(core, g, got)


def _chip_add(part_f32, got, where):
    _, rows, width = part_f32.shape
    nb = 2
    rb = rows // nb

    def kern(w_ref, a_ref, b_ref, o_ref):
        o_ref[...] = ((a_ref[0] + b_ref[0].astype(F32)) + b_ref[1].astype(F32)) + b_ref[2].astype(F32)

    return pl.pallas_call(
        kern, name="rs_chip_add",
        grid_spec=pltpu.PrefetchScalarGridSpec(
            num_scalar_prefetch=1, grid=(nb,),
            in_specs=[pl.BlockSpec((1, rb, width), lambda i, w_ref: (w_ref[0], i, 0)),
                      pl.BlockSpec((N_CHIPS - 1, rb, width), lambda i, w_ref: (0, i, 0))],
            out_specs=pl.BlockSpec((rb, width), lambda i, w_ref: (w_ref[1] * nb + i, 0))),
        out_shape=jax.ShapeDtypeStruct((2 * rows, width), F32),
        compiler_params=_cparams("parallel"))(where, part_f32, got)


def _adamw(w, g, m, v, name):
    rows, width = w.shape
    rb = rows
    for cand in (512, 256, 128, 64, 32, 16, 8):
        if rows % cand == 0 and cand * width * 4 <= ADAM_BLOCK_BYTES:
            rb = cand
            break
    spec = pl.BlockSpec((rb, width), lambda i: (i, 0))

    def kern(w_ref, g_ref, m_ref, v_ref, d_ref, nm_ref, nv_ref):
        gv = g_ref[...]
        nm = ADAM_B1 * m_ref[...] + (1.0 - ADAM_B1) * gv
        nv = ADAM_B2 * v_ref[...] + (1.0 - ADAM_B2) * (gv * gv)
        m_hat = nm / (1.0 - ADAM_B1 ** ADAM_STEP)
        v_hat = nv / (1.0 - ADAM_B2 ** ADAM_STEP)
        d_ref[...] = -ADAM_LR * (m_hat / (jnp.sqrt(v_hat) + ADAM_EPS) + ADAM_WD * w_ref[...])
        nm_ref[...] = nm
        nv_ref[...] = nv

    shp = jax.ShapeDtypeStruct(w.shape, F32)
    return pl.pallas_call(
        kern, name=name, grid=(rows // rb,), in_specs=[spec] * 4, out_specs=(spec, spec, spec),
        out_shape=(shp, shp, shp), compiler_params=_cparams("parallel"))(w, g, m, v)


def _wada_grad(silu_t, dmod_cols):
    n = dmod_cols.shape[1]

    def kern(s_ref, d_ref, o_ref):
        acc = s_ref[:, 0:1] * d_ref[0:1, :]
        for b in range(1, N_DEV):
            acc = acc + s_ref[:, b:b + 1] * d_ref[b:b + 1, :]
        o_ref[...] = acc

    return pl.pallas_call(kern, name="wada_grad", out_shape=jax.ShapeDtypeStruct((D_MODEL, n), F32),
                          compiler_params=_cparams())(silu_t, dmod_cols)


def _rows(a, multiple):
    flat = a.reshape(-1)
    pad = (-flat.shape[0]) % (D_MODEL * multiple)
    if pad:
        flat = jnp.concatenate([flat, jnp.zeros((pad,), flat.dtype)])
    return flat.reshape(-1, D_MODEL)


def _part_rows(shape, multiple):
    return -(-int(np.prod(shape)) // (D_MODEL * multiple)) * multiple


def _pack_rows(parts, multiple):
    return jnp.concatenate([_rows(p, multiple) for p in parts], axis=0)


def _unpack_rows(buf, shapes, multiple):
    out, r = [], 0
    for shp in shapes:
        n = int(np.prod(shp))
        nr = _part_rows(shp, multiple)
        out.append(buf[r:r + nr].reshape(-1)[:n].reshape(shp))
        r += nr
    return out


def kernel(x, c, w_ada, b_ada, norm_pre, norm_post, w_in, pool_w, pool_scale, ssm_a_re, ssm_a_im, ssm_log_dt, ssm_b_re, ssm_b_im, ssm_c_re, ssm_c_im, ssm_d, glu_w, glu_b, w_branch_pool, w_branch_ssm, w_out, loss_target, m_w_ada, m_b_ada, m_norm_pre, m_norm_post, m_w_in, m_pool_w, m_pool_scale, m_ssm_a_re, m_ssm_a_im, m_ssm_log_dt, m_ssm_b_re, m_ssm_b_im, m_ssm_c_re, m_ssm_c_im, m_ssm_d, m_glu_w, m_glu_b, m_w_branch_pool, m_w_branch_ssm, m_w_out, v_w_ada, v_b_ada, v_norm_pre, v_norm_post, v_w_in, v_pool_w, v_pool_scale, v_ssm_a_re, v_ssm_a_im, v_ssm_log_dt, v_ssm_b_re, v_ssm_b_im, v_ssm_c_re, v_ssm_c_im, v_ssm_d, v_glu_w, v_glu_b, v_w_branch_pool, v_w_branch_ssm, v_w_out):
    n_ada = w_ada.shape[2]
    n_in = w_in.shape[2]
    n_row = glu_w.shape[1]
    n_pool = pool_w.shape[2]
    n_groups = pool_w.shape[1]

    big_shards = [w_ada[0], w_in[0], pool_w[0], glu_w[0], w_branch_pool[0], w_branch_ssm[0], w_out[0]]
    packed = _pad_rows(_pack_rows([s.astype(BF16) for s in big_shards], 2 * SUBLANES), 2 * COMM_CHUNKS * COMM_ROW_ALIGN)
    gathered = _ag_weights(packed)
    r = 0
    w_ada_bf = gathered[:, r:r + n_ada].reshape(N_CHIPS, D_MODEL, n_ada).transpose(1, 0, 2).reshape(D_MODEL, 3 * D_MODEL)
    r += n_ada
    w_in_bf = gathered[:, r:r + n_in].reshape(N_CHIPS, D_MODEL, n_in).transpose(1, 0, 2).reshape(D_MODEL, N_CHIPS * n_in)
    r += n_in
    pool_rows = n_groups * n_pool * POOL_GW // D_MODEL
    pool_w_bf = gathered[:, r:r + pool_rows].reshape(N_CHIPS, n_groups, n_pool, POOL_GW).transpose(1, 0, 2, 3)
    pool_w_bf = pool_w_bf.reshape(n_groups, POOL_GW, POOL_GW)
    r += pool_rows
    squares = []
    for _ in range(4):
        squares.append(gathered[:, r:r + n_row].reshape(D_MODEL, D_MODEL))
        r += n_row
    glu_w_bf, wbp_bf, wbs_bf, wout_bf = squares

    a_re, a_im, log_dt = ssm_a_re[0], ssm_a_im[0], ssm_log_dt[0].reshape(SSM_G, 1)
    b_re_t, b_im_t = ssm_b_re[0].transpose(2, 0, 1), ssm_b_im[0].transpose(2, 0, 1)
    res = _local_step(x[0], c, loss_target[0], w_ada_bf, b_ada, norm_pre, norm_post, w_in_bf, pool_w_bf, pool_scale,
                      a_re, a_im, log_dt, b_re_t, b_im_t, ssm_c_re[0], ssm_c_im[0], ssm_d[0], glu_w_bf, glu_b[0:1],
                      wbp_bf, wbs_bf, wout_bf)
    loss = lax.psum(res["loss"], ("x", "y", "c"))

    small_parts = [res["dmod"], res["silu_c"], res["dg1"], res["dg2"], res["d_pscale"], res["d_glu_b"], res["d_dskip"],
                   res["d_abar_re"], res["d_abar_im"], res["d_bb_re_t"], res["d_bb_im_t"], res["d_c_re"], res["d_c_im"]]
    small_shapes = [p.shape for p in small_parts]
    head_rows = _part_rows(small_shapes[0], SUBLANES) + _part_rows(small_shapes[1], SUBLANES)
    all_small, sum_small = _small_allgather_sum(
        _pad_rows(_pack_rows(small_parts, SUBLANES), COMM_CHUNKS * COMM_ROW_ALIGN), head_rows)
    (g_b_ada, _, g_norm_pre, g_norm_post, g_pscale, g_glu_b, g_dskip, s_abar_re, s_abar_im, s_bb_re, s_bb_im,
     g_c_re, g_c_im) = _unpack_rows(sum_small, small_shapes, SUBLANES)
    g_a_re, g_a_im, g_log_dt, g_b_re_t, g_b_im_t = _ssm_params_bwd(
        a_re, a_im, log_dt, b_re_t, b_im_t, s_abar_re.reshape(SSM_G, SSM_P), s_abar_im.reshape(SSM_G, SSM_P),
        s_bb_re, s_bb_im)
    chip = 2 * lax.axis_index("x") + lax.axis_index("y")
    dmod_all = all_small[:, 0:3].reshape(N_DEV, 3 * D_MODEL)
    dmod_cols = lax.dynamic_slice_in_dim(dmod_all, chip * n_ada, n_ada, axis=1)
    silu_t = all_small[:, _part_rows(small_shapes[0], SUBLANES)].transpose(1, 0)
    g_w_ada = _wada_grad(silu_t, dmod_cols)

    def by_cols(a, n):
        return a.reshape(D_MODEL, N_CHIPS, n).transpose(1, 0, 2).reshape(N_CHIPS, -1, D_MODEL)

    def by_rows(a):
        return a.reshape(N_CHIPS, n_row, D_MODEL)

    pool_by_chip = res["d_pool_w"].reshape(n_groups, N_CHIPS, n_pool, POOL_GW).transpose(1, 0, 2, 3)
    g_packed = jnp.concatenate(
        [by_cols(res["d_win"], n_in), by_rows(res["d_glu_w"]), by_rows(res["d_wbp"]), by_rows(res["d_wbs"]),
         by_rows(res["d_wout"]), pool_by_chip.reshape(N_CHIPS, pool_rows, D_MODEL)], axis=1)
    g_packed = _pad_rows(g_packed, 2 * COMM_CHUNKS * COMM_ROW_ALIGN, axis=1)
    core = lax.axis_index("c").astype(jnp.int32)
    part_f32, part_bf = _pair_add(g_packed, _rs_pair(g_packed), core.reshape(1))
    shard = _rs_join(_chip_add(part_f32, _rs_chips(part_bf), jnp.stack([chip.astype(jnp.int32), core])))
    r = 0
    g_w_in = shard[r:r + n_in].reshape(D_MODEL, n_in)
    r += n_in
    g_squares = []
    for _ in range(4):
        g_squares.append(shard[r:r + n_row])
        r += n_row
    g_glu_w, g_wbp, g_wbs, g_wout = g_squares
    g_pool_w = shard[r:r + pool_rows].reshape(n_groups * n_pool, POOL_GW)

    big = [("w_ada", w_ada[0], g_w_ada, m_w_ada[0], v_w_ada[0]),
           ("w_in", w_in[0], g_w_in, m_w_in[0], v_w_in[0]),
           ("pool_w", pool_w[0].reshape(n_groups * n_pool, POOL_GW), g_pool_w,
            m_pool_w[0].reshape(n_groups * n_pool, POOL_GW), v_pool_w[0].reshape(n_groups * n_pool, POOL_GW)),
           ("glu_w", glu_w[0], g_glu_w, m_glu_w[0], v_glu_w[0]),
           ("w_branch_pool", w_branch_pool[0], g_wbp, m_w_branch_pool[0], v_w_branch_pool[0]),
           ("w_branch_ssm", w_branch_ssm[0], g_wbs, m_w_branch_ssm[0], v_w_branch_ssm[0]),
           ("w_out", w_out[0], g_wout, m_w_out[0], v_w_out[0])]
    out = {}
    for name, w_, g_, m_, v_ in big:
        d_, nm_, nv_ = _adamw(w_, g_, m_, v_, "adamw_" + name)
        out[name] = (g_, d_, nm_, nv_)

    g_b_re = g_b_re_t.transpose(1, 2, 0)
    g_b_im = g_b_im_t.transpose(1, 2, 0)
    small = [("b_ada", b_ada, g_b_ada, m_b_ada, v_b_ada),
             ("norm_pre", norm_pre, g_norm_pre, m_norm_pre, v_norm_pre),
             ("norm_post", norm_post, g_norm_post, m_norm_post, v_norm_post),
             ("pool_scale", pool_scale, g_pscale, m_pool_scale, v_pool_scale),
             ("ssm_a_re", ssm_a_re, g_a_re, m_ssm_a_re, v_ssm_a_re),
             ("ssm_a_im", ssm_a_im, g_a_im, m_ssm_a_im, v_ssm_a_im),
             ("ssm_log_dt", ssm_log_dt, g_log_dt, m_ssm_log_dt, v_ssm_log_dt),
             ("ssm_b_re", ssm_b_re, g_b_re, m_ssm_b_re, v_ssm_b_re),
             ("ssm_b_im", ssm_b_im, g_b_im, m_ssm_b_im, v_ssm_b_im),
             ("ssm_c_re", ssm_c_re, g_c_re, m_ssm_c_re, v_ssm_c_re),
             ("ssm_c_im", ssm_c_im, g_c_im, m_ssm_c_im, v_ssm_c_im),
             ("ssm_d", ssm_d, g_dskip, m_ssm_d, v_ssm_d),
             ("glu_b", glu_b, g_glu_b, m_glu_b, v_glu_b)]
    shapes = [w_.shape for _, w_, _, _, _ in small]
    pw_, pg_, pm_, pv_ = (_pack_rows([t[i] for t in small], SUBLANES) for i in (1, 2, 3, 4))
    pd_, pnm_, pnv_ = _adamw(pw_, pg_, pm_, pv_, "adamw_small")
    unpacked = [_unpack_rows(p, shapes, SUBLANES) for p in (pg_, pd_, pnm_, pnv_)]
    for (name, _, _, _, _), g_, d_, nm_, nv_ in zip(small, *unpacked):
        out[name] = (g_, d_, nm_, nv_)

    order = ["w_ada", "b_ada", "norm_pre", "norm_post", "w_in", "pool_w", "pool_scale", "ssm_a_re", "ssm_a_im",
             "ssm_log_dt", "ssm_b_re", "ssm_b_im", "ssm_c_re", "ssm_c_im", "ssm_d", "glu_w", "glu_b", "w_branch_pool",
             "w_branch_ssm", "w_out"]
    ref_shape = dict(w_ada=w_ada.shape, w_in=w_in.shape, pool_w=pool_w.shape, glu_w=glu_w.shape,
                     w_branch_pool=w_branch_pool.shape, w_branch_ssm=w_branch_ssm.shape, w_out=w_out.shape)
    for name, w_, _, _, _ in small:
        ref_shape[name] = w_.shape
    results = [loss, res["grad_x"][None]]
    for k in range(4):
        results += [out[name][k].reshape(ref_shape[name]) for name in order]
    return tuple(results)
```

```python
import functools
import math

import numpy as np
import jax
import jax.numpy as jnp
from jax import lax
from jax.experimental import pallas as pl
from jax.experimental.pallas import tpu as pltpu

F32 = jnp.float32
BF16 = jnp.bfloat16
MESH_ID = pl.DeviceIdType.MESH

D_MODEL = 1024
LANES = 128
SUBLANES = 8
SSM_G, SSM_P, SSM_H = 64, 64, 16
LANE_BLOCKS = D_MODEL // LANES
GROUPS_PER_BLOCK = LANES // SSM_H
STATE_W = GROUPS_PER_BLOCK * SSM_P
STATE_ALL = SSM_G * SSM_P
POOL_WINDOWS = (2, 4, 8, 16)
POOL_GW = D_MODEL // len(POOL_WINDOWS)
HALO = 16
RMS_EPS = 1e-6
N_CHIPS = 4
N_DEV = 8

SCAN_CHUNK = 512
SCAN_BLOCKS = 2
ROW_CHUNK = 256
VMEM_LIMIT_BYTES = 56 * 1024 * 1024

ADAM_BLOCK_BYTES = 1 << 20
ADAM_LR, ADAM_B1, ADAM_B2, ADAM_EPS, ADAM_WD, ADAM_STEP = 0.001, 0.9, 0.999, 1e-08, 0.01, 10

_GELU_C0 = math.sqrt(2.0 / math.pi)
_GELU_C1 = 0.044715


def _cparams(*sem):
    if sem:
        return pltpu.CompilerParams(dimension_semantics=sem, vmem_limit_bytes=VMEM_LIMIT_BYTES)
    return pltpu.CompilerParams(vmem_limit_bytes=VMEM_LIMIT_BYTES)


def _sigmoid(v):
    return jax.nn.sigmoid(v)


def _silu(v):
    return v * _sigmoid(v)


def _dsilu(v):
    s = _sigmoid(v)
    return s * (1.0 + v * (1.0 - s))


def _gelu(v):
    return 0.5 * v * (1.0 + jnp.tanh(_GELU_C0 * (v + _GELU_C1 * v * v * v)))


def _dgelu(v):
    t = jnp.tanh(_GELU_C0 * (v + _GELU_C1 * v * v * v))
    return 0.5 * (1.0 + t) + 0.5 * v * (1.0 - t * t) * _GELU_C0 * (1.0 + 3.0 * _GELU_C1 * v * v)


def _dot(a, b):
    return lax.dot_general(a, b, (((1,), (0,)), ((), ())), preferred_element_type=F32)


def _dot_nt(a, b):
    return lax.dot_general(a, b, (((1,), (1,)), ((), ())), preferred_element_type=F32)


def _dot_tn(a, b):
    return lax.dot_general(a, b, (((0,), (0,)), ((), ())), preferred_element_type=F32)


def _acc8(v):
    return v.reshape(v.shape[0] // SUBLANES, SUBLANES, v.shape[1]).sum(axis=0)


def _mm(a_parts, b_parts, *, name, ta=False, tb=False, out_dtype=F32, bm=512, bn=512, bk=512):
    a_parts, b_parts = list(a_parts), list(b_parts)
    if ta:
        assert len(a_parts) == 1
        k_dim, m_dim = a_parts[0].shape
    else:
        m_dim = a_parts[0].shape[0]
        k_dim = sum(a.shape[1] for a in a_parts)
    if tb:
        assert len(b_parts) == 1
        n_dim = b_parts[0].shape[0]
    else:
        n_dim = sum(b.shape[1] for b in b_parts)
    bm, bn, bk = min(bm, m_dim), min(bn, n_dim), min(bk, k_dim)
    nm, nn, nk = m_dim // bm, n_dim // bn, k_dim // bk
    a_ranges, off = [], 0
    for a in a_parts:
        cnt = (a.shape[0] if ta else a.shape[1]) // bk
        a_ranges.append((off, cnt))
        off += cnt
    b_ranges, off = [], 0
    for b in b_parts:
        cnt = (b.shape[0] if tb else b.shape[1]) // bn
        b_ranges.append((off, cnt))
        off += cnt

    def a_spec(off, cnt):
        if ta:
            return pl.BlockSpec((bk, bm), lambda i, n, k: (k, i))
        return pl.BlockSpec((bm, bk), lambda i, n, k: (i, jnp.clip(k - off, 0, cnt - 1)))

    def b_spec(off, cnt):
        if tb:
            return pl.BlockSpec((bn, bk), lambda i, n, k: (n, k))
        return pl.BlockSpec((bk, bn), lambda i, n, k: (k, jnp.clip(n - off, 0, cnt - 1)))

    na, nb = len(a_parts), len(b_parts)
    dims = (((0 if ta else 1,), (1 if tb else 0,)), ((), ()))

    def kern_single(a_ref, b_ref, o_ref):
        o_ref[...] = lax.dot_general(a_ref[...].astype(BF16), b_ref[...].astype(BF16), dims,
                                     preferred_element_type=F32).astype(out_dtype)

    if na == 1 and nb == 1 and nk == 1:
        return pl.pallas_call(
            kern_single, name=name, grid=(nm, nn),
            in_specs=[pl.BlockSpec((bk, bm), lambda i, n: (0, i)) if ta else pl.BlockSpec((bm, bk), lambda i, n: (i, 0)),
                      pl.BlockSpec((bn, bk), lambda i, n: (n, 0)) if tb else pl.BlockSpec((bk, bn), lambda i, n: (0, n))],
            out_specs=pl.BlockSpec((bm, bn), lambda i, n: (i, n)),
            out_shape=jax.ShapeDtypeStruct((m_dim, n_dim), out_dtype),
            compiler_params=_cparams("parallel", "parallel"),
        )(a_parts[0], b_parts[0])

    def kern(*refs):
        a_refs, b_refs = refs[:na], refs[na:na + nb]
        o_ref, acc = refs[na + nb], refs[na + nb + 1]
        n, k = pl.program_id(1), pl.program_id(2)

        @pl.when(k == 0)
        def _():
            acc[...] = jnp.zeros_like(acc)

        for ja, (koff, kcnt) in enumerate(a_ranges):
            for jb, (noff, ncnt) in enumerate(b_ranges):
                def step(ja=ja, jb=jb):
                    a = a_refs[ja][...].astype(BF16)
                    b = b_refs[jb][...].astype(BF16)
                    acc[...] += lax.dot_general(a, b, dims, preferred_element_type=F32)

                if na == 1 and nb == 1:
                    step()
                else:
                    cond = (k >= koff) & (k < koff + kcnt) & (n >= noff) & (n < noff + ncnt)
                    pl.when(cond)(step)

        @pl.when(k == nk - 1)
        def _():
            o_ref[...] = acc[...].astype(out_dtype)

    return pl.pallas_call(
        kern,
        name=name,
        grid=(nm, nn, nk),
        in_specs=[a_spec(*r) for r in a_ranges] + [b_spec(*r) for r in b_ranges],
        out_specs=pl.BlockSpec((bm, bn), lambda i, n, k: (i, n)),
        out_shape=jax.ShapeDtypeStruct((m_dim, n_dim), out_dtype),
        scratch_shapes=[pltpu.VMEM((bm, bn), F32)],
        compiler_params=_cparams("parallel", "parallel", "arbitrary"),
    )(*a_parts, *b_parts)


def _ssm_param_fn(a_re, a_im, log_dt, b_re, b_im):
    dt = jnp.exp(log_dt)
    lam_re = jnp.minimum(a_re, -1e-4)
    lam_im = a_im
    mag = jnp.exp(lam_re * dt)
    abar_re = mag * jnp.cos(lam_im * dt)
    abar_im = mag * jnp.sin(lam_im * dt)
    den = lam_re * lam_re + lam_im * lam_im
    num_re = abar_re - 1.0
    f_re = (num_re * lam_re + abar_im * lam_im) / den
    f_im = (abar_im * lam_re - num_re * lam_im) / den
    bb_re = f_re * b_re - f_im * b_im
    bb_im = f_re * b_im + f_im * b_re
    return abar_re, abar_im, bb_re, bb_im


def _ssm_params(a_re, a_im, log_dt, b_re_t, b_im_t):
    def kern(are, aim, ldt, bre, bim, o_ar, o_ai, o_br, o_bi):
        ar, ai, br, bi = _ssm_param_fn(are[...], aim[...], ldt[...], bre[...], bim[...])
        o_ar[...] = ar
        o_ai[...] = ai
        o_br[...] = br
        o_bi[...] = bi

    gp = jax.ShapeDtypeStruct((SSM_G, SSM_P), F32)
    hgp = jax.ShapeDtypeStruct((SSM_H, SSM_G, SSM_P), F32)
    return pl.pallas_call(kern, name="ssm_params", out_shape=(gp, gp, hgp, hgp), compiler_params=_cparams())(
        a_re, a_im, log_dt, b_re_t, b_im_t)


def _ssm_params_bwd(a_re, a_im, log_dt, b_re_t, b_im_t, d_ar, d_ai, d_bbr, d_bbi):
    def kern(are, aim, ldt, bre, bim, dar, dai, dbr, dbi, o_are, o_aim, o_ldt, o_bre, o_bim):
        prim = (are[...], aim[...], ldt[...], bre[...], bim[...])
        _, vjp = jax.vjp(_ssm_param_fn, *prim)
        g = vjp((dar[...], dai[...], dbr[...], dbi[...]))
        o_are[...] = g[0]
        o_aim[...] = g[1]
        o_ldt[...] = g[2]
        o_bre[...] = g[3]
        o_bim[...] = g[4]

    gp = jax.ShapeDtypeStruct((SSM_G, SSM_P), F32)
    g1 = jax.ShapeDtypeStruct((SSM_G, 1), F32)
    hgp = jax.ShapeDtypeStruct((SSM_H, SSM_G, SSM_P), F32)
    return pl.pallas_call(kern, name="ssm_params_bwd", out_shape=(gp, gp, g1, hgp, hgp), compiler_params=_cparams())(
        a_re, a_im, log_dt, b_re_t, b_im_t, d_ar, d_ai, d_bbr, d_bbi)


def _pow_tables(abar_re, abar_im, tc):
    ls = tc // SUBLANES

    def kern(ar_ref, ai_ref, fr_ref, fi_ref, rr_ref, ri_ref):
        a_re = jnp.broadcast_to(ar_ref[...], (SUBLANES, STATE_W))
        a_im = jnp.broadcast_to(ai_ref[...], (SUBLANES, STATE_W))
        p_re, p_im = a_re, a_im
        for i in range(ls):
            fwd = pl.ds(SUBLANES * i, SUBLANES)
            rev = pl.ds(SUBLANES * (ls - 1 - i), SUBLANES)
            fr_ref[fwd, :] = p_re
            fi_ref[fwd, :] = p_im
            rr_ref[rev, :] = p_re
            ri_ref[rev, :] = p_im
            p_re, p_im = p_re * a_re - p_im * a_im, p_re * a_im + p_im * a_re

    vec = pl.BlockSpec((1, STATE_W), lambda b: (0, b))
    tab = pl.BlockSpec((tc, STATE_W), lambda b: (0, b))
    shp = jax.ShapeDtypeStruct((tc, STATE_ALL), F32)
    return pl.pallas_call(
        kern, name="pow_tables", grid=(LANE_BLOCKS,), in_specs=[vec, vec], out_specs=(tab, tab, tab, tab),
        out_shape=(shp, shp, shp, shp), compiler_params=_cparams("parallel"))(abar_re, abar_im)


def _mod_kernel(c_row, w_ada_bf, b_ada):
    def kern(c_ref, w_ref, b_ref, m_ref, s_ref):
        cv = c_ref[...]
        sc = _silu(cv)
        s_ref[...] = sc
        lhs = jnp.broadcast_to(sc, (SUBLANES, D_MODEL)).astype(BF16)
        m_ref[...] = _dot(lhs, w_ref[...]) + b_ref[...]

    return pl.pallas_call(
        kern, name="ada_mod",
        out_shape=(jax.ShapeDtypeStruct((SUBLANES, 3 * D_MODEL), F32), jax.ShapeDtypeStruct((1, D_MODEL), F32)),
        compiler_params=_cparams())(c_row, w_ada_bf, b_ada)


def _row_spec(tr, width=D_MODEL, col=0):
    return pl.BlockSpec((tr, width), lambda c: (c, col))


def _vec_spec(width=D_MODEL):
    return pl.BlockSpec((1, width), lambda c: (0, 0))


def _col_spec(tr):
    return pl.BlockSpec((D_MODEL, tr), lambda c: (0, c))


def _in_norm(x, g1, scale, shift):
    seq = x.shape[0]
    tr = min(ROW_CHUNK, seq)

    def kern(x_ref, g_ref, sc_ref, sh_ref, h_ref, ht_ref):
        xv = x_ref[...]
        r = lax.rsqrt(jnp.mean(xv * xv, axis=-1, keepdims=True) + RMS_EPS)
        h = ((xv * r) * g_ref[...]) * (1.0 + sc_ref[...]) + sh_ref[...]
        h_ref[...] = h.astype(BF16)
        ht_ref[...] = h.T.astype(BF16)

    return pl.pallas_call(
        kern, name="in_norm", grid=(seq // tr,),
        in_specs=[_row_spec(tr), _vec_spec(), _vec_spec(), _vec_spec()], out_specs=(_row_spec(tr), _col_spec(tr)),
        out_shape=(jax.ShapeDtypeStruct((seq, D_MODEL), BF16), jax.ShapeDtypeStruct((D_MODEL, seq), BF16)),
        compiler_params=_cparams("parallel"))(x, g1, scale, shift)


def _pool_windows(ext, pos, g, w, tr):
    cols = pl.ds(g * POOL_GW, POOL_GW)
    cur = ext[pl.ds(HALO, tr), cols]
    acc = cur
    for k in range(1, w):
        acc = acc + ext[pl.ds(HALO - k, tr), cols]
    cnt = jnp.minimum(pos + 1, w).astype(F32)
    return acc / cnt - cur


def _pool_fwd(proj, pool_w_bf, pscale):
    seq = proj.shape[0]
    tr = min(ROW_CHUNK, seq)
    hb = tr // HALO

    def kern(up_ref, halo_ref, zp_ref, pw_ref, ps_ref, y_ref, yt_ref, ext):
        c = pl.program_id(0)
        ext[0:HALO, :] = jnp.where(c > 0, halo_ref[...], 0.0)
        ext[HALO:, :] = up_ref[...]
        pos = c * tr + lax.broadcasted_iota(jnp.int32, (tr, POOL_GW), 0)
        for g, w in enumerate(POOL_WINDOWS):
            cols = pl.ds(g * POOL_GW, POOL_GW)
            pooled = _pool_windows(ext, pos, g, w, tr)
            mixed = _dot(pooled.astype(BF16), pw_ref[g])
            y = mixed * ps_ref[:, cols] * _silu(zp_ref[:, cols])
            y_ref[:, cols] = y.astype(BF16)
            yt_ref[cols, :] = y.T.astype(BF16)

    return pl.pallas_call(
        kern, name="pool_fwd", grid=(seq // tr,),
        in_specs=[_row_spec(tr, col=0),
                  pl.BlockSpec((HALO, D_MODEL), lambda c: (jnp.maximum(c * hb - 1, 0), 0)),
                  _row_spec(tr, col=1),
                  pl.BlockSpec((len(POOL_WINDOWS), POOL_GW, POOL_GW), lambda c: (0, 0, 0)),
                  _vec_spec()],
        out_specs=(_row_spec(tr), _col_spec(tr)),
        out_shape=(jax.ShapeDtypeStruct((seq, D_MODEL), BF16), jax.ShapeDtypeStruct((D_MODEL, seq), BF16)),
        scratch_shapes=[pltpu.VMEM((tr + HALO, D_MODEL), F32)],
        compiler_params=_cparams("parallel"))(proj, proj, proj, pool_w_bf, pscale)


def _pool_bwd(proj, dyp, pool_w_bf, pscale):
    seq = proj.shape[0]
    tr = min(ROW_CHUNK, seq)
    hb = tr // HALO
    nc = seq // tr
    n_halo = seq // HALO

    def kern(up_ref, halo_ref, zp_ref, zpn_ref, dyp_ref, dypn_ref, pw_ref, ps_ref,
             d01_ref, dpw_ref, dps_ref, ext, dpn, acc_pw, acc_ps):
        c = pl.program_id(0)

        @pl.when(c == 0)
        def _():
            acc_pw[...] = jnp.zeros_like(acc_pw)
            acc_ps[...] = jnp.zeros_like(acc_ps)

        ext[0:HALO, :] = jnp.where(c > 0, halo_ref[...], 0.0)
        ext[HALO:, :] = up_ref[...]
        pos = c * tr + lax.broadcasted_iota(jnp.int32, (tr, POOL_GW), 0)
        pos_n = (c + 1) * tr + lax.broadcasted_iota(jnp.int32, (HALO, POOL_GW), 0)
        has_next = c < nc - 1
        for g, w in enumerate(POOL_WINDOWS):
            cols = pl.ds(g * POOL_GW, POOL_GW)
            pooled_bf = _pool_windows(ext, pos, g, w, tr).astype(BF16)
            wg = pw_ref[g]
            mixed = _dot(pooled_bf, wg)
            zp = zp_ref[:, cols]
            sz = _silu(zp)
            dyp_g = dyp_ref[:, cols]
            ps = ps_ref[:, cols]
            dmixed = (dyp_g * ps * sz).astype(BF16)
            acc_ps[:, cols] += _acc8(dyp_g * mixed * sz)
            d01_ref[:, pl.ds(D_MODEL + g * POOL_GW, POOL_GW)] = (dyp_g * mixed * ps * _dsilu(zp)).astype(BF16)
            acc_pw[g] += _dot_tn(pooled_bf, dmixed)
            dpooled = _dot_nt(dmixed, wg)
            dmixed_n = (jnp.where(has_next, dypn_ref[:, cols], 0.0) * ps * _silu(zpn_ref[:, cols])).astype(BF16)
            dpooled_n = _dot_nt(dmixed_n, wg)
            dpn[0:tr, :] = dpooled / jnp.minimum(pos + 1, w).astype(F32)
            dpn[tr:, :] = dpooled_n / jnp.minimum(pos_n + 1, w).astype(F32)
            acc = dpn[0:tr, :]
            for k in range(1, w):
                acc = acc + dpn[pl.ds(k, tr), :]
            d01_ref[:, cols] = (acc - dpooled).astype(BF16)

        @pl.when(c == nc - 1)
        def _():
            dpw_ref[...] = acc_pw[...]
            dps_ref[...] = jnp.sum(acc_ps[...], axis=0, keepdims=True)

    nxt = lambda c: (jnp.minimum((c + 1) * hb, n_halo - 1), 0)
    nxt1 = lambda c: (jnp.minimum((c + 1) * hb, n_halo - 1), 1)
    return pl.pallas_call(
        kern, name="pool_bwd", grid=(nc,),
        in_specs=[_row_spec(tr, col=0),
                  pl.BlockSpec((HALO, D_MODEL), lambda c: (jnp.maximum(c * hb - 1, 0), 0)),
                  _row_spec(tr, col=1),
                  pl.BlockSpec((HALO, D_MODEL), nxt1),
                  _row_spec(tr),
                  pl.BlockSpec((HALO, D_MODEL), nxt),
                  pl.BlockSpec((len(POOL_WINDOWS), POOL_GW, POOL_GW), lambda c: (0, 0, 0)),
                  _vec_spec()],
        out_specs=(pl.BlockSpec((tr, 2 * D_MODEL), lambda c: (c, 0)),
                   pl.BlockSpec((len(POOL_WINDOWS), POOL_GW, POOL_GW), lambda c: (0, 0, 0)),
                   _vec_spec()),
        out_shape=(jax.ShapeDtypeStruct((seq, 2 * D_MODEL), BF16),
                   jax.ShapeDtypeStruct((len(POOL_WINDOWS), POOL_GW, POOL_GW), F32),
                   jax.ShapeDtypeStruct((1, D_MODEL), F32)),
        scratch_shapes=[pltpu.VMEM((tr + HALO, D_MODEL), F32), pltpu.VMEM((tr + HALO, POOL_GW), F32),
                        pltpu.VMEM((len(POOL_WINDOWS), POOL_GW, POOL_GW), F32), pltpu.VMEM((SUBLANES, D_MODEL), F32)],
        compiler_params=_cparams("arbitrary"))(proj, proj, proj, proj, dyp, dyp, pool_w_bf, pscale)


def _glu_fwd(ys, proj, glu_w_bf, glu_b):
    seq = ys.shape[0]
    tr = min(ROW_CHUNK, seq)

    def kern(ys_ref, zs_ref, w_ref, b_ref, o_ref, ot_ref):
        yg = _gelu(ys_ref[...])
        q = _dot(yg.astype(BF16), w_ref[...]) + b_ref[...]
        y = yg * _sigmoid(q) * _silu(zs_ref[...])
        o_ref[...] = y.astype(BF16)
        ot_ref[...] = y.T.astype(BF16)

    return pl.pallas_call(
        kern, name="glu_fwd", grid=(seq // tr,),
        in_specs=[_row_spec(tr), _row_spec(tr, col=3), pl.BlockSpec((D_MODEL, D_MODEL), lambda c: (0, 0)), _vec_spec()],
        out_specs=(_row_spec(tr), _col_spec(tr)),
        out_shape=(jax.ShapeDtypeStruct((seq, D_MODEL), BF16), jax.ShapeDtypeStruct((D_MODEL, seq), BF16)),
        compiler_params=_cparams("parallel"))(ys, proj, glu_w_bf, glu_b)


def _glu_bwd(ys, proj, dyssm, glu_w_bf, glu_b):
    seq = ys.shape[0]
    tr = min(ROW_CHUNK, seq)
    nc = seq // tr

    def kern(ys_ref, zs_ref, dy_ref, w_ref, b_ref, dys_ref, dzs_ref, dq_ref, yg_ref, db_ref, acc_b):
        c = pl.program_id(0)

        @pl.when(c == 0)
        def _():
            acc_b[...] = jnp.zeros_like(acc_b)

        ysv = ys_ref[...]
        yg = _gelu(ysv)
        yg_bf = yg.astype(BF16)
        q = _dot(yg_bf, w_ref[...]) + b_ref[...]
        sg = _sigmoid(q)
        zs = zs_ref[...]
        dyv = dy_ref[...]
        dyglu = dyv * _silu(zs)
        dzs_ref[...] = (dyv * (yg * sg) * _dsilu(zs)).astype(BF16)
        dq = dyglu * yg * sg * (1.0 - sg)
        dq_bf = dq.astype(BF16)
        acc_b[...] += _acc8(dq)
        dyg = dyglu * sg + _dot_nt(dq_bf, w_ref[...])
        dys_ref[...] = dyg * _dgelu(ysv)
        dq_ref[...] = dq_bf
        yg_ref[...] = yg.T.astype(BF16)

        @pl.when(c == nc - 1)
        def _():
            db_ref[...] = jnp.sum(acc_b[...], axis=0, keepdims=True)

    bf = jax.ShapeDtypeStruct((seq, D_MODEL), BF16)
    return pl.pallas_call(
        kern, name="glu_bwd", grid=(nc,),
        in_specs=[_row_spec(tr), _row_spec(tr, col=3), _row_spec(tr),
                  pl.BlockSpec((D_MODEL, D_MODEL), lambda c: (0, 0)), _vec_spec()],
        out_specs=(_row_spec(tr), _row_spec(tr), _row_spec(tr), _col_spec(tr), _vec_spec()),
        out_shape=(jax.ShapeDtypeStruct((seq, D_MODEL), F32), bf, bf, jax.ShapeDtypeStruct((D_MODEL, seq), BF16),
                   jax.ShapeDtypeStruct((1, D_MODEL), F32)),
        scratch_shapes=[pltpu.VMEM((SUBLANES, D_MODEL), F32)],
        compiler_params=_cparams("arbitrary"))(ys, proj, dyssm, glu_w_bf, glu_b)


def _out_fwd_bwd(ypool, yssm, proj, x, tgt, gate, g2, wbp_bf, wbs_bf, wout_bf):
    seq = x.shape[0]
    tr = min(ROW_CHUNK, seq)
    nc = seq // tr

    def kern(yp_ref, ysm_ref, gp_ref, gs_ref, x_ref, t_ref, gate_ref, g2_ref, wbp_ref, wbs_ref, wo_ref,
             dy_ref, dyp_ref, dys_ref, d45_ref, mb_ref, dob_ref, dbp_ref, dbs_ref, loss_ref, dgate_ref, dg2_ref,
             acc_l, acc_gate, acc_g2):
        c = pl.program_id(0)

        @pl.when(c == 0)
        def _():
            acc_l[...] = jnp.zeros_like(acc_l)
            acc_gate[...] = jnp.zeros_like(acc_gate)
            acc_g2[...] = jnp.zeros_like(acc_g2)

        bp = _dot(yp_ref[...], wbp_ref[...])
        bs = _dot(ysm_ref[...], wbs_ref[...])
        sp = _sigmoid(gp_ref[...])
        ss = _sigmoid(gs_ref[...])
        merged = sp * bp + ss * bs
        mb = merged.astype(BF16)
        out = _dot(mb, wo_ref[...])
        r2 = lax.rsqrt(jnp.mean(out * out, axis=-1, keepdims=True) + RMS_EPS)
        oh = out * r2
        gate_v, g2_v = gate_ref[...], g2_ref[...]
        ohg = oh * g2_v
        diff = (x_ref[...] + gate_v * ohg) - t_ref[...]
        acc_l[...] += _acc8(diff * diff)
        dyv = diff * (1.0 / D_MODEL)
        dy_ref[...] = dyv
        acc_gate[...] += _acc8(dyv * ohg)
        t = dyv * gate_v
        acc_g2[...] += _acc8(t * oh)
        doh = t * g2_v
        dout = r2 * (doh - oh * jnp.mean(doh * oh, axis=-1, keepdims=True))
        dob = dout.astype(BF16)
        dmerged = _dot_nt(dob, wo_ref[...])
        dbp = (dmerged * sp).astype(BF16)
        dbs = (dmerged * ss).astype(BF16)
        d45_ref[:, 0:D_MODEL] = (dmerged * bp * sp * (1.0 - sp)).astype(BF16)
        d45_ref[:, D_MODEL:] = (dmerged * bs * ss * (1.0 - ss)).astype(BF16)
        dyp_ref[...] = _dot_nt(dbp, wbp_ref[...])
        dys_ref[...] = _dot_nt(dbs, wbs_ref[...])
        mb_ref[...] = merged.T.astype(BF16)
        dob_ref[...] = dob
        dbp_ref[...] = dbp
        dbs_ref[...] = dbs

        @pl.when(c == nc - 1)
        def _():
            tot = jnp.sum(acc_l[...], axis=0, keepdims=True)
            loss_ref[...] = jnp.sum(tot, axis=1, keepdims=True) * (0.5 / D_MODEL)
            dgate_ref[...] = jnp.sum(acc_gate[...], axis=0, keepdims=True)
            dg2_ref[...] = jnp.sum(acc_g2[...], axis=0, keepdims=True)

    wspec = pl.BlockSpec((D_MODEL, D_MODEL), lambda c: (0, 0))
    f32 = jax.ShapeDtypeStruct((seq, D_MODEL), F32)
    bf = jax.ShapeDtypeStruct((seq, D_MODEL), BF16)
    vec = jax.ShapeDtypeStruct((1, D_MODEL), F32)
    acc = pltpu.VMEM((SUBLANES, D_MODEL), F32)
    return pl.pallas_call(
        kern, name="out_fwd_bwd", grid=(nc,),
        in_specs=[_row_spec(tr), _row_spec(tr), _row_spec(tr, col=4), _row_spec(tr, col=5), _row_spec(tr), _row_spec(tr),
                  _vec_spec(), _vec_spec(), wspec, wspec, wspec],
        out_specs=(_row_spec(tr), _row_spec(tr), _row_spec(tr), pl.BlockSpec((tr, 2 * D_MODEL), lambda c: (c, 0)),
                   _col_spec(tr), _row_spec(tr), _row_spec(tr), _row_spec(tr),
                   pl.BlockSpec((1, 1), lambda c: (0, 0)), _vec_spec(), _vec_spec()),
        out_shape=(f32, f32, f32, jax.ShapeDtypeStruct((seq, 2 * D_MODEL), BF16),
                   jax.ShapeDtypeStruct((D_MODEL, seq), BF16), bf, bf, bf,
                   jax.ShapeDtypeStruct((1, 1), F32), vec, vec),
        scratch_shapes=[acc, acc, acc],
        compiler_params=_cparams("arbitrary"))(ypool, yssm, proj, proj, x, tgt, gate, g2, wbp_bf, wbs_bf, wout_bf)


def _in_bwd(dh, x, dy, g1, scale):
    seq = x.shape[0]
    tr = min(ROW_CHUNK, seq)
    nc = seq // tr

    def kern(dh_ref, x_ref, dy_ref, g_ref, sc_ref, dx_ref, dsh_ref, dsc_ref, dg_ref, a_sh, a_sc, a_g):
        c = pl.program_id(0)

        @pl.when(c == 0)
        def _():
            a_sh[...] = jnp.zeros_like(a_sh)
            a_sc[...] = jnp.zeros_like(a_sc)
            a_g[...] = jnp.zeros_like(a_g)

        xv = x_ref[...]
        r = lax.rsqrt(jnp.mean(xv * xv, axis=-1, keepdims=True) + RMS_EPS)
        xh = xv * r
        g = g_ref[...]
        dhv = dh_ref[...]
        a_sh[...] += _acc8(dhv)
        a_sc[...] += _acc8(dhv * (xh * g))
        dn = dhv * (1.0 + sc_ref[...])
        a_g[...] += _acc8(dn * xh)
        dxh = dn * g
        dx_ref[...] = dy_ref[...] + r * (dxh - xh * jnp.mean(dxh * xh, axis=-1, keepdims=True))

        @pl.when(c == nc - 1)
        def _():
            dsh_ref[...] = jnp.sum(a_sh[...], axis=0, keepdims=True)
            dsc_ref[...] = jnp.sum(a_sc[...], axis=0, keepdims=True)
            dg_ref[...] = jnp.sum(a_g[...], axis=0, keepdims=True)

    vec = jax.ShapeDtypeStruct((1, D_MODEL), F32)
    acc = pltpu.VMEM((SUBLANES, D_MODEL), F32)
    return pl.pallas_call(
        kern, name="in_bwd", grid=(nc,),
        in_specs=[_row_spec(tr), _row_spec(tr), _row_spec(tr), _vec_spec(), _vec_spec()],
        out_specs=(_row_spec(tr), _vec_spec(), _vec_spec(), _vec_spec()),
        out_shape=(jax.ShapeDtypeStruct((seq, D_MODEL), F32), vec, vec, vec),
        scratch_shapes=[acc, acc, acc],
        compiler_params=_cparams("arbitrary"))(dh, x, dy, g1, scale)


def _local_scan(a_re, a_im, br, bi, xr, xi, row0, ls, reverse, init=None):
    if init is None:
        x_re = jnp.zeros((SUBLANES, STATE_W), F32)
        x_im = jnp.zeros((SUBLANES, STATE_W), F32)
    else:
        x_re, x_im = init
    for i in (range(ls - 1, -1, -1) if reverse else range(ls)):
        src = pl.ds(SUBLANES * i, SUBLANES)
        dst = pl.ds(row0 + SUBLANES * i, SUBLANES)
        n_re = a_re * x_re - a_im * x_im + br[src, :]
        n_im = a_re * x_im + a_im * x_re + bi[src, :]
        x_re, x_im = n_re, n_im
        xr[dst, :] = x_re
        xi[dst, :] = x_im
    return x_re, x_im


def _unpermute_rhs(v, sel):
    hi = v.astype(BF16)
    r1 = v - hi.astype(F32)
    mid = r1.astype(BF16)
    lo = (r1 - mid.astype(F32)).astype(BF16)
    return _dot(hi, sel) + _dot(mid, sel) + _dot(lo, sel)


def _scan_specs(tc, nb, rows_of):
    return dict(
        us=pl.BlockSpec((tc, nb * LANES), lambda b, c: (rows_of(c), 2 * D_MODEL // (nb * LANES) + b)),
        tok=pl.BlockSpec((tc, nb * LANES), lambda b, c: (rows_of(c), b)),
        bblk=pl.BlockSpec((nb, LANES, STATE_W), lambda b, c: (b, 0, 0)),
        cblk=pl.BlockSpec((nb, STATE_W, LANES), lambda b, c: (b, 0, 0)),
        vec=pl.BlockSpec((1, nb * STATE_W), lambda b, c: (0, b)),
        tab=pl.BlockSpec((tc, nb * STATE_W), lambda b, c: (0, b)),
        car=pl.BlockSpec((SUBLANES, nb * STATE_W), lambda b, c: (rows_of(c), b)),
        dvec=pl.BlockSpec((1, nb * LANES), lambda b, c: (0, b)))


def _ssm_scan_fwd(proj, bb_re, bb_im, cm_re, cm_im, abar_re, abar_im, pw_re, pw_im, d_skip, tc):
    seq = proj.shape[0]
    nc = seq // tc
    ls = tc // SUBLANES
    nb = SCAN_BLOCKS

    def kern(us_ref, bbr_ref, bbi_ref, cmr_ref, cmi_ref, ar_ref, ai_ref, pwr_ref, pwi_ref, d_ref,
             ys_ref, ecr_ref, eci_ref, bur, bui, car_r, car_i, end_r, end_i, upb, *nat):
        c = pl.program_id(1)

        @pl.when(c == 0)
        def _():
            car_r[...] = jnp.zeros_like(car_r)
            car_i[...] = jnp.zeros_like(car_i)

        for j in range(nb):
            cols = pl.ds(j * LANES, LANES)
            scols = pl.ds(j * STATE_W, STATE_W)
            nat[j][...] = us_ref[:, cols]
            for i in range(ls):
                upb[j, pl.ds(SUBLANES * i, SUBLANES), :] = nat[j][pl.ds(i, SUBLANES, stride=ls), :]
            u = upb[j]
            up = u.astype(BF16)
            bur[j] = _dot(up, bbr_ref[j])
            bui[j] = _dot(up, bbi_ref[j])
            a_re = jnp.broadcast_to(ar_ref[:, scols], (SUBLANES, STATE_W))
            a_im = jnp.broadcast_to(ai_ref[:, scols], (SUBLANES, STATE_W))
            x_re, x_im = _local_scan(a_re, a_im, bur.at[j], bui.at[j], bur.at[j], bui.at[j], 0, ls, False)
            end_r[j] = x_re
            end_i[j] = x_im
            big_re = pwr_ref[tc - 1:tc, scols]
            big_im = pwi_ref[tc - 1:tc, scols]
            e_re = car_r[j, 0:1, :]
            e_im = car_i[j, 0:1, :]
            for s in range(SUBLANES):
                n_re = end_r[j, s:s + 1, :] + big_re * e_re - big_im * e_im
                n_im = end_i[j, s:s + 1, :] + big_re * e_im + big_im * e_re
                e_re, e_im = n_re, n_im
                if s < SUBLANES - 1:
                    car_r[j, s + 1:s + 2, :] = e_re
                    car_i[j, s + 1:s + 2, :] = e_im
            ec_re = car_r[j]
            ec_im = car_i[j]
            ecr_ref[:, scols] = ec_re
            eci_ref[:, scols] = ec_im
            p_re = pwr_ref[:, scols].reshape(ls, SUBLANES, STATE_W)
            p_im = pwi_ref[:, scols].reshape(ls, SUBLANES, STATE_W)
            xf_re = bur[j].reshape(ls, SUBLANES, STATE_W) + p_re * ec_re[None] - p_im * ec_im[None]
            xf_im = bui[j].reshape(ls, SUBLANES, STATE_W) + p_re * ec_im[None] + p_im * ec_re[None]
            xb_re = xf_re.reshape(tc, STATE_W).astype(BF16)
            xb_im = xf_im.reshape(tc, STATE_W).astype(BF16)
            upb[j] = _dot(xb_re, cmr_ref[j]) - _dot(xb_im, cmi_ref[j]) + d_ref[:, cols] * u
            for i in range(ls):
                nat[j][pl.ds(i, SUBLANES, stride=ls), :] = upb[j, pl.ds(SUBLANES * i, SUBLANES), :]
            ys_ref[:, cols] = nat[j][...]
            car_r[j, 0:1, :] = e_re
            car_i[j, 0:1, :] = e_im

    sp = _scan_specs(tc, nb, lambda c: c)
    carry_shape = jax.ShapeDtypeStruct((nc * SUBLANES, STATE_ALL), F32)
    small = pltpu.VMEM((nb, SUBLANES, STATE_W), F32)
    big = pltpu.VMEM((nb, tc, STATE_W), F32)
    return pl.pallas_call(
        kern, name="ssm_scan_fwd", grid=(LANE_BLOCKS // nb, nc),
        in_specs=[sp["us"], sp["bblk"], sp["bblk"], sp["cblk"], sp["cblk"], sp["vec"], sp["vec"], sp["tab"], sp["tab"],
                  sp["dvec"]],
        out_specs=(sp["tok"], sp["car"], sp["car"]),
        out_shape=(jax.ShapeDtypeStruct((seq, D_MODEL), F32), carry_shape, carry_shape),
        scratch_shapes=[big, big, small, small, small, small, pltpu.VMEM((nb, tc, LANES), F32)]
        + [pltpu.VMEM((tc, LANES), F32)] * nb,
        compiler_params=_cparams("parallel", "arbitrary"),
    )(proj, bb_re, bb_im, cm_re, cm_im, abar_re, abar_im, pw_re, pw_im, d_skip)


def _ssm_scan_bwd(proj, dys, ec_re, ec_im, bb_re, bb_im, cm_re, cm_im, abar_re, abar_im,
                  pw_re, pw_im, pv_re, pv_im, d_skip, tc):
    seq = proj.shape[0]
    nc = seq // tc
    ls = tc // SUBLANES
    nb = SCAN_BLOCKS

    def kern(us_ref, dys_ref, ecr_ref, eci_ref, bbr_ref, bbi_ref, cmr_ref, cmi_ref, ar_ref, ai_ref,
             pwr_ref, pwi_ref, pvr_ref, pvi_ref, d_ref,
             dus_ref, dbbr_ref, dbbi_ref, dcmr_ref, dcmi_ref, dar_ref, dai_ref, dd_ref,
             bur, bui, xr, xi, gr, gi, fc_r, fc_i, a_bbr, a_bbi, a_cmr, a_cmi, a_ar, a_ai, a_dd, upb, dpb, *nat):
        c = pl.program_id(1)

        @pl.when(c == 0)
        def _():
            for ref in (fc_r, fc_i, a_bbr, a_bbi, a_cmr, a_cmi, a_ar, a_ai, a_dd):
                ref[...] = jnp.zeros_like(ref)

        for j in range(nb):
            cols = pl.ds(j * LANES, LANES)
            scols = pl.ds(j * STATE_W, STATE_W)
            nat_u, nat_d = nat[2 * j], nat[2 * j + 1]
            nat_u[...] = us_ref[:, cols]
            nat_d[...] = dys_ref[:, cols]
            for i in range(ls):
                rows_i = pl.ds(SUBLANES * i, SUBLANES)
                upb[j, rows_i, :] = nat_u[pl.ds(i, SUBLANES, stride=ls), :]
                dpb[j, rows_i, :] = nat_d[pl.ds(i, SUBLANES, stride=ls), :]
            u = upb[j]
            dysv = dpb[j]
            a_dd[j] += _acc8(dysv * u)
            up = u.astype(BF16)
            bur[j] = _dot(up, bbr_ref[j])
            bui[j] = _dot(up, bbi_ref[j])
            a_re = jnp.broadcast_to(ar_ref[:, scols], (SUBLANES, STATE_W))
            a_im = jnp.broadcast_to(ai_ref[:, scols], (SUBLANES, STATE_W))
            ec_r = ecr_ref[:, scols]
            ec_i = eci_ref[:, scols]
            xr[j, 0:SUBLANES, :] = ec_r
            xi[j, 0:SUBLANES, :] = ec_i
            _local_scan(a_re, a_im, bur.at[j], bui.at[j], xr.at[j], xi.at[j], SUBLANES, ls, False, init=(ec_r, ec_i))
            xf_re = xr[j, SUBLANES:, :]
            xf_im = xi[j, SUBLANES:, :]
            dysp = dysv.astype(BF16)
            a_cmr[j] += _dot_tn(dysp, xf_re.astype(BF16))
            a_cmi[j] -= _dot_tn(dysp, xf_im.astype(BF16))
            gr[j] = _dot_nt(dysp, cmr_ref[j])
            gi[j] = -_dot_nt(dysp, cmi_ref[j])
            _local_scan(a_re, -a_im, gr.at[j], gi.at[j], gr.at[j], gi.at[j], 0, ls, True)
            big_re = pwr_ref[tc - 1:tc, scols]
            big_im = -pwi_ref[tc - 1:tc, scols]
            f_re = fc_r[j, SUBLANES - 1:SUBLANES, :]
            f_im = fc_i[j, SUBLANES - 1:SUBLANES, :]
            for s in range(SUBLANES - 1, -1, -1):
                n_re = gr[j, s:s + 1, :] + big_re * f_re - big_im * f_im
                n_im = gi[j, s:s + 1, :] + big_re * f_im + big_im * f_re
                f_re, f_im = n_re, n_im
                if s > 0:
                    fc_r[j, s - 1:s, :] = f_re
                    fc_i[j, s - 1:s, :] = f_im
            fcv_r = fc_r[j]
            fcv_i = fc_i[j]
            q_re = pvr_ref[:, scols].reshape(ls, SUBLANES, STATE_W)
            q_im = -pvi_ref[:, scols].reshape(ls, SUBLANES, STATE_W)
            lam_re = (gr[j].reshape(ls, SUBLANES, STATE_W) + q_re * fcv_r[None] - q_im * fcv_i[None]).reshape(tc, STATE_W)
            lam_im = (gi[j].reshape(ls, SUBLANES, STATE_W) + q_re * fcv_i[None] + q_im * fcv_r[None]).reshape(tc, STATE_W)
            fc_r[j, SUBLANES - 1:SUBLANES, :] = f_re
            fc_i[j, SUBLANES - 1:SUBLANES, :] = f_im
            xp_re = xr[j, 0:tc, :]
            xp_im = xi[j, 0:tc, :]
            a_ar[j] += _acc8(lam_re * xp_re + lam_im * xp_im)
            a_ai[j] += _acc8(lam_im * xp_re - lam_re * xp_im)
            lb_re = lam_re.astype(BF16)
            lb_im = lam_im.astype(BF16)
            a_bbr[j] += _dot_tn(up, lb_re)
            a_bbi[j] += _dot_tn(up, lb_im)
            dpb[j] = _dot_nt(lb_re, bbr_ref[j]) + _dot_nt(lb_im, bbi_ref[j]) + dysv * d_ref[:, cols]
            for i in range(ls):
                nat_d[pl.ds(i, SUBLANES, stride=ls), :] = dpb[j, pl.ds(SUBLANES * i, SUBLANES), :]
            dus_ref[:, cols] = nat_d[...]

        @pl.when(c == nc - 1)
        def _():
            row_g = lax.broadcasted_iota(jnp.int32, (LANES, STATE_W), 0) // SSM_H
            col_g = lax.broadcasted_iota(jnp.int32, (LANES, STATE_W), 1) // SSM_P
            fold = (lax.broadcasted_iota(jnp.int32, (STATE_W, SSM_P), 0) % SSM_P
                    == lax.broadcasted_iota(jnp.int32, (STATE_W, SSM_P), 1)).astype(BF16)
            for j in range(nb):
                rows_j = pl.ds(j * LANES, LANES)
                for acc, out in ((a_bbr, dbbr_ref), (a_bbi, dbbi_ref), (a_cmr, dcmr_ref), (a_cmi, dcmi_ref)):
                    out[rows_j, :] = _unpermute_rhs(jnp.where(row_g == col_g, acc[j], 0.0), fold)
                dar_ref[:, pl.ds(j * STATE_W, STATE_W)] = jnp.sum(a_ar[j], axis=0, keepdims=True)
                dai_ref[:, pl.ds(j * STATE_W, STATE_W)] = jnp.sum(a_ai[j], axis=0, keepdims=True)
                dd_ref[:, pl.ds(j * LANES, LANES)] = jnp.sum(a_dd[j], axis=0, keepdims=True)

    sp = _scan_specs(tc, nb, lambda c: nc - 1 - c)
    ghp = pl.BlockSpec((nb * LANES, SSM_P), lambda b, c: (b, 0))
    ghp_shape = jax.ShapeDtypeStruct((SSM_G * SSM_H, SSM_P), F32)
    small = pltpu.VMEM((nb, SUBLANES, STATE_W), F32)
    big = pltpu.VMEM((nb, tc, STATE_W), F32)
    bigp = pltpu.VMEM((nb, tc + SUBLANES, STATE_W), F32)
    blk = pltpu.VMEM((nb, LANES, STATE_W), F32)
    tok = pltpu.VMEM((nb, tc, LANES), F32)
    return pl.pallas_call(
        kern, name="ssm_scan_bwd", grid=(LANE_BLOCKS // nb, nc),
        in_specs=[sp["us"], sp["tok"], sp["car"], sp["car"], sp["bblk"], sp["bblk"], sp["cblk"], sp["cblk"],
                  sp["vec"], sp["vec"], sp["tab"], sp["tab"], sp["tab"], sp["tab"], sp["dvec"]],
        out_specs=(sp["tok"], ghp, ghp, ghp, ghp, sp["vec"], sp["vec"], sp["dvec"]),
        out_shape=(jax.ShapeDtypeStruct((seq, D_MODEL), F32), ghp_shape, ghp_shape, ghp_shape, ghp_shape,
                   jax.ShapeDtypeStruct((1, STATE_ALL), F32), jax.ShapeDtypeStruct((1, STATE_ALL), F32),
                   jax.ShapeDtypeStruct((1, D_MODEL), F32)),
        scratch_shapes=[big, big, bigp, bigp, big, big, small, small, blk, blk, blk, blk,
                        small, small, pltpu.VMEM((nb, SUBLANES, LANES), F32), tok, tok]
        + [pltpu.VMEM((tc, LANES), F32)] * (2 * nb),
        compiler_params=_cparams("parallel", "arbitrary"),
    )(proj, dys, ec_re, ec_im, bb_re, bb_im, cm_re, cm_im, abar_re, abar_im, pw_re, pw_im, pv_re, pv_im, d_skip)


def _eye5():
    return jnp.eye(GROUPS_PER_BLOCK, dtype=F32)[None, :, None, :, None]


def _embed_b(bb_t):
    t = bb_t.transpose(1, 0, 2).reshape(LANE_BLOCKS, GROUPS_PER_BLOCK, SSM_H, 1, SSM_P)
    return (t * _eye5()).reshape(LANE_BLOCKS, LANES, STATE_W)


def _embed_c(c_ghp):
    t = c_ghp.transpose(0, 2, 1).reshape(LANE_BLOCKS, GROUPS_PER_BLOCK, SSM_P, 1, SSM_H)
    return (t * _eye5()).reshape(LANE_BLOCKS, STATE_W, LANES)


def _local_step(x, c_row, tgt, w_ada_bf, b_ada, g1, g2, w_in_bf, pool_w_bf, pscale, a_re, a_im, log_dt,
                b_re_t, b_im_t, c_re, c_im, d_skip, glu_w_bf, glu_b, wbp_bf, wbs_bf, wout_bf):
    seq = x.shape[0]
    tc = min(SCAN_CHUNK, seq)
    mod8, silu_c = _mod_kernel(c_row, w_ada_bf, b_ada)
    mod = mod8[0:1]
    shift, scale, gate = mod[:, 0:D_MODEL], mod[:, D_MODEL:2 * D_MODEL], mod[:, 2 * D_MODEL:]

    abar_re, abar_im, bb_re_t, bb_im_t = _ssm_params(a_re, a_im, log_dt, b_re_t, b_im_t)
    abar_re_f, abar_im_f = abar_re.reshape(1, STATE_ALL), abar_im.reshape(1, STATE_ALL)
    pw_re, pw_im, pv_re, pv_im = _pow_tables(abar_re_f, abar_im_f, tc)
    bbe_re, bbe_im = _embed_b(bb_re_t).astype(BF16), _embed_b(bb_im_t).astype(BF16)
    cme_re, cme_im = _embed_c(c_re).astype(BF16), _embed_c(c_im).astype(BF16)
    d_row = d_skip.reshape(1, D_MODEL)

    h, h_t = _in_norm(x, g1, scale, shift)
    proj = _mm([h], [w_in_bf], name="proj", bm=1024, bn=1024, bk=1024)
    ypool, ypool_t = _pool_fwd(proj, pool_w_bf, pscale)
    ys, ec_re, ec_im = _ssm_scan_fwd(proj, bbe_re, bbe_im, cme_re, cme_im, abar_re_f, abar_im_f,
                                      pw_re, pw_im, d_row, tc)
    yssm, yssm_t = _glu_fwd(ys, proj, glu_w_bf, glu_b)
    (dy, dypool, dyssm, d45, merged_t, dob, dbp, dbs, loss, dgate, dg2) = _out_fwd_bwd(
        ypool, yssm, proj, x, tgt, gate, g2, wbp_bf, wbs_bf, wout_bf)

    d_wout = _mm([merged_t], [dob], name="dw_out", bm=1024, bk=1024)
    d_wbp = _mm([ypool_t], [dbp], name="dw_bp", bm=1024, bk=1024)
    d_wbs = _mm([yssm_t], [dbs], name="dw_bs", bm=1024, bk=1024)
    dys, dzs, dq, yg_t, d_glu_b = _glu_bwd(ys, proj, dyssm, glu_w_bf, glu_b)
    d_glu_w = _mm([yg_t], [dq], name="dw_glu", bm=1024, bk=1024)
    (dus, dbbe_re, dbbe_im, dcme_re, dcme_im, d_abar_re, d_abar_im, d_dskip) = _ssm_scan_bwd(
        proj, dys, ec_re, ec_im, bbe_re, bbe_im, cme_re, cme_im, abar_re_f, abar_im_f,
        pw_re, pw_im, pv_re, pv_im, d_row, tc)
    d01, d_pool_w, d_pscale = _pool_bwd(proj, dypool, pool_w_bf, pscale)
    dparts = [d01, dus, dzs, d45]
    dh = _mm(dparts, [w_in_bf], tb=True, name="dh", bm=1024, bn=1024, bk=1024)
    d_win = _mm([h_t], dparts, name="dw_in", bm=1024, bk=1024)
    grad_x, dshift, dscale, dg1 = _in_bwd(dh, x, dy, g1, scale)
    dmod = jnp.concatenate([dshift, dscale, dgate], axis=1)
    return dict(
        loss=loss[0, 0], grad_x=grad_x, dmod=dmod, silu_c=silu_c, dg1=dg1, dg2=dg2, d_pscale=d_pscale,
        d_glu_b=d_glu_b, d_dskip=d_dskip, d_abar_re=d_abar_re, d_abar_im=d_abar_im,
        d_bb_re_t=dbbe_re.reshape(SSM_G, SSM_H, SSM_P).transpose(1, 0, 2),
        d_bb_im_t=dbbe_im.reshape(SSM_G, SSM_H, SSM_P).transpose(1, 0, 2),
        d_c_re=dcme_re.reshape(SSM_G, SSM_H, SSM_P), d_c_im=dcme_im.reshape(SSM_G, SSM_H, SSM_P),
        d_win=d_win, d_glu_w=d_glu_w, d_wbp=d_wbp, d_wbs=d_wbs, d_wout=d_wout, d_pool_w=d_pool_w)


def _position():
    x, y, c = lax.axis_index("x"), lax.axis_index("y"), lax.axis_index("c")
    chips = [(1 - x, y), (x, 1 - y), (1 - x, 1 - y)]
    return x, y, c, chips


_ANY = pl.BlockSpec(memory_space=pl.ANY)
COMM_CHUNKS = 4
COMM_ROW_ALIGN = 16


def _row_chunks(rows, k):
    assert rows % (k * COMM_ROW_ALIGN) == 0, (rows, k)
    step = rows // k
    return [(q * step, step) for q in range(k)]


def _pad_rows(buf, multiple, axis=0):
    pad = (-buf.shape[axis]) % multiple
    if not pad:
        return buf
    shape = list(buf.shape)
    shape[axis] = pad
    return jnp.concatenate([buf, jnp.zeros(shape, buf.dtype)], axis=axis)


def _ag_weights(packed):
    rows, width = packed.shape
    half = rows // 2
    chunks = _row_chunks(half, COMM_CHUNKS)
    nq = len(chunks)

    def body(p_ref, out_ref, send_sems, recv_sems):
        x, y, c, chips = _position()
        sibling = (x, y, 1 - c)

        def copy(k, chip, h, q, to, src=None):
            start, size = chunks[q]
            rows_q = pl.ds(h * half + start, size)
            dst = out_ref.at[2 * chip[0] + chip[1], rows_q, :]
            return pltpu.make_async_remote_copy(
                src_ref=dst if src is None else src.at[rows_q, :], dst_ref=dst, send_sem=send_sems.at[k * nq + q],
                recv_sem=recv_sems.at[k * nq + q], device_id=to, device_id_type=MESH_ID)

        mine = [copy(6 + h, (x, y), h, q, sibling, src=p_ref) for h in range(2) for q in range(nq)]
        first = [copy(j, (x, y), c, q, (*chip, c), src=p_ref) for q in range(nq) for j, chip in enumerate(chips)]
        for cp in first + mine:
            cp.start()
        passed = []
        for q in range(nq):
            for j, chip in enumerate(chips):
                copy(j, chip, c, q, (x, y, c)).wait_recv()
                fwd = copy(3 + j, chip, c, q, sibling)
                fwd.start()
                passed.append(fwd)
        for q in range(nq):
            for j, chip in enumerate(chips):
                copy(3 + j, chip, 1 - c, q, (x, y, c)).wait_recv()
        for cp in mine:
            cp.wait_recv()
        for cp in first + passed + mine:
            cp.wait_send()

    return pl.pallas_call(
        body, name="ag_weights", in_specs=[_ANY], out_specs=_ANY,
        out_shape=jax.ShapeDtypeStruct((N_CHIPS, rows, width), packed.dtype),
        scratch_shapes=[pltpu.SemaphoreType.DMA((8 * nq,)), pltpu.SemaphoreType.DMA((8 * nq,))],
    )(packed)


def _small_allgather_sum(buf, head_rows):
    rows, width = buf.shape
    chunks = _row_chunks(rows, COMM_CHUNKS)
    nq = len(chunks)

    def body(b_ref, head_ref, sum_ref, all_ref, send_sems, recv_sems, local_sem):
        x, y, c, chips = _position()
        me, sibling = (x, y, c), (x, y, 1 - c)

        def slot(px, py, pc):
            return all_ref.at[4 * px + 2 * py + pc]

        def copy(k, block, q, to, src=None):
            rows_q = pl.ds(chunks[q][0], chunks[q][1])
            dst = slot(*block).at[rows_q, :]
            return pltpu.make_async_remote_copy(
                src_ref=dst if src is None else src.at[rows_q, :], dst_ref=dst, send_sem=send_sems.at[k * nq + q],
                recv_sem=recv_sems.at[k * nq + q], device_id=to, device_id_type=MESH_ID)

        mine = pltpu.make_async_copy(b_ref, slot(*me), local_sem)
        mine.start()
        first = []
        for q in range(nq):
            first += [copy(1 + j, me, q, (*chip, c), src=b_ref) for j, chip in enumerate(chips)]
            first.append(copy(0, me, q, sibling, src=b_ref))
        for cp in first:
            cp.start()
        passed = []
        for q in range(nq):
            for j, chip in enumerate(chips):
                copy(1 + j, (*chip, c), q, me).wait_recv()
                fwd = copy(4 + j, (*chip, c), q, sibling)
                fwd.start()
                passed.append(fwd)
        for q in range(nq):
            copy(0, sibling, q, me).wait_recv()
            for j, chip in enumerate(chips):
                copy(4 + j, (*chip, 1 - c), q, me).wait_recv()
        for cp in first + passed:
            cp.wait_send()
        mine.wait()
        total = all_ref[0]
        for d in range(1, N_DEV):
            total = total + all_ref[d]
        sum_ref[...] = total
        head_ref[...] = all_ref[:, 0:head_rows, :]

    vm = pl.BlockSpec(memory_space=pltpu.VMEM)
    return pl.pallas_call(
        body, name="small_allgather_sum", in_specs=[vm], out_specs=(vm, vm),
        out_shape=(jax.ShapeDtypeStruct((N_DEV, head_rows, width), F32), jax.ShapeDtypeStruct((rows, width), F32)),
        scratch_shapes=[pltpu.VMEM((N_DEV, rows, width), F32), pltpu.SemaphoreType.DMA((7 * nq,)),
                        pltpu.SemaphoreType.DMA((7 * nq,)), pltpu.SemaphoreType.DMA],
        compiler_params=_cparams(),
    )(buf)


def _rs_pair(g):
    n, rows, width = g.shape
    half = rows // 2
    chunks = _row_chunks(half, COMM_CHUNKS)
    nq = len(chunks)

    def body(g_ref, got_ref, send_sems, recv_sems):
        x, y, c, _ = _position()
        swaps = []
        for k in range(n):
            for q, (start, size) in enumerate(chunks):
                swaps.append(pltpu.make_async_remote_copy(
                    src_ref=g_ref.at[k, pl.ds((1 - c) * half + start, size), :], dst_ref=got_ref.at[k, pl.ds(start, size), :],
                    send_sem=send_sems.at[k * nq + q], recv_sem=recv_sems.at[k * nq + q],
                    device_id=(x, y, 1 - c), device_id_type=MESH_ID))
        for cp in swaps:
            cp.start()
        for cp in swaps:
            cp.wait()

    return pl.pallas_call(
        body, name="rs_pair", in_specs=[_ANY], out_specs=_ANY, out_shape=jax.ShapeDtypeStruct((n, half, width), g.dtype),
        scratch_shapes=[pltpu.SemaphoreType.DMA((n * nq,)), pltpu.SemaphoreType.DMA((n * nq,))],
    )(g)


def _rs_chips(part_bf):
    n, rows, width = part_bf.shape
    chunks = _row_chunks(rows, COMM_CHUNKS)
    nq = len(chunks)

    def body(pb_ref, got_ref, send_sems, recv_sems):
        x, y, c, chips = _position()
        sends = []
        for q, (start, size) in enumerate(chunks):
            for j, chip in enumerate(chips):
                sends.append(pltpu.make_async_remote_copy(
                    src_ref=pb_ref.at[2 * chip[0] + chip[1], pl.ds(start, size), :], dst_ref=got_ref.at[j, pl.ds(start, size), :],
                    send_sem=send_sems.at[j * nq + q], recv_sem=recv_sems.at[j * nq + q],
                    device_id=(*chip, c), device_id_type=MESH_ID))
        for cp in sends:
            cp.start()
        for cp in sends:
            cp.wait()

    return pl.pallas_call(
        body, name="rs_chips", in_specs=[_ANY], out_specs=_ANY,
        out_shape=jax.ShapeDtypeStruct((N_CHIPS - 1, rows, width), BF16),
        scratch_shapes=[pltpu.SemaphoreType.DMA((3 * nq,)), pltpu.SemaphoreType.DMA((3 * nq,))],
    )(part_bf)


def _rs_join(shard):
    rows, width = shard.shape
    half = rows // 2
    chunks = _row_chunks(half, COMM_CHUNKS)
    nq = len(chunks)

    def body(in_ref, out_ref, send_sems, recv_sems):
        x, y, c, _ = _position()
        def swap(q, h):
            rows_q = pl.ds(h * half + chunks[q][0], chunks[q][1])
            return pltpu.make_async_remote_copy(
                src_ref=in_ref.at[rows_q, :], dst_ref=out_ref.at[rows_q, :], send_sem=send_sems.at[q],
                recv_sem=recv_sems.at[q], device_id=(x, y, 1 - c), device_id_type=MESH_ID)

        for q in range(nq):
            swap(q, c).start()
        for q in range(nq):
            swap(q, 1 - c).wait_recv()
        for q in range(nq):
            swap(q, c).wait_send()

    return pl.pallas_call(
        body, name="rs_join", in_specs=[_ANY], out_specs=_ANY, input_output_aliases={0: 0},
        out_shape=jax.ShapeDtypeStruct(shard.shape, shard.dtype),
        scratch_shapes=[pltpu.SemaphoreType.DMA((nq,)), pltpu.SemaphoreType.DMA((nq,))],
    )(shard)


def _pair_add(g, got, core):
    n, half, width = got.shape
    nb = 2
    rb = half // nb

    def kern(c_ref, a_ref, b_ref, f_ref, h_ref):
        s = a_ref[...] + b_ref[...]
        f_ref[...] = s
        h_ref[...] = s.astype(BF16)

    spec = pl.BlockSpec((1, rb, width), lambda k, i, c_ref: (k, i, 0))
    return pl.pallas_call(
        kern, name="rs_pair_add",
        grid_spec=pltpu.PrefetchScalarGridSpec(
            num_scalar_prefetch=1, grid=(n, nb),
            in_specs=[pl.BlockSpec((1, rb, width), lambda k, i, c_ref: (k, c_ref[0] * nb + i, 0)), spec],
            out_specs=(spec, spec)),
        out_shape=(jax.ShapeDtypeStruct(got.shape, F32), jax.ShapeDtypeStruct(got.shape, BF16)),
        compiler_params=_cparams("parallel", "parallel"))(core, g, got)


def _chip_add(part_f32, got, where):
    _, rows, width = part_f32.shape
    nb = 2
    rb = rows // nb

    def kern(w_ref, a_ref, b_ref, o_ref):
        o_ref[...] = ((a_ref[0] + b_ref[0].astype(F32)) + b_ref[1].astype(F32)) + b_ref[2].astype(F32)

    return pl.pallas_call(
        kern, name="rs_chip_add",
        grid_spec=pltpu.PrefetchScalarGridSpec(
            num_scalar_prefetch=1, grid=(nb,),
            in_specs=[pl.BlockSpec((1, rb, width), lambda i, w_ref: (w_ref[0], i, 0)),
                      pl.BlockSpec((N_CHIPS - 1, rb, width), lambda i, w_ref: (0, i, 0))],
            out_specs=pl.BlockSpec((rb, width), lambda i, w_ref: (w_ref[1] * nb + i, 0))),
        out_shape=jax.ShapeDtypeStruct((2 * rows, width), F32),
        compiler_params=_cparams("parallel"))(where, part_f32, got)


def _adamw(w, g, m, v, name):
    rows, width = w.shape
    rb = rows
    for cand in (512, 256, 128, 64, 32, 16, 8):
        if rows % cand == 0 and cand * width * 4 <= ADAM_BLOCK_BYTES:
            rb = cand
            break
    spec = pl.BlockSpec((rb, width), lambda i: (i, 0))

    def kern(w_ref, g_ref, m_ref, v_ref, d_ref, nm_ref, nv_ref):
        gv = g_ref[...]
        nm = ADAM_B1 * m_ref[...] + (1.0 - ADAM_B1) * gv
        nv = ADAM_B2 * v_ref[...] + (1.0 - ADAM_B2) * (gv * gv)
        m_hat = nm / (1.0 - ADAM_B1 ** ADAM_STEP)
        v_hat = nv / (1.0 - ADAM_B2 ** ADAM_STEP)
        d_ref[...] = -ADAM_LR * (m_hat / (jnp.sqrt(v_hat) + ADAM_EPS) + ADAM_WD * w_ref[...])
        nm_ref[...] = nm
        nv_ref[...] = nv

    shp = jax.ShapeDtypeStruct(w.shape, F32)
    return pl.pallas_call(
        kern, name=name, grid=(rows // rb,), in_specs=[spec] * 4, out_specs=(spec, spec, spec),
        out_shape=(shp, shp, shp), compiler_params=_cparams("parallel"))(w, g, m, v)


def _wada_grad(silu_t, dmod_cols):
    n = dmod_cols.shape[1]

    def kern(s_ref, d_ref, o_ref):
        acc = s_ref[:, 0:1] * d_ref[0:1, :]
        for b in range(1, N_DEV):
            acc = acc + s_ref[:, b:b + 1] * d_ref[b:b + 1, :]
        o_ref[...] = acc

    return pl.pallas_call(kern, name="wada_grad", out_shape=jax.ShapeDtypeStruct((D_MODEL, n), F32),
                          compiler_params=_cparams())(silu_t, dmod_cols)


def _rows(a, multiple):
    flat = a.reshape(-1)
    pad = (-flat.shape[0]) % (D_MODEL * multiple)
    if pad:
        flat = jnp.concatenate([flat, jnp.zeros((pad,), flat.dtype)])
    return flat.reshape(-1, D_MODEL)


def _part_rows(shape, multiple):
    return -(-int(np.prod(shape)) // (D_MODEL * multiple)) * multiple


def _pack_rows(parts, multiple):
    return jnp.concatenate([_rows(p, multiple) for p in parts], axis=0)


def _unpack_rows(buf, shapes, multiple):
    out, r = [], 0
    for shp in shapes:
        n = int(np.prod(shp))
        nr = _part_rows(shp, multiple)
        out.append(buf[r:r + nr].reshape(-1)[:n].reshape(shp))
        r += nr
    return out


def kernel(x, c, w_ada, b_ada, norm_pre, norm_post, w_in, pool_w, pool_scale, ssm_a_re, ssm_a_im, ssm_log_dt, ssm_b_re, ssm_b_im, ssm_c_re, ssm_c_im, ssm_d, glu_w, glu_b, w_branch_pool, w_branch_ssm, w_out, loss_target, m_w_ada, m_b_ada, m_norm_pre, m_norm_post, m_w_in, m_pool_w, m_pool_scale, m_ssm_a_re, m_ssm_a_im, m_ssm_log_dt, m_ssm_b_re, m_ssm_b_im, m_ssm_c_re, m_ssm_c_im, m_ssm_d, m_glu_w, m_glu_b, m_w_branch_pool, m_w_branch_ssm, m_w_out, v_w_ada, v_b_ada, v_norm_pre, v_norm_post, v_w_in, v_pool_w, v_pool_scale, v_ssm_a_re, v_ssm_a_im, v_ssm_log_dt, v_ssm_b_re, v_ssm_b_im, v_ssm_c_re, v_ssm_c_im, v_ssm_d, v_glu_w, v_glu_b, v_w_branch_pool, v_w_branch_ssm, v_w_out):
    n_ada = w_ada.shape[2]
    n_in = w_in.shape[2]
    n_row = glu_w.shape[1]
    n_pool = pool_w.shape[2]
    n_groups = pool_w.shape[1]

    big_shards = [w_ada[0], w_in[0], pool_w[0], glu_w[0], w_branch_pool[0], w_branch_ssm[0], w_out[0]]
    packed = _pad_rows(_pack_rows([s.astype(BF16) for s in big_shards], 2 * SUBLANES), 2 * COMM_CHUNKS * COMM_ROW_ALIGN)
    gathered = _ag_weights(packed)
    r = 0
    w_ada_bf = gathered[:, r:r + n_ada].reshape(N_CHIPS, D_MODEL, n_ada).transpose(1, 0, 2).reshape(D_MODEL, 3 * D_MODEL)
    r += n_ada
    w_in_bf = gathered[:, r:r + n_in].reshape(N_CHIPS, D_MODEL, n_in).transpose(1, 0, 2).reshape(D_MODEL, N_CHIPS * n_in)
    r += n_in
    pool_rows = n_groups * n_pool * POOL_GW // D_MODEL
    pool_w_bf = gathered[:, r:r + pool_rows].reshape(N_CHIPS, n_groups, n_pool, POOL_GW).transpose(1, 0, 2, 3)
    pool_w_bf = pool_w_bf.reshape(n_groups, POOL_GW, POOL_GW)
    r += pool_rows
    squares = []
    for _ in range(4):
        squares.append(gathered[:, r:r + n_row].reshape(D_MODEL, D_MODEL))
        r += n_row
    glu_w_bf, wbp_bf, wbs_bf, wout_bf = squares

    a_re, a_im, log_dt = ssm_a_re[0], ssm_a_im[0], ssm_log_dt[0].reshape(SSM_G, 1)
    b_re_t, b_im_t = ssm_b_re[0].transpose(2, 0, 1), ssm_b_im[0].transpose(2, 0, 1)
    res = _local_step(x[0], c, loss_target[0], w_ada_bf, b_ada, norm_pre, norm_post, w_in_bf, pool_w_bf, pool_scale,
                      a_re, a_im, log_dt, b_re_t, b_im_t, ssm_c_re[0], ssm_c_im[0], ssm_d[0], glu_w_bf, glu_b[0:1],
                      wbp_bf, wbs_bf, wout_bf)
    loss = lax.psum(res["loss"], ("x", "y", "c"))

    small_parts = [res["dmod"], res["silu_c"], res["dg1"], res["dg2"], res["d_pscale"], res["d_glu_b"], res["d_dskip"],
                   res["d_abar_re"], res["d_abar_im"], res["d_bb_re_t"], res["d_bb_im_t"], res["d_c_re"], res["d_c_im"]]
    small_shapes = [p.shape for p in small_parts]
    head_rows = _part_rows(small_shapes[0], SUBLANES) + _part_rows(small_shapes[1], SUBLANES)
    all_small, sum_small = _small_allgather_sum(
        _pad_rows(_pack_rows(small_parts, SUBLANES), COMM_CHUNKS * COMM_ROW_ALIGN), head_rows)
    (g_b_ada, _, g_norm_pre, g_norm_post, g_pscale, g_glu_b, g_dskip, s_abar_re, s_abar_im, s_bb_re, s_bb_im,
     g_c_re, g_c_im) = _unpack_rows(sum_small, small_shapes, SUBLANES)
    g_a_re, g_a_im, g_log_dt, g_b_re_t, g_b_im_t = _ssm_params_bwd(
        a_re, a_im, log_dt, b_re_t, b_im_t, s_abar_re.reshape(SSM_G, SSM_P), s_abar_im.reshape(SSM_G, SSM_P),
        s_bb_re, s_bb_im)
    chip = 2 * lax.axis_index("x") + lax.axis_index("y")
    dmod_all = all_small[:, 0:3].reshape(N_DEV, 3 * D_MODEL)
    dmod_cols = lax.dynamic_slice_in_dim(dmod_all, chip * n_ada, n_ada, axis=1)
    silu_t = all_small[:, _part_rows(small_shapes[0], SUBLANES)].transpose(1, 0)
    g_w_ada = _wada_grad(silu_t, dmod_cols)

    def by_cols(a, n):
        return a.reshape(D_MODEL, N_CHIPS, n).transpose(1, 0, 2).reshape(N_CHIPS, -1, D_MODEL)

    def by_rows(a):
        return a.reshape(N_CHIPS, n_row, D_MODEL)

    pool_by_chip = res["d_pool_w"].reshape(n_groups, N_CHIPS, n_pool, POOL_GW).transpose(1, 0, 2, 3)
    g_packed = jnp.concatenate(
        [by_cols(res["d_win"], n_in), by_rows(res["d_glu_w"]), by_rows(res["d_wbp"]), by_rows(res["d_wbs"]),
         by_rows(res["d_wout"]), pool_by_chip.reshape(N_CHIPS, pool_rows, D_MODEL)], axis=1)
    g_packed = _pad_rows(g_packed, 2 * COMM_CHUNKS * COMM_ROW_ALIGN, axis=1)
    core = lax.axis_index("c").astype(jnp.int32)
    part_f32, part_bf = _pair_add(g_packed, _rs_pair(g_packed), core.reshape(1))
    shard = _rs_join(_chip_add(part_f32, _rs_chips(part_bf), jnp.stack([chip.astype(jnp.int32), core])))
    r = 0
    g_w_in = shard[r:r + n_in].reshape(D_MODEL, n_in)
    r += n_in
    g_squares = []
    for _ in range(4):
        g_squares.append(shard[r:r + n_row])
        r += n_row
    g_glu_w, g_wbp, g_wbs, g_wout = g_squares
    g_pool_w = shard[r:r + pool_rows].reshape(n_groups * n_pool, POOL_GW)

    big = [("w_ada", w_ada[0], g_w_ada, m_w_ada[0], v_w_ada[0]),
           ("w_in", w_in[0], g_w_in, m_w_in[0], v_w_in[0]),
           ("pool_w", pool_w[0].reshape(n_groups * n_pool, POOL_GW), g_pool_w,
            m_pool_w[0].reshape(n_groups * n_pool, POOL_GW), v_pool_w[0].reshape(n_groups * n_pool, POOL_GW)),
           ("glu_w", glu_w[0], g_glu_w, m_glu_w[0], v_glu_w[0]),
           ("w_branch_pool", w_branch_pool[0], g_wbp, m_w_branch_pool[0], v_w_branch_pool[0]),
           ("w_branch_ssm", w_branch_ssm[0], g_wbs, m_w_branch_ssm[0], v_w_branch_ssm[0]),
           ("w_out", w_out[0], g_wout, m_w_out[0], v_w_out[0])]
    out = {}
    for name, w_, g_, m_, v_ in big:
        d_, nm_, nv_ = _adamw(w_, g_, m_, v_, "adamw_" + name)
        out[name] = (g_, d_, nm_, nv_)

    g_b_re = g_b_re_t.transpose(1, 2, 0)
    g_b_im = g_b_im_t.transpose(1, 2, 0)
    small = [("b_ada", b_ada, g_b_ada, m_b_ada, v_b_ada),
             ("norm_pre", norm_pre, g_norm_pre, m_norm_pre, v_norm_pre),
             ("norm_post", norm_post, g_norm_post, m_norm_post, v_norm_post),
             ("pool_scale", pool_scale, g_pscale, m_pool_scale, v_pool_scale),
             ("ssm_a_re", ssm_a_re, g_a_re, m_ssm_a_re, v_ssm_a_re),
             ("ssm_a_im", ssm_a_im, g_a_im, m_ssm_a_im, v_ssm_a_im),
             ("ssm_log_dt", ssm_log_dt, g_log_dt, m_ssm_log_dt, v_ssm_log_dt),
             ("ssm_b_re", ssm_b_re, g_b_re, m_ssm_b_re, v_ssm_b_re),
             ("ssm_b_im", ssm_b_im, g_b_im, m_ssm_b_im, v_ssm_b_im),
             ("ssm_c_re", ssm_c_re, g_c_re, m_ssm_c_re, v_ssm_c_re),
             ("ssm_c_im", ssm_c_im, g_c_im, m_ssm_c_im, v_ssm_c_im),
             ("ssm_d", ssm_d, g_dskip, m_ssm_d, v_ssm_d),
             ("glu_b", glu_b, g_glu_b, m_glu_b, v_glu_b)]
    shapes = [w_.shape for _, w_, _, _, _ in small]
    pw_, pg_, pm_, pv_ = (_pack_rows([t[i] for t in small], SUBLANES) for i in (1, 2, 3, 4))
    pd_, pnm_, pnv_ = _adamw(pw_, pg_, pm_, pv_, "adamw_small")
    unpacked = [_unpack_rows(p, shapes, SUBLANES) for p in (pg_, pd_, pnm_, pnv_)]
    for (name, _, _, _, _), g_, d_, nm_, nv_ in zip(small, *unpacked):
        out[name] = (g_, d_, nm_, nv_)

    order = ["w_ada", "b_ada", "norm_pre", "norm_post", "w_in", "pool_w", "pool_scale", "ssm_a_re", "ssm_a_im",
             "ssm_log_dt", "ssm_b_re", "ssm_b_im", "ssm_c_re", "ssm_c_im", "ssm_d", "glu_w", "glu_b", "w_branch_pool",
             "w_branch_ssm", "w_out"]
    ref_shape = dict(w_ada=w_ada.shape, w_in=w_in.shape, pool_w=pool_w.shape, glu_w=glu_w.shape,
                     w_branch_pool=w_branch_pool.shape, w_branch_ssm=w_branch_ssm.shape, w_out=w_out.shape)
    for name, w_, _, _, _ in small:
        ref_shape[name] = w_.shape
    results = [loss, res["grad_x"][None]]
    for k in range(4):
        results += [out[name][k].reshape(ref_shape[name]) for name in order]
    return tuple(results)
```

```python
import functools
import math

import numpy as np
import jax
import jax.numpy as jnp
from jax import lax
from jax.experimental import pallas as pl
from jax.experimental.pallas import tpu as pltpu

F32 = jnp.float32
BF16 = jnp.bfloat16
MESH_ID = pl.DeviceIdType.MESH

D_MODEL = 1024
LANES = 128
SUBLANES = 8
SSM_G, SSM_P, SSM_H = 64, 64, 16
LANE_BLOCKS = D_MODEL // LANES
GROUPS_PER_BLOCK = LANES // SSM_H
STATE_W = GROUPS_PER_BLOCK * SSM_P
STATE_ALL = SSM_G * SSM_P
POOL_WINDOWS = (2, 4, 8, 16)
POOL_GW = D_MODEL // len(POOL_WINDOWS)
HALO = 16
RMS_EPS = 1e-6
N_CHIPS = 4
N_DEV = 8

SCAN_CHUNK = 512
SCAN_BLOCKS = 2
ROW_CHUNK = 256
VMEM_LIMIT_BYTES = 56 * 1024 * 1024

ADAM_BLOCK_BYTES = 1 << 20
ADAM_LR, ADAM_B1, ADAM_B2, ADAM_EPS, ADAM_WD, ADAM_STEP = 0.001, 0.9, 0.999, 1e-08, 0.01, 10

_GELU_C0 = math.sqrt(2.0 / math.pi)
_GELU_C1 = 0.044715


def _cparams(*sem):
    if sem:
        return pltpu.CompilerParams(dimension_semantics=sem, vmem_limit_bytes=VMEM_LIMIT_BYTES)
    return pltpu.CompilerParams(vmem_limit_bytes=VMEM_LIMIT_BYTES)


def _sigmoid(v):
    return jax.nn.sigmoid(v)


def _silu(v):
    return v * _sigmoid(v)


def _dsilu(v):
    s = _sigmoid(v)
    return s * (1.0 + v * (1.0 - s))


def _gelu(v):
    return 0.5 * v * (1.0 + jnp.tanh(_GELU_C0 * (v + _GELU_C1 * v * v * v)))


def _dgelu(v):
    t = jnp.tanh(_GELU_C0 * (v + _GELU_C1 * v * v * v))
    return 0.5 * (1.0 + t) + 0.5 * v * (1.0 - t * t) * _GELU_C0 * (1.0 + 3.0 * _GELU_C1 * v * v)


def _dot(a, b):
    return lax.dot_general(a, b, (((1,), (0,)), ((), ())), preferred_element_type=F32)


def _dot_nt(a, b):
    return lax.dot_general(a, b, (((1,), (1,)), ((), ())), preferred_element_type=F32)


def _dot_tn(a, b):
    return lax.dot_general(a, b, (((0,), (0,)), ((), ())), preferred_element_type=F32)


def _acc8(v):
    return v.reshape(v.shape[0] // SUBLANES, SUBLANES, v.shape[1]).sum(axis=0)


class _Ride:
    def __init__(self, inputs, out_shapes, scratch, start, wait):
        self.inputs, self.out_shapes, self.scratch, self.start, self.wait = inputs, out_shapes, scratch, start, wait


def _mm(a_parts, b_parts, *, name, ta=False, tb=False, out_dtype=F32, bm=512, bn=512, bk=512, ride=None):
    a_parts, b_parts = list(a_parts), list(b_parts)
    if ta:
        assert len(a_parts) == 1
        k_dim, m_dim = a_parts[0].shape
    else:
        m_dim = a_parts[0].shape[0]
        k_dim = sum(a.shape[1] for a in a_parts)
    if tb:
        assert len(b_parts) == 1
        n_dim = b_parts[0].shape[0]
    else:
        n_dim = sum(b.shape[1] for b in b_parts)
    bm, bn, bk = min(bm, m_dim), min(bn, n_dim), min(bk, k_dim)
    nm, nn, nk = m_dim // bm, n_dim // bn, k_dim // bk
    a_ranges, off = [], 0
    for a in a_parts:
        cnt = (a.shape[0] if ta else a.shape[1]) // bk
        a_ranges.append((off, cnt))
        off += cnt
    b_ranges, off = [], 0
    for b in b_parts:
        cnt = (b.shape[0] if tb else b.shape[1]) // bn
        b_ranges.append((off, cnt))
        off += cnt

    def a_spec(off, cnt):
        if ta:
            return pl.BlockSpec((bk, bm), lambda i, n, k: (k, i))
        return pl.BlockSpec((bm, bk), lambda i, n, k: (i, jnp.clip(k - off, 0, cnt - 1)))

    def b_spec(off, cnt):
        if tb:
            return pl.BlockSpec((bn, bk), lambda i, n, k: (n, k))
        return pl.BlockSpec((bk, bn), lambda i, n, k: (k, jnp.clip(n - off, 0, cnt - 1)))

    na, nb = len(a_parts), len(b_parts)
    dims = (((0 if ta else 1,), (1 if tb else 0,)), ((), ()))

    def kern_single(a_ref, b_ref, o_ref):
        o_ref[...] = lax.dot_general(a_ref[...].astype(BF16), b_ref[...].astype(BF16), dims,
                                     preferred_element_type=F32).astype(out_dtype)

    if na == 1 and nb == 1 and nk == 1:
        return pl.pallas_call(
            kern_single, name=name, grid=(nm, nn),
            in_specs=[pl.BlockSpec((bk, bm), lambda i, n: (0, i)) if ta else pl.BlockSpec((bm, bk), lambda i, n: (i, 0)),
                      pl.BlockSpec((bn, bk), lambda i, n: (n, 0)) if tb else pl.BlockSpec((bk, bn), lambda i, n: (0, n))],
            out_specs=pl.BlockSpec((bm, bn), lambda i, n: (i, n)),
            out_shape=jax.ShapeDtypeStruct((m_dim, n_dim), out_dtype),
            compiler_params=_cparams("parallel", "parallel"),
        )(a_parts[0], b_parts[0])

    n_rin = len(ride.inputs) if ride else 0
    n_rout = len(ride.out_shapes) if ride else 0

    def kern(*refs):
        a_refs, b_refs = refs[:na], refs[na:na + nb]
        rin = refs[na + nb:na + nb + n_rin]
        o_ref = refs[na + nb + n_rin]
        rout = refs[na + nb + n_rin + 1:na + nb + n_rin + 1 + n_rout]
        acc = refs[na + nb + n_rin + 1 + n_rout]
        rsem = refs[na + nb + n_rin + 2 + n_rout:]
        i, n, k = pl.program_id(0), pl.program_id(1), pl.program_id(2)

        if ride:
            @pl.when((i == 0) & (n == 0) & (k == 0))
            def _():
                ride.start(rin, rout, rsem)

        @pl.when(k == 0)
        def _():
            acc[...] = jnp.zeros_like(acc)

        for ja, (koff, kcnt) in enumerate(a_ranges):
            for jb, (noff, ncnt) in enumerate(b_ranges):
                def step(ja=ja, jb=jb):
                    a = a_refs[ja][...].astype(BF16)
                    b = b_refs[jb][...].astype(BF16)
                    acc[...] += lax.dot_general(a, b, dims, preferred_element_type=F32)

                if na == 1 and nb == 1:
                    step()
                else:
                    cond = (k >= koff) & (k < koff + kcnt) & (n >= noff) & (n < noff + ncnt)
                    pl.when(cond)(step)

        @pl.when(k == nk - 1)
        def _():
            o_ref[...] = acc[...].astype(out_dtype)

        if ride:
            @pl.when((i == nm - 1) & (n == nn - 1) & (k == nk - 1))
            def _():
                ride.wait(rin, rout, rsem)

    any_spec = pl.BlockSpec(memory_space=pl.ANY)
    out_spec = pl.BlockSpec((bm, bn), lambda i, n, k: (i, n))
    out_shape = jax.ShapeDtypeStruct((m_dim, n_dim), out_dtype)
    if not ride:
        return pl.pallas_call(
            kern, name=name, grid=(nm, nn, nk),
            in_specs=[a_spec(*r) for r in a_ranges] + [b_spec(*r) for r in b_ranges],
            out_specs=out_spec, out_shape=out_shape, scratch_shapes=[pltpu.VMEM((bm, bn), F32)],
            compiler_params=_cparams("parallel", "parallel", "arbitrary"),
        )(*a_parts, *b_parts)
    return pl.pallas_call(
        kern, name=name, grid=(nm, nn, nk),
        in_specs=[a_spec(*r) for r in a_ranges] + [b_spec(*r) for r in b_ranges] + [any_spec] * n_rin,
        out_specs=(out_spec,) + (any_spec,) * n_rout, out_shape=(out_shape,) + tuple(ride.out_shapes),
        scratch_shapes=[pltpu.VMEM((bm, bn), F32)] + list(ride.scratch),
        compiler_params=_cparams("arbitrary", "arbitrary", "arbitrary"),
    )(*a_parts, *b_parts, *ride.inputs)


def _ssm_param_fn(a_re, a_im, log_dt, b_re, b_im):
    dt = jnp.exp(log_dt)
    lam_re = jnp.minimum(a_re, -1e-4)
    lam_im = a_im
    mag = jnp.exp(lam_re * dt)
    abar_re = mag * jnp.cos(lam_im * dt)
    abar_im = mag * jnp.sin(lam_im * dt)
    den = lam_re * lam_re + lam_im * lam_im
    num_re = abar_re - 1.0
    f_re = (num_re * lam_re + abar_im * lam_im) / den
    f_im = (abar_im * lam_re - num_re * lam_im) / den
    bb_re = f_re * b_re - f_im * b_im
    bb_im = f_re * b_im + f_im * b_re
    return abar_re, abar_im, bb_re, bb_im


def _ssm_params(a_re, a_im, log_dt, b_re_t, b_im_t):
    def kern(are, aim, ldt, bre, bim, o_ar, o_ai, o_br, o_bi):
        ar, ai, br, bi = _ssm_param_fn(are[...], aim[...], ldt[...], bre[...], bim[...])
        o_ar[...] = ar
        o_ai[...] = ai
        o_br[...] = br
        o_bi[...] = bi

    gp = jax.ShapeDtypeStruct((SSM_G, SSM_P), F32)
    hgp = jax.ShapeDtypeStruct((SSM_H, SSM_G, SSM_P), F32)
    return pl.pallas_call(kern, name="ssm_params", out_shape=(gp, gp, hgp, hgp), compiler_params=_cparams())(
        a_re, a_im, log_dt, b_re_t, b_im_t)


def _ssm_params_bwd(a_re, a_im, log_dt, b_re_t, b_im_t, d_ar, d_ai, d_bbr, d_bbi):
    def kern(are, aim, ldt, bre, bim, dar, dai, dbr, dbi, o_are, o_aim, o_ldt, o_bre, o_bim):
        prim = (are[...], aim[...], ldt[...], bre[...], bim[...])
        _, vjp = jax.vjp(_ssm_param_fn, *prim)
        g = vjp((dar[...], dai[...], dbr[...], dbi[...]))
        o_are[...] = g[0]
        o_aim[...] = g[1]
        o_ldt[...] = g[2]
        o_bre[...] = g[3]
        o_bim[...] = g[4]

    gp = jax.ShapeDtypeStruct((SSM_G, SSM_P), F32)
    g1 = jax.ShapeDtypeStruct((SSM_G, 1), F32)
    hgp = jax.ShapeDtypeStruct((SSM_H, SSM_G, SSM_P), F32)
    return pl.pallas_call(kern, name="ssm_params_bwd", out_shape=(gp, gp, g1, hgp, hgp), compiler_params=_cparams())(
        a_re, a_im, log_dt, b_re_t, b_im_t, d_ar, d_ai, d_bbr, d_bbi)


def _pow_tables(abar_re, abar_im, tc):
    ls = tc // SUBLANES

    def kern(ar_ref, ai_ref, fr_ref, fi_ref, rr_ref, ri_ref):
        a_re = jnp.broadcast_to(ar_ref[...], (SUBLANES, STATE_W))
        a_im = jnp.broadcast_to(ai_ref[...], (SUBLANES, STATE_W))
        p_re, p_im = a_re, a_im
        for i in range(ls):
            fwd = pl.ds(SUBLANES * i, SUBLANES)
            rev = pl.ds(SUBLANES * (ls - 1 - i), SUBLANES)
            fr_ref[fwd, :] = p_re
            fi_ref[fwd, :] = p_im
            rr_ref[rev, :] = p_re
            ri_ref[rev, :] = p_im
            p_re, p_im = p_re * a_re - p_im * a_im, p_re * a_im + p_im * a_re

    vec = pl.BlockSpec((1, STATE_W), lambda b: (0, b))
    tab = pl.BlockSpec((tc, STATE_W), lambda b: (0, b))
    shp = jax.ShapeDtypeStruct((tc, STATE_ALL), F32)
    return pl.pallas_call(
        kern, name="pow_tables", grid=(LANE_BLOCKS,), in_specs=[vec, vec], out_specs=(tab, tab, tab, tab),
        out_shape=(shp, shp, shp, shp), compiler_params=_cparams("parallel"))(abar_re, abar_im)


def _mod_kernel(c_row, w_ada_bf, b_ada):
    def kern(c_ref, w_ref, b_ref, m_ref, s_ref):
        cv = c_ref[...]
        sc = _silu(cv)
        s_ref[...] = sc
        lhs = jnp.broadcast_to(sc, (SUBLANES, D_MODEL)).astype(BF16)
        m_ref[...] = _dot(lhs, w_ref[...]) + b_ref[...]

    return pl.pallas_call(
        kern, name="ada_mod",
        out_shape=(jax.ShapeDtypeStruct((SUBLANES, 3 * D_MODEL), F32), jax.ShapeDtypeStruct((1, D_MODEL), F32)),
        compiler_params=_cparams())(c_row, w_ada_bf, b_ada)


def _row_spec(tr, width=D_MODEL, col=0):
    return pl.BlockSpec((tr, width), lambda c: (c, col))


def _vec_spec(width=D_MODEL):
    return pl.BlockSpec((1, width), lambda c: (0, 0))


def _col_spec(tr):
    return pl.BlockSpec((D_MODEL, tr), lambda c: (0, c))


def _in_norm(x, g1, scale, shift):
    seq = x.shape[0]
    tr = min(ROW_CHUNK, seq)

    def kern(x_ref, g_ref, sc_ref, sh_ref, h_ref, ht_ref):
        xv = x_ref[...]
        r = lax.rsqrt(jnp.mean(xv * xv, axis=-1, keepdims=True) + RMS_EPS)
        h = ((xv * r) * g_ref[...]) * (1.0 + sc_ref[...]) + sh_ref[...]
        h_ref[...] = h.astype(BF16)
        ht_ref[...] = h.T.astype(BF16)

    return pl.pallas_call(
        kern, name="in_norm", grid=(seq // tr,),
        in_specs=[_row_spec(tr), _vec_spec(), _vec_spec(), _vec_spec()], out_specs=(_row_spec(tr), _col_spec(tr)),
        out_shape=(jax.ShapeDtypeStruct((seq, D_MODEL), BF16), jax.ShapeDtypeStruct((D_MODEL, seq), BF16)),
        compiler_params=_cparams("parallel"))(x, g1, scale, shift)


def _pool_windows(ext, pos, g, w, tr):
    cols = pl.ds(g * POOL_GW, POOL_GW)
    cur = ext[pl.ds(HALO, tr), cols]
    acc = cur
    for k in range(1, w):
        acc = acc + ext[pl.ds(HALO - k, tr), cols]
    cnt = jnp.minimum(pos + 1, w).astype(F32)
    return acc / cnt - cur


def _pool_fwd(proj, pool_w_bf, pscale):
    seq = proj.shape[0]
    tr = min(ROW_CHUNK, seq)
    hb = tr // HALO

    def kern(up_ref, halo_ref, zp_ref, pw_ref, ps_ref, y_ref, yt_ref, ext):
        c = pl.program_id(0)
        ext[0:HALO, :] = jnp.where(c > 0, halo_ref[...], 0.0)
        ext[HALO:, :] = up_ref[...]
        pos = c * tr + lax.broadcasted_iota(jnp.int32, (tr, POOL_GW), 0)
        for g, w in enumerate(POOL_WINDOWS):
            cols = pl.ds(g * POOL_GW, POOL_GW)
            pooled = _pool_windows(ext, pos, g, w, tr)
            mixed = _dot(pooled.astype(BF16), pw_ref[g])
            y = mixed * ps_ref[:, cols] * _silu(zp_ref[:, cols])
            y_ref[:, cols] = y.astype(BF16)
            yt_ref[cols, :] = y.T.astype(BF16)

    return pl.pallas_call(
        kern, name="pool_fwd", grid=(seq // tr,),
        in_specs=[_row_spec(tr, col=0),
                  pl.BlockSpec((HALO, D_MODEL), lambda c: (jnp.maximum(c * hb - 1, 0), 0)),
                  _row_spec(tr, col=1),
                  pl.BlockSpec((len(POOL_WINDOWS), POOL_GW, POOL_GW), lambda c: (0, 0, 0)),
                  _vec_spec()],
        out_specs=(_row_spec(tr), _col_spec(tr)),
        out_shape=(jax.ShapeDtypeStruct((seq, D_MODEL), BF16), jax.ShapeDtypeStruct((D_MODEL, seq), BF16)),
        scratch_shapes=[pltpu.VMEM((tr + HALO, D_MODEL), F32)],
        compiler_params=_cparams("parallel"))(proj, proj, proj, pool_w_bf, pscale)


def _pool_bwd(proj, dyp, pool_w_bf, pscale):
    seq = proj.shape[0]
    tr = min(ROW_CHUNK, seq)
    hb = tr // HALO
    nc = seq // tr
    n_halo = seq // HALO

    def kern(up_ref, halo_ref, zp_ref, zpn_ref, dyp_ref, dypn_ref, pw_ref, ps_ref,
             d01_ref, dpw_ref, dps_ref, ext, dpn, acc_pw, acc_ps):
        c = pl.program_id(0)

        @pl.when(c == 0)
        def _():
            acc_pw[...] = jnp.zeros_like(acc_pw)
            acc_ps[...] = jnp.zeros_like(acc_ps)

        ext[0:HALO, :] = jnp.where(c > 0, halo_ref[...], 0.0)
        ext[HALO:, :] = up_ref[...]
        pos = c * tr + lax.broadcasted_iota(jnp.int32, (tr, POOL_GW), 0)
        pos_n = (c + 1) * tr + lax.broadcasted_iota(jnp.int32, (HALO, POOL_GW), 0)
        has_next = c < nc - 1
        for g, w in enumerate(POOL_WINDOWS):
            cols = pl.ds(g * POOL_GW, POOL_GW)
            pooled_bf = _pool_windows(ext, pos, g, w, tr).astype(BF16)
            wg = pw_ref[g]
            mixed = _dot(pooled_bf, wg)
            zp = zp_ref[:, cols]
            sz = _silu(zp)
            dyp_g = dyp_ref[:, cols]
            ps = ps_ref[:, cols]
            dmixed = (dyp_g * ps * sz).astype(BF16)
            acc_ps[:, cols] += _acc8(dyp_g * mixed * sz)
            d01_ref[:, pl.ds(D_MODEL + g * POOL_GW, POOL_GW)] = (dyp_g * mixed * ps * _dsilu(zp)).astype(BF16)
            acc_pw[g] += _dot_tn(pooled_bf, dmixed)
            dpooled = _dot_nt(dmixed, wg)
            dmixed_n = (jnp.where(has_next, dypn_ref[:, cols], 0.0) * ps * _silu(zpn_ref[:, cols])).astype(BF16)
            dpooled_n = _dot_nt(dmixed_n, wg)
            dpn[0:tr, :] = dpooled / jnp.minimum(pos + 1, w).astype(F32)
            dpn[tr:, :] = dpooled_n / jnp.minimum(pos_n + 1, w).astype(F32)
            acc = dpn[0:tr, :]
            for k in range(1, w):
                acc = acc + dpn[pl.ds(k, tr), :]
            d01_ref[:, cols] = (acc - dpooled).astype(BF16)

        @pl.when(c == nc - 1)
        def _():
            dpw_ref[...] = acc_pw[...]
            dps_ref[...] = jnp.sum(acc_ps[...], axis=0, keepdims=True)

    nxt = lambda c: (jnp.minimum((c + 1) * hb, n_halo - 1), 0)
    nxt1 = lambda c: (jnp.minimum((c + 1) * hb, n_halo - 1), 1)
    return pl.pallas_call(
        kern, name="pool_bwd", grid=(nc,),
        in_specs=[_row_spec(tr, col=0),
                  pl.BlockSpec((HALO, D_MODEL), lambda c: (jnp.maximum(c * hb - 1, 0), 0)),
                  _row_spec(tr, col=1),
                  pl.BlockSpec((HALO, D_MODEL), nxt1),
                  _row_spec(tr),
                  pl.BlockSpec((HALO, D_MODEL), nxt),
                  pl.BlockSpec((len(POOL_WINDOWS), POOL_GW, POOL_GW), lambda c: (0, 0, 0)),
                  _vec_spec()],
        out_specs=(pl.BlockSpec((tr, 2 * D_MODEL), lambda c: (c, 0)),
                   pl.BlockSpec((len(POOL_WINDOWS), POOL_GW, POOL_GW), lambda c: (0, 0, 0)),
                   _vec_spec()),
        out_shape=(jax.ShapeDtypeStruct((seq, 2 * D_MODEL), BF16),
                   jax.ShapeDtypeStruct((len(POOL_WINDOWS), POOL_GW, POOL_GW), F32),
                   jax.ShapeDtypeStruct((1, D_MODEL), F32)),
        scratch_shapes=[pltpu.VMEM((tr + HALO, D_MODEL), F32), pltpu.VMEM((tr + HALO, POOL_GW), F32),
                        pltpu.VMEM((len(POOL_WINDOWS), POOL_GW, POOL_GW), F32), pltpu.VMEM((SUBLANES, D_MODEL), F32)],
        compiler_params=_cparams("arbitrary"))(proj, proj, proj, proj, dyp, dyp, pool_w_bf, pscale)


def _glu_fwd(ys, proj, glu_w_bf, glu_b):
    seq = ys.shape[0]
    tr = min(ROW_CHUNK, seq)

    def kern(ys_ref, zs_ref, w_ref, b_ref, o_ref, ot_ref):
        yg = _gelu(ys_ref[...])
        q = _dot(yg.astype(BF16), w_ref[...]) + b_ref[...]
        y = yg * _sigmoid(q) * _silu(zs_ref[...])
        o_ref[...] = y.astype(BF16)
        ot_ref[...] = y.T.astype(BF16)

    return pl.pallas_call(
        kern, name="glu_fwd", grid=(seq // tr,),
        in_specs=[_row_spec(tr), _row_spec(tr, col=3), pl.BlockSpec((D_MODEL, D_MODEL), lambda c: (0, 0)), _vec_spec()],
        out_specs=(_row_spec(tr), _col_spec(tr)),
        out_shape=(jax.ShapeDtypeStruct((seq, D_MODEL), BF16), jax.ShapeDtypeStruct((D_MODEL, seq), BF16)),
        compiler_params=_cparams("parallel"))(ys, proj, glu_w_bf, glu_b)


def _glu_bwd(ys, proj, dyssm, glu_w_bf, glu_b):
    seq = ys.shape[0]
    tr = min(ROW_CHUNK, seq)
    nc = seq // tr

    def kern(ys_ref, zs_ref, dy_ref, w_ref, b_ref, dys_ref, dzs_ref, dq_ref, yg_ref, db_ref, acc_b):
        c = pl.program_id(0)

        @pl.when(c == 0)
        def _():
            acc_b[...] = jnp.zeros_like(acc_b)

        ysv = ys_ref[...]
        yg = _gelu(ysv)
        yg_bf = yg.astype(BF16)
        q = _dot(yg_bf, w_ref[...]) + b_ref[...]
        sg = _sigmoid(q)
        zs = zs_ref[...]
        dyv = dy_ref[...]
        dyglu = dyv * _silu(zs)
        dzs_ref[...] = (dyv * (yg * sg) * _dsilu(zs)).astype(BF16)
        dq = dyglu * yg * sg * (1.0 - sg)
        dq_bf = dq.astype(BF16)
        acc_b[...] += _acc8(dq)
        dyg = dyglu * sg + _dot_nt(dq_bf, w_ref[...])
        dys_ref[...] = dyg * _dgelu(ysv)
        dq_ref[...] = dq_bf
        yg_ref[...] = yg.T.astype(BF16)

        @pl.when(c == nc - 1)
        def _():
            db_ref[...] = jnp.sum(acc_b[...], axis=0, keepdims=True)

    bf = jax.ShapeDtypeStruct((seq, D_MODEL), BF16)
    return pl.pallas_call(
        kern, name="glu_bwd", grid=(nc,),
        in_specs=[_row_spec(tr), _row_spec(tr, col=3), _row_spec(tr),
                  pl.BlockSpec((D_MODEL, D_MODEL), lambda c: (0, 0)), _vec_spec()],
        out_specs=(_row_spec(tr), _row_spec(tr), _row_spec(tr), _col_spec(tr), _vec_spec()),
        out_shape=(jax.ShapeDtypeStruct((seq, D_MODEL), F32), bf, bf, jax.ShapeDtypeStruct((D_MODEL, seq), BF16),
                   jax.ShapeDtypeStruct((1, D_MODEL), F32)),
        scratch_shapes=[pltpu.VMEM((SUBLANES, D_MODEL), F32)],
        compiler_params=_cparams("arbitrary"))(ys, proj, dyssm, glu_w_bf, glu_b)


def _out_fwd_bwd(ypool, yssm, proj, x, tgt, gate, g2, wbp_bf, wbs_bf, wout_bf):
    seq = x.shape[0]
    tr = min(ROW_CHUNK, seq)
    nc = seq // tr

    def kern(yp_ref, ysm_ref, gp_ref, gs_ref, x_ref, t_ref, gate_ref, g2_ref, wbp_ref, wbs_ref, wo_ref,
             dy_ref, dyp_ref, dys_ref, d45_ref, mb_ref, dob_ref, dbp_ref, dbs_ref, loss_ref, dgate_ref, dg2_ref,
             acc_l, acc_gate, acc_g2):
        c = pl.program_id(0)

        @pl.when(c == 0)
        def _():
            acc_l[...] = jnp.zeros_like(acc_l)
            acc_gate[...] = jnp.zeros_like(acc_gate)
            acc_g2[...] = jnp.zeros_like(acc_g2)

        bp = _dot(yp_ref[...], wbp_ref[...])
        bs = _dot(ysm_ref[...], wbs_ref[...])
        sp = _sigmoid(gp_ref[...])
        ss = _sigmoid(gs_ref[...])
        merged = sp * bp + ss * bs
        mb = merged.astype(BF16)
        out = _dot(mb, wo_ref[...])
        r2 = lax.rsqrt(jnp.mean(out * out, axis=-1, keepdims=True) + RMS_EPS)
        oh = out * r2
        gate_v, g2_v = gate_ref[...], g2_ref[...]
        ohg = oh * g2_v
        diff = (x_ref[...] + gate_v * ohg) - t_ref[...]
        acc_l[...] += _acc8(diff * diff)
        dyv = diff * (1.0 / D_MODEL)
        dy_ref[...] = dyv
        acc_gate[...] += _acc8(dyv * ohg)
        t = dyv * gate_v
        acc_g2[...] += _acc8(t * oh)
        doh = t * g2_v
        dout = r2 * (doh - oh * jnp.mean(doh * oh, axis=-1, keepdims=True))
        dob = dout.astype(BF16)
        dmerged = _dot_nt(dob, wo_ref[...])
        dbp = (dmerged * sp).astype(BF16)
        dbs = (dmerged * ss).astype(BF16)
        d45_ref[:, 0:D_MODEL] = (dmerged * bp * sp * (1.0 - sp)).astype(BF16)
        d45_ref[:, D_MODEL:] = (dmerged * bs * ss * (1.0 - ss)).astype(BF16)
        dyp_ref[...] = _dot_nt(dbp, wbp_ref[...])
        dys_ref[...] = _dot_nt(dbs, wbs_ref[...])
        mb_ref[...] = merged.T.astype(BF16)
        dob_ref[...] = dob
        dbp_ref[...] = dbp
        dbs_ref[...] = dbs

        @pl.when(c == nc - 1)
        def _():
            tot = jnp.sum(acc_l[...], axis=0, keepdims=True)
            loss_ref[...] = jnp.sum(tot, axis=1, keepdims=True) * (0.5 / D_MODEL)
            dgate_ref[...] = jnp.sum(acc_gate[...], axis=0, keepdims=True)
            dg2_ref[...] = jnp.sum(acc_g2[...], axis=0, keepdims=True)

    wspec = pl.BlockSpec((D_MODEL, D_MODEL), lambda c: (0, 0))
    f32 = jax.ShapeDtypeStruct((seq, D_MODEL), F32)
    bf = jax.ShapeDtypeStruct((seq, D_MODEL), BF16)
    vec = jax.ShapeDtypeStruct((1, D_MODEL), F32)
    acc = pltpu.VMEM((SUBLANES, D_MODEL), F32)
    return pl.pallas_call(
        kern, name="out_fwd_bwd", grid=(nc,),
        in_specs=[_row_spec(tr), _row_spec(tr), _row_spec(tr, col=4), _row_spec(tr, col=5), _row_spec(tr), _row_spec(tr),
                  _vec_spec(), _vec_spec(), wspec, wspec, wspec],
        out_specs=(_row_spec(tr), _row_spec(tr), _row_spec(tr), pl.BlockSpec((tr, 2 * D_MODEL), lambda c: (c, 0)),
                   _col_spec(tr), _row_spec(tr), _row_spec(tr), _row_spec(tr),
                   pl.BlockSpec((1, 1), lambda c: (0, 0)), _vec_spec(), _vec_spec()),
        out_shape=(f32, f32, f32, jax.ShapeDtypeStruct((seq, 2 * D_MODEL), BF16),
                   jax.ShapeDtypeStruct((D_MODEL, seq), BF16), bf, bf, bf,
                   jax.ShapeDtypeStruct((1, 1), F32), vec, vec),
        scratch_shapes=[acc, acc, acc],
        compiler_params=_cparams("arbitrary"))(ypool, yssm, proj, proj, x, tgt, gate, g2, wbp_bf, wbs_bf, wout_bf)


def _in_bwd(dh, x, dy, g1, scale):
    seq = x.shape[0]
    tr = min(ROW_CHUNK, seq)
    nc = seq // tr

    def kern(dh_ref, x_ref, dy_ref, g_ref, sc_ref, dx_ref, dsh_ref, dsc_ref, dg_ref, a_sh, a_sc, a_g):
        c = pl.program_id(0)

        @pl.when(c == 0)
        def _():
            a_sh[...] = jnp.zeros_like(a_sh)
            a_sc[...] = jnp.zeros_like(a_sc)
            a_g[...] = jnp.zeros_like(a_g)

        xv = x_ref[...]
        r = lax.rsqrt(jnp.mean(xv * xv, axis=-1, keepdims=True) + RMS_EPS)
        xh = xv * r
        g = g_ref[...]
        dhv = dh_ref[...]
        a_sh[...] += _acc8(dhv)
        a_sc[...] += _acc8(dhv * (xh * g))
        dn = dhv * (1.0 + sc_ref[...])
        a_g[...] += _acc8(dn * xh)
        dxh = dn * g
        dx_ref[...] = dy_ref[...] + r * (dxh - xh * jnp.mean(dxh * xh, axis=-1, keepdims=True))

        @pl.when(c == nc - 1)
        def _():
            dsh_ref[...] = jnp.sum(a_sh[...], axis=0, keepdims=True)
            dsc_ref[...] = jnp.sum(a_sc[...], axis=0, keepdims=True)
            dg_ref[...] = jnp.sum(a_g[...], axis=0, keepdims=True)

    vec = jax.ShapeDtypeStruct((1, D_MODEL), F32)
    acc = pltpu.VMEM((SUBLANES, D_MODEL), F32)
    return pl.pallas_call(
        kern, name="in_bwd", grid=(nc,),
        in_specs=[_row_spec(tr), _row_spec(tr), _row_spec(tr), _vec_spec(), _vec_spec()],
        out_specs=(_row_spec(tr), _vec_spec(), _vec_spec(), _vec_spec()),
        out_shape=(jax.ShapeDtypeStruct((seq, D_MODEL), F32), vec, vec, vec),
        scratch_shapes=[acc, acc, acc],
        compiler_params=_cparams("arbitrary"))(dh, x, dy, g1, scale)


def _local_scan(a_re, a_im, br, bi, xr, xi, row0, ls, reverse, init=None):
    if init is None:
        x_re = jnp.zeros((SUBLANES, STATE_W), F32)
        x_im = jnp.zeros((SUBLANES, STATE_W), F32)
    else:
        x_re, x_im = init
    for i in (range(ls - 1, -1, -1) if reverse else range(ls)):
        src = pl.ds(SUBLANES * i, SUBLANES)
        dst = pl.ds(row0 + SUBLANES * i, SUBLANES)
        n_re = a_re * x_re - a_im * x_im + br[src, :]
        n_im = a_re * x_im + a_im * x_re + bi[src, :]
        x_re, x_im = n_re, n_im
        xr[dst, :] = x_re
        xi[dst, :] = x_im
    return x_re, x_im


def _unpermute_rhs(v, sel):
    hi = v.astype(BF16)
    r1 = v - hi.astype(F32)
    mid = r1.astype(BF16)
    lo = (r1 - mid.astype(F32)).astype(BF16)
    return _dot(hi, sel) + _dot(mid, sel) + _dot(lo, sel)


def _scan_specs(tc, nb, rows_of):
    return dict(
        us=pl.BlockSpec((tc, nb * LANES), lambda b, c: (rows_of(c), 2 * D_MODEL // (nb * LANES) + b)),
        tok=pl.BlockSpec((tc, nb * LANES), lambda b, c: (rows_of(c), b)),
        bblk=pl.BlockSpec((nb, LANES, STATE_W), lambda b, c: (b, 0, 0)),
        cblk=pl.BlockSpec((nb, STATE_W, LANES), lambda b, c: (b, 0, 0)),
        vec=pl.BlockSpec((1, nb * STATE_W), lambda b, c: (0, b)),
        tab=pl.BlockSpec((tc, nb * STATE_W), lambda b, c: (0, b)),
        car=pl.BlockSpec((SUBLANES, nb * STATE_W), lambda b, c: (rows_of(c), b)),
        dvec=pl.BlockSpec((1, nb * LANES), lambda b, c: (0, b)))


def _ssm_scan_fwd(proj, bb_re, bb_im, cm_re, cm_im, abar_re, abar_im, pw_re, pw_im, d_skip, tc):
    seq = proj.shape[0]
    nc = seq // tc
    ls = tc // SUBLANES
    nb = SCAN_BLOCKS

    def kern(us_ref, bbr_ref, bbi_ref, cmr_ref, cmi_ref, ar_ref, ai_ref, pwr_ref, pwi_ref, d_ref,
             ys_ref, ecr_ref, eci_ref, bur, bui, car_r, car_i, end_r, end_i, upb, *nat):
        c = pl.program_id(1)

        @pl.when(c == 0)
        def _():
            car_r[...] = jnp.zeros_like(car_r)
            car_i[...] = jnp.zeros_like(car_i)

        for j in range(nb):
            cols = pl.ds(j * LANES, LANES)
            scols = pl.ds(j * STATE_W, STATE_W)
            nat[j][...] = us_ref[:, cols]
            for i in range(ls):
                upb[j, pl.ds(SUBLANES * i, SUBLANES), :] = nat[j][pl.ds(i, SUBLANES, stride=ls), :]
            u = upb[j]
            up = u.astype(BF16)
            bur[j] = _dot(up, bbr_ref[j])
            bui[j] = _dot(up, bbi_ref[j])
            a_re = jnp.broadcast_to(ar_ref[:, scols], (SUBLANES, STATE_W))
            a_im = jnp.broadcast_to(ai_ref[:, scols], (SUBLANES, STATE_W))
            x_re, x_im = _local_scan(a_re, a_im, bur.at[j], bui.at[j], bur.at[j], bui.at[j], 0, ls, False)
            end_r[j] = x_re
            end_i[j] = x_im
            big_re = pwr_ref[tc - 1:tc, scols]
            big_im = pwi_ref[tc - 1:tc, scols]
            e_re = car_r[j, 0:1, :]
            e_im = car_i[j, 0:1, :]
            for s in range(SUBLANES):
                n_re = end_r[j, s:s + 1, :] + big_re * e_re - big_im * e_im
                n_im = end_i[j, s:s + 1, :] + big_re * e_im + big_im * e_re
                e_re, e_im = n_re, n_im
                if s < SUBLANES - 1:
                    car_r[j, s + 1:s + 2, :] = e_re
                    car_i[j, s + 1:s + 2, :] = e_im
            ec_re = car_r[j]
            ec_im = car_i[j]
            ecr_ref[:, scols] = ec_re
            eci_ref[:, scols] = ec_im
            p_re = pwr_ref[:, scols].reshape(ls, SUBLANES, STATE_W)
            p_im = pwi_ref[:, scols].reshape(ls, SUBLANES, STATE_W)
            xf_re = bur[j].reshape(ls, SUBLANES, STATE_W) + p_re * ec_re[None] - p_im * ec_im[None]
            xf_im = bui[j].reshape(ls, SUBLANES, STATE_W) + p_re * ec_im[None] + p_im * ec_re[None]
            xb_re = xf_re.reshape(tc, STATE_W).astype(BF16)
            xb_im = xf_im.reshape(tc, STATE_W).astype(BF16)
            upb[j] = _dot(xb_re, cmr_ref[j]) - _dot(xb_im, cmi_ref[j]) + d_ref[:, cols] * u
            for i in range(ls):
                nat[j][pl.ds(i, SUBLANES, stride=ls), :] = upb[j, pl.ds(SUBLANES * i, SUBLANES), :]
            ys_ref[:, cols] = nat[j][...]
            car_r[j, 0:1, :] = e_re
            car_i[j, 0:1, :] = e_im

    sp = _scan_specs(tc, nb, lambda c: c)
    carry_shape = jax.ShapeDtypeStruct((nc * SUBLANES, STATE_ALL), F32)
    small = pltpu.VMEM((nb, SUBLANES, STATE_W), F32)
    big = pltpu.VMEM((nb, tc, STATE_W), F32)
    return pl.pallas_call(
        kern, name="ssm_scan_fwd", grid=(LANE_BLOCKS // nb, nc),
        in_specs=[sp["us"], sp["bblk"], sp["bblk"], sp["cblk"], sp["cblk"], sp["vec"], sp["vec"], sp["tab"], sp["tab"],
                  sp["dvec"]],
        out_specs=(sp["tok"], sp["car"], sp["car"]),
        out_shape=(jax.ShapeDtypeStruct((seq, D_MODEL), F32), carry_shape, carry_shape),
        scratch_shapes=[big, big, small, small, small, small, pltpu.VMEM((nb, tc, LANES), F32)]
        + [pltpu.VMEM((tc, LANES), F32)] * nb,
        compiler_params=_cparams("parallel", "arbitrary"),
    )(proj, bb_re, bb_im, cm_re, cm_im, abar_re, abar_im, pw_re, pw_im, d_skip)


def _ssm_scan_bwd(proj, dys, ec_re, ec_im, bb_re, bb_im, cm_re, cm_im, abar_re, abar_im,
                  pw_re, pw_im, pv_re, pv_im, d_skip, tc):
    seq = proj.shape[0]
    nc = seq // tc
    ls = tc // SUBLANES
    nb = SCAN_BLOCKS

    def kern(us_ref, dys_ref, ecr_ref, eci_ref, bbr_ref, bbi_ref, cmr_ref, cmi_ref, ar_ref, ai_ref,
             pwr_ref, pwi_ref, pvr_ref, pvi_ref, d_ref,
             dus_ref, dbbr_ref, dbbi_ref, dcmr_ref, dcmi_ref, dar_ref, dai_ref, dd_ref,
             bur, bui, xr, xi, gr, gi, fc_r, fc_i, a_bbr, a_bbi, a_cmr, a_cmi, a_ar, a_ai, a_dd, upb, dpb, *nat):
        c = pl.program_id(1)

        @pl.when(c == 0)
        def _():
            for ref in (fc_r, fc_i, a_bbr, a_bbi, a_cmr, a_cmi, a_ar, a_ai, a_dd):
                ref[...] = jnp.zeros_like(ref)

        for j in range(nb):
            cols = pl.ds(j * LANES, LANES)
            scols = pl.ds(j * STATE_W, STATE_W)
            nat_u, nat_d = nat[2 * j], nat[2 * j + 1]
            nat_u[...] = us_ref[:, cols]
            nat_d[...] = dys_ref[:, cols]
            for i in range(ls):
                rows_i = pl.ds(SUBLANES * i, SUBLANES)
                upb[j, rows_i, :] = nat_u[pl.ds(i, SUBLANES, stride=ls), :]
                dpb[j, rows_i, :] = nat_d[pl.ds(i, SUBLANES, stride=ls), :]
            u = upb[j]
            dysv = dpb[j]
            a_dd[j] += _acc8(dysv * u)
            up = u.astype(BF16)
            bur[j] = _dot(up, bbr_ref[j])
            bui[j] = _dot(up, bbi_ref[j])
            a_re = jnp.broadcast_to(ar_ref[:, scols], (SUBLANES, STATE_W))
            a_im = jnp.broadcast_to(ai_ref[:, scols], (SUBLANES, STATE_W))
            ec_r = ecr_ref[:, scols]
            ec_i = eci_ref[:, scols]
            xr[j, 0:SUBLANES, :] = ec_r
            xi[j, 0:SUBLANES, :] = ec_i
            _local_scan(a_re, a_im, bur.at[j], bui.at[j], xr.at[j], xi.at[j], SUBLANES, ls, False, init=(ec_r, ec_i))
            xf_re = xr[j, SUBLANES:, :]
            xf_im = xi[j, SUBLANES:, :]
            dysp = dysv.astype(BF16)
            a_cmr[j] += _dot_tn(dysp, xf_re.astype(BF16))
            a_cmi[j] -= _dot_tn(dysp, xf_im.astype(BF16))
            gr[j] = _dot_nt(dysp, cmr_ref[j])
            gi[j] = -_dot_nt(dysp, cmi_ref[j])
            _local_scan(a_re, -a_im, gr.at[j], gi.at[j], gr.at[j], gi.at[j], 0, ls, True)
            big_re = pwr_ref[tc - 1:tc, scols]
            big_im = -pwi_ref[tc - 1:tc, scols]
            f_re = fc_r[j, SUBLANES - 1:SUBLANES, :]
            f_im = fc_i[j, SUBLANES - 1:SUBLANES, :]
            for s in range(SUBLANES - 1, -1, -1):
                n_re = gr[j, s:s + 1, :] + big_re * f_re - big_im * f_im
                n_im = gi[j, s:s + 1, :] + big_re * f_im + big_im * f_re
                f_re, f_im = n_re, n_im
                if s > 0:
                    fc_r[j, s - 1:s, :] = f_re
                    fc_i[j, s - 1:s, :] = f_im
            fcv_r = fc_r[j]
            fcv_i = fc_i[j]
            q_re = pvr_ref[:, scols].reshape(ls, SUBLANES, STATE_W)
            q_im = -pvi_ref[:, scols].reshape(ls, SUBLANES, STATE_W)
            lam_re = (gr[j].reshape(ls, SUBLANES, STATE_W) + q_re * fcv_r[None] - q_im * fcv_i[None]).reshape(tc, STATE_W)
            lam_im = (gi[j].reshape(ls, SUBLANES, STATE_W) + q_re * fcv_i[None] + q_im * fcv_r[None]).reshape(tc, STATE_W)
            fc_r[j, SUBLANES - 1:SUBLANES, :] = f_re
            fc_i[j, SUBLANES - 1:SUBLANES, :] = f_im
            xp_re = xr[j, 0:tc, :]
            xp_im = xi[j, 0:tc, :]
            a_ar[j] += _acc8(lam_re * xp_re + lam_im * xp_im)
            a_ai[j] += _acc8(lam_im * xp_re - lam_re * xp_im)
            lb_re = lam_re.astype(BF16)
            lb_im = lam_im.astype(BF16)
            a_bbr[j] += _dot_tn(up, lb_re)
            a_bbi[j] += _dot_tn(up, lb_im)
            dpb[j] = _dot_nt(lb_re, bbr_ref[j]) + _dot_nt(lb_im, bbi_ref[j]) + dysv * d_ref[:, cols]
            for i in range(ls):
                nat_d[pl.ds(i, SUBLANES, stride=ls), :] = dpb[j, pl.ds(SUBLANES * i, SUBLANES), :]
            dus_ref[:, cols] = nat_d[...]

        @pl.when(c == nc - 1)
        def _():
            row_g = lax.broadcasted_iota(jnp.int32, (LANES, STATE_W), 0) // SSM_H
            col_g = lax.broadcasted_iota(jnp.int32, (LANES, STATE_W), 1) // SSM_P
            fold = (lax.broadcasted_iota(jnp.int32, (STATE_W, SSM_P), 0) % SSM_P
                    == lax.broadcasted_iota(jnp.int32, (STATE_W, SSM_P), 1)).astype(BF16)
            for j in range(nb):
                rows_j = pl.ds(j * LANES, LANES)
                for acc, out in ((a_bbr, dbbr_ref), (a_bbi, dbbi_ref), (a_cmr, dcmr_ref), (a_cmi, dcmi_ref)):
                    out[rows_j, :] = _unpermute_rhs(jnp.where(row_g == col_g, acc[j], 0.0), fold)
                dar_ref[:, pl.ds(j * STATE_W, STATE_W)] = jnp.sum(a_ar[j], axis=0, keepdims=True)
                dai_ref[:, pl.ds(j * STATE_W, STATE_W)] = jnp.sum(a_ai[j], axis=0, keepdims=True)
                dd_ref[:, pl.ds(j * LANES, LANES)] = jnp.sum(a_dd[j], axis=0, keepdims=True)

    sp = _scan_specs(tc, nb, lambda c: nc - 1 - c)
    ghp = pl.BlockSpec((nb * LANES, SSM_P), lambda b, c: (b, 0))
    ghp_shape = jax.ShapeDtypeStruct((SSM_G * SSM_H, SSM_P), F32)
    small = pltpu.VMEM((nb, SUBLANES, STATE_W), F32)
    big = pltpu.VMEM((nb, tc, STATE_W), F32)
    bigp = pltpu.VMEM((nb, tc + SUBLANES, STATE_W), F32)
    blk = pltpu.VMEM((nb, LANES, STATE_W), F32)
    tok = pltpu.VMEM((nb, tc, LANES), F32)
    return pl.pallas_call(
        kern, name="ssm_scan_bwd", grid=(LANE_BLOCKS // nb, nc),
        in_specs=[sp["us"], sp["tok"], sp["car"], sp["car"], sp["bblk"], sp["bblk"], sp["cblk"], sp["cblk"],
                  sp["vec"], sp["vec"], sp["tab"], sp["tab"], sp["tab"], sp["tab"], sp["dvec"]],
        out_specs=(sp["tok"], ghp, ghp, ghp, ghp, sp["vec"], sp["vec"], sp["dvec"]),
        out_shape=(jax.ShapeDtypeStruct((seq, D_MODEL), F32), ghp_shape, ghp_shape, ghp_shape, ghp_shape,
                   jax.ShapeDtypeStruct((1, STATE_ALL), F32), jax.ShapeDtypeStruct((1, STATE_ALL), F32),
                   jax.ShapeDtypeStruct((1, D_MODEL), F32)),
        scratch_shapes=[big, big, bigp, bigp, big, big, small, small, blk, blk, blk, blk,
                        small, small, pltpu.VMEM((nb, SUBLANES, LANES), F32), tok, tok]
        + [pltpu.VMEM((tc, LANES), F32)] * (2 * nb),
        compiler_params=_cparams("parallel", "arbitrary"),
    )(proj, dys, ec_re, ec_im, bb_re, bb_im, cm_re, cm_im, abar_re, abar_im, pw_re, pw_im, pv_re, pv_im, d_skip)


def _eye5():
    return jnp.eye(GROUPS_PER_BLOCK, dtype=F32)[None, :, None, :, None]


def _embed_b(bb_t):
    t = bb_t.transpose(1, 0, 2).reshape(LANE_BLOCKS, GROUPS_PER_BLOCK, SSM_H, 1, SSM_P)
    return (t * _eye5()).reshape(LANE_BLOCKS, LANES, STATE_W)


def _embed_c(c_ghp):
    t = c_ghp.transpose(0, 2, 1).reshape(LANE_BLOCKS, GROUPS_PER_BLOCK, SSM_P, 1, SSM_H)
    return (t * _eye5()).reshape(LANE_BLOCKS, STATE_W, LANES)


def _local_step(x, c_row, tgt, w_ada_bf, b_ada, g1, g2, w_in_bf, pool_w_bf, pscale, a_re, a_im, log_dt,
                b_re_t, b_im_t, c_re, c_im, d_skip, glu_w_bf, glu_b, wbp_bf, wbs_bf, wout_bf, ride_for_dh=None):
    seq = x.shape[0]
    tc = min(SCAN_CHUNK, seq)
    mod8, silu_c = _mod_kernel(c_row, w_ada_bf, b_ada)
    mod = mod8[0:1]
    shift, scale, gate = mod[:, 0:D_MODEL], mod[:, D_MODEL:2 * D_MODEL], mod[:, 2 * D_MODEL:]

    abar_re, abar_im, bb_re_t, bb_im_t = _ssm_params(a_re, a_im, log_dt, b_re_t, b_im_t)
    abar_re_f, abar_im_f = abar_re.reshape(1, STATE_ALL), abar_im.reshape(1, STATE_ALL)
    pw_re, pw_im, pv_re, pv_im = _pow_tables(abar_re_f, abar_im_f, tc)
    bbe_re, bbe_im = _embed_b(bb_re_t).astype(BF16), _embed_b(bb_im_t).astype(BF16)
    cme_re, cme_im = _embed_c(c_re).astype(BF16), _embed_c(c_im).astype(BF16)
    d_row = d_skip.reshape(1, D_MODEL)

    h, h_t = _in_norm(x, g1, scale, shift)
    proj = _mm([h], [w_in_bf], name="proj", bm=1024, bn=1024, bk=1024)
    ypool, ypool_t = _pool_fwd(proj, pool_w_bf, pscale)
    ys, ec_re, ec_im = _ssm_scan_fwd(proj, bbe_re, bbe_im, cme_re, cme_im, abar_re_f, abar_im_f,
                                      pw_re, pw_im, d_row, tc)
    yssm, yssm_t = _glu_fwd(ys, proj, glu_w_bf, glu_b)
    (dy, dypool, dyssm, d45, merged_t, dob, dbp, dbs, loss, dgate, dg2) = _out_fwd_bwd(
        ypool, yssm, proj, x, tgt, gate, g2, wbp_bf, wbs_bf, wout_bf)

    d_wout = _mm([merged_t], [dob], name="dw_out", bm=1024, bn=1024, bk=1024)
    d_wbp = _mm([ypool_t], [dbp], name="dw_bp", bm=1024, bn=1024, bk=1024)
    d_wbs = _mm([yssm_t], [dbs], name="dw_bs", bm=1024, bn=1024, bk=1024)
    dys, dzs, dq, yg_t, d_glu_b = _glu_bwd(ys, proj, dyssm, glu_w_bf, glu_b)
    d_glu_w = _mm([yg_t], [dq], name="dw_glu", bm=1024, bn=1024, bk=1024)
    (dus, dbbe_re, dbbe_im, dcme_re, dcme_im, d_abar_re, d_abar_im, d_dskip) = _ssm_scan_bwd(
        proj, dys, ec_re, ec_im, bbe_re, bbe_im, cme_re, cme_im, abar_re_f, abar_im_f,
        pw_re, pw_im, pv_re, pv_im, d_row, tc)
    d01, d_pool_w, d_pscale = _pool_bwd(proj, dypool, pool_w_bf, pscale)
    dparts = [d01, dus, dzs, d45]
    d_win = _mm([h_t], dparts, name="dw_in", bm=1024, bn=1024, bk=1024)
    big_grads = dict(d_win=d_win, d_glu_w=d_glu_w, d_wbp=d_wbp, d_wbs=d_wbs, d_wout=d_wout, d_pool_w=d_pool_w)
    ride = ride_for_dh(big_grads) if ride_for_dh else None
    dh = _mm(dparts, [w_in_bf], tb=True, name="dh", bm=1024, bn=1024, bk=1024, ride=ride)
    rode = ()
    if ride:
        dh, rode = dh[0], tuple(dh[1:])
    grad_x, dshift, dscale, dg1 = _in_bwd(dh, x, dy, g1, scale)
    dmod = jnp.concatenate([dshift, dscale, dgate], axis=1)
    return dict(
        rode=rode,
        loss=loss[0, 0], grad_x=grad_x, dmod=dmod, silu_c=silu_c, dg1=dg1, dg2=dg2, d_pscale=d_pscale,
        d_glu_b=d_glu_b, d_dskip=d_dskip, d_abar_re=d_abar_re, d_abar_im=d_abar_im,
        d_bb_re_t=dbbe_re.reshape(SSM_G, SSM_H, SSM_P).transpose(1, 0, 2),
        d_bb_im_t=dbbe_im.reshape(SSM_G, SSM_H, SSM_P).transpose(1, 0, 2),
        d_c_re=dcme_re.reshape(SSM_G, SSM_H, SSM_P), d_c_im=dcme_im.reshape(SSM_G, SSM_H, SSM_P),
        d_win=d_win, d_glu_w=d_glu_w, d_wbp=d_wbp, d_wbs=d_wbs, d_wout=d_wout, d_pool_w=d_pool_w)


def _position():
    x, y, c = lax.axis_index("x"), lax.axis_index("y"), lax.axis_index("c")
    chips = [(1 - x, y), (x, 1 - y), (1 - x, 1 - y)]
    return x, y, c, chips


_ANY = pl.BlockSpec(memory_space=pl.ANY)
COMM_CHUNKS = 4
COMM_ROW_ALIGN = 16


def _row_chunks(rows, k):
    assert rows % (k * COMM_ROW_ALIGN) == 0, (rows, k)
    step = rows // k
    return [(q * step, step) for q in range(k)]


def _pad_rows(buf, multiple, axis=0):
    pad = (-buf.shape[axis]) % multiple
    if not pad:
        return buf
    shape = list(buf.shape)
    shape[axis] = pad
    return jnp.concatenate([buf, jnp.zeros(shape, buf.dtype)], axis=axis)


def _ag_weights(packed):
    rows, width = packed.shape
    half = rows // 2
    chunks = _row_chunks(half, COMM_CHUNKS)
    nq = len(chunks)

    def body(p_ref, out_ref, send_sems, recv_sems):
        x, y, c, chips = _position()
        sibling = (x, y, 1 - c)

        def copy(k, chip, h, q, to, src=None):
            start, size = chunks[q]
            rows_q = pl.ds(h * half + start, size)
            dst = out_ref.at[2 * chip[0] + chip[1], rows_q, :]
            return pltpu.make_async_remote_copy(
                src_ref=dst if src is None else src.at[rows_q, :], dst_ref=dst, send_sem=send_sems.at[k * nq + q],
                recv_sem=recv_sems.at[k * nq + q], device_id=to, device_id_type=MESH_ID)

        mine = [copy(6 + h, (x, y), h, q, sibling, src=p_ref) for h in range(2) for q in range(nq)]
        first = [copy(j, (x, y), c, q, (*chip, c), src=p_ref) for q in range(nq) for j, chip in enumerate(chips)]
        for cp in first + mine:
            cp.start()
        passed = []
        for q in range(nq):
            for j, chip in enumerate(chips):
                copy(j, chip, c, q, (x, y, c)).wait_recv()
                fwd = copy(3 + j, chip, c, q, sibling)
                fwd.start()
                passed.append(fwd)
        for q in range(nq):
            for j, chip in enumerate(chips):
                copy(3 + j, chip, 1 - c, q, (x, y, c)).wait_recv()
        for cp in mine:
            cp.wait_recv()
        for cp in first + passed + mine:
            cp.wait_send()

    return pl.pallas_call(
        body, name="ag_weights", in_specs=[_ANY], out_specs=_ANY,
        out_shape=jax.ShapeDtypeStruct((N_CHIPS, rows, width), packed.dtype),
        scratch_shapes=[pltpu.SemaphoreType.DMA((8 * nq,)), pltpu.SemaphoreType.DMA((8 * nq,))],
    )(packed)


def _small_allgather_sum(buf, head_rows):
    rows, width = buf.shape
    chunks = _row_chunks(rows, COMM_CHUNKS)
    nq = len(chunks)

    def body(b_ref, head_ref, sum_ref, all_ref, send_sems, recv_sems, local_sem):
        x, y, c, chips = _position()
        me, sibling = (x, y, c), (x, y, 1 - c)

        def slot(px, py, pc):
            return all_ref.at[4 * px + 2 * py + pc]

        def copy(k, block, q, to, src=None):
            rows_q = pl.ds(chunks[q][0], chunks[q][1])
            dst = slot(*block).at[rows_q, :]
            return pltpu.make_async_remote_copy(
                src_ref=dst if src is None else src.at[rows_q, :], dst_ref=dst, send_sem=send_sems.at[k * nq + q],
                recv_sem=recv_sems.at[k * nq + q], device_id=to, device_id_type=MESH_ID)

        mine = pltpu.make_async_copy(b_ref, slot(*me), local_sem)
        mine.start()
        first = []
        for q in range(nq):
            first += [copy(1 + j, me, q, (*chip, c), src=b_ref) for j, chip in enumerate(chips)]
            first.append(copy(0, me, q, sibling, src=b_ref))
        for cp in first:
            cp.start()
        passed = []
        for q in range(nq):
            for j, chip in enumerate(chips):
                copy(1 + j, (*chip, c), q, me).wait_recv()
                fwd = copy(4 + j, (*chip, c), q, sibling)
                fwd.start()
                passed.append(fwd)
        for q in range(nq):
            copy(0, sibling, q, me).wait_recv()
            for j, chip in enumerate(chips):
                copy(4 + j, (*chip, 1 - c), q, me).wait_recv()
        for cp in first + passed:
            cp.wait_send()
        mine.wait()
        total = all_ref[0]
        for d in range(1, N_DEV):
            total = total + all_ref[d]
        sum_ref[...] = total
        head_ref[...] = all_ref[:, 0:head_rows, :]

    vm = pl.BlockSpec(memory_space=pltpu.VMEM)
    return pl.pallas_call(
        body, name="small_allgather_sum", in_specs=[vm], out_specs=(vm, vm),
        out_shape=(jax.ShapeDtypeStruct((N_DEV, head_rows, width), F32), jax.ShapeDtypeStruct((rows, width), F32)),
        scratch_shapes=[pltpu.VMEM((N_DEV, rows, width), F32), pltpu.SemaphoreType.DMA((7 * nq,)),
                        pltpu.SemaphoreType.DMA((7 * nq,)), pltpu.SemaphoreType.DMA],
        compiler_params=_cparams(),
    )(buf)


def _rs_pair(g):
    n, rows, width = g.shape
    half = rows // 2
    chunks = _row_chunks(half, COMM_CHUNKS)
    nq = len(chunks)

    def body(g_ref, got_ref, send_sems, recv_sems):
        x, y, c, _ = _position()
        swaps = []
        for k in range(n):
            for q, (start, size) in enumerate(chunks):
                swaps.append(pltpu.make_async_remote_copy(
                    src_ref=g_ref.at[k, pl.ds((1 - c) * half + start, size), :], dst_ref=got_ref.at[k, pl.ds(start, size), :],
                    send_sem=send_sems.at[k * nq + q], recv_sem=recv_sems.at[k * nq + q],
                    device_id=(x, y, 1 - c), device_id_type=MESH_ID))
        for cp in swaps:
            cp.start()
        for cp in swaps:
            cp.wait()

    return pl.pallas_call(
        body, name="rs_pair", in_specs=[_ANY], out_specs=_ANY, out_shape=jax.ShapeDtypeStruct((n, half, width), g.dtype),
        scratch_shapes=[pltpu.SemaphoreType.DMA((n * nq,)), pltpu.SemaphoreType.DMA((n * nq,))],
    )(g)


def _rs_chips_ride(part_bf):
    n, rows, width = part_bf.shape
    chunks = _row_chunks(rows, COMM_CHUNKS)
    nq = len(chunks)

    def sends(pb_ref, got_ref, send_sems, recv_sems):
        x, y, c, chips = _position()
        out = []
        for q, (start, size) in enumerate(chunks):
            for j, chip in enumerate(chips):
                out.append(pltpu.make_async_remote_copy(
                    src_ref=pb_ref.at[2 * chip[0] + chip[1], pl.ds(start, size), :], dst_ref=got_ref.at[j, pl.ds(start, size), :],
                    send_sem=send_sems.at[j * nq + q], recv_sem=recv_sems.at[j * nq + q],
                    device_id=(*chip, c), device_id_type=MESH_ID))
        return out

    def start(ins, outs, sems):
        for cp in sends(ins[0], outs[0], sems[0], sems[1]):
            cp.start()

    def wait(ins, outs, sems):
        for cp in sends(ins[0], outs[0], sems[0], sems[1]):
            cp.wait()

    return _Ride([part_bf], [jax.ShapeDtypeStruct((N_CHIPS - 1, rows, width), BF16)],
                 [pltpu.SemaphoreType.DMA((3 * nq,)), pltpu.SemaphoreType.DMA((3 * nq,))], start, wait)


def _rs_join(shard):
    rows, width = shard.shape
    half = rows // 2
    chunks = _row_chunks(half, COMM_CHUNKS)
    nq = len(chunks)

    def body(in_ref, out_ref, send_sems, recv_sems):
        x, y, c, _ = _position()
        def swap(q, h):
            rows_q = pl.ds(h * half + chunks[q][0], chunks[q][1])
            return pltpu.make_async_remote_copy(
                src_ref=in_ref.at[rows_q, :], dst_ref=out_ref.at[rows_q, :], send_sem=send_sems.at[q],
                recv_sem=recv_sems.at[q], device_id=(x, y, 1 - c), device_id_type=MESH_ID)

        for q in range(nq):
            swap(q, c).start()
        for q in range(nq):
            swap(q, 1 - c).wait_recv()
        for q in range(nq):
            swap(q, c).wait_send()

    return pl.pallas_call(
        body, name="rs_join", in_specs=[_ANY], out_specs=_ANY, input_output_aliases={0: 0},
        out_shape=jax.ShapeDtypeStruct(shard.shape, shard.dtype),
        scratch_shapes=[pltpu.SemaphoreType.DMA((nq,)), pltpu.SemaphoreType.DMA((nq,))],
    )(shard)


def _pair_add(g, got, core):
    n, half, width = got.shape
    nb = 2
    rb = half // nb

    def kern(c_ref, a_ref, b_ref, f_ref, h_ref):
        s = a_ref[...] + b_ref[...]
        f_ref[...] = s
        h_ref[...] = s.astype(BF16)

    spec = pl.BlockSpec((1, rb, width), lambda k, i, c_ref: (k, i, 0))
    return pl.pallas_call(
        kern, name="rs_pair_add",
        grid_spec=pltpu.PrefetchScalarGridSpec(
            num_scalar_prefetch=1, grid=(n, nb),
            in_specs=[pl.BlockSpec((1, rb, width), lambda k, i, c_ref: (k, c_ref[0] * nb + i, 0)), spec],
            out_specs=(spec, spec)),
        out_shape=(jax.ShapeDtypeStruct(got.shape, F32), jax.ShapeDtypeStruct(got.shape, BF16)),
        compiler_params=_cparams("parallel", "parallel"))(core, g, got)


def _chip_add(part_f32, got, where):
    _, rows, width = part_f32.shape
    nb = 2
    rb = rows // nb

    def kern(w_ref, a_ref, b_ref, o_ref):
        o_ref[...] = ((a_ref[0] + b_ref[0].astype(F32)) + b_ref[1].astype(F32)) + b_ref[2].astype(F32)

    return pl.pallas_call(
        kern, name="rs_chip_add",
        grid_spec=pltpu.PrefetchScalarGridSpec(
            num_scalar_prefetch=1, grid=(nb,),
            in_specs=[pl.BlockSpec((1, rb, width), lambda i, w_ref: (w_ref[0], i, 0)),
                      pl.BlockSpec((N_CHIPS - 1, rb, width), lambda i, w_ref: (0, i, 0))],
            out_specs=pl.BlockSpec((rb, width), lambda i, w_ref: (w_ref[1] * nb + i, 0))),
        out_shape=jax.ShapeDtypeStruct((2 * rows, width), F32),
        compiler_params=_cparams("parallel"))(where, part_f32, got)


def _adamw(w, g, m, v, name):
    rows, width = w.shape
    rb = rows
    for cand in (512, 256, 128, 64, 32, 16, 8):
        if rows % cand == 0 and cand * width * 4 <= ADAM_BLOCK_BYTES:
            rb = cand
            break
    spec = pl.BlockSpec((rb, width), lambda i: (i, 0))

    def kern(w_ref, g_ref, m_ref, v_ref, d_ref, nm_ref, nv_ref):
        gv = g_ref[...]
        nm = ADAM_B1 * m_ref[...] + (1.0 - ADAM_B1) * gv
        nv = ADAM_B2 * v_ref[...] + (1.0 - ADAM_B2) * (gv * gv)
        m_hat = nm / (1.0 - ADAM_B1 ** ADAM_STEP)
        v_hat = nv / (1.0 - ADAM_B2 ** ADAM_STEP)
        d_ref[...] = -ADAM_LR * (m_hat / (jnp.sqrt(v_hat) + ADAM_EPS) + ADAM_WD * w_ref[...])
        nm_ref[...] = nm
        nv_ref[...] = nv

    shp = jax.ShapeDtypeStruct(w.shape, F32)
    return pl.pallas_call(
        kern, name=name, grid=(rows // rb,), in_specs=[spec] * 4, out_specs=(spec, spec, spec),
        out_shape=(shp, shp, shp), compiler_params=_cparams("parallel"))(w, g, m, v)


def _wada_grad(silu_t, dmod_cols):
    n = dmod_cols.shape[1]

    def kern(s_ref, d_ref, o_ref):
        acc = s_ref[:, 0:1] * d_ref[0:1, :]
        for b in range(1, N_DEV):
            acc = acc + s_ref[:, b:b + 1] * d_ref[b:b + 1, :]
        o_ref[...] = acc

    return pl.pallas_call(kern, name="wada_grad", out_shape=jax.ShapeDtypeStruct((D_MODEL, n), F32),
                          compiler_params=_cparams())(silu_t, dmod_cols)


def _rows(a, multiple):
    flat = a.reshape(-1)
    pad = (-flat.shape[0]) % (D_MODEL * multiple)
    if pad:
        flat = jnp.concatenate([flat, jnp.zeros((pad,), flat.dtype)])
    return flat.reshape(-1, D_MODEL)


def _part_rows(shape, multiple):
    return -(-int(np.prod(shape)) // (D_MODEL * multiple)) * multiple


def _pack_rows(parts, multiple):
    return jnp.concatenate([_rows(p, multiple) for p in parts], axis=0)


def _unpack_rows(buf, shapes, multiple):
    out, r = [], 0
    for shp in shapes:
        n = int(np.prod(shp))
        nr = _part_rows(shp, multiple)
        out.append(buf[r:r + nr].reshape(-1)[:n].reshape(shp))
        r += nr
    return out


def kernel(x, c, w_ada, b_ada, norm_pre, norm_post, w_in, pool_w, pool_scale, ssm_a_re, ssm_a_im, ssm_log_dt, ssm_b_re, ssm_b_im, ssm_c_re, ssm_c_im, ssm_d, glu_w, glu_b, w_branch_pool, w_branch_ssm, w_out, loss_target, m_w_ada, m_b_ada, m_norm_pre, m_norm_post, m_w_in, m_pool_w, m_pool_scale, m_ssm_a_re, m_ssm_a_im, m_ssm_log_dt, m_ssm_b_re, m_ssm_b_im, m_ssm_c_re, m_ssm_c_im, m_ssm_d, m_glu_w, m_glu_b, m_w_branch_pool, m_w_branch_ssm, m_w_out, v_w_ada, v_b_ada, v_norm_pre, v_norm_post, v_w_in, v_pool_w, v_pool_scale, v_ssm_a_re, v_ssm_a_im, v_ssm_log_dt, v_ssm_b_re, v_ssm_b_im, v_ssm_c_re, v_ssm_c_im, v_ssm_d, v_glu_w, v_glu_b, v_w_branch_pool, v_w_branch_ssm, v_w_out):
    n_ada = w_ada.shape[2]
    n_in = w_in.shape[2]
    n_row = glu_w.shape[1]
    n_pool = pool_w.shape[2]
    n_groups = pool_w.shape[1]

    big_shards = [w_ada[0], w_in[0], pool_w[0], glu_w[0], w_branch_pool[0], w_branch_ssm[0], w_out[0]]
    packed = _pad_rows(_pack_rows([s.astype(BF16) for s in big_shards], 2 * SUBLANES), 2 * COMM_CHUNKS * COMM_ROW_ALIGN)
    gathered = _ag_weights(packed)
    r = 0
    w_ada_bf = gathered[:, r:r + n_ada].reshape(N_CHIPS, D_MODEL, n_ada).transpose(1, 0, 2).reshape(D_MODEL, 3 * D_MODEL)
    r += n_ada
    w_in_bf = gathered[:, r:r + n_in].reshape(N_CHIPS, D_MODEL, n_in).transpose(1, 0, 2).reshape(D_MODEL, N_CHIPS * n_in)
    r += n_in
    pool_rows = n_groups * n_pool * POOL_GW // D_MODEL
    pool_w_bf = gathered[:, r:r + pool_rows].reshape(N_CHIPS, n_groups, n_pool, POOL_GW).transpose(1, 0, 2, 3)
    pool_w_bf = pool_w_bf.reshape(n_groups, POOL_GW, POOL_GW)
    r += pool_rows
    squares = []
    for _ in range(4):
        squares.append(gathered[:, r:r + n_row].reshape(D_MODEL, D_MODEL))
        r += n_row
    glu_w_bf, wbp_bf, wbs_bf, wout_bf = squares

    chip = 2 * lax.axis_index("x") + lax.axis_index("y")
    core = lax.axis_index("c").astype(jnp.int32)
    kept = {}

    def by_cols(a, n):
        return a.reshape(D_MODEL, N_CHIPS, n).transpose(1, 0, 2).reshape(N_CHIPS, -1, D_MODEL)

    def by_rows(a):
        return a.reshape(N_CHIPS, n_row, D_MODEL)

    def exchange_big(g):
        pool_by_chip = g["d_pool_w"].reshape(n_groups, N_CHIPS, n_pool, POOL_GW).transpose(1, 0, 2, 3)
        g_packed = jnp.concatenate(
            [by_cols(g["d_win"], n_in), by_rows(g["d_glu_w"]), by_rows(g["d_wbp"]), by_rows(g["d_wbs"]),
             by_rows(g["d_wout"]), pool_by_chip.reshape(N_CHIPS, pool_rows, D_MODEL)], axis=1)
        g_packed = _pad_rows(g_packed, 2 * COMM_CHUNKS * COMM_ROW_ALIGN, axis=1)
        kept["part_f32"], part_bf = _pair_add(g_packed, _rs_pair(g_packed), core.reshape(1))
        return _rs_chips_ride(part_bf)

    a_re, a_im, log_dt = ssm_a_re[0], ssm_a_im[0], ssm_log_dt[0].reshape(SSM_G, 1)
    b_re_t, b_im_t = ssm_b_re[0].transpose(2, 0, 1), ssm_b_im[0].transpose(2, 0, 1)
    res = _local_step(x[0], c, loss_target[0], w_ada_bf, b_ada, norm_pre, norm_post, w_in_bf, pool_w_bf, pool_scale,
                      a_re, a_im, log_dt, b_re_t, b_im_t, ssm_c_re[0], ssm_c_im[0], ssm_d[0], glu_w_bf, glu_b[0:1],
                      wbp_bf, wbs_bf, wout_bf, ride_for_dh=exchange_big)
    loss = lax.psum(res["loss"], ("x", "y", "c"))

    small_parts = [res["dmod"], res["silu_c"], res["dg1"], res["dg2"], res["d_pscale"], res["d_glu_b"], res["d_dskip"],
                   res["d_abar_re"], res["d_abar_im"], res["d_bb_re_t"], res["d_bb_im_t"], res["d_c_re"], res["d_c_im"]]
    small_shapes = [p.shape for p in small_parts]
    head_rows = _part_rows(small_shapes[0], SUBLANES) + _part_rows(small_shapes[1], SUBLANES)
    all_small, sum_small = _small_allgather_sum(
        _pad_rows(_pack_rows(small_parts, SUBLANES), COMM_CHUNKS * COMM_ROW_ALIGN), head_rows)
    (g_b_ada, _, g_norm_pre, g_norm_post, g_pscale, g_glu_b, g_dskip, s_abar_re, s_abar_im, s_bb_re, s_bb_im,
     g_c_re, g_c_im) = _unpack_rows(sum_small, small_shapes, SUBLANES)
    g_a_re, g_a_im, g_log_dt, g_b_re_t, g_b_im_t = _ssm_params_bwd(
        a_re, a_im, log_dt, b_re_t, b_im_t, s_abar_re.reshape(SSM_G, SSM_P), s_abar_im.reshape(SSM_G, SSM_P),
        s_bb_re, s_bb_im)
    dmod_all = all_small[:, 0:3].reshape(N_DEV, 3 * D_MODEL)
    dmod_cols = lax.dynamic_slice_in_dim(dmod_all, chip * n_ada, n_ada, axis=1)
    silu_t = all_small[:, _part_rows(small_shapes[0], SUBLANES)].transpose(1, 0)
    g_w_ada = _wada_grad(silu_t, dmod_cols)

    (got_chips,) = res["rode"]
    shard = _rs_join(_chip_add(kept["part_f32"], got_chips, jnp.stack([chip.astype(jnp.int32), core])))
    r = 0
    g_w_in = shard[r:r + n_in].reshape(D_MODEL, n_in)
    r += n_in
    g_squares = []
    for _ in range(4):
        g_squares.append(shard[r:r + n_row])
        r += n_row
    g_glu_w, g_wbp, g_wbs, g_wout = g_squares
    g_pool_w = shard[r:r + pool_rows].reshape(n_groups * n_pool, POOL_GW)

    big = [("w_ada", w_ada[0], g_w_ada, m_w_ada[0], v_w_ada[0]),
           ("w_in", w_in[0], g_w_in, m_w_in[0], v_w_in[0]),
           ("pool_w", pool_w[0].reshape(n_groups * n_pool, POOL_GW), g_pool_w,
            m_pool_w[0].reshape(n_groups * n_pool, POOL_GW), v_pool_w[0].reshape(n_groups * n_pool, POOL_GW)),
           ("glu_w", glu_w[0], g_glu_w, m_glu_w[0], v_glu_w[0]),
           ("w_branch_pool", w_branch_pool[0], g_wbp, m_w_branch_pool[0], v_w_branch_pool[0]),
           ("w_branch_ssm", w_branch_ssm[0], g_wbs, m_w_branch_ssm[0], v_w_branch_ssm[0]),
           ("w_out", w_out[0], g_wout, m_w_out[0], v_w_out[0])]
    out = {}
    for name, w_, g_, m_, v_ in big:
        d_, nm_, nv_ = _adamw(w_, g_, m_, v_, "adamw_" + name)
        out[name] = (g_, d_, nm_, nv_)

    g_b_re = g_b_re_t.transpose(1, 2, 0)
    g_b_im = g_b_im_t.transpose(1, 2, 0)
    small = [("b_ada", b_ada, g_b_ada, m_b_ada, v_b_ada),
             ("norm_pre", norm_pre, g_norm_pre, m_norm_pre, v_norm_pre),
             ("norm_post", norm_post, g_norm_post, m_norm_post, v_norm_post),
             ("pool_scale", pool_scale, g_pscale, m_pool_scale, v_pool_scale),
             ("ssm_a_re", ssm_a_re, g_a_re, m_ssm_a_re, v_ssm_a_re),
             ("ssm_a_im", ssm_a_im, g_a_im, m_ssm_a_im, v_ssm_a_im),
             ("ssm_log_dt", ssm_log_dt, g_log_dt, m_ssm_log_dt, v_ssm_log_dt),
             ("ssm_b_re", ssm_b_re, g_b_re, m_ssm_b_re, v_ssm_b_re),
             ("ssm_b_im", ssm_b_im, g_b_im, m_ssm_b_im, v_ssm_b_im),
             ("ssm_c_re", ssm_c_re, g_c_re, m_ssm_c_re, v_ssm_c_re),
             ("ssm_c_im", ssm_c_im, g_c_im, m_ssm_c_im, v_ssm_c_im),
             ("ssm_d", ssm_d, g_dskip, m_ssm_d, v_ssm_d),
             ("glu_b", glu_b, g_glu_b, m_glu_b, v_glu_b)]
    shapes = [w_.shape for _, w_, _, _, _ in small]
    pw_, pg_, pm_, pv_ = (_pack_rows([t[i] for t in small], SUBLANES) for i in (1, 2, 3, 4))
    pd_, pnm_, pnv_ = _adamw(pw_, pg_, pm_, pv_, "adamw_small")
    unpacked = [_unpack_rows(p, shapes, SUBLANES) for p in (pg_, pd_, pnm_, pnv_)]
    for (name, _, _, _, _), g_, d_, nm_, nv_ in zip(small, *unpacked):
        out[name] = (g_, d_, nm_, nv_)

    order = ["w_ada", "b_ada", "norm_pre", "norm_post", "w_in", "pool_w", "pool_scale", "ssm_a_re", "ssm_a_im",
             "ssm_log_dt", "ssm_b_re", "ssm_b_im", "ssm_c_re", "ssm_c_im", "ssm_d", "glu_w", "glu_b", "w_branch_pool",
             "w_branch_ssm", "w_out"]
    ref_shape = dict(w_ada=w_ada.shape, w_in=w_in.shape, pool_w=pool_w.shape, glu_w=glu_w.shape,
                     w_branch_pool=w_branch_pool.shape, w_branch_ssm=w_branch_ssm.shape, w_out=w_out.shape)
    for name, w_, _, _, _ in small:
        ref_shape[name] = w_.shape
    results = [loss, res["grad_x"][None]]
    for k in range(4):
        results += [out[name][k].reshape(ref_shape[name]) for name in order]
    return tuple(results)
```

```python
import functools
import math

import numpy as np
import jax
import jax.numpy as jnp
from jax import lax
from jax.experimental import pallas as pl
from jax.experimental.pallas import tpu as pltpu

F32 = jnp.float32
BF16 = jnp.bfloat16
MESH_ID = pl.DeviceIdType.MESH

D_MODEL = 1024
LANES = 128
SUBLANES = 8
SSM_G, SSM_P, SSM_H = 64, 64, 16
LANE_BLOCKS = D_MODEL // LANES
GROUPS_PER_BLOCK = LANES // SSM_H
STATE_W = GROUPS_PER_BLOCK * SSM_P
STATE_ALL = SSM_G * SSM_P
POOL_WINDOWS = (2, 4, 8, 16)
POOL_GW = D_MODEL // len(POOL_WINDOWS)
HALO = 16
RMS_EPS = 1e-6
N_CHIPS = 4
N_DEV = 8

SCAN_CHUNK = 512
SCAN_BLOCKS = 2
ROW_CHUNK = 256
VMEM_LIMIT_BYTES = 56 * 1024 * 1024

ADAM_BLOCK_BYTES = 1 << 20
ADAM_LR, ADAM_B1, ADAM_B2, ADAM_EPS, ADAM_WD, ADAM_STEP = 0.001, 0.9, 0.999, 1e-08, 0.01, 10

_GELU_C0 = math.sqrt(2.0 / math.pi)
_GELU_C1 = 0.044715


def _cparams(*sem):
    if sem:
        return pltpu.CompilerParams(dimension_semantics=sem, vmem_limit_bytes=VMEM_LIMIT_BYTES)
    return pltpu.CompilerParams(vmem_limit_bytes=VMEM_LIMIT_BYTES)


def _sigmoid(v):
    return jax.nn.sigmoid(v)


def _silu(v):
    return v * _sigmoid(v)


def _dsilu(v):
    s = _sigmoid(v)
    return s * (1.0 + v * (1.0 - s))


def _gelu(v):
    return 0.5 * v * (1.0 + jnp.tanh(_GELU_C0 * (v + _GELU_C1 * v * v * v)))


def _dgelu(v):
    t = jnp.tanh(_GELU_C0 * (v + _GELU_C1 * v * v * v))
    return 0.5 * (1.0 + t) + 0.5 * v * (1.0 - t * t) * _GELU_C0 * (1.0 + 3.0 * _GELU_C1 * v * v)


def _dot(a, b):
    return lax.dot_general(a, b, (((1,), (0,)), ((), ())), preferred_element_type=F32)


def _dot_nt(a, b):
    return lax.dot_general(a, b, (((1,), (1,)), ((), ())), preferred_element_type=F32)


def _dot_tn(a, b):
    return lax.dot_general(a, b, (((0,), (0,)), ((), ())), preferred_element_type=F32)


def _acc8(v):
    return v.reshape(v.shape[0] // SUBLANES, SUBLANES, v.shape[1]).sum(axis=0)


class _Ride:
    def __init__(self, inputs, out_shapes, scratch, start, wait):
        self.inputs, self.out_shapes, self.scratch, self.start, self.wait = inputs, out_shapes, scratch, start, wait


def _mm(a_parts, b_parts, *, name, ta=False, tb=False, out_dtype=F32, bm=512, bn=512, bk=512, ride=None):
    a_parts, b_parts = list(a_parts), list(b_parts)
    if ta:
        assert len(a_parts) == 1
        k_dim, m_dim = a_parts[0].shape
    else:
        m_dim = a_parts[0].shape[0]
        k_dim = sum(a.shape[1] for a in a_parts)
    if tb:
        assert len(b_parts) == 1
        n_dim = b_parts[0].shape[0]
    else:
        n_dim = sum(b.shape[1] for b in b_parts)
    bm, bn, bk = min(bm, m_dim), min(bn, n_dim), min(bk, k_dim)
    nm, nn, nk = m_dim // bm, n_dim // bn, k_dim // bk
    a_ranges, off = [], 0
    for a in a_parts:
        cnt = (a.shape[0] if ta else a.shape[1]) // bk
        a_ranges.append((off, cnt))
        off += cnt
    b_ranges, off = [], 0
    for b in b_parts:
        cnt = (b.shape[0] if tb else b.shape[1]) // bn
        b_ranges.append((off, cnt))
        off += cnt

    def a_spec(off, cnt):
        if ta:
            return pl.BlockSpec((bk, bm), lambda i, n, k: (k, i))
        return pl.BlockSpec((bm, bk), lambda i, n, k: (i, jnp.clip(k - off, 0, cnt - 1)))

    def b_spec(off, cnt):
        if tb:
            return pl.BlockSpec((bn, bk), lambda i, n, k: (n, k))
        return pl.BlockSpec((bk, bn), lambda i, n, k: (k, jnp.clip(n - off, 0, cnt - 1)))

    na, nb = len(a_parts), len(b_parts)
    dims = (((0 if ta else 1,), (1 if tb else 0,)), ((), ()))

    def kern_single(a_ref, b_ref, o_ref):
        o_ref[...] = lax.dot_general(a_ref[...].astype(BF16), b_ref[...].astype(BF16), dims,
                                     preferred_element_type=F32).astype(out_dtype)

    if na == 1 and nb == 1 and nk == 1 and not ride:
        return pl.pallas_call(
            kern_single, name=name, grid=(nm, nn),
            in_specs=[pl.BlockSpec((bk, bm), lambda i, n: (0, i)) if ta else pl.BlockSpec((bm, bk), lambda i, n: (i, 0)),
                      pl.BlockSpec((bn, bk), lambda i, n: (n, 0)) if tb else pl.BlockSpec((bk, bn), lambda i, n: (0, n))],
            out_specs=pl.BlockSpec((bm, bn), lambda i, n: (i, n)),
            out_shape=jax.ShapeDtypeStruct((m_dim, n_dim), out_dtype),
            compiler_params=_cparams("parallel", "parallel"),
        )(a_parts[0], b_parts[0])

    n_rin = len(ride.inputs) if ride else 0
    n_rout = len(ride.out_shapes) if ride else 0

    def kern(*refs):
        a_refs, b_refs = refs[:na], refs[na:na + nb]
        rin = refs[na + nb:na + nb + n_rin]
        o_ref = refs[na + nb + n_rin]
        rout = refs[na + nb + n_rin + 1:na + nb + n_rin + 1 + n_rout]
        acc = refs[na + nb + n_rin + 1 + n_rout]
        rsem = refs[na + nb + n_rin + 2 + n_rout:]
        i, n, k = pl.program_id(0), pl.program_id(1), pl.program_id(2)

        if ride:
            @pl.when((i == 0) & (n == 0) & (k == 0))
            def _():
                ride.start(rin, rout, rsem)

        @pl.when(k == 0)
        def _():
            acc[...] = jnp.zeros_like(acc)

        for ja, (koff, kcnt) in enumerate(a_ranges):
            for jb, (noff, ncnt) in enumerate(b_ranges):
                def step(ja=ja, jb=jb):
                    a = a_refs[ja][...].astype(BF16)
                    b = b_refs[jb][...].astype(BF16)
                    acc[...] += lax.dot_general(a, b, dims, preferred_element_type=F32)

                if na == 1 and nb == 1:
                    step()
                else:
                    cond = (k >= koff) & (k < koff + kcnt) & (n >= noff) & (n < noff + ncnt)
                    pl.when(cond)(step)

        @pl.when(k == nk - 1)
        def _():
            o_ref[...] = acc[...].astype(out_dtype)

        if ride:
            @pl.when((i == nm - 1) & (n == nn - 1) & (k == nk - 1))
            def _():
                ride.wait(rin, rout, rsem)

    any_spec = pl.BlockSpec(memory_space=pl.ANY)
    out_spec = pl.BlockSpec((bm, bn), lambda i, n, k: (i, n))
    out_shape = jax.ShapeDtypeStruct((m_dim, n_dim), out_dtype)
    if not ride:
        return pl.pallas_call(
            kern, name=name, grid=(nm, nn, nk),
            in_specs=[a_spec(*r) for r in a_ranges] + [b_spec(*r) for r in b_ranges],
            out_specs=out_spec, out_shape=out_shape, scratch_shapes=[pltpu.VMEM((bm, bn), F32)],
            compiler_params=_cparams("parallel", "parallel", "arbitrary"),
        )(*a_parts, *b_parts)
    return pl.pallas_call(
        kern, name=name, grid=(nm, nn, nk),
        in_specs=[a_spec(*r) for r in a_ranges] + [b_spec(*r) for r in b_ranges] + [any_spec] * n_rin,
        out_specs=(out_spec,) + (any_spec,) * n_rout, out_shape=(out_shape,) + tuple(ride.out_shapes),
        scratch_shapes=[pltpu.VMEM((bm, bn), F32)] + list(ride.scratch),
        compiler_params=_cparams("arbitrary", "arbitrary", "arbitrary"),
    )(*a_parts, *b_parts, *ride.inputs)


def _ssm_param_fn(a_re, a_im, log_dt, b_re, b_im):
    dt = jnp.exp(log_dt)
    lam_re = jnp.minimum(a_re, -1e-4)
    lam_im = a_im
    mag = jnp.exp(lam_re * dt)
    abar_re = mag * jnp.cos(lam_im * dt)
    abar_im = mag * jnp.sin(lam_im * dt)
    den = lam_re * lam_re + lam_im * lam_im
    num_re = abar_re - 1.0
    f_re = (num_re * lam_re + abar_im * lam_im) / den
    f_im = (abar_im * lam_re - num_re * lam_im) / den
    bb_re = f_re * b_re - f_im * b_im
    bb_im = f_re * b_im + f_im * b_re
    return abar_re, abar_im, bb_re, bb_im


def _ssm_params(a_re, a_im, log_dt, b_re_t, b_im_t):
    def kern(are, aim, ldt, bre, bim, o_ar, o_ai, o_br, o_bi):
        ar, ai, br, bi = _ssm_param_fn(are[...], aim[...], ldt[...], bre[...], bim[...])
        o_ar[...] = ar
        o_ai[...] = ai
        o_br[...] = br
        o_bi[...] = bi

    gp = jax.ShapeDtypeStruct((SSM_G, SSM_P), F32)
    hgp = jax.ShapeDtypeStruct((SSM_H, SSM_G, SSM_P), F32)
    return pl.pallas_call(kern, name="ssm_params", out_shape=(gp, gp, hgp, hgp), compiler_params=_cparams())(
        a_re, a_im, log_dt, b_re_t, b_im_t)


def _ssm_params_bwd(a_re, a_im, log_dt, b_re_t, b_im_t, d_ar, d_ai, d_bbr, d_bbi):
    def kern(are, aim, ldt, bre, bim, dar, dai, dbr, dbi, o_are, o_aim, o_ldt, o_bre, o_bim):
        prim = (are[...], aim[...], ldt[...], bre[...], bim[...])
        _, vjp = jax.vjp(_ssm_param_fn, *prim)
        g = vjp((dar[...], dai[...], dbr[...], dbi[...]))
        o_are[...] = g[0]
        o_aim[...] = g[1]
        o_ldt[...] = g[2]
        o_bre[...] = g[3]
        o_bim[...] = g[4]

    gp = jax.ShapeDtypeStruct((SSM_G, SSM_P), F32)
    g1 = jax.ShapeDtypeStruct((SSM_G, 1), F32)
    hgp = jax.ShapeDtypeStruct((SSM_H, SSM_G, SSM_P), F32)
    return pl.pallas_call(kern, name="ssm_params_bwd", out_shape=(gp, gp, g1, hgp, hgp), compiler_params=_cparams())(
        a_re, a_im, log_dt, b_re_t, b_im_t, d_ar, d_ai, d_bbr, d_bbi)


def _pow_tables(abar_re, abar_im, tc):
    ls = tc // SUBLANES

    def kern(ar_ref, ai_ref, fr_ref, fi_ref, rr_ref, ri_ref):
        a_re = jnp.broadcast_to(ar_ref[...], (SUBLANES, STATE_W))
        a_im = jnp.broadcast_to(ai_ref[...], (SUBLANES, STATE_W))
        p_re, p_im = a_re, a_im
        for i in range(ls):
            fwd = pl.ds(SUBLANES * i, SUBLANES)
            rev = pl.ds(SUBLANES * (ls - 1 - i), SUBLANES)
            fr_ref[fwd, :] = p_re
            fi_ref[fwd, :] = p_im
            rr_ref[rev, :] = p_re
            ri_ref[rev, :] = p_im
            p_re, p_im = p_re * a_re - p_im * a_im, p_re * a_im + p_im * a_re

    vec = pl.BlockSpec((1, STATE_W), lambda b: (0, b))
    tab = pl.BlockSpec((tc, STATE_W), lambda b: (0, b))
    shp = jax.ShapeDtypeStruct((tc, STATE_ALL), F32)
    return pl.pallas_call(
        kern, name="pow_tables", grid=(LANE_BLOCKS,), in_specs=[vec, vec], out_specs=(tab, tab, tab, tab),
        out_shape=(shp, shp, shp, shp), compiler_params=_cparams("parallel"))(abar_re, abar_im)


def _mod_kernel(c_row, w_ada_bf, b_ada):
    def kern(c_ref, w_ref, b_ref, m_ref, s_ref):
        cv = c_ref[...]
        sc = _silu(cv)
        s_ref[...] = sc
        lhs = jnp.broadcast_to(sc, (SUBLANES, D_MODEL)).astype(BF16)
        m_ref[...] = _dot(lhs, w_ref[...]) + b_ref[...]

    return pl.pallas_call(
        kern, name="ada_mod",
        out_shape=(jax.ShapeDtypeStruct((SUBLANES, 3 * D_MODEL), F32), jax.ShapeDtypeStruct((1, D_MODEL), F32)),
        compiler_params=_cparams())(c_row, w_ada_bf, b_ada)


def _row_spec(tr, width=D_MODEL, col=0):
    return pl.BlockSpec((tr, width), lambda c: (c, col))


def _vec_spec(width=D_MODEL):
    return pl.BlockSpec((1, width), lambda c: (0, 0))


def _col_spec(tr):
    return pl.BlockSpec((D_MODEL, tr), lambda c: (0, c))


def _in_norm(x, g1, scale, shift):
    seq = x.shape[0]
    tr = min(ROW_CHUNK, seq)

    def kern(x_ref, g_ref, sc_ref, sh_ref, h_ref, ht_ref):
        xv = x_ref[...]
        r = lax.rsqrt(jnp.mean(xv * xv, axis=-1, keepdims=True) + RMS_EPS)
        h = ((xv * r) * g_ref[...]) * (1.0 + sc_ref[...]) + sh_ref[...]
        h_ref[...] = h.astype(BF16)
        ht_ref[...] = h.T.astype(BF16)

    return pl.pallas_call(
        kern, name="in_norm", grid=(seq // tr,),
        in_specs=[_row_spec(tr), _vec_spec(), _vec_spec(), _vec_spec()], out_specs=(_row_spec(tr), _col_spec(tr)),
        out_shape=(jax.ShapeDtypeStruct((seq, D_MODEL), BF16), jax.ShapeDtypeStruct((D_MODEL, seq), BF16)),
        compiler_params=_cparams("parallel"))(x, g1, scale, shift)


def _pool_windows(ext, pos, g, w, tr):
    cols = pl.ds(g * POOL_GW, POOL_GW)
    cur = ext[pl.ds(HALO, tr), cols]
    acc = cur
    for k in range(1, w):
        acc = acc + ext[pl.ds(HALO - k, tr), cols]
    cnt = jnp.minimum(pos + 1, w).astype(F32)
    return acc / cnt - cur


def _pool_fwd(proj, pool_w_bf, pscale):
    seq = proj.shape[0]
    tr = min(ROW_CHUNK, seq)
    hb = tr // HALO

    def kern(up_ref, halo_ref, zp_ref, pw_ref, ps_ref, y_ref, yt_ref, ext):
        c = pl.program_id(0)
        ext[0:HALO, :] = jnp.where(c > 0, halo_ref[...], 0.0)
        ext[HALO:, :] = up_ref[...]
        pos = c * tr + lax.broadcasted_iota(jnp.int32, (tr, POOL_GW), 0)
        for g, w in enumerate(POOL_WINDOWS):
            cols = pl.ds(g * POOL_GW, POOL_GW)
            pooled = _pool_windows(ext, pos, g, w, tr)
            mixed = _dot(pooled.astype(BF16), pw_ref[g])
            y = mixed * ps_ref[:, cols] * _silu(zp_ref[:, cols])
            y_ref[:, cols] = y.astype(BF16)
            yt_ref[cols, :] = y.T.astype(BF16)

    return pl.pallas_call(
        kern, name="pool_fwd", grid=(seq // tr,),
        in_specs=[_row_spec(tr, col=0),
                  pl.BlockSpec((HALO, D_MODEL), lambda c: (jnp.maximum(c * hb - 1, 0), 0)),
                  _row_spec(tr, col=1),
                  pl.BlockSpec((len(POOL_WINDOWS), POOL_GW, POOL_GW), lambda c: (0, 0, 0)),
                  _vec_spec()],
        out_specs=(_row_spec(tr), _col_spec(tr)),
        out_shape=(jax.ShapeDtypeStruct((seq, D_MODEL), BF16), jax.ShapeDtypeStruct((D_MODEL, seq), BF16)),
        scratch_shapes=[pltpu.VMEM((tr + HALO, D_MODEL), F32)],
        compiler_params=_cparams("parallel"))(proj, proj, proj, pool_w_bf, pscale)


def _pool_bwd(proj, dyp, pool_w_bf, pscale):
    seq = proj.shape[0]
    tr = min(ROW_CHUNK, seq)
    hb = tr // HALO
    nc = seq // tr
    n_halo = seq // HALO

    def kern(up_ref, halo_ref, zp_ref, zpn_ref, dyp_ref, dypn_ref, pw_ref, ps_ref,
             d01_ref, dpw_ref, dps_ref, ext, dpn, acc_pw, acc_ps):
        c = pl.program_id(0)

        @pl.when(c == 0)
        def _():
            acc_pw[...] = jnp.zeros_like(acc_pw)
            acc_ps[...] = jnp.zeros_like(acc_ps)

        ext[0:HALO, :] = jnp.where(c > 0, halo_ref[...], 0.0)
        ext[HALO:, :] = up_ref[...]
        pos = c * tr + lax.broadcasted_iota(jnp.int32, (tr, POOL_GW), 0)
        pos_n = (c + 1) * tr + lax.broadcasted_iota(jnp.int32, (HALO, POOL_GW), 0)
        has_next = c < nc - 1
        for g, w in enumerate(POOL_WINDOWS):
            cols = pl.ds(g * POOL_GW, POOL_GW)
            pooled_bf = _pool_windows(ext, pos, g, w, tr).astype(BF16)
            wg = pw_ref[g]
            mixed = _dot(pooled_bf, wg)
            zp = zp_ref[:, cols]
            sz = _silu(zp)
            dyp_g = dyp_ref[:, cols]
            ps = ps_ref[:, cols]
            dmixed = (dyp_g * ps * sz).astype(BF16)
            acc_ps[:, cols] += _acc8(dyp_g * mixed * sz)
            d01_ref[:, pl.ds(D_MODEL + g * POOL_GW, POOL_GW)] = (dyp_g * mixed * ps * _dsilu(zp)).astype(BF16)
            acc_pw[g] += _dot_tn(pooled_bf, dmixed)
            dpooled = _dot_nt(dmixed, wg)
            dmixed_n = (jnp.where(has_next, dypn_ref[:, cols], 0.0) * ps * _silu(zpn_ref[:, cols])).astype(BF16)
            dpooled_n = _dot_nt(dmixed_n, wg)
            dpn[0:tr, :] = dpooled / jnp.minimum(pos + 1, w).astype(F32)
            dpn[tr:, :] = dpooled_n / jnp.minimum(pos_n + 1, w).astype(F32)
            acc = dpn[0:tr, :]
            for k in range(1, w):
                acc = acc + dpn[pl.ds(k, tr), :]
            d01_ref[:, cols] = (acc - dpooled).astype(BF16)

        @pl.when(c == nc - 1)
        def _():
            dpw_ref[...] = acc_pw[...]
            dps_ref[...] = jnp.sum(acc_ps[...], axis=0, keepdims=True)

    nxt = lambda c: (jnp.minimum((c + 1) * hb, n_halo - 1), 0)
    nxt1 = lambda c: (jnp.minimum((c + 1) * hb, n_halo - 1), 1)
    return pl.pallas_call(
        kern, name="pool_bwd", grid=(nc,),
        in_specs=[_row_spec(tr, col=0),
                  pl.BlockSpec((HALO, D_MODEL), lambda c: (jnp.maximum(c * hb - 1, 0), 0)),
                  _row_spec(tr, col=1),
                  pl.BlockSpec((HALO, D_MODEL), nxt1),
                  _row_spec(tr),
                  pl.BlockSpec((HALO, D_MODEL), nxt),
                  pl.BlockSpec((len(POOL_WINDOWS), POOL_GW, POOL_GW), lambda c: (0, 0, 0)),
                  _vec_spec()],
        out_specs=(pl.BlockSpec((tr, 2 * D_MODEL), lambda c: (c, 0)),
                   pl.BlockSpec((len(POOL_WINDOWS), POOL_GW, POOL_GW), lambda c: (0, 0, 0)),
                   _vec_spec()),
        out_shape=(jax.ShapeDtypeStruct((seq, 2 * D_MODEL), BF16),
                   jax.ShapeDtypeStruct((len(POOL_WINDOWS), POOL_GW, POOL_GW), F32),
                   jax.ShapeDtypeStruct((1, D_MODEL), F32)),
        scratch_shapes=[pltpu.VMEM((tr + HALO, D_MODEL), F32), pltpu.VMEM((tr + HALO, POOL_GW), F32),
                        pltpu.VMEM((len(POOL_WINDOWS), POOL_GW, POOL_GW), F32), pltpu.VMEM((SUBLANES, D_MODEL), F32)],
        compiler_params=_cparams("arbitrary"))(proj, proj, proj, proj, dyp, dyp, pool_w_bf, pscale)


def _glu_fwd(ys, proj, glu_w_bf, glu_b):
    seq = ys.shape[0]
    tr = min(ROW_CHUNK, seq)

    def kern(ys_ref, zs_ref, w_ref, b_ref, o_ref, ot_ref):
        yg = _gelu(ys_ref[...])
        q = _dot(yg.astype(BF16), w_ref[...]) + b_ref[...]
        y = yg * _sigmoid(q) * _silu(zs_ref[...])
        o_ref[...] = y.astype(BF16)
        ot_ref[...] = y.T.astype(BF16)

    return pl.pallas_call(
        kern, name="glu_fwd", grid=(seq // tr,),
        in_specs=[_row_spec(tr), _row_spec(tr, col=3), pl.BlockSpec((D_MODEL, D_MODEL), lambda c: (0, 0)), _vec_spec()],
        out_specs=(_row_spec(tr), _col_spec(tr)),
        out_shape=(jax.ShapeDtypeStruct((seq, D_MODEL), BF16), jax.ShapeDtypeStruct((D_MODEL, seq), BF16)),
        compiler_params=_cparams("parallel"))(ys, proj, glu_w_bf, glu_b)


def _glu_bwd(ys, proj, dyssm, glu_w_bf, glu_b):
    seq = ys.shape[0]
    tr = min(ROW_CHUNK, seq)
    nc = seq // tr

    def kern(ys_ref, zs_ref, dy_ref, w_ref, b_ref, dys_ref, dzs_ref, dq_ref, yg_ref, db_ref, acc_b):
        c = pl.program_id(0)

        @pl.when(c == 0)
        def _():
            acc_b[...] = jnp.zeros_like(acc_b)

        ysv = ys_ref[...]
        yg = _gelu(ysv)
        yg_bf = yg.astype(BF16)
        q = _dot(yg_bf, w_ref[...]) + b_ref[...]
        sg = _sigmoid(q)
        zs = zs_ref[...]
        dyv = dy_ref[...]
        dyglu = dyv * _silu(zs)
        dzs_ref[...] = (dyv * (yg * sg) * _dsilu(zs)).astype(BF16)
        dq = dyglu * yg * sg * (1.0 - sg)
        dq_bf = dq.astype(BF16)
        acc_b[...] += _acc8(dq)
        dyg = dyglu * sg + _dot_nt(dq_bf, w_ref[...])
        dys_ref[...] = dyg * _dgelu(ysv)
        dq_ref[...] = dq_bf
        yg_ref[...] = yg.T.astype(BF16)

        @pl.when(c == nc - 1)
        def _():
            db_ref[...] = jnp.sum(acc_b[...], axis=0, keepdims=True)

    bf = jax.ShapeDtypeStruct((seq, D_MODEL), BF16)
    return pl.pallas_call(
        kern, name="glu_bwd", grid=(nc,),
        in_specs=[_row_spec(tr), _row_spec(tr, col=3), _row_spec(tr),
                  pl.BlockSpec((D_MODEL, D_MODEL), lambda c: (0, 0)), _vec_spec()],
        out_specs=(_row_spec(tr), _row_spec(tr), _row_spec(tr), _col_spec(tr), _vec_spec()),
        out_shape=(jax.ShapeDtypeStruct((seq, D_MODEL), F32), bf, bf, jax.ShapeDtypeStruct((D_MODEL, seq), BF16),
                   jax.ShapeDtypeStruct((1, D_MODEL), F32)),
        scratch_shapes=[pltpu.VMEM((SUBLANES, D_MODEL), F32)],
        compiler_params=_cparams("arbitrary"))(ys, proj, dyssm, glu_w_bf, glu_b)


def _out_fwd_bwd(ypool, yssm, proj, x, tgt, gate, g2, wbp_bf, wbs_bf, wout_bf):
    seq = x.shape[0]
    tr = min(ROW_CHUNK, seq)
    nc = seq // tr

    def kern(yp_ref, ysm_ref, gp_ref, gs_ref, x_ref, t_ref, gate_ref, g2_ref, wbp_ref, wbs_ref, wo_ref,
             dy_ref, dyp_ref, dys_ref, d45_ref, mb_ref, dob_ref, dbp_ref, dbs_ref, loss_ref, dgate_ref, dg2_ref,
             acc_l, acc_gate, acc_g2):
        c = pl.program_id(0)

        @pl.when(c == 0)
        def _():
            acc_l[...] = jnp.zeros_like(acc_l)
            acc_gate[...] = jnp.zeros_like(acc_gate)
            acc_g2[...] = jnp.zeros_like(acc_g2)

        bp = _dot(yp_ref[...], wbp_ref[...])
        bs = _dot(ysm_ref[...], wbs_ref[...])
        sp = _sigmoid(gp_ref[...])
        ss = _sigmoid(gs_ref[...])
        merged = sp * bp + ss * bs
        mb = merged.astype(BF16)
        out = _dot(mb, wo_ref[...])
        r2 = lax.rsqrt(jnp.mean(out * out, axis=-1, keepdims=True) + RMS_EPS)
        oh = out * r2
        gate_v, g2_v = gate_ref[...], g2_ref[...]
        ohg = oh * g2_v
        diff = (x_ref[...] + gate_v * ohg) - t_ref[...]
        acc_l[...] += _acc8(diff * diff)
        dyv = diff * (1.0 / D_MODEL)
        dy_ref[...] = dyv
        acc_gate[...] += _acc8(dyv * ohg)
        t = dyv * gate_v
        acc_g2[...] += _acc8(t * oh)
        doh = t * g2_v
        dout = r2 * (doh - oh * jnp.mean(doh * oh, axis=-1, keepdims=True))
        dob = dout.astype(BF16)
        dmerged = _dot_nt(dob, wo_ref[...])
        dbp = (dmerged * sp).astype(BF16)
        dbs = (dmerged * ss).astype(BF16)
        d45_ref[:, 0:D_MODEL] = (dmerged * bp * sp * (1.0 - sp)).astype(BF16)
        d45_ref[:, D_MODEL:] = (dmerged * bs * ss * (1.0 - ss)).astype(BF16)
        dyp_ref[...] = _dot_nt(dbp, wbp_ref[...])
        dys_ref[...] = _dot_nt(dbs, wbs_ref[...])
        mb_ref[...] = merged.T.astype(BF16)
        dob_ref[...] = dob
        dbp_ref[...] = dbp
        dbs_ref[...] = dbs

        @pl.when(c == nc - 1)
        def _():
            tot = jnp.sum(acc_l[...], axis=0, keepdims=True)
            loss_ref[...] = jnp.sum(tot, axis=1, keepdims=True) * (0.5 / D_MODEL)
            dgate_ref[...] = jnp.sum(acc_gate[...], axis=0, keepdims=True)
            dg2_ref[...] = jnp.sum(acc_g2[...], axis=0, keepdims=True)

    wspec = pl.BlockSpec((D_MODEL, D_MODEL), lambda c: (0, 0))
    f32 = jax.ShapeDtypeStruct((seq, D_MODEL), F32)
    bf = jax.ShapeDtypeStruct((seq, D_MODEL), BF16)
    vec = jax.ShapeDtypeStruct((1, D_MODEL), F32)
    acc = pltpu.VMEM((SUBLANES, D_MODEL), F32)
    return pl.pallas_call(
        kern, name="out_fwd_bwd", grid=(nc,),
        in_specs=[_row_spec(tr), _row_spec(tr), _row_spec(tr, col=4), _row_spec(tr, col=5), _row_spec(tr), _row_spec(tr),
                  _vec_spec(), _vec_spec(), wspec, wspec, wspec],
        out_specs=(_row_spec(tr), _row_spec(tr), _row_spec(tr), pl.BlockSpec((tr, 2 * D_MODEL), lambda c: (c, 0)),
                   _col_spec(tr), _row_spec(tr), _row_spec(tr), _row_spec(tr),
                   pl.BlockSpec((1, 1), lambda c: (0, 0)), _vec_spec(), _vec_spec()),
        out_shape=(f32, f32, f32, jax.ShapeDtypeStruct((seq, 2 * D_MODEL), BF16),
                   jax.ShapeDtypeStruct((D_MODEL, seq), BF16), bf, bf, bf,
                   jax.ShapeDtypeStruct((1, 1), F32), vec, vec),
        scratch_shapes=[acc, acc, acc],
        compiler_params=_cparams("arbitrary"))(ypool, yssm, proj, proj, x, tgt, gate, g2, wbp_bf, wbs_bf, wout_bf)


def _in_bwd(dh, x, dy, g1, scale):
    seq = x.shape[0]
    tr = min(ROW_CHUNK, seq)
    nc = seq // tr

    def kern(dh_ref, x_ref, dy_ref, g_ref, sc_ref, dx_ref, dsh_ref, dsc_ref, dg_ref, a_sh, a_sc, a_g):
        c = pl.program_id(0)

        @pl.when(c == 0)
        def _():
            a_sh[...] = jnp.zeros_like(a_sh)
            a_sc[...] = jnp.zeros_like(a_sc)
            a_g[...] = jnp.zeros_like(a_g)

        xv = x_ref[...]
        r = lax.rsqrt(jnp.mean(xv * xv, axis=-1, keepdims=True) + RMS_EPS)
        xh = xv * r
        g = g_ref[...]
        dhv = dh_ref[...]
        a_sh[...] += _acc8(dhv)
        a_sc[...] += _acc8(dhv * (xh * g))
        dn = dhv * (1.0 + sc_ref[...])
        a_g[...] += _acc8(dn * xh)
        dxh = dn * g
        dx_ref[...] = dy_ref[...] + r * (dxh - xh * jnp.mean(dxh * xh, axis=-1, keepdims=True))

        @pl.when(c == nc - 1)
        def _():
            dsh_ref[...] = jnp.sum(a_sh[...], axis=0, keepdims=True)
            dsc_ref[...] = jnp.sum(a_sc[...], axis=0, keepdims=True)
            dg_ref[...] = jnp.sum(a_g[...], axis=0, keepdims=True)

    vec = jax.ShapeDtypeStruct((1, D_MODEL), F32)
    acc = pltpu.VMEM((SUBLANES, D_MODEL), F32)
    return pl.pallas_call(
        kern, name="in_bwd", grid=(nc,),
        in_specs=[_row_spec(tr), _row_spec(tr), _row_spec(tr), _vec_spec(), _vec_spec()],
        out_specs=(_row_spec(tr), _vec_spec(), _vec_spec(), _vec_spec()),
        out_shape=(jax.ShapeDtypeStruct((seq, D_MODEL), F32), vec, vec, vec),
        scratch_shapes=[acc, acc, acc],
        compiler_params=_cparams("arbitrary"))(dh, x, dy, g1, scale)


def _local_scan(a_re, a_im, br, bi, xr, xi, row0, ls, reverse, init=None):
    if init is None:
        x_re = jnp.zeros((SUBLANES, STATE_W), F32)
        x_im = jnp.zeros((SUBLANES, STATE_W), F32)
    else:
        x_re, x_im = init
    for i in (range(ls - 1, -1, -1) if reverse else range(ls)):
        src = pl.ds(SUBLANES * i, SUBLANES)
        dst = pl.ds(row0 + SUBLANES * i, SUBLANES)
        n_re = a_re * x_re - a_im * x_im + br[src, :]
        n_im = a_re * x_im + a_im * x_re + bi[src, :]
        x_re, x_im = n_re, n_im
        xr[dst, :] = x_re
        xi[dst, :] = x_im
    return x_re, x_im


def _unpermute_rhs(v, sel):
    hi = v.astype(BF16)
    r1 = v - hi.astype(F32)
    mid = r1.astype(BF16)
    lo = (r1 - mid.astype(F32)).astype(BF16)
    return _dot(hi, sel) + _dot(mid, sel) + _dot(lo, sel)


def _scan_specs(tc, nb, rows_of):
    return dict(
        us=pl.BlockSpec((tc, nb * LANES), lambda b, c: (rows_of(c), 2 * D_MODEL // (nb * LANES) + b)),
        tok=pl.BlockSpec((tc, nb * LANES), lambda b, c: (rows_of(c), b)),
        bblk=pl.BlockSpec((nb, LANES, STATE_W), lambda b, c: (b, 0, 0)),
        cblk=pl.BlockSpec((nb, STATE_W, LANES), lambda b, c: (b, 0, 0)),
        vec=pl.BlockSpec((1, nb * STATE_W), lambda b, c: (0, b)),
        tab=pl.BlockSpec((tc, nb * STATE_W), lambda b, c: (0, b)),
        car=pl.BlockSpec((SUBLANES, nb * STATE_W), lambda b, c: (rows_of(c), b)),
        dvec=pl.BlockSpec((1, nb * LANES), lambda b, c: (0, b)))


def _ssm_scan_fwd(proj, bb_re, bb_im, cm_re, cm_im, abar_re, abar_im, pw_re, pw_im, d_skip, tc):
    seq = proj.shape[0]
    nc = seq // tc
    ls = tc // SUBLANES
    nb = SCAN_BLOCKS

    def kern(us_ref, bbr_ref, bbi_ref, cmr_ref, cmi_ref, ar_ref, ai_ref, pwr_ref, pwi_ref, d_ref,
             ys_ref, ecr_ref, eci_ref, bur, bui, car_r, car_i, end_r, end_i, upb, *nat):
        c = pl.program_id(1)

        @pl.when(c == 0)
        def _():
            car_r[...] = jnp.zeros_like(car_r)
            car_i[...] = jnp.zeros_like(car_i)

        for j in range(nb):
            cols = pl.ds(j * LANES, LANES)
            scols = pl.ds(j * STATE_W, STATE_W)
            nat[j][...] = us_ref[:, cols]
            for i in range(ls):
                upb[j, pl.ds(SUBLANES * i, SUBLANES), :] = nat[j][pl.ds(i, SUBLANES, stride=ls), :]
            u = upb[j]
            up = u.astype(BF16)
            bur[j] = _dot(up, bbr_ref[j])
            bui[j] = _dot(up, bbi_ref[j])
            a_re = jnp.broadcast_to(ar_ref[:, scols], (SUBLANES, STATE_W))
            a_im = jnp.broadcast_to(ai_ref[:, scols], (SUBLANES, STATE_W))
            x_re, x_im = _local_scan(a_re, a_im, bur.at[j], bui.at[j], bur.at[j], bui.at[j], 0, ls, False)
            end_r[j] = x_re
            end_i[j] = x_im
            big_re = pwr_ref[tc - 1:tc, scols]
            big_im = pwi_ref[tc - 1:tc, scols]
            e_re = car_r[j, 0:1, :]
            e_im = car_i[j, 0:1, :]
            for s in range(SUBLANES):
                n_re = end_r[j, s:s + 1, :] + big_re * e_re - big_im * e_im
                n_im = end_i[j, s:s + 1, :] + big_re * e_im + big_im * e_re
                e_re, e_im = n_re, n_im
                if s < SUBLANES - 1:
                    car_r[j, s + 1:s + 2, :] = e_re
                    car_i[j, s + 1:s + 2, :] = e_im
            ec_re = car_r[j]
            ec_im = car_i[j]
            ecr_ref[:, scols] = ec_re
            eci_ref[:, scols] = ec_im
            p_re = pwr_ref[:, scols].reshape(ls, SUBLANES, STATE_W)
            p_im = pwi_ref[:, scols].reshape(ls, SUBLANES, STATE_W)
            xf_re = bur[j].reshape(ls, SUBLANES, STATE_W) + p_re * ec_re[None] - p_im * ec_im[None]
            xf_im = bui[j].reshape(ls, SUBLANES, STATE_W) + p_re * ec_im[None] + p_im * ec_re[None]
            xb_re = xf_re.reshape(tc, STATE_W).astype(BF16)
            xb_im = xf_im.reshape(tc, STATE_W).astype(BF16)
            upb[j] = _dot(xb_re, cmr_ref[j]) - _dot(xb_im, cmi_ref[j]) + d_ref[:, cols] * u
            for i in range(ls):
                nat[j][pl.ds(i, SUBLANES, stride=ls), :] = upb[j, pl.ds(SUBLANES * i, SUBLANES), :]
            ys_ref[:, cols] = nat[j][...]
            car_r[j, 0:1, :] = e_re
            car_i[j, 0:1, :] = e_im

    sp = _scan_specs(tc, nb, lambda c: c)
    carry_shape = jax.ShapeDtypeStruct((nc * SUBLANES, STATE_ALL), F32)
    small = pltpu.VMEM((nb, SUBLANES, STATE_W), F32)
    big = pltpu.VMEM((nb, tc, STATE_W), F32)
    return pl.pallas_call(
        kern, name="ssm_scan_fwd", grid=(LANE_BLOCKS // nb, nc),
        in_specs=[sp["us"], sp["bblk"], sp["bblk"], sp["cblk"], sp["cblk"], sp["vec"], sp["vec"], sp["tab"], sp["tab"],
                  sp["dvec"]],
        out_specs=(sp["tok"], sp["car"], sp["car"]),
        out_shape=(jax.ShapeDtypeStruct((seq, D_MODEL), F32), carry_shape, carry_shape),
        scratch_shapes=[big, big, small, small, small, small, pltpu.VMEM((nb, tc, LANES), F32)]
        + [pltpu.VMEM((tc, LANES), F32)] * nb,
        compiler_params=_cparams("parallel", "arbitrary"),
    )(proj, bb_re, bb_im, cm_re, cm_im, abar_re, abar_im, pw_re, pw_im, d_skip)


def _ssm_scan_bwd(proj, dys, ec_re, ec_im, bb_re, bb_im, cm_re, cm_im, abar_re, abar_im,
                  pw_re, pw_im, pv_re, pv_im, d_skip, tc):
    seq = proj.shape[0]
    nc = seq // tc
    ls = tc // SUBLANES
    nb = SCAN_BLOCKS

    def kern(us_ref, dys_ref, ecr_ref, eci_ref, bbr_ref, bbi_ref, cmr_ref, cmi_ref, ar_ref, ai_ref,
             pwr_ref, pwi_ref, pvr_ref, pvi_ref, d_ref,
             dus_ref, dbbr_ref, dbbi_ref, dcmr_ref, dcmi_ref, dar_ref, dai_ref, dd_ref,
             bur, bui, xr, xi, gr, gi, fc_r, fc_i, a_bbr, a_bbi, a_cmr, a_cmi, a_ar, a_ai, a_dd, upb, dpb, *nat):
        c = pl.program_id(1)

        @pl.when(c == 0)
        def _():
            for ref in (fc_r, fc_i, a_bbr, a_bbi, a_cmr, a_cmi, a_ar, a_ai, a_dd):
                ref[...] = jnp.zeros_like(ref)

        for j in range(nb):
            cols = pl.ds(j * LANES, LANES)
            scols = pl.ds(j * STATE_W, STATE_W)
            nat_u, nat_d = nat[2 * j], nat[2 * j + 1]
            nat_u[...] = us_ref[:, cols]
            nat_d[...] = dys_ref[:, cols]
            for i in range(ls):
                rows_i = pl.ds(SUBLANES * i, SUBLANES)
                upb[j, rows_i, :] = nat_u[pl.ds(i, SUBLANES, stride=ls), :]
                dpb[j, rows_i, :] = nat_d[pl.ds(i, SUBLANES, stride=ls), :]
            u = upb[j]
            dysv = dpb[j]
            a_dd[j] += _acc8(dysv * u)
            up = u.astype(BF16)
            bur[j] = _dot(up, bbr_ref[j])
            bui[j] = _dot(up, bbi_ref[j])
            a_re = jnp.broadcast_to(ar_ref[:, scols], (SUBLANES, STATE_W))
            a_im = jnp.broadcast_to(ai_ref[:, scols], (SUBLANES, STATE_W))
            ec_r = ecr_ref[:, scols]
            ec_i = eci_ref[:, scols]
            xr[j, 0:SUBLANES, :] = ec_r
            xi[j, 0:SUBLANES, :] = ec_i
            _local_scan(a_re, a_im, bur.at[j], bui.at[j], xr.at[j], xi.at[j], SUBLANES, ls, False, init=(ec_r, ec_i))
            xf_re = xr[j, SUBLANES:, :]
            xf_im = xi[j, SUBLANES:, :]
            dysp = dysv.astype(BF16)
            a_cmr[j] += _dot_tn(dysp, xf_re.astype(BF16))
            a_cmi[j] -= _dot_tn(dysp, xf_im.astype(BF16))
            gr[j] = _dot_nt(dysp, cmr_ref[j])
            gi[j] = -_dot_nt(dysp, cmi_ref[j])
            _local_scan(a_re, -a_im, gr.at[j], gi.at[j], gr.at[j], gi.at[j], 0, ls, True)
            big_re = pwr_ref[tc - 1:tc, scols]
            big_im = -pwi_ref[tc - 1:tc, scols]
            f_re = fc_r[j, SUBLANES - 1:SUBLANES, :]
            f_im = fc_i[j, SUBLANES - 1:SUBLANES, :]
            for s in range(SUBLANES - 1, -1, -1):
                n_re = gr[j, s:s + 1, :] + big_re * f_re - big_im * f_im
                n_im = gi[j, s:s + 1, :] + big_re * f_im + big_im * f_re
                f_re, f_im = n_re, n_im
                if s > 0:
                    fc_r[j, s - 1:s, :] = f_re
                    fc_i[j, s - 1:s, :] = f_im
            fcv_r = fc_r[j]
            fcv_i = fc_i[j]
            q_re = pvr_ref[:, scols].reshape(ls, SUBLANES, STATE_W)
            q_im = -pvi_ref[:, scols].reshape(ls, SUBLANES, STATE_W)
            lam_re = (gr[j].reshape(ls, SUBLANES, STATE_W) + q_re * fcv_r[None] - q_im * fcv_i[None]).reshape(tc, STATE_W)
            lam_im = (gi[j].reshape(ls, SUBLANES, STATE_W) + q_re * fcv_i[None] + q_im * fcv_r[None]).reshape(tc, STATE_W)
            fc_r[j, SUBLANES - 1:SUBLANES, :] = f_re
            fc_i[j, SUBLANES - 1:SUBLANES, :] = f_im
            xp_re = xr[j, 0:tc, :]
            xp_im = xi[j, 0:tc, :]
            a_ar[j] += _acc8(lam_re * xp_re + lam_im * xp_im)
            a_ai[j] += _acc8(lam_im * xp_re - lam_re * xp_im)
            lb_re = lam_re.astype(BF16)
            lb_im = lam_im.astype(BF16)
            a_bbr[j] += _dot_tn(up, lb_re)
            a_bbi[j] += _dot_tn(up, lb_im)
            dpb[j] = _dot_nt(lb_re, bbr_ref[j]) + _dot_nt(lb_im, bbi_ref[j]) + dysv * d_ref[:, cols]
            for i in range(ls):
                nat_d[pl.ds(i, SUBLANES, stride=ls), :] = dpb[j, pl.ds(SUBLANES * i, SUBLANES), :]
            dus_ref[:, cols] = nat_d[...]

        @pl.when(c == nc - 1)
        def _():
            row_g = lax.broadcasted_iota(jnp.int32, (LANES, STATE_W), 0) // SSM_H
            col_g = lax.broadcasted_iota(jnp.int32, (LANES, STATE_W), 1) // SSM_P
            fold = (lax.broadcasted_iota(jnp.int32, (STATE_W, SSM_P), 0) % SSM_P
                    == lax.broadcasted_iota(jnp.int32, (STATE_W, SSM_P), 1)).astype(BF16)
            for j in range(nb):
                rows_j = pl.ds(j * LANES, LANES)
                for acc, out in ((a_bbr, dbbr_ref), (a_bbi, dbbi_ref), (a_cmr, dcmr_ref), (a_cmi, dcmi_ref)):
                    out[rows_j, :] = _unpermute_rhs(jnp.where(row_g == col_g, acc[j], 0.0), fold)
                dar_ref[:, pl.ds(j * STATE_W, STATE_W)] = jnp.sum(a_ar[j], axis=0, keepdims=True)
                dai_ref[:, pl.ds(j * STATE_W, STATE_W)] = jnp.sum(a_ai[j], axis=0, keepdims=True)
                dd_ref[:, pl.ds(j * LANES, LANES)] = jnp.sum(a_dd[j], axis=0, keepdims=True)

    sp = _scan_specs(tc, nb, lambda c: nc - 1 - c)
    ghp = pl.BlockSpec((nb * LANES, SSM_P), lambda b, c: (b, 0))
    ghp_shape = jax.ShapeDtypeStruct((SSM_G * SSM_H, SSM_P), F32)
    small = pltpu.VMEM((nb, SUBLANES, STATE_W), F32)
    big = pltpu.VMEM((nb, tc, STATE_W), F32)
    bigp = pltpu.VMEM((nb, tc + SUBLANES, STATE_W), F32)
    blk = pltpu.VMEM((nb, LANES, STATE_W), F32)
    tok = pltpu.VMEM((nb, tc, LANES), F32)
    return pl.pallas_call(
        kern, name="ssm_scan_bwd", grid=(LANE_BLOCKS // nb, nc),
        in_specs=[sp["us"], sp["tok"], sp["car"], sp["car"], sp["bblk"], sp["bblk"], sp["cblk"], sp["cblk"],
                  sp["vec"], sp["vec"], sp["tab"], sp["tab"], sp["tab"], sp["tab"], sp["dvec"]],
        out_specs=(sp["tok"], ghp, ghp, ghp, ghp, sp["vec"], sp["vec"], sp["dvec"]),
        out_shape=(jax.ShapeDtypeStruct((seq, D_MODEL), F32), ghp_shape, ghp_shape, ghp_shape, ghp_shape,
                   jax.ShapeDtypeStruct((1, STATE_ALL), F32), jax.ShapeDtypeStruct((1, STATE_ALL), F32),
                   jax.ShapeDtypeStruct((1, D_MODEL), F32)),
        scratch_shapes=[big, big, bigp, bigp, big, big, small, small, blk, blk, blk, blk,
                        small, small, pltpu.VMEM((nb, SUBLANES, LANES), F32), tok, tok]
        + [pltpu.VMEM((tc, LANES), F32)] * (2 * nb),
        compiler_params=_cparams("parallel", "arbitrary"),
    )(proj, dys, ec_re, ec_im, bb_re, bb_im, cm_re, cm_im, abar_re, abar_im, pw_re, pw_im, pv_re, pv_im, d_skip)


def _eye5():
    return jnp.eye(GROUPS_PER_BLOCK, dtype=F32)[None, :, None, :, None]


def _embed_b(bb_t):
    t = bb_t.transpose(1, 0, 2).reshape(LANE_BLOCKS, GROUPS_PER_BLOCK, SSM_H, 1, SSM_P)
    return (t * _eye5()).reshape(LANE_BLOCKS, LANES, STATE_W)


def _embed_c(c_ghp):
    t = c_ghp.transpose(0, 2, 1).reshape(LANE_BLOCKS, GROUPS_PER_BLOCK, SSM_P, 1, SSM_H)
    return (t * _eye5()).reshape(LANE_BLOCKS, STATE_W, LANES)


def _local_step(x, c_row, tgt, w_ada_bf, b_ada, g1, g2, w_in_bf, pool_w_bf, pscale, a_re, a_im, log_dt,
                b_re_t, b_im_t, c_re, c_im, d_skip, glu_w_bf, glu_b, wbp_bf, wbs_bf, wout_bf,
                late_weights=None, ride_for_dw_in=None, ride_for_dh=None):
    seq = x.shape[0]
    tc = min(SCAN_CHUNK, seq)
    mod8, silu_c = _mod_kernel(c_row, w_ada_bf, b_ada)
    mod = mod8[0:1]
    shift, scale, gate = mod[:, 0:D_MODEL], mod[:, D_MODEL:2 * D_MODEL], mod[:, 2 * D_MODEL:]

    abar_re, abar_im, bb_re_t, bb_im_t = _ssm_params(a_re, a_im, log_dt, b_re_t, b_im_t)
    abar_re_f, abar_im_f = abar_re.reshape(1, STATE_ALL), abar_im.reshape(1, STATE_ALL)
    pw_re, pw_im, pv_re, pv_im = _pow_tables(abar_re_f, abar_im_f, tc)
    bbe_re, bbe_im = _embed_b(bb_re_t).astype(BF16), _embed_b(bb_im_t).astype(BF16)
    cme_re, cme_im = _embed_c(c_re).astype(BF16), _embed_c(c_im).astype(BF16)
    d_row = d_skip.reshape(1, D_MODEL)

    h, h_t = _in_norm(x, g1, scale, shift)
    if late_weights:
        proj, gathered = _mm([h], [w_in_bf], name="proj", bm=1024, bn=1024, bk=1024, ride=late_weights[0])
        pool_w_bf, glu_w_bf, wbp_bf, wbs_bf, wout_bf = late_weights[1](gathered)
    else:
        proj = _mm([h], [w_in_bf], name="proj", bm=1024, bn=1024, bk=1024)
    ypool, ypool_t = _pool_fwd(proj, pool_w_bf, pscale)
    ys, ec_re, ec_im = _ssm_scan_fwd(proj, bbe_re, bbe_im, cme_re, cme_im, abar_re_f, abar_im_f,
                                      pw_re, pw_im, d_row, tc)
    yssm, yssm_t = _glu_fwd(ys, proj, glu_w_bf, glu_b)
    (dy, dypool, dyssm, d45, merged_t, dob, dbp, dbs, loss, dgate, dg2) = _out_fwd_bwd(
        ypool, yssm, proj, x, tgt, gate, g2, wbp_bf, wbs_bf, wout_bf)

    d_wout = _mm([merged_t], [dob], name="dw_out", bm=1024, bn=1024, bk=1024)
    d_wbp = _mm([ypool_t], [dbp], name="dw_bp", bm=1024, bn=1024, bk=1024)
    d_wbs = _mm([yssm_t], [dbs], name="dw_bs", bm=1024, bn=1024, bk=1024)
    dys, dzs, dq, yg_t, d_glu_b = _glu_bwd(ys, proj, dyssm, glu_w_bf, glu_b)
    d_glu_w = _mm([yg_t], [dq], name="dw_glu", bm=1024, bn=1024, bk=1024)
    (dus, dbbe_re, dbbe_im, dcme_re, dcme_im, d_abar_re, d_abar_im, d_dskip) = _ssm_scan_bwd(
        proj, dys, ec_re, ec_im, bbe_re, bbe_im, cme_re, cme_im, abar_re_f, abar_im_f,
        pw_re, pw_im, pv_re, pv_im, d_row, tc)
    d01, d_pool_w, d_pscale = _pool_bwd(proj, dypool, pool_w_bf, pscale)
    dparts = [d01, dus, dzs, d45]
    small_ready = dict(
        dg2=dg2, d_pscale=d_pscale, d_glu_b=d_glu_b, d_dskip=d_dskip, d_abar_re=d_abar_re, d_abar_im=d_abar_im,
        d_bb_re_t=dbbe_re.reshape(SSM_G, SSM_H, SSM_P).transpose(1, 0, 2),
        d_bb_im_t=dbbe_im.reshape(SSM_G, SSM_H, SSM_P).transpose(1, 0, 2),
        d_c_re=dcme_re.reshape(SSM_G, SSM_H, SSM_P), d_c_im=dcme_im.reshape(SSM_G, SSM_H, SSM_P))
    ride = ride_for_dw_in(small_ready) if ride_for_dw_in else None
    d_win = _mm([h_t], dparts, name="dw_in", bm=1024, bn=1024, bk=1024, ride=ride)
    rode_dw_in = ()
    if ride:
        d_win, rode_dw_in = d_win[0], tuple(d_win[1:])
    big_grads = dict(d_win=d_win, d_glu_w=d_glu_w, d_wbp=d_wbp, d_wbs=d_wbs, d_wout=d_wout, d_pool_w=d_pool_w)
    ride = ride_for_dh(big_grads) if ride_for_dh else None
    dh = _mm(dparts, [w_in_bf], tb=True, name="dh", bm=1024, bn=1024, bk=1024, ride=ride)
    rode = ()
    if ride:
        dh, rode = dh[0], tuple(dh[1:])
    grad_x, dshift, dscale, dg1 = _in_bwd(dh, x, dy, g1, scale)
    dmod = jnp.concatenate([dshift, dscale, dgate], axis=1)
    return dict(
        rode=rode, rode_dw_in=rode_dw_in, loss=loss[0, 0], grad_x=grad_x, dmod=dmod, silu_c=silu_c, dg1=dg1,
        **small_ready, **big_grads)


def _position():
    x, y, c = lax.axis_index("x"), lax.axis_index("y"), lax.axis_index("c")
    chips = [(1 - x, y), (x, 1 - y), (1 - x, 1 - y)]
    return x, y, c, chips


_ANY = pl.BlockSpec(memory_space=pl.ANY)
COMM_CHUNKS = 4
COMM_ROW_ALIGN = 16


def _row_chunks(rows, k):
    assert rows % (k * COMM_ROW_ALIGN) == 0, (rows, k)
    step = rows // k
    return [(q * step, step) for q in range(k)]


def _pad_rows(buf, multiple, axis=0):
    pad = (-buf.shape[axis]) % multiple
    if not pad:
        return buf
    shape = list(buf.shape)
    shape[axis] = pad
    return jnp.concatenate([buf, jnp.zeros(shape, buf.dtype)], axis=axis)


def _ag_weights_ride(packed):
    rows, width = packed.shape
    half = rows // 2
    chunks = _row_chunks(half, COMM_CHUNKS)
    nq = len(chunks)

    def parts(p_ref, out_ref, send_sems, recv_sems):
        x, y, c, chips = _position()
        sibling = (x, y, 1 - c)

        def copy(k, chip, h, q, to, src=None):
            start, size = chunks[q]
            rows_q = pl.ds(h * half + start, size)
            dst = out_ref.at[2 * chip[0] + chip[1], rows_q, :]
            return pltpu.make_async_remote_copy(
                src_ref=dst if src is None else src.at[rows_q, :], dst_ref=dst, send_sem=send_sems.at[k * nq + q],
                recv_sem=recv_sems.at[k * nq + q], device_id=to, device_id_type=MESH_ID)

        mine = [copy(6 + h, (x, y), h, q, sibling, src=p_ref) for h in range(2) for q in range(nq)]
        first = [copy(j, (x, y), c, q, (*chip, c), src=p_ref) for q in range(nq) for j, chip in enumerate(chips)]
        return (x, y, c), chips, sibling, copy, mine, first

    def start(ins, outs, sems):
        _, _, _, _, mine, first = parts(ins[0], outs[0], sems[0], sems[1])
        for cp in first + mine:
            cp.start()

    def wait(ins, outs, sems):
        (x, y, c), chips, sibling, copy, mine, first = parts(ins[0], outs[0], sems[0], sems[1])
        passed = []
        for q in range(nq):
            for j, chip in enumerate(chips):
                copy(j, chip, c, q, (x, y, c)).wait_recv()
                fwd = copy(3 + j, chip, c, q, sibling)
                fwd.start()
                passed.append(fwd)
        for q in range(nq):
            for j, chip in enumerate(chips):
                copy(3 + j, chip, 1 - c, q, (x, y, c)).wait_recv()
        for cp in mine:
            cp.wait_recv()
        for cp in first + passed + mine:
            cp.wait_send()

    return _Ride([packed], [jax.ShapeDtypeStruct((N_CHIPS, rows, width), packed.dtype)],
                 [pltpu.SemaphoreType.DMA((8 * nq,)), pltpu.SemaphoreType.DMA((8 * nq,))], start, wait)


def _run_ride(ride, name):
    n_in, n_out = len(ride.inputs), len(ride.out_shapes)

    def body(*refs):
        ins, outs, sems = refs[:n_in], refs[n_in:n_in + n_out], refs[n_in + n_out:]
        ride.start(ins, outs, sems)
        ride.wait(ins, outs, sems)

    return pl.pallas_call(
        body, name=name, in_specs=[_ANY] * n_in, out_specs=(_ANY,) * n_out, out_shape=tuple(ride.out_shapes),
        scratch_shapes=list(ride.scratch))(*ride.inputs)


def _small_allgather_ride(buf):
    rows, width = buf.shape
    chunks = _row_chunks(rows, COMM_CHUNKS)
    nq = len(chunks)

    def parts(b_ref, all_ref, send_sems, recv_sems, local_sem):
        x, y, c, chips = _position()
        me, sibling = (x, y, c), (x, y, 1 - c)

        def copy(k, block, q, to, src=None):
            rows_q = pl.ds(chunks[q][0], chunks[q][1])
            dst = all_ref.at[4 * block[0] + 2 * block[1] + block[2], rows_q, :]
            return pltpu.make_async_remote_copy(
                src_ref=dst if src is None else src.at[rows_q, :], dst_ref=dst, send_sem=send_sems.at[k * nq + q],
                recv_sem=recv_sems.at[k * nq + q], device_id=to, device_id_type=MESH_ID)

        mine = pltpu.make_async_copy(b_ref, all_ref.at[4 * x + 2 * y + c], local_sem)
        first = []
        for q in range(nq):
            first += [copy(1 + j, me, q, (*chip, c), src=b_ref) for j, chip in enumerate(chips)]
            first.append(copy(0, me, q, sibling, src=b_ref))
        return me, sibling, c, chips, copy, mine, first

    def start(ins, outs, sems):
        _, _, _, _, _, mine, first = parts(ins[0], outs[0], *sems)
        mine.start()
        for cp in first:
            cp.start()

    def wait(ins, outs, sems):
        me, sibling, c, chips, copy, mine, first = parts(ins[0], outs[0], *sems)
        passed = []
        for q in range(nq):
            for j, chip in enumerate(chips):
                copy(1 + j, (*chip, c), q, me).wait_recv()
                fwd = copy(4 + j, (*chip, c), q, sibling)
                fwd.start()
                passed.append(fwd)
        for q in range(nq):
            copy(0, sibling, q, me).wait_recv()
            for j, chip in enumerate(chips):
                copy(4 + j, (*chip, 1 - c), q, me).wait_recv()
        for cp in first + passed:
            cp.wait_send()
        mine.wait()

    return _Ride([buf], [jax.ShapeDtypeStruct((N_DEV, rows, width), F32)],
                 [pltpu.SemaphoreType.DMA((7 * nq,)), pltpu.SemaphoreType.DMA((7 * nq,)), pltpu.SemaphoreType.DMA],
                 start, wait)


def _sum_devices(blocks):
    n, rows, width = blocks.shape
    rb = rows // 2 if (rows // 2) % SUBLANES == 0 else rows

    def kern(b_ref, o_ref):
        total = b_ref[0]
        for d in range(1, n):
            total = total + b_ref[d]
        o_ref[...] = total

    return pl.pallas_call(
        kern, name="small_sum", grid=(rows // rb,), in_specs=[pl.BlockSpec((n, rb, width), lambda i: (0, i, 0))],
        out_specs=pl.BlockSpec((rb, width), lambda i: (i, 0)), out_shape=jax.ShapeDtypeStruct((rows, width), F32),
        compiler_params=_cparams("parallel"))(blocks)


def _small_allgather_sum(buf, head_rows, n_chunks=COMM_CHUNKS):
    rows, width = buf.shape
    chunks = _row_chunks(rows, n_chunks)
    nq = len(chunks)

    def body(b_ref, head_ref, sum_ref, all_ref, send_sems, recv_sems, local_sem):
        x, y, c, chips = _position()
        me, sibling = (x, y, c), (x, y, 1 - c)

        def slot(px, py, pc):
            return all_ref.at[4 * px + 2 * py + pc]

        def copy(k, block, q, to, src=None):
            rows_q = pl.ds(chunks[q][0], chunks[q][1])
            dst = slot(*block).at[rows_q, :]
            return pltpu.make_async_remote_copy(
                src_ref=dst if src is None else src.at[rows_q, :], dst_ref=dst, send_sem=send_sems.at[k * nq + q],
                recv_sem=recv_sems.at[k * nq + q], device_id=to, device_id_type=MESH_ID)

        mine = pltpu.make_async_copy(b_ref, slot(*me), local_sem)
        mine.start()
        first = []
        for q in range(nq):
            first += [copy(1 + j, me, q, (*chip, c), src=b_ref) for j, chip in enumerate(chips)]
            first.append(copy(0, me, q, sibling, src=b_ref))
        for cp in first:
            cp.start()
        passed = []
        for q in range(nq):
            for j, chip in enumerate(chips):
                copy(1 + j, (*chip, c), q, me).wait_recv()
                fwd = copy(4 + j, (*chip, c), q, sibling)
                fwd.start()
                passed.append(fwd)
        for q in range(nq):
            copy(0, sibling, q, me).wait_recv()
            for j, chip in enumerate(chips):
                copy(4 + j, (*chip, 1 - c), q, me).wait_recv()
        for cp in first + passed:
            cp.wait_send()
        mine.wait()
        total = all_ref[0]
        for d in range(1, N_DEV):
            total = total + all_ref[d]
        sum_ref[...] = total
        head_ref[...] = all_ref[:, 0:head_rows, :]

    vm = pl.BlockSpec(memory_space=pltpu.VMEM)
    return pl.pallas_call(
        body, name="small_allgather_sum", in_specs=[vm], out_specs=(vm, vm),
        out_shape=(jax.ShapeDtypeStruct((N_DEV, head_rows, width), F32), jax.ShapeDtypeStruct((rows, width), F32)),
        scratch_shapes=[pltpu.VMEM((N_DEV, rows, width), F32), pltpu.SemaphoreType.DMA((7 * nq,)),
                        pltpu.SemaphoreType.DMA((7 * nq,)), pltpu.SemaphoreType.DMA],
        compiler_params=_cparams(),
    )(buf)


def _rs_pair(g):
    n, rows, width = g.shape
    half = rows // 2
    chunks = _row_chunks(half, COMM_CHUNKS)
    nq = len(chunks)

    def body(g_ref, got_ref, send_sems, recv_sems):
        x, y, c, _ = _position()
        swaps = []
        for k in range(n):
            for q, (start, size) in enumerate(chunks):
                swaps.append(pltpu.make_async_remote_copy(
                    src_ref=g_ref.at[k, pl.ds((1 - c) * half + start, size), :], dst_ref=got_ref.at[k, pl.ds(start, size), :],
                    send_sem=send_sems.at[k * nq + q], recv_sem=recv_sems.at[k * nq + q],
                    device_id=(x, y, 1 - c), device_id_type=MESH_ID))
        for cp in swaps:
            cp.start()
        for cp in swaps:
            cp.wait()

    return pl.pallas_call(
        body, name="rs_pair", in_specs=[_ANY], out_specs=_ANY, out_shape=jax.ShapeDtypeStruct((n, half, width), g.dtype),
        scratch_shapes=[pltpu.SemaphoreType.DMA((n * nq,)), pltpu.SemaphoreType.DMA((n * nq,))],
    )(g)


def _rs_chips_ride(part_bf):
    n, rows, width = part_bf.shape
    chunks = _row_chunks(rows, COMM_CHUNKS)
    nq = len(chunks)

    def sends(pb_ref, got_ref, send_sems, recv_sems):
        x, y, c, chips = _position()
        out = []
        for q, (start, size) in enumerate(chunks):
            for j, chip in enumerate(chips):
                out.append(pltpu.make_async_remote_copy(
                    src_ref=pb_ref.at[2 * chip[0] + chip[1], pl.ds(start, size), :], dst_ref=got_ref.at[j, pl.ds(start, size), :],
                    send_sem=send_sems.at[j * nq + q], recv_sem=recv_sems.at[j * nq + q],
                    device_id=(*chip, c), device_id_type=MESH_ID))
        return out

    def start(ins, outs, sems):
        for cp in sends(ins[0], outs[0], sems[0], sems[1]):
            cp.start()

    def wait(ins, outs, sems):
        for cp in sends(ins[0], outs[0], sems[0], sems[1]):
            cp.wait()

    return _Ride([part_bf], [jax.ShapeDtypeStruct((N_CHIPS - 1, rows, width), BF16)],
                 [pltpu.SemaphoreType.DMA((3 * nq,)), pltpu.SemaphoreType.DMA((3 * nq,))], start, wait)


def _rs_join(shard):
    rows, width = shard.shape
    half = rows // 2
    chunks = _row_chunks(half, COMM_CHUNKS)
    nq = len(chunks)

    def body(in_ref, out_ref, send_sems, recv_sems):
        x, y, c, _ = _position()
        def swap(q, h):
            rows_q = pl.ds(h * half + chunks[q][0], chunks[q][1])
            return pltpu.make_async_remote_copy(
                src_ref=in_ref.at[rows_q, :], dst_ref=out_ref.at[rows_q, :], send_sem=send_sems.at[q],
                recv_sem=recv_sems.at[q], device_id=(x, y, 1 - c), device_id_type=MESH_ID)

        for q in range(nq):
            swap(q, c).start()
        for q in range(nq):
            swap(q, 1 - c).wait_recv()
        for q in range(nq):
            swap(q, c).wait_send()

    return pl.pallas_call(
        body, name="rs_join", in_specs=[_ANY], out_specs=_ANY, input_output_aliases={0: 0},
        out_shape=jax.ShapeDtypeStruct(shard.shape, shard.dtype),
        scratch_shapes=[pltpu.SemaphoreType.DMA((nq,)), pltpu.SemaphoreType.DMA((nq,))],
    )(shard)


def _pair_add(g, got, core):
    n, half, width = got.shape
    nb = 2
    rb = half // nb

    def kern(c_ref, a_ref, b_ref, f_ref, h_ref):
        s = a_ref[...] + b_ref[...]
        f_ref[...] = s
        h_ref[...] = s.astype(BF16)

    spec = pl.BlockSpec((1, rb, width), lambda k, i, c_ref: (k, i, 0))
    return pl.pallas_call(
        kern, name="rs_pair_add",
        grid_spec=pltpu.PrefetchScalarGridSpec(
            num_scalar_prefetch=1, grid=(n, nb),
            in_specs=[pl.BlockSpec((1, rb, width), lambda k, i, c_ref: (k, c_ref[0] * nb + i, 0)), spec],
            out_specs=(spec, spec)),
        out_shape=(jax.ShapeDtypeStruct(got.shape, F32), jax.ShapeDtypeStruct(got.shape, BF16)),
        compiler_params=_cparams("parallel", "parallel"))(core, g, got)


def _chip_add(part_f32, got, where):
    _, rows, width = part_f32.shape
    nb = 2
    rb = rows // nb

    def kern(w_ref, a_ref, b_ref, o_ref):
        o_ref[...] = ((a_ref[0] + b_ref[0].astype(F32)) + b_ref[1].astype(F32)) + b_ref[2].astype(F32)

    return pl.pallas_call(
        kern, name="rs_chip_add",
        grid_spec=pltpu.PrefetchScalarGridSpec(
            num_scalar_prefetch=1, grid=(nb,),
            in_specs=[pl.BlockSpec((1, rb, width), lambda i, w_ref: (w_ref[0], i, 0)),
                      pl.BlockSpec((N_CHIPS - 1, rb, width), lambda i, w_ref: (0, i, 0))],
            out_specs=pl.BlockSpec((rb, width), lambda i, w_ref: (w_ref[1] * nb + i, 0))),
        out_shape=jax.ShapeDtypeStruct((2 * rows, width), F32),
        compiler_params=_cparams("parallel"))(where, part_f32, got)


def _adamw(w, g, m, v, name):
    rows, width = w.shape
    rb = rows
    for cand in (512, 256, 128, 64, 32, 16, 8):
        if rows % cand == 0 and cand * width * 4 <= ADAM_BLOCK_BYTES:
            rb = cand
            break
    spec = pl.BlockSpec((rb, width), lambda i: (i, 0))

    def kern(w_ref, g_ref, m_ref, v_ref, d_ref, nm_ref, nv_ref):
        gv = g_ref[...]
        nm = ADAM_B1 * m_ref[...] + (1.0 - ADAM_B1) * gv
        nv = ADAM_B2 * v_ref[...] + (1.0 - ADAM_B2) * (gv * gv)
        m_hat = nm / (1.0 - ADAM_B1 ** ADAM_STEP)
        v_hat = nv / (1.0 - ADAM_B2 ** ADAM_STEP)
        d_ref[...] = -ADAM_LR * (m_hat / (jnp.sqrt(v_hat) + ADAM_EPS) + ADAM_WD * w_ref[...])
        nm_ref[...] = nm
        nv_ref[...] = nv

    shp = jax.ShapeDtypeStruct(w.shape, F32)
    return pl.pallas_call(
        kern, name=name, grid=(rows // rb,), in_specs=[spec] * 4, out_specs=(spec, spec, spec),
        out_shape=(shp, shp, shp), compiler_params=_cparams("parallel"))(w, g, m, v)


def _wada_grad(silu_t, dmod_cols):
    n = dmod_cols.shape[1]

    def kern(s_ref, d_ref, o_ref):
        acc = s_ref[:, 0:1] * d_ref[0:1, :]
        for b in range(1, N_DEV):
            acc = acc + s_ref[:, b:b + 1] * d_ref[b:b + 1, :]
        o_ref[...] = acc

    return pl.pallas_call(kern, name="wada_grad", out_shape=jax.ShapeDtypeStruct((D_MODEL, n), F32),
                          compiler_params=_cparams())(silu_t, dmod_cols)


def _rows(a, multiple):
    flat = a.reshape(-1)
    pad = (-flat.shape[0]) % (D_MODEL * multiple)
    if pad:
        flat = jnp.concatenate([flat, jnp.zeros((pad,), flat.dtype)])
    return flat.reshape(-1, D_MODEL)


def _part_rows(shape, multiple):
    return -(-int(np.prod(shape)) // (D_MODEL * multiple)) * multiple


def _pack_rows(parts, multiple):
    return jnp.concatenate([_rows(p, multiple) for p in parts], axis=0)


def _unpack_rows(buf, shapes, multiple):
    out, r = [], 0
    for shp in shapes:
        n = int(np.prod(shp))
        nr = _part_rows(shp, multiple)
        out.append(buf[r:r + nr].reshape(-1)[:n].reshape(shp))
        r += nr
    return out


def kernel(x, c, w_ada, b_ada, norm_pre, norm_post, w_in, pool_w, pool_scale, ssm_a_re, ssm_a_im, ssm_log_dt, ssm_b_re, ssm_b_im, ssm_c_re, ssm_c_im, ssm_d, glu_w, glu_b, w_branch_pool, w_branch_ssm, w_out, loss_target, m_w_ada, m_b_ada, m_norm_pre, m_norm_post, m_w_in, m_pool_w, m_pool_scale, m_ssm_a_re, m_ssm_a_im, m_ssm_log_dt, m_ssm_b_re, m_ssm_b_im, m_ssm_c_re, m_ssm_c_im, m_ssm_d, m_glu_w, m_glu_b, m_w_branch_pool, m_w_branch_ssm, m_w_out, v_w_ada, v_b_ada, v_norm_pre, v_norm_post, v_w_in, v_pool_w, v_pool_scale, v_ssm_a_re, v_ssm_a_im, v_ssm_log_dt, v_ssm_b_re, v_ssm_b_im, v_ssm_c_re, v_ssm_c_im, v_ssm_d, v_glu_w, v_glu_b, v_w_branch_pool, v_w_branch_ssm, v_w_out):
    n_ada = w_ada.shape[2]
    n_in = w_in.shape[2]
    n_row = glu_w.shape[1]
    n_pool = pool_w.shape[2]
    n_groups = pool_w.shape[1]

    def pack_bf16(shards):
        return _pad_rows(_pack_rows([s.astype(BF16) for s in shards], 2 * SUBLANES), 2 * COMM_CHUNKS * COMM_ROW_ALIGN)

    (gathered,) = _run_ride(_ag_weights_ride(pack_bf16([w_ada[0], w_in[0]])), "ag_weights")
    w_ada_bf = gathered[:, 0:n_ada].reshape(N_CHIPS, D_MODEL, n_ada).transpose(1, 0, 2).reshape(D_MODEL, 3 * D_MODEL)
    w_in_bf = gathered[:, n_ada:n_ada + n_in].reshape(N_CHIPS, D_MODEL, n_in).transpose(1, 0, 2)
    w_in_bf = w_in_bf.reshape(D_MODEL, N_CHIPS * n_in)
    pool_rows = n_groups * n_pool * POOL_GW // D_MODEL
    packed_late = pack_bf16([pool_w[0], glu_w[0], w_branch_pool[0], w_branch_ssm[0], w_out[0]])

    def unpack_late(g):
        pool = g[:, 0:pool_rows].reshape(N_CHIPS, n_groups, n_pool, POOL_GW).transpose(1, 0, 2, 3)
        squares = [g[:, pool_rows + k * n_row:pool_rows + (k + 1) * n_row].reshape(D_MODEL, D_MODEL) for k in range(4)]
        return (pool.reshape(n_groups, POOL_GW, POOL_GW), *squares)

    chip = 2 * lax.axis_index("x") + lax.axis_index("y")
    core = lax.axis_index("c").astype(jnp.int32)
    kept = {}

    def by_cols(a, n):
        return a.reshape(D_MODEL, N_CHIPS, n).transpose(1, 0, 2).reshape(N_CHIPS, -1, D_MODEL)

    def by_rows(a):
        return a.reshape(N_CHIPS, n_row, D_MODEL)

    def exchange_big(g):
        pool_by_chip = g["d_pool_w"].reshape(n_groups, N_CHIPS, n_pool, POOL_GW).transpose(1, 0, 2, 3)
        g_packed = jnp.concatenate(
            [by_cols(g["d_win"], n_in), by_rows(g["d_glu_w"]), by_rows(g["d_wbp"]), by_rows(g["d_wbs"]),
             by_rows(g["d_wout"]), pool_by_chip.reshape(N_CHIPS, pool_rows, D_MODEL)], axis=1)
        g_packed = _pad_rows(g_packed, 2 * COMM_CHUNKS * COMM_ROW_ALIGN, axis=1)
        kept["part_f32"], part_bf = _pair_add(g_packed, _rs_pair(g_packed), core.reshape(1))
        return _rs_chips_ride(part_bf)

    a_re, a_im, log_dt = ssm_a_re[0], ssm_a_im[0], ssm_log_dt[0].reshape(SSM_G, 1)
    b_re_t, b_im_t = ssm_b_re[0].transpose(2, 0, 1), ssm_b_im[0].transpose(2, 0, 1)
    early_names = ["dg2", "d_pscale", "d_glu_b", "d_dskip", "d_abar_re", "d_abar_im", "d_bb_re_t", "d_bb_im_t",
                   "d_c_re", "d_c_im"]

    def exchange_small(s):
        parts = [s[k] for k in early_names]
        kept["early_shapes"] = [p.shape for p in parts]
        return _small_allgather_ride(_pad_rows(_pack_rows(parts, SUBLANES), COMM_CHUNKS * COMM_ROW_ALIGN))

    res = _local_step(x[0], c, loss_target[0], w_ada_bf, b_ada, norm_pre, norm_post, w_in_bf, None, pool_scale,
                      a_re, a_im, log_dt, b_re_t, b_im_t, ssm_c_re[0], ssm_c_im[0], ssm_d[0], None, glu_b[0:1],
                      None, None, None, late_weights=(_ag_weights_ride(packed_late), unpack_late),
                      ride_for_dw_in=exchange_small, ride_for_dh=exchange_big)
    loss = lax.psum(res["loss"], ("x", "y", "c"))

    (all_early,) = res["rode_dw_in"]
    (g_norm_post, g_pscale, g_glu_b, g_dskip, s_abar_re, s_abar_im, s_bb_re, s_bb_im, g_c_re, g_c_im) = _unpack_rows(
        _sum_devices(all_early), kept["early_shapes"], SUBLANES)
    g_a_re, g_a_im, g_log_dt, g_b_re_t, g_b_im_t = _ssm_params_bwd(
        a_re, a_im, log_dt, b_re_t, b_im_t, s_abar_re.reshape(SSM_G, SSM_P), s_abar_im.reshape(SSM_G, SSM_P),
        s_bb_re, s_bb_im)
    late_parts = [res["dmod"], res["silu_c"], res["dg1"]]
    late_shapes = [p.shape for p in late_parts]
    head_rows = _part_rows(late_shapes[0], SUBLANES) + _part_rows(late_shapes[1], SUBLANES)
    all_late, sum_late = _small_allgather_sum(
        _pad_rows(_pack_rows(late_parts, SUBLANES), COMM_ROW_ALIGN), head_rows, n_chunks=1)
    g_b_ada, _, g_norm_pre = _unpack_rows(sum_late, late_shapes, SUBLANES)
    dmod_all = all_late[:, 0:3].reshape(N_DEV, 3 * D_MODEL)
    dmod_cols = lax.dynamic_slice_in_dim(dmod_all, chip * n_ada, n_ada, axis=1)
    silu_t = all_late[:, _part_rows(late_shapes[0], SUBLANES)].transpose(1, 0)
    g_w_ada = _wada_grad(silu_t, dmod_cols)

    (got_chips,) = res["rode"]
    shard = _rs_join(_chip_add(kept["part_f32"], got_chips, jnp.stack([chip.astype(jnp.int32), core])))
    r = 0
    g_w_in = shard[r:r + n_in].reshape(D_MODEL, n_in)
    r += n_in
    g_squares = []
    for _ in range(4):
        g_squares.append(shard[r:r + n_row])
        r += n_row
    g_glu_w, g_wbp, g_wbs, g_wout = g_squares
    g_pool_w = shard[r:r + pool_rows].reshape(n_groups * n_pool, POOL_GW)

    big = [("w_ada", w_ada[0], g_w_ada, m_w_ada[0], v_w_ada[0]),
           ("w_in", w_in[0], g_w_in, m_w_in[0], v_w_in[0]),
           ("pool_w", pool_w[0].reshape(n_groups * n_pool, POOL_GW), g_pool_w,
            m_pool_w[0].reshape(n_groups * n_pool, POOL_GW), v_pool_w[0].reshape(n_groups * n_pool, POOL_GW)),
           ("glu_w", glu_w[0], g_glu_w, m_glu_w[0], v_glu_w[0]),
           ("w_branch_pool", w_branch_pool[0], g_wbp, m_w_branch_pool[0], v_w_branch_pool[0]),
           ("w_branch_ssm", w_branch_ssm[0], g_wbs, m_w_branch_ssm[0], v_w_branch_ssm[0]),
           ("w_out", w_out[0], g_wout, m_w_out[0], v_w_out[0])]
    out = {}
    for name, w_, g_, m_, v_ in big:
        d_, nm_, nv_ = _adamw(w_, g_, m_, v_, "adamw_" + name)
        out[name] = (g_, d_, nm_, nv_)

    g_b_re = g_b_re_t.transpose(1, 2, 0)
    g_b_im = g_b_im_t.transpose(1, 2, 0)
    small = [("b_ada", b_ada, g_b_ada, m_b_ada, v_b_ada),
             ("norm_pre", norm_pre, g_norm_pre, m_norm_pre, v_norm_pre),
             ("norm_post", norm_post, g_norm_post, m_norm_post, v_norm_post),
             ("pool_scale", pool_scale, g_pscale, m_pool_scale, v_pool_scale),
             ("ssm_a_re", ssm_a_re, g_a_re, m_ssm_a_re, v_ssm_a_re),
             ("ssm_a_im", ssm_a_im, g_a_im, m_ssm_a_im, v_ssm_a_im),
             ("ssm_log_dt", ssm_log_dt, g_log_dt, m_ssm_log_dt, v_ssm_log_dt),
             ("ssm_b_re", ssm_b_re, g_b_re, m_ssm_b_re, v_ssm_b_re),
             ("ssm_b_im", ssm_b_im, g_b_im, m_ssm_b_im, v_ssm_b_im),
             ("ssm_c_re", ssm_c_re, g_c_re, m_ssm_c_re, v_ssm_c_re),
             ("ssm_c_im", ssm_c_im, g_c_im, m_ssm_c_im, v_ssm_c_im),
             ("ssm_d", ssm_d, g_dskip, m_ssm_d, v_ssm_d),
             ("glu_b", glu_b, g_glu_b, m_glu_b, v_glu_b)]
    shapes = [w_.shape for _, w_, _, _, _ in small]
    pw_, pg_, pm_, pv_ = (_pack_rows([t[i] for t in small], SUBLANES) for i in (1, 2, 3, 4))
    pd_, pnm_, pnv_ = _adamw(pw_, pg_, pm_, pv_, "adamw_small")
    unpacked = [_unpack_rows(p, shapes, SUBLANES) for p in (pg_, pd_, pnm_, pnv_)]
    for (name, _, _, _, _), g_, d_, nm_, nv_ in zip(small, *unpacked):
        out[name] = (g_, d_, nm_, nv_)

    order = ["w_ada", "b_ada", "norm_pre", "norm_post", "w_in", "pool_w", "pool_scale", "ssm_a_re", "ssm_a_im",
             "ssm_log_dt", "ssm_b_re", "ssm_b_im", "ssm_c_re", "ssm_c_im", "ssm_d", "glu_w", "glu_b", "w_branch_pool",
             "w_branch_ssm", "w_out"]
    ref_shape = dict(w_ada=w_ada.shape, w_in=w_in.shape, pool_w=pool_w.shape, glu_w=glu_w.shape,
                     w_branch_pool=w_branch_pool.shape, w_branch_ssm=w_branch_ssm.shape, w_out=w_out.shape)
    for name, w_, _, _, _ in small:
        ref_shape[name] = w_.shape
    results = [loss, res["grad_x"][None]]
    for k in range(4):
        results += [out[name][k].reshape(ref_shape[name]) for name in order]
    return tuple(results)
```

```python
import functools
import math

import numpy as np
import jax
import jax.numpy as jnp
from jax import lax
from jax.experimental import pallas as pl
from jax.experimental.pallas import tpu as pltpu

F32 = jnp.float32
BF16 = jnp.bfloat16
MESH_ID = pl.DeviceIdType.MESH

D_MODEL = 1024
LANES = 128
SUBLANES = 8
SSM_G, SSM_P, SSM_H = 64, 64, 16
LANE_BLOCKS = D_MODEL // LANES
GROUPS_PER_BLOCK = LANES // SSM_H
STATE_W = GROUPS_PER_BLOCK * SSM_P
STATE_ALL = SSM_G * SSM_P
POOL_WINDOWS = (2, 4, 8, 16)
POOL_GW = D_MODEL // len(POOL_WINDOWS)
HALO = 16
RMS_EPS = 1e-6
N_CHIPS = 4
N_DEV = 8

SCAN_CHUNK = 512
SCAN_BLOCKS = 2
ROW_CHUNK = 256
VMEM_LIMIT_BYTES = 56 * 1024 * 1024

ADAM_BLOCK_BYTES = 1 << 20
ADAM_LR, ADAM_B1, ADAM_B2, ADAM_EPS, ADAM_WD, ADAM_STEP = 0.001, 0.9, 0.999, 1e-08, 0.01, 10

_GELU_C0 = math.sqrt(2.0 / math.pi)
_GELU_C1 = 0.044715


def _cparams(*sem):
    if sem:
        return pltpu.CompilerParams(dimension_semantics=sem, vmem_limit_bytes=VMEM_LIMIT_BYTES)
    return pltpu.CompilerParams(vmem_limit_bytes=VMEM_LIMIT_BYTES)


def _sigmoid(v):
    return jax.nn.sigmoid(v)


def _silu(v):
    return v * _sigmoid(v)


def _dsilu(v):
    s = _sigmoid(v)
    return s * (1.0 + v * (1.0 - s))


def _gelu(v):
    return 0.5 * v * (1.0 + jnp.tanh(_GELU_C0 * (v + _GELU_C1 * v * v * v)))


def _dgelu(v):
    t = jnp.tanh(_GELU_C0 * (v + _GELU_C1 * v * v * v))
    return 0.5 * (1.0 + t) + 0.5 * v * (1.0 - t * t) * _GELU_C0 * (1.0 + 3.0 * _GELU_C1 * v * v)


def _dot(a, b):
    return lax.dot_general(a, b, (((1,), (0,)), ((), ())), preferred_element_type=F32)


def _dot_nt(a, b):
    return lax.dot_general(a, b, (((1,), (1,)), ((), ())), preferred_element_type=F32)


def _dot_tn(a, b):
    return lax.dot_general(a, b, (((0,), (0,)), ((), ())), preferred_element_type=F32)


def _acc8(v):
    return v.reshape(v.shape[0] // SUBLANES, SUBLANES, v.shape[1]).sum(axis=0)


class _Ride:
    def __init__(self, inputs, out_shapes, scratch, start, wait):
        self.inputs, self.out_shapes, self.scratch, self.start, self.wait = inputs, out_shapes, scratch, start, wait


def _mm(a_parts, b_parts, *, name, ta=False, tb=False, out_dtype=F32, bm=512, bn=512, bk=512, ride=None):
    a_parts, b_parts = list(a_parts), list(b_parts)
    if ta:
        assert len(a_parts) == 1
        k_dim, m_dim = a_parts[0].shape
    else:
        m_dim = a_parts[0].shape[0]
        k_dim = sum(a.shape[1] for a in a_parts)
    if tb:
        assert len(b_parts) == 1
        n_dim = b_parts[0].shape[0]
    else:
        n_dim = sum(b.shape[1] for b in b_parts)
    bm, bn, bk = min(bm, m_dim), min(bn, n_dim), min(bk, k_dim)
    nm, nn, nk = m_dim // bm, n_dim // bn, k_dim // bk
    a_ranges, off = [], 0
    for a in a_parts:
        cnt = (a.shape[0] if ta else a.shape[1]) // bk
        a_ranges.append((off, cnt))
        off += cnt
    b_ranges, off = [], 0
    for b in b_parts:
        cnt = (b.shape[0] if tb else b.shape[1]) // bn
        b_ranges.append((off, cnt))
        off += cnt

    def a_spec(off, cnt):
        if ta:
            return pl.BlockSpec((bk, bm), lambda i, n, k: (k, i))
        return pl.BlockSpec((bm, bk), lambda i, n, k: (i, jnp.clip(k - off, 0, cnt - 1)))

    def b_spec(off, cnt):
        if tb:
            return pl.BlockSpec((bn, bk), lambda i, n, k: (n, k))
        return pl.BlockSpec((bk, bn), lambda i, n, k: (k, jnp.clip(n - off, 0, cnt - 1)))

    na, nb = len(a_parts), len(b_parts)
    dims = (((0 if ta else 1,), (1 if tb else 0,)), ((), ()))

    def kern_single(a_ref, b_ref, o_ref):
        o_ref[...] = lax.dot_general(a_ref[...].astype(BF16), b_ref[...].astype(BF16), dims,
                                     preferred_element_type=F32).astype(out_dtype)

    if na == 1 and nb == 1 and nk == 1 and not ride:
        return pl.pallas_call(
            kern_single, name=name, grid=(nm, nn),
            in_specs=[pl.BlockSpec((bk, bm), lambda i, n: (0, i)) if ta else pl.BlockSpec((bm, bk), lambda i, n: (i, 0)),
                      pl.BlockSpec((bn, bk), lambda i, n: (n, 0)) if tb else pl.BlockSpec((bk, bn), lambda i, n: (0, n))],
            out_specs=pl.BlockSpec((bm, bn), lambda i, n: (i, n)),
            out_shape=jax.ShapeDtypeStruct((m_dim, n_dim), out_dtype),
            compiler_params=_cparams("parallel", "parallel"),
        )(a_parts[0], b_parts[0])

    n_rin = len(ride.inputs) if ride else 0
    n_rout = len(ride.out_shapes) if ride else 0

    def kern(*refs):
        a_refs, b_refs = refs[:na], refs[na:na + nb]
        rin = refs[na + nb:na + nb + n_rin]
        o_ref = refs[na + nb + n_rin]
        rout = refs[na + nb + n_rin + 1:na + nb + n_rin + 1 + n_rout]
        acc = refs[na + nb + n_rin + 1 + n_rout]
        rsem = refs[na + nb + n_rin + 2 + n_rout:]
        i, n, k = pl.program_id(0), pl.program_id(1), pl.program_id(2)

        if ride:
            @pl.when((i == 0) & (n == 0) & (k == 0))
            def _():
                ride.start(rin, rout, rsem)

        @pl.when(k == 0)
        def _():
            acc[...] = jnp.zeros_like(acc)

        for ja, (koff, kcnt) in enumerate(a_ranges):
            for jb, (noff, ncnt) in enumerate(b_ranges):
                def step(ja=ja, jb=jb):
                    a = a_refs[ja][...].astype(BF16)
                    b = b_refs[jb][...].astype(BF16)
                    acc[...] += lax.dot_general(a, b, dims, preferred_element_type=F32)

                if na == 1 and nb == 1:
                    step()
                else:
                    cond = (k >= koff) & (k < koff + kcnt) & (n >= noff) & (n < noff + ncnt)
                    pl.when(cond)(step)

        @pl.when(k == nk - 1)
        def _():
            o_ref[...] = acc[...].astype(out_dtype)

        if ride:
            @pl.when((i == nm - 1) & (n == nn - 1) & (k == nk - 1))
            def _():
                ride.wait(rin, rout, rsem)

    any_spec = pl.BlockSpec(memory_space=pl.ANY)
    out_spec = pl.BlockSpec((bm, bn), lambda i, n, k: (i, n))
    out_shape = jax.ShapeDtypeStruct((m_dim, n_dim), out_dtype)
    if not ride:
        return pl.pallas_call(
            kern, name=name, grid=(nm, nn, nk),
            in_specs=[a_spec(*r) for r in a_ranges] + [b_spec(*r) for r in b_ranges],
            out_specs=out_spec, out_shape=out_shape, scratch_shapes=[pltpu.VMEM((bm, bn), F32)],
            compiler_params=_cparams("parallel", "parallel", "arbitrary"),
        )(*a_parts, *b_parts)
    return pl.pallas_call(
        kern, name=name, grid=(nm, nn, nk),
        in_specs=[a_spec(*r) for r in a_ranges] + [b_spec(*r) for r in b_ranges] + [any_spec] * n_rin,
        out_specs=(out_spec,) + (any_spec,) * n_rout, out_shape=(out_shape,) + tuple(ride.out_shapes),
        scratch_shapes=[pltpu.VMEM((bm, bn), F32)] + list(ride.scratch),
        compiler_params=_cparams("arbitrary", "arbitrary", "arbitrary"),
    )(*a_parts, *b_parts, *ride.inputs)


def _ssm_param_fn(a_re, a_im, log_dt, b_re, b_im):
    dt = jnp.exp(log_dt)
    lam_re = jnp.minimum(a_re, -1e-4)
    lam_im = a_im
    mag = jnp.exp(lam_re * dt)
    abar_re = mag * jnp.cos(lam_im * dt)
    abar_im = mag * jnp.sin(lam_im * dt)
    den = lam_re * lam_re + lam_im * lam_im
    num_re = abar_re - 1.0
    f_re = (num_re * lam_re + abar_im * lam_im) / den
    f_im = (abar_im * lam_re - num_re * lam_im) / den
    bb_re = f_re * b_re - f_im * b_im
    bb_im = f_re * b_im + f_im * b_re
    return abar_re, abar_im, bb_re, bb_im


def _ssm_params(a_re, a_im, log_dt, b_re_t, b_im_t):
    def kern(are, aim, ldt, bre, bim, o_ar, o_ai, o_br, o_bi):
        ar, ai, br, bi = _ssm_param_fn(are[...], aim[...], ldt[...], bre[...], bim[...])
        o_ar[...] = ar
        o_ai[...] = ai
        o_br[...] = br
        o_bi[...] = bi

    gp = jax.ShapeDtypeStruct((SSM_G, SSM_P), F32)
    hgp = jax.ShapeDtypeStruct((SSM_H, SSM_G, SSM_P), F32)
    return pl.pallas_call(kern, name="ssm_params", out_shape=(gp, gp, hgp, hgp), compiler_params=_cparams())(
        a_re, a_im, log_dt, b_re_t, b_im_t)


def _ssm_params_bwd(a_re, a_im, log_dt, b_re_t, b_im_t, d_ar, d_ai, d_bbr, d_bbi):
    def kern(are, aim, ldt, bre, bim, dar, dai, dbr, dbi, o_are, o_aim, o_ldt, o_bre, o_bim):
        prim = (are[...], aim[...], ldt[...], bre[...], bim[...])
        _, vjp = jax.vjp(_ssm_param_fn, *prim)
        g = vjp((dar[...], dai[...], dbr[...], dbi[...]))
        o_are[...] = g[0]
        o_aim[...] = g[1]
        o_ldt[...] = g[2]
        o_bre[...] = g[3]
        o_bim[...] = g[4]

    gp = jax.ShapeDtypeStruct((SSM_G, SSM_P), F32)
    g1 = jax.ShapeDtypeStruct((SSM_G, 1), F32)
    hgp = jax.ShapeDtypeStruct((SSM_H, SSM_G, SSM_P), F32)
    return pl.pallas_call(kern, name="ssm_params_bwd", out_shape=(gp, gp, g1, hgp, hgp), compiler_params=_cparams())(
        a_re, a_im, log_dt, b_re_t, b_im_t, d_ar, d_ai, d_bbr, d_bbi)


def _pow_tables(abar_re, abar_im, tc):
    ls = tc // SUBLANES

    def kern(ar_ref, ai_ref, fr_ref, fi_ref, rr_ref, ri_ref):
        a_re = jnp.broadcast_to(ar_ref[...], (SUBLANES, STATE_W))
        a_im = jnp.broadcast_to(ai_ref[...], (SUBLANES, STATE_W))
        p_re, p_im = a_re, a_im
        for i in range(ls):
            fwd = pl.ds(SUBLANES * i, SUBLANES)
            rev = pl.ds(SUBLANES * (ls - 1 - i), SUBLANES)
            fr_ref[fwd, :] = p_re
            fi_ref[fwd, :] = p_im
            rr_ref[rev, :] = p_re
            ri_ref[rev, :] = p_im
            p_re, p_im = p_re * a_re - p_im * a_im, p_re * a_im + p_im * a_re

    vec = pl.BlockSpec((1, STATE_W), lambda b: (0, b))
    tab = pl.BlockSpec((tc, STATE_W), lambda b: (0, b))
    shp = jax.ShapeDtypeStruct((tc, STATE_ALL), F32)
    return pl.pallas_call(
        kern, name="pow_tables", grid=(LANE_BLOCKS,), in_specs=[vec, vec], out_specs=(tab, tab, tab, tab),
        out_shape=(shp, shp, shp, shp), compiler_params=_cparams("parallel"))(abar_re, abar_im)


def _mod_kernel(c_row, w_ada_bf, b_ada):
    def kern(c_ref, w_ref, b_ref, m_ref, s_ref):
        cv = c_ref[...]
        sc = _silu(cv)
        s_ref[...] = sc
        lhs = jnp.broadcast_to(sc, (SUBLANES, D_MODEL)).astype(BF16)
        m_ref[...] = _dot(lhs, w_ref[...]) + b_ref[...]

    return pl.pallas_call(
        kern, name="ada_mod",
        out_shape=(jax.ShapeDtypeStruct((SUBLANES, 3 * D_MODEL), F32), jax.ShapeDtypeStruct((1, D_MODEL), F32)),
        compiler_params=_cparams())(c_row, w_ada_bf, b_ada)


def _row_spec(tr, width=D_MODEL, col=0):
    return pl.BlockSpec((tr, width), lambda c: (c, col))


def _vec_spec(width=D_MODEL):
    return pl.BlockSpec((1, width), lambda c: (0, 0))


def _col_spec(tr):
    return pl.BlockSpec((D_MODEL, tr), lambda c: (0, c))


def _in_norm(x, g1, scale, shift):
    seq = x.shape[0]
    tr = min(ROW_CHUNK, seq)

    def kern(x_ref, g_ref, sc_ref, sh_ref, h_ref, ht_ref):
        xv = x_ref[...]
        r = lax.rsqrt(jnp.mean(xv * xv, axis=-1, keepdims=True) + RMS_EPS)
        h = ((xv * r) * g_ref[...]) * (1.0 + sc_ref[...]) + sh_ref[...]
        h_ref[...] = h.astype(BF16)
        ht_ref[...] = h.T.astype(BF16)

    return pl.pallas_call(
        kern, name="in_norm", grid=(seq // tr,),
        in_specs=[_row_spec(tr), _vec_spec(), _vec_spec(), _vec_spec()], out_specs=(_row_spec(tr), _col_spec(tr)),
        out_shape=(jax.ShapeDtypeStruct((seq, D_MODEL), BF16), jax.ShapeDtypeStruct((D_MODEL, seq), BF16)),
        compiler_params=_cparams("parallel"))(x, g1, scale, shift)


def _pool_windows(ext, pos, g, w, tr):
    cols = pl.ds(g * POOL_GW, POOL_GW)
    cur = ext[pl.ds(HALO, tr), cols]
    acc = cur
    for k in range(1, w):
        acc = acc + ext[pl.ds(HALO - k, tr), cols]
    cnt = jnp.minimum(pos + 1, w).astype(F32)
    return acc / cnt - cur


def _pool_fwd(proj, pool_w_bf, pscale):
    seq = proj.shape[0]
    tr = min(ROW_CHUNK, seq)
    hb = tr // HALO

    def kern(up_ref, halo_ref, zp_ref, pw_ref, ps_ref, y_ref, yt_ref, ext):
        c = pl.program_id(0)
        ext[0:HALO, :] = jnp.where(c > 0, halo_ref[...].astype(F32), 0.0)
        ext[HALO:, :] = up_ref[...].astype(F32)
        pos = c * tr + lax.broadcasted_iota(jnp.int32, (tr, POOL_GW), 0)
        for g, w in enumerate(POOL_WINDOWS):
            cols = pl.ds(g * POOL_GW, POOL_GW)
            pooled = _pool_windows(ext, pos, g, w, tr)
            mixed = _dot(pooled.astype(BF16), pw_ref[g])
            y = mixed * ps_ref[:, cols] * _silu(zp_ref[:, cols].astype(F32))
            y_ref[:, cols] = y.astype(BF16)
            yt_ref[cols, :] = y.T.astype(BF16)

    return pl.pallas_call(
        kern, name="pool_fwd", grid=(seq // tr,),
        in_specs=[_row_spec(tr, col=0),
                  pl.BlockSpec((HALO, D_MODEL), lambda c: (jnp.maximum(c * hb - 1, 0), 0)),
                  _row_spec(tr, col=1),
                  pl.BlockSpec((len(POOL_WINDOWS), POOL_GW, POOL_GW), lambda c: (0, 0, 0)),
                  _vec_spec()],
        out_specs=(_row_spec(tr), _col_spec(tr)),
        out_shape=(jax.ShapeDtypeStruct((seq, D_MODEL), BF16), jax.ShapeDtypeStruct((D_MODEL, seq), BF16)),
        scratch_shapes=[pltpu.VMEM((tr + HALO, D_MODEL), F32)],
        compiler_params=_cparams("parallel"))(proj, proj, proj, pool_w_bf, pscale)


def _pool_bwd(proj, dyp, pool_w_bf, pscale):
    seq = proj.shape[0]
    tr = min(ROW_CHUNK, seq)
    hb = tr // HALO
    nc = seq // tr
    n_halo = seq // HALO

    def kern(up_ref, halo_ref, zp_ref, zpn_ref, dyp_ref, dypn_ref, pw_ref, ps_ref,
             d01_ref, dpw_ref, dps_ref, ext, dpn, acc_pw, acc_ps):
        c = pl.program_id(0)

        @pl.when(c == 0)
        def _():
            acc_pw[...] = jnp.zeros_like(acc_pw)
            acc_ps[...] = jnp.zeros_like(acc_ps)

        ext[0:HALO, :] = jnp.where(c > 0, halo_ref[...].astype(F32), 0.0)
        ext[HALO:, :] = up_ref[...].astype(F32)
        pos = c * tr + lax.broadcasted_iota(jnp.int32, (tr, POOL_GW), 0)
        pos_n = (c + 1) * tr + lax.broadcasted_iota(jnp.int32, (HALO, POOL_GW), 0)
        has_next = c < nc - 1
        for g, w in enumerate(POOL_WINDOWS):
            cols = pl.ds(g * POOL_GW, POOL_GW)
            pooled_bf = _pool_windows(ext, pos, g, w, tr).astype(BF16)
            wg = pw_ref[g]
            mixed = _dot(pooled_bf, wg)
            zp = zp_ref[:, cols].astype(F32)
            sz = _silu(zp)
            dyp_g = dyp_ref[:, cols]
            ps = ps_ref[:, cols]
            dmixed = (dyp_g * ps * sz).astype(BF16)
            acc_ps[:, cols] += _acc8(dyp_g * mixed * sz)
            d01_ref[:, pl.ds(D_MODEL + g * POOL_GW, POOL_GW)] = (dyp_g * mixed * ps * _dsilu(zp)).astype(BF16)
            acc_pw[g] += _dot_tn(pooled_bf, dmixed)
            dpooled = _dot_nt(dmixed, wg)
            dmixed_n = (jnp.where(has_next, dypn_ref[:, cols], 0.0) * ps * _silu(zpn_ref[:, cols].astype(F32))).astype(BF16)
            dpooled_n = _dot_nt(dmixed_n, wg)
            dpn[0:tr, :] = dpooled / jnp.minimum(pos + 1, w).astype(F32)
            dpn[tr:, :] = dpooled_n / jnp.minimum(pos_n + 1, w).astype(F32)
            acc = dpn[0:tr, :]
            for k in range(1, w):
                acc = acc + dpn[pl.ds(k, tr), :]
            d01_ref[:, cols] = (acc - dpooled).astype(BF16)

        @pl.when(c == nc - 1)
        def _():
            dpw_ref[...] = acc_pw[...]
            dps_ref[...] = jnp.sum(acc_ps[...], axis=0, keepdims=True)

    nxt = lambda c: (jnp.minimum((c + 1) * hb, n_halo - 1), 0)
    nxt1 = lambda c: (jnp.minimum((c + 1) * hb, n_halo - 1), 1)
    return pl.pallas_call(
        kern, name="pool_bwd", grid=(nc,),
        in_specs=[_row_spec(tr, col=0),
                  pl.BlockSpec((HALO, D_MODEL), lambda c: (jnp.maximum(c * hb - 1, 0), 0)),
                  _row_spec(tr, col=1),
                  pl.BlockSpec((HALO, D_MODEL), nxt1),
                  _row_spec(tr),
                  pl.BlockSpec((HALO, D_MODEL), nxt),
                  pl.BlockSpec((len(POOL_WINDOWS), POOL_GW, POOL_GW), lambda c: (0, 0, 0)),
                  _vec_spec()],
        out_specs=(pl.BlockSpec((tr, 2 * D_MODEL), lambda c: (c, 0)),
                   pl.BlockSpec((len(POOL_WINDOWS), POOL_GW, POOL_GW), lambda c: (0, 0, 0)),
                   _vec_spec()),
        out_shape=(jax.ShapeDtypeStruct((seq, 2 * D_MODEL), BF16),
                   jax.ShapeDtypeStruct((len(POOL_WINDOWS), POOL_GW, POOL_GW), F32),
                   jax.ShapeDtypeStruct((1, D_MODEL), F32)),
        scratch_shapes=[pltpu.VMEM((tr + HALO, D_MODEL), F32), pltpu.VMEM((tr + HALO, POOL_GW), F32),
                        pltpu.VMEM((len(POOL_WINDOWS), POOL_GW, POOL_GW), F32), pltpu.VMEM((SUBLANES, D_MODEL), F32)],
        compiler_params=_cparams("arbitrary"))(proj, proj, proj, proj, dyp, dyp, pool_w_bf, pscale)


def _glu_fwd(ys, proj, glu_w_bf, glu_b):
    seq = ys.shape[0]
    tr = min(ROW_CHUNK, seq)

    def kern(ys_ref, zs_ref, w_ref, b_ref, o_ref, ot_ref):
        yg = _gelu(ys_ref[...])
        q = _dot(yg.astype(BF16), w_ref[...]) + b_ref[...]
        y = yg * _sigmoid(q) * _silu(zs_ref[...].astype(F32))
        o_ref[...] = y.astype(BF16)
        ot_ref[...] = y.T.astype(BF16)

    return pl.pallas_call(
        kern, name="glu_fwd", grid=(seq // tr,),
        in_specs=[_row_spec(tr), _row_spec(tr, col=3), pl.BlockSpec((D_MODEL, D_MODEL), lambda c: (0, 0)), _vec_spec()],
        out_specs=(_row_spec(tr), _col_spec(tr)),
        out_shape=(jax.ShapeDtypeStruct((seq, D_MODEL), BF16), jax.ShapeDtypeStruct((D_MODEL, seq), BF16)),
        compiler_params=_cparams("parallel"))(ys, proj, glu_w_bf, glu_b)


def _glu_bwd(ys, proj, dyssm, glu_w_bf, glu_b):
    seq = ys.shape[0]
    tr = min(ROW_CHUNK, seq)
    nc = seq // tr

    def kern(ys_ref, zs_ref, dy_ref, w_ref, b_ref, dys_ref, dzs_ref, dq_ref, yg_ref, db_ref, acc_b):
        c = pl.program_id(0)

        @pl.when(c == 0)
        def _():
            acc_b[...] = jnp.zeros_like(acc_b)

        ysv = ys_ref[...]
        yg = _gelu(ysv)
        yg_bf = yg.astype(BF16)
        q = _dot(yg_bf, w_ref[...]) + b_ref[...]
        sg = _sigmoid(q)
        zs = zs_ref[...].astype(F32)
        dyv = dy_ref[...]
        dyglu = dyv * _silu(zs)
        dzs_ref[...] = (dyv * (yg * sg) * _dsilu(zs)).astype(BF16)
        dq = dyglu * yg * sg * (1.0 - sg)
        dq_bf = dq.astype(BF16)
        acc_b[...] += _acc8(dq)
        dyg = dyglu * sg + _dot_nt(dq_bf, w_ref[...])
        dys_ref[...] = dyg * _dgelu(ysv)
        dq_ref[...] = dq_bf
        yg_ref[...] = yg.T.astype(BF16)

        @pl.when(c == nc - 1)
        def _():
            db_ref[...] = jnp.sum(acc_b[...], axis=0, keepdims=True)

    bf = jax.ShapeDtypeStruct((seq, D_MODEL), BF16)
    return pl.pallas_call(
        kern, name="glu_bwd", grid=(nc,),
        in_specs=[_row_spec(tr), _row_spec(tr, col=3), _row_spec(tr),
                  pl.BlockSpec((D_MODEL, D_MODEL), lambda c: (0, 0)), _vec_spec()],
        out_specs=(_row_spec(tr), _row_spec(tr), _row_spec(tr), _col_spec(tr), _vec_spec()),
        out_shape=(jax.ShapeDtypeStruct((seq, D_MODEL), F32), bf, bf, jax.ShapeDtypeStruct((D_MODEL, seq), BF16),
                   jax.ShapeDtypeStruct((1, D_MODEL), F32)),
        scratch_shapes=[pltpu.VMEM((SUBLANES, D_MODEL), F32)],
        compiler_params=_cparams("arbitrary"))(ys, proj, dyssm, glu_w_bf, glu_b)


def _out_fwd_bwd(ypool, yssm, proj, x, tgt, gate, g2, wbp_bf, wbs_bf, wout_bf):
    seq = x.shape[0]
    tr = min(ROW_CHUNK, seq)
    nc = seq // tr

    def kern(yp_ref, ysm_ref, gp_ref, gs_ref, x_ref, t_ref, gate_ref, g2_ref, wbp_ref, wbs_ref, wo_ref,
             dy_ref, dyp_ref, dys_ref, d45_ref, mb_ref, dob_ref, dbp_ref, dbs_ref, loss_ref, dgate_ref, dg2_ref,
             acc_l, acc_gate, acc_g2):
        c = pl.program_id(0)

        @pl.when(c == 0)
        def _():
            acc_l[...] = jnp.zeros_like(acc_l)
            acc_gate[...] = jnp.zeros_like(acc_gate)
            acc_g2[...] = jnp.zeros_like(acc_g2)

        bp = _dot(yp_ref[...], wbp_ref[...])
        bs = _dot(ysm_ref[...], wbs_ref[...])
        sp = _sigmoid(gp_ref[...].astype(F32))
        ss = _sigmoid(gs_ref[...].astype(F32))
        merged = sp * bp + ss * bs
        mb = merged.astype(BF16)
        out = _dot(mb, wo_ref[...])
        r2 = lax.rsqrt(jnp.mean(out * out, axis=-1, keepdims=True) + RMS_EPS)
        oh = out * r2
        gate_v, g2_v = gate_ref[...], g2_ref[...]
        ohg = oh * g2_v
        diff = (x_ref[...] + gate_v * ohg) - t_ref[...]
        acc_l[...] += _acc8(diff * diff)
        dyv = diff * (1.0 / D_MODEL)
        dy_ref[...] = dyv
        acc_gate[...] += _acc8(dyv * ohg)
        t = dyv * gate_v
        acc_g2[...] += _acc8(t * oh)
        doh = t * g2_v
        dout = r2 * (doh - oh * jnp.mean(doh * oh, axis=-1, keepdims=True))
        dob = dout.astype(BF16)
        dmerged = _dot_nt(dob, wo_ref[...])
        dbp = (dmerged * sp).astype(BF16)
        dbs = (dmerged * ss).astype(BF16)
        d45_ref[:, 0:D_MODEL] = (dmerged * bp * sp * (1.0 - sp)).astype(BF16)
        d45_ref[:, D_MODEL:] = (dmerged * bs * ss * (1.0 - ss)).astype(BF16)
        dyp_ref[...] = _dot_nt(dbp, wbp_ref[...])
        dys_ref[...] = _dot_nt(dbs, wbs_ref[...])
        mb_ref[...] = merged.T.astype(BF16)
        dob_ref[...] = dob
        dbp_ref[...] = dbp
        dbs_ref[...] = dbs

        @pl.when(c == nc - 1)
        def _():
            tot = jnp.sum(acc_l[...], axis=0, keepdims=True)
            loss_ref[...] = jnp.sum(tot, axis=1, keepdims=True) * (0.5 / D_MODEL)
            dgate_ref[...] = jnp.sum(acc_gate[...], axis=0, keepdims=True)
            dg2_ref[...] = jnp.sum(acc_g2[...], axis=0, keepdims=True)

    wspec = pl.BlockSpec((D_MODEL, D_MODEL), lambda c: (0, 0))
    f32 = jax.ShapeDtypeStruct((seq, D_MODEL), F32)
    bf = jax.ShapeDtypeStruct((seq, D_MODEL), BF16)
    vec = jax.ShapeDtypeStruct((1, D_MODEL), F32)
    acc = pltpu.VMEM((SUBLANES, D_MODEL), F32)
    return pl.pallas_call(
        kern, name="out_fwd_bwd", grid=(nc,),
        in_specs=[_row_spec(tr), _row_spec(tr), _row_spec(tr, col=4), _row_spec(tr, col=5), _row_spec(tr), _row_spec(tr),
                  _vec_spec(), _vec_spec(), wspec, wspec, wspec],
        out_specs=(_row_spec(tr), _row_spec(tr), _row_spec(tr), pl.BlockSpec((tr, 2 * D_MODEL), lambda c: (c, 0)),
                   _col_spec(tr), _row_spec(tr), _row_spec(tr), _row_spec(tr),
                   pl.BlockSpec((1, 1), lambda c: (0, 0)), _vec_spec(), _vec_spec()),
        out_shape=(f32, f32, f32, jax.ShapeDtypeStruct((seq, 2 * D_MODEL), BF16),
                   jax.ShapeDtypeStruct((D_MODEL, seq), BF16), bf, bf, bf,
                   jax.ShapeDtypeStruct((1, 1), F32), vec, vec),
        scratch_shapes=[acc, acc, acc],
        compiler_params=_cparams("arbitrary"))(ypool, yssm, proj, proj, x, tgt, gate, g2, wbp_bf, wbs_bf, wout_bf)


def _in_bwd(dh, x, dy, g1, scale):
    seq = x.shape[0]
    tr = min(ROW_CHUNK, seq)
    nc = seq // tr

    def kern(dh_ref, x_ref, dy_ref, g_ref, sc_ref, dx_ref, dsh_ref, dsc_ref, dg_ref, a_sh, a_sc, a_g):
        c = pl.program_id(0)

        @pl.when(c == 0)
        def _():
            a_sh[...] = jnp.zeros_like(a_sh)
            a_sc[...] = jnp.zeros_like(a_sc)
            a_g[...] = jnp.zeros_like(a_g)

        xv = x_ref[...]
        r = lax.rsqrt(jnp.mean(xv * xv, axis=-1, keepdims=True) + RMS_EPS)
        xh = xv * r
        g = g_ref[...]
        dhv = dh_ref[...]
        a_sh[...] += _acc8(dhv)
        a_sc[...] += _acc8(dhv * (xh * g))
        dn = dhv * (1.0 + sc_ref[...])
        a_g[...] += _acc8(dn * xh)
        dxh = dn * g
        dx_ref[...] = dy_ref[...] + r * (dxh - xh * jnp.mean(dxh * xh, axis=-1, keepdims=True))

        @pl.when(c == nc - 1)
        def _():
            dsh_ref[...] = jnp.sum(a_sh[...], axis=0, keepdims=True)
            dsc_ref[...] = jnp.sum(a_sc[...], axis=0, keepdims=True)
            dg_ref[...] = jnp.sum(a_g[...], axis=0, keepdims=True)

    vec = jax.ShapeDtypeStruct((1, D_MODEL), F32)
    acc = pltpu.VMEM((SUBLANES, D_MODEL), F32)
    return pl.pallas_call(
        kern, name="in_bwd", grid=(nc,),
        in_specs=[_row_spec(tr), _row_spec(tr), _row_spec(tr), _vec_spec(), _vec_spec()],
        out_specs=(_row_spec(tr), _vec_spec(), _vec_spec(), _vec_spec()),
        out_shape=(jax.ShapeDtypeStruct((seq, D_MODEL), F32), vec, vec, vec),
        scratch_shapes=[acc, acc, acc],
        compiler_params=_cparams("arbitrary"))(dh, x, dy, g1, scale)


def _local_scan(a_re, a_im, br, bi, xr, xi, row0, ls, reverse, init=None):
    if init is None:
        x_re = jnp.zeros((SUBLANES, STATE_W), F32)
        x_im = jnp.zeros((SUBLANES, STATE_W), F32)
    else:
        x_re, x_im = init
    for i in (range(ls - 1, -1, -1) if reverse else range(ls)):
        src = pl.ds(SUBLANES * i, SUBLANES)
        dst = pl.ds(row0 + SUBLANES * i, SUBLANES)
        n_re = a_re * x_re - a_im * x_im + br[src, :]
        n_im = a_re * x_im + a_im * x_re + bi[src, :]
        x_re, x_im = n_re, n_im
        xr[dst, :] = x_re
        xi[dst, :] = x_im
    return x_re, x_im


def _unpermute_rhs(v, sel):
    hi = v.astype(BF16)
    r1 = v - hi.astype(F32)
    mid = r1.astype(BF16)
    lo = (r1 - mid.astype(F32)).astype(BF16)
    return _dot(hi, sel) + _dot(mid, sel) + _dot(lo, sel)


def _scan_specs(tc, nb, rows_of):
    return dict(
        us=pl.BlockSpec((tc, nb * LANES), lambda b, c: (rows_of(c), 2 * D_MODEL // (nb * LANES) + b)),
        tok=pl.BlockSpec((tc, nb * LANES), lambda b, c: (rows_of(c), b)),
        bblk=pl.BlockSpec((nb, LANES, STATE_W), lambda b, c: (b, 0, 0)),
        cblk=pl.BlockSpec((nb, STATE_W, LANES), lambda b, c: (b, 0, 0)),
        vec=pl.BlockSpec((1, nb * STATE_W), lambda b, c: (0, b)),
        tab=pl.BlockSpec((tc, nb * STATE_W), lambda b, c: (0, b)),
        car=pl.BlockSpec((SUBLANES, nb * STATE_W), lambda b, c: (rows_of(c), b)),
        dvec=pl.BlockSpec((1, nb * LANES), lambda b, c: (0, b)))


def _ssm_scan_fwd(proj, bb_re, bb_im, cm_re, cm_im, abar_re, abar_im, pw_re, pw_im, d_skip, tc):
    seq = proj.shape[0]
    nc = seq // tc
    ls = tc // SUBLANES
    nb = SCAN_BLOCKS

    def kern(us_ref, bbr_ref, bbi_ref, cmr_ref, cmi_ref, ar_ref, ai_ref, pwr_ref, pwi_ref, d_ref,
             ys_ref, ecr_ref, eci_ref, bur, bui, car_r, car_i, end_r, end_i, upb, *nat):
        c = pl.program_id(1)

        @pl.when(c == 0)
        def _():
            car_r[...] = jnp.zeros_like(car_r)
            car_i[...] = jnp.zeros_like(car_i)

        for j in range(nb):
            cols = pl.ds(j * LANES, LANES)
            scols = pl.ds(j * STATE_W, STATE_W)
            nat[j][...] = us_ref[:, cols].astype(F32)
            for i in range(ls):
                upb[j, pl.ds(SUBLANES * i, SUBLANES), :] = nat[j][pl.ds(i, SUBLANES, stride=ls), :]
            u = upb[j]
            up = u.astype(BF16)
            bur[j] = _dot(up, bbr_ref[j])
            bui[j] = _dot(up, bbi_ref[j])
            a_re = jnp.broadcast_to(ar_ref[:, scols], (SUBLANES, STATE_W))
            a_im = jnp.broadcast_to(ai_ref[:, scols], (SUBLANES, STATE_W))
            x_re, x_im = _local_scan(a_re, a_im, bur.at[j], bui.at[j], bur.at[j], bui.at[j], 0, ls, False)
            end_r[j] = x_re
            end_i[j] = x_im
            big_re = pwr_ref[tc - 1:tc, scols]
            big_im = pwi_ref[tc - 1:tc, scols]
            e_re = car_r[j, 0:1, :]
            e_im = car_i[j, 0:1, :]
            for s in range(SUBLANES):
                n_re = end_r[j, s:s + 1, :] + big_re * e_re - big_im * e_im
                n_im = end_i[j, s:s + 1, :] + big_re * e_im + big_im * e_re
                e_re, e_im = n_re, n_im
                if s < SUBLANES - 1:
                    car_r[j, s + 1:s + 2, :] = e_re
                    car_i[j, s + 1:s + 2, :] = e_im
            ec_re = car_r[j]
            ec_im = car_i[j]
            ecr_ref[:, scols] = ec_re
            eci_ref[:, scols] = ec_im
            p_re = pwr_ref[:, scols].reshape(ls, SUBLANES, STATE_W)
            p_im = pwi_ref[:, scols].reshape(ls, SUBLANES, STATE_W)
            xf_re = bur[j].reshape(ls, SUBLANES, STATE_W) + p_re * ec_re[None] - p_im * ec_im[None]
            xf_im = bui[j].reshape(ls, SUBLANES, STATE_W) + p_re * ec_im[None] + p_im * ec_re[None]
            xb_re = xf_re.reshape(tc, STATE_W).astype(BF16)
            xb_im = xf_im.reshape(tc, STATE_W).astype(BF16)
            upb[j] = _dot(xb_re, cmr_ref[j]) - _dot(xb_im, cmi_ref[j]) + d_ref[:, cols] * u
            for i in range(ls):
                nat[j][pl.ds(i, SUBLANES, stride=ls), :] = upb[j, pl.ds(SUBLANES * i, SUBLANES), :]
            ys_ref[:, cols] = nat[j][...]
            car_r[j, 0:1, :] = e_re
            car_i[j, 0:1, :] = e_im

    sp = _scan_specs(tc, nb, lambda c: c)
    carry_shape = jax.ShapeDtypeStruct((nc * SUBLANES, STATE_ALL), F32)
    small = pltpu.VMEM((nb, SUBLANES, STATE_W), F32)
    big = pltpu.VMEM((nb, tc, STATE_W), F32)
    return pl.pallas_call(
        kern, name="ssm_scan_fwd", grid=(LANE_BLOCKS // nb, nc),
        in_specs=[sp["us"], sp["bblk"], sp["bblk"], sp["cblk"], sp["cblk"], sp["vec"], sp["vec"], sp["tab"], sp["tab"],
                  sp["dvec"]],
        out_specs=(sp["tok"], sp["car"], sp["car"]),
        out_shape=(jax.ShapeDtypeStruct((seq, D_MODEL), F32), carry_shape, carry_shape),
        scratch_shapes=[big, big, small, small, small, small, pltpu.VMEM((nb, tc, LANES), F32)]
        + [pltpu.VMEM((tc, LANES), F32)] * nb,
        compiler_params=_cparams("parallel", "arbitrary"),
    )(proj, bb_re, bb_im, cm_re, cm_im, abar_re, abar_im, pw_re, pw_im, d_skip)


def _ssm_scan_bwd(proj, dys, ec_re, ec_im, bb_re, bb_im, cm_re, cm_im, abar_re, abar_im,
                  pw_re, pw_im, pv_re, pv_im, d_skip, tc):
    seq = proj.shape[0]
    nc = seq // tc
    ls = tc // SUBLANES
    nb = SCAN_BLOCKS

    def kern(us_ref, dys_ref, ecr_ref, eci_ref, bbr_ref, bbi_ref, cmr_ref, cmi_ref, ar_ref, ai_ref,
             pwr_ref, pwi_ref, pvr_ref, pvi_ref, d_ref,
             dus_ref, dbbr_ref, dbbi_ref, dcmr_ref, dcmi_ref, dar_ref, dai_ref, dd_ref,
             bur, bui, xr, xi, gr, gi, fc_r, fc_i, a_bbr, a_bbi, a_cmr, a_cmi, a_ar, a_ai, a_dd, upb, dpb, *nat):
        c = pl.program_id(1)

        @pl.when(c == 0)
        def _():
            for ref in (fc_r, fc_i, a_bbr, a_bbi, a_cmr, a_cmi, a_ar, a_ai, a_dd):
                ref[...] = jnp.zeros_like(ref)

        for j in range(nb):
            cols = pl.ds(j * LANES, LANES)
            scols = pl.ds(j * STATE_W, STATE_W)
            nat_u, nat_d = nat[2 * j], nat[2 * j + 1]
            nat_u[...] = us_ref[:, cols].astype(F32)
            nat_d[...] = dys_ref[:, cols]
            for i in range(ls):
                rows_i = pl.ds(SUBLANES * i, SUBLANES)
                upb[j, rows_i, :] = nat_u[pl.ds(i, SUBLANES, stride=ls), :]
                dpb[j, rows_i, :] = nat_d[pl.ds(i, SUBLANES, stride=ls), :]
            u = upb[j]
            dysv = dpb[j]
            a_dd[j] += _acc8(dysv * u)
            up = u.astype(BF16)
            bur[j] = _dot(up, bbr_ref[j])
            bui[j] = _dot(up, bbi_ref[j])
            a_re = jnp.broadcast_to(ar_ref[:, scols], (SUBLANES, STATE_W))
            a_im = jnp.broadcast_to(ai_ref[:, scols], (SUBLANES, STATE_W))
            ec_r = ecr_ref[:, scols]
            ec_i = eci_ref[:, scols]
            xr[j, 0:SUBLANES, :] = ec_r
            xi[j, 0:SUBLANES, :] = ec_i
            _local_scan(a_re, a_im, bur.at[j], bui.at[j], xr.at[j], xi.at[j], SUBLANES, ls, False, init=(ec_r, ec_i))
            xf_re = xr[j, SUBLANES:, :]
            xf_im = xi[j, SUBLANES:, :]
            dysp = dysv.astype(BF16)
            a_cmr[j] += _dot_tn(dysp, xf_re.astype(BF16))
            a_cmi[j] -= _dot_tn(dysp, xf_im.astype(BF16))
            gr[j] = _dot_nt(dysp, cmr_ref[j])
            gi[j] = -_dot_nt(dysp, cmi_ref[j])
            _local_scan(a_re, -a_im, gr.at[j], gi.at[j], gr.at[j], gi.at[j], 0, ls, True)
            big_re = pwr_ref[tc - 1:tc, scols]
            big_im = -pwi_ref[tc - 1:tc, scols]
            f_re = fc_r[j, SUBLANES - 1:SUBLANES, :]
            f_im = fc_i[j, SUBLANES - 1:SUBLANES, :]
            for s in range(SUBLANES - 1, -1, -1):
                n_re = gr[j, s:s + 1, :] + big_re * f_re - big_im * f_im
                n_im = gi[j, s:s + 1, :] + big_re * f_im + big_im * f_re
                f_re, f_im = n_re, n_im
                if s > 0:
                    fc_r[j, s - 1:s, :] = f_re
                    fc_i[j, s - 1:s, :] = f_im
            fcv_r = fc_r[j]
            fcv_i = fc_i[j]
            q_re = pvr_ref[:, scols].reshape(ls, SUBLANES, STATE_W)
            q_im = -pvi_ref[:, scols].reshape(ls, SUBLANES, STATE_W)
            lam_re = (gr[j].reshape(ls, SUBLANES, STATE_W) + q_re * fcv_r[None] - q_im * fcv_i[None]).reshape(tc, STATE_W)
            lam_im = (gi[j].reshape(ls, SUBLANES, STATE_W) + q_re * fcv_i[None] + q_im * fcv_r[None]).reshape(tc, STATE_W)
            fc_r[j, SUBLANES - 1:SUBLANES, :] = f_re
            fc_i[j, SUBLANES - 1:SUBLANES, :] = f_im
            xp_re = xr[j, 0:tc, :]
            xp_im = xi[j, 0:tc, :]
            a_ar[j] += _acc8(lam_re * xp_re + lam_im * xp_im)
            a_ai[j] += _acc8(lam_im * xp_re - lam_re * xp_im)
            lb_re = lam_re.astype(BF16)
            lb_im = lam_im.astype(BF16)
            a_bbr[j] += _dot_tn(up, lb_re)
            a_bbi[j] += _dot_tn(up, lb_im)
            dpb[j] = _dot_nt(lb_re, bbr_ref[j]) + _dot_nt(lb_im, bbi_ref[j]) + dysv * d_ref[:, cols]
            for i in range(ls):
                nat_d[pl.ds(i, SUBLANES, stride=ls), :] = dpb[j, pl.ds(SUBLANES * i, SUBLANES), :]
            dus_ref[:, cols] = nat_d[...].astype(BF16)

        @pl.when(c == nc - 1)
        def _():
            row_g = lax.broadcasted_iota(jnp.int32, (LANES, STATE_W), 0) // SSM_H
            col_g = lax.broadcasted_iota(jnp.int32, (LANES, STATE_W), 1) // SSM_P
            fold = (lax.broadcasted_iota(jnp.int32, (STATE_W, SSM_P), 0) % SSM_P
                    == lax.broadcasted_iota(jnp.int32, (STATE_W, SSM_P), 1)).astype(BF16)
            for j in range(nb):
                rows_j = pl.ds(j * LANES, LANES)
                for acc, out in ((a_bbr, dbbr_ref), (a_bbi, dbbi_ref), (a_cmr, dcmr_ref), (a_cmi, dcmi_ref)):
                    out[rows_j, :] = _unpermute_rhs(jnp.where(row_g == col_g, acc[j], 0.0), fold)
                dar_ref[:, pl.ds(j * STATE_W, STATE_W)] = jnp.sum(a_ar[j], axis=0, keepdims=True)
                dai_ref[:, pl.ds(j * STATE_W, STATE_W)] = jnp.sum(a_ai[j], axis=0, keepdims=True)
                dd_ref[:, pl.ds(j * LANES, LANES)] = jnp.sum(a_dd[j], axis=0, keepdims=True)

    sp = _scan_specs(tc, nb, lambda c: nc - 1 - c)
    ghp = pl.BlockSpec((nb * LANES, SSM_P), lambda b, c: (b, 0))
    ghp_shape = jax.ShapeDtypeStruct((SSM_G * SSM_H, SSM_P), F32)
    small = pltpu.VMEM((nb, SUBLANES, STATE_W), F32)
    big = pltpu.VMEM((nb, tc, STATE_W), F32)
    bigp = pltpu.VMEM((nb, tc + SUBLANES, STATE_W), F32)
    blk = pltpu.VMEM((nb, LANES, STATE_W), F32)
    tok = pltpu.VMEM((nb, tc, LANES), F32)
    return pl.pallas_call(
        kern, name="ssm_scan_bwd", grid=(LANE_BLOCKS // nb, nc),
        in_specs=[sp["us"], sp["tok"], sp["car"], sp["car"], sp["bblk"], sp["bblk"], sp["cblk"], sp["cblk"],
                  sp["vec"], sp["vec"], sp["tab"], sp["tab"], sp["tab"], sp["tab"], sp["dvec"]],
        out_specs=(sp["tok"], ghp, ghp, ghp, ghp, sp["vec"], sp["vec"], sp["dvec"]),
        out_shape=(jax.ShapeDtypeStruct((seq, D_MODEL), BF16), ghp_shape, ghp_shape, ghp_shape, ghp_shape,
                   jax.ShapeDtypeStruct((1, STATE_ALL), F32), jax.ShapeDtypeStruct((1, STATE_ALL), F32),
                   jax.ShapeDtypeStruct((1, D_MODEL), F32)),
        scratch_shapes=[big, big, bigp, bigp, big, big, small, small, blk, blk, blk, blk,
                        small, small, pltpu.VMEM((nb, SUBLANES, LANES), F32), tok, tok]
        + [pltpu.VMEM((tc, LANES), F32)] * (2 * nb),
        compiler_params=_cparams("parallel", "arbitrary"),
    )(proj, dys, ec_re, ec_im, bb_re, bb_im, cm_re, cm_im, abar_re, abar_im, pw_re, pw_im, pv_re, pv_im, d_skip)


def _eye5():
    return jnp.eye(GROUPS_PER_BLOCK, dtype=F32)[None, :, None, :, None]


def _embed_b(bb_t):
    t = bb_t.transpose(1, 0, 2).reshape(LANE_BLOCKS, GROUPS_PER_BLOCK, SSM_H, 1, SSM_P)
    return (t * _eye5()).reshape(LANE_BLOCKS, LANES, STATE_W)


def _embed_c(c_ghp):
    t = c_ghp.transpose(0, 2, 1).reshape(LANE_BLOCKS, GROUPS_PER_BLOCK, SSM_P, 1, SSM_H)
    return (t * _eye5()).reshape(LANE_BLOCKS, STATE_W, LANES)


def _local_step(x, c_row, tgt, w_ada_bf, b_ada, g1, g2, w_in_bf, pool_w_bf, pscale, a_re, a_im, log_dt,
                b_re_t, b_im_t, c_re, c_im, d_skip, glu_w_bf, glu_b, wbp_bf, wbs_bf, wout_bf,
                late_weights=None, ride_for_dw_in=None, ride_for_dh=None):
    seq = x.shape[0]
    tc = min(SCAN_CHUNK, seq)
    mod8, silu_c = _mod_kernel(c_row, w_ada_bf, b_ada)
    mod = mod8[0:1]
    shift, scale, gate = mod[:, 0:D_MODEL], mod[:, D_MODEL:2 * D_MODEL], mod[:, 2 * D_MODEL:]

    abar_re, abar_im, bb_re_t, bb_im_t = _ssm_params(a_re, a_im, log_dt, b_re_t, b_im_t)
    abar_re_f, abar_im_f = abar_re.reshape(1, STATE_ALL), abar_im.reshape(1, STATE_ALL)
    pw_re, pw_im, pv_re, pv_im = _pow_tables(abar_re_f, abar_im_f, tc)
    bbe_re, bbe_im = _embed_b(bb_re_t).astype(BF16), _embed_b(bb_im_t).astype(BF16)
    cme_re, cme_im = _embed_c(c_re).astype(BF16), _embed_c(c_im).astype(BF16)
    d_row = d_skip.reshape(1, D_MODEL)

    h, h_t = _in_norm(x, g1, scale, shift)
    if late_weights:
        proj, gathered = _mm([h], [w_in_bf], name="proj", out_dtype=BF16, bm=1024, bn=1024, bk=1024,
                             ride=late_weights[0])
        pool_w_bf, glu_w_bf, wbp_bf, wbs_bf, wout_bf = late_weights[1](gathered)
    else:
        proj = _mm([h], [w_in_bf], name="proj", out_dtype=BF16, bm=1024, bn=1024, bk=1024)
    ypool, ypool_t = _pool_fwd(proj, pool_w_bf, pscale)
    ys, ec_re, ec_im = _ssm_scan_fwd(proj, bbe_re, bbe_im, cme_re, cme_im, abar_re_f, abar_im_f,
                                      pw_re, pw_im, d_row, tc)
    yssm, yssm_t = _glu_fwd(ys, proj, glu_w_bf, glu_b)
    (dy, dypool, dyssm, d45, merged_t, dob, dbp, dbs, loss, dgate, dg2) = _out_fwd_bwd(
        ypool, yssm, proj, x, tgt, gate, g2, wbp_bf, wbs_bf, wout_bf)

    d_wout = _mm([merged_t], [dob], name="dw_out", bm=1024, bn=1024, bk=1024)
    d_wbp = _mm([ypool_t], [dbp], name="dw_bp", bm=1024, bn=1024, bk=1024)
    d_wbs = _mm([yssm_t], [dbs], name="dw_bs", bm=1024, bn=1024, bk=1024)
    dys, dzs, dq, yg_t, d_glu_b = _glu_bwd(ys, proj, dyssm, glu_w_bf, glu_b)
    d_glu_w = _mm([yg_t], [dq], name="dw_glu", bm=1024, bn=1024, bk=1024)
    (dus, dbbe_re, dbbe_im, dcme_re, dcme_im, d_abar_re, d_abar_im, d_dskip) = _ssm_scan_bwd(
        proj, dys, ec_re, ec_im, bbe_re, bbe_im, cme_re, cme_im, abar_re_f, abar_im_f,
        pw_re, pw_im, pv_re, pv_im, d_row, tc)
    d01, d_pool_w, d_pscale = _pool_bwd(proj, dypool, pool_w_bf, pscale)
    dparts = [d01, dus, dzs, d45]
    small_ready = dict(
        dg2=dg2, d_pscale=d_pscale, d_glu_b=d_glu_b, d_dskip=d_dskip, d_abar_re=d_abar_re, d_abar_im=d_abar_im,
        d_bb_re_t=dbbe_re.reshape(SSM_G, SSM_H, SSM_P).transpose(1, 0, 2),
        d_bb_im_t=dbbe_im.reshape(SSM_G, SSM_H, SSM_P).transpose(1, 0, 2),
        d_c_re=dcme_re.reshape(SSM_G, SSM_H, SSM_P), d_c_im=dcme_im.reshape(SSM_G, SSM_H, SSM_P))
    ride = ride_for_dw_in(small_ready) if ride_for_dw_in else None
    d_win = _mm([h_t], dparts, name="dw_in", bm=1024, bn=1024, bk=1024, ride=ride)
    rode_dw_in = ()
    if ride:
        d_win, rode_dw_in = d_win[0], tuple(d_win[1:])
    big_grads = dict(d_win=d_win, d_glu_w=d_glu_w, d_wbp=d_wbp, d_wbs=d_wbs, d_wout=d_wout, d_pool_w=d_pool_w)
    ride = ride_for_dh(big_grads) if ride_for_dh else None
    dh = _mm(dparts, [w_in_bf], tb=True, name="dh", bm=1024, bn=1024, bk=1024, ride=ride)
    rode = ()
    if ride:
        dh, rode = dh[0], tuple(dh[1:])
    grad_x, dshift, dscale, dg1 = _in_bwd(dh, x, dy, g1, scale)
    dmod = jnp.concatenate([dshift, dscale, dgate], axis=1)
    return dict(
        rode=rode, rode_dw_in=rode_dw_in, loss=loss[0, 0], grad_x=grad_x, dmod=dmod, silu_c=silu_c, dg1=dg1,
        **small_ready, **big_grads)


def _position():
    x, y, c = lax.axis_index("x"), lax.axis_index("y"), lax.axis_index("c")
    chips = [(1 - x, y), (x, 1 - y), (1 - x, 1 - y)]
    return x, y, c, chips


_ANY = pl.BlockSpec(memory_space=pl.ANY)
COMM_CHUNKS = 4
COMM_ROW_ALIGN = 16


def _row_chunks(rows, k):
    assert rows % (k * COMM_ROW_ALIGN) == 0, (rows, k)
    step = rows // k
    return [(q * step, step) for q in range(k)]


def _ag_weights_ride(packed):
    rows, width = packed.shape
    half = rows // 2
    chunks = _row_chunks(half, COMM_CHUNKS)
    nq = len(chunks)

    def parts(p_ref, out_ref, send_sems, recv_sems):
        x, y, c, chips = _position()
        sibling = (x, y, 1 - c)

        def copy(k, chip, h, q, to, src=None):
            start, size = chunks[q]
            rows_q = pl.ds(h * half + start, size)
            dst = out_ref.at[2 * chip[0] + chip[1], rows_q, :]
            return pltpu.make_async_remote_copy(
                src_ref=dst if src is None else src.at[rows_q, :], dst_ref=dst, send_sem=send_sems.at[k * nq + q],
                recv_sem=recv_sems.at[k * nq + q], device_id=to, device_id_type=MESH_ID)

        mine = [copy(6 + h, (x, y), h, q, sibling, src=p_ref) for h in range(2) for q in range(nq)]
        first = [copy(j, (x, y), c, q, (*chip, c), src=p_ref) for q in range(nq) for j, chip in enumerate(chips)]
        return (x, y, c), chips, sibling, copy, mine, first

    def start(ins, outs, sems):
        _, _, _, _, mine, first = parts(ins[0], outs[0], sems[0], sems[1])
        for cp in first + mine:
            cp.start()

    def wait(ins, outs, sems):
        (x, y, c), chips, sibling, copy, mine, first = parts(ins[0], outs[0], sems[0], sems[1])
        passed = []
        for q in range(nq):
            for j, chip in enumerate(chips):
                copy(j, chip, c, q, (x, y, c)).wait_recv()
                fwd = copy(3 + j, chip, c, q, sibling)
                fwd.start()
                passed.append(fwd)
        for q in range(nq):
            for j, chip in enumerate(chips):
                copy(3 + j, chip, 1 - c, q, (x, y, c)).wait_recv()
        for cp in mine:
            cp.wait_recv()
        for cp in first + passed + mine:
            cp.wait_send()

    return _Ride([packed], [jax.ShapeDtypeStruct((N_CHIPS, rows, width), packed.dtype)],
                 [pltpu.SemaphoreType.DMA((8 * nq,)), pltpu.SemaphoreType.DMA((8 * nq,))], start, wait)


def _run_ride(ride, name):
    n_in, n_out = len(ride.inputs), len(ride.out_shapes)

    def body(*refs):
        ins, outs, sems = refs[:n_in], refs[n_in:n_in + n_out], refs[n_in + n_out:]
        ride.start(ins, outs, sems)
        ride.wait(ins, outs, sems)

    return pl.pallas_call(
        body, name=name, in_specs=[_ANY] * n_in, out_specs=(_ANY,) * n_out, out_shape=tuple(ride.out_shapes),
        scratch_shapes=list(ride.scratch))(*ride.inputs)


def _small_allgather_ride(buf):
    rows, width = buf.shape
    chunks = _row_chunks(rows, COMM_CHUNKS)
    nq = len(chunks)

    def parts(b_ref, all_ref, send_sems, recv_sems, local_sem):
        x, y, c, chips = _position()
        me, sibling = (x, y, c), (x, y, 1 - c)

        def copy(k, block, q, to, src=None):
            rows_q = pl.ds(chunks[q][0], chunks[q][1])
            dst = all_ref.at[4 * block[0] + 2 * block[1] + block[2], rows_q, :]
            return pltpu.make_async_remote_copy(
                src_ref=dst if src is None else src.at[rows_q, :], dst_ref=dst, send_sem=send_sems.at[k * nq + q],
                recv_sem=recv_sems.at[k * nq + q], device_id=to, device_id_type=MESH_ID)

        mine = pltpu.make_async_copy(b_ref, all_ref.at[4 * x + 2 * y + c], local_sem)
        first = []
        for q in range(nq):
            first += [copy(1 + j, me, q, (*chip, c), src=b_ref) for j, chip in enumerate(chips)]
            first.append(copy(0, me, q, sibling, src=b_ref))
        return me, sibling, c, chips, copy, mine, first

    def start(ins, outs, sems):
        _, _, _, _, _, mine, first = parts(ins[0], outs[0], *sems)
        mine.start()
        for cp in first:
            cp.start()

    def wait(ins, outs, sems):
        me, sibling, c, chips, copy, mine, first = parts(ins[0], outs[0], *sems)
        passed = []
        for q in range(nq):
            for j, chip in enumerate(chips):
                copy(1 + j, (*chip, c), q, me).wait_recv()
                fwd = copy(4 + j, (*chip, c), q, sibling)
                fwd.start()
                passed.append(fwd)
        for q in range(nq):
            copy(0, sibling, q, me).wait_recv()
            for j, chip in enumerate(chips):
                copy(4 + j, (*chip, 1 - c), q, me).wait_recv()
        for cp in first + passed:
            cp.wait_send()
        mine.wait()

    return _Ride([buf], [jax.ShapeDtypeStruct((N_DEV, rows, width), F32)],
                 [pltpu.SemaphoreType.DMA((7 * nq,)), pltpu.SemaphoreType.DMA((7 * nq,)), pltpu.SemaphoreType.DMA],
                 start, wait)


def _sum_devices(blocks):
    n, rows, width = blocks.shape
    rb = rows // 2 if (rows // 2) % SUBLANES == 0 else rows

    def kern(b_ref, o_ref):
        total = b_ref[0]
        for d in range(1, n):
            total = total + b_ref[d]
        o_ref[...] = total

    return pl.pallas_call(
        kern, name="small_sum", grid=(rows // rb,), in_specs=[pl.BlockSpec((n, rb, width), lambda i: (0, i, 0))],
        out_specs=pl.BlockSpec((rb, width), lambda i: (i, 0)), out_shape=jax.ShapeDtypeStruct((rows, width), F32),
        compiler_params=_cparams("parallel"))(blocks)


def _small_allgather_sum(buf, head_rows, n_chunks=COMM_CHUNKS):
    rows, width = buf.shape
    chunks = _row_chunks(rows, n_chunks)
    nq = len(chunks)

    def body(b_ref, head_ref, sum_ref, all_ref, send_sems, recv_sems, local_sem):
        x, y, c, chips = _position()
        me, sibling = (x, y, c), (x, y, 1 - c)

        def slot(px, py, pc):
            return all_ref.at[4 * px + 2 * py + pc]

        def copy(k, block, q, to, src=None):
            rows_q = pl.ds(chunks[q][0], chunks[q][1])
            dst = slot(*block).at[rows_q, :]
            return pltpu.make_async_remote_copy(
                src_ref=dst if src is None else src.at[rows_q, :], dst_ref=dst, send_sem=send_sems.at[k * nq + q],
                recv_sem=recv_sems.at[k * nq + q], device_id=to, device_id_type=MESH_ID)

        mine = pltpu.make_async_copy(b_ref, slot(*me), local_sem)
        mine.start()
        first = []
        for q in range(nq):
            first += [copy(1 + j, me, q, (*chip, c), src=b_ref) for j, chip in enumerate(chips)]
            first.append(copy(0, me, q, sibling, src=b_ref))
        for cp in first:
            cp.start()
        passed = []
        for q in range(nq):
            for j, chip in enumerate(chips):
                copy(1 + j, (*chip, c), q, me).wait_recv()
                fwd = copy(4 + j, (*chip, c), q, sibling)
                fwd.start()
                passed.append(fwd)
        for q in range(nq):
            copy(0, sibling, q, me).wait_recv()
            for j, chip in enumerate(chips):
                copy(4 + j, (*chip, 1 - c), q, me).wait_recv()
        for cp in first + passed:
            cp.wait_send()
        mine.wait()
        total = all_ref[0]
        for d in range(1, N_DEV):
            total = total + all_ref[d]
        sum_ref[...] = total
        head_ref[...] = all_ref[:, 0:head_rows, :]

    vm = pl.BlockSpec(memory_space=pltpu.VMEM)
    return pl.pallas_call(
        body, name="small_allgather_sum", in_specs=[vm], out_specs=(vm, vm),
        out_shape=(jax.ShapeDtypeStruct((N_DEV, head_rows, width), F32), jax.ShapeDtypeStruct((rows, width), F32)),
        scratch_shapes=[pltpu.VMEM((N_DEV, rows, width), F32), pltpu.SemaphoreType.DMA((7 * nq,)),
                        pltpu.SemaphoreType.DMA((7 * nq,)), pltpu.SemaphoreType.DMA],
        compiler_params=_cparams(),
    )(buf)


def _rs_pair(g):
    n, rows, width = g.shape
    half = rows // 2
    chunks = _row_chunks(half, COMM_CHUNKS)
    nq = len(chunks)

    def body(g_ref, got_ref, send_sems, recv_sems):
        x, y, c, _ = _position()
        swaps = []
        for k in range(n):
            for q, (start, size) in enumerate(chunks):
                swaps.append(pltpu.make_async_remote_copy(
                    src_ref=g_ref.at[k, pl.ds((1 - c) * half + start, size), :], dst_ref=got_ref.at[k, pl.ds(start, size), :],
                    send_sem=send_sems.at[k * nq + q], recv_sem=recv_sems.at[k * nq + q],
                    device_id=(x, y, 1 - c), device_id_type=MESH_ID))
        for cp in swaps:
            cp.start()
        for cp in swaps:
            cp.wait()

    return pl.pallas_call(
        body, name="rs_pair", in_specs=[_ANY], out_specs=_ANY, out_shape=jax.ShapeDtypeStruct((n, half, width), g.dtype),
        scratch_shapes=[pltpu.SemaphoreType.DMA((n * nq,)), pltpu.SemaphoreType.DMA((n * nq,))],
    )(g)


def _rs_chips_ride(part_bf):
    n, rows, width = part_bf.shape
    chunks = _row_chunks(rows, COMM_CHUNKS)
    nq = len(chunks)

    def sends(pb_ref, got_ref, send_sems, recv_sems):
        x, y, c, chips = _position()
        out = []
        for q, (start, size) in enumerate(chunks):
            for j, chip in enumerate(chips):
                out.append(pltpu.make_async_remote_copy(
                    src_ref=pb_ref.at[2 * chip[0] + chip[1], pl.ds(start, size), :], dst_ref=got_ref.at[j, pl.ds(start, size), :],
                    send_sem=send_sems.at[j * nq + q], recv_sem=recv_sems.at[j * nq + q],
                    device_id=(*chip, c), device_id_type=MESH_ID))
        return out

    def start(ins, outs, sems):
        for cp in sends(ins[0], outs[0], sems[0], sems[1]):
            cp.start()

    def wait(ins, outs, sems):
        for cp in sends(ins[0], outs[0], sems[0], sems[1]):
            cp.wait()

    return _Ride([part_bf], [jax.ShapeDtypeStruct((N_CHIPS - 1, rows, width), BF16)],
                 [pltpu.SemaphoreType.DMA((3 * nq,)), pltpu.SemaphoreType.DMA((3 * nq,))], start, wait)


def _rs_join(shard):
    rows, width = shard.shape
    half = rows // 2
    chunks = _row_chunks(half, COMM_CHUNKS)
    nq = len(chunks)

    def body(in_ref, out_ref, send_sems, recv_sems):
        x, y, c, _ = _position()
        def swap(q, h):
            rows_q = pl.ds(h * half + chunks[q][0], chunks[q][1])
            return pltpu.make_async_remote_copy(
                src_ref=in_ref.at[rows_q, :], dst_ref=out_ref.at[rows_q, :], send_sem=send_sems.at[q],
                recv_sem=recv_sems.at[q], device_id=(x, y, 1 - c), device_id_type=MESH_ID)

        for q in range(nq):
            swap(q, c).start()
        for q in range(nq):
            swap(q, 1 - c).wait_recv()
        for q in range(nq):
            swap(q, c).wait_send()

    return pl.pallas_call(
        body, name="rs_join", in_specs=[_ANY], out_specs=_ANY, input_output_aliases={0: 0},
        out_shape=jax.ShapeDtypeStruct(shard.shape, shard.dtype),
        scratch_shapes=[pltpu.SemaphoreType.DMA((nq,)), pltpu.SemaphoreType.DMA((nq,))],
    )(shard)


def _pair_add(g, got, core):
    n, half, width = got.shape
    nb = 2
    rb = half // nb

    def kern(c_ref, a_ref, b_ref, f_ref, h_ref):
        s = a_ref[...] + b_ref[...]
        f_ref[...] = s
        h_ref[...] = s.astype(BF16)

    spec = pl.BlockSpec((1, rb, width), lambda k, i, c_ref: (k, i, 0))
    return pl.pallas_call(
        kern, name="rs_pair_add",
        grid_spec=pltpu.PrefetchScalarGridSpec(
            num_scalar_prefetch=1, grid=(n, nb),
            in_specs=[pl.BlockSpec((1, rb, width), lambda k, i, c_ref: (k, c_ref[0] * nb + i, 0)), spec],
            out_specs=(spec, spec)),
        out_shape=(jax.ShapeDtypeStruct(got.shape, F32), jax.ShapeDtypeStruct(got.shape, BF16)),
        compiler_params=_cparams("parallel", "parallel"))(core, g, got)


def _chip_add(part_f32, got, where):
    _, rows, width = part_f32.shape
    nb = 2
    rb = rows // nb

    def kern(w_ref, a_ref, b_ref, o_ref):
        o_ref[...] = ((a_ref[0] + b_ref[0].astype(F32)) + b_ref[1].astype(F32)) + b_ref[2].astype(F32)

    return pl.pallas_call(
        kern, name="rs_chip_add",
        grid_spec=pltpu.PrefetchScalarGridSpec(
            num_scalar_prefetch=1, grid=(nb,),
            in_specs=[pl.BlockSpec((1, rb, width), lambda i, w_ref: (w_ref[0], i, 0)),
                      pl.BlockSpec((N_CHIPS - 1, rb, width), lambda i, w_ref: (0, i, 0))],
            out_specs=pl.BlockSpec((rb, width), lambda i, w_ref: (w_ref[1] * nb + i, 0))),
        out_shape=jax.ShapeDtypeStruct((2 * rows, width), F32),
        compiler_params=_cparams("parallel"))(where, part_f32, got)


def _adamw(w, g, m, v, name):
    rows, width = w.shape
    rb = rows
    for cand in (512, 256, 128, 64, 32, 16, 8):
        if rows % cand == 0 and cand * width * 4 <= ADAM_BLOCK_BYTES:
            rb = cand
            break
    spec = pl.BlockSpec((rb, width), lambda i: (i, 0))

    def kern(w_ref, g_ref, m_ref, v_ref, d_ref, nm_ref, nv_ref):
        gv = g_ref[...]
        nm = ADAM_B1 * m_ref[...] + (1.0 - ADAM_B1) * gv
        nv = ADAM_B2 * v_ref[...] + (1.0 - ADAM_B2) * (gv * gv)
        m_hat = nm / (1.0 - ADAM_B1 ** ADAM_STEP)
        v_hat = nv / (1.0 - ADAM_B2 ** ADAM_STEP)
        d_ref[...] = -ADAM_LR * (m_hat / (jnp.sqrt(v_hat) + ADAM_EPS) + ADAM_WD * w_ref[...])
        nm_ref[...] = nm
        nv_ref[...] = nv

    shp = jax.ShapeDtypeStruct(w.shape, F32)
    return pl.pallas_call(
        kern, name=name, grid=(rows // rb,), in_specs=[spec] * 4, out_specs=(spec, spec, spec),
        out_shape=(shp, shp, shp), compiler_params=_cparams("parallel"))(w, g, m, v)


def _wada_grad(silu_t, dmod_cols):
    n = dmod_cols.shape[1]

    def kern(s_ref, d_ref, o_ref):
        acc = s_ref[:, 0:1] * d_ref[0:1, :]
        for b in range(1, N_DEV):
            acc = acc + s_ref[:, b:b + 1] * d_ref[b:b + 1, :]
        o_ref[...] = acc

    return pl.pallas_call(kern, name="wada_grad", out_shape=jax.ShapeDtypeStruct((D_MODEL, n), F32),
                          compiler_params=_cparams())(silu_t, dmod_cols)


def _rows(a, multiple):
    flat = a.reshape(-1)
    pad = (-flat.shape[0]) % (D_MODEL * multiple)
    if pad:
        flat = jnp.concatenate([flat, jnp.zeros((pad,), flat.dtype)])
    return flat.reshape(-1, D_MODEL)


def _part_rows(shape, multiple):
    return -(-int(np.prod(shape)) // (D_MODEL * multiple)) * multiple


def _pack_rows(parts, multiple, total_multiple=1):
    blocks = [_rows(p, multiple) for p in parts]
    pad = (-sum(b.shape[0] for b in blocks)) % total_multiple
    if pad:
        blocks.append(jnp.zeros((pad, D_MODEL), blocks[0].dtype))
    return jnp.concatenate(blocks, axis=0)


def _unpack_rows(buf, shapes, multiple):
    out, r = [], 0
    for shp in shapes:
        n = int(np.prod(shp))
        nr = _part_rows(shp, multiple)
        out.append(buf[r:r + nr].reshape(-1)[:n].reshape(shp))
        r += nr
    return out


def kernel(x, c, w_ada, b_ada, norm_pre, norm_post, w_in, pool_w, pool_scale, ssm_a_re, ssm_a_im, ssm_log_dt, ssm_b_re, ssm_b_im, ssm_c_re, ssm_c_im, ssm_d, glu_w, glu_b, w_branch_pool, w_branch_ssm, w_out, loss_target, m_w_ada, m_b_ada, m_norm_pre, m_norm_post, m_w_in, m_pool_w, m_pool_scale, m_ssm_a_re, m_ssm_a_im, m_ssm_log_dt, m_ssm_b_re, m_ssm_b_im, m_ssm_c_re, m_ssm_c_im, m_ssm_d, m_glu_w, m_glu_b, m_w_branch_pool, m_w_branch_ssm, m_w_out, v_w_ada, v_b_ada, v_norm_pre, v_norm_post, v_w_in, v_pool_w, v_pool_scale, v_ssm_a_re, v_ssm_a_im, v_ssm_log_dt, v_ssm_b_re, v_ssm_b_im, v_ssm_c_re, v_ssm_c_im, v_ssm_d, v_glu_w, v_glu_b, v_w_branch_pool, v_w_branch_ssm, v_w_out):
    n_ada = w_ada.shape[2]
    n_in = w_in.shape[2]
    n_row = glu_w.shape[1]
    n_pool = pool_w.shape[2]
    n_groups = pool_w.shape[1]

    def pack_bf16(shards):
        return _pack_rows([s.astype(BF16) for s in shards], 2 * SUBLANES, 2 * COMM_CHUNKS * COMM_ROW_ALIGN)

    (gathered,) = _run_ride(_ag_weights_ride(pack_bf16([w_ada[0], w_in[0]])), "ag_weights")
    w_ada_bf = gathered[:, 0:n_ada].reshape(N_CHIPS, D_MODEL, n_ada).transpose(1, 0, 2).reshape(D_MODEL, 3 * D_MODEL)
    w_in_bf = gathered[:, n_ada:n_ada + n_in].reshape(N_CHIPS, D_MODEL, n_in).transpose(1, 0, 2)
    w_in_bf = w_in_bf.reshape(D_MODEL, N_CHIPS * n_in)
    pool_rows = n_groups * n_pool * POOL_GW // D_MODEL
    packed_late = pack_bf16([pool_w[0], glu_w[0], w_branch_pool[0], w_branch_ssm[0], w_out[0]])

    def unpack_late(g):
        pool = g[:, 0:pool_rows].reshape(N_CHIPS, n_groups, n_pool, POOL_GW).transpose(1, 0, 2, 3)
        squares = [g[:, pool_rows + k * n_row:pool_rows + (k + 1) * n_row].reshape(D_MODEL, D_MODEL) for k in range(4)]
        return (pool.reshape(n_groups, POOL_GW, POOL_GW), *squares)

    chip = 2 * lax.axis_index("x") + lax.axis_index("y")
    core = lax.axis_index("c").astype(jnp.int32)
    kept = {}

    def by_cols(a, n):
        return a.reshape(D_MODEL, N_CHIPS, n).transpose(1, 0, 2).reshape(N_CHIPS, -1, D_MODEL)

    def by_rows(a):
        return a.reshape(N_CHIPS, n_row, D_MODEL)

    def exchange_big(g):
        pool_by_chip = g["d_pool_w"].reshape(n_groups, N_CHIPS, n_pool, POOL_GW).transpose(1, 0, 2, 3)
        blocks = [by_cols(g["d_win"], n_in), by_rows(g["d_glu_w"]), by_rows(g["d_wbp"]), by_rows(g["d_wbs"]),
                  by_rows(g["d_wout"]), pool_by_chip.reshape(N_CHIPS, pool_rows, D_MODEL)]
        pad = (-sum(b.shape[1] for b in blocks)) % (2 * COMM_CHUNKS * COMM_ROW_ALIGN)
        if pad:
            blocks.append(jnp.zeros((N_CHIPS, pad, D_MODEL), F32))
        g_packed = jnp.concatenate(blocks, axis=1)
        kept["part_f32"], part_bf = _pair_add(g_packed, _rs_pair(g_packed), core.reshape(1))
        return _rs_chips_ride(part_bf)

    a_re, a_im, log_dt = ssm_a_re[0], ssm_a_im[0], ssm_log_dt[0].reshape(SSM_G, 1)
    b_re_t, b_im_t = ssm_b_re[0].transpose(2, 0, 1), ssm_b_im[0].transpose(2, 0, 1)
    early_names = ["dg2", "d_pscale", "d_glu_b", "d_dskip", "d_abar_re", "d_abar_im", "d_bb_re_t", "d_bb_im_t",
                   "d_c_re", "d_c_im"]

    def exchange_small(s):
        parts = [s[k] for k in early_names]
        kept["early_shapes"] = [p.shape for p in parts]
        return _small_allgather_ride(_pack_rows(parts, SUBLANES, COMM_CHUNKS * COMM_ROW_ALIGN))

    res = _local_step(x[0], c, loss_target[0], w_ada_bf, b_ada, norm_pre, norm_post, w_in_bf, None, pool_scale,
                      a_re, a_im, log_dt, b_re_t, b_im_t, ssm_c_re[0], ssm_c_im[0], ssm_d[0], None, glu_b[0:1],
                      None, None, None, late_weights=(_ag_weights_ride(packed_late), unpack_late),
                      ride_for_dw_in=exchange_small, ride_for_dh=exchange_big)
    loss = lax.psum(res["loss"], ("x", "y", "c"))

    (all_early,) = res["rode_dw_in"]
    (g_norm_post, g_pscale, g_glu_b, g_dskip, s_abar_re, s_abar_im, s_bb_re, s_bb_im, g_c_re, g_c_im) = _unpack_rows(
        _sum_devices(all_early), kept["early_shapes"], SUBLANES)
    g_a_re, g_a_im, g_log_dt, g_b_re_t, g_b_im_t = _ssm_params_bwd(
        a_re, a_im, log_dt, b_re_t, b_im_t, s_abar_re.reshape(SSM_G, SSM_P), s_abar_im.reshape(SSM_G, SSM_P),
        s_bb_re, s_bb_im)
    late_parts = [res["dmod"], res["silu_c"], res["dg1"]]
    late_shapes = [p.shape for p in late_parts]
    head_rows = _part_rows(late_shapes[0], SUBLANES) + _part_rows(late_shapes[1], SUBLANES)
    all_late, sum_late = _small_allgather_sum(_pack_rows(late_parts, SUBLANES, COMM_ROW_ALIGN), head_rows, n_chunks=1)
    g_b_ada, _, g_norm_pre = _unpack_rows(sum_late, late_shapes, SUBLANES)
    dmod_all = all_late[:, 0:3].reshape(N_DEV, 3 * D_MODEL)
    dmod_cols = lax.dynamic_slice_in_dim(dmod_all, chip * n_ada, n_ada, axis=1)
    silu_t = all_late[:, _part_rows(late_shapes[0], SUBLANES)].transpose(1, 0)
    g_w_ada = _wada_grad(silu_t, dmod_cols)

    (got_chips,) = res["rode"]
    shard = _rs_join(_chip_add(kept["part_f32"], got_chips, jnp.stack([chip.astype(jnp.int32), core])))
    r = 0
    g_w_in = shard[r:r + n_in].reshape(D_MODEL, n_in)
    r += n_in
    g_squares = []
    for _ in range(4):
        g_squares.append(shard[r:r + n_row])
        r += n_row
    g_glu_w, g_wbp, g_wbs, g_wout = g_squares
    g_pool_w = shard[r:r + pool_rows].reshape(n_groups * n_pool, POOL_GW)

    big = [("w_ada", w_ada[0], g_w_ada, m_w_ada[0], v_w_ada[0]),
           ("w_in", w_in[0], g_w_in, m_w_in[0], v_w_in[0]),
           ("pool_w", pool_w[0].reshape(n_groups * n_pool, POOL_GW), g_pool_w,
            m_pool_w[0].reshape(n_groups * n_pool, POOL_GW), v_pool_w[0].reshape(n_groups * n_pool, POOL_GW)),
           ("glu_w", glu_w[0], g_glu_w, m_glu_w[0], v_glu_w[0]),
           ("w_branch_pool", w_branch_pool[0], g_wbp, m_w_branch_pool[0], v_w_branch_pool[0]),
           ("w_branch_ssm", w_branch_ssm[0], g_wbs, m_w_branch_ssm[0], v_w_branch_ssm[0]),
           ("w_out", w_out[0], g_wout, m_w_out[0], v_w_out[0])]
    out = {}
    for name, w_, g_, m_, v_ in big:
        d_, nm_, nv_ = _adamw(w_, g_, m_, v_, "adamw_" + name)
        out[name] = (g_, d_, nm_, nv_)

    g_b_re = g_b_re_t.transpose(1, 2, 0)
    g_b_im = g_b_im_t.transpose(1, 2, 0)
    small = [("b_ada", b_ada, g_b_ada, m_b_ada, v_b_ada),
             ("norm_pre", norm_pre, g_norm_pre, m_norm_pre, v_norm_pre),
             ("norm_post", norm_post, g_norm_post, m_norm_post, v_norm_post),
             ("pool_scale", pool_scale, g_pscale, m_pool_scale, v_pool_scale),
             ("ssm_a_re", ssm_a_re, g_a_re, m_ssm_a_re, v_ssm_a_re),
             ("ssm_a_im", ssm_a_im, g_a_im, m_ssm_a_im, v_ssm_a_im),
             ("ssm_log_dt", ssm_log_dt, g_log_dt, m_ssm_log_dt, v_ssm_log_dt),
             ("ssm_b_re", ssm_b_re, g_b_re, m_ssm_b_re, v_ssm_b_re),
             ("ssm_b_im", ssm_b_im, g_b_im, m_ssm_b_im, v_ssm_b_im),
             ("ssm_c_re", ssm_c_re, g_c_re, m_ssm_c_re, v_ssm_c_re),
             ("ssm_c_im", ssm_c_im, g_c_im, m_ssm_c_im, v_ssm_c_im),
             ("ssm_d", ssm_d, g_dskip, m_ssm_d, v_ssm_d),
             ("glu_b", glu_b, g_glu_b, m_glu_b, v_glu_b)]
    shapes = [w_.shape for _, w_, _, _, _ in small]
    pw_, pg_, pm_, pv_ = (_pack_rows([t[i] for t in small], SUBLANES) for i in (1, 2, 3, 4))
    pd_, pnm_, pnv_ = _adamw(pw_, pg_, pm_, pv_, "adamw_small")
    unpacked = [_unpack_rows(p, shapes, SUBLANES) for p in (pg_, pd_, pnm_, pnv_)]
    for (name, _, _, _, _), g_, d_, nm_, nv_ in zip(small, *unpacked):
        out[name] = (g_, d_, nm_, nv_)

    order = ["w_ada", "b_ada", "norm_pre", "norm_post", "w_in", "pool_w", "pool_scale", "ssm_a_re", "ssm_a_im",
             "ssm_log_dt", "ssm_b_re", "ssm_b_im", "ssm_c_re", "ssm_c_im", "ssm_d", "glu_w", "glu_b", "w_branch_pool",
             "w_branch_ssm", "w_out"]
    ref_shape = dict(w_ada=w_ada.shape, w_in=w_in.shape, pool_w=pool_w.shape, glu_w=glu_w.shape,
                     w_branch_pool=w_branch_pool.shape, w_branch_ssm=w_branch_ssm.shape, w_out=w_out.shape)
    for name, w_, _, _, _ in small:
        ref_shape[name] = w_.shape
    results = [loss, res["grad_x"][None]]
    for k in range(4):
        results += [out[name][k].reshape(ref_shape[name]) for name in order]
    return tuple(results)
```

```python
import functools
import math

import numpy as np
import jax
import jax.numpy as jnp
from jax import lax
from jax.experimental import pallas as pl
from jax.experimental.pallas import tpu as pltpu

F32 = jnp.float32
BF16 = jnp.bfloat16
MESH_ID = pl.DeviceIdType.MESH

D_MODEL = 1024
LANES = 128
SUBLANES = 8
SSM_G, SSM_P, SSM_H = 64, 64, 16
LANE_BLOCKS = D_MODEL // LANES
GROUPS_PER_BLOCK = LANES // SSM_H
STATE_W = GROUPS_PER_BLOCK * SSM_P
STATE_ALL = SSM_G * SSM_P
POOL_WINDOWS = (2, 4, 8, 16)
POOL_GW = D_MODEL // len(POOL_WINDOWS)
HALO = 16
RMS_EPS = 1e-6
N_CHIPS = 4
N_DEV = 8

SCAN_CHUNK = 512
SCAN_BLOCKS = 2
ROW_CHUNK = 256
VMEM_LIMIT_BYTES = 56 * 1024 * 1024

ADAM_BLOCK_BYTES = 1 << 20
ADAM_LR, ADAM_B1, ADAM_B2, ADAM_EPS, ADAM_WD, ADAM_STEP = 0.001, 0.9, 0.999, 1e-08, 0.01, 10

_GELU_C0 = math.sqrt(2.0 / math.pi)
_GELU_C1 = 0.044715


def _cparams(*sem):
    if sem:
        return pltpu.CompilerParams(dimension_semantics=sem, vmem_limit_bytes=VMEM_LIMIT_BYTES)
    return pltpu.CompilerParams(vmem_limit_bytes=VMEM_LIMIT_BYTES)


def _sigmoid(v):
    return jax.nn.sigmoid(v)


def _silu(v):
    return v * _sigmoid(v)


def _dsilu(v):
    s = _sigmoid(v)
    return s * (1.0 + v * (1.0 - s))


def _gelu(v):
    return 0.5 * v * (1.0 + jnp.tanh(_GELU_C0 * (v + _GELU_C1 * v * v * v)))


def _dgelu(v):
    t = jnp.tanh(_GELU_C0 * (v + _GELU_C1 * v * v * v))
    return 0.5 * (1.0 + t) + 0.5 * v * (1.0 - t * t) * _GELU_C0 * (1.0 + 3.0 * _GELU_C1 * v * v)


def _dot(a, b):
    return lax.dot_general(a, b, (((1,), (0,)), ((), ())), preferred_element_type=F32)


def _dot_nt(a, b):
    return lax.dot_general(a, b, (((1,), (1,)), ((), ())), preferred_element_type=F32)


def _dot_tn(a, b):
    return lax.dot_general(a, b, (((0,), (0,)), ((), ())), preferred_element_type=F32)


def _acc8(v):
    return v.reshape(v.shape[0] // SUBLANES, SUBLANES, v.shape[1]).sum(axis=0)


class _Ride:
    def __init__(self, inputs, out_shapes, scratch, start, wait):
        self.inputs, self.out_shapes, self.scratch, self.start, self.wait = inputs, out_shapes, scratch, start, wait


def _mm(a_parts, b_parts, *, name, ta=False, tb=False, out_dtype=F32, bm=512, bn=512, bk=512, ride=None):
    a_parts, b_parts = list(a_parts), list(b_parts)
    if ta:
        assert len(a_parts) == 1
        k_dim, m_dim = a_parts[0].shape
    else:
        m_dim = a_parts[0].shape[0]
        k_dim = sum(a.shape[1] for a in a_parts)
    if tb:
        assert len(b_parts) == 1
        n_dim = b_parts[0].shape[0]
    else:
        n_dim = sum(b.shape[1] for b in b_parts)
    bm, bn, bk = min(bm, m_dim), min(bn, n_dim), min(bk, k_dim)
    nm, nn, nk = m_dim // bm, n_dim // bn, k_dim // bk
    a_ranges, off = [], 0
    for a in a_parts:
        cnt = (a.shape[0] if ta else a.shape[1]) // bk
        a_ranges.append((off, cnt))
        off += cnt
    b_ranges, off = [], 0
    for b in b_parts:
        cnt = (b.shape[0] if tb else b.shape[1]) // bn
        b_ranges.append((off, cnt))
        off += cnt

    def a_spec(off, cnt):
        if ta:
            return pl.BlockSpec((bk, bm), lambda i, n, k: (k, i))
        return pl.BlockSpec((bm, bk), lambda i, n, k: (i, jnp.clip(k - off, 0, cnt - 1)))

    def b_spec(off, cnt):
        if tb:
            return pl.BlockSpec((bn, bk), lambda i, n, k: (n, k))
        return pl.BlockSpec((bk, bn), lambda i, n, k: (k, jnp.clip(n - off, 0, cnt - 1)))

    na, nb = len(a_parts), len(b_parts)
    dims = (((0 if ta else 1,), (1 if tb else 0,)), ((), ()))

    def kern_single(a_ref, b_ref, o_ref):
        o_ref[...] = lax.dot_general(a_ref[...].astype(BF16), b_ref[...].astype(BF16), dims,
                                     preferred_element_type=F32).astype(out_dtype)

    if na == 1 and nb == 1 and nk == 1 and not ride:
        return pl.pallas_call(
            kern_single, name=name, grid=(nm, nn),
            in_specs=[pl.BlockSpec((bk, bm), lambda i, n: (0, i)) if ta else pl.BlockSpec((bm, bk), lambda i, n: (i, 0)),
                      pl.BlockSpec((bn, bk), lambda i, n: (n, 0)) if tb else pl.BlockSpec((bk, bn), lambda i, n: (0, n))],
            out_specs=pl.BlockSpec((bm, bn), lambda i, n: (i, n)),
            out_shape=jax.ShapeDtypeStruct((m_dim, n_dim), out_dtype),
            compiler_params=_cparams("parallel", "parallel"),
        )(a_parts[0], b_parts[0])

    n_rin = len(ride.inputs) if ride else 0
    n_rout = len(ride.out_shapes) if ride else 0

    def kern(*refs):
        a_refs, b_refs = refs[:na], refs[na:na + nb]
        rin = refs[na + nb:na + nb + n_rin]
        o_ref = refs[na + nb + n_rin]
        rout = refs[na + nb + n_rin + 1:na + nb + n_rin + 1 + n_rout]
        acc = refs[na + nb + n_rin + 1 + n_rout]
        rsem = refs[na + nb + n_rin + 2 + n_rout:]
        i, n, k = pl.program_id(0), pl.program_id(1), pl.program_id(2)

        if ride:
            @pl.when((i == 0) & (n == 0) & (k == 0))
            def _():
                ride.start(rin, rout, rsem)

        @pl.when(k == 0)
        def _():
            acc[...] = jnp.zeros_like(acc)

        for ja, (koff, kcnt) in enumerate(a_ranges):
            for jb, (noff, ncnt) in enumerate(b_ranges):
                def step(ja=ja, jb=jb):
                    a = a_refs[ja][...].astype(BF16)
                    b = b_refs[jb][...].astype(BF16)
                    acc[...] += lax.dot_general(a, b, dims, preferred_element_type=F32)

                if na == 1 and nb == 1:
                    step()
                else:
                    cond = (k >= koff) & (k < koff + kcnt) & (n >= noff) & (n < noff + ncnt)
                    pl.when(cond)(step)

        @pl.when(k == nk - 1)
        def _():
            o_ref[...] = acc[...].astype(out_dtype)

        if ride:
            @pl.when((i == nm - 1) & (n == nn - 1) & (k == nk - 1))
            def _():
                ride.wait(rin, rout, rsem)

    any_spec = pl.BlockSpec(memory_space=pl.ANY)
    out_spec = pl.BlockSpec((bm, bn), lambda i, n, k: (i, n))
    out_shape = jax.ShapeDtypeStruct((m_dim, n_dim), out_dtype)
    if not ride:
        return pl.pallas_call(
            kern, name=name, grid=(nm, nn, nk),
            in_specs=[a_spec(*r) for r in a_ranges] + [b_spec(*r) for r in b_ranges],
            out_specs=out_spec, out_shape=out_shape, scratch_shapes=[pltpu.VMEM((bm, bn), F32)],
            compiler_params=_cparams("parallel", "parallel", "arbitrary"),
        )(*a_parts, *b_parts)
    return pl.pallas_call(
        kern, name=name, grid=(nm, nn, nk),
        in_specs=[a_spec(*r) for r in a_ranges] + [b_spec(*r) for r in b_ranges] + [any_spec] * n_rin,
        out_specs=(out_spec,) + (any_spec,) * n_rout, out_shape=(out_shape,) + tuple(ride.out_shapes),
        scratch_shapes=[pltpu.VMEM((bm, bn), F32)] + list(ride.scratch),
        compiler_params=_cparams("arbitrary", "arbitrary", "arbitrary"),
    )(*a_parts, *b_parts, *ride.inputs)


def _ssm_param_fn(a_re, a_im, log_dt, b_re, b_im):
    dt = jnp.exp(log_dt)
    lam_re = jnp.minimum(a_re, -1e-4)
    lam_im = a_im
    mag = jnp.exp(lam_re * dt)
    abar_re = mag * jnp.cos(lam_im * dt)
    abar_im = mag * jnp.sin(lam_im * dt)
    den = lam_re * lam_re + lam_im * lam_im
    num_re = abar_re - 1.0
    f_re = (num_re * lam_re + abar_im * lam_im) / den
    f_im = (abar_im * lam_re - num_re * lam_im) / den
    bb_re = f_re * b_re - f_im * b_im
    bb_im = f_re * b_im + f_im * b_re
    return abar_re, abar_im, bb_re, bb_im


def _ssm_params(a_re, a_im, log_dt, b_re_t, b_im_t):
    def kern(are, aim, ldt, bre, bim, o_ar, o_ai, o_br, o_bi):
        ar, ai, br, bi = _ssm_param_fn(are[...], aim[...], ldt[...], bre[...], bim[...])
        o_ar[...] = ar
        o_ai[...] = ai
        o_br[...] = br
        o_bi[...] = bi

    gp = jax.ShapeDtypeStruct((SSM_G, SSM_P), F32)
    hgp = jax.ShapeDtypeStruct((SSM_H, SSM_G, SSM_P), F32)
    return pl.pallas_call(kern, name="ssm_params", out_shape=(gp, gp, hgp, hgp), compiler_params=_cparams())(
        a_re, a_im, log_dt, b_re_t, b_im_t)


def _ssm_params_bwd(a_re, a_im, log_dt, b_re_t, b_im_t, d_ar, d_ai, d_bbr, d_bbi):
    def kern(are, aim, ldt, bre, bim, dar, dai, dbr, dbi, o_are, o_aim, o_ldt, o_bre, o_bim):
        prim = (are[...], aim[...], ldt[...], bre[...], bim[...])
        _, vjp = jax.vjp(_ssm_param_fn, *prim)
        g = vjp((dar[...], dai[...], dbr[...], dbi[...]))
        o_are[...] = g[0]
        o_aim[...] = g[1]
        o_ldt[...] = g[2]
        o_bre[...] = g[3]
        o_bim[...] = g[4]

    gp = jax.ShapeDtypeStruct((SSM_G, SSM_P), F32)
    g1 = jax.ShapeDtypeStruct((SSM_G, 1), F32)
    hgp = jax.ShapeDtypeStruct((SSM_H, SSM_G, SSM_P), F32)
    return pl.pallas_call(kern, name="ssm_params_bwd", out_shape=(gp, gp, g1, hgp, hgp), compiler_params=_cparams())(
        a_re, a_im, log_dt, b_re_t, b_im_t, d_ar, d_ai, d_bbr, d_bbi)


def _pow_tables(abar_re, abar_im, tc):
    ls = tc // SUBLANES

    def kern(ar_ref, ai_ref, fr_ref, fi_ref, rr_ref, ri_ref):
        a_re = jnp.broadcast_to(ar_ref[...], (SUBLANES, STATE_W))
        a_im = jnp.broadcast_to(ai_ref[...], (SUBLANES, STATE_W))
        p_re, p_im = a_re, a_im
        for i in range(ls):
            fwd = pl.ds(SUBLANES * i, SUBLANES)
            rev = pl.ds(SUBLANES * (ls - 1 - i), SUBLANES)
            fr_ref[fwd, :] = p_re
            fi_ref[fwd, :] = p_im
            rr_ref[rev, :] = p_re
            ri_ref[rev, :] = p_im
            p_re, p_im = p_re * a_re - p_im * a_im, p_re * a_im + p_im * a_re

    vec = pl.BlockSpec((1, STATE_W), lambda b: (0, b))
    tab = pl.BlockSpec((tc, STATE_W), lambda b: (0, b))
    shp = jax.ShapeDtypeStruct((tc, STATE_ALL), F32)
    return pl.pallas_call(
        kern, name="pow_tables", grid=(LANE_BLOCKS,), in_specs=[vec, vec], out_specs=(tab, tab, tab, tab),
        out_shape=(shp, shp, shp, shp), compiler_params=_cparams("parallel"))(abar_re, abar_im)


def _mod_kernel(c_row, w_ada_bf, b_ada):
    def kern(c_ref, w_ref, b_ref, m_ref, s_ref):
        cv = c_ref[...]
        sc = _silu(cv)
        s_ref[...] = sc
        lhs = jnp.broadcast_to(sc, (SUBLANES, D_MODEL)).astype(BF16)
        m_ref[...] = _dot(lhs, w_ref[...]) + b_ref[...]

    return pl.pallas_call(
        kern, name="ada_mod",
        out_shape=(jax.ShapeDtypeStruct((SUBLANES, 3 * D_MODEL), F32), jax.ShapeDtypeStruct((1, D_MODEL), F32)),
        compiler_params=_cparams())(c_row, w_ada_bf, b_ada)


def _row_spec(tr, width=D_MODEL, col=0):
    return pl.BlockSpec((tr, width), lambda c: (c, col))


def _vec_spec(width=D_MODEL):
    return pl.BlockSpec((1, width), lambda c: (0, 0))


def _col_spec(tr):
    return pl.BlockSpec((D_MODEL, tr), lambda c: (0, c))


def _in_norm(x, g1, scale, shift):
    seq = x.shape[0]
    tr = min(ROW_CHUNK, seq)

    def kern(x_ref, g_ref, sc_ref, sh_ref, h_ref, ht_ref):
        xv = x_ref[...]
        r = lax.rsqrt(jnp.mean(xv * xv, axis=-1, keepdims=True) + RMS_EPS)
        h = ((xv * r) * g_ref[...]) * (1.0 + sc_ref[...]) + sh_ref[...]
        h_ref[...] = h.astype(BF16)
        ht_ref[...] = h.T.astype(BF16)

    return pl.pallas_call(
        kern, name="in_norm", grid=(seq // tr,),
        in_specs=[_row_spec(tr), _vec_spec(), _vec_spec(), _vec_spec()], out_specs=(_row_spec(tr), _col_spec(tr)),
        out_shape=(jax.ShapeDtypeStruct((seq, D_MODEL), BF16), jax.ShapeDtypeStruct((D_MODEL, seq), BF16)),
        compiler_params=_cparams("parallel"))(x, g1, scale, shift)


def _pool_windows(ext, pos, g, w, tr):
    cols = pl.ds(g * POOL_GW, POOL_GW)
    cur = ext[pl.ds(HALO, tr), cols]
    acc = cur
    for k in range(1, w):
        acc = acc + ext[pl.ds(HALO - k, tr), cols]
    cnt = jnp.minimum(pos + 1, w).astype(F32)
    return acc / cnt - cur


def _pool_fwd(proj, pool_w_bf, pscale):
    seq = proj.shape[0]
    tr = min(ROW_CHUNK, seq)
    hb = tr // HALO

    def kern(up_ref, halo_ref, zp_ref, pw_ref, ps_ref, y_ref, yt_ref, ext):
        c = pl.program_id(0)
        ext[0:HALO, :] = jnp.where(c > 0, halo_ref[...].astype(F32), 0.0)
        ext[HALO:, :] = up_ref[...].astype(F32)
        pos = c * tr + lax.broadcasted_iota(jnp.int32, (tr, POOL_GW), 0)
        for g, w in enumerate(POOL_WINDOWS):
            cols = pl.ds(g * POOL_GW, POOL_GW)
            pooled = _pool_windows(ext, pos, g, w, tr)
            mixed = _dot(pooled.astype(BF16), pw_ref[g])
            y = mixed * ps_ref[:, cols] * _silu(zp_ref[:, cols].astype(F32))
            y_ref[:, cols] = y.astype(BF16)
            yt_ref[cols, :] = y.T.astype(BF16)

    return pl.pallas_call(
        kern, name="pool_fwd", grid=(seq // tr,),
        in_specs=[_row_spec(tr, col=0),
                  pl.BlockSpec((HALO, D_MODEL), lambda c: (jnp.maximum(c * hb - 1, 0), 0)),
                  _row_spec(tr, col=1),
                  pl.BlockSpec((len(POOL_WINDOWS), POOL_GW, POOL_GW), lambda c: (0, 0, 0)),
                  _vec_spec()],
        out_specs=(_row_spec(tr), _col_spec(tr)),
        out_shape=(jax.ShapeDtypeStruct((seq, D_MODEL), BF16), jax.ShapeDtypeStruct((D_MODEL, seq), BF16)),
        scratch_shapes=[pltpu.VMEM((tr + HALO, D_MODEL), F32)],
        compiler_params=_cparams("parallel"))(proj, proj, proj, pool_w_bf, pscale)


def _pool_bwd(proj, dyp, pool_w_bf, pscale):
    seq = proj.shape[0]
    tr = min(ROW_CHUNK, seq)
    hb = tr // HALO
    nc = seq // tr
    n_halo = seq // HALO

    def kern(up_ref, halo_ref, zp_ref, zpn_ref, dyp_ref, dypn_ref, pw_ref, ps_ref,
             d01_ref, dpw_ref, dps_ref, ext, dpn, acc_pw, acc_ps):
        c = pl.program_id(0)

        @pl.when(c == 0)
        def _():
            acc_pw[...] = jnp.zeros_like(acc_pw)
            acc_ps[...] = jnp.zeros_like(acc_ps)

        ext[0:HALO, :] = jnp.where(c > 0, halo_ref[...].astype(F32), 0.0)
        ext[HALO:, :] = up_ref[...].astype(F32)
        pos = c * tr + lax.broadcasted_iota(jnp.int32, (tr, POOL_GW), 0)
        pos_n = (c + 1) * tr + lax.broadcasted_iota(jnp.int32, (HALO, POOL_GW), 0)
        has_next = c < nc - 1
        for g, w in enumerate(POOL_WINDOWS):
            cols = pl.ds(g * POOL_GW, POOL_GW)
            pooled_bf = _pool_windows(ext, pos, g, w, tr).astype(BF16)
            wg = pw_ref[g]
            mixed = _dot(pooled_bf, wg)
            zp = zp_ref[:, cols].astype(F32)
            sz = _silu(zp)
            dyp_g = dyp_ref[:, cols]
            ps = ps_ref[:, cols]
            dmixed = (dyp_g * ps * sz).astype(BF16)
            acc_ps[:, cols] += _acc8(dyp_g * mixed * sz)
            d01_ref[:, pl.ds(D_MODEL + g * POOL_GW, POOL_GW)] = (dyp_g * mixed * ps * _dsilu(zp)).astype(BF16)
            acc_pw[g] += _dot_tn(pooled_bf, dmixed)
            dpooled = _dot_nt(dmixed, wg)
            dmixed_n = (jnp.where(has_next, dypn_ref[:, cols], 0.0) * ps * _silu(zpn_ref[:, cols].astype(F32))).astype(BF16)
            dpooled_n = _dot_nt(dmixed_n, wg)
            dpn[0:tr, :] = dpooled / jnp.minimum(pos + 1, w).astype(F32)
            dpn[tr:, :] = dpooled_n / jnp.minimum(pos_n + 1, w).astype(F32)
            acc = dpn[0:tr, :]
            for k in range(1, w):
                acc = acc + dpn[pl.ds(k, tr), :]
            d01_ref[:, cols] = (acc - dpooled).astype(BF16)

        @pl.when(c == nc - 1)
        def _():
            dpw_ref[...] = acc_pw[...]
            dps_ref[...] = jnp.sum(acc_ps[...], axis=0, keepdims=True)

    nxt = lambda c: (jnp.minimum((c + 1) * hb, n_halo - 1), 0)
    nxt1 = lambda c: (jnp.minimum((c + 1) * hb, n_halo - 1), 1)
    return pl.pallas_call(
        kern, name="pool_bwd", grid=(nc,),
        in_specs=[_row_spec(tr, col=0),
                  pl.BlockSpec((HALO, D_MODEL), lambda c: (jnp.maximum(c * hb - 1, 0), 0)),
                  _row_spec(tr, col=1),
                  pl.BlockSpec((HALO, D_MODEL), nxt1),
                  _row_spec(tr),
                  pl.BlockSpec((HALO, D_MODEL), nxt),
                  pl.BlockSpec((len(POOL_WINDOWS), POOL_GW, POOL_GW), lambda c: (0, 0, 0)),
                  _vec_spec()],
        out_specs=(pl.BlockSpec((tr, 2 * D_MODEL), lambda c: (c, 0)),
                   pl.BlockSpec((len(POOL_WINDOWS), POOL_GW, POOL_GW), lambda c: (0, 0, 0)),
                   _vec_spec()),
        out_shape=(jax.ShapeDtypeStruct((seq, 2 * D_MODEL), BF16),
                   jax.ShapeDtypeStruct((len(POOL_WINDOWS), POOL_GW, POOL_GW), F32),
                   jax.ShapeDtypeStruct((1, D_MODEL), F32)),
        scratch_shapes=[pltpu.VMEM((tr + HALO, D_MODEL), F32), pltpu.VMEM((tr + HALO, POOL_GW), F32),
                        pltpu.VMEM((len(POOL_WINDOWS), POOL_GW, POOL_GW), F32), pltpu.VMEM((SUBLANES, D_MODEL), F32)],
        compiler_params=_cparams("arbitrary"))(proj, proj, proj, proj, dyp, dyp, pool_w_bf, pscale)


def _glu_fwd(ys, proj, glu_w_bf, glu_b):
    seq = ys.shape[0]
    tr = min(ROW_CHUNK, seq)

    def kern(ys_ref, zs_ref, w_ref, b_ref, o_ref, ot_ref):
        yg = _gelu(ys_ref[...])
        q = _dot(yg.astype(BF16), w_ref[...]) + b_ref[...]
        y = yg * _sigmoid(q) * _silu(zs_ref[...].astype(F32))
        o_ref[...] = y.astype(BF16)
        ot_ref[...] = y.T.astype(BF16)

    return pl.pallas_call(
        kern, name="glu_fwd", grid=(seq // tr,),
        in_specs=[_row_spec(tr), _row_spec(tr, col=3), pl.BlockSpec((D_MODEL, D_MODEL), lambda c: (0, 0)), _vec_spec()],
        out_specs=(_row_spec(tr), _col_spec(tr)),
        out_shape=(jax.ShapeDtypeStruct((seq, D_MODEL), BF16), jax.ShapeDtypeStruct((D_MODEL, seq), BF16)),
        compiler_params=_cparams("parallel"))(ys, proj, glu_w_bf, glu_b)


def _glu_bwd(ys, proj, dyssm, glu_w_bf, glu_b):
    seq = ys.shape[0]
    tr = min(ROW_CHUNK, seq)
    nc = seq // tr

    def kern(ys_ref, zs_ref, dy_ref, w_ref, b_ref, dys_ref, dzs_ref, dq_ref, yg_ref, db_ref, acc_b):
        c = pl.program_id(0)

        @pl.when(c == 0)
        def _():
            acc_b[...] = jnp.zeros_like(acc_b)

        ysv = ys_ref[...]
        yg = _gelu(ysv)
        yg_bf = yg.astype(BF16)
        q = _dot(yg_bf, w_ref[...]) + b_ref[...]
        sg = _sigmoid(q)
        zs = zs_ref[...].astype(F32)
        dyv = dy_ref[...]
        dyglu = dyv * _silu(zs)
        dzs_ref[...] = (dyv * (yg * sg) * _dsilu(zs)).astype(BF16)
        dq = dyglu * yg * sg * (1.0 - sg)
        dq_bf = dq.astype(BF16)
        acc_b[...] += _acc8(dq)
        dyg = dyglu * sg + _dot_nt(dq_bf, w_ref[...])
        dys_ref[...] = dyg * _dgelu(ysv)
        dq_ref[...] = dq_bf
        yg_ref[...] = yg.T.astype(BF16)

        @pl.when(c == nc - 1)
        def _():
            db_ref[...] = jnp.sum(acc_b[...], axis=0, keepdims=True)

    bf = jax.ShapeDtypeStruct((seq, D_MODEL), BF16)
    return pl.pallas_call(
        kern, name="glu_bwd", grid=(nc,),
        in_specs=[_row_spec(tr), _row_spec(tr, col=3), _row_spec(tr),
                  pl.BlockSpec((D_MODEL, D_MODEL), lambda c: (0, 0)), _vec_spec()],
        out_specs=(_row_spec(tr), _row_spec(tr), _row_spec(tr), _col_spec(tr), _vec_spec()),
        out_shape=(jax.ShapeDtypeStruct((seq, D_MODEL), F32), bf, bf, jax.ShapeDtypeStruct((D_MODEL, seq), BF16),
                   jax.ShapeDtypeStruct((1, D_MODEL), F32)),
        scratch_shapes=[pltpu.VMEM((SUBLANES, D_MODEL), F32)],
        compiler_params=_cparams("arbitrary"))(ys, proj, dyssm, glu_w_bf, glu_b)


def _out_fwd_bwd(ypool, yssm, proj, x, tgt, gate, g2, wbp_bf, wbs_bf, wout_bf):
    seq = x.shape[0]
    tr = min(ROW_CHUNK, seq)
    nc = seq // tr

    def kern(yp_ref, ysm_ref, gp_ref, gs_ref, x_ref, t_ref, gate_ref, g2_ref, wbp_ref, wbs_ref, wo_ref,
             dy_ref, dyp_ref, dys_ref, d45_ref, mb_ref, dob_ref, dbp_ref, dbs_ref, loss_ref, dgate_ref, dg2_ref,
             acc_l, acc_gate, acc_g2):
        c = pl.program_id(0)

        @pl.when(c == 0)
        def _():
            acc_l[...] = jnp.zeros_like(acc_l)
            acc_gate[...] = jnp.zeros_like(acc_gate)
            acc_g2[...] = jnp.zeros_like(acc_g2)

        bp = _dot(yp_ref[...], wbp_ref[...])
        bs = _dot(ysm_ref[...], wbs_ref[...])
        sp = _sigmoid(gp_ref[...].astype(F32))
        ss = _sigmoid(gs_ref[...].astype(F32))
        merged = sp * bp + ss * bs
        mb = merged.astype(BF16)
        out = _dot(mb, wo_ref[...])
        r2 = lax.rsqrt(jnp.mean(out * out, axis=-1, keepdims=True) + RMS_EPS)
        oh = out * r2
        gate_v, g2_v = gate_ref[...], g2_ref[...]
        ohg = oh * g2_v
        diff = (x_ref[...] + gate_v * ohg) - t_ref[...]
        acc_l[...] += _acc8(diff * diff)
        dyv = diff * (1.0 / D_MODEL)
        dy_ref[...] = dyv
        acc_gate[...] += _acc8(dyv * ohg)
        t = dyv * gate_v
        acc_g2[...] += _acc8(t * oh)
        doh = t * g2_v
        dout = r2 * (doh - oh * jnp.mean(doh * oh, axis=-1, keepdims=True))
        dob = dout.astype(BF16)
        dmerged = _dot_nt(dob, wo_ref[...])
        dbp = (dmerged * sp).astype(BF16)
        dbs = (dmerged * ss).astype(BF16)
        d45_ref[:, 0:D_MODEL] = (dmerged * bp * sp * (1.0 - sp)).astype(BF16)
        d45_ref[:, D_MODEL:] = (dmerged * bs * ss * (1.0 - ss)).astype(BF16)
        dyp_ref[...] = _dot_nt(dbp, wbp_ref[...])
        dys_ref[...] = _dot_nt(dbs, wbs_ref[...])
        mb_ref[...] = merged.T.astype(BF16)
        dob_ref[...] = dob
        dbp_ref[...] = dbp
        dbs_ref[...] = dbs

        @pl.when(c == nc - 1)
        def _():
            tot = jnp.sum(acc_l[...], axis=0, keepdims=True)
            loss_ref[...] = jnp.sum(tot, axis=1, keepdims=True) * (0.5 / D_MODEL)
            dgate_ref[...] = jnp.sum(acc_gate[...], axis=0, keepdims=True)
            dg2_ref[...] = jnp.sum(acc_g2[...], axis=0, keepdims=True)

    wspec = pl.BlockSpec((D_MODEL, D_MODEL), lambda c: (0, 0))
    f32 = jax.ShapeDtypeStruct((seq, D_MODEL), F32)
    bf = jax.ShapeDtypeStruct((seq, D_MODEL), BF16)
    vec = jax.ShapeDtypeStruct((1, D_MODEL), F32)
    acc = pltpu.VMEM((SUBLANES, D_MODEL), F32)
    return pl.pallas_call(
        kern, name="out_fwd_bwd", grid=(nc,),
        in_specs=[_row_spec(tr), _row_spec(tr), _row_spec(tr, col=4), _row_spec(tr, col=5), _row_spec(tr), _row_spec(tr),
                  _vec_spec(), _vec_spec(), wspec, wspec, wspec],
        out_specs=(_row_spec(tr), _row_spec(tr), _row_spec(tr), pl.BlockSpec((tr, 2 * D_MODEL), lambda c: (c, 0)),
                   _col_spec(tr), _row_spec(tr), _row_spec(tr), _row_spec(tr),
                   pl.BlockSpec((1, 1), lambda c: (0, 0)), _vec_spec(), _vec_spec()),
        out_shape=(f32, f32, f32, jax.ShapeDtypeStruct((seq, 2 * D_MODEL), BF16),
                   jax.ShapeDtypeStruct((D_MODEL, seq), BF16), bf, bf, bf,
                   jax.ShapeDtypeStruct((1, 1), F32), vec, vec),
        scratch_shapes=[acc, acc, acc],
        compiler_params=_cparams("arbitrary"))(ypool, yssm, proj, proj, x, tgt, gate, g2, wbp_bf, wbs_bf, wout_bf)


def _in_bwd(dh, x, dy, g1, scale):
    seq = x.shape[0]
    tr = min(ROW_CHUNK, seq)
    nc = seq // tr

    def kern(dh_ref, x_ref, dy_ref, g_ref, sc_ref, dx_ref, dsh_ref, dsc_ref, dg_ref, a_sh, a_sc, a_g):
        c = pl.program_id(0)

        @pl.when(c == 0)
        def _():
            a_sh[...] = jnp.zeros_like(a_sh)
            a_sc[...] = jnp.zeros_like(a_sc)
            a_g[...] = jnp.zeros_like(a_g)

        xv = x_ref[...]
        r = lax.rsqrt(jnp.mean(xv * xv, axis=-1, keepdims=True) + RMS_EPS)
        xh = xv * r
        g = g_ref[...]
        dhv = dh_ref[...]
        a_sh[...] += _acc8(dhv)
        a_sc[...] += _acc8(dhv * (xh * g))
        dn = dhv * (1.0 + sc_ref[...])
        a_g[...] += _acc8(dn * xh)
        dxh = dn * g
        dx_ref[...] = dy_ref[...] + r * (dxh - xh * jnp.mean(dxh * xh, axis=-1, keepdims=True))

        @pl.when(c == nc - 1)
        def _():
            dsh_ref[...] = jnp.sum(a_sh[...], axis=0, keepdims=True)
            dsc_ref[...] = jnp.sum(a_sc[...], axis=0, keepdims=True)
            dg_ref[...] = jnp.sum(a_g[...], axis=0, keepdims=True)

    vec = jax.ShapeDtypeStruct((1, D_MODEL), F32)
    acc = pltpu.VMEM((SUBLANES, D_MODEL), F32)
    return pl.pallas_call(
        kern, name="in_bwd", grid=(nc,),
        in_specs=[_row_spec(tr), _row_spec(tr), _row_spec(tr), _vec_spec(), _vec_spec()],
        out_specs=(_row_spec(tr), _vec_spec(), _vec_spec(), _vec_spec()),
        out_shape=(jax.ShapeDtypeStruct((seq, D_MODEL), F32), vec, vec, vec),
        scratch_shapes=[acc, acc, acc],
        compiler_params=_cparams("arbitrary"))(dh, x, dy, g1, scale)


SLAB = 2 * SUBLANES


def _local_scan(a_re, a_im, br, bi, xr, xi, row0, ls, reverse, init=None, xb=None):
    if init is None:
        x_re = jnp.zeros((SUBLANES, STATE_W), F32)
        x_im = jnp.zeros((SUBLANES, STATE_W), F32)
    else:
        x_re, x_im = init
    for i in (range(ls - 1, -1, -1) if reverse else range(ls)):
        src = pl.ds(SUBLANES * i, SUBLANES)
        dst = pl.ds(row0 + SUBLANES * i, SUBLANES)
        n_re = a_re * x_re - a_im * x_im + br[src, :]
        n_im = a_re * x_im + a_im * x_re + bi[src, :]
        if xb is not None and i % 2 == 1:
            pair = pl.ds(SUBLANES * (i - 1), SLAB)
            xb[0][pair, :] = jnp.concatenate([x_re, n_re], axis=0).astype(BF16)
            xb[1][pair, :] = jnp.concatenate([x_im, n_im], axis=0).astype(BF16)
        x_re, x_im = n_re, n_im
        xr[dst, :] = x_re
        xi[dst, :] = x_im
    return x_re, x_im


def _two(v):
    return jnp.concatenate([v, v], axis=0)


def _unpermute_rhs(v, sel):
    hi = v.astype(BF16)
    r1 = v - hi.astype(F32)
    mid = r1.astype(BF16)
    lo = (r1 - mid.astype(F32)).astype(BF16)
    return _dot(hi, sel) + _dot(mid, sel) + _dot(lo, sel)


def _scan_specs(tc, nb, rows_of):
    return dict(
        us=pl.BlockSpec((tc, nb * LANES), lambda b, c: (rows_of(c), 2 * D_MODEL // (nb * LANES) + b)),
        tok=pl.BlockSpec((tc, nb * LANES), lambda b, c: (rows_of(c), b)),
        bblk=pl.BlockSpec((nb, LANES, STATE_W), lambda b, c: (b, 0, 0)),
        cblk=pl.BlockSpec((nb, STATE_W, LANES), lambda b, c: (b, 0, 0)),
        vec=pl.BlockSpec((1, nb * STATE_W), lambda b, c: (0, b)),
        tab=pl.BlockSpec((tc, nb * STATE_W), lambda b, c: (0, b)),
        car=pl.BlockSpec((SUBLANES, nb * STATE_W), lambda b, c: (rows_of(c), b)),
        dvec=pl.BlockSpec((1, nb * LANES), lambda b, c: (0, b)))


def _ssm_scan_fwd(proj, bb_re, bb_im, cm_re, cm_im, abar_re, abar_im, pw_re, pw_im, d_skip, tc):
    seq = proj.shape[0]
    nc = seq // tc
    ls = tc // SUBLANES
    nb = SCAN_BLOCKS

    def kern(us_ref, bbr_ref, bbi_ref, cmr_ref, cmi_ref, ar_ref, ai_ref, pwr_ref, pwi_ref, d_ref,
             ys_ref, ecr_ref, eci_ref, bur, bui, car_r, car_i, end_r, end_i, upb, xb_r, xb_i, *nat):
        c = pl.program_id(1)

        @pl.when(c == 0)
        def _():
            car_r[...] = jnp.zeros_like(car_r)
            car_i[...] = jnp.zeros_like(car_i)

        for j in range(nb):
            cols = pl.ds(j * LANES, LANES)
            scols = pl.ds(j * STATE_W, STATE_W)
            nat[j][...] = us_ref[:, cols].astype(F32)
            for i in range(ls):
                upb[j, pl.ds(SUBLANES * i, SUBLANES), :] = nat[j][pl.ds(i, SUBLANES, stride=ls), :]
            u = upb[j]
            up = u.astype(BF16)
            bur[j] = _dot(up, bbr_ref[j])
            bui[j] = _dot(up, bbi_ref[j])
            a_re = jnp.broadcast_to(ar_ref[:, scols], (SUBLANES, STATE_W))
            a_im = jnp.broadcast_to(ai_ref[:, scols], (SUBLANES, STATE_W))
            x_re, x_im = _local_scan(a_re, a_im, bur.at[j], bui.at[j], bur.at[j], bui.at[j], 0, ls, False)
            end_r[j] = x_re
            end_i[j] = x_im
            big_re = pwr_ref[tc - 1:tc, scols]
            big_im = pwi_ref[tc - 1:tc, scols]
            e_re = car_r[j, 0:1, :]
            e_im = car_i[j, 0:1, :]
            for s in range(SUBLANES):
                n_re = end_r[j, s:s + 1, :] + big_re * e_re - big_im * e_im
                n_im = end_i[j, s:s + 1, :] + big_re * e_im + big_im * e_re
                e_re, e_im = n_re, n_im
                if s < SUBLANES - 1:
                    car_r[j, s + 1:s + 2, :] = e_re
                    car_i[j, s + 1:s + 2, :] = e_im
            ec_re = car_r[j]
            ec_im = car_i[j]
            ecr_ref[:, scols] = ec_re
            eci_ref[:, scols] = ec_im
            e2_re, e2_im = _two(ec_re), _two(ec_im)
            for k in range(tc // SLAB):
                rows_k = pl.ds(SLAB * k, SLAB)
                p_re = pwr_ref[rows_k, scols]
                p_im = pwi_ref[rows_k, scols]
                xb_r[j, rows_k, :] = (bur[j, rows_k, :] + p_re * e2_re - p_im * e2_im).astype(BF16)
                xb_i[j, rows_k, :] = (bui[j, rows_k, :] + p_re * e2_im + p_im * e2_re).astype(BF16)
            upb[j] = _dot(xb_r[j], cmr_ref[j]) - _dot(xb_i[j], cmi_ref[j]) + d_ref[:, cols] * u
            for i in range(ls):
                nat[j][pl.ds(i, SUBLANES, stride=ls), :] = upb[j, pl.ds(SUBLANES * i, SUBLANES), :]
            ys_ref[:, cols] = nat[j][...]
            car_r[j, 0:1, :] = e_re
            car_i[j, 0:1, :] = e_im

    sp = _scan_specs(tc, nb, lambda c: c)
    carry_shape = jax.ShapeDtypeStruct((nc * SUBLANES, STATE_ALL), F32)
    small = pltpu.VMEM((nb, SUBLANES, STATE_W), F32)
    big = pltpu.VMEM((nb, tc, STATE_W), F32)
    return pl.pallas_call(
        kern, name="ssm_scan_fwd", grid=(LANE_BLOCKS // nb, nc),
        in_specs=[sp["us"], sp["bblk"], sp["bblk"], sp["cblk"], sp["cblk"], sp["vec"], sp["vec"], sp["tab"], sp["tab"],
                  sp["dvec"]],
        out_specs=(sp["tok"], sp["car"], sp["car"]),
        out_shape=(jax.ShapeDtypeStruct((seq, D_MODEL), F32), carry_shape, carry_shape),
        scratch_shapes=[big, big, small, small, small, small, pltpu.VMEM((nb, tc, LANES), F32),
                        pltpu.VMEM((nb, tc, STATE_W), BF16), pltpu.VMEM((nb, tc, STATE_W), BF16)]
        + [pltpu.VMEM((tc, LANES), F32)] * nb,
        compiler_params=_cparams("parallel", "arbitrary"),
    )(proj, bb_re, bb_im, cm_re, cm_im, abar_re, abar_im, pw_re, pw_im, d_skip)


def _ssm_scan_bwd(proj, dys, ec_re, ec_im, bb_re, bb_im, cm_re, cm_im, abar_re, abar_im,
                  pw_re, pw_im, pv_re, pv_im, d_skip, tc):
    seq = proj.shape[0]
    nc = seq // tc
    ls = tc // SUBLANES
    nb = SCAN_BLOCKS

    def kern(us_ref, dys_ref, ecr_ref, eci_ref, bbr_ref, bbi_ref, cmr_ref, cmi_ref, ar_ref, ai_ref,
             pwr_ref, pwi_ref, pvr_ref, pvi_ref, d_ref,
             dus_ref, dbbr_ref, dbbi_ref, dcmr_ref, dcmi_ref, dar_ref, dai_ref, dd_ref,
             bur, bui, xr, xi, gr, gi, fc_r, fc_i, a_bbr, a_bbi, a_cmr, a_cmi, a_ar, a_ai, a_dd, upb, dpb, hb_r, hb_i,
             *nat):
        c = pl.program_id(1)

        @pl.when(c == 0)
        def _():
            for ref in (fc_r, fc_i, a_bbr, a_bbi, a_cmr, a_cmi, a_ar, a_ai, a_dd):
                ref[...] = jnp.zeros_like(ref)

        for j in range(nb):
            cols = pl.ds(j * LANES, LANES)
            scols = pl.ds(j * STATE_W, STATE_W)
            nat_u, nat_d = nat[2 * j], nat[2 * j + 1]
            nat_u[...] = us_ref[:, cols].astype(F32)
            nat_d[...] = dys_ref[:, cols]
            for i in range(ls):
                rows_i = pl.ds(SUBLANES * i, SUBLANES)
                upb[j, rows_i, :] = nat_u[pl.ds(i, SUBLANES, stride=ls), :]
                dpb[j, rows_i, :] = nat_d[pl.ds(i, SUBLANES, stride=ls), :]
            u = upb[j]
            dysv = dpb[j]
            a_dd[j] += _acc8(dysv * u)
            up = u.astype(BF16)
            bur[j] = _dot(up, bbr_ref[j])
            bui[j] = _dot(up, bbi_ref[j])
            a_re = jnp.broadcast_to(ar_ref[:, scols], (SUBLANES, STATE_W))
            a_im = jnp.broadcast_to(ai_ref[:, scols], (SUBLANES, STATE_W))
            ec_r = ecr_ref[:, scols]
            ec_i = eci_ref[:, scols]
            xr[j, 0:SUBLANES, :] = ec_r
            xi[j, 0:SUBLANES, :] = ec_i
            _local_scan(a_re, a_im, bur.at[j], bui.at[j], xr.at[j], xi.at[j], SUBLANES, ls, False, init=(ec_r, ec_i),
                        xb=(hb_r.at[j], hb_i.at[j]))
            dysp = dysv.astype(BF16)
            a_cmr[j] += _dot_tn(dysp, hb_r[j])
            a_cmi[j] -= _dot_tn(dysp, hb_i[j])
            gr[j] = _dot_nt(dysp, cmr_ref[j])
            gi[j] = -_dot_nt(dysp, cmi_ref[j])
            _local_scan(a_re, -a_im, gr.at[j], gi.at[j], gr.at[j], gi.at[j], 0, ls, True)
            big_re = pwr_ref[tc - 1:tc, scols]
            big_im = -pwi_ref[tc - 1:tc, scols]
            f_re = fc_r[j, SUBLANES - 1:SUBLANES, :]
            f_im = fc_i[j, SUBLANES - 1:SUBLANES, :]
            for s in range(SUBLANES - 1, -1, -1):
                n_re = gr[j, s:s + 1, :] + big_re * f_re - big_im * f_im
                n_im = gi[j, s:s + 1, :] + big_re * f_im + big_im * f_re
                f_re, f_im = n_re, n_im
                if s > 0:
                    fc_r[j, s - 1:s, :] = f_re
                    fc_i[j, s - 1:s, :] = f_im
            f2_r, f2_i = _two(fc_r[j]), _two(fc_i[j])
            acc_r = jnp.zeros((SUBLANES, STATE_W), F32)
            acc_i = jnp.zeros((SUBLANES, STATE_W), F32)
            for k in range(tc // SLAB):
                rows_k = pl.ds(SLAB * k, SLAB)
                q_re = pvr_ref[rows_k, scols]
                q_im = pvi_ref[rows_k, scols]
                lam_re = gr[j, rows_k, :] + q_re * f2_r + q_im * f2_i
                lam_im = gi[j, rows_k, :] + q_re * f2_i - q_im * f2_r
                xp_re = xr[j, rows_k, :]
                xp_im = xi[j, rows_k, :]
                d_r = lam_re * xp_re + lam_im * xp_im
                d_i = lam_im * xp_re - lam_re * xp_im
                acc_r = acc_r + (d_r[0:SUBLANES] + d_r[SUBLANES:])
                acc_i = acc_i + (d_i[0:SUBLANES] + d_i[SUBLANES:])
                hb_r[j, rows_k, :] = lam_re.astype(BF16)
                hb_i[j, rows_k, :] = lam_im.astype(BF16)
            a_ar[j] += acc_r
            a_ai[j] += acc_i
            fc_r[j, SUBLANES - 1:SUBLANES, :] = f_re
            fc_i[j, SUBLANES - 1:SUBLANES, :] = f_im
            lb_re = hb_r[j]
            lb_im = hb_i[j]
            a_bbr[j] += _dot_tn(up, lb_re)
            a_bbi[j] += _dot_tn(up, lb_im)
            dpb[j] = _dot_nt(lb_re, bbr_ref[j]) + _dot_nt(lb_im, bbi_ref[j]) + dysv * d_ref[:, cols]
            for i in range(ls):
                nat_d[pl.ds(i, SUBLANES, stride=ls), :] = dpb[j, pl.ds(SUBLANES * i, SUBLANES), :]
            dus_ref[:, cols] = nat_d[...].astype(BF16)

        @pl.when(c == nc - 1)
        def _():
            row_g = lax.broadcasted_iota(jnp.int32, (LANES, STATE_W), 0) // SSM_H
            col_g = lax.broadcasted_iota(jnp.int32, (LANES, STATE_W), 1) // SSM_P
            fold = (lax.broadcasted_iota(jnp.int32, (STATE_W, SSM_P), 0) % SSM_P
                    == lax.broadcasted_iota(jnp.int32, (STATE_W, SSM_P), 1)).astype(BF16)
            for j in range(nb):
                rows_j = pl.ds(j * LANES, LANES)
                for acc, out in ((a_bbr, dbbr_ref), (a_bbi, dbbi_ref), (a_cmr, dcmr_ref), (a_cmi, dcmi_ref)):
                    out[rows_j, :] = _unpermute_rhs(jnp.where(row_g == col_g, acc[j], 0.0), fold)
                dar_ref[:, pl.ds(j * STATE_W, STATE_W)] = jnp.sum(a_ar[j], axis=0, keepdims=True)
                dai_ref[:, pl.ds(j * STATE_W, STATE_W)] = jnp.sum(a_ai[j], axis=0, keepdims=True)
                dd_ref[:, pl.ds(j * LANES, LANES)] = jnp.sum(a_dd[j], axis=0, keepdims=True)

    sp = _scan_specs(tc, nb, lambda c: nc - 1 - c)
    ghp = pl.BlockSpec((nb * LANES, SSM_P), lambda b, c: (b, 0))
    ghp_shape = jax.ShapeDtypeStruct((SSM_G * SSM_H, SSM_P), F32)
    small = pltpu.VMEM((nb, SUBLANES, STATE_W), F32)
    big = pltpu.VMEM((nb, tc, STATE_W), F32)
    bigp = pltpu.VMEM((nb, tc + SUBLANES, STATE_W), F32)
    blk = pltpu.VMEM((nb, LANES, STATE_W), F32)
    tok = pltpu.VMEM((nb, tc, LANES), F32)
    return pl.pallas_call(
        kern, name="ssm_scan_bwd", grid=(LANE_BLOCKS // nb, nc),
        in_specs=[sp["us"], sp["tok"], sp["car"], sp["car"], sp["bblk"], sp["bblk"], sp["cblk"], sp["cblk"],
                  sp["vec"], sp["vec"], sp["tab"], sp["tab"], sp["tab"], sp["tab"], sp["dvec"]],
        out_specs=(sp["tok"], ghp, ghp, ghp, ghp, sp["vec"], sp["vec"], sp["dvec"]),
        out_shape=(jax.ShapeDtypeStruct((seq, D_MODEL), BF16), ghp_shape, ghp_shape, ghp_shape, ghp_shape,
                   jax.ShapeDtypeStruct((1, STATE_ALL), F32), jax.ShapeDtypeStruct((1, STATE_ALL), F32),
                   jax.ShapeDtypeStruct((1, D_MODEL), F32)),
        scratch_shapes=[big, big, bigp, bigp, big, big, small, small, blk, blk, blk, blk,
                        small, small, pltpu.VMEM((nb, SUBLANES, LANES), F32), tok, tok,
                        pltpu.VMEM((nb, tc, STATE_W), BF16), pltpu.VMEM((nb, tc, STATE_W), BF16)]
        + [pltpu.VMEM((tc, LANES), F32)] * (2 * nb),
        compiler_params=_cparams("parallel", "arbitrary"),
    )(proj, dys, ec_re, ec_im, bb_re, bb_im, cm_re, cm_im, abar_re, abar_im, pw_re, pw_im, pv_re, pv_im, d_skip)


def _eye5():
    return jnp.asarray(np.eye(GROUPS_PER_BLOCK, dtype=np.float32)[None, :, None, :, None])


def _embed_b(bb_t):
    t = bb_t.transpose(1, 0, 2).reshape(LANE_BLOCKS, GROUPS_PER_BLOCK, SSM_H, 1, SSM_P)
    return (t * _eye5()).reshape(LANE_BLOCKS, LANES, STATE_W)


def _embed_c(c_ghp):
    t = c_ghp.transpose(0, 2, 1).reshape(LANE_BLOCKS, GROUPS_PER_BLOCK, SSM_P, 1, SSM_H)
    return (t * _eye5()).reshape(LANE_BLOCKS, STATE_W, LANES)


def _local_step(x, c_row, tgt, w_ada_bf, b_ada, g1, g2, w_in_bf, pool_w_bf, pscale, a_re, a_im, log_dt,
                b_re_t, b_im_t, c_re, c_im, d_skip, glu_w_bf, glu_b, wbp_bf, wbs_bf, wout_bf,
                late_weights=None, ride_for_dw_in=None, ride_for_dh=None):
    seq = x.shape[0]
    tc = min(SCAN_CHUNK, seq)
    mod8, silu_c = _mod_kernel(c_row, w_ada_bf, b_ada)
    mod = mod8[0:1]
    shift, scale, gate = mod[:, 0:D_MODEL], mod[:, D_MODEL:2 * D_MODEL], mod[:, 2 * D_MODEL:]

    abar_re, abar_im, bb_re_t, bb_im_t = _ssm_params(a_re, a_im, log_dt, b_re_t, b_im_t)
    abar_re_f, abar_im_f = abar_re.reshape(1, STATE_ALL), abar_im.reshape(1, STATE_ALL)
    pw_re, pw_im, pv_re, pv_im = _pow_tables(abar_re_f, abar_im_f, tc)
    bbe_re, bbe_im = _embed_b(bb_re_t).astype(BF16), _embed_b(bb_im_t).astype(BF16)
    cme_re, cme_im = _embed_c(c_re).astype(BF16), _embed_c(c_im).astype(BF16)
    d_row = d_skip.reshape(1, D_MODEL)

    h, h_t = _in_norm(x, g1, scale, shift)
    if late_weights:
        proj, *gathered = _mm([h], [w_in_bf], name="proj", out_dtype=BF16, bm=1024, bn=1024, bk=1024,
                              ride=late_weights[0])
        pool_w_bf, glu_w_bf, wbp_bf, wbs_bf, wout_bf = late_weights[1](*gathered)
    else:
        proj = _mm([h], [w_in_bf], name="proj", out_dtype=BF16, bm=1024, bn=1024, bk=1024)
    ypool, ypool_t = _pool_fwd(proj, pool_w_bf, pscale)
    ys, ec_re, ec_im = _ssm_scan_fwd(proj, bbe_re, bbe_im, cme_re, cme_im, abar_re_f, abar_im_f,
                                      pw_re, pw_im, d_row, tc)
    yssm, yssm_t = _glu_fwd(ys, proj, glu_w_bf, glu_b)
    (dy, dypool, dyssm, d45, merged_t, dob, dbp, dbs, loss, dgate, dg2) = _out_fwd_bwd(
        ypool, yssm, proj, x, tgt, gate, g2, wbp_bf, wbs_bf, wout_bf)

    d_wout = _mm([merged_t], [dob], name="dw_out", bm=1024, bn=1024, bk=1024)
    d_wbp = _mm([ypool_t], [dbp], name="dw_bp", bm=1024, bn=1024, bk=1024)
    d_wbs = _mm([yssm_t], [dbs], name="dw_bs", bm=1024, bn=1024, bk=1024)
    dys, dzs, dq, yg_t, d_glu_b = _glu_bwd(ys, proj, dyssm, glu_w_bf, glu_b)
    d_glu_w = _mm([yg_t], [dq], name="dw_glu", bm=1024, bn=1024, bk=1024)
    (dus, dbbe_re, dbbe_im, dcme_re, dcme_im, d_abar_re, d_abar_im, d_dskip) = _ssm_scan_bwd(
        proj, dys, ec_re, ec_im, bbe_re, bbe_im, cme_re, cme_im, abar_re_f, abar_im_f,
        pw_re, pw_im, pv_re, pv_im, d_row, tc)
    d01, d_pool_w, d_pscale = _pool_bwd(proj, dypool, pool_w_bf, pscale)
    dparts = [d01, dus, dzs, d45]
    small_ready = dict(
        dg2=dg2, d_pscale=d_pscale, d_glu_b=d_glu_b, d_dskip=d_dskip, d_abar_re=d_abar_re, d_abar_im=d_abar_im,
        d_bb_re_t=dbbe_re.reshape(SSM_G, SSM_H, SSM_P).transpose(1, 0, 2),
        d_bb_im_t=dbbe_im.reshape(SSM_G, SSM_H, SSM_P).transpose(1, 0, 2),
        d_c_re=dcme_re.reshape(SSM_G, SSM_H, SSM_P), d_c_im=dcme_im.reshape(SSM_G, SSM_H, SSM_P))
    ride = ride_for_dw_in(small_ready) if ride_for_dw_in else None
    d_win = _mm([h_t], dparts, name="dw_in", bm=1024, bn=1024, bk=1024, ride=ride)
    rode_dw_in = ()
    if ride:
        d_win, rode_dw_in = d_win[0], tuple(d_win[1:])
    big_grads = dict(d_win=d_win, d_glu_w=d_glu_w, d_wbp=d_wbp, d_wbs=d_wbs, d_wout=d_wout, d_pool_w=d_pool_w)
    ride = ride_for_dh(big_grads) if ride_for_dh else None
    dh = _mm(dparts, [w_in_bf], tb=True, name="dh", bm=1024, bn=1024, bk=1024, ride=ride)
    rode = ()
    if ride:
        dh, rode = dh[0], tuple(dh[1:])
    grad_x, dshift, dscale, dg1 = _in_bwd(dh, x, dy, g1, scale)
    dmod = jnp.concatenate([dshift, dscale, dgate], axis=1)
    return dict(
        rode=rode, rode_dw_in=rode_dw_in, loss=loss[0, 0], grad_x=grad_x, dmod=dmod, silu_c=silu_c, dg1=dg1,
        **small_ready, **big_grads)


def _position():
    x, y, c = lax.axis_index("x"), lax.axis_index("y"), lax.axis_index("c")
    chips = [(1 - x, y), (x, 1 - y), (1 - x, 1 - y)]
    return x, y, c, chips


_ANY = pl.BlockSpec(memory_space=pl.ANY)
COMM_CHUNKS = 4
COMM_ROW_ALIGN = 16


def _row_chunks(rows, k):
    assert rows % (k * COMM_ROW_ALIGN) == 0, (rows, k)
    step = rows // k
    return [(q * step, step) for q in range(k)]


def _ag_weights_ride(packed, n_chunks=COMM_CHUNKS):
    rows, width = packed.shape
    half = rows // 2
    chunks = _row_chunks(half, n_chunks)
    nq = len(chunks)

    def parts(p_ref, out_ref, send_sems, recv_sems):
        x, y, c, chips = _position()
        sibling = (x, y, 1 - c)

        def copy(k, chip, h, q, to, src=None):
            start, size = chunks[q]
            rows_q = pl.ds(h * half + start, size)
            dst = out_ref.at[2 * chip[0] + chip[1], rows_q, :]
            return pltpu.make_async_remote_copy(
                src_ref=dst if src is None else src.at[rows_q, :], dst_ref=dst, send_sem=send_sems.at[k * nq + q],
                recv_sem=recv_sems.at[k * nq + q], device_id=to, device_id_type=MESH_ID)

        mine = [copy(6 + h, (x, y), h, q, sibling, src=p_ref) for h in range(2) for q in range(nq)]
        first = [copy(j, (x, y), c, q, (*chip, c), src=p_ref) for q in range(nq) for j, chip in enumerate(chips)]
        return (x, y, c), chips, sibling, copy, mine, first

    def start(ins, outs, sems):
        _, _, _, _, mine, first = parts(ins[0], outs[0], sems[0], sems[1])
        for cp in first + mine:
            cp.start()

    def wait(ins, outs, sems):
        (x, y, c), chips, sibling, copy, mine, first = parts(ins[0], outs[0], sems[0], sems[1])
        passed = []
        for q in range(nq):
            for j, chip in enumerate(chips):
                copy(j, chip, c, q, (x, y, c)).wait_recv()
                fwd = copy(3 + j, chip, c, q, sibling)
                fwd.start()
                passed.append(fwd)
        for q in range(nq):
            for j, chip in enumerate(chips):
                copy(3 + j, chip, 1 - c, q, (x, y, c)).wait_recv()
        for cp in mine:
            cp.wait_recv()
        for cp in first + passed + mine:
            cp.wait_send()

    return _Ride([packed], [jax.ShapeDtypeStruct((N_CHIPS, rows, width), packed.dtype)],
                 [pltpu.SemaphoreType.DMA((8 * nq,)), pltpu.SemaphoreType.DMA((8 * nq,))], start, wait)


def _join_rides(rides):
    def split(seq, counts):
        out, at = [], 0
        for n in counts:
            out.append(seq[at:at + n])
            at += n
        return out

    n_in = [len(r.inputs) for r in rides]
    n_out = [len(r.out_shapes) for r in rides]
    n_sem = [len(r.scratch) for r in rides]

    def start(ins, outs, sems):
        for r, i, o, s in zip(rides, split(ins, n_in), split(outs, n_out), split(sems, n_sem)):
            r.start(i, o, s)

    def wait(ins, outs, sems):
        for r, i, o, s in zip(rides, split(ins, n_in), split(outs, n_out), split(sems, n_sem)):
            r.wait(i, o, s)

    return _Ride([a for r in rides for a in r.inputs], [a for r in rides for a in r.out_shapes],
                 [a for r in rides for a in r.scratch], start, wait)


def _run_ride(ride, name):
    n_in, n_out = len(ride.inputs), len(ride.out_shapes)

    def body(*refs):
        ins, outs, sems = refs[:n_in], refs[n_in:n_in + n_out], refs[n_in + n_out:]
        ride.start(ins, outs, sems)
        ride.wait(ins, outs, sems)

    return pl.pallas_call(
        body, name=name, in_specs=[_ANY] * n_in, out_specs=(_ANY,) * n_out, out_shape=tuple(ride.out_shapes),
        scratch_shapes=list(ride.scratch))(*ride.inputs)


def _small_allgather_ride(buf):
    rows, width = buf.shape
    chunks = _row_chunks(rows, COMM_CHUNKS)
    nq = len(chunks)

    def parts(b_ref, all_ref, send_sems, recv_sems, local_sem):
        x, y, c, chips = _position()
        me, sibling = (x, y, c), (x, y, 1 - c)

        def copy(k, block, q, to, src=None):
            rows_q = pl.ds(chunks[q][0], chunks[q][1])
            dst = all_ref.at[4 * block[0] + 2 * block[1] + block[2], rows_q, :]
            return pltpu.make_async_remote_copy(
                src_ref=dst if src is None else src.at[rows_q, :], dst_ref=dst, send_sem=send_sems.at[k * nq + q],
                recv_sem=recv_sems.at[k * nq + q], device_id=to, device_id_type=MESH_ID)

        mine = pltpu.make_async_copy(b_ref, all_ref.at[4 * x + 2 * y + c], local_sem)
        first = []
        for q in range(nq):
            first += [copy(1 + j, me, q, (*chip, c), src=b_ref) for j, chip in enumerate(chips)]
            first.append(copy(0, me, q, sibling, src=b_ref))
        return me, sibling, c, chips, copy, mine, first

    def start(ins, outs, sems):
        _, _, _, _, _, mine, first = parts(ins[0], outs[0], *sems)
        mine.start()
        for cp in first:
            cp.start()

    def wait(ins, outs, sems):
        me, sibling, c, chips, copy, mine, first = parts(ins[0], outs[0], *sems)
        passed = []
        for q in range(nq):
            for j, chip in enumerate(chips):
                copy(1 + j, (*chip, c), q, me).wait_recv()
                fwd = copy(4 + j, (*chip, c), q, sibling)
                fwd.start()
                passed.append(fwd)
        for q in range(nq):
            copy(0, sibling, q, me).wait_recv()
            for j, chip in enumerate(chips):
                copy(4 + j, (*chip, 1 - c), q, me).wait_recv()
        for cp in first + passed:
            cp.wait_send()
        mine.wait()

    return _Ride([buf], [jax.ShapeDtypeStruct((N_DEV, rows, width), F32)],
                 [pltpu.SemaphoreType.DMA((7 * nq,)), pltpu.SemaphoreType.DMA((7 * nq,)), pltpu.SemaphoreType.DMA],
                 start, wait)


def _sum_devices(blocks):
    n, rows, width = blocks.shape
    rb = rows // 2 if (rows // 2) % SUBLANES == 0 else rows

    def kern(b_ref, o_ref):
        total = b_ref[0]
        for d in range(1, n):
            total = total + b_ref[d]
        o_ref[...] = total

    return pl.pallas_call(
        kern, name="small_sum", grid=(rows // rb,), in_specs=[pl.BlockSpec((n, rb, width), lambda i: (0, i, 0))],
        out_specs=pl.BlockSpec((rb, width), lambda i: (i, 0)), out_shape=jax.ShapeDtypeStruct((rows, width), F32),
        compiler_params=_cparams("parallel"))(blocks)


def _small_allgather_sum(buf, head_rows, n_chunks=COMM_CHUNKS):
    rows, width = buf.shape
    chunks = _row_chunks(rows, n_chunks)
    nq = len(chunks)

    def body(b_ref, head_ref, sum_ref, all_ref, send_sems, recv_sems, local_sem):
        x, y, c, chips = _position()
        me, sibling = (x, y, c), (x, y, 1 - c)

        def slot(px, py, pc):
            return all_ref.at[4 * px + 2 * py + pc]

        def copy(k, block, q, to, src=None):
            rows_q = pl.ds(chunks[q][0], chunks[q][1])
            dst = slot(*block).at[rows_q, :]
            return pltpu.make_async_remote_copy(
                src_ref=dst if src is None else src.at[rows_q, :], dst_ref=dst, send_sem=send_sems.at[k * nq + q],
                recv_sem=recv_sems.at[k * nq + q], device_id=to, device_id_type=MESH_ID)

        mine = pltpu.make_async_copy(b_ref, slot(*me), local_sem)
        mine.start()
        first = []
        for q in range(nq):
            first += [copy(1 + j, me, q, (*chip, c), src=b_ref) for j, chip in enumerate(chips)]
            first.append(copy(0, me, q, sibling, src=b_ref))
        for cp in first:
            cp.start()
        passed = []
        for q in range(nq):
            for j, chip in enumerate(chips):
                copy(1 + j, (*chip, c), q, me).wait_recv()
                fwd = copy(4 + j, (*chip, c), q, sibling)
                fwd.start()
                passed.append(fwd)
        for q in range(nq):
            copy(0, sibling, q, me).wait_recv()
            for j, chip in enumerate(chips):
                copy(4 + j, (*chip, 1 - c), q, me).wait_recv()
        for cp in first + passed:
            cp.wait_send()
        mine.wait()
        total = all_ref[0]
        for d in range(1, N_DEV):
            total = total + all_ref[d]
        sum_ref[...] = total
        head_ref[...] = all_ref[:, 0:head_rows, :]

    vm = pl.BlockSpec(memory_space=pltpu.VMEM)
    return pl.pallas_call(
        body, name="small_allgather_sum", in_specs=[vm], out_specs=(vm, vm),
        out_shape=(jax.ShapeDtypeStruct((N_DEV, head_rows, width), F32), jax.ShapeDtypeStruct((rows, width), F32)),
        scratch_shapes=[pltpu.VMEM((N_DEV, rows, width), F32), pltpu.SemaphoreType.DMA((7 * nq,)),
                        pltpu.SemaphoreType.DMA((7 * nq,)), pltpu.SemaphoreType.DMA],
        compiler_params=_cparams(),
    )(buf)


def _rs_pair(g):
    n, rows, width = g.shape
    half = rows // 2
    chunks = _row_chunks(half, COMM_CHUNKS)
    nq = len(chunks)

    def body(g_ref, got_ref, send_sems, recv_sems):
        x, y, c, _ = _position()
        swaps = []
        for k in range(n):
            for q, (start, size) in enumerate(chunks):
                swaps.append(pltpu.make_async_remote_copy(
                    src_ref=g_ref.at[k, pl.ds((1 - c) * half + start, size), :], dst_ref=got_ref.at[k, pl.ds(start, size), :],
                    send_sem=send_sems.at[k * nq + q], recv_sem=recv_sems.at[k * nq + q],
                    device_id=(x, y, 1 - c), device_id_type=MESH_ID))
        for cp in swaps:
            cp.start()
        for cp in swaps:
            cp.wait()

    return pl.pallas_call(
        body, name="rs_pair", in_specs=[_ANY], out_specs=_ANY, out_shape=jax.ShapeDtypeStruct((n, half, width), g.dtype),
        scratch_shapes=[pltpu.SemaphoreType.DMA((n * nq,)), pltpu.SemaphoreType.DMA((n * nq,))],
    )(g)


def _rs_chips_ride(part_bf):
    n, rows, width = part_bf.shape
    chunks = _row_chunks(rows, COMM_CHUNKS)
    nq = len(chunks)

    def sends(pb_ref, got_ref, send_sems, recv_sems):
        x, y, c, chips = _position()
        out = []
        for q, (start, size) in enumerate(chunks):
            for j, chip in enumerate(chips):
                out.append(pltpu.make_async_remote_copy(
                    src_ref=pb_ref.at[2 * chip[0] + chip[1], pl.ds(start, size), :], dst_ref=got_ref.at[j, pl.ds(start, size), :],
                    send_sem=send_sems.at[j * nq + q], recv_sem=recv_sems.at[j * nq + q],
                    device_id=(*chip, c), device_id_type=MESH_ID))
        return out

    def start(ins, outs, sems):
        for cp in sends(ins[0], outs[0], sems[0], sems[1]):
            cp.start()

    def wait(ins, outs, sems):
        for cp in sends(ins[0], outs[0], sems[0], sems[1]):
            cp.wait()

    return _Ride([part_bf], [jax.ShapeDtypeStruct((N_CHIPS - 1, rows, width), BF16)],
                 [pltpu.SemaphoreType.DMA((3 * nq,)), pltpu.SemaphoreType.DMA((3 * nq,))], start, wait)


def _rs_join(shard):
    rows, width = shard.shape
    half = rows // 2
    chunks = _row_chunks(half, COMM_CHUNKS)
    nq = len(chunks)

    def body(in_ref, out_ref, send_sems, recv_sems):
        x, y, c, _ = _position()
        def swap(q, h):
            rows_q = pl.ds(h * half + chunks[q][0], chunks[q][1])
            return pltpu.make_async_remote_copy(
                src_ref=in_ref.at[rows_q, :], dst_ref=out_ref.at[rows_q, :], send_sem=send_sems.at[q],
                recv_sem=recv_sems.at[q], device_id=(x, y, 1 - c), device_id_type=MESH_ID)

        for q in range(nq):
            swap(q, c).start()
        for q in range(nq):
            swap(q, 1 - c).wait_recv()
        for q in range(nq):
            swap(q, c).wait_send()

    return pl.pallas_call(
        body, name="rs_join", in_specs=[_ANY], out_specs=_ANY, input_output_aliases={0: 0},
        out_shape=jax.ShapeDtypeStruct(shard.shape, shard.dtype),
        scratch_shapes=[pltpu.SemaphoreType.DMA((nq,)), pltpu.SemaphoreType.DMA((nq,))],
    )(shard)


def _pair_add(g, got, core):
    n, half, width = got.shape
    nb = 2
    rb = half // nb

    def kern(c_ref, a_ref, b_ref, f_ref, h_ref):
        s = a_ref[...] + b_ref[...]
        f_ref[...] = s
        h_ref[...] = s.astype(BF16)

    spec = pl.BlockSpec((1, rb, width), lambda k, i, c_ref: (k, i, 0))
    return pl.pallas_call(
        kern, name="rs_pair_add",
        grid_spec=pltpu.PrefetchScalarGridSpec(
            num_scalar_prefetch=1, grid=(n, nb),
            in_specs=[pl.BlockSpec((1, rb, width), lambda k, i, c_ref: (k, c_ref[0] * nb + i, 0)), spec],
            out_specs=(spec, spec)),
        out_shape=(jax.ShapeDtypeStruct(got.shape, F32), jax.ShapeDtypeStruct(got.shape, BF16)),
        compiler_params=_cparams("parallel", "parallel"))(core, g, got)


def _chip_add(part_f32, got, where):
    _, rows, width = part_f32.shape
    nb = 2
    rb = rows // nb

    def kern(w_ref, a_ref, b_ref, o_ref):
        o_ref[...] = ((a_ref[0] + b_ref[0].astype(F32)) + b_ref[1].astype(F32)) + b_ref[2].astype(F32)

    return pl.pallas_call(
        kern, name="rs_chip_add",
        grid_spec=pltpu.PrefetchScalarGridSpec(
            num_scalar_prefetch=1, grid=(nb,),
            in_specs=[pl.BlockSpec((1, rb, width), lambda i, w_ref: (w_ref[0], i, 0)),
                      pl.BlockSpec((N_CHIPS - 1, rb, width), lambda i, w_ref: (0, i, 0))],
            out_specs=pl.BlockSpec((rb, width), lambda i, w_ref: (w_ref[1] * nb + i, 0))),
        out_shape=jax.ShapeDtypeStruct((2 * rows, width), F32),
        compiler_params=_cparams("parallel"))(where, part_f32, got)


def _adamw(w, g, m, v, name):
    rows, width = w.shape
    rb = rows
    for cand in (512, 256, 128, 64, 32, 16, 8):
        if rows % cand == 0 and cand * width * 4 <= ADAM_BLOCK_BYTES:
            rb = cand
            break
    spec = pl.BlockSpec((rb, width), lambda i: (i, 0))

    def kern(w_ref, g_ref, m_ref, v_ref, d_ref, nm_ref, nv_ref):
        gv = g_ref[...]
        nm = ADAM_B1 * m_ref[...] + (1.0 - ADAM_B1) * gv
        nv = ADAM_B2 * v_ref[...] + (1.0 - ADAM_B2) * (gv * gv)
        m_hat = nm / (1.0 - ADAM_B1 ** ADAM_STEP)
        v_hat = nv / (1.0 - ADAM_B2 ** ADAM_STEP)
        d_ref[...] = -ADAM_LR * (m_hat / (jnp.sqrt(v_hat) + ADAM_EPS) + ADAM_WD * w_ref[...])
        nm_ref[...] = nm
        nv_ref[...] = nv

    shp = jax.ShapeDtypeStruct(w.shape, F32)
    return pl.pallas_call(
        kern, name=name, grid=(rows // rb,), in_specs=[spec] * 4, out_specs=(spec, spec, spec),
        out_shape=(shp, shp, shp), compiler_params=_cparams("parallel"))(w, g, m, v)


def _wada_grad(silu_t, dmod_cols):
    n = dmod_cols.shape[1]

    def kern(s_ref, d_ref, o_ref):
        acc = s_ref[:, 0:1] * d_ref[0:1, :]
        for b in range(1, N_DEV):
            acc = acc + s_ref[:, b:b + 1] * d_ref[b:b + 1, :]
        o_ref[...] = acc

    return pl.pallas_call(kern, name="wada_grad", out_shape=jax.ShapeDtypeStruct((D_MODEL, n), F32),
                          compiler_params=_cparams())(silu_t, dmod_cols)


def _rows(a, multiple):
    flat = a.reshape(-1)
    pad = (-flat.shape[0]) % (D_MODEL * multiple)
    if pad:
        flat = jnp.concatenate([flat, jnp.zeros((pad,), flat.dtype)])
    return flat.reshape(-1, D_MODEL)


def _part_rows(shape, multiple):
    return -(-int(np.prod(shape)) // (D_MODEL * multiple)) * multiple


def _pack_rows(parts, multiple, total_multiple=1):
    blocks = [_rows(p, multiple) for p in parts]
    pad = (-sum(b.shape[0] for b in blocks)) % total_multiple
    if pad:
        blocks.append(jnp.zeros((pad, D_MODEL), blocks[0].dtype))
    return jnp.concatenate(blocks, axis=0)


def _unpack_rows(buf, shapes, multiple):
    out, r = [], 0
    for shp in shapes:
        n = int(np.prod(shp))
        nr = _part_rows(shp, multiple)
        out.append(buf[r:r + nr].reshape(-1)[:n].reshape(shp))
        r += nr
    return out


def kernel(x, c, w_ada, b_ada, norm_pre, norm_post, w_in, pool_w, pool_scale, ssm_a_re, ssm_a_im, ssm_log_dt, ssm_b_re, ssm_b_im, ssm_c_re, ssm_c_im, ssm_d, glu_w, glu_b, w_branch_pool, w_branch_ssm, w_out, loss_target, m_w_ada, m_b_ada, m_norm_pre, m_norm_post, m_w_in, m_pool_w, m_pool_scale, m_ssm_a_re, m_ssm_a_im, m_ssm_log_dt, m_ssm_b_re, m_ssm_b_im, m_ssm_c_re, m_ssm_c_im, m_ssm_d, m_glu_w, m_glu_b, m_w_branch_pool, m_w_branch_ssm, m_w_out, v_w_ada, v_b_ada, v_norm_pre, v_norm_post, v_w_in, v_pool_w, v_pool_scale, v_ssm_a_re, v_ssm_a_im, v_ssm_log_dt, v_ssm_b_re, v_ssm_b_im, v_ssm_c_re, v_ssm_c_im, v_ssm_d, v_glu_w, v_glu_b, v_w_branch_pool, v_w_branch_ssm, v_w_out):
    n_ada = w_ada.shape[2]
    n_in = w_in.shape[2]
    n_row = glu_w.shape[1]
    n_pool = pool_w.shape[2]
    n_groups = pool_w.shape[1]

    g_ada, g_in = _run_ride(_join_rides([_ag_weights_ride(w_ada[0].astype(BF16)), _ag_weights_ride(w_in[0].astype(BF16))]),
                            "ag_weights")
    w_ada_bf = g_ada.transpose(1, 0, 2).reshape(D_MODEL, N_CHIPS * n_ada)
    w_in_bf = g_in.transpose(1, 0, 2).reshape(D_MODEL, N_CHIPS * n_in)
    pool_rows = n_groups * n_pool * POOL_GW // D_MODEL
    late_shards = [pool_w[0].reshape(n_groups * n_pool, POOL_GW), glu_w[0], w_branch_pool[0], w_branch_ssm[0], w_out[0]]
    late_ride = _join_rides([_ag_weights_ride(s.astype(BF16), n_chunks=2) for s in late_shards])

    def unpack_late(pool, *squares):
        pool = pool.reshape(N_CHIPS, n_groups, n_pool, POOL_GW).transpose(1, 0, 2, 3)
        return (pool.reshape(n_groups, POOL_GW, POOL_GW), *[s.reshape(D_MODEL, D_MODEL) for s in squares])

    chip = 2 * lax.axis_index("x") + lax.axis_index("y")
    core = lax.axis_index("c").astype(jnp.int32)
    kept = {}

    def by_cols(a, n):
        return a.reshape(D_MODEL, N_CHIPS, n).transpose(1, 0, 2).reshape(N_CHIPS, -1, D_MODEL)

    def by_rows(a):
        return a.reshape(N_CHIPS, n_row, D_MODEL)

    def exchange_big(g):
        pool_by_chip = g["d_pool_w"].reshape(n_groups, N_CHIPS, n_pool, POOL_GW).transpose(1, 0, 2, 3)
        blocks = [by_cols(g["d_win"], n_in), by_rows(g["d_glu_w"]), by_rows(g["d_wbp"]), by_rows(g["d_wbs"]),
                  by_rows(g["d_wout"]), pool_by_chip.reshape(N_CHIPS, pool_rows, D_MODEL)]
        pad = (-sum(b.shape[1] for b in blocks)) % (2 * COMM_CHUNKS * COMM_ROW_ALIGN)
        if pad:
            blocks.append(jnp.zeros((N_CHIPS, pad, D_MODEL), F32))
        g_packed = jnp.concatenate(blocks, axis=1)
        kept["part_f32"], part_bf = _pair_add(g_packed, _rs_pair(g_packed), core.reshape(1))
        return _rs_chips_ride(part_bf)

    a_re, a_im, log_dt = ssm_a_re[0], ssm_a_im[0], ssm_log_dt[0].reshape(SSM_G, 1)
    b_re_t, b_im_t = ssm_b_re[0].transpose(2, 0, 1), ssm_b_im[0].transpose(2, 0, 1)
    early_names = ["dg2", "d_pscale", "d_glu_b", "d_dskip", "d_abar_re", "d_abar_im", "d_bb_re_t", "d_bb_im_t",
                   "d_c_re", "d_c_im"]

    def exchange_small(s):
        parts = [s[k] for k in early_names]
        kept["early_shapes"] = [p.shape for p in parts]
        return _small_allgather_ride(_pack_rows(parts, SUBLANES, COMM_CHUNKS * COMM_ROW_ALIGN))

    res = _local_step(x[0], c, loss_target[0], w_ada_bf, b_ada, norm_pre, norm_post, w_in_bf, None, pool_scale,
                      a_re, a_im, log_dt, b_re_t, b_im_t, ssm_c_re[0], ssm_c_im[0], ssm_d[0], None, glu_b[0:1],
                      None, None, None, late_weights=(late_ride, unpack_late),
                      ride_for_dw_in=exchange_small, ride_for_dh=exchange_big)
    loss = lax.psum(res["loss"], ("x", "y", "c"))

    (all_early,) = res["rode_dw_in"]
    (g_norm_post, g_pscale, g_glu_b, g_dskip, s_abar_re, s_abar_im, s_bb_re, s_bb_im, g_c_re, g_c_im) = _unpack_rows(
        _sum_devices(all_early), kept["early_shapes"], SUBLANES)
    g_a_re, g_a_im, g_log_dt, g_b_re_t, g_b_im_t = _ssm_params_bwd(
        a_re, a_im, log_dt, b_re_t, b_im_t, s_abar_re.reshape(SSM_G, SSM_P), s_abar_im.reshape(SSM_G, SSM_P),
        s_bb_re, s_bb_im)
    late_parts = [res["dmod"], res["silu_c"], res["dg1"]]
    late_shapes = [p.shape for p in late_parts]
    head_rows = _part_rows(late_shapes[0], SUBLANES) + _part_rows(late_shapes[1], SUBLANES)
    all_late, sum_late = _small_allgather_sum(_pack_rows(late_parts, SUBLANES, COMM_ROW_ALIGN), head_rows, n_chunks=1)
    g_b_ada, _, g_norm_pre = _unpack_rows(sum_late, late_shapes, SUBLANES)
    dmod_all = all_late[:, 0:3].reshape(N_DEV, 3 * D_MODEL)
    dmod_cols = lax.dynamic_slice_in_dim(dmod_all, chip * n_ada, n_ada, axis=1)
    silu_t = all_late[:, _part_rows(late_shapes[0], SUBLANES)].transpose(1, 0)
    g_w_ada = _wada_grad(silu_t, dmod_cols)

    (got_chips,) = res["rode"]
    shard = _rs_join(_chip_add(kept["part_f32"], got_chips, jnp.stack([chip.astype(jnp.int32), core])))
    r = 0
    g_w_in = shard[r:r + n_in].reshape(D_MODEL, n_in)
    r += n_in
    g_squares = []
    for _ in range(4):
        g_squares.append(shard[r:r + n_row])
        r += n_row
    g_glu_w, g_wbp, g_wbs, g_wout = g_squares
    g_pool_w = shard[r:r + pool_rows].reshape(n_groups * n_pool, POOL_GW)

    big = [("w_ada", w_ada[0], g_w_ada, m_w_ada[0], v_w_ada[0]),
           ("w_in", w_in[0], g_w_in, m_w_in[0], v_w_in[0]),
           ("pool_w", pool_w[0].reshape(n_groups * n_pool, POOL_GW), g_pool_w,
            m_pool_w[0].reshape(n_groups * n_pool, POOL_GW), v_pool_w[0].reshape(n_groups * n_pool, POOL_GW)),
           ("glu_w", glu_w[0], g_glu_w, m_glu_w[0], v_glu_w[0]),
           ("w_branch_pool", w_branch_pool[0], g_wbp, m_w_branch_pool[0], v_w_branch_pool[0]),
           ("w_branch_ssm", w_branch_ssm[0], g_wbs, m_w_branch_ssm[0], v_w_branch_ssm[0]),
           ("w_out", w_out[0], g_wout, m_w_out[0], v_w_out[0])]
    out = {}
    for name, w_, g_, m_, v_ in big:
        d_, nm_, nv_ = _adamw(w_, g_, m_, v_, "adamw_" + name)
        out[name] = (g_, d_, nm_, nv_)

    g_b_re = g_b_re_t.transpose(1, 2, 0)
    g_b_im = g_b_im_t.transpose(1, 2, 0)
    small = [("b_ada", b_ada, g_b_ada, m_b_ada, v_b_ada),
             ("norm_pre", norm_pre, g_norm_pre, m_norm_pre, v_norm_pre),
             ("norm_post", norm_post, g_norm_post, m_norm_post, v_norm_post),
             ("pool_scale", pool_scale, g_pscale, m_pool_scale, v_pool_scale),
             ("ssm_a_re", ssm_a_re, g_a_re, m_ssm_a_re, v_ssm_a_re),
             ("ssm_a_im", ssm_a_im, g_a_im, m_ssm_a_im, v_ssm_a_im),
             ("ssm_log_dt", ssm_log_dt, g_log_dt, m_ssm_log_dt, v_ssm_log_dt),
             ("ssm_b_re", ssm_b_re, g_b_re, m_ssm_b_re, v_ssm_b_re),
             ("ssm_b_im", ssm_b_im, g_b_im, m_ssm_b_im, v_ssm_b_im),
             ("ssm_c_re", ssm_c_re, g_c_re, m_ssm_c_re, v_ssm_c_re),
             ("ssm_c_im", ssm_c_im, g_c_im, m_ssm_c_im, v_ssm_c_im),
             ("ssm_d", ssm_d, g_dskip, m_ssm_d, v_ssm_d),
             ("glu_b", glu_b, g_glu_b, m_glu_b, v_glu_b)]
    shapes = [w_.shape for _, w_, _, _, _ in small]
    pw_, pg_, pm_, pv_ = (_pack_rows([t[i] for t in small], SUBLANES) for i in (1, 2, 3, 4))
    pd_, pnm_, pnv_ = _adamw(pw_, pg_, pm_, pv_, "adamw_small")
    unpacked = [_unpack_rows(p, shapes, SUBLANES) for p in (pg_, pd_, pnm_, pnv_)]
    for (name, _, _, _, _), g_, d_, nm_, nv_ in zip(small, *unpacked):
        out[name] = (g_, d_, nm_, nv_)

    order = ["w_ada", "b_ada", "norm_pre", "norm_post", "w_in", "pool_w", "pool_scale", "ssm_a_re", "ssm_a_im",
             "ssm_log_dt", "ssm_b_re", "ssm_b_im", "ssm_c_re", "ssm_c_im", "ssm_d", "glu_w", "glu_b", "w_branch_pool",
             "w_branch_ssm", "w_out"]
    ref_shape = dict(w_ada=w_ada.shape, w_in=w_in.shape, pool_w=pool_w.shape, glu_w=glu_w.shape,
                     w_branch_pool=w_branch_pool.shape, w_branch_ssm=w_branch_ssm.shape, w_out=w_out.shape)
    for name, w_, _, _, _ in small:
        ref_shape[name] = w_.shape
    results = [loss, res["grad_x"][None]]
    for k in range(4):
        results += [out[name][k].reshape(ref_shape[name]) for name in order]
    return tuple(results)
```

```python
import functools
import math

import numpy as np
import jax
import jax.numpy as jnp
from jax import lax
from jax.experimental import pallas as pl
from jax.experimental.pallas import tpu as pltpu

F32 = jnp.float32
BF16 = jnp.bfloat16
MESH_ID = pl.DeviceIdType.MESH

D_MODEL = 1024
LANES = 128
SUBLANES = 8
SSM_G, SSM_P, SSM_H = 64, 64, 16
LANE_BLOCKS = D_MODEL // LANES
GROUPS_PER_BLOCK = LANES // SSM_H
STATE_W = GROUPS_PER_BLOCK * SSM_P
STATE_ALL = SSM_G * SSM_P
POOL_WINDOWS = (2, 4, 8, 16)
POOL_GW = D_MODEL // len(POOL_WINDOWS)
HALO = 16
RMS_EPS = 1e-6
N_CHIPS = 4
N_DEV = 8

SCAN_CHUNK = 512
SCAN_BLOCKS = 2
ROW_CHUNK = 256
ROW_CHUNK_WIDE = 512
VMEM_LIMIT_BYTES = 56 * 1024 * 1024

ADAM_BLOCK_BYTES = 1 << 20
ADAM_LR, ADAM_B1, ADAM_B2, ADAM_EPS, ADAM_WD, ADAM_STEP = 0.001, 0.9, 0.999, 1e-08, 0.01, 10

_GELU_C0 = math.sqrt(2.0 / math.pi)
_GELU_C1 = 0.044715


def _cparams(*sem):
    if sem:
        return pltpu.CompilerParams(dimension_semantics=sem, vmem_limit_bytes=VMEM_LIMIT_BYTES)
    return pltpu.CompilerParams(vmem_limit_bytes=VMEM_LIMIT_BYTES)


def _sigmoid(v):
    return jax.nn.sigmoid(v)


def _silu(v):
    return v * _sigmoid(v)


def _dsilu(v):
    s = _sigmoid(v)
    return s * (1.0 + v * (1.0 - s))


def _gelu(v):
    return 0.5 * v * (1.0 + jnp.tanh(_GELU_C0 * (v + _GELU_C1 * v * v * v)))


def _dgelu(v):
    t = jnp.tanh(_GELU_C0 * (v + _GELU_C1 * v * v * v))
    return 0.5 * (1.0 + t) + 0.5 * v * (1.0 - t * t) * _GELU_C0 * (1.0 + 3.0 * _GELU_C1 * v * v)


def _dot(a, b):
    return lax.dot_general(a, b, (((1,), (0,)), ((), ())), preferred_element_type=F32)


def _dot_nt(a, b):
    return lax.dot_general(a, b, (((1,), (1,)), ((), ())), preferred_element_type=F32)


def _dot_tn(a, b):
    return lax.dot_general(a, b, (((0,), (0,)), ((), ())), preferred_element_type=F32)


def _acc8(v):
    return v.reshape(v.shape[0] // SUBLANES, SUBLANES, v.shape[1]).sum(axis=0)


class _Ride:
    def __init__(self, inputs, out_shapes, scratch, start, wait):
        self.inputs, self.out_shapes, self.scratch, self.start, self.wait = inputs, out_shapes, scratch, start, wait


def _mm(a_parts, b_parts, *, name, ta=False, tb=False, out_dtype=F32, bm=512, bn=512, bk=512, ride=None):
    a_parts, b_parts = list(a_parts), list(b_parts)
    if ta:
        assert len(a_parts) == 1
        k_dim, m_dim = a_parts[0].shape
    else:
        m_dim = a_parts[0].shape[0]
        k_dim = sum(a.shape[1] for a in a_parts)
    if tb:
        assert len(b_parts) == 1
        n_dim = b_parts[0].shape[0]
    else:
        n_dim = sum(b.shape[1] for b in b_parts)
    bm, bn, bk = min(bm, m_dim), min(bn, n_dim), min(bk, k_dim)
    nm, nn, nk = m_dim // bm, n_dim // bn, k_dim // bk
    a_ranges, off = [], 0
    for a in a_parts:
        cnt = (a.shape[0] if ta else a.shape[1]) // bk
        a_ranges.append((off, cnt))
        off += cnt
    b_ranges, off = [], 0
    for b in b_parts:
        cnt = (b.shape[0] if tb else b.shape[1]) // bn
        b_ranges.append((off, cnt))
        off += cnt

    def a_spec(off, cnt):
        if ta:
            return pl.BlockSpec((bk, bm), lambda i, n, k: (k, i))
        return pl.BlockSpec((bm, bk), lambda i, n, k: (i, jnp.clip(k - off, 0, cnt - 1)))

    def b_spec(off, cnt):
        if tb:
            return pl.BlockSpec((bn, bk), lambda i, n, k: (n, k))
        return pl.BlockSpec((bk, bn), lambda i, n, k: (k, jnp.clip(n - off, 0, cnt - 1)))

    na, nb = len(a_parts), len(b_parts)
    dims = (((0 if ta else 1,), (1 if tb else 0,)), ((), ()))

    def kern_single(a_ref, b_ref, o_ref):
        o_ref[...] = lax.dot_general(a_ref[...].astype(BF16), b_ref[...].astype(BF16), dims,
                                     preferred_element_type=F32).astype(out_dtype)

    if na == 1 and nb == 1 and nk == 1 and not ride:
        return pl.pallas_call(
            kern_single, name=name, grid=(nm, nn),
            in_specs=[pl.BlockSpec((bk, bm), lambda i, n: (0, i)) if ta else pl.BlockSpec((bm, bk), lambda i, n: (i, 0)),
                      pl.BlockSpec((bn, bk), lambda i, n: (n, 0)) if tb else pl.BlockSpec((bk, bn), lambda i, n: (0, n))],
            out_specs=pl.BlockSpec((bm, bn), lambda i, n: (i, n)),
            out_shape=jax.ShapeDtypeStruct((m_dim, n_dim), out_dtype),
            compiler_params=_cparams("parallel", "parallel"),
        )(a_parts[0], b_parts[0])

    n_rin = len(ride.inputs) if ride else 0
    n_rout = len(ride.out_shapes) if ride else 0

    def kern(*refs):
        a_refs, b_refs = refs[:na], refs[na:na + nb]
        rin = refs[na + nb:na + nb + n_rin]
        o_ref = refs[na + nb + n_rin]
        rout = refs[na + nb + n_rin + 1:na + nb + n_rin + 1 + n_rout]
        acc = refs[na + nb + n_rin + 1 + n_rout]
        rsem = refs[na + nb + n_rin + 2 + n_rout:]
        i, n, k = pl.program_id(0), pl.program_id(1), pl.program_id(2)

        if ride:
            @pl.when((i == 0) & (n == 0) & (k == 0))
            def _():
                ride.start(rin, rout, rsem)

        @pl.when(k == 0)
        def _():
            acc[...] = jnp.zeros_like(acc)

        for ja, (koff, kcnt) in enumerate(a_ranges):
            for jb, (noff, ncnt) in enumerate(b_ranges):
                def step(ja=ja, jb=jb):
                    a = a_refs[ja][...].astype(BF16)
                    b = b_refs[jb][...].astype(BF16)
                    acc[...] += lax.dot_general(a, b, dims, preferred_element_type=F32)

                if na == 1 and nb == 1:
                    step()
                else:
                    cond = (k >= koff) & (k < koff + kcnt) & (n >= noff) & (n < noff + ncnt)
                    pl.when(cond)(step)

        @pl.when(k == nk - 1)
        def _():
            o_ref[...] = acc[...].astype(out_dtype)

        if ride:
            @pl.when((i == nm - 1) & (n == nn - 1) & (k == nk - 1))
            def _():
                ride.wait(rin, rout, rsem)

    any_spec = pl.BlockSpec(memory_space=pl.ANY)
    out_spec = pl.BlockSpec((bm, bn), lambda i, n, k: (i, n))
    out_shape = jax.ShapeDtypeStruct((m_dim, n_dim), out_dtype)
    if not ride:
        return pl.pallas_call(
            kern, name=name, grid=(nm, nn, nk),
            in_specs=[a_spec(*r) for r in a_ranges] + [b_spec(*r) for r in b_ranges],
            out_specs=out_spec, out_shape=out_shape, scratch_shapes=[pltpu.VMEM((bm, bn), F32)],
            compiler_params=_cparams("parallel", "parallel", "arbitrary"),
        )(*a_parts, *b_parts)
    return pl.pallas_call(
        kern, name=name, grid=(nm, nn, nk),
        in_specs=[a_spec(*r) for r in a_ranges] + [b_spec(*r) for r in b_ranges] + [any_spec] * n_rin,
        out_specs=(out_spec,) + (any_spec,) * n_rout, out_shape=(out_shape,) + tuple(ride.out_shapes),
        scratch_shapes=[pltpu.VMEM((bm, bn), F32)] + list(ride.scratch),
        compiler_params=_cparams("arbitrary", "arbitrary", "arbitrary"),
    )(*a_parts, *b_parts, *ride.inputs)


def _ssm_param_fn(a_re, a_im, log_dt, b_re, b_im):
    dt = jnp.exp(log_dt)
    lam_re = jnp.minimum(a_re, -1e-4)
    lam_im = a_im
    mag = jnp.exp(lam_re * dt)
    abar_re = mag * jnp.cos(lam_im * dt)
    abar_im = mag * jnp.sin(lam_im * dt)
    den = lam_re * lam_re + lam_im * lam_im
    num_re = abar_re - 1.0
    f_re = (num_re * lam_re + abar_im * lam_im) / den
    f_im = (abar_im * lam_re - num_re * lam_im) / den
    bb_re = f_re * b_re - f_im * b_im
    bb_im = f_re * b_im + f_im * b_re
    return abar_re, abar_im, bb_re, bb_im


def _ssm_params(a_re, a_im, log_dt, b_re_t, b_im_t):
    def kern(are, aim, ldt, bre, bim, o_ar, o_ai, o_br, o_bi):
        ar, ai, br, bi = _ssm_param_fn(are[...], aim[...], ldt[...], bre[...], bim[...])
        o_ar[...] = ar
        o_ai[...] = ai
        o_br[...] = br
        o_bi[...] = bi

    gp = jax.ShapeDtypeStruct((SSM_G, SSM_P), F32)
    hgp = jax.ShapeDtypeStruct((SSM_H, SSM_G, SSM_P), F32)
    return pl.pallas_call(kern, name="ssm_params", out_shape=(gp, gp, hgp, hgp), compiler_params=_cparams())(
        a_re, a_im, log_dt, b_re_t, b_im_t)


def _ssm_params_bwd(a_re, a_im, log_dt, b_re_t, b_im_t, d_ar, d_ai, d_bbr, d_bbi):
    def kern(are, aim, ldt, bre, bim, dar, dai, dbr, dbi, o_are, o_aim, o_ldt, o_bre, o_bim):
        prim = (are[...], aim[...], ldt[...], bre[...], bim[...])
        _, vjp = jax.vjp(_ssm_param_fn, *prim)
        g = vjp((dar[...], dai[...], dbr[...], dbi[...]))
        o_are[...] = g[0]
        o_aim[...] = g[1]
        o_ldt[...] = g[2]
        o_bre[...] = g[3]
        o_bim[...] = g[4]

    gp = jax.ShapeDtypeStruct((SSM_G, SSM_P), F32)
    g1 = jax.ShapeDtypeStruct((SSM_G, 1), F32)
    hgp = jax.ShapeDtypeStruct((SSM_H, SSM_G, SSM_P), F32)
    return pl.pallas_call(kern, name="ssm_params_bwd", out_shape=(gp, gp, g1, hgp, hgp), compiler_params=_cparams())(
        a_re, a_im, log_dt, b_re_t, b_im_t, d_ar, d_ai, d_bbr, d_bbi)


def _pow_tables(abar_re, abar_im, tc):
    ls = tc // SUBLANES

    def kern(ar_ref, ai_ref, fr_ref, fi_ref, rr_ref, ri_ref):
        a_re = jnp.broadcast_to(ar_ref[...], (SUBLANES, STATE_W))
        a_im = jnp.broadcast_to(ai_ref[...], (SUBLANES, STATE_W))
        p_re, p_im = a_re, a_im
        for i in range(ls):
            fwd = pl.ds(SUBLANES * i, SUBLANES)
            rev = pl.ds(SUBLANES * (ls - 1 - i), SUBLANES)
            fr_ref[fwd, :] = p_re
            fi_ref[fwd, :] = p_im
            rr_ref[rev, :] = p_re
            ri_ref[rev, :] = p_im
            p_re, p_im = p_re * a_re - p_im * a_im, p_re * a_im + p_im * a_re

    vec = pl.BlockSpec((1, STATE_W), lambda b: (0, b))
    tab = pl.BlockSpec((tc, STATE_W), lambda b: (0, b))
    shp = jax.ShapeDtypeStruct((tc, STATE_ALL), F32)
    return pl.pallas_call(
        kern, name="pow_tables", grid=(LANE_BLOCKS,), in_specs=[vec, vec], out_specs=(tab, tab, tab, tab),
        out_shape=(shp, shp, shp, shp), compiler_params=_cparams("parallel"))(abar_re, abar_im)


def _mod_kernel(c_row, w_ada_bf, b_ada):
    def kern(c_ref, w_ref, b_ref, m_ref, s_ref):
        cv = c_ref[...]
        sc = _silu(cv)
        s_ref[...] = sc
        lhs = jnp.broadcast_to(sc, (SUBLANES, D_MODEL)).astype(BF16)
        m_ref[...] = _dot(lhs, w_ref[...]) + b_ref[...]

    return pl.pallas_call(
        kern, name="ada_mod",
        out_shape=(jax.ShapeDtypeStruct((SUBLANES, 3 * D_MODEL), F32), jax.ShapeDtypeStruct((1, D_MODEL), F32)),
        compiler_params=_cparams())(c_row, w_ada_bf, b_ada)


def _row_spec(tr, width=D_MODEL, col=0):
    return pl.BlockSpec((tr, width), lambda c: (c, col))


def _vec_spec(width=D_MODEL):
    return pl.BlockSpec((1, width), lambda c: (0, 0))


def _col_spec(tr):
    return pl.BlockSpec((D_MODEL, tr), lambda c: (0, c))


def _in_norm(x, g1, scale, shift, ride=None):
    seq = x.shape[0]
    tr = min(ROW_CHUNK_WIDE, seq)
    nc = seq // tr
    n_rin = len(ride.inputs) if ride else 0
    n_rout = len(ride.out_shapes) if ride else 0

    def kern(x_ref, g_ref, sc_ref, sh_ref, *rest):
        rin, (h_ref, ht_ref) = rest[:n_rin], rest[n_rin:n_rin + 2]
        rout, rsem = rest[n_rin + 2:n_rin + 2 + n_rout], rest[n_rin + 2 + n_rout:]
        c = pl.program_id(0)
        if ride:
            @pl.when(c == 0)
            def _():
                ride.start(rin, rout, rsem)

        xv = x_ref[...]
        r = lax.rsqrt(jnp.mean(xv * xv, axis=-1, keepdims=True) + RMS_EPS)
        h = ((xv * r) * g_ref[...]) * (1.0 + sc_ref[...]) + sh_ref[...]
        h_ref[...] = h.astype(BF16)
        ht_ref[...] = h.T.astype(BF16)

        if ride:
            @pl.when(c == nc - 1)
            def _():
                ride.wait(rin, rout, rsem)

    outs = pl.pallas_call(
        kern, name="in_norm", grid=(nc,),
        in_specs=[_row_spec(tr), _vec_spec(), _vec_spec(), _vec_spec()] + [_ANY] * n_rin,
        out_specs=(_row_spec(tr), _col_spec(tr)) + (_ANY,) * n_rout,
        out_shape=(jax.ShapeDtypeStruct((seq, D_MODEL), BF16), jax.ShapeDtypeStruct((D_MODEL, seq), BF16))
        + tuple(ride.out_shapes if ride else ()),
        scratch_shapes=list(ride.scratch) if ride else [],
        compiler_params=_cparams("arbitrary" if ride else "parallel"))(x, g1, scale, shift, *(ride.inputs if ride else ()))
    return outs


def _pool_windows(ext, pos, g, w, tr):
    cols = pl.ds(g * POOL_GW, POOL_GW)
    cur = ext[pl.ds(HALO, tr), cols]
    acc = cur
    for k in range(1, w):
        acc = acc + ext[pl.ds(HALO - k, tr), cols]
    cnt = jnp.minimum(pos + 1, w).astype(F32)
    return acc / cnt - cur


def _pool_fwd(proj, pool_w_bf, pscale):
    seq = proj.shape[0]
    tr = min(ROW_CHUNK_WIDE, seq)
    hb = tr // HALO

    def kern(up_ref, halo_ref, zp_ref, pw_ref, ps_ref, y_ref, yt_ref, ext):
        c = pl.program_id(0)
        ext[0:HALO, :] = jnp.where(c > 0, halo_ref[...].astype(F32), 0.0)
        ext[HALO:, :] = up_ref[...].astype(F32)
        pos = c * tr + lax.broadcasted_iota(jnp.int32, (tr, POOL_GW), 0)
        for g, w in enumerate(POOL_WINDOWS):
            cols = pl.ds(g * POOL_GW, POOL_GW)
            pooled = _pool_windows(ext, pos, g, w, tr)
            mixed = _dot(pooled.astype(BF16), pw_ref[g])
            y = mixed * ps_ref[:, cols] * _silu(zp_ref[:, cols].astype(F32))
            y_ref[:, cols] = y.astype(BF16)
            yt_ref[cols, :] = y.T.astype(BF16)

    return pl.pallas_call(
        kern, name="pool_fwd", grid=(seq // tr,),
        in_specs=[_row_spec(tr, col=0),
                  pl.BlockSpec((HALO, D_MODEL), lambda c: (jnp.maximum(c * hb - 1, 0), 0)),
                  _row_spec(tr, col=1),
                  pl.BlockSpec((len(POOL_WINDOWS), POOL_GW, POOL_GW), lambda c: (0, 0, 0)),
                  _vec_spec()],
        out_specs=(_row_spec(tr), _col_spec(tr)),
        out_shape=(jax.ShapeDtypeStruct((seq, D_MODEL), BF16), jax.ShapeDtypeStruct((D_MODEL, seq), BF16)),
        scratch_shapes=[pltpu.VMEM((tr + HALO, D_MODEL), F32)],
        compiler_params=_cparams("parallel"))(proj, proj, proj, pool_w_bf, pscale)


def _pool_bwd(proj, dyp, pool_w_bf, pscale):
    seq = proj.shape[0]
    tr = min(ROW_CHUNK_WIDE, seq)
    hb = tr // HALO
    nc = seq // tr
    n_halo = seq // HALO

    def kern(up_ref, halo_ref, zp_ref, zpn_ref, dyp_ref, dypn_ref, pw_ref, ps_ref,
             d01_ref, dpw_ref, dps_ref, ext, dpn, acc_pw, acc_ps):
        c = pl.program_id(0)

        @pl.when(c == 0)
        def _():
            acc_pw[...] = jnp.zeros_like(acc_pw)
            acc_ps[...] = jnp.zeros_like(acc_ps)

        ext[0:HALO, :] = jnp.where(c > 0, halo_ref[...].astype(F32), 0.0)
        ext[HALO:, :] = up_ref[...].astype(F32)
        pos = c * tr + lax.broadcasted_iota(jnp.int32, (tr, POOL_GW), 0)
        pos_n = (c + 1) * tr + lax.broadcasted_iota(jnp.int32, (HALO, POOL_GW), 0)
        has_next = c < nc - 1
        for g, w in enumerate(POOL_WINDOWS):
            cols = pl.ds(g * POOL_GW, POOL_GW)
            pooled_bf = _pool_windows(ext, pos, g, w, tr).astype(BF16)
            wg = pw_ref[g]
            mixed = _dot(pooled_bf, wg)
            zp = zp_ref[:, cols].astype(F32)
            sz = _silu(zp)
            dyp_g = dyp_ref[:, cols]
            ps = ps_ref[:, cols]
            dmixed = (dyp_g * ps * sz).astype(BF16)
            acc_ps[:, cols] += _acc8(dyp_g * mixed * sz)
            d01_ref[:, pl.ds(D_MODEL + g * POOL_GW, POOL_GW)] = (dyp_g * mixed * ps * _dsilu(zp)).astype(BF16)
            acc_pw[g] += _dot_tn(pooled_bf, dmixed)
            dpooled = _dot_nt(dmixed, wg)
            dmixed_n = (jnp.where(has_next, dypn_ref[:, cols], 0.0) * ps * _silu(zpn_ref[:, cols].astype(F32))).astype(BF16)
            dpooled_n = _dot_nt(dmixed_n, wg)
            dpn[0:tr, :] = dpooled / jnp.minimum(pos + 1, w).astype(F32)
            dpn[tr:, :] = dpooled_n / jnp.minimum(pos_n + 1, w).astype(F32)
            acc = dpn[0:tr, :]
            for k in range(1, w):
                acc = acc + dpn[pl.ds(k, tr), :]
            d01_ref[:, cols] = (acc - dpooled).astype(BF16)

        @pl.when(c == nc - 1)
        def _():
            dpw_ref[...] = acc_pw[...]
            dps_ref[...] = jnp.sum(acc_ps[...], axis=0, keepdims=True)

    nxt = lambda c: (jnp.minimum((c + 1) * hb, n_halo - 1), 0)
    nxt1 = lambda c: (jnp.minimum((c + 1) * hb, n_halo - 1), 1)
    return pl.pallas_call(
        kern, name="pool_bwd", grid=(nc,),
        in_specs=[_row_spec(tr, col=0),
                  pl.BlockSpec((HALO, D_MODEL), lambda c: (jnp.maximum(c * hb - 1, 0), 0)),
                  _row_spec(tr, col=1),
                  pl.BlockSpec((HALO, D_MODEL), nxt1),
                  _row_spec(tr),
                  pl.BlockSpec((HALO, D_MODEL), nxt),
                  pl.BlockSpec((len(POOL_WINDOWS), POOL_GW, POOL_GW), lambda c: (0, 0, 0)),
                  _vec_spec()],
        out_specs=(pl.BlockSpec((tr, 2 * D_MODEL), lambda c: (c, 0)),
                   pl.BlockSpec((len(POOL_WINDOWS), POOL_GW, POOL_GW), lambda c: (0, 0, 0)),
                   _vec_spec()),
        out_shape=(jax.ShapeDtypeStruct((seq, 2 * D_MODEL), BF16),
                   jax.ShapeDtypeStruct((len(POOL_WINDOWS), POOL_GW, POOL_GW), F32),
                   jax.ShapeDtypeStruct((1, D_MODEL), F32)),
        scratch_shapes=[pltpu.VMEM((tr + HALO, D_MODEL), F32), pltpu.VMEM((tr + HALO, POOL_GW), F32),
                        pltpu.VMEM((len(POOL_WINDOWS), POOL_GW, POOL_GW), F32), pltpu.VMEM((SUBLANES, D_MODEL), F32)],
        compiler_params=_cparams("arbitrary"))(proj, proj, proj, proj, dyp, dyp, pool_w_bf, pscale)


def _glu_fwd(ys, proj, glu_w_bf, glu_b):
    seq = ys.shape[0]
    tr = min(ROW_CHUNK_WIDE, seq)

    def kern(ys_ref, zs_ref, w_ref, b_ref, o_ref, ot_ref):
        yg = _gelu(ys_ref[...])
        q = _dot(yg.astype(BF16), w_ref[...]) + b_ref[...]
        y = yg * _sigmoid(q) * _silu(zs_ref[...].astype(F32))
        o_ref[...] = y.astype(BF16)
        ot_ref[...] = y.T.astype(BF16)

    return pl.pallas_call(
        kern, name="glu_fwd", grid=(seq // tr,),
        in_specs=[_row_spec(tr), _row_spec(tr, col=3), pl.BlockSpec((D_MODEL, D_MODEL), lambda c: (0, 0)), _vec_spec()],
        out_specs=(_row_spec(tr), _col_spec(tr)),
        out_shape=(jax.ShapeDtypeStruct((seq, D_MODEL), BF16), jax.ShapeDtypeStruct((D_MODEL, seq), BF16)),
        compiler_params=_cparams("parallel"))(ys, proj, glu_w_bf, glu_b)


def _glu_bwd(ys, proj, dyssm, glu_w_bf, glu_b):
    seq = ys.shape[0]
    tr = min(ROW_CHUNK_WIDE, seq)
    nc = seq // tr

    def kern(ys_ref, zs_ref, dy_ref, w_ref, b_ref, dys_ref, dzs_ref, dq_ref, yg_ref, db_ref, acc_b):
        c = pl.program_id(0)

        @pl.when(c == 0)
        def _():
            acc_b[...] = jnp.zeros_like(acc_b)

        ysv = ys_ref[...]
        yg = _gelu(ysv)
        yg_bf = yg.astype(BF16)
        q = _dot(yg_bf, w_ref[...]) + b_ref[...]
        sg = _sigmoid(q)
        zs = zs_ref[...].astype(F32)
        dyv = dy_ref[...]
        dyglu = dyv * _silu(zs)
        dzs_ref[...] = (dyv * (yg * sg) * _dsilu(zs)).astype(BF16)
        dq = dyglu * yg * sg * (1.0 - sg)
        dq_bf = dq.astype(BF16)
        acc_b[...] += _acc8(dq)
        dyg = dyglu * sg + _dot_nt(dq_bf, w_ref[...])
        dys_ref[...] = dyg * _dgelu(ysv)
        dq_ref[...] = dq_bf
        yg_ref[...] = yg.T.astype(BF16)

        @pl.when(c == nc - 1)
        def _():
            db_ref[...] = jnp.sum(acc_b[...], axis=0, keepdims=True)

    bf = jax.ShapeDtypeStruct((seq, D_MODEL), BF16)
    return pl.pallas_call(
        kern, name="glu_bwd", grid=(nc,),
        in_specs=[_row_spec(tr), _row_spec(tr, col=3), _row_spec(tr),
                  pl.BlockSpec((D_MODEL, D_MODEL), lambda c: (0, 0)), _vec_spec()],
        out_specs=(_row_spec(tr), _row_spec(tr), _row_spec(tr), _col_spec(tr), _vec_spec()),
        out_shape=(jax.ShapeDtypeStruct((seq, D_MODEL), F32), bf, bf, jax.ShapeDtypeStruct((D_MODEL, seq), BF16),
                   jax.ShapeDtypeStruct((1, D_MODEL), F32)),
        scratch_shapes=[pltpu.VMEM((SUBLANES, D_MODEL), F32)],
        compiler_params=_cparams("arbitrary"))(ys, proj, dyssm, glu_w_bf, glu_b)


def _out_fwd_bwd(ypool, yssm, proj, x, tgt, gate, g2, wbp_bf, wbs_bf, wout_bf):
    seq = x.shape[0]
    tr = min(ROW_CHUNK, seq)
    nc = seq // tr

    def kern(yp_ref, ysm_ref, gp_ref, gs_ref, x_ref, t_ref, gate_ref, g2_ref, wbp_ref, wbs_ref, wo_ref,
             dy_ref, dyp_ref, dys_ref, d45_ref, mb_ref, dob_ref, dbp_ref, dbs_ref, loss_ref, dgate_ref, dg2_ref,
             acc_l, acc_gate, acc_g2):
        c = pl.program_id(0)

        @pl.when(c == 0)
        def _():
            acc_l[...] = jnp.zeros_like(acc_l)
            acc_gate[...] = jnp.zeros_like(acc_gate)
            acc_g2[...] = jnp.zeros_like(acc_g2)

        bp = _dot(yp_ref[...], wbp_ref[...])
        bs = _dot(ysm_ref[...], wbs_ref[...])
        sp = _sigmoid(gp_ref[...].astype(F32))
        ss = _sigmoid(gs_ref[...].astype(F32))
        merged = sp * bp + ss * bs
        mb = merged.astype(BF16)
        out = _dot(mb, wo_ref[...])
        r2 = lax.rsqrt(jnp.mean(out * out, axis=-1, keepdims=True) + RMS_EPS)
        oh = out * r2
        gate_v, g2_v = gate_ref[...], g2_ref[...]
        ohg = oh * g2_v
        diff = (x_ref[...] + gate_v * ohg) - t_ref[...]
        acc_l[...] += _acc8(diff * diff)
        dyv = diff * (1.0 / D_MODEL)
        dy_ref[...] = dyv
        acc_gate[...] += _acc8(dyv * ohg)
        t = dyv * gate_v
        acc_g2[...] += _acc8(t * oh)
        doh = t * g2_v
        dout = r2 * (doh - oh * jnp.mean(doh * oh, axis=-1, keepdims=True))
        dob = dout.astype(BF16)
        dmerged = _dot_nt(dob, wo_ref[...])
        dbp = (dmerged * sp).astype(BF16)
        dbs = (dmerged * ss).astype(BF16)
        d45_ref[:, 0:D_MODEL] = (dmerged * bp * sp * (1.0 - sp)).astype(BF16)
        d45_ref[:, D_MODEL:] = (dmerged * bs * ss * (1.0 - ss)).astype(BF16)
        dyp_ref[...] = _dot_nt(dbp, wbp_ref[...])
        dys_ref[...] = _dot_nt(dbs, wbs_ref[...])
        mb_ref[...] = merged.T.astype(BF16)
        dob_ref[...] = dob
        dbp_ref[...] = dbp
        dbs_ref[...] = dbs

        @pl.when(c == nc - 1)
        def _():
            tot = jnp.sum(acc_l[...], axis=0, keepdims=True)
            loss_ref[...] = jnp.sum(tot, axis=1, keepdims=True) * (0.5 / D_MODEL)
            dgate_ref[...] = jnp.sum(acc_gate[...], axis=0, keepdims=True)
            dg2_ref[...] = jnp.sum(acc_g2[...], axis=0, keepdims=True)

    wspec = pl.BlockSpec((D_MODEL, D_MODEL), lambda c: (0, 0))
    f32 = jax.ShapeDtypeStruct((seq, D_MODEL), F32)
    bf = jax.ShapeDtypeStruct((seq, D_MODEL), BF16)
    vec = jax.ShapeDtypeStruct((1, D_MODEL), F32)
    acc = pltpu.VMEM((SUBLANES, D_MODEL), F32)
    return pl.pallas_call(
        kern, name="out_fwd_bwd", grid=(nc,),
        in_specs=[_row_spec(tr), _row_spec(tr), _row_spec(tr, col=4), _row_spec(tr, col=5), _row_spec(tr), _row_spec(tr),
                  _vec_spec(), _vec_spec(), wspec, wspec, wspec],
        out_specs=(_row_spec(tr), _row_spec(tr), _row_spec(tr), pl.BlockSpec((tr, 2 * D_MODEL), lambda c: (c, 0)),
                   _col_spec(tr), _row_spec(tr), _row_spec(tr), _row_spec(tr),
                   pl.BlockSpec((1, 1), lambda c: (0, 0)), _vec_spec(), _vec_spec()),
        out_shape=(f32, f32, f32, jax.ShapeDtypeStruct((seq, 2 * D_MODEL), BF16),
                   jax.ShapeDtypeStruct((D_MODEL, seq), BF16), bf, bf, bf,
                   jax.ShapeDtypeStruct((1, 1), F32), vec, vec),
        scratch_shapes=[acc, acc, acc],
        compiler_params=_cparams("arbitrary"))(ypool, yssm, proj, proj, x, tgt, gate, g2, wbp_bf, wbs_bf, wout_bf)


def _in_bwd(dh, x, dy, g1, scale):
    seq = x.shape[0]
    tr = min(ROW_CHUNK_WIDE, seq)
    nc = seq // tr

    def kern(dh_ref, x_ref, dy_ref, g_ref, sc_ref, dx_ref, dsh_ref, dsc_ref, dg_ref, a_sh, a_sc, a_g):
        c = pl.program_id(0)

        @pl.when(c == 0)
        def _():
            a_sh[...] = jnp.zeros_like(a_sh)
            a_sc[...] = jnp.zeros_like(a_sc)
            a_g[...] = jnp.zeros_like(a_g)

        xv = x_ref[...]
        r = lax.rsqrt(jnp.mean(xv * xv, axis=-1, keepdims=True) + RMS_EPS)
        xh = xv * r
        g = g_ref[...]
        dhv = dh_ref[...]
        a_sh[...] += _acc8(dhv)
        a_sc[...] += _acc8(dhv * (xh * g))
        dn = dhv * (1.0 + sc_ref[...])
        a_g[...] += _acc8(dn * xh)
        dxh = dn * g
        dx_ref[...] = dy_ref[...] + r * (dxh - xh * jnp.mean(dxh * xh, axis=-1, keepdims=True))

        @pl.when(c == nc - 1)
        def _():
            dsh_ref[...] = jnp.sum(a_sh[...], axis=0, keepdims=True)
            dsc_ref[...] = jnp.sum(a_sc[...], axis=0, keepdims=True)
            dg_ref[...] = jnp.sum(a_g[...], axis=0, keepdims=True)

    vec = jax.ShapeDtypeStruct((1, D_MODEL), F32)
    acc = pltpu.VMEM((SUBLANES, D_MODEL), F32)
    return pl.pallas_call(
        kern, name="in_bwd", grid=(nc,),
        in_specs=[_row_spec(tr), _row_spec(tr), _row_spec(tr), _vec_spec(), _vec_spec()],
        out_specs=(_row_spec(tr), _vec_spec(), _vec_spec(), _vec_spec()),
        out_shape=(jax.ShapeDtypeStruct((seq, D_MODEL), F32), vec, vec, vec),
        scratch_shapes=[acc, acc, acc],
        compiler_params=_cparams("arbitrary"))(dh, x, dy, g1, scale)


SLAB = 2 * SUBLANES


def _local_scan(a_re, a_im, br, bi, xr, xi, row0, ls, reverse, init=None, xb=None):
    if init is None:
        x_re = jnp.zeros((SUBLANES, STATE_W), F32)
        x_im = jnp.zeros((SUBLANES, STATE_W), F32)
    else:
        x_re, x_im = init
    for i in (range(ls - 1, -1, -1) if reverse else range(ls)):
        src = pl.ds(SUBLANES * i, SUBLANES)
        dst = pl.ds(row0 + SUBLANES * i, SUBLANES)
        n_re = a_re * x_re - a_im * x_im + br[src, :]
        n_im = a_re * x_im + a_im * x_re + bi[src, :]
        if xb is not None and i % 2 == 1:
            pair = pl.ds(SUBLANES * (i - 1), SLAB)
            xb[0][pair, :] = jnp.concatenate([x_re, n_re], axis=0).astype(BF16)
            xb[1][pair, :] = jnp.concatenate([x_im, n_im], axis=0).astype(BF16)
        x_re, x_im = n_re, n_im
        xr[dst, :] = x_re
        xi[dst, :] = x_im
    return x_re, x_im


def _two(v):
    return jnp.concatenate([v, v], axis=0)


def _unpermute_rhs(v, sel):
    hi = v.astype(BF16)
    r1 = v - hi.astype(F32)
    mid = r1.astype(BF16)
    lo = (r1 - mid.astype(F32)).astype(BF16)
    return _dot(hi, sel) + _dot(mid, sel) + _dot(lo, sel)


def _scan_specs(tc, nb, rows_of):
    return dict(
        us=pl.BlockSpec((tc, nb * LANES), lambda b, c: (rows_of(c), 2 * D_MODEL // (nb * LANES) + b)),
        tok=pl.BlockSpec((tc, nb * LANES), lambda b, c: (rows_of(c), b)),
        bblk=pl.BlockSpec((nb, LANES, STATE_W), lambda b, c: (b, 0, 0)),
        cblk=pl.BlockSpec((nb, STATE_W, LANES), lambda b, c: (b, 0, 0)),
        vec=pl.BlockSpec((1, nb * STATE_W), lambda b, c: (0, b)),
        tab=pl.BlockSpec((tc, nb * STATE_W), lambda b, c: (0, b)),
        car=pl.BlockSpec((SUBLANES, nb * STATE_W), lambda b, c: (rows_of(c), b)),
        dvec=pl.BlockSpec((1, nb * LANES), lambda b, c: (0, b)))


def _ssm_scan_fwd(proj, bb_re, bb_im, cm_re, cm_im, abar_re, abar_im, pw_re, pw_im, d_skip, tc):
    seq = proj.shape[0]
    nc = seq // tc
    ls = tc // SUBLANES
    nb = SCAN_BLOCKS

    def kern(us_ref, bbr_ref, bbi_ref, cmr_ref, cmi_ref, ar_ref, ai_ref, pwr_ref, pwi_ref, d_ref,
             ys_ref, ecr_ref, eci_ref, bur, bui, car_r, car_i, end_r, end_i, upb, xb_r, xb_i, *nat):
        c = pl.program_id(1)

        @pl.when(c == 0)
        def _():
            car_r[...] = jnp.zeros_like(car_r)
            car_i[...] = jnp.zeros_like(car_i)

        for j in range(nb):
            cols = pl.ds(j * LANES, LANES)
            scols = pl.ds(j * STATE_W, STATE_W)
            nat[j][...] = us_ref[:, cols].astype(F32)
            for i in range(ls):
                upb[j, pl.ds(SUBLANES * i, SUBLANES), :] = nat[j][pl.ds(i, SUBLANES, stride=ls), :]
            u = upb[j]
            up = u.astype(BF16)
            bur[j] = _dot(up, bbr_ref[j])
            bui[j] = _dot(up, bbi_ref[j])
            a_re = jnp.broadcast_to(ar_ref[:, scols], (SUBLANES, STATE_W))
            a_im = jnp.broadcast_to(ai_ref[:, scols], (SUBLANES, STATE_W))
            x_re, x_im = _local_scan(a_re, a_im, bur.at[j], bui.at[j], bur.at[j], bui.at[j], 0, ls, False)
            end_r[j] = x_re
            end_i[j] = x_im
            big_re = pwr_ref[tc - 1:tc, scols]
            big_im = pwi_ref[tc - 1:tc, scols]
            e_re = car_r[j, 0:1, :]
            e_im = car_i[j, 0:1, :]
            for s in range(SUBLANES):
                n_re = end_r[j, s:s + 1, :] + big_re * e_re - big_im * e_im
                n_im = end_i[j, s:s + 1, :] + big_re * e_im + big_im * e_re
                e_re, e_im = n_re, n_im
                if s < SUBLANES - 1:
                    car_r[j, s + 1:s + 2, :] = e_re
                    car_i[j, s + 1:s + 2, :] = e_im
            ec_re = car_r[j]
            ec_im = car_i[j]
            ecr_ref[:, scols] = ec_re
            eci_ref[:, scols] = ec_im
            e2_re, e2_im = _two(ec_re), _two(ec_im)
            for k in range(tc // SLAB):
                rows_k = pl.ds(SLAB * k, SLAB)
                p_re = pwr_ref[rows_k, scols]
                p_im = pwi_ref[rows_k, scols]
                xb_r[j, rows_k, :] = (bur[j, rows_k, :] + p_re * e2_re - p_im * e2_im).astype(BF16)
                xb_i[j, rows_k, :] = (bui[j, rows_k, :] + p_re * e2_im + p_im * e2_re).astype(BF16)
            upb[j] = _dot(xb_r[j], cmr_ref[j]) - _dot(xb_i[j], cmi_ref[j]) + d_ref[:, cols] * u
            for i in range(ls):
                nat[j][pl.ds(i, SUBLANES, stride=ls), :] = upb[j, pl.ds(SUBLANES * i, SUBLANES), :]
            ys_ref[:, cols] = nat[j][...]
            car_r[j, 0:1, :] = e_re
            car_i[j, 0:1, :] = e_im

    sp = _scan_specs(tc, nb, lambda c: c)
    carry_shape = jax.ShapeDtypeStruct((nc * SUBLANES, STATE_ALL), F32)
    small = pltpu.VMEM((nb, SUBLANES, STATE_W), F32)
    big = pltpu.VMEM((nb, tc, STATE_W), F32)
    return pl.pallas_call(
        kern, name="ssm_scan_fwd", grid=(LANE_BLOCKS // nb, nc),
        in_specs=[sp["us"], sp["bblk"], sp["bblk"], sp["cblk"], sp["cblk"], sp["vec"], sp["vec"], sp["tab"], sp["tab"],
                  sp["dvec"]],
        out_specs=(sp["tok"], sp["car"], sp["car"]),
        out_shape=(jax.ShapeDtypeStruct((seq, D_MODEL), F32), carry_shape, carry_shape),
        scratch_shapes=[big, big, small, small, small, small, pltpu.VMEM((nb, tc, LANES), F32),
                        pltpu.VMEM((nb, tc, STATE_W), BF16), pltpu.VMEM((nb, tc, STATE_W), BF16)]
        + [pltpu.VMEM((tc, LANES), F32)] * nb,
        compiler_params=_cparams("parallel", "arbitrary"),
    )(proj, bb_re, bb_im, cm_re, cm_im, abar_re, abar_im, pw_re, pw_im, d_skip)


def _ssm_scan_bwd(proj, dys, ec_re, ec_im, bb_re, bb_im, cm_re, cm_im, abar_re, abar_im,
                  pw_re, pw_im, pv_re, pv_im, d_skip, tc):
    seq = proj.shape[0]
    nc = seq // tc
    ls = tc // SUBLANES
    nb = SCAN_BLOCKS

    def kern(us_ref, dys_ref, ecr_ref, eci_ref, bbr_ref, bbi_ref, cmr_ref, cmi_ref, ar_ref, ai_ref,
             pwr_ref, pwi_ref, pvr_ref, pvi_ref, d_ref,
             dus_ref, dbbr_ref, dbbi_ref, dcmr_ref, dcmi_ref, dar_ref, dai_ref, dd_ref,
             bur, bui, xr, xi, gr, gi, fc_r, fc_i, a_bbr, a_bbi, a_cmr, a_cmi, a_ar, a_ai, a_dd, upb, dpb, hb_r, hb_i,
             *nat):
        c = pl.program_id(1)

        @pl.when(c == 0)
        def _():
            for ref in (fc_r, fc_i, a_bbr, a_bbi, a_cmr, a_cmi, a_ar, a_ai, a_dd):
                ref[...] = jnp.zeros_like(ref)

        for j in range(nb):
            cols = pl.ds(j * LANES, LANES)
            scols = pl.ds(j * STATE_W, STATE_W)
            nat_u, nat_d = nat[2 * j], nat[2 * j + 1]
            nat_u[...] = us_ref[:, cols].astype(F32)
            nat_d[...] = dys_ref[:, cols]
            for i in range(ls):
                rows_i = pl.ds(SUBLANES * i, SUBLANES)
                upb[j, rows_i, :] = nat_u[pl.ds(i, SUBLANES, stride=ls), :]
                dpb[j, rows_i, :] = nat_d[pl.ds(i, SUBLANES, stride=ls), :]
            u = upb[j]
            dysv = dpb[j]
            a_dd[j] += _acc8(dysv * u)
            up = u.astype(BF16)
            bur[j] = _dot(up, bbr_ref[j])
            bui[j] = _dot(up, bbi_ref[j])
            a_re = jnp.broadcast_to(ar_ref[:, scols], (SUBLANES, STATE_W))
            a_im = jnp.broadcast_to(ai_ref[:, scols], (SUBLANES, STATE_W))
            ec_r = ecr_ref[:, scols]
            ec_i = eci_ref[:, scols]
            xr[j, 0:SUBLANES, :] = ec_r
            xi[j, 0:SUBLANES, :] = ec_i
            _local_scan(a_re, a_im, bur.at[j], bui.at[j], xr.at[j], xi.at[j], SUBLANES, ls, False, init=(ec_r, ec_i),
                        xb=(hb_r.at[j], hb_i.at[j]))
            dysp = dysv.astype(BF16)
            a_cmr[j] += _dot_tn(dysp, hb_r[j])
            a_cmi[j] -= _dot_tn(dysp, hb_i[j])
            gr[j] = _dot_nt(dysp, cmr_ref[j])
            gi[j] = -_dot_nt(dysp, cmi_ref[j])
            _local_scan(a_re, -a_im, gr.at[j], gi.at[j], gr.at[j], gi.at[j], 0, ls, True)
            big_re = pwr_ref[tc - 1:tc, scols]
            big_im = -pwi_ref[tc - 1:tc, scols]
            f_re = fc_r[j, SUBLANES - 1:SUBLANES, :]
            f_im = fc_i[j, SUBLANES - 1:SUBLANES, :]
            for s in range(SUBLANES - 1, -1, -1):
                n_re = gr[j, s:s + 1, :] + big_re * f_re - big_im * f_im
                n_im = gi[j, s:s + 1, :] + big_re * f_im + big_im * f_re
                f_re, f_im = n_re, n_im
                if s > 0:
                    fc_r[j, s - 1:s, :] = f_re
                    fc_i[j, s - 1:s, :] = f_im
            f2_r, f2_i = _two(fc_r[j]), _two(fc_i[j])
            acc_r = jnp.zeros((SUBLANES, STATE_W), F32)
            acc_i = jnp.zeros((SUBLANES, STATE_W), F32)
            for k in range(tc // SLAB):
                rows_k = pl.ds(SLAB * k, SLAB)
                q_re = pvr_ref[rows_k, scols]
                q_im = pvi_ref[rows_k, scols]
                lam_re = gr[j, rows_k, :] + q_re * f2_r + q_im * f2_i
                lam_im = gi[j, rows_k, :] + q_re * f2_i - q_im * f2_r
                xp_re = xr[j, rows_k, :]
                xp_im = xi[j, rows_k, :]
                d_r = lam_re * xp_re + lam_im * xp_im
                d_i = lam_im * xp_re - lam_re * xp_im
                acc_r = acc_r + (d_r[0:SUBLANES] + d_r[SUBLANES:])
                acc_i = acc_i + (d_i[0:SUBLANES] + d_i[SUBLANES:])
                hb_r[j, rows_k, :] = lam_re.astype(BF16)
                hb_i[j, rows_k, :] = lam_im.astype(BF16)
            a_ar[j] += acc_r
            a_ai[j] += acc_i
            fc_r[j, SUBLANES - 1:SUBLANES, :] = f_re
            fc_i[j, SUBLANES - 1:SUBLANES, :] = f_im
            lb_re = hb_r[j]
            lb_im = hb_i[j]
            a_bbr[j] += _dot_tn(up, lb_re)
            a_bbi[j] += _dot_tn(up, lb_im)
            dpb[j] = _dot_nt(lb_re, bbr_ref[j]) + _dot_nt(lb_im, bbi_ref[j]) + dysv * d_ref[:, cols]
            for i in range(ls):
                nat_d[pl.ds(i, SUBLANES, stride=ls), :] = dpb[j, pl.ds(SUBLANES * i, SUBLANES), :]
            dus_ref[:, cols] = nat_d[...].astype(BF16)

        @pl.when(c == nc - 1)
        def _():
            row_g = lax.broadcasted_iota(jnp.int32, (LANES, STATE_W), 0) // SSM_H
            col_g = lax.broadcasted_iota(jnp.int32, (LANES, STATE_W), 1) // SSM_P
            fold = (lax.broadcasted_iota(jnp.int32, (STATE_W, SSM_P), 0) % SSM_P
                    == lax.broadcasted_iota(jnp.int32, (STATE_W, SSM_P), 1)).astype(BF16)
            for j in range(nb):
                rows_j = pl.ds(j * LANES, LANES)
                for acc, out in ((a_bbr, dbbr_ref), (a_bbi, dbbi_ref), (a_cmr, dcmr_ref), (a_cmi, dcmi_ref)):
                    out[rows_j, :] = _unpermute_rhs(jnp.where(row_g == col_g, acc[j], 0.0), fold)
                dar_ref[:, pl.ds(j * STATE_W, STATE_W)] = jnp.sum(a_ar[j], axis=0, keepdims=True)
                dai_ref[:, pl.ds(j * STATE_W, STATE_W)] = jnp.sum(a_ai[j], axis=0, keepdims=True)
                dd_ref[:, pl.ds(j * LANES, LANES)] = jnp.sum(a_dd[j], axis=0, keepdims=True)

    sp = _scan_specs(tc, nb, lambda c: nc - 1 - c)
    ghp = pl.BlockSpec((nb * LANES, SSM_P), lambda b, c: (b, 0))
    ghp_shape = jax.ShapeDtypeStruct((SSM_G * SSM_H, SSM_P), F32)
    small = pltpu.VMEM((nb, SUBLANES, STATE_W), F32)
    big = pltpu.VMEM((nb, tc, STATE_W), F32)
    bigp = pltpu.VMEM((nb, tc + SUBLANES, STATE_W), F32)
    blk = pltpu.VMEM((nb, LANES, STATE_W), F32)
    tok = pltpu.VMEM((nb, tc, LANES), F32)
    return pl.pallas_call(
        kern, name="ssm_scan_bwd", grid=(LANE_BLOCKS // nb, nc),
        in_specs=[sp["us"], sp["tok"], sp["car"], sp["car"], sp["bblk"], sp["bblk"], sp["cblk"], sp["cblk"],
                  sp["vec"], sp["vec"], sp["tab"], sp["tab"], sp["tab"], sp["tab"], sp["dvec"]],
        out_specs=(sp["tok"], ghp, ghp, ghp, ghp, sp["vec"], sp["vec"], sp["dvec"]),
        out_shape=(jax.ShapeDtypeStruct((seq, D_MODEL), BF16), ghp_shape, ghp_shape, ghp_shape, ghp_shape,
                   jax.ShapeDtypeStruct((1, STATE_ALL), F32), jax.ShapeDtypeStruct((1, STATE_ALL), F32),
                   jax.ShapeDtypeStruct((1, D_MODEL), F32)),
        scratch_shapes=[big, big, bigp, bigp, big, big, small, small, blk, blk, blk, blk,
                        small, small, pltpu.VMEM((nb, SUBLANES, LANES), F32), tok, tok,
                        pltpu.VMEM((nb, tc, STATE_W), BF16), pltpu.VMEM((nb, tc, STATE_W), BF16)]
        + [pltpu.VMEM((tc, LANES), F32)] * (2 * nb),
        compiler_params=_cparams("parallel", "arbitrary"),
    )(proj, dys, ec_re, ec_im, bb_re, bb_im, cm_re, cm_im, abar_re, abar_im, pw_re, pw_im, pv_re, pv_im, d_skip)


def _eye5():
    return jnp.asarray(np.eye(GROUPS_PER_BLOCK, dtype=np.float32)[None, :, None, :, None])


def _embed_b(bb_t):
    t = bb_t.transpose(1, 0, 2).reshape(LANE_BLOCKS, GROUPS_PER_BLOCK, SSM_H, 1, SSM_P)
    return (t * _eye5()).reshape(LANE_BLOCKS, LANES, STATE_W)


def _embed_c(c_ghp):
    t = c_ghp.transpose(0, 2, 1).reshape(LANE_BLOCKS, GROUPS_PER_BLOCK, SSM_P, 1, SSM_H)
    return (t * _eye5()).reshape(LANE_BLOCKS, STATE_W, LANES)


def _local_step(x, c_row, tgt, w_ada_bf, b_ada, g1, g2, w_in_bf, pool_w_bf, pscale, a_re, a_im, log_dt,
                b_re_t, b_im_t, c_re, c_im, d_skip, glu_w_bf, glu_b, wbp_bf, wbs_bf, wout_bf,
                early_weight=None, late_weights=None, ride_for_dw_in=None, ride_for_dh=None):
    seq = x.shape[0]
    tc = min(SCAN_CHUNK, seq)
    mod8, silu_c = _mod_kernel(c_row, w_ada_bf, b_ada)
    mod = mod8[0:1]
    shift, scale, gate = mod[:, 0:D_MODEL], mod[:, D_MODEL:2 * D_MODEL], mod[:, 2 * D_MODEL:]

    abar_re, abar_im, bb_re_t, bb_im_t = _ssm_params(a_re, a_im, log_dt, b_re_t, b_im_t)
    abar_re_f, abar_im_f = abar_re.reshape(1, STATE_ALL), abar_im.reshape(1, STATE_ALL)
    pw_re, pw_im, pv_re, pv_im = _pow_tables(abar_re_f, abar_im_f, tc)
    bbe_re, bbe_im = _embed_b(bb_re_t).astype(BF16), _embed_b(bb_im_t).astype(BF16)
    cme_re, cme_im = _embed_c(c_re).astype(BF16), _embed_c(c_im).astype(BF16)
    d_row = d_skip.reshape(1, D_MODEL)

    if early_weight:
        h, h_t, *gathered = _in_norm(x, g1, scale, shift, ride=early_weight[0])
        w_in_bf = early_weight[1](*gathered)
    else:
        h, h_t = _in_norm(x, g1, scale, shift)
    if late_weights:
        proj, *gathered = _mm([h], [w_in_bf], name="proj", out_dtype=BF16, bm=1024, bn=1024, bk=1024,
                              ride=late_weights[0])
        pool_w_bf, glu_w_bf, wbp_bf, wbs_bf, wout_bf = late_weights[1](*gathered)
    else:
        proj = _mm([h], [w_in_bf], name="proj", out_dtype=BF16, bm=1024, bn=1024, bk=1024)
    ypool, ypool_t = _pool_fwd(proj, pool_w_bf, pscale)
    ys, ec_re, ec_im = _ssm_scan_fwd(proj, bbe_re, bbe_im, cme_re, cme_im, abar_re_f, abar_im_f,
                                      pw_re, pw_im, d_row, tc)
    yssm, yssm_t = _glu_fwd(ys, proj, glu_w_bf, glu_b)
    (dy, dypool, dyssm, d45, merged_t, dob, dbp, dbs, loss, dgate, dg2) = _out_fwd_bwd(
        ypool, yssm, proj, x, tgt, gate, g2, wbp_bf, wbs_bf, wout_bf)

    d_wout = _mm([merged_t], [dob], name="dw_out", bm=1024, bn=1024, bk=1024)
    d_wbp = _mm([ypool_t], [dbp], name="dw_bp", bm=1024, bn=1024, bk=1024)
    d_wbs = _mm([yssm_t], [dbs], name="dw_bs", bm=1024, bn=1024, bk=1024)
    dys, dzs, dq, yg_t, d_glu_b = _glu_bwd(ys, proj, dyssm, glu_w_bf, glu_b)
    d_glu_w = _mm([yg_t], [dq], name="dw_glu", bm=1024, bn=1024, bk=1024)
    (dus, dbbe_re, dbbe_im, dcme_re, dcme_im, d_abar_re, d_abar_im, d_dskip) = _ssm_scan_bwd(
        proj, dys, ec_re, ec_im, bbe_re, bbe_im, cme_re, cme_im, abar_re_f, abar_im_f,
        pw_re, pw_im, pv_re, pv_im, d_row, tc)
    d01, d_pool_w, d_pscale = _pool_bwd(proj, dypool, pool_w_bf, pscale)
    dparts = [d01, dus, dzs, d45]
    small_ready = dict(
        dg2=dg2, d_pscale=d_pscale, d_glu_b=d_glu_b, d_dskip=d_dskip, d_abar_re=d_abar_re, d_abar_im=d_abar_im,
        d_bb_re_t=dbbe_re.reshape(SSM_G, SSM_H, SSM_P).transpose(1, 0, 2),
        d_bb_im_t=dbbe_im.reshape(SSM_G, SSM_H, SSM_P).transpose(1, 0, 2),
        d_c_re=dcme_re.reshape(SSM_G, SSM_H, SSM_P), d_c_im=dcme_im.reshape(SSM_G, SSM_H, SSM_P))
    ride = ride_for_dw_in(small_ready) if ride_for_dw_in else None
    d_win = _mm([h_t], dparts, name="dw_in", bm=1024, bn=1024, bk=1024, ride=ride)
    rode_dw_in = ()
    if ride:
        d_win, rode_dw_in = d_win[0], tuple(d_win[1:])
    big_grads = dict(d_win=d_win, d_glu_w=d_glu_w, d_wbp=d_wbp, d_wbs=d_wbs, d_wout=d_wout, d_pool_w=d_pool_w)
    ride = ride_for_dh(big_grads) if ride_for_dh else None
    dh = _mm(dparts, [w_in_bf], tb=True, name="dh", bm=1024, bn=1024, bk=1024, ride=ride)
    rode = ()
    if ride:
        dh, rode = dh[0], tuple(dh[1:])
    grad_x, dshift, dscale, dg1 = _in_bwd(dh, x, dy, g1, scale)
    dmod = jnp.concatenate([dshift, dscale, dgate], axis=1)
    return dict(
        rode=rode, rode_dw_in=rode_dw_in, loss=loss[0, 0], grad_x=grad_x, dmod=dmod, silu_c=silu_c, dg1=dg1,
        **small_ready, **big_grads)


def _position():
    x, y, c = lax.axis_index("x"), lax.axis_index("y"), lax.axis_index("c")
    chips = [(1 - x, y), (x, 1 - y), (1 - x, 1 - y)]
    return x, y, c, chips


_ANY = pl.BlockSpec(memory_space=pl.ANY)
COMM_CHUNKS = 4
COMM_ROW_ALIGN = 16


def _row_chunks(rows, k):
    assert rows % (k * COMM_ROW_ALIGN) == 0, (rows, k)
    step = rows // k
    return [(q * step, step) for q in range(k)]


def _ag_weights_ride(packed, n_chunks=COMM_CHUNKS):
    rows, width = packed.shape
    half = rows // 2
    chunks = _row_chunks(half, n_chunks)
    nq = len(chunks)

    def parts(p_ref, out_ref, send_sems, recv_sems):
        x, y, c, chips = _position()
        sibling = (x, y, 1 - c)

        def copy(k, chip, h, q, to, src=None):
            start, size = chunks[q]
            rows_q = pl.ds(h * half + start, size)
            dst = out_ref.at[2 * chip[0] + chip[1], rows_q, :]
            return pltpu.make_async_remote_copy(
                src_ref=dst if src is None else src.at[rows_q, :], dst_ref=dst, send_sem=send_sems.at[k * nq + q],
                recv_sem=recv_sems.at[k * nq + q], device_id=to, device_id_type=MESH_ID)

        mine = [copy(6 + h, (x, y), h, q, sibling, src=p_ref) for h in range(2) for q in range(nq)]
        first = [copy(j, (x, y), c, q, (*chip, c), src=p_ref) for q in range(nq) for j, chip in enumerate(chips)]
        return (x, y, c), chips, sibling, copy, mine, first

    def start(ins, outs, sems):
        _, _, _, _, mine, first = parts(ins[0], outs[0], sems[0], sems[1])
        for cp in first + mine:
            cp.start()

    def wait(ins, outs, sems):
        (x, y, c), chips, sibling, copy, mine, first = parts(ins[0], outs[0], sems[0], sems[1])
        passed = []
        for q in range(nq):
            for j, chip in enumerate(chips):
                copy(j, chip, c, q, (x, y, c)).wait_recv()
                fwd = copy(3 + j, chip, c, q, sibling)
                fwd.start()
                passed.append(fwd)
        for q in range(nq):
            for j, chip in enumerate(chips):
                copy(3 + j, chip, 1 - c, q, (x, y, c)).wait_recv()
        for cp in mine:
            cp.wait_recv()
        for cp in first + passed + mine:
            cp.wait_send()

    return _Ride([packed], [jax.ShapeDtypeStruct((N_CHIPS, rows, width), packed.dtype)],
                 [pltpu.SemaphoreType.DMA((8 * nq,)), pltpu.SemaphoreType.DMA((8 * nq,))], start, wait)


def _join_rides(rides):
    def split(seq, counts):
        out, at = [], 0
        for n in counts:
            out.append(seq[at:at + n])
            at += n
        return out

    n_in = [len(r.inputs) for r in rides]
    n_out = [len(r.out_shapes) for r in rides]
    n_sem = [len(r.scratch) for r in rides]

    def start(ins, outs, sems):
        for r, i, o, s in zip(rides, split(ins, n_in), split(outs, n_out), split(sems, n_sem)):
            r.start(i, o, s)

    def wait(ins, outs, sems):
        for r, i, o, s in zip(rides, split(ins, n_in), split(outs, n_out), split(sems, n_sem)):
            r.wait(i, o, s)

    return _Ride([a for r in rides for a in r.inputs], [a for r in rides for a in r.out_shapes],
                 [a for r in rides for a in r.scratch], start, wait)


def _run_ride(ride, name):
    n_in, n_out = len(ride.inputs), len(ride.out_shapes)

    def body(*refs):
        ins, outs, sems = refs[:n_in], refs[n_in:n_in + n_out], refs[n_in + n_out:]
        ride.start(ins, outs, sems)
        ride.wait(ins, outs, sems)

    return pl.pallas_call(
        body, name=name, in_specs=[_ANY] * n_in, out_specs=(_ANY,) * n_out, out_shape=tuple(ride.out_shapes),
        scratch_shapes=list(ride.scratch))(*ride.inputs)


def _small_allgather_ride(buf):
    rows, width = buf.shape
    chunks = _row_chunks(rows, COMM_CHUNKS)
    nq = len(chunks)

    def parts(b_ref, all_ref, send_sems, recv_sems, local_sem):
        x, y, c, chips = _position()
        me, sibling = (x, y, c), (x, y, 1 - c)

        def copy(k, block, q, to, src=None):
            rows_q = pl.ds(chunks[q][0], chunks[q][1])
            dst = all_ref.at[4 * block[0] + 2 * block[1] + block[2], rows_q, :]
            return pltpu.make_async_remote_copy(
                src_ref=dst if src is None else src.at[rows_q, :], dst_ref=dst, send_sem=send_sems.at[k * nq + q],
                recv_sem=recv_sems.at[k * nq + q], device_id=to, device_id_type=MESH_ID)

        mine = pltpu.make_async_copy(b_ref, all_ref.at[4 * x + 2 * y + c], local_sem)
        first = []
        for q in range(nq):
            first += [copy(1 + j, me, q, (*chip, c), src=b_ref) for j, chip in enumerate(chips)]
            first.append(copy(0, me, q, sibling, src=b_ref))
        return me, sibling, c, chips, copy, mine, first

    def start(ins, outs, sems):
        _, _, _, _, _, mine, first = parts(ins[0], outs[0], *sems)
        mine.start()
        for cp in first:
            cp.start()

    def wait(ins, outs, sems):
        me, sibling, c, chips, copy, mine, first = parts(ins[0], outs[0], *sems)
        passed = []
        for q in range(nq):
            for j, chip in enumerate(chips):
                copy(1 + j, (*chip, c), q, me).wait_recv()
                fwd = copy(4 + j, (*chip, c), q, sibling)
                fwd.start()
                passed.append(fwd)
        for q in range(nq):
            copy(0, sibling, q, me).wait_recv()
            for j, chip in enumerate(chips):
                copy(4 + j, (*chip, 1 - c), q, me).wait_recv()
        for cp in first + passed:
            cp.wait_send()
        mine.wait()

    return _Ride([buf], [jax.ShapeDtypeStruct((N_DEV, rows, width), F32)],
                 [pltpu.SemaphoreType.DMA((7 * nq,)), pltpu.SemaphoreType.DMA((7 * nq,)), pltpu.SemaphoreType.DMA],
                 start, wait)


def _sum_devices(blocks):
    n, rows, width = blocks.shape
    rb = rows // 2 if (rows // 2) % SUBLANES == 0 else rows

    def kern(b_ref, o_ref):
        total = b_ref[0]
        for d in range(1, n):
            total = total + b_ref[d]
        o_ref[...] = total

    return pl.pallas_call(
        kern, name="small_sum", grid=(rows // rb,), in_specs=[pl.BlockSpec((n, rb, width), lambda i: (0, i, 0))],
        out_specs=pl.BlockSpec((rb, width), lambda i: (i, 0)), out_shape=jax.ShapeDtypeStruct((rows, width), F32),
        compiler_params=_cparams("parallel"))(blocks)


def _small_allgather_sum(buf, head_rows, n_chunks=COMM_CHUNKS):
    rows, width = buf.shape
    chunks = _row_chunks(rows, n_chunks)
    nq = len(chunks)

    def body(b_ref, head_ref, sum_ref, all_ref, send_sems, recv_sems, local_sem):
        x, y, c, chips = _position()
        me, sibling = (x, y, c), (x, y, 1 - c)

        def slot(px, py, pc):
            return all_ref.at[4 * px + 2 * py + pc]

        def copy(k, block, q, to, src=None):
            rows_q = pl.ds(chunks[q][0], chunks[q][1])
            dst = slot(*block).at[rows_q, :]
            return pltpu.make_async_remote_copy(
                src_ref=dst if src is None else src.at[rows_q, :], dst_ref=dst, send_sem=send_sems.at[k * nq + q],
                recv_sem=recv_sems.at[k * nq + q], device_id=to, device_id_type=MESH_ID)

        mine = pltpu.make_async_copy(b_ref, slot(*me), local_sem)
        mine.start()
        first = []
        for q in range(nq):
            first += [copy(1 + j, me, q, (*chip, c), src=b_ref) for j, chip in enumerate(chips)]
            first.append(copy(0, me, q, sibling, src=b_ref))
        for cp in first:
            cp.start()
        passed = []
        for q in range(nq):
            for j, chip in enumerate(chips):
                copy(1 + j, (*chip, c), q, me).wait_recv()
                fwd = copy(4 + j, (*chip, c), q, sibling)
                fwd.start()
                passed.append(fwd)
        for q in range(nq):
            copy(0, sibling, q, me).wait_recv()
            for j, chip in enumerate(chips):
                copy(4 + j, (*chip, 1 - c), q, me).wait_recv()
        for cp in first + passed:
            cp.wait_send()
        mine.wait()
        total = all_ref[0]
        for d in range(1, N_DEV):
            total = total + all_ref[d]
        sum_ref[...] = total
        head_ref[...] = all_ref[:, 0:head_rows, :]

    vm = pl.BlockSpec(memory_space=pltpu.VMEM)
    return pl.pallas_call(
        body, name="small_allgather_sum", in_specs=[vm], out_specs=(vm, vm),
        out_shape=(jax.ShapeDtypeStruct((N_DEV, head_rows, width), F32), jax.ShapeDtypeStruct((rows, width), F32)),
        scratch_shapes=[pltpu.VMEM((N_DEV, rows, width), F32), pltpu.SemaphoreType.DMA((7 * nq,)),
                        pltpu.SemaphoreType.DMA((7 * nq,)), pltpu.SemaphoreType.DMA],
        compiler_params=_cparams(),
    )(buf)


def _rs_pair(g):
    n, rows, width = g.shape
    half = rows // 2
    chunks = _row_chunks(half, COMM_CHUNKS)
    nq = len(chunks)

    def body(g_ref, got_ref, send_sems, recv_sems):
        x, y, c, _ = _position()
        swaps = []
        for k in range(n):
            for q, (start, size) in enumerate(chunks):
                swaps.append(pltpu.make_async_remote_copy(
                    src_ref=g_ref.at[k, pl.ds((1 - c) * half + start, size), :], dst_ref=got_ref.at[k, pl.ds(start, size), :],
                    send_sem=send_sems.at[k * nq + q], recv_sem=recv_sems.at[k * nq + q],
                    device_id=(x, y, 1 - c), device_id_type=MESH_ID))
        for cp in swaps:
            cp.start()
        for cp in swaps:
            cp.wait()

    return pl.pallas_call(
        body, name="rs_pair", in_specs=[_ANY], out_specs=_ANY, out_shape=jax.ShapeDtypeStruct((n, half, width), g.dtype),
        scratch_shapes=[pltpu.SemaphoreType.DMA((n * nq,)), pltpu.SemaphoreType.DMA((n * nq,))],
    )(g)


def _rs_chips_ride(part_bf):
    n, rows, width = part_bf.shape
    chunks = _row_chunks(rows, COMM_CHUNKS)
    nq = len(chunks)

    def sends(pb_ref, got_ref, send_sems, recv_sems):
        x, y, c, chips = _position()
        out = []
        for q, (start, size) in enumerate(chunks):
            for j, chip in enumerate(chips):
                out.append(pltpu.make_async_remote_copy(
                    src_ref=pb_ref.at[2 * chip[0] + chip[1], pl.ds(start, size), :], dst_ref=got_ref.at[j, pl.ds(start, size), :],
                    send_sem=send_sems.at[j * nq + q], recv_sem=recv_sems.at[j * nq + q],
                    device_id=(*chip, c), device_id_type=MESH_ID))
        return out

    def start(ins, outs, sems):
        for cp in sends(ins[0], outs[0], sems[0], sems[1]):
            cp.start()

    def wait(ins, outs, sems):
        for cp in sends(ins[0], outs[0], sems[0], sems[1]):
            cp.wait()

    return _Ride([part_bf], [jax.ShapeDtypeStruct((N_CHIPS - 1, rows, width), BF16)],
                 [pltpu.SemaphoreType.DMA((3 * nq,)), pltpu.SemaphoreType.DMA((3 * nq,))], start, wait)


def _rs_join(shard):
    rows, width = shard.shape
    half = rows // 2
    chunks = _row_chunks(half, COMM_CHUNKS)
    nq = len(chunks)

    def body(in_ref, out_ref, send_sems, recv_sems):
        x, y, c, _ = _position()
        def swap(q, h):
            rows_q = pl.ds(h * half + chunks[q][0], chunks[q][1])
            return pltpu.make_async_remote_copy(
                src_ref=in_ref.at[rows_q, :], dst_ref=out_ref.at[rows_q, :], send_sem=send_sems.at[q],
                recv_sem=recv_sems.at[q], device_id=(x, y, 1 - c), device_id_type=MESH_ID)

        for q in range(nq):
            swap(q, c).start()
        for q in range(nq):
            swap(q, 1 - c).wait_recv()
        for q in range(nq):
            swap(q, c).wait_send()

    return pl.pallas_call(
        body, name="rs_join", in_specs=[_ANY], out_specs=_ANY, input_output_aliases={0: 0},
        out_shape=jax.ShapeDtypeStruct(shard.shape, shard.dtype),
        scratch_shapes=[pltpu.SemaphoreType.DMA((nq,)), pltpu.SemaphoreType.DMA((nq,))],
    )(shard)


def _pair_add(g, got, core):
    n, half, width = got.shape
    nb = 2
    rb = half // nb

    def kern(c_ref, a_ref, b_ref, f_ref, h_ref):
        s = a_ref[...] + b_ref[...]
        f_ref[...] = s
        h_ref[...] = s.astype(BF16)

    spec = pl.BlockSpec((1, rb, width), lambda k, i, c_ref: (k, i, 0))
    return pl.pallas_call(
        kern, name="rs_pair_add",
        grid_spec=pltpu.PrefetchScalarGridSpec(
            num_scalar_prefetch=1, grid=(n, nb),
            in_specs=[pl.BlockSpec((1, rb, width), lambda k, i, c_ref: (k, c_ref[0] * nb + i, 0)), spec],
            out_specs=(spec, spec)),
        out_shape=(jax.ShapeDtypeStruct(got.shape, F32), jax.ShapeDtypeStruct(got.shape, BF16)),
        compiler_params=_cparams("parallel", "parallel"))(core, g, got)


def _chip_add(part_f32, got, where):
    _, rows, width = part_f32.shape
    nb = 2
    rb = rows // nb

    def kern(w_ref, a_ref, b_ref, o_ref):
        o_ref[...] = ((a_ref[0] + b_ref[0].astype(F32)) + b_ref[1].astype(F32)) + b_ref[2].astype(F32)

    return pl.pallas_call(
        kern, name="rs_chip_add",
        grid_spec=pltpu.PrefetchScalarGridSpec(
            num_scalar_prefetch=1, grid=(nb,),
            in_specs=[pl.BlockSpec((1, rb, width), lambda i, w_ref: (w_ref[0], i, 0)),
                      pl.BlockSpec((N_CHIPS - 1, rb, width), lambda i, w_ref: (0, i, 0))],
            out_specs=pl.BlockSpec((rb, width), lambda i, w_ref: (w_ref[1] * nb + i, 0))),
        out_shape=jax.ShapeDtypeStruct((2 * rows, width), F32),
        compiler_params=_cparams("parallel"))(where, part_f32, got)


def _adamw(w, g, m, v, name):
    rows, width = w.shape
    rb = rows
    for cand in (512, 256, 128, 64, 32, 16, 8):
        if rows % cand == 0 and cand * width * 4 <= ADAM_BLOCK_BYTES:
            rb = cand
            break
    spec = pl.BlockSpec((rb, width), lambda i: (i, 0))

    def kern(w_ref, g_ref, m_ref, v_ref, d_ref, nm_ref, nv_ref):
        gv = g_ref[...]
        nm = ADAM_B1 * m_ref[...] + (1.0 - ADAM_B1) * gv
        nv = ADAM_B2 * v_ref[...] + (1.0 - ADAM_B2) * (gv * gv)
        m_hat = nm / (1.0 - ADAM_B1 ** ADAM_STEP)
        v_hat = nv / (1.0 - ADAM_B2 ** ADAM_STEP)
        d_ref[...] = -ADAM_LR * (m_hat / (jnp.sqrt(v_hat) + ADAM_EPS) + ADAM_WD * w_ref[...])
        nm_ref[...] = nm
        nv_ref[...] = nv

    shp = jax.ShapeDtypeStruct(w.shape, F32)
    return pl.pallas_call(
        kern, name=name, grid=(rows // rb,), in_specs=[spec] * 4, out_specs=(spec, spec, spec),
        out_shape=(shp, shp, shp), compiler_params=_cparams("parallel"))(w, g, m, v)


def _wada_grad(silu_t, dmod_cols):
    n = dmod_cols.shape[1]

    def kern(s_ref, d_ref, o_ref):
        acc = s_ref[:, 0:1] * d_ref[0:1, :]
        for b in range(1, N_DEV):
            acc = acc + s_ref[:, b:b + 1] * d_ref[b:b + 1, :]
        o_ref[...] = acc

    return pl.pallas_call(kern, name="wada_grad", out_shape=jax.ShapeDtypeStruct((D_MODEL, n), F32),
                          compiler_params=_cparams())(silu_t, dmod_cols)


def _rows(a, multiple):
    flat = a.reshape(-1)
    pad = (-flat.shape[0]) % (D_MODEL * multiple)
    if pad:
        flat = jnp.concatenate([flat, jnp.zeros((pad,), flat.dtype)])
    return flat.reshape(-1, D_MODEL)


def _part_rows(shape, multiple):
    return -(-int(np.prod(shape)) // (D_MODEL * multiple)) * multiple


def _pack_rows(parts, multiple, total_multiple=1):
    blocks = [_rows(p, multiple) for p in parts]
    pad = (-sum(b.shape[0] for b in blocks)) % total_multiple
    if pad:
        blocks.append(jnp.zeros((pad, D_MODEL), blocks[0].dtype))
    return jnp.concatenate(blocks, axis=0)


def _unpack_rows(buf, shapes, multiple):
    out, r = [], 0
    for shp in shapes:
        n = int(np.prod(shp))
        nr = _part_rows(shp, multiple)
        out.append(buf[r:r + nr].reshape(-1)[:n].reshape(shp))
        r += nr
    return out


def kernel(x, c, w_ada, b_ada, norm_pre, norm_post, w_in, pool_w, pool_scale, ssm_a_re, ssm_a_im, ssm_log_dt, ssm_b_re, ssm_b_im, ssm_c_re, ssm_c_im, ssm_d, glu_w, glu_b, w_branch_pool, w_branch_ssm, w_out, loss_target, m_w_ada, m_b_ada, m_norm_pre, m_norm_post, m_w_in, m_pool_w, m_pool_scale, m_ssm_a_re, m_ssm_a_im, m_ssm_log_dt, m_ssm_b_re, m_ssm_b_im, m_ssm_c_re, m_ssm_c_im, m_ssm_d, m_glu_w, m_glu_b, m_w_branch_pool, m_w_branch_ssm, m_w_out, v_w_ada, v_b_ada, v_norm_pre, v_norm_post, v_w_in, v_pool_w, v_pool_scale, v_ssm_a_re, v_ssm_a_im, v_ssm_log_dt, v_ssm_b_re, v_ssm_b_im, v_ssm_c_re, v_ssm_c_im, v_ssm_d, v_glu_w, v_glu_b, v_w_branch_pool, v_w_branch_ssm, v_w_out):
    n_ada = w_ada.shape[2]
    n_in = w_in.shape[2]
    n_row = glu_w.shape[1]
    n_pool = pool_w.shape[2]
    n_groups = pool_w.shape[1]

    (g_ada,) = _run_ride(_ag_weights_ride(w_ada[0].astype(BF16)), "ag_weights")
    w_ada_bf = g_ada.transpose(1, 0, 2).reshape(D_MODEL, N_CHIPS * n_ada)
    w_in_ride = _ag_weights_ride(w_in[0].astype(BF16))

    def unpack_w_in(g_in):
        return g_in.transpose(1, 0, 2).reshape(D_MODEL, N_CHIPS * n_in)
    pool_rows = n_groups * n_pool * POOL_GW // D_MODEL
    late_shards = [pool_w[0].reshape(n_groups * n_pool, POOL_GW), glu_w[0], w_branch_pool[0], w_branch_ssm[0], w_out[0]]
    late_ride = _join_rides([_ag_weights_ride(s.astype(BF16), n_chunks=2) for s in late_shards])

    def unpack_late(pool, *squares):
        pool = pool.reshape(N_CHIPS, n_groups, n_pool, POOL_GW).transpose(1, 0, 2, 3)
        return (pool.reshape(n_groups, POOL_GW, POOL_GW), *[s.reshape(D_MODEL, D_MODEL) for s in squares])

    chip = 2 * lax.axis_index("x") + lax.axis_index("y")
    core = lax.axis_index("c").astype(jnp.int32)
    kept = {}

    def by_cols(a, n):
        return a.reshape(D_MODEL, N_CHIPS, n).transpose(1, 0, 2).reshape(N_CHIPS, -1, D_MODEL)

    def by_rows(a):
        return a.reshape(N_CHIPS, n_row, D_MODEL)

    def exchange_big(g):
        pool_by_chip = g["d_pool_w"].reshape(n_groups, N_CHIPS, n_pool, POOL_GW).transpose(1, 0, 2, 3)
        blocks = [by_cols(g["d_win"], n_in), by_rows(g["d_glu_w"]), by_rows(g["d_wbp"]), by_rows(g["d_wbs"]),
                  by_rows(g["d_wout"]), pool_by_chip.reshape(N_CHIPS, pool_rows, D_MODEL)]
        pad = (-sum(b.shape[1] for b in blocks)) % (2 * COMM_CHUNKS * COMM_ROW_ALIGN)
        if pad:
            blocks.append(jnp.zeros((N_CHIPS, pad, D_MODEL), F32))
        g_packed = jnp.concatenate(blocks, axis=1)
        kept["part_f32"], part_bf = _pair_add(g_packed, _rs_pair(g_packed), core.reshape(1))
        return _rs_chips_ride(part_bf)

    a_re, a_im, log_dt = ssm_a_re[0], ssm_a_im[0], ssm_log_dt[0].reshape(SSM_G, 1)
    b_re_t, b_im_t = ssm_b_re[0].transpose(2, 0, 1), ssm_b_im[0].transpose(2, 0, 1)
    early_names = ["dg2", "d_pscale", "d_glu_b", "d_dskip", "d_abar_re", "d_abar_im", "d_bb_re_t", "d_bb_im_t",
                   "d_c_re", "d_c_im"]

    def exchange_small(s):
        parts = [s[k] for k in early_names]
        kept["early_shapes"] = [p.shape for p in parts]
        return _small_allgather_ride(_pack_rows(parts, SUBLANES, COMM_CHUNKS * COMM_ROW_ALIGN))

    res = _local_step(x[0], c, loss_target[0], w_ada_bf, b_ada, norm_pre, norm_post, None, None, pool_scale,
                      a_re, a_im, log_dt, b_re_t, b_im_t, ssm_c_re[0], ssm_c_im[0], ssm_d[0], None, glu_b[0:1],
                      None, None, None, early_weight=(w_in_ride, unpack_w_in), late_weights=(late_ride, unpack_late),
                      ride_for_dw_in=exchange_small, ride_for_dh=exchange_big)
    loss = lax.psum(res["loss"], ("x", "y", "c"))

    (all_early,) = res["rode_dw_in"]
    (g_norm_post, g_pscale, g_glu_b, g_dskip, s_abar_re, s_abar_im, s_bb_re, s_bb_im, g_c_re, g_c_im) = _unpack_rows(
        _sum_devices(all_early), kept["early_shapes"], SUBLANES)
    g_a_re, g_a_im, g_log_dt, g_b_re_t, g_b_im_t = _ssm_params_bwd(
        a_re, a_im, log_dt, b_re_t, b_im_t, s_abar_re.reshape(SSM_G, SSM_P), s_abar_im.reshape(SSM_G, SSM_P),
        s_bb_re, s_bb_im)
    late_parts = [res["dmod"], res["silu_c"], res["dg1"]]
    late_shapes = [p.shape for p in late_parts]
    head_rows = _part_rows(late_shapes[0], SUBLANES) + _part_rows(late_shapes[1], SUBLANES)
    all_late, sum_late = _small_allgather_sum(_pack_rows(late_parts, SUBLANES, COMM_ROW_ALIGN), head_rows, n_chunks=1)
    g_b_ada, _, g_norm_pre = _unpack_rows(sum_late, late_shapes, SUBLANES)
    dmod_all = all_late[:, 0:3].reshape(N_DEV, 3 * D_MODEL)
    dmod_cols = lax.dynamic_slice_in_dim(dmod_all, chip * n_ada, n_ada, axis=1)
    silu_t = all_late[:, _part_rows(late_shapes[0], SUBLANES)].transpose(1, 0)
    g_w_ada = _wada_grad(silu_t, dmod_cols)

    (got_chips,) = res["rode"]
    shard = _rs_join(_chip_add(kept["part_f32"], got_chips, jnp.stack([chip.astype(jnp.int32), core])))
    r = 0
    g_w_in = shard[r:r + n_in].reshape(D_MODEL, n_in)
    r += n_in
    g_squares = []
    for _ in range(4):
        g_squares.append(shard[r:r + n_row])
        r += n_row
    g_glu_w, g_wbp, g_wbs, g_wout = g_squares
    g_pool_w = shard[r:r + pool_rows].reshape(n_groups * n_pool, POOL_GW)

    big = [("w_ada", w_ada[0], g_w_ada, m_w_ada[0], v_w_ada[0]),
           ("w_in", w_in[0], g_w_in, m_w_in[0], v_w_in[0]),
           ("pool_w", pool_w[0].reshape(n_groups * n_pool, POOL_GW), g_pool_w,
            m_pool_w[0].reshape(n_groups * n_pool, POOL_GW), v_pool_w[0].reshape(n_groups * n_pool, POOL_GW)),
           ("glu_w", glu_w[0], g_glu_w, m_glu_w[0], v_glu_w[0]),
           ("w_branch_pool", w_branch_pool[0], g_wbp, m_w_branch_pool[0], v_w_branch_pool[0]),
           ("w_branch_ssm", w_branch_ssm[0], g_wbs, m_w_branch_ssm[0], v_w_branch_ssm[0]),
           ("w_out", w_out[0], g_wout, m_w_out[0], v_w_out[0])]
    out = {}
    for name, w_, g_, m_, v_ in big:
        d_, nm_, nv_ = _adamw(w_, g_, m_, v_, "adamw_" + name)
        out[name] = (g_, d_, nm_, nv_)

    g_b_re = g_b_re_t.transpose(1, 2, 0)
    g_b_im = g_b_im_t.transpose(1, 2, 0)
    small = [("b_ada", b_ada, g_b_ada, m_b_ada, v_b_ada),
             ("norm_pre", norm_pre, g_norm_pre, m_norm_pre, v_norm_pre),
             ("norm_post", norm_post, g_norm_post, m_norm_post, v_norm_post),
             ("pool_scale", pool_scale, g_pscale, m_pool_scale, v_pool_scale),
             ("ssm_a_re", ssm_a_re, g_a_re, m_ssm_a_re, v_ssm_a_re),
             ("ssm_a_im", ssm_a_im, g_a_im, m_ssm_a_im, v_ssm_a_im),
             ("ssm_log_dt", ssm_log_dt, g_log_dt, m_ssm_log_dt, v_ssm_log_dt),
             ("ssm_b_re", ssm_b_re, g_b_re, m_ssm_b_re, v_ssm_b_re),
             ("ssm_b_im", ssm_b_im, g_b_im, m_ssm_b_im, v_ssm_b_im),
             ("ssm_c_re", ssm_c_re, g_c_re, m_ssm_c_re, v_ssm_c_re),
             ("ssm_c_im", ssm_c_im, g_c_im, m_ssm_c_im, v_ssm_c_im),
             ("ssm_d", ssm_d, g_dskip, m_ssm_d, v_ssm_d),
             ("glu_b", glu_b, g_glu_b, m_glu_b, v_glu_b)]
    shapes = [w_.shape for _, w_, _, _, _ in small]
    pw_, pg_, pm_, pv_ = (_pack_rows([t[i] for t in small], SUBLANES) for i in (1, 2, 3, 4))
    pd_, pnm_, pnv_ = _adamw(pw_, pg_, pm_, pv_, "adamw_small")
    unpacked = [_unpack_rows(p, shapes, SUBLANES) for p in (pg_, pd_, pnm_, pnv_)]
    for (name, _, _, _, _), g_, d_, nm_, nv_ in zip(small, *unpacked):
        out[name] = (g_, d_, nm_, nv_)

    order = ["w_ada", "b_ada", "norm_pre", "norm_post", "w_in", "pool_w", "pool_scale", "ssm_a_re", "ssm_a_im",
             "ssm_log_dt", "ssm_b_re", "ssm_b_im", "ssm_c_re", "ssm_c_im", "ssm_d", "glu_w", "glu_b", "w_branch_pool",
             "w_branch_ssm", "w_out"]
    ref_shape = dict(w_ada=w_ada.shape, w_in=w_in.shape, pool_w=pool_w.shape, glu_w=glu_w.shape,
                     w_branch_pool=w_branch_pool.shape, w_branch_ssm=w_branch_ssm.shape, w_out=w_out.shape)
    for name, w_, _, _, _ in small:
        ref_shape[name] = w_.shape
    results = [loss, res["grad_x"][None]]
    for k in range(4):
        results += [out[name][k].reshape(ref_shape[name]) for name in order]
    return tuple(results)
```

```python
import functools
import math

import numpy as np
import jax
import jax.numpy as jnp
from jax import lax
from jax.experimental import pallas as pl
from jax.experimental.pallas import tpu as pltpu

F32 = jnp.float32
BF16 = jnp.bfloat16
MESH_ID = pl.DeviceIdType.MESH

D_MODEL = 1024
LANES = 128
SUBLANES = 8
SSM_G, SSM_P, SSM_H = 64, 64, 16
LANE_BLOCKS = D_MODEL // LANES
GROUPS_PER_BLOCK = LANES // SSM_H
STATE_W = GROUPS_PER_BLOCK * SSM_P
STATE_ALL = SSM_G * SSM_P
POOL_WINDOWS = (2, 4, 8, 16)
POOL_GW = D_MODEL // len(POOL_WINDOWS)
HALO = 16
RMS_EPS = 1e-6
N_CHIPS = 4
N_DEV = 8

SCAN_CHUNK = 512
SCAN_BLOCKS = 2
ROW_CHUNK = 256
ROW_CHUNK_WIDE = 512
VMEM_LIMIT_BYTES = 56 * 1024 * 1024

ADAM_BLOCK_BYTES = 1 << 20
ADAM_LR, ADAM_B1, ADAM_B2, ADAM_EPS, ADAM_WD, ADAM_STEP = 0.001, 0.9, 0.999, 1e-08, 0.01, 10

_GELU_C0 = math.sqrt(2.0 / math.pi)
_GELU_C1 = 0.044715


def _cparams(*sem):
    if sem:
        return pltpu.CompilerParams(dimension_semantics=sem, vmem_limit_bytes=VMEM_LIMIT_BYTES)
    return pltpu.CompilerParams(vmem_limit_bytes=VMEM_LIMIT_BYTES)


def _sigmoid(v):
    return jax.nn.sigmoid(v)


def _silu(v):
    return v * _sigmoid(v)


def _dsilu(v):
    s = _sigmoid(v)
    return s * (1.0 + v * (1.0 - s))


def _gelu(v):
    return 0.5 * v * (1.0 + jnp.tanh(_GELU_C0 * (v + _GELU_C1 * v * v * v)))


def _dgelu(v):
    t = jnp.tanh(_GELU_C0 * (v + _GELU_C1 * v * v * v))
    return 0.5 * (1.0 + t) + 0.5 * v * (1.0 - t * t) * _GELU_C0 * (1.0 + 3.0 * _GELU_C1 * v * v)


def _dot(a, b):
    return lax.dot_general(a, b, (((1,), (0,)), ((), ())), preferred_element_type=F32)


def _dot_nt(a, b):
    return lax.dot_general(a, b, (((1,), (1,)), ((), ())), preferred_element_type=F32)


def _dot_tn(a, b):
    return lax.dot_general(a, b, (((0,), (0,)), ((), ())), preferred_element_type=F32)


def _acc8(v):
    return v.reshape(v.shape[0] // SUBLANES, SUBLANES, v.shape[1]).sum(axis=0)


class _Ride:
    def __init__(self, inputs, out_shapes, scratch, start, wait):
        self.inputs, self.out_shapes, self.scratch, self.start, self.wait = inputs, out_shapes, scratch, start, wait


def _mm(a_parts, b_parts, *, name, ta=False, tb=False, out_dtype=F32, bm=512, bn=512, bk=512, ride=None):
    a_parts, b_parts = list(a_parts), list(b_parts)
    if ta:
        assert len(a_parts) == 1
        k_dim, m_dim = a_parts[0].shape
    else:
        m_dim = a_parts[0].shape[0]
        k_dim = sum(a.shape[1] for a in a_parts)
    if tb:
        assert len(b_parts) == 1
        n_dim = b_parts[0].shape[0]
    else:
        n_dim = sum(b.shape[1] for b in b_parts)
    bm, bn, bk = min(bm, m_dim), min(bn, n_dim), min(bk, k_dim)
    nm, nn, nk = m_dim // bm, n_dim // bn, k_dim // bk
    a_ranges, off = [], 0
    for a in a_parts:
        cnt = (a.shape[0] if ta else a.shape[1]) // bk
        a_ranges.append((off, cnt))
        off += cnt
    b_ranges, off = [], 0
    for b in b_parts:
        cnt = (b.shape[0] if tb else b.shape[1]) // bn
        b_ranges.append((off, cnt))
        off += cnt

    def a_spec(off, cnt):
        if ta:
            return pl.BlockSpec((bk, bm), lambda i, n, k: (k, i))
        return pl.BlockSpec((bm, bk), lambda i, n, k: (i, jnp.clip(k - off, 0, cnt - 1)))

    def b_spec(off, cnt):
        if tb:
            return pl.BlockSpec((bn, bk), lambda i, n, k: (n, k))
        return pl.BlockSpec((bk, bn), lambda i, n, k: (k, jnp.clip(n - off, 0, cnt - 1)))

    na, nb = len(a_parts), len(b_parts)
    dims = (((0 if ta else 1,), (1 if tb else 0,)), ((), ()))

    def kern_single(a_ref, b_ref, o_ref):
        o_ref[...] = lax.dot_general(a_ref[...].astype(BF16), b_ref[...].astype(BF16), dims,
                                     preferred_element_type=F32).astype(out_dtype)

    if na == 1 and nb == 1 and nk == 1 and not ride:
        return pl.pallas_call(
            kern_single, name=name, grid=(nm, nn),
            in_specs=[pl.BlockSpec((bk, bm), lambda i, n: (0, i)) if ta else pl.BlockSpec((bm, bk), lambda i, n: (i, 0)),
                      pl.BlockSpec((bn, bk), lambda i, n: (n, 0)) if tb else pl.BlockSpec((bk, bn), lambda i, n: (0, n))],
            out_specs=pl.BlockSpec((bm, bn), lambda i, n: (i, n)),
            out_shape=jax.ShapeDtypeStruct((m_dim, n_dim), out_dtype),
            compiler_params=_cparams("parallel", "parallel"),
        )(a_parts[0], b_parts[0])

    n_rin = len(ride.inputs) if ride else 0
    n_rout = len(ride.out_shapes) if ride else 0

    def kern(*refs):
        a_refs, b_refs = refs[:na], refs[na:na + nb]
        rin = refs[na + nb:na + nb + n_rin]
        o_ref = refs[na + nb + n_rin]
        rout = refs[na + nb + n_rin + 1:na + nb + n_rin + 1 + n_rout]
        acc = refs[na + nb + n_rin + 1 + n_rout]
        rsem = refs[na + nb + n_rin + 2 + n_rout:]
        i, n, k = pl.program_id(0), pl.program_id(1), pl.program_id(2)

        if ride:
            @pl.when((i == 0) & (n == 0) & (k == 0))
            def _():
                ride.start(rin, rout, rsem)

        @pl.when(k == 0)
        def _():
            acc[...] = jnp.zeros_like(acc)

        for ja, (koff, kcnt) in enumerate(a_ranges):
            for jb, (noff, ncnt) in enumerate(b_ranges):
                def step(ja=ja, jb=jb):
                    a = a_refs[ja][...].astype(BF16)
                    b = b_refs[jb][...].astype(BF16)
                    acc[...] += lax.dot_general(a, b, dims, preferred_element_type=F32)

                if na == 1 and nb == 1:
                    step()
                else:
                    cond = (k >= koff) & (k < koff + kcnt) & (n >= noff) & (n < noff + ncnt)
                    pl.when(cond)(step)

        @pl.when(k == nk - 1)
        def _():
            o_ref[...] = acc[...].astype(out_dtype)

        if ride:
            @pl.when((i == nm - 1) & (n == nn - 1) & (k == nk - 1))
            def _():
                ride.wait(rin, rout, rsem)

    any_spec = pl.BlockSpec(memory_space=pl.ANY)
    out_spec = pl.BlockSpec((bm, bn), lambda i, n, k: (i, n))
    out_shape = jax.ShapeDtypeStruct((m_dim, n_dim), out_dtype)
    if not ride:
        return pl.pallas_call(
            kern, name=name, grid=(nm, nn, nk),
            in_specs=[a_spec(*r) for r in a_ranges] + [b_spec(*r) for r in b_ranges],
            out_specs=out_spec, out_shape=out_shape, scratch_shapes=[pltpu.VMEM((bm, bn), F32)],
            compiler_params=_cparams("parallel", "parallel", "arbitrary"),
        )(*a_parts, *b_parts)
    return pl.pallas_call(
        kern, name=name, grid=(nm, nn, nk),
        in_specs=[a_spec(*r) for r in a_ranges] + [b_spec(*r) for r in b_ranges] + [any_spec] * n_rin,
        out_specs=(out_spec,) + (any_spec,) * n_rout, out_shape=(out_shape,) + tuple(ride.out_shapes),
        scratch_shapes=[pltpu.VMEM((bm, bn), F32)] + list(ride.scratch),
        compiler_params=_cparams("arbitrary", "arbitrary", "arbitrary"),
    )(*a_parts, *b_parts, *ride.inputs)


def _ssm_param_fn(a_re, a_im, log_dt, b_re, b_im):
    dt = jnp.exp(log_dt)
    lam_re = jnp.minimum(a_re, -1e-4)
    lam_im = a_im
    mag = jnp.exp(lam_re * dt)
    abar_re = mag * jnp.cos(lam_im * dt)
    abar_im = mag * jnp.sin(lam_im * dt)
    den = lam_re * lam_re + lam_im * lam_im
    num_re = abar_re - 1.0
    f_re = (num_re * lam_re + abar_im * lam_im) / den
    f_im = (abar_im * lam_re - num_re * lam_im) / den
    bb_re = f_re * b_re - f_im * b_im
    bb_im = f_re * b_im + f_im * b_re
    return abar_re, abar_im, bb_re, bb_im


def _ssm_params(a_re, a_im, log_dt, b_re_t, b_im_t):
    def kern(are, aim, ldt, bre, bim, o_ar, o_ai, o_br, o_bi):
        ar, ai, br, bi = _ssm_param_fn(are[...], aim[...], ldt[...], bre[...], bim[...])
        o_ar[...] = ar
        o_ai[...] = ai
        o_br[...] = br
        o_bi[...] = bi

    gp = jax.ShapeDtypeStruct((SSM_G, SSM_P), F32)
    hgp = jax.ShapeDtypeStruct((SSM_H, SSM_G, SSM_P), F32)
    return pl.pallas_call(kern, name="ssm_params", out_shape=(gp, gp, hgp, hgp), compiler_params=_cparams())(
        a_re, a_im, log_dt, b_re_t, b_im_t)


def _ssm_params_bwd(a_re, a_im, log_dt, b_re_t, b_im_t, d_ar, d_ai, d_bbr, d_bbi):
    def kern(are, aim, ldt, bre, bim, dar, dai, dbr, dbi, o_are, o_aim, o_ldt, o_bre, o_bim):
        prim = (are[...], aim[...], ldt[...], bre[...], bim[...])
        _, vjp = jax.vjp(_ssm_param_fn, *prim)
        g = vjp((dar[...], dai[...], dbr[...], dbi[...]))
        o_are[...] = g[0]
        o_aim[...] = g[1]
        o_ldt[...] = g[2]
        o_bre[...] = g[3]
        o_bim[...] = g[4]

    gp = jax.ShapeDtypeStruct((SSM_G, SSM_P), F32)
    g1 = jax.ShapeDtypeStruct((SSM_G, 1), F32)
    hgp = jax.ShapeDtypeStruct((SSM_H, SSM_G, SSM_P), F32)
    return pl.pallas_call(kern, name="ssm_params_bwd", out_shape=(gp, gp, g1, hgp, hgp), compiler_params=_cparams())(
        a_re, a_im, log_dt, b_re_t, b_im_t, d_ar, d_ai, d_bbr, d_bbi)


def _pow_tables(abar_re, abar_im, tc):
    ls = tc // SUBLANES

    def kern(ar_ref, ai_ref, fr_ref, fi_ref, rr_ref, ri_ref):
        a_re = jnp.broadcast_to(ar_ref[...], (SUBLANES, STATE_W))
        a_im = jnp.broadcast_to(ai_ref[...], (SUBLANES, STATE_W))
        p_re, p_im = a_re, a_im
        for i in range(ls):
            fwd = pl.ds(SUBLANES * i, SUBLANES)
            rev = pl.ds(SUBLANES * (ls - 1 - i), SUBLANES)
            fr_ref[fwd, :] = p_re
            fi_ref[fwd, :] = p_im
            rr_ref[rev, :] = p_re
            ri_ref[rev, :] = p_im
            p_re, p_im = p_re * a_re - p_im * a_im, p_re * a_im + p_im * a_re

    vec = pl.BlockSpec((1, STATE_W), lambda b: (0, b))
    tab = pl.BlockSpec((tc, STATE_W), lambda b: (0, b))
    shp = jax.ShapeDtypeStruct((tc, STATE_ALL), F32)
    return pl.pallas_call(
        kern, name="pow_tables", grid=(LANE_BLOCKS,), in_specs=[vec, vec], out_specs=(tab, tab, tab, tab),
        out_shape=(shp, shp, shp, shp), compiler_params=_cparams("parallel"))(abar_re, abar_im)


def _mod_kernel(c_row, w_ada_bf, b_ada):
    def kern(c_ref, w_ref, b_ref, m_ref, s_ref):
        cv = c_ref[...]
        sc = _silu(cv)
        s_ref[...] = sc
        lhs = jnp.broadcast_to(sc, (SUBLANES, D_MODEL)).astype(BF16)
        m_ref[...] = _dot(lhs, w_ref[...]) + b_ref[...]

    return pl.pallas_call(
        kern, name="ada_mod",
        out_shape=(jax.ShapeDtypeStruct((SUBLANES, 3 * D_MODEL), F32), jax.ShapeDtypeStruct((1, D_MODEL), F32)),
        compiler_params=_cparams())(c_row, w_ada_bf, b_ada)


def _row_spec(tr, width=D_MODEL, col=0):
    return pl.BlockSpec((tr, width), lambda c: (c, col))


def _vec_spec(width=D_MODEL):
    return pl.BlockSpec((1, width), lambda c: (0, 0))


def _col_spec(tr):
    return pl.BlockSpec((D_MODEL, tr), lambda c: (0, c))


def _in_norm(x, g1, scale, shift, ride=None):
    seq = x.shape[0]
    tr = min(ROW_CHUNK_WIDE, seq)
    nc = seq // tr
    n_rin = len(ride.inputs) if ride else 0
    n_rout = len(ride.out_shapes) if ride else 0

    def kern(x_ref, g_ref, sc_ref, sh_ref, *rest):
        rin, (h_ref, ht_ref) = rest[:n_rin], rest[n_rin:n_rin + 2]
        rout, rsem = rest[n_rin + 2:n_rin + 2 + n_rout], rest[n_rin + 2 + n_rout:]
        c = pl.program_id(0)
        if ride:
            @pl.when(c == 0)
            def _():
                ride.start(rin, rout, rsem)

        xv = x_ref[...]
        r = lax.rsqrt(jnp.mean(xv * xv, axis=-1, keepdims=True) + RMS_EPS)
        h = ((xv * r) * g_ref[...]) * (1.0 + sc_ref[...]) + sh_ref[...]
        h_ref[...] = h.astype(BF16)
        ht_ref[...] = h.T.astype(BF16)

        if ride:
            @pl.when(c == nc - 1)
            def _():
                ride.wait(rin, rout, rsem)

    outs = pl.pallas_call(
        kern, name="in_norm", grid=(nc,),
        in_specs=[_row_spec(tr), _vec_spec(), _vec_spec(), _vec_spec()] + [_ANY] * n_rin,
        out_specs=(_row_spec(tr), _col_spec(tr)) + (_ANY,) * n_rout,
        out_shape=(jax.ShapeDtypeStruct((seq, D_MODEL), BF16), jax.ShapeDtypeStruct((D_MODEL, seq), BF16))
        + tuple(ride.out_shapes if ride else ()),
        scratch_shapes=list(ride.scratch) if ride else [],
        compiler_params=_cparams("arbitrary" if ride else "parallel"))(x, g1, scale, shift, *(ride.inputs if ride else ()))
    return outs


def _pool_windows(ext, pos, g, w, tr):
    cols = pl.ds(g * POOL_GW, POOL_GW)
    cur = ext[pl.ds(HALO, tr), cols]
    acc = cur
    for k in range(1, w):
        acc = acc + ext[pl.ds(HALO - k, tr), cols]
    cnt = jnp.minimum(pos + 1, w).astype(F32)
    return acc / cnt - cur


def _pool_fwd(proj, pool_w_bf, pscale):
    seq = proj.shape[0]
    tr = min(ROW_CHUNK_WIDE, seq)
    hb = tr // HALO

    def kern(up_ref, halo_ref, zp_ref, pw_ref, ps_ref, y_ref, yt_ref, ext):
        c = pl.program_id(0)
        ext[0:HALO, :] = jnp.where(c > 0, halo_ref[...].astype(F32), 0.0)
        ext[HALO:, :] = up_ref[...].astype(F32)
        pos = c * tr + lax.broadcasted_iota(jnp.int32, (tr, POOL_GW), 0)
        for g, w in enumerate(POOL_WINDOWS):
            cols = pl.ds(g * POOL_GW, POOL_GW)
            pooled = _pool_windows(ext, pos, g, w, tr)
            mixed = _dot(pooled.astype(BF16), pw_ref[g])
            y = mixed * ps_ref[:, cols] * _silu(zp_ref[:, cols].astype(F32))
            y_ref[:, cols] = y.astype(BF16)
            yt_ref[cols, :] = y.T.astype(BF16)

    return pl.pallas_call(
        kern, name="pool_fwd", grid=(seq // tr,),
        in_specs=[_row_spec(tr, col=0),
                  pl.BlockSpec((HALO, D_MODEL), lambda c: (jnp.maximum(c * hb - 1, 0), 0)),
                  _row_spec(tr, col=1),
                  pl.BlockSpec((len(POOL_WINDOWS), POOL_GW, POOL_GW), lambda c: (0, 0, 0)),
                  _vec_spec()],
        out_specs=(_row_spec(tr), _col_spec(tr)),
        out_shape=(jax.ShapeDtypeStruct((seq, D_MODEL), BF16), jax.ShapeDtypeStruct((D_MODEL, seq), BF16)),
        scratch_shapes=[pltpu.VMEM((tr + HALO, D_MODEL), F32)],
        compiler_params=_cparams("parallel"))(proj, proj, proj, pool_w_bf, pscale)


def _pool_bwd(proj, dyp, pool_w_bf, pscale):
    seq = proj.shape[0]
    tr = min(ROW_CHUNK_WIDE, seq)
    hb = tr // HALO
    nc = seq // tr
    n_halo = seq // HALO

    def kern(up_ref, halo_ref, zp_ref, zpn_ref, dyp_ref, dypn_ref, pw_ref, ps_ref,
             d01_ref, dpw_ref, dps_ref, ext, dpn, acc_pw, acc_ps):
        c = pl.program_id(0)

        @pl.when(c == 0)
        def _():
            acc_pw[...] = jnp.zeros_like(acc_pw)
            acc_ps[...] = jnp.zeros_like(acc_ps)

        ext[0:HALO, :] = jnp.where(c > 0, halo_ref[...].astype(F32), 0.0)
        ext[HALO:, :] = up_ref[...].astype(F32)
        pos = c * tr + lax.broadcasted_iota(jnp.int32, (tr, POOL_GW), 0)
        pos_n = (c + 1) * tr + lax.broadcasted_iota(jnp.int32, (HALO, POOL_GW), 0)
        has_next = c < nc - 1
        for g, w in enumerate(POOL_WINDOWS):
            cols = pl.ds(g * POOL_GW, POOL_GW)
            pooled_bf = _pool_windows(ext, pos, g, w, tr).astype(BF16)
            wg = pw_ref[g]
            mixed = _dot(pooled_bf, wg)
            zp = zp_ref[:, cols].astype(F32)
            sz = _silu(zp)
            dyp_g = dyp_ref[:, cols]
            ps = ps_ref[:, cols]
            dmixed = (dyp_g * ps * sz).astype(BF16)
            acc_ps[:, cols] += _acc8(dyp_g * mixed * sz)
            d01_ref[:, pl.ds(D_MODEL + g * POOL_GW, POOL_GW)] = (dyp_g * mixed * ps * _dsilu(zp)).astype(BF16)
            acc_pw[g] += _dot_tn(pooled_bf, dmixed)
            dpooled = _dot_nt(dmixed, wg)
            dmixed_n = (jnp.where(has_next, dypn_ref[:, cols], 0.0) * ps * _silu(zpn_ref[:, cols].astype(F32))).astype(BF16)
            dpooled_n = _dot_nt(dmixed_n, wg)
            dpn[0:tr, :] = dpooled / jnp.minimum(pos + 1, w).astype(F32)
            dpn[tr:, :] = dpooled_n / jnp.minimum(pos_n + 1, w).astype(F32)
            acc = dpn[0:tr, :]
            for k in range(1, w):
                acc = acc + dpn[pl.ds(k, tr), :]
            d01_ref[:, cols] = (acc - dpooled).astype(BF16)

        @pl.when(c == nc - 1)
        def _():
            dpw_ref[...] = acc_pw[...]
            dps_ref[...] = jnp.sum(acc_ps[...], axis=0, keepdims=True)

    nxt = lambda c: (jnp.minimum((c + 1) * hb, n_halo - 1), 0)
    nxt1 = lambda c: (jnp.minimum((c + 1) * hb, n_halo - 1), 1)
    return pl.pallas_call(
        kern, name="pool_bwd", grid=(nc,),
        in_specs=[_row_spec(tr, col=0),
                  pl.BlockSpec((HALO, D_MODEL), lambda c: (jnp.maximum(c * hb - 1, 0), 0)),
                  _row_spec(tr, col=1),
                  pl.BlockSpec((HALO, D_MODEL), nxt1),
                  _row_spec(tr),
                  pl.BlockSpec((HALO, D_MODEL), nxt),
                  pl.BlockSpec((len(POOL_WINDOWS), POOL_GW, POOL_GW), lambda c: (0, 0, 0)),
                  _vec_spec()],
        out_specs=(pl.BlockSpec((tr, 2 * D_MODEL), lambda c: (c, 0)),
                   pl.BlockSpec((len(POOL_WINDOWS), POOL_GW, POOL_GW), lambda c: (0, 0, 0)),
                   _vec_spec()),
        out_shape=(jax.ShapeDtypeStruct((seq, 2 * D_MODEL), BF16),
                   jax.ShapeDtypeStruct((len(POOL_WINDOWS), POOL_GW, POOL_GW), F32),
                   jax.ShapeDtypeStruct((1, D_MODEL), F32)),
        scratch_shapes=[pltpu.VMEM((tr + HALO, D_MODEL), F32), pltpu.VMEM((tr + HALO, POOL_GW), F32),
                        pltpu.VMEM((len(POOL_WINDOWS), POOL_GW, POOL_GW), F32), pltpu.VMEM((SUBLANES, D_MODEL), F32)],
        compiler_params=_cparams("arbitrary"))(proj, proj, proj, proj, dyp, dyp, pool_w_bf, pscale)


def _glu_fwd(ys, proj, glu_w_bf, glu_b):
    seq = ys.shape[0]
    tr = min(ROW_CHUNK_WIDE, seq)

    def kern(ys_ref, zs_ref, w_ref, b_ref, o_ref, ot_ref):
        yg = _gelu(ys_ref[...])
        q = _dot(yg.astype(BF16), w_ref[...]) + b_ref[...]
        y = yg * _sigmoid(q) * _silu(zs_ref[...].astype(F32))
        o_ref[...] = y.astype(BF16)
        ot_ref[...] = y.T.astype(BF16)

    return pl.pallas_call(
        kern, name="glu_fwd", grid=(seq // tr,),
        in_specs=[_row_spec(tr), _row_spec(tr, col=3), pl.BlockSpec((D_MODEL, D_MODEL), lambda c: (0, 0)), _vec_spec()],
        out_specs=(_row_spec(tr), _col_spec(tr)),
        out_shape=(jax.ShapeDtypeStruct((seq, D_MODEL), BF16), jax.ShapeDtypeStruct((D_MODEL, seq), BF16)),
        compiler_params=_cparams("parallel"))(ys, proj, glu_w_bf, glu_b)


def _glu_bwd(ys, proj, dyssm, glu_w_bf, glu_b):
    seq = ys.shape[0]
    tr = min(ROW_CHUNK_WIDE, seq)
    nc = seq // tr

    def kern(ys_ref, zs_ref, dy_ref, w_ref, b_ref, dys_ref, dzs_ref, dq_ref, yg_ref, db_ref, acc_b):
        c = pl.program_id(0)

        @pl.when(c == 0)
        def _():
            acc_b[...] = jnp.zeros_like(acc_b)

        ysv = ys_ref[...]
        yg = _gelu(ysv)
        yg_bf = yg.astype(BF16)
        q = _dot(yg_bf, w_ref[...]) + b_ref[...]
        sg = _sigmoid(q)
        zs = zs_ref[...].astype(F32)
        dyv = dy_ref[...]
        dyglu = dyv * _silu(zs)
        dzs_ref[...] = (dyv * (yg * sg) * _dsilu(zs)).astype(BF16)
        dq = dyglu * yg * sg * (1.0 - sg)
        dq_bf = dq.astype(BF16)
        acc_b[...] += _acc8(dq)
        dyg = dyglu * sg + _dot_nt(dq_bf, w_ref[...])
        dys_ref[...] = dyg * _dgelu(ysv)
        dq_ref[...] = dq_bf
        yg_ref[...] = yg.T.astype(BF16)

        @pl.when(c == nc - 1)
        def _():
            db_ref[...] = jnp.sum(acc_b[...], axis=0, keepdims=True)

    bf = jax.ShapeDtypeStruct((seq, D_MODEL), BF16)
    return pl.pallas_call(
        kern, name="glu_bwd", grid=(nc,),
        in_specs=[_row_spec(tr), _row_spec(tr, col=3), _row_spec(tr),
                  pl.BlockSpec((D_MODEL, D_MODEL), lambda c: (0, 0)), _vec_spec()],
        out_specs=(_row_spec(tr), _row_spec(tr), _row_spec(tr), _col_spec(tr), _vec_spec()),
        out_shape=(jax.ShapeDtypeStruct((seq, D_MODEL), F32), bf, bf, jax.ShapeDtypeStruct((D_MODEL, seq), BF16),
                   jax.ShapeDtypeStruct((1, D_MODEL), F32)),
        scratch_shapes=[pltpu.VMEM((SUBLANES, D_MODEL), F32)],
        compiler_params=_cparams("arbitrary"))(ys, proj, dyssm, glu_w_bf, glu_b)


def _out_fwd_bwd(ypool, yssm, proj, x, tgt, gate, g2, wbp_bf, wbs_bf, wout_bf):
    seq = x.shape[0]
    tr = min(ROW_CHUNK, seq)
    nc = seq // tr

    def kern(yp_ref, ysm_ref, gp_ref, gs_ref, x_ref, t_ref, gate_ref, g2_ref, wbp_ref, wbs_ref, wo_ref,
             dy_ref, dyp_ref, dys_ref, d45_ref, mb_ref, dob_ref, dbp_ref, dbs_ref, loss_ref, dgate_ref, dg2_ref,
             acc_l, acc_gate, acc_g2):
        c = pl.program_id(0)

        @pl.when(c == 0)
        def _():
            acc_l[...] = jnp.zeros_like(acc_l)
            acc_gate[...] = jnp.zeros_like(acc_gate)
            acc_g2[...] = jnp.zeros_like(acc_g2)

        bp = _dot(yp_ref[...], wbp_ref[...])
        bs = _dot(ysm_ref[...], wbs_ref[...])
        sp = _sigmoid(gp_ref[...].astype(F32))
        ss = _sigmoid(gs_ref[...].astype(F32))
        merged = sp * bp + ss * bs
        mb = merged.astype(BF16)
        out = _dot(mb, wo_ref[...])
        r2 = lax.rsqrt(jnp.mean(out * out, axis=-1, keepdims=True) + RMS_EPS)
        oh = out * r2
        gate_v, g2_v = gate_ref[...], g2_ref[...]
        ohg = oh * g2_v
        diff = (x_ref[...] + gate_v * ohg) - t_ref[...]
        acc_l[...] += _acc8(diff * diff)
        dyv = diff * (1.0 / D_MODEL)
        dy_ref[...] = dyv
        acc_gate[...] += _acc8(dyv * ohg)
        t = dyv * gate_v
        acc_g2[...] += _acc8(t * oh)
        doh = t * g2_v
        dout = r2 * (doh - oh * jnp.mean(doh * oh, axis=-1, keepdims=True))
        dob = dout.astype(BF16)
        dmerged = _dot_nt(dob, wo_ref[...])
        dbp = (dmerged * sp).astype(BF16)
        dbs = (dmerged * ss).astype(BF16)
        d45_ref[:, 0:D_MODEL] = (dmerged * bp * sp * (1.0 - sp)).astype(BF16)
        d45_ref[:, D_MODEL:] = (dmerged * bs * ss * (1.0 - ss)).astype(BF16)
        dyp_ref[...] = _dot_nt(dbp, wbp_ref[...])
        dys_ref[...] = _dot_nt(dbs, wbs_ref[...])
        mb_ref[...] = merged.T.astype(BF16)
        dob_ref[...] = dob
        dbp_ref[...] = dbp
        dbs_ref[...] = dbs

        @pl.when(c == nc - 1)
        def _():
            tot = jnp.sum(acc_l[...], axis=0, keepdims=True)
            loss_ref[...] = jnp.sum(tot, axis=1, keepdims=True) * (0.5 / D_MODEL)
            dgate_ref[...] = jnp.sum(acc_gate[...], axis=0, keepdims=True)
            dg2_ref[...] = jnp.sum(acc_g2[...], axis=0, keepdims=True)

    wspec = pl.BlockSpec((D_MODEL, D_MODEL), lambda c: (0, 0))
    f32 = jax.ShapeDtypeStruct((seq, D_MODEL), F32)
    bf = jax.ShapeDtypeStruct((seq, D_MODEL), BF16)
    vec = jax.ShapeDtypeStruct((1, D_MODEL), F32)
    acc = pltpu.VMEM((SUBLANES, D_MODEL), F32)
    return pl.pallas_call(
        kern, name="out_fwd_bwd", grid=(nc,),
        in_specs=[_row_spec(tr), _row_spec(tr), _row_spec(tr, col=4), _row_spec(tr, col=5), _row_spec(tr), _row_spec(tr),
                  _vec_spec(), _vec_spec(), wspec, wspec, wspec],
        out_specs=(_row_spec(tr), _row_spec(tr), _row_spec(tr), pl.BlockSpec((tr, 2 * D_MODEL), lambda c: (c, 0)),
                   _col_spec(tr), _row_spec(tr), _row_spec(tr), _row_spec(tr),
                   pl.BlockSpec((1, 1), lambda c: (0, 0)), _vec_spec(), _vec_spec()),
        out_shape=(f32, f32, f32, jax.ShapeDtypeStruct((seq, 2 * D_MODEL), BF16),
                   jax.ShapeDtypeStruct((D_MODEL, seq), BF16), bf, bf, bf,
                   jax.ShapeDtypeStruct((1, 1), F32), vec, vec),
        scratch_shapes=[acc, acc, acc],
        compiler_params=_cparams("arbitrary"))(ypool, yssm, proj, proj, x, tgt, gate, g2, wbp_bf, wbs_bf, wout_bf)


def _in_bwd(dh, x, dy, g1, scale):
    seq = x.shape[0]
    tr = min(ROW_CHUNK_WIDE, seq)
    nc = seq // tr

    def kern(dh_ref, x_ref, dy_ref, g_ref, sc_ref, dx_ref, dsh_ref, dsc_ref, dg_ref, a_sh, a_sc, a_g):
        c = pl.program_id(0)

        @pl.when(c == 0)
        def _():
            a_sh[...] = jnp.zeros_like(a_sh)
            a_sc[...] = jnp.zeros_like(a_sc)
            a_g[...] = jnp.zeros_like(a_g)

        xv = x_ref[...]
        r = lax.rsqrt(jnp.mean(xv * xv, axis=-1, keepdims=True) + RMS_EPS)
        xh = xv * r
        g = g_ref[...]
        dhv = dh_ref[...]
        a_sh[...] += _acc8(dhv)
        a_sc[...] += _acc8(dhv * (xh * g))
        dn = dhv * (1.0 + sc_ref[...])
        a_g[...] += _acc8(dn * xh)
        dxh = dn * g
        dx_ref[...] = dy_ref[...] + r * (dxh - xh * jnp.mean(dxh * xh, axis=-1, keepdims=True))

        @pl.when(c == nc - 1)
        def _():
            dsh_ref[...] = jnp.sum(a_sh[...], axis=0, keepdims=True)
            dsc_ref[...] = jnp.sum(a_sc[...], axis=0, keepdims=True)
            dg_ref[...] = jnp.sum(a_g[...], axis=0, keepdims=True)

    vec = jax.ShapeDtypeStruct((1, D_MODEL), F32)
    acc = pltpu.VMEM((SUBLANES, D_MODEL), F32)
    return pl.pallas_call(
        kern, name="in_bwd", grid=(nc,),
        in_specs=[_row_spec(tr), _row_spec(tr), _row_spec(tr), _vec_spec(), _vec_spec()],
        out_specs=(_row_spec(tr), _vec_spec(), _vec_spec(), _vec_spec()),
        out_shape=(jax.ShapeDtypeStruct((seq, D_MODEL), F32), vec, vec, vec),
        scratch_shapes=[acc, acc, acc],
        compiler_params=_cparams("arbitrary"))(dh, x, dy, g1, scale)


SLAB = 2 * SUBLANES


def _local_scan(a_re, a_im, br, bi, xr, xi, row0, ls, reverse, init=None, xb=None):
    if init is None:
        x_re = jnp.zeros((SUBLANES, STATE_W), F32)
        x_im = jnp.zeros((SUBLANES, STATE_W), F32)
    else:
        x_re, x_im = init
    for i in (range(ls - 1, -1, -1) if reverse else range(ls)):
        src = pl.ds(SUBLANES * i, SUBLANES)
        dst = pl.ds(row0 + SUBLANES * i, SUBLANES)
        n_re = a_re * x_re - a_im * x_im + br[src, :]
        n_im = a_re * x_im + a_im * x_re + bi[src, :]
        if xb is not None and i % 2 == 1:
            pair = pl.ds(SUBLANES * (i - 1), SLAB)
            xb[0][pair, :] = jnp.concatenate([x_re, n_re], axis=0).astype(BF16)
            xb[1][pair, :] = jnp.concatenate([x_im, n_im], axis=0).astype(BF16)
        x_re, x_im = n_re, n_im
        xr[dst, :] = x_re
        xi[dst, :] = x_im
    return x_re, x_im


def _two(v):
    return jnp.concatenate([v, v], axis=0)


def _unpermute_rhs(v, sel):
    hi = v.astype(BF16)
    r1 = v - hi.astype(F32)
    mid = r1.astype(BF16)
    lo = (r1 - mid.astype(F32)).astype(BF16)
    return _dot(hi, sel) + _dot(mid, sel) + _dot(lo, sel)


def _scan_specs(tc, nb, rows_of):
    return dict(
        us=pl.BlockSpec((tc, nb * LANES), lambda b, c: (rows_of(c), 2 * D_MODEL // (nb * LANES) + b)),
        tok=pl.BlockSpec((tc, nb * LANES), lambda b, c: (rows_of(c), b)),
        bblk=pl.BlockSpec((nb, LANES, STATE_W), lambda b, c: (b, 0, 0)),
        cblk=pl.BlockSpec((nb, STATE_W, LANES), lambda b, c: (b, 0, 0)),
        vec=pl.BlockSpec((1, nb * STATE_W), lambda b, c: (0, b)),
        tab=pl.BlockSpec((tc, nb * STATE_W), lambda b, c: (0, b)),
        car=pl.BlockSpec((SUBLANES, nb * STATE_W), lambda b, c: (rows_of(c), b)),
        dvec=pl.BlockSpec((1, nb * LANES), lambda b, c: (0, b)))


def _ssm_scan_fwd(proj, bb_re, bb_im, cm_re, cm_im, abar_re, abar_im, pw_re, pw_im, d_skip, tc):
    seq = proj.shape[0]
    nc = seq // tc
    ls = tc // SUBLANES
    nb = SCAN_BLOCKS

    def kern(us_ref, bbr_ref, bbi_ref, cmr_ref, cmi_ref, ar_ref, ai_ref, pwr_ref, pwi_ref, d_ref,
             ys_ref, ecr_ref, eci_ref, bur, bui, car_r, car_i, end_r, end_i, upb, xb_r, xb_i, *nat):
        c = pl.program_id(1)

        @pl.when(c == 0)
        def _():
            car_r[...] = jnp.zeros_like(car_r)
            car_i[...] = jnp.zeros_like(car_i)

        for j in range(nb):
            cols = pl.ds(j * LANES, LANES)
            scols = pl.ds(j * STATE_W, STATE_W)
            nat[j][...] = us_ref[:, cols].astype(F32)
            for i in range(ls):
                upb[j, pl.ds(SUBLANES * i, SUBLANES), :] = nat[j][pl.ds(i, SUBLANES, stride=ls), :]
            u = upb[j]
            up = u.astype(BF16)
            bur[j] = _dot(up, bbr_ref[j])
            bui[j] = _dot(up, bbi_ref[j])
            a_re = jnp.broadcast_to(ar_ref[:, scols], (SUBLANES, STATE_W))
            a_im = jnp.broadcast_to(ai_ref[:, scols], (SUBLANES, STATE_W))
            x_re, x_im = _local_scan(a_re, a_im, bur.at[j], bui.at[j], bur.at[j], bui.at[j], 0, ls, False)
            end_r[j] = x_re
            end_i[j] = x_im
            big_re = pwr_ref[tc - 1:tc, scols]
            big_im = pwi_ref[tc - 1:tc, scols]
            e_re = car_r[j, 0:1, :]
            e_im = car_i[j, 0:1, :]
            for s in range(SUBLANES):
                n_re = end_r[j, s:s + 1, :] + big_re * e_re - big_im * e_im
                n_im = end_i[j, s:s + 1, :] + big_re * e_im + big_im * e_re
                e_re, e_im = n_re, n_im
                if s < SUBLANES - 1:
                    car_r[j, s + 1:s + 2, :] = e_re
                    car_i[j, s + 1:s + 2, :] = e_im
            ec_re = car_r[j]
            ec_im = car_i[j]
            ecr_ref[:, scols] = ec_re
            eci_ref[:, scols] = ec_im
            e2_re, e2_im = _two(ec_re), _two(ec_im)
            for k in range(tc // SLAB):
                rows_k = pl.ds(SLAB * k, SLAB)
                p_re = pwr_ref[rows_k, scols]
                p_im = pwi_ref[rows_k, scols]
                xb_r[j, rows_k, :] = (bur[j, rows_k, :] + p_re * e2_re - p_im * e2_im).astype(BF16)
                xb_i[j, rows_k, :] = (bui[j, rows_k, :] + p_re * e2_im + p_im * e2_re).astype(BF16)
            upb[j] = _dot(xb_r[j], cmr_ref[j]) - _dot(xb_i[j], cmi_ref[j]) + d_ref[:, cols] * u
            for i in range(ls):
                nat[j][pl.ds(i, SUBLANES, stride=ls), :] = upb[j, pl.ds(SUBLANES * i, SUBLANES), :]
            ys_ref[:, cols] = nat[j][...]
            car_r[j, 0:1, :] = e_re
            car_i[j, 0:1, :] = e_im

    sp = _scan_specs(tc, nb, lambda c: c)
    carry_shape = jax.ShapeDtypeStruct((nc * SUBLANES, STATE_ALL), F32)
    small = pltpu.VMEM((nb, SUBLANES, STATE_W), F32)
    big = pltpu.VMEM((nb, tc, STATE_W), F32)
    return pl.pallas_call(
        kern, name="ssm_scan_fwd", grid=(LANE_BLOCKS // nb, nc),
        in_specs=[sp["us"], sp["bblk"], sp["bblk"], sp["cblk"], sp["cblk"], sp["vec"], sp["vec"], sp["tab"], sp["tab"],
                  sp["dvec"]],
        out_specs=(sp["tok"], sp["car"], sp["car"]),
        out_shape=(jax.ShapeDtypeStruct((seq, D_MODEL), F32), carry_shape, carry_shape),
        scratch_shapes=[big, big, small, small, small, small, pltpu.VMEM((nb, tc, LANES), F32),
                        pltpu.VMEM((nb, tc, STATE_W), BF16), pltpu.VMEM((nb, tc, STATE_W), BF16)]
        + [pltpu.VMEM((tc, LANES), F32)] * nb,
        compiler_params=_cparams("parallel", "arbitrary"),
    )(proj, bb_re, bb_im, cm_re, cm_im, abar_re, abar_im, pw_re, pw_im, d_skip)


def _ssm_scan_bwd(proj, dys, ec_re, ec_im, bb_re, bb_im, cm_re, cm_im, abar_re, abar_im,
                  pw_re, pw_im, pv_re, pv_im, d_skip, tc):
    seq = proj.shape[0]
    nc = seq // tc
    ls = tc // SUBLANES
    nb = SCAN_BLOCKS

    def kern(us_ref, dys_ref, ecr_ref, eci_ref, bbr_ref, bbi_ref, cmr_ref, cmi_ref, ar_ref, ai_ref,
             pwr_ref, pwi_ref, pvr_ref, pvi_ref, d_ref,
             dus_ref, dbbr_ref, dbbi_ref, dcmr_ref, dcmi_ref, dar_ref, dai_ref, dd_ref,
             bur, bui, xr, xi, gr, gi, fc_r, fc_i, a_bbr, a_bbi, a_cmr, a_cmi, a_ar, a_ai, a_dd, upb, dpb, hb_r, hb_i,
             *nat):
        c = pl.program_id(1)

        @pl.when(c == 0)
        def _():
            for ref in (fc_r, fc_i, a_bbr, a_bbi, a_cmr, a_cmi, a_ar, a_ai, a_dd):
                ref[...] = jnp.zeros_like(ref)

        for j in range(nb):
            cols = pl.ds(j * LANES, LANES)
            scols = pl.ds(j * STATE_W, STATE_W)
            nat_u, nat_d = nat[2 * j], nat[2 * j + 1]
            nat_u[...] = us_ref[:, cols].astype(F32)
            nat_d[...] = dys_ref[:, cols]
            for i in range(ls):
                rows_i = pl.ds(SUBLANES * i, SUBLANES)
                upb[j, rows_i, :] = nat_u[pl.ds(i, SUBLANES, stride=ls), :]
                dpb[j, rows_i, :] = nat_d[pl.ds(i, SUBLANES, stride=ls), :]
            u = upb[j]
            dysv = dpb[j]
            a_dd[j] += _acc8(dysv * u)
            up = u.astype(BF16)
            bur[j] = _dot(up, bbr_ref[j])
            bui[j] = _dot(up, bbi_ref[j])
            a_re = jnp.broadcast_to(ar_ref[:, scols], (SUBLANES, STATE_W))
            a_im = jnp.broadcast_to(ai_ref[:, scols], (SUBLANES, STATE_W))
            ec_r = ecr_ref[:, scols]
            ec_i = eci_ref[:, scols]
            xr[j, 0:SUBLANES, :] = ec_r
            xi[j, 0:SUBLANES, :] = ec_i
            _local_scan(a_re, a_im, bur.at[j], bui.at[j], xr.at[j], xi.at[j], SUBLANES, ls, False, init=(ec_r, ec_i),
                        xb=(hb_r.at[j], hb_i.at[j]))
            dysp = dysv.astype(BF16)
            a_cmr[j] += _dot_tn(dysp, hb_r[j])
            a_cmi[j] -= _dot_tn(dysp, hb_i[j])
            gr[j] = _dot_nt(dysp, cmr_ref[j])
            gi[j] = -_dot_nt(dysp, cmi_ref[j])
            _local_scan(a_re, -a_im, gr.at[j], gi.at[j], gr.at[j], gi.at[j], 0, ls, True)
            big_re = pwr_ref[tc - 1:tc, scols]
            big_im = -pwi_ref[tc - 1:tc, scols]
            f_re = fc_r[j, SUBLANES - 1:SUBLANES, :]
            f_im = fc_i[j, SUBLANES - 1:SUBLANES, :]
            for s in range(SUBLANES - 1, -1, -1):
                n_re = gr[j, s:s + 1, :] + big_re * f_re - big_im * f_im
                n_im = gi[j, s:s + 1, :] + big_re * f_im + big_im * f_re
                f_re, f_im = n_re, n_im
                if s > 0:
                    fc_r[j, s - 1:s, :] = f_re
                    fc_i[j, s - 1:s, :] = f_im
            f2_r, f2_i = _two(fc_r[j]), _two(fc_i[j])
            acc_r = jnp.zeros((SUBLANES, STATE_W), F32)
            acc_i = jnp.zeros((SUBLANES, STATE_W), F32)
            for k in range(tc // SLAB):
                rows_k = pl.ds(SLAB * k, SLAB)
                q_re = pvr_ref[rows_k, scols]
                q_im = pvi_ref[rows_k, scols]
                lam_re = gr[j, rows_k, :] + q_re * f2_r + q_im * f2_i
                lam_im = gi[j, rows_k, :] + q_re * f2_i - q_im * f2_r
                xp_re = xr[j, rows_k, :]
                xp_im = xi[j, rows_k, :]
                d_r = lam_re * xp_re + lam_im * xp_im
                d_i = lam_im * xp_re - lam_re * xp_im
                acc_r = acc_r + (d_r[0:SUBLANES] + d_r[SUBLANES:])
                acc_i = acc_i + (d_i[0:SUBLANES] + d_i[SUBLANES:])
                hb_r[j, rows_k, :] = lam_re.astype(BF16)
                hb_i[j, rows_k, :] = lam_im.astype(BF16)
            a_ar[j] += acc_r
            a_ai[j] += acc_i
            fc_r[j, SUBLANES - 1:SUBLANES, :] = f_re
            fc_i[j, SUBLANES - 1:SUBLANES, :] = f_im
            lb_re = hb_r[j]
            lb_im = hb_i[j]
            a_bbr[j] += _dot_tn(up, lb_re)
            a_bbi[j] += _dot_tn(up, lb_im)
            dpb[j] = _dot_nt(lb_re, bbr_ref[j]) + _dot_nt(lb_im, bbi_ref[j]) + dysv * d_ref[:, cols]
            for i in range(ls):
                nat_d[pl.ds(i, SUBLANES, stride=ls), :] = dpb[j, pl.ds(SUBLANES * i, SUBLANES), :]
            dus_ref[:, cols] = nat_d[...].astype(BF16)

        @pl.when(c == nc - 1)
        def _():
            row_g = lax.broadcasted_iota(jnp.int32, (LANES, STATE_W), 0) // SSM_H
            col_g = lax.broadcasted_iota(jnp.int32, (LANES, STATE_W), 1) // SSM_P
            fold = (lax.broadcasted_iota(jnp.int32, (STATE_W, SSM_P), 0) % SSM_P
                    == lax.broadcasted_iota(jnp.int32, (STATE_W, SSM_P), 1)).astype(BF16)
            for j in range(nb):
                rows_j = pl.ds(j * LANES, LANES)
                for acc, out in ((a_bbr, dbbr_ref), (a_bbi, dbbi_ref), (a_cmr, dcmr_ref), (a_cmi, dcmi_ref)):
                    out[rows_j, :] = _unpermute_rhs(jnp.where(row_g == col_g, acc[j], 0.0), fold)
                dar_ref[:, pl.ds(j * STATE_W, STATE_W)] = jnp.sum(a_ar[j], axis=0, keepdims=True)
                dai_ref[:, pl.ds(j * STATE_W, STATE_W)] = jnp.sum(a_ai[j], axis=0, keepdims=True)
                dd_ref[:, pl.ds(j * LANES, LANES)] = jnp.sum(a_dd[j], axis=0, keepdims=True)

    sp = _scan_specs(tc, nb, lambda c: nc - 1 - c)
    ghp = pl.BlockSpec((nb * LANES, SSM_P), lambda b, c: (b, 0))
    ghp_shape = jax.ShapeDtypeStruct((SSM_G * SSM_H, SSM_P), F32)
    small = pltpu.VMEM((nb, SUBLANES, STATE_W), F32)
    big = pltpu.VMEM((nb, tc, STATE_W), F32)
    bigp = pltpu.VMEM((nb, tc + SUBLANES, STATE_W), F32)
    blk = pltpu.VMEM((nb, LANES, STATE_W), F32)
    tok = pltpu.VMEM((nb, tc, LANES), F32)
    return pl.pallas_call(
        kern, name="ssm_scan_bwd", grid=(LANE_BLOCKS // nb, nc),
        in_specs=[sp["us"], sp["tok"], sp["car"], sp["car"], sp["bblk"], sp["bblk"], sp["cblk"], sp["cblk"],
                  sp["vec"], sp["vec"], sp["tab"], sp["tab"], sp["tab"], sp["tab"], sp["dvec"]],
        out_specs=(sp["tok"], ghp, ghp, ghp, ghp, sp["vec"], sp["vec"], sp["dvec"]),
        out_shape=(jax.ShapeDtypeStruct((seq, D_MODEL), BF16), ghp_shape, ghp_shape, ghp_shape, ghp_shape,
                   jax.ShapeDtypeStruct((1, STATE_ALL), F32), jax.ShapeDtypeStruct((1, STATE_ALL), F32),
                   jax.ShapeDtypeStruct((1, D_MODEL), F32)),
        scratch_shapes=[big, big, bigp, bigp, big, big, small, small, blk, blk, blk, blk,
                        small, small, pltpu.VMEM((nb, SUBLANES, LANES), F32), tok, tok,
                        pltpu.VMEM((nb, tc, STATE_W), BF16), pltpu.VMEM((nb, tc, STATE_W), BF16)]
        + [pltpu.VMEM((tc, LANES), F32)] * (2 * nb),
        compiler_params=_cparams("parallel", "arbitrary"),
    )(proj, dys, ec_re, ec_im, bb_re, bb_im, cm_re, cm_im, abar_re, abar_im, pw_re, pw_im, pv_re, pv_im, d_skip)


def _eye5():
    return jnp.asarray(np.eye(GROUPS_PER_BLOCK, dtype=np.float32)[None, :, None, :, None])


def _embed_b(bb_t):
    t = bb_t.transpose(1, 0, 2).reshape(LANE_BLOCKS, GROUPS_PER_BLOCK, SSM_H, 1, SSM_P)
    return (t * _eye5()).reshape(LANE_BLOCKS, LANES, STATE_W)


def _embed_c(c_ghp):
    t = c_ghp.transpose(0, 2, 1).reshape(LANE_BLOCKS, GROUPS_PER_BLOCK, SSM_P, 1, SSM_H)
    return (t * _eye5()).reshape(LANE_BLOCKS, STATE_W, LANES)


def _local_step(x, c_row, tgt, w_ada_bf, b_ada, g1, g2, w_in_bf, pool_w_bf, pscale, a_re, a_im, log_dt,
                b_re_t, b_im_t, c_re, c_im, d_skip, glu_w_bf, glu_b, wbp_bf, wbs_bf, wout_bf,
                early_weight=None, late_weights=None, ride_for_dw_in=None, ride_for_dh=None):
    seq = x.shape[0]
    tc = min(SCAN_CHUNK, seq)
    mod8, silu_c = _mod_kernel(c_row, w_ada_bf, b_ada)
    mod = mod8[0:1]
    shift, scale, gate = mod[:, 0:D_MODEL], mod[:, D_MODEL:2 * D_MODEL], mod[:, 2 * D_MODEL:]

    abar_re, abar_im, bb_re_t, bb_im_t = _ssm_params(a_re, a_im, log_dt, b_re_t, b_im_t)
    abar_re_f, abar_im_f = abar_re.reshape(1, STATE_ALL), abar_im.reshape(1, STATE_ALL)
    pw_re, pw_im, pv_re, pv_im = _pow_tables(abar_re_f, abar_im_f, tc)
    bbe_re, bbe_im = _embed_b(bb_re_t).astype(BF16), _embed_b(bb_im_t).astype(BF16)
    cme_re, cme_im = _embed_c(c_re).astype(BF16), _embed_c(c_im).astype(BF16)
    d_row = d_skip.reshape(1, D_MODEL)

    if early_weight:
        h, h_t, *gathered = _in_norm(x, g1, scale, shift, ride=early_weight[0])
        w_in_bf, w_in_cols = early_weight[1](*gathered)
    else:
        h, h_t = _in_norm(x, g1, scale, shift)
        w_in_cols = [w_in_bf]
    bn_proj = next(b for b in (1024, 768, 512, 256) if w_in_cols[0].shape[1] % b == 0)
    if late_weights:
        proj, *gathered = _mm([h], w_in_cols, name="proj", out_dtype=BF16, bm=1024, bn=bn_proj, bk=1024,
                              ride=late_weights[0])
        pool_w_bf, glu_w_bf, wbp_bf, wbs_bf, wout_bf = late_weights[1](*gathered)
    else:
        proj = _mm([h], w_in_cols, name="proj", out_dtype=BF16, bm=1024, bn=bn_proj, bk=1024)
    ypool, ypool_t = _pool_fwd(proj, pool_w_bf, pscale)
    ys, ec_re, ec_im = _ssm_scan_fwd(proj, bbe_re, bbe_im, cme_re, cme_im, abar_re_f, abar_im_f,
                                      pw_re, pw_im, d_row, tc)
    yssm, yssm_t = _glu_fwd(ys, proj, glu_w_bf, glu_b)
    (dy, dypool, dyssm, d45, merged_t, dob, dbp, dbs, loss, dgate, dg2) = _out_fwd_bwd(
        ypool, yssm, proj, x, tgt, gate, g2, wbp_bf, wbs_bf, wout_bf)

    d_wout = _mm([merged_t], [dob], name="dw_out", bm=1024, bn=1024, bk=1024)
    d_wbp = _mm([ypool_t], [dbp], name="dw_bp", bm=1024, bn=1024, bk=1024)
    d_wbs = _mm([yssm_t], [dbs], name="dw_bs", bm=1024, bn=1024, bk=1024)
    dys, dzs, dq, yg_t, d_glu_b = _glu_bwd(ys, proj, dyssm, glu_w_bf, glu_b)
    d_glu_w = _mm([yg_t], [dq], name="dw_glu", bm=1024, bn=1024, bk=1024)
    (dus, dbbe_re, dbbe_im, dcme_re, dcme_im, d_abar_re, d_abar_im, d_dskip) = _ssm_scan_bwd(
        proj, dys, ec_re, ec_im, bbe_re, bbe_im, cme_re, cme_im, abar_re_f, abar_im_f,
        pw_re, pw_im, pv_re, pv_im, d_row, tc)
    d01, d_pool_w, d_pscale = _pool_bwd(proj, dypool, pool_w_bf, pscale)
    dparts = [d01, dus, dzs, d45]
    small_ready = dict(
        dg2=dg2, d_pscale=d_pscale, d_glu_b=d_glu_b, d_dskip=d_dskip, d_abar_re=d_abar_re, d_abar_im=d_abar_im,
        d_bb_re_t=dbbe_re.reshape(SSM_G, SSM_H, SSM_P).transpose(1, 0, 2),
        d_bb_im_t=dbbe_im.reshape(SSM_G, SSM_H, SSM_P).transpose(1, 0, 2),
        d_c_re=dcme_re.reshape(SSM_G, SSM_H, SSM_P), d_c_im=dcme_im.reshape(SSM_G, SSM_H, SSM_P))
    ride = ride_for_dw_in(small_ready) if ride_for_dw_in else None
    d_win = _mm([h_t], dparts, name="dw_in", bm=1024, bn=1024, bk=1024, ride=ride)
    rode_dw_in = ()
    if ride:
        d_win, rode_dw_in = d_win[0], tuple(d_win[1:])
    big_grads = dict(d_win=d_win, d_glu_w=d_glu_w, d_wbp=d_wbp, d_wbs=d_wbs, d_wout=d_wout, d_pool_w=d_pool_w)
    ride = ride_for_dh(big_grads) if ride_for_dh else None
    dh = _mm(dparts, [w_in_bf], tb=True, name="dh", bm=1024, bn=1024, bk=1024, ride=ride)
    rode = ()
    if ride:
        dh, rode = dh[0], tuple(dh[1:])
    grad_x, dshift, dscale, dg1 = _in_bwd(dh, x, dy, g1, scale)
    dmod = jnp.concatenate([dshift, dscale, dgate], axis=1)
    return dict(
        rode=rode, rode_dw_in=rode_dw_in, loss=loss[0, 0], grad_x=grad_x, dmod=dmod, silu_c=silu_c, dg1=dg1,
        **small_ready, **big_grads)


def _position():
    x, y, c = lax.axis_index("x"), lax.axis_index("y"), lax.axis_index("c")
    chips = [(1 - x, y), (x, 1 - y), (1 - x, 1 - y)]
    return x, y, c, chips


_ANY = pl.BlockSpec(memory_space=pl.ANY)
COMM_CHUNKS = 4
COMM_ROW_ALIGN = 16


def _row_chunks(rows, k):
    assert rows % (k * COMM_ROW_ALIGN) == 0, (rows, k)
    step = rows // k
    return [(q * step, step) for q in range(k)]


def _ag_weights_ride(packed, n_chunks=COMM_CHUNKS):
    rows, width = packed.shape
    half = rows // 2
    chunks = _row_chunks(half, n_chunks)
    nq = len(chunks)

    def parts(p_ref, out_ref, send_sems, recv_sems):
        x, y, c, chips = _position()
        sibling = (x, y, 1 - c)

        def copy(k, chip, h, q, to, src=None):
            start, size = chunks[q]
            rows_q = pl.ds(h * half + start, size)
            dst = out_ref.at[2 * chip[0] + chip[1], rows_q, :]
            return pltpu.make_async_remote_copy(
                src_ref=dst if src is None else src.at[rows_q, :], dst_ref=dst, send_sem=send_sems.at[k * nq + q],
                recv_sem=recv_sems.at[k * nq + q], device_id=to, device_id_type=MESH_ID)

        mine = [copy(6 + h, (x, y), h, q, sibling, src=p_ref) for h in range(2) for q in range(nq)]
        first = [copy(j, (x, y), c, q, (*chip, c), src=p_ref) for q in range(nq) for j, chip in enumerate(chips)]
        return (x, y, c), chips, sibling, copy, mine, first

    def start(ins, outs, sems):
        _, _, _, _, mine, first = parts(ins[0], outs[0], sems[0], sems[1])
        for cp in first + mine:
            cp.start()

    def wait(ins, outs, sems):
        (x, y, c), chips, sibling, copy, mine, first = parts(ins[0], outs[0], sems[0], sems[1])
        passed = []
        for q in range(nq):
            for j, chip in enumerate(chips):
                copy(j, chip, c, q, (x, y, c)).wait_recv()
                fwd = copy(3 + j, chip, c, q, sibling)
                fwd.start()
                passed.append(fwd)
        for q in range(nq):
            for j, chip in enumerate(chips):
                copy(3 + j, chip, 1 - c, q, (x, y, c)).wait_recv()
        for cp in mine:
            cp.wait_recv()
        for cp in first + passed + mine:
            cp.wait_send()

    return _Ride([packed], [jax.ShapeDtypeStruct((N_CHIPS, rows, width), packed.dtype)],
                 [pltpu.SemaphoreType.DMA((8 * nq,)), pltpu.SemaphoreType.DMA((8 * nq,))], start, wait)


def _join_rides(rides):
    def split(seq, counts):
        out, at = [], 0
        for n in counts:
            out.append(seq[at:at + n])
            at += n
        return out

    n_in = [len(r.inputs) for r in rides]
    n_out = [len(r.out_shapes) for r in rides]
    n_sem = [len(r.scratch) for r in rides]

    def start(ins, outs, sems):
        for r, i, o, s in zip(rides, split(ins, n_in), split(outs, n_out), split(sems, n_sem)):
            r.start(i, o, s)

    def wait(ins, outs, sems):
        for r, i, o, s in zip(rides, split(ins, n_in), split(outs, n_out), split(sems, n_sem)):
            r.wait(i, o, s)

    return _Ride([a for r in rides for a in r.inputs], [a for r in rides for a in r.out_shapes],
                 [a for r in rides for a in r.scratch], start, wait)


def _run_ride(ride, name):
    n_in, n_out = len(ride.inputs), len(ride.out_shapes)

    def body(*refs):
        ins, outs, sems = refs[:n_in], refs[n_in:n_in + n_out], refs[n_in + n_out:]
        ride.start(ins, outs, sems)
        ride.wait(ins, outs, sems)

    return pl.pallas_call(
        body, name=name, in_specs=[_ANY] * n_in, out_specs=(_ANY,) * n_out, out_shape=tuple(ride.out_shapes),
        scratch_shapes=list(ride.scratch))(*ride.inputs)


def _small_allgather_ride(buf):
    rows, width = buf.shape
    chunks = _row_chunks(rows, COMM_CHUNKS)
    nq = len(chunks)

    def parts(b_ref, all_ref, send_sems, recv_sems, local_sem):
        x, y, c, chips = _position()
        me, sibling = (x, y, c), (x, y, 1 - c)

        def copy(k, block, q, to, src=None):
            rows_q = pl.ds(chunks[q][0], chunks[q][1])
            dst = all_ref.at[4 * block[0] + 2 * block[1] + block[2], rows_q, :]
            return pltpu.make_async_remote_copy(
                src_ref=dst if src is None else src.at[rows_q, :], dst_ref=dst, send_sem=send_sems.at[k * nq + q],
                recv_sem=recv_sems.at[k * nq + q], device_id=to, device_id_type=MESH_ID)

        mine = pltpu.make_async_copy(b_ref, all_ref.at[4 * x + 2 * y + c], local_sem)
        first = []
        for q in range(nq):
            first += [copy(1 + j, me, q, (*chip, c), src=b_ref) for j, chip in enumerate(chips)]
            first.append(copy(0, me, q, sibling, src=b_ref))
        return me, sibling, c, chips, copy, mine, first

    def start(ins, outs, sems):
        _, _, _, _, _, mine, first = parts(ins[0], outs[0], *sems)
        mine.start()
        for cp in first:
            cp.start()

    def wait(ins, outs, sems):
        me, sibling, c, chips, copy, mine, first = parts(ins[0], outs[0], *sems)
        passed = []
        for q in range(nq):
            for j, chip in enumerate(chips):
                copy(1 + j, (*chip, c), q, me).wait_recv()
                fwd = copy(4 + j, (*chip, c), q, sibling)
                fwd.start()
                passed.append(fwd)
        for q in range(nq):
            copy(0, sibling, q, me).wait_recv()
            for j, chip in enumerate(chips):
                copy(4 + j, (*chip, 1 - c), q, me).wait_recv()
        for cp in first + passed:
            cp.wait_send()
        mine.wait()

    return _Ride([buf], [jax.ShapeDtypeStruct((N_DEV, rows, width), F32)],
                 [pltpu.SemaphoreType.DMA((7 * nq,)), pltpu.SemaphoreType.DMA((7 * nq,)), pltpu.SemaphoreType.DMA],
                 start, wait)


def _sum_devices(blocks):
    n, rows, width = blocks.shape
    rb = rows // 2 if (rows // 2) % SUBLANES == 0 else rows

    def kern(b_ref, o_ref):
        total = b_ref[0]
        for d in range(1, n):
            total = total + b_ref[d]
        o_ref[...] = total

    return pl.pallas_call(
        kern, name="small_sum", grid=(rows // rb,), in_specs=[pl.BlockSpec((n, rb, width), lambda i: (0, i, 0))],
        out_specs=pl.BlockSpec((rb, width), lambda i: (i, 0)), out_shape=jax.ShapeDtypeStruct((rows, width), F32),
        compiler_params=_cparams("parallel"))(blocks)


def _small_allgather_sum(buf, head_rows, n_chunks=COMM_CHUNKS):
    rows, width = buf.shape
    chunks = _row_chunks(rows, n_chunks)
    nq = len(chunks)

    def body(b_ref, head_ref, sum_ref, all_ref, send_sems, recv_sems, local_sem):
        x, y, c, chips = _position()
        me, sibling = (x, y, c), (x, y, 1 - c)

        def slot(px, py, pc):
            return all_ref.at[4 * px + 2 * py + pc]

        def copy(k, block, q, to, src=None):
            rows_q = pl.ds(chunks[q][0], chunks[q][1])
            dst = slot(*block).at[rows_q, :]
            return pltpu.make_async_remote_copy(
                src_ref=dst if src is None else src.at[rows_q, :], dst_ref=dst, send_sem=send_sems.at[k * nq + q],
                recv_sem=recv_sems.at[k * nq + q], device_id=to, device_id_type=MESH_ID)

        mine = pltpu.make_async_copy(b_ref, slot(*me), local_sem)
        mine.start()
        first = []
        for q in range(nq):
            first += [copy(1 + j, me, q, (*chip, c), src=b_ref) for j, chip in enumerate(chips)]
            first.append(copy(0, me, q, sibling, src=b_ref))
        for cp in first:
            cp.start()
        passed = []
        for q in range(nq):
            for j, chip in enumerate(chips):
                copy(1 + j, (*chip, c), q, me).wait_recv()
                fwd = copy(4 + j, (*chip, c), q, sibling)
                fwd.start()
                passed.append(fwd)
        for q in range(nq):
            copy(0, sibling, q, me).wait_recv()
            for j, chip in enumerate(chips):
                copy(4 + j, (*chip, 1 - c), q, me).wait_recv()
        for cp in first + passed:
            cp.wait_send()
        mine.wait()
        total = all_ref[0]
        for d in range(1, N_DEV):
            total = total + all_ref[d]
        sum_ref[...] = total
        head_ref[...] = all_ref[:, 0:head_rows, :]

    vm = pl.BlockSpec(memory_space=pltpu.VMEM)
    return pl.pallas_call(
        body, name="small_allgather_sum", in_specs=[vm], out_specs=(vm, vm),
        out_shape=(jax.ShapeDtypeStruct((N_DEV, head_rows, width), F32), jax.ShapeDtypeStruct((rows, width), F32)),
        scratch_shapes=[pltpu.VMEM((N_DEV, rows, width), F32), pltpu.SemaphoreType.DMA((7 * nq,)),
                        pltpu.SemaphoreType.DMA((7 * nq,)), pltpu.SemaphoreType.DMA],
        compiler_params=_cparams(),
    )(buf)


def _rs_pair(g):
    n, rows, width = g.shape
    half = rows // 2
    chunks = _row_chunks(half, COMM_CHUNKS)
    nq = len(chunks)

    def body(g_ref, got_ref, send_sems, recv_sems):
        x, y, c, _ = _position()
        swaps = []
        for k in range(n):
            for q, (start, size) in enumerate(chunks):
                swaps.append(pltpu.make_async_remote_copy(
                    src_ref=g_ref.at[k, pl.ds((1 - c) * half + start, size), :], dst_ref=got_ref.at[k, pl.ds(start, size), :],
                    send_sem=send_sems.at[k * nq + q], recv_sem=recv_sems.at[k * nq + q],
                    device_id=(x, y, 1 - c), device_id_type=MESH_ID))
        for cp in swaps:
            cp.start()
        for cp in swaps:
            cp.wait()

    return pl.pallas_call(
        body, name="rs_pair", in_specs=[_ANY], out_specs=_ANY, out_shape=jax.ShapeDtypeStruct((n, half, width), g.dtype),
        scratch_shapes=[pltpu.SemaphoreType.DMA((n * nq,)), pltpu.SemaphoreType.DMA((n * nq,))],
    )(g)


def _rs_chips_ride(part_bf):
    n, rows, width = part_bf.shape
    chunks = _row_chunks(rows, COMM_CHUNKS)
    nq = len(chunks)

    def sends(pb_ref, got_ref, send_sems, recv_sems):
        x, y, c, chips = _position()
        out = []
        for q, (start, size) in enumerate(chunks):
            for j, chip in enumerate(chips):
                out.append(pltpu.make_async_remote_copy(
                    src_ref=pb_ref.at[2 * chip[0] + chip[1], pl.ds(start, size), :], dst_ref=got_ref.at[j, pl.ds(start, size), :],
                    send_sem=send_sems.at[j * nq + q], recv_sem=recv_sems.at[j * nq + q],
                    device_id=(*chip, c), device_id_type=MESH_ID))
        return out

    def start(ins, outs, sems):
        for cp in sends(ins[0], outs[0], sems[0], sems[1]):
            cp.start()

    def wait(ins, outs, sems):
        for cp in sends(ins[0], outs[0], sems[0], sems[1]):
            cp.wait()

    return _Ride([part_bf], [jax.ShapeDtypeStruct((N_CHIPS - 1, rows, width), BF16)],
                 [pltpu.SemaphoreType.DMA((3 * nq,)), pltpu.SemaphoreType.DMA((3 * nq,))], start, wait)


def _rs_join(shard):
    rows, width = shard.shape
    half = rows // 2
    chunks = _row_chunks(half, COMM_CHUNKS)
    nq = len(chunks)

    def body(in_ref, out_ref, send_sems, recv_sems):
        x, y, c, _ = _position()
        def swap(q, h):
            rows_q = pl.ds(h * half + chunks[q][0], chunks[q][1])
            return pltpu.make_async_remote_copy(
                src_ref=in_ref.at[rows_q, :], dst_ref=out_ref.at[rows_q, :], send_sem=send_sems.at[q],
                recv_sem=recv_sems.at[q], device_id=(x, y, 1 - c), device_id_type=MESH_ID)

        for q in range(nq):
            swap(q, c).start()
        for q in range(nq):
            swap(q, 1 - c).wait_recv()
        for q in range(nq):
            swap(q, c).wait_send()

    return pl.pallas_call(
        body, name="rs_join", in_specs=[_ANY], out_specs=_ANY, input_output_aliases={0: 0},
        out_shape=jax.ShapeDtypeStruct(shard.shape, shard.dtype),
        scratch_shapes=[pltpu.SemaphoreType.DMA((nq,)), pltpu.SemaphoreType.DMA((nq,))],
    )(shard)


def _pair_add(g, got, core):
    n, half, width = got.shape
    nb = 2
    rb = half // nb

    def kern(c_ref, a_ref, b_ref, f_ref, h_ref):
        s = a_ref[...] + b_ref[...]
        f_ref[...] = s
        h_ref[...] = s.astype(BF16)

    spec = pl.BlockSpec((1, rb, width), lambda k, i, c_ref: (k, i, 0))
    return pl.pallas_call(
        kern, name="rs_pair_add",
        grid_spec=pltpu.PrefetchScalarGridSpec(
            num_scalar_prefetch=1, grid=(n, nb),
            in_specs=[pl.BlockSpec((1, rb, width), lambda k, i, c_ref: (k, c_ref[0] * nb + i, 0)), spec],
            out_specs=(spec, spec)),
        out_shape=(jax.ShapeDtypeStruct(got.shape, F32), jax.ShapeDtypeStruct(got.shape, BF16)),
        compiler_params=_cparams("parallel", "parallel"))(core, g, got)


def _chip_add(part_f32, got, where):
    _, rows, width = part_f32.shape
    nb = 2
    rb = rows // nb

    def kern(w_ref, a_ref, b_ref, o_ref):
        o_ref[...] = ((a_ref[0] + b_ref[0].astype(F32)) + b_ref[1].astype(F32)) + b_ref[2].astype(F32)

    return pl.pallas_call(
        kern, name="rs_chip_add",
        grid_spec=pltpu.PrefetchScalarGridSpec(
            num_scalar_prefetch=1, grid=(nb,),
            in_specs=[pl.BlockSpec((1, rb, width), lambda i, w_ref: (w_ref[0], i, 0)),
                      pl.BlockSpec((N_CHIPS - 1, rb, width), lambda i, w_ref: (0, i, 0))],
            out_specs=pl.BlockSpec((rb, width), lambda i, w_ref: (w_ref[1] * nb + i, 0))),
        out_shape=jax.ShapeDtypeStruct((2 * rows, width), F32),
        compiler_params=_cparams("parallel"))(where, part_f32, got)


def _adamw(w, g, m, v, name):
    rows, width = w.shape
    rb = rows
    for cand in (512, 256, 128, 64, 32, 16, 8):
        if rows % cand == 0 and cand * width * 4 <= ADAM_BLOCK_BYTES:
            rb = cand
            break
    spec = pl.BlockSpec((rb, width), lambda i: (i, 0))

    def kern(w_ref, g_ref, m_ref, v_ref, d_ref, nm_ref, nv_ref):
        d_ref[...], nm_ref[...], nv_ref[...] = _adamw_update(w_ref[...], g_ref[...], m_ref[...], v_ref[...])

    shp = jax.ShapeDtypeStruct(w.shape, F32)
    return pl.pallas_call(
        kern, name=name, grid=(rows // rb,), in_specs=[spec] * 4, out_specs=(spec, spec, spec),
        out_shape=(shp, shp, shp), compiler_params=_cparams("parallel"))(w, g, m, v)


def _adamw_update(w, g, m, v):
    nm = ADAM_B1 * m + (1.0 - ADAM_B1) * g
    nv = ADAM_B2 * v + (1.0 - ADAM_B2) * (g * g)
    m_hat = nm / (1.0 - ADAM_B1 ** ADAM_STEP)
    v_hat = nv / (1.0 - ADAM_B2 ** ADAM_STEP)
    return -ADAM_LR * (m_hat / (jnp.sqrt(v_hat) + ADAM_EPS) + ADAM_WD * w), nm, nv


def _adamw_small(params):
    n = len(params)

    def kern(*refs):
        ins, outs = refs[:4 * n], refs[4 * n:]
        for p in range(n):
            w_ref, g_ref, m_ref, v_ref = ins[4 * p:4 * p + 4]
            d, nm, nv = _adamw_update(w_ref[...], g_ref[...], m_ref[...], v_ref[...])
            outs[3 * p][...] = d
            outs[3 * p + 1][...] = nm
            outs[3 * p + 2][...] = nv

    flat = [a for group in params for a in group]
    shapes = [jax.ShapeDtypeStruct(group[0].shape, F32) for group in params for _ in range(3)]
    res = pl.pallas_call(kern, name="adamw_small", out_shape=tuple(shapes), compiler_params=_cparams())(*flat)
    return [tuple(res[3 * p:3 * p + 3]) for p in range(n)]


def _wada_grad(silu_t, dmod_cols):
    n = dmod_cols.shape[1]

    def kern(s_ref, d_ref, o_ref):
        acc = s_ref[:, 0:1] * d_ref[0:1, :]
        for b in range(1, N_DEV):
            acc = acc + s_ref[:, b:b + 1] * d_ref[b:b + 1, :]
        o_ref[...] = acc

    return pl.pallas_call(kern, name="wada_grad", out_shape=jax.ShapeDtypeStruct((D_MODEL, n), F32),
                          compiler_params=_cparams())(silu_t, dmod_cols)


def _rows(a, multiple):
    flat = a.reshape(-1)
    pad = (-flat.shape[0]) % (D_MODEL * multiple)
    if pad:
        flat = jnp.concatenate([flat, jnp.zeros((pad,), flat.dtype)])
    return flat.reshape(-1, D_MODEL)


def _part_rows(shape, multiple):
    return -(-int(np.prod(shape)) // (D_MODEL * multiple)) * multiple


def _pack_rows(parts, multiple, total_multiple=1):
    blocks = [_rows(p, multiple) for p in parts]
    pad = (-sum(b.shape[0] for b in blocks)) % total_multiple
    if pad:
        blocks.append(jnp.zeros((pad, D_MODEL), blocks[0].dtype))
    return jnp.concatenate(blocks, axis=0)


def _unpack_rows(buf, shapes, multiple):
    out, r = [], 0
    for shp in shapes:
        n = int(np.prod(shp))
        nr = _part_rows(shp, multiple)
        out.append(buf[r:r + nr].reshape(-1)[:n].reshape(shp))
        r += nr
    return out


def kernel(x, c, w_ada, b_ada, norm_pre, norm_post, w_in, pool_w, pool_scale, ssm_a_re, ssm_a_im, ssm_log_dt, ssm_b_re, ssm_b_im, ssm_c_re, ssm_c_im, ssm_d, glu_w, glu_b, w_branch_pool, w_branch_ssm, w_out, loss_target, m_w_ada, m_b_ada, m_norm_pre, m_norm_post, m_w_in, m_pool_w, m_pool_scale, m_ssm_a_re, m_ssm_a_im, m_ssm_log_dt, m_ssm_b_re, m_ssm_b_im, m_ssm_c_re, m_ssm_c_im, m_ssm_d, m_glu_w, m_glu_b, m_w_branch_pool, m_w_branch_ssm, m_w_out, v_w_ada, v_b_ada, v_norm_pre, v_norm_post, v_w_in, v_pool_w, v_pool_scale, v_ssm_a_re, v_ssm_a_im, v_ssm_log_dt, v_ssm_b_re, v_ssm_b_im, v_ssm_c_re, v_ssm_c_im, v_ssm_d, v_glu_w, v_glu_b, v_w_branch_pool, v_w_branch_ssm, v_w_out):
    n_ada = w_ada.shape[2]
    n_in = w_in.shape[2]
    n_row = glu_w.shape[1]
    n_pool = pool_w.shape[2]
    n_groups = pool_w.shape[1]

    (g_ada,) = _run_ride(_ag_weights_ride(w_ada[0].astype(BF16)), "ag_weights")
    w_ada_bf = g_ada.transpose(1, 0, 2).reshape(D_MODEL, N_CHIPS * n_ada)
    w_in_ride = _ag_weights_ride(w_in[0].astype(BF16))

    def unpack_w_in(g_in):
        return g_in.transpose(1, 0, 2).reshape(D_MODEL, N_CHIPS * n_in), [g_in[k] for k in range(N_CHIPS)]
    pool_rows = n_groups * n_pool * POOL_GW // D_MODEL
    late_shards = [pool_w[0].reshape(n_groups * n_pool, POOL_GW), glu_w[0], w_branch_pool[0], w_branch_ssm[0], w_out[0]]
    late_ride = _join_rides([_ag_weights_ride(s.astype(BF16), n_chunks=2) for s in late_shards])

    def unpack_late(pool, *squares):
        pool = pool.reshape(N_CHIPS, n_groups, n_pool, POOL_GW).transpose(1, 0, 2, 3)
        return (pool.reshape(n_groups, POOL_GW, POOL_GW), *[s.reshape(D_MODEL, D_MODEL) for s in squares])

    chip = 2 * lax.axis_index("x") + lax.axis_index("y")
    core = lax.axis_index("c").astype(jnp.int32)
    kept = {}

    def by_cols(a, n):
        return a.reshape(D_MODEL, N_CHIPS, n).transpose(1, 0, 2).reshape(N_CHIPS, -1, D_MODEL)

    def by_rows(a):
        return a.reshape(N_CHIPS, n_row, D_MODEL)

    def exchange_big(g):
        pool_by_chip = g["d_pool_w"].reshape(n_groups, N_CHIPS, n_pool, POOL_GW).transpose(1, 0, 2, 3)
        blocks = [by_cols(g["d_win"], n_in), by_rows(g["d_glu_w"]), by_rows(g["d_wbp"]), by_rows(g["d_wbs"]),
                  by_rows(g["d_wout"]), pool_by_chip.reshape(N_CHIPS, pool_rows, D_MODEL)]
        pad = (-sum(b.shape[1] for b in blocks)) % (2 * COMM_CHUNKS * COMM_ROW_ALIGN)
        if pad:
            blocks.append(jnp.zeros((N_CHIPS, pad, D_MODEL), F32))
        g_packed = jnp.concatenate(blocks, axis=1)
        kept["part_f32"], part_bf = _pair_add(g_packed, _rs_pair(g_packed), core.reshape(1))
        return _rs_chips_ride(part_bf)

    a_re, a_im, log_dt = ssm_a_re[0], ssm_a_im[0], ssm_log_dt[0].reshape(SSM_G, 1)
    b_re_t, b_im_t = ssm_b_re[0].transpose(2, 0, 1), ssm_b_im[0].transpose(2, 0, 1)
    early_names = ["dg2", "d_pscale", "d_glu_b", "d_dskip", "d_abar_re", "d_abar_im", "d_bb_re_t", "d_bb_im_t",
                   "d_c_re", "d_c_im"]

    def exchange_small(s):
        parts = [s[k] for k in early_names]
        kept["early_shapes"] = [p.shape for p in parts]
        return _small_allgather_ride(_pack_rows(parts, SUBLANES, COMM_CHUNKS * COMM_ROW_ALIGN))

    res = _local_step(x[0], c, loss_target[0], w_ada_bf, b_ada, norm_pre, norm_post, None, None, pool_scale,
                      a_re, a_im, log_dt, b_re_t, b_im_t, ssm_c_re[0], ssm_c_im[0], ssm_d[0], None, glu_b[0:1],
                      None, None, None, early_weight=(w_in_ride, unpack_w_in), late_weights=(late_ride, unpack_late),
                      ride_for_dw_in=exchange_small, ride_for_dh=exchange_big)
    loss = lax.psum(res["loss"], ("x", "y", "c"))

    (all_early,) = res["rode_dw_in"]
    (g_norm_post, g_pscale, g_glu_b, g_dskip, s_abar_re, s_abar_im, s_bb_re, s_bb_im, g_c_re, g_c_im) = _unpack_rows(
        _sum_devices(all_early), kept["early_shapes"], SUBLANES)
    g_a_re, g_a_im, g_log_dt, g_b_re_t, g_b_im_t = _ssm_params_bwd(
        a_re, a_im, log_dt, b_re_t, b_im_t, s_abar_re.reshape(SSM_G, SSM_P), s_abar_im.reshape(SSM_G, SSM_P),
        s_bb_re, s_bb_im)
    late_parts = [res["dmod"], res["silu_c"], res["dg1"]]
    late_shapes = [p.shape for p in late_parts]
    head_rows = _part_rows(late_shapes[0], SUBLANES) + _part_rows(late_shapes[1], SUBLANES)
    all_late, sum_late = _small_allgather_sum(_pack_rows(late_parts, SUBLANES, COMM_ROW_ALIGN), head_rows, n_chunks=1)
    g_b_ada, _, g_norm_pre = _unpack_rows(sum_late, late_shapes, SUBLANES)
    dmod_all = all_late[:, 0:3].reshape(N_DEV, 3 * D_MODEL)
    dmod_cols = lax.dynamic_slice_in_dim(dmod_all, chip * n_ada, n_ada, axis=1)
    silu_t = all_late[:, _part_rows(late_shapes[0], SUBLANES)].transpose(1, 0)
    g_w_ada = _wada_grad(silu_t, dmod_cols)

    (got_chips,) = res["rode"]
    shard = _rs_join(_chip_add(kept["part_f32"], got_chips, jnp.stack([chip.astype(jnp.int32), core])))
    r = 0
    g_w_in = shard[r:r + n_in].reshape(D_MODEL, n_in)
    r += n_in
    g_squares = []
    for _ in range(4):
        g_squares.append(shard[r:r + n_row])
        r += n_row
    g_glu_w, g_wbp, g_wbs, g_wout = g_squares
    g_pool_w = shard[r:r + pool_rows].reshape(n_groups * n_pool, POOL_GW)

    big = [("w_ada", w_ada[0], g_w_ada, m_w_ada[0], v_w_ada[0]),
           ("w_in", w_in[0], g_w_in, m_w_in[0], v_w_in[0]),
           ("pool_w", pool_w[0].reshape(n_groups * n_pool, POOL_GW), g_pool_w,
            m_pool_w[0].reshape(n_groups * n_pool, POOL_GW), v_pool_w[0].reshape(n_groups * n_pool, POOL_GW)),
           ("glu_w", glu_w[0], g_glu_w, m_glu_w[0], v_glu_w[0]),
           ("w_branch_pool", w_branch_pool[0], g_wbp, m_w_branch_pool[0], v_w_branch_pool[0]),
           ("w_branch_ssm", w_branch_ssm[0], g_wbs, m_w_branch_ssm[0], v_w_branch_ssm[0]),
           ("w_out", w_out[0], g_wout, m_w_out[0], v_w_out[0])]
    out = {}
    for name, w_, g_, m_, v_ in big:
        d_, nm_, nv_ = _adamw(w_, g_, m_, v_, "adamw_" + name)
        out[name] = (g_, d_, nm_, nv_)

    g_b_re = g_b_re_t.transpose(1, 2, 0)
    g_b_im = g_b_im_t.transpose(1, 2, 0)
    small = [("b_ada", b_ada, g_b_ada, m_b_ada, v_b_ada),
             ("norm_pre", norm_pre, g_norm_pre, m_norm_pre, v_norm_pre),
             ("norm_post", norm_post, g_norm_post, m_norm_post, v_norm_post),
             ("pool_scale", pool_scale, g_pscale, m_pool_scale, v_pool_scale),
             ("ssm_a_re", ssm_a_re, g_a_re, m_ssm_a_re, v_ssm_a_re),
             ("ssm_a_im", ssm_a_im, g_a_im, m_ssm_a_im, v_ssm_a_im),
             ("ssm_log_dt", ssm_log_dt, g_log_dt, m_ssm_log_dt, v_ssm_log_dt),
             ("ssm_b_re", ssm_b_re, g_b_re, m_ssm_b_re, v_ssm_b_re),
             ("ssm_b_im", ssm_b_im, g_b_im, m_ssm_b_im, v_ssm_b_im),
             ("ssm_c_re", ssm_c_re, g_c_re, m_ssm_c_re, v_ssm_c_re),
             ("ssm_c_im", ssm_c_im, g_c_im, m_ssm_c_im, v_ssm_c_im),
             ("ssm_d", ssm_d, g_dskip, m_ssm_d, v_ssm_d),
             ("glu_b", glu_b, g_glu_b, m_glu_b, v_glu_b)]
    small = [(name, w_, g_.reshape(w_.shape), m_, v_) for name, w_, g_, m_, v_ in small]
    updates = _adamw_small([t[1:] for t in small])
    for (name, _, g_, _, _), (d_, nm_, nv_) in zip(small, updates):
        out[name] = (g_, d_, nm_, nv_)

    order = ["w_ada", "b_ada", "norm_pre", "norm_post", "w_in", "pool_w", "pool_scale", "ssm_a_re", "ssm_a_im",
             "ssm_log_dt", "ssm_b_re", "ssm_b_im", "ssm_c_re", "ssm_c_im", "ssm_d", "glu_w", "glu_b", "w_branch_pool",
             "w_branch_ssm", "w_out"]
    ref_shape = dict(w_ada=w_ada.shape, w_in=w_in.shape, pool_w=pool_w.shape, glu_w=glu_w.shape,
                     w_branch_pool=w_branch_pool.shape, w_branch_ssm=w_branch_ssm.shape, w_out=w_out.shape)
    for name, w_, _, _, _ in small:
        ref_shape[name] = w_.shape
    results = [loss, res["grad_x"][None]]
    for k in range(4):
        results += [out[name][k].reshape(ref_shape[name]) for name in order]
    return tuple(results)
```

```python
import functools
import math

import numpy as np
import jax
import jax.numpy as jnp
from jax import lax
from jax.experimental import pallas as pl
from jax.experimental.pallas import tpu as pltpu

F32 = jnp.float32
BF16 = jnp.bfloat16
MESH_ID = pl.DeviceIdType.MESH

D_MODEL = 1024
LANES = 128
SUBLANES = 8
SSM_G, SSM_P, SSM_H = 64, 64, 16
LANE_BLOCKS = D_MODEL // LANES
GROUPS_PER_BLOCK = LANES // SSM_H
STATE_W = GROUPS_PER_BLOCK * SSM_P
STATE_ALL = SSM_G * SSM_P
POOL_WINDOWS = (2, 4, 8, 16)
POOL_GW = D_MODEL // len(POOL_WINDOWS)
HALO = 16
RMS_EPS = 1e-6
N_CHIPS = 4
N_DEV = 8

SCAN_CHUNK = 512
SCAN_BLOCKS = 2
ROW_CHUNK = 256
ROW_CHUNK_WIDE = 512
VMEM_LIMIT_BYTES = 56 * 1024 * 1024

ADAM_BLOCK_BYTES = 1 << 20
ADAM_LR, ADAM_B1, ADAM_B2, ADAM_EPS, ADAM_WD, ADAM_STEP = 0.001, 0.9, 0.999, 1e-08, 0.01, 10

_GELU_C0 = math.sqrt(2.0 / math.pi)
_GELU_C1 = 0.044715


def _cparams(*sem):
    if sem:
        return pltpu.CompilerParams(dimension_semantics=sem, vmem_limit_bytes=VMEM_LIMIT_BYTES)
    return pltpu.CompilerParams(vmem_limit_bytes=VMEM_LIMIT_BYTES)


def _sigmoid(v):
    return jax.nn.sigmoid(v)


def _silu(v):
    return v * _sigmoid(v)


def _dsilu(v):
    s = _sigmoid(v)
    return s * (1.0 + v * (1.0 - s))


def _gelu(v):
    return 0.5 * v * (1.0 + jnp.tanh(_GELU_C0 * (v + _GELU_C1 * v * v * v)))


def _dgelu(v):
    t = jnp.tanh(_GELU_C0 * (v + _GELU_C1 * v * v * v))
    return 0.5 * (1.0 + t) + 0.5 * v * (1.0 - t * t) * _GELU_C0 * (1.0 + 3.0 * _GELU_C1 * v * v)


def _dot(a, b):
    return lax.dot_general(a, b, (((1,), (0,)), ((), ())), preferred_element_type=F32)


def _dot_nt(a, b):
    return lax.dot_general(a, b, (((1,), (1,)), ((), ())), preferred_element_type=F32)


def _dot_tn(a, b):
    return lax.dot_general(a, b, (((0,), (0,)), ((), ())), preferred_element_type=F32)


def _acc8(v):
    return v.reshape(v.shape[0] // SUBLANES, SUBLANES, v.shape[1]).sum(axis=0)


class _Ride:
    def __init__(self, inputs, out_shapes, scratch, start, wait):
        self.inputs, self.out_shapes, self.scratch, self.start, self.wait = inputs, out_shapes, scratch, start, wait


def _mm(a_parts, b_parts, *, name, ta=False, tb=False, out_dtype=F32, bm=512, bn=512, bk=512, ride=None):
    a_parts, b_parts = list(a_parts), list(b_parts)
    if ta:
        assert len(a_parts) == 1
        k_dim, m_dim = a_parts[0].shape
    else:
        m_dim = a_parts[0].shape[0]
        k_dim = sum(a.shape[1] for a in a_parts)
    if tb:
        assert len(b_parts) == 1
        n_dim = b_parts[0].shape[0]
    else:
        n_dim = sum(b.shape[1] for b in b_parts)
    bm, bn, bk = min(bm, m_dim), min(bn, n_dim), min(bk, k_dim)
    nm, nn, nk = m_dim // bm, n_dim // bn, k_dim // bk
    a_ranges, off = [], 0
    for a in a_parts:
        cnt = (a.shape[0] if ta else a.shape[1]) // bk
        a_ranges.append((off, cnt))
        off += cnt
    b_ranges, off = [], 0
    for b in b_parts:
        cnt = (b.shape[0] if tb else b.shape[1]) // bn
        b_ranges.append((off, cnt))
        off += cnt

    def a_spec(off, cnt):
        if ta:
            return pl.BlockSpec((bk, bm), lambda i, n, k: (k, i))
        return pl.BlockSpec((bm, bk), lambda i, n, k: (i, jnp.clip(k - off, 0, cnt - 1)))

    def b_spec(off, cnt):
        if tb:
            return pl.BlockSpec((bn, bk), lambda i, n, k: (n, k))
        return pl.BlockSpec((bk, bn), lambda i, n, k: (k, jnp.clip(n - off, 0, cnt - 1)))

    na, nb = len(a_parts), len(b_parts)
    dims = (((0 if ta else 1,), (1 if tb else 0,)), ((), ()))

    def kern_single(a_ref, b_ref, o_ref):
        o_ref[...] = lax.dot_general(a_ref[...].astype(BF16), b_ref[...].astype(BF16), dims,
                                     preferred_element_type=F32).astype(out_dtype)

    if na == 1 and nb == 1 and nk == 1 and not ride:
        return pl.pallas_call(
            kern_single, name=name, grid=(nm, nn),
            in_specs=[pl.BlockSpec((bk, bm), lambda i, n: (0, i)) if ta else pl.BlockSpec((bm, bk), lambda i, n: (i, 0)),
                      pl.BlockSpec((bn, bk), lambda i, n: (n, 0)) if tb else pl.BlockSpec((bk, bn), lambda i, n: (0, n))],
            out_specs=pl.BlockSpec((bm, bn), lambda i, n: (i, n)),
            out_shape=jax.ShapeDtypeStruct((m_dim, n_dim), out_dtype),
            compiler_params=_cparams("parallel", "parallel"),
        )(a_parts[0], b_parts[0])

    n_rin = len(ride.inputs) if ride else 0
    n_rout = len(ride.out_shapes) if ride else 0

    def kern(*refs):
        a_refs, b_refs = refs[:na], refs[na:na + nb]
        rin = refs[na + nb:na + nb + n_rin]
        o_ref = refs[na + nb + n_rin]
        rout = refs[na + nb + n_rin + 1:na + nb + n_rin + 1 + n_rout]
        acc = refs[na + nb + n_rin + 1 + n_rout]
        rsem = refs[na + nb + n_rin + 2 + n_rout:]
        i, n, k = pl.program_id(0), pl.program_id(1), pl.program_id(2)

        if ride:
            @pl.when((i == 0) & (n == 0) & (k == 0))
            def _():
                ride.start(rin, rout, rsem)

        if nk > 1:
            @pl.when(k == 0)
            def _():
                acc[...] = jnp.zeros_like(acc)

        for ja, (koff, kcnt) in enumerate(a_ranges):
            for jb, (noff, ncnt) in enumerate(b_ranges):
                def step(ja=ja, jb=jb):
                    a = a_refs[ja][...].astype(BF16)
                    b = b_refs[jb][...].astype(BF16)
                    prod = lax.dot_general(a, b, dims, preferred_element_type=F32)
                    if nk > 1:
                        acc[...] += prod
                    else:
                        o_ref[...] = prod.astype(out_dtype)

                if na == 1 and nb == 1:
                    step()
                else:
                    cond = (k >= koff) & (k < koff + kcnt) & (n >= noff) & (n < noff + ncnt)
                    pl.when(cond)(step)

        if nk > 1:
            @pl.when(k == nk - 1)
            def _():
                o_ref[...] = acc[...].astype(out_dtype)

        if ride:
            @pl.when((i == nm - 1) & (n == nn - 1) & (k == nk - 1))
            def _():
                ride.wait(rin, rout, rsem)

    any_spec = pl.BlockSpec(memory_space=pl.ANY)
    out_spec = pl.BlockSpec((bm, bn), lambda i, n, k: (i, n))
    out_shape = jax.ShapeDtypeStruct((m_dim, n_dim), out_dtype)
    acc_shape = pltpu.VMEM((bm, bn) if nk > 1 else (SUBLANES, LANES), F32)
    if not ride:
        return pl.pallas_call(
            kern, name=name, grid=(nm, nn, nk),
            in_specs=[a_spec(*r) for r in a_ranges] + [b_spec(*r) for r in b_ranges],
            out_specs=out_spec, out_shape=out_shape, scratch_shapes=[acc_shape],
            compiler_params=_cparams("parallel", "parallel", "arbitrary"),
        )(*a_parts, *b_parts)
    return pl.pallas_call(
        kern, name=name, grid=(nm, nn, nk),
        in_specs=[a_spec(*r) for r in a_ranges] + [b_spec(*r) for r in b_ranges] + [any_spec] * n_rin,
        out_specs=(out_spec,) + (any_spec,) * n_rout, out_shape=(out_shape,) + tuple(ride.out_shapes),
        scratch_shapes=[acc_shape] + list(ride.scratch),
        compiler_params=_cparams("arbitrary", "arbitrary", "arbitrary"),
    )(*a_parts, *b_parts, *ride.inputs)


def _ssm_param_fn(a_re, a_im, log_dt, b_re, b_im):
    dt = jnp.exp(log_dt)
    lam_re = jnp.minimum(a_re, -1e-4)
    lam_im = a_im
    mag = jnp.exp(lam_re * dt)
    abar_re = mag * jnp.cos(lam_im * dt)
    abar_im = mag * jnp.sin(lam_im * dt)
    den = lam_re * lam_re + lam_im * lam_im
    num_re = abar_re - 1.0
    f_re = (num_re * lam_re + abar_im * lam_im) / den
    f_im = (abar_im * lam_re - num_re * lam_im) / den
    bb_re = f_re * b_re - f_im * b_im
    bb_im = f_re * b_im + f_im * b_re
    return abar_re, abar_im, bb_re, bb_im


def _ssm_params(a_re, a_im, log_dt, b_re_t, b_im_t):
    def kern(are, aim, ldt, bre, bim, o_ar, o_ai, o_br, o_bi):
        ar, ai, br, bi = _ssm_param_fn(are[...], aim[...], ldt[...], bre[...], bim[...])
        o_ar[...] = ar
        o_ai[...] = ai
        o_br[...] = br
        o_bi[...] = bi

    gp = jax.ShapeDtypeStruct((SSM_G, SSM_P), F32)
    hgp = jax.ShapeDtypeStruct((SSM_H, SSM_G, SSM_P), F32)
    return pl.pallas_call(kern, name="ssm_params", out_shape=(gp, gp, hgp, hgp), compiler_params=_cparams())(
        a_re, a_im, log_dt, b_re_t, b_im_t)


def _ssm_params_bwd(a_re, a_im, log_dt, b_re_t, b_im_t, d_ar, d_ai, d_bbr, d_bbi):
    def kern(are, aim, ldt, bre, bim, dar, dai, dbr, dbi, o_are, o_aim, o_ldt, o_bre, o_bim):
        prim = (are[...], aim[...], ldt[...], bre[...], bim[...])
        _, vjp = jax.vjp(_ssm_param_fn, *prim)
        g = vjp((dar[...], dai[...], dbr[...], dbi[...]))
        o_are[...] = g[0]
        o_aim[...] = g[1]
        o_ldt[...] = g[2]
        o_bre[...] = g[3]
        o_bim[...] = g[4]

    gp = jax.ShapeDtypeStruct((SSM_G, SSM_P), F32)
    g1 = jax.ShapeDtypeStruct((SSM_G, 1), F32)
    hgp = jax.ShapeDtypeStruct((SSM_H, SSM_G, SSM_P), F32)
    return pl.pallas_call(kern, name="ssm_params_bwd", out_shape=(gp, gp, g1, hgp, hgp), compiler_params=_cparams())(
        a_re, a_im, log_dt, b_re_t, b_im_t, d_ar, d_ai, d_bbr, d_bbi)


def _pow_tables(abar_re, abar_im, tc):
    ls = tc // SUBLANES

    def kern(ar_ref, ai_ref, fr_ref, fi_ref, rr_ref, ri_ref):
        a_re = jnp.broadcast_to(ar_ref[...], (SUBLANES, STATE_W))
        a_im = jnp.broadcast_to(ai_ref[...], (SUBLANES, STATE_W))
        p_re, p_im = a_re, a_im
        for i in range(ls):
            fwd = pl.ds(SUBLANES * i, SUBLANES)
            rev = pl.ds(SUBLANES * (ls - 1 - i), SUBLANES)
            fr_ref[fwd, :] = p_re
            fi_ref[fwd, :] = p_im
            rr_ref[rev, :] = p_re
            ri_ref[rev, :] = p_im
            p_re, p_im = p_re * a_re - p_im * a_im, p_re * a_im + p_im * a_re

    vec = pl.BlockSpec((1, STATE_W), lambda b: (0, b))
    tab = pl.BlockSpec((tc, STATE_W), lambda b: (0, b))
    shp = jax.ShapeDtypeStruct((tc, STATE_ALL), F32)
    return pl.pallas_call(
        kern, name="pow_tables", grid=(LANE_BLOCKS,), in_specs=[vec, vec], out_specs=(tab, tab, tab, tab),
        out_shape=(shp, shp, shp, shp), compiler_params=_cparams("parallel"))(abar_re, abar_im)


def _mod_kernel(c_row, w_ada_bf, b_ada):
    def kern(c_ref, w_ref, b_ref, m_ref, s_ref):
        cv = c_ref[...]
        sc = _silu(cv)
        s_ref[...] = sc
        lhs = jnp.broadcast_to(sc, (SUBLANES, D_MODEL)).astype(BF16)
        m_ref[...] = _dot(lhs, w_ref[...]) + b_ref[...]

    return pl.pallas_call(
        kern, name="ada_mod",
        out_shape=(jax.ShapeDtypeStruct((SUBLANES, 3 * D_MODEL), F32), jax.ShapeDtypeStruct((1, D_MODEL), F32)),
        compiler_params=_cparams())(c_row, w_ada_bf, b_ada)


def _row_spec(tr, width=D_MODEL, col=0):
    return pl.BlockSpec((tr, width), lambda c: (c, col))


def _vec_spec(width=D_MODEL):
    return pl.BlockSpec((1, width), lambda c: (0, 0))


def _col_spec(tr):
    return pl.BlockSpec((D_MODEL, tr), lambda c: (0, c))


def _in_norm(x, g1, scale, shift, ride=None):
    seq = x.shape[0]
    tr = min(ROW_CHUNK_WIDE, seq)
    nc = seq // tr
    n_rin = len(ride.inputs) if ride else 0
    n_rout = len(ride.out_shapes) if ride else 0

    def kern(x_ref, g_ref, sc_ref, sh_ref, *rest):
        rin, (h_ref, ht_ref) = rest[:n_rin], rest[n_rin:n_rin + 2]
        rout, rsem = rest[n_rin + 2:n_rin + 2 + n_rout], rest[n_rin + 2 + n_rout:]
        c = pl.program_id(0)
        if ride:
            @pl.when(c == 0)
            def _():
                ride.start(rin, rout, rsem)

        xv = x_ref[...]
        r = lax.rsqrt(jnp.mean(xv * xv, axis=-1, keepdims=True) + RMS_EPS)
        h = ((xv * r) * g_ref[...]) * (1.0 + sc_ref[...]) + sh_ref[...]
        h_ref[...] = h.astype(BF16)
        ht_ref[...] = h.T.astype(BF16)

        if ride:
            @pl.when(c == nc - 1)
            def _():
                ride.wait(rin, rout, rsem)

    outs = pl.pallas_call(
        kern, name="in_norm", grid=(nc,),
        in_specs=[_row_spec(tr), _vec_spec(), _vec_spec(), _vec_spec()] + [_ANY] * n_rin,
        out_specs=(_row_spec(tr), _col_spec(tr)) + (_ANY,) * n_rout,
        out_shape=(jax.ShapeDtypeStruct((seq, D_MODEL), BF16), jax.ShapeDtypeStruct((D_MODEL, seq), BF16))
        + tuple(ride.out_shapes if ride else ()),
        scratch_shapes=list(ride.scratch) if ride else [],
        compiler_params=_cparams("arbitrary" if ride else "parallel"))(x, g1, scale, shift, *(ride.inputs if ride else ()))
    return outs


def _pool_windows(ext, pos, g, w, tr):
    cols = pl.ds(g * POOL_GW, POOL_GW)
    cur = ext[pl.ds(HALO, tr), cols]
    acc = cur
    for k in range(1, w):
        acc = acc + ext[pl.ds(HALO - k, tr), cols]
    cnt = jnp.minimum(pos + 1, w).astype(F32)
    return acc / cnt - cur


def _pool_fwd(proj, pool_w_bf, pscale):
    seq = proj.shape[0]
    tr = min(ROW_CHUNK_WIDE, seq)
    hb = tr // HALO

    def kern(up_ref, halo_ref, zp_ref, pw_ref, ps_ref, y_ref, yt_ref, ext):
        c = pl.program_id(0)
        ext[0:HALO, :] = jnp.where(c > 0, halo_ref[...].astype(F32), 0.0)
        ext[HALO:, :] = up_ref[...].astype(F32)
        pos = c * tr + lax.broadcasted_iota(jnp.int32, (tr, POOL_GW), 0)
        for g, w in enumerate(POOL_WINDOWS):
            cols = pl.ds(g * POOL_GW, POOL_GW)
            pooled = _pool_windows(ext, pos, g, w, tr)
            mixed = _dot(pooled.astype(BF16), pw_ref[g])
            y = mixed * ps_ref[:, cols] * _silu(zp_ref[:, cols].astype(F32))
            y_ref[:, cols] = y.astype(BF16)
            yt_ref[cols, :] = y.T.astype(BF16)

    return pl.pallas_call(
        kern, name="pool_fwd", grid=(seq // tr,),
        in_specs=[_row_spec(tr, col=0),
                  pl.BlockSpec((HALO, D_MODEL), lambda c: (jnp.maximum(c * hb - 1, 0), 0)),
                  _row_spec(tr, col=1),
                  pl.BlockSpec((len(POOL_WINDOWS), POOL_GW, POOL_GW), lambda c: (0, 0, 0)),
                  _vec_spec()],
        out_specs=(_row_spec(tr), _col_spec(tr)),
        out_shape=(jax.ShapeDtypeStruct((seq, D_MODEL), BF16), jax.ShapeDtypeStruct((D_MODEL, seq), BF16)),
        scratch_shapes=[pltpu.VMEM((tr + HALO, D_MODEL), F32)],
        compiler_params=_cparams("parallel"))(proj, proj, proj, pool_w_bf, pscale)


def _pool_bwd(proj, dyp, pool_w_bf, pscale, dproj):
    seq = proj.shape[0]
    tr = min(ROW_CHUNK_WIDE, seq)
    hb = tr // HALO
    nc = seq // tr
    n_halo = seq // HALO

    def kern(up_ref, halo_ref, zp_ref, zpn_ref, dyp_ref, dypn_ref, pw_ref, ps_ref, _,
             d01_ref, dpw_ref, dps_ref, ext, dpn, acc_pw, acc_ps):
        c = pl.program_id(0)

        @pl.when(c == 0)
        def _():
            acc_pw[...] = jnp.zeros_like(acc_pw)
            acc_ps[...] = jnp.zeros_like(acc_ps)

        ext[0:HALO, :] = jnp.where(c > 0, halo_ref[...].astype(F32), 0.0)
        ext[HALO:, :] = up_ref[...].astype(F32)
        pos = c * tr + lax.broadcasted_iota(jnp.int32, (tr, POOL_GW), 0)
        pos_n = (c + 1) * tr + lax.broadcasted_iota(jnp.int32, (HALO, POOL_GW), 0)
        has_next = c < nc - 1
        for g, w in enumerate(POOL_WINDOWS):
            cols = pl.ds(g * POOL_GW, POOL_GW)
            pooled_bf = _pool_windows(ext, pos, g, w, tr).astype(BF16)
            wg = pw_ref[g]
            mixed = _dot(pooled_bf, wg)
            zp = zp_ref[:, cols].astype(F32)
            sz = _silu(zp)
            dyp_g = dyp_ref[:, cols]
            ps = ps_ref[:, cols]
            dmixed = (dyp_g * ps * sz).astype(BF16)
            acc_ps[:, cols] += _acc8(dyp_g * mixed * sz)
            d01_ref[:, pl.ds(D_MODEL + g * POOL_GW, POOL_GW)] = (dyp_g * mixed * ps * _dsilu(zp)).astype(BF16)
            acc_pw[g] += _dot_tn(pooled_bf, dmixed)
            dpooled = _dot_nt(dmixed, wg)
            dmixed_n = (jnp.where(has_next, dypn_ref[:, cols], 0.0) * ps * _silu(zpn_ref[:, cols].astype(F32))).astype(BF16)
            dpooled_n = _dot_nt(dmixed_n, wg)
            dpn[0:tr, :] = dpooled / jnp.minimum(pos + 1, w).astype(F32)
            dpn[tr:, :] = dpooled_n / jnp.minimum(pos_n + 1, w).astype(F32)
            acc = dpn[0:tr, :]
            for k in range(1, w):
                acc = acc + dpn[pl.ds(k, tr), :]
            d01_ref[:, cols] = (acc - dpooled).astype(BF16)

        @pl.when(c == nc - 1)
        def _():
            dpw_ref[...] = acc_pw[...]
            dps_ref[...] = jnp.sum(acc_ps[...], axis=0, keepdims=True)

    nxt = lambda c: (jnp.minimum((c + 1) * hb, n_halo - 1), 0)
    nxt1 = lambda c: (jnp.minimum((c + 1) * hb, n_halo - 1), 1)
    return pl.pallas_call(
        kern, name="pool_bwd", grid=(nc,),
        in_specs=[_row_spec(tr, col=0),
                  pl.BlockSpec((HALO, D_MODEL), lambda c: (jnp.maximum(c * hb - 1, 0), 0)),
                  _row_spec(tr, col=1),
                  pl.BlockSpec((HALO, D_MODEL), nxt1),
                  _row_spec(tr),
                  pl.BlockSpec((HALO, D_MODEL), nxt),
                  pl.BlockSpec((len(POOL_WINDOWS), POOL_GW, POOL_GW), lambda c: (0, 0, 0)),
                  _vec_spec(), _ANY],
        out_specs=(pl.BlockSpec((tr, 2 * D_MODEL), lambda c: (c, 0)),
                   pl.BlockSpec((len(POOL_WINDOWS), POOL_GW, POOL_GW), lambda c: (0, 0, 0)),
                   _vec_spec()),
        out_shape=(jax.ShapeDtypeStruct(dproj.shape, BF16),
                   jax.ShapeDtypeStruct((len(POOL_WINDOWS), POOL_GW, POOL_GW), F32),
                   jax.ShapeDtypeStruct((1, D_MODEL), F32)),
        scratch_shapes=[pltpu.VMEM((tr + HALO, D_MODEL), F32), pltpu.VMEM((tr + HALO, POOL_GW), F32),
                        pltpu.VMEM((len(POOL_WINDOWS), POOL_GW, POOL_GW), F32), pltpu.VMEM((SUBLANES, D_MODEL), F32)],
        input_output_aliases={8: 0},
        compiler_params=_cparams("arbitrary"))(proj, proj, proj, proj, dyp, dyp, pool_w_bf, pscale, dproj)


def _glu_fwd(ys, proj, glu_w_bf, glu_b):
    seq = ys.shape[0]
    tr = min(ROW_CHUNK_WIDE, seq)

    def kern(ys_ref, zs_ref, w_ref, b_ref, o_ref, ot_ref):
        yg = _gelu(ys_ref[...])
        q = _dot(yg.astype(BF16), w_ref[...]) + b_ref[...]
        y = yg * _sigmoid(q) * _silu(zs_ref[...].astype(F32))
        o_ref[...] = y.astype(BF16)
        ot_ref[...] = y.T.astype(BF16)

    return pl.pallas_call(
        kern, name="glu_fwd", grid=(seq // tr,),
        in_specs=[_row_spec(tr), _row_spec(tr, col=3), pl.BlockSpec((D_MODEL, D_MODEL), lambda c: (0, 0)), _vec_spec()],
        out_specs=(_row_spec(tr), _col_spec(tr)),
        out_shape=(jax.ShapeDtypeStruct((seq, D_MODEL), BF16), jax.ShapeDtypeStruct((D_MODEL, seq), BF16)),
        compiler_params=_cparams("parallel"))(ys, proj, glu_w_bf, glu_b)


def _glu_bwd(ys, proj, dyssm, glu_w_bf, glu_b, dproj):
    seq = ys.shape[0]
    tr = min(ROW_CHUNK_WIDE, seq)
    nc = seq // tr

    def kern(ys_ref, zs_ref, dy_ref, w_ref, b_ref, _, dys_ref, dzs_ref, dq_ref, yg_ref, db_ref, acc_b):
        c = pl.program_id(0)

        @pl.when(c == 0)
        def _():
            acc_b[...] = jnp.zeros_like(acc_b)

        ysv = ys_ref[...]
        yg = _gelu(ysv)
        yg_bf = yg.astype(BF16)
        q = _dot(yg_bf, w_ref[...]) + b_ref[...]
        sg = _sigmoid(q)
        zs = zs_ref[...].astype(F32)
        dyv = dy_ref[...]
        dyglu = dyv * _silu(zs)
        dzs_ref[...] = (dyv * (yg * sg) * _dsilu(zs)).astype(BF16)
        dq = dyglu * yg * sg * (1.0 - sg)
        dq_bf = dq.astype(BF16)
        acc_b[...] += _acc8(dq)
        dyg = dyglu * sg + _dot_nt(dq_bf, w_ref[...])
        dys_ref[...] = dyg * _dgelu(ysv)
        dq_ref[...] = dq_bf
        yg_ref[...] = yg.T.astype(BF16)

        @pl.when(c == nc - 1)
        def _():
            db_ref[...] = jnp.sum(acc_b[...], axis=0, keepdims=True)

    bf = jax.ShapeDtypeStruct((seq, D_MODEL), BF16)
    return pl.pallas_call(
        kern, name="glu_bwd", grid=(nc,),
        in_specs=[_row_spec(tr), _row_spec(tr, col=3), _row_spec(tr),
                  pl.BlockSpec((D_MODEL, D_MODEL), lambda c: (0, 0)), _vec_spec(), _ANY],
        out_specs=(_row_spec(tr), _row_spec(tr, col=3), _row_spec(tr), _col_spec(tr), _vec_spec()),
        out_shape=(jax.ShapeDtypeStruct((seq, D_MODEL), F32), jax.ShapeDtypeStruct(dproj.shape, BF16), bf,
                   jax.ShapeDtypeStruct((D_MODEL, seq), BF16), jax.ShapeDtypeStruct((1, D_MODEL), F32)),
        scratch_shapes=[pltpu.VMEM((SUBLANES, D_MODEL), F32)],
        input_output_aliases={5: 1},
        compiler_params=_cparams("arbitrary"))(ys, proj, dyssm, glu_w_bf, glu_b, dproj)


def _out_fwd_bwd(ypool, yssm, proj, x, tgt, gate, g2, wbp_bf, wbs_bf, wout_bf):
    seq = x.shape[0]
    tr = min(ROW_CHUNK, seq)
    nc = seq // tr

    def kern(yp_ref, ysm_ref, gp_ref, gs_ref, x_ref, t_ref, gate_ref, g2_ref, wbp_ref, wbs_ref, wo_ref,
             dy_ref, dyp_ref, dys_ref, d45_ref, mb_ref, dob_ref, dbp_ref, dbs_ref, loss_ref, dgate_ref, dg2_ref,
             acc_l, acc_gate, acc_g2):
        c = pl.program_id(0)

        @pl.when(c == 0)
        def _():
            acc_l[...] = jnp.zeros_like(acc_l)
            acc_gate[...] = jnp.zeros_like(acc_gate)
            acc_g2[...] = jnp.zeros_like(acc_g2)

        bp = _dot(yp_ref[...], wbp_ref[...])
        bs = _dot(ysm_ref[...], wbs_ref[...])
        sp = _sigmoid(gp_ref[...].astype(F32))
        ss = _sigmoid(gs_ref[...].astype(F32))
        merged = sp * bp + ss * bs
        mb = merged.astype(BF16)
        out = _dot(mb, wo_ref[...])
        r2 = lax.rsqrt(jnp.mean(out * out, axis=-1, keepdims=True) + RMS_EPS)
        oh = out * r2
        gate_v, g2_v = gate_ref[...], g2_ref[...]
        ohg = oh * g2_v
        diff = (x_ref[...] + gate_v * ohg) - t_ref[...]
        acc_l[...] += _acc8(diff * diff)
        dyv = diff * (1.0 / D_MODEL)
        dy_ref[...] = dyv
        acc_gate[...] += _acc8(dyv * ohg)
        t = dyv * gate_v
        acc_g2[...] += _acc8(t * oh)
        doh = t * g2_v
        dout = r2 * (doh - oh * jnp.mean(doh * oh, axis=-1, keepdims=True))
        dob = dout.astype(BF16)
        dmerged = _dot_nt(dob, wo_ref[...])
        dbp = (dmerged * sp).astype(BF16)
        dbs = (dmerged * ss).astype(BF16)
        d45_ref[:, 0:D_MODEL] = (dmerged * bp * sp * (1.0 - sp)).astype(BF16)
        d45_ref[:, D_MODEL:] = (dmerged * bs * ss * (1.0 - ss)).astype(BF16)
        dyp_ref[...] = _dot_nt(dbp, wbp_ref[...])
        dys_ref[...] = _dot_nt(dbs, wbs_ref[...])
        mb_ref[...] = merged.T.astype(BF16)
        dob_ref[...] = dob
        dbp_ref[...] = dbp
        dbs_ref[...] = dbs

        @pl.when(c == nc - 1)
        def _():
            tot = jnp.sum(acc_l[...], axis=0, keepdims=True)
            loss_ref[...] = jnp.sum(tot, axis=1, keepdims=True) * (0.5 / D_MODEL)
            dgate_ref[...] = jnp.sum(acc_gate[...], axis=0, keepdims=True)
            dg2_ref[...] = jnp.sum(acc_g2[...], axis=0, keepdims=True)

    wspec = pl.BlockSpec((D_MODEL, D_MODEL), lambda c: (0, 0))
    f32 = jax.ShapeDtypeStruct((seq, D_MODEL), F32)
    bf = jax.ShapeDtypeStruct((seq, D_MODEL), BF16)
    vec = jax.ShapeDtypeStruct((1, D_MODEL), F32)
    acc = pltpu.VMEM((SUBLANES, D_MODEL), F32)
    return pl.pallas_call(
        kern, name="out_fwd_bwd", grid=(nc,),
        in_specs=[_row_spec(tr), _row_spec(tr), _row_spec(tr, col=4), _row_spec(tr, col=5), _row_spec(tr), _row_spec(tr),
                  _vec_spec(), _vec_spec(), wspec, wspec, wspec],
        out_specs=(_row_spec(tr), _row_spec(tr), _row_spec(tr), pl.BlockSpec((tr, 2 * D_MODEL), lambda c: (c, 2)),
                   _col_spec(tr), _row_spec(tr), _row_spec(tr), _row_spec(tr),
                   pl.BlockSpec((1, 1), lambda c: (0, 0)), _vec_spec(), _vec_spec()),
        out_shape=(f32, f32, f32, jax.ShapeDtypeStruct((seq, proj.shape[1]), BF16),
                   jax.ShapeDtypeStruct((D_MODEL, seq), BF16), bf, bf, bf,
                   jax.ShapeDtypeStruct((1, 1), F32), vec, vec),
        scratch_shapes=[acc, acc, acc],
        compiler_params=_cparams("arbitrary"))(ypool, yssm, proj, proj, x, tgt, gate, g2, wbp_bf, wbs_bf, wout_bf)


def _in_bwd(dh, x, dy, g1, scale):
    seq = x.shape[0]
    tr = min(ROW_CHUNK_WIDE, seq)
    nc = seq // tr

    def kern(dh_ref, x_ref, dy_ref, g_ref, sc_ref, dx_ref, dsh_ref, dsc_ref, dg_ref, a_sh, a_sc, a_g):
        c = pl.program_id(0)

        @pl.when(c == 0)
        def _():
            a_sh[...] = jnp.zeros_like(a_sh)
            a_sc[...] = jnp.zeros_like(a_sc)
            a_g[...] = jnp.zeros_like(a_g)

        xv = x_ref[...]
        r = lax.rsqrt(jnp.mean(xv * xv, axis=-1, keepdims=True) + RMS_EPS)
        xh = xv * r
        g = g_ref[...]
        dhv = dh_ref[...]
        a_sh[...] += _acc8(dhv)
        a_sc[...] += _acc8(dhv * (xh * g))
        dn = dhv * (1.0 + sc_ref[...])
        a_g[...] += _acc8(dn * xh)
        dxh = dn * g
        dx_ref[...] = dy_ref[...] + r * (dxh - xh * jnp.mean(dxh * xh, axis=-1, keepdims=True))

        @pl.when(c == nc - 1)
        def _():
            dsh_ref[...] = jnp.sum(a_sh[...], axis=0, keepdims=True)
            dsc_ref[...] = jnp.sum(a_sc[...], axis=0, keepdims=True)
            dg_ref[...] = jnp.sum(a_g[...], axis=0, keepdims=True)

    vec = jax.ShapeDtypeStruct((1, D_MODEL), F32)
    acc = pltpu.VMEM((SUBLANES, D_MODEL), F32)
    return pl.pallas_call(
        kern, name="in_bwd", grid=(nc,),
        in_specs=[_row_spec(tr), _row_spec(tr), _row_spec(tr), _vec_spec(), _vec_spec()],
        out_specs=(_row_spec(tr), _vec_spec(), _vec_spec(), _vec_spec()),
        out_shape=(jax.ShapeDtypeStruct((seq, D_MODEL), F32), vec, vec, vec),
        scratch_shapes=[acc, acc, acc],
        compiler_params=_cparams("arbitrary"))(dh, x, dy, g1, scale)


SLAB = 2 * SUBLANES


def _local_scan(a_re, a_im, br, bi, xr, xi, row0, ls, reverse, init=None, xb=None):
    if init is None:
        x_re = jnp.zeros((SUBLANES, STATE_W), F32)
        x_im = jnp.zeros((SUBLANES, STATE_W), F32)
    else:
        x_re, x_im = init
    for i in (range(ls - 1, -1, -1) if reverse else range(ls)):
        src = pl.ds(SUBLANES * i, SUBLANES)
        dst = pl.ds(row0 + SUBLANES * i, SUBLANES)
        n_re = a_re * x_re - a_im * x_im + br[src, :]
        n_im = a_re * x_im + a_im * x_re + bi[src, :]
        if xb is not None and i % 2 == 1:
            pair = pl.ds(SUBLANES * (i - 1), SLAB)
            xb[0][pair, :] = jnp.concatenate([x_re, n_re], axis=0).astype(BF16)
            xb[1][pair, :] = jnp.concatenate([x_im, n_im], axis=0).astype(BF16)
        x_re, x_im = n_re, n_im
        xr[dst, :] = x_re
        xi[dst, :] = x_im
    return x_re, x_im


def _two(v):
    return jnp.concatenate([v, v], axis=0)


def _unpermute_rhs(v, sel):
    hi = v.astype(BF16)
    r1 = v - hi.astype(F32)
    mid = r1.astype(BF16)
    lo = (r1 - mid.astype(F32)).astype(BF16)
    return _dot(hi, sel) + _dot(mid, sel) + _dot(lo, sel)


def _scan_specs(tc, nb, rows_of):
    return dict(
        us=pl.BlockSpec((tc, nb * LANES), lambda b, c: (rows_of(c), 2 * D_MODEL // (nb * LANES) + b)),
        tok=pl.BlockSpec((tc, nb * LANES), lambda b, c: (rows_of(c), b)),
        bblk=pl.BlockSpec((nb, LANES, STATE_W), lambda b, c: (b, 0, 0)),
        cblk=pl.BlockSpec((nb, STATE_W, LANES), lambda b, c: (b, 0, 0)),
        vec=pl.BlockSpec((1, nb * STATE_W), lambda b, c: (0, b)),
        tab=pl.BlockSpec((tc, nb * STATE_W), lambda b, c: (0, b)),
        car=pl.BlockSpec((SUBLANES, nb * STATE_W), lambda b, c: (rows_of(c), b)),
        dvec=pl.BlockSpec((1, nb * LANES), lambda b, c: (0, b)))


def _ssm_scan_fwd(proj, bb_re, bb_im, cm_re, cm_im, abar_re, abar_im, pw_re, pw_im, d_skip, tc):
    seq = proj.shape[0]
    nc = seq // tc
    ls = tc // SUBLANES
    nb = SCAN_BLOCKS

    def kern(us_ref, bbr_ref, bbi_ref, cmr_ref, cmi_ref, ar_ref, ai_ref, pwr_ref, pwi_ref, d_ref,
             ys_ref, ecr_ref, eci_ref, bur, bui, car_r, car_i, end_r, end_i, upb, xb_r, xb_i, *nat):
        c = pl.program_id(1)

        @pl.when(c == 0)
        def _():
            car_r[...] = jnp.zeros_like(car_r)
            car_i[...] = jnp.zeros_like(car_i)

        for j in range(nb):
            cols = pl.ds(j * LANES, LANES)
            scols = pl.ds(j * STATE_W, STATE_W)
            nat[j][...] = us_ref[:, cols].astype(F32)
            for i in range(ls):
                upb[j, pl.ds(SUBLANES * i, SUBLANES), :] = nat[j][pl.ds(i, SUBLANES, stride=ls), :]
            u = upb[j]
            up = u.astype(BF16)
            bur[j] = _dot(up, bbr_ref[j])
            bui[j] = _dot(up, bbi_ref[j])
            a_re = jnp.broadcast_to(ar_ref[:, scols], (SUBLANES, STATE_W))
            a_im = jnp.broadcast_to(ai_ref[:, scols], (SUBLANES, STATE_W))
            x_re, x_im = _local_scan(a_re, a_im, bur.at[j], bui.at[j], bur.at[j], bui.at[j], 0, ls, False)
            end_r[j] = x_re
            end_i[j] = x_im
            big_re = pwr_ref[tc - 1:tc, scols]
            big_im = pwi_ref[tc - 1:tc, scols]
            e_re = car_r[j, 0:1, :]
            e_im = car_i[j, 0:1, :]
            for s in range(SUBLANES):
                n_re = end_r[j, s:s + 1, :] + big_re * e_re - big_im * e_im
                n_im = end_i[j, s:s + 1, :] + big_re * e_im + big_im * e_re
                e_re, e_im = n_re, n_im
                if s < SUBLANES - 1:
                    car_r[j, s + 1:s + 2, :] = e_re
                    car_i[j, s + 1:s + 2, :] = e_im
            ec_re = car_r[j]
            ec_im = car_i[j]
            ecr_ref[:, scols] = ec_re
            eci_ref[:, scols] = ec_im
            e2_re, e2_im = _two(ec_re), _two(ec_im)
            for k in range(tc // SLAB):
                rows_k = pl.ds(SLAB * k, SLAB)
                p_re = pwr_ref[rows_k, scols]
                p_im = pwi_ref[rows_k, scols]
                xb_r[j, rows_k, :] = (bur[j, rows_k, :] + p_re * e2_re - p_im * e2_im).astype(BF16)
                xb_i[j, rows_k, :] = (bui[j, rows_k, :] + p_re * e2_im + p_im * e2_re).astype(BF16)
            upb[j] = _dot(xb_r[j], cmr_ref[j]) - _dot(xb_i[j], cmi_ref[j]) + d_ref[:, cols] * u
            for i in range(ls):
                nat[j][pl.ds(i, SUBLANES, stride=ls), :] = upb[j, pl.ds(SUBLANES * i, SUBLANES), :]
            ys_ref[:, cols] = nat[j][...]
            car_r[j, 0:1, :] = e_re
            car_i[j, 0:1, :] = e_im

    sp = _scan_specs(tc, nb, lambda c: c)
    carry_shape = jax.ShapeDtypeStruct((nc * SUBLANES, STATE_ALL), F32)
    small = pltpu.VMEM((nb, SUBLANES, STATE_W), F32)
    big = pltpu.VMEM((nb, tc, STATE_W), F32)
    return pl.pallas_call(
        kern, name="ssm_scan_fwd", grid=(LANE_BLOCKS // nb, nc),
        in_specs=[sp["us"], sp["bblk"], sp["bblk"], sp["cblk"], sp["cblk"], sp["vec"], sp["vec"], sp["tab"], sp["tab"],
                  sp["dvec"]],
        out_specs=(sp["tok"], sp["car"], sp["car"]),
        out_shape=(jax.ShapeDtypeStruct((seq, D_MODEL), F32), carry_shape, carry_shape),
        scratch_shapes=[big, big, small, small, small, small, pltpu.VMEM((nb, tc, LANES), F32),
                        pltpu.VMEM((nb, tc, STATE_W), BF16), pltpu.VMEM((nb, tc, STATE_W), BF16)]
        + [pltpu.VMEM((tc, LANES), F32)] * nb,
        compiler_params=_cparams("parallel", "arbitrary"),
    )(proj, bb_re, bb_im, cm_re, cm_im, abar_re, abar_im, pw_re, pw_im, d_skip)


def _ssm_scan_bwd(proj, dys, ec_re, ec_im, bb_re, bb_im, cm_re, cm_im, abar_re, abar_im,
                  pw_re, pw_im, pv_re, pv_im, d_skip, dproj, tc):
    seq = proj.shape[0]
    nc = seq // tc
    ls = tc // SUBLANES
    nb = SCAN_BLOCKS

    def kern(us_ref, dys_ref, ecr_ref, eci_ref, bbr_ref, bbi_ref, cmr_ref, cmi_ref, ar_ref, ai_ref,
             pwr_ref, pwi_ref, pvr_ref, pvi_ref, d_ref, _,
             dus_ref, dbbr_ref, dbbi_ref, dcmr_ref, dcmi_ref, dar_ref, dai_ref, dd_ref,
             bur, bui, xr, xi, gr, gi, fc_r, fc_i, a_bbr, a_bbi, a_cmr, a_cmi, a_ar, a_ai, a_dd, upb, dpb, hb_r, hb_i,
             *nat):
        c = pl.program_id(1)

        @pl.when(c == 0)
        def _():
            for ref in (fc_r, fc_i, a_bbr, a_bbi, a_cmr, a_cmi, a_ar, a_ai, a_dd):
                ref[...] = jnp.zeros_like(ref)

        for j in range(nb):
            cols = pl.ds(j * LANES, LANES)
            scols = pl.ds(j * STATE_W, STATE_W)
            nat_u, nat_d = nat[2 * j], nat[2 * j + 1]
            nat_u[...] = us_ref[:, cols].astype(F32)
            nat_d[...] = dys_ref[:, cols]
            for i in range(ls):
                rows_i = pl.ds(SUBLANES * i, SUBLANES)
                upb[j, rows_i, :] = nat_u[pl.ds(i, SUBLANES, stride=ls), :]
                dpb[j, rows_i, :] = nat_d[pl.ds(i, SUBLANES, stride=ls), :]
            u = upb[j]
            dysv = dpb[j]
            a_dd[j] += _acc8(dysv * u)
            up = u.astype(BF16)
            bur[j] = _dot(up, bbr_ref[j])
            bui[j] = _dot(up, bbi_ref[j])
            a_re = jnp.broadcast_to(ar_ref[:, scols], (SUBLANES, STATE_W))
            a_im = jnp.broadcast_to(ai_ref[:, scols], (SUBLANES, STATE_W))
            ec_r = ecr_ref[:, scols]
            ec_i = eci_ref[:, scols]
            xr[j, 0:SUBLANES, :] = ec_r
            xi[j, 0:SUBLANES, :] = ec_i
            _local_scan(a_re, a_im, bur.at[j], bui.at[j], xr.at[j], xi.at[j], SUBLANES, ls, False, init=(ec_r, ec_i),
                        xb=(hb_r.at[j], hb_i.at[j]))
            dysp = dysv.astype(BF16)
            a_cmr[j] += _dot_tn(dysp, hb_r[j])
            a_cmi[j] -= _dot_tn(dysp, hb_i[j])
            gr[j] = _dot_nt(dysp, cmr_ref[j])
            gi[j] = -_dot_nt(dysp, cmi_ref[j])
            _local_scan(a_re, -a_im, gr.at[j], gi.at[j], gr.at[j], gi.at[j], 0, ls, True)
            big_re = pwr_ref[tc - 1:tc, scols]
            big_im = -pwi_ref[tc - 1:tc, scols]
            f_re = fc_r[j, SUBLANES - 1:SUBLANES, :]
            f_im = fc_i[j, SUBLANES - 1:SUBLANES, :]
            for s in range(SUBLANES - 1, -1, -1):
                n_re = gr[j, s:s + 1, :] + big_re * f_re - big_im * f_im
                n_im = gi[j, s:s + 1, :] + big_re * f_im + big_im * f_re
                f_re, f_im = n_re, n_im
                if s > 0:
                    fc_r[j, s - 1:s, :] = f_re
                    fc_i[j, s - 1:s, :] = f_im
            f2_r, f2_i = _two(fc_r[j]), _two(fc_i[j])
            acc_r = jnp.zeros((SUBLANES, STATE_W), F32)
            acc_i = jnp.zeros((SUBLANES, STATE_W), F32)
            for k in range(tc // SLAB):
                rows_k = pl.ds(SLAB * k, SLAB)
                q_re = pvr_ref[rows_k, scols]
                q_im = pvi_ref[rows_k, scols]
                lam_re = gr[j, rows_k, :] + q_re * f2_r + q_im * f2_i
                lam_im = gi[j, rows_k, :] + q_re * f2_i - q_im * f2_r
                xp_re = xr[j, rows_k, :]
                xp_im = xi[j, rows_k, :]
                d_r = lam_re * xp_re + lam_im * xp_im
                d_i = lam_im * xp_re - lam_re * xp_im
                acc_r = acc_r + (d_r[0:SUBLANES] + d_r[SUBLANES:])
                acc_i = acc_i + (d_i[0:SUBLANES] + d_i[SUBLANES:])
                hb_r[j, rows_k, :] = lam_re.astype(BF16)
                hb_i[j, rows_k, :] = lam_im.astype(BF16)
            a_ar[j] += acc_r
            a_ai[j] += acc_i
            fc_r[j, SUBLANES - 1:SUBLANES, :] = f_re
            fc_i[j, SUBLANES - 1:SUBLANES, :] = f_im
            lb_re = hb_r[j]
            lb_im = hb_i[j]
            a_bbr[j] += _dot_tn(up, lb_re)
            a_bbi[j] += _dot_tn(up, lb_im)
            dpb[j] = _dot_nt(lb_re, bbr_ref[j]) + _dot_nt(lb_im, bbi_ref[j]) + dysv * d_ref[:, cols]
            for i in range(ls):
                nat_d[pl.ds(i, SUBLANES, stride=ls), :] = dpb[j, pl.ds(SUBLANES * i, SUBLANES), :]
            dus_ref[:, cols] = nat_d[...].astype(BF16)

        @pl.when(c == nc - 1)
        def _():
            row_g = lax.broadcasted_iota(jnp.int32, (LANES, STATE_W), 0) // SSM_H
            col_g = lax.broadcasted_iota(jnp.int32, (LANES, STATE_W), 1) // SSM_P
            fold = (lax.broadcasted_iota(jnp.int32, (STATE_W, SSM_P), 0) % SSM_P
                    == lax.broadcasted_iota(jnp.int32, (STATE_W, SSM_P), 1)).astype(BF16)
            for j in range(nb):
                rows_j = pl.ds(j * LANES, LANES)
                for acc, out in ((a_bbr, dbbr_ref), (a_bbi, dbbi_ref), (a_cmr, dcmr_ref), (a_cmi, dcmi_ref)):
                    out[rows_j, :] = _unpermute_rhs(jnp.where(row_g == col_g, acc[j], 0.0), fold)
                dar_ref[:, pl.ds(j * STATE_W, STATE_W)] = jnp.sum(a_ar[j], axis=0, keepdims=True)
                dai_ref[:, pl.ds(j * STATE_W, STATE_W)] = jnp.sum(a_ai[j], axis=0, keepdims=True)
                dd_ref[:, pl.ds(j * LANES, LANES)] = jnp.sum(a_dd[j], axis=0, keepdims=True)

    sp = _scan_specs(tc, nb, lambda c: nc - 1 - c)
    ghp = pl.BlockSpec((nb * LANES, SSM_P), lambda b, c: (b, 0))
    ghp_shape = jax.ShapeDtypeStruct((SSM_G * SSM_H, SSM_P), F32)
    small = pltpu.VMEM((nb, SUBLANES, STATE_W), F32)
    big = pltpu.VMEM((nb, tc, STATE_W), F32)
    bigp = pltpu.VMEM((nb, tc + SUBLANES, STATE_W), F32)
    blk = pltpu.VMEM((nb, LANES, STATE_W), F32)
    tok = pltpu.VMEM((nb, tc, LANES), F32)
    return pl.pallas_call(
        kern, name="ssm_scan_bwd", grid=(LANE_BLOCKS // nb, nc),
        in_specs=[sp["us"], sp["tok"], sp["car"], sp["car"], sp["bblk"], sp["bblk"], sp["cblk"], sp["cblk"],
                  sp["vec"], sp["vec"], sp["tab"], sp["tab"], sp["tab"], sp["tab"], sp["dvec"], _ANY],
        out_specs=(sp["us"], ghp, ghp, ghp, ghp, sp["vec"], sp["vec"], sp["dvec"]),
        out_shape=(jax.ShapeDtypeStruct(dproj.shape, BF16), ghp_shape, ghp_shape, ghp_shape, ghp_shape,
                   jax.ShapeDtypeStruct((1, STATE_ALL), F32), jax.ShapeDtypeStruct((1, STATE_ALL), F32),
                   jax.ShapeDtypeStruct((1, D_MODEL), F32)),
        scratch_shapes=[big, big, bigp, bigp, big, big, small, small, blk, blk, blk, blk,
                        small, small, pltpu.VMEM((nb, SUBLANES, LANES), F32), tok, tok,
                        pltpu.VMEM((nb, tc, STATE_W), BF16), pltpu.VMEM((nb, tc, STATE_W), BF16)]
        + [pltpu.VMEM((tc, LANES), F32)] * (2 * nb),
        input_output_aliases={15: 0},
        compiler_params=_cparams("parallel", "arbitrary"),
    )(proj, dys, ec_re, ec_im, bb_re, bb_im, cm_re, cm_im, abar_re, abar_im, pw_re, pw_im, pv_re, pv_im, d_skip, dproj)


def _eye5():
    return jnp.asarray(np.eye(GROUPS_PER_BLOCK, dtype=np.float32)[None, :, None, :, None])


def _embed_b(bb_t):
    t = bb_t.transpose(1, 0, 2).reshape(LANE_BLOCKS, GROUPS_PER_BLOCK, SSM_H, 1, SSM_P)
    return (t * _eye5()).reshape(LANE_BLOCKS, LANES, STATE_W)


def _embed_c(c_ghp):
    t = c_ghp.transpose(0, 2, 1).reshape(LANE_BLOCKS, GROUPS_PER_BLOCK, SSM_P, 1, SSM_H)
    return (t * _eye5()).reshape(LANE_BLOCKS, STATE_W, LANES)


def _local_step(x, c_row, tgt, w_ada_bf, b_ada, g1, g2, w_in_bf, pool_w_bf, pscale, a_re, a_im, log_dt,
                b_re_t, b_im_t, c_re, c_im, d_skip, glu_w_bf, glu_b, wbp_bf, wbs_bf, wout_bf,
                early_weight=None, late_weights=None, ride_for_dw_in=None, ride_for_dh=None):
    seq = x.shape[0]
    tc = min(SCAN_CHUNK, seq)
    mod8, silu_c = _mod_kernel(c_row, w_ada_bf, b_ada)
    mod = mod8[0:1]
    shift, scale, gate = mod[:, 0:D_MODEL], mod[:, D_MODEL:2 * D_MODEL], mod[:, 2 * D_MODEL:]

    abar_re, abar_im, bb_re_t, bb_im_t = _ssm_params(a_re, a_im, log_dt, b_re_t, b_im_t)
    abar_re_f, abar_im_f = abar_re.reshape(1, STATE_ALL), abar_im.reshape(1, STATE_ALL)
    pw_re, pw_im, pv_re, pv_im = _pow_tables(abar_re_f, abar_im_f, tc)
    bbe_re, bbe_im = _embed_b(bb_re_t).astype(BF16), _embed_b(bb_im_t).astype(BF16)
    cme_re, cme_im = _embed_c(c_re).astype(BF16), _embed_c(c_im).astype(BF16)
    d_row = d_skip.reshape(1, D_MODEL)

    if early_weight:
        h, h_t, *gathered = _in_norm(x, g1, scale, shift, ride=early_weight[0])
        w_in_bf, w_in_cols = early_weight[1](*gathered)
    else:
        h, h_t = _in_norm(x, g1, scale, shift)
        w_in_cols = [w_in_bf]
    bn_proj = next(b for b in (1536, 1024, 768, 512, 256) if w_in_cols[0].shape[1] % b == 0)
    if late_weights:
        proj, *gathered = _mm([h], w_in_cols, name="proj", out_dtype=BF16, bm=1024, bn=bn_proj, bk=1024,
                              ride=late_weights[0])
        pool_w_bf, glu_w_bf, wbp_bf, wbs_bf, wout_bf = late_weights[1](*gathered)
    else:
        proj = _mm([h], w_in_cols, name="proj", out_dtype=BF16, bm=1024, bn=bn_proj, bk=1024)
    ypool, ypool_t = _pool_fwd(proj, pool_w_bf, pscale)
    ys, ec_re, ec_im = _ssm_scan_fwd(proj, bbe_re, bbe_im, cme_re, cme_im, abar_re_f, abar_im_f,
                                      pw_re, pw_im, d_row, tc)
    yssm, yssm_t = _glu_fwd(ys, proj, glu_w_bf, glu_b)
    (dy, dypool, dyssm, dproj, merged_t, dob, dbp, dbs, loss, dgate, dg2) = _out_fwd_bwd(
        ypool, yssm, proj, x, tgt, gate, g2, wbp_bf, wbs_bf, wout_bf)

    d_wout = _mm([merged_t], [dob], name="dw_out", bm=1024, bn=1024, bk=1024)
    d_wbp = _mm([ypool_t], [dbp], name="dw_bp", bm=1024, bn=1024, bk=1024)
    d_wbs = _mm([yssm_t], [dbs], name="dw_bs", bm=1024, bn=1024, bk=1024)
    dys, dproj, dq, yg_t, d_glu_b = _glu_bwd(ys, proj, dyssm, glu_w_bf, glu_b, dproj)
    d_glu_w = _mm([yg_t], [dq], name="dw_glu", bm=1024, bn=1024, bk=1024)
    (dproj, dbbe_re, dbbe_im, dcme_re, dcme_im, d_abar_re, d_abar_im, d_dskip) = _ssm_scan_bwd(
        proj, dys, ec_re, ec_im, bbe_re, bbe_im, cme_re, cme_im, abar_re_f, abar_im_f,
        pw_re, pw_im, pv_re, pv_im, d_row, dproj, tc)
    dproj, d_pool_w, d_pscale = _pool_bwd(proj, dypool, pool_w_bf, pscale, dproj)
    dparts = [dproj]
    small_ready = dict(
        dg2=dg2, d_pscale=d_pscale, d_glu_b=d_glu_b, d_dskip=d_dskip, d_abar_re=d_abar_re, d_abar_im=d_abar_im,
        d_bb_re_t=dbbe_re.reshape(SSM_G, SSM_H, SSM_P).transpose(1, 0, 2),
        d_bb_im_t=dbbe_im.reshape(SSM_G, SSM_H, SSM_P).transpose(1, 0, 2),
        d_c_re=dcme_re.reshape(SSM_G, SSM_H, SSM_P), d_c_im=dcme_im.reshape(SSM_G, SSM_H, SSM_P))
    ride = ride_for_dw_in(small_ready) if ride_for_dw_in else None
    d_win = _mm([h_t], dparts, name="dw_in", bm=1024, bn=1024, bk=1024, ride=ride)
    rode_dw_in = ()
    if ride:
        d_win, rode_dw_in = d_win[0], tuple(d_win[1:])
    big_grads = dict(d_win=d_win, d_glu_w=d_glu_w, d_wbp=d_wbp, d_wbs=d_wbs, d_wout=d_wout, d_pool_w=d_pool_w)
    ride = ride_for_dh(big_grads) if ride_for_dh else None
    dh = _mm(dparts, [w_in_bf], tb=True, name="dh", bm=1024, bn=1024, bk=1024, ride=ride)
    rode = ()
    if ride:
        dh, rode = dh[0], tuple(dh[1:])
    grad_x, dshift, dscale, dg1 = _in_bwd(dh, x, dy, g1, scale)
    dmod = jnp.concatenate([dshift, dscale, dgate], axis=1)
    return dict(
        rode=rode, rode_dw_in=rode_dw_in, loss=loss[0, 0], grad_x=grad_x, dmod=dmod, silu_c=silu_c, dg1=dg1,
        **small_ready, **big_grads)


def _position():
    x, y, c = lax.axis_index("x"), lax.axis_index("y"), lax.axis_index("c")
    chips = [(1 - x, y), (x, 1 - y), (1 - x, 1 - y)]
    return x, y, c, chips


_ANY = pl.BlockSpec(memory_space=pl.ANY)
COMM_CHUNKS = 4
COMM_ROW_ALIGN = 16


def _row_chunks(rows, k):
    assert rows % (k * COMM_ROW_ALIGN) == 0, (rows, k)
    step = rows // k
    return [(q * step, step) for q in range(k)]


def _ag_weights_ride(packed, n_chunks=COMM_CHUNKS):
    rows, width = packed.shape
    half = rows // 2
    chunks = _row_chunks(half, n_chunks)
    nq = len(chunks)

    def parts(p_ref, out_ref, send_sems, recv_sems):
        x, y, c, chips = _position()
        sibling = (x, y, 1 - c)

        def copy(k, chip, h, q, to, src=None):
            start, size = chunks[q]
            rows_q = pl.ds(h * half + start, size)
            dst = out_ref.at[2 * chip[0] + chip[1], rows_q, :]
            return pltpu.make_async_remote_copy(
                src_ref=dst if src is None else src.at[rows_q, :], dst_ref=dst, send_sem=send_sems.at[k * nq + q],
                recv_sem=recv_sems.at[k * nq + q], device_id=to, device_id_type=MESH_ID)

        mine = [copy(6 + h, (x, y), h, q, sibling, src=p_ref) for h in range(2) for q in range(nq)]
        first = [copy(j, (x, y), c, q, (*chip, c), src=p_ref) for q in range(nq) for j, chip in enumerate(chips)]
        return (x, y, c), chips, sibling, copy, mine, first

    def start(ins, outs, sems):
        _, _, _, _, mine, first = parts(ins[0], outs[0], sems[0], sems[1])
        for cp in first + mine:
            cp.start()

    def wait(ins, outs, sems):
        (x, y, c), chips, sibling, copy, mine, first = parts(ins[0], outs[0], sems[0], sems[1])
        passed = []
        for q in range(nq):
            for j, chip in enumerate(chips):
                copy(j, chip, c, q, (x, y, c)).wait_recv()
                fwd = copy(3 + j, chip, c, q, sibling)
                fwd.start()
                passed.append(fwd)
        for q in range(nq):
            for j, chip in enumerate(chips):
                copy(3 + j, chip, 1 - c, q, (x, y, c)).wait_recv()
        for cp in mine:
            cp.wait_recv()
        for cp in first + passed + mine:
            cp.wait_send()

    return _Ride([packed], [jax.ShapeDtypeStruct((N_CHIPS, rows, width), packed.dtype)],
                 [pltpu.SemaphoreType.DMA((8 * nq,)), pltpu.SemaphoreType.DMA((8 * nq,))], start, wait)


def _join_rides(rides):
    def split(seq, counts):
        out, at = [], 0
        for n in counts:
            out.append(seq[at:at + n])
            at += n
        return out

    n_in = [len(r.inputs) for r in rides]
    n_out = [len(r.out_shapes) for r in rides]
    n_sem = [len(r.scratch) for r in rides]

    def start(ins, outs, sems):
        for r, i, o, s in zip(rides, split(ins, n_in), split(outs, n_out), split(sems, n_sem)):
            r.start(i, o, s)

    def wait(ins, outs, sems):
        for r, i, o, s in zip(rides, split(ins, n_in), split(outs, n_out), split(sems, n_sem)):
            r.wait(i, o, s)

    return _Ride([a for r in rides for a in r.inputs], [a for r in rides for a in r.out_shapes],
                 [a for r in rides for a in r.scratch], start, wait)


def _run_ride(ride, name):
    n_in, n_out = len(ride.inputs), len(ride.out_shapes)

    def body(*refs):
        ins, outs, sems = refs[:n_in], refs[n_in:n_in + n_out], refs[n_in + n_out:]
        ride.start(ins, outs, sems)
        ride.wait(ins, outs, sems)

    return pl.pallas_call(
        body, name=name, in_specs=[_ANY] * n_in, out_specs=(_ANY,) * n_out, out_shape=tuple(ride.out_shapes),
        scratch_shapes=list(ride.scratch))(*ride.inputs)


def _small_allgather_ride(buf):
    rows, width = buf.shape
    chunks = _row_chunks(rows, COMM_CHUNKS)
    nq = len(chunks)

    def parts(b_ref, all_ref, send_sems, recv_sems, local_sem):
        x, y, c, chips = _position()
        me, sibling = (x, y, c), (x, y, 1 - c)

        def copy(k, block, q, to, src=None):
            rows_q = pl.ds(chunks[q][0], chunks[q][1])
            dst = all_ref.at[4 * block[0] + 2 * block[1] + block[2], rows_q, :]
            return pltpu.make_async_remote_copy(
                src_ref=dst if src is None else src.at[rows_q, :], dst_ref=dst, send_sem=send_sems.at[k * nq + q],
                recv_sem=recv_sems.at[k * nq + q], device_id=to, device_id_type=MESH_ID)

        mine = pltpu.make_async_copy(b_ref, all_ref.at[4 * x + 2 * y + c], local_sem)
        first = []
        for q in range(nq):
            first += [copy(1 + j, me, q, (*chip, c), src=b_ref) for j, chip in enumerate(chips)]
            first.append(copy(0, me, q, sibling, src=b_ref))
        return me, sibling, c, chips, copy, mine, first

    def start(ins, outs, sems):
        _, _, _, _, _, mine, first = parts(ins[0], outs[0], *sems)
        mine.start()
        for cp in first:
            cp.start()

    def wait(ins, outs, sems):
        me, sibling, c, chips, copy, mine, first = parts(ins[0], outs[0], *sems)
        passed = []
        for q in range(nq):
            for j, chip in enumerate(chips):
                copy(1 + j, (*chip, c), q, me).wait_recv()
                fwd = copy(4 + j, (*chip, c), q, sibling)
                fwd.start()
                passed.append(fwd)
        for q in range(nq):
            copy(0, sibling, q, me).wait_recv()
            for j, chip in enumerate(chips):
                copy(4 + j, (*chip, 1 - c), q, me).wait_recv()
        for cp in first + passed:
            cp.wait_send()
        mine.wait()

    return _Ride([buf], [jax.ShapeDtypeStruct((N_DEV, rows, width), F32)],
                 [pltpu.SemaphoreType.DMA((7 * nq,)), pltpu.SemaphoreType.DMA((7 * nq,)), pltpu.SemaphoreType.DMA],
                 start, wait)


def _sum_devices(blocks):
    n, rows, width = blocks.shape
    rb = rows // 2 if (rows // 2) % SUBLANES == 0 else rows

    def kern(b_ref, o_ref):
        total = b_ref[0]
        for d in range(1, n):
            total = total + b_ref[d]
        o_ref[...] = total

    return pl.pallas_call(
        kern, name="small_sum", grid=(rows // rb,), in_specs=[pl.BlockSpec((n, rb, width), lambda i: (0, i, 0))],
        out_specs=pl.BlockSpec((rb, width), lambda i: (i, 0)), out_shape=jax.ShapeDtypeStruct((rows, width), F32),
        compiler_params=_cparams("parallel"))(blocks)


def _small_allgather_sum(buf, head_rows, n_chunks=COMM_CHUNKS):
    rows, width = buf.shape
    chunks = _row_chunks(rows, n_chunks)
    nq = len(chunks)

    def body(b_ref, head_ref, sum_ref, all_ref, send_sems, recv_sems, local_sem):
        x, y, c, chips = _position()
        me, sibling = (x, y, c), (x, y, 1 - c)

        def slot(px, py, pc):
            return all_ref.at[4 * px + 2 * py + pc]

        def copy(k, block, q, to, src=None):
            rows_q = pl.ds(chunks[q][0], chunks[q][1])
            dst = slot(*block).at[rows_q, :]
            return pltpu.make_async_remote_copy(
                src_ref=dst if src is None else src.at[rows_q, :], dst_ref=dst, send_sem=send_sems.at[k * nq + q],
                recv_sem=recv_sems.at[k * nq + q], device_id=to, device_id_type=MESH_ID)

        mine = pltpu.make_async_copy(b_ref, slot(*me), local_sem)
        mine.start()
        first = []
        for q in range(nq):
            first += [copy(1 + j, me, q, (*chip, c), src=b_ref) for j, chip in enumerate(chips)]
            first.append(copy(0, me, q, sibling, src=b_ref))
        for cp in first:
            cp.start()
        passed = []
        for q in range(nq):
            for j, chip in enumerate(chips):
                copy(1 + j, (*chip, c), q, me).wait_recv()
                fwd = copy(4 + j, (*chip, c), q, sibling)
                fwd.start()
                passed.append(fwd)
        for q in range(nq):
            copy(0, sibling, q, me).wait_recv()
            for j, chip in enumerate(chips):
                copy(4 + j, (*chip, 1 - c), q, me).wait_recv()
        for cp in first + passed:
            cp.wait_send()
        mine.wait()
        total = all_ref[0]
        for d in range(1, N_DEV):
            total = total + all_ref[d]
        sum_ref[...] = total
        head_ref[...] = all_ref[:, 0:head_rows, :]

    vm = pl.BlockSpec(memory_space=pltpu.VMEM)
    return pl.pallas_call(
        body, name="small_allgather_sum", in_specs=[vm], out_specs=(vm, vm),
        out_shape=(jax.ShapeDtypeStruct((N_DEV, head_rows, width), F32), jax.ShapeDtypeStruct((rows, width), F32)),
        scratch_shapes=[pltpu.VMEM((N_DEV, rows, width), F32), pltpu.SemaphoreType.DMA((7 * nq,)),
                        pltpu.SemaphoreType.DMA((7 * nq,)), pltpu.SemaphoreType.DMA],
        compiler_params=_cparams(),
    )(buf)


def _rs_pair(g):
    n, rows, width = g.shape
    half = rows // 2
    chunks = _row_chunks(half, COMM_CHUNKS)
    nq = len(chunks)

    def body(g_ref, got_ref, send_sems, recv_sems):
        x, y, c, _ = _position()
        swaps = []
        for k in range(n):
            for q, (start, size) in enumerate(chunks):
                swaps.append(pltpu.make_async_remote_copy(
                    src_ref=g_ref.at[k, pl.ds((1 - c) * half + start, size), :], dst_ref=got_ref.at[k, pl.ds(start, size), :],
                    send_sem=send_sems.at[k * nq + q], recv_sem=recv_sems.at[k * nq + q],
                    device_id=(x, y, 1 - c), device_id_type=MESH_ID))
        for cp in swaps:
            cp.start()
        for cp in swaps:
            cp.wait()

    return pl.pallas_call(
        body, name="rs_pair", in_specs=[_ANY], out_specs=_ANY, out_shape=jax.ShapeDtypeStruct((n, half, width), g.dtype),
        scratch_shapes=[pltpu.SemaphoreType.DMA((n * nq,)), pltpu.SemaphoreType.DMA((n * nq,))],
    )(g)


def _rs_chips_ride(part_bf):
    n, rows, width = part_bf.shape
    chunks = _row_chunks(rows, COMM_CHUNKS)
    nq = len(chunks)

    def sends(pb_ref, got_ref, send_sems, recv_sems):
        x, y, c, chips = _position()
        out = []
        for q, (start, size) in enumerate(chunks):
            for j, chip in enumerate(chips):
                out.append(pltpu.make_async_remote_copy(
                    src_ref=pb_ref.at[2 * chip[0] + chip[1], pl.ds(start, size), :], dst_ref=got_ref.at[j, pl.ds(start, size), :],
                    send_sem=send_sems.at[j * nq + q], recv_sem=recv_sems.at[j * nq + q],
                    device_id=(*chip, c), device_id_type=MESH_ID))
        return out

    def start(ins, outs, sems):
        for cp in sends(ins[0], outs[0], sems[0], sems[1]):
            cp.start()

    def wait(ins, outs, sems):
        for cp in sends(ins[0], outs[0], sems[0], sems[1]):
            cp.wait()

    return _Ride([part_bf], [jax.ShapeDtypeStruct((N_CHIPS - 1, rows, width), BF16)],
                 [pltpu.SemaphoreType.DMA((3 * nq,)), pltpu.SemaphoreType.DMA((3 * nq,))], start, wait)


def _rs_join(shard):
    rows, width = shard.shape
    half = rows // 2
    chunks = _row_chunks(half, COMM_CHUNKS)
    nq = len(chunks)

    def body(in_ref, out_ref, send_sems, recv_sems):
        x, y, c, _ = _position()
        def swap(q, h):
            rows_q = pl.ds(h * half + chunks[q][0], chunks[q][1])
            return pltpu.make_async_remote_copy(
                src_ref=in_ref.at[rows_q, :], dst_ref=out_ref.at[rows_q, :], send_sem=send_sems.at[q],
                recv_sem=recv_sems.at[q], device_id=(x, y, 1 - c), device_id_type=MESH_ID)

        for q in range(nq):
            swap(q, c).start()
        for q in range(nq):
            swap(q, 1 - c).wait_recv()
        for q in range(nq):
            swap(q, c).wait_send()

    return pl.pallas_call(
        body, name="rs_join", in_specs=[_ANY], out_specs=_ANY, input_output_aliases={0: 0},
        out_shape=jax.ShapeDtypeStruct(shard.shape, shard.dtype),
        scratch_shapes=[pltpu.SemaphoreType.DMA((nq,)), pltpu.SemaphoreType.DMA((nq,))],
    )(shard)


def _pair_add(g, got, core):
    n, half, width = got.shape
    nb = 2
    rb = half // nb

    def kern(c_ref, a_ref, b_ref, f_ref, h_ref):
        s = a_ref[...] + b_ref[...]
        f_ref[...] = s
        h_ref[...] = s.astype(BF16)

    spec = pl.BlockSpec((1, rb, width), lambda k, i, c_ref: (k, i, 0))
    return pl.pallas_call(
        kern, name="rs_pair_add",
        grid_spec=pltpu.PrefetchScalarGridSpec(
            num_scalar_prefetch=1, grid=(n, nb),
            in_specs=[pl.BlockSpec((1, rb, width), lambda k, i, c_ref: (k, c_ref[0] * nb + i, 0)), spec],
            out_specs=(spec, spec)),
        out_shape=(jax.ShapeDtypeStruct(got.shape, F32), jax.ShapeDtypeStruct(got.shape, BF16)),
        compiler_params=_cparams("parallel", "parallel"))(core, g, got)


def _chip_add(part_f32, got, where):
    _, rows, width = part_f32.shape
    nb = 2
    rb = rows // nb

    def kern(w_ref, a_ref, b_ref, o_ref):
        o_ref[...] = ((a_ref[0] + b_ref[0].astype(F32)) + b_ref[1].astype(F32)) + b_ref[2].astype(F32)

    return pl.pallas_call(
        kern, name="rs_chip_add",
        grid_spec=pltpu.PrefetchScalarGridSpec(
            num_scalar_prefetch=1, grid=(nb,),
            in_specs=[pl.BlockSpec((1, rb, width), lambda i, w_ref: (w_ref[0], i, 0)),
                      pl.BlockSpec((N_CHIPS - 1, rb, width), lambda i, w_ref: (0, i, 0))],
            out_specs=pl.BlockSpec((rb, width), lambda i, w_ref: (w_ref[1] * nb + i, 0))),
        out_shape=jax.ShapeDtypeStruct((2 * rows, width), F32),
        compiler_params=_cparams("parallel"))(where, part_f32, got)


def _adamw(w, g, m, v, name):
    rows, width = w.shape
    rb = rows
    for cand in (512, 256, 128, 64, 32, 16, 8):
        if rows % cand == 0 and cand * width * 4 <= ADAM_BLOCK_BYTES:
            rb = cand
            break
    spec = pl.BlockSpec((rb, width), lambda i: (i, 0))

    def kern(w_ref, g_ref, m_ref, v_ref, d_ref, nm_ref, nv_ref):
        d_ref[...], nm_ref[...], nv_ref[...] = _adamw_update(w_ref[...], g_ref[...], m_ref[...], v_ref[...])

    shp = jax.ShapeDtypeStruct(w.shape, F32)
    return pl.pallas_call(
        kern, name=name, grid=(rows // rb,), in_specs=[spec] * 4, out_specs=(spec, spec, spec),
        out_shape=(shp, shp, shp), compiler_params=_cparams("parallel"))(w, g, m, v)


def _adamw_update(w, g, m, v):
    nm = ADAM_B1 * m + (1.0 - ADAM_B1) * g
    nv = ADAM_B2 * v + (1.0 - ADAM_B2) * (g * g)
    m_hat = nm / (1.0 - ADAM_B1 ** ADAM_STEP)
    v_hat = nv / (1.0 - ADAM_B2 ** ADAM_STEP)
    return -ADAM_LR * (m_hat / (jnp.sqrt(v_hat) + ADAM_EPS) + ADAM_WD * w), nm, nv


def _adamw_small(params):
    n = len(params)

    def kern(*refs):
        ins, outs = refs[:4 * n], refs[4 * n:]
        for p in range(n):
            w_ref, g_ref, m_ref, v_ref = ins[4 * p:4 * p + 4]
            d, nm, nv = _adamw_update(w_ref[...], g_ref[...], m_ref[...], v_ref[...])
            outs[3 * p][...] = d
            outs[3 * p + 1][...] = nm
            outs[3 * p + 2][...] = nv

    flat = [a for group in params for a in group]
    shapes = [jax.ShapeDtypeStruct(group[0].shape, F32) for group in params for _ in range(3)]
    res = pl.pallas_call(kern, name="adamw_small", out_shape=tuple(shapes), compiler_params=_cparams())(*flat)
    return [tuple(res[3 * p:3 * p + 3]) for p in range(n)]


def _wada_grad(silu_t, dmod_cols):
    n = dmod_cols.shape[1]

    def kern(s_ref, d_ref, o_ref):
        acc = s_ref[:, 0:1] * d_ref[0:1, :]
        for b in range(1, N_DEV):
            acc = acc + s_ref[:, b:b + 1] * d_ref[b:b + 1, :]
        o_ref[...] = acc

    return pl.pallas_call(kern, name="wada_grad", out_shape=jax.ShapeDtypeStruct((D_MODEL, n), F32),
                          compiler_params=_cparams())(silu_t, dmod_cols)


def _rows(a, multiple):
    flat = a.reshape(-1)
    pad = (-flat.shape[0]) % (D_MODEL * multiple)
    if pad:
        flat = jnp.concatenate([flat, jnp.zeros((pad,), flat.dtype)])
    return flat.reshape(-1, D_MODEL)


def _part_rows(shape, multiple):
    return -(-int(np.prod(shape)) // (D_MODEL * multiple)) * multiple


def _pack_rows(parts, multiple, total_multiple=1):
    blocks = [_rows(p, multiple) for p in parts]
    pad = (-sum(b.shape[0] for b in blocks)) % total_multiple
    if pad:
        blocks.append(jnp.zeros((pad, D_MODEL), blocks[0].dtype))
    return jnp.concatenate(blocks, axis=0)


def _unpack_rows(buf, shapes, multiple):
    out, r = [], 0
    for shp in shapes:
        n = int(np.prod(shp))
        nr = _part_rows(shp, multiple)
        out.append(buf[r:r + nr].reshape(-1)[:n].reshape(shp))
        r += nr
    return out


def kernel(x, c, w_ada, b_ada, norm_pre, norm_post, w_in, pool_w, pool_scale, ssm_a_re, ssm_a_im, ssm_log_dt, ssm_b_re, ssm_b_im, ssm_c_re, ssm_c_im, ssm_d, glu_w, glu_b, w_branch_pool, w_branch_ssm, w_out, loss_target, m_w_ada, m_b_ada, m_norm_pre, m_norm_post, m_w_in, m_pool_w, m_pool_scale, m_ssm_a_re, m_ssm_a_im, m_ssm_log_dt, m_ssm_b_re, m_ssm_b_im, m_ssm_c_re, m_ssm_c_im, m_ssm_d, m_glu_w, m_glu_b, m_w_branch_pool, m_w_branch_ssm, m_w_out, v_w_ada, v_b_ada, v_norm_pre, v_norm_post, v_w_in, v_pool_w, v_pool_scale, v_ssm_a_re, v_ssm_a_im, v_ssm_log_dt, v_ssm_b_re, v_ssm_b_im, v_ssm_c_re, v_ssm_c_im, v_ssm_d, v_glu_w, v_glu_b, v_w_branch_pool, v_w_branch_ssm, v_w_out):
    n_ada = w_ada.shape[2]
    n_in = w_in.shape[2]
    n_row = glu_w.shape[1]
    n_pool = pool_w.shape[2]
    n_groups = pool_w.shape[1]

    (g_ada,) = _run_ride(_ag_weights_ride(w_ada[0].astype(BF16)), "ag_weights")
    w_ada_bf = g_ada.transpose(1, 0, 2).reshape(D_MODEL, N_CHIPS * n_ada)
    w_in_ride = _ag_weights_ride(w_in[0].astype(BF16))

    def unpack_w_in(g_in):
        return g_in.transpose(1, 0, 2).reshape(D_MODEL, N_CHIPS * n_in), [g_in[k] for k in range(N_CHIPS)]
    pool_rows = n_groups * n_pool * POOL_GW // D_MODEL
    late_shards = [pool_w[0].reshape(n_groups * n_pool, POOL_GW), glu_w[0], w_branch_pool[0], w_branch_ssm[0], w_out[0]]
    late_ride = _join_rides([_ag_weights_ride(s.astype(BF16), n_chunks=2) for s in late_shards])

    def unpack_late(pool, *squares):
        pool = pool.reshape(N_CHIPS, n_groups, n_pool, POOL_GW).transpose(1, 0, 2, 3)
        return (pool.reshape(n_groups, POOL_GW, POOL_GW), *[s.reshape(D_MODEL, D_MODEL) for s in squares])

    chip = 2 * lax.axis_index("x") + lax.axis_index("y")
    core = lax.axis_index("c").astype(jnp.int32)
    kept = {}

    def by_cols(a, n):
        return a.reshape(D_MODEL, N_CHIPS, n).transpose(1, 0, 2).reshape(N_CHIPS, -1, D_MODEL)

    def by_rows(a):
        return a.reshape(N_CHIPS, n_row, D_MODEL)

    def exchange_big(g):
        pool_by_chip = g["d_pool_w"].reshape(n_groups, N_CHIPS, n_pool, POOL_GW).transpose(1, 0, 2, 3)
        blocks = [by_cols(g["d_win"], n_in), by_rows(g["d_glu_w"]), by_rows(g["d_wbp"]), by_rows(g["d_wbs"]),
                  by_rows(g["d_wout"]), pool_by_chip.reshape(N_CHIPS, pool_rows, D_MODEL)]
        pad = (-sum(b.shape[1] for b in blocks)) % (2 * COMM_CHUNKS * COMM_ROW_ALIGN)
        if pad:
            blocks.append(jnp.zeros((N_CHIPS, pad, D_MODEL), F32))
        g_packed = jnp.concatenate(blocks, axis=1)
        kept["part_f32"], part_bf = _pair_add(g_packed, _rs_pair(g_packed), core.reshape(1))
        return _rs_chips_ride(part_bf)

    a_re, a_im, log_dt = ssm_a_re[0], ssm_a_im[0], ssm_log_dt[0].reshape(SSM_G, 1)
    b_re_t, b_im_t = ssm_b_re[0].transpose(2, 0, 1), ssm_b_im[0].transpose(2, 0, 1)
    early_names = ["dg2", "d_pscale", "d_glu_b", "d_dskip", "d_abar_re", "d_abar_im", "d_bb_re_t", "d_bb_im_t",
                   "d_c_re", "d_c_im"]

    def exchange_small(s):
        parts = [s[k] for k in early_names]
        kept["early_shapes"] = [p.shape for p in parts]
        return _small_allgather_ride(_pack_rows(parts, SUBLANES, COMM_CHUNKS * COMM_ROW_ALIGN))

    res = _local_step(x[0], c, loss_target[0], w_ada_bf, b_ada, norm_pre, norm_post, None, None, pool_scale,
                      a_re, a_im, log_dt, b_re_t, b_im_t, ssm_c_re[0], ssm_c_im[0], ssm_d[0], None, glu_b[0:1],
                      None, None, None, early_weight=(w_in_ride, unpack_w_in), late_weights=(late_ride, unpack_late),
                      ride_for_dw_in=exchange_small, ride_for_dh=exchange_big)
    loss = lax.psum(res["loss"], ("x", "y", "c"))

    (all_early,) = res["rode_dw_in"]
    (g_norm_post, g_pscale, g_glu_b, g_dskip, s_abar_re, s_abar_im, s_bb_re, s_bb_im, g_c_re, g_c_im) = _unpack_rows(
        _sum_devices(all_early), kept["early_shapes"], SUBLANES)
    g_a_re, g_a_im, g_log_dt, g_b_re_t, g_b_im_t = _ssm_params_bwd(
        a_re, a_im, log_dt, b_re_t, b_im_t, s_abar_re.reshape(SSM_G, SSM_P), s_abar_im.reshape(SSM_G, SSM_P),
        s_bb_re, s_bb_im)
    late_parts = [res["dmod"], res["silu_c"], res["dg1"]]
    late_shapes = [p.shape for p in late_parts]
    head_rows = _part_rows(late_shapes[0], SUBLANES) + _part_rows(late_shapes[1], SUBLANES)
    all_late, sum_late = _small_allgather_sum(_pack_rows(late_parts, SUBLANES, COMM_ROW_ALIGN), head_rows, n_chunks=1)
    g_b_ada, _, g_norm_pre = _unpack_rows(sum_late, late_shapes, SUBLANES)
    dmod_all = all_late[:, 0:3].reshape(N_DEV, 3 * D_MODEL)
    dmod_cols = lax.dynamic_slice_in_dim(dmod_all, chip * n_ada, n_ada, axis=1)
    silu_t = all_late[:, _part_rows(late_shapes[0], SUBLANES)].transpose(1, 0)
    g_w_ada = _wada_grad(silu_t, dmod_cols)

    (got_chips,) = res["rode"]
    shard = _rs_join(_chip_add(kept["part_f32"], got_chips, jnp.stack([chip.astype(jnp.int32), core])))
    r = 0
    g_w_in = shard[r:r + n_in].reshape(D_MODEL, n_in)
    r += n_in
    g_squares = []
    for _ in range(4):
        g_squares.append(shard[r:r + n_row])
        r += n_row
    g_glu_w, g_wbp, g_wbs, g_wout = g_squares
    g_pool_w = shard[r:r + pool_rows].reshape(n_groups * n_pool, POOL_GW)

    big = [("w_ada", w_ada[0], g_w_ada, m_w_ada[0], v_w_ada[0]),
           ("w_in", w_in[0], g_w_in, m_w_in[0], v_w_in[0]),
           ("pool_w", pool_w[0].reshape(n_groups * n_pool, POOL_GW), g_pool_w,
            m_pool_w[0].reshape(n_groups * n_pool, POOL_GW), v_pool_w[0].reshape(n_groups * n_pool, POOL_GW)),
           ("glu_w", glu_w[0], g_glu_w, m_glu_w[0], v_glu_w[0]),
           ("w_branch_pool", w_branch_pool[0], g_wbp, m_w_branch_pool[0], v_w_branch_pool[0]),
           ("w_branch_ssm", w_branch_ssm[0], g_wbs, m_w_branch_ssm[0], v_w_branch_ssm[0]),
           ("w_out", w_out[0], g_wout, m_w_out[0], v_w_out[0])]
    out = {}
    for name, w_, g_, m_, v_ in big:
        d_, nm_, nv_ = _adamw(w_, g_, m_, v_, "adamw_" + name)
        out[name] = (g_, d_, nm_, nv_)

    g_b_re = g_b_re_t.transpose(1, 2, 0)
    g_b_im = g_b_im_t.transpose(1, 2, 0)
    small = [("b_ada", b_ada, g_b_ada, m_b_ada, v_b_ada),
             ("norm_pre", norm_pre, g_norm_pre, m_norm_pre, v_norm_pre),
             ("norm_post", norm_post, g_norm_post, m_norm_post, v_norm_post),
             ("pool_scale", pool_scale, g_pscale, m_pool_scale, v_pool_scale),
             ("ssm_a_re", ssm_a_re, g_a_re, m_ssm_a_re, v_ssm_a_re),
             ("ssm_a_im", ssm_a_im, g_a_im, m_ssm_a_im, v_ssm_a_im),
             ("ssm_log_dt", ssm_log_dt, g_log_dt, m_ssm_log_dt, v_ssm_log_dt),
             ("ssm_b_re", ssm_b_re, g_b_re, m_ssm_b_re, v_ssm_b_re),
             ("ssm_b_im", ssm_b_im, g_b_im, m_ssm_b_im, v_ssm_b_im),
             ("ssm_c_re", ssm_c_re, g_c_re, m_ssm_c_re, v_ssm_c_re),
             ("ssm_c_im", ssm_c_im, g_c_im, m_ssm_c_im, v_ssm_c_im),
             ("ssm_d", ssm_d, g_dskip, m_ssm_d, v_ssm_d),
             ("glu_b", glu_b, g_glu_b, m_glu_b, v_glu_b)]
    small = [(name, w_, g_.reshape(w_.shape), m_, v_) for name, w_, g_, m_, v_ in small]
    updates = _adamw_small([t[1:] for t in small])
    for (name, _, g_, _, _), (d_, nm_, nv_) in zip(small, updates):
        out[name] = (g_, d_, nm_, nv_)

    order = ["w_ada", "b_ada", "norm_pre", "norm_post", "w_in", "pool_w", "pool_scale", "ssm_a_re", "ssm_a_im",
             "ssm_log_dt", "ssm_b_re", "ssm_b_im", "ssm_c_re", "ssm_c_im", "ssm_d", "glu_w", "glu_b", "w_branch_pool",
             "w_branch_ssm", "w_out"]
    ref_shape = dict(w_ada=w_ada.shape, w_in=w_in.shape, pool_w=pool_w.shape, glu_w=glu_w.shape,
                     w_branch_pool=w_branch_pool.shape, w_branch_ssm=w_branch_ssm.shape, w_out=w_out.shape)
    for name, w_, _, _, _ in small:
        ref_shape[name] = w_.shape
    results = [loss, res["grad_x"][None]]
    for k in range(4):
        results += [out[name][k].reshape(ref_shape[name]) for name in order]
    return tuple(results)
```

```python
import functools
import math

import numpy as np
import jax
import jax.numpy as jnp
from jax import lax
from jax.experimental import pallas as pl
from jax.experimental.pallas import tpu as pltpu

F32 = jnp.float32
BF16 = jnp.bfloat16
MESH_ID = pl.DeviceIdType.MESH

D_MODEL = 1024
LANES = 128
SUBLANES = 8
SSM_G, SSM_P, SSM_H = 64, 64, 16
LANE_BLOCKS = D_MODEL // LANES
GROUPS_PER_BLOCK = LANES // SSM_H
STATE_W = GROUPS_PER_BLOCK * SSM_P
STATE_ALL = SSM_G * SSM_P
POOL_WINDOWS = (2, 4, 8, 16)
POOL_GW = D_MODEL // len(POOL_WINDOWS)
HALO = 16
RMS_EPS = 1e-6
N_CHIPS = 4
N_DEV = 8

SCAN_CHUNK = 512
SCAN_BLOCKS = 2
ROW_CHUNK = 256
ROW_CHUNK_WIDE = 512
VMEM_LIMIT_BYTES = 56 * 1024 * 1024

ADAM_BLOCK_BYTES = 1 << 20
ADAM_LR, ADAM_B1, ADAM_B2, ADAM_EPS, ADAM_WD, ADAM_STEP = 0.001, 0.9, 0.999, 1e-08, 0.01, 10

_GELU_C0 = math.sqrt(2.0 / math.pi)
_GELU_C1 = 0.044715


def _cparams(*sem):
    if sem:
        return pltpu.CompilerParams(dimension_semantics=sem, vmem_limit_bytes=VMEM_LIMIT_BYTES)
    return pltpu.CompilerParams(vmem_limit_bytes=VMEM_LIMIT_BYTES)


def _sigmoid(v):
    return jax.nn.sigmoid(v)


def _silu(v):
    return v * _sigmoid(v)


def _dsilu(v):
    s = _sigmoid(v)
    return s * (1.0 + v * (1.0 - s))


def _gelu(v):
    return 0.5 * v * (1.0 + jnp.tanh(_GELU_C0 * (v + _GELU_C1 * v * v * v)))


def _dgelu(v):
    t = jnp.tanh(_GELU_C0 * (v + _GELU_C1 * v * v * v))
    return 0.5 * (1.0 + t) + 0.5 * v * (1.0 - t * t) * _GELU_C0 * (1.0 + 3.0 * _GELU_C1 * v * v)


def _dot(a, b):
    return lax.dot_general(a, b, (((1,), (0,)), ((), ())), preferred_element_type=F32)


def _dot_nt(a, b):
    return lax.dot_general(a, b, (((1,), (1,)), ((), ())), preferred_element_type=F32)


def _dot_tn(a, b):
    return lax.dot_general(a, b, (((0,), (0,)), ((), ())), preferred_element_type=F32)


def _acc8(v):
    return v.reshape(v.shape[0] // SUBLANES, SUBLANES, v.shape[1]).sum(axis=0)


class _Ride:
    def __init__(self, inputs, out_shapes, scratch, start, wait):
        self.inputs, self.out_shapes, self.scratch, self.start, self.wait = inputs, out_shapes, scratch, start, wait


def _mm(a_parts, b_parts, *, name, ta=False, tb=False, out_dtype=F32, bm=512, bn=512, bk=512, ride=None):
    a_parts, b_parts = list(a_parts), list(b_parts)
    if ta:
        assert len(a_parts) == 1
        k_dim, m_dim = a_parts[0].shape
    else:
        m_dim = a_parts[0].shape[0]
        k_dim = sum(a.shape[1] for a in a_parts)
    if tb:
        assert len(b_parts) == 1
        n_dim = b_parts[0].shape[0]
    else:
        n_dim = sum(b.shape[1] for b in b_parts)
    bm, bn, bk = min(bm, m_dim), min(bn, n_dim), min(bk, k_dim)
    nm, nn, nk = m_dim // bm, n_dim // bn, k_dim // bk
    a_ranges, off = [], 0
    for a in a_parts:
        cnt = (a.shape[0] if ta else a.shape[1]) // bk
        a_ranges.append((off, cnt))
        off += cnt
    b_ranges, off = [], 0
    for b in b_parts:
        cnt = (b.shape[0] if tb else b.shape[1]) // bn
        b_ranges.append((off, cnt))
        off += cnt

    def a_spec(off, cnt):
        if ta:
            return pl.BlockSpec((bk, bm), lambda i, n, k: (k, i))
        return pl.BlockSpec((bm, bk), lambda i, n, k: (i, jnp.clip(k - off, 0, cnt - 1)))

    def b_spec(off, cnt):
        if tb:
            return pl.BlockSpec((bn, bk), lambda i, n, k: (n, k))
        return pl.BlockSpec((bk, bn), lambda i, n, k: (k, jnp.clip(n - off, 0, cnt - 1)))

    na, nb = len(a_parts), len(b_parts)
    dims = (((0 if ta else 1,), (1 if tb else 0,)), ((), ()))

    def kern_single(a_ref, b_ref, o_ref):
        o_ref[...] = lax.dot_general(a_ref[...].astype(BF16), b_ref[...].astype(BF16), dims,
                                     preferred_element_type=F32).astype(out_dtype)

    if na == 1 and nb == 1 and nk == 1 and not ride:
        return pl.pallas_call(
            kern_single, name=name, grid=(nm, nn),
            in_specs=[pl.BlockSpec((bk, bm), lambda i, n: (0, i)) if ta else pl.BlockSpec((bm, bk), lambda i, n: (i, 0)),
                      pl.BlockSpec((bn, bk), lambda i, n: (n, 0)) if tb else pl.BlockSpec((bk, bn), lambda i, n: (0, n))],
            out_specs=pl.BlockSpec((bm, bn), lambda i, n: (i, n)),
            out_shape=jax.ShapeDtypeStruct((m_dim, n_dim), out_dtype),
            compiler_params=_cparams("parallel", "parallel"),
        )(a_parts[0], b_parts[0])

    n_rin = len(ride.inputs) if ride else 0
    n_rout = len(ride.out_shapes) if ride else 0

    def kern(*refs):
        a_refs, b_refs = refs[:na], refs[na:na + nb]
        rin = refs[na + nb:na + nb + n_rin]
        o_ref = refs[na + nb + n_rin]
        rout = refs[na + nb + n_rin + 1:na + nb + n_rin + 1 + n_rout]
        acc = refs[na + nb + n_rin + 1 + n_rout]
        rsem = refs[na + nb + n_rin + 2 + n_rout:]
        i, n, k = pl.program_id(0), pl.program_id(1), pl.program_id(2)

        if ride:
            @pl.when((i == 0) & (n == 0) & (k == 0))
            def _():
                ride.start(rin, rout, rsem)

        if nk > 1:
            @pl.when(k == 0)
            def _():
                acc[...] = jnp.zeros_like(acc)

        for ja, (koff, kcnt) in enumerate(a_ranges):
            for jb, (noff, ncnt) in enumerate(b_ranges):
                def step(ja=ja, jb=jb):
                    a = a_refs[ja][...].astype(BF16)
                    b = b_refs[jb][...].astype(BF16)
                    prod = lax.dot_general(a, b, dims, preferred_element_type=F32)
                    if nk > 1:
                        acc[...] += prod
                    else:
                        o_ref[...] = prod.astype(out_dtype)

                if na == 1 and nb == 1:
                    step()
                else:
                    cond = (k >= koff) & (k < koff + kcnt) & (n >= noff) & (n < noff + ncnt)
                    pl.when(cond)(step)

        if nk > 1:
            @pl.when(k == nk - 1)
            def _():
                o_ref[...] = acc[...].astype(out_dtype)

        if ride:
            @pl.when((i == nm - 1) & (n == nn - 1) & (k == nk - 1))
            def _():
                ride.wait(rin, rout, rsem)

    any_spec = pl.BlockSpec(memory_space=pl.ANY)
    out_spec = pl.BlockSpec((bm, bn), lambda i, n, k: (i, n))
    out_shape = jax.ShapeDtypeStruct((m_dim, n_dim), out_dtype)
    acc_shape = pltpu.VMEM((bm, bn) if nk > 1 else (SUBLANES, LANES), F32)
    if not ride:
        return pl.pallas_call(
            kern, name=name, grid=(nm, nn, nk),
            in_specs=[a_spec(*r) for r in a_ranges] + [b_spec(*r) for r in b_ranges],
            out_specs=out_spec, out_shape=out_shape, scratch_shapes=[acc_shape],
            compiler_params=_cparams("parallel", "parallel", "arbitrary"),
        )(*a_parts, *b_parts)
    return pl.pallas_call(
        kern, name=name, grid=(nm, nn, nk),
        in_specs=[a_spec(*r) for r in a_ranges] + [b_spec(*r) for r in b_ranges] + [any_spec] * n_rin,
        out_specs=(out_spec,) + (any_spec,) * n_rout, out_shape=(out_shape,) + tuple(ride.out_shapes),
        scratch_shapes=[acc_shape] + list(ride.scratch),
        compiler_params=_cparams("arbitrary", "arbitrary", "arbitrary"),
    )(*a_parts, *b_parts, *ride.inputs)


def _ssm_param_fn(a_re, a_im, log_dt, b_re, b_im):
    dt = jnp.exp(log_dt)
    lam_re = jnp.minimum(a_re, -1e-4)
    lam_im = a_im
    mag = jnp.exp(lam_re * dt)
    abar_re = mag * jnp.cos(lam_im * dt)
    abar_im = mag * jnp.sin(lam_im * dt)
    den = lam_re * lam_re + lam_im * lam_im
    num_re = abar_re - 1.0
    f_re = (num_re * lam_re + abar_im * lam_im) / den
    f_im = (abar_im * lam_re - num_re * lam_im) / den
    bb_re = f_re * b_re - f_im * b_im
    bb_im = f_re * b_im + f_im * b_re
    return abar_re, abar_im, bb_re, bb_im


def _ssm_params(a_re, a_im, log_dt, b_re_t, b_im_t):
    def kern(are, aim, ldt, bre, bim, o_ar, o_ai, o_br, o_bi):
        ar, ai, br, bi = _ssm_param_fn(are[...], aim[...], ldt[...], bre[...], bim[...])
        o_ar[...] = ar
        o_ai[...] = ai
        o_br[...] = br
        o_bi[...] = bi

    gp = jax.ShapeDtypeStruct((SSM_G, SSM_P), F32)
    hgp = jax.ShapeDtypeStruct((SSM_H, SSM_G, SSM_P), F32)
    return pl.pallas_call(kern, name="ssm_params", out_shape=(gp, gp, hgp, hgp), compiler_params=_cparams())(
        a_re, a_im, log_dt, b_re_t, b_im_t)


def _ssm_params_bwd(a_re, a_im, log_dt, b_re_t, b_im_t, d_ar, d_ai, d_bbr, d_bbi):
    def kern(are, aim, ldt, bre, bim, dar, dai, dbr, dbi, o_are, o_aim, o_ldt, o_bre, o_bim):
        prim = (are[...], aim[...], ldt[...], bre[...], bim[...])
        _, vjp = jax.vjp(_ssm_param_fn, *prim)
        g = vjp((dar[...], dai[...], dbr[...], dbi[...]))
        o_are[...] = g[0]
        o_aim[...] = g[1]
        o_ldt[...] = g[2]
        o_bre[...] = g[3]
        o_bim[...] = g[4]

    gp = jax.ShapeDtypeStruct((SSM_G, SSM_P), F32)
    g1 = jax.ShapeDtypeStruct((SSM_G, 1), F32)
    hgp = jax.ShapeDtypeStruct((SSM_H, SSM_G, SSM_P), F32)
    return pl.pallas_call(kern, name="ssm_params_bwd", out_shape=(gp, gp, g1, hgp, hgp), compiler_params=_cparams())(
        a_re, a_im, log_dt, b_re_t, b_im_t, d_ar, d_ai, d_bbr, d_bbi)


def _pow_tables(abar_re, abar_im, tc):
    ls = tc // SUBLANES

    def kern(ar_ref, ai_ref, fr_ref, fi_ref, rr_ref, ri_ref):
        a_re = jnp.broadcast_to(ar_ref[...], (SUBLANES, STATE_W))
        a_im = jnp.broadcast_to(ai_ref[...], (SUBLANES, STATE_W))
        p_re, p_im = a_re, a_im
        for i in range(ls):
            fwd = pl.ds(SUBLANES * i, SUBLANES)
            rev = pl.ds(SUBLANES * (ls - 1 - i), SUBLANES)
            fr_ref[fwd, :] = p_re
            fi_ref[fwd, :] = p_im
            rr_ref[rev, :] = p_re
            ri_ref[rev, :] = p_im
            p_re, p_im = p_re * a_re - p_im * a_im, p_re * a_im + p_im * a_re

    vec = pl.BlockSpec((1, STATE_W), lambda b: (0, b))
    tab = pl.BlockSpec((tc, STATE_W), lambda b: (0, b))
    shp = jax.ShapeDtypeStruct((tc, STATE_ALL), F32)
    return pl.pallas_call(
        kern, name="pow_tables", grid=(LANE_BLOCKS,), in_specs=[vec, vec], out_specs=(tab, tab, tab, tab),
        out_shape=(shp, shp, shp, shp), compiler_params=_cparams("parallel"))(abar_re, abar_im)


def _mod_kernel(c_row, w_ada_bf, b_ada):
    def kern(c_ref, w_ref, b_ref, m_ref, s_ref):
        cv = c_ref[...]
        sc = _silu(cv)
        s_ref[...] = sc
        lhs = jnp.broadcast_to(sc, (SUBLANES, D_MODEL)).astype(BF16)
        m_ref[...] = _dot(lhs, w_ref[...]) + b_ref[...]

    return pl.pallas_call(
        kern, name="ada_mod",
        out_shape=(jax.ShapeDtypeStruct((SUBLANES, 3 * D_MODEL), F32), jax.ShapeDtypeStruct((1, D_MODEL), F32)),
        compiler_params=_cparams())(c_row, w_ada_bf, b_ada)


def _row_spec(tr, width=D_MODEL, col=0):
    return pl.BlockSpec((tr, width), lambda c: (c, col))


def _vec_spec(width=D_MODEL):
    return pl.BlockSpec((1, width), lambda c: (0, 0))


def _col_spec(tr):
    return pl.BlockSpec((D_MODEL, tr), lambda c: (0, c))


def _in_norm(x, g1, scale, shift, ride=None):
    seq = x.shape[0]
    tr = min(ROW_CHUNK_WIDE, seq)
    nc = seq // tr
    n_rin = len(ride.inputs) if ride else 0
    n_rout = len(ride.out_shapes) if ride else 0

    def kern(x_ref, g_ref, sc_ref, sh_ref, *rest):
        rin, (h_ref, ht_ref) = rest[:n_rin], rest[n_rin:n_rin + 2]
        rout, rsem = rest[n_rin + 2:n_rin + 2 + n_rout], rest[n_rin + 2 + n_rout:]
        c = pl.program_id(0)
        if ride:
            @pl.when(c == 0)
            def _():
                ride.start(rin, rout, rsem)

        xv = x_ref[...]
        r = lax.rsqrt(jnp.mean(xv * xv, axis=-1, keepdims=True) + RMS_EPS)
        h = ((xv * r) * g_ref[...]) * (1.0 + sc_ref[...]) + sh_ref[...]
        h_ref[...] = h.astype(BF16)
        ht_ref[...] = h.T.astype(BF16)

        if ride:
            @pl.when(c == nc - 1)
            def _():
                ride.wait(rin, rout, rsem)

    outs = pl.pallas_call(
        kern, name="in_norm", grid=(nc,),
        in_specs=[_row_spec(tr), _vec_spec(), _vec_spec(), _vec_spec()] + [_ANY] * n_rin,
        out_specs=(_row_spec(tr), _col_spec(tr)) + (_ANY,) * n_rout,
        out_shape=(jax.ShapeDtypeStruct((seq, D_MODEL), BF16), jax.ShapeDtypeStruct((D_MODEL, seq), BF16))
        + tuple(ride.out_shapes if ride else ()),
        scratch_shapes=list(ride.scratch) if ride else [],
        compiler_params=_cparams("arbitrary" if ride else "parallel"))(x, g1, scale, shift, *(ride.inputs if ride else ()))
    return outs


def _pool_windows(ext, pos, g, w, tr):
    cols = pl.ds(g * POOL_GW, POOL_GW)
    cur = ext[pl.ds(HALO, tr), cols]
    acc = cur
    for k in range(1, w):
        acc = acc + ext[pl.ds(HALO - k, tr), cols]
    cnt = jnp.minimum(pos + 1, w).astype(F32)
    return acc / cnt - cur


def _pool_fwd(proj, pool_w_bf, pscale):
    seq = proj.shape[0]
    tr = min(ROW_CHUNK_WIDE, seq)
    hb = tr // HALO

    def kern(up_ref, halo_ref, zp_ref, pw_ref, ps_ref, y_ref, yt_ref, ext):
        c = pl.program_id(0)
        ext[0:HALO, :] = jnp.where(c > 0, halo_ref[...].astype(F32), 0.0)
        ext[HALO:, :] = up_ref[...].astype(F32)
        pos = c * tr + lax.broadcasted_iota(jnp.int32, (tr, POOL_GW), 0)
        for g, w in enumerate(POOL_WINDOWS):
            cols = pl.ds(g * POOL_GW, POOL_GW)
            pooled = _pool_windows(ext, pos, g, w, tr)
            mixed = _dot(pooled.astype(BF16), pw_ref[g])
            y = mixed * ps_ref[:, cols] * _silu(zp_ref[:, cols].astype(F32))
            y_ref[:, cols] = y.astype(BF16)
            yt_ref[cols, :] = y.T.astype(BF16)

    return pl.pallas_call(
        kern, name="pool_fwd", grid=(seq // tr,),
        in_specs=[_row_spec(tr, col=0),
                  pl.BlockSpec((HALO, D_MODEL), lambda c: (jnp.maximum(c * hb - 1, 0), 0)),
                  _row_spec(tr, col=1),
                  pl.BlockSpec((len(POOL_WINDOWS), POOL_GW, POOL_GW), lambda c: (0, 0, 0)),
                  _vec_spec()],
        out_specs=(_row_spec(tr), _col_spec(tr)),
        out_shape=(jax.ShapeDtypeStruct((seq, D_MODEL), BF16), jax.ShapeDtypeStruct((D_MODEL, seq), BF16)),
        scratch_shapes=[pltpu.VMEM((tr + HALO, D_MODEL), F32)],
        compiler_params=_cparams("parallel"))(proj, proj, proj, pool_w_bf, pscale)


def _pool_bwd(proj, dyp, pool_w_bf, pscale, dproj):
    seq = proj.shape[0]
    tr = min(ROW_CHUNK_WIDE, seq)
    hb = tr // HALO
    nc = seq // tr
    n_halo = seq // HALO

    def kern(up_ref, halo_ref, zp_ref, zpn_ref, dyp_ref, dypn_ref, pw_ref, ps_ref, _,
             d01_ref, dpw_ref, dps_ref, ext, dpn, acc_pw, acc_ps):
        c = pl.program_id(0)

        @pl.when(c == 0)
        def _():
            acc_pw[...] = jnp.zeros_like(acc_pw)
            acc_ps[...] = jnp.zeros_like(acc_ps)

        ext[0:HALO, :] = jnp.where(c > 0, halo_ref[...].astype(F32), 0.0)
        ext[HALO:, :] = up_ref[...].astype(F32)
        pos = c * tr + lax.broadcasted_iota(jnp.int32, (tr, POOL_GW), 0)
        pos_n = (c + 1) * tr + lax.broadcasted_iota(jnp.int32, (HALO, POOL_GW), 0)
        has_next = c < nc - 1
        for g, w in enumerate(POOL_WINDOWS):
            cols = pl.ds(g * POOL_GW, POOL_GW)
            pooled_bf = _pool_windows(ext, pos, g, w, tr).astype(BF16)
            wg = pw_ref[g]
            mixed = _dot(pooled_bf, wg)
            zp = zp_ref[:, cols].astype(F32)
            sz = _silu(zp)
            dyp_g = dyp_ref[:, cols].astype(F32)
            ps = ps_ref[:, cols]
            dmixed = (dyp_g * ps * sz).astype(BF16)
            acc_ps[:, cols] += _acc8(dyp_g * mixed * sz)
            d01_ref[:, pl.ds(D_MODEL + g * POOL_GW, POOL_GW)] = (dyp_g * mixed * ps * _dsilu(zp)).astype(BF16)
            acc_pw[g] += _dot_tn(pooled_bf, dmixed)
            dpooled = _dot_nt(dmixed, wg)
            dmixed_n = (jnp.where(has_next, dypn_ref[:, cols].astype(F32), 0.0) * ps * _silu(zpn_ref[:, cols].astype(F32))).astype(BF16)
            dpooled_n = _dot_nt(dmixed_n, wg)
            dpn[0:tr, :] = dpooled / jnp.minimum(pos + 1, w).astype(F32)
            dpn[tr:, :] = dpooled_n / jnp.minimum(pos_n + 1, w).astype(F32)
            acc = dpn[0:tr, :]
            for k in range(1, w):
                acc = acc + dpn[pl.ds(k, tr), :]
            d01_ref[:, cols] = (acc - dpooled).astype(BF16)

        @pl.when(c == nc - 1)
        def _():
            dpw_ref[...] = acc_pw[...]
            dps_ref[...] = jnp.sum(acc_ps[...], axis=0, keepdims=True)

    nxt = lambda c: (jnp.minimum((c + 1) * hb, n_halo - 1), 0)
    nxt1 = lambda c: (jnp.minimum((c + 1) * hb, n_halo - 1), 1)
    return pl.pallas_call(
        kern, name="pool_bwd", grid=(nc,),
        in_specs=[_row_spec(tr, col=0),
                  pl.BlockSpec((HALO, D_MODEL), lambda c: (jnp.maximum(c * hb - 1, 0), 0)),
                  _row_spec(tr, col=1),
                  pl.BlockSpec((HALO, D_MODEL), nxt1),
                  _row_spec(tr),
                  pl.BlockSpec((HALO, D_MODEL), nxt),
                  pl.BlockSpec((len(POOL_WINDOWS), POOL_GW, POOL_GW), lambda c: (0, 0, 0)),
                  _vec_spec(), _ANY],
        out_specs=(pl.BlockSpec((tr, 2 * D_MODEL), lambda c: (c, 0)),
                   pl.BlockSpec((len(POOL_WINDOWS), POOL_GW, POOL_GW), lambda c: (0, 0, 0)),
                   _vec_spec()),
        out_shape=(jax.ShapeDtypeStruct(dproj.shape, BF16),
                   jax.ShapeDtypeStruct((len(POOL_WINDOWS), POOL_GW, POOL_GW), F32),
                   jax.ShapeDtypeStruct((1, D_MODEL), F32)),
        scratch_shapes=[pltpu.VMEM((tr + HALO, D_MODEL), F32), pltpu.VMEM((tr + HALO, POOL_GW), F32),
                        pltpu.VMEM((len(POOL_WINDOWS), POOL_GW, POOL_GW), F32), pltpu.VMEM((SUBLANES, D_MODEL), F32)],
        input_output_aliases={8: 0},
        compiler_params=_cparams("arbitrary"))(proj, proj, proj, proj, dyp, dyp, pool_w_bf, pscale, dproj)


def _glu_fwd(ys, proj, glu_w_bf, glu_b):
    seq = ys.shape[0]
    tr = min(ROW_CHUNK_WIDE, seq)

    def kern(ys_ref, zs_ref, w_ref, b_ref, o_ref, ot_ref):
        yg = _gelu(ys_ref[...])
        q = _dot(yg.astype(BF16), w_ref[...]) + b_ref[...]
        y = yg * _sigmoid(q) * _silu(zs_ref[...].astype(F32))
        o_ref[...] = y.astype(BF16)
        ot_ref[...] = y.T.astype(BF16)

    return pl.pallas_call(
        kern, name="glu_fwd", grid=(seq // tr,),
        in_specs=[_row_spec(tr), _row_spec(tr, col=3), pl.BlockSpec((D_MODEL, D_MODEL), lambda c: (0, 0)), _vec_spec()],
        out_specs=(_row_spec(tr), _col_spec(tr)),
        out_shape=(jax.ShapeDtypeStruct((seq, D_MODEL), BF16), jax.ShapeDtypeStruct((D_MODEL, seq), BF16)),
        compiler_params=_cparams("parallel"))(ys, proj, glu_w_bf, glu_b)


def _glu_bwd(ys, proj, dyssm, glu_w_bf, glu_b, dproj):
    seq = ys.shape[0]
    tr = min(ROW_CHUNK_WIDE, seq)
    nc = seq // tr

    def kern(ys_ref, zs_ref, dy_ref, w_ref, b_ref, _, dys_ref, dzs_ref, dq_ref, yg_ref, db_ref, acc_b):
        c = pl.program_id(0)

        @pl.when(c == 0)
        def _():
            acc_b[...] = jnp.zeros_like(acc_b)

        ysv = ys_ref[...]
        yg = _gelu(ysv)
        yg_bf = yg.astype(BF16)
        q = _dot(yg_bf, w_ref[...]) + b_ref[...]
        sg = _sigmoid(q)
        zs = zs_ref[...].astype(F32)
        dyv = dy_ref[...].astype(F32)
        dyglu = dyv * _silu(zs)
        dzs_ref[...] = (dyv * (yg * sg) * _dsilu(zs)).astype(BF16)
        dq = dyglu * yg * sg * (1.0 - sg)
        dq_bf = dq.astype(BF16)
        acc_b[...] += _acc8(dq)
        dyg = dyglu * sg + _dot_nt(dq_bf, w_ref[...])
        dys_ref[...] = dyg * _dgelu(ysv)
        dq_ref[...] = dq_bf
        yg_ref[...] = yg.T.astype(BF16)

        @pl.when(c == nc - 1)
        def _():
            db_ref[...] = jnp.sum(acc_b[...], axis=0, keepdims=True)

    bf = jax.ShapeDtypeStruct((seq, D_MODEL), BF16)
    return pl.pallas_call(
        kern, name="glu_bwd", grid=(nc,),
        in_specs=[_row_spec(tr), _row_spec(tr, col=3), _row_spec(tr),
                  pl.BlockSpec((D_MODEL, D_MODEL), lambda c: (0, 0)), _vec_spec(), _ANY],
        out_specs=(_row_spec(tr), _row_spec(tr, col=3), _row_spec(tr), _col_spec(tr), _vec_spec()),
        out_shape=(jax.ShapeDtypeStruct((seq, D_MODEL), F32), jax.ShapeDtypeStruct(dproj.shape, BF16), bf,
                   jax.ShapeDtypeStruct((D_MODEL, seq), BF16), jax.ShapeDtypeStruct((1, D_MODEL), F32)),
        scratch_shapes=[pltpu.VMEM((SUBLANES, D_MODEL), F32)],
        input_output_aliases={5: 1},
        compiler_params=_cparams("arbitrary"))(ys, proj, dyssm, glu_w_bf, glu_b, dproj)


def _out_fwd_bwd(ypool, yssm, proj, x, tgt, gate, g2, wbp_bf, wbs_bf, wout_bf):
    seq = x.shape[0]
    tr = min(ROW_CHUNK, seq)
    nc = seq // tr

    def kern(yp_ref, ysm_ref, gp_ref, gs_ref, x_ref, t_ref, gate_ref, g2_ref, wbp_ref, wbs_ref, wo_ref,
             dy_ref, dyp_ref, dys_ref, d45_ref, mb_ref, dob_ref, dbp_ref, dbs_ref, loss_ref, dgate_ref, dg2_ref,
             acc_l, acc_gate, acc_g2):
        c = pl.program_id(0)

        @pl.when(c == 0)
        def _():
            acc_l[...] = jnp.zeros_like(acc_l)
            acc_gate[...] = jnp.zeros_like(acc_gate)
            acc_g2[...] = jnp.zeros_like(acc_g2)

        bp = _dot(yp_ref[...], wbp_ref[...])
        bs = _dot(ysm_ref[...], wbs_ref[...])
        sp = _sigmoid(gp_ref[...].astype(F32))
        ss = _sigmoid(gs_ref[...].astype(F32))
        merged = sp * bp + ss * bs
        mb = merged.astype(BF16)
        out = _dot(mb, wo_ref[...])
        r2 = lax.rsqrt(jnp.mean(out * out, axis=-1, keepdims=True) + RMS_EPS)
        oh = out * r2
        gate_v, g2_v = gate_ref[...], g2_ref[...]
        ohg = oh * g2_v
        diff = (x_ref[...] + gate_v * ohg) - t_ref[...]
        acc_l[...] += _acc8(diff * diff)
        dyv = diff * (1.0 / D_MODEL)
        dy_ref[...] = dyv
        acc_gate[...] += _acc8(dyv * ohg)
        t = dyv * gate_v
        acc_g2[...] += _acc8(t * oh)
        doh = t * g2_v
        dout = r2 * (doh - oh * jnp.mean(doh * oh, axis=-1, keepdims=True))
        dob = dout.astype(BF16)
        dmerged = _dot_nt(dob, wo_ref[...])
        dbp = (dmerged * sp).astype(BF16)
        dbs = (dmerged * ss).astype(BF16)
        d45_ref[:, 0:D_MODEL] = (dmerged * bp * sp * (1.0 - sp)).astype(BF16)
        d45_ref[:, D_MODEL:] = (dmerged * bs * ss * (1.0 - ss)).astype(BF16)
        dyp_ref[...] = _dot_nt(dbp, wbp_ref[...]).astype(BF16)
        dys_ref[...] = _dot_nt(dbs, wbs_ref[...]).astype(BF16)
        mb_ref[...] = merged.T.astype(BF16)
        dob_ref[...] = dob
        dbp_ref[...] = dbp
        dbs_ref[...] = dbs

        @pl.when(c == nc - 1)
        def _():
            tot = jnp.sum(acc_l[...], axis=0, keepdims=True)
            loss_ref[...] = jnp.sum(tot, axis=1, keepdims=True) * (0.5 / D_MODEL)
            dgate_ref[...] = jnp.sum(acc_gate[...], axis=0, keepdims=True)
            dg2_ref[...] = jnp.sum(acc_g2[...], axis=0, keepdims=True)

    wspec = pl.BlockSpec((D_MODEL, D_MODEL), lambda c: (0, 0))
    f32 = jax.ShapeDtypeStruct((seq, D_MODEL), F32)
    bf = jax.ShapeDtypeStruct((seq, D_MODEL), BF16)
    vec = jax.ShapeDtypeStruct((1, D_MODEL), F32)
    acc = pltpu.VMEM((SUBLANES, D_MODEL), F32)
    return pl.pallas_call(
        kern, name="out_fwd_bwd", grid=(nc,),
        in_specs=[_row_spec(tr), _row_spec(tr), _row_spec(tr, col=4), _row_spec(tr, col=5), _row_spec(tr), _row_spec(tr),
                  _vec_spec(), _vec_spec(), wspec, wspec, wspec],
        out_specs=(_row_spec(tr), _row_spec(tr), _row_spec(tr), pl.BlockSpec((tr, 2 * D_MODEL), lambda c: (c, 2)),
                   _col_spec(tr), _row_spec(tr), _row_spec(tr), _row_spec(tr),
                   pl.BlockSpec((1, 1), lambda c: (0, 0)), _vec_spec(), _vec_spec()),
        out_shape=(f32, bf, bf, jax.ShapeDtypeStruct((seq, proj.shape[1]), BF16),
                   jax.ShapeDtypeStruct((D_MODEL, seq), BF16), bf, bf, bf,
                   jax.ShapeDtypeStruct((1, 1), F32), vec, vec),
        scratch_shapes=[acc, acc, acc],
        compiler_params=_cparams("arbitrary"))(ypool, yssm, proj, proj, x, tgt, gate, g2, wbp_bf, wbs_bf, wout_bf)


def _in_bwd(dh, x, dy, g1, scale):
    seq = x.shape[0]
    tr = min(ROW_CHUNK_WIDE, seq)
    nc = seq // tr

    def kern(dh_ref, x_ref, dy_ref, g_ref, sc_ref, dx_ref, dsh_ref, dsc_ref, dg_ref, a_sh, a_sc, a_g):
        c = pl.program_id(0)

        @pl.when(c == 0)
        def _():
            a_sh[...] = jnp.zeros_like(a_sh)
            a_sc[...] = jnp.zeros_like(a_sc)
            a_g[...] = jnp.zeros_like(a_g)

        xv = x_ref[...]
        r = lax.rsqrt(jnp.mean(xv * xv, axis=-1, keepdims=True) + RMS_EPS)
        xh = xv * r
        g = g_ref[...]
        dhv = dh_ref[...]
        a_sh[...] += _acc8(dhv)
        a_sc[...] += _acc8(dhv * (xh * g))
        dn = dhv * (1.0 + sc_ref[...])
        a_g[...] += _acc8(dn * xh)
        dxh = dn * g
        dx_ref[...] = dy_ref[...] + r * (dxh - xh * jnp.mean(dxh * xh, axis=-1, keepdims=True))

        @pl.when(c == nc - 1)
        def _():
            dsh_ref[...] = jnp.sum(a_sh[...], axis=0, keepdims=True)
            dsc_ref[...] = jnp.sum(a_sc[...], axis=0, keepdims=True)
            dg_ref[...] = jnp.sum(a_g[...], axis=0, keepdims=True)

    vec = jax.ShapeDtypeStruct((1, D_MODEL), F32)
    acc = pltpu.VMEM((SUBLANES, D_MODEL), F32)
    return pl.pallas_call(
        kern, name="in_bwd", grid=(nc,),
        in_specs=[_row_spec(tr), _row_spec(tr), _row_spec(tr), _vec_spec(), _vec_spec()],
        out_specs=(_row_spec(tr), _vec_spec(), _vec_spec(), _vec_spec()),
        out_shape=(jax.ShapeDtypeStruct((seq, D_MODEL), F32), vec, vec, vec),
        scratch_shapes=[acc, acc, acc],
        compiler_params=_cparams("arbitrary"))(dh, x, dy, g1, scale)


SLAB = 2 * SUBLANES


def _local_scan(a_re, a_im, br, bi, xr, xi, row0, ls, reverse, init=None, xb=None):
    if init is None:
        x_re = jnp.zeros((SUBLANES, STATE_W), F32)
        x_im = jnp.zeros((SUBLANES, STATE_W), F32)
    else:
        x_re, x_im = init
    for i in (range(ls - 1, -1, -1) if reverse else range(ls)):
        src = pl.ds(SUBLANES * i, SUBLANES)
        dst = pl.ds(row0 + SUBLANES * i, SUBLANES)
        n_re = a_re * x_re - a_im * x_im + br[src, :]
        n_im = a_re * x_im + a_im * x_re + bi[src, :]
        if xb is not None and i % 2 == 1:
            pair = pl.ds(SUBLANES * (i - 1), SLAB)
            xb[0][pair, :] = jnp.concatenate([x_re, n_re], axis=0).astype(BF16)
            xb[1][pair, :] = jnp.concatenate([x_im, n_im], axis=0).astype(BF16)
        x_re, x_im = n_re, n_im
        xr[dst, :] = x_re
        xi[dst, :] = x_im
    return x_re, x_im


def _two(v):
    return jnp.concatenate([v, v], axis=0)


def _unpermute_rhs(v, sel):
    hi = v.astype(BF16)
    r1 = v - hi.astype(F32)
    mid = r1.astype(BF16)
    lo = (r1 - mid.astype(F32)).astype(BF16)
    return _dot(hi, sel) + _dot(mid, sel) + _dot(lo, sel)


def _scan_specs(tc, nb, rows_of):
    return dict(
        us=pl.BlockSpec((tc, nb * LANES), lambda b, c: (rows_of(c), 2 * D_MODEL // (nb * LANES) + b)),
        tok=pl.BlockSpec((tc, nb * LANES), lambda b, c: (rows_of(c), b)),
        bblk=pl.BlockSpec((nb, LANES, STATE_W), lambda b, c: (b, 0, 0)),
        cblk=pl.BlockSpec((nb, STATE_W, LANES), lambda b, c: (b, 0, 0)),
        vec=pl.BlockSpec((1, nb * STATE_W), lambda b, c: (0, b)),
        tab=pl.BlockSpec((tc, nb * STATE_W), lambda b, c: (0, b)),
        car=pl.BlockSpec((SUBLANES, nb * STATE_W), lambda b, c: (rows_of(c), b)),
        dvec=pl.BlockSpec((1, nb * LANES), lambda b, c: (0, b)))


def _ssm_scan_fwd(proj, bb_re, bb_im, cm_re, cm_im, abar_re, abar_im, pw_re, pw_im, d_skip, tc):
    seq = proj.shape[0]
    nc = seq // tc
    ls = tc // SUBLANES
    nb = SCAN_BLOCKS

    def kern(us_ref, bbr_ref, bbi_ref, cmr_ref, cmi_ref, ar_ref, ai_ref, pwr_ref, pwi_ref, d_ref,
             ys_ref, ecr_ref, eci_ref, bur, bui, car_r, car_i, end_r, end_i, upb, xb_r, xb_i, *nat):
        c = pl.program_id(1)

        @pl.when(c == 0)
        def _():
            car_r[...] = jnp.zeros_like(car_r)
            car_i[...] = jnp.zeros_like(car_i)

        for j in range(nb):
            cols = pl.ds(j * LANES, LANES)
            scols = pl.ds(j * STATE_W, STATE_W)
            nat[j][...] = us_ref[:, cols].astype(F32)
            for i in range(ls):
                upb[j, pl.ds(SUBLANES * i, SUBLANES), :] = nat[j][pl.ds(i, SUBLANES, stride=ls), :]
            u = upb[j]
            up = u.astype(BF16)
            bur[j] = _dot(up, bbr_ref[j])
            bui[j] = _dot(up, bbi_ref[j])
            a_re = jnp.broadcast_to(ar_ref[:, scols], (SUBLANES, STATE_W))
            a_im = jnp.broadcast_to(ai_ref[:, scols], (SUBLANES, STATE_W))
            x_re, x_im = _local_scan(a_re, a_im, bur.at[j], bui.at[j], bur.at[j], bui.at[j], 0, ls, False)
            end_r[j] = x_re
            end_i[j] = x_im
            big_re = pwr_ref[tc - 1:tc, scols]
            big_im = pwi_ref[tc - 1:tc, scols]
            e_re = car_r[j, 0:1, :]
            e_im = car_i[j, 0:1, :]
            for s in range(SUBLANES):
                n_re = end_r[j, s:s + 1, :] + big_re * e_re - big_im * e_im
                n_im = end_i[j, s:s + 1, :] + big_re * e_im + big_im * e_re
                e_re, e_im = n_re, n_im
                if s < SUBLANES - 1:
                    car_r[j, s + 1:s + 2, :] = e_re
                    car_i[j, s + 1:s + 2, :] = e_im
            ec_re = car_r[j]
            ec_im = car_i[j]
            ecr_ref[:, scols] = ec_re
            eci_ref[:, scols] = ec_im
            e2_re, e2_im = _two(ec_re), _two(ec_im)
            for k in range(tc // SLAB):
                rows_k = pl.ds(SLAB * k, SLAB)
                p_re = pwr_ref[rows_k, scols]
                p_im = pwi_ref[rows_k, scols]
                xb_r[j, rows_k, :] = (bur[j, rows_k, :] + p_re * e2_re - p_im * e2_im).astype(BF16)
                xb_i[j, rows_k, :] = (bui[j, rows_k, :] + p_re * e2_im + p_im * e2_re).astype(BF16)
            upb[j] = _dot(xb_r[j], cmr_ref[j]) - _dot(xb_i[j], cmi_ref[j]) + d_ref[:, cols] * u
            for i in range(ls):
                nat[j][pl.ds(i, SUBLANES, stride=ls), :] = upb[j, pl.ds(SUBLANES * i, SUBLANES), :]
            ys_ref[:, cols] = nat[j][...]
            car_r[j, 0:1, :] = e_re
            car_i[j, 0:1, :] = e_im

    sp = _scan_specs(tc, nb, lambda c: c)
    carry_shape = jax.ShapeDtypeStruct((nc * SUBLANES, STATE_ALL), F32)
    small = pltpu.VMEM((nb, SUBLANES, STATE_W), F32)
    big = pltpu.VMEM((nb, tc, STATE_W), F32)
    return pl.pallas_call(
        kern, name="ssm_scan_fwd", grid=(LANE_BLOCKS // nb, nc),
        in_specs=[sp["us"], sp["bblk"], sp["bblk"], sp["cblk"], sp["cblk"], sp["vec"], sp["vec"], sp["tab"], sp["tab"],
                  sp["dvec"]],
        out_specs=(sp["tok"], sp["car"], sp["car"]),
        out_shape=(jax.ShapeDtypeStruct((seq, D_MODEL), F32), carry_shape, carry_shape),
        scratch_shapes=[big, big, small, small, small, small, pltpu.VMEM((nb, tc, LANES), F32),
                        pltpu.VMEM((nb, tc, STATE_W), BF16), pltpu.VMEM((nb, tc, STATE_W), BF16)]
        + [pltpu.VMEM((tc, LANES), F32)] * nb,
        compiler_params=_cparams("parallel", "arbitrary"),
    )(proj, bb_re, bb_im, cm_re, cm_im, abar_re, abar_im, pw_re, pw_im, d_skip)


def _ssm_scan_bwd(proj, dys, ec_re, ec_im, bb_re, bb_im, cm_re, cm_im, abar_re, abar_im,
                  pw_re, pw_im, pv_re, pv_im, d_skip, dproj, tc):
    seq = proj.shape[0]
    nc = seq // tc
    ls = tc // SUBLANES
    nb = SCAN_BLOCKS

    def kern(us_ref, dys_ref, ecr_ref, eci_ref, bbr_ref, bbi_ref, cmr_ref, cmi_ref, ar_ref, ai_ref,
             pwr_ref, pwi_ref, pvr_ref, pvi_ref, d_ref, _,
             dus_ref, dbbr_ref, dbbi_ref, dcmr_ref, dcmi_ref, dar_ref, dai_ref, dd_ref,
             bur, bui, xr, xi, gr, gi, fc_r, fc_i, a_bbr, a_bbi, a_cmr, a_cmi, a_ar, a_ai, a_dd, upb, dpb, hb_r, hb_i,
             *nat):
        c = pl.program_id(1)

        @pl.when(c == 0)
        def _():
            for ref in (fc_r, fc_i, a_bbr, a_bbi, a_cmr, a_cmi, a_ar, a_ai, a_dd):
                ref[...] = jnp.zeros_like(ref)

        for j in range(nb):
            cols = pl.ds(j * LANES, LANES)
            scols = pl.ds(j * STATE_W, STATE_W)
            nat_u, nat_d = nat[2 * j], nat[2 * j + 1]
            nat_u[...] = us_ref[:, cols].astype(F32)
            nat_d[...] = dys_ref[:, cols]
            for i in range(ls):
                rows_i = pl.ds(SUBLANES * i, SUBLANES)
                upb[j, rows_i, :] = nat_u[pl.ds(i, SUBLANES, stride=ls), :]
                dpb[j, rows_i, :] = nat_d[pl.ds(i, SUBLANES, stride=ls), :]
            u = upb[j]
            dysv = dpb[j]
            a_dd[j] += _acc8(dysv * u)
            up = u.astype(BF16)
            bur[j] = _dot(up, bbr_ref[j])
            bui[j] = _dot(up, bbi_ref[j])
            a_re = jnp.broadcast_to(ar_ref[:, scols], (SUBLANES, STATE_W))
            a_im = jnp.broadcast_to(ai_ref[:, scols], (SUBLANES, STATE_W))
            ec_r = ecr_ref[:, scols]
            ec_i = eci_ref[:, scols]
            xr[j, 0:SUBLANES, :] = ec_r
            xi[j, 0:SUBLANES, :] = ec_i
            _local_scan(a_re, a_im, bur.at[j], bui.at[j], xr.at[j], xi.at[j], SUBLANES, ls, False, init=(ec_r, ec_i),
                        xb=(hb_r.at[j], hb_i.at[j]))
            dysp = dysv.astype(BF16)
            a_cmr[j] += _dot_tn(dysp, hb_r[j])
            a_cmi[j] -= _dot_tn(dysp, hb_i[j])
            gr[j] = _dot_nt(dysp, cmr_ref[j])
            gi[j] = -_dot_nt(dysp, cmi_ref[j])
            _local_scan(a_re, -a_im, gr.at[j], gi.at[j], gr.at[j], gi.at[j], 0, ls, True)
            big_re = pwr_ref[tc - 1:tc, scols]
            big_im = -pwi_ref[tc - 1:tc, scols]
            f_re = fc_r[j, SUBLANES - 1:SUBLANES, :]
            f_im = fc_i[j, SUBLANES - 1:SUBLANES, :]
            for s in range(SUBLANES - 1, -1, -1):
                n_re = gr[j, s:s + 1, :] + big_re * f_re - big_im * f_im
                n_im = gi[j, s:s + 1, :] + big_re * f_im + big_im * f_re
                f_re, f_im = n_re, n_im
                if s > 0:
                    fc_r[j, s - 1:s, :] = f_re
                    fc_i[j, s - 1:s, :] = f_im
            f2_r, f2_i = _two(fc_r[j]), _two(fc_i[j])
            acc_r = jnp.zeros((SUBLANES, STATE_W), F32)
            acc_i = jnp.zeros((SUBLANES, STATE_W), F32)
            for k in range(tc // SLAB):
                rows_k = pl.ds(SLAB * k, SLAB)
                q_re = pvr_ref[rows_k, scols]
                q_im = pvi_ref[rows_k, scols]
                lam_re = gr[j, rows_k, :] + q_re * f2_r + q_im * f2_i
                lam_im = gi[j, rows_k, :] + q_re * f2_i - q_im * f2_r
                xp_re = xr[j, rows_k, :]
                xp_im = xi[j, rows_k, :]
                d_r = lam_re * xp_re + lam_im * xp_im
                d_i = lam_im * xp_re - lam_re * xp_im
                acc_r = acc_r + (d_r[0:SUBLANES] + d_r[SUBLANES:])
                acc_i = acc_i + (d_i[0:SUBLANES] + d_i[SUBLANES:])
                hb_r[j, rows_k, :] = lam_re.astype(BF16)
                hb_i[j, rows_k, :] = lam_im.astype(BF16)
            a_ar[j] += acc_r
            a_ai[j] += acc_i
            fc_r[j, SUBLANES - 1:SUBLANES, :] = f_re
            fc_i[j, SUBLANES - 1:SUBLANES, :] = f_im
            lb_re = hb_r[j]
            lb_im = hb_i[j]
            a_bbr[j] += _dot_tn(up, lb_re)
            a_bbi[j] += _dot_tn(up, lb_im)
            dpb[j] = _dot_nt(lb_re, bbr_ref[j]) + _dot_nt(lb_im, bbi_ref[j]) + dysv * d_ref[:, cols]
            for i in range(ls):
                nat_d[pl.ds(i, SUBLANES, stride=ls), :] = dpb[j, pl.ds(SUBLANES * i, SUBLANES), :]
            dus_ref[:, cols] = nat_d[...].astype(BF16)

        @pl.when(c == nc - 1)
        def _():
            row_g = lax.broadcasted_iota(jnp.int32, (LANES, STATE_W), 0) // SSM_H
            col_g = lax.broadcasted_iota(jnp.int32, (LANES, STATE_W), 1) // SSM_P
            fold = (lax.broadcasted_iota(jnp.int32, (STATE_W, SSM_P), 0) % SSM_P
                    == lax.broadcasted_iota(jnp.int32, (STATE_W, SSM_P), 1)).astype(BF16)
            for j in range(nb):
                rows_j = pl.ds(j * LANES, LANES)
                for acc, out in ((a_bbr, dbbr_ref), (a_bbi, dbbi_ref), (a_cmr, dcmr_ref), (a_cmi, dcmi_ref)):
                    out[rows_j, :] = _unpermute_rhs(jnp.where(row_g == col_g, acc[j], 0.0), fold)
                dar_ref[:, pl.ds(j * STATE_W, STATE_W)] = jnp.sum(a_ar[j], axis=0, keepdims=True)
                dai_ref[:, pl.ds(j * STATE_W, STATE_W)] = jnp.sum(a_ai[j], axis=0, keepdims=True)
                dd_ref[:, pl.ds(j * LANES, LANES)] = jnp.sum(a_dd[j], axis=0, keepdims=True)

    sp = _scan_specs(tc, nb, lambda c: nc - 1 - c)
    ghp = pl.BlockSpec((nb * LANES, SSM_P), lambda b, c: (b, 0))
    ghp_shape = jax.ShapeDtypeStruct((SSM_G * SSM_H, SSM_P), F32)
    small = pltpu.VMEM((nb, SUBLANES, STATE_W), F32)
    big = pltpu.VMEM((nb, tc, STATE_W), F32)
    bigp = pltpu.VMEM((nb, tc + SUBLANES, STATE_W), F32)
    blk = pltpu.VMEM((nb, LANES, STATE_W), F32)
    tok = pltpu.VMEM((nb, tc, LANES), F32)
    return pl.pallas_call(
        kern, name="ssm_scan_bwd", grid=(LANE_BLOCKS // nb, nc),
        in_specs=[sp["us"], sp["tok"], sp["car"], sp["car"], sp["bblk"], sp["bblk"], sp["cblk"], sp["cblk"],
                  sp["vec"], sp["vec"], sp["tab"], sp["tab"], sp["tab"], sp["tab"], sp["dvec"], _ANY],
        out_specs=(sp["us"], ghp, ghp, ghp, ghp, sp["vec"], sp["vec"], sp["dvec"]),
        out_shape=(jax.ShapeDtypeStruct(dproj.shape, BF16), ghp_shape, ghp_shape, ghp_shape, ghp_shape,
                   jax.ShapeDtypeStruct((1, STATE_ALL), F32), jax.ShapeDtypeStruct((1, STATE_ALL), F32),
                   jax.ShapeDtypeStruct((1, D_MODEL), F32)),
        scratch_shapes=[big, big, bigp, bigp, big, big, small, small, blk, blk, blk, blk,
                        small, small, pltpu.VMEM((nb, SUBLANES, LANES), F32), tok, tok,
                        pltpu.VMEM((nb, tc, STATE_W), BF16), pltpu.VMEM((nb, tc, STATE_W), BF16)]
        + [pltpu.VMEM((tc, LANES), F32)] * (2 * nb),
        input_output_aliases={15: 0},
        compiler_params=_cparams("parallel", "arbitrary"),
    )(proj, dys, ec_re, ec_im, bb_re, bb_im, cm_re, cm_im, abar_re, abar_im, pw_re, pw_im, pv_re, pv_im, d_skip, dproj)


def _eye5():
    return jnp.asarray(np.eye(GROUPS_PER_BLOCK, dtype=np.float32)[None, :, None, :, None])


def _embed_b(bb_t):
    t = bb_t.transpose(1, 0, 2).reshape(LANE_BLOCKS, GROUPS_PER_BLOCK, SSM_H, 1, SSM_P)
    return (t * _eye5()).reshape(LANE_BLOCKS, LANES, STATE_W)


def _embed_c(c_ghp):
    t = c_ghp.transpose(0, 2, 1).reshape(LANE_BLOCKS, GROUPS_PER_BLOCK, SSM_P, 1, SSM_H)
    return (t * _eye5()).reshape(LANE_BLOCKS, STATE_W, LANES)


def _local_step(x, c_row, tgt, w_ada_bf, b_ada, g1, g2, w_in_bf, pool_w_bf, pscale, a_re, a_im, log_dt,
                b_re_t, b_im_t, c_re, c_im, d_skip, glu_w_bf, glu_b, wbp_bf, wbs_bf, wout_bf,
                early_weight=None, late_weights=None, ride_for_dw_in=None, ride_for_dh=None):
    seq = x.shape[0]
    tc = min(SCAN_CHUNK, seq)
    mod8, silu_c = _mod_kernel(c_row, w_ada_bf, b_ada)
    mod = mod8[0:1]
    shift, scale, gate = mod[:, 0:D_MODEL], mod[:, D_MODEL:2 * D_MODEL], mod[:, 2 * D_MODEL:]

    abar_re, abar_im, bb_re_t, bb_im_t = _ssm_params(a_re, a_im, log_dt, b_re_t, b_im_t)
    abar_re_f, abar_im_f = abar_re.reshape(1, STATE_ALL), abar_im.reshape(1, STATE_ALL)
    pw_re, pw_im, pv_re, pv_im = _pow_tables(abar_re_f, abar_im_f, tc)
    bbe_re, bbe_im = _embed_b(bb_re_t).astype(BF16), _embed_b(bb_im_t).astype(BF16)
    cme_re, cme_im = _embed_c(c_re).astype(BF16), _embed_c(c_im).astype(BF16)
    d_row = d_skip.reshape(1, D_MODEL)

    if early_weight:
        h, h_t, *gathered = _in_norm(x, g1, scale, shift, ride=early_weight[0])
        w_in_bf, w_in_cols = early_weight[1](*gathered)
    else:
        h, h_t = _in_norm(x, g1, scale, shift)
        w_in_cols = [w_in_bf]
    bn_proj = next(b for b in (1536, 1024, 768, 512, 256) if w_in_cols[0].shape[1] % b == 0)
    if late_weights:
        proj, *gathered = _mm([h], w_in_cols, name="proj", out_dtype=BF16, bm=1024, bn=bn_proj, bk=1024,
                              ride=late_weights[0])
        pool_w_bf, glu_w_bf, wbp_bf, wbs_bf, wout_bf = late_weights[1](*gathered)
    else:
        proj = _mm([h], w_in_cols, name="proj", out_dtype=BF16, bm=1024, bn=bn_proj, bk=1024)
    ypool, ypool_t = _pool_fwd(proj, pool_w_bf, pscale)
    ys, ec_re, ec_im = _ssm_scan_fwd(proj, bbe_re, bbe_im, cme_re, cme_im, abar_re_f, abar_im_f,
                                      pw_re, pw_im, d_row, tc)
    yssm, yssm_t = _glu_fwd(ys, proj, glu_w_bf, glu_b)
    (dy, dypool, dyssm, dproj, merged_t, dob, dbp, dbs, loss, dgate, dg2) = _out_fwd_bwd(
        ypool, yssm, proj, x, tgt, gate, g2, wbp_bf, wbs_bf, wout_bf)

    d_wout = _mm([merged_t], [dob], name="dw_out", bm=1024, bn=1024, bk=2048)
    d_wbp = _mm([ypool_t], [dbp], name="dw_bp", bm=1024, bn=1024, bk=2048)
    d_wbs = _mm([yssm_t], [dbs], name="dw_bs", bm=1024, bn=1024, bk=2048)
    dys, dproj, dq, yg_t, d_glu_b = _glu_bwd(ys, proj, dyssm, glu_w_bf, glu_b, dproj)
    d_glu_w = _mm([yg_t], [dq], name="dw_glu", bm=1024, bn=1024, bk=2048)
    (dproj, dbbe_re, dbbe_im, dcme_re, dcme_im, d_abar_re, d_abar_im, d_dskip) = _ssm_scan_bwd(
        proj, dys, ec_re, ec_im, bbe_re, bbe_im, cme_re, cme_im, abar_re_f, abar_im_f,
        pw_re, pw_im, pv_re, pv_im, d_row, dproj, tc)
    dproj, d_pool_w, d_pscale = _pool_bwd(proj, dypool, pool_w_bf, pscale, dproj)
    dparts = [dproj]
    small_ready = dict(
        dg2=dg2, d_pscale=d_pscale, d_glu_b=d_glu_b, d_dskip=d_dskip, d_abar_re=d_abar_re, d_abar_im=d_abar_im,
        d_bb_re_t=dbbe_re.reshape(SSM_G, SSM_H, SSM_P).transpose(1, 0, 2),
        d_bb_im_t=dbbe_im.reshape(SSM_G, SSM_H, SSM_P).transpose(1, 0, 2),
        d_c_re=dcme_re.reshape(SSM_G, SSM_H, SSM_P), d_c_im=dcme_im.reshape(SSM_G, SSM_H, SSM_P))
    ride = ride_for_dw_in(small_ready) if ride_for_dw_in else None
    d_win = _mm([h_t], dparts, name="dw_in", bm=1024, bn=1024, bk=2048, ride=ride)
    rode_dw_in = ()
    if ride:
        d_win, rode_dw_in = d_win[0], tuple(d_win[1:])
    big_grads = dict(d_win=d_win, d_glu_w=d_glu_w, d_wbp=d_wbp, d_wbs=d_wbs, d_wout=d_wout, d_pool_w=d_pool_w)
    ride = ride_for_dh(big_grads) if ride_for_dh else None
    dh = _mm(dparts, [w_in_bf], tb=True, name="dh", bm=2048, bn=1024, bk=1024, ride=ride)
    rode = ()
    if ride:
        dh, rode = dh[0], tuple(dh[1:])
    grad_x, dshift, dscale, dg1 = _in_bwd(dh, x, dy, g1, scale)
    dmod = jnp.concatenate([dshift, dscale, dgate], axis=1)
    return dict(
        rode=rode, rode_dw_in=rode_dw_in, loss=loss[0, 0], grad_x=grad_x, dmod=dmod, silu_c=silu_c, dg1=dg1,
        **small_ready, **big_grads)


def _position():
    x, y, c = lax.axis_index("x"), lax.axis_index("y"), lax.axis_index("c")
    chips = [(1 - x, y), (x, 1 - y), (1 - x, 1 - y)]
    return x, y, c, chips


_ANY = pl.BlockSpec(memory_space=pl.ANY)
COMM_CHUNKS = 4
COMM_ROW_ALIGN = 16


def _row_chunks(rows, k):
    assert rows % (k * COMM_ROW_ALIGN) == 0, (rows, k)
    step = rows // k
    return [(q * step, step) for q in range(k)]


def _ag_weights_ride(packed, n_chunks=COMM_CHUNKS):
    rows, width = packed.shape
    half = rows // 2
    chunks = _row_chunks(half, n_chunks)
    nq = len(chunks)

    def parts(p_ref, out_ref, send_sems, recv_sems):
        x, y, c, chips = _position()
        sibling = (x, y, 1 - c)

        def copy(k, chip, h, q, to, src=None):
            start, size = chunks[q]
            rows_q = pl.ds(h * half + start, size)
            dst = out_ref.at[2 * chip[0] + chip[1], rows_q, :]
            return pltpu.make_async_remote_copy(
                src_ref=dst if src is None else src.at[rows_q, :], dst_ref=dst, send_sem=send_sems.at[k * nq + q],
                recv_sem=recv_sems.at[k * nq + q], device_id=to, device_id_type=MESH_ID)

        mine = [copy(6 + h, (x, y), h, q, sibling, src=p_ref) for h in range(2) for q in range(nq)]
        first = [copy(j, (x, y), c, q, (*chip, c), src=p_ref) for q in range(nq) for j, chip in enumerate(chips)]
        return (x, y, c), chips, sibling, copy, mine, first

    def start(ins, outs, sems):
        _, _, _, _, mine, first = parts(ins[0], outs[0], sems[0], sems[1])
        for cp in first + mine:
            cp.start()

    def wait(ins, outs, sems):
        (x, y, c), chips, sibling, copy, mine, first = parts(ins[0], outs[0], sems[0], sems[1])
        passed = []
        for q in range(nq):
            for j, chip in enumerate(chips):
                copy(j, chip, c, q, (x, y, c)).wait_recv()
                fwd = copy(3 + j, chip, c, q, sibling)
                fwd.start()
                passed.append(fwd)
        for q in range(nq):
            for j, chip in enumerate(chips):
                copy(3 + j, chip, 1 - c, q, (x, y, c)).wait_recv()
        for cp in mine:
            cp.wait_recv()
        for cp in first + passed + mine:
            cp.wait_send()

    return _Ride([packed], [jax.ShapeDtypeStruct((N_CHIPS, rows, width), packed.dtype)],
                 [pltpu.SemaphoreType.DMA((8 * nq,)), pltpu.SemaphoreType.DMA((8 * nq,))], start, wait)


def _join_rides(rides):
    def split(seq, counts):
        out, at = [], 0
        for n in counts:
            out.append(seq[at:at + n])
            at += n
        return out

    n_in = [len(r.inputs) for r in rides]
    n_out = [len(r.out_shapes) for r in rides]
    n_sem = [len(r.scratch) for r in rides]

    def start(ins, outs, sems):
        for r, i, o, s in zip(rides, split(ins, n_in), split(outs, n_out), split(sems, n_sem)):
            r.start(i, o, s)

    def wait(ins, outs, sems):
        for r, i, o, s in zip(rides, split(ins, n_in), split(outs, n_out), split(sems, n_sem)):
            r.wait(i, o, s)

    return _Ride([a for r in rides for a in r.inputs], [a for r in rides for a in r.out_shapes],
                 [a for r in rides for a in r.scratch], start, wait)


def _run_ride(ride, name):
    n_in, n_out = len(ride.inputs), len(ride.out_shapes)

    def body(*refs):
        ins, outs, sems = refs[:n_in], refs[n_in:n_in + n_out], refs[n_in + n_out:]
        ride.start(ins, outs, sems)
        ride.wait(ins, outs, sems)

    return pl.pallas_call(
        body, name=name, in_specs=[_ANY] * n_in, out_specs=(_ANY,) * n_out, out_shape=tuple(ride.out_shapes),
        scratch_shapes=list(ride.scratch))(*ride.inputs)


def _small_allgather_ride(buf):
    rows, width = buf.shape
    chunks = _row_chunks(rows, COMM_CHUNKS)
    nq = len(chunks)

    def parts(b_ref, all_ref, send_sems, recv_sems, local_sem):
        x, y, c, chips = _position()
        me, sibling = (x, y, c), (x, y, 1 - c)

        def copy(k, block, q, to, src=None):
            rows_q = pl.ds(chunks[q][0], chunks[q][1])
            dst = all_ref.at[4 * block[0] + 2 * block[1] + block[2], rows_q, :]
            return pltpu.make_async_remote_copy(
                src_ref=dst if src is None else src.at[rows_q, :], dst_ref=dst, send_sem=send_sems.at[k * nq + q],
                recv_sem=recv_sems.at[k * nq + q], device_id=to, device_id_type=MESH_ID)

        mine = pltpu.make_async_copy(b_ref, all_ref.at[4 * x + 2 * y + c], local_sem)
        first = []
        for q in range(nq):
            first += [copy(1 + j, me, q, (*chip, c), src=b_ref) for j, chip in enumerate(chips)]
            first.append(copy(0, me, q, sibling, src=b_ref))
        return me, sibling, c, chips, copy, mine, first

    def start(ins, outs, sems):
        _, _, _, _, _, mine, first = parts(ins[0], outs[0], *sems)
        mine.start()
        for cp in first:
            cp.start()

    def wait(ins, outs, sems):
        me, sibling, c, chips, copy, mine, first = parts(ins[0], outs[0], *sems)
        passed = []
        for q in range(nq):
            for j, chip in enumerate(chips):
                copy(1 + j, (*chip, c), q, me).wait_recv()
                fwd = copy(4 + j, (*chip, c), q, sibling)
                fwd.start()
                passed.append(fwd)
        for q in range(nq):
            copy(0, sibling, q, me).wait_recv()
            for j, chip in enumerate(chips):
                copy(4 + j, (*chip, 1 - c), q, me).wait_recv()
        for cp in first + passed:
            cp.wait_send()
        mine.wait()

    return _Ride([buf], [jax.ShapeDtypeStruct((N_DEV, rows, width), F32)],
                 [pltpu.SemaphoreType.DMA((7 * nq,)), pltpu.SemaphoreType.DMA((7 * nq,)), pltpu.SemaphoreType.DMA],
                 start, wait)


def _sum_devices(blocks):
    n, rows, width = blocks.shape
    rb = rows // 2 if (rows // 2) % SUBLANES == 0 else rows

    def kern(b_ref, o_ref):
        total = b_ref[0]
        for d in range(1, n):
            total = total + b_ref[d]
        o_ref[...] = total

    return pl.pallas_call(
        kern, name="small_sum", grid=(rows // rb,), in_specs=[pl.BlockSpec((n, rb, width), lambda i: (0, i, 0))],
        out_specs=pl.BlockSpec((rb, width), lambda i: (i, 0)), out_shape=jax.ShapeDtypeStruct((rows, width), F32),
        compiler_params=_cparams("parallel"))(blocks)


def _small_allgather_sum(buf, head_rows, n_chunks=COMM_CHUNKS):
    rows, width = buf.shape
    chunks = _row_chunks(rows, n_chunks)
    nq = len(chunks)

    def body(b_ref, head_ref, sum_ref, all_ref, send_sems, recv_sems, local_sem):
        x, y, c, chips = _position()
        me, sibling = (x, y, c), (x, y, 1 - c)

        def slot(px, py, pc):
            return all_ref.at[4 * px + 2 * py + pc]

        def copy(k, block, q, to, src=None):
            rows_q = pl.ds(chunks[q][0], chunks[q][1])
            dst = slot(*block).at[rows_q, :]
            return pltpu.make_async_remote_copy(
                src_ref=dst if src is None else src.at[rows_q, :], dst_ref=dst, send_sem=send_sems.at[k * nq + q],
                recv_sem=recv_sems.at[k * nq + q], device_id=to, device_id_type=MESH_ID)

        mine = pltpu.make_async_copy(b_ref, slot(*me), local_sem)
        mine.start()
        first = []
        for q in range(nq):
            first += [copy(1 + j, me, q, (*chip, c), src=b_ref) for j, chip in enumerate(chips)]
            first.append(copy(0, me, q, sibling, src=b_ref))
        for cp in first:
            cp.start()
        passed = []
        for q in range(nq):
            for j, chip in enumerate(chips):
                copy(1 + j, (*chip, c), q, me).wait_recv()
                fwd = copy(4 + j, (*chip, c), q, sibling)
                fwd.start()
                passed.append(fwd)
        for q in range(nq):
            copy(0, sibling, q, me).wait_recv()
            for j, chip in enumerate(chips):
                copy(4 + j, (*chip, 1 - c), q, me).wait_recv()
        for cp in first + passed:
            cp.wait_send()
        mine.wait()
        total = all_ref[0]
        for d in range(1, N_DEV):
            total = total + all_ref[d]
        sum_ref[...] = total
        head_ref[...] = all_ref[:, 0:head_rows, :]

    vm = pl.BlockSpec(memory_space=pltpu.VMEM)
    return pl.pallas_call(
        body, name="small_allgather_sum", in_specs=[vm], out_specs=(vm, vm),
        out_shape=(jax.ShapeDtypeStruct((N_DEV, head_rows, width), F32), jax.ShapeDtypeStruct((rows, width), F32)),
        scratch_shapes=[pltpu.VMEM((N_DEV, rows, width), F32), pltpu.SemaphoreType.DMA((7 * nq,)),
                        pltpu.SemaphoreType.DMA((7 * nq,)), pltpu.SemaphoreType.DMA],
        compiler_params=_cparams(),
    )(buf)


def _rs_pair(g):
    n, rows, width = g.shape
    half = rows // 2
    chunks = _row_chunks(half, COMM_CHUNKS)
    nq = len(chunks)

    def body(g_ref, got_ref, send_sems, recv_sems):
        x, y, c, _ = _position()
        swaps = []
        for k in range(n):
            for q, (start, size) in enumerate(chunks):
                swaps.append(pltpu.make_async_remote_copy(
                    src_ref=g_ref.at[k, pl.ds((1 - c) * half + start, size), :], dst_ref=got_ref.at[k, pl.ds(start, size), :],
                    send_sem=send_sems.at[k * nq + q], recv_sem=recv_sems.at[k * nq + q],
                    device_id=(x, y, 1 - c), device_id_type=MESH_ID))
        for cp in swaps:
            cp.start()
        for cp in swaps:
            cp.wait()

    return pl.pallas_call(
        body, name="rs_pair", in_specs=[_ANY], out_specs=_ANY, out_shape=jax.ShapeDtypeStruct((n, half, width), g.dtype),
        scratch_shapes=[pltpu.SemaphoreType.DMA((n * nq,)), pltpu.SemaphoreType.DMA((n * nq,))],
    )(g)


def _rs_chips_ride(part_bf):
    n, rows, width = part_bf.shape
    chunks = _row_chunks(rows, COMM_CHUNKS)
    nq = len(chunks)

    def sends(pb_ref, got_ref, send_sems, recv_sems):
        x, y, c, chips = _position()
        out = []
        for q, (start, size) in enumerate(chunks):
            for j, chip in enumerate(chips):
                out.append(pltpu.make_async_remote_copy(
                    src_ref=pb_ref.at[2 * chip[0] + chip[1], pl.ds(start, size), :], dst_ref=got_ref.at[j, pl.ds(start, size), :],
                    send_sem=send_sems.at[j * nq + q], recv_sem=recv_sems.at[j * nq + q],
                    device_id=(*chip, c), device_id_type=MESH_ID))
        return out

    def start(ins, outs, sems):
        for cp in sends(ins[0], outs[0], sems[0], sems[1]):
            cp.start()

    def wait(ins, outs, sems):
        for cp in sends(ins[0], outs[0], sems[0], sems[1]):
            cp.wait()

    return _Ride([part_bf], [jax.ShapeDtypeStruct((N_CHIPS - 1, rows, width), BF16)],
                 [pltpu.SemaphoreType.DMA((3 * nq,)), pltpu.SemaphoreType.DMA((3 * nq,))], start, wait)


def _rs_join(shard):
    rows, width = shard.shape
    half = rows // 2
    chunks = _row_chunks(half, COMM_CHUNKS)
    nq = len(chunks)

    def body(in_ref, out_ref, send_sems, recv_sems):
        x, y, c, _ = _position()
        def swap(q, h):
            rows_q = pl.ds(h * half + chunks[q][0], chunks[q][1])
            return pltpu.make_async_remote_copy(
                src_ref=in_ref.at[rows_q, :], dst_ref=out_ref.at[rows_q, :], send_sem=send_sems.at[q],
                recv_sem=recv_sems.at[q], device_id=(x, y, 1 - c), device_id_type=MESH_ID)

        for q in range(nq):
            swap(q, c).start()
        for q in range(nq):
            swap(q, 1 - c).wait_recv()
        for q in range(nq):
            swap(q, c).wait_send()

    return pl.pallas_call(
        body, name="rs_join", in_specs=[_ANY], out_specs=_ANY, input_output_aliases={0: 0},
        out_shape=jax.ShapeDtypeStruct(shard.shape, shard.dtype),
        scratch_shapes=[pltpu.SemaphoreType.DMA((nq,)), pltpu.SemaphoreType.DMA((nq,))],
    )(shard)


def _pair_add(g, got, core):
    n, half, width = got.shape
    nb = 2
    rb = half // nb

    def kern(c_ref, a_ref, b_ref, f_ref, h_ref):
        s = a_ref[...] + b_ref[...]
        f_ref[...] = s
        h_ref[...] = s.astype(BF16)

    spec = pl.BlockSpec((1, rb, width), lambda k, i, c_ref: (k, i, 0))
    return pl.pallas_call(
        kern, name="rs_pair_add",
        grid_spec=pltpu.PrefetchScalarGridSpec(
            num_scalar_prefetch=1, grid=(n, nb),
            in_specs=[pl.BlockSpec((1, rb, width), lambda k, i, c_ref: (k, c_ref[0] * nb + i, 0)), spec],
            out_specs=(spec, spec)),
        out_shape=(jax.ShapeDtypeStruct(got.shape, F32), jax.ShapeDtypeStruct(got.shape, BF16)),
        compiler_params=_cparams("parallel", "parallel"))(core, g, got)


def _chip_add(part_f32, got, where):
    _, rows, width = part_f32.shape
    nb = 2
    rb = rows // nb

    def kern(w_ref, a_ref, b_ref, o_ref):
        o_ref[...] = ((a_ref[0] + b_ref[0].astype(F32)) + b_ref[1].astype(F32)) + b_ref[2].astype(F32)

    return pl.pallas_call(
        kern, name="rs_chip_add",
        grid_spec=pltpu.PrefetchScalarGridSpec(
            num_scalar_prefetch=1, grid=(nb,),
            in_specs=[pl.BlockSpec((1, rb, width), lambda i, w_ref: (w_ref[0], i, 0)),
                      pl.BlockSpec((N_CHIPS - 1, rb, width), lambda i, w_ref: (0, i, 0))],
            out_specs=pl.BlockSpec((rb, width), lambda i, w_ref: (w_ref[1] * nb + i, 0))),
        out_shape=jax.ShapeDtypeStruct((2 * rows, width), F32),
        compiler_params=_cparams("parallel"))(where, part_f32, got)


def _adamw(w, g, m, v, name):
    rows, width = w.shape
    rb = rows
    for cand in (512, 256, 128, 64, 32, 16, 8):
        if rows % cand == 0 and cand * width * 4 <= ADAM_BLOCK_BYTES:
            rb = cand
            break
    spec = pl.BlockSpec((rb, width), lambda i: (i, 0))

    def kern(w_ref, g_ref, m_ref, v_ref, d_ref, nm_ref, nv_ref):
        d_ref[...], nm_ref[...], nv_ref[...] = _adamw_update(w_ref[...], g_ref[...], m_ref[...], v_ref[...])

    shp = jax.ShapeDtypeStruct(w.shape, F32)
    return pl.pallas_call(
        kern, name=name, grid=(rows // rb,), in_specs=[spec] * 4, out_specs=(spec, spec, spec),
        out_shape=(shp, shp, shp), compiler_params=_cparams("parallel"))(w, g, m, v)


def _adamw_update(w, g, m, v):
    nm = ADAM_B1 * m + (1.0 - ADAM_B1) * g
    nv = ADAM_B2 * v + (1.0 - ADAM_B2) * (g * g)
    m_hat = nm / (1.0 - ADAM_B1 ** ADAM_STEP)
    v_hat = nv / (1.0 - ADAM_B2 ** ADAM_STEP)
    return -ADAM_LR * (m_hat / (jnp.sqrt(v_hat) + ADAM_EPS) + ADAM_WD * w), nm, nv


def _adamw_small(params):
    n = len(params)

    def kern(*refs):
        ins, outs = refs[:4 * n], refs[4 * n:]
        for p in range(n):
            w_ref, g_ref, m_ref, v_ref = ins[4 * p:4 * p + 4]
            d, nm, nv = _adamw_update(w_ref[...], g_ref[...], m_ref[...], v_ref[...])
            outs[3 * p][...] = d
            outs[3 * p + 1][...] = nm
            outs[3 * p + 2][...] = nv

    flat = [a for group in params for a in group]
    shapes = [jax.ShapeDtypeStruct(group[0].shape, F32) for group in params for _ in range(3)]
    res = pl.pallas_call(kern, name="adamw_small", out_shape=tuple(shapes), compiler_params=_cparams())(*flat)
    return [tuple(res[3 * p:3 * p + 3]) for p in range(n)]


def _wada_grad(silu_t, dmod_cols):
    n = dmod_cols.shape[1]

    def kern(s_ref, d_ref, o_ref):
        acc = s_ref[:, 0:1] * d_ref[0:1, :]
        for b in range(1, N_DEV):
            acc = acc + s_ref[:, b:b + 1] * d_ref[b:b + 1, :]
        o_ref[...] = acc

    return pl.pallas_call(kern, name="wada_grad", out_shape=jax.ShapeDtypeStruct((D_MODEL, n), F32),
                          compiler_params=_cparams())(silu_t, dmod_cols)


def _rows(a, multiple):
    flat = a.reshape(-1)
    pad = (-flat.shape[0]) % (D_MODEL * multiple)
    if pad:
        flat = jnp.concatenate([flat, jnp.zeros((pad,), flat.dtype)])
    return flat.reshape(-1, D_MODEL)


def _part_rows(shape, multiple):
    return -(-int(np.prod(shape)) // (D_MODEL * multiple)) * multiple


def _pack_rows(parts, multiple, total_multiple=1):
    blocks = [_rows(p, multiple) for p in parts]
    pad = (-sum(b.shape[0] for b in blocks)) % total_multiple
    if pad:
        blocks.append(jnp.zeros((pad, D_MODEL), blocks[0].dtype))
    return jnp.concatenate(blocks, axis=0)


def _unpack_rows(buf, shapes, multiple):
    out, r = [], 0
    for shp in shapes:
        n = int(np.prod(shp))
        nr = _part_rows(shp, multiple)
        out.append(buf[r:r + nr].reshape(-1)[:n].reshape(shp))
        r += nr
    return out


def kernel(x, c, w_ada, b_ada, norm_pre, norm_post, w_in, pool_w, pool_scale, ssm_a_re, ssm_a_im, ssm_log_dt, ssm_b_re, ssm_b_im, ssm_c_re, ssm_c_im, ssm_d, glu_w, glu_b, w_branch_pool, w_branch_ssm, w_out, loss_target, m_w_ada, m_b_ada, m_norm_pre, m_norm_post, m_w_in, m_pool_w, m_pool_scale, m_ssm_a_re, m_ssm_a_im, m_ssm_log_dt, m_ssm_b_re, m_ssm_b_im, m_ssm_c_re, m_ssm_c_im, m_ssm_d, m_glu_w, m_glu_b, m_w_branch_pool, m_w_branch_ssm, m_w_out, v_w_ada, v_b_ada, v_norm_pre, v_norm_post, v_w_in, v_pool_w, v_pool_scale, v_ssm_a_re, v_ssm_a_im, v_ssm_log_dt, v_ssm_b_re, v_ssm_b_im, v_ssm_c_re, v_ssm_c_im, v_ssm_d, v_glu_w, v_glu_b, v_w_branch_pool, v_w_branch_ssm, v_w_out):
    n_ada = w_ada.shape[2]
    n_in = w_in.shape[2]
    n_row = glu_w.shape[1]
    n_pool = pool_w.shape[2]
    n_groups = pool_w.shape[1]

    (g_ada,) = _run_ride(_ag_weights_ride(w_ada[0].astype(BF16)), "ag_weights")
    w_ada_bf = g_ada.transpose(1, 0, 2).reshape(D_MODEL, N_CHIPS * n_ada)
    w_in_ride = _ag_weights_ride(w_in[0].astype(BF16))

    def unpack_w_in(g_in):
        return g_in.transpose(1, 0, 2).reshape(D_MODEL, N_CHIPS * n_in), [g_in[k] for k in range(N_CHIPS)]
    pool_rows = n_groups * n_pool * POOL_GW // D_MODEL
    late_shards = [pool_w[0].reshape(n_groups * n_pool, POOL_GW), glu_w[0], w_branch_pool[0], w_branch_ssm[0], w_out[0]]
    late_ride = _join_rides([_ag_weights_ride(s.astype(BF16), n_chunks=2) for s in late_shards])

    def unpack_late(pool, *squares):
        pool = pool.reshape(N_CHIPS, n_groups, n_pool, POOL_GW).transpose(1, 0, 2, 3)
        return (pool.reshape(n_groups, POOL_GW, POOL_GW), *[s.reshape(D_MODEL, D_MODEL) for s in squares])

    chip = 2 * lax.axis_index("x") + lax.axis_index("y")
    core = lax.axis_index("c").astype(jnp.int32)
    kept = {}

    def by_cols(a, n):
        return a.reshape(D_MODEL, N_CHIPS, n).transpose(1, 0, 2).reshape(N_CHIPS, -1, D_MODEL)

    def by_rows(a):
        return a.reshape(N_CHIPS, n_row, D_MODEL)

    def exchange_big(g):
        pool_by_chip = g["d_pool_w"].reshape(n_groups, N_CHIPS, n_pool, POOL_GW).transpose(1, 0, 2, 3)
        blocks = [by_cols(g["d_win"], n_in), by_rows(g["d_glu_w"]), by_rows(g["d_wbp"]), by_rows(g["d_wbs"]),
                  by_rows(g["d_wout"]), pool_by_chip.reshape(N_CHIPS, pool_rows, D_MODEL)]
        pad = (-sum(b.shape[1] for b in blocks)) % (2 * COMM_CHUNKS * COMM_ROW_ALIGN)
        if pad:
            blocks.append(jnp.zeros((N_CHIPS, pad, D_MODEL), F32))
        g_packed = jnp.concatenate(blocks, axis=1)
        kept["part_f32"], part_bf = _pair_add(g_packed, _rs_pair(g_packed), core.reshape(1))
        return _rs_chips_ride(part_bf)

    a_re, a_im, log_dt = ssm_a_re[0], ssm_a_im[0], ssm_log_dt[0].reshape(SSM_G, 1)
    b_re_t, b_im_t = ssm_b_re[0].transpose(2, 0, 1), ssm_b_im[0].transpose(2, 0, 1)
    early_names = ["dg2", "d_pscale", "d_glu_b", "d_dskip", "d_abar_re", "d_abar_im", "d_bb_re_t", "d_bb_im_t",
                   "d_c_re", "d_c_im"]

    def exchange_small(s):
        parts = [s[k] for k in early_names]
        kept["early_shapes"] = [p.shape for p in parts]
        return _small_allgather_ride(_pack_rows(parts, SUBLANES, COMM_CHUNKS * COMM_ROW_ALIGN))

    res = _local_step(x[0], c, loss_target[0], w_ada_bf, b_ada, norm_pre, norm_post, None, None, pool_scale,
                      a_re, a_im, log_dt, b_re_t, b_im_t, ssm_c_re[0], ssm_c_im[0], ssm_d[0], None, glu_b[0:1],
                      None, None, None, early_weight=(w_in_ride, unpack_w_in), late_weights=(late_ride, unpack_late),
                      ride_for_dw_in=exchange_small, ride_for_dh=exchange_big)

    (all_early,) = res["rode_dw_in"]
    (g_norm_post, g_pscale, g_glu_b, g_dskip, s_abar_re, s_abar_im, s_bb_re, s_bb_im, g_c_re, g_c_im) = _unpack_rows(
        _sum_devices(all_early), kept["early_shapes"], SUBLANES)
    g_a_re, g_a_im, g_log_dt, g_b_re_t, g_b_im_t = _ssm_params_bwd(
        a_re, a_im, log_dt, b_re_t, b_im_t, s_abar_re.reshape(SSM_G, SSM_P), s_abar_im.reshape(SSM_G, SSM_P),
        s_bb_re, s_bb_im)
    late_parts = [res["dmod"], res["silu_c"], res["dg1"], res["loss"].reshape(1, 1)]
    late_shapes = [p.shape for p in late_parts]
    head_rows = _part_rows(late_shapes[0], SUBLANES) + _part_rows(late_shapes[1], SUBLANES)
    all_late, sum_late = _small_allgather_sum(_pack_rows(late_parts, SUBLANES, COMM_ROW_ALIGN), head_rows, n_chunks=1)
    g_b_ada, _, g_norm_pre, loss = _unpack_rows(sum_late, late_shapes, SUBLANES)
    loss = loss[0, 0]
    dmod_all = all_late[:, 0:3].reshape(N_DEV, 3 * D_MODEL)
    dmod_cols = lax.dynamic_slice_in_dim(dmod_all, chip * n_ada, n_ada, axis=1)
    silu_t = all_late[:, _part_rows(late_shapes[0], SUBLANES)].transpose(1, 0)
    g_w_ada = _wada_grad(silu_t, dmod_cols)

    (got_chips,) = res["rode"]
    shard = _rs_join(_chip_add(kept["part_f32"], got_chips, jnp.stack([chip.astype(jnp.int32), core])))
    r = 0
    g_w_in = shard[r:r + n_in].reshape(D_MODEL, n_in)
    r += n_in
    g_squares = []
    for _ in range(4):
        g_squares.append(shard[r:r + n_row])
        r += n_row
    g_glu_w, g_wbp, g_wbs, g_wout = g_squares
    g_pool_w = shard[r:r + pool_rows].reshape(n_groups * n_pool, POOL_GW)

    big = [("w_ada", w_ada[0], g_w_ada, m_w_ada[0], v_w_ada[0]),
           ("w_in", w_in[0], g_w_in, m_w_in[0], v_w_in[0]),
           ("pool_w", pool_w[0].reshape(n_groups * n_pool, POOL_GW), g_pool_w,
            m_pool_w[0].reshape(n_groups * n_pool, POOL_GW), v_pool_w[0].reshape(n_groups * n_pool, POOL_GW)),
           ("glu_w", glu_w[0], g_glu_w, m_glu_w[0], v_glu_w[0]),
           ("w_branch_pool", w_branch_pool[0], g_wbp, m_w_branch_pool[0], v_w_branch_pool[0]),
           ("w_branch_ssm", w_branch_ssm[0], g_wbs, m_w_branch_ssm[0], v_w_branch_ssm[0]),
           ("w_out", w_out[0], g_wout, m_w_out[0], v_w_out[0])]
    out = {}
    for name, w_, g_, m_, v_ in big:
        d_, nm_, nv_ = _adamw(w_, g_, m_, v_, "adamw_" + name)
        out[name] = (g_, d_, nm_, nv_)

    g_b_re = g_b_re_t.transpose(1, 2, 0)
    g_b_im = g_b_im_t.transpose(1, 2, 0)
    small = [("b_ada", b_ada, g_b_ada, m_b_ada, v_b_ada),
             ("norm_pre", norm_pre, g_norm_pre, m_norm_pre, v_norm_pre),
             ("norm_post", norm_post, g_norm_post, m_norm_post, v_norm_post),
             ("pool_scale", pool_scale, g_pscale, m_pool_scale, v_pool_scale),
             ("ssm_a_re", ssm_a_re, g_a_re, m_ssm_a_re, v_ssm_a_re),
             ("ssm_a_im", ssm_a_im, g_a_im, m_ssm_a_im, v_ssm_a_im),
             ("ssm_log_dt", ssm_log_dt, g_log_dt, m_ssm_log_dt, v_ssm_log_dt),
             ("ssm_b_re", ssm_b_re, g_b_re, m_ssm_b_re, v_ssm_b_re),
             ("ssm_b_im", ssm_b_im, g_b_im, m_ssm_b_im, v_ssm_b_im),
             ("ssm_c_re", ssm_c_re, g_c_re, m_ssm_c_re, v_ssm_c_re),
             ("ssm_c_im", ssm_c_im, g_c_im, m_ssm_c_im, v_ssm_c_im),
             ("ssm_d", ssm_d, g_dskip, m_ssm_d, v_ssm_d),
             ("glu_b", glu_b, g_glu_b, m_glu_b, v_glu_b)]
    small = [(name, w_, g_.reshape(w_.shape), m_, v_) for name, w_, g_, m_, v_ in small]
    updates = _adamw_small([t[1:] for t in small])
    for (name, _, g_, _, _), (d_, nm_, nv_) in zip(small, updates):
        out[name] = (g_, d_, nm_, nv_)

    order = ["w_ada", "b_ada", "norm_pre", "norm_post", "w_in", "pool_w", "pool_scale", "ssm_a_re", "ssm_a_im",
             "ssm_log_dt", "ssm_b_re", "ssm_b_im", "ssm_c_re", "ssm_c_im", "ssm_d", "glu_w", "glu_b", "w_branch_pool",
             "w_branch_ssm", "w_out"]
    ref_shape = dict(w_ada=w_ada.shape, w_in=w_in.shape, pool_w=pool_w.shape, glu_w=glu_w.shape,
                     w_branch_pool=w_branch_pool.shape, w_branch_ssm=w_branch_ssm.shape, w_out=w_out.shape)
    for name, w_, _, _, _ in small:
        ref_shape[name] = w_.shape
    results = [loss, res["grad_x"][None]]
    for k in range(4):
        results += [out[name][k].reshape(ref_shape[name]) for name in order]
    return tuple(results)
```

```python
import functools
import math

import numpy as np
import jax
import jax.numpy as jnp
from jax import lax
from jax.experimental import pallas as pl
from jax.experimental.pallas import tpu as pltpu

F32 = jnp.float32
BF16 = jnp.bfloat16
MESH_ID = pl.DeviceIdType.MESH

D_MODEL = 1024
LANES = 128
SUBLANES = 8
SSM_G, SSM_P, SSM_H = 64, 64, 16
LANE_BLOCKS = D_MODEL // LANES
GROUPS_PER_BLOCK = LANES // SSM_H
STATE_W = GROUPS_PER_BLOCK * SSM_P
STATE_ALL = SSM_G * SSM_P
POOL_WINDOWS = (2, 4, 8, 16)
POOL_GW = D_MODEL // len(POOL_WINDOWS)
HALO = 16
RMS_EPS = 1e-6
N_CHIPS = 4
N_DEV = 8

SCAN_CHUNK = 512
SCAN_BLOCKS = 2
ROW_CHUNK = 256
ROW_CHUNK_WIDE = 512
VMEM_LIMIT_BYTES = 56 * 1024 * 1024

ADAM_BLOCK_BYTES = 1 << 20
ADAM_LR, ADAM_B1, ADAM_B2, ADAM_EPS, ADAM_WD, ADAM_STEP = 0.001, 0.9, 0.999, 1e-08, 0.01, 10

_GELU_C0 = math.sqrt(2.0 / math.pi)
_GELU_C1 = 0.044715


def _cparams(*sem):
    if sem:
        return pltpu.CompilerParams(dimension_semantics=sem, vmem_limit_bytes=VMEM_LIMIT_BYTES)
    return pltpu.CompilerParams(vmem_limit_bytes=VMEM_LIMIT_BYTES)


def _sigmoid(v):
    return jax.nn.sigmoid(v)


def _silu(v):
    return v * _sigmoid(v)


def _dsilu(v):
    s = _sigmoid(v)
    return s * (1.0 + v * (1.0 - s))


def _gelu(v):
    return v * (0.5 * (1.0 + jnp.tanh(_GELU_C0 * v * (1.0 + _GELU_C1 * (v * v)))))


def _gelu_and_grad(v):
    v2 = v * v
    t = jnp.tanh(_GELU_C0 * v * (1.0 + _GELU_C1 * v2))
    half = 0.5 * (1.0 + t)
    grad = half + (0.5 * _GELU_C0) * v * (1.0 - t * t) * (1.0 + (3.0 * _GELU_C1) * v2)
    return v * half, grad


def _silu_and_grad(v):
    s = _sigmoid(v)
    return v * s, s * (1.0 + v * (1.0 - s))


def _dot(a, b):
    return lax.dot_general(a, b, (((1,), (0,)), ((), ())), preferred_element_type=F32)


def _dot_nt(a, b):
    return lax.dot_general(a, b, (((1,), (1,)), ((), ())), preferred_element_type=F32)


def _dot_tn(a, b):
    return lax.dot_general(a, b, (((0,), (0,)), ((), ())), preferred_element_type=F32)


def _acc8(v):
    return v.reshape(v.shape[0] // SUBLANES, SUBLANES, v.shape[1]).sum(axis=0)


class _Ride:
    def __init__(self, inputs, out_shapes, scratch, start, wait):
        self.inputs, self.out_shapes, self.scratch, self.start, self.wait = inputs, out_shapes, scratch, start, wait


def _mm(a_parts, b_parts, *, name, ta=False, tb=False, out_dtype=F32, bm=512, bn=512, bk=512, ride=None):
    a_parts, b_parts = list(a_parts), list(b_parts)
    if ta:
        assert len(a_parts) == 1
        k_dim, m_dim = a_parts[0].shape
    else:
        m_dim = a_parts[0].shape[0]
        k_dim = sum(a.shape[1] for a in a_parts)
    if tb:
        assert len(b_parts) == 1
        n_dim = b_parts[0].shape[0]
    else:
        n_dim = sum(b.shape[1] for b in b_parts)
    bm, bn, bk = min(bm, m_dim), min(bn, n_dim), min(bk, k_dim)
    nm, nn, nk = m_dim // bm, n_dim // bn, k_dim // bk
    a_ranges, off = [], 0
    for a in a_parts:
        cnt = (a.shape[0] if ta else a.shape[1]) // bk
        a_ranges.append((off, cnt))
        off += cnt
    b_ranges, off = [], 0
    for b in b_parts:
        cnt = (b.shape[0] if tb else b.shape[1]) // bn
        b_ranges.append((off, cnt))
        off += cnt

    def a_spec(off, cnt):
        if ta:
            return pl.BlockSpec((bk, bm), lambda i, n, k: (k, i))
        return pl.BlockSpec((bm, bk), lambda i, n, k: (i, jnp.clip(k - off, 0, cnt - 1)))

    def b_spec(off, cnt):
        if tb:
            return pl.BlockSpec((bn, bk), lambda i, n, k: (n, k))
        return pl.BlockSpec((bk, bn), lambda i, n, k: (k, jnp.clip(n - off, 0, cnt - 1)))

    na, nb = len(a_parts), len(b_parts)
    dims = (((0 if ta else 1,), (1 if tb else 0,)), ((), ()))

    def kern_single(a_ref, b_ref, o_ref):
        o_ref[...] = lax.dot_general(a_ref[...].astype(BF16), b_ref[...].astype(BF16), dims,
                                     preferred_element_type=F32).astype(out_dtype)

    if na == 1 and nb == 1 and nk == 1 and not ride:
        return pl.pallas_call(
            kern_single, name=name, grid=(nm, nn),
            in_specs=[pl.BlockSpec((bk, bm), lambda i, n: (0, i)) if ta else pl.BlockSpec((bm, bk), lambda i, n: (i, 0)),
                      pl.BlockSpec((bn, bk), lambda i, n: (n, 0)) if tb else pl.BlockSpec((bk, bn), lambda i, n: (0, n))],
            out_specs=pl.BlockSpec((bm, bn), lambda i, n: (i, n)),
            out_shape=jax.ShapeDtypeStruct((m_dim, n_dim), out_dtype),
            compiler_params=_cparams("parallel", "parallel"),
        )(a_parts[0], b_parts[0])

    n_rin = len(ride.inputs) if ride else 0
    n_rout = len(ride.out_shapes) if ride else 0

    def kern(*refs):
        a_refs, b_refs = refs[:na], refs[na:na + nb]
        rin = refs[na + nb:na + nb + n_rin]
        o_ref = refs[na + nb + n_rin]
        rout = refs[na + nb + n_rin + 1:na + nb + n_rin + 1 + n_rout]
        acc = refs[na + nb + n_rin + 1 + n_rout]
        rsem = refs[na + nb + n_rin + 2 + n_rout:]
        i, n, k = pl.program_id(0), pl.program_id(1), pl.program_id(2)

        if ride:
            @pl.when((i == 0) & (n == 0) & (k == 0))
            def _():
                ride.start(rin, rout, rsem)

        if nk > 1:
            @pl.when(k == 0)
            def _():
                acc[...] = jnp.zeros_like(acc)

        for ja, (koff, kcnt) in enumerate(a_ranges):
            for jb, (noff, ncnt) in enumerate(b_ranges):
                def step(ja=ja, jb=jb):
                    a = a_refs[ja][...].astype(BF16)
                    b = b_refs[jb][...].astype(BF16)
                    prod = lax.dot_general(a, b, dims, preferred_element_type=F32)
                    if nk > 1:
                        acc[...] += prod
                    else:
                        o_ref[...] = prod.astype(out_dtype)

                if na == 1 and nb == 1:
                    step()
                else:
                    cond = (k >= koff) & (k < koff + kcnt) & (n >= noff) & (n < noff + ncnt)
                    pl.when(cond)(step)

        if nk > 1:
            @pl.when(k == nk - 1)
            def _():
                o_ref[...] = acc[...].astype(out_dtype)

        if ride:
            @pl.when((i == nm - 1) & (n == nn - 1) & (k == nk - 1))
            def _():
                ride.wait(rin, rout, rsem)

    any_spec = pl.BlockSpec(memory_space=pl.ANY)
    out_spec = pl.BlockSpec((bm, bn), lambda i, n, k: (i, n))
    out_shape = jax.ShapeDtypeStruct((m_dim, n_dim), out_dtype)
    acc_shape = pltpu.VMEM((bm, bn) if nk > 1 else (SUBLANES, LANES), F32)
    if not ride:
        return pl.pallas_call(
            kern, name=name, grid=(nm, nn, nk),
            in_specs=[a_spec(*r) for r in a_ranges] + [b_spec(*r) for r in b_ranges],
            out_specs=out_spec, out_shape=out_shape, scratch_shapes=[acc_shape],
            compiler_params=_cparams("parallel", "parallel", "arbitrary"),
        )(*a_parts, *b_parts)
    return pl.pallas_call(
        kern, name=name, grid=(nm, nn, nk),
        in_specs=[a_spec(*r) for r in a_ranges] + [b_spec(*r) for r in b_ranges] + [any_spec] * n_rin,
        out_specs=(out_spec,) + (any_spec,) * n_rout, out_shape=(out_shape,) + tuple(ride.out_shapes),
        scratch_shapes=[acc_shape] + list(ride.scratch),
        compiler_params=_cparams("arbitrary", "arbitrary", "arbitrary"),
    )(*a_parts, *b_parts, *ride.inputs)


def _ssm_param_fn(a_re, a_im, log_dt, b_re, b_im):
    dt = jnp.exp(log_dt)
    lam_re = jnp.minimum(a_re, -1e-4)
    lam_im = a_im
    mag = jnp.exp(lam_re * dt)
    abar_re = mag * jnp.cos(lam_im * dt)
    abar_im = mag * jnp.sin(lam_im * dt)
    den = lam_re * lam_re + lam_im * lam_im
    num_re = abar_re - 1.0
    f_re = (num_re * lam_re + abar_im * lam_im) / den
    f_im = (abar_im * lam_re - num_re * lam_im) / den
    bb_re = f_re * b_re - f_im * b_im
    bb_im = f_re * b_im + f_im * b_re
    return abar_re, abar_im, bb_re, bb_im


def _ssm_params(a_re, a_im, log_dt, b_re_t, b_im_t):
    def kern(are, aim, ldt, bre, bim, o_ar, o_ai, o_br, o_bi):
        ar, ai, br, bi = _ssm_param_fn(are[...], aim[...], ldt[...], bre[...], bim[...])
        o_ar[...] = ar
        o_ai[...] = ai
        o_br[...] = br
        o_bi[...] = bi

    gp = jax.ShapeDtypeStruct((SSM_G, SSM_P), F32)
    hgp = jax.ShapeDtypeStruct((SSM_H, SSM_G, SSM_P), F32)
    return pl.pallas_call(kern, name="ssm_params", out_shape=(gp, gp, hgp, hgp), compiler_params=_cparams())(
        a_re, a_im, log_dt, b_re_t, b_im_t)


def _ssm_params_bwd(a_re, a_im, log_dt, b_re_t, b_im_t, d_ar, d_ai, d_bbr, d_bbi):
    def kern(are, aim, ldt, bre, bim, dar, dai, dbr, dbi, o_are, o_aim, o_ldt, o_bre, o_bim):
        prim = (are[...], aim[...], ldt[...], bre[...], bim[...])
        _, vjp = jax.vjp(_ssm_param_fn, *prim)
        g = vjp((dar[...], dai[...], dbr[...], dbi[...]))
        o_are[...] = g[0]
        o_aim[...] = g[1]
        o_ldt[...] = g[2]
        o_bre[...] = g[3]
        o_bim[...] = g[4]

    gp = jax.ShapeDtypeStruct((SSM_G, SSM_P), F32)
    g1 = jax.ShapeDtypeStruct((SSM_G, 1), F32)
    hgp = jax.ShapeDtypeStruct((SSM_H, SSM_G, SSM_P), F32)
    return pl.pallas_call(kern, name="ssm_params_bwd", out_shape=(gp, gp, g1, hgp, hgp), compiler_params=_cparams())(
        a_re, a_im, log_dt, b_re_t, b_im_t, d_ar, d_ai, d_bbr, d_bbi)


def _pow_tables(abar_re, abar_im, tc):
    ls = tc // SUBLANES

    def kern(ar_ref, ai_ref, fr_ref, fi_ref, rr_ref, ri_ref):
        a_re = jnp.broadcast_to(ar_ref[...], (SUBLANES, STATE_W))
        a_im = jnp.broadcast_to(ai_ref[...], (SUBLANES, STATE_W))
        p_re, p_im = a_re, a_im
        for i in range(ls):
            fwd = pl.ds(SUBLANES * i, SUBLANES)
            rev = pl.ds(SUBLANES * (ls - 1 - i), SUBLANES)
            fr_ref[fwd, :] = p_re
            fi_ref[fwd, :] = p_im
            rr_ref[rev, :] = p_re
            ri_ref[rev, :] = p_im
            p_re, p_im = p_re * a_re - p_im * a_im, p_re * a_im + p_im * a_re

    vec = pl.BlockSpec((1, STATE_W), lambda b: (0, b))
    tab = pl.BlockSpec((tc, STATE_W), lambda b: (0, b))
    shp = jax.ShapeDtypeStruct((tc, STATE_ALL), F32)
    return pl.pallas_call(
        kern, name="pow_tables", grid=(LANE_BLOCKS,), in_specs=[vec, vec], out_specs=(tab, tab, tab, tab),
        out_shape=(shp, shp, shp, shp), compiler_params=_cparams("parallel"))(abar_re, abar_im)


def _mod_kernel(c_row, w_ada_bf, b_ada):
    def kern(c_ref, w_ref, b_ref, m_ref, s_ref):
        cv = c_ref[...]
        sc = _silu(cv)
        s_ref[...] = sc
        lhs = jnp.broadcast_to(sc, (SUBLANES, D_MODEL)).astype(BF16)
        m_ref[...] = _dot(lhs, w_ref[...]) + b_ref[...]

    return pl.pallas_call(
        kern, name="ada_mod",
        out_shape=(jax.ShapeDtypeStruct((SUBLANES, 3 * D_MODEL), F32), jax.ShapeDtypeStruct((1, D_MODEL), F32)),
        compiler_params=_cparams())(c_row, w_ada_bf, b_ada)


def _row_spec(tr, width=D_MODEL, col=0):
    return pl.BlockSpec((tr, width), lambda c: (c, col))


def _vec_spec(width=D_MODEL):
    return pl.BlockSpec((1, width), lambda c: (0, 0))


def _col_spec(tr):
    return pl.BlockSpec((D_MODEL, tr), lambda c: (0, c))


def _in_norm(x, g1, scale, shift, ride=None):
    seq = x.shape[0]
    tr = min(ROW_CHUNK_WIDE, seq)
    nc = seq // tr
    n_rin = len(ride.inputs) if ride else 0
    n_rout = len(ride.out_shapes) if ride else 0

    def kern(x_ref, g_ref, sc_ref, sh_ref, *rest):
        rin, (h_ref, ht_ref) = rest[:n_rin], rest[n_rin:n_rin + 2]
        rout, rsem = rest[n_rin + 2:n_rin + 2 + n_rout], rest[n_rin + 2 + n_rout:]
        c = pl.program_id(0)
        if ride:
            @pl.when(c == 0)
            def _():
                ride.start(rin, rout, rsem)

        xv = x_ref[...]
        r = lax.rsqrt(jnp.mean(xv * xv, axis=-1, keepdims=True) + RMS_EPS)
        h = ((xv * r) * g_ref[...]) * (1.0 + sc_ref[...]) + sh_ref[...]
        h_ref[...] = h.astype(BF16)
        ht_ref[...] = h.T.astype(BF16)

        if ride:
            @pl.when(c == nc - 1)
            def _():
                ride.wait(rin, rout, rsem)

    outs = pl.pallas_call(
        kern, name="in_norm", grid=(nc,),
        in_specs=[_row_spec(tr), _vec_spec(), _vec_spec(), _vec_spec()] + [_ANY] * n_rin,
        out_specs=(_row_spec(tr), _col_spec(tr)) + (_ANY,) * n_rout,
        out_shape=(jax.ShapeDtypeStruct((seq, D_MODEL), BF16), jax.ShapeDtypeStruct((D_MODEL, seq), BF16))
        + tuple(ride.out_shapes if ride else ()),
        scratch_shapes=list(ride.scratch) if ride else [],
        compiler_params=_cparams("arbitrary" if ride else "parallel"))(x, g1, scale, shift, *(ride.inputs if ride else ()))
    return outs


PAD = SUBLANES


def _window_sums(src, cols, w, bufs, rows, ahead):
    cur, cur_cols, step, k = src, cols, 1, 0
    data = pl.ds(PAD, rows)
    while step < w:
        dst = bufs[k % 2]
        dst[data, :] = cur[data, cur_cols] + cur[pl.ds(PAD + (step if ahead else -step), rows), cur_cols]
        cur, cur_cols, step, k = dst, slice(None), 2 * step, k + 1
    return cur, cur_cols


def _pool_windows(ext, bufs, pos, g, w, tr):
    cols = pl.ds(g * POOL_GW, POOL_GW)
    chunk = pl.ds(PAD + HALO, tr)
    cur = ext[chunk, cols]
    win, win_cols = _window_sums(ext, cols, w, bufs, HALO + tr, ahead=False)
    cnt = jnp.minimum(pos + 1, w).astype(F32)
    return win[chunk, win_cols] / cnt - cur


def _zero_pads(refs, rows):
    for ref in refs:
        ref[0:PAD, :] = jnp.zeros((PAD, ref.shape[1]), F32)
        ref[PAD + rows:, :] = jnp.zeros((PAD, ref.shape[1]), F32)


def _pool_fwd(proj, pool_w_bf, pscale):
    seq = proj.shape[0]
    tr = min(ROW_CHUNK_WIDE, seq)
    hb = tr // HALO

    def kern(up_ref, halo_ref, zp_ref, pw_ref, ps_ref, y_ref, yt_ref, ext, buf_a, buf_b):
        c = pl.program_id(0)
        _zero_pads((ext, buf_a, buf_b), HALO + tr)
        ext[pl.ds(PAD, HALO), :] = jnp.where(c > 0, halo_ref[...].astype(F32), 0.0)
        ext[pl.ds(PAD + HALO, tr), :] = up_ref[...].astype(F32)
        pos = c * tr + lax.broadcasted_iota(jnp.int32, (tr, POOL_GW), 0)
        for g, w in enumerate(POOL_WINDOWS):
            cols = pl.ds(g * POOL_GW, POOL_GW)
            pooled = _pool_windows(ext, (buf_a, buf_b), pos, g, w, tr)
            mixed = _dot(pooled.astype(BF16), pw_ref[g])
            y = mixed * ps_ref[:, cols] * _silu(zp_ref[:, cols].astype(F32))
            y_ref[:, cols] = y.astype(BF16)
            yt_ref[cols, :] = y.T.astype(BF16)

    return pl.pallas_call(
        kern, name="pool_fwd", grid=(seq // tr,),
        in_specs=[_row_spec(tr, col=0),
                  pl.BlockSpec((HALO, D_MODEL), lambda c: (jnp.maximum(c * hb - 1, 0), 0)),
                  _row_spec(tr, col=1),
                  pl.BlockSpec((len(POOL_WINDOWS), POOL_GW, POOL_GW), lambda c: (0, 0, 0)),
                  _vec_spec()],
        out_specs=(_row_spec(tr), _col_spec(tr)),
        out_shape=(jax.ShapeDtypeStruct((seq, D_MODEL), BF16), jax.ShapeDtypeStruct((D_MODEL, seq), BF16)),
        scratch_shapes=[pltpu.VMEM((tr + HALO + 2 * PAD, D_MODEL), F32), pltpu.VMEM((tr + HALO + 2 * PAD, POOL_GW), F32),
                        pltpu.VMEM((tr + HALO + 2 * PAD, POOL_GW), F32)],
        compiler_params=_cparams("parallel"))(proj, proj, proj, pool_w_bf, pscale)


def _pool_bwd(proj, dyp, pool_w_bf, pscale, dproj):
    seq = proj.shape[0]
    tr = min(ROW_CHUNK_WIDE, seq)
    hb = tr // HALO
    nc = seq // tr
    n_halo = seq // HALO

    def kern(up_ref, halo_ref, zp_ref, zpn_ref, dyp_ref, dypn_ref, pw_ref, ps_ref, _,
             d01_ref, dpw_ref, dps_ref, ext, dpn, buf_a, buf_b, acc_pw, acc_ps):
        c = pl.program_id(0)

        @pl.when(c == 0)
        def _():
            acc_pw[...] = jnp.zeros_like(acc_pw)
            acc_ps[...] = jnp.zeros_like(acc_ps)

        _zero_pads((ext, dpn, buf_a, buf_b), HALO + tr)
        ext[pl.ds(PAD, HALO), :] = jnp.where(c > 0, halo_ref[...].astype(F32), 0.0)
        ext[pl.ds(PAD + HALO, tr), :] = up_ref[...].astype(F32)
        pos = c * tr + lax.broadcasted_iota(jnp.int32, (tr, POOL_GW), 0)
        pos_n = (c + 1) * tr + lax.broadcasted_iota(jnp.int32, (HALO, POOL_GW), 0)
        has_next = c < nc - 1
        for g, w in enumerate(POOL_WINDOWS):
            cols = pl.ds(g * POOL_GW, POOL_GW)
            pooled_bf = _pool_windows(ext, (buf_a, buf_b), pos, g, w, tr).astype(BF16)
            wg = pw_ref[g]
            mixed = _dot(pooled_bf, wg)
            zp = zp_ref[:, cols].astype(F32)
            sz = _silu(zp)
            dyp_g = dyp_ref[:, cols].astype(F32)
            ps = ps_ref[:, cols]
            dmixed = (dyp_g * ps * sz).astype(BF16)
            acc_ps[:, cols] += _acc8(dyp_g * mixed * sz)
            d01_ref[:, pl.ds(D_MODEL + g * POOL_GW, POOL_GW)] = (dyp_g * mixed * ps * _dsilu(zp)).astype(BF16)
            acc_pw[g] += _dot_tn(pooled_bf, dmixed)
            dpooled = _dot_nt(dmixed, wg)
            dmixed_n = (jnp.where(has_next, dypn_ref[:, cols].astype(F32), 0.0) * ps * _silu(zpn_ref[:, cols].astype(F32))).astype(BF16)
            dpooled_n = _dot_nt(dmixed_n, wg)
            dpn[pl.ds(PAD, tr), :] = dpooled / jnp.minimum(pos + 1, w).astype(F32)
            dpn[pl.ds(PAD + tr, HALO), :] = dpooled_n / jnp.minimum(pos_n + 1, w).astype(F32)
            win, _ = _window_sums(dpn, slice(None), w, (buf_a, buf_b), tr + HALO, ahead=True)
            d01_ref[:, cols] = (win[pl.ds(PAD, tr), :] - dpooled).astype(BF16)

        @pl.when(c == nc - 1)
        def _():
            dpw_ref[...] = acc_pw[...]
            dps_ref[...] = jnp.sum(acc_ps[...], axis=0, keepdims=True)

    nxt = lambda c: (jnp.minimum((c + 1) * hb, n_halo - 1), 0)
    nxt1 = lambda c: (jnp.minimum((c + 1) * hb, n_halo - 1), 1)
    return pl.pallas_call(
        kern, name="pool_bwd", grid=(nc,),
        in_specs=[_row_spec(tr, col=0),
                  pl.BlockSpec((HALO, D_MODEL), lambda c: (jnp.maximum(c * hb - 1, 0), 0)),
                  _row_spec(tr, col=1),
                  pl.BlockSpec((HALO, D_MODEL), nxt1),
                  _row_spec(tr),
                  pl.BlockSpec((HALO, D_MODEL), nxt),
                  pl.BlockSpec((len(POOL_WINDOWS), POOL_GW, POOL_GW), lambda c: (0, 0, 0)),
                  _vec_spec(), _ANY],
        out_specs=(pl.BlockSpec((tr, 2 * D_MODEL), lambda c: (c, 0)),
                   pl.BlockSpec((len(POOL_WINDOWS), POOL_GW, POOL_GW), lambda c: (0, 0, 0)),
                   _vec_spec()),
        out_shape=(jax.ShapeDtypeStruct(dproj.shape, BF16),
                   jax.ShapeDtypeStruct((len(POOL_WINDOWS), POOL_GW, POOL_GW), F32),
                   jax.ShapeDtypeStruct((1, D_MODEL), F32)),
        scratch_shapes=[pltpu.VMEM((tr + HALO + 2 * PAD, D_MODEL), F32)]
        + [pltpu.VMEM((tr + HALO + 2 * PAD, POOL_GW), F32)] * 3
        + [pltpu.VMEM((len(POOL_WINDOWS), POOL_GW, POOL_GW), F32), pltpu.VMEM((SUBLANES, D_MODEL), F32)],
        input_output_aliases={8: 0},
        compiler_params=_cparams("arbitrary"))(proj, proj, proj, proj, dyp, dyp, pool_w_bf, pscale, dproj)


def _glu_fwd(ys, proj, glu_w_bf, glu_b):
    seq = ys.shape[0]
    tr = min(ROW_CHUNK_WIDE, seq)

    def kern(ys_ref, zs_ref, w_ref, b_ref, o_ref, ot_ref):
        yg = _gelu(ys_ref[...])
        q = _dot(yg.astype(BF16), w_ref[...]) + b_ref[...]
        y = yg * _sigmoid(q) * _silu(zs_ref[...].astype(F32))
        o_ref[...] = y.astype(BF16)
        ot_ref[...] = y.T.astype(BF16)

    return pl.pallas_call(
        kern, name="glu_fwd", grid=(seq // tr,),
        in_specs=[_row_spec(tr), _row_spec(tr, col=3), pl.BlockSpec((D_MODEL, D_MODEL), lambda c: (0, 0)), _vec_spec()],
        out_specs=(_row_spec(tr), _col_spec(tr)),
        out_shape=(jax.ShapeDtypeStruct((seq, D_MODEL), BF16), jax.ShapeDtypeStruct((D_MODEL, seq), BF16)),
        compiler_params=_cparams("parallel"))(ys, proj, glu_w_bf, glu_b)


def _glu_bwd(ys, proj, dyssm, glu_w_bf, glu_b, dproj):
    seq = ys.shape[0]
    tr = min(ROW_CHUNK_WIDE, seq)
    nc = seq // tr

    def kern(ys_ref, zs_ref, dy_ref, w_ref, b_ref, _, dys_ref, dzs_ref, dq_ref, yg_ref, db_ref, acc_b):
        c = pl.program_id(0)

        @pl.when(c == 0)
        def _():
            acc_b[...] = jnp.zeros_like(acc_b)

        yg, dgelu = _gelu_and_grad(ys_ref[...])
        yg_bf = yg.astype(BF16)
        q = _dot(yg_bf, w_ref[...]) + b_ref[...]
        sg = _sigmoid(q)
        silu_z, dsilu_z = _silu_and_grad(zs_ref[...].astype(F32))
        dyv = dy_ref[...].astype(F32)
        dyglu = dyv * silu_z
        yglu = yg * sg
        dzs_ref[...] = (dyv * yglu * dsilu_z).astype(BF16)
        dq = dyglu * yglu * (1.0 - sg)
        dq_bf = dq.astype(BF16)
        acc_b[...] += _acc8(dq)
        dyg = dyglu * sg + _dot_nt(dq_bf, w_ref[...])
        dys_ref[...] = dyg * dgelu
        dq_ref[...] = dq_bf
        yg_ref[...] = yg.T.astype(BF16)

        @pl.when(c == nc - 1)
        def _():
            db_ref[...] = jnp.sum(acc_b[...], axis=0, keepdims=True)

    bf = jax.ShapeDtypeStruct((seq, D_MODEL), BF16)
    return pl.pallas_call(
        kern, name="glu_bwd", grid=(nc,),
        in_specs=[_row_spec(tr), _row_spec(tr, col=3), _row_spec(tr),
                  pl.BlockSpec((D_MODEL, D_MODEL), lambda c: (0, 0)), _vec_spec(), _ANY],
        out_specs=(_row_spec(tr), _row_spec(tr, col=3), _row_spec(tr), _col_spec(tr), _vec_spec()),
        out_shape=(jax.ShapeDtypeStruct((seq, D_MODEL), F32), jax.ShapeDtypeStruct(dproj.shape, BF16), bf,
                   jax.ShapeDtypeStruct((D_MODEL, seq), BF16), jax.ShapeDtypeStruct((1, D_MODEL), F32)),
        scratch_shapes=[pltpu.VMEM((SUBLANES, D_MODEL), F32)],
        input_output_aliases={5: 1},
        compiler_params=_cparams("arbitrary"))(ys, proj, dyssm, glu_w_bf, glu_b, dproj)


def _out_fwd_bwd(ypool, yssm, proj, x, tgt, gate, g2, wbp_bf, wbs_bf, wout_bf):
    seq = x.shape[0]
    tr = min(ROW_CHUNK, seq)
    nc = seq // tr

    def kern(yp_ref, ysm_ref, gp_ref, gs_ref, x_ref, t_ref, gate_ref, g2_ref, wbp_ref, wbs_ref, wo_ref,
             dy_ref, dyp_ref, dys_ref, d45_ref, mb_ref, dob_ref, dbp_ref, dbs_ref, loss_ref, dgate_ref, dg2_ref,
             acc_l, acc_gate, acc_g2):
        c = pl.program_id(0)

        @pl.when(c == 0)
        def _():
            acc_l[...] = jnp.zeros_like(acc_l)
            acc_gate[...] = jnp.zeros_like(acc_gate)
            acc_g2[...] = jnp.zeros_like(acc_g2)

        bp = _dot(yp_ref[...], wbp_ref[...])
        bs = _dot(ysm_ref[...], wbs_ref[...])
        sp = _sigmoid(gp_ref[...].astype(F32))
        ss = _sigmoid(gs_ref[...].astype(F32))
        merged = sp * bp + ss * bs
        mb = merged.astype(BF16)
        out = _dot(mb, wo_ref[...])
        r2 = lax.rsqrt(jnp.mean(out * out, axis=-1, keepdims=True) + RMS_EPS)
        oh = out * r2
        gate_v, g2_v = gate_ref[...], g2_ref[...]
        ohg = oh * g2_v
        diff = (x_ref[...] + gate_v * ohg) - t_ref[...]
        acc_l[...] += _acc8(diff * diff)
        dyv = diff * (1.0 / D_MODEL)
        dy_ref[...] = dyv
        acc_gate[...] += _acc8(dyv * ohg)
        t = dyv * gate_v
        acc_g2[...] += _acc8(t * oh)
        doh = t * g2_v
        dout = r2 * (doh - oh * jnp.mean(doh * oh, axis=-1, keepdims=True))
        dob = dout.astype(BF16)
        dmerged = _dot_nt(dob, wo_ref[...])
        dbp = (dmerged * sp).astype(BF16)
        dbs = (dmerged * ss).astype(BF16)
        d45_ref[:, 0:D_MODEL] = (dmerged * bp * sp * (1.0 - sp)).astype(BF16)
        d45_ref[:, D_MODEL:] = (dmerged * bs * ss * (1.0 - ss)).astype(BF16)
        dyp_ref[...] = _dot_nt(dbp, wbp_ref[...]).astype(BF16)
        dys_ref[...] = _dot_nt(dbs, wbs_ref[...]).astype(BF16)
        mb_ref[...] = merged.T.astype(BF16)
        dob_ref[...] = dob
        dbp_ref[...] = dbp
        dbs_ref[...] = dbs

        @pl.when(c == nc - 1)
        def _():
            tot = jnp.sum(acc_l[...], axis=0, keepdims=True)
            loss_ref[...] = jnp.sum(tot, axis=1, keepdims=True) * (0.5 / D_MODEL)
            dgate_ref[...] = jnp.sum(acc_gate[...], axis=0, keepdims=True)
            dg2_ref[...] = jnp.sum(acc_g2[...], axis=0, keepdims=True)

    wspec = pl.BlockSpec((D_MODEL, D_MODEL), lambda c: (0, 0))
    f32 = jax.ShapeDtypeStruct((seq, D_MODEL), F32)
    bf = jax.ShapeDtypeStruct((seq, D_MODEL), BF16)
    vec = jax.ShapeDtypeStruct((1, D_MODEL), F32)
    acc = pltpu.VMEM((SUBLANES, D_MODEL), F32)
    return pl.pallas_call(
        kern, name="out_fwd_bwd", grid=(nc,),
        in_specs=[_row_spec(tr), _row_spec(tr), _row_spec(tr, col=4), _row_spec(tr, col=5), _row_spec(tr), _row_spec(tr),
                  _vec_spec(), _vec_spec(), wspec, wspec, wspec],
        out_specs=(_row_spec(tr), _row_spec(tr), _row_spec(tr), pl.BlockSpec((tr, 2 * D_MODEL), lambda c: (c, 2)),
                   _col_spec(tr), _row_spec(tr), _row_spec(tr), _row_spec(tr),
                   pl.BlockSpec((1, 1), lambda c: (0, 0)), _vec_spec(), _vec_spec()),
        out_shape=(f32, bf, bf, jax.ShapeDtypeStruct((seq, proj.shape[1]), BF16),
                   jax.ShapeDtypeStruct((D_MODEL, seq), BF16), bf, bf, bf,
                   jax.ShapeDtypeStruct((1, 1), F32), vec, vec),
        scratch_shapes=[acc, acc, acc],
        compiler_params=_cparams("arbitrary"))(ypool, yssm, proj, proj, x, tgt, gate, g2, wbp_bf, wbs_bf, wout_bf)


def _in_bwd(dh, x, dy, g1, scale):
    seq = x.shape[0]
    tr = min(ROW_CHUNK_WIDE, seq)
    nc = seq // tr

    def kern(dh_ref, x_ref, dy_ref, g_ref, sc_ref, dx_ref, dsh_ref, dsc_ref, dg_ref, a_sh, a_sc, a_g):
        c = pl.program_id(0)

        @pl.when(c == 0)
        def _():
            a_sh[...] = jnp.zeros_like(a_sh)
            a_sc[...] = jnp.zeros_like(a_sc)
            a_g[...] = jnp.zeros_like(a_g)

        xv = x_ref[...]
        r = lax.rsqrt(jnp.mean(xv * xv, axis=-1, keepdims=True) + RMS_EPS)
        xh = xv * r
        g = g_ref[...]
        dhv = dh_ref[...]
        a_sh[...] += _acc8(dhv)
        a_sc[...] += _acc8(dhv * (xh * g))
        dn = dhv * (1.0 + sc_ref[...])
        a_g[...] += _acc8(dn * xh)
        dxh = dn * g
        dx_ref[...] = dy_ref[...] + r * (dxh - xh * jnp.mean(dxh * xh, axis=-1, keepdims=True))

        @pl.when(c == nc - 1)
        def _():
            dsh_ref[...] = jnp.sum(a_sh[...], axis=0, keepdims=True)
            dsc_ref[...] = jnp.sum(a_sc[...], axis=0, keepdims=True)
            dg_ref[...] = jnp.sum(a_g[...], axis=0, keepdims=True)

    vec = jax.ShapeDtypeStruct((1, D_MODEL), F32)
    acc = pltpu.VMEM((SUBLANES, D_MODEL), F32)
    return pl.pallas_call(
        kern, name="in_bwd", grid=(nc,),
        in_specs=[_row_spec(tr), _row_spec(tr), _row_spec(tr), _vec_spec(), _vec_spec()],
        out_specs=(_row_spec(tr), _vec_spec(), _vec_spec(), _vec_spec()),
        out_shape=(jax.ShapeDtypeStruct((seq, D_MODEL), F32), vec, vec, vec),
        scratch_shapes=[acc, acc, acc],
        compiler_params=_cparams("arbitrary"))(dh, x, dy, g1, scale)


SLAB = 2 * SUBLANES


def _local_scan(a_re, a_im, br, bi, xr, xi, row0, ls, reverse, init=None, xb=None):
    if init is None:
        x_re = jnp.zeros((SUBLANES, STATE_W), F32)
        x_im = jnp.zeros((SUBLANES, STATE_W), F32)
    else:
        x_re, x_im = init
    for i in (range(ls - 1, -1, -1) if reverse else range(ls)):
        src = pl.ds(SUBLANES * i, SUBLANES)
        dst = pl.ds(row0 + SUBLANES * i, SUBLANES)
        n_re = a_re * x_re - a_im * x_im + br[src, :]
        n_im = a_re * x_im + a_im * x_re + bi[src, :]
        if xb is not None and i % 2 == 1:
            pair = pl.ds(SUBLANES * (i - 1), SLAB)
            xb[0][pair, :] = jnp.concatenate([x_re, n_re], axis=0).astype(BF16)
            xb[1][pair, :] = jnp.concatenate([x_im, n_im], axis=0).astype(BF16)
        x_re, x_im = n_re, n_im
        xr[dst, :] = x_re
        xi[dst, :] = x_im
    return x_re, x_im


def _two(v):
    return jnp.concatenate([v, v], axis=0)


def _unpermute_rhs(v, sel):
    hi = v.astype(BF16)
    r1 = v - hi.astype(F32)
    mid = r1.astype(BF16)
    lo = (r1 - mid.astype(F32)).astype(BF16)
    return _dot(hi, sel) + _dot(mid, sel) + _dot(lo, sel)


def _scan_specs(tc, nb, rows_of):
    return dict(
        us=pl.BlockSpec((tc, nb * LANES), lambda b, c: (rows_of(c), 2 * D_MODEL // (nb * LANES) + b)),
        tok=pl.BlockSpec((tc, nb * LANES), lambda b, c: (rows_of(c), b)),
        bblk=pl.BlockSpec((nb, LANES, STATE_W), lambda b, c: (b, 0, 0)),
        cblk=pl.BlockSpec((nb, STATE_W, LANES), lambda b, c: (b, 0, 0)),
        vec=pl.BlockSpec((1, nb * STATE_W), lambda b, c: (0, b)),
        tab=pl.BlockSpec((tc, nb * STATE_W), lambda b, c: (0, b)),
        car=pl.BlockSpec((SUBLANES, nb * STATE_W), lambda b, c: (rows_of(c), b)),
        dvec=pl.BlockSpec((1, nb * LANES), lambda b, c: (0, b)))


def _ssm_scan_fwd(proj, bb_re, bb_im, cm_re, cm_im, abar_re, abar_im, pw_re, pw_im, d_skip, tc):
    seq = proj.shape[0]
    nc = seq // tc
    ls = tc // SUBLANES
    nb = SCAN_BLOCKS

    def kern(us_ref, bbr_ref, bbi_ref, cmr_ref, cmi_ref, ar_ref, ai_ref, pwr_ref, pwi_ref, d_ref,
             ys_ref, ecr_ref, eci_ref, bur, bui, car_r, car_i, end_r, end_i, upb, xb_r, xb_i, *nat):
        c = pl.program_id(1)

        @pl.when(c == 0)
        def _():
            car_r[...] = jnp.zeros_like(car_r)
            car_i[...] = jnp.zeros_like(car_i)

        for j in range(nb):
            cols = pl.ds(j * LANES, LANES)
            scols = pl.ds(j * STATE_W, STATE_W)
            nat[j][...] = us_ref[:, cols].astype(F32)
            for i in range(ls):
                upb[j, pl.ds(SUBLANES * i, SUBLANES), :] = nat[j][pl.ds(i, SUBLANES, stride=ls), :]
            u = upb[j]
            up = u.astype(BF16)
            bur[j] = _dot(up, bbr_ref[j])
            bui[j] = _dot(up, bbi_ref[j])
            a_re = jnp.broadcast_to(ar_ref[:, scols], (SUBLANES, STATE_W))
            a_im = jnp.broadcast_to(ai_ref[:, scols], (SUBLANES, STATE_W))
            x_re, x_im = _local_scan(a_re, a_im, bur.at[j], bui.at[j], bur.at[j], bui.at[j], 0, ls, False)
            end_r[j] = x_re
            end_i[j] = x_im
            big_re = pwr_ref[tc - 1:tc, scols]
            big_im = pwi_ref[tc - 1:tc, scols]
            e_re = car_r[j, 0:1, :]
            e_im = car_i[j, 0:1, :]
            for s in range(SUBLANES):
                n_re = end_r[j, s:s + 1, :] + big_re * e_re - big_im * e_im
                n_im = end_i[j, s:s + 1, :] + big_re * e_im + big_im * e_re
                e_re, e_im = n_re, n_im
                if s < SUBLANES - 1:
                    car_r[j, s + 1:s + 2, :] = e_re
                    car_i[j, s + 1:s + 2, :] = e_im
            ec_re = car_r[j]
            ec_im = car_i[j]
            ecr_ref[:, scols] = ec_re
            eci_ref[:, scols] = ec_im
            e2_re, e2_im = _two(ec_re), _two(ec_im)
            for k in range(tc // SLAB):
                rows_k = pl.ds(SLAB * k, SLAB)
                p_re = pwr_ref[rows_k, scols]
                p_im = pwi_ref[rows_k, scols]
                xb_r[j, rows_k, :] = (bur[j, rows_k, :] + p_re * e2_re - p_im * e2_im).astype(BF16)
                xb_i[j, rows_k, :] = (bui[j, rows_k, :] + p_re * e2_im + p_im * e2_re).astype(BF16)
            upb[j] = _dot(xb_r[j], cmr_ref[j]) - _dot(xb_i[j], cmi_ref[j]) + d_ref[:, cols] * u
            for i in range(ls):
                nat[j][pl.ds(i, SUBLANES, stride=ls), :] = upb[j, pl.ds(SUBLANES * i, SUBLANES), :]
            ys_ref[:, cols] = nat[j][...]
            car_r[j, 0:1, :] = e_re
            car_i[j, 0:1, :] = e_im

    sp = _scan_specs(tc, nb, lambda c: c)
    carry_shape = jax.ShapeDtypeStruct((nc * SUBLANES, STATE_ALL), F32)
    small = pltpu.VMEM((nb, SUBLANES, STATE_W), F32)
    big = pltpu.VMEM((nb, tc, STATE_W), F32)
    return pl.pallas_call(
        kern, name="ssm_scan_fwd", grid=(LANE_BLOCKS // nb, nc),
        in_specs=[sp["us"], sp["bblk"], sp["bblk"], sp["cblk"], sp["cblk"], sp["vec"], sp["vec"], sp["tab"], sp["tab"],
                  sp["dvec"]],
        out_specs=(sp["tok"], sp["car"], sp["car"]),
        out_shape=(jax.ShapeDtypeStruct((seq, D_MODEL), F32), carry_shape, carry_shape),
        scratch_shapes=[big, big, small, small, small, small, pltpu.VMEM((nb, tc, LANES), F32),
                        pltpu.VMEM((nb, tc, STATE_W), BF16), pltpu.VMEM((nb, tc, STATE_W), BF16)]
        + [pltpu.VMEM((tc, LANES), F32)] * nb,
        compiler_params=_cparams("parallel", "arbitrary"),
    )(proj, bb_re, bb_im, cm_re, cm_im, abar_re, abar_im, pw_re, pw_im, d_skip)


def _ssm_scan_bwd(proj, dys, ec_re, ec_im, bb_re, bb_im, cm_re, cm_im, abar_re, abar_im,
                  pw_re, pw_im, pv_re, pv_im, d_skip, dproj, tc):
    seq = proj.shape[0]
    nc = seq // tc
    ls = tc // SUBLANES
    nb = SCAN_BLOCKS

    def kern(us_ref, dys_ref, ecr_ref, eci_ref, bbr_ref, bbi_ref, cmr_ref, cmi_ref, ar_ref, ai_ref,
             pwr_ref, pwi_ref, pvr_ref, pvi_ref, d_ref, _,
             dus_ref, dbbr_ref, dbbi_ref, dcmr_ref, dcmi_ref, dar_ref, dai_ref, dd_ref,
             bur, bui, xr, xi, gr, gi, fc_r, fc_i, a_bbr, a_bbi, a_cmr, a_cmi, a_ar, a_ai, a_dd, upb, dpb, hb_r, hb_i,
             *nat):
        c = pl.program_id(1)

        @pl.when(c == 0)
        def _():
            for ref in (fc_r, fc_i, a_bbr, a_bbi, a_cmr, a_cmi, a_ar, a_ai, a_dd):
                ref[...] = jnp.zeros_like(ref)

        for j in range(nb):
            cols = pl.ds(j * LANES, LANES)
            scols = pl.ds(j * STATE_W, STATE_W)
            nat_u, nat_d = nat[2 * j], nat[2 * j + 1]
            nat_u[...] = us_ref[:, cols].astype(F32)
            nat_d[...] = dys_ref[:, cols]
            for i in range(ls):
                rows_i = pl.ds(SUBLANES * i, SUBLANES)
                upb[j, rows_i, :] = nat_u[pl.ds(i, SUBLANES, stride=ls), :]
                dpb[j, rows_i, :] = nat_d[pl.ds(i, SUBLANES, stride=ls), :]
            u = upb[j]
            dysv = dpb[j]
            a_dd[j] += _acc8(dysv * u)
            up = u.astype(BF16)
            bur[j] = _dot(up, bbr_ref[j])
            bui[j] = _dot(up, bbi_ref[j])
            a_re = jnp.broadcast_to(ar_ref[:, scols], (SUBLANES, STATE_W))
            a_im = jnp.broadcast_to(ai_ref[:, scols], (SUBLANES, STATE_W))
            ec_r = ecr_ref[:, scols]
            ec_i = eci_ref[:, scols]
            xr[j, 0:SUBLANES, :] = ec_r
            xi[j, 0:SUBLANES, :] = ec_i
            _local_scan(a_re, a_im, bur.at[j], bui.at[j], xr.at[j], xi.at[j], SUBLANES, ls, False, init=(ec_r, ec_i),
                        xb=(hb_r.at[j], hb_i.at[j]))
            dysp = dysv.astype(BF16)
            a_cmr[j] += _dot_tn(dysp, hb_r[j])
            a_cmi[j] -= _dot_tn(dysp, hb_i[j])
            gr[j] = _dot_nt(dysp, cmr_ref[j])
            gi[j] = -_dot_nt(dysp, cmi_ref[j])
            _local_scan(a_re, -a_im, gr.at[j], gi.at[j], gr.at[j], gi.at[j], 0, ls, True)
            big_re = pwr_ref[tc - 1:tc, scols]
            big_im = -pwi_ref[tc - 1:tc, scols]
            f_re = fc_r[j, SUBLANES - 1:SUBLANES, :]
            f_im = fc_i[j, SUBLANES - 1:SUBLANES, :]
            for s in range(SUBLANES - 1, -1, -1):
                n_re = gr[j, s:s + 1, :] + big_re * f_re - big_im * f_im
                n_im = gi[j, s:s + 1, :] + big_re * f_im + big_im * f_re
                f_re, f_im = n_re, n_im
                if s > 0:
                    fc_r[j, s - 1:s, :] = f_re
                    fc_i[j, s - 1:s, :] = f_im
            f2_r, f2_i = _two(fc_r[j]), _two(fc_i[j])
            acc_r = jnp.zeros((SUBLANES, STATE_W), F32)
            acc_i = jnp.zeros((SUBLANES, STATE_W), F32)
            for k in range(tc // SLAB):
                rows_k = pl.ds(SLAB * k, SLAB)
                q_re = pvr_ref[rows_k, scols]
                q_im = pvi_ref[rows_k, scols]
                lam_re = gr[j, rows_k, :] + q_re * f2_r + q_im * f2_i
                lam_im = gi[j, rows_k, :] + q_re * f2_i - q_im * f2_r
                xp_re = xr[j, rows_k, :]
                xp_im = xi[j, rows_k, :]
                d_r = lam_re * xp_re + lam_im * xp_im
                d_i = lam_im * xp_re - lam_re * xp_im
                acc_r = acc_r + (d_r[0:SUBLANES] + d_r[SUBLANES:])
                acc_i = acc_i + (d_i[0:SUBLANES] + d_i[SUBLANES:])
                hb_r[j, rows_k, :] = lam_re.astype(BF16)
                hb_i[j, rows_k, :] = lam_im.astype(BF16)
            a_ar[j] += acc_r
            a_ai[j] += acc_i
            fc_r[j, SUBLANES - 1:SUBLANES, :] = f_re
            fc_i[j, SUBLANES - 1:SUBLANES, :] = f_im
            lb_re = hb_r[j]
            lb_im = hb_i[j]
            a_bbr[j] += _dot_tn(up, lb_re)
            a_bbi[j] += _dot_tn(up, lb_im)
            dpb[j] = _dot_nt(lb_re, bbr_ref[j]) + _dot_nt(lb_im, bbi_ref[j]) + dysv * d_ref[:, cols]
            for i in range(ls):
                nat_d[pl.ds(i, SUBLANES, stride=ls), :] = dpb[j, pl.ds(SUBLANES * i, SUBLANES), :]
            dus_ref[:, cols] = nat_d[...].astype(BF16)

        @pl.when(c == nc - 1)
        def _():
            row_g = lax.broadcasted_iota(jnp.int32, (LANES, STATE_W), 0) // SSM_H
            col_g = lax.broadcasted_iota(jnp.int32, (LANES, STATE_W), 1) // SSM_P
            fold = (lax.broadcasted_iota(jnp.int32, (STATE_W, SSM_P), 0) % SSM_P
                    == lax.broadcasted_iota(jnp.int32, (STATE_W, SSM_P), 1)).astype(BF16)
            for j in range(nb):
                rows_j = pl.ds(j * LANES, LANES)
                for acc, out in ((a_bbr, dbbr_ref), (a_bbi, dbbi_ref), (a_cmr, dcmr_ref), (a_cmi, dcmi_ref)):
                    out[rows_j, :] = _unpermute_rhs(jnp.where(row_g == col_g, acc[j], 0.0), fold)
                dar_ref[:, pl.ds(j * STATE_W, STATE_W)] = jnp.sum(a_ar[j], axis=0, keepdims=True)
                dai_ref[:, pl.ds(j * STATE_W, STATE_W)] = jnp.sum(a_ai[j], axis=0, keepdims=True)
                dd_ref[:, pl.ds(j * LANES, LANES)] = jnp.sum(a_dd[j], axis=0, keepdims=True)

    sp = _scan_specs(tc, nb, lambda c: nc - 1 - c)
    ghp = pl.BlockSpec((nb * LANES, SSM_P), lambda b, c: (b, 0))
    ghp_shape = jax.ShapeDtypeStruct((SSM_G * SSM_H, SSM_P), F32)
    small = pltpu.VMEM((nb, SUBLANES, STATE_W), F32)
    big = pltpu.VMEM((nb, tc, STATE_W), F32)
    bigp = pltpu.VMEM((nb, tc + SUBLANES, STATE_W), F32)
    blk = pltpu.VMEM((nb, LANES, STATE_W), F32)
    tok = pltpu.VMEM((nb, tc, LANES), F32)
    return pl.pallas_call(
        kern, name="ssm_scan_bwd", grid=(LANE_BLOCKS // nb, nc),
        in_specs=[sp["us"], sp["tok"], sp["car"], sp["car"], sp["bblk"], sp["bblk"], sp["cblk"], sp["cblk"],
                  sp["vec"], sp["vec"], sp["tab"], sp["tab"], sp["tab"], sp["tab"], sp["dvec"], _ANY],
        out_specs=(sp["us"], ghp, ghp, ghp, ghp, sp["vec"], sp["vec"], sp["dvec"]),
        out_shape=(jax.ShapeDtypeStruct(dproj.shape, BF16), ghp_shape, ghp_shape, ghp_shape, ghp_shape,
                   jax.ShapeDtypeStruct((1, STATE_ALL), F32), jax.ShapeDtypeStruct((1, STATE_ALL), F32),
                   jax.ShapeDtypeStruct((1, D_MODEL), F32)),
        scratch_shapes=[big, big, bigp, bigp, big, big, small, small, blk, blk, blk, blk,
                        small, small, pltpu.VMEM((nb, SUBLANES, LANES), F32), tok, tok,
                        pltpu.VMEM((nb, tc, STATE_W), BF16), pltpu.VMEM((nb, tc, STATE_W), BF16)]
        + [pltpu.VMEM((tc, LANES), F32)] * (2 * nb),
        input_output_aliases={15: 0},
        compiler_params=_cparams("parallel", "arbitrary"),
    )(proj, dys, ec_re, ec_im, bb_re, bb_im, cm_re, cm_im, abar_re, abar_im, pw_re, pw_im, pv_re, pv_im, d_skip, dproj)


def _eye5():
    return jnp.asarray(np.eye(GROUPS_PER_BLOCK, dtype=np.float32)[None, :, None, :, None])


def _embed_b(bb_t):
    t = bb_t.transpose(1, 0, 2).reshape(LANE_BLOCKS, GROUPS_PER_BLOCK, SSM_H, 1, SSM_P)
    return (t * _eye5()).reshape(LANE_BLOCKS, LANES, STATE_W)


def _embed_c(c_ghp):
    t = c_ghp.transpose(0, 2, 1).reshape(LANE_BLOCKS, GROUPS_PER_BLOCK, SSM_P, 1, SSM_H)
    return (t * _eye5()).reshape(LANE_BLOCKS, STATE_W, LANES)


def _local_step(x, c_row, tgt, w_ada_bf, b_ada, g1, g2, w_in_bf, pool_w_bf, pscale, a_re, a_im, log_dt,
                b_re_t, b_im_t, c_re, c_im, d_skip, glu_w_bf, glu_b, wbp_bf, wbs_bf, wout_bf,
                early_weight=None, late_weights=None, ride_for_dw_in=None, ride_for_dh=None):
    seq = x.shape[0]
    tc = min(SCAN_CHUNK, seq)
    mod8, silu_c = _mod_kernel(c_row, w_ada_bf, b_ada)
    mod = mod8[0:1]
    shift, scale, gate = mod[:, 0:D_MODEL], mod[:, D_MODEL:2 * D_MODEL], mod[:, 2 * D_MODEL:]

    abar_re, abar_im, bb_re_t, bb_im_t = _ssm_params(a_re, a_im, log_dt, b_re_t, b_im_t)
    abar_re_f, abar_im_f = abar_re.reshape(1, STATE_ALL), abar_im.reshape(1, STATE_ALL)
    pw_re, pw_im, pv_re, pv_im = _pow_tables(abar_re_f, abar_im_f, tc)
    bbe_re, bbe_im = _embed_b(bb_re_t).astype(BF16), _embed_b(bb_im_t).astype(BF16)
    cme_re, cme_im = _embed_c(c_re).astype(BF16), _embed_c(c_im).astype(BF16)
    d_row = d_skip.reshape(1, D_MODEL)

    if early_weight:
        h, h_t, *gathered = _in_norm(x, g1, scale, shift, ride=early_weight[0])
        w_in_bf, w_in_cols = early_weight[1](*gathered)
    else:
        h, h_t = _in_norm(x, g1, scale, shift)
        w_in_cols = [w_in_bf]
    bn_proj = next(b for b in (1536, 1024, 768, 512, 256) if w_in_cols[0].shape[1] % b == 0)
    if late_weights:
        proj, *gathered = _mm([h], w_in_cols, name="proj", out_dtype=BF16, bm=1024, bn=bn_proj, bk=1024,
                              ride=late_weights[0])
        pool_w_bf, glu_w_bf, wbp_bf, wbs_bf, wout_bf = late_weights[1](*gathered)
    else:
        proj = _mm([h], w_in_cols, name="proj", out_dtype=BF16, bm=1024, bn=bn_proj, bk=1024)
    ypool, ypool_t = _pool_fwd(proj, pool_w_bf, pscale)
    ys, ec_re, ec_im = _ssm_scan_fwd(proj, bbe_re, bbe_im, cme_re, cme_im, abar_re_f, abar_im_f,
                                      pw_re, pw_im, d_row, tc)
    yssm, yssm_t = _glu_fwd(ys, proj, glu_w_bf, glu_b)
    (dy, dypool, dyssm, dproj, merged_t, dob, dbp, dbs, loss, dgate, dg2) = _out_fwd_bwd(
        ypool, yssm, proj, x, tgt, gate, g2, wbp_bf, wbs_bf, wout_bf)

    d_wout = _mm([merged_t], [dob], name="dw_out", bm=1024, bn=1024, bk=2048)
    d_wbp = _mm([ypool_t], [dbp], name="dw_bp", bm=1024, bn=1024, bk=2048)
    d_wbs = _mm([yssm_t], [dbs], name="dw_bs", bm=1024, bn=1024, bk=2048)
    dys, dproj, dq, yg_t, d_glu_b = _glu_bwd(ys, proj, dyssm, glu_w_bf, glu_b, dproj)
    d_glu_w = _mm([yg_t], [dq], name="dw_glu", bm=1024, bn=1024, bk=2048)
    (dproj, dbbe_re, dbbe_im, dcme_re, dcme_im, d_abar_re, d_abar_im, d_dskip) = _ssm_scan_bwd(
        proj, dys, ec_re, ec_im, bbe_re, bbe_im, cme_re, cme_im, abar_re_f, abar_im_f,
        pw_re, pw_im, pv_re, pv_im, d_row, dproj, tc)
    dproj, d_pool_w, d_pscale = _pool_bwd(proj, dypool, pool_w_bf, pscale, dproj)
    dparts = [dproj]
    small_ready = dict(
        dg2=dg2, d_pscale=d_pscale, d_glu_b=d_glu_b, d_dskip=d_dskip, d_abar_re=d_abar_re, d_abar_im=d_abar_im,
        d_bb_re_t=dbbe_re.reshape(SSM_G, SSM_H, SSM_P).transpose(1, 0, 2),
        d_bb_im_t=dbbe_im.reshape(SSM_G, SSM_H, SSM_P).transpose(1, 0, 2),
        d_c_re=dcme_re.reshape(SSM_G, SSM_H, SSM_P), d_c_im=dcme_im.reshape(SSM_G, SSM_H, SSM_P))
    ride = ride_for_dw_in(small_ready) if ride_for_dw_in else None
    d_win = _mm([h_t], dparts, name="dw_in", bm=1024, bn=1024, bk=2048, ride=ride)
    rode_dw_in = ()
    if ride:
        d_win, rode_dw_in = d_win[0], tuple(d_win[1:])
    big_grads = dict(d_win=d_win, d_glu_w=d_glu_w, d_wbp=d_wbp, d_wbs=d_wbs, d_wout=d_wout, d_pool_w=d_pool_w)
    ride = ride_for_dh(big_grads) if ride_for_dh else None
    dh = _mm(dparts, [w_in_bf], tb=True, name="dh", bm=2048, bn=1024, bk=1024, ride=ride)
    rode = ()
    if ride:
        dh, rode = dh[0], tuple(dh[1:])
    grad_x, dshift, dscale, dg1 = _in_bwd(dh, x, dy, g1, scale)
    dmod = jnp.concatenate([dshift, dscale, dgate], axis=1)
    return dict(
        rode=rode, rode_dw_in=rode_dw_in, loss=loss[0, 0], grad_x=grad_x, dmod=dmod, silu_c=silu_c, dg1=dg1,
        **small_ready, **big_grads)


def _position():
    x, y, c = lax.axis_index("x"), lax.axis_index("y"), lax.axis_index("c")
    chips = [(1 - x, y), (x, 1 - y), (1 - x, 1 - y)]
    return x, y, c, chips


_ANY = pl.BlockSpec(memory_space=pl.ANY)
COMM_CHUNKS = 4
COMM_ROW_ALIGN = 16


def _row_chunks(rows, k):
    assert rows % (k * COMM_ROW_ALIGN) == 0, (rows, k)
    step = rows // k
    return [(q * step, step) for q in range(k)]


def _ag_weights_ride(packed, n_chunks=COMM_CHUNKS):
    rows, width = packed.shape
    half = rows // 2
    chunks = _row_chunks(half, n_chunks)
    nq = len(chunks)

    def parts(p_ref, out_ref, send_sems, recv_sems):
        x, y, c, chips = _position()
        sibling = (x, y, 1 - c)

        def copy(k, chip, h, q, to, src=None):
            start, size = chunks[q]
            rows_q = pl.ds(h * half + start, size)
            dst = out_ref.at[2 * chip[0] + chip[1], rows_q, :]
            return pltpu.make_async_remote_copy(
                src_ref=dst if src is None else src.at[rows_q, :], dst_ref=dst, send_sem=send_sems.at[k * nq + q],
                recv_sem=recv_sems.at[k * nq + q], device_id=to, device_id_type=MESH_ID)

        mine = [copy(6 + h, (x, y), h, q, sibling, src=p_ref) for h in range(2) for q in range(nq)]
        first = [copy(j, (x, y), c, q, (*chip, c), src=p_ref) for q in range(nq) for j, chip in enumerate(chips)]
        return (x, y, c), chips, sibling, copy, mine, first

    def start(ins, outs, sems):
        _, _, _, _, mine, first = parts(ins[0], outs[0], sems[0], sems[1])
        for cp in first + mine:
            cp.start()

    def wait(ins, outs, sems):
        (x, y, c), chips, sibling, copy, mine, first = parts(ins[0], outs[0], sems[0], sems[1])
        passed = []
        for q in range(nq):
            for j, chip in enumerate(chips):
                copy(j, chip, c, q, (x, y, c)).wait_recv()
                fwd = copy(3 + j, chip, c, q, sibling)
                fwd.start()
                passed.append(fwd)
        for q in range(nq):
            for j, chip in enumerate(chips):
                copy(3 + j, chip, 1 - c, q, (x, y, c)).wait_recv()
        for cp in mine:
            cp.wait_recv()
        for cp in first + passed + mine:
            cp.wait_send()

    return _Ride([packed], [jax.ShapeDtypeStruct((N_CHIPS, rows, width), packed.dtype)],
                 [pltpu.SemaphoreType.DMA((8 * nq,)), pltpu.SemaphoreType.DMA((8 * nq,))], start, wait)


def _join_rides(rides):
    def split(seq, counts):
        out, at = [], 0
        for n in counts:
            out.append(seq[at:at + n])
            at += n
        return out

    n_in = [len(r.inputs) for r in rides]
    n_out = [len(r.out_shapes) for r in rides]
    n_sem = [len(r.scratch) for r in rides]

    def start(ins, outs, sems):
        for r, i, o, s in zip(rides, split(ins, n_in), split(outs, n_out), split(sems, n_sem)):
            r.start(i, o, s)

    def wait(ins, outs, sems):
        for r, i, o, s in zip(rides, split(ins, n_in), split(outs, n_out), split(sems, n_sem)):
            r.wait(i, o, s)

    return _Ride([a for r in rides for a in r.inputs], [a for r in rides for a in r.out_shapes],
                 [a for r in rides for a in r.scratch], start, wait)


def _run_ride(ride, name):
    n_in, n_out = len(ride.inputs), len(ride.out_shapes)

    def body(*refs):
        ins, outs, sems = refs[:n_in], refs[n_in:n_in + n_out], refs[n_in + n_out:]
        ride.start(ins, outs, sems)
        ride.wait(ins, outs, sems)

    return pl.pallas_call(
        body, name=name, in_specs=[_ANY] * n_in, out_specs=(_ANY,) * n_out, out_shape=tuple(ride.out_shapes),
        scratch_shapes=list(ride.scratch))(*ride.inputs)


def _small_allgather_ride(buf):
    rows, width = buf.shape
    chunks = _row_chunks(rows, COMM_CHUNKS)
    nq = len(chunks)

    def parts(b_ref, all_ref, send_sems, recv_sems, local_sem):
        x, y, c, chips = _position()
        me, sibling = (x, y, c), (x, y, 1 - c)

        def copy(k, block, q, to, src=None):
            rows_q = pl.ds(chunks[q][0], chunks[q][1])
            dst = all_ref.at[4 * block[0] + 2 * block[1] + block[2], rows_q, :]
            return pltpu.make_async_remote_copy(
                src_ref=dst if src is None else src.at[rows_q, :], dst_ref=dst, send_sem=send_sems.at[k * nq + q],
                recv_sem=recv_sems.at[k * nq + q], device_id=to, device_id_type=MESH_ID)

        mine = pltpu.make_async_copy(b_ref, all_ref.at[4 * x + 2 * y + c], local_sem)
        first = []
        for q in range(nq):
            first += [copy(1 + j, me, q, (*chip, c), src=b_ref) for j, chip in enumerate(chips)]
            first.append(copy(0, me, q, sibling, src=b_ref))
        return me, sibling, c, chips, copy, mine, first

    def start(ins, outs, sems):
        _, _, _, _, _, mine, first = parts(ins[0], outs[0], *sems)
        mine.start()
        for cp in first:
            cp.start()

    def wait(ins, outs, sems):
        me, sibling, c, chips, copy, mine, first = parts(ins[0], outs[0], *sems)
        passed = []
        for q in range(nq):
            for j, chip in enumerate(chips):
                copy(1 + j, (*chip, c), q, me).wait_recv()
                fwd = copy(4 + j, (*chip, c), q, sibling)
                fwd.start()
                passed.append(fwd)
        for q in range(nq):
            copy(0, sibling, q, me).wait_recv()
            for j, chip in enumerate(chips):
                copy(4 + j, (*chip, 1 - c), q, me).wait_recv()
        for cp in first + passed:
            cp.wait_send()
        mine.wait()

    return _Ride([buf], [jax.ShapeDtypeStruct((N_DEV, rows, width), F32)],
                 [pltpu.SemaphoreType.DMA((7 * nq,)), pltpu.SemaphoreType.DMA((7 * nq,)), pltpu.SemaphoreType.DMA],
                 start, wait)


def _sum_devices(blocks):
    n, rows, width = blocks.shape
    rb = rows // 2 if (rows // 2) % SUBLANES == 0 else rows

    def kern(b_ref, o_ref):
        total = b_ref[0]
        for d in range(1, n):
            total = total + b_ref[d]
        o_ref[...] = total

    return pl.pallas_call(
        kern, name="small_sum", grid=(rows // rb,), in_specs=[pl.BlockSpec((n, rb, width), lambda i: (0, i, 0))],
        out_specs=pl.BlockSpec((rb, width), lambda i: (i, 0)), out_shape=jax.ShapeDtypeStruct((rows, width), F32),
        compiler_params=_cparams("parallel"))(blocks)


def _small_allgather_sum(buf, head_rows, n_chunks=COMM_CHUNKS):
    rows, width = buf.shape
    chunks = _row_chunks(rows, n_chunks)
    nq = len(chunks)

    def body(b_ref, head_ref, sum_ref, all_ref, send_sems, recv_sems, local_sem):
        x, y, c, chips = _position()
        me, sibling = (x, y, c), (x, y, 1 - c)

        def slot(px, py, pc):
            return all_ref.at[4 * px + 2 * py + pc]

        def copy(k, block, q, to, src=None):
            rows_q = pl.ds(chunks[q][0], chunks[q][1])
            dst = slot(*block).at[rows_q, :]
            return pltpu.make_async_remote_copy(
                src_ref=dst if src is None else src.at[rows_q, :], dst_ref=dst, send_sem=send_sems.at[k * nq + q],
                recv_sem=recv_sems.at[k * nq + q], device_id=to, device_id_type=MESH_ID)

        mine = pltpu.make_async_copy(b_ref, slot(*me), local_sem)
        mine.start()
        first = []
        for q in range(nq):
            first += [copy(1 + j, me, q, (*chip, c), src=b_ref) for j, chip in enumerate(chips)]
            first.append(copy(0, me, q, sibling, src=b_ref))
        for cp in first:
            cp.start()
        passed = []
        for q in range(nq):
            for j, chip in enumerate(chips):
                copy(1 + j, (*chip, c), q, me).wait_recv()
                fwd = copy(4 + j, (*chip, c), q, sibling)
                fwd.start()
                passed.append(fwd)
        for q in range(nq):
            copy(0, sibling, q, me).wait_recv()
            for j, chip in enumerate(chips):
                copy(4 + j, (*chip, 1 - c), q, me).wait_recv()
        for cp in first + passed:
            cp.wait_send()
        mine.wait()
        total = all_ref[0]
        for d in range(1, N_DEV):
            total = total + all_ref[d]
        sum_ref[...] = total
        head_ref[...] = all_ref[:, 0:head_rows, :]

    vm = pl.BlockSpec(memory_space=pltpu.VMEM)
    return pl.pallas_call(
        body, name="small_allgather_sum", in_specs=[vm], out_specs=(vm, vm),
        out_shape=(jax.ShapeDtypeStruct((N_DEV, head_rows, width), F32), jax.ShapeDtypeStruct((rows, width), F32)),
        scratch_shapes=[pltpu.VMEM((N_DEV, rows, width), F32), pltpu.SemaphoreType.DMA((7 * nq,)),
                        pltpu.SemaphoreType.DMA((7 * nq,)), pltpu.SemaphoreType.DMA],
        compiler_params=_cparams(),
    )(buf)


def _rs_pair(g):
    n, rows, width = g.shape
    half = rows // 2
    chunks = _row_chunks(half, COMM_CHUNKS)
    nq = len(chunks)

    def body(g_ref, got_ref, send_sems, recv_sems):
        x, y, c, _ = _position()
        swaps = []
        for k in range(n):
            for q, (start, size) in enumerate(chunks):
                swaps.append(pltpu.make_async_remote_copy(
                    src_ref=g_ref.at[k, pl.ds((1 - c) * half + start, size), :], dst_ref=got_ref.at[k, pl.ds(start, size), :],
                    send_sem=send_sems.at[k * nq + q], recv_sem=recv_sems.at[k * nq + q],
                    device_id=(x, y, 1 - c), device_id_type=MESH_ID))
        for cp in swaps:
            cp.start()
        for cp in swaps:
            cp.wait()

    return pl.pallas_call(
        body, name="rs_pair", in_specs=[_ANY], out_specs=_ANY, out_shape=jax.ShapeDtypeStruct((n, half, width), g.dtype),
        scratch_shapes=[pltpu.SemaphoreType.DMA((n * nq,)), pltpu.SemaphoreType.DMA((n * nq,))],
    )(g)


def _rs_chips_ride(part_bf):
    n, rows, width = part_bf.shape
    chunks = _row_chunks(rows, COMM_CHUNKS)
    nq = len(chunks)

    def sends(pb_ref, got_ref, send_sems, recv_sems):
        x, y, c, chips = _position()
        out = []
        for q, (start, size) in enumerate(chunks):
            for j, chip in enumerate(chips):
                out.append(pltpu.make_async_remote_copy(
                    src_ref=pb_ref.at[2 * chip[0] + chip[1], pl.ds(start, size), :], dst_ref=got_ref.at[j, pl.ds(start, size), :],
                    send_sem=send_sems.at[j * nq + q], recv_sem=recv_sems.at[j * nq + q],
                    device_id=(*chip, c), device_id_type=MESH_ID))
        return out

    def start(ins, outs, sems):
        for cp in sends(ins[0], outs[0], sems[0], sems[1]):
            cp.start()

    def wait(ins, outs, sems):
        for cp in sends(ins[0], outs[0], sems[0], sems[1]):
            cp.wait()

    return _Ride([part_bf], [jax.ShapeDtypeStruct((N_CHIPS - 1, rows, width), BF16)],
                 [pltpu.SemaphoreType.DMA((3 * nq,)), pltpu.SemaphoreType.DMA((3 * nq,))], start, wait)


def _rs_join(shard):
    rows, width = shard.shape
    half = rows // 2
    chunks = _row_chunks(half, COMM_CHUNKS)
    nq = len(chunks)

    def body(in_ref, out_ref, send_sems, recv_sems):
        x, y, c, _ = _position()
        def swap(q, h):
            rows_q = pl.ds(h * half + chunks[q][0], chunks[q][1])
            return pltpu.make_async_remote_copy(
                src_ref=in_ref.at[rows_q, :], dst_ref=out_ref.at[rows_q, :], send_sem=send_sems.at[q],
                recv_sem=recv_sems.at[q], device_id=(x, y, 1 - c), device_id_type=MESH_ID)

        for q in range(nq):
            swap(q, c).start()
        for q in range(nq):
            swap(q, 1 - c).wait_recv()
        for q in range(nq):
            swap(q, c).wait_send()

    return pl.pallas_call(
        body, name="rs_join", in_specs=[_ANY], out_specs=_ANY, input_output_aliases={0: 0},
        out_shape=jax.ShapeDtypeStruct(shard.shape, shard.dtype),
        scratch_shapes=[pltpu.SemaphoreType.DMA((nq,)), pltpu.SemaphoreType.DMA((nq,))],
    )(shard)


def _pair_add(g, got, core):
    n, half, width = got.shape
    nb = 2
    rb = half // nb

    def kern(c_ref, a_ref, b_ref, f_ref, h_ref):
        s = a_ref[...] + b_ref[...]
        f_ref[...] = s
        h_ref[...] = s.astype(BF16)

    spec = pl.BlockSpec((1, rb, width), lambda k, i, c_ref: (k, i, 0))
    return pl.pallas_call(
        kern, name="rs_pair_add",
        grid_spec=pltpu.PrefetchScalarGridSpec(
            num_scalar_prefetch=1, grid=(n, nb),
            in_specs=[pl.BlockSpec((1, rb, width), lambda k, i, c_ref: (k, c_ref[0] * nb + i, 0)), spec],
            out_specs=(spec, spec)),
        out_shape=(jax.ShapeDtypeStruct(got.shape, F32), jax.ShapeDtypeStruct(got.shape, BF16)),
        compiler_params=_cparams("parallel", "parallel"))(core, g, got)


def _chip_add(part_f32, got, where):
    _, rows, width = part_f32.shape
    nb = 2
    rb = rows // nb

    def kern(w_ref, a_ref, b_ref, o_ref):
        o_ref[...] = ((a_ref[0] + b_ref[0].astype(F32)) + b_ref[1].astype(F32)) + b_ref[2].astype(F32)

    return pl.pallas_call(
        kern, name="rs_chip_add",
        grid_spec=pltpu.PrefetchScalarGridSpec(
            num_scalar_prefetch=1, grid=(nb,),
            in_specs=[pl.BlockSpec((1, rb, width), lambda i, w_ref: (w_ref[0], i, 0)),
                      pl.BlockSpec((N_CHIPS - 1, rb, width), lambda i, w_ref: (0, i, 0))],
            out_specs=pl.BlockSpec((rb, width), lambda i, w_ref: (w_ref[1] * nb + i, 0))),
        out_shape=jax.ShapeDtypeStruct((2 * rows, width), F32),
        compiler_params=_cparams("parallel"))(where, part_f32, got)


def _adamw(w, g, m, v, name):
    rows, width = w.shape
    rb = rows
    for cand in (512, 256, 128, 64, 32, 16, 8):
        if rows % cand == 0 and cand * width * 4 <= ADAM_BLOCK_BYTES:
            rb = cand
            break
    spec = pl.BlockSpec((rb, width), lambda i: (i, 0))

    def kern(w_ref, g_ref, m_ref, v_ref, d_ref, nm_ref, nv_ref):
        d_ref[...], nm_ref[...], nv_ref[...] = _adamw_update(w_ref[...], g_ref[...], m_ref[...], v_ref[...])

    shp = jax.ShapeDtypeStruct(w.shape, F32)
    return pl.pallas_call(
        kern, name=name, grid=(rows // rb,), in_specs=[spec] * 4, out_specs=(spec, spec, spec),
        out_shape=(shp, shp, shp), compiler_params=_cparams("parallel"))(w, g, m, v)


def _adamw_update(w, g, m, v):
    nm = ADAM_B1 * m + (1.0 - ADAM_B1) * g
    nv = ADAM_B2 * v + (1.0 - ADAM_B2) * (g * g)
    m_hat = nm / (1.0 - ADAM_B1 ** ADAM_STEP)
    v_hat = nv / (1.0 - ADAM_B2 ** ADAM_STEP)
    return -ADAM_LR * (m_hat / (jnp.sqrt(v_hat) + ADAM_EPS) + ADAM_WD * w), nm, nv


def _adamw_small(params):
    n = len(params)

    def kern(*refs):
        ins, outs = refs[:4 * n], refs[4 * n:]
        for p in range(n):
            w_ref, g_ref, m_ref, v_ref = ins[4 * p:4 * p + 4]
            d, nm, nv = _adamw_update(w_ref[...], g_ref[...], m_ref[...], v_ref[...])
            outs[3 * p][...] = d
            outs[3 * p + 1][...] = nm
            outs[3 * p + 2][...] = nv

    flat = [a for group in params for a in group]
    shapes = [jax.ShapeDtypeStruct(group[0].shape, F32) for group in params for _ in range(3)]
    res = pl.pallas_call(kern, name="adamw_small", out_shape=tuple(shapes), compiler_params=_cparams())(*flat)
    return [tuple(res[3 * p:3 * p + 3]) for p in range(n)]


def _wada_grad(silu_t, dmod_cols):
    n = dmod_cols.shape[1]

    def kern(s_ref, d_ref, o_ref):
        acc = s_ref[:, 0:1] * d_ref[0:1, :]
        for b in range(1, N_DEV):
            acc = acc + s_ref[:, b:b + 1] * d_ref[b:b + 1, :]
        o_ref[...] = acc

    return pl.pallas_call(kern, name="wada_grad", out_shape=jax.ShapeDtypeStruct((D_MODEL, n), F32),
                          compiler_params=_cparams())(silu_t, dmod_cols)


def _rows(a, multiple):
    flat = a.reshape(-1)
    pad = (-flat.shape[0]) % (D_MODEL * multiple)
    if pad:
        flat = jnp.concatenate([flat, jnp.zeros((pad,), flat.dtype)])
    return flat.reshape(-1, D_MODEL)


def _part_rows(shape, multiple):
    return -(-int(np.prod(shape)) // (D_MODEL * multiple)) * multiple


def _pack_rows(parts, multiple, total_multiple=1):
    blocks = [_rows(p, multiple) for p in parts]
    pad = (-sum(b.shape[0] for b in blocks)) % total_multiple
    if pad:
        blocks.append(jnp.zeros((pad, D_MODEL), blocks[0].dtype))
    return jnp.concatenate(blocks, axis=0)


def _unpack_rows(buf, shapes, multiple):
    out, r = [], 0
    for shp in shapes:
        n = int(np.prod(shp))
        nr = _part_rows(shp, multiple)
        out.append(buf[r:r + nr].reshape(-1)[:n].reshape(shp))
        r += nr
    return out


def kernel(x, c, w_ada, b_ada, norm_pre, norm_post, w_in, pool_w, pool_scale, ssm_a_re, ssm_a_im, ssm_log_dt, ssm_b_re, ssm_b_im, ssm_c_re, ssm_c_im, ssm_d, glu_w, glu_b, w_branch_pool, w_branch_ssm, w_out, loss_target, m_w_ada, m_b_ada, m_norm_pre, m_norm_post, m_w_in, m_pool_w, m_pool_scale, m_ssm_a_re, m_ssm_a_im, m_ssm_log_dt, m_ssm_b_re, m_ssm_b_im, m_ssm_c_re, m_ssm_c_im, m_ssm_d, m_glu_w, m_glu_b, m_w_branch_pool, m_w_branch_ssm, m_w_out, v_w_ada, v_b_ada, v_norm_pre, v_norm_post, v_w_in, v_pool_w, v_pool_scale, v_ssm_a_re, v_ssm_a_im, v_ssm_log_dt, v_ssm_b_re, v_ssm_b_im, v_ssm_c_re, v_ssm_c_im, v_ssm_d, v_glu_w, v_glu_b, v_w_branch_pool, v_w_branch_ssm, v_w_out):
    n_ada = w_ada.shape[2]
    n_in = w_in.shape[2]
    n_row = glu_w.shape[1]
    n_pool = pool_w.shape[2]
    n_groups = pool_w.shape[1]

    (g_ada,) = _run_ride(_ag_weights_ride(w_ada[0].astype(BF16)), "ag_weights")
    w_ada_bf = g_ada.transpose(1, 0, 2).reshape(D_MODEL, N_CHIPS * n_ada)
    w_in_ride = _ag_weights_ride(w_in[0].astype(BF16))

    def unpack_w_in(g_in):
        return g_in.transpose(1, 0, 2).reshape(D_MODEL, N_CHIPS * n_in), [g_in[k] for k in range(N_CHIPS)]
    pool_rows = n_groups * n_pool * POOL_GW // D_MODEL
    late_shards = [pool_w[0].reshape(n_groups * n_pool, POOL_GW), glu_w[0], w_branch_pool[0], w_branch_ssm[0], w_out[0]]
    late_ride = _join_rides([_ag_weights_ride(s.astype(BF16), n_chunks=2) for s in late_shards])

    def unpack_late(pool, *squares):
        pool = pool.reshape(N_CHIPS, n_groups, n_pool, POOL_GW).transpose(1, 0, 2, 3)
        return (pool.reshape(n_groups, POOL_GW, POOL_GW), *[s.reshape(D_MODEL, D_MODEL) for s in squares])

    chip = 2 * lax.axis_index("x") + lax.axis_index("y")
    core = lax.axis_index("c").astype(jnp.int32)
    kept = {}

    def by_cols(a, n):
        return a.reshape(D_MODEL, N_CHIPS, n).transpose(1, 0, 2).reshape(N_CHIPS, -1, D_MODEL)

    def by_rows(a):
        return a.reshape(N_CHIPS, n_row, D_MODEL)

    def exchange_big(g):
        pool_by_chip = g["d_pool_w"].reshape(n_groups, N_CHIPS, n_pool, POOL_GW).transpose(1, 0, 2, 3)
        blocks = [by_cols(g["d_win"], n_in), by_rows(g["d_glu_w"]), by_rows(g["d_wbp"]), by_rows(g["d_wbs"]),
                  by_rows(g["d_wout"]), pool_by_chip.reshape(N_CHIPS, pool_rows, D_MODEL)]
        pad = (-sum(b.shape[1] for b in blocks)) % (2 * COMM_CHUNKS * COMM_ROW_ALIGN)
        if pad:
            blocks.append(jnp.zeros((N_CHIPS, pad, D_MODEL), F32))
        g_packed = jnp.concatenate(blocks, axis=1)
        kept["part_f32"], part_bf = _pair_add(g_packed, _rs_pair(g_packed), core.reshape(1))
        return _rs_chips_ride(part_bf)

    a_re, a_im, log_dt = ssm_a_re[0], ssm_a_im[0], ssm_log_dt[0].reshape(SSM_G, 1)
    b_re_t, b_im_t = ssm_b_re[0].transpose(2, 0, 1), ssm_b_im[0].transpose(2, 0, 1)
    early_names = ["dg2", "d_pscale", "d_glu_b", "d_dskip", "d_abar_re", "d_abar_im", "d_bb_re_t", "d_bb_im_t",
                   "d_c_re", "d_c_im"]

    def exchange_small(s):
        parts = [s[k] for k in early_names]
        kept["early_shapes"] = [p.shape for p in parts]
        return _small_allgather_ride(_pack_rows(parts, SUBLANES, COMM_CHUNKS * COMM_ROW_ALIGN))

    res = _local_step(x[0], c, loss_target[0], w_ada_bf, b_ada, norm_pre, norm_post, None, None, pool_scale,
                      a_re, a_im, log_dt, b_re_t, b_im_t, ssm_c_re[0], ssm_c_im[0], ssm_d[0], None, glu_b[0:1],
                      None, None, None, early_weight=(w_in_ride, unpack_w_in), late_weights=(late_ride, unpack_late),
                      ride_for_dw_in=exchange_small, ride_for_dh=exchange_big)

    (all_early,) = res["rode_dw_in"]
    (g_norm_post, g_pscale, g_glu_b, g_dskip, s_abar_re, s_abar_im, s_bb_re, s_bb_im, g_c_re, g_c_im) = _unpack_rows(
        _sum_devices(all_early), kept["early_shapes"], SUBLANES)
    g_a_re, g_a_im, g_log_dt, g_b_re_t, g_b_im_t = _ssm_params_bwd(
        a_re, a_im, log_dt, b_re_t, b_im_t, s_abar_re.reshape(SSM_G, SSM_P), s_abar_im.reshape(SSM_G, SSM_P),
        s_bb_re, s_bb_im)
    late_parts = [res["dmod"], res["silu_c"], res["dg1"], res["loss"].reshape(1, 1)]
    late_shapes = [p.shape for p in late_parts]
    head_rows = _part_rows(late_shapes[0], SUBLANES) + _part_rows(late_shapes[1], SUBLANES)
    all_late, sum_late = _small_allgather_sum(_pack_rows(late_parts, SUBLANES, COMM_ROW_ALIGN), head_rows, n_chunks=1)
    g_b_ada, _, g_norm_pre, loss = _unpack_rows(sum_late, late_shapes, SUBLANES)
    loss = loss[0, 0]
    dmod_all = all_late[:, 0:3].reshape(N_DEV, 3 * D_MODEL)
    dmod_cols = lax.dynamic_slice_in_dim(dmod_all, chip * n_ada, n_ada, axis=1)
    silu_t = all_late[:, _part_rows(late_shapes[0], SUBLANES)].transpose(1, 0)
    g_w_ada = _wada_grad(silu_t, dmod_cols)

    (got_chips,) = res["rode"]
    shard = _rs_join(_chip_add(kept["part_f32"], got_chips, jnp.stack([chip.astype(jnp.int32), core])))
    r = 0
    g_w_in = shard[r:r + n_in].reshape(D_MODEL, n_in)
    r += n_in
    g_squares = []
    for _ in range(4):
        g_squares.append(shard[r:r + n_row])
        r += n_row
    g_glu_w, g_wbp, g_wbs, g_wout = g_squares
    g_pool_w = shard[r:r + pool_rows].reshape(n_groups * n_pool, POOL_GW)

    big = [("w_ada", w_ada[0], g_w_ada, m_w_ada[0], v_w_ada[0]),
           ("w_in", w_in[0], g_w_in, m_w_in[0], v_w_in[0]),
           ("pool_w", pool_w[0].reshape(n_groups * n_pool, POOL_GW), g_pool_w,
            m_pool_w[0].reshape(n_groups * n_pool, POOL_GW), v_pool_w[0].reshape(n_groups * n_pool, POOL_GW)),
           ("glu_w", glu_w[0], g_glu_w, m_glu_w[0], v_glu_w[0]),
           ("w_branch_pool", w_branch_pool[0], g_wbp, m_w_branch_pool[0], v_w_branch_pool[0]),
           ("w_branch_ssm", w_branch_ssm[0], g_wbs, m_w_branch_ssm[0], v_w_branch_ssm[0]),
           ("w_out", w_out[0], g_wout, m_w_out[0], v_w_out[0])]
    out = {}
    for name, w_, g_, m_, v_ in big:
        d_, nm_, nv_ = _adamw(w_, g_, m_, v_, "adamw_" + name)
        out[name] = (g_, d_, nm_, nv_)

    g_b_re = g_b_re_t.transpose(1, 2, 0)
    g_b_im = g_b_im_t.transpose(1, 2, 0)
    small = [("b_ada", b_ada, g_b_ada, m_b_ada, v_b_ada),
             ("norm_pre", norm_pre, g_norm_pre, m_norm_pre, v_norm_pre),
             ("norm_post", norm_post, g_norm_post, m_norm_post, v_norm_post),
             ("pool_scale", pool_scale, g_pscale, m_pool_scale, v_pool_scale),
             ("ssm_a_re", ssm_a_re, g_a_re, m_ssm_a_re, v_ssm_a_re),
             ("ssm_a_im", ssm_a_im, g_a_im, m_ssm_a_im, v_ssm_a_im),
             ("ssm_log_dt", ssm_log_dt, g_log_dt, m_ssm_log_dt, v_ssm_log_dt),
             ("ssm_b_re", ssm_b_re, g_b_re, m_ssm_b_re, v_ssm_b_re),
             ("ssm_b_im", ssm_b_im, g_b_im, m_ssm_b_im, v_ssm_b_im),
             ("ssm_c_re", ssm_c_re, g_c_re, m_ssm_c_re, v_ssm_c_re),
             ("ssm_c_im", ssm_c_im, g_c_im, m_ssm_c_im, v_ssm_c_im),
             ("ssm_d", ssm_d, g_dskip, m_ssm_d, v_ssm_d),
             ("glu_b", glu_b, g_glu_b, m_glu_b, v_glu_b)]
    small = [(name, w_, g_.reshape(w_.shape), m_, v_) for name, w_, g_, m_, v_ in small]
    updates = _adamw_small([t[1:] for t in small])
    for (name, _, g_, _, _), (d_, nm_, nv_) in zip(small, updates):
        out[name] = (g_, d_, nm_, nv_)

    order = ["w_ada", "b_ada", "norm_pre", "norm_post", "w_in", "pool_w", "pool_scale", "ssm_a_re", "ssm_a_im",
             "ssm_log_dt", "ssm_b_re", "ssm_b_im", "ssm_c_re", "ssm_c_im", "ssm_d", "glu_w", "glu_b", "w_branch_pool",
             "w_branch_ssm", "w_out"]
    ref_shape = dict(w_ada=w_ada.shape, w_in=w_in.shape, pool_w=pool_w.shape, glu_w=glu_w.shape,
                     w_branch_pool=w_branch_pool.shape, w_branch_ssm=w_branch_ssm.shape, w_out=w_out.shape)
    for name, w_, _, _, _ in small:
        ref_shape[name] = w_.shape
    results = [loss, res["grad_x"][None]]
    for k in range(4):
        results += [out[name][k].reshape(ref_shape[name]) for name in order]
    return tuple(results)
```

```python
import functools
import math

import numpy as np
import jax
import jax.numpy as jnp
from jax import lax
from jax.experimental import pallas as pl
from jax.experimental.pallas import tpu as pltpu

F32 = jnp.float32
BF16 = jnp.bfloat16
MESH_ID = pl.DeviceIdType.MESH

D_MODEL = 1024
LANES = 128
SUBLANES = 8
SSM_G, SSM_P, SSM_H = 64, 64, 16
LANE_BLOCKS = D_MODEL // LANES
GROUPS_PER_BLOCK = LANES // SSM_H
STATE_W = GROUPS_PER_BLOCK * SSM_P
STATE_ALL = SSM_G * SSM_P
POOL_WINDOWS = (2, 4, 8, 16)
POOL_GW = D_MODEL // len(POOL_WINDOWS)
HALO = 16
RMS_EPS = 1e-6
N_CHIPS = 4
N_DEV = 8

SCAN_CHUNK = 1024
SCAN_BLOCKS = 1
ROW_CHUNK = 256
ROW_CHUNK_WIDE = 512
VMEM_LIMIT_BYTES = 56 * 1024 * 1024

ADAM_BLOCK_BYTES = 1 << 20
ADAM_LR, ADAM_B1, ADAM_B2, ADAM_EPS, ADAM_WD, ADAM_STEP = 0.001, 0.9, 0.999, 1e-08, 0.01, 10

_GELU_C0 = math.sqrt(2.0 / math.pi)
_GELU_C1 = 0.044715


def _cparams(*sem):
    if sem:
        return pltpu.CompilerParams(dimension_semantics=sem, vmem_limit_bytes=VMEM_LIMIT_BYTES)
    return pltpu.CompilerParams(vmem_limit_bytes=VMEM_LIMIT_BYTES)


def _sigmoid(v):
    return jax.nn.sigmoid(v)


def _silu(v):
    return v * _sigmoid(v)


def _dsilu(v):
    s = _sigmoid(v)
    return s * (1.0 + v * (1.0 - s))


def _gelu(v):
    return v * (0.5 * (1.0 + jnp.tanh(_GELU_C0 * v * (1.0 + _GELU_C1 * (v * v)))))


def _gelu_and_grad(v):
    v2 = v * v
    t = jnp.tanh(_GELU_C0 * v * (1.0 + _GELU_C1 * v2))
    half = 0.5 * (1.0 + t)
    grad = half + (0.5 * _GELU_C0) * v * (1.0 - t * t) * (1.0 + (3.0 * _GELU_C1) * v2)
    return v * half, grad


def _silu_and_grad(v):
    s = _sigmoid(v)
    return v * s, s * (1.0 + v * (1.0 - s))


def _dot(a, b):
    return lax.dot_general(a, b, (((1,), (0,)), ((), ())), preferred_element_type=F32)


def _dot_nt(a, b):
    return lax.dot_general(a, b, (((1,), (1,)), ((), ())), preferred_element_type=F32)


def _dot_tn(a, b):
    return lax.dot_general(a, b, (((0,), (0,)), ((), ())), preferred_element_type=F32)


def _acc8(v):
    return v.reshape(v.shape[0] // SUBLANES, SUBLANES, v.shape[1]).sum(axis=0)


class _Ride:
    def __init__(self, inputs, out_shapes, scratch, start, wait):
        self.inputs, self.out_shapes, self.scratch, self.start, self.wait = inputs, out_shapes, scratch, start, wait


def _mm(a_parts, b_parts, *, name, ta=False, tb=False, out_dtype=F32, bm=512, bn=512, bk=512, ride=None):
    a_parts, b_parts = list(a_parts), list(b_parts)
    if ta:
        assert len(a_parts) == 1
        k_dim, m_dim = a_parts[0].shape
    else:
        m_dim = a_parts[0].shape[0]
        k_dim = sum(a.shape[1] for a in a_parts)
    if tb:
        assert len(b_parts) == 1
        n_dim = b_parts[0].shape[0]
    else:
        n_dim = sum(b.shape[1] for b in b_parts)
    bm, bn, bk = min(bm, m_dim), min(bn, n_dim), min(bk, k_dim)
    nm, nn, nk = m_dim // bm, n_dim // bn, k_dim // bk
    a_ranges, off = [], 0
    for a in a_parts:
        cnt = (a.shape[0] if ta else a.shape[1]) // bk
        a_ranges.append((off, cnt))
        off += cnt
    b_ranges, off = [], 0
    for b in b_parts:
        cnt = (b.shape[0] if tb else b.shape[1]) // bn
        b_ranges.append((off, cnt))
        off += cnt

    def a_spec(off, cnt):
        if ta:
            return pl.BlockSpec((bk, bm), lambda i, n, k: (k, i))
        return pl.BlockSpec((bm, bk), lambda i, n, k: (i, jnp.clip(k - off, 0, cnt - 1)))

    def b_spec(off, cnt):
        if tb:
            return pl.BlockSpec((bn, bk), lambda i, n, k: (n, k))
        return pl.BlockSpec((bk, bn), lambda i, n, k: (k, jnp.clip(n - off, 0, cnt - 1)))

    na, nb = len(a_parts), len(b_parts)
    dims = (((0 if ta else 1,), (1 if tb else 0,)), ((), ()))

    def kern_single(a_ref, b_ref, o_ref):
        o_ref[...] = lax.dot_general(a_ref[...].astype(BF16), b_ref[...].astype(BF16), dims,
                                     preferred_element_type=F32).astype(out_dtype)

    if na == 1 and nb == 1 and nk == 1 and not ride:
        return pl.pallas_call(
            kern_single, name=name, grid=(nm, nn),
            in_specs=[pl.BlockSpec((bk, bm), lambda i, n: (0, i)) if ta else pl.BlockSpec((bm, bk), lambda i, n: (i, 0)),
                      pl.BlockSpec((bn, bk), lambda i, n: (n, 0)) if tb else pl.BlockSpec((bk, bn), lambda i, n: (0, n))],
            out_specs=pl.BlockSpec((bm, bn), lambda i, n: (i, n)),
            out_shape=jax.ShapeDtypeStruct((m_dim, n_dim), out_dtype),
            compiler_params=_cparams("parallel", "parallel"),
        )(a_parts[0], b_parts[0])

    n_rin = len(ride.inputs) if ride else 0
    n_rout = len(ride.out_shapes) if ride else 0

    def kern(*refs):
        a_refs, b_refs = refs[:na], refs[na:na + nb]
        rin = refs[na + nb:na + nb + n_rin]
        o_ref = refs[na + nb + n_rin]
        rout = refs[na + nb + n_rin + 1:na + nb + n_rin + 1 + n_rout]
        acc = refs[na + nb + n_rin + 1 + n_rout]
        rsem = refs[na + nb + n_rin + 2 + n_rout:]
        i, n, k = pl.program_id(0), pl.program_id(1), pl.program_id(2)

        if ride:
            @pl.when((i == 0) & (n == 0) & (k == 0))
            def _():
                ride.start(rin, rout, rsem)

        if nk > 1:
            @pl.when(k == 0)
            def _():
                acc[...] = jnp.zeros_like(acc)

        for ja, (koff, kcnt) in enumerate(a_ranges):
            for jb, (noff, ncnt) in enumerate(b_ranges):
                def step(ja=ja, jb=jb):
                    a = a_refs[ja][...].astype(BF16)
                    b = b_refs[jb][...].astype(BF16)
                    prod = lax.dot_general(a, b, dims, preferred_element_type=F32)
                    if nk > 1:
                        acc[...] += prod
                    else:
                        o_ref[...] = prod.astype(out_dtype)

                if na == 1 and nb == 1:
                    step()
                else:
                    cond = (k >= koff) & (k < koff + kcnt) & (n >= noff) & (n < noff + ncnt)
                    pl.when(cond)(step)

        if nk > 1:
            @pl.when(k == nk - 1)
            def _():
                o_ref[...] = acc[...].astype(out_dtype)

        if ride:
            @pl.when((i == nm - 1) & (n == nn - 1) & (k == nk - 1))
            def _():
                ride.wait(rin, rout, rsem)

    any_spec = pl.BlockSpec(memory_space=pl.ANY)
    out_spec = pl.BlockSpec((bm, bn), lambda i, n, k: (i, n))
    out_shape = jax.ShapeDtypeStruct((m_dim, n_dim), out_dtype)
    acc_shape = pltpu.VMEM((bm, bn) if nk > 1 else (SUBLANES, LANES), F32)
    if not ride:
        return pl.pallas_call(
            kern, name=name, grid=(nm, nn, nk),
            in_specs=[a_spec(*r) for r in a_ranges] + [b_spec(*r) for r in b_ranges],
            out_specs=out_spec, out_shape=out_shape, scratch_shapes=[acc_shape],
            compiler_params=_cparams("parallel", "parallel", "arbitrary"),
        )(*a_parts, *b_parts)
    return pl.pallas_call(
        kern, name=name, grid=(nm, nn, nk),
        in_specs=[a_spec(*r) for r in a_ranges] + [b_spec(*r) for r in b_ranges] + [any_spec] * n_rin,
        out_specs=(out_spec,) + (any_spec,) * n_rout, out_shape=(out_shape,) + tuple(ride.out_shapes),
        scratch_shapes=[acc_shape] + list(ride.scratch),
        compiler_params=_cparams("arbitrary", "arbitrary", "arbitrary"),
    )(*a_parts, *b_parts, *ride.inputs)


def _ssm_param_fn(a_re, a_im, log_dt, b_re, b_im):
    dt = jnp.exp(log_dt)
    lam_re = jnp.minimum(a_re, -1e-4)
    lam_im = a_im
    mag = jnp.exp(lam_re * dt)
    abar_re = mag * jnp.cos(lam_im * dt)
    abar_im = mag * jnp.sin(lam_im * dt)
    den = lam_re * lam_re + lam_im * lam_im
    num_re = abar_re - 1.0
    f_re = (num_re * lam_re + abar_im * lam_im) / den
    f_im = (abar_im * lam_re - num_re * lam_im) / den
    bb_re = f_re * b_re - f_im * b_im
    bb_im = f_re * b_im + f_im * b_re
    return abar_re, abar_im, bb_re, bb_im


def _ssm_params(a_re, a_im, log_dt, b_re_t, b_im_t):
    def kern(are, aim, ldt, bre, bim, o_ar, o_ai, o_br, o_bi):
        ar, ai, br, bi = _ssm_param_fn(are[...], aim[...], ldt[...], bre[...], bim[...])
        o_ar[...] = ar
        o_ai[...] = ai
        o_br[...] = br
        o_bi[...] = bi

    gp = jax.ShapeDtypeStruct((SSM_G, SSM_P), F32)
    hgp = jax.ShapeDtypeStruct((SSM_H, SSM_G, SSM_P), F32)
    return pl.pallas_call(kern, name="ssm_params", out_shape=(gp, gp, hgp, hgp), compiler_params=_cparams())(
        a_re, a_im, log_dt, b_re_t, b_im_t)


def _ssm_params_bwd(a_re, a_im, log_dt, b_re_t, b_im_t, d_ar, d_ai, d_bbr, d_bbi):
    def kern(are, aim, ldt, bre, bim, dar, dai, dbr, dbi, o_are, o_aim, o_ldt, o_bre, o_bim):
        prim = (are[...], aim[...], ldt[...], bre[...], bim[...])
        _, vjp = jax.vjp(_ssm_param_fn, *prim)
        g = vjp((dar[...], dai[...], dbr[...], dbi[...]))
        o_are[...] = g[0]
        o_aim[...] = g[1]
        o_ldt[...] = g[2]
        o_bre[...] = g[3]
        o_bim[...] = g[4]

    gp = jax.ShapeDtypeStruct((SSM_G, SSM_P), F32)
    g1 = jax.ShapeDtypeStruct((SSM_G, 1), F32)
    hgp = jax.ShapeDtypeStruct((SSM_H, SSM_G, SSM_P), F32)
    return pl.pallas_call(kern, name="ssm_params_bwd", out_shape=(gp, gp, g1, hgp, hgp), compiler_params=_cparams())(
        a_re, a_im, log_dt, b_re_t, b_im_t, d_ar, d_ai, d_bbr, d_bbi)


def _pow_tables(abar_re, abar_im, tc):
    ls = tc // SUBLANES

    def kern(ar_ref, ai_ref, fr_ref, fi_ref, rr_ref, ri_ref):
        a_re = jnp.broadcast_to(ar_ref[...], (SUBLANES, STATE_W))
        a_im = jnp.broadcast_to(ai_ref[...], (SUBLANES, STATE_W))
        p_re, p_im = a_re, a_im
        for i in range(ls):
            fwd = pl.ds(SUBLANES * i, SUBLANES)
            rev = pl.ds(SUBLANES * (ls - 1 - i), SUBLANES)
            fr_ref[fwd, :] = p_re
            fi_ref[fwd, :] = p_im
            rr_ref[rev, :] = p_re
            ri_ref[rev, :] = p_im
            p_re, p_im = p_re * a_re - p_im * a_im, p_re * a_im + p_im * a_re

    vec = pl.BlockSpec((1, STATE_W), lambda b: (0, b))
    tab = pl.BlockSpec((tc, STATE_W), lambda b: (0, b))
    shp = jax.ShapeDtypeStruct((tc, STATE_ALL), F32)
    return pl.pallas_call(
        kern, name="pow_tables", grid=(LANE_BLOCKS,), in_specs=[vec, vec], out_specs=(tab, tab, tab, tab),
        out_shape=(shp, shp, shp, shp), compiler_params=_cparams("parallel"))(abar_re, abar_im)


def _mod_kernel(c_row, w_ada_bf, b_ada):
    def kern(c_ref, w_ref, b_ref, m_ref, s_ref):
        cv = c_ref[...]
        sc = _silu(cv)
        s_ref[...] = sc
        lhs = jnp.broadcast_to(sc, (SUBLANES, D_MODEL)).astype(BF16)
        m_ref[...] = _dot(lhs, w_ref[...]) + b_ref[...]

    return pl.pallas_call(
        kern, name="ada_mod",
        out_shape=(jax.ShapeDtypeStruct((SUBLANES, 3 * D_MODEL), F32), jax.ShapeDtypeStruct((1, D_MODEL), F32)),
        compiler_params=_cparams())(c_row, w_ada_bf, b_ada)


def _row_spec(tr, width=D_MODEL, col=0):
    return pl.BlockSpec((tr, width), lambda c: (c, col))


def _vec_spec(width=D_MODEL):
    return pl.BlockSpec((1, width), lambda c: (0, 0))


def _col_spec(tr):
    return pl.BlockSpec((D_MODEL, tr), lambda c: (0, c))


def _in_norm(x, g1, scale, shift, ride=None):
    seq = x.shape[0]
    tr = min(ROW_CHUNK_WIDE, seq)
    nc = seq // tr
    n_rin = len(ride.inputs) if ride else 0
    n_rout = len(ride.out_shapes) if ride else 0

    def kern(x_ref, g_ref, sc_ref, sh_ref, *rest):
        rin, (h_ref, ht_ref) = rest[:n_rin], rest[n_rin:n_rin + 2]
        rout, rsem = rest[n_rin + 2:n_rin + 2 + n_rout], rest[n_rin + 2 + n_rout:]
        c = pl.program_id(0)
        if ride:
            @pl.when(c == 0)
            def _():
                ride.start(rin, rout, rsem)

        xv = x_ref[...]
        r = lax.rsqrt(jnp.mean(xv * xv, axis=-1, keepdims=True) + RMS_EPS)
        h = ((xv * r) * g_ref[...]) * (1.0 + sc_ref[...]) + sh_ref[...]
        h_ref[...] = h.astype(BF16)
        ht_ref[...] = h.T.astype(BF16)

        if ride:
            @pl.when(c == nc - 1)
            def _():
                ride.wait(rin, rout, rsem)

    outs = pl.pallas_call(
        kern, name="in_norm", grid=(nc,),
        in_specs=[_row_spec(tr), _vec_spec(), _vec_spec(), _vec_spec()] + [_ANY] * n_rin,
        out_specs=(_row_spec(tr), _col_spec(tr)) + (_ANY,) * n_rout,
        out_shape=(jax.ShapeDtypeStruct((seq, D_MODEL), BF16), jax.ShapeDtypeStruct((D_MODEL, seq), BF16))
        + tuple(ride.out_shapes if ride else ()),
        scratch_shapes=list(ride.scratch) if ride else [],
        compiler_params=_cparams("arbitrary" if ride else "parallel"))(x, g1, scale, shift, *(ride.inputs if ride else ()))
    return outs


PAD = SUBLANES


def _window_sums(src, cols, w, bufs, rows, ahead):
    cur, cur_cols, step, k = src, cols, 1, 0
    data = pl.ds(PAD, rows)
    while step < w:
        dst = bufs[k % 2]
        dst[data, :] = cur[data, cur_cols] + cur[pl.ds(PAD + (step if ahead else -step), rows), cur_cols]
        cur, cur_cols, step, k = dst, slice(None), 2 * step, k + 1
    return cur, cur_cols


def _pool_windows(ext, bufs, pos, g, w, tr):
    cols = pl.ds(g * POOL_GW, POOL_GW)
    chunk = pl.ds(PAD + HALO, tr)
    cur = ext[chunk, cols]
    win, win_cols = _window_sums(ext, cols, w, bufs, HALO + tr, ahead=False)
    cnt = jnp.minimum(pos + 1, w).astype(F32)
    return win[chunk, win_cols] / cnt - cur


def _zero_pads(refs, rows):
    for ref in refs:
        ref[0:PAD, :] = jnp.zeros((PAD, ref.shape[1]), F32)
        ref[PAD + rows:, :] = jnp.zeros((PAD, ref.shape[1]), F32)


def _pool_fwd(proj, pool_w_bf, pscale):
    seq = proj.shape[0]
    tr = min(ROW_CHUNK_WIDE, seq)
    hb = tr // HALO

    def kern(up_ref, halo_ref, zp_ref, pw_ref, ps_ref, y_ref, yt_ref, ext, buf_a, buf_b):
        c = pl.program_id(0)
        _zero_pads((ext, buf_a, buf_b), HALO + tr)
        ext[pl.ds(PAD, HALO), :] = jnp.where(c > 0, halo_ref[...].astype(F32), 0.0)
        ext[pl.ds(PAD + HALO, tr), :] = up_ref[...].astype(F32)
        pos = c * tr + lax.broadcasted_iota(jnp.int32, (tr, POOL_GW), 0)
        for g, w in enumerate(POOL_WINDOWS):
            cols = pl.ds(g * POOL_GW, POOL_GW)
            pooled = _pool_windows(ext, (buf_a, buf_b), pos, g, w, tr)
            mixed = _dot(pooled.astype(BF16), pw_ref[g])
            y = mixed * ps_ref[:, cols] * _silu(zp_ref[:, cols].astype(F32))
            y_ref[:, cols] = y.astype(BF16)
            yt_ref[cols, :] = y.T.astype(BF16)

    return pl.pallas_call(
        kern, name="pool_fwd", grid=(seq // tr,),
        in_specs=[_row_spec(tr, col=0),
                  pl.BlockSpec((HALO, D_MODEL), lambda c: (jnp.maximum(c * hb - 1, 0), 0)),
                  _row_spec(tr, col=1),
                  pl.BlockSpec((len(POOL_WINDOWS), POOL_GW, POOL_GW), lambda c: (0, 0, 0)),
                  _vec_spec()],
        out_specs=(_row_spec(tr), _col_spec(tr)),
        out_shape=(jax.ShapeDtypeStruct((seq, D_MODEL), BF16), jax.ShapeDtypeStruct((D_MODEL, seq), BF16)),
        scratch_shapes=[pltpu.VMEM((tr + HALO + 2 * PAD, D_MODEL), F32), pltpu.VMEM((tr + HALO + 2 * PAD, POOL_GW), F32),
                        pltpu.VMEM((tr + HALO + 2 * PAD, POOL_GW), F32)],
        compiler_params=_cparams("parallel"))(proj, proj, proj, pool_w_bf, pscale)


def _pool_bwd(proj, dyp, pool_w_bf, pscale, dproj):
    seq = proj.shape[0]
    tr = min(ROW_CHUNK_WIDE, seq)
    hb = tr // HALO
    nc = seq // tr
    n_halo = seq // HALO

    def kern(up_ref, halo_ref, zp_ref, zpn_ref, dyp_ref, dypn_ref, pw_ref, ps_ref, _,
             d01_ref, dpw_ref, dps_ref, ext, dpn, buf_a, buf_b, acc_pw, acc_ps):
        c = pl.program_id(0)

        @pl.when(c == 0)
        def _():
            acc_pw[...] = jnp.zeros_like(acc_pw)
            acc_ps[...] = jnp.zeros_like(acc_ps)

        _zero_pads((ext, dpn, buf_a, buf_b), HALO + tr)
        ext[pl.ds(PAD, HALO), :] = jnp.where(c > 0, halo_ref[...].astype(F32), 0.0)
        ext[pl.ds(PAD + HALO, tr), :] = up_ref[...].astype(F32)
        pos = c * tr + lax.broadcasted_iota(jnp.int32, (tr, POOL_GW), 0)
        pos_n = (c + 1) * tr + lax.broadcasted_iota(jnp.int32, (HALO, POOL_GW), 0)
        has_next = c < nc - 1
        for g, w in enumerate(POOL_WINDOWS):
            cols = pl.ds(g * POOL_GW, POOL_GW)
            pooled_bf = _pool_windows(ext, (buf_a, buf_b), pos, g, w, tr).astype(BF16)
            wg = pw_ref[g]
            mixed = _dot(pooled_bf, wg)
            zp = zp_ref[:, cols].astype(F32)
            sz = _silu(zp)
            dyp_g = dyp_ref[:, cols].astype(F32)
            ps = ps_ref[:, cols]
            dmixed = (dyp_g * ps * sz).astype(BF16)
            acc_ps[:, cols] += _acc8(dyp_g * mixed * sz)
            d01_ref[:, pl.ds(D_MODEL + g * POOL_GW, POOL_GW)] = (dyp_g * mixed * ps * _dsilu(zp)).astype(BF16)
            acc_pw[g] += _dot_tn(pooled_bf, dmixed)
            dpooled = _dot_nt(dmixed, wg)
            dmixed_n = (jnp.where(has_next, dypn_ref[:, cols].astype(F32), 0.0) * ps * _silu(zpn_ref[:, cols].astype(F32))).astype(BF16)
            dpooled_n = _dot_nt(dmixed_n, wg)
            dpn[pl.ds(PAD, tr), :] = dpooled / jnp.minimum(pos + 1, w).astype(F32)
            dpn[pl.ds(PAD + tr, HALO), :] = dpooled_n / jnp.minimum(pos_n + 1, w).astype(F32)
            win, _ = _window_sums(dpn, slice(None), w, (buf_a, buf_b), tr + HALO, ahead=True)
            d01_ref[:, cols] = (win[pl.ds(PAD, tr), :] - dpooled).astype(BF16)

        @pl.when(c == nc - 1)
        def _():
            dpw_ref[...] = acc_pw[...]
            dps_ref[...] = jnp.sum(acc_ps[...], axis=0, keepdims=True)

    nxt = lambda c: (jnp.minimum((c + 1) * hb, n_halo - 1), 0)
    nxt1 = lambda c: (jnp.minimum((c + 1) * hb, n_halo - 1), 1)
    return pl.pallas_call(
        kern, name="pool_bwd", grid=(nc,),
        in_specs=[_row_spec(tr, col=0),
                  pl.BlockSpec((HALO, D_MODEL), lambda c: (jnp.maximum(c * hb - 1, 0), 0)),
                  _row_spec(tr, col=1),
                  pl.BlockSpec((HALO, D_MODEL), nxt1),
                  _row_spec(tr),
                  pl.BlockSpec((HALO, D_MODEL), nxt),
                  pl.BlockSpec((len(POOL_WINDOWS), POOL_GW, POOL_GW), lambda c: (0, 0, 0)),
                  _vec_spec(), _ANY],
        out_specs=(pl.BlockSpec((tr, 2 * D_MODEL), lambda c: (c, 0)),
                   pl.BlockSpec((len(POOL_WINDOWS), POOL_GW, POOL_GW), lambda c: (0, 0, 0)),
                   _vec_spec()),
        out_shape=(jax.ShapeDtypeStruct(dproj.shape, BF16),
                   jax.ShapeDtypeStruct((len(POOL_WINDOWS), POOL_GW, POOL_GW), F32),
                   jax.ShapeDtypeStruct((1, D_MODEL), F32)),
        scratch_shapes=[pltpu.VMEM((tr + HALO + 2 * PAD, D_MODEL), F32)]
        + [pltpu.VMEM((tr + HALO + 2 * PAD, POOL_GW), F32)] * 3
        + [pltpu.VMEM((len(POOL_WINDOWS), POOL_GW, POOL_GW), F32), pltpu.VMEM((SUBLANES, D_MODEL), F32)],
        input_output_aliases={8: 0},
        compiler_params=_cparams("arbitrary"))(proj, proj, proj, proj, dyp, dyp, pool_w_bf, pscale, dproj)


def _glu_fwd(ys, proj, glu_w_bf, glu_b):
    seq = ys.shape[0]
    tr = min(ROW_CHUNK_WIDE, seq)

    def kern(ys_ref, zs_ref, w_ref, b_ref, o_ref, ot_ref):
        yg = _gelu(ys_ref[...])
        q = _dot(yg.astype(BF16), w_ref[...]) + b_ref[...]
        y = yg * _sigmoid(q) * _silu(zs_ref[...].astype(F32))
        o_ref[...] = y.astype(BF16)
        ot_ref[...] = y.T.astype(BF16)

    return pl.pallas_call(
        kern, name="glu_fwd", grid=(seq // tr,),
        in_specs=[_row_spec(tr), _row_spec(tr, col=3), pl.BlockSpec((D_MODEL, D_MODEL), lambda c: (0, 0)), _vec_spec()],
        out_specs=(_row_spec(tr), _col_spec(tr)),
        out_shape=(jax.ShapeDtypeStruct((seq, D_MODEL), BF16), jax.ShapeDtypeStruct((D_MODEL, seq), BF16)),
        compiler_params=_cparams("parallel"))(ys, proj, glu_w_bf, glu_b)


def _glu_bwd(ys, proj, dyssm, glu_w_bf, glu_b, dproj):
    seq = ys.shape[0]
    tr = min(ROW_CHUNK_WIDE, seq)
    nc = seq // tr

    def kern(ys_ref, zs_ref, dy_ref, w_ref, b_ref, _, dys_ref, dzs_ref, dq_ref, yg_ref, db_ref, acc_b):
        c = pl.program_id(0)

        @pl.when(c == 0)
        def _():
            acc_b[...] = jnp.zeros_like(acc_b)

        yg, dgelu = _gelu_and_grad(ys_ref[...])
        yg_bf = yg.astype(BF16)
        q = _dot(yg_bf, w_ref[...]) + b_ref[...]
        sg = _sigmoid(q)
        silu_z, dsilu_z = _silu_and_grad(zs_ref[...].astype(F32))
        dyv = dy_ref[...].astype(F32)
        dyglu = dyv * silu_z
        yglu = yg * sg
        dzs_ref[...] = (dyv * yglu * dsilu_z).astype(BF16)
        dq = dyglu * yglu * (1.0 - sg)
        dq_bf = dq.astype(BF16)
        acc_b[...] += _acc8(dq)
        dyg = dyglu * sg + _dot_nt(dq_bf, w_ref[...])
        dys_ref[...] = dyg * dgelu
        dq_ref[...] = dq_bf
        yg_ref[...] = yg.T.astype(BF16)

        @pl.when(c == nc - 1)
        def _():
            db_ref[...] = jnp.sum(acc_b[...], axis=0, keepdims=True)

    bf = jax.ShapeDtypeStruct((seq, D_MODEL), BF16)
    return pl.pallas_call(
        kern, name="glu_bwd", grid=(nc,),
        in_specs=[_row_spec(tr), _row_spec(tr, col=3), _row_spec(tr),
                  pl.BlockSpec((D_MODEL, D_MODEL), lambda c: (0, 0)), _vec_spec(), _ANY],
        out_specs=(_row_spec(tr), _row_spec(tr, col=3), _row_spec(tr), _col_spec(tr), _vec_spec()),
        out_shape=(jax.ShapeDtypeStruct((seq, D_MODEL), F32), jax.ShapeDtypeStruct(dproj.shape, BF16), bf,
                   jax.ShapeDtypeStruct((D_MODEL, seq), BF16), jax.ShapeDtypeStruct((1, D_MODEL), F32)),
        scratch_shapes=[pltpu.VMEM((SUBLANES, D_MODEL), F32)],
        input_output_aliases={5: 1},
        compiler_params=_cparams("arbitrary"))(ys, proj, dyssm, glu_w_bf, glu_b, dproj)


def _out_fwd_bwd(ypool, yssm, proj, x, tgt, gate, g2, wbp_bf, wbs_bf, wout_bf):
    seq = x.shape[0]
    tr = min(ROW_CHUNK, seq)
    nc = seq // tr

    def kern(yp_ref, ysm_ref, gp_ref, gs_ref, x_ref, t_ref, gate_ref, g2_ref, wbp_ref, wbs_ref, wo_ref,
             dy_ref, dyp_ref, dys_ref, d45_ref, mb_ref, dob_ref, dbp_ref, dbs_ref, loss_ref, dgate_ref, dg2_ref,
             acc_l, acc_gate, acc_g2):
        c = pl.program_id(0)

        @pl.when(c == 0)
        def _():
            acc_l[...] = jnp.zeros_like(acc_l)
            acc_gate[...] = jnp.zeros_like(acc_gate)
            acc_g2[...] = jnp.zeros_like(acc_g2)

        bp = _dot(yp_ref[...], wbp_ref[...])
        bs = _dot(ysm_ref[...], wbs_ref[...])
        sp = _sigmoid(gp_ref[...].astype(F32))
        ss = _sigmoid(gs_ref[...].astype(F32))
        merged = sp * bp + ss * bs
        mb = merged.astype(BF16)
        out = _dot(mb, wo_ref[...])
        r2 = lax.rsqrt(jnp.mean(out * out, axis=-1, keepdims=True) + RMS_EPS)
        oh = out * r2
        gate_v, g2_v = gate_ref[...], g2_ref[...]
        ohg = oh * g2_v
        diff = (x_ref[...] + gate_v * ohg) - t_ref[...]
        acc_l[...] += _acc8(diff * diff)
        dyv = diff * (1.0 / D_MODEL)
        dy_ref[...] = dyv
        dy_oh = dyv * oh
        acc_gate[...] += _acc8(dy_oh * g2_v)
        acc_g2[...] += _acc8(dy_oh * gate_v)
        gg = gate_v * g2_v
        doh = dyv * gg
        dout = r2 * (doh - oh * jnp.mean(dy_oh * gg, axis=-1, keepdims=True))
        dob = dout.astype(BF16)
        dmerged = _dot_nt(dob, wo_ref[...])
        dbp_f = dmerged * sp
        dbs_f = dmerged * ss
        dbp = dbp_f.astype(BF16)
        dbs = dbs_f.astype(BF16)
        d45_ref[:, 0:D_MODEL] = (dbp_f * bp * (1.0 - sp)).astype(BF16)
        d45_ref[:, D_MODEL:] = (dbs_f * bs * (1.0 - ss)).astype(BF16)
        dyp_ref[...] = _dot_nt(dbp, wbp_ref[...]).astype(BF16)
        dys_ref[...] = _dot_nt(dbs, wbs_ref[...]).astype(BF16)
        mb_ref[...] = merged.T.astype(BF16)
        dob_ref[...] = dob
        dbp_ref[...] = dbp
        dbs_ref[...] = dbs

        @pl.when(c == nc - 1)
        def _():
            tot = jnp.sum(acc_l[...], axis=0, keepdims=True)
            loss_ref[...] = jnp.sum(tot, axis=1, keepdims=True) * (0.5 / D_MODEL)
            dgate_ref[...] = jnp.sum(acc_gate[...], axis=0, keepdims=True)
            dg2_ref[...] = jnp.sum(acc_g2[...], axis=0, keepdims=True)

    wspec = pl.BlockSpec((D_MODEL, D_MODEL), lambda c: (0, 0))
    f32 = jax.ShapeDtypeStruct((seq, D_MODEL), F32)
    bf = jax.ShapeDtypeStruct((seq, D_MODEL), BF16)
    vec = jax.ShapeDtypeStruct((1, D_MODEL), F32)
    acc = pltpu.VMEM((SUBLANES, D_MODEL), F32)
    return pl.pallas_call(
        kern, name="out_fwd_bwd", grid=(nc,),
        in_specs=[_row_spec(tr), _row_spec(tr), _row_spec(tr, col=4), _row_spec(tr, col=5), _row_spec(tr), _row_spec(tr),
                  _vec_spec(), _vec_spec(), wspec, wspec, wspec],
        out_specs=(_row_spec(tr), _row_spec(tr), _row_spec(tr), pl.BlockSpec((tr, 2 * D_MODEL), lambda c: (c, 2)),
                   _col_spec(tr), _row_spec(tr), _row_spec(tr), _row_spec(tr),
                   pl.BlockSpec((1, 1), lambda c: (0, 0)), _vec_spec(), _vec_spec()),
        out_shape=(f32, bf, bf, jax.ShapeDtypeStruct((seq, proj.shape[1]), BF16),
                   jax.ShapeDtypeStruct((D_MODEL, seq), BF16), bf, bf, bf,
                   jax.ShapeDtypeStruct((1, 1), F32), vec, vec),
        scratch_shapes=[acc, acc, acc],
        compiler_params=_cparams("arbitrary"))(ypool, yssm, proj, proj, x, tgt, gate, g2, wbp_bf, wbs_bf, wout_bf)


def _in_bwd(dh, x, dy, g1, scale):
    seq = x.shape[0]
    tr = min(ROW_CHUNK_WIDE, seq)
    nc = seq // tr

    def kern(dh_ref, x_ref, dy_ref, g_ref, sc_ref, dx_ref, dsh_ref, dsc_ref, dg_ref, a_sh, a_sc, a_g):
        c = pl.program_id(0)

        @pl.when(c == 0)
        def _():
            a_sh[...] = jnp.zeros_like(a_sh)
            a_sc[...] = jnp.zeros_like(a_sc)
            a_g[...] = jnp.zeros_like(a_g)

        xv = x_ref[...]
        r = lax.rsqrt(jnp.mean(xv * xv, axis=-1, keepdims=True) + RMS_EPS)
        xh = xv * r
        g = g_ref[...]
        dhv = dh_ref[...]
        a_sh[...] += _acc8(dhv)
        a_sc[...] += _acc8(dhv * (xh * g))
        dn = dhv * (1.0 + sc_ref[...])
        a_g[...] += _acc8(dn * xh)
        dxh = dn * g
        dx_ref[...] = dy_ref[...] + r * (dxh - xh * jnp.mean(dxh * xh, axis=-1, keepdims=True))

        @pl.when(c == nc - 1)
        def _():
            dsh_ref[...] = jnp.sum(a_sh[...], axis=0, keepdims=True)
            dsc_ref[...] = jnp.sum(a_sc[...], axis=0, keepdims=True)
            dg_ref[...] = jnp.sum(a_g[...], axis=0, keepdims=True)

    vec = jax.ShapeDtypeStruct((1, D_MODEL), F32)
    acc = pltpu.VMEM((SUBLANES, D_MODEL), F32)
    return pl.pallas_call(
        kern, name="in_bwd", grid=(nc,),
        in_specs=[_row_spec(tr), _row_spec(tr), _row_spec(tr), _vec_spec(), _vec_spec()],
        out_specs=(_row_spec(tr), _vec_spec(), _vec_spec(), _vec_spec()),
        out_shape=(jax.ShapeDtypeStruct((seq, D_MODEL), F32), vec, vec, vec),
        scratch_shapes=[acc, acc, acc],
        compiler_params=_cparams("arbitrary"))(dh, x, dy, g1, scale)


SLAB = 2 * SUBLANES


def _local_scan(a_re, a_im, br, bi, xr, xi, row0, ls, reverse, init=None, xb=None):
    if init is None:
        x_re = jnp.zeros((SUBLANES, STATE_W), F32)
        x_im = jnp.zeros((SUBLANES, STATE_W), F32)
    else:
        x_re, x_im = init
    for i in (range(ls - 1, -1, -1) if reverse else range(ls)):
        src = pl.ds(SUBLANES * i, SUBLANES)
        dst = pl.ds(row0 + SUBLANES * i, SUBLANES)
        n_re = a_re * x_re - a_im * x_im + br[src, :]
        n_im = a_re * x_im + a_im * x_re + bi[src, :]
        if xb is not None and i % 2 == 1:
            pair = pl.ds(SUBLANES * (i - 1), SLAB)
            xb[0][pair, :] = jnp.concatenate([x_re, n_re], axis=0).astype(BF16)
            xb[1][pair, :] = jnp.concatenate([x_im, n_im], axis=0).astype(BF16)
        x_re, x_im = n_re, n_im
        xr[dst, :] = x_re
        xi[dst, :] = x_im
    return x_re, x_im


def _two(v):
    return jnp.concatenate([v, v], axis=0)


def _unpermute_rhs(v, sel):
    hi = v.astype(BF16)
    r1 = v - hi.astype(F32)
    mid = r1.astype(BF16)
    lo = (r1 - mid.astype(F32)).astype(BF16)
    return _dot(hi, sel) + _dot(mid, sel) + _dot(lo, sel)


def _scan_specs(tc, nb, rows_of):
    return dict(
        us=pl.BlockSpec((tc, nb * LANES), lambda b, c: (rows_of(c), 2 * D_MODEL // (nb * LANES) + b)),
        tok=pl.BlockSpec((tc, nb * LANES), lambda b, c: (rows_of(c), b)),
        bblk=pl.BlockSpec((nb, LANES, STATE_W), lambda b, c: (b, 0, 0)),
        cblk=pl.BlockSpec((nb, STATE_W, LANES), lambda b, c: (b, 0, 0)),
        vec=pl.BlockSpec((1, nb * STATE_W), lambda b, c: (0, b)),
        tab=pl.BlockSpec((tc, nb * STATE_W), lambda b, c: (0, b)),
        car=pl.BlockSpec((SUBLANES, nb * STATE_W), lambda b, c: (rows_of(c), b)),
        dvec=pl.BlockSpec((1, nb * LANES), lambda b, c: (0, b)))


def _ssm_scan_fwd(proj, bb_re, bb_im, cm_re, cm_im, abar_re, abar_im, pw_re, pw_im, d_skip, tc):
    seq = proj.shape[0]
    nc = seq // tc
    ls = tc // SUBLANES
    nb = SCAN_BLOCKS

    def kern(us_ref, bbr_ref, bbi_ref, cmr_ref, cmi_ref, ar_ref, ai_ref, pwr_ref, pwi_ref, d_ref,
             ys_ref, ecr_ref, eci_ref, bur, bui, car_r, car_i, end_r, end_i, upb, xb_r, xb_i, *nat):
        c = pl.program_id(1)

        @pl.when(c == 0)
        def _():
            car_r[...] = jnp.zeros_like(car_r)
            car_i[...] = jnp.zeros_like(car_i)

        for j in range(nb):
            cols = pl.ds(j * LANES, LANES)
            scols = pl.ds(j * STATE_W, STATE_W)
            nat[j][...] = us_ref[:, cols].astype(F32)
            for i in range(ls):
                upb[j, pl.ds(SUBLANES * i, SUBLANES), :] = nat[j][pl.ds(i, SUBLANES, stride=ls), :]
            u = upb[j]
            up = u.astype(BF16)
            bur[j] = _dot(up, bbr_ref[j])
            bui[j] = _dot(up, bbi_ref[j])
            a_re = jnp.broadcast_to(ar_ref[:, scols], (SUBLANES, STATE_W))
            a_im = jnp.broadcast_to(ai_ref[:, scols], (SUBLANES, STATE_W))
            x_re, x_im = _local_scan(a_re, a_im, bur.at[j], bui.at[j], bur.at[j], bui.at[j], 0, ls, False)
            end_r[j] = x_re
            end_i[j] = x_im
            big_re = pwr_ref[tc - 1:tc, scols]
            big_im = pwi_ref[tc - 1:tc, scols]
            e_re = car_r[j, 0:1, :]
            e_im = car_i[j, 0:1, :]
            for s in range(SUBLANES):
                n_re = end_r[j, s:s + 1, :] + big_re * e_re - big_im * e_im
                n_im = end_i[j, s:s + 1, :] + big_re * e_im + big_im * e_re
                e_re, e_im = n_re, n_im
                if s < SUBLANES - 1:
                    car_r[j, s + 1:s + 2, :] = e_re
                    car_i[j, s + 1:s + 2, :] = e_im
            ec_re = car_r[j]
            ec_im = car_i[j]
            ecr_ref[:, scols] = ec_re
            eci_ref[:, scols] = ec_im
            e2_re, e2_im = _two(ec_re), _two(ec_im)
            for k in range(tc // SLAB):
                rows_k = pl.ds(SLAB * k, SLAB)
                p_re = pwr_ref[rows_k, scols]
                p_im = pwi_ref[rows_k, scols]
                xb_r[j, rows_k, :] = (bur[j, rows_k, :] + p_re * e2_re - p_im * e2_im).astype(BF16)
                xb_i[j, rows_k, :] = (bui[j, rows_k, :] + p_re * e2_im + p_im * e2_re).astype(BF16)
            upb[j] = _dot(xb_r[j], cmr_ref[j]) - _dot(xb_i[j], cmi_ref[j]) + d_ref[:, cols] * u
            for i in range(ls):
                nat[j][pl.ds(i, SUBLANES, stride=ls), :] = upb[j, pl.ds(SUBLANES * i, SUBLANES), :]
            ys_ref[:, cols] = nat[j][...]
            car_r[j, 0:1, :] = e_re
            car_i[j, 0:1, :] = e_im

    sp = _scan_specs(tc, nb, lambda c: c)
    carry_shape = jax.ShapeDtypeStruct((nc * SUBLANES, STATE_ALL), F32)
    small = pltpu.VMEM((nb, SUBLANES, STATE_W), F32)
    big = pltpu.VMEM((nb, tc, STATE_W), F32)
    return pl.pallas_call(
        kern, name="ssm_scan_fwd", grid=(LANE_BLOCKS // nb, nc),
        in_specs=[sp["us"], sp["bblk"], sp["bblk"], sp["cblk"], sp["cblk"], sp["vec"], sp["vec"], sp["tab"], sp["tab"],
                  sp["dvec"]],
        out_specs=(sp["tok"], sp["car"], sp["car"]),
        out_shape=(jax.ShapeDtypeStruct((seq, D_MODEL), F32), carry_shape, carry_shape),
        scratch_shapes=[big, big, small, small, small, small, pltpu.VMEM((nb, tc, LANES), F32),
                        pltpu.VMEM((nb, tc, STATE_W), BF16), pltpu.VMEM((nb, tc, STATE_W), BF16)]
        + [pltpu.VMEM((tc, LANES), F32)] * nb,
        compiler_params=_cparams("parallel", "arbitrary"),
    )(proj, bb_re, bb_im, cm_re, cm_im, abar_re, abar_im, pw_re, pw_im, d_skip)


def _ssm_scan_bwd(proj, dys, ec_re, ec_im, bb_re, bb_im, cm_re, cm_im, abar_re, abar_im,
                  pw_re, pw_im, pv_re, pv_im, d_skip, dproj, tc):
    seq = proj.shape[0]
    nc = seq // tc
    ls = tc // SUBLANES
    nb = SCAN_BLOCKS

    def kern(us_ref, dys_ref, ecr_ref, eci_ref, bbr_ref, bbi_ref, cmr_ref, cmi_ref, ar_ref, ai_ref,
             pwr_ref, pwi_ref, pvr_ref, pvi_ref, d_ref, _,
             dus_ref, dbbr_ref, dbbi_ref, dcmr_ref, dcmi_ref, dar_ref, dai_ref, dd_ref,
             bur, bui, xr, xi, gr, gi, fc_r, fc_i, a_bbr, a_bbi, a_cmr, a_cmi, a_ar, a_ai, a_dd, upb, dpb, hb_r, hb_i,
             *nat):
        c = pl.program_id(1)

        @pl.when(c == 0)
        def _():
            for ref in (fc_r, fc_i, a_bbr, a_bbi, a_cmr, a_cmi, a_ar, a_ai, a_dd):
                ref[...] = jnp.zeros_like(ref)

        for j in range(nb):
            cols = pl.ds(j * LANES, LANES)
            scols = pl.ds(j * STATE_W, STATE_W)
            nat_u, nat_d = nat[2 * j], nat[2 * j + 1]
            nat_u[...] = us_ref[:, cols].astype(F32)
            nat_d[...] = dys_ref[:, cols]
            for i in range(ls):
                rows_i = pl.ds(SUBLANES * i, SUBLANES)
                upb[j, rows_i, :] = nat_u[pl.ds(i, SUBLANES, stride=ls), :]
                dpb[j, rows_i, :] = nat_d[pl.ds(i, SUBLANES, stride=ls), :]
            u = upb[j]
            dysv = dpb[j]
            a_dd[j] += _acc8(dysv * u)
            up = u.astype(BF16)
            bur[j] = _dot(up, bbr_ref[j])
            bui[j] = _dot(up, bbi_ref[j])
            a_re = jnp.broadcast_to(ar_ref[:, scols], (SUBLANES, STATE_W))
            a_im = jnp.broadcast_to(ai_ref[:, scols], (SUBLANES, STATE_W))
            ec_r = ecr_ref[:, scols]
            ec_i = eci_ref[:, scols]
            xr[j, 0:SUBLANES, :] = ec_r
            xi[j, 0:SUBLANES, :] = ec_i
            _local_scan(a_re, a_im, bur.at[j], bui.at[j], xr.at[j], xi.at[j], SUBLANES, ls, False, init=(ec_r, ec_i),
                        xb=(hb_r.at[j], hb_i.at[j]))
            dysp = dysv.astype(BF16)
            a_cmr[j] += _dot_tn(dysp, hb_r[j])
            a_cmi[j] -= _dot_tn(dysp, hb_i[j])
            gr[j] = _dot_nt(dysp, cmr_ref[j])
            gi[j] = -_dot_nt(dysp, cmi_ref[j])
            _local_scan(a_re, -a_im, gr.at[j], gi.at[j], gr.at[j], gi.at[j], 0, ls, True)
            big_re = pwr_ref[tc - 1:tc, scols]
            big_im = -pwi_ref[tc - 1:tc, scols]
            f_re = fc_r[j, SUBLANES - 1:SUBLANES, :]
            f_im = fc_i[j, SUBLANES - 1:SUBLANES, :]
            for s in range(SUBLANES - 1, -1, -1):
                n_re = gr[j, s:s + 1, :] + big_re * f_re - big_im * f_im
                n_im = gi[j, s:s + 1, :] + big_re * f_im + big_im * f_re
                f_re, f_im = n_re, n_im
                if s > 0:
                    fc_r[j, s - 1:s, :] = f_re
                    fc_i[j, s - 1:s, :] = f_im
            f2_r, f2_i = _two(fc_r[j]), _two(fc_i[j])
            acc_r = jnp.zeros((SUBLANES, STATE_W), F32)
            acc_i = jnp.zeros((SUBLANES, STATE_W), F32)
            for k in range(tc // SLAB):
                rows_k = pl.ds(SLAB * k, SLAB)
                q_re = pvr_ref[rows_k, scols]
                q_im = pvi_ref[rows_k, scols]
                lam_re = gr[j, rows_k, :] + q_re * f2_r + q_im * f2_i
                lam_im = gi[j, rows_k, :] + q_re * f2_i - q_im * f2_r
                xp_re = xr[j, rows_k, :]
                xp_im = xi[j, rows_k, :]
                d_r = lam_re * xp_re + lam_im * xp_im
                d_i = lam_im * xp_re - lam_re * xp_im
                acc_r = acc_r + (d_r[0:SUBLANES] + d_r[SUBLANES:])
                acc_i = acc_i + (d_i[0:SUBLANES] + d_i[SUBLANES:])
                hb_r[j, rows_k, :] = lam_re.astype(BF16)
                hb_i[j, rows_k, :] = lam_im.astype(BF16)
            a_ar[j] += acc_r
            a_ai[j] += acc_i
            fc_r[j, SUBLANES - 1:SUBLANES, :] = f_re
            fc_i[j, SUBLANES - 1:SUBLANES, :] = f_im
            lb_re = hb_r[j]
            lb_im = hb_i[j]
            a_bbr[j] += _dot_tn(up, lb_re)
            a_bbi[j] += _dot_tn(up, lb_im)
            dpb[j] = _dot_nt(lb_re, bbr_ref[j]) + _dot_nt(lb_im, bbi_ref[j]) + dysv * d_ref[:, cols]
            for i in range(ls):
                nat_d[pl.ds(i, SUBLANES, stride=ls), :] = dpb[j, pl.ds(SUBLANES * i, SUBLANES), :]
            dus_ref[:, cols] = nat_d[...].astype(BF16)

        @pl.when(c == nc - 1)
        def _():
            row_g = lax.broadcasted_iota(jnp.int32, (LANES, STATE_W), 0) // SSM_H
            col_g = lax.broadcasted_iota(jnp.int32, (LANES, STATE_W), 1) // SSM_P
            fold = (lax.broadcasted_iota(jnp.int32, (STATE_W, SSM_P), 0) % SSM_P
                    == lax.broadcasted_iota(jnp.int32, (STATE_W, SSM_P), 1)).astype(BF16)
            for j in range(nb):
                rows_j = pl.ds(j * LANES, LANES)
                for acc, out in ((a_bbr, dbbr_ref), (a_bbi, dbbi_ref), (a_cmr, dcmr_ref), (a_cmi, dcmi_ref)):
                    out[rows_j, :] = _unpermute_rhs(jnp.where(row_g == col_g, acc[j], 0.0), fold)
                dar_ref[:, pl.ds(j * STATE_W, STATE_W)] = jnp.sum(a_ar[j], axis=0, keepdims=True)
                dai_ref[:, pl.ds(j * STATE_W, STATE_W)] = jnp.sum(a_ai[j], axis=0, keepdims=True)
                dd_ref[:, pl.ds(j * LANES, LANES)] = jnp.sum(a_dd[j], axis=0, keepdims=True)

    sp = _scan_specs(tc, nb, lambda c: nc - 1 - c)
    ghp = pl.BlockSpec((nb * LANES, SSM_P), lambda b, c: (b, 0))
    ghp_shape = jax.ShapeDtypeStruct((SSM_G * SSM_H, SSM_P), F32)
    small = pltpu.VMEM((nb, SUBLANES, STATE_W), F32)
    big = pltpu.VMEM((nb, tc, STATE_W), F32)
    bigp = pltpu.VMEM((nb, tc + SUBLANES, STATE_W), F32)
    blk = pltpu.VMEM((nb, LANES, STATE_W), F32)
    tok = pltpu.VMEM((nb, tc, LANES), F32)
    return pl.pallas_call(
        kern, name="ssm_scan_bwd", grid=(LANE_BLOCKS // nb, nc),
        in_specs=[sp["us"], sp["tok"], sp["car"], sp["car"], sp["bblk"], sp["bblk"], sp["cblk"], sp["cblk"],
                  sp["vec"], sp["vec"], sp["tab"], sp["tab"], sp["tab"], sp["tab"], sp["dvec"], _ANY],
        out_specs=(sp["us"], ghp, ghp, ghp, ghp, sp["vec"], sp["vec"], sp["dvec"]),
        out_shape=(jax.ShapeDtypeStruct(dproj.shape, BF16), ghp_shape, ghp_shape, ghp_shape, ghp_shape,
                   jax.ShapeDtypeStruct((1, STATE_ALL), F32), jax.ShapeDtypeStruct((1, STATE_ALL), F32),
                   jax.ShapeDtypeStruct((1, D_MODEL), F32)),
        scratch_shapes=[big, big, bigp, bigp, big, big, small, small, blk, blk, blk, blk,
                        small, small, pltpu.VMEM((nb, SUBLANES, LANES), F32), tok, tok,
                        pltpu.VMEM((nb, tc, STATE_W), BF16), pltpu.VMEM((nb, tc, STATE_W), BF16)]
        + [pltpu.VMEM((tc, LANES), F32)] * (2 * nb),
        input_output_aliases={15: 0},
        compiler_params=_cparams("parallel", "arbitrary"),
    )(proj, dys, ec_re, ec_im, bb_re, bb_im, cm_re, cm_im, abar_re, abar_im, pw_re, pw_im, pv_re, pv_im, d_skip, dproj)


def _eye5():
    return jnp.asarray(np.eye(GROUPS_PER_BLOCK, dtype=np.float32)[None, :, None, :, None])


def _embed_b(bb_t):
    t = bb_t.transpose(1, 0, 2).reshape(LANE_BLOCKS, GROUPS_PER_BLOCK, SSM_H, 1, SSM_P)
    return (t * _eye5()).reshape(LANE_BLOCKS, LANES, STATE_W)


def _embed_c(c_ghp):
    t = c_ghp.transpose(0, 2, 1).reshape(LANE_BLOCKS, GROUPS_PER_BLOCK, SSM_P, 1, SSM_H)
    return (t * _eye5()).reshape(LANE_BLOCKS, STATE_W, LANES)


def _local_step(x, c_row, tgt, w_ada_bf, b_ada, g1, g2, w_in_bf, pool_w_bf, pscale, a_re, a_im, log_dt,
                b_re_t, b_im_t, c_re, c_im, d_skip, glu_w_bf, glu_b, wbp_bf, wbs_bf, wout_bf,
                early_weight=None, late_weights=None, ride_for_dw_in=None, ride_for_dh=None):
    seq = x.shape[0]
    tc = min(SCAN_CHUNK, seq)
    mod8, silu_c = _mod_kernel(c_row, w_ada_bf, b_ada)
    mod = mod8[0:1]
    shift, scale, gate = mod[:, 0:D_MODEL], mod[:, D_MODEL:2 * D_MODEL], mod[:, 2 * D_MODEL:]

    abar_re, abar_im, bb_re_t, bb_im_t = _ssm_params(a_re, a_im, log_dt, b_re_t, b_im_t)
    abar_re_f, abar_im_f = abar_re.reshape(1, STATE_ALL), abar_im.reshape(1, STATE_ALL)
    pw_re, pw_im, pv_re, pv_im = _pow_tables(abar_re_f, abar_im_f, tc)
    bbe_re, bbe_im = _embed_b(bb_re_t).astype(BF16), _embed_b(bb_im_t).astype(BF16)
    cme_re, cme_im = _embed_c(c_re).astype(BF16), _embed_c(c_im).astype(BF16)
    d_row = d_skip.reshape(1, D_MODEL)

    if early_weight:
        h, h_t, *gathered = _in_norm(x, g1, scale, shift, ride=early_weight[0])
        w_in_bf, w_in_cols = early_weight[1](*gathered)
    else:
        h, h_t = _in_norm(x, g1, scale, shift)
        w_in_cols = [w_in_bf]
    bn_proj = next(b for b in (1536, 1024, 768, 512, 256) if w_in_cols[0].shape[1] % b == 0)
    if late_weights:
        proj, *gathered = _mm([h], w_in_cols, name="proj", out_dtype=BF16, bm=1024, bn=bn_proj, bk=1024,
                              ride=late_weights[0])
        pool_w_bf, glu_w_bf, wbp_bf, wbs_bf, wout_bf = late_weights[1](*gathered)
    else:
        proj = _mm([h], w_in_cols, name="proj", out_dtype=BF16, bm=1024, bn=bn_proj, bk=1024)
    ypool, ypool_t = _pool_fwd(proj, pool_w_bf, pscale)
    ys, ec_re, ec_im = _ssm_scan_fwd(proj, bbe_re, bbe_im, cme_re, cme_im, abar_re_f, abar_im_f,
                                      pw_re, pw_im, d_row, tc)
    yssm, yssm_t = _glu_fwd(ys, proj, glu_w_bf, glu_b)
    (dy, dypool, dyssm, dproj, merged_t, dob, dbp, dbs, loss, dgate, dg2) = _out_fwd_bwd(
        ypool, yssm, proj, x, tgt, gate, g2, wbp_bf, wbs_bf, wout_bf)

    d_wout = _mm([merged_t], [dob], name="dw_out", bm=1024, bn=1024, bk=2048)
    d_wbp = _mm([ypool_t], [dbp], name="dw_bp", bm=1024, bn=1024, bk=2048)
    d_wbs = _mm([yssm_t], [dbs], name="dw_bs", bm=1024, bn=1024, bk=2048)
    dys, dproj, dq, yg_t, d_glu_b = _glu_bwd(ys, proj, dyssm, glu_w_bf, glu_b, dproj)
    d_glu_w = _mm([yg_t], [dq], name="dw_glu", bm=1024, bn=1024, bk=2048)
    (dproj, dbbe_re, dbbe_im, dcme_re, dcme_im, d_abar_re, d_abar_im, d_dskip) = _ssm_scan_bwd(
        proj, dys, ec_re, ec_im, bbe_re, bbe_im, cme_re, cme_im, abar_re_f, abar_im_f,
        pw_re, pw_im, pv_re, pv_im, d_row, dproj, tc)
    dproj, d_pool_w, d_pscale = _pool_bwd(proj, dypool, pool_w_bf, pscale, dproj)
    dparts = [dproj]
    small_ready = dict(
        dg2=dg2, d_pscale=d_pscale, d_glu_b=d_glu_b, d_dskip=d_dskip, d_abar_re=d_abar_re, d_abar_im=d_abar_im,
        d_bb_re_t=dbbe_re.reshape(SSM_G, SSM_H, SSM_P).transpose(1, 0, 2),
        d_bb_im_t=dbbe_im.reshape(SSM_G, SSM_H, SSM_P).transpose(1, 0, 2),
        d_c_re=dcme_re.reshape(SSM_G, SSM_H, SSM_P), d_c_im=dcme_im.reshape(SSM_G, SSM_H, SSM_P))
    ride = ride_for_dw_in(small_ready) if ride_for_dw_in else None
    d_win = _mm([h_t], dparts, name="dw_in", bm=1024, bn=1024, bk=2048, ride=ride)
    rode_dw_in = ()
    if ride:
        d_win, rode_dw_in = d_win[0], tuple(d_win[1:])
    big_grads = dict(d_win=d_win, d_glu_w=d_glu_w, d_wbp=d_wbp, d_wbs=d_wbs, d_wout=d_wout, d_pool_w=d_pool_w)
    ride = ride_for_dh(big_grads) if ride_for_dh else None
    dh = _mm(dparts, [w_in_bf], tb=True, name="dh", bm=2048, bn=1024, bk=1024, ride=ride)
    rode = ()
    if ride:
        dh, rode = dh[0], tuple(dh[1:])
    grad_x, dshift, dscale, dg1 = _in_bwd(dh, x, dy, g1, scale)
    dmod = jnp.concatenate([dshift, dscale, dgate], axis=1)
    return dict(
        rode=rode, rode_dw_in=rode_dw_in, loss=loss[0, 0], grad_x=grad_x, dmod=dmod, silu_c=silu_c, dg1=dg1,
        **small_ready, **big_grads)


def _position():
    x, y, c = lax.axis_index("x"), lax.axis_index("y"), lax.axis_index("c")
    chips = [(1 - x, y), (x, 1 - y), (1 - x, 1 - y)]
    return x, y, c, chips


_ANY = pl.BlockSpec(memory_space=pl.ANY)
COMM_CHUNKS = 4
COMM_ROW_ALIGN = 16


def _row_chunks(rows, k):
    assert rows % (k * COMM_ROW_ALIGN) == 0, (rows, k)
    step = rows // k
    return [(q * step, step) for q in range(k)]


def _ag_weights_ride(packed, n_chunks=COMM_CHUNKS):
    rows, width = packed.shape
    half = rows // 2
    chunks = _row_chunks(half, n_chunks)
    nq = len(chunks)

    def parts(p_ref, out_ref, send_sems, recv_sems):
        x, y, c, chips = _position()
        sibling = (x, y, 1 - c)

        def copy(k, chip, h, q, to, src=None):
            start, size = chunks[q]
            rows_q = pl.ds(h * half + start, size)
            dst = out_ref.at[2 * chip[0] + chip[1], rows_q, :]
            return pltpu.make_async_remote_copy(
                src_ref=dst if src is None else src.at[rows_q, :], dst_ref=dst, send_sem=send_sems.at[k * nq + q],
                recv_sem=recv_sems.at[k * nq + q], device_id=to, device_id_type=MESH_ID)

        mine = [copy(6 + h, (x, y), h, q, sibling, src=p_ref) for h in range(2) for q in range(nq)]
        first = [copy(j, (x, y), c, q, (*chip, c), src=p_ref) for q in range(nq) for j, chip in enumerate(chips)]
        return (x, y, c), chips, sibling, copy, mine, first

    def start(ins, outs, sems):
        _, _, _, _, mine, first = parts(ins[0], outs[0], sems[0], sems[1])
        for cp in first + mine:
            cp.start()

    def wait(ins, outs, sems):
        (x, y, c), chips, sibling, copy, mine, first = parts(ins[0], outs[0], sems[0], sems[1])
        passed = []
        for q in range(nq):
            for j, chip in enumerate(chips):
                copy(j, chip, c, q, (x, y, c)).wait_recv()
                fwd = copy(3 + j, chip, c, q, sibling)
                fwd.start()
                passed.append(fwd)
        for q in range(nq):
            for j, chip in enumerate(chips):
                copy(3 + j, chip, 1 - c, q, (x, y, c)).wait_recv()
        for cp in mine:
            cp.wait_recv()
        for cp in first + passed + mine:
            cp.wait_send()

    return _Ride([packed], [jax.ShapeDtypeStruct((N_CHIPS, rows, width), packed.dtype)],
                 [pltpu.SemaphoreType.DMA((8 * nq,)), pltpu.SemaphoreType.DMA((8 * nq,))], start, wait)


def _join_rides(rides):
    def split(seq, counts):
        out, at = [], 0
        for n in counts:
            out.append(seq[at:at + n])
            at += n
        return out

    n_in = [len(r.inputs) for r in rides]
    n_out = [len(r.out_shapes) for r in rides]
    n_sem = [len(r.scratch) for r in rides]

    def start(ins, outs, sems):
        for r, i, o, s in zip(rides, split(ins, n_in), split(outs, n_out), split(sems, n_sem)):
            r.start(i, o, s)

    def wait(ins, outs, sems):
        for r, i, o, s in zip(rides, split(ins, n_in), split(outs, n_out), split(sems, n_sem)):
            r.wait(i, o, s)

    return _Ride([a for r in rides for a in r.inputs], [a for r in rides for a in r.out_shapes],
                 [a for r in rides for a in r.scratch], start, wait)


def _run_ride(ride, name):
    n_in, n_out = len(ride.inputs), len(ride.out_shapes)

    def body(*refs):
        ins, outs, sems = refs[:n_in], refs[n_in:n_in + n_out], refs[n_in + n_out:]
        ride.start(ins, outs, sems)
        ride.wait(ins, outs, sems)

    return pl.pallas_call(
        body, name=name, in_specs=[_ANY] * n_in, out_specs=(_ANY,) * n_out, out_shape=tuple(ride.out_shapes),
        scratch_shapes=list(ride.scratch))(*ride.inputs)


def _small_allgather_ride(buf):
    rows, width = buf.shape
    chunks = _row_chunks(rows, COMM_CHUNKS)
    nq = len(chunks)

    def parts(b_ref, all_ref, send_sems, recv_sems, local_sem):
        x, y, c, chips = _position()
        me, sibling = (x, y, c), (x, y, 1 - c)

        def copy(k, block, q, to, src=None):
            rows_q = pl.ds(chunks[q][0], chunks[q][1])
            dst = all_ref.at[4 * block[0] + 2 * block[1] + block[2], rows_q, :]
            return pltpu.make_async_remote_copy(
                src_ref=dst if src is None else src.at[rows_q, :], dst_ref=dst, send_sem=send_sems.at[k * nq + q],
                recv_sem=recv_sems.at[k * nq + q], device_id=to, device_id_type=MESH_ID)

        mine = pltpu.make_async_copy(b_ref, all_ref.at[4 * x + 2 * y + c], local_sem)
        first = []
        for q in range(nq):
            first += [copy(1 + j, me, q, (*chip, c), src=b_ref) for j, chip in enumerate(chips)]
            first.append(copy(0, me, q, sibling, src=b_ref))
        return me, sibling, c, chips, copy, mine, first

    def start(ins, outs, sems):
        _, _, _, _, _, mine, first = parts(ins[0], outs[0], *sems)
        mine.start()
        for cp in first:
            cp.start()

    def wait(ins, outs, sems):
        me, sibling, c, chips, copy, mine, first = parts(ins[0], outs[0], *sems)
        passed = []
        for q in range(nq):
            for j, chip in enumerate(chips):
                copy(1 + j, (*chip, c), q, me).wait_recv()
                fwd = copy(4 + j, (*chip, c), q, sibling)
                fwd.start()
                passed.append(fwd)
        for q in range(nq):
            copy(0, sibling, q, me).wait_recv()
            for j, chip in enumerate(chips):
                copy(4 + j, (*chip, 1 - c), q, me).wait_recv()
        for cp in first + passed:
            cp.wait_send()
        mine.wait()

    return _Ride([buf], [jax.ShapeDtypeStruct((N_DEV, rows, width), F32)],
                 [pltpu.SemaphoreType.DMA((7 * nq,)), pltpu.SemaphoreType.DMA((7 * nq,)), pltpu.SemaphoreType.DMA],
                 start, wait)


def _sum_devices(blocks):
    n, rows, width = blocks.shape
    rb = rows // 2 if (rows // 2) % SUBLANES == 0 else rows

    def kern(b_ref, o_ref):
        total = b_ref[0]
        for d in range(1, n):
            total = total + b_ref[d]
        o_ref[...] = total

    return pl.pallas_call(
        kern, name="small_sum", grid=(rows // rb,), in_specs=[pl.BlockSpec((n, rb, width), lambda i: (0, i, 0))],
        out_specs=pl.BlockSpec((rb, width), lambda i: (i, 0)), out_shape=jax.ShapeDtypeStruct((rows, width), F32),
        compiler_params=_cparams("parallel"))(blocks)


def _small_allgather_sum(buf, head_rows, n_chunks=COMM_CHUNKS):
    rows, width = buf.shape
    chunks = _row_chunks(rows, n_chunks)
    nq = len(chunks)

    def body(b_ref, head_ref, sum_ref, all_ref, send_sems, recv_sems, local_sem):
        x, y, c, chips = _position()
        me, sibling = (x, y, c), (x, y, 1 - c)

        def slot(px, py, pc):
            return all_ref.at[4 * px + 2 * py + pc]

        def copy(k, block, q, to, src=None):
            rows_q = pl.ds(chunks[q][0], chunks[q][1])
            dst = slot(*block).at[rows_q, :]
            return pltpu.make_async_remote_copy(
                src_ref=dst if src is None else src.at[rows_q, :], dst_ref=dst, send_sem=send_sems.at[k * nq + q],
                recv_sem=recv_sems.at[k * nq + q], device_id=to, device_id_type=MESH_ID)

        mine = pltpu.make_async_copy(b_ref, slot(*me), local_sem)
        mine.start()
        first = []
        for q in range(nq):
            first += [copy(1 + j, me, q, (*chip, c), src=b_ref) for j, chip in enumerate(chips)]
            first.append(copy(0, me, q, sibling, src=b_ref))
        for cp in first:
            cp.start()
        passed = []
        for q in range(nq):
            for j, chip in enumerate(chips):
                copy(1 + j, (*chip, c), q, me).wait_recv()
                fwd = copy(4 + j, (*chip, c), q, sibling)
                fwd.start()
                passed.append(fwd)
        for q in range(nq):
            copy(0, sibling, q, me).wait_recv()
            for j, chip in enumerate(chips):
                copy(4 + j, (*chip, 1 - c), q, me).wait_recv()
        for cp in first + passed:
            cp.wait_send()
        mine.wait()
        total = all_ref[0]
        for d in range(1, N_DEV):
            total = total + all_ref[d]
        sum_ref[...] = total
        head_ref[...] = all_ref[:, 0:head_rows, :]

    vm = pl.BlockSpec(memory_space=pltpu.VMEM)
    return pl.pallas_call(
        body, name="small_allgather_sum", in_specs=[vm], out_specs=(vm, vm),
        out_shape=(jax.ShapeDtypeStruct((N_DEV, head_rows, width), F32), jax.ShapeDtypeStruct((rows, width), F32)),
        scratch_shapes=[pltpu.VMEM((N_DEV, rows, width), F32), pltpu.SemaphoreType.DMA((7 * nq,)),
                        pltpu.SemaphoreType.DMA((7 * nq,)), pltpu.SemaphoreType.DMA],
        compiler_params=_cparams(),
    )(buf)


def _rs_pair(g):
    n, rows, width = g.shape
    half = rows // 2
    chunks = _row_chunks(half, COMM_CHUNKS)
    nq = len(chunks)

    def body(g_ref, got_ref, send_sems, recv_sems):
        x, y, c, _ = _position()
        swaps = []
        for k in range(n):
            for q, (start, size) in enumerate(chunks):
                swaps.append(pltpu.make_async_remote_copy(
                    src_ref=g_ref.at[k, pl.ds((1 - c) * half + start, size), :], dst_ref=got_ref.at[k, pl.ds(start, size), :],
                    send_sem=send_sems.at[k * nq + q], recv_sem=recv_sems.at[k * nq + q],
                    device_id=(x, y, 1 - c), device_id_type=MESH_ID))
        for cp in swaps:
            cp.start()
        for cp in swaps:
            cp.wait()

    return pl.pallas_call(
        body, name="rs_pair", in_specs=[_ANY], out_specs=_ANY, out_shape=jax.ShapeDtypeStruct((n, half, width), g.dtype),
        scratch_shapes=[pltpu.SemaphoreType.DMA((n * nq,)), pltpu.SemaphoreType.DMA((n * nq,))],
    )(g)


def _rs_chips_ride(part_bf):
    n, rows, width = part_bf.shape
    chunks = _row_chunks(rows, COMM_CHUNKS)
    nq = len(chunks)

    def sends(pb_ref, got_ref, send_sems, recv_sems):
        x, y, c, chips = _position()
        out = []
        for q, (start, size) in enumerate(chunks):
            for j, chip in enumerate(chips):
                out.append(pltpu.make_async_remote_copy(
                    src_ref=pb_ref.at[2 * chip[0] + chip[1], pl.ds(start, size), :], dst_ref=got_ref.at[j, pl.ds(start, size), :],
                    send_sem=send_sems.at[j * nq + q], recv_sem=recv_sems.at[j * nq + q],
                    device_id=(*chip, c), device_id_type=MESH_ID))
        return out

    def start(ins, outs, sems):
        for cp in sends(ins[0], outs[0], sems[0], sems[1]):
            cp.start()

    def wait(ins, outs, sems):
        for cp in sends(ins[0], outs[0], sems[0], sems[1]):
            cp.wait()

    return _Ride([part_bf], [jax.ShapeDtypeStruct((N_CHIPS - 1, rows, width), BF16)],
                 [pltpu.SemaphoreType.DMA((3 * nq,)), pltpu.SemaphoreType.DMA((3 * nq,))], start, wait)


def _rs_join(shard):
    rows, width = shard.shape
    half = rows // 2
    chunks = _row_chunks(half, COMM_CHUNKS)
    nq = len(chunks)

    def body(in_ref, out_ref, send_sems, recv_sems):
        x, y, c, _ = _position()
        def swap(q, h):
            rows_q = pl.ds(h * half + chunks[q][0], chunks[q][1])
            return pltpu.make_async_remote_copy(
                src_ref=in_ref.at[rows_q, :], dst_ref=out_ref.at[rows_q, :], send_sem=send_sems.at[q],
                recv_sem=recv_sems.at[q], device_id=(x, y, 1 - c), device_id_type=MESH_ID)

        for q in range(nq):
            swap(q, c).start()
        for q in range(nq):
            swap(q, 1 - c).wait_recv()
        for q in range(nq):
            swap(q, c).wait_send()

    return pl.pallas_call(
        body, name="rs_join", in_specs=[_ANY], out_specs=_ANY, input_output_aliases={0: 0},
        out_shape=jax.ShapeDtypeStruct(shard.shape, shard.dtype),
        scratch_shapes=[pltpu.SemaphoreType.DMA((nq,)), pltpu.SemaphoreType.DMA((nq,))],
    )(shard)


def _pair_add(g, got, core):
    n, half, width = got.shape
    nb = 2
    rb = half // nb

    def kern(c_ref, a_ref, b_ref, f_ref, h_ref):
        s = a_ref[...] + b_ref[...]
        f_ref[...] = s
        h_ref[...] = s.astype(BF16)

    spec = pl.BlockSpec((1, rb, width), lambda k, i, c_ref: (k, i, 0))
    return pl.pallas_call(
        kern, name="rs_pair_add",
        grid_spec=pltpu.PrefetchScalarGridSpec(
            num_scalar_prefetch=1, grid=(n, nb),
            in_specs=[pl.BlockSpec((1, rb, width), lambda k, i, c_ref: (k, c_ref[0] * nb + i, 0)), spec],
            out_specs=(spec, spec)),
        out_shape=(jax.ShapeDtypeStruct(got.shape, F32), jax.ShapeDtypeStruct(got.shape, BF16)),
        compiler_params=_cparams("parallel", "parallel"))(core, g, got)


def _chip_add(part_f32, got, where):
    _, rows, width = part_f32.shape
    nb = 2
    rb = rows // nb

    def kern(w_ref, a_ref, b_ref, o_ref):
        o_ref[...] = ((a_ref[0] + b_ref[0].astype(F32)) + b_ref[1].astype(F32)) + b_ref[2].astype(F32)

    return pl.pallas_call(
        kern, name="rs_chip_add",
        grid_spec=pltpu.PrefetchScalarGridSpec(
            num_scalar_prefetch=1, grid=(nb,),
            in_specs=[pl.BlockSpec((1, rb, width), lambda i, w_ref: (w_ref[0], i, 0)),
                      pl.BlockSpec((N_CHIPS - 1, rb, width), lambda i, w_ref: (0, i, 0))],
            out_specs=pl.BlockSpec((rb, width), lambda i, w_ref: (w_ref[1] * nb + i, 0))),
        out_shape=jax.ShapeDtypeStruct((2 * rows, width), F32),
        compiler_params=_cparams("parallel"))(where, part_f32, got)


def _adamw(w, g, m, v, name):
    rows, width = w.shape
    rb = rows
    for cand in (512, 256, 128, 64, 32, 16, 8):
        if rows % cand == 0 and cand * width * 4 <= ADAM_BLOCK_BYTES:
            rb = cand
            break
    spec = pl.BlockSpec((rb, width), lambda i: (i, 0))

    def kern(w_ref, g_ref, m_ref, v_ref, d_ref, nm_ref, nv_ref):
        d_ref[...], nm_ref[...], nv_ref[...] = _adamw_update(w_ref[...], g_ref[...], m_ref[...], v_ref[...])

    shp = jax.ShapeDtypeStruct(w.shape, F32)
    return pl.pallas_call(
        kern, name=name, grid=(rows // rb,), in_specs=[spec] * 4, out_specs=(spec, spec, spec),
        out_shape=(shp, shp, shp), compiler_params=_cparams("parallel"))(w, g, m, v)


def _adamw_update(w, g, m, v):
    nm = ADAM_B1 * m + (1.0 - ADAM_B1) * g
    nv = ADAM_B2 * v + (1.0 - ADAM_B2) * (g * g)
    m_hat = nm / (1.0 - ADAM_B1 ** ADAM_STEP)
    v_hat = nv / (1.0 - ADAM_B2 ** ADAM_STEP)
    return -ADAM_LR * (m_hat / (jnp.sqrt(v_hat) + ADAM_EPS) + ADAM_WD * w), nm, nv


def _adamw_small(params):
    n = len(params)

    def kern(*refs):
        ins, outs = refs[:4 * n], refs[4 * n:]
        for p in range(n):
            w_ref, g_ref, m_ref, v_ref = ins[4 * p:4 * p + 4]
            d, nm, nv = _adamw_update(w_ref[...], g_ref[...], m_ref[...], v_ref[...])
            outs[3 * p][...] = d
            outs[3 * p + 1][...] = nm
            outs[3 * p + 2][...] = nv

    flat = [a for group in params for a in group]
    shapes = [jax.ShapeDtypeStruct(group[0].shape, F32) for group in params for _ in range(3)]
    res = pl.pallas_call(kern, name="adamw_small", out_shape=tuple(shapes), compiler_params=_cparams())(*flat)
    return [tuple(res[3 * p:3 * p + 3]) for p in range(n)]


def _wada_grad(silu_t, dmod_cols):
    n = dmod_cols.shape[1]

    def kern(s_ref, d_ref, o_ref):
        acc = s_ref[:, 0:1] * d_ref[0:1, :]
        for b in range(1, N_DEV):
            acc = acc + s_ref[:, b:b + 1] * d_ref[b:b + 1, :]
        o_ref[...] = acc

    return pl.pallas_call(kern, name="wada_grad", out_shape=jax.ShapeDtypeStruct((D_MODEL, n), F32),
                          compiler_params=_cparams())(silu_t, dmod_cols)


def _rows(a, multiple):
    flat = a.reshape(-1)
    pad = (-flat.shape[0]) % (D_MODEL * multiple)
    if pad:
        flat = jnp.concatenate([flat, jnp.zeros((pad,), flat.dtype)])
    return flat.reshape(-1, D_MODEL)


def _part_rows(shape, multiple):
    return -(-int(np.prod(shape)) // (D_MODEL * multiple)) * multiple


def _pack_rows(parts, multiple, total_multiple=1):
    blocks = [_rows(p, multiple) for p in parts]
    pad = (-sum(b.shape[0] for b in blocks)) % total_multiple
    if pad:
        blocks.append(jnp.zeros((pad, D_MODEL), blocks[0].dtype))
    return jnp.concatenate(blocks, axis=0)


def _unpack_rows(buf, shapes, multiple):
    out, r = [], 0
    for shp in shapes:
        n = int(np.prod(shp))
        nr = _part_rows(shp, multiple)
        out.append(buf[r:r + nr].reshape(-1)[:n].reshape(shp))
        r += nr
    return out


def kernel(x, c, w_ada, b_ada, norm_pre, norm_post, w_in, pool_w, pool_scale, ssm_a_re, ssm_a_im, ssm_log_dt, ssm_b_re, ssm_b_im, ssm_c_re, ssm_c_im, ssm_d, glu_w, glu_b, w_branch_pool, w_branch_ssm, w_out, loss_target, m_w_ada, m_b_ada, m_norm_pre, m_norm_post, m_w_in, m_pool_w, m_pool_scale, m_ssm_a_re, m_ssm_a_im, m_ssm_log_dt, m_ssm_b_re, m_ssm_b_im, m_ssm_c_re, m_ssm_c_im, m_ssm_d, m_glu_w, m_glu_b, m_w_branch_pool, m_w_branch_ssm, m_w_out, v_w_ada, v_b_ada, v_norm_pre, v_norm_post, v_w_in, v_pool_w, v_pool_scale, v_ssm_a_re, v_ssm_a_im, v_ssm_log_dt, v_ssm_b_re, v_ssm_b_im, v_ssm_c_re, v_ssm_c_im, v_ssm_d, v_glu_w, v_glu_b, v_w_branch_pool, v_w_branch_ssm, v_w_out):
    n_ada = w_ada.shape[2]
    n_in = w_in.shape[2]
    n_row = glu_w.shape[1]
    n_pool = pool_w.shape[2]
    n_groups = pool_w.shape[1]

    (g_ada,) = _run_ride(_ag_weights_ride(w_ada[0].astype(BF16)), "ag_weights")
    w_ada_bf = g_ada.transpose(1, 0, 2).reshape(D_MODEL, N_CHIPS * n_ada)
    w_in_ride = _ag_weights_ride(w_in[0].astype(BF16))

    def unpack_w_in(g_in):
        return g_in.transpose(1, 0, 2).reshape(D_MODEL, N_CHIPS * n_in), [g_in[k] for k in range(N_CHIPS)]
    pool_rows = n_groups * n_pool * POOL_GW // D_MODEL
    late_shards = [pool_w[0].reshape(n_groups * n_pool, POOL_GW), glu_w[0], w_branch_pool[0], w_branch_ssm[0], w_out[0]]
    late_ride = _join_rides([_ag_weights_ride(s.astype(BF16), n_chunks=2) for s in late_shards])

    def unpack_late(pool, *squares):
        pool = pool.reshape(N_CHIPS, n_groups, n_pool, POOL_GW).transpose(1, 0, 2, 3)
        return (pool.reshape(n_groups, POOL_GW, POOL_GW), *[s.reshape(D_MODEL, D_MODEL) for s in squares])

    chip = 2 * lax.axis_index("x") + lax.axis_index("y")
    core = lax.axis_index("c").astype(jnp.int32)
    kept = {}

    def by_cols(a, n):
        return a.reshape(D_MODEL, N_CHIPS, n).transpose(1, 0, 2).reshape(N_CHIPS, -1, D_MODEL)

    def by_rows(a):
        return a.reshape(N_CHIPS, n_row, D_MODEL)

    def exchange_big(g):
        pool_by_chip = g["d_pool_w"].reshape(n_groups, N_CHIPS, n_pool, POOL_GW).transpose(1, 0, 2, 3)
        blocks = [by_cols(g["d_win"], n_in), by_rows(g["d_glu_w"]), by_rows(g["d_wbp"]), by_rows(g["d_wbs"]),
                  by_rows(g["d_wout"]), pool_by_chip.reshape(N_CHIPS, pool_rows, D_MODEL)]
        pad = (-sum(b.shape[1] for b in blocks)) % (2 * COMM_CHUNKS * COMM_ROW_ALIGN)
        if pad:
            blocks.append(jnp.zeros((N_CHIPS, pad, D_MODEL), F32))
        g_packed = jnp.concatenate(blocks, axis=1)
        kept["part_f32"], part_bf = _pair_add(g_packed, _rs_pair(g_packed), core.reshape(1))
        return _rs_chips_ride(part_bf)

    a_re, a_im, log_dt = ssm_a_re[0], ssm_a_im[0], ssm_log_dt[0].reshape(SSM_G, 1)
    b_re_t, b_im_t = ssm_b_re[0].transpose(2, 0, 1), ssm_b_im[0].transpose(2, 0, 1)
    early_names = ["dg2", "d_pscale", "d_glu_b", "d_dskip", "d_abar_re", "d_abar_im", "d_bb_re_t", "d_bb_im_t",
                   "d_c_re", "d_c_im"]

    def exchange_small(s):
        parts = [s[k] for k in early_names]
        kept["early_shapes"] = [p.shape for p in parts]
        return _small_allgather_ride(_pack_rows(parts, SUBLANES, COMM_CHUNKS * COMM_ROW_ALIGN))

    res = _local_step(x[0], c, loss_target[0], w_ada_bf, b_ada, norm_pre, norm_post, None, None, pool_scale,
                      a_re, a_im, log_dt, b_re_t, b_im_t, ssm_c_re[0], ssm_c_im[0], ssm_d[0], None, glu_b[0:1],
                      None, None, None, early_weight=(w_in_ride, unpack_w_in), late_weights=(late_ride, unpack_late),
                      ride_for_dw_in=exchange_small, ride_for_dh=exchange_big)

    (all_early,) = res["rode_dw_in"]
    (g_norm_post, g_pscale, g_glu_b, g_dskip, s_abar_re, s_abar_im, s_bb_re, s_bb_im, g_c_re, g_c_im) = _unpack_rows(
        _sum_devices(all_early), kept["early_shapes"], SUBLANES)
    g_a_re, g_a_im, g_log_dt, g_b_re_t, g_b_im_t = _ssm_params_bwd(
        a_re, a_im, log_dt, b_re_t, b_im_t, s_abar_re.reshape(SSM_G, SSM_P), s_abar_im.reshape(SSM_G, SSM_P),
        s_bb_re, s_bb_im)
    late_parts = [res["dmod"], res["silu_c"], res["dg1"], res["loss"].reshape(1, 1)]
    late_shapes = [p.shape for p in late_parts]
    head_rows = _part_rows(late_shapes[0], SUBLANES) + _part_rows(late_shapes[1], SUBLANES)
    all_late, sum_late = _small_allgather_sum(_pack_rows(late_parts, SUBLANES, COMM_ROW_ALIGN), head_rows, n_chunks=1)
    g_b_ada, _, g_norm_pre, loss = _unpack_rows(sum_late, late_shapes, SUBLANES)
    loss = loss[0, 0]
    dmod_all = all_late[:, 0:3].reshape(N_DEV, 3 * D_MODEL)
    dmod_cols = lax.dynamic_slice_in_dim(dmod_all, chip * n_ada, n_ada, axis=1)
    silu_t = all_late[:, _part_rows(late_shapes[0], SUBLANES)].transpose(1, 0)
    g_w_ada = _wada_grad(silu_t, dmod_cols)

    (got_chips,) = res["rode"]
    shard = _rs_join(_chip_add(kept["part_f32"], got_chips, jnp.stack([chip.astype(jnp.int32), core])))
    r = 0
    g_w_in = shard[r:r + n_in].reshape(D_MODEL, n_in)
    r += n_in
    g_squares = []
    for _ in range(4):
        g_squares.append(shard[r:r + n_row])
        r += n_row
    g_glu_w, g_wbp, g_wbs, g_wout = g_squares
    g_pool_w = shard[r:r + pool_rows].reshape(n_groups * n_pool, POOL_GW)

    big = [("w_ada", w_ada[0], g_w_ada, m_w_ada[0], v_w_ada[0]),
           ("w_in", w_in[0], g_w_in, m_w_in[0], v_w_in[0]),
           ("pool_w", pool_w[0].reshape(n_groups * n_pool, POOL_GW), g_pool_w,
            m_pool_w[0].reshape(n_groups * n_pool, POOL_GW), v_pool_w[0].reshape(n_groups * n_pool, POOL_GW)),
           ("glu_w", glu_w[0], g_glu_w, m_glu_w[0], v_glu_w[0]),
           ("w_branch_pool", w_branch_pool[0], g_wbp, m_w_branch_pool[0], v_w_branch_pool[0]),
           ("w_branch_ssm", w_branch_ssm[0], g_wbs, m_w_branch_ssm[0], v_w_branch_ssm[0]),
           ("w_out", w_out[0], g_wout, m_w_out[0], v_w_out[0])]
    out = {}
    for name, w_, g_, m_, v_ in big:
        d_, nm_, nv_ = _adamw(w_, g_, m_, v_, "adamw_" + name)
        out[name] = (g_, d_, nm_, nv_)

    g_b_re = g_b_re_t.transpose(1, 2, 0)
    g_b_im = g_b_im_t.transpose(1, 2, 0)
    small = [("b_ada", b_ada, g_b_ada, m_b_ada, v_b_ada),
             ("norm_pre", norm_pre, g_norm_pre, m_norm_pre, v_norm_pre),
             ("norm_post", norm_post, g_norm_post, m_norm_post, v_norm_post),
             ("pool_scale", pool_scale, g_pscale, m_pool_scale, v_pool_scale),
             ("ssm_a_re", ssm_a_re, g_a_re, m_ssm_a_re, v_ssm_a_re),
             ("ssm_a_im", ssm_a_im, g_a_im, m_ssm_a_im, v_ssm_a_im),
             ("ssm_log_dt", ssm_log_dt, g_log_dt, m_ssm_log_dt, v_ssm_log_dt),
             ("ssm_b_re", ssm_b_re, g_b_re, m_ssm_b_re, v_ssm_b_re),
             ("ssm_b_im", ssm_b_im, g_b_im, m_ssm_b_im, v_ssm_b_im),
             ("ssm_c_re", ssm_c_re, g_c_re, m_ssm_c_re, v_ssm_c_re),
             ("ssm_c_im", ssm_c_im, g_c_im, m_ssm_c_im, v_ssm_c_im),
             ("ssm_d", ssm_d, g_dskip, m_ssm_d, v_ssm_d),
             ("glu_b", glu_b, g_glu_b, m_glu_b, v_glu_b)]
    small = [(name, w_, g_.reshape(w_.shape), m_, v_) for name, w_, g_, m_, v_ in small]
    updates = _adamw_small([t[1:] for t in small])
    for (name, _, g_, _, _), (d_, nm_, nv_) in zip(small, updates):
        out[name] = (g_, d_, nm_, nv_)

    order = ["w_ada", "b_ada", "norm_pre", "norm_post", "w_in", "pool_w", "pool_scale", "ssm_a_re", "ssm_a_im",
             "ssm_log_dt", "ssm_b_re", "ssm_b_im", "ssm_c_re", "ssm_c_im", "ssm_d", "glu_w", "glu_b", "w_branch_pool",
             "w_branch_ssm", "w_out"]
    ref_shape = dict(w_ada=w_ada.shape, w_in=w_in.shape, pool_w=pool_w.shape, glu_w=glu_w.shape,
                     w_branch_pool=w_branch_pool.shape, w_branch_ssm=w_branch_ssm.shape, w_out=w_out.shape)
    for name, w_, _, _, _ in small:
        ref_shape[name] = w_.shape
    results = [loss, res["grad_x"][None]]
    for k in range(4):
        results += [out[name][k].reshape(ref_shape[name]) for name in order]
    return tuple(results)
```

```python
import functools
import math

import numpy as np
import jax
import jax.numpy as jnp
from jax import lax
from jax.experimental import pallas as pl
from jax.experimental.pallas import tpu as pltpu

F32 = jnp.float32
BF16 = jnp.bfloat16
MESH_ID = pl.DeviceIdType.MESH

D_MODEL = 1024
LANES = 128
SUBLANES = 8
SSM_G, SSM_P, SSM_H = 64, 64, 16
LANE_BLOCKS = D_MODEL // LANES
GROUPS_PER_BLOCK = LANES // SSM_H
STATE_W = GROUPS_PER_BLOCK * SSM_P
STATE_ALL = SSM_G * SSM_P
POOL_WINDOWS = (2, 4, 8, 16)
POOL_GW = D_MODEL // len(POOL_WINDOWS)
HALO = 16
RMS_EPS = 1e-6
N_CHIPS = 4
N_DEV = 8

SCAN_CHUNK = 1024
SCAN_BLOCKS = 1
ROW_CHUNK = 256
ROW_CHUNK_WIDE = 512
PROJ_ROWS = 1024
VMEM_LIMIT_BYTES = 56 * 1024 * 1024

ADAM_BLOCK_BYTES = 1 << 20
ADAM_LR, ADAM_B1, ADAM_B2, ADAM_EPS, ADAM_WD, ADAM_STEP = 0.001, 0.9, 0.999, 1e-08, 0.01, 10

_GELU_C0 = math.sqrt(2.0 / math.pi)
_GELU_C1 = 0.044715


def _cparams(*sem):
    if sem:
        return pltpu.CompilerParams(dimension_semantics=sem, vmem_limit_bytes=VMEM_LIMIT_BYTES)
    return pltpu.CompilerParams(vmem_limit_bytes=VMEM_LIMIT_BYTES)


def _sigmoid(v):
    return jax.nn.sigmoid(v)


def _silu(v):
    return v * _sigmoid(v)


def _dsilu(v):
    s = _sigmoid(v)
    return s * (1.0 + v * (1.0 - s))


def _gelu(v):
    return v * (0.5 * (1.0 + jnp.tanh(_GELU_C0 * v * (1.0 + _GELU_C1 * (v * v)))))


def _gelu_and_grad(v):
    v2 = v * v
    t = jnp.tanh(_GELU_C0 * v * (1.0 + _GELU_C1 * v2))
    half = 0.5 * (1.0 + t)
    grad = half + (0.5 * _GELU_C0) * v * (1.0 - t * t) * (1.0 + (3.0 * _GELU_C1) * v2)
    return v * half, grad


def _silu_and_grad(v):
    s = _sigmoid(v)
    return v * s, s * (1.0 + v * (1.0 - s))


def _dot(a, b):
    return lax.dot_general(a, b, (((1,), (0,)), ((), ())), preferred_element_type=F32)


def _dot_nt(a, b):
    return lax.dot_general(a, b, (((1,), (1,)), ((), ())), preferred_element_type=F32)


def _dot_tn(a, b):
    return lax.dot_general(a, b, (((0,), (0,)), ((), ())), preferred_element_type=F32)


def _acc8(v):
    return v.reshape(v.shape[0] // SUBLANES, SUBLANES, v.shape[1]).sum(axis=0)


class _Ride:
    def __init__(self, inputs, out_shapes, scratch, start, wait):
        self.inputs, self.out_shapes, self.scratch, self.start, self.wait = inputs, out_shapes, scratch, start, wait


def _mm(a_parts, b_parts, *, name, ta=False, tb=False, out_dtype=F32, bm=512, bn=512, bk=512, ride=None):
    a_parts, b_parts = list(a_parts), list(b_parts)
    if ta:
        assert len(a_parts) == 1
        k_dim, m_dim = a_parts[0].shape
    else:
        m_dim = a_parts[0].shape[0]
        k_dim = sum(a.shape[1] for a in a_parts)
    if tb:
        assert len(b_parts) == 1
        n_dim = b_parts[0].shape[0]
    else:
        n_dim = sum(b.shape[1] for b in b_parts)
    bm, bn, bk = min(bm, m_dim), min(bn, n_dim), min(bk, k_dim)
    nm, nn, nk = m_dim // bm, n_dim // bn, k_dim // bk
    a_ranges, off = [], 0
    for a in a_parts:
        cnt = (a.shape[0] if ta else a.shape[1]) // bk
        a_ranges.append((off, cnt))
        off += cnt
    b_ranges, off = [], 0
    for b in b_parts:
        cnt = (b.shape[0] if tb else b.shape[1]) // bn
        b_ranges.append((off, cnt))
        off += cnt

    def a_spec(off, cnt):
        if ta:
            return pl.BlockSpec((bk, bm), lambda i, n, k: (k, i))
        return pl.BlockSpec((bm, bk), lambda i, n, k: (i, jnp.clip(k - off, 0, cnt - 1)))

    def b_spec(off, cnt):
        if tb:
            return pl.BlockSpec((bn, bk), lambda i, n, k: (n, k))
        return pl.BlockSpec((bk, bn), lambda i, n, k: (k, jnp.clip(n - off, 0, cnt - 1)))

    na, nb = len(a_parts), len(b_parts)
    dims = (((0 if ta else 1,), (1 if tb else 0,)), ((), ()))

    def kern_single(a_ref, b_ref, o_ref):
        o_ref[...] = lax.dot_general(a_ref[...].astype(BF16), b_ref[...].astype(BF16), dims,
                                     preferred_element_type=F32).astype(out_dtype)

    if na == 1 and nb == 1 and nk == 1 and not ride:
        return pl.pallas_call(
            kern_single, name=name, grid=(nm, nn),
            in_specs=[pl.BlockSpec((bk, bm), lambda i, n: (0, i)) if ta else pl.BlockSpec((bm, bk), lambda i, n: (i, 0)),
                      pl.BlockSpec((bn, bk), lambda i, n: (n, 0)) if tb else pl.BlockSpec((bk, bn), lambda i, n: (0, n))],
            out_specs=pl.BlockSpec((bm, bn), lambda i, n: (i, n)),
            out_shape=jax.ShapeDtypeStruct((m_dim, n_dim), out_dtype),
            compiler_params=_cparams("parallel", "parallel"),
        )(a_parts[0], b_parts[0])

    n_rin = len(ride.inputs) if ride else 0
    n_rout = len(ride.out_shapes) if ride else 0

    def kern(*refs):
        a_refs, b_refs = refs[:na], refs[na:na + nb]
        rin = refs[na + nb:na + nb + n_rin]
        o_ref = refs[na + nb + n_rin]
        rout = refs[na + nb + n_rin + 1:na + nb + n_rin + 1 + n_rout]
        acc = refs[na + nb + n_rin + 1 + n_rout]
        rsem = refs[na + nb + n_rin + 2 + n_rout:]
        i, n, k = pl.program_id(0), pl.program_id(1), pl.program_id(2)

        if ride:
            @pl.when((i == 0) & (n == 0) & (k == 0))
            def _():
                ride.start(rin, rout, rsem)

        if nk > 1:
            @pl.when(k == 0)
            def _():
                acc[...] = jnp.zeros_like(acc)

        for ja, (koff, kcnt) in enumerate(a_ranges):
            for jb, (noff, ncnt) in enumerate(b_ranges):
                def step(ja=ja, jb=jb):
                    a = a_refs[ja][...].astype(BF16)
                    b = b_refs[jb][...].astype(BF16)
                    prod = lax.dot_general(a, b, dims, preferred_element_type=F32)
                    if nk > 1:
                        acc[...] += prod
                    else:
                        o_ref[...] = prod.astype(out_dtype)

                if na == 1 and nb == 1:
                    step()
                else:
                    cond = (k >= koff) & (k < koff + kcnt) & (n >= noff) & (n < noff + ncnt)
                    pl.when(cond)(step)

        if nk > 1:
            @pl.when(k == nk - 1)
            def _():
                o_ref[...] = acc[...].astype(out_dtype)

        if ride:
            @pl.when((i == nm - 1) & (n == nn - 1) & (k == nk - 1))
            def _():
                ride.wait(rin, rout, rsem)

    any_spec = pl.BlockSpec(memory_space=pl.ANY)
    out_spec = pl.BlockSpec((bm, bn), lambda i, n, k: (i, n))
    out_shape = jax.ShapeDtypeStruct((m_dim, n_dim), out_dtype)
    acc_shape = pltpu.VMEM((bm, bn) if nk > 1 else (SUBLANES, LANES), F32)
    if not ride:
        return pl.pallas_call(
            kern, name=name, grid=(nm, nn, nk),
            in_specs=[a_spec(*r) for r in a_ranges] + [b_spec(*r) for r in b_ranges],
            out_specs=out_spec, out_shape=out_shape, scratch_shapes=[acc_shape],
            compiler_params=_cparams("parallel", "parallel", "arbitrary"),
        )(*a_parts, *b_parts)
    return pl.pallas_call(
        kern, name=name, grid=(nm, nn, nk),
        in_specs=[a_spec(*r) for r in a_ranges] + [b_spec(*r) for r in b_ranges] + [any_spec] * n_rin,
        out_specs=(out_spec,) + (any_spec,) * n_rout, out_shape=(out_shape,) + tuple(ride.out_shapes),
        scratch_shapes=[acc_shape] + list(ride.scratch),
        compiler_params=_cparams("arbitrary", "arbitrary", "arbitrary"),
    )(*a_parts, *b_parts, *ride.inputs)


def _ssm_param_fn(a_re, a_im, log_dt, b_re, b_im):
    dt = jnp.exp(log_dt)
    lam_re = jnp.minimum(a_re, -1e-4)
    lam_im = a_im
    mag = jnp.exp(lam_re * dt)
    abar_re = mag * jnp.cos(lam_im * dt)
    abar_im = mag * jnp.sin(lam_im * dt)
    den = lam_re * lam_re + lam_im * lam_im
    num_re = abar_re - 1.0
    f_re = (num_re * lam_re + abar_im * lam_im) / den
    f_im = (abar_im * lam_re - num_re * lam_im) / den
    bb_re = f_re * b_re - f_im * b_im
    bb_im = f_re * b_im + f_im * b_re
    return abar_re, abar_im, bb_re, bb_im


def _ssm_params(a_re, a_im, log_dt, b_re_t, b_im_t):
    def kern(are, aim, ldt, bre, bim, o_ar, o_ai, o_br, o_bi):
        ar, ai, br, bi = _ssm_param_fn(are[...], aim[...], ldt[...], bre[...], bim[...])
        o_ar[...] = ar
        o_ai[...] = ai
        o_br[...] = br
        o_bi[...] = bi

    gp = jax.ShapeDtypeStruct((SSM_G, SSM_P), F32)
    hgp = jax.ShapeDtypeStruct((SSM_H, SSM_G, SSM_P), F32)
    return pl.pallas_call(kern, name="ssm_params", out_shape=(gp, gp, hgp, hgp), compiler_params=_cparams())(
        a_re, a_im, log_dt, b_re_t, b_im_t)


def _ssm_params_bwd(a_re, a_im, log_dt, b_re_t, b_im_t, d_ar, d_ai, d_bbr, d_bbi):
    def kern(are, aim, ldt, bre, bim, dar, dai, dbr, dbi, o_are, o_aim, o_ldt, o_bre, o_bim):
        prim = (are[...], aim[...], ldt[...], bre[...], bim[...])
        _, vjp = jax.vjp(_ssm_param_fn, *prim)
        g = vjp((dar[...], dai[...], dbr[...], dbi[...]))
        o_are[...] = g[0]
        o_aim[...] = g[1]
        o_ldt[...] = g[2]
        o_bre[...] = g[3]
        o_bim[...] = g[4]

    gp = jax.ShapeDtypeStruct((SSM_G, SSM_P), F32)
    g1 = jax.ShapeDtypeStruct((SSM_G, 1), F32)
    hgp = jax.ShapeDtypeStruct((SSM_H, SSM_G, SSM_P), F32)
    return pl.pallas_call(kern, name="ssm_params_bwd", out_shape=(gp, gp, g1, hgp, hgp), compiler_params=_cparams())(
        a_re, a_im, log_dt, b_re_t, b_im_t, d_ar, d_ai, d_bbr, d_bbi)


def _pow_tables(abar_re, abar_im, tc):
    ls = tc // SUBLANES

    def kern(ar_ref, ai_ref, fr_ref, fi_ref, rr_ref, ri_ref):
        a_re = jnp.broadcast_to(ar_ref[...], (SUBLANES, STATE_W))
        a_im = jnp.broadcast_to(ai_ref[...], (SUBLANES, STATE_W))
        p_re, p_im = a_re, a_im
        for i in range(ls):
            fwd = pl.ds(SUBLANES * i, SUBLANES)
            rev = pl.ds(SUBLANES * (ls - 1 - i), SUBLANES)
            fr_ref[fwd, :] = p_re
            fi_ref[fwd, :] = p_im
            rr_ref[rev, :] = p_re
            ri_ref[rev, :] = p_im
            p_re, p_im = p_re * a_re - p_im * a_im, p_re * a_im + p_im * a_re

    vec = pl.BlockSpec((1, STATE_W), lambda b: (0, b))
    tab = pl.BlockSpec((tc, STATE_W), lambda b: (0, b))
    shp = jax.ShapeDtypeStruct((tc, STATE_ALL), F32)
    return pl.pallas_call(
        kern, name="pow_tables", grid=(LANE_BLOCKS,), in_specs=[vec, vec], out_specs=(tab, tab, tab, tab),
        out_shape=(shp, shp, shp, shp), compiler_params=_cparams("parallel"))(abar_re, abar_im)


def _mod_kernel(c_row, w_ada_bf, b_ada):
    def kern(c_ref, w_ref, b_ref, m_ref, s_ref):
        cv = c_ref[...]
        sc = _silu(cv)
        s_ref[...] = sc
        lhs = jnp.broadcast_to(sc, (SUBLANES, D_MODEL)).astype(BF16)
        m_ref[...] = _dot(lhs, w_ref[...]) + b_ref[...]

    return pl.pallas_call(
        kern, name="ada_mod",
        out_shape=(jax.ShapeDtypeStruct((SUBLANES, 3 * D_MODEL), F32), jax.ShapeDtypeStruct((1, D_MODEL), F32)),
        compiler_params=_cparams())(c_row, w_ada_bf, b_ada)


def _row_spec(tr, width=D_MODEL, col=0):
    return pl.BlockSpec((tr, width), lambda c: (c, col))


def _vec_spec(width=D_MODEL):
    return pl.BlockSpec((1, width), lambda c: (0, 0))


def _col_spec(tr):
    return pl.BlockSpec((D_MODEL, tr), lambda c: (0, c))


def _in_norm(x, g1, scale, shift):
    seq = x.shape[0]
    tr = min(ROW_CHUNK_WIDE, seq)

    def kern(x_ref, g_ref, sc_ref, sh_ref, h_ref, ht_ref):
        xv = x_ref[...]
        r = lax.rsqrt(jnp.mean(xv * xv, axis=-1, keepdims=True) + RMS_EPS)
        h = ((xv * r) * g_ref[...]) * (1.0 + sc_ref[...]) + sh_ref[...]
        h_ref[...] = h.astype(BF16)
        ht_ref[...] = h.T.astype(BF16)

    return pl.pallas_call(
        kern, name="in_norm", grid=(seq // tr,),
        in_specs=[_row_spec(tr), _vec_spec(), _vec_spec(), _vec_spec()], out_specs=(_row_spec(tr), _col_spec(tr)),
        out_shape=(jax.ShapeDtypeStruct((seq, D_MODEL), BF16), jax.ShapeDtypeStruct((D_MODEL, seq), BF16)),
        compiler_params=_cparams("parallel"))(x, g1, scale, shift)


PAD = SUBLANES


def _window_sums(src, cols, w, bufs, rows, ahead):
    cur, cur_cols, step, k = src, cols, 1, 0
    data = pl.ds(PAD, rows)
    while step < w:
        dst = bufs[k % 2]
        dst[data, :] = cur[data, cur_cols] + cur[pl.ds(PAD + (step if ahead else -step), rows), cur_cols]
        cur, cur_cols, step, k = dst, slice(None), 2 * step, k + 1
    return cur, cur_cols


def _in_norm_proj_own(x, g1, scale, shift, w_own, chip, ride):
    seq, n_own = x.shape[0], w_own.shape[1]
    tr = min(PROJ_ROWS, seq)
    nc = seq // tr
    n_rin, n_rout = len(ride.inputs), len(ride.out_shapes)

    def kern(chip_ref, x_ref, g_ref, sc_ref, sh_ref, w_ref, *rest):
        rin, (h_ref, ht_ref, p_ref) = rest[:n_rin], rest[n_rin:n_rin + 3]
        rout, rsem = rest[n_rin + 3:n_rin + 3 + n_rout], rest[n_rin + 3 + n_rout:]
        c = pl.program_id(0)

        @pl.when(c == 0)
        def _():
            ride.start(rin, rout, rsem)

        xv = x_ref[...]
        r = lax.rsqrt(jnp.mean(xv * xv, axis=-1, keepdims=True) + RMS_EPS)
        h = ((xv * r) * g_ref[...]) * (1.0 + sc_ref[...]) + sh_ref[...]
        hb = h.astype(BF16)
        h_ref[...] = hb
        ht_ref[...] = h.T.astype(BF16)
        p_ref[...] = _dot(hb, w_ref[...]).astype(BF16)

        @pl.when(c == nc - 1)
        def _():
            ride.wait(rin, rout, rsem)

    vec = pl.BlockSpec((1, D_MODEL), lambda c, k: (0, 0))
    return pl.pallas_call(
        kern, name="in_norm_proj_own",
        grid_spec=pltpu.PrefetchScalarGridSpec(
            num_scalar_prefetch=1, grid=(nc,),
            in_specs=[pl.BlockSpec((tr, D_MODEL), lambda c, k: (c, 0)), vec, vec, vec,
                      pl.BlockSpec((D_MODEL, n_own), lambda c, k: (0, 0))] + [_ANY] * n_rin,
            out_specs=(pl.BlockSpec((tr, D_MODEL), lambda c, k: (c, 0)), pl.BlockSpec((D_MODEL, tr), lambda c, k: (0, c)),
                       pl.BlockSpec((tr, n_own), lambda c, k: (c, k[0]))) + (_ANY,) * n_rout,
            scratch_shapes=list(ride.scratch)),
        out_shape=(jax.ShapeDtypeStruct((seq, D_MODEL), BF16), jax.ShapeDtypeStruct((D_MODEL, seq), BF16),
                   jax.ShapeDtypeStruct((seq, N_CHIPS * n_own), BF16)) + tuple(ride.out_shapes),
        compiler_params=_cparams("arbitrary"))(chip, x, g1, scale, shift, w_own, *ride.inputs)


def _proj_rest(h, w_blocks, proj, chip, ride):
    seq, n_own = h.shape[0], w_blocks.shape[2]
    tr = min(PROJ_ROWS, seq)
    nm, nn = seq // tr, N_CHIPS - 1
    n_rin, n_rout = len(ride.inputs), len(ride.out_shapes)

    def kern(chip_ref, h_ref, w_ref, _, *rest):
        rin, p_ref = rest[:n_rin], rest[n_rin]
        rout, rsem = rest[n_rin + 1:n_rin + 1 + n_rout], rest[n_rin + 1 + n_rout:]
        i, n = pl.program_id(0), pl.program_id(1)

        @pl.when((i == 0) & (n == 0))
        def _():
            ride.start(rin, rout, rsem)

        p_ref[...] = _dot(h_ref[...], w_ref[0]).astype(BF16)

        @pl.when((i == nm - 1) & (n == nn - 1))
        def _():
            ride.wait(rin, rout, rsem)

    def other(n, k):
        return (k[0] + 1 + n) % N_CHIPS

    return pl.pallas_call(
        kern, name="proj_rest",
        grid_spec=pltpu.PrefetchScalarGridSpec(
            num_scalar_prefetch=1, grid=(nm, nn),
            in_specs=[pl.BlockSpec((tr, D_MODEL), lambda i, n, k: (i, 0)),
                      pl.BlockSpec((1, D_MODEL, n_own), lambda i, n, k: (other(n, k), 0, 0)), _ANY] + [_ANY] * n_rin,
            out_specs=(pl.BlockSpec((tr, n_own), lambda i, n, k: (i, other(n, k))),) + (_ANY,) * n_rout,
            scratch_shapes=list(ride.scratch)),
        out_shape=(jax.ShapeDtypeStruct(proj.shape, BF16),) + tuple(ride.out_shapes),
        input_output_aliases={3: 0},
        compiler_params=_cparams("arbitrary", "arbitrary"))(chip, h, w_blocks, proj, *ride.inputs)


def _pool_windows(ext, bufs, pos, g, w, tr):
    cols = pl.ds(g * POOL_GW, POOL_GW)
    chunk = pl.ds(PAD + HALO, tr)
    cur = ext[chunk, cols]
    win, win_cols = _window_sums(ext, cols, w, bufs, HALO + tr, ahead=False)
    cnt = jnp.minimum(pos + 1, w).astype(F32)
    return win[chunk, win_cols] / cnt - cur


def _zero_pads(refs, rows):
    for ref in refs:
        ref[0:PAD, :] = jnp.zeros((PAD, ref.shape[1]), F32)
        ref[PAD + rows:, :] = jnp.zeros((PAD, ref.shape[1]), F32)


def _pool_fwd(proj, pool_w_bf, pscale):
    seq = proj.shape[0]
    tr = min(ROW_CHUNK_WIDE, seq)
    hb = tr // HALO

    def kern(up_ref, halo_ref, zp_ref, pw_ref, ps_ref, y_ref, yt_ref, ext, buf_a, buf_b):
        c = pl.program_id(0)
        _zero_pads((ext, buf_a, buf_b), HALO + tr)
        ext[pl.ds(PAD, HALO), :] = jnp.where(c > 0, halo_ref[...].astype(F32), 0.0)
        ext[pl.ds(PAD + HALO, tr), :] = up_ref[...].astype(F32)
        pos = c * tr + lax.broadcasted_iota(jnp.int32, (tr, POOL_GW), 0)
        for g, w in enumerate(POOL_WINDOWS):
            cols = pl.ds(g * POOL_GW, POOL_GW)
            pooled = _pool_windows(ext, (buf_a, buf_b), pos, g, w, tr)
            mixed = _dot(pooled.astype(BF16), pw_ref[g])
            y = mixed * ps_ref[:, cols] * _silu(zp_ref[:, cols].astype(F32))
            y_ref[:, cols] = y.astype(BF16)
            yt_ref[cols, :] = y.T.astype(BF16)

    return pl.pallas_call(
        kern, name="pool_fwd", grid=(seq // tr,),
        in_specs=[_row_spec(tr, col=0),
                  pl.BlockSpec((HALO, D_MODEL), lambda c: (jnp.maximum(c * hb - 1, 0), 0)),
                  _row_spec(tr, col=1),
                  pl.BlockSpec((len(POOL_WINDOWS), POOL_GW, POOL_GW), lambda c: (0, 0, 0)),
                  _vec_spec()],
        out_specs=(_row_spec(tr), _col_spec(tr)),
        out_shape=(jax.ShapeDtypeStruct((seq, D_MODEL), BF16), jax.ShapeDtypeStruct((D_MODEL, seq), BF16)),
        scratch_shapes=[pltpu.VMEM((tr + HALO + 2 * PAD, D_MODEL), F32), pltpu.VMEM((tr + HALO + 2 * PAD, POOL_GW), F32),
                        pltpu.VMEM((tr + HALO + 2 * PAD, POOL_GW), F32)],
        compiler_params=_cparams("parallel"))(proj, proj, proj, pool_w_bf, pscale)


def _pool_bwd(proj, dyp, pool_w_bf, pscale, dproj):
    seq = proj.shape[0]
    tr = min(ROW_CHUNK_WIDE, seq)
    hb = tr // HALO
    nc = seq // tr
    n_halo = seq // HALO

    def kern(up_ref, halo_ref, zp_ref, zpn_ref, dyp_ref, dypn_ref, pw_ref, ps_ref, _,
             d01_ref, dpw_ref, dps_ref, ext, dpn, buf_a, buf_b, acc_pw, acc_ps):
        c = pl.program_id(0)

        @pl.when(c == 0)
        def _():
            acc_pw[...] = jnp.zeros_like(acc_pw)
            acc_ps[...] = jnp.zeros_like(acc_ps)

        _zero_pads((ext, dpn, buf_a, buf_b), HALO + tr)
        ext[pl.ds(PAD, HALO), :] = jnp.where(c > 0, halo_ref[...].astype(F32), 0.0)
        ext[pl.ds(PAD + HALO, tr), :] = up_ref[...].astype(F32)
        pos = c * tr + lax.broadcasted_iota(jnp.int32, (tr, POOL_GW), 0)
        pos_n = (c + 1) * tr + lax.broadcasted_iota(jnp.int32, (HALO, POOL_GW), 0)
        has_next = c < nc - 1
        for g, w in enumerate(POOL_WINDOWS):
            cols = pl.ds(g * POOL_GW, POOL_GW)
            pooled_bf = _pool_windows(ext, (buf_a, buf_b), pos, g, w, tr).astype(BF16)
            wg = pw_ref[g]
            mixed = _dot(pooled_bf, wg)
            zp = zp_ref[:, cols].astype(F32)
            sz = _silu(zp)
            dyp_g = dyp_ref[:, cols].astype(F32)
            ps = ps_ref[:, cols]
            dmixed = (dyp_g * ps * sz).astype(BF16)
            acc_ps[:, cols] += _acc8(dyp_g * mixed * sz)
            d01_ref[:, pl.ds(D_MODEL + g * POOL_GW, POOL_GW)] = (dyp_g * mixed * ps * _dsilu(zp)).astype(BF16)
            acc_pw[g] += _dot_tn(pooled_bf, dmixed)
            dpooled = _dot_nt(dmixed, wg)
            dmixed_n = (jnp.where(has_next, dypn_ref[:, cols].astype(F32), 0.0) * ps * _silu(zpn_ref[:, cols].astype(F32))).astype(BF16)
            dpooled_n = _dot_nt(dmixed_n, wg)
            dpn[pl.ds(PAD, tr), :] = dpooled / jnp.minimum(pos + 1, w).astype(F32)
            dpn[pl.ds(PAD + tr, HALO), :] = dpooled_n / jnp.minimum(pos_n + 1, w).astype(F32)
            win, _ = _window_sums(dpn, slice(None), w, (buf_a, buf_b), tr + HALO, ahead=True)
            d01_ref[:, cols] = (win[pl.ds(PAD, tr), :] - dpooled).astype(BF16)

        @pl.when(c == nc - 1)
        def _():
            dpw_ref[...] = acc_pw[...]
            dps_ref[...] = jnp.sum(acc_ps[...], axis=0, keepdims=True)

    nxt = lambda c: (jnp.minimum((c + 1) * hb, n_halo - 1), 0)
    nxt1 = lambda c: (jnp.minimum((c + 1) * hb, n_halo - 1), 1)
    return pl.pallas_call(
        kern, name="pool_bwd", grid=(nc,),
        in_specs=[_row_spec(tr, col=0),
                  pl.BlockSpec((HALO, D_MODEL), lambda c: (jnp.maximum(c * hb - 1, 0), 0)),
                  _row_spec(tr, col=1),
                  pl.BlockSpec((HALO, D_MODEL), nxt1),
                  _row_spec(tr),
                  pl.BlockSpec((HALO, D_MODEL), nxt),
                  pl.BlockSpec((len(POOL_WINDOWS), POOL_GW, POOL_GW), lambda c: (0, 0, 0)),
                  _vec_spec(), _ANY],
        out_specs=(pl.BlockSpec((tr, 2 * D_MODEL), lambda c: (c, 0)),
                   pl.BlockSpec((len(POOL_WINDOWS), POOL_GW, POOL_GW), lambda c: (0, 0, 0)),
                   _vec_spec()),
        out_shape=(jax.ShapeDtypeStruct(dproj.shape, BF16),
                   jax.ShapeDtypeStruct((len(POOL_WINDOWS), POOL_GW, POOL_GW), F32),
                   jax.ShapeDtypeStruct((1, D_MODEL), F32)),
        scratch_shapes=[pltpu.VMEM((tr + HALO + 2 * PAD, D_MODEL), F32)]
        + [pltpu.VMEM((tr + HALO + 2 * PAD, POOL_GW), F32)] * 3
        + [pltpu.VMEM((len(POOL_WINDOWS), POOL_GW, POOL_GW), F32), pltpu.VMEM((SUBLANES, D_MODEL), F32)],
        input_output_aliases={8: 0},
        compiler_params=_cparams("arbitrary"))(proj, proj, proj, proj, dyp, dyp, pool_w_bf, pscale, dproj)


def _glu_fwd(ys, proj, glu_w_bf, glu_b):
    seq = ys.shape[0]
    tr = min(ROW_CHUNK_WIDE, seq)

    def kern(ys_ref, zs_ref, w_ref, b_ref, o_ref, ot_ref):
        yg = _gelu(ys_ref[...])
        q = _dot(yg.astype(BF16), w_ref[...]) + b_ref[...]
        y = yg * _sigmoid(q) * _silu(zs_ref[...].astype(F32))
        o_ref[...] = y.astype(BF16)
        ot_ref[...] = y.T.astype(BF16)

    return pl.pallas_call(
        kern, name="glu_fwd", grid=(seq // tr,),
        in_specs=[_row_spec(tr), _row_spec(tr, col=3), pl.BlockSpec((D_MODEL, D_MODEL), lambda c: (0, 0)), _vec_spec()],
        out_specs=(_row_spec(tr), _col_spec(tr)),
        out_shape=(jax.ShapeDtypeStruct((seq, D_MODEL), BF16), jax.ShapeDtypeStruct((D_MODEL, seq), BF16)),
        compiler_params=_cparams("parallel"))(ys, proj, glu_w_bf, glu_b)


def _glu_bwd(ys, proj, dyssm, glu_w_bf, glu_b, dproj):
    seq = ys.shape[0]
    tr = min(ROW_CHUNK_WIDE, seq)
    nc = seq // tr

    def kern(ys_ref, zs_ref, dy_ref, w_ref, b_ref, _, dys_ref, dzs_ref, dq_ref, yg_ref, db_ref, acc_b):
        c = pl.program_id(0)

        @pl.when(c == 0)
        def _():
            acc_b[...] = jnp.zeros_like(acc_b)

        yg, dgelu = _gelu_and_grad(ys_ref[...])
        yg_bf = yg.astype(BF16)
        q = _dot(yg_bf, w_ref[...]) + b_ref[...]
        sg = _sigmoid(q)
        silu_z, dsilu_z = _silu_and_grad(zs_ref[...].astype(F32))
        dyv = dy_ref[...].astype(F32)
        dyglu = dyv * silu_z
        yglu = yg * sg
        dzs_ref[...] = (dyv * yglu * dsilu_z).astype(BF16)
        dq = dyglu * yglu * (1.0 - sg)
        dq_bf = dq.astype(BF16)
        acc_b[...] += _acc8(dq)
        dyg = dyglu * sg + _dot_nt(dq_bf, w_ref[...])
        dys_ref[...] = dyg * dgelu
        dq_ref[...] = dq_bf
        yg_ref[...] = yg.T.astype(BF16)

        @pl.when(c == nc - 1)
        def _():
            db_ref[...] = jnp.sum(acc_b[...], axis=0, keepdims=True)

    bf = jax.ShapeDtypeStruct((seq, D_MODEL), BF16)
    return pl.pallas_call(
        kern, name="glu_bwd", grid=(nc,),
        in_specs=[_row_spec(tr), _row_spec(tr, col=3), _row_spec(tr),
                  pl.BlockSpec((D_MODEL, D_MODEL), lambda c: (0, 0)), _vec_spec(), _ANY],
        out_specs=(_row_spec(tr), _row_spec(tr, col=3), _row_spec(tr), _col_spec(tr), _vec_spec()),
        out_shape=(jax.ShapeDtypeStruct((seq, D_MODEL), F32), jax.ShapeDtypeStruct(dproj.shape, BF16), bf,
                   jax.ShapeDtypeStruct((D_MODEL, seq), BF16), jax.ShapeDtypeStruct((1, D_MODEL), F32)),
        scratch_shapes=[pltpu.VMEM((SUBLANES, D_MODEL), F32)],
        input_output_aliases={5: 1},
        compiler_params=_cparams("arbitrary"))(ys, proj, dyssm, glu_w_bf, glu_b, dproj)


def _out_fwd_bwd(ypool, yssm, proj, x, tgt, gate, g2, wbp_bf, wbs_bf, wout_bf):
    seq = x.shape[0]
    tr = min(ROW_CHUNK, seq)
    nc = seq // tr

    def kern(yp_ref, ysm_ref, gp_ref, gs_ref, x_ref, t_ref, gate_ref, g2_ref, wbp_ref, wbs_ref, wo_ref,
             dy_ref, dyp_ref, dys_ref, d45_ref, mb_ref, dob_ref, dbp_ref, dbs_ref, loss_ref, dgate_ref, dg2_ref,
             acc_l, acc_gate, acc_g2):
        c = pl.program_id(0)

        @pl.when(c == 0)
        def _():
            acc_l[...] = jnp.zeros_like(acc_l)
            acc_gate[...] = jnp.zeros_like(acc_gate)
            acc_g2[...] = jnp.zeros_like(acc_g2)

        bp = _dot(yp_ref[...], wbp_ref[...])
        bs = _dot(ysm_ref[...], wbs_ref[...])
        sp = _sigmoid(gp_ref[...].astype(F32))
        ss = _sigmoid(gs_ref[...].astype(F32))
        merged = sp * bp + ss * bs
        mb = merged.astype(BF16)
        out = _dot(mb, wo_ref[...])
        r2 = lax.rsqrt(jnp.mean(out * out, axis=-1, keepdims=True) + RMS_EPS)
        oh = out * r2
        gate_v, g2_v = gate_ref[...], g2_ref[...]
        ohg = oh * g2_v
        diff = (x_ref[...] + gate_v * ohg) - t_ref[...]
        acc_l[...] += _acc8(diff * diff)
        dyv = diff * (1.0 / D_MODEL)
        dy_ref[...] = dyv
        dy_oh = dyv * oh
        acc_gate[...] += _acc8(dy_oh * g2_v)
        acc_g2[...] += _acc8(dy_oh * gate_v)
        gg = gate_v * g2_v
        doh = dyv * gg
        dout = r2 * (doh - oh * jnp.mean(dy_oh * gg, axis=-1, keepdims=True))
        dob = dout.astype(BF16)
        dmerged = _dot_nt(dob, wo_ref[...])
        dbp_f = dmerged * sp
        dbs_f = dmerged * ss
        dbp = dbp_f.astype(BF16)
        dbs = dbs_f.astype(BF16)
        d45_ref[:, 0:D_MODEL] = (dbp_f * bp * (1.0 - sp)).astype(BF16)
        d45_ref[:, D_MODEL:] = (dbs_f * bs * (1.0 - ss)).astype(BF16)
        dyp_ref[...] = _dot_nt(dbp, wbp_ref[...]).astype(BF16)
        dys_ref[...] = _dot_nt(dbs, wbs_ref[...]).astype(BF16)
        mb_ref[...] = merged.T.astype(BF16)
        dob_ref[...] = dob
        dbp_ref[...] = dbp
        dbs_ref[...] = dbs

        @pl.when(c == nc - 1)
        def _():
            tot = jnp.sum(acc_l[...], axis=0, keepdims=True)
            loss_ref[...] = jnp.sum(tot, axis=1, keepdims=True) * (0.5 / D_MODEL)
            dgate_ref[...] = jnp.sum(acc_gate[...], axis=0, keepdims=True)
            dg2_ref[...] = jnp.sum(acc_g2[...], axis=0, keepdims=True)

    wspec = pl.BlockSpec((D_MODEL, D_MODEL), lambda c: (0, 0))
    f32 = jax.ShapeDtypeStruct((seq, D_MODEL), F32)
    bf = jax.ShapeDtypeStruct((seq, D_MODEL), BF16)
    vec = jax.ShapeDtypeStruct((1, D_MODEL), F32)
    acc = pltpu.VMEM((SUBLANES, D_MODEL), F32)
    return pl.pallas_call(
        kern, name="out_fwd_bwd", grid=(nc,),
        in_specs=[_row_spec(tr), _row_spec(tr), _row_spec(tr, col=4), _row_spec(tr, col=5), _row_spec(tr), _row_spec(tr),
                  _vec_spec(), _vec_spec(), wspec, wspec, wspec],
        out_specs=(_row_spec(tr), _row_spec(tr), _row_spec(tr), pl.BlockSpec((tr, 2 * D_MODEL), lambda c: (c, 2)),
                   _col_spec(tr), _row_spec(tr), _row_spec(tr), _row_spec(tr),
                   pl.BlockSpec((1, 1), lambda c: (0, 0)), _vec_spec(), _vec_spec()),
        out_shape=(f32, bf, bf, jax.ShapeDtypeStruct((seq, proj.shape[1]), BF16),
                   jax.ShapeDtypeStruct((D_MODEL, seq), BF16), bf, bf, bf,
                   jax.ShapeDtypeStruct((1, 1), F32), vec, vec),
        scratch_shapes=[acc, acc, acc],
        compiler_params=_cparams("arbitrary"))(ypool, yssm, proj, proj, x, tgt, gate, g2, wbp_bf, wbs_bf, wout_bf)


def _in_bwd(dh, x, dy, g1, scale):
    seq = x.shape[0]
    tr = min(ROW_CHUNK_WIDE, seq)
    nc = seq // tr

    def kern(dh_ref, x_ref, dy_ref, g_ref, sc_ref, dx_ref, dsh_ref, dsc_ref, dg_ref, a_sh, a_sc, a_g):
        c = pl.program_id(0)

        @pl.when(c == 0)
        def _():
            a_sh[...] = jnp.zeros_like(a_sh)
            a_sc[...] = jnp.zeros_like(a_sc)
            a_g[...] = jnp.zeros_like(a_g)

        xv = x_ref[...]
        r = lax.rsqrt(jnp.mean(xv * xv, axis=-1, keepdims=True) + RMS_EPS)
        xh = xv * r
        g = g_ref[...]
        dhv = dh_ref[...]
        a_sh[...] += _acc8(dhv)
        a_sc[...] += _acc8(dhv * (xh * g))
        dn = dhv * (1.0 + sc_ref[...])
        a_g[...] += _acc8(dn * xh)
        dxh = dn * g
        dx_ref[...] = dy_ref[...] + r * (dxh - xh * jnp.mean(dxh * xh, axis=-1, keepdims=True))

        @pl.when(c == nc - 1)
        def _():
            dsh_ref[...] = jnp.sum(a_sh[...], axis=0, keepdims=True)
            dsc_ref[...] = jnp.sum(a_sc[...], axis=0, keepdims=True)
            dg_ref[...] = jnp.sum(a_g[...], axis=0, keepdims=True)

    vec = jax.ShapeDtypeStruct((1, D_MODEL), F32)
    acc = pltpu.VMEM((SUBLANES, D_MODEL), F32)
    return pl.pallas_call(
        kern, name="in_bwd", grid=(nc,),
        in_specs=[_row_spec(tr), _row_spec(tr), _row_spec(tr), _vec_spec(), _vec_spec()],
        out_specs=(_row_spec(tr), _vec_spec(), _vec_spec(), _vec_spec()),
        out_shape=(jax.ShapeDtypeStruct((seq, D_MODEL), F32), vec, vec, vec),
        scratch_shapes=[acc, acc, acc],
        compiler_params=_cparams("arbitrary"))(dh, x, dy, g1, scale)


SLAB = 2 * SUBLANES


def _local_scan(a_re, a_im, br, bi, xr, xi, row0, ls, reverse, init=None, xb=None):
    if init is None:
        x_re = jnp.zeros((SUBLANES, STATE_W), F32)
        x_im = jnp.zeros((SUBLANES, STATE_W), F32)
    else:
        x_re, x_im = init
    for i in (range(ls - 1, -1, -1) if reverse else range(ls)):
        src = pl.ds(SUBLANES * i, SUBLANES)
        dst = pl.ds(row0 + SUBLANES * i, SUBLANES)
        n_re = a_re * x_re - a_im * x_im + br[src, :]
        n_im = a_re * x_im + a_im * x_re + bi[src, :]
        if xb is not None and i % 2 == 1:
            pair = pl.ds(SUBLANES * (i - 1), SLAB)
            xb[0][pair, :] = jnp.concatenate([x_re, n_re], axis=0).astype(BF16)
            xb[1][pair, :] = jnp.concatenate([x_im, n_im], axis=0).astype(BF16)
        x_re, x_im = n_re, n_im
        xr[dst, :] = x_re
        xi[dst, :] = x_im
    return x_re, x_im


def _two(v):
    return jnp.concatenate([v, v], axis=0)


def _unpermute_rhs(v, sel):
    hi = v.astype(BF16)
    r1 = v - hi.astype(F32)
    mid = r1.astype(BF16)
    lo = (r1 - mid.astype(F32)).astype(BF16)
    return _dot(hi, sel) + _dot(mid, sel) + _dot(lo, sel)


def _scan_specs(tc, nb, rows_of):
    return dict(
        us=pl.BlockSpec((tc, nb * LANES), lambda b, c: (rows_of(c), 2 * D_MODEL // (nb * LANES) + b)),
        tok=pl.BlockSpec((tc, nb * LANES), lambda b, c: (rows_of(c), b)),
        bblk=pl.BlockSpec((nb, LANES, STATE_W), lambda b, c: (b, 0, 0)),
        cblk=pl.BlockSpec((nb, STATE_W, LANES), lambda b, c: (b, 0, 0)),
        vec=pl.BlockSpec((1, nb * STATE_W), lambda b, c: (0, b)),
        tab=pl.BlockSpec((tc, nb * STATE_W), lambda b, c: (0, b)),
        car=pl.BlockSpec((SUBLANES, nb * STATE_W), lambda b, c: (rows_of(c), b)),
        dvec=pl.BlockSpec((1, nb * LANES), lambda b, c: (0, b)))


def _ssm_scan_fwd(proj, bb_re, bb_im, cm_re, cm_im, abar_re, abar_im, pw_re, pw_im, d_skip, tc):
    seq = proj.shape[0]
    nc = seq // tc
    ls = tc // SUBLANES
    nb = SCAN_BLOCKS

    def kern(us_ref, bbr_ref, bbi_ref, cmr_ref, cmi_ref, ar_ref, ai_ref, pwr_ref, pwi_ref, d_ref,
             ys_ref, ecr_ref, eci_ref, bur, bui, car_r, car_i, end_r, end_i, upb, xb_r, xb_i, *nat):
        c = pl.program_id(1)

        @pl.when(c == 0)
        def _():
            car_r[...] = jnp.zeros_like(car_r)
            car_i[...] = jnp.zeros_like(car_i)

        for j in range(nb):
            cols = pl.ds(j * LANES, LANES)
            scols = pl.ds(j * STATE_W, STATE_W)
            nat[j][...] = us_ref[:, cols].astype(F32)
            for i in range(ls):
                upb[j, pl.ds(SUBLANES * i, SUBLANES), :] = nat[j][pl.ds(i, SUBLANES, stride=ls), :]
            u = upb[j]
            up = u.astype(BF16)
            bur[j] = _dot(up, bbr_ref[j])
            bui[j] = _dot(up, bbi_ref[j])
            a_re = jnp.broadcast_to(ar_ref[:, scols], (SUBLANES, STATE_W))
            a_im = jnp.broadcast_to(ai_ref[:, scols], (SUBLANES, STATE_W))
            x_re, x_im = _local_scan(a_re, a_im, bur.at[j], bui.at[j], bur.at[j], bui.at[j], 0, ls, False)
            end_r[j] = x_re
            end_i[j] = x_im
            big_re = pwr_ref[tc - 1:tc, scols]
            big_im = pwi_ref[tc - 1:tc, scols]
            e_re = car_r[j, 0:1, :]
            e_im = car_i[j, 0:1, :]
            for s in range(SUBLANES):
                n_re = end_r[j, s:s + 1, :] + big_re * e_re - big_im * e_im
                n_im = end_i[j, s:s + 1, :] + big_re * e_im + big_im * e_re
                e_re, e_im = n_re, n_im
                if s < SUBLANES - 1:
                    car_r[j, s + 1:s + 2, :] = e_re
                    car_i[j, s + 1:s + 2, :] = e_im
            ec_re = car_r[j]
            ec_im = car_i[j]
            ecr_ref[:, scols] = ec_re
            eci_ref[:, scols] = ec_im
            e2_re, e2_im = _two(ec_re), _two(ec_im)
            for k in range(tc // SLAB):
                rows_k = pl.ds(SLAB * k, SLAB)
                p_re = pwr_ref[rows_k, scols]
                p_im = pwi_ref[rows_k, scols]
                xb_r[j, rows_k, :] = (bur[j, rows_k, :] + p_re * e2_re - p_im * e2_im).astype(BF16)
                xb_i[j, rows_k, :] = (bui[j, rows_k, :] + p_re * e2_im + p_im * e2_re).astype(BF16)
            upb[j] = _dot(xb_r[j], cmr_ref[j]) - _dot(xb_i[j], cmi_ref[j]) + d_ref[:, cols] * u
            for i in range(ls):
                nat[j][pl.ds(i, SUBLANES, stride=ls), :] = upb[j, pl.ds(SUBLANES * i, SUBLANES), :]
            ys_ref[:, cols] = nat[j][...]
            car_r[j, 0:1, :] = e_re
            car_i[j, 0:1, :] = e_im

    sp = _scan_specs(tc, nb, lambda c: c)
    carry_shape = jax.ShapeDtypeStruct((nc * SUBLANES, STATE_ALL), F32)
    small = pltpu.VMEM((nb, SUBLANES, STATE_W), F32)
    big = pltpu.VMEM((nb, tc, STATE_W), F32)
    return pl.pallas_call(
        kern, name="ssm_scan_fwd", grid=(LANE_BLOCKS // nb, nc),
        in_specs=[sp["us"], sp["bblk"], sp["bblk"], sp["cblk"], sp["cblk"], sp["vec"], sp["vec"], sp["tab"], sp["tab"],
                  sp["dvec"]],
        out_specs=(sp["tok"], sp["car"], sp["car"]),
        out_shape=(jax.ShapeDtypeStruct((seq, D_MODEL), F32), carry_shape, carry_shape),
        scratch_shapes=[big, big, small, small, small, small, pltpu.VMEM((nb, tc, LANES), F32),
                        pltpu.VMEM((nb, tc, STATE_W), BF16), pltpu.VMEM((nb, tc, STATE_W), BF16)]
        + [pltpu.VMEM((tc, LANES), F32)] * nb,
        compiler_params=_cparams("parallel", "arbitrary"),
    )(proj, bb_re, bb_im, cm_re, cm_im, abar_re, abar_im, pw_re, pw_im, d_skip)


def _ssm_scan_bwd(proj, dys, ec_re, ec_im, bb_re, bb_im, cm_re, cm_im, abar_re, abar_im,
                  pw_re, pw_im, pv_re, pv_im, d_skip, dproj, tc):
    seq = proj.shape[0]
    nc = seq // tc
    ls = tc // SUBLANES
    nb = SCAN_BLOCKS

    def kern(us_ref, dys_ref, ecr_ref, eci_ref, bbr_ref, bbi_ref, cmr_ref, cmi_ref, ar_ref, ai_ref,
             pwr_ref, pwi_ref, pvr_ref, pvi_ref, d_ref, _,
             dus_ref, dbbr_ref, dbbi_ref, dcmr_ref, dcmi_ref, dar_ref, dai_ref, dd_ref,
             bur, bui, xr, xi, gr, gi, fc_r, fc_i, a_bbr, a_bbi, a_cmr, a_cmi, a_ar, a_ai, a_dd, upb, dpb, hb_r, hb_i,
             *nat):
        c = pl.program_id(1)

        @pl.when(c == 0)
        def _():
            for ref in (fc_r, fc_i, a_bbr, a_bbi, a_cmr, a_cmi, a_ar, a_ai, a_dd):
                ref[...] = jnp.zeros_like(ref)

        for j in range(nb):
            cols = pl.ds(j * LANES, LANES)
            scols = pl.ds(j * STATE_W, STATE_W)
            nat_u, nat_d = nat[2 * j], nat[2 * j + 1]
            nat_u[...] = us_ref[:, cols].astype(F32)
            nat_d[...] = dys_ref[:, cols]
            for i in range(ls):
                rows_i = pl.ds(SUBLANES * i, SUBLANES)
                upb[j, rows_i, :] = nat_u[pl.ds(i, SUBLANES, stride=ls), :]
                dpb[j, rows_i, :] = nat_d[pl.ds(i, SUBLANES, stride=ls), :]
            u = upb[j]
            dysv = dpb[j]
            a_dd[j] += _acc8(dysv * u)
            up = u.astype(BF16)
            bur[j] = _dot(up, bbr_ref[j])
            bui[j] = _dot(up, bbi_ref[j])
            a_re = jnp.broadcast_to(ar_ref[:, scols], (SUBLANES, STATE_W))
            a_im = jnp.broadcast_to(ai_ref[:, scols], (SUBLANES, STATE_W))
            ec_r = ecr_ref[:, scols]
            ec_i = eci_ref[:, scols]
            xr[j, 0:SUBLANES, :] = ec_r
            xi[j, 0:SUBLANES, :] = ec_i
            _local_scan(a_re, a_im, bur.at[j], bui.at[j], xr.at[j], xi.at[j], SUBLANES, ls, False, init=(ec_r, ec_i),
                        xb=(hb_r.at[j], hb_i.at[j]))
            dysp = dysv.astype(BF16)
            a_cmr[j] += _dot_tn(dysp, hb_r[j])
            a_cmi[j] -= _dot_tn(dysp, hb_i[j])
            gr[j] = _dot_nt(dysp, cmr_ref[j])
            gi[j] = -_dot_nt(dysp, cmi_ref[j])
            _local_scan(a_re, -a_im, gr.at[j], gi.at[j], gr.at[j], gi.at[j], 0, ls, True)
            big_re = pwr_ref[tc - 1:tc, scols]
            big_im = -pwi_ref[tc - 1:tc, scols]
            f_re = fc_r[j, SUBLANES - 1:SUBLANES, :]
            f_im = fc_i[j, SUBLANES - 1:SUBLANES, :]
            for s in range(SUBLANES - 1, -1, -1):
                n_re = gr[j, s:s + 1, :] + big_re * f_re - big_im * f_im
                n_im = gi[j, s:s + 1, :] + big_re * f_im + big_im * f_re
                f_re, f_im = n_re, n_im
                if s > 0:
                    fc_r[j, s - 1:s, :] = f_re
                    fc_i[j, s - 1:s, :] = f_im
            f2_r, f2_i = _two(fc_r[j]), _two(fc_i[j])
            acc_r = jnp.zeros((SUBLANES, STATE_W), F32)
            acc_i = jnp.zeros((SUBLANES, STATE_W), F32)
            for k in range(tc // SLAB):
                rows_k = pl.ds(SLAB * k, SLAB)
                q_re = pvr_ref[rows_k, scols]
                q_im = pvi_ref[rows_k, scols]
                lam_re = gr[j, rows_k, :] + q_re * f2_r + q_im * f2_i
                lam_im = gi[j, rows_k, :] + q_re * f2_i - q_im * f2_r
                xp_re = xr[j, rows_k, :]
                xp_im = xi[j, rows_k, :]
                d_r = lam_re * xp_re + lam_im * xp_im
                d_i = lam_im * xp_re - lam_re * xp_im
                acc_r = acc_r + (d_r[0:SUBLANES] + d_r[SUBLANES:])
                acc_i = acc_i + (d_i[0:SUBLANES] + d_i[SUBLANES:])
                hb_r[j, rows_k, :] = lam_re.astype(BF16)
                hb_i[j, rows_k, :] = lam_im.astype(BF16)
            a_ar[j] += acc_r
            a_ai[j] += acc_i
            fc_r[j, SUBLANES - 1:SUBLANES, :] = f_re
            fc_i[j, SUBLANES - 1:SUBLANES, :] = f_im
            lb_re = hb_r[j]
            lb_im = hb_i[j]
            a_bbr[j] += _dot_tn(up, lb_re)
            a_bbi[j] += _dot_tn(up, lb_im)
            dpb[j] = _dot_nt(lb_re, bbr_ref[j]) + _dot_nt(lb_im, bbi_ref[j]) + dysv * d_ref[:, cols]
            for i in range(ls):
                nat_d[pl.ds(i, SUBLANES, stride=ls), :] = dpb[j, pl.ds(SUBLANES * i, SUBLANES), :]
            dus_ref[:, cols] = nat_d[...].astype(BF16)

        @pl.when(c == nc - 1)
        def _():
            row_g = lax.broadcasted_iota(jnp.int32, (LANES, STATE_W), 0) // SSM_H
            col_g = lax.broadcasted_iota(jnp.int32, (LANES, STATE_W), 1) // SSM_P
            fold = (lax.broadcasted_iota(jnp.int32, (STATE_W, SSM_P), 0) % SSM_P
                    == lax.broadcasted_iota(jnp.int32, (STATE_W, SSM_P), 1)).astype(BF16)
            for j in range(nb):
                rows_j = pl.ds(j * LANES, LANES)
                for acc, out in ((a_bbr, dbbr_ref), (a_bbi, dbbi_ref), (a_cmr, dcmr_ref), (a_cmi, dcmi_ref)):
                    out[rows_j, :] = _unpermute_rhs(jnp.where(row_g == col_g, acc[j], 0.0), fold)
                dar_ref[:, pl.ds(j * STATE_W, STATE_W)] = jnp.sum(a_ar[j], axis=0, keepdims=True)
                dai_ref[:, pl.ds(j * STATE_W, STATE_W)] = jnp.sum(a_ai[j], axis=0, keepdims=True)
                dd_ref[:, pl.ds(j * LANES, LANES)] = jnp.sum(a_dd[j], axis=0, keepdims=True)

    sp = _scan_specs(tc, nb, lambda c: nc - 1 - c)
    ghp = pl.BlockSpec((nb * LANES, SSM_P), lambda b, c: (b, 0))
    ghp_shape = jax.ShapeDtypeStruct((SSM_G * SSM_H, SSM_P), F32)
    small = pltpu.VMEM((nb, SUBLANES, STATE_W), F32)
    big = pltpu.VMEM((nb, tc, STATE_W), F32)
    bigp = pltpu.VMEM((nb, tc + SUBLANES, STATE_W), F32)
    blk = pltpu.VMEM((nb, LANES, STATE_W), F32)
    tok = pltpu.VMEM((nb, tc, LANES), F32)
    return pl.pallas_call(
        kern, name="ssm_scan_bwd", grid=(LANE_BLOCKS // nb, nc),
        in_specs=[sp["us"], sp["tok"], sp["car"], sp["car"], sp["bblk"], sp["bblk"], sp["cblk"], sp["cblk"],
                  sp["vec"], sp["vec"], sp["tab"], sp["tab"], sp["tab"], sp["tab"], sp["dvec"], _ANY],
        out_specs=(sp["us"], ghp, ghp, ghp, ghp, sp["vec"], sp["vec"], sp["dvec"]),
        out_shape=(jax.ShapeDtypeStruct(dproj.shape, BF16), ghp_shape, ghp_shape, ghp_shape, ghp_shape,
                   jax.ShapeDtypeStruct((1, STATE_ALL), F32), jax.ShapeDtypeStruct((1, STATE_ALL), F32),
                   jax.ShapeDtypeStruct((1, D_MODEL), F32)),
        scratch_shapes=[big, big, bigp, bigp, big, big, small, small, blk, blk, blk, blk,
                        small, small, pltpu.VMEM((nb, SUBLANES, LANES), F32), tok, tok,
                        pltpu.VMEM((nb, tc, STATE_W), BF16), pltpu.VMEM((nb, tc, STATE_W), BF16)]
        + [pltpu.VMEM((tc, LANES), F32)] * (2 * nb),
        input_output_aliases={15: 0},
        compiler_params=_cparams("parallel", "arbitrary"),
    )(proj, dys, ec_re, ec_im, bb_re, bb_im, cm_re, cm_im, abar_re, abar_im, pw_re, pw_im, pv_re, pv_im, d_skip, dproj)


def _eye5():
    return jnp.asarray(np.eye(GROUPS_PER_BLOCK, dtype=np.float32)[None, :, None, :, None])


def _embed_b(bb_t):
    t = bb_t.transpose(1, 0, 2).reshape(LANE_BLOCKS, GROUPS_PER_BLOCK, SSM_H, 1, SSM_P)
    return (t * _eye5()).reshape(LANE_BLOCKS, LANES, STATE_W)


def _embed_c(c_ghp):
    t = c_ghp.transpose(0, 2, 1).reshape(LANE_BLOCKS, GROUPS_PER_BLOCK, SSM_P, 1, SSM_H)
    return (t * _eye5()).reshape(LANE_BLOCKS, STATE_W, LANES)


def _local_step(x, c_row, tgt, w_ada_bf, b_ada, g1, g2, w_in_bf, pool_w_bf, pscale, a_re, a_im, log_dt,
                b_re_t, b_im_t, c_re, c_im, d_skip, glu_w_bf, glu_b, wbp_bf, wbs_bf, wout_bf,
                split_proj=None, ride_for_dw_in=None, ride_for_dh=None):
    seq = x.shape[0]
    tc = min(SCAN_CHUNK, seq)
    mod8, silu_c = _mod_kernel(c_row, w_ada_bf, b_ada)
    mod = mod8[0:1]
    shift, scale, gate = mod[:, 0:D_MODEL], mod[:, D_MODEL:2 * D_MODEL], mod[:, 2 * D_MODEL:]

    abar_re, abar_im, bb_re_t, bb_im_t = _ssm_params(a_re, a_im, log_dt, b_re_t, b_im_t)
    abar_re_f, abar_im_f = abar_re.reshape(1, STATE_ALL), abar_im.reshape(1, STATE_ALL)
    pw_re, pw_im, pv_re, pv_im = _pow_tables(abar_re_f, abar_im_f, tc)
    bbe_re, bbe_im = _embed_b(bb_re_t).astype(BF16), _embed_b(bb_im_t).astype(BF16)
    cme_re, cme_im = _embed_c(c_re).astype(BF16), _embed_c(c_im).astype(BF16)
    d_row = d_skip.reshape(1, D_MODEL)

    if split_proj:
        w_own, chip, w_in_ride, unpack_w_in, late_ride, unpack_late = split_proj
        h, h_t, proj, w_blocks = _in_norm_proj_own(x, g1, scale, shift, w_own, chip, w_in_ride)
        w_in_bf = unpack_w_in(w_blocks)
        proj, *gathered = _proj_rest(h, w_blocks, proj, chip, late_ride)
        pool_w_bf, glu_w_bf, wbp_bf, wbs_bf, wout_bf = unpack_late(*gathered)
    else:
        h, h_t = _in_norm(x, g1, scale, shift)
        proj = _mm([h], [w_in_bf], name="proj", out_dtype=BF16, bm=1024, bn=1536, bk=1024)
    ypool, ypool_t = _pool_fwd(proj, pool_w_bf, pscale)
    ys, ec_re, ec_im = _ssm_scan_fwd(proj, bbe_re, bbe_im, cme_re, cme_im, abar_re_f, abar_im_f,
                                      pw_re, pw_im, d_row, tc)
    yssm, yssm_t = _glu_fwd(ys, proj, glu_w_bf, glu_b)
    (dy, dypool, dyssm, dproj, merged_t, dob, dbp, dbs, loss, dgate, dg2) = _out_fwd_bwd(
        ypool, yssm, proj, x, tgt, gate, g2, wbp_bf, wbs_bf, wout_bf)

    d_wout = _mm([merged_t], [dob], name="dw_out", bm=1024, bn=1024, bk=2048)
    d_wbp = _mm([ypool_t], [dbp], name="dw_bp", bm=1024, bn=1024, bk=2048)
    d_wbs = _mm([yssm_t], [dbs], name="dw_bs", bm=1024, bn=1024, bk=2048)
    dys, dproj, dq, yg_t, d_glu_b = _glu_bwd(ys, proj, dyssm, glu_w_bf, glu_b, dproj)
    d_glu_w = _mm([yg_t], [dq], name="dw_glu", bm=1024, bn=1024, bk=2048)
    (dproj, dbbe_re, dbbe_im, dcme_re, dcme_im, d_abar_re, d_abar_im, d_dskip) = _ssm_scan_bwd(
        proj, dys, ec_re, ec_im, bbe_re, bbe_im, cme_re, cme_im, abar_re_f, abar_im_f,
        pw_re, pw_im, pv_re, pv_im, d_row, dproj, tc)
    dproj, d_pool_w, d_pscale = _pool_bwd(proj, dypool, pool_w_bf, pscale, dproj)
    dparts = [dproj]
    small_ready = dict(
        dg2=dg2, d_pscale=d_pscale, d_glu_b=d_glu_b, d_dskip=d_dskip, d_abar_re=d_abar_re, d_abar_im=d_abar_im,
        d_bb_re_t=dbbe_re.reshape(SSM_G, SSM_H, SSM_P).transpose(1, 0, 2),
        d_bb_im_t=dbbe_im.reshape(SSM_G, SSM_H, SSM_P).transpose(1, 0, 2),
        d_c_re=dcme_re.reshape(SSM_G, SSM_H, SSM_P), d_c_im=dcme_im.reshape(SSM_G, SSM_H, SSM_P))
    ride = ride_for_dw_in(small_ready) if ride_for_dw_in else None
    d_win = _mm([h_t], dparts, name="dw_in", bm=1024, bn=1024, bk=2048, ride=ride)
    rode_dw_in = ()
    if ride:
        d_win, rode_dw_in = d_win[0], tuple(d_win[1:])
    big_grads = dict(d_win=d_win, d_glu_w=d_glu_w, d_wbp=d_wbp, d_wbs=d_wbs, d_wout=d_wout, d_pool_w=d_pool_w)
    ride = ride_for_dh(big_grads) if ride_for_dh else None
    dh = _mm(dparts, [w_in_bf], tb=True, name="dh", bm=2048, bn=1024, bk=1024, ride=ride)
    rode = ()
    if ride:
        dh, rode = dh[0], tuple(dh[1:])
    grad_x, dshift, dscale, dg1 = _in_bwd(dh, x, dy, g1, scale)
    dmod = jnp.concatenate([dshift, dscale, dgate], axis=1)
    return dict(
        rode=rode, rode_dw_in=rode_dw_in, loss=loss[0, 0], grad_x=grad_x, dmod=dmod, silu_c=silu_c, dg1=dg1,
        **small_ready, **big_grads)


def _position():
    x, y, c = lax.axis_index("x"), lax.axis_index("y"), lax.axis_index("c")
    chips = [(1 - x, y), (x, 1 - y), (1 - x, 1 - y)]
    return x, y, c, chips


_ANY = pl.BlockSpec(memory_space=pl.ANY)
COMM_CHUNKS = 4
COMM_ROW_ALIGN = 16


def _row_chunks(rows, k):
    assert rows % (k * COMM_ROW_ALIGN) == 0, (rows, k)
    step = rows // k
    return [(q * step, step) for q in range(k)]


def _ag_weights_ride(packed, n_chunks=COMM_CHUNKS):
    rows, width = packed.shape
    half = rows // 2
    chunks = _row_chunks(half, n_chunks)
    nq = len(chunks)

    def parts(p_ref, out_ref, send_sems, recv_sems):
        x, y, c, chips = _position()
        sibling = (x, y, 1 - c)

        def copy(k, chip, h, q, to, src=None):
            start, size = chunks[q]
            rows_q = pl.ds(h * half + start, size)
            dst = out_ref.at[2 * chip[0] + chip[1], rows_q, :]
            return pltpu.make_async_remote_copy(
                src_ref=dst if src is None else src.at[rows_q, :], dst_ref=dst, send_sem=send_sems.at[k * nq + q],
                recv_sem=recv_sems.at[k * nq + q], device_id=to, device_id_type=MESH_ID)

        mine = [copy(6 + h, (x, y), h, q, sibling, src=p_ref) for h in range(2) for q in range(nq)]
        first = [copy(j, (x, y), c, q, (*chip, c), src=p_ref) for q in range(nq) for j, chip in enumerate(chips)]
        return (x, y, c), chips, sibling, copy, mine, first

    def start(ins, outs, sems):
        _, _, _, _, mine, first = parts(ins[0], outs[0], sems[0], sems[1])
        for cp in first + mine:
            cp.start()

    def wait(ins, outs, sems):
        (x, y, c), chips, sibling, copy, mine, first = parts(ins[0], outs[0], sems[0], sems[1])
        passed = []
        for q in range(nq):
            for j, chip in enumerate(chips):
                copy(j, chip, c, q, (x, y, c)).wait_recv()
                fwd = copy(3 + j, chip, c, q, sibling)
                fwd.start()
                passed.append(fwd)
        for q in range(nq):
            for j, chip in enumerate(chips):
                copy(3 + j, chip, 1 - c, q, (x, y, c)).wait_recv()
        for cp in mine:
            cp.wait_recv()
        for cp in first + passed + mine:
            cp.wait_send()

    return _Ride([packed], [jax.ShapeDtypeStruct((N_CHIPS, rows, width), packed.dtype)],
                 [pltpu.SemaphoreType.DMA((8 * nq,)), pltpu.SemaphoreType.DMA((8 * nq,))], start, wait)


def _join_rides(rides):
    def split(seq, counts):
        out, at = [], 0
        for n in counts:
            out.append(seq[at:at + n])
            at += n
        return out

    n_in = [len(r.inputs) for r in rides]
    n_out = [len(r.out_shapes) for r in rides]
    n_sem = [len(r.scratch) for r in rides]

    def start(ins, outs, sems):
        for r, i, o, s in zip(rides, split(ins, n_in), split(outs, n_out), split(sems, n_sem)):
            r.start(i, o, s)

    def wait(ins, outs, sems):
        for r, i, o, s in zip(rides, split(ins, n_in), split(outs, n_out), split(sems, n_sem)):
            r.wait(i, o, s)

    return _Ride([a for r in rides for a in r.inputs], [a for r in rides for a in r.out_shapes],
                 [a for r in rides for a in r.scratch], start, wait)


def _run_ride(ride, name):
    n_in, n_out = len(ride.inputs), len(ride.out_shapes)

    def body(*refs):
        ins, outs, sems = refs[:n_in], refs[n_in:n_in + n_out], refs[n_in + n_out:]
        ride.start(ins, outs, sems)
        ride.wait(ins, outs, sems)

    return pl.pallas_call(
        body, name=name, in_specs=[_ANY] * n_in, out_specs=(_ANY,) * n_out, out_shape=tuple(ride.out_shapes),
        scratch_shapes=list(ride.scratch))(*ride.inputs)


def _small_allgather_ride(buf):
    rows, width = buf.shape
    chunks = _row_chunks(rows, COMM_CHUNKS)
    nq = len(chunks)

    def parts(b_ref, all_ref, send_sems, recv_sems, local_sem):
        x, y, c, chips = _position()
        me, sibling = (x, y, c), (x, y, 1 - c)

        def copy(k, block, q, to, src=None):
            rows_q = pl.ds(chunks[q][0], chunks[q][1])
            dst = all_ref.at[4 * block[0] + 2 * block[1] + block[2], rows_q, :]
            return pltpu.make_async_remote_copy(
                src_ref=dst if src is None else src.at[rows_q, :], dst_ref=dst, send_sem=send_sems.at[k * nq + q],
                recv_sem=recv_sems.at[k * nq + q], device_id=to, device_id_type=MESH_ID)

        mine = pltpu.make_async_copy(b_ref, all_ref.at[4 * x + 2 * y + c], local_sem)
        first = []
        for q in range(nq):
            first += [copy(1 + j, me, q, (*chip, c), src=b_ref) for j, chip in enumerate(chips)]
            first.append(copy(0, me, q, sibling, src=b_ref))
        return me, sibling, c, chips, copy, mine, first

    def start(ins, outs, sems):
        _, _, _, _, _, mine, first = parts(ins[0], outs[0], *sems)
        mine.start()
        for cp in first:
            cp.start()

    def wait(ins, outs, sems):
        me, sibling, c, chips, copy, mine, first = parts(ins[0], outs[0], *sems)
        passed = []
        for q in range(nq):
            for j, chip in enumerate(chips):
                copy(1 + j, (*chip, c), q, me).wait_recv()
                fwd = copy(4 + j, (*chip, c), q, sibling)
                fwd.start()
                passed.append(fwd)
        for q in range(nq):
            copy(0, sibling, q, me).wait_recv()
            for j, chip in enumerate(chips):
                copy(4 + j, (*chip, 1 - c), q, me).wait_recv()
        for cp in first + passed:
            cp.wait_send()
        mine.wait()

    return _Ride([buf], [jax.ShapeDtypeStruct((N_DEV, rows, width), F32)],
                 [pltpu.SemaphoreType.DMA((7 * nq,)), pltpu.SemaphoreType.DMA((7 * nq,)), pltpu.SemaphoreType.DMA],
                 start, wait)


def _sum_devices(blocks):
    n, rows, width = blocks.shape
    rb = rows // 2 if (rows // 2) % SUBLANES == 0 else rows

    def kern(b_ref, o_ref):
        total = b_ref[0]
        for d in range(1, n):
            total = total + b_ref[d]
        o_ref[...] = total

    return pl.pallas_call(
        kern, name="small_sum", grid=(rows // rb,), in_specs=[pl.BlockSpec((n, rb, width), lambda i: (0, i, 0))],
        out_specs=pl.BlockSpec((rb, width), lambda i: (i, 0)), out_shape=jax.ShapeDtypeStruct((rows, width), F32),
        compiler_params=_cparams("parallel"))(blocks)


def _small_allgather_sum(buf, head_rows, n_chunks=COMM_CHUNKS):
    rows, width = buf.shape
    chunks = _row_chunks(rows, n_chunks)
    nq = len(chunks)

    def body(b_ref, head_ref, sum_ref, all_ref, send_sems, recv_sems, local_sem):
        x, y, c, chips = _position()
        me, sibling = (x, y, c), (x, y, 1 - c)

        def slot(px, py, pc):
            return all_ref.at[4 * px + 2 * py + pc]

        def copy(k, block, q, to, src=None):
            rows_q = pl.ds(chunks[q][0], chunks[q][1])
            dst = slot(*block).at[rows_q, :]
            return pltpu.make_async_remote_copy(
                src_ref=dst if src is None else src.at[rows_q, :], dst_ref=dst, send_sem=send_sems.at[k * nq + q],
                recv_sem=recv_sems.at[k * nq + q], device_id=to, device_id_type=MESH_ID)

        mine = pltpu.make_async_copy(b_ref, slot(*me), local_sem)
        mine.start()
        first = []
        for q in range(nq):
            first += [copy(1 + j, me, q, (*chip, c), src=b_ref) for j, chip in enumerate(chips)]
            first.append(copy(0, me, q, sibling, src=b_ref))
        for cp in first:
            cp.start()
        passed = []
        for q in range(nq):
            for j, chip in enumerate(chips):
                copy(1 + j, (*chip, c), q, me).wait_recv()
                fwd = copy(4 + j, (*chip, c), q, sibling)
                fwd.start()
                passed.append(fwd)
        for q in range(nq):
            copy(0, sibling, q, me).wait_recv()
            for j, chip in enumerate(chips):
                copy(4 + j, (*chip, 1 - c), q, me).wait_recv()
        for cp in first + passed:
            cp.wait_send()
        mine.wait()
        total = all_ref[0]
        for d in range(1, N_DEV):
            total = total + all_ref[d]
        sum_ref[...] = total
        head_ref[...] = all_ref[:, 0:head_rows, :]

    vm = pl.BlockSpec(memory_space=pltpu.VMEM)
    return pl.pallas_call(
        body, name="small_allgather_sum", in_specs=[vm], out_specs=(vm, vm),
        out_shape=(jax.ShapeDtypeStruct((N_DEV, head_rows, width), F32), jax.ShapeDtypeStruct((rows, width), F32)),
        scratch_shapes=[pltpu.VMEM((N_DEV, rows, width), F32), pltpu.SemaphoreType.DMA((7 * nq,)),
                        pltpu.SemaphoreType.DMA((7 * nq,)), pltpu.SemaphoreType.DMA],
        compiler_params=_cparams(),
    )(buf)


def _rs_pair(g):
    n, rows, width = g.shape
    half = rows // 2
    chunks = _row_chunks(half, COMM_CHUNKS)
    nq = len(chunks)

    def body(g_ref, got_ref, send_sems, recv_sems):
        x, y, c, _ = _position()
        swaps = []
        for k in range(n):
            for q, (start, size) in enumerate(chunks):
                swaps.append(pltpu.make_async_remote_copy(
                    src_ref=g_ref.at[k, pl.ds((1 - c) * half + start, size), :], dst_ref=got_ref.at[k, pl.ds(start, size), :],
                    send_sem=send_sems.at[k * nq + q], recv_sem=recv_sems.at[k * nq + q],
                    device_id=(x, y, 1 - c), device_id_type=MESH_ID))
        for cp in swaps:
            cp.start()
        for cp in swaps:
            cp.wait()

    return pl.pallas_call(
        body, name="rs_pair", in_specs=[_ANY], out_specs=_ANY, out_shape=jax.ShapeDtypeStruct((n, half, width), g.dtype),
        scratch_shapes=[pltpu.SemaphoreType.DMA((n * nq,)), pltpu.SemaphoreType.DMA((n * nq,))],
    )(g)


def _rs_chips_ride(part_bf):
    n, rows, width = part_bf.shape
    chunks = _row_chunks(rows, COMM_CHUNKS)
    nq = len(chunks)

    def sends(pb_ref, got_ref, send_sems, recv_sems):
        x, y, c, chips = _position()
        out = []
        for q, (start, size) in enumerate(chunks):
            for j, chip in enumerate(chips):
                out.append(pltpu.make_async_remote_copy(
                    src_ref=pb_ref.at[2 * chip[0] + chip[1], pl.ds(start, size), :], dst_ref=got_ref.at[j, pl.ds(start, size), :],
                    send_sem=send_sems.at[j * nq + q], recv_sem=recv_sems.at[j * nq + q],
                    device_id=(*chip, c), device_id_type=MESH_ID))
        return out

    def start(ins, outs, sems):
        for cp in sends(ins[0], outs[0], sems[0], sems[1]):
            cp.start()

    def wait(ins, outs, sems):
        for cp in sends(ins[0], outs[0], sems[0], sems[1]):
            cp.wait()

    return _Ride([part_bf], [jax.ShapeDtypeStruct((N_CHIPS - 1, rows, width), BF16)],
                 [pltpu.SemaphoreType.DMA((3 * nq,)), pltpu.SemaphoreType.DMA((3 * nq,))], start, wait)


def _rs_join(shard):
    rows, width = shard.shape
    half = rows // 2
    chunks = _row_chunks(half, COMM_CHUNKS)
    nq = len(chunks)

    def body(in_ref, out_ref, send_sems, recv_sems):
        x, y, c, _ = _position()
        def swap(q, h):
            rows_q = pl.ds(h * half + chunks[q][0], chunks[q][1])
            return pltpu.make_async_remote_copy(
                src_ref=in_ref.at[rows_q, :], dst_ref=out_ref.at[rows_q, :], send_sem=send_sems.at[q],
                recv_sem=recv_sems.at[q], device_id=(x, y, 1 - c), device_id_type=MESH_ID)

        for q in range(nq):
            swap(q, c).start()
        for q in range(nq):
            swap(q, 1 - c).wait_recv()
        for q in range(nq):
            swap(q, c).wait_send()

    return pl.pallas_call(
        body, name="rs_join", in_specs=[_ANY], out_specs=_ANY, input_output_aliases={0: 0},
        out_shape=jax.ShapeDtypeStruct(shard.shape, shard.dtype),
        scratch_shapes=[pltpu.SemaphoreType.DMA((nq,)), pltpu.SemaphoreType.DMA((nq,))],
    )(shard)


def _pair_add(g, got, core):
    n, half, width = got.shape
    nb = 2
    rb = half // nb

    def kern(c_ref, a_ref, b_ref, f_ref, h_ref):
        s = a_ref[...] + b_ref[...]
        f_ref[...] = s
        h_ref[...] = s.astype(BF16)

    spec = pl.BlockSpec((1, rb, width), lambda k, i, c_ref: (k, i, 0))
    return pl.pallas_call(
        kern, name="rs_pair_add",
        grid_spec=pltpu.PrefetchScalarGridSpec(
            num_scalar_prefetch=1, grid=(n, nb),
            in_specs=[pl.BlockSpec((1, rb, width), lambda k, i, c_ref: (k, c_ref[0] * nb + i, 0)), spec],
            out_specs=(spec, spec)),
        out_shape=(jax.ShapeDtypeStruct(got.shape, F32), jax.ShapeDtypeStruct(got.shape, BF16)),
        compiler_params=_cparams("parallel", "parallel"))(core, g, got)


def _chip_add(part_f32, got, where):
    _, rows, width = part_f32.shape
    nb = 2
    rb = rows // nb

    def kern(w_ref, a_ref, b_ref, o_ref):
        o_ref[...] = ((a_ref[0] + b_ref[0].astype(F32)) + b_ref[1].astype(F32)) + b_ref[2].astype(F32)

    return pl.pallas_call(
        kern, name="rs_chip_add",
        grid_spec=pltpu.PrefetchScalarGridSpec(
            num_scalar_prefetch=1, grid=(nb,),
            in_specs=[pl.BlockSpec((1, rb, width), lambda i, w_ref: (w_ref[0], i, 0)),
                      pl.BlockSpec((N_CHIPS - 1, rb, width), lambda i, w_ref: (0, i, 0))],
            out_specs=pl.BlockSpec((rb, width), lambda i, w_ref: (w_ref[1] * nb + i, 0))),
        out_shape=jax.ShapeDtypeStruct((2 * rows, width), F32),
        compiler_params=_cparams("parallel"))(where, part_f32, got)


def _adamw(w, g, m, v, name):
    rows, width = w.shape
    rb = rows
    for cand in (512, 256, 128, 64, 32, 16, 8):
        if rows % cand == 0 and cand * width * 4 <= ADAM_BLOCK_BYTES:
            rb = cand
            break
    spec = pl.BlockSpec((rb, width), lambda i: (i, 0))

    def kern(w_ref, g_ref, m_ref, v_ref, d_ref, nm_ref, nv_ref):
        d_ref[...], nm_ref[...], nv_ref[...] = _adamw_update(w_ref[...], g_ref[...], m_ref[...], v_ref[...])

    shp = jax.ShapeDtypeStruct(w.shape, F32)
    return pl.pallas_call(
        kern, name=name, grid=(rows // rb,), in_specs=[spec] * 4, out_specs=(spec, spec, spec),
        out_shape=(shp, shp, shp), compiler_params=_cparams("parallel"))(w, g, m, v)


def _adamw_update(w, g, m, v):
    nm = ADAM_B1 * m + (1.0 - ADAM_B1) * g
    nv = ADAM_B2 * v + (1.0 - ADAM_B2) * (g * g)
    m_hat = nm / (1.0 - ADAM_B1 ** ADAM_STEP)
    v_hat = nv / (1.0 - ADAM_B2 ** ADAM_STEP)
    return -ADAM_LR * (m_hat / (jnp.sqrt(v_hat) + ADAM_EPS) + ADAM_WD * w), nm, nv


def _adamw_small(params):
    n = len(params)

    def kern(*refs):
        ins, outs = refs[:4 * n], refs[4 * n:]
        for p in range(n):
            w_ref, g_ref, m_ref, v_ref = ins[4 * p:4 * p + 4]
            d, nm, nv = _adamw_update(w_ref[...], g_ref[...], m_ref[...], v_ref[...])
            outs[3 * p][...] = d
            outs[3 * p + 1][...] = nm
            outs[3 * p + 2][...] = nv

    flat = [a for group in params for a in group]
    shapes = [jax.ShapeDtypeStruct(group[0].shape, F32) for group in params for _ in range(3)]
    res = pl.pallas_call(kern, name="adamw_small", out_shape=tuple(shapes), compiler_params=_cparams())(*flat)
    return [tuple(res[3 * p:3 * p + 3]) for p in range(n)]


def _wada_grad(silu_t, dmod_cols):
    n = dmod_cols.shape[1]

    def kern(s_ref, d_ref, o_ref):
        acc = s_ref[:, 0:1] * d_ref[0:1, :]
        for b in range(1, N_DEV):
            acc = acc + s_ref[:, b:b + 1] * d_ref[b:b + 1, :]
        o_ref[...] = acc

    return pl.pallas_call(kern, name="wada_grad", out_shape=jax.ShapeDtypeStruct((D_MODEL, n), F32),
                          compiler_params=_cparams())(silu_t, dmod_cols)


def _rows(a, multiple):
    flat = a.reshape(-1)
    pad = (-flat.shape[0]) % (D_MODEL * multiple)
    if pad:
        flat = jnp.concatenate([flat, jnp.zeros((pad,), flat.dtype)])
    return flat.reshape(-1, D_MODEL)


def _part_rows(shape, multiple):
    return -(-int(np.prod(shape)) // (D_MODEL * multiple)) * multiple


def _pack_rows(parts, multiple, total_multiple=1):
    blocks = [_rows(p, multiple) for p in parts]
    pad = (-sum(b.shape[0] for b in blocks)) % total_multiple
    if pad:
        blocks.append(jnp.zeros((pad, D_MODEL), blocks[0].dtype))
    return jnp.concatenate(blocks, axis=0)


def _unpack_rows(buf, shapes, multiple):
    out, r = [], 0
    for shp in shapes:
        n = int(np.prod(shp))
        nr = _part_rows(shp, multiple)
        out.append(buf[r:r + nr].reshape(-1)[:n].reshape(shp))
        r += nr
    return out


def kernel(x, c, w_ada, b_ada, norm_pre, norm_post, w_in, pool_w, pool_scale, ssm_a_re, ssm_a_im, ssm_log_dt, ssm_b_re, ssm_b_im, ssm_c_re, ssm_c_im, ssm_d, glu_w, glu_b, w_branch_pool, w_branch_ssm, w_out, loss_target, m_w_ada, m_b_ada, m_norm_pre, m_norm_post, m_w_in, m_pool_w, m_pool_scale, m_ssm_a_re, m_ssm_a_im, m_ssm_log_dt, m_ssm_b_re, m_ssm_b_im, m_ssm_c_re, m_ssm_c_im, m_ssm_d, m_glu_w, m_glu_b, m_w_branch_pool, m_w_branch_ssm, m_w_out, v_w_ada, v_b_ada, v_norm_pre, v_norm_post, v_w_in, v_pool_w, v_pool_scale, v_ssm_a_re, v_ssm_a_im, v_ssm_log_dt, v_ssm_b_re, v_ssm_b_im, v_ssm_c_re, v_ssm_c_im, v_ssm_d, v_glu_w, v_glu_b, v_w_branch_pool, v_w_branch_ssm, v_w_out):
    n_ada = w_ada.shape[2]
    n_in = w_in.shape[2]
    n_row = glu_w.shape[1]
    n_pool = pool_w.shape[2]
    n_groups = pool_w.shape[1]

    (g_ada,) = _run_ride(_ag_weights_ride(w_ada[0].astype(BF16)), "ag_weights")
    w_ada_bf = g_ada.transpose(1, 0, 2).reshape(D_MODEL, N_CHIPS * n_ada)
    w_in_own = w_in[0].astype(BF16)
    w_in_ride = _ag_weights_ride(w_in_own)

    def unpack_w_in(g_in):
        return g_in.transpose(1, 0, 2).reshape(D_MODEL, N_CHIPS * n_in)
    pool_rows = n_groups * n_pool * POOL_GW // D_MODEL
    late_shards = [pool_w[0].reshape(n_groups * n_pool, POOL_GW), glu_w[0], w_branch_pool[0], w_branch_ssm[0], w_out[0]]
    late_ride = _join_rides([_ag_weights_ride(s.astype(BF16), n_chunks=2) for s in late_shards])

    def unpack_late(pool, *squares):
        pool = pool.reshape(N_CHIPS, n_groups, n_pool, POOL_GW).transpose(1, 0, 2, 3)
        return (pool.reshape(n_groups, POOL_GW, POOL_GW), *[s.reshape(D_MODEL, D_MODEL) for s in squares])

    chip = 2 * lax.axis_index("x") + lax.axis_index("y")
    core = lax.axis_index("c").astype(jnp.int32)
    kept = {}

    def by_cols(a, n):
        return a.reshape(D_MODEL, N_CHIPS, n).transpose(1, 0, 2).reshape(N_CHIPS, -1, D_MODEL)

    def by_rows(a):
        return a.reshape(N_CHIPS, n_row, D_MODEL)

    def exchange_big(g):
        pool_by_chip = g["d_pool_w"].reshape(n_groups, N_CHIPS, n_pool, POOL_GW).transpose(1, 0, 2, 3)
        blocks = [by_cols(g["d_win"], n_in), by_rows(g["d_glu_w"]), by_rows(g["d_wbp"]), by_rows(g["d_wbs"]),
                  by_rows(g["d_wout"]), pool_by_chip.reshape(N_CHIPS, pool_rows, D_MODEL)]
        pad = (-sum(b.shape[1] for b in blocks)) % (2 * COMM_CHUNKS * COMM_ROW_ALIGN)
        if pad:
            blocks.append(jnp.zeros((N_CHIPS, pad, D_MODEL), F32))
        g_packed = jnp.concatenate(blocks, axis=1)
        kept["part_f32"], part_bf = _pair_add(g_packed, _rs_pair(g_packed), core.reshape(1))
        return _rs_chips_ride(part_bf)

    a_re, a_im, log_dt = ssm_a_re[0], ssm_a_im[0], ssm_log_dt[0].reshape(SSM_G, 1)
    b_re_t, b_im_t = ssm_b_re[0].transpose(2, 0, 1), ssm_b_im[0].transpose(2, 0, 1)
    early_names = ["dg2", "d_pscale", "d_glu_b", "d_dskip", "d_abar_re", "d_abar_im", "d_bb_re_t", "d_bb_im_t",
                   "d_c_re", "d_c_im"]

    def exchange_small(s):
        parts = [s[k] for k in early_names]
        kept["early_shapes"] = [p.shape for p in parts]
        return _small_allgather_ride(_pack_rows(parts, SUBLANES, COMM_CHUNKS * COMM_ROW_ALIGN))

    res = _local_step(x[0], c, loss_target[0], w_ada_bf, b_ada, norm_pre, norm_post, None, None, pool_scale,
                      a_re, a_im, log_dt, b_re_t, b_im_t, ssm_c_re[0], ssm_c_im[0], ssm_d[0], None, glu_b[0:1],
                      None, None, None,
                      split_proj=(w_in_own, chip.astype(jnp.int32).reshape(1), w_in_ride, unpack_w_in, late_ride, unpack_late),
                      ride_for_dw_in=exchange_small, ride_for_dh=exchange_big)

    (all_early,) = res["rode_dw_in"]
    (g_norm_post, g_pscale, g_glu_b, g_dskip, s_abar_re, s_abar_im, s_bb_re, s_bb_im, g_c_re, g_c_im) = _unpack_rows(
        _sum_devices(all_early), kept["early_shapes"], SUBLANES)
    g_a_re, g_a_im, g_log_dt, g_b_re_t, g_b_im_t = _ssm_params_bwd(
        a_re, a_im, log_dt, b_re_t, b_im_t, s_abar_re.reshape(SSM_G, SSM_P), s_abar_im.reshape(SSM_G, SSM_P),
        s_bb_re, s_bb_im)
    late_parts = [res["dmod"], res["silu_c"], res["dg1"], res["loss"].reshape(1, 1)]
    late_shapes = [p.shape for p in late_parts]
    head_rows = _part_rows(late_shapes[0], SUBLANES) + _part_rows(late_shapes[1], SUBLANES)
    all_late, sum_late = _small_allgather_sum(_pack_rows(late_parts, SUBLANES, COMM_ROW_ALIGN), head_rows, n_chunks=1)
    g_b_ada, _, g_norm_pre, loss = _unpack_rows(sum_late, late_shapes, SUBLANES)
    loss = loss[0, 0]
    dmod_all = all_late[:, 0:3].reshape(N_DEV, 3 * D_MODEL)
    dmod_cols = lax.dynamic_slice_in_dim(dmod_all, chip * n_ada, n_ada, axis=1)
    silu_t = all_late[:, _part_rows(late_shapes[0], SUBLANES)].transpose(1, 0)
    g_w_ada = _wada_grad(silu_t, dmod_cols)

    (got_chips,) = res["rode"]
    shard = _rs_join(_chip_add(kept["part_f32"], got_chips, jnp.stack([chip.astype(jnp.int32), core])))
    r = 0
    g_w_in = shard[r:r + n_in].reshape(D_MODEL, n_in)
    r += n_in
    g_squares = []
    for _ in range(4):
        g_squares.append(shard[r:r + n_row])
        r += n_row
    g_glu_w, g_wbp, g_wbs, g_wout = g_squares
    g_pool_w = shard[r:r + pool_rows].reshape(n_groups * n_pool, POOL_GW)

    big = [("w_ada", w_ada[0], g_w_ada, m_w_ada[0], v_w_ada[0]),
           ("w_in", w_in[0], g_w_in, m_w_in[0], v_w_in[0]),
           ("pool_w", pool_w[0].reshape(n_groups * n_pool, POOL_GW), g_pool_w,
            m_pool_w[0].reshape(n_groups * n_pool, POOL_GW), v_pool_w[0].reshape(n_groups * n_pool, POOL_GW)),
           ("glu_w", glu_w[0], g_glu_w, m_glu_w[0], v_glu_w[0]),
           ("w_branch_pool", w_branch_pool[0], g_wbp, m_w_branch_pool[0], v_w_branch_pool[0]),
           ("w_branch_ssm", w_branch_ssm[0], g_wbs, m_w_branch_ssm[0], v_w_branch_ssm[0]),
           ("w_out", w_out[0], g_wout, m_w_out[0], v_w_out[0])]
    out = {}
    for name, w_, g_, m_, v_ in big:
        d_, nm_, nv_ = _adamw(w_, g_, m_, v_, "adamw_" + name)
        out[name] = (g_, d_, nm_, nv_)

    g_b_re = g_b_re_t.transpose(1, 2, 0)
    g_b_im = g_b_im_t.transpose(1, 2, 0)
    small = [("b_ada", b_ada, g_b_ada, m_b_ada, v_b_ada),
             ("norm_pre", norm_pre, g_norm_pre, m_norm_pre, v_norm_pre),
             ("norm_post", norm_post, g_norm_post, m_norm_post, v_norm_post),
             ("pool_scale", pool_scale, g_pscale, m_pool_scale, v_pool_scale),
             ("ssm_a_re", ssm_a_re, g_a_re, m_ssm_a_re, v_ssm_a_re),
             ("ssm_a_im", ssm_a_im, g_a_im, m_ssm_a_im, v_ssm_a_im),
             ("ssm_log_dt", ssm_log_dt, g_log_dt, m_ssm_log_dt, v_ssm_log_dt),
             ("ssm_b_re", ssm_b_re, g_b_re, m_ssm_b_re, v_ssm_b_re),
             ("ssm_b_im", ssm_b_im, g_b_im, m_ssm_b_im, v_ssm_b_im),
             ("ssm_c_re", ssm_c_re, g_c_re, m_ssm_c_re, v_ssm_c_re),
             ("ssm_c_im", ssm_c_im, g_c_im, m_ssm_c_im, v_ssm_c_im),
             ("ssm_d", ssm_d, g_dskip, m_ssm_d, v_ssm_d),
             ("glu_b", glu_b, g_glu_b, m_glu_b, v_glu_b)]
    small = [(name, w_, g_.reshape(w_.shape), m_, v_) for name, w_, g_, m_, v_ in small]
    updates = _adamw_small([t[1:] for t in small])
    for (name, _, g_, _, _), (d_, nm_, nv_) in zip(small, updates):
        out[name] = (g_, d_, nm_, nv_)

    order = ["w_ada", "b_ada", "norm_pre", "norm_post", "w_in", "pool_w", "pool_scale", "ssm_a_re", "ssm_a_im",
             "ssm_log_dt", "ssm_b_re", "ssm_b_im", "ssm_c_re", "ssm_c_im", "ssm_d", "glu_w", "glu_b", "w_branch_pool",
             "w_branch_ssm", "w_out"]
    ref_shape = dict(w_ada=w_ada.shape, w_in=w_in.shape, pool_w=pool_w.shape, glu_w=glu_w.shape,
                     w_branch_pool=w_branch_pool.shape, w_branch_ssm=w_branch_ssm.shape, w_out=w_out.shape)
    for name, w_, _, _, _ in small:
        ref_shape[name] = w_.shape
    results = [loss, res["grad_x"][None]]
    for k in range(4):
        results += [out[name][k].reshape(ref_shape[name]) for name in order]
    return tuple(results)
```

```python
import functools
import math

import numpy as np
import jax
import jax.numpy as jnp
from jax import lax
from jax.experimental import pallas as pl
from jax.experimental.pallas import tpu as pltpu

F32 = jnp.float32
BF16 = jnp.bfloat16
MESH_ID = pl.DeviceIdType.MESH

D_MODEL = 1024
LANES = 128
SUBLANES = 8
SSM_G, SSM_P, SSM_H = 64, 64, 16
LANE_BLOCKS = D_MODEL // LANES
GROUPS_PER_BLOCK = LANES // SSM_H
STATE_W = GROUPS_PER_BLOCK * SSM_P
STATE_ALL = SSM_G * SSM_P
POOL_WINDOWS = (2, 4, 8, 16)
POOL_GW = D_MODEL // len(POOL_WINDOWS)
HALO = 16
RMS_EPS = 1e-6
N_CHIPS = 4
N_DEV = 8

SCAN_CHUNK = 1024
SCAN_BLOCKS = 1
ROW_CHUNK = 256
ROW_CHUNK_WIDE = 512
PROJ_ROWS = 1024
VMEM_LIMIT_BYTES = 56 * 1024 * 1024

ADAM_BLOCK_BYTES = 1 << 20
ADAM_LR, ADAM_B1, ADAM_B2, ADAM_EPS, ADAM_WD, ADAM_STEP = 0.001, 0.9, 0.999, 1e-08, 0.01, 10

_GELU_C0 = math.sqrt(2.0 / math.pi)
_GELU_C1 = 0.044715


def _cparams(*sem):
    if sem:
        return pltpu.CompilerParams(dimension_semantics=sem, vmem_limit_bytes=VMEM_LIMIT_BYTES)
    return pltpu.CompilerParams(vmem_limit_bytes=VMEM_LIMIT_BYTES)


def _sigmoid(v):
    return jax.nn.sigmoid(v)


def _silu(v):
    return v * _sigmoid(v)


def _dsilu(v):
    s = _sigmoid(v)
    return s * (1.0 + v * (1.0 - s))


def _gelu(v):
    return v * (0.5 * (1.0 + jnp.tanh(_GELU_C0 * v * (1.0 + _GELU_C1 * (v * v)))))


def _gelu_and_grad(v):
    v2 = v * v
    t = jnp.tanh(_GELU_C0 * v * (1.0 + _GELU_C1 * v2))
    half = 0.5 * (1.0 + t)
    grad = half + (0.5 * _GELU_C0) * v * (1.0 - t * t) * (1.0 + (3.0 * _GELU_C1) * v2)
    return v * half, grad


def _silu_and_grad(v):
    s = _sigmoid(v)
    return v * s, s * (1.0 + v * (1.0 - s))


def _dot(a, b):
    return lax.dot_general(a, b, (((1,), (0,)), ((), ())), preferred_element_type=F32)


def _dot_nt(a, b):
    return lax.dot_general(a, b, (((1,), (1,)), ((), ())), preferred_element_type=F32)


def _dot_tn(a, b):
    return lax.dot_general(a, b, (((0,), (0,)), ((), ())), preferred_element_type=F32)


def _acc8(v):
    return v.reshape(v.shape[0] // SUBLANES, SUBLANES, v.shape[1]).sum(axis=0)


class _Ride:
    def __init__(self, inputs, out_shapes, scratch, start, wait):
        self.inputs, self.out_shapes, self.scratch, self.start, self.wait = inputs, out_shapes, scratch, start, wait


def _mm(a_parts, b_parts, *, name, ta=False, tb=False, out_dtype=F32, bm=512, bn=512, bk=512, ride=None):
    a_parts, b_parts = list(a_parts), list(b_parts)
    if ta:
        assert len(a_parts) == 1
        k_dim, m_dim = a_parts[0].shape
    else:
        m_dim = a_parts[0].shape[0]
        k_dim = sum(a.shape[1] for a in a_parts)
    if tb:
        assert len(b_parts) == 1
        n_dim = b_parts[0].shape[0]
    else:
        n_dim = sum(b.shape[1] for b in b_parts)
    bm, bn, bk = min(bm, m_dim), min(bn, n_dim), min(bk, k_dim)
    nm, nn, nk = m_dim // bm, n_dim // bn, k_dim // bk
    a_ranges, off = [], 0
    for a in a_parts:
        cnt = (a.shape[0] if ta else a.shape[1]) // bk
        a_ranges.append((off, cnt))
        off += cnt
    b_ranges, off = [], 0
    for b in b_parts:
        cnt = (b.shape[0] if tb else b.shape[1]) // bn
        b_ranges.append((off, cnt))
        off += cnt

    def a_spec(off, cnt):
        if ta:
            return pl.BlockSpec((bk, bm), lambda i, n, k: (k, i))
        return pl.BlockSpec((bm, bk), lambda i, n, k: (i, jnp.clip(k - off, 0, cnt - 1)))

    def b_spec(off, cnt):
        if tb:
            return pl.BlockSpec((bn, bk), lambda i, n, k: (n, k))
        return pl.BlockSpec((bk, bn), lambda i, n, k: (k, jnp.clip(n - off, 0, cnt - 1)))

    na, nb = len(a_parts), len(b_parts)
    dims = (((0 if ta else 1,), (1 if tb else 0,)), ((), ()))

    def kern_single(a_ref, b_ref, o_ref):
        o_ref[...] = lax.dot_general(a_ref[...].astype(BF16), b_ref[...].astype(BF16), dims,
                                     preferred_element_type=F32).astype(out_dtype)

    if na == 1 and nb == 1 and nk == 1 and not ride:
        return pl.pallas_call(
            kern_single, name=name, grid=(nm, nn),
            in_specs=[pl.BlockSpec((bk, bm), lambda i, n: (0, i)) if ta else pl.BlockSpec((bm, bk), lambda i, n: (i, 0)),
                      pl.BlockSpec((bn, bk), lambda i, n: (n, 0)) if tb else pl.BlockSpec((bk, bn), lambda i, n: (0, n))],
            out_specs=pl.BlockSpec((bm, bn), lambda i, n: (i, n)),
            out_shape=jax.ShapeDtypeStruct((m_dim, n_dim), out_dtype),
            compiler_params=_cparams("parallel", "parallel"),
        )(a_parts[0], b_parts[0])

    n_rin = len(ride.inputs) if ride else 0
    n_rout = len(ride.out_shapes) if ride else 0

    def kern(*refs):
        a_refs, b_refs = refs[:na], refs[na:na + nb]
        rin = refs[na + nb:na + nb + n_rin]
        o_ref = refs[na + nb + n_rin]
        rout = refs[na + nb + n_rin + 1:na + nb + n_rin + 1 + n_rout]
        acc = refs[na + nb + n_rin + 1 + n_rout]
        rsem = refs[na + nb + n_rin + 2 + n_rout:]
        i, n, k = pl.program_id(0), pl.program_id(1), pl.program_id(2)

        if ride:
            @pl.when((i == 0) & (n == 0) & (k == 0))
            def _():
                ride.start(rin, rout, rsem)

        if nk > 1:
            @pl.when(k == 0)
            def _():
                acc[...] = jnp.zeros_like(acc)

        for ja, (koff, kcnt) in enumerate(a_ranges):
            for jb, (noff, ncnt) in enumerate(b_ranges):
                def step(ja=ja, jb=jb):
                    a = a_refs[ja][...].astype(BF16)
                    b = b_refs[jb][...].astype(BF16)
                    prod = lax.dot_general(a, b, dims, preferred_element_type=F32)
                    if nk > 1:
                        acc[...] += prod
                    else:
                        o_ref[...] = prod.astype(out_dtype)

                if na == 1 and nb == 1:
                    step()
                else:
                    cond = (k >= koff) & (k < koff + kcnt) & (n >= noff) & (n < noff + ncnt)
                    pl.when(cond)(step)

        if nk > 1:
            @pl.when(k == nk - 1)
            def _():
                o_ref[...] = acc[...].astype(out_dtype)

        if ride:
            @pl.when((i == nm - 1) & (n == nn - 1) & (k == nk - 1))
            def _():
                ride.wait(rin, rout, rsem)

    any_spec = pl.BlockSpec(memory_space=pl.ANY)
    out_spec = pl.BlockSpec((bm, bn), lambda i, n, k: (i, n))
    out_shape = jax.ShapeDtypeStruct((m_dim, n_dim), out_dtype)
    acc_shape = pltpu.VMEM((bm, bn) if nk > 1 else (SUBLANES, LANES), F32)
    if not ride:
        return pl.pallas_call(
            kern, name=name, grid=(nm, nn, nk),
            in_specs=[a_spec(*r) for r in a_ranges] + [b_spec(*r) for r in b_ranges],
            out_specs=out_spec, out_shape=out_shape, scratch_shapes=[acc_shape],
            compiler_params=_cparams("parallel", "parallel", "arbitrary"),
        )(*a_parts, *b_parts)
    return pl.pallas_call(
        kern, name=name, grid=(nm, nn, nk),
        in_specs=[a_spec(*r) for r in a_ranges] + [b_spec(*r) for r in b_ranges] + [any_spec] * n_rin,
        out_specs=(out_spec,) + (any_spec,) * n_rout, out_shape=(out_shape,) + tuple(ride.out_shapes),
        scratch_shapes=[acc_shape] + list(ride.scratch),
        compiler_params=_cparams("arbitrary", "arbitrary", "arbitrary"),
    )(*a_parts, *b_parts, *ride.inputs)


def _ssm_param_fn(a_re, a_im, log_dt, b_re, b_im):
    dt = jnp.exp(log_dt)
    lam_re = jnp.minimum(a_re, -1e-4)
    lam_im = a_im
    mag = jnp.exp(lam_re * dt)
    abar_re = mag * jnp.cos(lam_im * dt)
    abar_im = mag * jnp.sin(lam_im * dt)
    den = lam_re * lam_re + lam_im * lam_im
    num_re = abar_re - 1.0
    f_re = (num_re * lam_re + abar_im * lam_im) / den
    f_im = (abar_im * lam_re - num_re * lam_im) / den
    bb_re = f_re * b_re - f_im * b_im
    bb_im = f_re * b_im + f_im * b_re
    return abar_re, abar_im, bb_re, bb_im


def _ssm_params(a_re, a_im, log_dt, b_re_t, b_im_t):
    def kern(are, aim, ldt, bre, bim, o_ar, o_ai, o_br, o_bi):
        ar, ai, br, bi = _ssm_param_fn(are[...], aim[...], ldt[...], bre[...], bim[...])
        o_ar[...] = ar
        o_ai[...] = ai
        o_br[...] = br
        o_bi[...] = bi

    gp = jax.ShapeDtypeStruct((SSM_G, SSM_P), F32)
    hgp = jax.ShapeDtypeStruct((SSM_H, SSM_G, SSM_P), F32)
    return pl.pallas_call(kern, name="ssm_params", out_shape=(gp, gp, hgp, hgp), compiler_params=_cparams())(
        a_re, a_im, log_dt, b_re_t, b_im_t)


def _ssm_params_bwd(a_re, a_im, log_dt, b_re_t, b_im_t, d_ar, d_ai, d_bbr, d_bbi):
    def kern(are, aim, ldt, bre, bim, dar, dai, dbr, dbi, o_are, o_aim, o_ldt, o_bre, o_bim):
        prim = (are[...], aim[...], ldt[...], bre[...], bim[...])
        _, vjp = jax.vjp(_ssm_param_fn, *prim)
        g = vjp((dar[...], dai[...], dbr[...], dbi[...]))
        o_are[...] = g[0]
        o_aim[...] = g[1]
        o_ldt[...] = g[2]
        o_bre[...] = g[3]
        o_bim[...] = g[4]

    gp = jax.ShapeDtypeStruct((SSM_G, SSM_P), F32)
    g1 = jax.ShapeDtypeStruct((SSM_G, 1), F32)
    hgp = jax.ShapeDtypeStruct((SSM_H, SSM_G, SSM_P), F32)
    return pl.pallas_call(kern, name="ssm_params_bwd", out_shape=(gp, gp, g1, hgp, hgp), compiler_params=_cparams())(
        a_re, a_im, log_dt, b_re_t, b_im_t, d_ar, d_ai, d_bbr, d_bbi)


def _pow_tables(abar_re, abar_im, tc):
    ls = tc // SUBLANES

    def kern(ar_ref, ai_ref, fr_ref, fi_ref, rr_ref, ri_ref):
        a_re = jnp.broadcast_to(ar_ref[...], (SUBLANES, STATE_W))
        a_im = jnp.broadcast_to(ai_ref[...], (SUBLANES, STATE_W))
        p_re, p_im = a_re, a_im
        for i in range(ls):
            fwd = pl.ds(SUBLANES * i, SUBLANES)
            rev = pl.ds(SUBLANES * (ls - 1 - i), SUBLANES)
            fr_ref[fwd, :] = p_re
            fi_ref[fwd, :] = p_im
            rr_ref[rev, :] = p_re
            ri_ref[rev, :] = p_im
            p_re, p_im = p_re * a_re - p_im * a_im, p_re * a_im + p_im * a_re

    vec = pl.BlockSpec((1, STATE_W), lambda b: (0, b))
    tab = pl.BlockSpec((tc, STATE_W), lambda b: (0, b))
    shp = jax.ShapeDtypeStruct((tc, STATE_ALL), F32)
    return pl.pallas_call(
        kern, name="pow_tables", grid=(LANE_BLOCKS,), in_specs=[vec, vec], out_specs=(tab, tab, tab, tab),
        out_shape=(shp, shp, shp, shp), compiler_params=_cparams("parallel"))(abar_re, abar_im)


def _mod_kernel(c_row, w_ada_bf, b_ada):
    def kern(c_ref, w_ref, b_ref, m_ref, s_ref):
        cv = c_ref[...]
        sc = _silu(cv)
        s_ref[...] = sc
        lhs = jnp.broadcast_to(sc, (SUBLANES, D_MODEL)).astype(BF16)
        m_ref[...] = _dot(lhs, w_ref[...]) + b_ref[...]

    return pl.pallas_call(
        kern, name="ada_mod",
        out_shape=(jax.ShapeDtypeStruct((SUBLANES, 3 * D_MODEL), F32), jax.ShapeDtypeStruct((1, D_MODEL), F32)),
        compiler_params=_cparams())(c_row, w_ada_bf, b_ada)


def _row_spec(tr, width=D_MODEL, col=0):
    return pl.BlockSpec((tr, width), lambda c: (c, col))


def _vec_spec(width=D_MODEL):
    return pl.BlockSpec((1, width), lambda c: (0, 0))


def _col_spec(tr):
    return pl.BlockSpec((D_MODEL, tr), lambda c: (0, c))


def _in_norm(x, g1, scale, shift):
    seq = x.shape[0]
    tr = min(ROW_CHUNK_WIDE, seq)

    def kern(x_ref, g_ref, sc_ref, sh_ref, h_ref, ht_ref):
        xv = x_ref[...]
        r = lax.rsqrt(jnp.mean(xv * xv, axis=-1, keepdims=True) + RMS_EPS)
        h = ((xv * r) * g_ref[...]) * (1.0 + sc_ref[...]) + sh_ref[...]
        h_ref[...] = h.astype(BF16)
        ht_ref[...] = h.T.astype(BF16)

    return pl.pallas_call(
        kern, name="in_norm", grid=(seq // tr,),
        in_specs=[_row_spec(tr), _vec_spec(), _vec_spec(), _vec_spec()], out_specs=(_row_spec(tr), _col_spec(tr)),
        out_shape=(jax.ShapeDtypeStruct((seq, D_MODEL), BF16), jax.ShapeDtypeStruct((D_MODEL, seq), BF16)),
        compiler_params=_cparams("parallel"))(x, g1, scale, shift)


PAD = SUBLANES


def _window_sums(src, cols, w, bufs, rows, ahead):
    cur, cur_cols, step, k = src, cols, 1, 0
    data = pl.ds(PAD, rows)
    while step < w:
        dst = bufs[k % 2]
        dst[data, :] = cur[data, cur_cols] + cur[pl.ds(PAD + (step if ahead else -step), rows), cur_cols]
        cur, cur_cols, step, k = dst, slice(None), 2 * step, k + 1
    return cur, cur_cols


def _in_norm_proj_own(x, g1, scale, shift, w_own, chip, ride):
    seq, n_own = x.shape[0], w_own.shape[1]
    tr = min(PROJ_ROWS, seq)
    nc = seq // tr
    n_rin, n_rout = len(ride.inputs), len(ride.out_shapes)

    def kern(chip_ref, x_ref, g_ref, sc_ref, sh_ref, w_ref, *rest):
        rin, (h_ref, ht_ref, p_ref) = rest[:n_rin], rest[n_rin:n_rin + 3]
        rout, rsem = rest[n_rin + 3:n_rin + 3 + n_rout], rest[n_rin + 3 + n_rout:]
        c = pl.program_id(0)

        @pl.when(c == 0)
        def _():
            ride.start(rin, rout, rsem)

        xv = x_ref[...]
        r = lax.rsqrt(jnp.mean(xv * xv, axis=-1, keepdims=True) + RMS_EPS)
        h = ((xv * r) * g_ref[...]) * (1.0 + sc_ref[...]) + sh_ref[...]
        hb = h.astype(BF16)
        h_ref[...] = hb
        ht_ref[...] = h.T.astype(BF16)
        p_ref[...] = _dot(hb, w_ref[...]).astype(BF16)

        @pl.when(c == nc - 1)
        def _():
            ride.wait(rin, rout, rsem)

    vec = pl.BlockSpec((1, D_MODEL), lambda c, k: (0, 0))
    return pl.pallas_call(
        kern, name="in_norm_proj_own",
        grid_spec=pltpu.PrefetchScalarGridSpec(
            num_scalar_prefetch=1, grid=(nc,),
            in_specs=[pl.BlockSpec((tr, D_MODEL), lambda c, k: (c, 0)), vec, vec, vec,
                      pl.BlockSpec((D_MODEL, n_own), lambda c, k: (0, 0))] + [_ANY] * n_rin,
            out_specs=(pl.BlockSpec((tr, D_MODEL), lambda c, k: (c, 0)), pl.BlockSpec((D_MODEL, tr), lambda c, k: (0, c)),
                       pl.BlockSpec((tr, n_own), lambda c, k: (c, k[0]))) + (_ANY,) * n_rout,
            scratch_shapes=list(ride.scratch)),
        out_shape=(jax.ShapeDtypeStruct((seq, D_MODEL), BF16), jax.ShapeDtypeStruct((D_MODEL, seq), BF16),
                   jax.ShapeDtypeStruct((seq, N_CHIPS * n_own), BF16)) + tuple(ride.out_shapes),
        compiler_params=_cparams("arbitrary"))(chip, x, g1, scale, shift, w_own, *ride.inputs)


def _proj_rest(h, w_blocks, proj, chip, ride):
    seq, n_own = h.shape[0], w_blocks.shape[2]
    tr = min(PROJ_ROWS, seq)
    nm, nn = seq // tr, N_CHIPS - 1
    n_rin, n_rout = len(ride.inputs), len(ride.out_shapes)

    def kern(chip_ref, h_ref, w_ref, _, *rest):
        rin, p_ref = rest[:n_rin], rest[n_rin]
        rout, rsem = rest[n_rin + 1:n_rin + 1 + n_rout], rest[n_rin + 1 + n_rout:]
        i, n = pl.program_id(0), pl.program_id(1)

        @pl.when((i == 0) & (n == 0))
        def _():
            ride.start(rin, rout, rsem)

        p_ref[...] = _dot(h_ref[...], w_ref[0]).astype(BF16)

        @pl.when((i == nm - 1) & (n == nn - 1))
        def _():
            ride.wait(rin, rout, rsem)

    def other(n, k):
        return (k[0] + 1 + n) % N_CHIPS

    return pl.pallas_call(
        kern, name="proj_rest",
        grid_spec=pltpu.PrefetchScalarGridSpec(
            num_scalar_prefetch=1, grid=(nm, nn),
            in_specs=[pl.BlockSpec((tr, D_MODEL), lambda i, n, k: (i, 0)),
                      pl.BlockSpec((1, D_MODEL, n_own), lambda i, n, k: (other(n, k), 0, 0)), _ANY] + [_ANY] * n_rin,
            out_specs=(pl.BlockSpec((tr, n_own), lambda i, n, k: (i, other(n, k))),) + (_ANY,) * n_rout,
            scratch_shapes=list(ride.scratch)),
        out_shape=(jax.ShapeDtypeStruct(proj.shape, BF16),) + tuple(ride.out_shapes),
        input_output_aliases={3: 0},
        compiler_params=_cparams("arbitrary", "arbitrary"))(chip, h, w_blocks, proj, *ride.inputs)


def _pool_windows(ext, bufs, pos, g, w, tr):
    cols = pl.ds(g * POOL_GW, POOL_GW)
    chunk = pl.ds(PAD + HALO, tr)
    cur = ext[chunk, cols]
    win, win_cols = _window_sums(ext, cols, w, bufs, HALO + tr, ahead=False)
    cnt = jnp.minimum(pos + 1, w).astype(F32)
    return win[chunk, win_cols] / cnt - cur


def _zero_pads(refs, rows):
    for ref in refs:
        ref[0:PAD, :] = jnp.zeros((PAD, ref.shape[1]), F32)
        ref[PAD + rows:, :] = jnp.zeros((PAD, ref.shape[1]), F32)


def _pool_fwd(proj, pool_w_bf, pscale):
    seq = proj.shape[0]
    tr = min(ROW_CHUNK_WIDE, seq)
    hb = tr // HALO

    def kern(up_ref, halo_ref, zp_ref, pw_ref, ps_ref, y_ref, yt_ref, ext, buf_a, buf_b):
        c = pl.program_id(0)
        _zero_pads((ext, buf_a, buf_b), HALO + tr)
        ext[pl.ds(PAD, HALO), :] = jnp.where(c > 0, halo_ref[...].astype(F32), 0.0)
        ext[pl.ds(PAD + HALO, tr), :] = up_ref[...].astype(F32)
        pos = c * tr + lax.broadcasted_iota(jnp.int32, (tr, POOL_GW), 0)
        for g, w in enumerate(POOL_WINDOWS):
            cols = pl.ds(g * POOL_GW, POOL_GW)
            pooled = _pool_windows(ext, (buf_a, buf_b), pos, g, w, tr)
            mixed = _dot(pooled.astype(BF16), pw_ref[g])
            y = mixed * ps_ref[:, cols] * _silu(zp_ref[:, cols].astype(F32))
            y_ref[:, cols] = y.astype(BF16)
            yt_ref[cols, :] = y.T.astype(BF16)

    return pl.pallas_call(
        kern, name="pool_fwd", grid=(seq // tr,),
        in_specs=[_row_spec(tr, col=0),
                  pl.BlockSpec((HALO, D_MODEL), lambda c: (jnp.maximum(c * hb - 1, 0), 0)),
                  _row_spec(tr, col=1),
                  pl.BlockSpec((len(POOL_WINDOWS), POOL_GW, POOL_GW), lambda c: (0, 0, 0)),
                  _vec_spec()],
        out_specs=(_row_spec(tr), _col_spec(tr)),
        out_shape=(jax.ShapeDtypeStruct((seq, D_MODEL), BF16), jax.ShapeDtypeStruct((D_MODEL, seq), BF16)),
        scratch_shapes=[pltpu.VMEM((tr + HALO + 2 * PAD, D_MODEL), F32), pltpu.VMEM((tr + HALO + 2 * PAD, POOL_GW), F32),
                        pltpu.VMEM((tr + HALO + 2 * PAD, POOL_GW), F32)],
        compiler_params=_cparams("parallel"))(proj, proj, proj, pool_w_bf, pscale)


def _pool_bwd(proj, dbp, wbp_bf, pool_w_bf, pscale, dproj):
    seq = proj.shape[0]
    tr = min(ROW_CHUNK_WIDE, seq)
    hb = tr // HALO
    nc = seq // tr
    n_halo = seq // HALO

    def kern(up_ref, halo_ref, zp_ref, zpn_ref, dbp_ref, dbpn_ref, wbp_ref, pw_ref, ps_ref, _,
             d01_ref, dpw_ref, dps_ref, ext, dpn, buf_a, buf_b, acc_pw, acc_ps):
        c = pl.program_id(0)

        @pl.when(c == 0)
        def _():
            acc_pw[...] = jnp.zeros_like(acc_pw)
            acc_ps[...] = jnp.zeros_like(acc_ps)

        _zero_pads((ext, dpn, buf_a, buf_b), HALO + tr)
        ext[pl.ds(PAD, HALO), :] = jnp.where(c > 0, halo_ref[...].astype(F32), 0.0)
        ext[pl.ds(PAD + HALO, tr), :] = up_ref[...].astype(F32)
        pos = c * tr + lax.broadcasted_iota(jnp.int32, (tr, POOL_GW), 0)
        pos_n = (c + 1) * tr + lax.broadcasted_iota(jnp.int32, (HALO, POOL_GW), 0)
        has_next = c < nc - 1
        for g, w in enumerate(POOL_WINDOWS):
            cols = pl.ds(g * POOL_GW, POOL_GW)
            pooled_bf = _pool_windows(ext, (buf_a, buf_b), pos, g, w, tr).astype(BF16)
            wg = pw_ref[g]
            mixed = _dot(pooled_bf, wg)
            zp = zp_ref[:, cols].astype(F32)
            sz = _silu(zp)
            wbp_g = wbp_ref[cols, :]
            dyp_g = _dot_nt(dbp_ref[...], wbp_g)
            ps = ps_ref[:, cols]
            dmixed = (dyp_g * ps * sz).astype(BF16)
            acc_ps[:, cols] += _acc8(dyp_g * mixed * sz)
            d01_ref[:, pl.ds(D_MODEL + g * POOL_GW, POOL_GW)] = (dyp_g * mixed * ps * _dsilu(zp)).astype(BF16)
            acc_pw[g] += _dot_tn(pooled_bf, dmixed)
            dpooled = _dot_nt(dmixed, wg)
            dmixed_n = (jnp.where(has_next, _dot_nt(dbpn_ref[...], wbp_g), 0.0) * ps * _silu(zpn_ref[:, cols].astype(F32))).astype(BF16)
            dpooled_n = _dot_nt(dmixed_n, wg)
            dpn[pl.ds(PAD, tr), :] = dpooled / jnp.minimum(pos + 1, w).astype(F32)
            dpn[pl.ds(PAD + tr, HALO), :] = dpooled_n / jnp.minimum(pos_n + 1, w).astype(F32)
            win, _ = _window_sums(dpn, slice(None), w, (buf_a, buf_b), tr + HALO, ahead=True)
            d01_ref[:, cols] = (win[pl.ds(PAD, tr), :] - dpooled).astype(BF16)

        @pl.when(c == nc - 1)
        def _():
            dpw_ref[...] = acc_pw[...]
            dps_ref[...] = jnp.sum(acc_ps[...], axis=0, keepdims=True)

    nxt = lambda c: (jnp.minimum((c + 1) * hb, n_halo - 1), 0)
    nxt1 = lambda c: (jnp.minimum((c + 1) * hb, n_halo - 1), 1)
    return pl.pallas_call(
        kern, name="pool_bwd", grid=(nc,),
        in_specs=[_row_spec(tr, col=0),
                  pl.BlockSpec((HALO, D_MODEL), lambda c: (jnp.maximum(c * hb - 1, 0), 0)),
                  _row_spec(tr, col=1),
                  pl.BlockSpec((HALO, D_MODEL), nxt1),
                  _row_spec(tr),
                  pl.BlockSpec((HALO, D_MODEL), nxt),
                  pl.BlockSpec((D_MODEL, D_MODEL), lambda c: (0, 0)),
                  pl.BlockSpec((len(POOL_WINDOWS), POOL_GW, POOL_GW), lambda c: (0, 0, 0)),
                  _vec_spec(), _ANY],
        out_specs=(pl.BlockSpec((tr, 2 * D_MODEL), lambda c: (c, 0)),
                   pl.BlockSpec((len(POOL_WINDOWS), POOL_GW, POOL_GW), lambda c: (0, 0, 0)),
                   _vec_spec()),
        out_shape=(jax.ShapeDtypeStruct(dproj.shape, BF16),
                   jax.ShapeDtypeStruct((len(POOL_WINDOWS), POOL_GW, POOL_GW), F32),
                   jax.ShapeDtypeStruct((1, D_MODEL), F32)),
        scratch_shapes=[pltpu.VMEM((tr + HALO + 2 * PAD, D_MODEL), F32)]
        + [pltpu.VMEM((tr + HALO + 2 * PAD, POOL_GW), F32)] * 3
        + [pltpu.VMEM((len(POOL_WINDOWS), POOL_GW, POOL_GW), F32), pltpu.VMEM((SUBLANES, D_MODEL), F32)],
        input_output_aliases={9: 0},
        compiler_params=_cparams("arbitrary"))(proj, proj, proj, proj, dbp, dbp, wbp_bf, pool_w_bf, pscale, dproj)


def _glu_fwd(ys, proj, glu_w_bf, glu_b):
    seq = ys.shape[0]
    tr = min(ROW_CHUNK_WIDE, seq)

    def kern(ys_ref, zs_ref, w_ref, b_ref, o_ref, ot_ref):
        yg = _gelu(ys_ref[...])
        q = _dot(yg.astype(BF16), w_ref[...]) + b_ref[...]
        y = yg * _sigmoid(q) * _silu(zs_ref[...].astype(F32))
        o_ref[...] = y.astype(BF16)
        ot_ref[...] = y.T.astype(BF16)

    return pl.pallas_call(
        kern, name="glu_fwd", grid=(seq // tr,),
        in_specs=[_row_spec(tr), _row_spec(tr, col=3), pl.BlockSpec((D_MODEL, D_MODEL), lambda c: (0, 0)), _vec_spec()],
        out_specs=(_row_spec(tr), _col_spec(tr)),
        out_shape=(jax.ShapeDtypeStruct((seq, D_MODEL), BF16), jax.ShapeDtypeStruct((D_MODEL, seq), BF16)),
        compiler_params=_cparams("parallel"))(ys, proj, glu_w_bf, glu_b)


def _glu_bwd(ys, proj, dbs, wbs_bf, glu_w_bf, glu_b, dproj):
    seq = ys.shape[0]
    tr = min(ROW_CHUNK_WIDE, seq)
    nc = seq // tr

    def kern(ys_ref, zs_ref, dbs_ref, wbs_ref, w_ref, b_ref, _, dys_ref, dzs_ref, dq_ref, yg_ref, db_ref, acc_b):
        c = pl.program_id(0)

        @pl.when(c == 0)
        def _():
            acc_b[...] = jnp.zeros_like(acc_b)

        yg, dgelu = _gelu_and_grad(ys_ref[...])
        yg_bf = yg.astype(BF16)
        q = _dot(yg_bf, w_ref[...]) + b_ref[...]
        sg = _sigmoid(q)
        silu_z, dsilu_z = _silu_and_grad(zs_ref[...].astype(F32))
        dyv = _dot_nt(dbs_ref[...], wbs_ref[...])
        dyglu = dyv * silu_z
        yglu = yg * sg
        dzs_ref[...] = (dyv * yglu * dsilu_z).astype(BF16)
        dq = dyglu * yglu * (1.0 - sg)
        dq_bf = dq.astype(BF16)
        acc_b[...] += _acc8(dq)
        dyg = dyglu * sg + _dot_nt(dq_bf, w_ref[...])
        dys_ref[...] = dyg * dgelu
        dq_ref[...] = dq_bf
        yg_ref[...] = yg.T.astype(BF16)

        @pl.when(c == nc - 1)
        def _():
            db_ref[...] = jnp.sum(acc_b[...], axis=0, keepdims=True)

    bf = jax.ShapeDtypeStruct((seq, D_MODEL), BF16)
    return pl.pallas_call(
        kern, name="glu_bwd", grid=(nc,),
        in_specs=[_row_spec(tr), _row_spec(tr, col=3), _row_spec(tr),
                  pl.BlockSpec((D_MODEL, D_MODEL), lambda c: (0, 0)),
                  pl.BlockSpec((D_MODEL, D_MODEL), lambda c: (0, 0)), _vec_spec(), _ANY],
        out_specs=(_row_spec(tr), _row_spec(tr, col=3), _row_spec(tr), _col_spec(tr), _vec_spec()),
        out_shape=(jax.ShapeDtypeStruct((seq, D_MODEL), F32), jax.ShapeDtypeStruct(dproj.shape, BF16), bf,
                   jax.ShapeDtypeStruct((D_MODEL, seq), BF16), jax.ShapeDtypeStruct((1, D_MODEL), F32)),
        scratch_shapes=[pltpu.VMEM((SUBLANES, D_MODEL), F32)],
        input_output_aliases={6: 1},
        compiler_params=_cparams("arbitrary"))(ys, proj, dbs, wbs_bf, glu_w_bf, glu_b, dproj)


def _out_fwd_bwd(ypool, yssm, proj, x, tgt, gate, g2, wbp_bf, wbs_bf, wout_bf):
    seq = x.shape[0]
    tr = min(ROW_CHUNK, seq)
    nc = seq // tr

    def kern(yp_ref, ysm_ref, gp_ref, gs_ref, x_ref, t_ref, gate_ref, g2_ref, wbp_ref, wbs_ref, wo_ref,
             dy_ref, d45_ref, mb_ref, dob_ref, dbp_ref, dbs_ref, loss_ref, dgate_ref, dg2_ref,
             acc_l, acc_gate, acc_g2):
        c = pl.program_id(0)

        @pl.when(c == 0)
        def _():
            acc_l[...] = jnp.zeros_like(acc_l)
            acc_gate[...] = jnp.zeros_like(acc_gate)
            acc_g2[...] = jnp.zeros_like(acc_g2)

        bp = _dot(yp_ref[...], wbp_ref[...])
        bs = _dot(ysm_ref[...], wbs_ref[...])
        sp = _sigmoid(gp_ref[...].astype(F32))
        ss = _sigmoid(gs_ref[...].astype(F32))
        merged = sp * bp + ss * bs
        mb = merged.astype(BF16)
        out = _dot(mb, wo_ref[...])
        r2 = lax.rsqrt(jnp.mean(out * out, axis=-1, keepdims=True) + RMS_EPS)
        oh = out * r2
        gate_v, g2_v = gate_ref[...], g2_ref[...]
        ohg = oh * g2_v
        diff = (x_ref[...] + gate_v * ohg) - t_ref[...]
        acc_l[...] += _acc8(diff * diff)
        dyv = diff * (1.0 / D_MODEL)
        dy_ref[...] = dyv
        dy_oh = dyv * oh
        acc_gate[...] += _acc8(dy_oh * g2_v)
        acc_g2[...] += _acc8(dy_oh * gate_v)
        gg = gate_v * g2_v
        doh = dyv * gg
        dout = r2 * (doh - oh * jnp.mean(dy_oh * gg, axis=-1, keepdims=True))
        dob = dout.astype(BF16)
        dmerged = _dot_nt(dob, wo_ref[...])
        dbp_f = dmerged * sp
        dbs_f = dmerged * ss
        dbp = dbp_f.astype(BF16)
        dbs = dbs_f.astype(BF16)
        d45_ref[:, 0:D_MODEL] = (dbp_f * bp * (1.0 - sp)).astype(BF16)
        d45_ref[:, D_MODEL:] = (dbs_f * bs * (1.0 - ss)).astype(BF16)
        mb_ref[...] = merged.T.astype(BF16)
        dob_ref[...] = dob
        dbp_ref[...] = dbp
        dbs_ref[...] = dbs

        @pl.when(c == nc - 1)
        def _():
            tot = jnp.sum(acc_l[...], axis=0, keepdims=True)
            loss_ref[...] = jnp.sum(tot, axis=1, keepdims=True) * (0.5 / D_MODEL)
            dgate_ref[...] = jnp.sum(acc_gate[...], axis=0, keepdims=True)
            dg2_ref[...] = jnp.sum(acc_g2[...], axis=0, keepdims=True)

    wspec = pl.BlockSpec((D_MODEL, D_MODEL), lambda c: (0, 0))
    f32 = jax.ShapeDtypeStruct((seq, D_MODEL), F32)
    bf = jax.ShapeDtypeStruct((seq, D_MODEL), BF16)
    vec = jax.ShapeDtypeStruct((1, D_MODEL), F32)
    acc = pltpu.VMEM((SUBLANES, D_MODEL), F32)
    return pl.pallas_call(
        kern, name="out_fwd_bwd", grid=(nc,),
        in_specs=[_row_spec(tr), _row_spec(tr), _row_spec(tr, col=4), _row_spec(tr, col=5), _row_spec(tr), _row_spec(tr),
                  _vec_spec(), _vec_spec(), wspec, wspec, wspec],
        out_specs=(_row_spec(tr), pl.BlockSpec((tr, 2 * D_MODEL), lambda c: (c, 2)),
                   _col_spec(tr), _row_spec(tr), _row_spec(tr), _row_spec(tr),
                   pl.BlockSpec((1, 1), lambda c: (0, 0)), _vec_spec(), _vec_spec()),
        out_shape=(f32, jax.ShapeDtypeStruct((seq, proj.shape[1]), BF16),
                   jax.ShapeDtypeStruct((D_MODEL, seq), BF16), bf, bf, bf,
                   jax.ShapeDtypeStruct((1, 1), F32), vec, vec),
        scratch_shapes=[acc, acc, acc],
        compiler_params=_cparams("arbitrary"))(ypool, yssm, proj, proj, x, tgt, gate, g2, wbp_bf, wbs_bf, wout_bf)


def _in_bwd(dh, x, dy, g1, scale):
    seq = x.shape[0]
    tr = min(ROW_CHUNK_WIDE, seq)
    nc = seq // tr

    def kern(dh_ref, x_ref, dy_ref, g_ref, sc_ref, dx_ref, dsh_ref, dsc_ref, dg_ref, a_sh, a_sc, a_g):
        c = pl.program_id(0)

        @pl.when(c == 0)
        def _():
            a_sh[...] = jnp.zeros_like(a_sh)
            a_sc[...] = jnp.zeros_like(a_sc)
            a_g[...] = jnp.zeros_like(a_g)

        xv = x_ref[...]
        r = lax.rsqrt(jnp.mean(xv * xv, axis=-1, keepdims=True) + RMS_EPS)
        xh = xv * r
        g = g_ref[...]
        dhv = dh_ref[...]
        a_sh[...] += _acc8(dhv)
        a_sc[...] += _acc8(dhv * (xh * g))
        dn = dhv * (1.0 + sc_ref[...])
        a_g[...] += _acc8(dn * xh)
        dxh = dn * g
        dx_ref[...] = dy_ref[...] + r * (dxh - xh * jnp.mean(dxh * xh, axis=-1, keepdims=True))

        @pl.when(c == nc - 1)
        def _():
            dsh_ref[...] = jnp.sum(a_sh[...], axis=0, keepdims=True)
            dsc_ref[...] = jnp.sum(a_sc[...], axis=0, keepdims=True)
            dg_ref[...] = jnp.sum(a_g[...], axis=0, keepdims=True)

    vec = jax.ShapeDtypeStruct((1, D_MODEL), F32)
    acc = pltpu.VMEM((SUBLANES, D_MODEL), F32)
    return pl.pallas_call(
        kern, name="in_bwd", grid=(nc,),
        in_specs=[_row_spec(tr), _row_spec(tr), _row_spec(tr), _vec_spec(), _vec_spec()],
        out_specs=(_row_spec(tr), _vec_spec(), _vec_spec(), _vec_spec()),
        out_shape=(jax.ShapeDtypeStruct((seq, D_MODEL), F32), vec, vec, vec),
        scratch_shapes=[acc, acc, acc],
        compiler_params=_cparams("arbitrary"))(dh, x, dy, g1, scale)


SLAB = 2 * SUBLANES


def _local_scan(a_re, a_im, br, bi, xr, xi, row0, ls, reverse, init=None, xb=None):
    if init is None:
        x_re = jnp.zeros((SUBLANES, STATE_W), F32)
        x_im = jnp.zeros((SUBLANES, STATE_W), F32)
    else:
        x_re, x_im = init
    for i in (range(ls - 1, -1, -1) if reverse else range(ls)):
        src = pl.ds(SUBLANES * i, SUBLANES)
        dst = pl.ds(row0 + SUBLANES * i, SUBLANES)
        n_re = a_re * x_re - a_im * x_im + br[src, :]
        n_im = a_re * x_im + a_im * x_re + bi[src, :]
        if xb is not None and i % 2 == 1:
            pair = pl.ds(SUBLANES * (i - 1), SLAB)
            xb[0][pair, :] = jnp.concatenate([x_re, n_re], axis=0).astype(BF16)
            xb[1][pair, :] = jnp.concatenate([x_im, n_im], axis=0).astype(BF16)
        x_re, x_im = n_re, n_im
        xr[dst, :] = x_re
        xi[dst, :] = x_im
    return x_re, x_im


def _two(v):
    return jnp.concatenate([v, v], axis=0)


def _unpermute_rhs(v, sel):
    hi = v.astype(BF16)
    r1 = v - hi.astype(F32)
    mid = r1.astype(BF16)
    lo = (r1 - mid.astype(F32)).astype(BF16)
    return _dot(hi, sel) + _dot(mid, sel) + _dot(lo, sel)


def _scan_specs(tc, nb, rows_of):
    return dict(
        us=pl.BlockSpec((tc, nb * LANES), lambda b, c: (rows_of(c), 2 * D_MODEL // (nb * LANES) + b)),
        tok=pl.BlockSpec((tc, nb * LANES), lambda b, c: (rows_of(c), b)),
        bblk=pl.BlockSpec((nb, LANES, STATE_W), lambda b, c: (b, 0, 0)),
        cblk=pl.BlockSpec((nb, STATE_W, LANES), lambda b, c: (b, 0, 0)),
        vec=pl.BlockSpec((1, nb * STATE_W), lambda b, c: (0, b)),
        tab=pl.BlockSpec((tc, nb * STATE_W), lambda b, c: (0, b)),
        car=pl.BlockSpec((SUBLANES, nb * STATE_W), lambda b, c: (rows_of(c), b)),
        dvec=pl.BlockSpec((1, nb * LANES), lambda b, c: (0, b)))


def _ssm_scan_fwd(proj, bb_re, bb_im, cm_re, cm_im, abar_re, abar_im, pw_re, pw_im, d_skip, tc):
    seq = proj.shape[0]
    nc = seq // tc
    ls = tc // SUBLANES
    nb = SCAN_BLOCKS

    def kern(us_ref, bbr_ref, bbi_ref, cmr_ref, cmi_ref, ar_ref, ai_ref, pwr_ref, pwi_ref, d_ref,
             ys_ref, ecr_ref, eci_ref, bur, bui, car_r, car_i, end_r, end_i, upb, xb_r, xb_i, *nat):
        c = pl.program_id(1)

        @pl.when(c == 0)
        def _():
            car_r[...] = jnp.zeros_like(car_r)
            car_i[...] = jnp.zeros_like(car_i)

        for j in range(nb):
            cols = pl.ds(j * LANES, LANES)
            scols = pl.ds(j * STATE_W, STATE_W)
            nat[j][...] = us_ref[:, cols].astype(F32)
            for i in range(ls):
                upb[j, pl.ds(SUBLANES * i, SUBLANES), :] = nat[j][pl.ds(i, SUBLANES, stride=ls), :]
            u = upb[j]
            up = u.astype(BF16)
            bur[j] = _dot(up, bbr_ref[j])
            bui[j] = _dot(up, bbi_ref[j])
            a_re = jnp.broadcast_to(ar_ref[:, scols], (SUBLANES, STATE_W))
            a_im = jnp.broadcast_to(ai_ref[:, scols], (SUBLANES, STATE_W))
            x_re, x_im = _local_scan(a_re, a_im, bur.at[j], bui.at[j], bur.at[j], bui.at[j], 0, ls, False)
            end_r[j] = x_re
            end_i[j] = x_im
            big_re = pwr_ref[tc - 1:tc, scols]
            big_im = pwi_ref[tc - 1:tc, scols]
            e_re = car_r[j, 0:1, :]
            e_im = car_i[j, 0:1, :]
            for s in range(SUBLANES):
                n_re = end_r[j, s:s + 1, :] + big_re * e_re - big_im * e_im
                n_im = end_i[j, s:s + 1, :] + big_re * e_im + big_im * e_re
                e_re, e_im = n_re, n_im
                if s < SUBLANES - 1:
                    car_r[j, s + 1:s + 2, :] = e_re
                    car_i[j, s + 1:s + 2, :] = e_im
            ec_re = car_r[j]
            ec_im = car_i[j]
            ecr_ref[:, scols] = ec_re
            eci_ref[:, scols] = ec_im
            e2_re, e2_im = _two(ec_re), _two(ec_im)
            for k in range(tc // SLAB):
                rows_k = pl.ds(SLAB * k, SLAB)
                p_re = pwr_ref[rows_k, scols]
                p_im = pwi_ref[rows_k, scols]
                xb_r[j, rows_k, :] = (bur[j, rows_k, :] + p_re * e2_re - p_im * e2_im).astype(BF16)
                xb_i[j, rows_k, :] = (bui[j, rows_k, :] + p_re * e2_im + p_im * e2_re).astype(BF16)
            upb[j] = _dot(xb_r[j], cmr_ref[j]) - _dot(xb_i[j], cmi_ref[j]) + d_ref[:, cols] * u
            for i in range(ls):
                nat[j][pl.ds(i, SUBLANES, stride=ls), :] = upb[j, pl.ds(SUBLANES * i, SUBLANES), :]
            ys_ref[:, cols] = nat[j][...]
            car_r[j, 0:1, :] = e_re
            car_i[j, 0:1, :] = e_im

    sp = _scan_specs(tc, nb, lambda c: c)
    carry_shape = jax.ShapeDtypeStruct((nc * SUBLANES, STATE_ALL), F32)
    small = pltpu.VMEM((nb, SUBLANES, STATE_W), F32)
    big = pltpu.VMEM((nb, tc, STATE_W), F32)
    return pl.pallas_call(
        kern, name="ssm_scan_fwd", grid=(LANE_BLOCKS // nb, nc),
        in_specs=[sp["us"], sp["bblk"], sp["bblk"], sp["cblk"], sp["cblk"], sp["vec"], sp["vec"], sp["tab"], sp["tab"],
                  sp["dvec"]],
        out_specs=(sp["tok"], sp["car"], sp["car"]),
        out_shape=(jax.ShapeDtypeStruct((seq, D_MODEL), F32), carry_shape, carry_shape),
        scratch_shapes=[big, big, small, small, small, small, pltpu.VMEM((nb, tc, LANES), F32),
                        pltpu.VMEM((nb, tc, STATE_W), BF16), pltpu.VMEM((nb, tc, STATE_W), BF16)]
        + [pltpu.VMEM((tc, LANES), F32)] * nb,
        compiler_params=_cparams("parallel", "arbitrary"),
    )(proj, bb_re, bb_im, cm_re, cm_im, abar_re, abar_im, pw_re, pw_im, d_skip)


def _ssm_scan_bwd(proj, dys, ec_re, ec_im, bb_re, bb_im, cm_re, cm_im, abar_re, abar_im,
                  pw_re, pw_im, pv_re, pv_im, d_skip, dproj, tc):
    seq = proj.shape[0]
    nc = seq // tc
    ls = tc // SUBLANES
    nb = SCAN_BLOCKS

    def kern(us_ref, dys_ref, ecr_ref, eci_ref, bbr_ref, bbi_ref, cmr_ref, cmi_ref, ar_ref, ai_ref,
             pwr_ref, pwi_ref, pvr_ref, pvi_ref, d_ref, _,
             dus_ref, dbbr_ref, dbbi_ref, dcmr_ref, dcmi_ref, dar_ref, dai_ref, dd_ref,
             bur, bui, xr, xi, gr, gi, fc_r, fc_i, a_bbr, a_bbi, a_cmr, a_cmi, a_ar, a_ai, a_dd, upb, dpb, hb_r, hb_i,
             *nat):
        c = pl.program_id(1)

        @pl.when(c == 0)
        def _():
            for ref in (fc_r, fc_i, a_bbr, a_bbi, a_cmr, a_cmi, a_ar, a_ai, a_dd):
                ref[...] = jnp.zeros_like(ref)

        for j in range(nb):
            cols = pl.ds(j * LANES, LANES)
            scols = pl.ds(j * STATE_W, STATE_W)
            nat_u, nat_d = nat[2 * j], nat[2 * j + 1]
            nat_u[...] = us_ref[:, cols].astype(F32)
            nat_d[...] = dys_ref[:, cols]
            for i in range(ls):
                rows_i = pl.ds(SUBLANES * i, SUBLANES)
                upb[j, rows_i, :] = nat_u[pl.ds(i, SUBLANES, stride=ls), :]
                dpb[j, rows_i, :] = nat_d[pl.ds(i, SUBLANES, stride=ls), :]
            u = upb[j]
            dysv = dpb[j]
            a_dd[j] += _acc8(dysv * u)
            up = u.astype(BF16)
            bur[j] = _dot(up, bbr_ref[j])
            bui[j] = _dot(up, bbi_ref[j])
            a_re = jnp.broadcast_to(ar_ref[:, scols], (SUBLANES, STATE_W))
            a_im = jnp.broadcast_to(ai_ref[:, scols], (SUBLANES, STATE_W))
            ec_r = ecr_ref[:, scols]
            ec_i = eci_ref[:, scols]
            xr[j, 0:SUBLANES, :] = ec_r
            xi[j, 0:SUBLANES, :] = ec_i
            _local_scan(a_re, a_im, bur.at[j], bui.at[j], xr.at[j], xi.at[j], SUBLANES, ls, False, init=(ec_r, ec_i),
                        xb=(hb_r.at[j], hb_i.at[j]))
            dysp = dysv.astype(BF16)
            a_cmr[j] += _dot_tn(dysp, hb_r[j])
            a_cmi[j] -= _dot_tn(dysp, hb_i[j])
            gr[j] = _dot_nt(dysp, cmr_ref[j])
            gi[j] = -_dot_nt(dysp, cmi_ref[j])
            _local_scan(a_re, -a_im, gr.at[j], gi.at[j], gr.at[j], gi.at[j], 0, ls, True)
            big_re = pwr_ref[tc - 1:tc, scols]
            big_im = -pwi_ref[tc - 1:tc, scols]
            f_re = fc_r[j, SUBLANES - 1:SUBLANES, :]
            f_im = fc_i[j, SUBLANES - 1:SUBLANES, :]
            for s in range(SUBLANES - 1, -1, -1):
                n_re = gr[j, s:s + 1, :] + big_re * f_re - big_im * f_im
                n_im = gi[j, s:s + 1, :] + big_re * f_im + big_im * f_re
                f_re, f_im = n_re, n_im
                if s > 0:
                    fc_r[j, s - 1:s, :] = f_re
                    fc_i[j, s - 1:s, :] = f_im
            f2_r, f2_i = _two(fc_r[j]), _two(fc_i[j])
            acc_r = jnp.zeros((SUBLANES, STATE_W), F32)
            acc_i = jnp.zeros((SUBLANES, STATE_W), F32)
            for k in range(tc // SLAB):
                rows_k = pl.ds(SLAB * k, SLAB)
                q_re = pvr_ref[rows_k, scols]
                q_im = pvi_ref[rows_k, scols]
                lam_re = gr[j, rows_k, :] + q_re * f2_r + q_im * f2_i
                lam_im = gi[j, rows_k, :] + q_re * f2_i - q_im * f2_r
                xp_re = xr[j, rows_k, :]
                xp_im = xi[j, rows_k, :]
                d_r = lam_re * xp_re + lam_im * xp_im
                d_i = lam_im * xp_re - lam_re * xp_im
                acc_r = acc_r + (d_r[0:SUBLANES] + d_r[SUBLANES:])
                acc_i = acc_i + (d_i[0:SUBLANES] + d_i[SUBLANES:])
                hb_r[j, rows_k, :] = lam_re.astype(BF16)
                hb_i[j, rows_k, :] = lam_im.astype(BF16)
            a_ar[j] += acc_r
            a_ai[j] += acc_i
            fc_r[j, SUBLANES - 1:SUBLANES, :] = f_re
            fc_i[j, SUBLANES - 1:SUBLANES, :] = f_im
            lb_re = hb_r[j]
            lb_im = hb_i[j]
            a_bbr[j] += _dot_tn(up, lb_re)
            a_bbi[j] += _dot_tn(up, lb_im)
            dpb[j] = _dot_nt(lb_re, bbr_ref[j]) + _dot_nt(lb_im, bbi_ref[j]) + dysv * d_ref[:, cols]
            for i in range(ls):
                nat_d[pl.ds(i, SUBLANES, stride=ls), :] = dpb[j, pl.ds(SUBLANES * i, SUBLANES), :]
            dus_ref[:, cols] = nat_d[...].astype(BF16)

        @pl.when(c == nc - 1)
        def _():
            row_g = lax.broadcasted_iota(jnp.int32, (LANES, STATE_W), 0) // SSM_H
            col_g = lax.broadcasted_iota(jnp.int32, (LANES, STATE_W), 1) // SSM_P
            fold = (lax.broadcasted_iota(jnp.int32, (STATE_W, SSM_P), 0) % SSM_P
                    == lax.broadcasted_iota(jnp.int32, (STATE_W, SSM_P), 1)).astype(BF16)
            for j in range(nb):
                rows_j = pl.ds(j * LANES, LANES)
                for acc, out in ((a_bbr, dbbr_ref), (a_bbi, dbbi_ref), (a_cmr, dcmr_ref), (a_cmi, dcmi_ref)):
                    out[rows_j, :] = _unpermute_rhs(jnp.where(row_g == col_g, acc[j], 0.0), fold)
                dar_ref[:, pl.ds(j * STATE_W, STATE_W)] = jnp.sum(a_ar[j], axis=0, keepdims=True)
                dai_ref[:, pl.ds(j * STATE_W, STATE_W)] = jnp.sum(a_ai[j], axis=0, keepdims=True)
                dd_ref[:, pl.ds(j * LANES, LANES)] = jnp.sum(a_dd[j], axis=0, keepdims=True)

    sp = _scan_specs(tc, nb, lambda c: nc - 1 - c)
    ghp = pl.BlockSpec((nb * LANES, SSM_P), lambda b, c: (b, 0))
    ghp_shape = jax.ShapeDtypeStruct((SSM_G * SSM_H, SSM_P), F32)
    small = pltpu.VMEM((nb, SUBLANES, STATE_W), F32)
    big = pltpu.VMEM((nb, tc, STATE_W), F32)
    bigp = pltpu.VMEM((nb, tc + SUBLANES, STATE_W), F32)
    blk = pltpu.VMEM((nb, LANES, STATE_W), F32)
    tok = pltpu.VMEM((nb, tc, LANES), F32)
    return pl.pallas_call(
        kern, name="ssm_scan_bwd", grid=(LANE_BLOCKS // nb, nc),
        in_specs=[sp["us"], sp["tok"], sp["car"], sp["car"], sp["bblk"], sp["bblk"], sp["cblk"], sp["cblk"],
                  sp["vec"], sp["vec"], sp["tab"], sp["tab"], sp["tab"], sp["tab"], sp["dvec"], _ANY],
        out_specs=(sp["us"], ghp, ghp, ghp, ghp, sp["vec"], sp["vec"], sp["dvec"]),
        out_shape=(jax.ShapeDtypeStruct(dproj.shape, BF16), ghp_shape, ghp_shape, ghp_shape, ghp_shape,
                   jax.ShapeDtypeStruct((1, STATE_ALL), F32), jax.ShapeDtypeStruct((1, STATE_ALL), F32),
                   jax.ShapeDtypeStruct((1, D_MODEL), F32)),
        scratch_shapes=[big, big, bigp, bigp, big, big, small, small, blk, blk, blk, blk,
                        small, small, pltpu.VMEM((nb, SUBLANES, LANES), F32), tok, tok,
                        pltpu.VMEM((nb, tc, STATE_W), BF16), pltpu.VMEM((nb, tc, STATE_W), BF16)]
        + [pltpu.VMEM((tc, LANES), F32)] * (2 * nb),
        input_output_aliases={15: 0},
        compiler_params=_cparams("parallel", "arbitrary"),
    )(proj, dys, ec_re, ec_im, bb_re, bb_im, cm_re, cm_im, abar_re, abar_im, pw_re, pw_im, pv_re, pv_im, d_skip, dproj)


def _eye5():
    return jnp.asarray(np.eye(GROUPS_PER_BLOCK, dtype=np.float32)[None, :, None, :, None])


def _embed_b(bb_t):
    t = bb_t.transpose(1, 0, 2).reshape(LANE_BLOCKS, GROUPS_PER_BLOCK, SSM_H, 1, SSM_P)
    return (t * _eye5()).reshape(LANE_BLOCKS, LANES, STATE_W)


def _embed_c(c_ghp):
    t = c_ghp.transpose(0, 2, 1).reshape(LANE_BLOCKS, GROUPS_PER_BLOCK, SSM_P, 1, SSM_H)
    return (t * _eye5()).reshape(LANE_BLOCKS, STATE_W, LANES)


def _local_step(x, c_row, tgt, w_ada_bf, b_ada, g1, g2, w_in_bf, pool_w_bf, pscale, a_re, a_im, log_dt,
                b_re_t, b_im_t, c_re, c_im, d_skip, glu_w_bf, glu_b, wbp_bf, wbs_bf, wout_bf,
                split_proj=None, ride_for_dw_in=None, ride_for_dh=None):
    seq = x.shape[0]
    tc = min(SCAN_CHUNK, seq)
    mod8, silu_c = _mod_kernel(c_row, w_ada_bf, b_ada)
    mod = mod8[0:1]
    shift, scale, gate = mod[:, 0:D_MODEL], mod[:, D_MODEL:2 * D_MODEL], mod[:, 2 * D_MODEL:]

    abar_re, abar_im, bb_re_t, bb_im_t = _ssm_params(a_re, a_im, log_dt, b_re_t, b_im_t)
    abar_re_f, abar_im_f = abar_re.reshape(1, STATE_ALL), abar_im.reshape(1, STATE_ALL)
    pw_re, pw_im, pv_re, pv_im = _pow_tables(abar_re_f, abar_im_f, tc)
    bbe_re, bbe_im = _embed_b(bb_re_t).astype(BF16), _embed_b(bb_im_t).astype(BF16)
    cme_re, cme_im = _embed_c(c_re).astype(BF16), _embed_c(c_im).astype(BF16)
    d_row = d_skip.reshape(1, D_MODEL)

    if split_proj:
        w_own, chip, w_in_ride, unpack_w_in, late_ride, unpack_late = split_proj
        h, h_t, proj, w_blocks = _in_norm_proj_own(x, g1, scale, shift, w_own, chip, w_in_ride)
        w_in_bf = unpack_w_in(w_blocks)
        proj, *gathered = _proj_rest(h, w_blocks, proj, chip, late_ride)
        pool_w_bf, glu_w_bf, wbp_bf, wbs_bf, wout_bf = unpack_late(*gathered)
    else:
        h, h_t = _in_norm(x, g1, scale, shift)
        proj = _mm([h], [w_in_bf], name="proj", out_dtype=BF16, bm=1024, bn=1536, bk=1024)
    ypool, ypool_t = _pool_fwd(proj, pool_w_bf, pscale)
    ys, ec_re, ec_im = _ssm_scan_fwd(proj, bbe_re, bbe_im, cme_re, cme_im, abar_re_f, abar_im_f,
                                      pw_re, pw_im, d_row, tc)
    yssm, yssm_t = _glu_fwd(ys, proj, glu_w_bf, glu_b)
    (dy, dproj, merged_t, dob, dbp, dbs, loss, dgate, dg2) = _out_fwd_bwd(
        ypool, yssm, proj, x, tgt, gate, g2, wbp_bf, wbs_bf, wout_bf)

    d_wout = _mm([merged_t], [dob], name="dw_out", bm=1024, bn=1024, bk=2048)
    d_wbp = _mm([ypool_t], [dbp], name="dw_bp", bm=1024, bn=1024, bk=2048)
    d_wbs = _mm([yssm_t], [dbs], name="dw_bs", bm=1024, bn=1024, bk=2048)
    dys, dproj, dq, yg_t, d_glu_b = _glu_bwd(ys, proj, dbs, wbs_bf, glu_w_bf, glu_b, dproj)
    d_glu_w = _mm([yg_t], [dq], name="dw_glu", bm=1024, bn=1024, bk=2048)
    (dproj, dbbe_re, dbbe_im, dcme_re, dcme_im, d_abar_re, d_abar_im, d_dskip) = _ssm_scan_bwd(
        proj, dys, ec_re, ec_im, bbe_re, bbe_im, cme_re, cme_im, abar_re_f, abar_im_f,
        pw_re, pw_im, pv_re, pv_im, d_row, dproj, tc)
    dproj, d_pool_w, d_pscale = _pool_bwd(proj, dbp, wbp_bf, pool_w_bf, pscale, dproj)
    dparts = [dproj]
    small_ready = dict(
        dg2=dg2, d_pscale=d_pscale, d_glu_b=d_glu_b, d_dskip=d_dskip, d_abar_re=d_abar_re, d_abar_im=d_abar_im,
        d_bb_re_t=dbbe_re.reshape(SSM_G, SSM_H, SSM_P).transpose(1, 0, 2),
        d_bb_im_t=dbbe_im.reshape(SSM_G, SSM_H, SSM_P).transpose(1, 0, 2),
        d_c_re=dcme_re.reshape(SSM_G, SSM_H, SSM_P), d_c_im=dcme_im.reshape(SSM_G, SSM_H, SSM_P))
    ride = ride_for_dw_in(small_ready) if ride_for_dw_in else None
    d_win = _mm([h_t], dparts, name="dw_in", bm=1024, bn=1024, bk=2048, ride=ride)
    rode_dw_in = ()
    if ride:
        d_win, rode_dw_in = d_win[0], tuple(d_win[1:])
    big_grads = dict(d_win=d_win, d_glu_w=d_glu_w, d_wbp=d_wbp, d_wbs=d_wbs, d_wout=d_wout, d_pool_w=d_pool_w)
    ride = ride_for_dh(big_grads) if ride_for_dh else None
    dh = _mm(dparts, [w_in_bf], tb=True, name="dh", bm=2048, bn=1024, bk=1024, ride=ride)
    rode = ()
    if ride:
        dh, rode = dh[0], tuple(dh[1:])
    grad_x, dshift, dscale, dg1 = _in_bwd(dh, x, dy, g1, scale)
    dmod = jnp.concatenate([dshift, dscale, dgate], axis=1)
    return dict(
        rode=rode, rode_dw_in=rode_dw_in, loss=loss[0, 0], grad_x=grad_x, dmod=dmod, silu_c=silu_c, dg1=dg1,
        **small_ready, **big_grads)


def _position():
    x, y, c = lax.axis_index("x"), lax.axis_index("y"), lax.axis_index("c")
    chips = [(1 - x, y), (x, 1 - y), (1 - x, 1 - y)]
    return x, y, c, chips


_ANY = pl.BlockSpec(memory_space=pl.ANY)
COMM_CHUNKS = 4
COMM_ROW_ALIGN = 16


def _row_chunks(rows, k):
    assert rows % (k * COMM_ROW_ALIGN) == 0, (rows, k)
    step = rows // k
    return [(q * step, step) for q in range(k)]


def _ag_weights_ride(packed, n_chunks=COMM_CHUNKS):
    rows, width = packed.shape
    half = rows // 2
    chunks = _row_chunks(half, n_chunks)
    nq = len(chunks)

    def parts(p_ref, out_ref, send_sems, recv_sems):
        x, y, c, chips = _position()
        sibling = (x, y, 1 - c)

        def copy(k, chip, h, q, to, src=None):
            start, size = chunks[q]
            rows_q = pl.ds(h * half + start, size)
            dst = out_ref.at[2 * chip[0] + chip[1], rows_q, :]
            return pltpu.make_async_remote_copy(
                src_ref=dst if src is None else src.at[rows_q, :], dst_ref=dst, send_sem=send_sems.at[k * nq + q],
                recv_sem=recv_sems.at[k * nq + q], device_id=to, device_id_type=MESH_ID)

        mine = [copy(6 + h, (x, y), h, q, sibling, src=p_ref) for h in range(2) for q in range(nq)]
        first = [copy(j, (x, y), c, q, (*chip, c), src=p_ref) for q in range(nq) for j, chip in enumerate(chips)]
        return (x, y, c), chips, sibling, copy, mine, first

    def start(ins, outs, sems):
        _, _, _, _, mine, first = parts(ins[0], outs[0], sems[0], sems[1])
        for cp in first + mine:
            cp.start()

    def wait(ins, outs, sems):
        (x, y, c), chips, sibling, copy, mine, first = parts(ins[0], outs[0], sems[0], sems[1])
        passed = []
        for q in range(nq):
            for j, chip in enumerate(chips):
                copy(j, chip, c, q, (x, y, c)).wait_recv()
                fwd = copy(3 + j, chip, c, q, sibling)
                fwd.start()
                passed.append(fwd)
        for q in range(nq):
            for j, chip in enumerate(chips):
                copy(3 + j, chip, 1 - c, q, (x, y, c)).wait_recv()
        for cp in mine:
            cp.wait_recv()
        for cp in first + passed + mine:
            cp.wait_send()

    return _Ride([packed], [jax.ShapeDtypeStruct((N_CHIPS, rows, width), packed.dtype)],
                 [pltpu.SemaphoreType.DMA((8 * nq,)), pltpu.SemaphoreType.DMA((8 * nq,))], start, wait)


def _join_rides(rides):
    def split(seq, counts):
        out, at = [], 0
        for n in counts:
            out.append(seq[at:at + n])
            at += n
        return out

    n_in = [len(r.inputs) for r in rides]
    n_out = [len(r.out_shapes) for r in rides]
    n_sem = [len(r.scratch) for r in rides]

    def start(ins, outs, sems):
        for r, i, o, s in zip(rides, split(ins, n_in), split(outs, n_out), split(sems, n_sem)):
            r.start(i, o, s)

    def wait(ins, outs, sems):
        for r, i, o, s in zip(rides, split(ins, n_in), split(outs, n_out), split(sems, n_sem)):
            r.wait(i, o, s)

    return _Ride([a for r in rides for a in r.inputs], [a for r in rides for a in r.out_shapes],
                 [a for r in rides for a in r.scratch], start, wait)


def _run_ride(ride, name):
    n_in, n_out = len(ride.inputs), len(ride.out_shapes)

    def body(*refs):
        ins, outs, sems = refs[:n_in], refs[n_in:n_in + n_out], refs[n_in + n_out:]
        ride.start(ins, outs, sems)
        ride.wait(ins, outs, sems)

    return pl.pallas_call(
        body, name=name, in_specs=[_ANY] * n_in, out_specs=(_ANY,) * n_out, out_shape=tuple(ride.out_shapes),
        scratch_shapes=list(ride.scratch))(*ride.inputs)


def _small_allgather_ride(buf):
    rows, width = buf.shape
    chunks = _row_chunks(rows, COMM_CHUNKS)
    nq = len(chunks)

    def parts(b_ref, all_ref, send_sems, recv_sems, local_sem):
        x, y, c, chips = _position()
        me, sibling = (x, y, c), (x, y, 1 - c)

        def copy(k, block, q, to, src=None):
            rows_q = pl.ds(chunks[q][0], chunks[q][1])
            dst = all_ref.at[4 * block[0] + 2 * block[1] + block[2], rows_q, :]
            return pltpu.make_async_remote_copy(
                src_ref=dst if src is None else src.at[rows_q, :], dst_ref=dst, send_sem=send_sems.at[k * nq + q],
                recv_sem=recv_sems.at[k * nq + q], device_id=to, device_id_type=MESH_ID)

        mine = pltpu.make_async_copy(b_ref, all_ref.at[4 * x + 2 * y + c], local_sem)
        first = []
        for q in range(nq):
            first += [copy(1 + j, me, q, (*chip, c), src=b_ref) for j, chip in enumerate(chips)]
            first.append(copy(0, me, q, sibling, src=b_ref))
        return me, sibling, c, chips, copy, mine, first

    def start(ins, outs, sems):
        _, _, _, _, _, mine, first = parts(ins[0], outs[0], *sems)
        mine.start()
        for cp in first:
            cp.start()

    def wait(ins, outs, sems):
        me, sibling, c, chips, copy, mine, first = parts(ins[0], outs[0], *sems)
        passed = []
        for q in range(nq):
            for j, chip in enumerate(chips):
                copy(1 + j, (*chip, c), q, me).wait_recv()
                fwd = copy(4 + j, (*chip, c), q, sibling)
                fwd.start()
                passed.append(fwd)
        for q in range(nq):
            copy(0, sibling, q, me).wait_recv()
            for j, chip in enumerate(chips):
                copy(4 + j, (*chip, 1 - c), q, me).wait_recv()
        for cp in first + passed:
            cp.wait_send()
        mine.wait()

    return _Ride([buf], [jax.ShapeDtypeStruct((N_DEV, rows, width), F32)],
                 [pltpu.SemaphoreType.DMA((7 * nq,)), pltpu.SemaphoreType.DMA((7 * nq,)), pltpu.SemaphoreType.DMA],
                 start, wait)


def _sum_devices(blocks):
    n, rows, width = blocks.shape
    rb = rows // 2 if (rows // 2) % SUBLANES == 0 else rows

    def kern(b_ref, o_ref):
        total = b_ref[0]
        for d in range(1, n):
            total = total + b_ref[d]
        o_ref[...] = total

    return pl.pallas_call(
        kern, name="small_sum", grid=(rows // rb,), in_specs=[pl.BlockSpec((n, rb, width), lambda i: (0, i, 0))],
        out_specs=pl.BlockSpec((rb, width), lambda i: (i, 0)), out_shape=jax.ShapeDtypeStruct((rows, width), F32),
        compiler_params=_cparams("parallel"))(blocks)


def _small_allgather_sum(buf, head_rows, n_chunks=COMM_CHUNKS):
    rows, width = buf.shape
    chunks = _row_chunks(rows, n_chunks)
    nq = len(chunks)

    def body(b_ref, head_ref, sum_ref, all_ref, send_sems, recv_sems, local_sem):
        x, y, c, chips = _position()
        me, sibling = (x, y, c), (x, y, 1 - c)

        def slot(px, py, pc):
            return all_ref.at[4 * px + 2 * py + pc]

        def copy(k, block, q, to, src=None):
            rows_q = pl.ds(chunks[q][0], chunks[q][1])
            dst = slot(*block).at[rows_q, :]
            return pltpu.make_async_remote_copy(
                src_ref=dst if src is None else src.at[rows_q, :], dst_ref=dst, send_sem=send_sems.at[k * nq + q],
                recv_sem=recv_sems.at[k * nq + q], device_id=to, device_id_type=MESH_ID)

        mine = pltpu.make_async_copy(b_ref, slot(*me), local_sem)
        mine.start()
        first = []
        for q in range(nq):
            first += [copy(1 + j, me, q, (*chip, c), src=b_ref) for j, chip in enumerate(chips)]
            first.append(copy(0, me, q, sibling, src=b_ref))
        for cp in first:
            cp.start()
        passed = []
        for q in range(nq):
            for j, chip in enumerate(chips):
                copy(1 + j, (*chip, c), q, me).wait_recv()
                fwd = copy(4 + j, (*chip, c), q, sibling)
                fwd.start()
                passed.append(fwd)
        for q in range(nq):
            copy(0, sibling, q, me).wait_recv()
            for j, chip in enumerate(chips):
                copy(4 + j, (*chip, 1 - c), q, me).wait_recv()
        for cp in first + passed:
            cp.wait_send()
        mine.wait()
        total = all_ref[0]
        for d in range(1, N_DEV):
            total = total + all_ref[d]
        sum_ref[...] = total
        head_ref[...] = all_ref[:, 0:head_rows, :]

    vm = pl.BlockSpec(memory_space=pltpu.VMEM)
    return pl.pallas_call(
        body, name="small_allgather_sum", in_specs=[vm], out_specs=(vm, vm),
        out_shape=(jax.ShapeDtypeStruct((N_DEV, head_rows, width), F32), jax.ShapeDtypeStruct((rows, width), F32)),
        scratch_shapes=[pltpu.VMEM((N_DEV, rows, width), F32), pltpu.SemaphoreType.DMA((7 * nq,)),
                        pltpu.SemaphoreType.DMA((7 * nq,)), pltpu.SemaphoreType.DMA],
        compiler_params=_cparams(),
    )(buf)


def _rs_pair(g):
    n, rows, width = g.shape
    half = rows // 2
    chunks = _row_chunks(half, COMM_CHUNKS)
    nq = len(chunks)

    def body(g_ref, got_ref, send_sems, recv_sems):
        x, y, c, _ = _position()
        swaps = []
        for k in range(n):
            for q, (start, size) in enumerate(chunks):
                swaps.append(pltpu.make_async_remote_copy(
                    src_ref=g_ref.at[k, pl.ds((1 - c) * half + start, size), :], dst_ref=got_ref.at[k, pl.ds(start, size), :],
                    send_sem=send_sems.at[k * nq + q], recv_sem=recv_sems.at[k * nq + q],
                    device_id=(x, y, 1 - c), device_id_type=MESH_ID))
        for cp in swaps:
            cp.start()
        for cp in swaps:
            cp.wait()

    return pl.pallas_call(
        body, name="rs_pair", in_specs=[_ANY], out_specs=_ANY, out_shape=jax.ShapeDtypeStruct((n, half, width), g.dtype),
        scratch_shapes=[pltpu.SemaphoreType.DMA((n * nq,)), pltpu.SemaphoreType.DMA((n * nq,))],
    )(g)


def _rs_chips_ride(part_bf):
    n, rows, width = part_bf.shape
    chunks = _row_chunks(rows, COMM_CHUNKS)
    nq = len(chunks)

    def sends(pb_ref, got_ref, send_sems, recv_sems):
        x, y, c, chips = _position()
        out = []
        for q, (start, size) in enumerate(chunks):
            for j, chip in enumerate(chips):
                out.append(pltpu.make_async_remote_copy(
                    src_ref=pb_ref.at[2 * chip[0] + chip[1], pl.ds(start, size), :], dst_ref=got_ref.at[j, pl.ds(start, size), :],
                    send_sem=send_sems.at[j * nq + q], recv_sem=recv_sems.at[j * nq + q],
                    device_id=(*chip, c), device_id_type=MESH_ID))
        return out

    def start(ins, outs, sems):
        for cp in sends(ins[0], outs[0], sems[0], sems[1]):
            cp.start()

    def wait(ins, outs, sems):
        for cp in sends(ins[0], outs[0], sems[0], sems[1]):
            cp.wait()

    return _Ride([part_bf], [jax.ShapeDtypeStruct((N_CHIPS - 1, rows, width), BF16)],
                 [pltpu.SemaphoreType.DMA((3 * nq,)), pltpu.SemaphoreType.DMA((3 * nq,))], start, wait)


def _rs_join(shard):
    rows, width = shard.shape
    half = rows // 2
    chunks = _row_chunks(half, COMM_CHUNKS)
    nq = len(chunks)

    def body(in_ref, out_ref, send_sems, recv_sems):
        x, y, c, _ = _position()
        def swap(q, h):
            rows_q = pl.ds(h * half + chunks[q][0], chunks[q][1])
            return pltpu.make_async_remote_copy(
                src_ref=in_ref.at[rows_q, :], dst_ref=out_ref.at[rows_q, :], send_sem=send_sems.at[q],
                recv_sem=recv_sems.at[q], device_id=(x, y, 1 - c), device_id_type=MESH_ID)

        for q in range(nq):
            swap(q, c).start()
        for q in range(nq):
            swap(q, 1 - c).wait_recv()
        for q in range(nq):
            swap(q, c).wait_send()

    return pl.pallas_call(
        body, name="rs_join", in_specs=[_ANY], out_specs=_ANY, input_output_aliases={0: 0},
        out_shape=jax.ShapeDtypeStruct(shard.shape, shard.dtype),
        scratch_shapes=[pltpu.SemaphoreType.DMA((nq,)), pltpu.SemaphoreType.DMA((nq,))],
    )(shard)


def _pair_add(g, got, core):
    n, half, width = got.shape
    nb = 2
    rb = half // nb

    def kern(c_ref, a_ref, b_ref, f_ref, h_ref):
        s = a_ref[...] + b_ref[...]
        f_ref[...] = s
        h_ref[...] = s.astype(BF16)

    spec = pl.BlockSpec((1, rb, width), lambda k, i, c_ref: (k, i, 0))
    return pl.pallas_call(
        kern, name="rs_pair_add",
        grid_spec=pltpu.PrefetchScalarGridSpec(
            num_scalar_prefetch=1, grid=(n, nb),
            in_specs=[pl.BlockSpec((1, rb, width), lambda k, i, c_ref: (k, c_ref[0] * nb + i, 0)), spec],
            out_specs=(spec, spec)),
        out_shape=(jax.ShapeDtypeStruct(got.shape, F32), jax.ShapeDtypeStruct(got.shape, BF16)),
        compiler_params=_cparams("parallel", "parallel"))(core, g, got)


def _chip_add(part_f32, got, where):
    _, rows, width = part_f32.shape
    nb = 2
    rb = rows // nb

    def kern(w_ref, a_ref, b_ref, o_ref):
        o_ref[...] = ((a_ref[0] + b_ref[0].astype(F32)) + b_ref[1].astype(F32)) + b_ref[2].astype(F32)

    return pl.pallas_call(
        kern, name="rs_chip_add",
        grid_spec=pltpu.PrefetchScalarGridSpec(
            num_scalar_prefetch=1, grid=(nb,),
            in_specs=[pl.BlockSpec((1, rb, width), lambda i, w_ref: (w_ref[0], i, 0)),
                      pl.BlockSpec((N_CHIPS - 1, rb, width), lambda i, w_ref: (0, i, 0))],
            out_specs=pl.BlockSpec((rb, width), lambda i, w_ref: (w_ref[1] * nb + i, 0))),
        out_shape=jax.ShapeDtypeStruct((2 * rows, width), F32),
        compiler_params=_cparams("parallel"))(where, part_f32, got)


def _adamw(w, g, m, v, name):
    rows, width = w.shape
    rb = rows
    for cand in (512, 256, 128, 64, 32, 16, 8):
        if rows % cand == 0 and cand * width * 4 <= ADAM_BLOCK_BYTES:
            rb = cand
            break
    spec = pl.BlockSpec((rb, width), lambda i: (i, 0))

    def kern(w_ref, g_ref, m_ref, v_ref, d_ref, nm_ref, nv_ref):
        d_ref[...], nm_ref[...], nv_ref[...] = _adamw_update(w_ref[...], g_ref[...], m_ref[...], v_ref[...])

    shp = jax.ShapeDtypeStruct(w.shape, F32)
    return pl.pallas_call(
        kern, name=name, grid=(rows // rb,), in_specs=[spec] * 4, out_specs=(spec, spec, spec),
        out_shape=(shp, shp, shp), compiler_params=_cparams("parallel"))(w, g, m, v)


def _adamw_update(w, g, m, v):
    nm = ADAM_B1 * m + (1.0 - ADAM_B1) * g
    nv = ADAM_B2 * v + (1.0 - ADAM_B2) * (g * g)
    m_hat = nm / (1.0 - ADAM_B1 ** ADAM_STEP)
    v_hat = nv / (1.0 - ADAM_B2 ** ADAM_STEP)
    return -ADAM_LR * (m_hat / (jnp.sqrt(v_hat) + ADAM_EPS) + ADAM_WD * w), nm, nv


def _adamw_small(params):
    n = len(params)

    def kern(*refs):
        ins, outs = refs[:4 * n], refs[4 * n:]
        for p in range(n):
            w_ref, g_ref, m_ref, v_ref = ins[4 * p:4 * p + 4]
            d, nm, nv = _adamw_update(w_ref[...], g_ref[...], m_ref[...], v_ref[...])
            outs[3 * p][...] = d
            outs[3 * p + 1][...] = nm
            outs[3 * p + 2][...] = nv

    flat = [a for group in params for a in group]
    shapes = [jax.ShapeDtypeStruct(group[0].shape, F32) for group in params for _ in range(3)]
    res = pl.pallas_call(kern, name="adamw_small", out_shape=tuple(shapes), compiler_params=_cparams())(*flat)
    return [tuple(res[3 * p:3 * p + 3]) for p in range(n)]


def _wada_grad(silu_t, dmod_cols):
    n = dmod_cols.shape[1]

    def kern(s_ref, d_ref, o_ref):
        acc = s_ref[:, 0:1] * d_ref[0:1, :]
        for b in range(1, N_DEV):
            acc = acc + s_ref[:, b:b + 1] * d_ref[b:b + 1, :]
        o_ref[...] = acc

    return pl.pallas_call(kern, name="wada_grad", out_shape=jax.ShapeDtypeStruct((D_MODEL, n), F32),
                          compiler_params=_cparams())(silu_t, dmod_cols)


def _rows(a, multiple):
    flat = a.reshape(-1)
    pad = (-flat.shape[0]) % (D_MODEL * multiple)
    if pad:
        flat = jnp.concatenate([flat, jnp.zeros((pad,), flat.dtype)])
    return flat.reshape(-1, D_MODEL)


def _part_rows(shape, multiple):
    return -(-int(np.prod(shape)) // (D_MODEL * multiple)) * multiple


def _pack_rows(parts, multiple, total_multiple=1):
    blocks = [_rows(p, multiple) for p in parts]
    pad = (-sum(b.shape[0] for b in blocks)) % total_multiple
    if pad:
        blocks.append(jnp.zeros((pad, D_MODEL), blocks[0].dtype))
    return jnp.concatenate(blocks, axis=0)


def _unpack_rows(buf, shapes, multiple):
    out, r = [], 0
    for shp in shapes:
        n = int(np.prod(shp))
        nr = _part_rows(shp, multiple)
        out.append(buf[r:r + nr].reshape(-1)[:n].reshape(shp))
        r += nr
    return out


def kernel(x, c, w_ada, b_ada, norm_pre, norm_post, w_in, pool_w, pool_scale, ssm_a_re, ssm_a_im, ssm_log_dt, ssm_b_re, ssm_b_im, ssm_c_re, ssm_c_im, ssm_d, glu_w, glu_b, w_branch_pool, w_branch_ssm, w_out, loss_target, m_w_ada, m_b_ada, m_norm_pre, m_norm_post, m_w_in, m_pool_w, m_pool_scale, m_ssm_a_re, m_ssm_a_im, m_ssm_log_dt, m_ssm_b_re, m_ssm_b_im, m_ssm_c_re, m_ssm_c_im, m_ssm_d, m_glu_w, m_glu_b, m_w_branch_pool, m_w_branch_ssm, m_w_out, v_w_ada, v_b_ada, v_norm_pre, v_norm_post, v_w_in, v_pool_w, v_pool_scale, v_ssm_a_re, v_ssm_a_im, v_ssm_log_dt, v_ssm_b_re, v_ssm_b_im, v_ssm_c_re, v_ssm_c_im, v_ssm_d, v_glu_w, v_glu_b, v_w_branch_pool, v_w_branch_ssm, v_w_out):
    n_ada = w_ada.shape[2]
    n_in = w_in.shape[2]
    n_row = glu_w.shape[1]
    n_pool = pool_w.shape[2]
    n_groups = pool_w.shape[1]

    (g_ada,) = _run_ride(_ag_weights_ride(w_ada[0].astype(BF16)), "ag_weights")
    w_ada_bf = g_ada.transpose(1, 0, 2).reshape(D_MODEL, N_CHIPS * n_ada)
    w_in_own = w_in[0].astype(BF16)
    w_in_ride = _ag_weights_ride(w_in_own)

    def unpack_w_in(g_in):
        return g_in.transpose(1, 0, 2).reshape(D_MODEL, N_CHIPS * n_in)
    pool_rows = n_groups * n_pool * POOL_GW // D_MODEL
    late_shards = [pool_w[0].reshape(n_groups * n_pool, POOL_GW), glu_w[0], w_branch_pool[0], w_branch_ssm[0], w_out[0]]
    late_ride = _join_rides([_ag_weights_ride(s.astype(BF16), n_chunks=2) for s in late_shards])

    def unpack_late(pool, *squares):
        pool = pool.reshape(N_CHIPS, n_groups, n_pool, POOL_GW).transpose(1, 0, 2, 3)
        return (pool.reshape(n_groups, POOL_GW, POOL_GW), *[s.reshape(D_MODEL, D_MODEL) for s in squares])

    chip = 2 * lax.axis_index("x") + lax.axis_index("y")
    core = lax.axis_index("c").astype(jnp.int32)
    kept = {}

    def by_cols(a, n):
        return a.reshape(D_MODEL, N_CHIPS, n).transpose(1, 0, 2).reshape(N_CHIPS, -1, D_MODEL)

    def by_rows(a):
        return a.reshape(N_CHIPS, n_row, D_MODEL)

    def exchange_big(g):
        pool_by_chip = g["d_pool_w"].reshape(n_groups, N_CHIPS, n_pool, POOL_GW).transpose(1, 0, 2, 3)
        blocks = [by_cols(g["d_win"], n_in), by_rows(g["d_glu_w"]), by_rows(g["d_wbp"]), by_rows(g["d_wbs"]),
                  by_rows(g["d_wout"]), pool_by_chip.reshape(N_CHIPS, pool_rows, D_MODEL)]
        pad = (-sum(b.shape[1] for b in blocks)) % (2 * COMM_CHUNKS * COMM_ROW_ALIGN)
        if pad:
            blocks.append(jnp.zeros((N_CHIPS, pad, D_MODEL), F32))
        g_packed = jnp.concatenate(blocks, axis=1)
        kept["part_f32"], part_bf = _pair_add(g_packed, _rs_pair(g_packed), core.reshape(1))
        return _rs_chips_ride(part_bf)

    a_re, a_im, log_dt = ssm_a_re[0], ssm_a_im[0], ssm_log_dt[0].reshape(SSM_G, 1)
    b_re_t, b_im_t = ssm_b_re[0].transpose(2, 0, 1), ssm_b_im[0].transpose(2, 0, 1)
    early_names = ["dg2", "d_pscale", "d_glu_b", "d_dskip", "d_abar_re", "d_abar_im", "d_bb_re_t", "d_bb_im_t",
                   "d_c_re", "d_c_im"]

    def exchange_small(s):
        parts = [s[k] for k in early_names]
        kept["early_shapes"] = [p.shape for p in parts]
        return _small_allgather_ride(_pack_rows(parts, SUBLANES, COMM_CHUNKS * COMM_ROW_ALIGN))

    res = _local_step(x[0], c, loss_target[0], w_ada_bf, b_ada, norm_pre, norm_post, None, None, pool_scale,
                      a_re, a_im, log_dt, b_re_t, b_im_t, ssm_c_re[0], ssm_c_im[0], ssm_d[0], None, glu_b[0:1],
                      None, None, None,
                      split_proj=(w_in_own, chip.astype(jnp.int32).reshape(1), w_in_ride, unpack_w_in, late_ride, unpack_late),
                      ride_for_dw_in=exchange_small, ride_for_dh=exchange_big)

    (all_early,) = res["rode_dw_in"]
    (g_norm_post, g_pscale, g_glu_b, g_dskip, s_abar_re, s_abar_im, s_bb_re, s_bb_im, g_c_re, g_c_im) = _unpack_rows(
        _sum_devices(all_early), kept["early_shapes"], SUBLANES)
    g_a_re, g_a_im, g_log_dt, g_b_re_t, g_b_im_t = _ssm_params_bwd(
        a_re, a_im, log_dt, b_re_t, b_im_t, s_abar_re.reshape(SSM_G, SSM_P), s_abar_im.reshape(SSM_G, SSM_P),
        s_bb_re, s_bb_im)
    late_parts = [res["dmod"], res["silu_c"], res["dg1"], res["loss"].reshape(1, 1)]
    late_shapes = [p.shape for p in late_parts]
    head_rows = _part_rows(late_shapes[0], SUBLANES) + _part_rows(late_shapes[1], SUBLANES)
    all_late, sum_late = _small_allgather_sum(_pack_rows(late_parts, SUBLANES, COMM_ROW_ALIGN), head_rows, n_chunks=1)
    g_b_ada, _, g_norm_pre, loss = _unpack_rows(sum_late, late_shapes, SUBLANES)
    loss = loss[0, 0]
    dmod_all = all_late[:, 0:3].reshape(N_DEV, 3 * D_MODEL)
    dmod_cols = lax.dynamic_slice_in_dim(dmod_all, chip * n_ada, n_ada, axis=1)
    silu_t = all_late[:, _part_rows(late_shapes[0], SUBLANES)].transpose(1, 0)
    g_w_ada = _wada_grad(silu_t, dmod_cols)

    (got_chips,) = res["rode"]
    shard = _rs_join(_chip_add(kept["part_f32"], got_chips, jnp.stack([chip.astype(jnp.int32), core])))
    r = 0
    g_w_in = shard[r:r + n_in].reshape(D_MODEL, n_in)
    r += n_in
    g_squares = []
    for _ in range(4):
        g_squares.append(shard[r:r + n_row])
        r += n_row
    g_glu_w, g_wbp, g_wbs, g_wout = g_squares
    g_pool_w = shard[r:r + pool_rows].reshape(n_groups * n_pool, POOL_GW)

    big = [("w_ada", w_ada[0], g_w_ada, m_w_ada[0], v_w_ada[0]),
           ("w_in", w_in[0], g_w_in, m_w_in[0], v_w_in[0]),
           ("pool_w", pool_w[0].reshape(n_groups * n_pool, POOL_GW), g_pool_w,
            m_pool_w[0].reshape(n_groups * n_pool, POOL_GW), v_pool_w[0].reshape(n_groups * n_pool, POOL_GW)),
           ("glu_w", glu_w[0], g_glu_w, m_glu_w[0], v_glu_w[0]),
           ("w_branch_pool", w_branch_pool[0], g_wbp, m_w_branch_pool[0], v_w_branch_pool[0]),
           ("w_branch_ssm", w_branch_ssm[0], g_wbs, m_w_branch_ssm[0], v_w_branch_ssm[0]),
           ("w_out", w_out[0], g_wout, m_w_out[0], v_w_out[0])]
    out = {}
    for name, w_, g_, m_, v_ in big:
        d_, nm_, nv_ = _adamw(w_, g_, m_, v_, "adamw_" + name)
        out[name] = (g_, d_, nm_, nv_)

    g_b_re = g_b_re_t.transpose(1, 2, 0)
    g_b_im = g_b_im_t.transpose(1, 2, 0)
    small = [("b_ada", b_ada, g_b_ada, m_b_ada, v_b_ada),
             ("norm_pre", norm_pre, g_norm_pre, m_norm_pre, v_norm_pre),
             ("norm_post", norm_post, g_norm_post, m_norm_post, v_norm_post),
             ("pool_scale", pool_scale, g_pscale, m_pool_scale, v_pool_scale),
             ("ssm_a_re", ssm_a_re, g_a_re, m_ssm_a_re, v_ssm_a_re),
             ("ssm_a_im", ssm_a_im, g_a_im, m_ssm_a_im, v_ssm_a_im),
             ("ssm_log_dt", ssm_log_dt, g_log_dt, m_ssm_log_dt, v_ssm_log_dt),
             ("ssm_b_re", ssm_b_re, g_b_re, m_ssm_b_re, v_ssm_b_re),
             ("ssm_b_im", ssm_b_im, g_b_im, m_ssm_b_im, v_ssm_b_im),
             ("ssm_c_re", ssm_c_re, g_c_re, m_ssm_c_re, v_ssm_c_re),
             ("ssm_c_im", ssm_c_im, g_c_im, m_ssm_c_im, v_ssm_c_im),
             ("ssm_d", ssm_d, g_dskip, m_ssm_d, v_ssm_d),
             ("glu_b", glu_b, g_glu_b, m_glu_b, v_glu_b)]
    small = [(name, w_, g_.reshape(w_.shape), m_, v_) for name, w_, g_, m_, v_ in small]
    updates = _adamw_small([t[1:] for t in small])
    for (name, _, g_, _, _), (d_, nm_, nv_) in zip(small, updates):
        out[name] = (g_, d_, nm_, nv_)

    order = ["w_ada", "b_ada", "norm_pre", "norm_post", "w_in", "pool_w", "pool_scale", "ssm_a_re", "ssm_a_im",
             "ssm_log_dt", "ssm_b_re", "ssm_b_im", "ssm_c_re", "ssm_c_im", "ssm_d", "glu_w", "glu_b", "w_branch_pool",
             "w_branch_ssm", "w_out"]
    ref_shape = dict(w_ada=w_ada.shape, w_in=w_in.shape, pool_w=pool_w.shape, glu_w=glu_w.shape,
                     w_branch_pool=w_branch_pool.shape, w_branch_ssm=w_branch_ssm.shape, w_out=w_out.shape)
    for name, w_, _, _, _ in small:
        ref_shape[name] = w_.shape
    results = [loss, res["grad_x"][None]]
    for k in range(4):
        results += [out[name][k].reshape(ref_shape[name]) for name in order]
    return tuple(results)
```

```python
import functools
import math

import numpy as np
import jax
import jax.numpy as jnp
from jax import lax
from jax.experimental import pallas as pl
from jax.experimental.pallas import tpu as pltpu

F32 = jnp.float32
BF16 = jnp.bfloat16
MESH_ID = pl.DeviceIdType.MESH

D_MODEL = 1024
LANES = 128
SUBLANES = 8
SSM_G, SSM_P, SSM_H = 64, 64, 16
LANE_BLOCKS = D_MODEL // LANES
GROUPS_PER_BLOCK = LANES // SSM_H
STATE_W = GROUPS_PER_BLOCK * SSM_P
STATE_ALL = SSM_G * SSM_P
POOL_WINDOWS = (2, 4, 8, 16)
POOL_GW = D_MODEL // len(POOL_WINDOWS)
HALO = 16
RMS_EPS = 1e-6
N_CHIPS = 4
N_DEV = 8

SCAN_CHUNK = 1024
SCAN_BLOCKS_FWD = 2
SCAN_BLOCKS_BWD = 1
ROW_CHUNK = 256
ROW_CHUNK_WIDE = 512
PROJ_ROWS = 1024
VMEM_LIMIT_BYTES = 56 * 1024 * 1024

ADAM_BLOCK_BYTES = 1 << 20
ADAM_LR, ADAM_B1, ADAM_B2, ADAM_EPS, ADAM_WD, ADAM_STEP = 0.001, 0.9, 0.999, 1e-08, 0.01, 10

_GELU_C0 = math.sqrt(2.0 / math.pi)
_GELU_C1 = 0.044715


def _cparams(*sem):
    if sem:
        return pltpu.CompilerParams(dimension_semantics=sem, vmem_limit_bytes=VMEM_LIMIT_BYTES)
    return pltpu.CompilerParams(vmem_limit_bytes=VMEM_LIMIT_BYTES)


def _sigmoid(v):
    return jax.nn.sigmoid(v)


def _silu(v):
    return v * _sigmoid(v)


def _dsilu(v):
    s = _sigmoid(v)
    return s * (1.0 + v * (1.0 - s))


def _gelu(v):
    return v * (0.5 * (1.0 + jnp.tanh(_GELU_C0 * v * (1.0 + _GELU_C1 * (v * v)))))


def _gelu_and_grad(v):
    v2 = v * v
    t = jnp.tanh(_GELU_C0 * v * (1.0 + _GELU_C1 * v2))
    half = 0.5 * (1.0 + t)
    grad = half + (0.5 * _GELU_C0) * v * (1.0 - t * t) * (1.0 + (3.0 * _GELU_C1) * v2)
    return v * half, grad


def _silu_and_grad(v):
    s = _sigmoid(v)
    return v * s, s * (1.0 + v * (1.0 - s))


def _dot(a, b):
    return lax.dot_general(a, b, (((1,), (0,)), ((), ())), preferred_element_type=F32)


def _dot_nt(a, b):
    return lax.dot_general(a, b, (((1,), (1,)), ((), ())), preferred_element_type=F32)


def _dot_tn(a, b):
    return lax.dot_general(a, b, (((0,), (0,)), ((), ())), preferred_element_type=F32)


def _acc8(v):
    return v.reshape(v.shape[0] // SUBLANES, SUBLANES, v.shape[1]).sum(axis=0)


class _Ride:
    def __init__(self, inputs, out_shapes, scratch, start, wait, in_place=()):
        self.inputs, self.out_shapes, self.scratch, self.start, self.wait = inputs, out_shapes, scratch, start, wait
        self.in_place = tuple(in_place)


def _mm(a_parts, b_parts, *, name, ta=False, tb=False, out_dtype=F32, bm=512, bn=512, bk=512, ride=None):
    a_parts, b_parts = list(a_parts), list(b_parts)
    if ta:
        assert len(a_parts) == 1
        k_dim, m_dim = a_parts[0].shape
    else:
        m_dim = a_parts[0].shape[0]
        k_dim = sum(a.shape[1] for a in a_parts)
    if tb:
        assert len(b_parts) == 1
        n_dim = b_parts[0].shape[0]
    else:
        n_dim = sum(b.shape[1] for b in b_parts)
    bm, bn, bk = min(bm, m_dim), min(bn, n_dim), min(bk, k_dim)
    nm, nn, nk = m_dim // bm, n_dim // bn, k_dim // bk
    a_ranges, off = [], 0
    for a in a_parts:
        cnt = (a.shape[0] if ta else a.shape[1]) // bk
        a_ranges.append((off, cnt))
        off += cnt
    b_ranges, off = [], 0
    for b in b_parts:
        cnt = (b.shape[0] if tb else b.shape[1]) // bn
        b_ranges.append((off, cnt))
        off += cnt

    def a_spec(off, cnt):
        if ta:
            return pl.BlockSpec((bk, bm), lambda i, n, k: (k, i))
        return pl.BlockSpec((bm, bk), lambda i, n, k: (i, jnp.clip(k - off, 0, cnt - 1)))

    def b_spec(off, cnt):
        if tb:
            return pl.BlockSpec((bn, bk), lambda i, n, k: (n, k))
        return pl.BlockSpec((bk, bn), lambda i, n, k: (k, jnp.clip(n - off, 0, cnt - 1)))

    na, nb = len(a_parts), len(b_parts)
    dims = (((0 if ta else 1,), (1 if tb else 0,)), ((), ()))

    def kern_single(a_ref, b_ref, o_ref):
        o_ref[...] = lax.dot_general(a_ref[...].astype(BF16), b_ref[...].astype(BF16), dims,
                                     preferred_element_type=F32).astype(out_dtype)

    if na == 1 and nb == 1 and nk == 1 and not ride:
        return pl.pallas_call(
            kern_single, name=name, grid=(nm, nn),
            in_specs=[pl.BlockSpec((bk, bm), lambda i, n: (0, i)) if ta else pl.BlockSpec((bm, bk), lambda i, n: (i, 0)),
                      pl.BlockSpec((bn, bk), lambda i, n: (n, 0)) if tb else pl.BlockSpec((bk, bn), lambda i, n: (0, n))],
            out_specs=pl.BlockSpec((bm, bn), lambda i, n: (i, n)),
            out_shape=jax.ShapeDtypeStruct((m_dim, n_dim), out_dtype),
            compiler_params=_cparams("parallel", "parallel"),
        )(a_parts[0], b_parts[0])

    n_rin = len(ride.inputs) if ride else 0
    n_rout = len(ride.out_shapes) if ride else 0

    def kern(*refs):
        a_refs, b_refs = refs[:na], refs[na:na + nb]
        rin = refs[na + nb:na + nb + n_rin]
        o_ref = refs[na + nb + n_rin]
        rout = refs[na + nb + n_rin + 1:na + nb + n_rin + 1 + n_rout]
        acc = refs[na + nb + n_rin + 1 + n_rout]
        rsem = refs[na + nb + n_rin + 2 + n_rout:]
        i, n, k = pl.program_id(0), pl.program_id(1), pl.program_id(2)

        if ride:
            @pl.when((i == 0) & (n == 0) & (k == 0))
            def _():
                ride.start(rin, rout, rsem)

        if nk > 1:
            @pl.when(k == 0)
            def _():
                acc[...] = jnp.zeros_like(acc)

        for ja, (koff, kcnt) in enumerate(a_ranges):
            for jb, (noff, ncnt) in enumerate(b_ranges):
                def step(ja=ja, jb=jb):
                    a = a_refs[ja][...].astype(BF16)
                    b = b_refs[jb][...].astype(BF16)
                    prod = lax.dot_general(a, b, dims, preferred_element_type=F32)
                    if nk > 1:
                        acc[...] += prod
                    else:
                        o_ref[...] = prod.astype(out_dtype)

                if na == 1 and nb == 1:
                    step()
                else:
                    cond = (k >= koff) & (k < koff + kcnt) & (n >= noff) & (n < noff + ncnt)
                    pl.when(cond)(step)

        if nk > 1:
            @pl.when(k == nk - 1)
            def _():
                o_ref[...] = acc[...].astype(out_dtype)

        if ride:
            @pl.when((i == nm - 1) & (n == nn - 1) & (k == nk - 1))
            def _():
                ride.wait(rin, rout, rsem)

    any_spec = pl.BlockSpec(memory_space=pl.ANY)
    out_spec = pl.BlockSpec((bm, bn), lambda i, n, k: (i, n))
    out_shape = jax.ShapeDtypeStruct((m_dim, n_dim), out_dtype)
    acc_shape = pltpu.VMEM((bm, bn) if nk > 1 else (SUBLANES, LANES), F32)
    if not ride:
        return pl.pallas_call(
            kern, name=name, grid=(nm, nn, nk),
            in_specs=[a_spec(*r) for r in a_ranges] + [b_spec(*r) for r in b_ranges],
            out_specs=out_spec, out_shape=out_shape, scratch_shapes=[acc_shape],
            compiler_params=_cparams("parallel", "parallel", "arbitrary"),
        )(*a_parts, *b_parts)
    return pl.pallas_call(
        kern, name=name, grid=(nm, nn, nk),
        in_specs=[a_spec(*r) for r in a_ranges] + [b_spec(*r) for r in b_ranges] + [any_spec] * n_rin,
        out_specs=(out_spec,) + (any_spec,) * n_rout, out_shape=(out_shape,) + tuple(ride.out_shapes),
        scratch_shapes=[acc_shape] + list(ride.scratch),
        compiler_params=_cparams("arbitrary", "arbitrary", "arbitrary"),
    )(*a_parts, *b_parts, *ride.inputs)


def _ssm_param_fn(a_re, a_im, log_dt, b_re, b_im):
    dt = jnp.exp(log_dt)
    lam_re = jnp.minimum(a_re, -1e-4)
    lam_im = a_im
    mag = jnp.exp(lam_re * dt)
    abar_re = mag * jnp.cos(lam_im * dt)
    abar_im = mag * jnp.sin(lam_im * dt)
    den = lam_re * lam_re + lam_im * lam_im
    num_re = abar_re - 1.0
    f_re = (num_re * lam_re + abar_im * lam_im) / den
    f_im = (abar_im * lam_re - num_re * lam_im) / den
    bb_re = f_re * b_re - f_im * b_im
    bb_im = f_re * b_im + f_im * b_re
    return abar_re, abar_im, bb_re, bb_im


def _ssm_params(a_re, a_im, log_dt, b_re_t, b_im_t):
    def kern(are, aim, ldt, bre, bim, o_ar, o_ai, o_br, o_bi):
        ar, ai, br, bi = _ssm_param_fn(are[...], aim[...], ldt[...], bre[...], bim[...])
        o_ar[...] = ar
        o_ai[...] = ai
        o_br[...] = br
        o_bi[...] = bi

    gp = jax.ShapeDtypeStruct((SSM_G, SSM_P), F32)
    hgp = jax.ShapeDtypeStruct((SSM_H, SSM_G, SSM_P), F32)
    return pl.pallas_call(kern, name="ssm_params", out_shape=(gp, gp, hgp, hgp), compiler_params=_cparams())(
        a_re, a_im, log_dt, b_re_t, b_im_t)


def _ssm_params_bwd(a_re, a_im, log_dt, b_re_t, b_im_t, d_ar, d_ai, d_bbr, d_bbi):
    def kern(are, aim, ldt, bre, bim, dar, dai, dbr, dbi, o_are, o_aim, o_ldt, o_bre, o_bim):
        prim = (are[...], aim[...], ldt[...], bre[...], bim[...])
        _, vjp = jax.vjp(_ssm_param_fn, *prim)
        g = vjp((dar[...], dai[...], dbr[...], dbi[...]))
        o_are[...] = g[0]
        o_aim[...] = g[1]
        o_ldt[...] = g[2]
        o_bre[...] = g[3]
        o_bim[...] = g[4]

    gp = jax.ShapeDtypeStruct((SSM_G, SSM_P), F32)
    g1 = jax.ShapeDtypeStruct((SSM_G, 1), F32)
    hgp = jax.ShapeDtypeStruct((SSM_H, SSM_G, SSM_P), F32)
    return pl.pallas_call(kern, name="ssm_params_bwd", out_shape=(gp, gp, g1, hgp, hgp), compiler_params=_cparams())(
        a_re, a_im, log_dt, b_re_t, b_im_t, d_ar, d_ai, d_bbr, d_bbi)


def _pow_tables(abar_re, abar_im, tc):
    ls = tc // SUBLANES

    def kern(ar_ref, ai_ref, fr_ref, fi_ref, rr_ref, ri_ref):
        a_re = jnp.broadcast_to(ar_ref[...], (SUBLANES, STATE_W))
        a_im = jnp.broadcast_to(ai_ref[...], (SUBLANES, STATE_W))
        p_re, p_im = a_re, a_im
        for i in range(ls):
            fwd = pl.ds(SUBLANES * i, SUBLANES)
            rev = pl.ds(SUBLANES * (ls - 1 - i), SUBLANES)
            fr_ref[fwd, :] = p_re
            fi_ref[fwd, :] = p_im
            rr_ref[rev, :] = p_re
            ri_ref[rev, :] = p_im
            p_re, p_im = p_re * a_re - p_im * a_im, p_re * a_im + p_im * a_re

    vec = pl.BlockSpec((1, STATE_W), lambda b: (0, b))
    tab = pl.BlockSpec((tc, STATE_W), lambda b: (0, b))
    shp = jax.ShapeDtypeStruct((tc, STATE_ALL), F32)
    return pl.pallas_call(
        kern, name="pow_tables", grid=(LANE_BLOCKS,), in_specs=[vec, vec], out_specs=(tab, tab, tab, tab),
        out_shape=(shp, shp, shp, shp), compiler_params=_cparams("parallel"))(abar_re, abar_im)


def _mod_kernel(c_row, w_ada_bf, b_ada):
    def kern(c_ref, w_ref, b_ref, m_ref, s_ref):
        cv = c_ref[...]
        sc = _silu(cv)
        s_ref[...] = sc
        lhs = jnp.broadcast_to(sc, (SUBLANES, D_MODEL)).astype(BF16)
        m_ref[...] = _dot(lhs, w_ref[...]) + b_ref[...]

    return pl.pallas_call(
        kern, name="ada_mod",
        out_shape=(jax.ShapeDtypeStruct((SUBLANES, 3 * D_MODEL), F32), jax.ShapeDtypeStruct((1, D_MODEL), F32)),
        compiler_params=_cparams())(c_row, w_ada_bf, b_ada)


def _row_spec(tr, width=D_MODEL, col=0):
    return pl.BlockSpec((tr, width), lambda c: (c, col))


def _vec_spec(width=D_MODEL):
    return pl.BlockSpec((1, width), lambda c: (0, 0))


def _col_spec(tr):
    return pl.BlockSpec((D_MODEL, tr), lambda c: (0, c))


def _in_norm(x, g1, scale, shift):
    seq = x.shape[0]
    tr = min(ROW_CHUNK_WIDE, seq)

    def kern(x_ref, g_ref, sc_ref, sh_ref, h_ref, ht_ref):
        xv = x_ref[...]
        r = lax.rsqrt(jnp.mean(xv * xv, axis=-1, keepdims=True) + RMS_EPS)
        h = ((xv * r) * g_ref[...]) * (1.0 + sc_ref[...]) + sh_ref[...]
        h_ref[...] = h.astype(BF16)
        ht_ref[...] = h.T.astype(BF16)

    return pl.pallas_call(
        kern, name="in_norm", grid=(seq // tr,),
        in_specs=[_row_spec(tr), _vec_spec(), _vec_spec(), _vec_spec()], out_specs=(_row_spec(tr), _col_spec(tr)),
        out_shape=(jax.ShapeDtypeStruct((seq, D_MODEL), BF16), jax.ShapeDtypeStruct((D_MODEL, seq), BF16)),
        compiler_params=_cparams("parallel"))(x, g1, scale, shift)


PAD = SUBLANES


def _window_sums(src, cols, w, bufs, rows, ahead):
    cur, cur_cols, step, k = src, cols, 1, 0
    data = pl.ds(PAD, rows)
    while step < w:
        dst = bufs[k % 2]
        dst[data, :] = cur[data, cur_cols] + cur[pl.ds(PAD + (step if ahead else -step), rows), cur_cols]
        cur, cur_cols, step, k = dst, slice(None), 2 * step, k + 1
    return cur, cur_cols


def _in_norm_proj_own(x, g1, scale, shift, w_own, chip, ride):
    seq, n_own = x.shape[0], w_own.shape[1]
    tr = min(PROJ_ROWS, seq)
    nc = seq // tr
    n_rin, n_rout = len(ride.inputs), len(ride.out_shapes)

    def kern(chip_ref, x_ref, g_ref, sc_ref, sh_ref, w_ref, *rest):
        rin, (h_ref, ht_ref, p_ref) = rest[:n_rin], rest[n_rin:n_rin + 3]
        rout, rsem = rest[n_rin + 3:n_rin + 3 + n_rout], rest[n_rin + 3 + n_rout:]
        c = pl.program_id(0)

        @pl.when(c == 0)
        def _():
            ride.start(rin, rout, rsem)

        xv = x_ref[...]
        r = lax.rsqrt(jnp.mean(xv * xv, axis=-1, keepdims=True) + RMS_EPS)
        h = ((xv * r) * g_ref[...]) * (1.0 + sc_ref[...]) + sh_ref[...]
        hb = h.astype(BF16)
        h_ref[...] = hb
        ht_ref[...] = h.T.astype(BF16)
        p_ref[...] = _dot(hb, w_ref[...]).astype(BF16)

        @pl.when(c == nc - 1)
        def _():
            ride.wait(rin, rout, rsem)

    vec = pl.BlockSpec((1, D_MODEL), lambda c, k: (0, 0))
    return pl.pallas_call(
        kern, name="in_norm_proj_own",
        grid_spec=pltpu.PrefetchScalarGridSpec(
            num_scalar_prefetch=1, grid=(nc,),
            in_specs=[pl.BlockSpec((tr, D_MODEL), lambda c, k: (c, 0)), vec, vec, vec,
                      pl.BlockSpec((D_MODEL, n_own), lambda c, k: (0, 0))] + [_ANY] * n_rin,
            out_specs=(pl.BlockSpec((tr, D_MODEL), lambda c, k: (c, 0)), pl.BlockSpec((D_MODEL, tr), lambda c, k: (0, c)),
                       pl.BlockSpec((tr, n_own), lambda c, k: (c, k[0]))) + (_ANY,) * n_rout,
            scratch_shapes=list(ride.scratch)),
        out_shape=(jax.ShapeDtypeStruct((seq, D_MODEL), BF16), jax.ShapeDtypeStruct((D_MODEL, seq), BF16),
                   jax.ShapeDtypeStruct((seq, N_CHIPS * n_own), BF16)) + tuple(ride.out_shapes),
        compiler_params=_cparams("arbitrary"))(chip, x, g1, scale, shift, w_own, *ride.inputs)


def _proj_rest(h, w_blocks, proj, chip, ride):
    seq, n_own = h.shape[0], w_blocks.shape[2]
    tr = min(PROJ_ROWS, seq)
    nm, nn = seq // tr, N_CHIPS - 1
    n_rin, n_rout = len(ride.inputs), len(ride.out_shapes)

    def kern(chip_ref, h_ref, w_ref, _, *rest):
        rin, p_ref = rest[:n_rin], rest[n_rin]
        rout, rsem = rest[n_rin + 1:n_rin + 1 + n_rout], rest[n_rin + 1 + n_rout:]
        i, n = pl.program_id(0), pl.program_id(1)

        @pl.when((i == 0) & (n == 0))
        def _():
            ride.start(rin, rout, rsem)

        p_ref[...] = _dot(h_ref[...], w_ref[0]).astype(BF16)

        @pl.when((i == nm - 1) & (n == nn - 1))
        def _():
            ride.wait(rin, rout, rsem)

    def other(n, k):
        return (k[0] + 1 + n) % N_CHIPS

    return pl.pallas_call(
        kern, name="proj_rest",
        grid_spec=pltpu.PrefetchScalarGridSpec(
            num_scalar_prefetch=1, grid=(nm, nn),
            in_specs=[pl.BlockSpec((tr, D_MODEL), lambda i, n, k: (i, 0)),
                      pl.BlockSpec((1, D_MODEL, n_own), lambda i, n, k: (other(n, k), 0, 0)), _ANY] + [_ANY] * n_rin,
            out_specs=(pl.BlockSpec((tr, n_own), lambda i, n, k: (i, other(n, k))),) + (_ANY,) * n_rout,
            scratch_shapes=list(ride.scratch)),
        out_shape=(jax.ShapeDtypeStruct(proj.shape, BF16),) + tuple(ride.out_shapes),
        input_output_aliases={3: 0},
        compiler_params=_cparams("arbitrary", "arbitrary"))(chip, h, w_blocks, proj, *ride.inputs)


def _pool_windows(ext, bufs, pos, g, w, tr):
    cols = pl.ds(g * POOL_GW, POOL_GW)
    chunk = pl.ds(PAD + HALO, tr)
    cur = ext[chunk, cols]
    win, win_cols = _window_sums(ext, cols, w, bufs, HALO + tr, ahead=False)
    cnt = jnp.minimum(pos + 1, w).astype(F32)
    return win[chunk, win_cols] / cnt - cur


def _zero_pads(refs, rows):
    for ref in refs:
        ref[0:PAD, :] = jnp.zeros((PAD, ref.shape[1]), F32)
        ref[PAD + rows:, :] = jnp.zeros((PAD, ref.shape[1]), F32)


def _pool_fwd(proj, pool_w_bf, pscale):
    seq = proj.shape[0]
    tr = min(ROW_CHUNK_WIDE, seq)
    hb = tr // HALO

    def kern(up_ref, halo_ref, zp_ref, pw_ref, ps_ref, y_ref, yt_ref, ext, buf_a, buf_b):
        c = pl.program_id(0)
        _zero_pads((ext, buf_a, buf_b), HALO + tr)
        ext[pl.ds(PAD, HALO), :] = jnp.where(c > 0, halo_ref[...].astype(F32), 0.0)
        ext[pl.ds(PAD + HALO, tr), :] = up_ref[...].astype(F32)
        pos = c * tr + lax.broadcasted_iota(jnp.int32, (tr, POOL_GW), 0)
        for g, w in enumerate(POOL_WINDOWS):
            cols = pl.ds(g * POOL_GW, POOL_GW)
            pooled = _pool_windows(ext, (buf_a, buf_b), pos, g, w, tr)
            mixed = _dot(pooled.astype(BF16), pw_ref[g])
            y = mixed * ps_ref[:, cols] * _silu(zp_ref[:, cols].astype(F32))
            y_ref[:, cols] = y.astype(BF16)
            yt_ref[cols, :] = y.T.astype(BF16)

    return pl.pallas_call(
        kern, name="pool_fwd", grid=(seq // tr,),
        in_specs=[_row_spec(tr, col=0),
                  pl.BlockSpec((HALO, D_MODEL), lambda c: (jnp.maximum(c * hb - 1, 0), 0)),
                  _row_spec(tr, col=1),
                  pl.BlockSpec((len(POOL_WINDOWS), POOL_GW, POOL_GW), lambda c: (0, 0, 0)),
                  _vec_spec()],
        out_specs=(_row_spec(tr), _col_spec(tr)),
        out_shape=(jax.ShapeDtypeStruct((seq, D_MODEL), BF16), jax.ShapeDtypeStruct((D_MODEL, seq), BF16)),
        scratch_shapes=[pltpu.VMEM((tr + HALO + 2 * PAD, D_MODEL), F32), pltpu.VMEM((tr + HALO + 2 * PAD, POOL_GW), F32),
                        pltpu.VMEM((tr + HALO + 2 * PAD, POOL_GW), F32)],
        compiler_params=_cparams("parallel"))(proj, proj, proj, pool_w_bf, pscale)


def _pool_bwd(proj, dyp, pool_w_bf, pscale, dproj):
    seq = proj.shape[0]
    tr = min(ROW_CHUNK_WIDE, seq)
    hb = tr // HALO
    nc = seq // tr
    n_halo = seq // HALO

    def kern(up_ref, halo_ref, zp_ref, zpn_ref, dyp_ref, dypn_ref, pw_ref, ps_ref, _,
             d01_ref, dpw_ref, dps_ref, ext, dpn, buf_a, buf_b, acc_pw, acc_ps):
        c = pl.program_id(0)

        @pl.when(c == 0)
        def _():
            acc_pw[...] = jnp.zeros_like(acc_pw)
            acc_ps[...] = jnp.zeros_like(acc_ps)

        _zero_pads((ext, dpn, buf_a, buf_b), HALO + tr)
        ext[pl.ds(PAD, HALO), :] = jnp.where(c > 0, halo_ref[...].astype(F32), 0.0)
        ext[pl.ds(PAD + HALO, tr), :] = up_ref[...].astype(F32)
        pos = c * tr + lax.broadcasted_iota(jnp.int32, (tr, POOL_GW), 0)
        pos_n = (c + 1) * tr + lax.broadcasted_iota(jnp.int32, (HALO, POOL_GW), 0)
        has_next = c < nc - 1
        for g, w in enumerate(POOL_WINDOWS):
            cols = pl.ds(g * POOL_GW, POOL_GW)
            pooled_bf = _pool_windows(ext, (buf_a, buf_b), pos, g, w, tr).astype(BF16)
            wg = pw_ref[g]
            mixed = _dot(pooled_bf, wg)
            zp = zp_ref[:, cols].astype(F32)
            sz = _silu(zp)
            dyp_g = dyp_ref[:, cols].astype(F32)
            ps = ps_ref[:, cols]
            dmixed = (dyp_g * ps * sz).astype(BF16)
            acc_ps[:, cols] += _acc8(dyp_g * mixed * sz)
            d01_ref[:, pl.ds(D_MODEL + g * POOL_GW, POOL_GW)] = (dyp_g * mixed * ps * _dsilu(zp)).astype(BF16)
            acc_pw[g] += _dot_tn(pooled_bf, dmixed)
            dpooled = _dot_nt(dmixed, wg)
            dmixed_n = (jnp.where(has_next, dypn_ref[:, cols].astype(F32), 0.0) * ps * _silu(zpn_ref[:, cols].astype(F32))).astype(BF16)
            dpooled_n = _dot_nt(dmixed_n, wg)
            dpn[pl.ds(PAD, tr), :] = dpooled / jnp.minimum(pos + 1, w).astype(F32)
            dpn[pl.ds(PAD + tr, HALO), :] = dpooled_n / jnp.minimum(pos_n + 1, w).astype(F32)
            win, _ = _window_sums(dpn, slice(None), w, (buf_a, buf_b), tr + HALO, ahead=True)
            d01_ref[:, cols] = (win[pl.ds(PAD, tr), :] - dpooled).astype(BF16)

        @pl.when(c == nc - 1)
        def _():
            dpw_ref[...] = acc_pw[...]
            dps_ref[...] = jnp.sum(acc_ps[...], axis=0, keepdims=True)

    nxt = lambda c: (jnp.minimum((c + 1) * hb, n_halo - 1), 0)
    nxt1 = lambda c: (jnp.minimum((c + 1) * hb, n_halo - 1), 1)
    return pl.pallas_call(
        kern, name="pool_bwd", grid=(nc,),
        in_specs=[_row_spec(tr, col=0),
                  pl.BlockSpec((HALO, D_MODEL), lambda c: (jnp.maximum(c * hb - 1, 0), 0)),
                  _row_spec(tr, col=1),
                  pl.BlockSpec((HALO, D_MODEL), nxt1),
                  _row_spec(tr),
                  pl.BlockSpec((HALO, D_MODEL), nxt),
                  pl.BlockSpec((len(POOL_WINDOWS), POOL_GW, POOL_GW), lambda c: (0, 0, 0)),
                  _vec_spec(), _ANY],
        out_specs=(pl.BlockSpec((tr, 2 * D_MODEL), lambda c: (c, 0)),
                   pl.BlockSpec((len(POOL_WINDOWS), POOL_GW, POOL_GW), lambda c: (0, 0, 0)),
                   _vec_spec()),
        out_shape=(jax.ShapeDtypeStruct(dproj.shape, BF16),
                   jax.ShapeDtypeStruct((len(POOL_WINDOWS), POOL_GW, POOL_GW), F32),
                   jax.ShapeDtypeStruct((1, D_MODEL), F32)),
        scratch_shapes=[pltpu.VMEM((tr + HALO + 2 * PAD, D_MODEL), F32)]
        + [pltpu.VMEM((tr + HALO + 2 * PAD, POOL_GW), F32)] * 3
        + [pltpu.VMEM((len(POOL_WINDOWS), POOL_GW, POOL_GW), F32), pltpu.VMEM((SUBLANES, D_MODEL), F32)],
        input_output_aliases={8: 0},
        compiler_params=_cparams("arbitrary"))(proj, proj, proj, proj, dyp, dyp, pool_w_bf, pscale, dproj)


def _glu_fwd(ys, proj, glu_w_bf, glu_b):
    seq = ys.shape[0]
    tr = min(ROW_CHUNK_WIDE, seq)

    def kern(ys_ref, zs_ref, w_ref, b_ref, o_ref, ot_ref):
        yg = _gelu(ys_ref[...])
        q = _dot(yg.astype(BF16), w_ref[...]) + b_ref[...]
        y = yg * _sigmoid(q) * _silu(zs_ref[...].astype(F32))
        o_ref[...] = y.astype(BF16)
        ot_ref[...] = y.T.astype(BF16)

    return pl.pallas_call(
        kern, name="glu_fwd", grid=(seq // tr,),
        in_specs=[_row_spec(tr), _row_spec(tr, col=3), pl.BlockSpec((D_MODEL, D_MODEL), lambda c: (0, 0)), _vec_spec()],
        out_specs=(_row_spec(tr), _col_spec(tr)),
        out_shape=(jax.ShapeDtypeStruct((seq, D_MODEL), BF16), jax.ShapeDtypeStruct((D_MODEL, seq), BF16)),
        compiler_params=_cparams("parallel"))(ys, proj, glu_w_bf, glu_b)


def _glu_bwd(ys, proj, dyssm, glu_w_bf, glu_b, dproj):
    seq = ys.shape[0]
    tr = min(ROW_CHUNK_WIDE, seq)
    nc = seq // tr

    def kern(ys_ref, zs_ref, dy_ref, w_ref, b_ref, _, dys_ref, dzs_ref, dq_ref, yg_ref, db_ref, acc_b):
        c = pl.program_id(0)

        @pl.when(c == 0)
        def _():
            acc_b[...] = jnp.zeros_like(acc_b)

        yg, dgelu = _gelu_and_grad(ys_ref[...])
        yg_bf = yg.astype(BF16)
        q = _dot(yg_bf, w_ref[...]) + b_ref[...]
        sg = _sigmoid(q)
        silu_z, dsilu_z = _silu_and_grad(zs_ref[...].astype(F32))
        dyv = dy_ref[...].astype(F32)
        dyglu = dyv * silu_z
        yglu = yg * sg
        dzs_ref[...] = (dyv * yglu * dsilu_z).astype(BF16)
        dq = dyglu * yglu * (1.0 - sg)
        dq_bf = dq.astype(BF16)
        acc_b[...] += _acc8(dq)
        dyg = dyglu * sg + _dot_nt(dq_bf, w_ref[...])
        dys_ref[...] = dyg * dgelu
        dq_ref[...] = dq_bf
        yg_ref[...] = yg.T.astype(BF16)

        @pl.when(c == nc - 1)
        def _():
            db_ref[...] = jnp.sum(acc_b[...], axis=0, keepdims=True)

    bf = jax.ShapeDtypeStruct((seq, D_MODEL), BF16)
    return pl.pallas_call(
        kern, name="glu_bwd", grid=(nc,),
        in_specs=[_row_spec(tr), _row_spec(tr, col=3), _row_spec(tr),
                  pl.BlockSpec((D_MODEL, D_MODEL), lambda c: (0, 0)), _vec_spec(), _ANY],
        out_specs=(_row_spec(tr), _row_spec(tr, col=3), _row_spec(tr), _col_spec(tr), _vec_spec()),
        out_shape=(jax.ShapeDtypeStruct((seq, D_MODEL), F32), jax.ShapeDtypeStruct(dproj.shape, BF16), bf,
                   jax.ShapeDtypeStruct((D_MODEL, seq), BF16), jax.ShapeDtypeStruct((1, D_MODEL), F32)),
        scratch_shapes=[pltpu.VMEM((SUBLANES, D_MODEL), F32)],
        input_output_aliases={5: 1},
        compiler_params=_cparams("arbitrary"))(ys, proj, dyssm, glu_w_bf, glu_b, dproj)


def _out_fwd_bwd(ypool, yssm, proj, x, tgt, gate, g2, wbp_bf, wbs_bf, wout_bf):
    seq = x.shape[0]
    tr = min(ROW_CHUNK, seq)
    nc = seq // tr

    def kern(yp_ref, ysm_ref, gp_ref, gs_ref, x_ref, t_ref, gate_ref, g2_ref, wbp_ref, wbs_ref, wo_ref,
             dy_ref, dyp_ref, dys_ref, d45_ref, mb_ref, dob_ref, dbp_ref, dbs_ref, loss_ref, dgate_ref, dg2_ref,
             acc_l, acc_gate, acc_g2):
        c = pl.program_id(0)

        @pl.when(c == 0)
        def _():
            acc_l[...] = jnp.zeros_like(acc_l)
            acc_gate[...] = jnp.zeros_like(acc_gate)
            acc_g2[...] = jnp.zeros_like(acc_g2)

        bp = _dot(yp_ref[...], wbp_ref[...])
        bs = _dot(ysm_ref[...], wbs_ref[...])
        sp = _sigmoid(gp_ref[...].astype(F32))
        ss = _sigmoid(gs_ref[...].astype(F32))
        merged = sp * bp + ss * bs
        mb = merged.astype(BF16)
        out = _dot(mb, wo_ref[...])
        r2 = lax.rsqrt(jnp.mean(out * out, axis=-1, keepdims=True) + RMS_EPS)
        oh = out * r2
        gate_v, g2_v = gate_ref[...], g2_ref[...]
        ohg = oh * g2_v
        diff = (x_ref[...] + gate_v * ohg) - t_ref[...]
        acc_l[...] += _acc8(diff * diff)
        dyv = diff * (1.0 / D_MODEL)
        dy_ref[...] = dyv
        dy_oh = dyv * oh
        acc_gate[...] += _acc8(dy_oh * g2_v)
        acc_g2[...] += _acc8(dy_oh * gate_v)
        gg = gate_v * g2_v
        doh = dyv * gg
        dout = r2 * (doh - oh * jnp.mean(dy_oh * gg, axis=-1, keepdims=True))
        dob = dout.astype(BF16)
        dmerged = _dot_nt(dob, wo_ref[...])
        dbp_f = dmerged * sp
        dbs_f = dmerged * ss
        dbp = dbp_f.astype(BF16)
        dbs = dbs_f.astype(BF16)
        d45_ref[:, 0:D_MODEL] = (dbp_f * bp * (1.0 - sp)).astype(BF16)
        d45_ref[:, D_MODEL:] = (dbs_f * bs * (1.0 - ss)).astype(BF16)
        dyp_ref[...] = _dot_nt(dbp, wbp_ref[...]).astype(BF16)
        dys_ref[...] = _dot_nt(dbs, wbs_ref[...]).astype(BF16)
        mb_ref[...] = merged.T.astype(BF16)
        dob_ref[...] = dob
        dbp_ref[...] = dbp
        dbs_ref[...] = dbs

        @pl.when(c == nc - 1)
        def _():
            tot = jnp.sum(acc_l[...], axis=0, keepdims=True)
            loss_ref[...] = jnp.sum(tot, axis=1, keepdims=True) * (0.5 / D_MODEL)
            dgate_ref[...] = jnp.sum(acc_gate[...], axis=0, keepdims=True)
            dg2_ref[...] = jnp.sum(acc_g2[...], axis=0, keepdims=True)

    wspec = pl.BlockSpec((D_MODEL, D_MODEL), lambda c: (0, 0))
    f32 = jax.ShapeDtypeStruct((seq, D_MODEL), F32)
    bf = jax.ShapeDtypeStruct((seq, D_MODEL), BF16)
    vec = jax.ShapeDtypeStruct((1, D_MODEL), F32)
    acc = pltpu.VMEM((SUBLANES, D_MODEL), F32)
    return pl.pallas_call(
        kern, name="out_fwd_bwd", grid=(nc,),
        in_specs=[_row_spec(tr), _row_spec(tr), _row_spec(tr, col=4), _row_spec(tr, col=5), _row_spec(tr), _row_spec(tr),
                  _vec_spec(), _vec_spec(), wspec, wspec, wspec],
        out_specs=(_row_spec(tr), _row_spec(tr), _row_spec(tr), pl.BlockSpec((tr, 2 * D_MODEL), lambda c: (c, 2)),
                   _col_spec(tr), _row_spec(tr), _row_spec(tr), _row_spec(tr),
                   pl.BlockSpec((1, 1), lambda c: (0, 0)), _vec_spec(), _vec_spec()),
        out_shape=(f32, bf, bf, jax.ShapeDtypeStruct((seq, proj.shape[1]), BF16),
                   jax.ShapeDtypeStruct((D_MODEL, seq), BF16), bf, bf, bf,
                   jax.ShapeDtypeStruct((1, 1), F32), vec, vec),
        scratch_shapes=[acc, acc, acc],
        compiler_params=_cparams("arbitrary"))(ypool, yssm, proj, proj, x, tgt, gate, g2, wbp_bf, wbs_bf, wout_bf)


def _in_bwd(dh, x, dy, g1, scale, ride=None):
    seq = x.shape[0]
    tr = min(ROW_CHUNK_WIDE, seq)
    nc = seq // tr
    n_rin = len(ride.inputs) if ride else 0
    n_rout = len(ride.out_shapes) if ride else 0

    def kern(dh_ref, x_ref, dy_ref, g_ref, sc_ref, *rest):
        rin, (dx_ref, dsh_ref, dsc_ref, dg_ref) = rest[:n_rin], rest[n_rin:n_rin + 4]
        rout = rest[n_rin + 4:n_rin + 4 + n_rout]
        a_sh, a_sc, a_g = rest[n_rin + 4 + n_rout:n_rin + 7 + n_rout]
        rsem = rest[n_rin + 7 + n_rout:]
        c = pl.program_id(0)

        @pl.when(c == 0)
        def _():
            a_sh[...] = jnp.zeros_like(a_sh)
            a_sc[...] = jnp.zeros_like(a_sc)
            a_g[...] = jnp.zeros_like(a_g)
            if ride:
                ride.start(rin, rout, rsem)

        xv = x_ref[...]
        r = lax.rsqrt(jnp.mean(xv * xv, axis=-1, keepdims=True) + RMS_EPS)
        xh = xv * r
        g = g_ref[...]
        dhv = dh_ref[...]
        a_sh[...] += _acc8(dhv)
        a_sc[...] += _acc8(dhv * (xh * g))
        dn = dhv * (1.0 + sc_ref[...])
        a_g[...] += _acc8(dn * xh)
        dxh = dn * g
        dx_ref[...] = dy_ref[...] + r * (dxh - xh * jnp.mean(dxh * xh, axis=-1, keepdims=True))

        @pl.when(c == nc - 1)
        def _():
            dsh_ref[...] = jnp.sum(a_sh[...], axis=0, keepdims=True)
            dsc_ref[...] = jnp.sum(a_sc[...], axis=0, keepdims=True)
            dg_ref[...] = jnp.sum(a_g[...], axis=0, keepdims=True)
            if ride:
                ride.wait(rin, rout, rsem)

    vec = jax.ShapeDtypeStruct((1, D_MODEL), F32)
    acc = pltpu.VMEM((SUBLANES, D_MODEL), F32)
    n_in, n_out = 5, 4
    return pl.pallas_call(
        kern, name="in_bwd", grid=(nc,),
        in_specs=[_row_spec(tr), _row_spec(tr), _row_spec(tr), _vec_spec(), _vec_spec()] + [_ANY] * n_rin,
        out_specs=(_row_spec(tr), _vec_spec(), _vec_spec(), _vec_spec()) + (_ANY,) * n_rout,
        out_shape=(jax.ShapeDtypeStruct((seq, D_MODEL), F32), vec, vec, vec) + tuple(ride.out_shapes if ride else ()),
        scratch_shapes=[acc, acc, acc] + list(ride.scratch if ride else ()),
        input_output_aliases={n_in + i: n_out + o for i, o in (ride.in_place if ride else ())},
        compiler_params=_cparams("arbitrary"))(dh, x, dy, g1, scale, *(ride.inputs if ride else ()))


SLAB = 2 * SUBLANES


def _local_scan(a_re, a_im, br, bi, xr, xi, row0, ls, reverse, init=None, xb=None):
    if init is None:
        x_re = jnp.zeros((SUBLANES, STATE_W), F32)
        x_im = jnp.zeros((SUBLANES, STATE_W), F32)
    else:
        x_re, x_im = init
    for i in (range(ls - 1, -1, -1) if reverse else range(ls)):
        src = pl.ds(SUBLANES * i, SUBLANES)
        dst = pl.ds(row0 + SUBLANES * i, SUBLANES)
        n_re = a_re * x_re - a_im * x_im + br[src, :]
        n_im = a_re * x_im + a_im * x_re + bi[src, :]
        if xb is not None and i % 2 == 1:
            pair = pl.ds(SUBLANES * (i - 1), SLAB)
            xb[0][pair, :] = jnp.concatenate([x_re, n_re], axis=0).astype(BF16)
            xb[1][pair, :] = jnp.concatenate([x_im, n_im], axis=0).astype(BF16)
        x_re, x_im = n_re, n_im
        xr[dst, :] = x_re
        xi[dst, :] = x_im
    return x_re, x_im


def _two(v):
    return jnp.concatenate([v, v], axis=0)


def _unpermute_rhs(v, sel):
    hi = v.astype(BF16)
    r1 = v - hi.astype(F32)
    mid = r1.astype(BF16)
    lo = (r1 - mid.astype(F32)).astype(BF16)
    return _dot(hi, sel) + _dot(mid, sel) + _dot(lo, sel)


def _scan_specs(tc, nb, rows_of):
    return dict(
        us=pl.BlockSpec((tc, nb * LANES), lambda b, c: (rows_of(c), 2 * D_MODEL // (nb * LANES) + b)),
        tok=pl.BlockSpec((tc, nb * LANES), lambda b, c: (rows_of(c), b)),
        bblk=pl.BlockSpec((nb, LANES, STATE_W), lambda b, c: (b, 0, 0)),
        cblk=pl.BlockSpec((nb, STATE_W, LANES), lambda b, c: (b, 0, 0)),
        vec=pl.BlockSpec((1, nb * STATE_W), lambda b, c: (0, b)),
        tab=pl.BlockSpec((tc, nb * STATE_W), lambda b, c: (0, b)),
        car=pl.BlockSpec((SUBLANES, nb * STATE_W), lambda b, c: (rows_of(c), b)),
        dvec=pl.BlockSpec((1, nb * LANES), lambda b, c: (0, b)))


def _ssm_scan_fwd(proj, bb_re, bb_im, cm_re, cm_im, abar_re, abar_im, pw_re, pw_im, d_skip, tc):
    seq = proj.shape[0]
    nc = seq // tc
    ls = tc // SUBLANES
    nb = SCAN_BLOCKS_FWD

    def kern(us_ref, bbr_ref, bbi_ref, cmr_ref, cmi_ref, ar_ref, ai_ref, pwr_ref, pwi_ref, d_ref,
             ys_ref, ecr_ref, eci_ref, bur, bui, car_r, car_i, end_r, end_i, upb, xb_r, xb_i, *nat):
        c = pl.program_id(1)

        @pl.when(c == 0)
        def _():
            car_r[...] = jnp.zeros_like(car_r)
            car_i[...] = jnp.zeros_like(car_i)

        for j in range(nb):
            cols = pl.ds(j * LANES, LANES)
            scols = pl.ds(j * STATE_W, STATE_W)
            nat[j][...] = us_ref[:, cols].astype(F32)
            for i in range(ls):
                upb[j, pl.ds(SUBLANES * i, SUBLANES), :] = nat[j][pl.ds(i, SUBLANES, stride=ls), :]
            u = upb[j]
            up = u.astype(BF16)
            bur[j] = _dot(up, bbr_ref[j])
            bui[j] = _dot(up, bbi_ref[j])
            a_re = jnp.broadcast_to(ar_ref[:, scols], (SUBLANES, STATE_W))
            a_im = jnp.broadcast_to(ai_ref[:, scols], (SUBLANES, STATE_W))
            x_re, x_im = _local_scan(a_re, a_im, bur.at[j], bui.at[j], bur.at[j], bui.at[j], 0, ls, False)
            end_r[j] = x_re
            end_i[j] = x_im
            big_re = pwr_ref[tc - 1:tc, scols]
            big_im = pwi_ref[tc - 1:tc, scols]
            e_re = car_r[j, 0:1, :]
            e_im = car_i[j, 0:1, :]
            for s in range(SUBLANES):
                n_re = end_r[j, s:s + 1, :] + big_re * e_re - big_im * e_im
                n_im = end_i[j, s:s + 1, :] + big_re * e_im + big_im * e_re
                e_re, e_im = n_re, n_im
                if s < SUBLANES - 1:
                    car_r[j, s + 1:s + 2, :] = e_re
                    car_i[j, s + 1:s + 2, :] = e_im
            ec_re = car_r[j]
            ec_im = car_i[j]
            ecr_ref[:, scols] = ec_re
            eci_ref[:, scols] = ec_im
            e2_re, e2_im = _two(ec_re), _two(ec_im)
            for k in range(tc // SLAB):
                rows_k = pl.ds(SLAB * k, SLAB)
                p_re = pwr_ref[rows_k, scols]
                p_im = pwi_ref[rows_k, scols]
                xb_r[j, rows_k, :] = (bur[j, rows_k, :] + p_re * e2_re - p_im * e2_im).astype(BF16)
                xb_i[j, rows_k, :] = (bui[j, rows_k, :] + p_re * e2_im + p_im * e2_re).astype(BF16)
            upb[j] = _dot(xb_r[j], cmr_ref[j]) - _dot(xb_i[j], cmi_ref[j]) + d_ref[:, cols] * u
            for i in range(ls):
                nat[j][pl.ds(i, SUBLANES, stride=ls), :] = upb[j, pl.ds(SUBLANES * i, SUBLANES), :]
            ys_ref[:, cols] = nat[j][...]
            car_r[j, 0:1, :] = e_re
            car_i[j, 0:1, :] = e_im

    sp = _scan_specs(tc, nb, lambda c: c)
    carry_shape = jax.ShapeDtypeStruct((nc * SUBLANES, STATE_ALL), F32)
    small = pltpu.VMEM((nb, SUBLANES, STATE_W), F32)
    big = pltpu.VMEM((nb, tc, STATE_W), F32)
    return pl.pallas_call(
        kern, name="ssm_scan_fwd", grid=(LANE_BLOCKS // nb, nc),
        in_specs=[sp["us"], sp["bblk"], sp["bblk"], sp["cblk"], sp["cblk"], sp["vec"], sp["vec"], sp["tab"], sp["tab"],
                  sp["dvec"]],
        out_specs=(sp["tok"], sp["car"], sp["car"]),
        out_shape=(jax.ShapeDtypeStruct((seq, D_MODEL), F32), carry_shape, carry_shape),
        scratch_shapes=[big, big, small, small, small, small, pltpu.VMEM((nb, tc, LANES), F32),
                        pltpu.VMEM((nb, tc, STATE_W), BF16), pltpu.VMEM((nb, tc, STATE_W), BF16)]
        + [pltpu.VMEM((tc, LANES), F32)] * nb,
        compiler_params=_cparams("parallel", "arbitrary"),
    )(proj, bb_re, bb_im, cm_re, cm_im, abar_re, abar_im, pw_re, pw_im, d_skip)


def _ssm_scan_bwd(proj, dys, ec_re, ec_im, bb_re, bb_im, cm_re, cm_im, abar_re, abar_im,
                  pw_re, pw_im, pv_re, pv_im, d_skip, dproj, tc):
    seq = proj.shape[0]
    nc = seq // tc
    ls = tc // SUBLANES
    nb = SCAN_BLOCKS_BWD

    def kern(us_ref, dys_ref, ecr_ref, eci_ref, bbr_ref, bbi_ref, cmr_ref, cmi_ref, ar_ref, ai_ref,
             pwr_ref, pwi_ref, pvr_ref, pvi_ref, d_ref, _,
             dus_ref, dbbr_ref, dbbi_ref, dcmr_ref, dcmi_ref, dar_ref, dai_ref, dd_ref,
             bur, bui, xr, xi, gr, gi, fc_r, fc_i, a_bbr, a_bbi, a_cmr, a_cmi, a_ar, a_ai, a_dd, upb, dpb, hb_r, hb_i,
             *nat):
        c = pl.program_id(1)

        @pl.when(c == 0)
        def _():
            for ref in (fc_r, fc_i, a_bbr, a_bbi, a_cmr, a_cmi, a_ar, a_ai, a_dd):
                ref[...] = jnp.zeros_like(ref)

        for j in range(nb):
            cols = pl.ds(j * LANES, LANES)
            scols = pl.ds(j * STATE_W, STATE_W)
            nat_u, nat_d = nat[2 * j], nat[2 * j + 1]
            nat_u[...] = us_ref[:, cols].astype(F32)
            nat_d[...] = dys_ref[:, cols]
            for i in range(ls):
                rows_i = pl.ds(SUBLANES * i, SUBLANES)
                upb[j, rows_i, :] = nat_u[pl.ds(i, SUBLANES, stride=ls), :]
                dpb[j, rows_i, :] = nat_d[pl.ds(i, SUBLANES, stride=ls), :]
            u = upb[j]
            dysv = dpb[j]
            a_dd[j] += _acc8(dysv * u)
            up = u.astype(BF16)
            bur[j] = _dot(up, bbr_ref[j])
            bui[j] = _dot(up, bbi_ref[j])
            a_re = jnp.broadcast_to(ar_ref[:, scols], (SUBLANES, STATE_W))
            a_im = jnp.broadcast_to(ai_ref[:, scols], (SUBLANES, STATE_W))
            ec_r = ecr_ref[:, scols]
            ec_i = eci_ref[:, scols]
            xr[j, 0:SUBLANES, :] = ec_r
            xi[j, 0:SUBLANES, :] = ec_i
            _local_scan(a_re, a_im, bur.at[j], bui.at[j], xr.at[j], xi.at[j], SUBLANES, ls, False, init=(ec_r, ec_i),
                        xb=(hb_r.at[j], hb_i.at[j]))
            dysp = dysv.astype(BF16)
            a_cmr[j] += _dot_tn(dysp, hb_r[j])
            a_cmi[j] -= _dot_tn(dysp, hb_i[j])
            gr[j] = _dot_nt(dysp, cmr_ref[j])
            gi[j] = -_dot_nt(dysp, cmi_ref[j])
            _local_scan(a_re, -a_im, gr.at[j], gi.at[j], gr.at[j], gi.at[j], 0, ls, True)
            big_re = pwr_ref[tc - 1:tc, scols]
            big_im = -pwi_ref[tc - 1:tc, scols]
            f_re = fc_r[j, SUBLANES - 1:SUBLANES, :]
            f_im = fc_i[j, SUBLANES - 1:SUBLANES, :]
            for s in range(SUBLANES - 1, -1, -1):
                n_re = gr[j, s:s + 1, :] + big_re * f_re - big_im * f_im
                n_im = gi[j, s:s + 1, :] + big_re * f_im + big_im * f_re
                f_re, f_im = n_re, n_im
                if s > 0:
                    fc_r[j, s - 1:s, :] = f_re
                    fc_i[j, s - 1:s, :] = f_im
            f2_r, f2_i = _two(fc_r[j]), _two(fc_i[j])
            acc_r = jnp.zeros((SUBLANES, STATE_W), F32)
            acc_i = jnp.zeros((SUBLANES, STATE_W), F32)
            for k in range(tc // SLAB):
                rows_k = pl.ds(SLAB * k, SLAB)
                q_re = pvr_ref[rows_k, scols]
                q_im = pvi_ref[rows_k, scols]
                lam_re = gr[j, rows_k, :] + q_re * f2_r + q_im * f2_i
                lam_im = gi[j, rows_k, :] + q_re * f2_i - q_im * f2_r
                xp_re = xr[j, rows_k, :]
                xp_im = xi[j, rows_k, :]
                d_r = lam_re * xp_re + lam_im * xp_im
                d_i = lam_im * xp_re - lam_re * xp_im
                acc_r = acc_r + (d_r[0:SUBLANES] + d_r[SUBLANES:])
                acc_i = acc_i + (d_i[0:SUBLANES] + d_i[SUBLANES:])
                hb_r[j, rows_k, :] = lam_re.astype(BF16)
                hb_i[j, rows_k, :] = lam_im.astype(BF16)
            a_ar[j] += acc_r
            a_ai[j] += acc_i
            fc_r[j, SUBLANES - 1:SUBLANES, :] = f_re
            fc_i[j, SUBLANES - 1:SUBLANES, :] = f_im
            lb_re = hb_r[j]
            lb_im = hb_i[j]
            a_bbr[j] += _dot_tn(up, lb_re)
            a_bbi[j] += _dot_tn(up, lb_im)
            dpb[j] = _dot_nt(lb_re, bbr_ref[j]) + _dot_nt(lb_im, bbi_ref[j]) + dysv * d_ref[:, cols]
            for i in range(ls):
                nat_d[pl.ds(i, SUBLANES, stride=ls), :] = dpb[j, pl.ds(SUBLANES * i, SUBLANES), :]
            dus_ref[:, cols] = nat_d[...].astype(BF16)

        @pl.when(c == nc - 1)
        def _():
            row_g = lax.broadcasted_iota(jnp.int32, (LANES, STATE_W), 0) // SSM_H
            col_g = lax.broadcasted_iota(jnp.int32, (LANES, STATE_W), 1) // SSM_P
            fold = (lax.broadcasted_iota(jnp.int32, (STATE_W, SSM_P), 0) % SSM_P
                    == lax.broadcasted_iota(jnp.int32, (STATE_W, SSM_P), 1)).astype(BF16)
            for j in range(nb):
                rows_j = pl.ds(j * LANES, LANES)
                for acc, out in ((a_bbr, dbbr_ref), (a_bbi, dbbi_ref), (a_cmr, dcmr_ref), (a_cmi, dcmi_ref)):
                    out[rows_j, :] = _unpermute_rhs(jnp.where(row_g == col_g, acc[j], 0.0), fold)
                dar_ref[:, pl.ds(j * STATE_W, STATE_W)] = jnp.sum(a_ar[j], axis=0, keepdims=True)
                dai_ref[:, pl.ds(j * STATE_W, STATE_W)] = jnp.sum(a_ai[j], axis=0, keepdims=True)
                dd_ref[:, pl.ds(j * LANES, LANES)] = jnp.sum(a_dd[j], axis=0, keepdims=True)

    sp = _scan_specs(tc, nb, lambda c: nc - 1 - c)
    ghp = pl.BlockSpec((nb * LANES, SSM_P), lambda b, c: (b, 0))
    ghp_shape = jax.ShapeDtypeStruct((SSM_G * SSM_H, SSM_P), F32)
    small = pltpu.VMEM((nb, SUBLANES, STATE_W), F32)
    big = pltpu.VMEM((nb, tc, STATE_W), F32)
    bigp = pltpu.VMEM((nb, tc + SUBLANES, STATE_W), F32)
    blk = pltpu.VMEM((nb, LANES, STATE_W), F32)
    tok = pltpu.VMEM((nb, tc, LANES), F32)
    return pl.pallas_call(
        kern, name="ssm_scan_bwd", grid=(LANE_BLOCKS // nb, nc),
        in_specs=[sp["us"], sp["tok"], sp["car"], sp["car"], sp["bblk"], sp["bblk"], sp["cblk"], sp["cblk"],
                  sp["vec"], sp["vec"], sp["tab"], sp["tab"], sp["tab"], sp["tab"], sp["dvec"], _ANY],
        out_specs=(sp["us"], ghp, ghp, ghp, ghp, sp["vec"], sp["vec"], sp["dvec"]),
        out_shape=(jax.ShapeDtypeStruct(dproj.shape, BF16), ghp_shape, ghp_shape, ghp_shape, ghp_shape,
                   jax.ShapeDtypeStruct((1, STATE_ALL), F32), jax.ShapeDtypeStruct((1, STATE_ALL), F32),
                   jax.ShapeDtypeStruct((1, D_MODEL), F32)),
        scratch_shapes=[big, big, bigp, bigp, big, big, small, small, blk, blk, blk, blk,
                        small, small, pltpu.VMEM((nb, SUBLANES, LANES), F32), tok, tok,
                        pltpu.VMEM((nb, tc, STATE_W), BF16), pltpu.VMEM((nb, tc, STATE_W), BF16)]
        + [pltpu.VMEM((tc, LANES), F32)] * (2 * nb),
        input_output_aliases={15: 0},
        compiler_params=_cparams("parallel", "arbitrary"),
    )(proj, dys, ec_re, ec_im, bb_re, bb_im, cm_re, cm_im, abar_re, abar_im, pw_re, pw_im, pv_re, pv_im, d_skip, dproj)


def _eye5():
    return jnp.asarray(np.eye(GROUPS_PER_BLOCK, dtype=np.float32)[None, :, None, :, None])


def _embed_b(bb_t):
    t = bb_t.transpose(1, 0, 2).reshape(LANE_BLOCKS, GROUPS_PER_BLOCK, SSM_H, 1, SSM_P)
    return (t * _eye5()).reshape(LANE_BLOCKS, LANES, STATE_W)


def _embed_c(c_ghp):
    t = c_ghp.transpose(0, 2, 1).reshape(LANE_BLOCKS, GROUPS_PER_BLOCK, SSM_P, 1, SSM_H)
    return (t * _eye5()).reshape(LANE_BLOCKS, STATE_W, LANES)


def _local_step(x, c_row, tgt, w_ada_bf, b_ada, g1, g2, w_in_bf, pool_w_bf, pscale, a_re, a_im, log_dt,
                b_re_t, b_im_t, c_re, c_im, d_skip, glu_w_bf, glu_b, wbp_bf, wbs_bf, wout_bf,
                split_proj=None, ride_for_dw_in=None, ride_for_dh=None, ride_for_in_bwd=None):
    seq = x.shape[0]
    tc = min(SCAN_CHUNK, seq)
    mod8, silu_c = _mod_kernel(c_row, w_ada_bf, b_ada)
    mod = mod8[0:1]
    shift, scale, gate = mod[:, 0:D_MODEL], mod[:, D_MODEL:2 * D_MODEL], mod[:, 2 * D_MODEL:]

    abar_re, abar_im, bb_re_t, bb_im_t = _ssm_params(a_re, a_im, log_dt, b_re_t, b_im_t)
    abar_re_f, abar_im_f = abar_re.reshape(1, STATE_ALL), abar_im.reshape(1, STATE_ALL)
    pw_re, pw_im, pv_re, pv_im = _pow_tables(abar_re_f, abar_im_f, tc)
    bbe_re, bbe_im = _embed_b(bb_re_t).astype(BF16), _embed_b(bb_im_t).astype(BF16)
    cme_re, cme_im = _embed_c(c_re).astype(BF16), _embed_c(c_im).astype(BF16)
    d_row = d_skip.reshape(1, D_MODEL)

    if split_proj:
        w_own, chip, w_in_ride, unpack_w_in, late_ride, unpack_late = split_proj
        h, h_t, proj, w_blocks = _in_norm_proj_own(x, g1, scale, shift, w_own, chip, w_in_ride)
        w_in_bf = unpack_w_in(w_blocks)
        proj, *gathered = _proj_rest(h, w_blocks, proj, chip, late_ride)
        pool_w_bf, glu_w_bf, wbp_bf, wbs_bf, wout_bf = unpack_late(*gathered)
    else:
        h, h_t = _in_norm(x, g1, scale, shift)
        proj = _mm([h], [w_in_bf], name="proj", out_dtype=BF16, bm=1024, bn=1536, bk=1024)
    ypool, ypool_t = _pool_fwd(proj, pool_w_bf, pscale)
    ys, ec_re, ec_im = _ssm_scan_fwd(proj, bbe_re, bbe_im, cme_re, cme_im, abar_re_f, abar_im_f,
                                      pw_re, pw_im, d_row, tc)
    yssm, yssm_t = _glu_fwd(ys, proj, glu_w_bf, glu_b)
    (dy, dypool, dyssm, dproj, merged_t, dob, dbp, dbs, loss, dgate, dg2) = _out_fwd_bwd(
        ypool, yssm, proj, x, tgt, gate, g2, wbp_bf, wbs_bf, wout_bf)

    d_wout = _mm([merged_t], [dob], name="dw_out", bm=1024, bn=1024, bk=2048)
    d_wbp = _mm([ypool_t], [dbp], name="dw_bp", bm=1024, bn=1024, bk=2048)
    d_wbs = _mm([yssm_t], [dbs], name="dw_bs", bm=1024, bn=1024, bk=2048)
    dys, dproj, dq, yg_t, d_glu_b = _glu_bwd(ys, proj, dyssm, glu_w_bf, glu_b, dproj)
    d_glu_w = _mm([yg_t], [dq], name="dw_glu", bm=1024, bn=1024, bk=2048)
    (dproj, dbbe_re, dbbe_im, dcme_re, dcme_im, d_abar_re, d_abar_im, d_dskip) = _ssm_scan_bwd(
        proj, dys, ec_re, ec_im, bbe_re, bbe_im, cme_re, cme_im, abar_re_f, abar_im_f,
        pw_re, pw_im, pv_re, pv_im, d_row, dproj, tc)
    dproj, d_pool_w, d_pscale = _pool_bwd(proj, dypool, pool_w_bf, pscale, dproj)
    dparts = [dproj]
    small_ready = dict(
        dg2=dg2, d_pscale=d_pscale, d_glu_b=d_glu_b, d_dskip=d_dskip, d_abar_re=d_abar_re, d_abar_im=d_abar_im,
        d_bb_re_t=dbbe_re.reshape(SSM_G, SSM_H, SSM_P).transpose(1, 0, 2),
        d_bb_im_t=dbbe_im.reshape(SSM_G, SSM_H, SSM_P).transpose(1, 0, 2),
        d_c_re=dcme_re.reshape(SSM_G, SSM_H, SSM_P), d_c_im=dcme_im.reshape(SSM_G, SSM_H, SSM_P))
    ride = ride_for_dw_in(small_ready) if ride_for_dw_in else None
    d_win = _mm([h_t], dparts, name="dw_in", bm=1024, bn=1024, bk=2048, ride=ride)
    rode_dw_in = ()
    if ride:
        d_win, rode_dw_in = d_win[0], tuple(d_win[1:])
    big_grads = dict(d_win=d_win, d_glu_w=d_glu_w, d_wbp=d_wbp, d_wbs=d_wbs, d_wout=d_wout, d_pool_w=d_pool_w)
    ride = ride_for_dh(big_grads) if ride_for_dh else None
    dh = _mm(dparts, [w_in_bf], tb=True, name="dh", bm=2048, bn=1024, bk=1024, ride=ride)
    rode = ()
    if ride:
        dh, rode = dh[0], tuple(dh[1:])
    ride = ride_for_in_bwd(rode) if ride_for_in_bwd else None
    grad_x, dshift, dscale, dg1, *rode_in_bwd = _in_bwd(dh, x, dy, g1, scale, ride=ride)
    dmod = jnp.concatenate([dshift, dscale, dgate], axis=1)
    return dict(
        rode_in_bwd=tuple(rode_in_bwd), rode_dw_in=rode_dw_in, loss=loss[0, 0], grad_x=grad_x, dmod=dmod,
        silu_c=silu_c, dg1=dg1,
        **small_ready, **big_grads)


def _position():
    x, y, c = lax.axis_index("x"), lax.axis_index("y"), lax.axis_index("c")
    chips = [(1 - x, y), (x, 1 - y), (1 - x, 1 - y)]
    return x, y, c, chips


_ANY = pl.BlockSpec(memory_space=pl.ANY)
COMM_CHUNKS = 4
COMM_ROW_ALIGN = 16


def _row_chunks(rows, k):
    assert rows % (k * COMM_ROW_ALIGN) == 0, (rows, k)
    step = rows // k
    return [(q * step, step) for q in range(k)]


def _ag_weights_ride(packed, n_chunks=COMM_CHUNKS):
    rows, width = packed.shape
    half = rows // 2
    chunks = _row_chunks(half, n_chunks)
    nq = len(chunks)

    def parts(p_ref, out_ref, send_sems, recv_sems):
        x, y, c, chips = _position()
        sibling = (x, y, 1 - c)

        def copy(k, chip, h, q, to, src=None):
            start, size = chunks[q]
            rows_q = pl.ds(h * half + start, size)
            dst = out_ref.at[2 * chip[0] + chip[1], rows_q, :]
            return pltpu.make_async_remote_copy(
                src_ref=dst if src is None else src.at[rows_q, :], dst_ref=dst, send_sem=send_sems.at[k * nq + q],
                recv_sem=recv_sems.at[k * nq + q], device_id=to, device_id_type=MESH_ID)

        mine = [copy(6 + h, (x, y), h, q, sibling, src=p_ref) for h in range(2) for q in range(nq)]
        first = [copy(j, (x, y), c, q, (*chip, c), src=p_ref) for q in range(nq) for j, chip in enumerate(chips)]
        return (x, y, c), chips, sibling, copy, mine, first

    def start(ins, outs, sems):
        _, _, _, _, mine, first = parts(ins[0], outs[0], sems[0], sems[1])
        for cp in first + mine:
            cp.start()

    def wait(ins, outs, sems):
        (x, y, c), chips, sibling, copy, mine, first = parts(ins[0], outs[0], sems[0], sems[1])
        passed = []
        for q in range(nq):
            for j, chip in enumerate(chips):
                copy(j, chip, c, q, (x, y, c)).wait_recv()
                fwd = copy(3 + j, chip, c, q, sibling)
                fwd.start()
                passed.append(fwd)
        for q in range(nq):
            for j, chip in enumerate(chips):
                copy(3 + j, chip, 1 - c, q, (x, y, c)).wait_recv()
        for cp in mine:
            cp.wait_recv()
        for cp in first + passed + mine:
            cp.wait_send()

    return _Ride([packed], [jax.ShapeDtypeStruct((N_CHIPS, rows, width), packed.dtype)],
                 [pltpu.SemaphoreType.DMA((8 * nq,)), pltpu.SemaphoreType.DMA((8 * nq,))], start, wait)


def _join_rides(rides):
    def split(seq, counts):
        out, at = [], 0
        for n in counts:
            out.append(seq[at:at + n])
            at += n
        return out

    n_in = [len(r.inputs) for r in rides]
    n_out = [len(r.out_shapes) for r in rides]
    n_sem = [len(r.scratch) for r in rides]

    def start(ins, outs, sems):
        for r, i, o, s in zip(rides, split(ins, n_in), split(outs, n_out), split(sems, n_sem)):
            r.start(i, o, s)

    def wait(ins, outs, sems):
        for r, i, o, s in zip(rides, split(ins, n_in), split(outs, n_out), split(sems, n_sem)):
            r.wait(i, o, s)

    return _Ride([a for r in rides for a in r.inputs], [a for r in rides for a in r.out_shapes],
                 [a for r in rides for a in r.scratch], start, wait)


def _run_ride(ride, name):
    n_in, n_out = len(ride.inputs), len(ride.out_shapes)

    def body(*refs):
        ins, outs, sems = refs[:n_in], refs[n_in:n_in + n_out], refs[n_in + n_out:]
        ride.start(ins, outs, sems)
        ride.wait(ins, outs, sems)

    return pl.pallas_call(
        body, name=name, in_specs=[_ANY] * n_in, out_specs=(_ANY,) * n_out, out_shape=tuple(ride.out_shapes),
        scratch_shapes=list(ride.scratch), input_output_aliases=dict(ride.in_place))(*ride.inputs)


def _small_allgather_ride(buf):
    rows, width = buf.shape
    chunks = _row_chunks(rows, COMM_CHUNKS)
    nq = len(chunks)

    def parts(b_ref, all_ref, send_sems, recv_sems, local_sem):
        x, y, c, chips = _position()
        me, sibling = (x, y, c), (x, y, 1 - c)

        def copy(k, block, q, to, src=None):
            rows_q = pl.ds(chunks[q][0], chunks[q][1])
            dst = all_ref.at[4 * block[0] + 2 * block[1] + block[2], rows_q, :]
            return pltpu.make_async_remote_copy(
                src_ref=dst if src is None else src.at[rows_q, :], dst_ref=dst, send_sem=send_sems.at[k * nq + q],
                recv_sem=recv_sems.at[k * nq + q], device_id=to, device_id_type=MESH_ID)

        mine = pltpu.make_async_copy(b_ref, all_ref.at[4 * x + 2 * y + c], local_sem)
        first = []
        for q in range(nq):
            first += [copy(1 + j, me, q, (*chip, c), src=b_ref) for j, chip in enumerate(chips)]
            first.append(copy(0, me, q, sibling, src=b_ref))
        return me, sibling, c, chips, copy, mine, first

    def start(ins, outs, sems):
        _, _, _, _, _, mine, first = parts(ins[0], outs[0], *sems)
        mine.start()
        for cp in first:
            cp.start()

    def wait(ins, outs, sems):
        me, sibling, c, chips, copy, mine, first = parts(ins[0], outs[0], *sems)
        passed = []
        for q in range(nq):
            for j, chip in enumerate(chips):
                copy(1 + j, (*chip, c), q, me).wait_recv()
                fwd = copy(4 + j, (*chip, c), q, sibling)
                fwd.start()
                passed.append(fwd)
        for q in range(nq):
            copy(0, sibling, q, me).wait_recv()
            for j, chip in enumerate(chips):
                copy(4 + j, (*chip, 1 - c), q, me).wait_recv()
        for cp in first + passed:
            cp.wait_send()
        mine.wait()

    return _Ride([buf], [jax.ShapeDtypeStruct((N_DEV, rows, width), F32)],
                 [pltpu.SemaphoreType.DMA((7 * nq,)), pltpu.SemaphoreType.DMA((7 * nq,)), pltpu.SemaphoreType.DMA],
                 start, wait)


def _sum_devices(blocks):
    n, rows, width = blocks.shape
    rb = rows // 2 if (rows // 2) % SUBLANES == 0 else rows

    def kern(b_ref, o_ref):
        total = b_ref[0]
        for d in range(1, n):
            total = total + b_ref[d]
        o_ref[...] = total

    return pl.pallas_call(
        kern, name="small_sum", grid=(rows // rb,), in_specs=[pl.BlockSpec((n, rb, width), lambda i: (0, i, 0))],
        out_specs=pl.BlockSpec((rb, width), lambda i: (i, 0)), out_shape=jax.ShapeDtypeStruct((rows, width), F32),
        compiler_params=_cparams("parallel"))(blocks)


def _small_allgather_sum(buf, head_rows, n_chunks=COMM_CHUNKS):
    rows, width = buf.shape
    chunks = _row_chunks(rows, n_chunks)
    nq = len(chunks)

    def body(b_ref, head_ref, sum_ref, all_ref, send_sems, recv_sems, local_sem):
        x, y, c, chips = _position()
        me, sibling = (x, y, c), (x, y, 1 - c)

        def slot(px, py, pc):
            return all_ref.at[4 * px + 2 * py + pc]

        def copy(k, block, q, to, src=None):
            rows_q = pl.ds(chunks[q][0], chunks[q][1])
            dst = slot(*block).at[rows_q, :]
            return pltpu.make_async_remote_copy(
                src_ref=dst if src is None else src.at[rows_q, :], dst_ref=dst, send_sem=send_sems.at[k * nq + q],
                recv_sem=recv_sems.at[k * nq + q], device_id=to, device_id_type=MESH_ID)

        mine = pltpu.make_async_copy(b_ref, slot(*me), local_sem)
        mine.start()
        first = []
        for q in range(nq):
            first += [copy(1 + j, me, q, (*chip, c), src=b_ref) for j, chip in enumerate(chips)]
            first.append(copy(0, me, q, sibling, src=b_ref))
        for cp in first:
            cp.start()
        passed = []
        for q in range(nq):
            for j, chip in enumerate(chips):
                copy(1 + j, (*chip, c), q, me).wait_recv()
                fwd = copy(4 + j, (*chip, c), q, sibling)
                fwd.start()
                passed.append(fwd)
        for q in range(nq):
            copy(0, sibling, q, me).wait_recv()
            for j, chip in enumerate(chips):
                copy(4 + j, (*chip, 1 - c), q, me).wait_recv()
        for cp in first + passed:
            cp.wait_send()
        mine.wait()
        total = all_ref[0]
        for d in range(1, N_DEV):
            total = total + all_ref[d]
        sum_ref[...] = total
        head_ref[...] = all_ref[:, 0:head_rows, :]

    vm = pl.BlockSpec(memory_space=pltpu.VMEM)
    return pl.pallas_call(
        body, name="small_allgather_sum", in_specs=[vm], out_specs=(vm, vm),
        out_shape=(jax.ShapeDtypeStruct((N_DEV, head_rows, width), F32), jax.ShapeDtypeStruct((rows, width), F32)),
        scratch_shapes=[pltpu.VMEM((N_DEV, rows, width), F32), pltpu.SemaphoreType.DMA((7 * nq,)),
                        pltpu.SemaphoreType.DMA((7 * nq,)), pltpu.SemaphoreType.DMA],
        compiler_params=_cparams(),
    )(buf)


def _rs_pair(g):
    n, rows, width = g.shape
    half = rows // 2
    chunks = _row_chunks(half, COMM_CHUNKS)
    nq = len(chunks)

    def body(g_ref, got_ref, send_sems, recv_sems):
        x, y, c, _ = _position()
        swaps = []
        for k in range(n):
            for q, (start, size) in enumerate(chunks):
                swaps.append(pltpu.make_async_remote_copy(
                    src_ref=g_ref.at[k, pl.ds((1 - c) * half + start, size), :], dst_ref=got_ref.at[k, pl.ds(start, size), :],
                    send_sem=send_sems.at[k * nq + q], recv_sem=recv_sems.at[k * nq + q],
                    device_id=(x, y, 1 - c), device_id_type=MESH_ID))
        for cp in swaps:
            cp.start()
        for cp in swaps:
            cp.wait()

    return pl.pallas_call(
        body, name="rs_pair", in_specs=[_ANY], out_specs=_ANY, out_shape=jax.ShapeDtypeStruct((n, half, width), g.dtype),
        scratch_shapes=[pltpu.SemaphoreType.DMA((n * nq,)), pltpu.SemaphoreType.DMA((n * nq,))],
    )(g)


def _rs_chips_ride(part_bf):
    n, rows, width = part_bf.shape
    chunks = _row_chunks(rows, COMM_CHUNKS)
    nq = len(chunks)

    def sends(pb_ref, got_ref, send_sems, recv_sems):
        x, y, c, chips = _position()
        out = []
        for q, (start, size) in enumerate(chunks):
            for j, chip in enumerate(chips):
                out.append(pltpu.make_async_remote_copy(
                    src_ref=pb_ref.at[2 * chip[0] + chip[1], pl.ds(start, size), :], dst_ref=got_ref.at[j, pl.ds(start, size), :],
                    send_sem=send_sems.at[j * nq + q], recv_sem=recv_sems.at[j * nq + q],
                    device_id=(*chip, c), device_id_type=MESH_ID))
        return out

    def start(ins, outs, sems):
        for cp in sends(ins[0], outs[0], sems[0], sems[1]):
            cp.start()

    def wait(ins, outs, sems):
        for cp in sends(ins[0], outs[0], sems[0], sems[1]):
            cp.wait()

    return _Ride([part_bf], [jax.ShapeDtypeStruct((N_CHIPS - 1, rows, width), BF16)],
                 [pltpu.SemaphoreType.DMA((3 * nq,)), pltpu.SemaphoreType.DMA((3 * nq,))], start, wait)


def _rs_join_ride(shard):
    rows, width = shard.shape
    half = rows // 2
    chunks = _row_chunks(half, COMM_CHUNKS)
    nq = len(chunks)

    def swap(ins, outs, sems, q, h):
        x, y, c, _ = _position()
        rows_q = pl.ds(h * half + chunks[q][0], chunks[q][1])
        return pltpu.make_async_remote_copy(
            src_ref=ins[0].at[rows_q, :], dst_ref=outs[0].at[rows_q, :], send_sem=sems[0].at[q],
            recv_sem=sems[1].at[q], device_id=(x, y, 1 - c), device_id_type=MESH_ID)

    def start(ins, outs, sems):
        c = lax.axis_index("c")
        for q in range(nq):
            swap(ins, outs, sems, q, c).start()

    def wait(ins, outs, sems):
        c = lax.axis_index("c")
        for q in range(nq):
            swap(ins, outs, sems, q, 1 - c).wait_recv()
        for q in range(nq):
            swap(ins, outs, sems, q, c).wait_send()

    return _Ride([shard], [jax.ShapeDtypeStruct(shard.shape, shard.dtype)],
                 [pltpu.SemaphoreType.DMA((nq,)), pltpu.SemaphoreType.DMA((nq,))], start, wait, in_place=[(0, 0)])


def _pair_add(g, got, core):
    n, half, width = got.shape
    nb = 2
    rb = half // nb

    def kern(c_ref, a_ref, b_ref, f_ref, h_ref):
        s = a_ref[...] + b_ref[...]
        f_ref[...] = s
        h_ref[...] = s.astype(BF16)

    spec = pl.BlockSpec((1, rb, width), lambda k, i, c_ref: (k, i, 0))
    return pl.pallas_call(
        kern, name="rs_pair_add",
        grid_spec=pltpu.PrefetchScalarGridSpec(
            num_scalar_prefetch=1, grid=(n, nb),
            in_specs=[pl.BlockSpec((1, rb, width), lambda k, i, c_ref: (k, c_ref[0] * nb + i, 0)), spec],
            out_specs=(spec, spec)),
        out_shape=(jax.ShapeDtypeStruct(got.shape, F32), jax.ShapeDtypeStruct(got.shape, BF16)),
        compiler_params=_cparams("parallel", "parallel"))(core, g, got)


def _chip_add(part_f32, got, where):
    _, rows, width = part_f32.shape
    nb = 2
    rb = rows // nb

    def kern(w_ref, a_ref, b_ref, o_ref):
        o_ref[...] = ((a_ref[0] + b_ref[0].astype(F32)) + b_ref[1].astype(F32)) + b_ref[2].astype(F32)

    return pl.pallas_call(
        kern, name="rs_chip_add",
        grid_spec=pltpu.PrefetchScalarGridSpec(
            num_scalar_prefetch=1, grid=(nb,),
            in_specs=[pl.BlockSpec((1, rb, width), lambda i, w_ref: (w_ref[0], i, 0)),
                      pl.BlockSpec((N_CHIPS - 1, rb, width), lambda i, w_ref: (0, i, 0))],
            out_specs=pl.BlockSpec((rb, width), lambda i, w_ref: (w_ref[1] * nb + i, 0))),
        out_shape=jax.ShapeDtypeStruct((2 * rows, width), F32),
        compiler_params=_cparams("parallel"))(where, part_f32, got)


def _adamw(w, g, m, v, name):
    rows, width = w.shape
    rb = rows
    for cand in (512, 256, 128, 64, 32, 16, 8):
        if rows % cand == 0 and cand * width * 4 <= ADAM_BLOCK_BYTES:
            rb = cand
            break
    spec = pl.BlockSpec((rb, width), lambda i: (i, 0))

    def kern(w_ref, g_ref, m_ref, v_ref, d_ref, nm_ref, nv_ref):
        d_ref[...], nm_ref[...], nv_ref[...] = _adamw_update(w_ref[...], g_ref[...], m_ref[...], v_ref[...])

    shp = jax.ShapeDtypeStruct(w.shape, F32)
    return pl.pallas_call(
        kern, name=name, grid=(rows // rb,), in_specs=[spec] * 4, out_specs=(spec, spec, spec),
        out_shape=(shp, shp, shp), compiler_params=_cparams("parallel"))(w, g, m, v)


def _adamw_update(w, g, m, v):
    nm = ADAM_B1 * m + (1.0 - ADAM_B1) * g
    nv = ADAM_B2 * v + (1.0 - ADAM_B2) * (g * g)
    m_hat = nm / (1.0 - ADAM_B1 ** ADAM_STEP)
    v_hat = nv / (1.0 - ADAM_B2 ** ADAM_STEP)
    return -ADAM_LR * (m_hat / (jnp.sqrt(v_hat) + ADAM_EPS) + ADAM_WD * w), nm, nv


def _adamw_small(params):
    n = len(params)

    def kern(*refs):
        ins, outs = refs[:4 * n], refs[4 * n:]
        for p in range(n):
            w_ref, g_ref, m_ref, v_ref = ins[4 * p:4 * p + 4]
            d, nm, nv = _adamw_update(w_ref[...], g_ref[...], m_ref[...], v_ref[...])
            outs[3 * p][...] = d
            outs[3 * p + 1][...] = nm
            outs[3 * p + 2][...] = nv

    flat = [a for group in params for a in group]
    shapes = [jax.ShapeDtypeStruct(group[0].shape, F32) for group in params for _ in range(3)]
    res = pl.pallas_call(kern, name="adamw_small", out_shape=tuple(shapes), compiler_params=_cparams())(*flat)
    return [tuple(res[3 * p:3 * p + 3]) for p in range(n)]


def _wada_grad(silu_t, dmod_cols):
    n = dmod_cols.shape[1]

    def kern(s_ref, d_ref, o_ref):
        acc = s_ref[:, 0:1] * d_ref[0:1, :]
        for b in range(1, N_DEV):
            acc = acc + s_ref[:, b:b + 1] * d_ref[b:b + 1, :]
        o_ref[...] = acc

    return pl.pallas_call(kern, name="wada_grad", out_shape=jax.ShapeDtypeStruct((D_MODEL, n), F32),
                          compiler_params=_cparams())(silu_t, dmod_cols)


def _rows(a, multiple):
    flat = a.reshape(-1)
    pad = (-flat.shape[0]) % (D_MODEL * multiple)
    if pad:
        flat = jnp.concatenate([flat, jnp.zeros((pad,), flat.dtype)])
    return flat.reshape(-1, D_MODEL)


def _part_rows(shape, multiple):
    return -(-int(np.prod(shape)) // (D_MODEL * multiple)) * multiple


def _pack_rows(parts, multiple, total_multiple=1):
    blocks = [_rows(p, multiple) for p in parts]
    pad = (-sum(b.shape[0] for b in blocks)) % total_multiple
    if pad:
        blocks.append(jnp.zeros((pad, D_MODEL), blocks[0].dtype))
    return jnp.concatenate(blocks, axis=0)


def _unpack_rows(buf, shapes, multiple):
    out, r = [], 0
    for shp in shapes:
        n = int(np.prod(shp))
        nr = _part_rows(shp, multiple)
        out.append(buf[r:r + nr].reshape(-1)[:n].reshape(shp))
        r += nr
    return out


def kernel(x, c, w_ada, b_ada, norm_pre, norm_post, w_in, pool_w, pool_scale, ssm_a_re, ssm_a_im, ssm_log_dt, ssm_b_re, ssm_b_im, ssm_c_re, ssm_c_im, ssm_d, glu_w, glu_b, w_branch_pool, w_branch_ssm, w_out, loss_target, m_w_ada, m_b_ada, m_norm_pre, m_norm_post, m_w_in, m_pool_w, m_pool_scale, m_ssm_a_re, m_ssm_a_im, m_ssm_log_dt, m_ssm_b_re, m_ssm_b_im, m_ssm_c_re, m_ssm_c_im, m_ssm_d, m_glu_w, m_glu_b, m_w_branch_pool, m_w_branch_ssm, m_w_out, v_w_ada, v_b_ada, v_norm_pre, v_norm_post, v_w_in, v_pool_w, v_pool_scale, v_ssm_a_re, v_ssm_a_im, v_ssm_log_dt, v_ssm_b_re, v_ssm_b_im, v_ssm_c_re, v_ssm_c_im, v_ssm_d, v_glu_w, v_glu_b, v_w_branch_pool, v_w_branch_ssm, v_w_out):
    n_ada = w_ada.shape[2]
    n_in = w_in.shape[2]
    n_row = glu_w.shape[1]
    n_pool = pool_w.shape[2]
    n_groups = pool_w.shape[1]

    (g_ada,) = _run_ride(_ag_weights_ride(w_ada[0].astype(BF16)), "ag_weights")
    w_ada_bf = g_ada.transpose(1, 0, 2).reshape(D_MODEL, N_CHIPS * n_ada)
    w_in_own = w_in[0].astype(BF16)
    w_in_ride = _ag_weights_ride(w_in_own)

    def unpack_w_in(g_in):
        return g_in.transpose(1, 0, 2).reshape(D_MODEL, N_CHIPS * n_in)
    pool_rows = n_groups * n_pool * POOL_GW // D_MODEL
    late_shards = [pool_w[0].reshape(n_groups * n_pool, POOL_GW), glu_w[0], w_branch_pool[0], w_branch_ssm[0], w_out[0]]
    late_ride = _join_rides([_ag_weights_ride(s.astype(BF16), n_chunks=2) for s in late_shards])

    def unpack_late(pool, *squares):
        pool = pool.reshape(N_CHIPS, n_groups, n_pool, POOL_GW).transpose(1, 0, 2, 3)
        return (pool.reshape(n_groups, POOL_GW, POOL_GW), *[s.reshape(D_MODEL, D_MODEL) for s in squares])

    chip = 2 * lax.axis_index("x") + lax.axis_index("y")
    core = lax.axis_index("c").astype(jnp.int32)
    kept = {}

    def by_cols(a, n):
        return a.reshape(D_MODEL, N_CHIPS, n).transpose(1, 0, 2).reshape(N_CHIPS, -1, D_MODEL)

    def by_rows(a):
        return a.reshape(N_CHIPS, n_row, D_MODEL)

    def exchange_big(g):
        pool_by_chip = g["d_pool_w"].reshape(n_groups, N_CHIPS, n_pool, POOL_GW).transpose(1, 0, 2, 3)
        blocks = [by_cols(g["d_win"], n_in), by_rows(g["d_glu_w"]), by_rows(g["d_wbp"]), by_rows(g["d_wbs"]),
                  by_rows(g["d_wout"]), pool_by_chip.reshape(N_CHIPS, pool_rows, D_MODEL)]
        pad = (-sum(b.shape[1] for b in blocks)) % (2 * COMM_CHUNKS * COMM_ROW_ALIGN)
        if pad:
            blocks.append(jnp.zeros((N_CHIPS, pad, D_MODEL), F32))
        g_packed = jnp.concatenate(blocks, axis=1)
        kept["part_f32"], part_bf = _pair_add(g_packed, _rs_pair(g_packed), core.reshape(1))
        return _rs_chips_ride(part_bf)

    def join_big(rode):
        (got_chips,) = rode
        return _rs_join_ride(_chip_add(kept["part_f32"], got_chips, jnp.stack([chip.astype(jnp.int32), core])))

    a_re, a_im, log_dt = ssm_a_re[0], ssm_a_im[0], ssm_log_dt[0].reshape(SSM_G, 1)
    b_re_t, b_im_t = ssm_b_re[0].transpose(2, 0, 1), ssm_b_im[0].transpose(2, 0, 1)
    early_names = ["dg2", "d_pscale", "d_glu_b", "d_dskip", "d_abar_re", "d_abar_im", "d_bb_re_t", "d_bb_im_t",
                   "d_c_re", "d_c_im"]

    def exchange_small(s):
        parts = [s[k] for k in early_names]
        kept["early_shapes"] = [p.shape for p in parts]
        return _small_allgather_ride(_pack_rows(parts, SUBLANES, COMM_CHUNKS * COMM_ROW_ALIGN))

    res = _local_step(x[0], c, loss_target[0], w_ada_bf, b_ada, norm_pre, norm_post, None, None, pool_scale,
                      a_re, a_im, log_dt, b_re_t, b_im_t, ssm_c_re[0], ssm_c_im[0], ssm_d[0], None, glu_b[0:1],
                      None, None, None,
                      split_proj=(w_in_own, chip.astype(jnp.int32).reshape(1), w_in_ride, unpack_w_in, late_ride, unpack_late),
                      ride_for_dw_in=exchange_small, ride_for_dh=exchange_big, ride_for_in_bwd=join_big)

    (all_early,) = res["rode_dw_in"]
    (g_norm_post, g_pscale, g_glu_b, g_dskip, s_abar_re, s_abar_im, s_bb_re, s_bb_im, g_c_re, g_c_im) = _unpack_rows(
        _sum_devices(all_early), kept["early_shapes"], SUBLANES)
    g_a_re, g_a_im, g_log_dt, g_b_re_t, g_b_im_t = _ssm_params_bwd(
        a_re, a_im, log_dt, b_re_t, b_im_t, s_abar_re.reshape(SSM_G, SSM_P), s_abar_im.reshape(SSM_G, SSM_P),
        s_bb_re, s_bb_im)
    late_parts = [res["dmod"], res["silu_c"], res["dg1"], res["loss"].reshape(1, 1)]
    late_shapes = [p.shape for p in late_parts]
    head_rows = _part_rows(late_shapes[0], SUBLANES) + _part_rows(late_shapes[1], SUBLANES)
    all_late, sum_late = _small_allgather_sum(_pack_rows(late_parts, SUBLANES, COMM_ROW_ALIGN), head_rows, n_chunks=1)
    g_b_ada, _, g_norm_pre, loss = _unpack_rows(sum_late, late_shapes, SUBLANES)
    loss = loss[0, 0]
    dmod_all = all_late[:, 0:3].reshape(N_DEV, 3 * D_MODEL)
    dmod_cols = lax.dynamic_slice_in_dim(dmod_all, chip * n_ada, n_ada, axis=1)
    silu_t = all_late[:, _part_rows(late_shapes[0], SUBLANES)].transpose(1, 0)
    g_w_ada = _wada_grad(silu_t, dmod_cols)

    (shard,) = res["rode_in_bwd"]
    r = 0
    g_w_in = shard[r:r + n_in].reshape(D_MODEL, n_in)
    r += n_in
    g_squares = []
    for _ in range(4):
        g_squares.append(shard[r:r + n_row])
        r += n_row
    g_glu_w, g_wbp, g_wbs, g_wout = g_squares
    g_pool_w = shard[r:r + pool_rows].reshape(n_groups * n_pool, POOL_GW)

    big = [("w_ada", w_ada[0], g_w_ada, m_w_ada[0], v_w_ada[0]),
           ("w_in", w_in[0], g_w_in, m_w_in[0], v_w_in[0]),
           ("pool_w", pool_w[0].reshape(n_groups * n_pool, POOL_GW), g_pool_w,
            m_pool_w[0].reshape(n_groups * n_pool, POOL_GW), v_pool_w[0].reshape(n_groups * n_pool, POOL_GW)),
           ("glu_w", glu_w[0], g_glu_w, m_glu_w[0], v_glu_w[0]),
           ("w_branch_pool", w_branch_pool[0], g_wbp, m_w_branch_pool[0], v_w_branch_pool[0]),
           ("w_branch_ssm", w_branch_ssm[0], g_wbs, m_w_branch_ssm[0], v_w_branch_ssm[0]),
           ("w_out", w_out[0], g_wout, m_w_out[0], v_w_out[0])]
    out = {}
    for name, w_, g_, m_, v_ in big:
        d_, nm_, nv_ = _adamw(w_, g_, m_, v_, "adamw_" + name)
        out[name] = (g_, d_, nm_, nv_)

    g_b_re = g_b_re_t.transpose(1, 2, 0)
    g_b_im = g_b_im_t.transpose(1, 2, 0)
    small = [("b_ada", b_ada, g_b_ada, m_b_ada, v_b_ada),
             ("norm_pre", norm_pre, g_norm_pre, m_norm_pre, v_norm_pre),
             ("norm_post", norm_post, g_norm_post, m_norm_post, v_norm_post),
             ("pool_scale", pool_scale, g_pscale, m_pool_scale, v_pool_scale),
             ("ssm_a_re", ssm_a_re, g_a_re, m_ssm_a_re, v_ssm_a_re),
             ("ssm_a_im", ssm_a_im, g_a_im, m_ssm_a_im, v_ssm_a_im),
             ("ssm_log_dt", ssm_log_dt, g_log_dt, m_ssm_log_dt, v_ssm_log_dt),
             ("ssm_b_re", ssm_b_re, g_b_re, m_ssm_b_re, v_ssm_b_re),
             ("ssm_b_im", ssm_b_im, g_b_im, m_ssm_b_im, v_ssm_b_im),
             ("ssm_c_re", ssm_c_re, g_c_re, m_ssm_c_re, v_ssm_c_re),
             ("ssm_c_im", ssm_c_im, g_c_im, m_ssm_c_im, v_ssm_c_im),
             ("ssm_d", ssm_d, g_dskip, m_ssm_d, v_ssm_d),
             ("glu_b", glu_b, g_glu_b, m_glu_b, v_glu_b)]
    small = [(name, w_, g_.reshape(w_.shape), m_, v_) for name, w_, g_, m_, v_ in small]
    updates = _adamw_small([t[1:] for t in small])
    for (name, _, g_, _, _), (d_, nm_, nv_) in zip(small, updates):
        out[name] = (g_, d_, nm_, nv_)

    order = ["w_ada", "b_ada", "norm_pre", "norm_post", "w_in", "pool_w", "pool_scale", "ssm_a_re", "ssm_a_im",
             "ssm_log_dt", "ssm_b_re", "ssm_b_im", "ssm_c_re", "ssm_c_im", "ssm_d", "glu_w", "glu_b", "w_branch_pool",
             "w_branch_ssm", "w_out"]
    ref_shape = dict(w_ada=w_ada.shape, w_in=w_in.shape, pool_w=pool_w.shape, glu_w=glu_w.shape,
                     w_branch_pool=w_branch_pool.shape, w_branch_ssm=w_branch_ssm.shape, w_out=w_out.shape)
    for name, w_, _, _, _ in small:
        ref_shape[name] = w_.shape
    results = [loss, res["grad_x"][None]]
    for k in range(4):
        results += [out[name][k].reshape(ref_shape[name]) for name in order]
    return tuple(results)
```

```python
import functools
import math

import numpy as np
import jax
import jax.numpy as jnp
from jax import lax
from jax.experimental import pallas as pl
from jax.experimental.pallas import tpu as pltpu

F32 = jnp.float32
BF16 = jnp.bfloat16
MESH_ID = pl.DeviceIdType.MESH

D_MODEL = 1024
LANES = 128
SUBLANES = 8
SSM_G, SSM_P, SSM_H = 64, 64, 16
LANE_BLOCKS = D_MODEL // LANES
GROUPS_PER_BLOCK = LANES // SSM_H
STATE_W = GROUPS_PER_BLOCK * SSM_P
STATE_ALL = SSM_G * SSM_P
POOL_WINDOWS = (2, 4, 8, 16)
POOL_GW = D_MODEL // len(POOL_WINDOWS)
HALO = 16
RMS_EPS = 1e-6
N_CHIPS = 4
N_DEV = 8

SCAN_CHUNK = 1024
SCAN_BLOCKS = 1
ROW_CHUNK = 512
ROW_CHUNK_WIDE = 512
PROJ_ROWS = 1024
VMEM_LIMIT_BYTES = 56 * 1024 * 1024

ADAM_BLOCK_BYTES = 1 << 20
ADAM_LR, ADAM_B1, ADAM_B2, ADAM_EPS, ADAM_WD, ADAM_STEP = 0.001, 0.9, 0.999, 1e-08, 0.01, 10

_GELU_C0 = math.sqrt(2.0 / math.pi)
_GELU_C1 = 0.044715


def _cparams(*sem):
    if sem:
        return pltpu.CompilerParams(dimension_semantics=sem, vmem_limit_bytes=VMEM_LIMIT_BYTES)
    return pltpu.CompilerParams(vmem_limit_bytes=VMEM_LIMIT_BYTES)


def _sigmoid(v):
    return jax.nn.sigmoid(v)


def _silu(v):
    return v * _sigmoid(v)


def _dsilu(v):
    s = _sigmoid(v)
    return s * (1.0 + v * (1.0 - s))


def _gelu(v):
    return v * (0.5 * (1.0 + jnp.tanh(_GELU_C0 * v * (1.0 + _GELU_C1 * (v * v)))))


def _gelu_and_grad(v):
    v2 = v * v
    t = jnp.tanh(_GELU_C0 * v * (1.0 + _GELU_C1 * v2))
    half = 0.5 * (1.0 + t)
    grad = half + (0.5 * _GELU_C0) * v * (1.0 - t * t) * (1.0 + (3.0 * _GELU_C1) * v2)
    return v * half, grad


def _silu_and_grad(v):
    s = _sigmoid(v)
    return v * s, s * (1.0 + v * (1.0 - s))


def _dot(a, b):
    return lax.dot_general(a, b, (((1,), (0,)), ((), ())), preferred_element_type=F32)


def _dot_nt(a, b):
    return lax.dot_general(a, b, (((1,), (1,)), ((), ())), preferred_element_type=F32)


def _dot_tn(a, b):
    return lax.dot_general(a, b, (((0,), (0,)), ((), ())), preferred_element_type=F32)


def _acc8(v):
    return v.reshape(v.shape[0] // SUBLANES, SUBLANES, v.shape[1]).sum(axis=0)


class _Ride:
    def __init__(self, inputs, out_shapes, scratch, start, wait):
        self.inputs, self.out_shapes, self.scratch, self.start, self.wait = inputs, out_shapes, scratch, start, wait


def _mm(a_parts, b_parts, *, name, ta=False, tb=False, out_dtype=F32, bm=512, bn=512, bk=512, ride=None):
    a_parts, b_parts = list(a_parts), list(b_parts)
    if ta:
        assert len(a_parts) == 1
        k_dim, m_dim = a_parts[0].shape
    else:
        m_dim = a_parts[0].shape[0]
        k_dim = sum(a.shape[1] for a in a_parts)
    if tb:
        assert len(b_parts) == 1
        n_dim = b_parts[0].shape[0]
    else:
        n_dim = sum(b.shape[1] for b in b_parts)
    bm, bn, bk = min(bm, m_dim), min(bn, n_dim), min(bk, k_dim)
    nm, nn, nk = m_dim // bm, n_dim // bn, k_dim // bk
    a_ranges, off = [], 0
    for a in a_parts:
        cnt = (a.shape[0] if ta else a.shape[1]) // bk
        a_ranges.append((off, cnt))
        off += cnt
    b_ranges, off = [], 0
    for b in b_parts:
        cnt = (b.shape[0] if tb else b.shape[1]) // bn
        b_ranges.append((off, cnt))
        off += cnt

    def a_spec(off, cnt):
        if ta:
            return pl.BlockSpec((bk, bm), lambda i, n, k: (k, i))
        return pl.BlockSpec((bm, bk), lambda i, n, k: (i, jnp.clip(k - off, 0, cnt - 1)))

    def b_spec(off, cnt):
        if tb:
            return pl.BlockSpec((bn, bk), lambda i, n, k: (n, k))
        return pl.BlockSpec((bk, bn), lambda i, n, k: (k, jnp.clip(n - off, 0, cnt - 1)))

    na, nb = len(a_parts), len(b_parts)
    dims = (((0 if ta else 1,), (1 if tb else 0,)), ((), ()))

    def kern_single(a_ref, b_ref, o_ref):
        o_ref[...] = lax.dot_general(a_ref[...].astype(BF16), b_ref[...].astype(BF16), dims,
                                     preferred_element_type=F32).astype(out_dtype)

    if na == 1 and nb == 1 and nk == 1 and not ride:
        return pl.pallas_call(
            kern_single, name=name, grid=(nm, nn),
            in_specs=[pl.BlockSpec((bk, bm), lambda i, n: (0, i)) if ta else pl.BlockSpec((bm, bk), lambda i, n: (i, 0)),
                      pl.BlockSpec((bn, bk), lambda i, n: (n, 0)) if tb else pl.BlockSpec((bk, bn), lambda i, n: (0, n))],
            out_specs=pl.BlockSpec((bm, bn), lambda i, n: (i, n)),
            out_shape=jax.ShapeDtypeStruct((m_dim, n_dim), out_dtype),
            compiler_params=_cparams("parallel", "parallel"),
        )(a_parts[0], b_parts[0])

    n_rin = len(ride.inputs) if ride else 0
    n_rout = len(ride.out_shapes) if ride else 0

    def kern(*refs):
        a_refs, b_refs = refs[:na], refs[na:na + nb]
        rin = refs[na + nb:na + nb + n_rin]
        o_ref = refs[na + nb + n_rin]
        rout = refs[na + nb + n_rin + 1:na + nb + n_rin + 1 + n_rout]
        acc = refs[na + nb + n_rin + 1 + n_rout]
        rsem = refs[na + nb + n_rin + 2 + n_rout:]
        i, n, k = pl.program_id(0), pl.program_id(1), pl.program_id(2)

        if ride:
            @pl.when((i == 0) & (n == 0) & (k == 0))
            def _():
                ride.start(rin, rout, rsem)

        if nk > 1:
            @pl.when(k == 0)
            def _():
                acc[...] = jnp.zeros_like(acc)

        for ja, (koff, kcnt) in enumerate(a_ranges):
            for jb, (noff, ncnt) in enumerate(b_ranges):
                def step(ja=ja, jb=jb):
                    a = a_refs[ja][...].astype(BF16)
                    b = b_refs[jb][...].astype(BF16)
                    prod = lax.dot_general(a, b, dims, preferred_element_type=F32)
                    if nk > 1:
                        acc[...] += prod
                    else:
                        o_ref[...] = prod.astype(out_dtype)

                if na == 1 and nb == 1:
                    step()
                else:
                    cond = (k >= koff) & (k < koff + kcnt) & (n >= noff) & (n < noff + ncnt)
                    pl.when(cond)(step)

        if nk > 1:
            @pl.when(k == nk - 1)
            def _():
                o_ref[...] = acc[...].astype(out_dtype)

        if ride:
            @pl.when((i == nm - 1) & (n == nn - 1) & (k == nk - 1))
            def _():
                ride.wait(rin, rout, rsem)

    any_spec = pl.BlockSpec(memory_space=pl.ANY)
    out_spec = pl.BlockSpec((bm, bn), lambda i, n, k: (i, n))
    out_shape = jax.ShapeDtypeStruct((m_dim, n_dim), out_dtype)
    acc_shape = pltpu.VMEM((bm, bn) if nk > 1 else (SUBLANES, LANES), F32)
    if not ride:
        return pl.pallas_call(
            kern, name=name, grid=(nm, nn, nk),
            in_specs=[a_spec(*r) for r in a_ranges] + [b_spec(*r) for r in b_ranges],
            out_specs=out_spec, out_shape=out_shape, scratch_shapes=[acc_shape],
            compiler_params=_cparams("parallel", "parallel", "arbitrary"),
        )(*a_parts, *b_parts)
    return pl.pallas_call(
        kern, name=name, grid=(nm, nn, nk),
        in_specs=[a_spec(*r) for r in a_ranges] + [b_spec(*r) for r in b_ranges] + [any_spec] * n_rin,
        out_specs=(out_spec,) + (any_spec,) * n_rout, out_shape=(out_shape,) + tuple(ride.out_shapes),
        scratch_shapes=[acc_shape] + list(ride.scratch),
        compiler_params=_cparams("arbitrary", "arbitrary", "arbitrary"),
    )(*a_parts, *b_parts, *ride.inputs)


def _ssm_param_fn(a_re, a_im, log_dt, b_re, b_im):
    dt = jnp.exp(log_dt)
    lam_re = jnp.minimum(a_re, -1e-4)
    lam_im = a_im
    mag = jnp.exp(lam_re * dt)
    abar_re = mag * jnp.cos(lam_im * dt)
    abar_im = mag * jnp.sin(lam_im * dt)
    den = lam_re * lam_re + lam_im * lam_im
    num_re = abar_re - 1.0
    f_re = (num_re * lam_re + abar_im * lam_im) / den
    f_im = (abar_im * lam_re - num_re * lam_im) / den
    bb_re = f_re * b_re - f_im * b_im
    bb_im = f_re * b_im + f_im * b_re
    return abar_re, abar_im, bb_re, bb_im


def _ssm_params(a_re, a_im, log_dt, b_re_t, b_im_t):
    def kern(are, aim, ldt, bre, bim, o_ar, o_ai, o_br, o_bi):
        ar, ai, br, bi = _ssm_param_fn(are[...], aim[...], ldt[...], bre[...], bim[...])
        o_ar[...] = ar
        o_ai[...] = ai
        o_br[...] = br
        o_bi[...] = bi

    gp = jax.ShapeDtypeStruct((SSM_G, SSM_P), F32)
    hgp = jax.ShapeDtypeStruct((SSM_H, SSM_G, SSM_P), F32)
    return pl.pallas_call(kern, name="ssm_params", out_shape=(gp, gp, hgp, hgp), compiler_params=_cparams())(
        a_re, a_im, log_dt, b_re_t, b_im_t)


def _ssm_params_bwd(a_re, a_im, log_dt, b_re_t, b_im_t, d_ar, d_ai, d_bbr, d_bbi):
    def kern(are, aim, ldt, bre, bim, dar, dai, dbr, dbi, o_are, o_aim, o_ldt, o_bre, o_bim):
        prim = (are[...], aim[...], ldt[...], bre[...], bim[...])
        _, vjp = jax.vjp(_ssm_param_fn, *prim)
        g = vjp((dar[...], dai[...], dbr[...], dbi[...]))
        o_are[...] = g[0]
        o_aim[...] = g[1]
        o_ldt[...] = g[2]
        o_bre[...] = g[3]
        o_bim[...] = g[4]

    gp = jax.ShapeDtypeStruct((SSM_G, SSM_P), F32)
    g1 = jax.ShapeDtypeStruct((SSM_G, 1), F32)
    hgp = jax.ShapeDtypeStruct((SSM_H, SSM_G, SSM_P), F32)
    return pl.pallas_call(kern, name="ssm_params_bwd", out_shape=(gp, gp, g1, hgp, hgp), compiler_params=_cparams())(
        a_re, a_im, log_dt, b_re_t, b_im_t, d_ar, d_ai, d_bbr, d_bbi)


def _pow_tables(abar_re, abar_im, tc):
    ls = tc // SUBLANES

    def kern(ar_ref, ai_ref, fr_ref, fi_ref, rr_ref, ri_ref):
        a_re = jnp.broadcast_to(ar_ref[...], (SUBLANES, STATE_W))
        a_im = jnp.broadcast_to(ai_ref[...], (SUBLANES, STATE_W))
        p_re, p_im = a_re, a_im
        for i in range(ls):
            fwd = pl.ds(SUBLANES * i, SUBLANES)
            rev = pl.ds(SUBLANES * (ls - 1 - i), SUBLANES)
            fr_ref[fwd, :] = p_re
            fi_ref[fwd, :] = p_im
            rr_ref[rev, :] = p_re
            ri_ref[rev, :] = p_im
            p_re, p_im = p_re * a_re - p_im * a_im, p_re * a_im + p_im * a_re

    vec = pl.BlockSpec((1, STATE_W), lambda b: (0, b))
    tab = pl.BlockSpec((tc, STATE_W), lambda b: (0, b))
    shp = jax.ShapeDtypeStruct((tc, STATE_ALL), F32)
    return pl.pallas_call(
        kern, name="pow_tables", grid=(LANE_BLOCKS,), in_specs=[vec, vec], out_specs=(tab, tab, tab, tab),
        out_shape=(shp, shp, shp, shp), compiler_params=_cparams("parallel"))(abar_re, abar_im)


def _mod_kernel(c_row, w_ada_bf, b_ada):
    def kern(c_ref, w_ref, b_ref, m_ref, s_ref):
        cv = c_ref[...]
        sc = _silu(cv)
        s_ref[...] = sc
        lhs = jnp.broadcast_to(sc, (SUBLANES, D_MODEL)).astype(BF16)
        m_ref[...] = _dot(lhs, w_ref[...]) + b_ref[...]

    return pl.pallas_call(
        kern, name="ada_mod",
        out_shape=(jax.ShapeDtypeStruct((SUBLANES, 3 * D_MODEL), F32), jax.ShapeDtypeStruct((1, D_MODEL), F32)),
        compiler_params=_cparams())(c_row, w_ada_bf, b_ada)


def _row_spec(tr, width=D_MODEL, col=0):
    return pl.BlockSpec((tr, width), lambda c: (c, col))


def _vec_spec(width=D_MODEL):
    return pl.BlockSpec((1, width), lambda c: (0, 0))


def _col_spec(tr):
    return pl.BlockSpec((D_MODEL, tr), lambda c: (0, c))


def _in_norm(x, g1, scale, shift):
    seq = x.shape[0]
    tr = min(ROW_CHUNK_WIDE, seq)

    def kern(x_ref, g_ref, sc_ref, sh_ref, h_ref, ht_ref):
        xv = x_ref[...]
        r = lax.rsqrt(jnp.mean(xv * xv, axis=-1, keepdims=True) + RMS_EPS)
        h = ((xv * r) * g_ref[...]) * (1.0 + sc_ref[...]) + sh_ref[...]
        h_ref[...] = h.astype(BF16)
        ht_ref[...] = h.T.astype(BF16)

    return pl.pallas_call(
        kern, name="in_norm", grid=(seq // tr,),
        in_specs=[_row_spec(tr), _vec_spec(), _vec_spec(), _vec_spec()], out_specs=(_row_spec(tr), _col_spec(tr)),
        out_shape=(jax.ShapeDtypeStruct((seq, D_MODEL), BF16), jax.ShapeDtypeStruct((D_MODEL, seq), BF16)),
        compiler_params=_cparams("parallel"))(x, g1, scale, shift)


PAD = SUBLANES


def _window_sums(src, cols, w, bufs, rows, ahead):
    cur, cur_cols, step, k = src, cols, 1, 0
    data = pl.ds(PAD, rows)
    while step < w:
        dst = bufs[k % 2]
        dst[data, :] = cur[data, cur_cols] + cur[pl.ds(PAD + (step if ahead else -step), rows), cur_cols]
        cur, cur_cols, step, k = dst, slice(None), 2 * step, k + 1
    return cur, cur_cols


def _in_norm_proj_own(x, g1, scale, shift, w_own, chip, ride):
    seq, n_own = x.shape[0], w_own.shape[1]
    tr = min(PROJ_ROWS, seq)
    nc = seq // tr
    n_rin, n_rout = len(ride.inputs), len(ride.out_shapes)

    def kern(chip_ref, x_ref, g_ref, sc_ref, sh_ref, w_ref, *rest):
        rin, (h_ref, ht_ref, p_ref) = rest[:n_rin], rest[n_rin:n_rin + 3]
        rout, rsem = rest[n_rin + 3:n_rin + 3 + n_rout], rest[n_rin + 3 + n_rout:]
        c = pl.program_id(0)

        @pl.when(c == 0)
        def _():
            ride.start(rin, rout, rsem)

        xv = x_ref[...]
        r = lax.rsqrt(jnp.mean(xv * xv, axis=-1, keepdims=True) + RMS_EPS)
        h = ((xv * r) * g_ref[...]) * (1.0 + sc_ref[...]) + sh_ref[...]
        hb = h.astype(BF16)
        h_ref[...] = hb
        ht_ref[...] = h.T.astype(BF16)
        p_ref[...] = _dot(hb, w_ref[...]).astype(BF16)

        @pl.when(c == nc - 1)
        def _():
            ride.wait(rin, rout, rsem)

    vec = pl.BlockSpec((1, D_MODEL), lambda c, k: (0, 0))
    return pl.pallas_call(
        kern, name="in_norm_proj_own",
        grid_spec=pltpu.PrefetchScalarGridSpec(
            num_scalar_prefetch=1, grid=(nc,),
            in_specs=[pl.BlockSpec((tr, D_MODEL), lambda c, k: (c, 0)), vec, vec, vec,
                      pl.BlockSpec((D_MODEL, n_own), lambda c, k: (0, 0))] + [_ANY] * n_rin,
            out_specs=(pl.BlockSpec((tr, D_MODEL), lambda c, k: (c, 0)), pl.BlockSpec((D_MODEL, tr), lambda c, k: (0, c)),
                       pl.BlockSpec((tr, n_own), lambda c, k: (c, k[0]))) + (_ANY,) * n_rout,
            scratch_shapes=list(ride.scratch)),
        out_shape=(jax.ShapeDtypeStruct((seq, D_MODEL), BF16), jax.ShapeDtypeStruct((D_MODEL, seq), BF16),
                   jax.ShapeDtypeStruct((seq, N_CHIPS * n_own), BF16)) + tuple(ride.out_shapes),
        compiler_params=_cparams("arbitrary"))(chip, x, g1, scale, shift, w_own, *ride.inputs)


def _proj_rest(h, w_blocks, proj, chip, ride):
    seq, n_own = h.shape[0], w_blocks.shape[2]
    tr = min(PROJ_ROWS, seq)
    nm, nn = seq // tr, N_CHIPS - 1
    n_rin, n_rout = len(ride.inputs), len(ride.out_shapes)

    def kern(chip_ref, h_ref, w_ref, _, *rest):
        rin, p_ref = rest[:n_rin], rest[n_rin]
        rout, rsem = rest[n_rin + 1:n_rin + 1 + n_rout], rest[n_rin + 1 + n_rout:]
        i, n = pl.program_id(0), pl.program_id(1)

        @pl.when((i == 0) & (n == 0))
        def _():
            ride.start(rin, rout, rsem)

        p_ref[...] = _dot(h_ref[...], w_ref[0]).astype(BF16)

        @pl.when((i == nm - 1) & (n == nn - 1))
        def _():
            ride.wait(rin, rout, rsem)

    def other(n, k):
        return (k[0] + 1 + n) % N_CHIPS

    return pl.pallas_call(
        kern, name="proj_rest",
        grid_spec=pltpu.PrefetchScalarGridSpec(
            num_scalar_prefetch=1, grid=(nm, nn),
            in_specs=[pl.BlockSpec((tr, D_MODEL), lambda i, n, k: (i, 0)),
                      pl.BlockSpec((1, D_MODEL, n_own), lambda i, n, k: (other(n, k), 0, 0)), _ANY] + [_ANY] * n_rin,
            out_specs=(pl.BlockSpec((tr, n_own), lambda i, n, k: (i, other(n, k))),) + (_ANY,) * n_rout,
            scratch_shapes=list(ride.scratch)),
        out_shape=(jax.ShapeDtypeStruct(proj.shape, BF16),) + tuple(ride.out_shapes),
        input_output_aliases={3: 0},
        compiler_params=_cparams("arbitrary", "arbitrary"))(chip, h, w_blocks, proj, *ride.inputs)


def _pool_windows(ext, bufs, pos, g, w, tr):
    cols = pl.ds(g * POOL_GW, POOL_GW)
    chunk = pl.ds(PAD + HALO, tr)
    cur = ext[chunk, cols]
    win, win_cols = _window_sums(ext, cols, w, bufs, HALO + tr, ahead=False)
    cnt = jnp.minimum(pos + 1, w).astype(F32)
    return win[chunk, win_cols] / cnt - cur


def _zero_pads(refs, rows):
    for ref in refs:
        ref[0:PAD, :] = jnp.zeros((PAD, ref.shape[1]), F32)
        ref[PAD + rows:, :] = jnp.zeros((PAD, ref.shape[1]), F32)


def _pool_fwd(proj, pool_w_bf, pscale):
    seq = proj.shape[0]
    tr = min(ROW_CHUNK_WIDE, seq)
    hb = tr // HALO

    def kern(up_ref, halo_ref, zp_ref, pw_ref, ps_ref, y_ref, yt_ref, ext, buf_a, buf_b):
        c = pl.program_id(0)
        _zero_pads((ext, buf_a, buf_b), HALO + tr)
        ext[pl.ds(PAD, HALO), :] = jnp.where(c > 0, halo_ref[...].astype(F32), 0.0)
        ext[pl.ds(PAD + HALO, tr), :] = up_ref[...].astype(F32)
        pos = c * tr + lax.broadcasted_iota(jnp.int32, (tr, POOL_GW), 0)
        for g, w in enumerate(POOL_WINDOWS):
            cols = pl.ds(g * POOL_GW, POOL_GW)
            pooled = _pool_windows(ext, (buf_a, buf_b), pos, g, w, tr)
            mixed = _dot(pooled.astype(BF16), pw_ref[g])
            y = mixed * ps_ref[:, cols] * _silu(zp_ref[:, cols].astype(F32))
            y_ref[:, cols] = y.astype(BF16)
            yt_ref[cols, :] = y.T.astype(BF16)

    return pl.pallas_call(
        kern, name="pool_fwd", grid=(seq // tr,),
        in_specs=[_row_spec(tr, col=0),
                  pl.BlockSpec((HALO, D_MODEL), lambda c: (jnp.maximum(c * hb - 1, 0), 0)),
                  _row_spec(tr, col=1),
                  pl.BlockSpec((len(POOL_WINDOWS), POOL_GW, POOL_GW), lambda c: (0, 0, 0)),
                  _vec_spec()],
        out_specs=(_row_spec(tr), _col_spec(tr)),
        out_shape=(jax.ShapeDtypeStruct((seq, D_MODEL), BF16), jax.ShapeDtypeStruct((D_MODEL, seq), BF16)),
        scratch_shapes=[pltpu.VMEM((tr + HALO + 2 * PAD, D_MODEL), F32), pltpu.VMEM((tr + HALO + 2 * PAD, POOL_GW), F32),
                        pltpu.VMEM((tr + HALO + 2 * PAD, POOL_GW), F32)],
        compiler_params=_cparams("parallel"))(proj, proj, proj, pool_w_bf, pscale)


def _pool_bwd(proj, dyp, pool_w_bf, pscale, dproj):
    seq = proj.shape[0]
    tr = min(ROW_CHUNK_WIDE, seq)
    hb = tr // HALO
    nc = seq // tr
    n_halo = seq // HALO

    def kern(up_ref, halo_ref, zp_ref, zpn_ref, dyp_ref, dypn_ref, pw_ref, ps_ref, _,
             d01_ref, dpw_ref, dps_ref, ext, dpn, buf_a, buf_b, acc_pw, acc_ps):
        c = pl.program_id(0)

        @pl.when(c == 0)
        def _():
            acc_pw[...] = jnp.zeros_like(acc_pw)
            acc_ps[...] = jnp.zeros_like(acc_ps)

        _zero_pads((ext, dpn, buf_a, buf_b), HALO + tr)
        ext[pl.ds(PAD, HALO), :] = jnp.where(c > 0, halo_ref[...].astype(F32), 0.0)
        ext[pl.ds(PAD + HALO, tr), :] = up_ref[...].astype(F32)
        pos = c * tr + lax.broadcasted_iota(jnp.int32, (tr, POOL_GW), 0)
        pos_n = (c + 1) * tr + lax.broadcasted_iota(jnp.int32, (HALO, POOL_GW), 0)
        has_next = c < nc - 1
        for g, w in enumerate(POOL_WINDOWS):
            cols = pl.ds(g * POOL_GW, POOL_GW)
            pooled_bf = _pool_windows(ext, (buf_a, buf_b), pos, g, w, tr).astype(BF16)
            wg = pw_ref[g]
            mixed = _dot(pooled_bf, wg)
            zp = zp_ref[:, cols].astype(F32)
            sz = _silu(zp)
            dyp_g = dyp_ref[:, cols].astype(F32)
            ps = ps_ref[:, cols]
            dmixed = (dyp_g * ps * sz).astype(BF16)
            acc_ps[:, cols] += _acc8(dyp_g * mixed * sz)
            d01_ref[:, pl.ds(D_MODEL + g * POOL_GW, POOL_GW)] = (dyp_g * mixed * ps * _dsilu(zp)).astype(BF16)
            acc_pw[g] += _dot_tn(pooled_bf, dmixed)
            dpooled = _dot_nt(dmixed, wg)
            dmixed_n = (jnp.where(has_next, dypn_ref[:, cols].astype(F32), 0.0) * ps * _silu(zpn_ref[:, cols].astype(F32))).astype(BF16)
            dpooled_n = _dot_nt(dmixed_n, wg)
            dpn[pl.ds(PAD, tr), :] = dpooled / jnp.minimum(pos + 1, w).astype(F32)
            dpn[pl.ds(PAD + tr, HALO), :] = dpooled_n / jnp.minimum(pos_n + 1, w).astype(F32)
            win, _ = _window_sums(dpn, slice(None), w, (buf_a, buf_b), tr + HALO, ahead=True)
            d01_ref[:, cols] = (win[pl.ds(PAD, tr), :] - dpooled).astype(BF16)

        @pl.when(c == nc - 1)
        def _():
            dpw_ref[...] = acc_pw[...]
            dps_ref[...] = jnp.sum(acc_ps[...], axis=0, keepdims=True)

    nxt = lambda c: (jnp.minimum((c + 1) * hb, n_halo - 1), 0)
    nxt1 = lambda c: (jnp.minimum((c + 1) * hb, n_halo - 1), 1)
    return pl.pallas_call(
        kern, name="pool_bwd", grid=(nc,),
        in_specs=[_row_spec(tr, col=0),
                  pl.BlockSpec((HALO, D_MODEL), lambda c: (jnp.maximum(c * hb - 1, 0), 0)),
                  _row_spec(tr, col=1),
                  pl.BlockSpec((HALO, D_MODEL), nxt1),
                  _row_spec(tr),
                  pl.BlockSpec((HALO, D_MODEL), nxt),
                  pl.BlockSpec((len(POOL_WINDOWS), POOL_GW, POOL_GW), lambda c: (0, 0, 0)),
                  _vec_spec(), _ANY],
        out_specs=(pl.BlockSpec((tr, 2 * D_MODEL), lambda c: (c, 0)),
                   pl.BlockSpec((len(POOL_WINDOWS), POOL_GW, POOL_GW), lambda c: (0, 0, 0)),
                   _vec_spec()),
        out_shape=(jax.ShapeDtypeStruct(dproj.shape, BF16),
                   jax.ShapeDtypeStruct((len(POOL_WINDOWS), POOL_GW, POOL_GW), F32),
                   jax.ShapeDtypeStruct((1, D_MODEL), F32)),
        scratch_shapes=[pltpu.VMEM((tr + HALO + 2 * PAD, D_MODEL), F32)]
        + [pltpu.VMEM((tr + HALO + 2 * PAD, POOL_GW), F32)] * 3
        + [pltpu.VMEM((len(POOL_WINDOWS), POOL_GW, POOL_GW), F32), pltpu.VMEM((SUBLANES, D_MODEL), F32)],
        input_output_aliases={8: 0},
        compiler_params=_cparams("arbitrary"))(proj, proj, proj, proj, dyp, dyp, pool_w_bf, pscale, dproj)


def _glu_fwd(ys, proj, glu_w_bf, glu_b):
    seq = ys.shape[0]
    tr = min(ROW_CHUNK_WIDE, seq)

    def kern(ys_ref, zs_ref, w_ref, b_ref, o_ref, ot_ref):
        yg = _gelu(ys_ref[...])
        q = _dot(yg.astype(BF16), w_ref[...]) + b_ref[...]
        y = yg * _sigmoid(q) * _silu(zs_ref[...].astype(F32))
        o_ref[...] = y.astype(BF16)
        ot_ref[...] = y.T.astype(BF16)

    return pl.pallas_call(
        kern, name="glu_fwd", grid=(seq // tr,),
        in_specs=[_row_spec(tr), _row_spec(tr, col=3), pl.BlockSpec((D_MODEL, D_MODEL), lambda c: (0, 0)), _vec_spec()],
        out_specs=(_row_spec(tr), _col_spec(tr)),
        out_shape=(jax.ShapeDtypeStruct((seq, D_MODEL), BF16), jax.ShapeDtypeStruct((D_MODEL, seq), BF16)),
        compiler_params=_cparams("parallel"))(ys, proj, glu_w_bf, glu_b)


def _glu_bwd(ys, proj, dyssm, glu_w_bf, glu_b, dproj):
    seq = ys.shape[0]
    tr = min(ROW_CHUNK_WIDE, seq)
    nc = seq // tr

    def kern(ys_ref, zs_ref, dy_ref, w_ref, b_ref, _, dys_ref, dzs_ref, dq_ref, yg_ref, db_ref, acc_b):
        c = pl.program_id(0)

        @pl.when(c == 0)
        def _():
            acc_b[...] = jnp.zeros_like(acc_b)

        yg, dgelu = _gelu_and_grad(ys_ref[...])
        yg_bf = yg.astype(BF16)
        q = _dot(yg_bf, w_ref[...]) + b_ref[...]
        sg = _sigmoid(q)
        silu_z, dsilu_z = _silu_and_grad(zs_ref[...].astype(F32))
        dyv = dy_ref[...].astype(F32)
        dyglu = dyv * silu_z
        yglu = yg * sg
        dzs_ref[...] = (dyv * yglu * dsilu_z).astype(BF16)
        dq = dyglu * yglu * (1.0 - sg)
        dq_bf = dq.astype(BF16)
        acc_b[...] += _acc8(dq)
        dyg = dyglu * sg + _dot_nt(dq_bf, w_ref[...])
        dys_ref[...] = dyg * dgelu
        dq_ref[...] = dq_bf
        yg_ref[...] = yg.T.astype(BF16)

        @pl.when(c == nc - 1)
        def _():
            db_ref[...] = jnp.sum(acc_b[...], axis=0, keepdims=True)

    bf = jax.ShapeDtypeStruct((seq, D_MODEL), BF16)
    return pl.pallas_call(
        kern, name="glu_bwd", grid=(nc,),
        in_specs=[_row_spec(tr), _row_spec(tr, col=3), _row_spec(tr),
                  pl.BlockSpec((D_MODEL, D_MODEL), lambda c: (0, 0)), _vec_spec(), _ANY],
        out_specs=(_row_spec(tr), _row_spec(tr, col=3), _row_spec(tr), _col_spec(tr), _vec_spec()),
        out_shape=(jax.ShapeDtypeStruct((seq, D_MODEL), F32), jax.ShapeDtypeStruct(dproj.shape, BF16), bf,
                   jax.ShapeDtypeStruct((D_MODEL, seq), BF16), jax.ShapeDtypeStruct((1, D_MODEL), F32)),
        scratch_shapes=[pltpu.VMEM((SUBLANES, D_MODEL), F32)],
        input_output_aliases={5: 1},
        compiler_params=_cparams("arbitrary"))(ys, proj, dyssm, glu_w_bf, glu_b, dproj)


def _out_fwd_bwd(ypool, yssm, proj, x, tgt, gate, g2, wbp_bf, wbs_bf, wout_bf):
    seq = x.shape[0]
    tr = min(ROW_CHUNK, seq)
    nc = seq // tr

    def kern(yp_ref, ysm_ref, gp_ref, gs_ref, x_ref, t_ref, gate_ref, g2_ref, wbp_ref, wbs_ref, wo_ref,
             dy_ref, dyp_ref, dys_ref, d45_ref, mb_ref, dob_ref, dbp_ref, dbs_ref, loss_ref, dgate_ref, dg2_ref,
             acc_l, acc_gate, acc_g2):
        c = pl.program_id(0)

        @pl.when(c == 0)
        def _():
            acc_l[...] = jnp.zeros_like(acc_l)
            acc_gate[...] = jnp.zeros_like(acc_gate)
            acc_g2[...] = jnp.zeros_like(acc_g2)

        bp = _dot(yp_ref[...], wbp_ref[...])
        bs = _dot(ysm_ref[...], wbs_ref[...])
        sp = _sigmoid(gp_ref[...].astype(F32))
        ss = _sigmoid(gs_ref[...].astype(F32))
        merged = sp * bp + ss * bs
        mb = merged.astype(BF16)
        out = _dot(mb, wo_ref[...])
        r2 = lax.rsqrt(jnp.mean(out * out, axis=-1, keepdims=True) + RMS_EPS)
        oh = out * r2
        gate_v, g2_v = gate_ref[...], g2_ref[...]
        ohg = oh * g2_v
        diff = (x_ref[...] + gate_v * ohg) - t_ref[...]
        acc_l[...] += _acc8(diff * diff)
        dyv = diff * (1.0 / D_MODEL)
        dy_ref[...] = dyv
        dy_oh = dyv * oh
        acc_gate[...] += _acc8(dy_oh * g2_v)
        acc_g2[...] += _acc8(dy_oh * gate_v)
        gg = gate_v * g2_v
        doh = dyv * gg
        dout = r2 * (doh - oh * jnp.mean(dy_oh * gg, axis=-1, keepdims=True))
        dob = dout.astype(BF16)
        dmerged = _dot_nt(dob, wo_ref[...])
        dbp_f = dmerged * sp
        dbs_f = dmerged * ss
        dbp = dbp_f.astype(BF16)
        dbs = dbs_f.astype(BF16)
        d45_ref[:, 0:D_MODEL] = (dbp_f * bp * (1.0 - sp)).astype(BF16)
        d45_ref[:, D_MODEL:] = (dbs_f * bs * (1.0 - ss)).astype(BF16)
        dyp_ref[...] = _dot_nt(dbp, wbp_ref[...]).astype(BF16)
        dys_ref[...] = _dot_nt(dbs, wbs_ref[...]).astype(BF16)
        mb_ref[...] = merged.T.astype(BF16)
        dob_ref[...] = dob
        dbp_ref[...] = dbp
        dbs_ref[...] = dbs

        @pl.when(c == nc - 1)
        def _():
            tot = jnp.sum(acc_l[...], axis=0, keepdims=True)
            loss_ref[...] = jnp.sum(tot, axis=1, keepdims=True) * (0.5 / D_MODEL)
            dgate_ref[...] = jnp.sum(acc_gate[...], axis=0, keepdims=True)
            dg2_ref[...] = jnp.sum(acc_g2[...], axis=0, keepdims=True)

    wspec = pl.BlockSpec((D_MODEL, D_MODEL), lambda c: (0, 0))
    f32 = jax.ShapeDtypeStruct((seq, D_MODEL), F32)
    bf = jax.ShapeDtypeStruct((seq, D_MODEL), BF16)
    vec = jax.ShapeDtypeStruct((1, D_MODEL), F32)
    acc = pltpu.VMEM((SUBLANES, D_MODEL), F32)
    return pl.pallas_call(
        kern, name="out_fwd_bwd", grid=(nc,),
        in_specs=[_row_spec(tr), _row_spec(tr), _row_spec(tr, col=4), _row_spec(tr, col=5), _row_spec(tr), _row_spec(tr),
                  _vec_spec(), _vec_spec(), wspec, wspec, wspec],
        out_specs=(_row_spec(tr), _row_spec(tr), _row_spec(tr), pl.BlockSpec((tr, 2 * D_MODEL), lambda c: (c, 2)),
                   _col_spec(tr), _row_spec(tr), _row_spec(tr), _row_spec(tr),
                   pl.BlockSpec((1, 1), lambda c: (0, 0)), _vec_spec(), _vec_spec()),
        out_shape=(f32, bf, bf, jax.ShapeDtypeStruct((seq, proj.shape[1]), BF16),
                   jax.ShapeDtypeStruct((D_MODEL, seq), BF16), bf, bf, bf,
                   jax.ShapeDtypeStruct((1, 1), F32), vec, vec),
        scratch_shapes=[acc, acc, acc],
        compiler_params=_cparams("arbitrary"))(ypool, yssm, proj, proj, x, tgt, gate, g2, wbp_bf, wbs_bf, wout_bf)


def _in_bwd(dh, x, dy, g1, scale):
    seq = x.shape[0]
    tr = min(ROW_CHUNK_WIDE, seq)
    nc = seq // tr

    def kern(dh_ref, x_ref, dy_ref, g_ref, sc_ref, dx_ref, dsh_ref, dsc_ref, dg_ref, a_sh, a_sc, a_g):
        c = pl.program_id(0)

        @pl.when(c == 0)
        def _():
            a_sh[...] = jnp.zeros_like(a_sh)
            a_sc[...] = jnp.zeros_like(a_sc)
            a_g[...] = jnp.zeros_like(a_g)

        xv = x_ref[...]
        r = lax.rsqrt(jnp.mean(xv * xv, axis=-1, keepdims=True) + RMS_EPS)
        xh = xv * r
        g = g_ref[...]
        dhv = dh_ref[...]
        a_sh[...] += _acc8(dhv)
        a_sc[...] += _acc8(dhv * (xh * g))
        dn = dhv * (1.0 + sc_ref[...])
        a_g[...] += _acc8(dn * xh)
        dxh = dn * g
        dx_ref[...] = dy_ref[...] + r * (dxh - xh * jnp.mean(dxh * xh, axis=-1, keepdims=True))

        @pl.when(c == nc - 1)
        def _():
            dsh_ref[...] = jnp.sum(a_sh[...], axis=0, keepdims=True)
            dsc_ref[...] = jnp.sum(a_sc[...], axis=0, keepdims=True)
            dg_ref[...] = jnp.sum(a_g[...], axis=0, keepdims=True)

    vec = jax.ShapeDtypeStruct((1, D_MODEL), F32)
    acc = pltpu.VMEM((SUBLANES, D_MODEL), F32)
    return pl.pallas_call(
        kern, name="in_bwd", grid=(nc,),
        in_specs=[_row_spec(tr), _row_spec(tr), _row_spec(tr), _vec_spec(), _vec_spec()],
        out_specs=(_row_spec(tr), _vec_spec(), _vec_spec(), _vec_spec()),
        out_shape=(jax.ShapeDtypeStruct((seq, D_MODEL), F32), vec, vec, vec),
        scratch_shapes=[acc, acc, acc],
        compiler_params=_cparams("arbitrary"))(dh, x, dy, g1, scale)


SLAB = 2 * SUBLANES


def _local_scan(a_re, a_im, br, bi, xr, xi, row0, ls, reverse, init=None, xb=None):
    if init is None:
        x_re = jnp.zeros((SUBLANES, STATE_W), F32)
        x_im = jnp.zeros((SUBLANES, STATE_W), F32)
    else:
        x_re, x_im = init
    for i in (range(ls - 1, -1, -1) if reverse else range(ls)):
        src = pl.ds(SUBLANES * i, SUBLANES)
        dst = pl.ds(row0 + SUBLANES * i, SUBLANES)
        n_re = a_re * x_re - a_im * x_im + br[src, :]
        n_im = a_re * x_im + a_im * x_re + bi[src, :]
        if xb is not None and i % 2 == 1:
            pair = pl.ds(SUBLANES * (i - 1), SLAB)
            xb[0][pair, :] = jnp.concatenate([x_re, n_re], axis=0).astype(BF16)
            xb[1][pair, :] = jnp.concatenate([x_im, n_im], axis=0).astype(BF16)
        x_re, x_im = n_re, n_im
        xr[dst, :] = x_re
        xi[dst, :] = x_im
    return x_re, x_im


def _two(v):
    return jnp.concatenate([v, v], axis=0)


def _unpermute_rhs(v, sel):
    hi = v.astype(BF16)
    r1 = v - hi.astype(F32)
    mid = r1.astype(BF16)
    lo = (r1 - mid.astype(F32)).astype(BF16)
    return _dot(hi, sel) + _dot(mid, sel) + _dot(lo, sel)


def _scan_specs(tc, nb, rows_of):
    return dict(
        us=pl.BlockSpec((tc, nb * LANES), lambda b, c: (rows_of(c), 2 * D_MODEL // (nb * LANES) + b)),
        tok=pl.BlockSpec((tc, nb * LANES), lambda b, c: (rows_of(c), b)),
        bblk=pl.BlockSpec((nb, LANES, STATE_W), lambda b, c: (b, 0, 0)),
        cblk=pl.BlockSpec((nb, STATE_W, LANES), lambda b, c: (b, 0, 0)),
        vec=pl.BlockSpec((1, nb * STATE_W), lambda b, c: (0, b)),
        tab=pl.BlockSpec((tc, nb * STATE_W), lambda b, c: (0, b)),
        car=pl.BlockSpec((SUBLANES, nb * STATE_W), lambda b, c: (rows_of(c), b)),
        dvec=pl.BlockSpec((1, nb * LANES), lambda b, c: (0, b)))


def _ssm_scan_fwd(proj, bb_re, bb_im, cm_re, cm_im, abar_re, abar_im, pw_re, pw_im, d_skip, tc):
    seq = proj.shape[0]
    nc = seq // tc
    ls = tc // SUBLANES
    nb = SCAN_BLOCKS

    def kern(us_ref, bbr_ref, bbi_ref, cmr_ref, cmi_ref, ar_ref, ai_ref, pwr_ref, pwi_ref, d_ref,
             ys_ref, ecr_ref, eci_ref, bur, bui, car_r, car_i, end_r, end_i, upb, xb_r, xb_i, *nat):
        c = pl.program_id(1)

        @pl.when(c == 0)
        def _():
            car_r[...] = jnp.zeros_like(car_r)
            car_i[...] = jnp.zeros_like(car_i)

        for j in range(nb):
            cols = pl.ds(j * LANES, LANES)
            scols = pl.ds(j * STATE_W, STATE_W)
            nat[j][...] = us_ref[:, cols].astype(F32)
            for i in range(ls):
                upb[j, pl.ds(SUBLANES * i, SUBLANES), :] = nat[j][pl.ds(i, SUBLANES, stride=ls), :]
            u = upb[j]
            up = u.astype(BF16)
            bur[j] = _dot(up, bbr_ref[j])
            bui[j] = _dot(up, bbi_ref[j])
            a_re = jnp.broadcast_to(ar_ref[:, scols], (SUBLANES, STATE_W))
            a_im = jnp.broadcast_to(ai_ref[:, scols], (SUBLANES, STATE_W))
            x_re, x_im = _local_scan(a_re, a_im, bur.at[j], bui.at[j], bur.at[j], bui.at[j], 0, ls, False)
            end_r[j] = x_re
            end_i[j] = x_im
            big_re = pwr_ref[tc - 1:tc, scols]
            big_im = pwi_ref[tc - 1:tc, scols]
            e_re = car_r[j, 0:1, :]
            e_im = car_i[j, 0:1, :]
            for s in range(SUBLANES):
                n_re = end_r[j, s:s + 1, :] + big_re * e_re - big_im * e_im
                n_im = end_i[j, s:s + 1, :] + big_re * e_im + big_im * e_re
                e_re, e_im = n_re, n_im
                if s < SUBLANES - 1:
                    car_r[j, s + 1:s + 2, :] = e_re
                    car_i[j, s + 1:s + 2, :] = e_im
            ec_re = car_r[j]
            ec_im = car_i[j]
            ecr_ref[:, scols] = ec_re
            eci_ref[:, scols] = ec_im
            e2_re, e2_im = _two(ec_re), _two(ec_im)
            for k in range(tc // SLAB):
                rows_k = pl.ds(SLAB * k, SLAB)
                p_re = pwr_ref[rows_k, scols]
                p_im = pwi_ref[rows_k, scols]
                xb_r[j, rows_k, :] = (bur[j, rows_k, :] + p_re * e2_re - p_im * e2_im).astype(BF16)
                xb_i[j, rows_k, :] = (bui[j, rows_k, :] + p_re * e2_im + p_im * e2_re).astype(BF16)
            upb[j] = _dot(xb_r[j], cmr_ref[j]) - _dot(xb_i[j], cmi_ref[j]) + d_ref[:, cols] * u
            for i in range(ls):
                nat[j][pl.ds(i, SUBLANES, stride=ls), :] = upb[j, pl.ds(SUBLANES * i, SUBLANES), :]
            ys_ref[:, cols] = nat[j][...]
            car_r[j, 0:1, :] = e_re
            car_i[j, 0:1, :] = e_im

    sp = _scan_specs(tc, nb, lambda c: c)
    carry_shape = jax.ShapeDtypeStruct((nc * SUBLANES, STATE_ALL), F32)
    small = pltpu.VMEM((nb, SUBLANES, STATE_W), F32)
    big = pltpu.VMEM((nb, tc, STATE_W), F32)
    return pl.pallas_call(
        kern, name="ssm_scan_fwd", grid=(LANE_BLOCKS // nb, nc),
        in_specs=[sp["us"], sp["bblk"], sp["bblk"], sp["cblk"], sp["cblk"], sp["vec"], sp["vec"], sp["tab"], sp["tab"],
                  sp["dvec"]],
        out_specs=(sp["tok"], sp["car"], sp["car"]),
        out_shape=(jax.ShapeDtypeStruct((seq, D_MODEL), F32), carry_shape, carry_shape),
        scratch_shapes=[big, big, small, small, small, small, pltpu.VMEM((nb, tc, LANES), F32),
                        pltpu.VMEM((nb, tc, STATE_W), BF16), pltpu.VMEM((nb, tc, STATE_W), BF16)]
        + [pltpu.VMEM((tc, LANES), F32)] * nb,
        compiler_params=_cparams("parallel", "arbitrary"),
    )(proj, bb_re, bb_im, cm_re, cm_im, abar_re, abar_im, pw_re, pw_im, d_skip)


def _ssm_scan_bwd(proj, dys, ec_re, ec_im, bb_re, bb_im, cm_re, cm_im, abar_re, abar_im,
                  pw_re, pw_im, pv_re, pv_im, d_skip, dproj, tc):
    seq = proj.shape[0]
    nc = seq // tc
    ls = tc // SUBLANES
    nb = SCAN_BLOCKS

    def kern(us_ref, dys_ref, ecr_ref, eci_ref, bbr_ref, bbi_ref, cmr_ref, cmi_ref, ar_ref, ai_ref,
             pwr_ref, pwi_ref, pvr_ref, pvi_ref, d_ref, _,
             dus_ref, dbbr_ref, dbbi_ref, dcmr_ref, dcmi_ref, dar_ref, dai_ref, dd_ref,
             bur, bui, xr, xi, gr, gi, fc_r, fc_i, a_bbr, a_bbi, a_cmr, a_cmi, a_ar, a_ai, a_dd, upb, dpb, hb_r, hb_i,
             *nat):
        c = pl.program_id(1)

        @pl.when(c == 0)
        def _():
            for ref in (fc_r, fc_i, a_bbr, a_bbi, a_cmr, a_cmi, a_ar, a_ai, a_dd):
                ref[...] = jnp.zeros_like(ref)

        for j in range(nb):
            cols = pl.ds(j * LANES, LANES)
            scols = pl.ds(j * STATE_W, STATE_W)
            nat_u, nat_d = nat[2 * j], nat[2 * j + 1]
            nat_u[...] = us_ref[:, cols].astype(F32)
            nat_d[...] = dys_ref[:, cols]
            for i in range(ls):
                rows_i = pl.ds(SUBLANES * i, SUBLANES)
                upb[j, rows_i, :] = nat_u[pl.ds(i, SUBLANES, stride=ls), :]
                dpb[j, rows_i, :] = nat_d[pl.ds(i, SUBLANES, stride=ls), :]
            u = upb[j]
            dysv = dpb[j]
            a_dd[j] += _acc8(dysv * u)
            up = u.astype(BF16)
            bur[j] = _dot(up, bbr_ref[j])
            bui[j] = _dot(up, bbi_ref[j])
            a_re = jnp.broadcast_to(ar_ref[:, scols], (SUBLANES, STATE_W))
            a_im = jnp.broadcast_to(ai_ref[:, scols], (SUBLANES, STATE_W))
            ec_r = ecr_ref[:, scols]
            ec_i = eci_ref[:, scols]
            xr[j, 0:SUBLANES, :] = ec_r
            xi[j, 0:SUBLANES, :] = ec_i
            _local_scan(a_re, a_im, bur.at[j], bui.at[j], xr.at[j], xi.at[j], SUBLANES, ls, False, init=(ec_r, ec_i),
                        xb=(hb_r.at[j], hb_i.at[j]))
            dysp = dysv.astype(BF16)
            a_cmr[j] += _dot_tn(dysp, hb_r[j])
            a_cmi[j] -= _dot_tn(dysp, hb_i[j])
            gr[j] = _dot_nt(dysp, cmr_ref[j])
            gi[j] = -_dot_nt(dysp, cmi_ref[j])
            _local_scan(a_re, -a_im, gr.at[j], gi.at[j], gr.at[j], gi.at[j], 0, ls, True)
            big_re = pwr_ref[tc - 1:tc, scols]
            big_im = -pwi_ref[tc - 1:tc, scols]
            f_re = fc_r[j, SUBLANES - 1:SUBLANES, :]
            f_im = fc_i[j, SUBLANES - 1:SUBLANES, :]
            for s in range(SUBLANES - 1, -1, -1):
                n_re = gr[j, s:s + 1, :] + big_re * f_re - big_im * f_im
                n_im = gi[j, s:s + 1, :] + big_re * f_im + big_im * f_re
                f_re, f_im = n_re, n_im
                if s > 0:
                    fc_r[j, s - 1:s, :] = f_re
                    fc_i[j, s - 1:s, :] = f_im
            f2_r, f2_i = _two(fc_r[j]), _two(fc_i[j])
            acc_r = jnp.zeros((SUBLANES, STATE_W), F32)
            acc_i = jnp.zeros((SUBLANES, STATE_W), F32)
            for k in range(tc // SLAB):
                rows_k = pl.ds(SLAB * k, SLAB)
                q_re = pvr_ref[rows_k, scols]
                q_im = pvi_ref[rows_k, scols]
                lam_re = gr[j, rows_k, :] + q_re * f2_r + q_im * f2_i
                lam_im = gi[j, rows_k, :] + q_re * f2_i - q_im * f2_r
                xp_re = xr[j, rows_k, :]
                xp_im = xi[j, rows_k, :]
                d_r = lam_re * xp_re + lam_im * xp_im
                d_i = lam_im * xp_re - lam_re * xp_im
                acc_r = acc_r + (d_r[0:SUBLANES] + d_r[SUBLANES:])
                acc_i = acc_i + (d_i[0:SUBLANES] + d_i[SUBLANES:])
                hb_r[j, rows_k, :] = lam_re.astype(BF16)
                hb_i[j, rows_k, :] = lam_im.astype(BF16)
            a_ar[j] += acc_r
            a_ai[j] += acc_i
            fc_r[j, SUBLANES - 1:SUBLANES, :] = f_re
            fc_i[j, SUBLANES - 1:SUBLANES, :] = f_im
            lb_re = hb_r[j]
            lb_im = hb_i[j]
            a_bbr[j] += _dot_tn(up, lb_re)
            a_bbi[j] += _dot_tn(up, lb_im)
            dpb[j] = _dot_nt(lb_re, bbr_ref[j]) + _dot_nt(lb_im, bbi_ref[j]) + dysv * d_ref[:, cols]
            for i in range(ls):
                nat_d[pl.ds(i, SUBLANES, stride=ls), :] = dpb[j, pl.ds(SUBLANES * i, SUBLANES), :]
            dus_ref[:, cols] = nat_d[...].astype(BF16)

        @pl.when(c == nc - 1)
        def _():
            row_g = lax.broadcasted_iota(jnp.int32, (LANES, STATE_W), 0) // SSM_H
            col_g = lax.broadcasted_iota(jnp.int32, (LANES, STATE_W), 1) // SSM_P
            fold = (lax.broadcasted_iota(jnp.int32, (STATE_W, SSM_P), 0) % SSM_P
                    == lax.broadcasted_iota(jnp.int32, (STATE_W, SSM_P), 1)).astype(BF16)
            for j in range(nb):
                rows_j = pl.ds(j * LANES, LANES)
                for acc, out in ((a_bbr, dbbr_ref), (a_bbi, dbbi_ref), (a_cmr, dcmr_ref), (a_cmi, dcmi_ref)):
                    out[rows_j, :] = _unpermute_rhs(jnp.where(row_g == col_g, acc[j], 0.0), fold)
                dar_ref[:, pl.ds(j * STATE_W, STATE_W)] = jnp.sum(a_ar[j], axis=0, keepdims=True)
                dai_ref[:, pl.ds(j * STATE_W, STATE_W)] = jnp.sum(a_ai[j], axis=0, keepdims=True)
                dd_ref[:, pl.ds(j * LANES, LANES)] = jnp.sum(a_dd[j], axis=0, keepdims=True)

    sp = _scan_specs(tc, nb, lambda c: nc - 1 - c)
    ghp = pl.BlockSpec((nb * LANES, SSM_P), lambda b, c: (b, 0))
    ghp_shape = jax.ShapeDtypeStruct((SSM_G * SSM_H, SSM_P), F32)
    small = pltpu.VMEM((nb, SUBLANES, STATE_W), F32)
    big = pltpu.VMEM((nb, tc, STATE_W), F32)
    bigp = pltpu.VMEM((nb, tc + SUBLANES, STATE_W), F32)
    blk = pltpu.VMEM((nb, LANES, STATE_W), F32)
    tok = pltpu.VMEM((nb, tc, LANES), F32)
    return pl.pallas_call(
        kern, name="ssm_scan_bwd", grid=(LANE_BLOCKS // nb, nc),
        in_specs=[sp["us"], sp["tok"], sp["car"], sp["car"], sp["bblk"], sp["bblk"], sp["cblk"], sp["cblk"],
                  sp["vec"], sp["vec"], sp["tab"], sp["tab"], sp["tab"], sp["tab"], sp["dvec"], _ANY],
        out_specs=(sp["us"], ghp, ghp, ghp, ghp, sp["vec"], sp["vec"], sp["dvec"]),
        out_shape=(jax.ShapeDtypeStruct(dproj.shape, BF16), ghp_shape, ghp_shape, ghp_shape, ghp_shape,
                   jax.ShapeDtypeStruct((1, STATE_ALL), F32), jax.ShapeDtypeStruct((1, STATE_ALL), F32),
                   jax.ShapeDtypeStruct((1, D_MODEL), F32)),
        scratch_shapes=[big, big, bigp, bigp, big, big, small, small, blk, blk, blk, blk,
                        small, small, pltpu.VMEM((nb, SUBLANES, LANES), F32), tok, tok,
                        pltpu.VMEM((nb, tc, STATE_W), BF16), pltpu.VMEM((nb, tc, STATE_W), BF16)]
        + [pltpu.VMEM((tc, LANES), F32)] * (2 * nb),
        input_output_aliases={15: 0},
        compiler_params=_cparams("parallel", "arbitrary"),
    )(proj, dys, ec_re, ec_im, bb_re, bb_im, cm_re, cm_im, abar_re, abar_im, pw_re, pw_im, pv_re, pv_im, d_skip, dproj)


def _eye5():
    return jnp.asarray(np.eye(GROUPS_PER_BLOCK, dtype=np.float32)[None, :, None, :, None])


def _embed_b(bb_t):
    t = bb_t.transpose(1, 0, 2).reshape(LANE_BLOCKS, GROUPS_PER_BLOCK, SSM_H, 1, SSM_P)
    return (t * _eye5()).reshape(LANE_BLOCKS, LANES, STATE_W)


def _embed_c(c_ghp):
    t = c_ghp.transpose(0, 2, 1).reshape(LANE_BLOCKS, GROUPS_PER_BLOCK, SSM_P, 1, SSM_H)
    return (t * _eye5()).reshape(LANE_BLOCKS, STATE_W, LANES)


def _local_step(x, c_row, tgt, w_ada_bf, b_ada, g1, g2, w_in_bf, pool_w_bf, pscale, a_re, a_im, log_dt,
                b_re_t, b_im_t, c_re, c_im, d_skip, glu_w_bf, glu_b, wbp_bf, wbs_bf, wout_bf,
                split_proj=None, ride_for_dw_in=None, ride_for_dh=None):
    seq = x.shape[0]
    tc = min(SCAN_CHUNK, seq)
    mod8, silu_c = _mod_kernel(c_row, w_ada_bf, b_ada)
    mod = mod8[0:1]
    shift, scale, gate = mod[:, 0:D_MODEL], mod[:, D_MODEL:2 * D_MODEL], mod[:, 2 * D_MODEL:]

    abar_re, abar_im, bb_re_t, bb_im_t = _ssm_params(a_re, a_im, log_dt, b_re_t, b_im_t)
    abar_re_f, abar_im_f = abar_re.reshape(1, STATE_ALL), abar_im.reshape(1, STATE_ALL)
    pw_re, pw_im, pv_re, pv_im = _pow_tables(abar_re_f, abar_im_f, tc)
    bbe_re, bbe_im = _embed_b(bb_re_t).astype(BF16), _embed_b(bb_im_t).astype(BF16)
    cme_re, cme_im = _embed_c(c_re).astype(BF16), _embed_c(c_im).astype(BF16)
    d_row = d_skip.reshape(1, D_MODEL)

    if split_proj:
        w_own, chip, w_in_ride, unpack_w_in, late_ride, unpack_late = split_proj
        h, h_t, proj, w_blocks = _in_norm_proj_own(x, g1, scale, shift, w_own, chip, w_in_ride)
        w_in_bf = unpack_w_in(w_blocks)
        proj, *gathered = _proj_rest(h, w_blocks, proj, chip, late_ride)
        pool_w_bf, glu_w_bf, wbp_bf, wbs_bf, wout_bf = unpack_late(*gathered)
    else:
        h, h_t = _in_norm(x, g1, scale, shift)
        proj = _mm([h], [w_in_bf], name="proj", out_dtype=BF16, bm=1024, bn=1536, bk=1024)
    ypool, ypool_t = _pool_fwd(proj, pool_w_bf, pscale)
    ys, ec_re, ec_im = _ssm_scan_fwd(proj, bbe_re, bbe_im, cme_re, cme_im, abar_re_f, abar_im_f,
                                      pw_re, pw_im, d_row, tc)
    yssm, yssm_t = _glu_fwd(ys, proj, glu_w_bf, glu_b)
    (dy, dypool, dyssm, dproj, merged_t, dob, dbp, dbs, loss, dgate, dg2) = _out_fwd_bwd(
        ypool, yssm, proj, x, tgt, gate, g2, wbp_bf, wbs_bf, wout_bf)

    d_wout = _mm([merged_t], [dob], name="dw_out", bm=1024, bn=1024, bk=2048)
    d_wbp = _mm([ypool_t], [dbp], name="dw_bp", bm=1024, bn=1024, bk=2048)
    d_wbs = _mm([yssm_t], [dbs], name="dw_bs", bm=1024, bn=1024, bk=2048)
    dys, dproj, dq, yg_t, d_glu_b = _glu_bwd(ys, proj, dyssm, glu_w_bf, glu_b, dproj)
    d_glu_w = _mm([yg_t], [dq], name="dw_glu", bm=1024, bn=1024, bk=2048)
    (dproj, dbbe_re, dbbe_im, dcme_re, dcme_im, d_abar_re, d_abar_im, d_dskip) = _ssm_scan_bwd(
        proj, dys, ec_re, ec_im, bbe_re, bbe_im, cme_re, cme_im, abar_re_f, abar_im_f,
        pw_re, pw_im, pv_re, pv_im, d_row, dproj, tc)
    dproj, d_pool_w, d_pscale = _pool_bwd(proj, dypool, pool_w_bf, pscale, dproj)
    dparts = [dproj]
    small_ready = dict(
        dg2=dg2, d_pscale=d_pscale, d_glu_b=d_glu_b, d_dskip=d_dskip, d_abar_re=d_abar_re, d_abar_im=d_abar_im,
        d_bb_re_t=dbbe_re.reshape(SSM_G, SSM_H, SSM_P).transpose(1, 0, 2),
        d_bb_im_t=dbbe_im.reshape(SSM_G, SSM_H, SSM_P).transpose(1, 0, 2),
        d_c_re=dcme_re.reshape(SSM_G, SSM_H, SSM_P), d_c_im=dcme_im.reshape(SSM_G, SSM_H, SSM_P))
    ride = ride_for_dw_in(small_ready) if ride_for_dw_in else None
    d_win = _mm([h_t], dparts, name="dw_in", bm=1024, bn=1024, bk=2048, ride=ride)
    rode_dw_in = ()
    if ride:
        d_win, rode_dw_in = d_win[0], tuple(d_win[1:])
    big_grads = dict(d_win=d_win, d_glu_w=d_glu_w, d_wbp=d_wbp, d_wbs=d_wbs, d_wout=d_wout, d_pool_w=d_pool_w)
    ride = ride_for_dh(big_grads) if ride_for_dh else None
    dh = _mm(dparts, [w_in_bf], tb=True, name="dh", bm=2048, bn=1024, bk=1024, ride=ride)
    rode = ()
    if ride:
        dh, rode = dh[0], tuple(dh[1:])
    grad_x, dshift, dscale, dg1 = _in_bwd(dh, x, dy, g1, scale)
    dmod = jnp.concatenate([dshift, dscale, dgate], axis=1)
    return dict(
        rode=rode, rode_dw_in=rode_dw_in, loss=loss[0, 0], grad_x=grad_x, dmod=dmod, silu_c=silu_c, dg1=dg1,
        **small_ready, **big_grads)


def _position():
    x, y, c = lax.axis_index("x"), lax.axis_index("y"), lax.axis_index("c")
    chips = [(1 - x, y), (x, 1 - y), (1 - x, 1 - y)]
    return x, y, c, chips


_ANY = pl.BlockSpec(memory_space=pl.ANY)
COMM_CHUNKS = 4
COMM_ROW_ALIGN = 16


def _row_chunks(rows, k):
    assert rows % (k * COMM_ROW_ALIGN) == 0, (rows, k)
    step = rows // k
    return [(q * step, step) for q in range(k)]


def _ag_weights_ride(packed, n_chunks=COMM_CHUNKS):
    rows, width = packed.shape
    half = rows // 2
    chunks = _row_chunks(half, n_chunks)
    nq = len(chunks)

    def parts(p_ref, out_ref, send_sems, recv_sems):
        x, y, c, chips = _position()
        sibling = (x, y, 1 - c)

        def copy(k, chip, h, q, to, src=None):
            start, size = chunks[q]
            rows_q = pl.ds(h * half + start, size)
            dst = out_ref.at[2 * chip[0] + chip[1], rows_q, :]
            return pltpu.make_async_remote_copy(
                src_ref=dst if src is None else src.at[rows_q, :], dst_ref=dst, send_sem=send_sems.at[k * nq + q],
                recv_sem=recv_sems.at[k * nq + q], device_id=to, device_id_type=MESH_ID)

        mine = [copy(6 + h, (x, y), h, q, sibling, src=p_ref) for h in range(2) for q in range(nq)]
        first = [copy(j, (x, y), c, q, (*chip, c), src=p_ref) for q in range(nq) for j, chip in enumerate(chips)]
        return (x, y, c), chips, sibling, copy, mine, first

    def start(ins, outs, sems):
        _, _, _, _, mine, first = parts(ins[0], outs[0], sems[0], sems[1])
        for cp in first + mine:
            cp.start()

    def wait(ins, outs, sems):
        (x, y, c), chips, sibling, copy, mine, first = parts(ins[0], outs[0], sems[0], sems[1])
        passed = []
        for q in range(nq):
            for j, chip in enumerate(chips):
                copy(j, chip, c, q, (x, y, c)).wait_recv()
                fwd = copy(3 + j, chip, c, q, sibling)
                fwd.start()
                passed.append(fwd)
        for q in range(nq):
            for j, chip in enumerate(chips):
                copy(3 + j, chip, 1 - c, q, (x, y, c)).wait_recv()
        for cp in mine:
            cp.wait_recv()
        for cp in first + passed + mine:
            cp.wait_send()

    return _Ride([packed], [jax.ShapeDtypeStruct((N_CHIPS, rows, width), packed.dtype)],
                 [pltpu.SemaphoreType.DMA((8 * nq,)), pltpu.SemaphoreType.DMA((8 * nq,))], start, wait)


def _join_rides(rides):
    def split(seq, counts):
        out, at = [], 0
        for n in counts:
            out.append(seq[at:at + n])
            at += n
        return out

    n_in = [len(r.inputs) for r in rides]
    n_out = [len(r.out_shapes) for r in rides]
    n_sem = [len(r.scratch) for r in rides]

    def start(ins, outs, sems):
        for r, i, o, s in zip(rides, split(ins, n_in), split(outs, n_out), split(sems, n_sem)):
            r.start(i, o, s)

    def wait(ins, outs, sems):
        for r, i, o, s in zip(rides, split(ins, n_in), split(outs, n_out), split(sems, n_sem)):
            r.wait(i, o, s)

    return _Ride([a for r in rides for a in r.inputs], [a for r in rides for a in r.out_shapes],
                 [a for r in rides for a in r.scratch], start, wait)


def _run_ride(ride, name):
    n_in, n_out = len(ride.inputs), len(ride.out_shapes)

    def body(*refs):
        ins, outs, sems = refs[:n_in], refs[n_in:n_in + n_out], refs[n_in + n_out:]
        ride.start(ins, outs, sems)
        ride.wait(ins, outs, sems)

    return pl.pallas_call(
        body, name=name, in_specs=[_ANY] * n_in, out_specs=(_ANY,) * n_out, out_shape=tuple(ride.out_shapes),
        scratch_shapes=list(ride.scratch))(*ride.inputs)


def _small_allgather_ride(buf):
    rows, width = buf.shape
    chunks = _row_chunks(rows, COMM_CHUNKS)
    nq = len(chunks)

    def parts(b_ref, all_ref, send_sems, recv_sems, local_sem):
        x, y, c, chips = _position()
        me, sibling = (x, y, c), (x, y, 1 - c)

        def copy(k, block, q, to, src=None):
            rows_q = pl.ds(chunks[q][0], chunks[q][1])
            dst = all_ref.at[4 * block[0] + 2 * block[1] + block[2], rows_q, :]
            return pltpu.make_async_remote_copy(
                src_ref=dst if src is None else src.at[rows_q, :], dst_ref=dst, send_sem=send_sems.at[k * nq + q],
                recv_sem=recv_sems.at[k * nq + q], device_id=to, device_id_type=MESH_ID)

        mine = pltpu.make_async_copy(b_ref, all_ref.at[4 * x + 2 * y + c], local_sem)
        first = []
        for q in range(nq):
            first += [copy(1 + j, me, q, (*chip, c), src=b_ref) for j, chip in enumerate(chips)]
            first.append(copy(0, me, q, sibling, src=b_ref))
        return me, sibling, c, chips, copy, mine, first

    def start(ins, outs, sems):
        _, _, _, _, _, mine, first = parts(ins[0], outs[0], *sems)
        mine.start()
        for cp in first:
            cp.start()

    def wait(ins, outs, sems):
        me, sibling, c, chips, copy, mine, first = parts(ins[0], outs[0], *sems)
        passed = []
        for q in range(nq):
            for j, chip in enumerate(chips):
                copy(1 + j, (*chip, c), q, me).wait_recv()
                fwd = copy(4 + j, (*chip, c), q, sibling)
                fwd.start()
                passed.append(fwd)
        for q in range(nq):
            copy(0, sibling, q, me).wait_recv()
            for j, chip in enumerate(chips):
                copy(4 + j, (*chip, 1 - c), q, me).wait_recv()
        for cp in first + passed:
            cp.wait_send()
        mine.wait()

    return _Ride([buf], [jax.ShapeDtypeStruct((N_DEV, rows, width), F32)],
                 [pltpu.SemaphoreType.DMA((7 * nq,)), pltpu.SemaphoreType.DMA((7 * nq,)), pltpu.SemaphoreType.DMA],
                 start, wait)


def _sum_devices(blocks):
    n, rows, width = blocks.shape
    rb = rows // 2 if (rows // 2) % SUBLANES == 0 else rows

    def kern(b_ref, o_ref):
        total = b_ref[0]
        for d in range(1, n):
            total = total + b_ref[d]
        o_ref[...] = total

    return pl.pallas_call(
        kern, name="small_sum", grid=(rows // rb,), in_specs=[pl.BlockSpec((n, rb, width), lambda i: (0, i, 0))],
        out_specs=pl.BlockSpec((rb, width), lambda i: (i, 0)), out_shape=jax.ShapeDtypeStruct((rows, width), F32),
        compiler_params=_cparams("parallel"))(blocks)


def _small_allgather_sum(buf, head_rows, n_chunks=COMM_CHUNKS):
    rows, width = buf.shape
    chunks = _row_chunks(rows, n_chunks)
    nq = len(chunks)

    def body(b_ref, head_ref, sum_ref, all_ref, send_sems, recv_sems, local_sem):
        x, y, c, chips = _position()
        me, sibling = (x, y, c), (x, y, 1 - c)

        def slot(px, py, pc):
            return all_ref.at[4 * px + 2 * py + pc]

        def copy(k, block, q, to, src=None):
            rows_q = pl.ds(chunks[q][0], chunks[q][1])
            dst = slot(*block).at[rows_q, :]
            return pltpu.make_async_remote_copy(
                src_ref=dst if src is None else src.at[rows_q, :], dst_ref=dst, send_sem=send_sems.at[k * nq + q],
                recv_sem=recv_sems.at[k * nq + q], device_id=to, device_id_type=MESH_ID)

        mine = pltpu.make_async_copy(b_ref, slot(*me), local_sem)
        mine.start()
        first = []
        for q in range(nq):
            first += [copy(1 + j, me, q, (*chip, c), src=b_ref) for j, chip in enumerate(chips)]
            first.append(copy(0, me, q, sibling, src=b_ref))
        for cp in first:
            cp.start()
        passed = []
        for q in range(nq):
            for j, chip in enumerate(chips):
                copy(1 + j, (*chip, c), q, me).wait_recv()
                fwd = copy(4 + j, (*chip, c), q, sibling)
                fwd.start()
                passed.append(fwd)
        for q in range(nq):
            copy(0, sibling, q, me).wait_recv()
            for j, chip in enumerate(chips):
                copy(4 + j, (*chip, 1 - c), q, me).wait_recv()
        for cp in first + passed:
            cp.wait_send()
        mine.wait()
        total = all_ref[0]
        for d in range(1, N_DEV):
            total = total + all_ref[d]
        sum_ref[...] = total
        head_ref[...] = all_ref[:, 0:head_rows, :]

    vm = pl.BlockSpec(memory_space=pltpu.VMEM)
    return pl.pallas_call(
        body, name="small_allgather_sum", in_specs=[vm], out_specs=(vm, vm),
        out_shape=(jax.ShapeDtypeStruct((N_DEV, head_rows, width), F32), jax.ShapeDtypeStruct((rows, width), F32)),
        scratch_shapes=[pltpu.VMEM((N_DEV, rows, width), F32), pltpu.SemaphoreType.DMA((7 * nq,)),
                        pltpu.SemaphoreType.DMA((7 * nq,)), pltpu.SemaphoreType.DMA],
        compiler_params=_cparams(),
    )(buf)


def _rs_pair(g):
    n, rows, width = g.shape
    half = rows // 2
    chunks = _row_chunks(half, COMM_CHUNKS)
    nq = len(chunks)

    def body(g_ref, got_ref, send_sems, recv_sems):
        x, y, c, _ = _position()
        swaps = []
        for k in range(n):
            for q, (start, size) in enumerate(chunks):
                swaps.append(pltpu.make_async_remote_copy(
                    src_ref=g_ref.at[k, pl.ds((1 - c) * half + start, size), :], dst_ref=got_ref.at[k, pl.ds(start, size), :],
                    send_sem=send_sems.at[k * nq + q], recv_sem=recv_sems.at[k * nq + q],
                    device_id=(x, y, 1 - c), device_id_type=MESH_ID))
        for cp in swaps:
            cp.start()
        for cp in swaps:
            cp.wait()

    return pl.pallas_call(
        body, name="rs_pair", in_specs=[_ANY], out_specs=_ANY, out_shape=jax.ShapeDtypeStruct((n, half, width), g.dtype),
        scratch_shapes=[pltpu.SemaphoreType.DMA((n * nq,)), pltpu.SemaphoreType.DMA((n * nq,))],
    )(g)


def _rs_chips_ride(part_bf):
    n, rows, width = part_bf.shape
    chunks = _row_chunks(rows, COMM_CHUNKS)
    nq = len(chunks)

    def sends(pb_ref, got_ref, send_sems, recv_sems):
        x, y, c, chips = _position()
        out = []
        for q, (start, size) in enumerate(chunks):
            for j, chip in enumerate(chips):
                out.append(pltpu.make_async_remote_copy(
                    src_ref=pb_ref.at[2 * chip[0] + chip[1], pl.ds(start, size), :], dst_ref=got_ref.at[j, pl.ds(start, size), :],
                    send_sem=send_sems.at[j * nq + q], recv_sem=recv_sems.at[j * nq + q],
                    device_id=(*chip, c), device_id_type=MESH_ID))
        return out

    def start(ins, outs, sems):
        for cp in sends(ins[0], outs[0], sems[0], sems[1]):
            cp.start()

    def wait(ins, outs, sems):
        for cp in sends(ins[0], outs[0], sems[0], sems[1]):
            cp.wait()

    return _Ride([part_bf], [jax.ShapeDtypeStruct((N_CHIPS - 1, rows, width), BF16)],
                 [pltpu.SemaphoreType.DMA((3 * nq,)), pltpu.SemaphoreType.DMA((3 * nq,))], start, wait)


def _rs_join(shard):
    rows, width = shard.shape
    half = rows // 2
    chunks = _row_chunks(half, COMM_CHUNKS)
    nq = len(chunks)

    def body(in_ref, out_ref, send_sems, recv_sems):
        x, y, c, _ = _position()
        def swap(q, h):
            rows_q = pl.ds(h * half + chunks[q][0], chunks[q][1])
            return pltpu.make_async_remote_copy(
                src_ref=in_ref.at[rows_q, :], dst_ref=out_ref.at[rows_q, :], send_sem=send_sems.at[q],
                recv_sem=recv_sems.at[q], device_id=(x, y, 1 - c), device_id_type=MESH_ID)

        for q in range(nq):
            swap(q, c).start()
        for q in range(nq):
            swap(q, 1 - c).wait_recv()
        for q in range(nq):
            swap(q, c).wait_send()

    return pl.pallas_call(
        body, name="rs_join", in_specs=[_ANY], out_specs=_ANY, input_output_aliases={0: 0},
        out_shape=jax.ShapeDtypeStruct(shard.shape, shard.dtype),
        scratch_shapes=[pltpu.SemaphoreType.DMA((nq,)), pltpu.SemaphoreType.DMA((nq,))],
    )(shard)


def _pair_add(g, got, core):
    n, half, width = got.shape
    nb = 2
    rb = half // nb

    def kern(c_ref, a_ref, b_ref, f_ref, h_ref):
        s = a_ref[...] + b_ref[...]
        f_ref[...] = s
        h_ref[...] = s.astype(BF16)

    spec = pl.BlockSpec((1, rb, width), lambda k, i, c_ref: (k, i, 0))
    return pl.pallas_call(
        kern, name="rs_pair_add",
        grid_spec=pltpu.PrefetchScalarGridSpec(
            num_scalar_prefetch=1, grid=(n, nb),
            in_specs=[pl.BlockSpec((1, rb, width), lambda k, i, c_ref: (k, c_ref[0] * nb + i, 0)), spec],
            out_specs=(spec, spec)),
        out_shape=(jax.ShapeDtypeStruct(got.shape, F32), jax.ShapeDtypeStruct(got.shape, BF16)),
        compiler_params=_cparams("parallel", "parallel"))(core, g, got)


def _chip_add(part_f32, got, where):
    _, rows, width = part_f32.shape
    nb = 2
    rb = rows // nb

    def kern(w_ref, a_ref, b_ref, o_ref):
        o_ref[...] = ((a_ref[0] + b_ref[0].astype(F32)) + b_ref[1].astype(F32)) + b_ref[2].astype(F32)

    return pl.pallas_call(
        kern, name="rs_chip_add",
        grid_spec=pltpu.PrefetchScalarGridSpec(
            num_scalar_prefetch=1, grid=(nb,),
            in_specs=[pl.BlockSpec((1, rb, width), lambda i, w_ref: (w_ref[0], i, 0)),
                      pl.BlockSpec((N_CHIPS - 1, rb, width), lambda i, w_ref: (0, i, 0))],
            out_specs=pl.BlockSpec((rb, width), lambda i, w_ref: (w_ref[1] * nb + i, 0))),
        out_shape=jax.ShapeDtypeStruct((2 * rows, width), F32),
        compiler_params=_cparams("parallel"))(where, part_f32, got)


def _adamw(w, g, m, v, name):
    rows, width = w.shape
    rb = rows
    for cand in (512, 256, 128, 64, 32, 16, 8):
        if rows % cand == 0 and cand * width * 4 <= ADAM_BLOCK_BYTES:
            rb = cand
            break
    spec = pl.BlockSpec((rb, width), lambda i: (i, 0))

    def kern(w_ref, g_ref, m_ref, v_ref, d_ref, nm_ref, nv_ref):
        d_ref[...], nm_ref[...], nv_ref[...] = _adamw_update(w_ref[...], g_ref[...], m_ref[...], v_ref[...])

    shp = jax.ShapeDtypeStruct(w.shape, F32)
    return pl.pallas_call(
        kern, name=name, grid=(rows // rb,), in_specs=[spec] * 4, out_specs=(spec, spec, spec),
        out_shape=(shp, shp, shp), compiler_params=_cparams("parallel"))(w, g, m, v)


def _adamw_update(w, g, m, v):
    nm = ADAM_B1 * m + (1.0 - ADAM_B1) * g
    nv = ADAM_B2 * v + (1.0 - ADAM_B2) * (g * g)
    m_hat = nm / (1.0 - ADAM_B1 ** ADAM_STEP)
    v_hat = nv / (1.0 - ADAM_B2 ** ADAM_STEP)
    return -ADAM_LR * (m_hat / (jnp.sqrt(v_hat) + ADAM_EPS) + ADAM_WD * w), nm, nv


def _adamw_small(params):
    n = len(params)

    def kern(*refs):
        ins, outs = refs[:4 * n], refs[4 * n:]
        for p in range(n):
            w_ref, g_ref, m_ref, v_ref = ins[4 * p:4 * p + 4]
            d, nm, nv = _adamw_update(w_ref[...], g_ref[...], m_ref[...], v_ref[...])
            outs[3 * p][...] = d
            outs[3 * p + 1][...] = nm
            outs[3 * p + 2][...] = nv

    flat = [a for group in params for a in group]
    shapes = [jax.ShapeDtypeStruct(group[0].shape, F32) for group in params for _ in range(3)]
    res = pl.pallas_call(kern, name="adamw_small", out_shape=tuple(shapes), compiler_params=_cparams())(*flat)
    return [tuple(res[3 * p:3 * p + 3]) for p in range(n)]


def _wada_grad(silu_t, dmod_cols):
    n = dmod_cols.shape[1]

    def kern(s_ref, d_ref, o_ref):
        acc = s_ref[:, 0:1] * d_ref[0:1, :]
        for b in range(1, N_DEV):
            acc = acc + s_ref[:, b:b + 1] * d_ref[b:b + 1, :]
        o_ref[...] = acc

    return pl.pallas_call(kern, name="wada_grad", out_shape=jax.ShapeDtypeStruct((D_MODEL, n), F32),
                          compiler_params=_cparams())(silu_t, dmod_cols)


def _rows(a, multiple):
    flat = a.reshape(-1)
    pad = (-flat.shape[0]) % (D_MODEL * multiple)
    if pad:
        flat = jnp.concatenate([flat, jnp.zeros((pad,), flat.dtype)])
    return flat.reshape(-1, D_MODEL)


def _part_rows(shape, multiple):
    return -(-int(np.prod(shape)) // (D_MODEL * multiple)) * multiple


def _pack_rows(parts, multiple, total_multiple=1):
    blocks = [_rows(p, multiple) for p in parts]
    pad = (-sum(b.shape[0] for b in blocks)) % total_multiple
    if pad:
        blocks.append(jnp.zeros((pad, D_MODEL), blocks[0].dtype))
    return jnp.concatenate(blocks, axis=0)


def _unpack_rows(buf, shapes, multiple):
    out, r = [], 0
    for shp in shapes:
        n = int(np.prod(shp))
        nr = _part_rows(shp, multiple)
        out.append(buf[r:r + nr].reshape(-1)[:n].reshape(shp))
        r += nr
    return out


def kernel(x, c, w_ada, b_ada, norm_pre, norm_post, w_in, pool_w, pool_scale, ssm_a_re, ssm_a_im, ssm_log_dt, ssm_b_re, ssm_b_im, ssm_c_re, ssm_c_im, ssm_d, glu_w, glu_b, w_branch_pool, w_branch_ssm, w_out, loss_target, m_w_ada, m_b_ada, m_norm_pre, m_norm_post, m_w_in, m_pool_w, m_pool_scale, m_ssm_a_re, m_ssm_a_im, m_ssm_log_dt, m_ssm_b_re, m_ssm_b_im, m_ssm_c_re, m_ssm_c_im, m_ssm_d, m_glu_w, m_glu_b, m_w_branch_pool, m_w_branch_ssm, m_w_out, v_w_ada, v_b_ada, v_norm_pre, v_norm_post, v_w_in, v_pool_w, v_pool_scale, v_ssm_a_re, v_ssm_a_im, v_ssm_log_dt, v_ssm_b_re, v_ssm_b_im, v_ssm_c_re, v_ssm_c_im, v_ssm_d, v_glu_w, v_glu_b, v_w_branch_pool, v_w_branch_ssm, v_w_out):
    n_ada = w_ada.shape[2]
    n_in = w_in.shape[2]
    n_row = glu_w.shape[1]
    n_pool = pool_w.shape[2]
    n_groups = pool_w.shape[1]

    (g_ada,) = _run_ride(_ag_weights_ride(w_ada[0].astype(BF16)), "ag_weights")
    w_ada_bf = g_ada.transpose(1, 0, 2).reshape(D_MODEL, N_CHIPS * n_ada)
    w_in_own = w_in[0].astype(BF16)
    w_in_ride = _ag_weights_ride(w_in_own)

    def unpack_w_in(g_in):
        return g_in.transpose(1, 0, 2).reshape(D_MODEL, N_CHIPS * n_in)
    pool_rows = n_groups * n_pool * POOL_GW // D_MODEL
    late_shards = [pool_w[0].reshape(n_groups * n_pool, POOL_GW), glu_w[0], w_branch_pool[0], w_branch_ssm[0], w_out[0]]
    late_ride = _join_rides([_ag_weights_ride(s.astype(BF16), n_chunks=2) for s in late_shards])

    def unpack_late(pool, *squares):
        pool = pool.reshape(N_CHIPS, n_groups, n_pool, POOL_GW).transpose(1, 0, 2, 3)
        return (pool.reshape(n_groups, POOL_GW, POOL_GW), *[s.reshape(D_MODEL, D_MODEL) for s in squares])

    chip = 2 * lax.axis_index("x") + lax.axis_index("y")
    core = lax.axis_index("c").astype(jnp.int32)
    kept = {}

    def by_cols(a, n):
        return a.reshape(D_MODEL, N_CHIPS, n).transpose(1, 0, 2).reshape(N_CHIPS, -1, D_MODEL)

    def by_rows(a):
        return a.reshape(N_CHIPS, n_row, D_MODEL)

    def exchange_big(g):
        pool_by_chip = g["d_pool_w"].reshape(n_groups, N_CHIPS, n_pool, POOL_GW).transpose(1, 0, 2, 3)
        blocks = [by_cols(g["d_win"], n_in), by_rows(g["d_glu_w"]), by_rows(g["d_wbp"]), by_rows(g["d_wbs"]),
                  by_rows(g["d_wout"]), pool_by_chip.reshape(N_CHIPS, pool_rows, D_MODEL)]
        pad = (-sum(b.shape[1] for b in blocks)) % (2 * COMM_CHUNKS * COMM_ROW_ALIGN)
        if pad:
            blocks.append(jnp.zeros((N_CHIPS, pad, D_MODEL), F32))
        g_packed = jnp.concatenate(blocks, axis=1)
        kept["part_f32"], part_bf = _pair_add(g_packed, _rs_pair(g_packed), core.reshape(1))
        return _rs_chips_ride(part_bf)

    a_re, a_im, log_dt = ssm_a_re[0], ssm_a_im[0], ssm_log_dt[0].reshape(SSM_G, 1)
    b_re_t, b_im_t = ssm_b_re[0].transpose(2, 0, 1), ssm_b_im[0].transpose(2, 0, 1)
    early_names = ["dg2", "d_pscale", "d_glu_b", "d_dskip", "d_abar_re", "d_abar_im", "d_bb_re_t", "d_bb_im_t",
                   "d_c_re", "d_c_im"]

    def exchange_small(s):
        parts = [s[k] for k in early_names]
        kept["early_shapes"] = [p.shape for p in parts]
        return _small_allgather_ride(_pack_rows(parts, SUBLANES, COMM_CHUNKS * COMM_ROW_ALIGN))

    res = _local_step(x[0], c, loss_target[0], w_ada_bf, b_ada, norm_pre, norm_post, None, None, pool_scale,
                      a_re, a_im, log_dt, b_re_t, b_im_t, ssm_c_re[0], ssm_c_im[0], ssm_d[0], None, glu_b[0:1],
                      None, None, None,
                      split_proj=(w_in_own, chip.astype(jnp.int32).reshape(1), w_in_ride, unpack_w_in, late_ride, unpack_late),
                      ride_for_dw_in=exchange_small, ride_for_dh=exchange_big)

    (all_early,) = res["rode_dw_in"]
    (g_norm_post, g_pscale, g_glu_b, g_dskip, s_abar_re, s_abar_im, s_bb_re, s_bb_im, g_c_re, g_c_im) = _unpack_rows(
        _sum_devices(all_early), kept["early_shapes"], SUBLANES)
    g_a_re, g_a_im, g_log_dt, g_b_re_t, g_b_im_t = _ssm_params_bwd(
        a_re, a_im, log_dt, b_re_t, b_im_t, s_abar_re.reshape(SSM_G, SSM_P), s_abar_im.reshape(SSM_G, SSM_P),
        s_bb_re, s_bb_im)
    late_parts = [res["dmod"], res["silu_c"], res["dg1"], res["loss"].reshape(1, 1)]
    late_shapes = [p.shape for p in late_parts]
    head_rows = _part_rows(late_shapes[0], SUBLANES) + _part_rows(late_shapes[1], SUBLANES)
    all_late, sum_late = _small_allgather_sum(_pack_rows(late_parts, SUBLANES, COMM_ROW_ALIGN), head_rows, n_chunks=1)
    g_b_ada, _, g_norm_pre, loss = _unpack_rows(sum_late, late_shapes, SUBLANES)
    loss = loss[0, 0]
    dmod_all = all_late[:, 0:3].reshape(N_DEV, 3 * D_MODEL)
    dmod_cols = lax.dynamic_slice_in_dim(dmod_all, chip * n_ada, n_ada, axis=1)
    silu_t = all_late[:, _part_rows(late_shapes[0], SUBLANES)].transpose(1, 0)
    g_w_ada = _wada_grad(silu_t, dmod_cols)

    (got_chips,) = res["rode"]
    shard = _rs_join(_chip_add(kept["part_f32"], got_chips, jnp.stack([chip.astype(jnp.int32), core])))
    r = 0
    g_w_in = shard[r:r + n_in].reshape(D_MODEL, n_in)
    r += n_in
    g_squares = []
    for _ in range(4):
        g_squares.append(shard[r:r + n_row])
        r += n_row
    g_glu_w, g_wbp, g_wbs, g_wout = g_squares
    g_pool_w = shard[r:r + pool_rows].reshape(n_groups * n_pool, POOL_GW)

    big = [("w_ada", w_ada[0], g_w_ada, m_w_ada[0], v_w_ada[0]),
           ("w_in", w_in[0], g_w_in, m_w_in[0], v_w_in[0]),
           ("pool_w", pool_w[0].reshape(n_groups * n_pool, POOL_GW), g_pool_w,
            m_pool_w[0].reshape(n_groups * n_pool, POOL_GW), v_pool_w[0].reshape(n_groups * n_pool, POOL_GW)),
           ("glu_w", glu_w[0], g_glu_w, m_glu_w[0], v_glu_w[0]),
           ("w_branch_pool", w_branch_pool[0], g_wbp, m_w_branch_pool[0], v_w_branch_pool[0]),
           ("w_branch_ssm", w_branch_ssm[0], g_wbs, m_w_branch_ssm[0], v_w_branch_ssm[0]),
           ("w_out", w_out[0], g_wout, m_w_out[0], v_w_out[0])]
    out = {}
    for name, w_, g_, m_, v_ in big:
        d_, nm_, nv_ = _adamw(w_, g_, m_, v_, "adamw_" + name)
        out[name] = (g_, d_, nm_, nv_)

    g_b_re = g_b_re_t.transpose(1, 2, 0)
    g_b_im = g_b_im_t.transpose(1, 2, 0)
    small = [("b_ada", b_ada, g_b_ada, m_b_ada, v_b_ada),
             ("norm_pre", norm_pre, g_norm_pre, m_norm_pre, v_norm_pre),
             ("norm_post", norm_post, g_norm_post, m_norm_post, v_norm_post),
             ("pool_scale", pool_scale, g_pscale, m_pool_scale, v_pool_scale),
             ("ssm_a_re", ssm_a_re, g_a_re, m_ssm_a_re, v_ssm_a_re),
             ("ssm_a_im", ssm_a_im, g_a_im, m_ssm_a_im, v_ssm_a_im),
             ("ssm_log_dt", ssm_log_dt, g_log_dt, m_ssm_log_dt, v_ssm_log_dt),
             ("ssm_b_re", ssm_b_re, g_b_re, m_ssm_b_re, v_ssm_b_re),
             ("ssm_b_im", ssm_b_im, g_b_im, m_ssm_b_im, v_ssm_b_im),
             ("ssm_c_re", ssm_c_re, g_c_re, m_ssm_c_re, v_ssm_c_re),
             ("ssm_c_im", ssm_c_im, g_c_im, m_ssm_c_im, v_ssm_c_im),
             ("ssm_d", ssm_d, g_dskip, m_ssm_d, v_ssm_d),
             ("glu_b", glu_b, g_glu_b, m_glu_b, v_glu_b)]
    small = [(name, w_, g_.reshape(w_.shape), m_, v_) for name, w_, g_, m_, v_ in small]
    updates = _adamw_small([t[1:] for t in small])
    for (name, _, g_, _, _), (d_, nm_, nv_) in zip(small, updates):
        out[name] = (g_, d_, nm_, nv_)

    order = ["w_ada", "b_ada", "norm_pre", "norm_post", "w_in", "pool_w", "pool_scale", "ssm_a_re", "ssm_a_im",
             "ssm_log_dt", "ssm_b_re", "ssm_b_im", "ssm_c_re", "ssm_c_im", "ssm_d", "glu_w", "glu_b", "w_branch_pool",
             "w_branch_ssm", "w_out"]
    ref_shape = dict(w_ada=w_ada.shape, w_in=w_in.shape, pool_w=pool_w.shape, glu_w=glu_w.shape,
                     w_branch_pool=w_branch_pool.shape, w_branch_ssm=w_branch_ssm.shape, w_out=w_out.shape)
    for name, w_, _, _, _ in small:
        ref_shape[name] = w_.shape
    results = [loss, res["grad_x"][None]]
    for k in range(4):
        results += [out[name][k].reshape(ref_shape[name]) for name in order]
    return tuple(results)
```

```python
import functools
import math

import numpy as np
import jax
import jax.numpy as jnp
from jax import lax
from jax.experimental import pallas as pl
from jax.experimental.pallas import tpu as pltpu

F32 = jnp.float32
BF16 = jnp.bfloat16
MESH_ID = pl.DeviceIdType.MESH

D_MODEL = 1024
LANES = 128
SUBLANES = 8
SSM_G, SSM_P, SSM_H = 64, 64, 16
LANE_BLOCKS = D_MODEL // LANES
GROUPS_PER_BLOCK = LANES // SSM_H
STATE_W = GROUPS_PER_BLOCK * SSM_P
STATE_ALL = SSM_G * SSM_P
POOL_WINDOWS = (2, 4, 8, 16)
POOL_GW = D_MODEL // len(POOL_WINDOWS)
HALO = 16
RMS_EPS = 1e-6
N_CHIPS = 4
N_DEV = 8

SCAN_CHUNK = 1024
SCAN_BLOCKS = 1
ROW_CHUNK = 512
ROW_CHUNK_WIDE = 512
PROJ_ROWS = 1024
VMEM_LIMIT_BYTES = 56 * 1024 * 1024

ADAM_BLOCK_BYTES = 1 << 20
ADAM_LR, ADAM_B1, ADAM_B2, ADAM_EPS, ADAM_WD, ADAM_STEP = 0.001, 0.9, 0.999, 1e-08, 0.01, 10

_GELU_C0 = math.sqrt(2.0 / math.pi)
_GELU_C1 = 0.044715


def _cparams(*sem):
    if sem:
        return pltpu.CompilerParams(dimension_semantics=sem, vmem_limit_bytes=VMEM_LIMIT_BYTES)
    return pltpu.CompilerParams(vmem_limit_bytes=VMEM_LIMIT_BYTES)


def _sigmoid(v):
    return jax.nn.sigmoid(v)


def _silu(v):
    return v * _sigmoid(v)


def _dsilu(v):
    s = _sigmoid(v)
    return s * (1.0 + v * (1.0 - s))


def _gelu(v):
    return v * (0.5 * (1.0 + jnp.tanh(_GELU_C0 * v * (1.0 + _GELU_C1 * (v * v)))))


def _gelu_and_grad(v):
    v2 = v * v
    t = jnp.tanh(_GELU_C0 * v * (1.0 + _GELU_C1 * v2))
    half = 0.5 * (1.0 + t)
    grad = half + (0.5 * _GELU_C0) * v * (1.0 - t * t) * (1.0 + (3.0 * _GELU_C1) * v2)
    return v * half, grad


def _silu_and_grad(v):
    s = _sigmoid(v)
    return v * s, s * (1.0 + v * (1.0 - s))


def _dot(a, b):
    return lax.dot_general(a, b, (((1,), (0,)), ((), ())), preferred_element_type=F32)


def _dot_nt(a, b):
    return lax.dot_general(a, b, (((1,), (1,)), ((), ())), preferred_element_type=F32)


def _dot_tn(a, b):
    return lax.dot_general(a, b, (((0,), (0,)), ((), ())), preferred_element_type=F32)


def _acc8(v):
    return v.reshape(v.shape[0] // SUBLANES, SUBLANES, v.shape[1]).sum(axis=0)


class _Ride:
    def __init__(self, inputs, out_shapes, scratch, start, wait):
        self.inputs, self.out_shapes, self.scratch, self.start, self.wait = inputs, out_shapes, scratch, start, wait


def _mm(a_parts, b_parts, *, name, ta=False, tb=False, out_dtype=F32, bm=512, bn=512, bk=512, ride=None):
    a_parts, b_parts = list(a_parts), list(b_parts)
    if ta:
        assert len(a_parts) == 1
        k_dim, m_dim = a_parts[0].shape
    else:
        m_dim = a_parts[0].shape[0]
        k_dim = sum(a.shape[1] for a in a_parts)
    if tb:
        assert len(b_parts) == 1
        n_dim = b_parts[0].shape[0]
    else:
        n_dim = sum(b.shape[1] for b in b_parts)
    bm, bn, bk = min(bm, m_dim), min(bn, n_dim), min(bk, k_dim)
    nm, nn, nk = m_dim // bm, n_dim // bn, k_dim // bk
    a_ranges, off = [], 0
    for a in a_parts:
        cnt = (a.shape[0] if ta else a.shape[1]) // bk
        a_ranges.append((off, cnt))
        off += cnt
    b_ranges, off = [], 0
    for b in b_parts:
        cnt = (b.shape[0] if tb else b.shape[1]) // bn
        b_ranges.append((off, cnt))
        off += cnt

    def a_spec(off, cnt):
        if ta:
            return pl.BlockSpec((bk, bm), lambda i, n, k: (k, i))
        return pl.BlockSpec((bm, bk), lambda i, n, k: (i, jnp.clip(k - off, 0, cnt - 1)))

    def b_spec(off, cnt):
        if tb:
            return pl.BlockSpec((bn, bk), lambda i, n, k: (n, k))
        return pl.BlockSpec((bk, bn), lambda i, n, k: (k, jnp.clip(n - off, 0, cnt - 1)))

    na, nb = len(a_parts), len(b_parts)
    dims = (((0 if ta else 1,), (1 if tb else 0,)), ((), ()))

    def kern_single(a_ref, b_ref, o_ref):
        o_ref[...] = lax.dot_general(a_ref[...].astype(BF16), b_ref[...].astype(BF16), dims,
                                     preferred_element_type=F32).astype(out_dtype)

    if na == 1 and nb == 1 and nk == 1 and not ride:
        return pl.pallas_call(
            kern_single, name=name, grid=(nm, nn),
            in_specs=[pl.BlockSpec((bk, bm), lambda i, n: (0, i)) if ta else pl.BlockSpec((bm, bk), lambda i, n: (i, 0)),
                      pl.BlockSpec((bn, bk), lambda i, n: (n, 0)) if tb else pl.BlockSpec((bk, bn), lambda i, n: (0, n))],
            out_specs=pl.BlockSpec((bm, bn), lambda i, n: (i, n)),
            out_shape=jax.ShapeDtypeStruct((m_dim, n_dim), out_dtype),
            compiler_params=_cparams("parallel", "parallel"),
        )(a_parts[0], b_parts[0])

    n_rin = len(ride.inputs) if ride else 0
    n_rout = len(ride.out_shapes) if ride else 0

    def kern(*refs):
        a_refs, b_refs = refs[:na], refs[na:na + nb]
        rin = refs[na + nb:na + nb + n_rin]
        o_ref = refs[na + nb + n_rin]
        rout = refs[na + nb + n_rin + 1:na + nb + n_rin + 1 + n_rout]
        acc = refs[na + nb + n_rin + 1 + n_rout]
        rsem = refs[na + nb + n_rin + 2 + n_rout:]
        i, n, k = pl.program_id(0), pl.program_id(1), pl.program_id(2)

        if ride:
            @pl.when((i == 0) & (n == 0) & (k == 0))
            def _():
                ride.start(rin, rout, rsem)

        if nk > 1:
            @pl.when(k == 0)
            def _():
                acc[...] = jnp.zeros_like(acc)

        for ja, (koff, kcnt) in enumerate(a_ranges):
            for jb, (noff, ncnt) in enumerate(b_ranges):
                def step(ja=ja, jb=jb):
                    a = a_refs[ja][...].astype(BF16)
                    b = b_refs[jb][...].astype(BF16)
                    prod = lax.dot_general(a, b, dims, preferred_element_type=F32)
                    if nk > 1:
                        acc[...] += prod
                    else:
                        o_ref[...] = prod.astype(out_dtype)

                if na == 1 and nb == 1:
                    step()
                else:
                    cond = (k >= koff) & (k < koff + kcnt) & (n >= noff) & (n < noff + ncnt)
                    pl.when(cond)(step)

        if nk > 1:
            @pl.when(k == nk - 1)
            def _():
                o_ref[...] = acc[...].astype(out_dtype)

        if ride:
            @pl.when((i == nm - 1) & (n == nn - 1) & (k == nk - 1))
            def _():
                ride.wait(rin, rout, rsem)

    any_spec = pl.BlockSpec(memory_space=pl.ANY)
    out_spec = pl.BlockSpec((bm, bn), lambda i, n, k: (i, n))
    out_shape = jax.ShapeDtypeStruct((m_dim, n_dim), out_dtype)
    acc_shape = pltpu.VMEM((bm, bn) if nk > 1 else (SUBLANES, LANES), F32)
    if not ride:
        return pl.pallas_call(
            kern, name=name, grid=(nm, nn, nk),
            in_specs=[a_spec(*r) for r in a_ranges] + [b_spec(*r) for r in b_ranges],
            out_specs=out_spec, out_shape=out_shape, scratch_shapes=[acc_shape],
            compiler_params=_cparams("parallel", "parallel", "arbitrary"),
        )(*a_parts, *b_parts)
    return pl.pallas_call(
        kern, name=name, grid=(nm, nn, nk),
        in_specs=[a_spec(*r) for r in a_ranges] + [b_spec(*r) for r in b_ranges] + [any_spec] * n_rin,
        out_specs=(out_spec,) + (any_spec,) * n_rout, out_shape=(out_shape,) + tuple(ride.out_shapes),
        scratch_shapes=[acc_shape] + list(ride.scratch),
        compiler_params=_cparams("arbitrary", "arbitrary", "arbitrary"),
    )(*a_parts, *b_parts, *ride.inputs)


def _ssm_param_fn(a_re, a_im, log_dt, b_re, b_im):
    dt = jnp.exp(log_dt)
    lam_re = jnp.minimum(a_re, -1e-4)
    lam_im = a_im
    mag = jnp.exp(lam_re * dt)
    abar_re = mag * jnp.cos(lam_im * dt)
    abar_im = mag * jnp.sin(lam_im * dt)
    den = lam_re * lam_re + lam_im * lam_im
    num_re = abar_re - 1.0
    f_re = (num_re * lam_re + abar_im * lam_im) / den
    f_im = (abar_im * lam_re - num_re * lam_im) / den
    bb_re = f_re * b_re - f_im * b_im
    bb_im = f_re * b_im + f_im * b_re
    return abar_re, abar_im, bb_re, bb_im


def _ssm_params(a_re, a_im, log_dt, b_re_t, b_im_t):
    def kern(are, aim, ldt, bre, bim, o_ar, o_ai, o_br, o_bi):
        ar, ai, br, bi = _ssm_param_fn(are[...], aim[...], ldt[...], bre[...], bim[...])
        o_ar[...] = ar
        o_ai[...] = ai
        o_br[...] = br
        o_bi[...] = bi

    gp = jax.ShapeDtypeStruct((SSM_G, SSM_P), F32)
    hgp = jax.ShapeDtypeStruct((SSM_H, SSM_G, SSM_P), F32)
    return pl.pallas_call(kern, name="ssm_params", out_shape=(gp, gp, hgp, hgp), compiler_params=_cparams())(
        a_re, a_im, log_dt, b_re_t, b_im_t)


def _ssm_params_bwd(a_re, a_im, log_dt, b_re_t, b_im_t, d_ar, d_ai, d_bbr, d_bbi):
    def kern(are, aim, ldt, bre, bim, dar, dai, dbr, dbi, o_are, o_aim, o_ldt, o_bre, o_bim):
        prim = (are[...], aim[...], ldt[...], bre[...], bim[...])
        _, vjp = jax.vjp(_ssm_param_fn, *prim)
        g = vjp((dar[...], dai[...], dbr[...], dbi[...]))
        o_are[...] = g[0]
        o_aim[...] = g[1]
        o_ldt[...] = g[2]
        o_bre[...] = g[3]
        o_bim[...] = g[4]

    gp = jax.ShapeDtypeStruct((SSM_G, SSM_P), F32)
    g1 = jax.ShapeDtypeStruct((SSM_G, 1), F32)
    hgp = jax.ShapeDtypeStruct((SSM_H, SSM_G, SSM_P), F32)
    return pl.pallas_call(kern, name="ssm_params_bwd", out_shape=(gp, gp, g1, hgp, hgp), compiler_params=_cparams())(
        a_re, a_im, log_dt, b_re_t, b_im_t, d_ar, d_ai, d_bbr, d_bbi)


def _pow_tables(abar_re, abar_im, tc):
    ls = tc // SUBLANES

    def kern(ar_ref, ai_ref, fr_ref, fi_ref, rr_ref, ri_ref):
        a_re = jnp.broadcast_to(ar_ref[...], (SUBLANES, STATE_W))
        a_im = jnp.broadcast_to(ai_ref[...], (SUBLANES, STATE_W))
        p_re, p_im = a_re, a_im
        for i in range(ls):
            fwd = pl.ds(SUBLANES * i, SUBLANES)
            rev = pl.ds(SUBLANES * (ls - 1 - i), SUBLANES)
            fr_ref[fwd, :] = p_re
            fi_ref[fwd, :] = p_im
            rr_ref[rev, :] = p_re
            ri_ref[rev, :] = p_im
            p_re, p_im = p_re * a_re - p_im * a_im, p_re * a_im + p_im * a_re

    vec = pl.BlockSpec((1, STATE_W), lambda b: (0, b))
    tab = pl.BlockSpec((tc, STATE_W), lambda b: (0, b))
    shp = jax.ShapeDtypeStruct((tc, STATE_ALL), F32)
    return pl.pallas_call(
        kern, name="pow_tables", grid=(LANE_BLOCKS,), in_specs=[vec, vec], out_specs=(tab, tab, tab, tab),
        out_shape=(shp, shp, shp, shp), compiler_params=_cparams("parallel"))(abar_re, abar_im)


def _mod_kernel(c_row, w_ada_bf, b_ada):
    def kern(c_ref, w_ref, b_ref, m_ref, s_ref):
        cv = c_ref[...]
        sc = _silu(cv)
        s_ref[...] = sc
        lhs = jnp.broadcast_to(sc, (SUBLANES, D_MODEL)).astype(BF16)
        m_ref[...] = _dot(lhs, w_ref[...]) + b_ref[...]

    return pl.pallas_call(
        kern, name="ada_mod",
        out_shape=(jax.ShapeDtypeStruct((SUBLANES, 3 * D_MODEL), F32), jax.ShapeDtypeStruct((1, D_MODEL), F32)),
        compiler_params=_cparams())(c_row, w_ada_bf, b_ada)


def _row_spec(tr, width=D_MODEL, col=0):
    return pl.BlockSpec((tr, width), lambda c: (c, col))


def _vec_spec(width=D_MODEL):
    return pl.BlockSpec((1, width), lambda c: (0, 0))


def _col_spec(tr):
    return pl.BlockSpec((D_MODEL, tr), lambda c: (0, c))


def _in_norm(x, g1, scale, shift):
    seq = x.shape[0]
    tr = min(ROW_CHUNK_WIDE, seq)

    def kern(x_ref, g_ref, sc_ref, sh_ref, h_ref, ht_ref):
        xv = x_ref[...]
        r = lax.rsqrt(jnp.mean(xv * xv, axis=-1, keepdims=True) + RMS_EPS)
        h = ((xv * r) * g_ref[...]) * (1.0 + sc_ref[...]) + sh_ref[...]
        h_ref[...] = h.astype(BF16)
        ht_ref[...] = h.T.astype(BF16)

    return pl.pallas_call(
        kern, name="in_norm", grid=(seq // tr,),
        in_specs=[_row_spec(tr), _vec_spec(), _vec_spec(), _vec_spec()], out_specs=(_row_spec(tr), _col_spec(tr)),
        out_shape=(jax.ShapeDtypeStruct((seq, D_MODEL), BF16), jax.ShapeDtypeStruct((D_MODEL, seq), BF16)),
        compiler_params=_cparams("parallel"))(x, g1, scale, shift)


PAD = SUBLANES


def _window_sums(src, cols, w, bufs, rows, ahead):
    cur, cur_cols, step, k = src, cols, 1, 0
    data = pl.ds(PAD, rows)
    while step < w:
        dst = bufs[k % 2]
        dst[data, :] = cur[data, cur_cols] + cur[pl.ds(PAD + (step if ahead else -step), rows), cur_cols]
        cur, cur_cols, step, k = dst, slice(None), 2 * step, k + 1
    return cur, cur_cols


def _in_norm_proj_own(x, g1, scale, shift, w_own, chip, ride):
    seq, n_own = x.shape[0], w_own.shape[1]
    tr = min(PROJ_ROWS, seq)
    nc = seq // tr
    n_rin, n_rout = len(ride.inputs), len(ride.out_shapes)

    def kern(chip_ref, x_ref, g_ref, sc_ref, sh_ref, w_ref, *rest):
        rin, (h_ref, ht_ref, p_ref) = rest[:n_rin], rest[n_rin:n_rin + 3]
        rout, rsem = rest[n_rin + 3:n_rin + 3 + n_rout], rest[n_rin + 3 + n_rout:]
        c = pl.program_id(0)

        @pl.when(c == 0)
        def _():
            ride.start(rin, rout, rsem)

        xv = x_ref[...]
        r = lax.rsqrt(jnp.mean(xv * xv, axis=-1, keepdims=True) + RMS_EPS)
        h = ((xv * r) * g_ref[...]) * (1.0 + sc_ref[...]) + sh_ref[...]
        hb = h.astype(BF16)
        h_ref[...] = hb
        ht_ref[...] = h.T.astype(BF16)
        p_ref[...] = _dot(hb, w_ref[...]).astype(BF16)

        @pl.when(c == nc - 1)
        def _():
            ride.wait(rin, rout, rsem)

    vec = pl.BlockSpec((1, D_MODEL), lambda c, k: (0, 0))
    return pl.pallas_call(
        kern, name="in_norm_proj_own",
        grid_spec=pltpu.PrefetchScalarGridSpec(
            num_scalar_prefetch=1, grid=(nc,),
            in_specs=[pl.BlockSpec((tr, D_MODEL), lambda c, k: (c, 0)), vec, vec, vec,
                      pl.BlockSpec((D_MODEL, n_own), lambda c, k: (0, 0))] + [_ANY] * n_rin,
            out_specs=(pl.BlockSpec((tr, D_MODEL), lambda c, k: (c, 0)), pl.BlockSpec((D_MODEL, tr), lambda c, k: (0, c)),
                       pl.BlockSpec((tr, n_own), lambda c, k: (c, k[0]))) + (_ANY,) * n_rout,
            scratch_shapes=list(ride.scratch)),
        out_shape=(jax.ShapeDtypeStruct((seq, D_MODEL), BF16), jax.ShapeDtypeStruct((D_MODEL, seq), BF16),
                   jax.ShapeDtypeStruct((seq, N_CHIPS * n_own), BF16)) + tuple(ride.out_shapes),
        compiler_params=_cparams("arbitrary"))(chip, x, g1, scale, shift, w_own, *ride.inputs)


def _proj_rest(h, w_blocks, proj, chip, ride):
    seq, n_own = h.shape[0], w_blocks.shape[2]
    tr = min(PROJ_ROWS, seq)
    nm, nn = seq // tr, N_CHIPS - 1
    n_rin, n_rout = len(ride.inputs), len(ride.out_shapes)

    def kern(chip_ref, h_ref, w_ref, _, *rest):
        rin, p_ref = rest[:n_rin], rest[n_rin]
        rout, rsem = rest[n_rin + 1:n_rin + 1 + n_rout], rest[n_rin + 1 + n_rout:]
        i, n = pl.program_id(0), pl.program_id(1)

        @pl.when((i == 0) & (n == 0))
        def _():
            ride.start(rin, rout, rsem)

        p_ref[...] = _dot(h_ref[...], w_ref[0]).astype(BF16)

        @pl.when((i == nm - 1) & (n == nn - 1))
        def _():
            ride.wait(rin, rout, rsem)

    def other(n, k):
        return (k[0] + 1 + n) % N_CHIPS

    return pl.pallas_call(
        kern, name="proj_rest",
        grid_spec=pltpu.PrefetchScalarGridSpec(
            num_scalar_prefetch=1, grid=(nm, nn),
            in_specs=[pl.BlockSpec((tr, D_MODEL), lambda i, n, k: (i, 0)),
                      pl.BlockSpec((1, D_MODEL, n_own), lambda i, n, k: (other(n, k), 0, 0)), _ANY] + [_ANY] * n_rin,
            out_specs=(pl.BlockSpec((tr, n_own), lambda i, n, k: (i, other(n, k))),) + (_ANY,) * n_rout,
            scratch_shapes=list(ride.scratch)),
        out_shape=(jax.ShapeDtypeStruct(proj.shape, BF16),) + tuple(ride.out_shapes),
        input_output_aliases={3: 0},
        compiler_params=_cparams("arbitrary", "arbitrary"))(chip, h, w_blocks, proj, *ride.inputs)


def _pool_windows(ext, bufs, pos, g, w, tr):
    cols = pl.ds(g * POOL_GW, POOL_GW)
    chunk = pl.ds(PAD + HALO, tr)
    cur = ext[chunk, cols]
    win, win_cols = _window_sums(ext, cols, w, bufs, HALO + tr, ahead=False)
    cnt = jnp.minimum(pos + 1, w).astype(F32)
    return win[chunk, win_cols] / cnt - cur


def _zero_pads(refs, rows):
    for ref in refs:
        ref[0:PAD, :] = jnp.zeros((PAD, ref.shape[1]), F32)
        ref[PAD + rows:, :] = jnp.zeros((PAD, ref.shape[1]), F32)


def _pool_fwd(proj, pool_w_bf, pscale):
    seq = proj.shape[0]
    tr = min(ROW_CHUNK_WIDE, seq)
    hb = tr // HALO

    def kern(up_ref, halo_ref, zp_ref, pw_ref, ps_ref, y_ref, yt_ref, ext, buf_a, buf_b):
        c = pl.program_id(0)
        _zero_pads((ext, buf_a, buf_b), HALO + tr)
        ext[pl.ds(PAD, HALO), :] = jnp.where(c > 0, halo_ref[...].astype(F32), 0.0)
        ext[pl.ds(PAD + HALO, tr), :] = up_ref[...].astype(F32)
        pos = c * tr + lax.broadcasted_iota(jnp.int32, (tr, POOL_GW), 0)
        for g, w in enumerate(POOL_WINDOWS):
            cols = pl.ds(g * POOL_GW, POOL_GW)
            pooled = _pool_windows(ext, (buf_a, buf_b), pos, g, w, tr)
            mixed = _dot(pooled.astype(BF16), pw_ref[g])
            y = mixed * ps_ref[:, cols] * _silu(zp_ref[:, cols].astype(F32))
            y_ref[:, cols] = y.astype(BF16)
            yt_ref[cols, :] = y.T.astype(BF16)

    return pl.pallas_call(
        kern, name="pool_fwd", grid=(seq // tr,),
        in_specs=[_row_spec(tr, col=0),
                  pl.BlockSpec((HALO, D_MODEL), lambda c: (jnp.maximum(c * hb - 1, 0), 0)),
                  _row_spec(tr, col=1),
                  pl.BlockSpec((len(POOL_WINDOWS), POOL_GW, POOL_GW), lambda c: (0, 0, 0)),
                  _vec_spec()],
        out_specs=(_row_spec(tr), _col_spec(tr)),
        out_shape=(jax.ShapeDtypeStruct((seq, D_MODEL), BF16), jax.ShapeDtypeStruct((D_MODEL, seq), BF16)),
        scratch_shapes=[pltpu.VMEM((tr + HALO + 2 * PAD, D_MODEL), F32), pltpu.VMEM((tr + HALO + 2 * PAD, POOL_GW), F32),
                        pltpu.VMEM((tr + HALO + 2 * PAD, POOL_GW), F32)],
        compiler_params=_cparams("parallel"))(proj, proj, proj, pool_w_bf, pscale)


def _pool_bwd(proj, dyp, pool_w_bf, pscale, dproj):
    seq = proj.shape[0]
    tr = min(ROW_CHUNK_WIDE, seq)
    hb = tr // HALO
    nc = seq // tr
    n_halo = seq // HALO

    def kern(up_ref, halo_ref, zp_ref, zpn_ref, dyp_ref, dypn_ref, pw_ref, ps_ref, _,
             d01_ref, dpw_ref, dps_ref, ext, dpn, buf_a, buf_b, acc_pw, acc_ps):
        c = pl.program_id(0)

        @pl.when(c == 0)
        def _():
            acc_pw[...] = jnp.zeros_like(acc_pw)
            acc_ps[...] = jnp.zeros_like(acc_ps)

        _zero_pads((ext, dpn, buf_a, buf_b), HALO + tr)
        ext[pl.ds(PAD, HALO), :] = jnp.where(c > 0, halo_ref[...].astype(F32), 0.0)
        ext[pl.ds(PAD + HALO, tr), :] = up_ref[...].astype(F32)
        pos = c * tr + lax.broadcasted_iota(jnp.int32, (tr, POOL_GW), 0)
        pos_n = (c + 1) * tr + lax.broadcasted_iota(jnp.int32, (HALO, POOL_GW), 0)
        has_next = c < nc - 1
        for g, w in enumerate(POOL_WINDOWS):
            cols = pl.ds(g * POOL_GW, POOL_GW)
            pooled_bf = _pool_windows(ext, (buf_a, buf_b), pos, g, w, tr).astype(BF16)
            wg = pw_ref[g]
            mixed = _dot(pooled_bf, wg)
            zp = zp_ref[:, cols].astype(F32)
            sz = _silu(zp)
            dyp_g = dyp_ref[:, cols].astype(F32)
            ps = ps_ref[:, cols]
            dmixed = (dyp_g * ps * sz).astype(BF16)
            acc_ps[:, cols] += _acc8(dyp_g * mixed * sz)
            d01_ref[:, pl.ds(D_MODEL + g * POOL_GW, POOL_GW)] = (dyp_g * mixed * ps * _dsilu(zp)).astype(BF16)
            acc_pw[g] += _dot_tn(pooled_bf, dmixed)
            dpooled = _dot_nt(dmixed, wg)
            dmixed_n = (jnp.where(has_next, dypn_ref[:, cols].astype(F32), 0.0) * ps * _silu(zpn_ref[:, cols].astype(F32))).astype(BF16)
            dpooled_n = _dot_nt(dmixed_n, wg)
            dpn[pl.ds(PAD, tr), :] = dpooled / jnp.minimum(pos + 1, w).astype(F32)
            dpn[pl.ds(PAD + tr, HALO), :] = dpooled_n / jnp.minimum(pos_n + 1, w).astype(F32)
            win, _ = _window_sums(dpn, slice(None), w, (buf_a, buf_b), tr + HALO, ahead=True)
            d01_ref[:, cols] = (win[pl.ds(PAD, tr), :] - dpooled).astype(BF16)

        @pl.when(c == nc - 1)
        def _():
            dpw_ref[...] = acc_pw[...]
            dps_ref[...] = jnp.sum(acc_ps[...], axis=0, keepdims=True)

    nxt = lambda c: (jnp.minimum((c + 1) * hb, n_halo - 1), 0)
    nxt1 = lambda c: (jnp.minimum((c + 1) * hb, n_halo - 1), 1)
    return pl.pallas_call(
        kern, name="pool_bwd", grid=(nc,),
        in_specs=[_row_spec(tr, col=0),
                  pl.BlockSpec((HALO, D_MODEL), lambda c: (jnp.maximum(c * hb - 1, 0), 0)),
                  _row_spec(tr, col=1),
                  pl.BlockSpec((HALO, D_MODEL), nxt1),
                  _row_spec(tr),
                  pl.BlockSpec((HALO, D_MODEL), nxt),
                  pl.BlockSpec((len(POOL_WINDOWS), POOL_GW, POOL_GW), lambda c: (0, 0, 0)),
                  _vec_spec(), _ANY],
        out_specs=(pl.BlockSpec((tr, 2 * D_MODEL), lambda c: (c, 0)),
                   pl.BlockSpec((len(POOL_WINDOWS), POOL_GW, POOL_GW), lambda c: (0, 0, 0)),
                   _vec_spec()),
        out_shape=(jax.ShapeDtypeStruct(dproj.shape, BF16),
                   jax.ShapeDtypeStruct((len(POOL_WINDOWS), POOL_GW, POOL_GW), F32),
                   jax.ShapeDtypeStruct((1, D_MODEL), F32)),
        scratch_shapes=[pltpu.VMEM((tr + HALO + 2 * PAD, D_MODEL), F32)]
        + [pltpu.VMEM((tr + HALO + 2 * PAD, POOL_GW), F32)] * 3
        + [pltpu.VMEM((len(POOL_WINDOWS), POOL_GW, POOL_GW), F32), pltpu.VMEM((SUBLANES, D_MODEL), F32)],
        input_output_aliases={8: 0},
        compiler_params=_cparams("arbitrary"))(proj, proj, proj, proj, dyp, dyp, pool_w_bf, pscale, dproj)


def _glu_fwd(ys, proj, glu_w_bf, glu_b):
    seq = ys.shape[0]
    tr = min(ROW_CHUNK_WIDE, seq)

    def kern(ys_ref, zs_ref, w_ref, b_ref, o_ref, ot_ref):
        yg = _gelu(ys_ref[...])
        q = _dot(yg.astype(BF16), w_ref[...]) + b_ref[...]
        y = yg * _sigmoid(q) * _silu(zs_ref[...].astype(F32))
        o_ref[...] = y.astype(BF16)
        ot_ref[...] = y.T.astype(BF16)

    return pl.pallas_call(
        kern, name="glu_fwd", grid=(seq // tr,),
        in_specs=[_row_spec(tr), _row_spec(tr, col=3), pl.BlockSpec((D_MODEL, D_MODEL), lambda c: (0, 0)), _vec_spec()],
        out_specs=(_row_spec(tr), _col_spec(tr)),
        out_shape=(jax.ShapeDtypeStruct((seq, D_MODEL), BF16), jax.ShapeDtypeStruct((D_MODEL, seq), BF16)),
        compiler_params=_cparams("parallel"))(ys, proj, glu_w_bf, glu_b)


def _glu_bwd(ys, proj, dyssm, glu_w_bf, glu_b, dproj):
    seq = ys.shape[0]
    tr = min(ROW_CHUNK_WIDE, seq)
    nc = seq // tr

    def kern(ys_ref, zs_ref, dy_ref, w_ref, b_ref, _, dys_ref, dzs_ref, dq_ref, yg_ref, db_ref, acc_b):
        c = pl.program_id(0)

        @pl.when(c == 0)
        def _():
            acc_b[...] = jnp.zeros_like(acc_b)

        yg, dgelu = _gelu_and_grad(ys_ref[...])
        yg_bf = yg.astype(BF16)
        q = _dot(yg_bf, w_ref[...]) + b_ref[...]
        sg = _sigmoid(q)
        silu_z, dsilu_z = _silu_and_grad(zs_ref[...].astype(F32))
        dyv = dy_ref[...].astype(F32)
        dyglu = dyv * silu_z
        yglu = yg * sg
        dzs_ref[...] = (dyv * yglu * dsilu_z).astype(BF16)
        dq = dyglu * yglu * (1.0 - sg)
        dq_bf = dq.astype(BF16)
        acc_b[...] += _acc8(dq)
        dyg = dyglu * sg + _dot_nt(dq_bf, w_ref[...])
        dys_ref[...] = dyg * dgelu
        dq_ref[...] = dq_bf
        yg_ref[...] = yg.T.astype(BF16)

        @pl.when(c == nc - 1)
        def _():
            db_ref[...] = jnp.sum(acc_b[...], axis=0, keepdims=True)

    bf = jax.ShapeDtypeStruct((seq, D_MODEL), BF16)
    return pl.pallas_call(
        kern, name="glu_bwd", grid=(nc,),
        in_specs=[_row_spec(tr), _row_spec(tr, col=3), _row_spec(tr),
                  pl.BlockSpec((D_MODEL, D_MODEL), lambda c: (0, 0)), _vec_spec(), _ANY],
        out_specs=(_row_spec(tr), _row_spec(tr, col=3), _row_spec(tr), _col_spec(tr), _vec_spec()),
        out_shape=(jax.ShapeDtypeStruct((seq, D_MODEL), F32), jax.ShapeDtypeStruct(dproj.shape, BF16), bf,
                   jax.ShapeDtypeStruct((D_MODEL, seq), BF16), jax.ShapeDtypeStruct((1, D_MODEL), F32)),
        scratch_shapes=[pltpu.VMEM((SUBLANES, D_MODEL), F32)],
        input_output_aliases={5: 1},
        compiler_params=_cparams("arbitrary"))(ys, proj, dyssm, glu_w_bf, glu_b, dproj)


def _out_fwd_bwd(ypool, yssm, proj, x, tgt, gate, g2, wbp_bf, wbs_bf, wout_bf):
    seq = x.shape[0]
    tr = min(ROW_CHUNK, seq)
    nc = seq // tr

    def kern(yp_ref, ysm_ref, gp_ref, gs_ref, x_ref, t_ref, gate_ref, g2_ref, wbp_ref, wbs_ref, wo_ref,
             dy_ref, dyp_ref, dys_ref, d45_ref, mb_ref, dob_ref, dbp_ref, dbs_ref, loss_ref, dgate_ref, dg2_ref,
             acc_l, acc_gate, acc_g2):
        c = pl.program_id(0)

        @pl.when(c == 0)
        def _():
            acc_l[...] = jnp.zeros_like(acc_l)
            acc_gate[...] = jnp.zeros_like(acc_gate)
            acc_g2[...] = jnp.zeros_like(acc_g2)

        bp = _dot(yp_ref[...], wbp_ref[...])
        bs = _dot(ysm_ref[...], wbs_ref[...])
        sp = _sigmoid(gp_ref[...].astype(F32))
        ss = _sigmoid(gs_ref[...].astype(F32))
        merged = sp * bp + ss * bs
        mb = merged.astype(BF16)
        out = _dot(mb, wo_ref[...])
        r2 = lax.rsqrt(jnp.mean(out * out, axis=-1, keepdims=True) + RMS_EPS)
        oh = out * r2
        gate_v, g2_v = gate_ref[...], g2_ref[...]
        ohg = oh * g2_v
        diff = (x_ref[...] + gate_v * ohg) - t_ref[...]
        acc_l[...] += _acc8(diff * diff)
        dyv = diff * (1.0 / D_MODEL)
        dy_ref[...] = dyv
        dy_oh = dyv * oh
        acc_gate[...] += _acc8(dy_oh * g2_v)
        acc_g2[...] += _acc8(dy_oh * gate_v)
        gg = gate_v * g2_v
        doh = dyv * gg
        dout = r2 * (doh - oh * jnp.mean(dy_oh * gg, axis=-1, keepdims=True))
        dob = dout.astype(BF16)
        dmerged = _dot_nt(dob, wo_ref[...])
        dbp_f = dmerged * sp
        dbs_f = dmerged * ss
        dbp = dbp_f.astype(BF16)
        dbs = dbs_f.astype(BF16)
        d45_ref[:, 0:D_MODEL] = (dbp_f * bp * (1.0 - sp)).astype(BF16)
        d45_ref[:, D_MODEL:] = (dbs_f * bs * (1.0 - ss)).astype(BF16)
        dyp_ref[...] = _dot_nt(dbp, wbp_ref[...]).astype(BF16)
        dys_ref[...] = _dot_nt(dbs, wbs_ref[...]).astype(BF16)
        mb_ref[...] = merged.T.astype(BF16)
        dob_ref[...] = dob
        dbp_ref[...] = dbp
        dbs_ref[...] = dbs

        @pl.when(c == nc - 1)
        def _():
            tot = jnp.sum(acc_l[...], axis=0, keepdims=True)
            loss_ref[...] = jnp.sum(tot, axis=1, keepdims=True) * (0.5 / D_MODEL)
            dgate_ref[...] = jnp.sum(acc_gate[...], axis=0, keepdims=True)
            dg2_ref[...] = jnp.sum(acc_g2[...], axis=0, keepdims=True)

    wspec = pl.BlockSpec((D_MODEL, D_MODEL), lambda c: (0, 0))
    f32 = jax.ShapeDtypeStruct((seq, D_MODEL), F32)
    bf = jax.ShapeDtypeStruct((seq, D_MODEL), BF16)
    vec = jax.ShapeDtypeStruct((1, D_MODEL), F32)
    acc = pltpu.VMEM((SUBLANES, D_MODEL), F32)
    return pl.pallas_call(
        kern, name="out_fwd_bwd", grid=(nc,),
        in_specs=[_row_spec(tr), _row_spec(tr), _row_spec(tr, col=4), _row_spec(tr, col=5), _row_spec(tr), _row_spec(tr),
                  _vec_spec(), _vec_spec(), wspec, wspec, wspec],
        out_specs=(_row_spec(tr), _row_spec(tr), _row_spec(tr), pl.BlockSpec((tr, 2 * D_MODEL), lambda c: (c, 2)),
                   _col_spec(tr), _row_spec(tr), _row_spec(tr), _row_spec(tr),
                   pl.BlockSpec((1, 1), lambda c: (0, 0)), _vec_spec(), _vec_spec()),
        out_shape=(f32, bf, bf, jax.ShapeDtypeStruct((seq, proj.shape[1]), BF16),
                   jax.ShapeDtypeStruct((D_MODEL, seq), BF16), bf, bf, bf,
                   jax.ShapeDtypeStruct((1, 1), F32), vec, vec),
        scratch_shapes=[acc, acc, acc],
        compiler_params=_cparams("arbitrary"))(ypool, yssm, proj, proj, x, tgt, gate, g2, wbp_bf, wbs_bf, wout_bf)


def _in_bwd(dh, x, dy, g1, scale):
    seq = x.shape[0]
    tr = min(ROW_CHUNK_WIDE, seq)
    nc = seq // tr

    def kern(dh_ref, x_ref, dy_ref, g_ref, sc_ref, dx_ref, dsh_ref, dsc_ref, dg_ref, a_sh, a_sc, a_g):
        c = pl.program_id(0)

        @pl.when(c == 0)
        def _():
            a_sh[...] = jnp.zeros_like(a_sh)
            a_sc[...] = jnp.zeros_like(a_sc)
            a_g[...] = jnp.zeros_like(a_g)

        xv = x_ref[...]
        r = lax.rsqrt(jnp.mean(xv * xv, axis=-1, keepdims=True) + RMS_EPS)
        xh = xv * r
        g = g_ref[...]
        dhv = dh_ref[...]
        a_sh[...] += _acc8(dhv)
        a_sc[...] += _acc8(dhv * (xh * g))
        dn = dhv * (1.0 + sc_ref[...])
        a_g[...] += _acc8(dn * xh)
        dxh = dn * g
        dx_ref[...] = dy_ref[...] + r * (dxh - xh * jnp.mean(dxh * xh, axis=-1, keepdims=True))

        @pl.when(c == nc - 1)
        def _():
            dsh_ref[...] = jnp.sum(a_sh[...], axis=0, keepdims=True)
            dsc_ref[...] = jnp.sum(a_sc[...], axis=0, keepdims=True)
            dg_ref[...] = jnp.sum(a_g[...], axis=0, keepdims=True)

    vec = jax.ShapeDtypeStruct((1, D_MODEL), F32)
    acc = pltpu.VMEM((SUBLANES, D_MODEL), F32)
    return pl.pallas_call(
        kern, name="in_bwd", grid=(nc,),
        in_specs=[_row_spec(tr), _row_spec(tr), _row_spec(tr), _vec_spec(), _vec_spec()],
        out_specs=(_row_spec(tr), _vec_spec(), _vec_spec(), _vec_spec()),
        out_shape=(jax.ShapeDtypeStruct((seq, D_MODEL), F32), vec, vec, vec),
        scratch_shapes=[acc, acc, acc],
        compiler_params=_cparams("arbitrary"))(dh, x, dy, g1, scale)


SLAB = 2 * SUBLANES


def _local_scan(a_re, a_im, br, bi, xr, xi, row0, ls, reverse, init=None, xb=None):
    if init is None:
        x_re = jnp.zeros((SUBLANES, STATE_W), F32)
        x_im = jnp.zeros((SUBLANES, STATE_W), F32)
    else:
        x_re, x_im = init
    for i in (range(ls - 1, -1, -1) if reverse else range(ls)):
        src = pl.ds(SUBLANES * i, SUBLANES)
        dst = pl.ds(row0 + SUBLANES * i, SUBLANES)
        n_re = a_re * x_re - a_im * x_im + br[src, :]
        n_im = a_re * x_im + a_im * x_re + bi[src, :]
        if xb is not None and i % 2 == 1:
            pair = pl.ds(SUBLANES * (i - 1), SLAB)
            xb[0][pair, :] = jnp.concatenate([x_re, n_re], axis=0).astype(BF16)
            xb[1][pair, :] = jnp.concatenate([x_im, n_im], axis=0).astype(BF16)
        x_re, x_im = n_re, n_im
        xr[dst, :] = x_re
        xi[dst, :] = x_im
    return x_re, x_im


def _two(v):
    return jnp.concatenate([v, v], axis=0)


def _unpermute_rhs(v, sel):
    hi = v.astype(BF16)
    r1 = v - hi.astype(F32)
    mid = r1.astype(BF16)
    lo = (r1 - mid.astype(F32)).astype(BF16)
    return _dot(hi, sel) + _dot(mid, sel) + _dot(lo, sel)


def _scan_specs(tc, nb, rows_of):
    return dict(
        us=pl.BlockSpec((tc, nb * LANES), lambda b, c: (rows_of(c), 2 * D_MODEL // (nb * LANES) + b)),
        tok=pl.BlockSpec((tc, nb * LANES), lambda b, c: (rows_of(c), b)),
        bblk=pl.BlockSpec((nb, LANES, STATE_W), lambda b, c: (b, 0, 0)),
        cblk=pl.BlockSpec((nb, STATE_W, LANES), lambda b, c: (b, 0, 0)),
        vec=pl.BlockSpec((1, nb * STATE_W), lambda b, c: (0, b)),
        tab=pl.BlockSpec((tc, nb * STATE_W), lambda b, c: (0, b)),
        car=pl.BlockSpec((SUBLANES, nb * STATE_W), lambda b, c: (rows_of(c), b)),
        dvec=pl.BlockSpec((1, nb * LANES), lambda b, c: (0, b)))


def _ssm_scan_fwd(proj, bb_re, bb_im, cm_re, cm_im, abar_re, abar_im, pw_re, pw_im, d_skip, tc):
    seq = proj.shape[0]
    nc = seq // tc
    ls = tc // SUBLANES
    nb = SCAN_BLOCKS

    def kern(us_ref, bbr_ref, bbi_ref, cmr_ref, cmi_ref, ar_ref, ai_ref, pwr_ref, pwi_ref, d_ref,
             ys_ref, ecr_ref, eci_ref, bur, bui, car_r, car_i, end_r, end_i, upb, xb_r, xb_i, *nat):
        c = pl.program_id(1)

        @pl.when(c == 0)
        def _():
            car_r[...] = jnp.zeros_like(car_r)
            car_i[...] = jnp.zeros_like(car_i)

        for j in range(nb):
            cols = pl.ds(j * LANES, LANES)
            scols = pl.ds(j * STATE_W, STATE_W)
            nat[j][...] = us_ref[:, cols].astype(F32)
            for i in range(ls):
                upb[j, pl.ds(SUBLANES * i, SUBLANES), :] = nat[j][pl.ds(i, SUBLANES, stride=ls), :]
            u = upb[j]
            up = u.astype(BF16)
            bur[j] = _dot(up, bbr_ref[j])
            bui[j] = _dot(up, bbi_ref[j])
            a_re = jnp.broadcast_to(ar_ref[:, scols], (SUBLANES, STATE_W))
            a_im = jnp.broadcast_to(ai_ref[:, scols], (SUBLANES, STATE_W))
            x_re, x_im = _local_scan(a_re, a_im, bur.at[j], bui.at[j], bur.at[j], bui.at[j], 0, ls, False)
            end_r[j] = x_re
            end_i[j] = x_im
            big_re = pwr_ref[tc - 1:tc, scols]
            big_im = pwi_ref[tc - 1:tc, scols]
            e_re = car_r[j, 0:1, :]
            e_im = car_i[j, 0:1, :]
            for s in range(SUBLANES):
                n_re = end_r[j, s:s + 1, :] + big_re * e_re - big_im * e_im
                n_im = end_i[j, s:s + 1, :] + big_re * e_im + big_im * e_re
                e_re, e_im = n_re, n_im
                if s < SUBLANES - 1:
                    car_r[j, s + 1:s + 2, :] = e_re
                    car_i[j, s + 1:s + 2, :] = e_im
            ec_re = car_r[j]
            ec_im = car_i[j]
            ecr_ref[:, scols] = ec_re
            eci_ref[:, scols] = ec_im
            e2_re, e2_im = _two(ec_re), _two(ec_im)
            for k in range(tc // SLAB):
                rows_k = pl.ds(SLAB * k, SLAB)
                p_re = pwr_ref[rows_k, scols]
                p_im = pwi_ref[rows_k, scols]
                xb_r[j, rows_k, :] = (bur[j, rows_k, :] + p_re * e2_re - p_im * e2_im).astype(BF16)
                xb_i[j, rows_k, :] = (bui[j, rows_k, :] + p_re * e2_im + p_im * e2_re).astype(BF16)
            upb[j] = _dot(xb_r[j], cmr_ref[j]) - _dot(xb_i[j], cmi_ref[j]) + d_ref[:, cols] * u
            for i in range(ls):
                nat[j][pl.ds(i, SUBLANES, stride=ls), :] = upb[j, pl.ds(SUBLANES * i, SUBLANES), :]
            ys_ref[:, cols] = nat[j][...]
            car_r[j, 0:1, :] = e_re
            car_i[j, 0:1, :] = e_im

    sp = _scan_specs(tc, nb, lambda c: c)
    carry_shape = jax.ShapeDtypeStruct((nc * SUBLANES, STATE_ALL), F32)
    small = pltpu.VMEM((nb, SUBLANES, STATE_W), F32)
    big = pltpu.VMEM((nb, tc, STATE_W), F32)
    return pl.pallas_call(
        kern, name="ssm_scan_fwd", grid=(LANE_BLOCKS // nb, nc),
        in_specs=[sp["us"], sp["bblk"], sp["bblk"], sp["cblk"], sp["cblk"], sp["vec"], sp["vec"], sp["tab"], sp["tab"],
                  sp["dvec"]],
        out_specs=(sp["tok"], sp["car"], sp["car"]),
        out_shape=(jax.ShapeDtypeStruct((seq, D_MODEL), F32), carry_shape, carry_shape),
        scratch_shapes=[big, big, small, small, small, small, pltpu.VMEM((nb, tc, LANES), F32),
                        pltpu.VMEM((nb, tc, STATE_W), BF16), pltpu.VMEM((nb, tc, STATE_W), BF16)]
        + [pltpu.VMEM((tc, LANES), F32)] * nb,
        compiler_params=_cparams("parallel", "arbitrary"),
    )(proj, bb_re, bb_im, cm_re, cm_im, abar_re, abar_im, pw_re, pw_im, d_skip)


def _ssm_scan_bwd(proj, dys, ec_re, ec_im, bb_re, bb_im, cm_re, cm_im, abar_re, abar_im,
                  pw_re, pw_im, pv_re, pv_im, d_skip, dproj, tc):
    seq = proj.shape[0]
    nc = seq // tc
    ls = tc // SUBLANES
    nb = SCAN_BLOCKS

    def kern(us_ref, dys_ref, ecr_ref, eci_ref, bbr_ref, bbi_ref, cmr_ref, cmi_ref, ar_ref, ai_ref,
             pwr_ref, pwi_ref, pvr_ref, pvi_ref, d_ref, _,
             dus_ref, dbbr_ref, dbbi_ref, dcmr_ref, dcmi_ref, dar_ref, dai_ref, dd_ref,
             bur, bui, xr, xi, gr, gi, fc_r, fc_i, a_bbr, a_bbi, a_cmr, a_cmi, a_ar, a_ai, a_dd, upb, dpb, hb_r, hb_i,
             *nat):
        c = pl.program_id(1)

        @pl.when(c == 0)
        def _():
            for ref in (fc_r, fc_i, a_bbr, a_bbi, a_cmr, a_cmi, a_ar, a_ai, a_dd):
                ref[...] = jnp.zeros_like(ref)

        for j in range(nb):
            cols = pl.ds(j * LANES, LANES)
            scols = pl.ds(j * STATE_W, STATE_W)
            nat_u, nat_d = nat[2 * j], nat[2 * j + 1]
            nat_u[...] = us_ref[:, cols].astype(F32)
            nat_d[...] = dys_ref[:, cols]
            for i in range(ls):
                rows_i = pl.ds(SUBLANES * i, SUBLANES)
                upb[j, rows_i, :] = nat_u[pl.ds(i, SUBLANES, stride=ls), :]
                dpb[j, rows_i, :] = nat_d[pl.ds(i, SUBLANES, stride=ls), :]
            u = upb[j]
            dysv = dpb[j]
            a_dd[j] += _acc8(dysv * u)
            up = u.astype(BF16)
            bur[j] = _dot(up, bbr_ref[j])
            bui[j] = _dot(up, bbi_ref[j])
            a_re = jnp.broadcast_to(ar_ref[:, scols], (SUBLANES, STATE_W))
            a_im = jnp.broadcast_to(ai_ref[:, scols], (SUBLANES, STATE_W))
            ec_r = ecr_ref[:, scols]
            ec_i = eci_ref[:, scols]
            xr[j, 0:SUBLANES, :] = ec_r
            xi[j, 0:SUBLANES, :] = ec_i
            _local_scan(a_re, a_im, bur.at[j], bui.at[j], xr.at[j], xi.at[j], SUBLANES, ls, False, init=(ec_r, ec_i),
                        xb=(hb_r.at[j], hb_i.at[j]))
            dysp = dysv.astype(BF16)
            a_cmr[j] += _dot_tn(dysp, hb_r[j])
            a_cmi[j] -= _dot_tn(dysp, hb_i[j])
            gr[j] = _dot_nt(dysp, cmr_ref[j])
            gi[j] = -_dot_nt(dysp, cmi_ref[j])
            _local_scan(a_re, -a_im, gr.at[j], gi.at[j], gr.at[j], gi.at[j], 0, ls, True)
            big_re = pwr_ref[tc - 1:tc, scols]
            big_im = -pwi_ref[tc - 1:tc, scols]
            f_re = fc_r[j, SUBLANES - 1:SUBLANES, :]
            f_im = fc_i[j, SUBLANES - 1:SUBLANES, :]
            for s in range(SUBLANES - 1, -1, -1):
                n_re = gr[j, s:s + 1, :] + big_re * f_re - big_im * f_im
                n_im = gi[j, s:s + 1, :] + big_re * f_im + big_im * f_re
                f_re, f_im = n_re, n_im
                if s > 0:
                    fc_r[j, s - 1:s, :] = f_re
                    fc_i[j, s - 1:s, :] = f_im
            f2_r, f2_i = _two(fc_r[j]), _two(fc_i[j])
            acc_r = jnp.zeros((SUBLANES, STATE_W), F32)
            acc_i = jnp.zeros((SUBLANES, STATE_W), F32)
            for k in range(tc // SLAB):
                rows_k = pl.ds(SLAB * k, SLAB)
                q_re = pvr_ref[rows_k, scols]
                q_im = pvi_ref[rows_k, scols]
                lam_re = gr[j, rows_k, :] + q_re * f2_r + q_im * f2_i
                lam_im = gi[j, rows_k, :] + q_re * f2_i - q_im * f2_r
                xp_re = xr[j, rows_k, :]
                xp_im = xi[j, rows_k, :]
                d_r = lam_re * xp_re + lam_im * xp_im
                d_i = lam_im * xp_re - lam_re * xp_im
                acc_r = acc_r + (d_r[0:SUBLANES] + d_r[SUBLANES:])
                acc_i = acc_i + (d_i[0:SUBLANES] + d_i[SUBLANES:])
                hb_r[j, rows_k, :] = lam_re.astype(BF16)
                hb_i[j, rows_k, :] = lam_im.astype(BF16)
            a_ar[j] += acc_r
            a_ai[j] += acc_i
            fc_r[j, SUBLANES - 1:SUBLANES, :] = f_re
            fc_i[j, SUBLANES - 1:SUBLANES, :] = f_im
            lb_re = hb_r[j]
            lb_im = hb_i[j]
            a_bbr[j] += _dot_tn(up, lb_re)
            a_bbi[j] += _dot_tn(up, lb_im)
            dpb[j] = _dot_nt(lb_re, bbr_ref[j]) + _dot_nt(lb_im, bbi_ref[j]) + dysv * d_ref[:, cols]
            for i in range(ls):
                nat_d[pl.ds(i, SUBLANES, stride=ls), :] = dpb[j, pl.ds(SUBLANES * i, SUBLANES), :]
            dus_ref[:, cols] = nat_d[...].astype(BF16)

        @pl.when(c == nc - 1)
        def _():
            row_g = lax.broadcasted_iota(jnp.int32, (LANES, STATE_W), 0) // SSM_H
            col_g = lax.broadcasted_iota(jnp.int32, (LANES, STATE_W), 1) // SSM_P
            fold = (lax.broadcasted_iota(jnp.int32, (STATE_W, SSM_P), 0) % SSM_P
                    == lax.broadcasted_iota(jnp.int32, (STATE_W, SSM_P), 1)).astype(BF16)
            for j in range(nb):
                rows_j = pl.ds(j * LANES, LANES)
                for acc, out in ((a_bbr, dbbr_ref), (a_bbi, dbbi_ref), (a_cmr, dcmr_ref), (a_cmi, dcmi_ref)):
                    out[rows_j, :] = _unpermute_rhs(jnp.where(row_g == col_g, acc[j], 0.0), fold)
                dar_ref[:, pl.ds(j * STATE_W, STATE_W)] = jnp.sum(a_ar[j], axis=0, keepdims=True)
                dai_ref[:, pl.ds(j * STATE_W, STATE_W)] = jnp.sum(a_ai[j], axis=0, keepdims=True)
                dd_ref[:, pl.ds(j * LANES, LANES)] = jnp.sum(a_dd[j], axis=0, keepdims=True)

    sp = _scan_specs(tc, nb, lambda c: nc - 1 - c)
    ghp = pl.BlockSpec((nb * LANES, SSM_P), lambda b, c: (b, 0))
    ghp_shape = jax.ShapeDtypeStruct((SSM_G * SSM_H, SSM_P), F32)
    small = pltpu.VMEM((nb, SUBLANES, STATE_W), F32)
    big = pltpu.VMEM((nb, tc, STATE_W), F32)
    bigp = pltpu.VMEM((nb, tc + SUBLANES, STATE_W), F32)
    blk = pltpu.VMEM((nb, LANES, STATE_W), F32)
    tok = pltpu.VMEM((nb, tc, LANES), F32)
    return pl.pallas_call(
        kern, name="ssm_scan_bwd", grid=(LANE_BLOCKS // nb, nc),
        in_specs=[sp["us"], sp["tok"], sp["car"], sp["car"], sp["bblk"], sp["bblk"], sp["cblk"], sp["cblk"],
                  sp["vec"], sp["vec"], sp["tab"], sp["tab"], sp["tab"], sp["tab"], sp["dvec"], _ANY],
        out_specs=(sp["us"], ghp, ghp, ghp, ghp, sp["vec"], sp["vec"], sp["dvec"]),
        out_shape=(jax.ShapeDtypeStruct(dproj.shape, BF16), ghp_shape, ghp_shape, ghp_shape, ghp_shape,
                   jax.ShapeDtypeStruct((1, STATE_ALL), F32), jax.ShapeDtypeStruct((1, STATE_ALL), F32),
                   jax.ShapeDtypeStruct((1, D_MODEL), F32)),
        scratch_shapes=[big, big, bigp, bigp, big, big, small, small, blk, blk, blk, blk,
                        small, small, pltpu.VMEM((nb, SUBLANES, LANES), F32), tok, tok,
                        pltpu.VMEM((nb, tc, STATE_W), BF16), pltpu.VMEM((nb, tc, STATE_W), BF16)]
        + [pltpu.VMEM((tc, LANES), F32)] * (2 * nb),
        input_output_aliases={15: 0},
        compiler_params=_cparams("parallel", "arbitrary"),
    )(proj, dys, ec_re, ec_im, bb_re, bb_im, cm_re, cm_im, abar_re, abar_im, pw_re, pw_im, pv_re, pv_im, d_skip, dproj)


def _eye5():
    return jnp.asarray(np.eye(GROUPS_PER_BLOCK, dtype=np.float32)[None, :, None, :, None])


def _embed_b(bb_t):
    t = bb_t.transpose(1, 0, 2).reshape(LANE_BLOCKS, GROUPS_PER_BLOCK, SSM_H, 1, SSM_P)
    return (t * _eye5()).reshape(LANE_BLOCKS, LANES, STATE_W)


def _embed_c(c_ghp):
    t = c_ghp.transpose(0, 2, 1).reshape(LANE_BLOCKS, GROUPS_PER_BLOCK, SSM_P, 1, SSM_H)
    return (t * _eye5()).reshape(LANE_BLOCKS, STATE_W, LANES)


def _local_step(x, c_row, tgt, w_ada_bf, b_ada, g1, g2, w_in_bf, pool_w_bf, pscale, a_re, a_im, log_dt,
                b_re_t, b_im_t, c_re, c_im, d_skip, glu_w_bf, glu_b, wbp_bf, wbs_bf, wout_bf,
                split_proj=None, ride_for_dw_in=None, ride_for_dh=None, mod_fn=None):
    seq = x.shape[0]
    tc = min(SCAN_CHUNK, seq)
    mod8, silu_c = _mod_kernel(c_row, w_ada_bf, b_ada) if mod_fn is None else mod_fn(c_row, b_ada)
    mod = mod8[0:1]
    shift, scale, gate = mod[:, 0:D_MODEL], mod[:, D_MODEL:2 * D_MODEL], mod[:, 2 * D_MODEL:]

    abar_re, abar_im, bb_re_t, bb_im_t = _ssm_params(a_re, a_im, log_dt, b_re_t, b_im_t)
    abar_re_f, abar_im_f = abar_re.reshape(1, STATE_ALL), abar_im.reshape(1, STATE_ALL)
    pw_re, pw_im, pv_re, pv_im = _pow_tables(abar_re_f, abar_im_f, tc)
    bbe_re, bbe_im = _embed_b(bb_re_t).astype(BF16), _embed_b(bb_im_t).astype(BF16)
    cme_re, cme_im = _embed_c(c_re).astype(BF16), _embed_c(c_im).astype(BF16)
    d_row = d_skip.reshape(1, D_MODEL)

    if split_proj:
        w_own, chip, w_in_ride, unpack_w_in, late_ride, unpack_late = split_proj
        h, h_t, proj, w_blocks = _in_norm_proj_own(x, g1, scale, shift, w_own, chip, w_in_ride)
        w_in_bf = unpack_w_in(w_blocks)
        proj, *gathered = _proj_rest(h, w_blocks, proj, chip, late_ride)
        pool_w_bf, glu_w_bf, wbp_bf, wbs_bf, wout_bf = unpack_late(*gathered)
    else:
        h, h_t = _in_norm(x, g1, scale, shift)
        proj = _mm([h], [w_in_bf], name="proj", out_dtype=BF16, bm=1024, bn=1536, bk=1024)
    ypool, ypool_t = _pool_fwd(proj, pool_w_bf, pscale)
    ys, ec_re, ec_im = _ssm_scan_fwd(proj, bbe_re, bbe_im, cme_re, cme_im, abar_re_f, abar_im_f,
                                      pw_re, pw_im, d_row, tc)
    yssm, yssm_t = _glu_fwd(ys, proj, glu_w_bf, glu_b)
    (dy, dypool, dyssm, dproj, merged_t, dob, dbp, dbs, loss, dgate, dg2) = _out_fwd_bwd(
        ypool, yssm, proj, x, tgt, gate, g2, wbp_bf, wbs_bf, wout_bf)

    d_wout = _mm([merged_t], [dob], name="dw_out", bm=1024, bn=1024, bk=2048)
    d_wbp = _mm([ypool_t], [dbp], name="dw_bp", bm=1024, bn=1024, bk=2048)
    d_wbs = _mm([yssm_t], [dbs], name="dw_bs", bm=1024, bn=1024, bk=2048)
    dys, dproj, dq, yg_t, d_glu_b = _glu_bwd(ys, proj, dyssm, glu_w_bf, glu_b, dproj)
    d_glu_w = _mm([yg_t], [dq], name="dw_glu", bm=1024, bn=1024, bk=2048)
    (dproj, dbbe_re, dbbe_im, dcme_re, dcme_im, d_abar_re, d_abar_im, d_dskip) = _ssm_scan_bwd(
        proj, dys, ec_re, ec_im, bbe_re, bbe_im, cme_re, cme_im, abar_re_f, abar_im_f,
        pw_re, pw_im, pv_re, pv_im, d_row, dproj, tc)
    dproj, d_pool_w, d_pscale = _pool_bwd(proj, dypool, pool_w_bf, pscale, dproj)
    dparts = [dproj]
    small_ready = dict(
        dg2=dg2, d_pscale=d_pscale, d_glu_b=d_glu_b, d_dskip=d_dskip, d_abar_re=d_abar_re, d_abar_im=d_abar_im,
        d_bb_re_t=dbbe_re.reshape(SSM_G, SSM_H, SSM_P).transpose(1, 0, 2),
        d_bb_im_t=dbbe_im.reshape(SSM_G, SSM_H, SSM_P).transpose(1, 0, 2),
        d_c_re=dcme_re.reshape(SSM_G, SSM_H, SSM_P), d_c_im=dcme_im.reshape(SSM_G, SSM_H, SSM_P))
    ride = ride_for_dw_in(small_ready) if ride_for_dw_in else None
    d_win = _mm([h_t], dparts, name="dw_in", bm=1024, bn=1024, bk=2048, ride=ride)
    rode_dw_in = ()
    if ride:
        d_win, rode_dw_in = d_win[0], tuple(d_win[1:])
    big_grads = dict(d_win=d_win, d_glu_w=d_glu_w, d_wbp=d_wbp, d_wbs=d_wbs, d_wout=d_wout, d_pool_w=d_pool_w)
    ride = ride_for_dh(big_grads) if ride_for_dh else None
    dh = _mm(dparts, [w_in_bf], tb=True, name="dh", bm=2048, bn=1024, bk=1024, ride=ride)
    rode = ()
    if ride:
        dh, rode = dh[0], tuple(dh[1:])
    grad_x, dshift, dscale, dg1 = _in_bwd(dh, x, dy, g1, scale)
    dmod = jnp.concatenate([dshift, dscale, dgate], axis=1)
    return dict(
        rode=rode, rode_dw_in=rode_dw_in, loss=loss[0, 0], grad_x=grad_x, dmod=dmod, silu_c=silu_c, dg1=dg1,
        **small_ready, **big_grads)


def _position():
    x, y, c = lax.axis_index("x"), lax.axis_index("y"), lax.axis_index("c")
    chips = [(1 - x, y), (x, 1 - y), (1 - x, 1 - y)]
    return x, y, c, chips


_ANY = pl.BlockSpec(memory_space=pl.ANY)
COMM_CHUNKS = 4
COMM_ROW_ALIGN = 16


def _row_chunks(rows, k):
    assert rows % (k * COMM_ROW_ALIGN) == 0, (rows, k)
    step = rows // k
    return [(q * step, step) for q in range(k)]


def _mod_sharded(c_row, w_own_bf, b_ada):
    n_own = w_own_bf.shape[1]
    assert N_CHIPS * n_own == b_ada.shape[1] and n_own % LANES == 0, (n_own, b_ada.shape)

    def kern(c_ref, w_ref, b_ref, m_ref, s_ref, rows, prods, got, send_sems, recv_sems):
        x, y, c, chips = _position()
        me = 2 * x + y

        def copy(k, src, dst, to):
            return pltpu.make_async_remote_copy(src_ref=src, dst_ref=dst, send_sem=send_sems.at[k],
                                                recv_sem=recv_sems.at[k], device_id=(*to, c), device_id_type=MESH_ID)

        number = [2 * cx + cy for cx, cy in chips]
        sc = _silu(c_ref[...])
        s_ref[...] = sc
        rows[me] = jnp.broadcast_to(sc, (SUBLANES, D_MODEL))
        out_rows = [copy(j, rows.at[me], rows.at[me], chip) for j, chip in enumerate(chips)]
        for cp in out_rows:
            cp.start()
        for j, chip in enumerate(chips):
            copy(j, rows.at[number[j]], rows.at[number[j]], chip).wait_recv()
        lhs = rows[...].reshape(N_CHIPS * SUBLANES, D_MODEL).astype(BF16)
        prods[...] = _dot(lhs, w_ref[...]).reshape(N_CHIPS, SUBLANES, n_own)
        got[me] = prods[me]
        out_prods = [copy(3 + j, prods.at[number[j]], got.at[me], chip) for j, chip in enumerate(chips)]
        for cp in out_prods:
            cp.start()
        for j, chip in enumerate(chips):
            copy(3 + j, got.at[number[j]], got.at[number[j]], chip).wait_recv()
        for cp in out_rows + out_prods:
            cp.wait_send()
        for k in range(N_CHIPS):
            cols = slice(k * n_own, (k + 1) * n_own)
            m_ref[:, cols] = got[k] + b_ref[:, cols]

    return pl.pallas_call(
        kern, name="ada_mod_sharded",
        out_shape=(jax.ShapeDtypeStruct((SUBLANES, 3 * D_MODEL), F32), jax.ShapeDtypeStruct((1, D_MODEL), F32)),
        scratch_shapes=[pltpu.VMEM((N_CHIPS, SUBLANES, D_MODEL), F32), pltpu.VMEM((N_CHIPS, SUBLANES, n_own), F32),
                        pltpu.VMEM((N_CHIPS, SUBLANES, n_own), F32), pltpu.SemaphoreType.DMA((6,)),
                        pltpu.SemaphoreType.DMA((6,))],
        compiler_params=_cparams())(c_row, w_own_bf, b_ada)


def _ag_weights_ride(packed, n_chunks=COMM_CHUNKS):
    rows, width = packed.shape
    half = rows // 2
    chunks = _row_chunks(half, n_chunks)
    nq = len(chunks)

    def parts(p_ref, out_ref, send_sems, recv_sems):
        x, y, c, chips = _position()
        sibling = (x, y, 1 - c)

        def copy(k, chip, h, q, to, src=None):
            start, size = chunks[q]
            rows_q = pl.ds(h * half + start, size)
            dst = out_ref.at[2 * chip[0] + chip[1], rows_q, :]
            return pltpu.make_async_remote_copy(
                src_ref=dst if src is None else src.at[rows_q, :], dst_ref=dst, send_sem=send_sems.at[k * nq + q],
                recv_sem=recv_sems.at[k * nq + q], device_id=to, device_id_type=MESH_ID)

        mine = [copy(6 + h, (x, y), h, q, sibling, src=p_ref) for h in range(2) for q in range(nq)]
        first = [copy(j, (x, y), c, q, (*chip, c), src=p_ref) for q in range(nq) for j, chip in enumerate(chips)]
        return (x, y, c), chips, sibling, copy, mine, first

    def start(ins, outs, sems):
        _, _, _, _, mine, first = parts(ins[0], outs[0], sems[0], sems[1])
        for cp in first + mine:
            cp.start()

    def wait(ins, outs, sems):
        (x, y, c), chips, sibling, copy, mine, first = parts(ins[0], outs[0], sems[0], sems[1])
        passed = []
        for q in range(nq):
            for j, chip in enumerate(chips):
                copy(j, chip, c, q, (x, y, c)).wait_recv()
                fwd = copy(3 + j, chip, c, q, sibling)
                fwd.start()
                passed.append(fwd)
        for q in range(nq):
            for j, chip in enumerate(chips):
                copy(3 + j, chip, 1 - c, q, (x, y, c)).wait_recv()
        for cp in mine:
            cp.wait_recv()
        for cp in first + passed + mine:
            cp.wait_send()

    return _Ride([packed], [jax.ShapeDtypeStruct((N_CHIPS, rows, width), packed.dtype)],
                 [pltpu.SemaphoreType.DMA((8 * nq,)), pltpu.SemaphoreType.DMA((8 * nq,))], start, wait)


def _join_rides(rides):
    def split(seq, counts):
        out, at = [], 0
        for n in counts:
            out.append(seq[at:at + n])
            at += n
        return out

    n_in = [len(r.inputs) for r in rides]
    n_out = [len(r.out_shapes) for r in rides]
    n_sem = [len(r.scratch) for r in rides]

    def start(ins, outs, sems):
        for r, i, o, s in zip(rides, split(ins, n_in), split(outs, n_out), split(sems, n_sem)):
            r.start(i, o, s)

    def wait(ins, outs, sems):
        for r, i, o, s in zip(rides, split(ins, n_in), split(outs, n_out), split(sems, n_sem)):
            r.wait(i, o, s)

    return _Ride([a for r in rides for a in r.inputs], [a for r in rides for a in r.out_shapes],
                 [a for r in rides for a in r.scratch], start, wait)


def _run_ride(ride, name):
    n_in, n_out = len(ride.inputs), len(ride.out_shapes)

    def body(*refs):
        ins, outs, sems = refs[:n_in], refs[n_in:n_in + n_out], refs[n_in + n_out:]
        ride.start(ins, outs, sems)
        ride.wait(ins, outs, sems)

    return pl.pallas_call(
        body, name=name, in_specs=[_ANY] * n_in, out_specs=(_ANY,) * n_out, out_shape=tuple(ride.out_shapes),
        scratch_shapes=list(ride.scratch))(*ride.inputs)


def _small_allgather_ride(buf):
    rows, width = buf.shape
    chunks = _row_chunks(rows, COMM_CHUNKS)
    nq = len(chunks)

    def parts(b_ref, all_ref, send_sems, recv_sems, local_sem):
        x, y, c, chips = _position()
        me, sibling = (x, y, c), (x, y, 1 - c)

        def copy(k, block, q, to, src=None):
            rows_q = pl.ds(chunks[q][0], chunks[q][1])
            dst = all_ref.at[4 * block[0] + 2 * block[1] + block[2], rows_q, :]
            return pltpu.make_async_remote_copy(
                src_ref=dst if src is None else src.at[rows_q, :], dst_ref=dst, send_sem=send_sems.at[k * nq + q],
                recv_sem=recv_sems.at[k * nq + q], device_id=to, device_id_type=MESH_ID)

        mine = pltpu.make_async_copy(b_ref, all_ref.at[4 * x + 2 * y + c], local_sem)
        first = []
        for q in range(nq):
            first += [copy(1 + j, me, q, (*chip, c), src=b_ref) for j, chip in enumerate(chips)]
            first.append(copy(0, me, q, sibling, src=b_ref))
        return me, sibling, c, chips, copy, mine, first

    def start(ins, outs, sems):
        _, _, _, _, _, mine, first = parts(ins[0], outs[0], *sems)
        mine.start()
        for cp in first:
            cp.start()

    def wait(ins, outs, sems):
        me, sibling, c, chips, copy, mine, first = parts(ins[0], outs[0], *sems)
        passed = []
        for q in range(nq):
            for j, chip in enumerate(chips):
                copy(1 + j, (*chip, c), q, me).wait_recv()
                fwd = copy(4 + j, (*chip, c), q, sibling)
                fwd.start()
                passed.append(fwd)
        for q in range(nq):
            copy(0, sibling, q, me).wait_recv()
            for j, chip in enumerate(chips):
                copy(4 + j, (*chip, 1 - c), q, me).wait_recv()
        for cp in first + passed:
            cp.wait_send()
        mine.wait()

    return _Ride([buf], [jax.ShapeDtypeStruct((N_DEV, rows, width), F32)],
                 [pltpu.SemaphoreType.DMA((7 * nq,)), pltpu.SemaphoreType.DMA((7 * nq,)), pltpu.SemaphoreType.DMA],
                 start, wait)


def _sum_devices(blocks):
    n, rows, width = blocks.shape
    rb = rows // 2 if (rows // 2) % SUBLANES == 0 else rows

    def kern(b_ref, o_ref):
        total = b_ref[0]
        for d in range(1, n):
            total = total + b_ref[d]
        o_ref[...] = total

    return pl.pallas_call(
        kern, name="small_sum", grid=(rows // rb,), in_specs=[pl.BlockSpec((n, rb, width), lambda i: (0, i, 0))],
        out_specs=pl.BlockSpec((rb, width), lambda i: (i, 0)), out_shape=jax.ShapeDtypeStruct((rows, width), F32),
        compiler_params=_cparams("parallel"))(blocks)


def _small_allgather_sum(buf, head_rows, n_chunks=COMM_CHUNKS):
    rows, width = buf.shape
    chunks = _row_chunks(rows, n_chunks)
    nq = len(chunks)

    def body(b_ref, head_ref, sum_ref, all_ref, send_sems, recv_sems, local_sem):
        x, y, c, chips = _position()
        me, sibling = (x, y, c), (x, y, 1 - c)

        def slot(px, py, pc):
            return all_ref.at[4 * px + 2 * py + pc]

        def copy(k, block, q, to, src=None):
            rows_q = pl.ds(chunks[q][0], chunks[q][1])
            dst = slot(*block).at[rows_q, :]
            return pltpu.make_async_remote_copy(
                src_ref=dst if src is None else src.at[rows_q, :], dst_ref=dst, send_sem=send_sems.at[k * nq + q],
                recv_sem=recv_sems.at[k * nq + q], device_id=to, device_id_type=MESH_ID)

        mine = pltpu.make_async_copy(b_ref, slot(*me), local_sem)
        mine.start()
        first = []
        for q in range(nq):
            first += [copy(1 + j, me, q, (*chip, c), src=b_ref) for j, chip in enumerate(chips)]
            first.append(copy(0, me, q, sibling, src=b_ref))
        for cp in first:
            cp.start()
        passed = []
        for q in range(nq):
            for j, chip in enumerate(chips):
                copy(1 + j, (*chip, c), q, me).wait_recv()
                fwd = copy(4 + j, (*chip, c), q, sibling)
                fwd.start()
                passed.append(fwd)
        for q in range(nq):
            copy(0, sibling, q, me).wait_recv()
            for j, chip in enumerate(chips):
                copy(4 + j, (*chip, 1 - c), q, me).wait_recv()
        for cp in first + passed:
            cp.wait_send()
        mine.wait()
        total = all_ref[0]
        for d in range(1, N_DEV):
            total = total + all_ref[d]
        sum_ref[...] = total
        head_ref[...] = all_ref[:, 0:head_rows, :]

    vm = pl.BlockSpec(memory_space=pltpu.VMEM)
    return pl.pallas_call(
        body, name="small_allgather_sum", in_specs=[vm], out_specs=(vm, vm),
        out_shape=(jax.ShapeDtypeStruct((N_DEV, head_rows, width), F32), jax.ShapeDtypeStruct((rows, width), F32)),
        scratch_shapes=[pltpu.VMEM((N_DEV, rows, width), F32), pltpu.SemaphoreType.DMA((7 * nq,)),
                        pltpu.SemaphoreType.DMA((7 * nq,)), pltpu.SemaphoreType.DMA],
        compiler_params=_cparams(),
    )(buf)


def _rs_pair(g):
    n, rows, width = g.shape
    half = rows // 2
    chunks = _row_chunks(half, COMM_CHUNKS)
    nq = len(chunks)

    def body(g_ref, got_ref, send_sems, recv_sems):
        x, y, c, _ = _position()
        swaps = []
        for k in range(n):
            for q, (start, size) in enumerate(chunks):
                swaps.append(pltpu.make_async_remote_copy(
                    src_ref=g_ref.at[k, pl.ds((1 - c) * half + start, size), :], dst_ref=got_ref.at[k, pl.ds(start, size), :],
                    send_sem=send_sems.at[k * nq + q], recv_sem=recv_sems.at[k * nq + q],
                    device_id=(x, y, 1 - c), device_id_type=MESH_ID))
        for cp in swaps:
            cp.start()
        for cp in swaps:
            cp.wait()

    return pl.pallas_call(
        body, name="rs_pair", in_specs=[_ANY], out_specs=_ANY, out_shape=jax.ShapeDtypeStruct((n, half, width), g.dtype),
        scratch_shapes=[pltpu.SemaphoreType.DMA((n * nq,)), pltpu.SemaphoreType.DMA((n * nq,))],
    )(g)


def _rs_chips_ride(part_bf):
    n, rows, width = part_bf.shape
    chunks = _row_chunks(rows, COMM_CHUNKS)
    nq = len(chunks)

    def sends(pb_ref, got_ref, send_sems, recv_sems):
        x, y, c, chips = _position()
        out = []
        for q, (start, size) in enumerate(chunks):
            for j, chip in enumerate(chips):
                out.append(pltpu.make_async_remote_copy(
                    src_ref=pb_ref.at[2 * chip[0] + chip[1], pl.ds(start, size), :], dst_ref=got_ref.at[j, pl.ds(start, size), :],
                    send_sem=send_sems.at[j * nq + q], recv_sem=recv_sems.at[j * nq + q],
                    device_id=(*chip, c), device_id_type=MESH_ID))
        return out

    def start(ins, outs, sems):
        for cp in sends(ins[0], outs[0], sems[0], sems[1]):
            cp.start()

    def wait(ins, outs, sems):
        for cp in sends(ins[0], outs[0], sems[0], sems[1]):
            cp.wait()

    return _Ride([part_bf], [jax.ShapeDtypeStruct((N_CHIPS - 1, rows, width), BF16)],
                 [pltpu.SemaphoreType.DMA((3 * nq,)), pltpu.SemaphoreType.DMA((3 * nq,))], start, wait)


def _rs_join(shard):
    rows, width = shard.shape
    half = rows // 2
    chunks = _row_chunks(half, COMM_CHUNKS)
    nq = len(chunks)

    def body(in_ref, out_ref, send_sems, recv_sems):
        x, y, c, _ = _position()
        def swap(q, h):
            rows_q = pl.ds(h * half + chunks[q][0], chunks[q][1])
            return pltpu.make_async_remote_copy(
                src_ref=in_ref.at[rows_q, :], dst_ref=out_ref.at[rows_q, :], send_sem=send_sems.at[q],
                recv_sem=recv_sems.at[q], device_id=(x, y, 1 - c), device_id_type=MESH_ID)

        for q in range(nq):
            swap(q, c).start()
        for q in range(nq):
            swap(q, 1 - c).wait_recv()
        for q in range(nq):
            swap(q, c).wait_send()

    return pl.pallas_call(
        body, name="rs_join", in_specs=[_ANY], out_specs=_ANY, input_output_aliases={0: 0},
        out_shape=jax.ShapeDtypeStruct(shard.shape, shard.dtype),
        scratch_shapes=[pltpu.SemaphoreType.DMA((nq,)), pltpu.SemaphoreType.DMA((nq,))],
    )(shard)


def _pair_add(g, got, core):
    n, half, width = got.shape
    nb = 2
    rb = half // nb

    def kern(c_ref, a_ref, b_ref, f_ref, h_ref):
        s = a_ref[...] + b_ref[...]
        f_ref[...] = s
        h_ref[...] = s.astype(BF16)

    spec = pl.BlockSpec((1, rb, width), lambda k, i, c_ref: (k, i, 0))
    return pl.pallas_call(
        kern, name="rs_pair_add",
        grid_spec=pltpu.PrefetchScalarGridSpec(
            num_scalar_prefetch=1, grid=(n, nb),
            in_specs=[pl.BlockSpec((1, rb, width), lambda k, i, c_ref: (k, c_ref[0] * nb + i, 0)), spec],
            out_specs=(spec, spec)),
        out_shape=(jax.ShapeDtypeStruct(got.shape, F32), jax.ShapeDtypeStruct(got.shape, BF16)),
        compiler_params=_cparams("parallel", "parallel"))(core, g, got)


def _chip_add(part_f32, got, where):
    _, rows, width = part_f32.shape
    nb = 2
    rb = rows // nb

    def kern(w_ref, a_ref, b_ref, o_ref):
        o_ref[...] = ((a_ref[0] + b_ref[0].astype(F32)) + b_ref[1].astype(F32)) + b_ref[2].astype(F32)

    return pl.pallas_call(
        kern, name="rs_chip_add",
        grid_spec=pltpu.PrefetchScalarGridSpec(
            num_scalar_prefetch=1, grid=(nb,),
            in_specs=[pl.BlockSpec((1, rb, width), lambda i, w_ref: (w_ref[0], i, 0)),
                      pl.BlockSpec((N_CHIPS - 1, rb, width), lambda i, w_ref: (0, i, 0))],
            out_specs=pl.BlockSpec((rb, width), lambda i, w_ref: (w_ref[1] * nb + i, 0))),
        out_shape=jax.ShapeDtypeStruct((2 * rows, width), F32),
        compiler_params=_cparams("parallel"))(where, part_f32, got)


def _adamw(w, g, m, v, name):
    rows, width = w.shape
    rb = rows
    for cand in (512, 256, 128, 64, 32, 16, 8):
        if rows % cand == 0 and cand * width * 4 <= ADAM_BLOCK_BYTES:
            rb = cand
            break
    spec = pl.BlockSpec((rb, width), lambda i: (i, 0))

    def kern(w_ref, g_ref, m_ref, v_ref, d_ref, nm_ref, nv_ref):
        d_ref[...], nm_ref[...], nv_ref[...] = _adamw_update(w_ref[...], g_ref[...], m_ref[...], v_ref[...])

    shp = jax.ShapeDtypeStruct(w.shape, F32)
    return pl.pallas_call(
        kern, name=name, grid=(rows // rb,), in_specs=[spec] * 4, out_specs=(spec, spec, spec),
        out_shape=(shp, shp, shp), compiler_params=_cparams("parallel"))(w, g, m, v)


def _adamw_update(w, g, m, v):
    nm = ADAM_B1 * m + (1.0 - ADAM_B1) * g
    nv = ADAM_B2 * v + (1.0 - ADAM_B2) * (g * g)
    m_hat = nm / (1.0 - ADAM_B1 ** ADAM_STEP)
    v_hat = nv / (1.0 - ADAM_B2 ** ADAM_STEP)
    return -ADAM_LR * (m_hat / (jnp.sqrt(v_hat) + ADAM_EPS) + ADAM_WD * w), nm, nv


def _adamw_small(params):
    n = len(params)

    def kern(*refs):
        ins, outs = refs[:4 * n], refs[4 * n:]
        for p in range(n):
            w_ref, g_ref, m_ref, v_ref = ins[4 * p:4 * p + 4]
            d, nm, nv = _adamw_update(w_ref[...], g_ref[...], m_ref[...], v_ref[...])
            outs[3 * p][...] = d
            outs[3 * p + 1][...] = nm
            outs[3 * p + 2][...] = nv

    flat = [a for group in params for a in group]
    shapes = [jax.ShapeDtypeStruct(group[0].shape, F32) for group in params for _ in range(3)]
    res = pl.pallas_call(kern, name="adamw_small", out_shape=tuple(shapes), compiler_params=_cparams())(*flat)
    return [tuple(res[3 * p:3 * p + 3]) for p in range(n)]


def _wada_grad(silu_t, dmod_cols):
    n = dmod_cols.shape[1]

    def kern(s_ref, d_ref, o_ref):
        acc = s_ref[:, 0:1] * d_ref[0:1, :]
        for b in range(1, N_DEV):
            acc = acc + s_ref[:, b:b + 1] * d_ref[b:b + 1, :]
        o_ref[...] = acc

    return pl.pallas_call(kern, name="wada_grad", out_shape=jax.ShapeDtypeStruct((D_MODEL, n), F32),
                          compiler_params=_cparams())(silu_t, dmod_cols)


def _rows(a, multiple):
    flat = a.reshape(-1)
    pad = (-flat.shape[0]) % (D_MODEL * multiple)
    if pad:
        flat = jnp.concatenate([flat, jnp.zeros((pad,), flat.dtype)])
    return flat.reshape(-1, D_MODEL)


def _part_rows(shape, multiple):
    return -(-int(np.prod(shape)) // (D_MODEL * multiple)) * multiple


def _pack_rows(parts, multiple, total_multiple=1):
    blocks = [_rows(p, multiple) for p in parts]
    pad = (-sum(b.shape[0] for b in blocks)) % total_multiple
    if pad:
        blocks.append(jnp.zeros((pad, D_MODEL), blocks[0].dtype))
    return jnp.concatenate(blocks, axis=0)


def _unpack_rows(buf, shapes, multiple):
    out, r = [], 0
    for shp in shapes:
        n = int(np.prod(shp))
        nr = _part_rows(shp, multiple)
        out.append(buf[r:r + nr].reshape(-1)[:n].reshape(shp))
        r += nr
    return out


def kernel(x, c, w_ada, b_ada, norm_pre, norm_post, w_in, pool_w, pool_scale, ssm_a_re, ssm_a_im, ssm_log_dt, ssm_b_re, ssm_b_im, ssm_c_re, ssm_c_im, ssm_d, glu_w, glu_b, w_branch_pool, w_branch_ssm, w_out, loss_target, m_w_ada, m_b_ada, m_norm_pre, m_norm_post, m_w_in, m_pool_w, m_pool_scale, m_ssm_a_re, m_ssm_a_im, m_ssm_log_dt, m_ssm_b_re, m_ssm_b_im, m_ssm_c_re, m_ssm_c_im, m_ssm_d, m_glu_w, m_glu_b, m_w_branch_pool, m_w_branch_ssm, m_w_out, v_w_ada, v_b_ada, v_norm_pre, v_norm_post, v_w_in, v_pool_w, v_pool_scale, v_ssm_a_re, v_ssm_a_im, v_ssm_log_dt, v_ssm_b_re, v_ssm_b_im, v_ssm_c_re, v_ssm_c_im, v_ssm_d, v_glu_w, v_glu_b, v_w_branch_pool, v_w_branch_ssm, v_w_out):
    n_ada = w_ada.shape[2]
    n_in = w_in.shape[2]
    n_row = glu_w.shape[1]
    n_pool = pool_w.shape[2]
    n_groups = pool_w.shape[1]

    w_ada_own = w_ada[0].astype(BF16)
    w_in_own = w_in[0].astype(BF16)
    w_in_ride = _ag_weights_ride(w_in_own)

    def unpack_w_in(g_in):
        return g_in.transpose(1, 0, 2).reshape(D_MODEL, N_CHIPS * n_in)
    pool_rows = n_groups * n_pool * POOL_GW // D_MODEL
    late_shards = [pool_w[0].reshape(n_groups * n_pool, POOL_GW), glu_w[0], w_branch_pool[0], w_branch_ssm[0], w_out[0]]
    late_ride = _join_rides([_ag_weights_ride(s.astype(BF16), n_chunks=2) for s in late_shards])

    def unpack_late(pool, *squares):
        pool = pool.reshape(N_CHIPS, n_groups, n_pool, POOL_GW).transpose(1, 0, 2, 3)
        return (pool.reshape(n_groups, POOL_GW, POOL_GW), *[s.reshape(D_MODEL, D_MODEL) for s in squares])

    chip = 2 * lax.axis_index("x") + lax.axis_index("y")
    core = lax.axis_index("c").astype(jnp.int32)
    kept = {}

    def by_cols(a, n):
        return a.reshape(D_MODEL, N_CHIPS, n).transpose(1, 0, 2).reshape(N_CHIPS, -1, D_MODEL)

    def by_rows(a):
        return a.reshape(N_CHIPS, n_row, D_MODEL)

    def exchange_big(g):
        pool_by_chip = g["d_pool_w"].reshape(n_groups, N_CHIPS, n_pool, POOL_GW).transpose(1, 0, 2, 3)
        blocks = [by_cols(g["d_win"], n_in), by_rows(g["d_glu_w"]), by_rows(g["d_wbp"]), by_rows(g["d_wbs"]),
                  by_rows(g["d_wout"]), pool_by_chip.reshape(N_CHIPS, pool_rows, D_MODEL)]
        pad = (-sum(b.shape[1] for b in blocks)) % (2 * COMM_CHUNKS * COMM_ROW_ALIGN)
        if pad:
            blocks.append(jnp.zeros((N_CHIPS, pad, D_MODEL), F32))
        g_packed = jnp.concatenate(blocks, axis=1)
        kept["part_f32"], part_bf = _pair_add(g_packed, _rs_pair(g_packed), core.reshape(1))
        return _rs_chips_ride(part_bf)

    a_re, a_im, log_dt = ssm_a_re[0], ssm_a_im[0], ssm_log_dt[0].reshape(SSM_G, 1)
    b_re_t, b_im_t = ssm_b_re[0].transpose(2, 0, 1), ssm_b_im[0].transpose(2, 0, 1)
    early_names = ["dg2", "d_pscale", "d_glu_b", "d_dskip", "d_abar_re", "d_abar_im", "d_bb_re_t", "d_bb_im_t",
                   "d_c_re", "d_c_im"]

    def exchange_small(s):
        parts = [s[k] for k in early_names]
        kept["early_shapes"] = [p.shape for p in parts]
        return _small_allgather_ride(_pack_rows(parts, SUBLANES, COMM_CHUNKS * COMM_ROW_ALIGN))

    res = _local_step(x[0], c, loss_target[0], None, b_ada, norm_pre, norm_post, None, None, pool_scale,
                      a_re, a_im, log_dt, b_re_t, b_im_t, ssm_c_re[0], ssm_c_im[0], ssm_d[0], None, glu_b[0:1],
                      None, None, None,
                      split_proj=(w_in_own, chip.astype(jnp.int32).reshape(1), w_in_ride, unpack_w_in, late_ride, unpack_late),
                      ride_for_dw_in=exchange_small, ride_for_dh=exchange_big,
                      mod_fn=lambda c_row, bias: _mod_sharded(c_row, w_ada_own, bias))

    (all_early,) = res["rode_dw_in"]
    (g_norm_post, g_pscale, g_glu_b, g_dskip, s_abar_re, s_abar_im, s_bb_re, s_bb_im, g_c_re, g_c_im) = _unpack_rows(
        _sum_devices(all_early), kept["early_shapes"], SUBLANES)
    g_a_re, g_a_im, g_log_dt, g_b_re_t, g_b_im_t = _ssm_params_bwd(
        a_re, a_im, log_dt, b_re_t, b_im_t, s_abar_re.reshape(SSM_G, SSM_P), s_abar_im.reshape(SSM_G, SSM_P),
        s_bb_re, s_bb_im)
    late_parts = [res["dmod"], res["silu_c"], res["dg1"], res["loss"].reshape(1, 1)]
    late_shapes = [p.shape for p in late_parts]
    head_rows = _part_rows(late_shapes[0], SUBLANES) + _part_rows(late_shapes[1], SUBLANES)
    all_late, sum_late = _small_allgather_sum(_pack_rows(late_parts, SUBLANES, COMM_ROW_ALIGN), head_rows, n_chunks=1)
    g_b_ada, _, g_norm_pre, loss = _unpack_rows(sum_late, late_shapes, SUBLANES)
    loss = loss[0, 0]
    dmod_all = all_late[:, 0:3].reshape(N_DEV, 3 * D_MODEL)
    dmod_cols = lax.dynamic_slice_in_dim(dmod_all, chip * n_ada, n_ada, axis=1)
    silu_t = all_late[:, _part_rows(late_shapes[0], SUBLANES)].transpose(1, 0)
    g_w_ada = _wada_grad(silu_t, dmod_cols)

    (got_chips,) = res["rode"]
    shard = _rs_join(_chip_add(kept["part_f32"], got_chips, jnp.stack([chip.astype(jnp.int32), core])))
    r = 0
    g_w_in = shard[r:r + n_in].reshape(D_MODEL, n_in)
    r += n_in
    g_squares = []
    for _ in range(4):
        g_squares.append(shard[r:r + n_row])
        r += n_row
    g_glu_w, g_wbp, g_wbs, g_wout = g_squares
    g_pool_w = shard[r:r + pool_rows].reshape(n_groups * n_pool, POOL_GW)

    big = [("w_ada", w_ada[0], g_w_ada, m_w_ada[0], v_w_ada[0]),
           ("w_in", w_in[0], g_w_in, m_w_in[0], v_w_in[0]),
           ("pool_w", pool_w[0].reshape(n_groups * n_pool, POOL_GW), g_pool_w,
            m_pool_w[0].reshape(n_groups * n_pool, POOL_GW), v_pool_w[0].reshape(n_groups * n_pool, POOL_GW)),
           ("glu_w", glu_w[0], g_glu_w, m_glu_w[0], v_glu_w[0]),
           ("w_branch_pool", w_branch_pool[0], g_wbp, m_w_branch_pool[0], v_w_branch_pool[0]),
           ("w_branch_ssm", w_branch_ssm[0], g_wbs, m_w_branch_ssm[0], v_w_branch_ssm[0]),
           ("w_out", w_out[0], g_wout, m_w_out[0], v_w_out[0])]
    out = {}
    for name, w_, g_, m_, v_ in big:
        d_, nm_, nv_ = _adamw(w_, g_, m_, v_, "adamw_" + name)
        out[name] = (g_, d_, nm_, nv_)

    g_b_re = g_b_re_t.transpose(1, 2, 0)
    g_b_im = g_b_im_t.transpose(1, 2, 0)
    small = [("b_ada", b_ada, g_b_ada, m_b_ada, v_b_ada),
             ("norm_pre", norm_pre, g_norm_pre, m_norm_pre, v_norm_pre),
             ("norm_post", norm_post, g_norm_post, m_norm_post, v_norm_post),
             ("pool_scale", pool_scale, g_pscale, m_pool_scale, v_pool_scale),
             ("ssm_a_re", ssm_a_re, g_a_re, m_ssm_a_re, v_ssm_a_re),
             ("ssm_a_im", ssm_a_im, g_a_im, m_ssm_a_im, v_ssm_a_im),
             ("ssm_log_dt", ssm_log_dt, g_log_dt, m_ssm_log_dt, v_ssm_log_dt),
             ("ssm_b_re", ssm_b_re, g_b_re, m_ssm_b_re, v_ssm_b_re),
             ("ssm_b_im", ssm_b_im, g_b_im, m_ssm_b_im, v_ssm_b_im),
             ("ssm_c_re", ssm_c_re, g_c_re, m_ssm_c_re, v_ssm_c_re),
             ("ssm_c_im", ssm_c_im, g_c_im, m_ssm_c_im, v_ssm_c_im),
             ("ssm_d", ssm_d, g_dskip, m_ssm_d, v_ssm_d),
             ("glu_b", glu_b, g_glu_b, m_glu_b, v_glu_b)]
    small = [(name, w_, g_.reshape(w_.shape), m_, v_) for name, w_, g_, m_, v_ in small]
    updates = _adamw_small([t[1:] for t in small])
    for (name, _, g_, _, _), (d_, nm_, nv_) in zip(small, updates):
        out[name] = (g_, d_, nm_, nv_)

    order = ["w_ada", "b_ada", "norm_pre", "norm_post", "w_in", "pool_w", "pool_scale", "ssm_a_re", "ssm_a_im",
             "ssm_log_dt", "ssm_b_re", "ssm_b_im", "ssm_c_re", "ssm_c_im", "ssm_d", "glu_w", "glu_b", "w_branch_pool",
             "w_branch_ssm", "w_out"]
    ref_shape = dict(w_ada=w_ada.shape, w_in=w_in.shape, pool_w=pool_w.shape, glu_w=glu_w.shape,
                     w_branch_pool=w_branch_pool.shape, w_branch_ssm=w_branch_ssm.shape, w_out=w_out.shape)
    for name, w_, _, _, _ in small:
        ref_shape[name] = w_.shape
    results = [loss, res["grad_x"][None]]
    for k in range(4):
        results += [out[name][k].reshape(ref_shape[name]) for name in order]
    return tuple(results)
```

```python
import functools
import math

import numpy as np
import jax
import jax.numpy as jnp
from jax import lax
from jax.experimental import pallas as pl
from jax.experimental.pallas import tpu as pltpu

F32 = jnp.float32
BF16 = jnp.bfloat16
MESH_ID = pl.DeviceIdType.MESH

D_MODEL = 1024
LANES = 128
SUBLANES = 8
SSM_G, SSM_P, SSM_H = 64, 64, 16
LANE_BLOCKS = D_MODEL // LANES
GROUPS_PER_BLOCK = LANES // SSM_H
STATE_W = GROUPS_PER_BLOCK * SSM_P
STATE_ALL = SSM_G * SSM_P
POOL_WINDOWS = (2, 4, 8, 16)
POOL_GW = D_MODEL // len(POOL_WINDOWS)
HALO = 16
RMS_EPS = 1e-6
N_CHIPS = 4
N_DEV = 8

SCAN_CHUNK = 1024
SCAN_BLOCKS = 1
ROW_CHUNK = 512
ROW_CHUNK_WIDE = 512
PROJ_ROWS = 1024
VMEM_LIMIT_BYTES = 56 * 1024 * 1024

ADAM_BLOCK_BYTES = 1 << 20
ADAM_LR, ADAM_B1, ADAM_B2, ADAM_EPS, ADAM_WD, ADAM_STEP = 0.001, 0.9, 0.999, 1e-08, 0.01, 10

_GELU_C0 = math.sqrt(2.0 / math.pi)
_GELU_C1 = 0.044715


def _cparams(*sem):
    if sem:
        return pltpu.CompilerParams(dimension_semantics=sem, vmem_limit_bytes=VMEM_LIMIT_BYTES)
    return pltpu.CompilerParams(vmem_limit_bytes=VMEM_LIMIT_BYTES)


def _sigmoid(v):
    return jax.nn.sigmoid(v)


def _silu(v):
    return v * _sigmoid(v)


def _dsilu(v):
    s = _sigmoid(v)
    return s * (1.0 + v * (1.0 - s))


def _gelu(v):
    return v * (0.5 * (1.0 + jnp.tanh(_GELU_C0 * v * (1.0 + _GELU_C1 * (v * v)))))


def _gelu_and_grad(v):
    v2 = v * v
    t = jnp.tanh(_GELU_C0 * v * (1.0 + _GELU_C1 * v2))
    half = 0.5 * (1.0 + t)
    grad = half + (0.5 * _GELU_C0) * v * (1.0 - t * t) * (1.0 + (3.0 * _GELU_C1) * v2)
    return v * half, grad


def _silu_and_grad(v):
    s = _sigmoid(v)
    return v * s, s * (1.0 + v * (1.0 - s))


def _dot(a, b):
    return lax.dot_general(a, b, (((1,), (0,)), ((), ())), preferred_element_type=F32)


def _dot_nt(a, b):
    return lax.dot_general(a, b, (((1,), (1,)), ((), ())), preferred_element_type=F32)


def _dot_tn(a, b):
    return lax.dot_general(a, b, (((0,), (0,)), ((), ())), preferred_element_type=F32)


def _acc8(v):
    return v.reshape(v.shape[0] // SUBLANES, SUBLANES, v.shape[1]).sum(axis=0)


class _Ride:
    def __init__(self, inputs, out_shapes, scratch, start, wait):
        self.inputs, self.out_shapes, self.scratch, self.start, self.wait = inputs, out_shapes, scratch, start, wait


def _mm(a_parts, b_parts, *, name, ta=False, tb=False, out_dtype=F32, bm=512, bn=512, bk=512, ride=None):
    a_parts, b_parts = list(a_parts), list(b_parts)
    if ta:
        assert len(a_parts) == 1
        k_dim, m_dim = a_parts[0].shape
    else:
        m_dim = a_parts[0].shape[0]
        k_dim = sum(a.shape[1] for a in a_parts)
    if tb:
        assert len(b_parts) == 1
        n_dim = b_parts[0].shape[0]
    else:
        n_dim = sum(b.shape[1] for b in b_parts)
    bm, bn, bk = min(bm, m_dim), min(bn, n_dim), min(bk, k_dim)
    nm, nn, nk = m_dim // bm, n_dim // bn, k_dim // bk
    a_ranges, off = [], 0
    for a in a_parts:
        cnt = (a.shape[0] if ta else a.shape[1]) // bk
        a_ranges.append((off, cnt))
        off += cnt
    b_ranges, off = [], 0
    for b in b_parts:
        cnt = (b.shape[0] if tb else b.shape[1]) // bn
        b_ranges.append((off, cnt))
        off += cnt

    def a_spec(off, cnt):
        if ta:
            return pl.BlockSpec((bk, bm), lambda i, n, k: (k, i))
        return pl.BlockSpec((bm, bk), lambda i, n, k: (i, jnp.clip(k - off, 0, cnt - 1)))

    def b_spec(off, cnt):
        if tb:
            return pl.BlockSpec((bn, bk), lambda i, n, k: (n, k))
        return pl.BlockSpec((bk, bn), lambda i, n, k: (k, jnp.clip(n - off, 0, cnt - 1)))

    na, nb = len(a_parts), len(b_parts)
    dims = (((0 if ta else 1,), (1 if tb else 0,)), ((), ()))

    def kern_single(a_ref, b_ref, o_ref):
        o_ref[...] = lax.dot_general(a_ref[...].astype(BF16), b_ref[...].astype(BF16), dims,
                                     preferred_element_type=F32).astype(out_dtype)

    if na == 1 and nb == 1 and nk == 1 and not ride:
        return pl.pallas_call(
            kern_single, name=name, grid=(nm, nn),
            in_specs=[pl.BlockSpec((bk, bm), lambda i, n: (0, i)) if ta else pl.BlockSpec((bm, bk), lambda i, n: (i, 0)),
                      pl.BlockSpec((bn, bk), lambda i, n: (n, 0)) if tb else pl.BlockSpec((bk, bn), lambda i, n: (0, n))],
            out_specs=pl.BlockSpec((bm, bn), lambda i, n: (i, n)),
            out_shape=jax.ShapeDtypeStruct((m_dim, n_dim), out_dtype),
            compiler_params=_cparams("parallel", "parallel"),
        )(a_parts[0], b_parts[0])

    n_rin = len(ride.inputs) if ride else 0
    n_rout = len(ride.out_shapes) if ride else 0

    def kern(*refs):
        a_refs, b_refs = refs[:na], refs[na:na + nb]
        rin = refs[na + nb:na + nb + n_rin]
        o_ref = refs[na + nb + n_rin]
        rout = refs[na + nb + n_rin + 1:na + nb + n_rin + 1 + n_rout]
        acc = refs[na + nb + n_rin + 1 + n_rout]
        rsem = refs[na + nb + n_rin + 2 + n_rout:]
        i, n, k = pl.program_id(0), pl.program_id(1), pl.program_id(2)

        if ride:
            @pl.when((i == 0) & (n == 0) & (k == 0))
            def _():
                ride.start(rin, rout, rsem)

        if nk > 1:
            @pl.when(k == 0)
            def _():
                acc[...] = jnp.zeros_like(acc)

        for ja, (koff, kcnt) in enumerate(a_ranges):
            for jb, (noff, ncnt) in enumerate(b_ranges):
                def step(ja=ja, jb=jb):
                    a = a_refs[ja][...].astype(BF16)
                    b = b_refs[jb][...].astype(BF16)
                    prod = lax.dot_general(a, b, dims, preferred_element_type=F32)
                    if nk > 1:
                        acc[...] += prod
                    else:
                        o_ref[...] = prod.astype(out_dtype)

                if na == 1 and nb == 1:
                    step()
                else:
                    cond = (k >= koff) & (k < koff + kcnt) & (n >= noff) & (n < noff + ncnt)
                    pl.when(cond)(step)

        if nk > 1:
            @pl.when(k == nk - 1)
            def _():
                o_ref[...] = acc[...].astype(out_dtype)

        if ride:
            @pl.when((i == nm - 1) & (n == nn - 1) & (k == nk - 1))
            def _():
                ride.wait(rin, rout, rsem)

    any_spec = pl.BlockSpec(memory_space=pl.ANY)
    out_spec = pl.BlockSpec((bm, bn), lambda i, n, k: (i, n))
    out_shape = jax.ShapeDtypeStruct((m_dim, n_dim), out_dtype)
    acc_shape = pltpu.VMEM((bm, bn) if nk > 1 else (SUBLANES, LANES), F32)
    if not ride:
        return pl.pallas_call(
            kern, name=name, grid=(nm, nn, nk),
            in_specs=[a_spec(*r) for r in a_ranges] + [b_spec(*r) for r in b_ranges],
            out_specs=out_spec, out_shape=out_shape, scratch_shapes=[acc_shape],
            compiler_params=_cparams("parallel", "parallel", "arbitrary"),
        )(*a_parts, *b_parts)
    return pl.pallas_call(
        kern, name=name, grid=(nm, nn, nk),
        in_specs=[a_spec(*r) for r in a_ranges] + [b_spec(*r) for r in b_ranges] + [any_spec] * n_rin,
        out_specs=(out_spec,) + (any_spec,) * n_rout, out_shape=(out_shape,) + tuple(ride.out_shapes),
        scratch_shapes=[acc_shape] + list(ride.scratch),
        compiler_params=_cparams("arbitrary", "arbitrary", "arbitrary"),
    )(*a_parts, *b_parts, *ride.inputs)


def _ssm_param_fn(a_re, a_im, log_dt, b_re, b_im):
    dt = jnp.exp(log_dt)
    lam_re = jnp.minimum(a_re, -1e-4)
    lam_im = a_im
    mag = jnp.exp(lam_re * dt)
    abar_re = mag * jnp.cos(lam_im * dt)
    abar_im = mag * jnp.sin(lam_im * dt)
    den = lam_re * lam_re + lam_im * lam_im
    num_re = abar_re - 1.0
    f_re = (num_re * lam_re + abar_im * lam_im) / den
    f_im = (abar_im * lam_re - num_re * lam_im) / den
    bb_re = f_re * b_re - f_im * b_im
    bb_im = f_re * b_im + f_im * b_re
    return abar_re, abar_im, bb_re, bb_im


def _ssm_params(a_re, a_im, log_dt, b_re_t, b_im_t):
    def kern(are, aim, ldt, bre, bim, o_ar, o_ai, o_br, o_bi):
        ar, ai, br, bi = _ssm_param_fn(are[...], aim[...], ldt[...], bre[...], bim[...])
        o_ar[...] = ar
        o_ai[...] = ai
        o_br[...] = br
        o_bi[...] = bi

    gp = jax.ShapeDtypeStruct((SSM_G, SSM_P), F32)
    hgp = jax.ShapeDtypeStruct((SSM_H, SSM_G, SSM_P), F32)
    return pl.pallas_call(kern, name="ssm_params", out_shape=(gp, gp, hgp, hgp), compiler_params=_cparams())(
        a_re, a_im, log_dt, b_re_t, b_im_t)


def _ssm_params_bwd(a_re, a_im, log_dt, b_re_t, b_im_t, d_ar, d_ai, d_bbr, d_bbi):
    def kern(are, aim, ldt, bre, bim, dar, dai, dbr, dbi, o_are, o_aim, o_ldt, o_bre, o_bim):
        prim = (are[...], aim[...], ldt[...], bre[...], bim[...])
        _, vjp = jax.vjp(_ssm_param_fn, *prim)
        g = vjp((dar[...], dai[...], dbr[...], dbi[...]))
        o_are[...] = g[0]
        o_aim[...] = g[1]
        o_ldt[...] = g[2]
        o_bre[...] = g[3]
        o_bim[...] = g[4]

    gp = jax.ShapeDtypeStruct((SSM_G, SSM_P), F32)
    g1 = jax.ShapeDtypeStruct((SSM_G, 1), F32)
    hgp = jax.ShapeDtypeStruct((SSM_H, SSM_G, SSM_P), F32)
    return pl.pallas_call(kern, name="ssm_params_bwd", out_shape=(gp, gp, g1, hgp, hgp), compiler_params=_cparams())(
        a_re, a_im, log_dt, b_re_t, b_im_t, d_ar, d_ai, d_bbr, d_bbi)


def _pow_tables(abar_re, abar_im, tc):
    ls = tc // SUBLANES

    def kern(ar_ref, ai_ref, fr_ref, fi_ref, rr_ref, ri_ref):
        a_re = jnp.broadcast_to(ar_ref[...], (SUBLANES, STATE_W))
        a_im = jnp.broadcast_to(ai_ref[...], (SUBLANES, STATE_W))
        p_re, p_im = a_re, a_im
        for i in range(ls):
            fwd = pl.ds(SUBLANES * i, SUBLANES)
            rev = pl.ds(SUBLANES * (ls - 1 - i), SUBLANES)
            fr_ref[fwd, :] = p_re
            fi_ref[fwd, :] = p_im
            rr_ref[rev, :] = p_re
            ri_ref[rev, :] = p_im
            p_re, p_im = p_re * a_re - p_im * a_im, p_re * a_im + p_im * a_re

    vec = pl.BlockSpec((1, STATE_W), lambda b: (0, b))
    tab = pl.BlockSpec((tc, STATE_W), lambda b: (0, b))
    shp = jax.ShapeDtypeStruct((tc, STATE_ALL), F32)
    return pl.pallas_call(
        kern, name="pow_tables", grid=(LANE_BLOCKS,), in_specs=[vec, vec], out_specs=(tab, tab, tab, tab),
        out_shape=(shp, shp, shp, shp), compiler_params=_cparams("parallel"))(abar_re, abar_im)


def _mod_kernel(c_row, w_ada_bf, b_ada):
    def kern(c_ref, w_ref, b_ref, m_ref, s_ref):
        cv = c_ref[...]
        sc = _silu(cv)
        s_ref[...] = sc
        lhs = jnp.broadcast_to(sc, (SUBLANES, D_MODEL)).astype(BF16)
        m_ref[...] = _dot(lhs, w_ref[...]) + b_ref[...]

    return pl.pallas_call(
        kern, name="ada_mod",
        out_shape=(jax.ShapeDtypeStruct((SUBLANES, 3 * D_MODEL), F32), jax.ShapeDtypeStruct((1, D_MODEL), F32)),
        compiler_params=_cparams())(c_row, w_ada_bf, b_ada)


def _row_spec(tr, width=D_MODEL, col=0):
    return pl.BlockSpec((tr, width), lambda c: (c, col))


def _vec_spec(width=D_MODEL):
    return pl.BlockSpec((1, width), lambda c: (0, 0))


def _col_spec(tr):
    return pl.BlockSpec((D_MODEL, tr), lambda c: (0, c))


def _in_norm(x, g1, scale, shift):
    seq = x.shape[0]
    tr = min(ROW_CHUNK_WIDE, seq)

    def kern(x_ref, g_ref, sc_ref, sh_ref, h_ref, ht_ref):
        xv = x_ref[...]
        r = lax.rsqrt(jnp.mean(xv * xv, axis=-1, keepdims=True) + RMS_EPS)
        h = ((xv * r) * g_ref[...]) * (1.0 + sc_ref[...]) + sh_ref[...]
        h_ref[...] = h.astype(BF16)
        ht_ref[...] = h.T.astype(BF16)

    return pl.pallas_call(
        kern, name="in_norm", grid=(seq // tr,),
        in_specs=[_row_spec(tr), _vec_spec(), _vec_spec(), _vec_spec()], out_specs=(_row_spec(tr), _col_spec(tr)),
        out_shape=(jax.ShapeDtypeStruct((seq, D_MODEL), BF16), jax.ShapeDtypeStruct((D_MODEL, seq), BF16)),
        compiler_params=_cparams("parallel"))(x, g1, scale, shift)


PAD = SUBLANES


def _window_sums(src, cols, w, bufs, rows, ahead):
    cur, cur_cols, step, k = src, cols, 1, 0
    data = pl.ds(PAD, rows)
    while step < w:
        dst = bufs[k % 2]
        dst[data, :] = cur[data, cur_cols] + cur[pl.ds(PAD + (step if ahead else -step), rows), cur_cols]
        cur, cur_cols, step, k = dst, slice(None), 2 * step, k + 1
    return cur, cur_cols


def _in_norm_proj_own(x, g1, scale, shift, w_own, chip, ride):
    seq, n_own = x.shape[0], w_own.shape[1]
    tr = min(PROJ_ROWS, seq)
    nc = seq // tr
    n_rin, n_rout = len(ride.inputs), len(ride.out_shapes)

    def kern(chip_ref, x_ref, g_ref, sc_ref, sh_ref, w_ref, *rest):
        rin, (h_ref, ht_ref, p_ref) = rest[:n_rin], rest[n_rin:n_rin + 3]
        rout, rsem = rest[n_rin + 3:n_rin + 3 + n_rout], rest[n_rin + 3 + n_rout:]
        c = pl.program_id(0)

        @pl.when(c == 0)
        def _():
            ride.start(rin, rout, rsem)

        xv = x_ref[...]
        r = lax.rsqrt(jnp.mean(xv * xv, axis=-1, keepdims=True) + RMS_EPS)
        h = ((xv * r) * g_ref[...]) * (1.0 + sc_ref[...]) + sh_ref[...]
        hb = h.astype(BF16)
        h_ref[...] = hb
        ht_ref[...] = h.T.astype(BF16)
        p_ref[...] = _dot(hb, w_ref[...]).astype(BF16)

        @pl.when(c == nc - 1)
        def _():
            ride.wait(rin, rout, rsem)

    vec = pl.BlockSpec((1, D_MODEL), lambda c, k: (0, 0))
    return pl.pallas_call(
        kern, name="in_norm_proj_own",
        grid_spec=pltpu.PrefetchScalarGridSpec(
            num_scalar_prefetch=1, grid=(nc,),
            in_specs=[pl.BlockSpec((tr, D_MODEL), lambda c, k: (c, 0)), vec, vec, vec,
                      pl.BlockSpec((D_MODEL, n_own), lambda c, k: (0, 0))] + [_ANY] * n_rin,
            out_specs=(pl.BlockSpec((tr, D_MODEL), lambda c, k: (c, 0)), pl.BlockSpec((D_MODEL, tr), lambda c, k: (0, c)),
                       pl.BlockSpec((tr, n_own), lambda c, k: (c, k[0]))) + (_ANY,) * n_rout,
            scratch_shapes=list(ride.scratch)),
        out_shape=(jax.ShapeDtypeStruct((seq, D_MODEL), BF16), jax.ShapeDtypeStruct((D_MODEL, seq), BF16),
                   jax.ShapeDtypeStruct((seq, N_CHIPS * n_own), BF16)) + tuple(ride.out_shapes),
        compiler_params=_cparams("arbitrary"))(chip, x, g1, scale, shift, w_own, *ride.inputs)


def _proj_rest(h, w_blocks, proj, chip, ride):
    seq, n_own = h.shape[0], w_blocks.shape[2]
    tr = min(PROJ_ROWS, seq)
    nm, nn = seq // tr, N_CHIPS - 1
    n_rin, n_rout = len(ride.inputs), len(ride.out_shapes)

    def kern(chip_ref, h_ref, w_ref, _, *rest):
        rin, p_ref = rest[:n_rin], rest[n_rin]
        rout, rsem = rest[n_rin + 1:n_rin + 1 + n_rout], rest[n_rin + 1 + n_rout:]
        i, n = pl.program_id(0), pl.program_id(1)

        @pl.when((i == 0) & (n == 0))
        def _():
            ride.start(rin, rout, rsem)

        p_ref[...] = _dot(h_ref[...], w_ref[0]).astype(BF16)

        @pl.when((i == nm - 1) & (n == nn - 1))
        def _():
            ride.wait(rin, rout, rsem)

    def other(n, k):
        return (k[0] + 1 + n) % N_CHIPS

    return pl.pallas_call(
        kern, name="proj_rest",
        grid_spec=pltpu.PrefetchScalarGridSpec(
            num_scalar_prefetch=1, grid=(nm, nn),
            in_specs=[pl.BlockSpec((tr, D_MODEL), lambda i, n, k: (i, 0)),
                      pl.BlockSpec((1, D_MODEL, n_own), lambda i, n, k: (other(n, k), 0, 0)), _ANY] + [_ANY] * n_rin,
            out_specs=(pl.BlockSpec((tr, n_own), lambda i, n, k: (i, other(n, k))),) + (_ANY,) * n_rout,
            scratch_shapes=list(ride.scratch)),
        out_shape=(jax.ShapeDtypeStruct(proj.shape, BF16),) + tuple(ride.out_shapes),
        input_output_aliases={3: 0},
        compiler_params=_cparams("arbitrary", "arbitrary"))(chip, h, w_blocks, proj, *ride.inputs)


def _pool_windows(ext, bufs, pos, g, w, tr):
    cols = pl.ds(g * POOL_GW, POOL_GW)
    chunk = pl.ds(PAD + HALO, tr)
    cur = ext[chunk, cols]
    win, win_cols = _window_sums(ext, cols, w, bufs, HALO + tr, ahead=False)
    cnt = jnp.minimum(pos + 1, w).astype(F32)
    return win[chunk, win_cols] / cnt - cur


def _zero_pads(refs, rows):
    for ref in refs:
        ref[0:PAD, :] = jnp.zeros((PAD, ref.shape[1]), F32)
        ref[PAD + rows:, :] = jnp.zeros((PAD, ref.shape[1]), F32)


def _pool_fwd(proj, pool_w_bf, pscale):
    seq = proj.shape[0]
    tr = min(ROW_CHUNK_WIDE, seq)
    hb = tr // HALO

    def kern(up_ref, halo_ref, zp_ref, pw_ref, ps_ref, y_ref, yt_ref, ext, buf_a, buf_b):
        c = pl.program_id(0)
        _zero_pads((ext, buf_a, buf_b), HALO + tr)
        ext[pl.ds(PAD, HALO), :] = jnp.where(c > 0, halo_ref[...].astype(F32), 0.0)
        ext[pl.ds(PAD + HALO, tr), :] = up_ref[...].astype(F32)
        pos = c * tr + lax.broadcasted_iota(jnp.int32, (tr, POOL_GW), 0)
        for g, w in enumerate(POOL_WINDOWS):
            cols = pl.ds(g * POOL_GW, POOL_GW)
            pooled = _pool_windows(ext, (buf_a, buf_b), pos, g, w, tr)
            mixed = _dot(pooled.astype(BF16), pw_ref[g])
            y = mixed * ps_ref[:, cols] * _silu(zp_ref[:, cols].astype(F32))
            y_ref[:, cols] = y.astype(BF16)
            yt_ref[cols, :] = y.T.astype(BF16)

    return pl.pallas_call(
        kern, name="pool_fwd", grid=(seq // tr,),
        in_specs=[_row_spec(tr, col=0),
                  pl.BlockSpec((HALO, D_MODEL), lambda c: (jnp.maximum(c * hb - 1, 0), 0)),
                  _row_spec(tr, col=1),
                  pl.BlockSpec((len(POOL_WINDOWS), POOL_GW, POOL_GW), lambda c: (0, 0, 0)),
                  _vec_spec()],
        out_specs=(_row_spec(tr), _col_spec(tr)),
        out_shape=(jax.ShapeDtypeStruct((seq, D_MODEL), BF16), jax.ShapeDtypeStruct((D_MODEL, seq), BF16)),
        scratch_shapes=[pltpu.VMEM((tr + HALO + 2 * PAD, D_MODEL), F32), pltpu.VMEM((tr + HALO + 2 * PAD, POOL_GW), F32),
                        pltpu.VMEM((tr + HALO + 2 * PAD, POOL_GW), F32)],
        compiler_params=_cparams("parallel"))(proj, proj, proj, pool_w_bf, pscale)


def _pool_bwd(proj, dyp, pool_w_bf, pscale, dproj):
    seq = proj.shape[0]
    tr = min(ROW_CHUNK_WIDE, seq)
    hb = tr // HALO
    nc = seq // tr
    n_halo = seq // HALO

    def kern(up_ref, halo_ref, zp_ref, zpn_ref, dyp_ref, dypn_ref, pw_ref, ps_ref, _,
             d01_ref, dpw_ref, dps_ref, ext, dpn, buf_a, buf_b, acc_pw, acc_ps):
        c = pl.program_id(0)

        @pl.when(c == 0)
        def _():
            acc_pw[...] = jnp.zeros_like(acc_pw)
            acc_ps[...] = jnp.zeros_like(acc_ps)

        _zero_pads((ext, dpn, buf_a, buf_b), HALO + tr)
        ext[pl.ds(PAD, HALO), :] = jnp.where(c > 0, halo_ref[...].astype(F32), 0.0)
        ext[pl.ds(PAD + HALO, tr), :] = up_ref[...].astype(F32)
        pos = c * tr + lax.broadcasted_iota(jnp.int32, (tr, POOL_GW), 0)
        pos_n = (c + 1) * tr + lax.broadcasted_iota(jnp.int32, (HALO, POOL_GW), 0)
        has_next = c < nc - 1
        for g, w in enumerate(POOL_WINDOWS):
            cols = pl.ds(g * POOL_GW, POOL_GW)
            pooled_bf = _pool_windows(ext, (buf_a, buf_b), pos, g, w, tr).astype(BF16)
            wg = pw_ref[g]
            mixed = _dot(pooled_bf, wg)
            zp = zp_ref[:, cols].astype(F32)
            sz = _silu(zp)
            dyp_g = dyp_ref[:, cols].astype(F32)
            ps = ps_ref[:, cols]
            dmixed = (dyp_g * ps * sz).astype(BF16)
            acc_ps[:, cols] += _acc8(dyp_g * mixed * sz)
            d01_ref[:, pl.ds(D_MODEL + g * POOL_GW, POOL_GW)] = (dyp_g * mixed * ps * _dsilu(zp)).astype(BF16)
            acc_pw[g] += _dot_tn(pooled_bf, dmixed)
            dpooled = _dot_nt(dmixed, wg)
            dmixed_n = (jnp.where(has_next, dypn_ref[:, cols].astype(F32), 0.0) * ps * _silu(zpn_ref[:, cols].astype(F32))).astype(BF16)
            dpooled_n = _dot_nt(dmixed_n, wg)
            dpn[pl.ds(PAD, tr), :] = dpooled / jnp.minimum(pos + 1, w).astype(F32)
            dpn[pl.ds(PAD + tr, HALO), :] = dpooled_n / jnp.minimum(pos_n + 1, w).astype(F32)
            win, _ = _window_sums(dpn, slice(None), w, (buf_a, buf_b), tr + HALO, ahead=True)
            d01_ref[:, cols] = (win[pl.ds(PAD, tr), :] - dpooled).astype(BF16)

        @pl.when(c == nc - 1)
        def _():
            dpw_ref[...] = acc_pw[...]
            dps_ref[...] = jnp.sum(acc_ps[...], axis=0, keepdims=True)

    nxt = lambda c: (jnp.minimum((c + 1) * hb, n_halo - 1), 0)
    nxt1 = lambda c: (jnp.minimum((c + 1) * hb, n_halo - 1), 1)
    return pl.pallas_call(
        kern, name="pool_bwd", grid=(nc,),
        in_specs=[_row_spec(tr, col=0),
                  pl.BlockSpec((HALO, D_MODEL), lambda c: (jnp.maximum(c * hb - 1, 0), 0)),
                  _row_spec(tr, col=1),
                  pl.BlockSpec((HALO, D_MODEL), nxt1),
                  _row_spec(tr),
                  pl.BlockSpec((HALO, D_MODEL), nxt),
                  pl.BlockSpec((len(POOL_WINDOWS), POOL_GW, POOL_GW), lambda c: (0, 0, 0)),
                  _vec_spec(), _ANY],
        out_specs=(pl.BlockSpec((tr, 2 * D_MODEL), lambda c: (c, 0)),
                   pl.BlockSpec((len(POOL_WINDOWS), POOL_GW, POOL_GW), lambda c: (0, 0, 0)),
                   _vec_spec()),
        out_shape=(jax.ShapeDtypeStruct(dproj.shape, BF16),
                   jax.ShapeDtypeStruct((len(POOL_WINDOWS), POOL_GW, POOL_GW), F32),
                   jax.ShapeDtypeStruct((1, D_MODEL), F32)),
        scratch_shapes=[pltpu.VMEM((tr + HALO + 2 * PAD, D_MODEL), F32)]
        + [pltpu.VMEM((tr + HALO + 2 * PAD, POOL_GW), F32)] * 3
        + [pltpu.VMEM((len(POOL_WINDOWS), POOL_GW, POOL_GW), F32), pltpu.VMEM((SUBLANES, D_MODEL), F32)],
        input_output_aliases={8: 0},
        compiler_params=_cparams("arbitrary"))(proj, proj, proj, proj, dyp, dyp, pool_w_bf, pscale, dproj)


def _glu_fwd(ys, proj, glu_w_bf, glu_b):
    seq = ys.shape[0]
    tr = min(ROW_CHUNK_WIDE, seq)

    def kern(ys_ref, zs_ref, w_ref, b_ref, o_ref, ot_ref):
        yg = _gelu(ys_ref[...])
        q = _dot(yg.astype(BF16), w_ref[...]) + b_ref[...]
        y = yg * _sigmoid(q) * _silu(zs_ref[...].astype(F32))
        o_ref[...] = y.astype(BF16)
        ot_ref[...] = y.T.astype(BF16)

    return pl.pallas_call(
        kern, name="glu_fwd", grid=(seq // tr,),
        in_specs=[_row_spec(tr), _row_spec(tr, col=3), pl.BlockSpec((D_MODEL, D_MODEL), lambda c: (0, 0)), _vec_spec()],
        out_specs=(_row_spec(tr), _col_spec(tr)),
        out_shape=(jax.ShapeDtypeStruct((seq, D_MODEL), BF16), jax.ShapeDtypeStruct((D_MODEL, seq), BF16)),
        compiler_params=_cparams("parallel"))(ys, proj, glu_w_bf, glu_b)


def _glu_bwd(ys, proj, dyssm, glu_w_bf, glu_b, dproj):
    seq = ys.shape[0]
    tr = min(ROW_CHUNK_WIDE, seq)
    nc = seq // tr

    def kern(ys_ref, zs_ref, dy_ref, w_ref, b_ref, _, dys_ref, dzs_ref, dq_ref, yg_ref, db_ref, acc_b):
        c = pl.program_id(0)

        @pl.when(c == 0)
        def _():
            acc_b[...] = jnp.zeros_like(acc_b)

        yg, dgelu = _gelu_and_grad(ys_ref[...])
        yg_bf = yg.astype(BF16)
        q = _dot(yg_bf, w_ref[...]) + b_ref[...]
        sg = _sigmoid(q)
        silu_z, dsilu_z = _silu_and_grad(zs_ref[...].astype(F32))
        dyv = dy_ref[...].astype(F32)
        dyglu = dyv * silu_z
        yglu = yg * sg
        dzs_ref[...] = (dyv * yglu * dsilu_z).astype(BF16)
        dq = dyglu * yglu * (1.0 - sg)
        dq_bf = dq.astype(BF16)
        acc_b[...] += _acc8(dq)
        dyg = dyglu * sg + _dot_nt(dq_bf, w_ref[...])
        dys_ref[...] = dyg * dgelu
        dq_ref[...] = dq_bf
        yg_ref[...] = yg.T.astype(BF16)

        @pl.when(c == nc - 1)
        def _():
            db_ref[...] = jnp.sum(acc_b[...], axis=0, keepdims=True)

    bf = jax.ShapeDtypeStruct((seq, D_MODEL), BF16)
    return pl.pallas_call(
        kern, name="glu_bwd", grid=(nc,),
        in_specs=[_row_spec(tr), _row_spec(tr, col=3), _row_spec(tr),
                  pl.BlockSpec((D_MODEL, D_MODEL), lambda c: (0, 0)), _vec_spec(), _ANY],
        out_specs=(_row_spec(tr), _row_spec(tr, col=3), _row_spec(tr), _col_spec(tr), _vec_spec()),
        out_shape=(jax.ShapeDtypeStruct((seq, D_MODEL), F32), jax.ShapeDtypeStruct(dproj.shape, BF16), bf,
                   jax.ShapeDtypeStruct((D_MODEL, seq), BF16), jax.ShapeDtypeStruct((1, D_MODEL), F32)),
        scratch_shapes=[pltpu.VMEM((SUBLANES, D_MODEL), F32)],
        input_output_aliases={5: 1},
        compiler_params=_cparams("arbitrary"))(ys, proj, dyssm, glu_w_bf, glu_b, dproj)


def _out_fwd_bwd(ypool, yssm, proj, x, tgt, gate, g2, wbp_bf, wbs_bf, wout_bf):
    seq = x.shape[0]
    tr = min(ROW_CHUNK, seq)
    nc = seq // tr

    def kern(yp_ref, ysm_ref, gp_ref, gs_ref, x_ref, t_ref, gate_ref, g2_ref, wbp_ref, wbs_ref, wo_ref,
             dy_ref, dyp_ref, dys_ref, d45_ref, mb_ref, dob_ref, dbp_ref, dbs_ref, loss_ref, dgate_ref, dg2_ref,
             acc_l, acc_gate, acc_g2):
        c = pl.program_id(0)

        @pl.when(c == 0)
        def _():
            acc_l[...] = jnp.zeros_like(acc_l)
            acc_gate[...] = jnp.zeros_like(acc_gate)
            acc_g2[...] = jnp.zeros_like(acc_g2)

        bp = _dot(yp_ref[...], wbp_ref[...])
        bs = _dot(ysm_ref[...], wbs_ref[...])
        sp = _sigmoid(gp_ref[...].astype(F32))
        ss = _sigmoid(gs_ref[...].astype(F32))
        merged = sp * bp + ss * bs
        mb = merged.astype(BF16)
        out = _dot(mb, wo_ref[...])
        r2 = lax.rsqrt(jnp.mean(out * out, axis=-1, keepdims=True) + RMS_EPS)
        oh = out * r2
        gate_v, g2_v = gate_ref[...], g2_ref[...]
        ohg = oh * g2_v
        diff = (x_ref[...] + gate_v * ohg) - t_ref[...]
        acc_l[...] += _acc8(diff * diff)
        dyv = diff * (1.0 / D_MODEL)
        dy_ref[...] = dyv
        dy_oh = dyv * oh
        acc_gate[...] += _acc8(dy_oh * g2_v)
        acc_g2[...] += _acc8(dy_oh * gate_v)
        gg = gate_v * g2_v
        doh = dyv * gg
        dout = r2 * (doh - oh * jnp.mean(dy_oh * gg, axis=-1, keepdims=True))
        dob = dout.astype(BF16)
        dmerged = _dot_nt(dob, wo_ref[...])
        dbp_f = dmerged * sp
        dbs_f = dmerged * ss
        dbp = dbp_f.astype(BF16)
        dbs = dbs_f.astype(BF16)
        d45_ref[:, 0:D_MODEL] = (dbp_f * bp * (1.0 - sp)).astype(BF16)
        d45_ref[:, D_MODEL:] = (dbs_f * bs * (1.0 - ss)).astype(BF16)
        dyp_ref[...] = _dot_nt(dbp, wbp_ref[...]).astype(BF16)
        dys_ref[...] = _dot_nt(dbs, wbs_ref[...]).astype(BF16)
        mb_ref[...] = merged.T.astype(BF16)
        dob_ref[...] = dob
        dbp_ref[...] = dbp
        dbs_ref[...] = dbs

        @pl.when(c == nc - 1)
        def _():
            tot = jnp.sum(acc_l[...], axis=0, keepdims=True)
            loss_ref[...] = jnp.sum(tot, axis=1, keepdims=True) * (0.5 / D_MODEL)
            dgate_ref[...] = jnp.sum(acc_gate[...], axis=0, keepdims=True)
            dg2_ref[...] = jnp.sum(acc_g2[...], axis=0, keepdims=True)

    wspec = pl.BlockSpec((D_MODEL, D_MODEL), lambda c: (0, 0))
    f32 = jax.ShapeDtypeStruct((seq, D_MODEL), F32)
    bf = jax.ShapeDtypeStruct((seq, D_MODEL), BF16)
    vec = jax.ShapeDtypeStruct((1, D_MODEL), F32)
    acc = pltpu.VMEM((SUBLANES, D_MODEL), F32)
    return pl.pallas_call(
        kern, name="out_fwd_bwd", grid=(nc,),
        in_specs=[_row_spec(tr), _row_spec(tr), _row_spec(tr, col=4), _row_spec(tr, col=5), _row_spec(tr), _row_spec(tr),
                  _vec_spec(), _vec_spec(), wspec, wspec, wspec],
        out_specs=(_row_spec(tr), _row_spec(tr), _row_spec(tr), pl.BlockSpec((tr, 2 * D_MODEL), lambda c: (c, 2)),
                   _col_spec(tr), _row_spec(tr), _row_spec(tr), _row_spec(tr),
                   pl.BlockSpec((1, 1), lambda c: (0, 0)), _vec_spec(), _vec_spec()),
        out_shape=(f32, bf, bf, jax.ShapeDtypeStruct((seq, proj.shape[1]), BF16),
                   jax.ShapeDtypeStruct((D_MODEL, seq), BF16), bf, bf, bf,
                   jax.ShapeDtypeStruct((1, 1), F32), vec, vec),
        scratch_shapes=[acc, acc, acc],
        compiler_params=_cparams("arbitrary"))(ypool, yssm, proj, proj, x, tgt, gate, g2, wbp_bf, wbs_bf, wout_bf)


def _in_bwd(dh, x, dy, g1, scale):
    seq = x.shape[0]
    tr = min(ROW_CHUNK_WIDE, seq)
    nc = seq // tr

    def kern(dh_ref, x_ref, dy_ref, g_ref, sc_ref, dx_ref, dsh_ref, dsc_ref, dg_ref, a_sh, a_sc, a_g):
        c = pl.program_id(0)

        @pl.when(c == 0)
        def _():
            a_sh[...] = jnp.zeros_like(a_sh)
            a_sc[...] = jnp.zeros_like(a_sc)
            a_g[...] = jnp.zeros_like(a_g)

        xv = x_ref[...]
        r = lax.rsqrt(jnp.mean(xv * xv, axis=-1, keepdims=True) + RMS_EPS)
        xh = xv * r
        g = g_ref[...]
        dhv = dh_ref[...]
        a_sh[...] += _acc8(dhv)
        a_sc[...] += _acc8(dhv * (xh * g))
        dn = dhv * (1.0 + sc_ref[...])
        a_g[...] += _acc8(dn * xh)
        dxh = dn * g
        dx_ref[...] = dy_ref[...] + r * (dxh - xh * jnp.mean(dxh * xh, axis=-1, keepdims=True))

        @pl.when(c == nc - 1)
        def _():
            dsh_ref[...] = jnp.sum(a_sh[...], axis=0, keepdims=True)
            dsc_ref[...] = jnp.sum(a_sc[...], axis=0, keepdims=True)
            dg_ref[...] = jnp.sum(a_g[...], axis=0, keepdims=True)

    vec = jax.ShapeDtypeStruct((1, D_MODEL), F32)
    acc = pltpu.VMEM((SUBLANES, D_MODEL), F32)
    return pl.pallas_call(
        kern, name="in_bwd", grid=(nc,),
        in_specs=[_row_spec(tr), _row_spec(tr), _row_spec(tr), _vec_spec(), _vec_spec()],
        out_specs=(_row_spec(tr), _vec_spec(), _vec_spec(), _vec_spec()),
        out_shape=(jax.ShapeDtypeStruct((seq, D_MODEL), F32), vec, vec, vec),
        scratch_shapes=[acc, acc, acc],
        compiler_params=_cparams("arbitrary"))(dh, x, dy, g1, scale)


SLAB = 2 * SUBLANES


def _local_scan(a_re, a_im, br, bi, xr, xi, row0, ls, reverse, init=None, xb=None):
    if init is None:
        x_re = jnp.zeros((SUBLANES, STATE_W), F32)
        x_im = jnp.zeros((SUBLANES, STATE_W), F32)
    else:
        x_re, x_im = init
    for i in (range(ls - 1, -1, -1) if reverse else range(ls)):
        src = pl.ds(SUBLANES * i, SUBLANES)
        dst = pl.ds(row0 + SUBLANES * i, SUBLANES)
        n_re = a_re * x_re - a_im * x_im + br[src, :]
        n_im = a_re * x_im + a_im * x_re + bi[src, :]
        if xb is not None and i % 2 == 1:
            pair = pl.ds(SUBLANES * (i - 1), SLAB)
            xb[0][pair, :] = jnp.concatenate([x_re, n_re], axis=0).astype(BF16)
            xb[1][pair, :] = jnp.concatenate([x_im, n_im], axis=0).astype(BF16)
        x_re, x_im = n_re, n_im
        xr[dst, :] = x_re
        xi[dst, :] = x_im
    return x_re, x_im


def _two(v):
    return jnp.concatenate([v, v], axis=0)


def _unpermute_rhs(v, sel):
    hi = v.astype(BF16)
    r1 = v - hi.astype(F32)
    mid = r1.astype(BF16)
    lo = (r1 - mid.astype(F32)).astype(BF16)
    return _dot(hi, sel) + _dot(mid, sel) + _dot(lo, sel)


def _scan_specs(tc, nb, rows_of):
    return dict(
        us=pl.BlockSpec((tc, nb * LANES), lambda b, c: (rows_of(c), 2 * D_MODEL // (nb * LANES) + b)),
        tok=pl.BlockSpec((tc, nb * LANES), lambda b, c: (rows_of(c), b)),
        bblk=pl.BlockSpec((nb, LANES, STATE_W), lambda b, c: (b, 0, 0)),
        cblk=pl.BlockSpec((nb, STATE_W, LANES), lambda b, c: (b, 0, 0)),
        vec=pl.BlockSpec((1, nb * STATE_W), lambda b, c: (0, b)),
        tab=pl.BlockSpec((tc, nb * STATE_W), lambda b, c: (0, b)),
        car=pl.BlockSpec((SUBLANES, nb * STATE_W), lambda b, c: (rows_of(c), b)),
        dvec=pl.BlockSpec((1, nb * LANES), lambda b, c: (0, b)))


def _ssm_scan_fwd(proj, bb_re, bb_im, cm_re, cm_im, abar_re, abar_im, pw_re, pw_im, d_skip, tc):
    seq = proj.shape[0]
    nc = seq // tc
    ls = tc // SUBLANES
    nb = SCAN_BLOCKS

    def kern(us_ref, bbr_ref, bbi_ref, cmr_ref, cmi_ref, ar_ref, ai_ref, pwr_ref, pwi_ref, d_ref,
             ys_ref, ecr_ref, eci_ref, bur, bui, car_r, car_i, end_r, end_i, upb, xb_r, xb_i, *nat):
        c = pl.program_id(1)

        @pl.when(c == 0)
        def _():
            car_r[...] = jnp.zeros_like(car_r)
            car_i[...] = jnp.zeros_like(car_i)

        for j in range(nb):
            cols = pl.ds(j * LANES, LANES)
            scols = pl.ds(j * STATE_W, STATE_W)
            nat[j][...] = us_ref[:, cols].astype(F32)
            for i in range(ls):
                upb[j, pl.ds(SUBLANES * i, SUBLANES), :] = nat[j][pl.ds(i, SUBLANES, stride=ls), :]
            u = upb[j]
            up = u.astype(BF16)
            bur[j] = _dot(up, bbr_ref[j])
            bui[j] = _dot(up, bbi_ref[j])
            a_re = jnp.broadcast_to(ar_ref[:, scols], (SUBLANES, STATE_W))
            a_im = jnp.broadcast_to(ai_ref[:, scols], (SUBLANES, STATE_W))
            x_re, x_im = _local_scan(a_re, a_im, bur.at[j], bui.at[j], bur.at[j], bui.at[j], 0, ls, False)
            end_r[j] = x_re
            end_i[j] = x_im
            big_re = pwr_ref[tc - 1:tc, scols]
            big_im = pwi_ref[tc - 1:tc, scols]
            e_re = car_r[j, 0:1, :]
            e_im = car_i[j, 0:1, :]
            for s in range(SUBLANES):
                n_re = end_r[j, s:s + 1, :] + big_re * e_re - big_im * e_im
                n_im = end_i[j, s:s + 1, :] + big_re * e_im + big_im * e_re
                e_re, e_im = n_re, n_im
                if s < SUBLANES - 1:
                    car_r[j, s + 1:s + 2, :] = e_re
                    car_i[j, s + 1:s + 2, :] = e_im
            ec_re = car_r[j]
            ec_im = car_i[j]
            ecr_ref[:, scols] = ec_re
            eci_ref[:, scols] = ec_im
            e2_re, e2_im = _two(ec_re), _two(ec_im)
            for k in range(tc // SLAB):
                rows_k = pl.ds(SLAB * k, SLAB)
                p_re = pwr_ref[rows_k, scols]
                p_im = pwi_ref[rows_k, scols]
                xb_r[j, rows_k, :] = (bur[j, rows_k, :] + p_re * e2_re - p_im * e2_im).astype(BF16)
                xb_i[j, rows_k, :] = (bui[j, rows_k, :] + p_re * e2_im + p_im * e2_re).astype(BF16)
            upb[j] = _dot(xb_r[j], cmr_ref[j]) - _dot(xb_i[j], cmi_ref[j]) + d_ref[:, cols] * u
            for i in range(ls):
                nat[j][pl.ds(i, SUBLANES, stride=ls), :] = upb[j, pl.ds(SUBLANES * i, SUBLANES), :]
            ys_ref[:, cols] = nat[j][...]
            car_r[j, 0:1, :] = e_re
            car_i[j, 0:1, :] = e_im

    sp = _scan_specs(tc, nb, lambda c: c)
    carry_shape = jax.ShapeDtypeStruct((nc * SUBLANES, STATE_ALL), F32)
    small = pltpu.VMEM((nb, SUBLANES, STATE_W), F32)
    big = pltpu.VMEM((nb, tc, STATE_W), F32)
    return pl.pallas_call(
        kern, name="ssm_scan_fwd", grid=(LANE_BLOCKS // nb, nc),
        in_specs=[sp["us"], sp["bblk"], sp["bblk"], sp["cblk"], sp["cblk"], sp["vec"], sp["vec"], sp["tab"], sp["tab"],
                  sp["dvec"]],
        out_specs=(sp["tok"], sp["car"], sp["car"]),
        out_shape=(jax.ShapeDtypeStruct((seq, D_MODEL), F32), carry_shape, carry_shape),
        scratch_shapes=[big, big, small, small, small, small, pltpu.VMEM((nb, tc, LANES), F32),
                        pltpu.VMEM((nb, tc, STATE_W), BF16), pltpu.VMEM((nb, tc, STATE_W), BF16)]
        + [pltpu.VMEM((tc, LANES), F32)] * nb,
        compiler_params=_cparams("parallel", "arbitrary"),
    )(proj, bb_re, bb_im, cm_re, cm_im, abar_re, abar_im, pw_re, pw_im, d_skip)


def _ssm_scan_bwd(proj, dys, ec_re, ec_im, bb_re, bb_im, cm_re, cm_im, abar_re, abar_im,
                  pw_re, pw_im, pv_re, pv_im, d_skip, dproj, tc):
    seq = proj.shape[0]
    nc = seq // tc
    ls = tc // SUBLANES
    nb = SCAN_BLOCKS

    def kern(us_ref, dys_ref, ecr_ref, eci_ref, bbr_ref, bbi_ref, cmr_ref, cmi_ref, ar_ref, ai_ref,
             pwr_ref, pwi_ref, pvr_ref, pvi_ref, d_ref, _,
             dus_ref, dbbr_ref, dbbi_ref, dcmr_ref, dcmi_ref, dar_ref, dai_ref, dd_ref,
             bur, bui, xr, xi, gr, gi, fc_r, fc_i, a_bbr, a_bbi, a_cmr, a_cmi, a_ar, a_ai, a_dd, upb, dpb, hb_r, hb_i,
             *nat):
        c = pl.program_id(1)

        @pl.when(c == 0)
        def _():
            for ref in (fc_r, fc_i, a_bbr, a_bbi, a_cmr, a_cmi, a_ar, a_ai, a_dd):
                ref[...] = jnp.zeros_like(ref)

        for j in range(nb):
            cols = pl.ds(j * LANES, LANES)
            scols = pl.ds(j * STATE_W, STATE_W)
            nat_u, nat_d = nat[2 * j], nat[2 * j + 1]
            nat_u[...] = us_ref[:, cols].astype(F32)
            nat_d[...] = dys_ref[:, cols]
            for i in range(ls):
                rows_i = pl.ds(SUBLANES * i, SUBLANES)
                upb[j, rows_i, :] = nat_u[pl.ds(i, SUBLANES, stride=ls), :]
                dpb[j, rows_i, :] = nat_d[pl.ds(i, SUBLANES, stride=ls), :]
            u = upb[j]
            dysv = dpb[j]
            a_dd[j] += _acc8(dysv * u)
            up = u.astype(BF16)
            bur[j] = _dot(up, bbr_ref[j])
            bui[j] = _dot(up, bbi_ref[j])
            a_re = jnp.broadcast_to(ar_ref[:, scols], (SUBLANES, STATE_W))
            a_im = jnp.broadcast_to(ai_ref[:, scols], (SUBLANES, STATE_W))
            ec_r = ecr_ref[:, scols]
            ec_i = eci_ref[:, scols]
            xr[j, 0:SUBLANES, :] = ec_r
            xi[j, 0:SUBLANES, :] = ec_i
            _local_scan(a_re, a_im, bur.at[j], bui.at[j], xr.at[j], xi.at[j], SUBLANES, ls, False, init=(ec_r, ec_i),
                        xb=(hb_r.at[j], hb_i.at[j]))
            dysp = dysv.astype(BF16)
            a_cmr[j] += _dot_tn(dysp, hb_r[j])
            a_cmi[j] -= _dot_tn(dysp, hb_i[j])
            gr[j] = _dot_nt(dysp, cmr_ref[j])
            gi[j] = -_dot_nt(dysp, cmi_ref[j])
            _local_scan(a_re, -a_im, gr.at[j], gi.at[j], gr.at[j], gi.at[j], 0, ls, True)
            big_re = pwr_ref[tc - 1:tc, scols]
            big_im = -pwi_ref[tc - 1:tc, scols]
            f_re = fc_r[j, SUBLANES - 1:SUBLANES, :]
            f_im = fc_i[j, SUBLANES - 1:SUBLANES, :]
            for s in range(SUBLANES - 1, -1, -1):
                n_re = gr[j, s:s + 1, :] + big_re * f_re - big_im * f_im
                n_im = gi[j, s:s + 1, :] + big_re * f_im + big_im * f_re
                f_re, f_im = n_re, n_im
                if s > 0:
                    fc_r[j, s - 1:s, :] = f_re
                    fc_i[j, s - 1:s, :] = f_im
            f2_r, f2_i = _two(fc_r[j]), _two(fc_i[j])
            acc_r = jnp.zeros((SUBLANES, STATE_W), F32)
            acc_i = jnp.zeros((SUBLANES, STATE_W), F32)
            for k in range(tc // SLAB):
                rows_k = pl.ds(SLAB * k, SLAB)
                q_re = pvr_ref[rows_k, scols]
                q_im = pvi_ref[rows_k, scols]
                lam_re = gr[j, rows_k, :] + q_re * f2_r + q_im * f2_i
                lam_im = gi[j, rows_k, :] + q_re * f2_i - q_im * f2_r
                xp_re = xr[j, rows_k, :]
                xp_im = xi[j, rows_k, :]
                d_r = lam_re * xp_re + lam_im * xp_im
                d_i = lam_im * xp_re - lam_re * xp_im
                acc_r = acc_r + (d_r[0:SUBLANES] + d_r[SUBLANES:])
                acc_i = acc_i + (d_i[0:SUBLANES] + d_i[SUBLANES:])
                hb_r[j, rows_k, :] = lam_re.astype(BF16)
                hb_i[j, rows_k, :] = lam_im.astype(BF16)
            a_ar[j] += acc_r
            a_ai[j] += acc_i
            fc_r[j, SUBLANES - 1:SUBLANES, :] = f_re
            fc_i[j, SUBLANES - 1:SUBLANES, :] = f_im
            lb_re = hb_r[j]
            lb_im = hb_i[j]
            a_bbr[j] += _dot_tn(up, lb_re)
            a_bbi[j] += _dot_tn(up, lb_im)
            dpb[j] = _dot_nt(lb_re, bbr_ref[j]) + _dot_nt(lb_im, bbi_ref[j]) + dysv * d_ref[:, cols]
            for i in range(ls):
                nat_d[pl.ds(i, SUBLANES, stride=ls), :] = dpb[j, pl.ds(SUBLANES * i, SUBLANES), :]
            dus_ref[:, cols] = nat_d[...].astype(BF16)

        @pl.when(c == nc - 1)
        def _():
            row_g = lax.broadcasted_iota(jnp.int32, (LANES, STATE_W), 0) // SSM_H
            col_g = lax.broadcasted_iota(jnp.int32, (LANES, STATE_W), 1) // SSM_P
            fold = (lax.broadcasted_iota(jnp.int32, (STATE_W, SSM_P), 0) % SSM_P
                    == lax.broadcasted_iota(jnp.int32, (STATE_W, SSM_P), 1)).astype(BF16)
            for j in range(nb):
                rows_j = pl.ds(j * LANES, LANES)
                for acc, out in ((a_bbr, dbbr_ref), (a_bbi, dbbi_ref), (a_cmr, dcmr_ref), (a_cmi, dcmi_ref)):
                    out[rows_j, :] = _unpermute_rhs(jnp.where(row_g == col_g, acc[j], 0.0), fold)
                dar_ref[:, pl.ds(j * STATE_W, STATE_W)] = jnp.sum(a_ar[j], axis=0, keepdims=True)
                dai_ref[:, pl.ds(j * STATE_W, STATE_W)] = jnp.sum(a_ai[j], axis=0, keepdims=True)
                dd_ref[:, pl.ds(j * LANES, LANES)] = jnp.sum(a_dd[j], axis=0, keepdims=True)

    sp = _scan_specs(tc, nb, lambda c: nc - 1 - c)
    ghp = pl.BlockSpec((nb * LANES, SSM_P), lambda b, c: (b, 0))
    ghp_shape = jax.ShapeDtypeStruct((SSM_G * SSM_H, SSM_P), F32)
    small = pltpu.VMEM((nb, SUBLANES, STATE_W), F32)
    big = pltpu.VMEM((nb, tc, STATE_W), F32)
    bigp = pltpu.VMEM((nb, tc + SUBLANES, STATE_W), F32)
    blk = pltpu.VMEM((nb, LANES, STATE_W), F32)
    tok = pltpu.VMEM((nb, tc, LANES), F32)
    return pl.pallas_call(
        kern, name="ssm_scan_bwd", grid=(LANE_BLOCKS // nb, nc),
        in_specs=[sp["us"], sp["tok"], sp["car"], sp["car"], sp["bblk"], sp["bblk"], sp["cblk"], sp["cblk"],
                  sp["vec"], sp["vec"], sp["tab"], sp["tab"], sp["tab"], sp["tab"], sp["dvec"], _ANY],
        out_specs=(sp["us"], ghp, ghp, ghp, ghp, sp["vec"], sp["vec"], sp["dvec"]),
        out_shape=(jax.ShapeDtypeStruct(dproj.shape, BF16), ghp_shape, ghp_shape, ghp_shape, ghp_shape,
                   jax.ShapeDtypeStruct((1, STATE_ALL), F32), jax.ShapeDtypeStruct((1, STATE_ALL), F32),
                   jax.ShapeDtypeStruct((1, D_MODEL), F32)),
        scratch_shapes=[big, big, bigp, bigp, big, big, small, small, blk, blk, blk, blk,
                        small, small, pltpu.VMEM((nb, SUBLANES, LANES), F32), tok, tok,
                        pltpu.VMEM((nb, tc, STATE_W), BF16), pltpu.VMEM((nb, tc, STATE_W), BF16)]
        + [pltpu.VMEM((tc, LANES), F32)] * (2 * nb),
        input_output_aliases={15: 0},
        compiler_params=_cparams("parallel", "arbitrary"),
    )(proj, dys, ec_re, ec_im, bb_re, bb_im, cm_re, cm_im, abar_re, abar_im, pw_re, pw_im, pv_re, pv_im, d_skip, dproj)


def _eye5():
    return jnp.asarray(np.eye(GROUPS_PER_BLOCK, dtype=np.float32)[None, :, None, :, None])


def _embed_b(bb_t):
    t = bb_t.transpose(1, 0, 2).reshape(LANE_BLOCKS, GROUPS_PER_BLOCK, SSM_H, 1, SSM_P)
    return (t * _eye5()).reshape(LANE_BLOCKS, LANES, STATE_W)


def _embed_c(c_ghp):
    t = c_ghp.transpose(0, 2, 1).reshape(LANE_BLOCKS, GROUPS_PER_BLOCK, SSM_P, 1, SSM_H)
    return (t * _eye5()).reshape(LANE_BLOCKS, STATE_W, LANES)


def _local_step(x, c_row, tgt, w_ada_bf, b_ada, g1, g2, w_in_bf, pool_w_bf, pscale, a_re, a_im, log_dt,
                b_re_t, b_im_t, c_re, c_im, d_skip, glu_w_bf, glu_b, wbp_bf, wbs_bf, wout_bf,
                split_proj=None, ride_for_dw_in=None, ride_for_dh=None, mod_fn=None):
    seq = x.shape[0]
    tc = min(SCAN_CHUNK, seq)
    mod8, silu_c = _mod_kernel(c_row, w_ada_bf, b_ada) if mod_fn is None else mod_fn(c_row, b_ada)
    mod = mod8[0:1]
    shift, scale, gate = mod[:, 0:D_MODEL], mod[:, D_MODEL:2 * D_MODEL], mod[:, 2 * D_MODEL:]

    abar_re, abar_im, bb_re_t, bb_im_t = _ssm_params(a_re, a_im, log_dt, b_re_t, b_im_t)
    abar_re_f, abar_im_f = abar_re.reshape(1, STATE_ALL), abar_im.reshape(1, STATE_ALL)
    pw_re, pw_im, pv_re, pv_im = _pow_tables(abar_re_f, abar_im_f, tc)
    bbe_re, bbe_im = _embed_b(bb_re_t).astype(BF16), _embed_b(bb_im_t).astype(BF16)
    cme_re, cme_im = _embed_c(c_re).astype(BF16), _embed_c(c_im).astype(BF16)
    d_row = d_skip.reshape(1, D_MODEL)

    if split_proj:
        w_own, chip, w_in_ride, unpack_w_in, late_ride, unpack_late = split_proj
        h, h_t, proj, w_blocks = _in_norm_proj_own(x, g1, scale, shift, w_own, chip, w_in_ride)
        w_in_bf = unpack_w_in(w_blocks)
        proj, *gathered = _proj_rest(h, w_blocks, proj, chip, late_ride)
        pool_w_bf, glu_w_bf, wbp_bf, wbs_bf, wout_bf = unpack_late(*gathered)
    else:
        h, h_t = _in_norm(x, g1, scale, shift)
        proj = _mm([h], [w_in_bf], name="proj", out_dtype=BF16, bm=1024, bn=1536, bk=1024)
    ypool, ypool_t = _pool_fwd(proj, pool_w_bf, pscale)
    ys, ec_re, ec_im = _ssm_scan_fwd(proj, bbe_re, bbe_im, cme_re, cme_im, abar_re_f, abar_im_f,
                                      pw_re, pw_im, d_row, tc)
    yssm, yssm_t = _glu_fwd(ys, proj, glu_w_bf, glu_b)
    (dy, dypool, dyssm, dproj, merged_t, dob, dbp, dbs, loss, dgate, dg2) = _out_fwd_bwd(
        ypool, yssm, proj, x, tgt, gate, g2, wbp_bf, wbs_bf, wout_bf)

    d_wout = _mm([merged_t], [dob], name="dw_out", bm=1024, bn=1024, bk=2048)
    d_wbp = _mm([ypool_t], [dbp], name="dw_bp", bm=1024, bn=1024, bk=2048)
    d_wbs = _mm([yssm_t], [dbs], name="dw_bs", bm=1024, bn=1024, bk=2048)
    dys, dproj, dq, yg_t, d_glu_b = _glu_bwd(ys, proj, dyssm, glu_w_bf, glu_b, dproj)
    d_glu_w = _mm([yg_t], [dq], name="dw_glu", bm=1024, bn=1024, bk=2048)
    (dproj, dbbe_re, dbbe_im, dcme_re, dcme_im, d_abar_re, d_abar_im, d_dskip) = _ssm_scan_bwd(
        proj, dys, ec_re, ec_im, bbe_re, bbe_im, cme_re, cme_im, abar_re_f, abar_im_f,
        pw_re, pw_im, pv_re, pv_im, d_row, dproj, tc)
    dproj, d_pool_w, d_pscale = _pool_bwd(proj, dypool, pool_w_bf, pscale, dproj)
    dparts = [dproj]
    small_ready = dict(
        dg2=dg2, d_pscale=d_pscale, d_glu_b=d_glu_b, d_dskip=d_dskip, d_abar_re=d_abar_re, d_abar_im=d_abar_im,
        d_bb_re_t=dbbe_re.reshape(SSM_G, SSM_H, SSM_P).transpose(1, 0, 2),
        d_bb_im_t=dbbe_im.reshape(SSM_G, SSM_H, SSM_P).transpose(1, 0, 2),
        d_c_re=dcme_re.reshape(SSM_G, SSM_H, SSM_P), d_c_im=dcme_im.reshape(SSM_G, SSM_H, SSM_P))
    ride = ride_for_dw_in(small_ready) if ride_for_dw_in else None
    d_win = _mm([h_t], dparts, name="dw_in", bm=1024, bn=1024, bk=2048, ride=ride)
    rode_dw_in = ()
    if ride:
        d_win, rode_dw_in = d_win[0], tuple(d_win[1:])
    big_grads = dict(d_win=d_win, d_glu_w=d_glu_w, d_wbp=d_wbp, d_wbs=d_wbs, d_wout=d_wout, d_pool_w=d_pool_w)
    ride = ride_for_dh(big_grads) if ride_for_dh else None
    dh = _mm(dparts, [w_in_bf], tb=True, name="dh", bm=2048, bn=1024, bk=1024, ride=ride)
    rode = ()
    if ride:
        dh, rode = dh[0], tuple(dh[1:])
    grad_x, dshift, dscale, dg1 = _in_bwd(dh, x, dy, g1, scale)
    dmod = jnp.concatenate([dshift, dscale, dgate], axis=1)
    return dict(
        rode=rode, rode_dw_in=rode_dw_in, loss=loss[0, 0], grad_x=grad_x, dmod=dmod, silu_c=silu_c, dg1=dg1,
        **small_ready, **big_grads)


def _position():
    x, y, c = lax.axis_index("x"), lax.axis_index("y"), lax.axis_index("c")
    chips = [(1 - x, y), (x, 1 - y), (1 - x, 1 - y)]
    return x, y, c, chips


_ANY = pl.BlockSpec(memory_space=pl.ANY)
COMM_CHUNKS = 4
COMM_ROW_ALIGN = 16


def _row_chunks(rows, k):
    assert rows % (k * COMM_ROW_ALIGN) == 0, (rows, k)
    step = rows // k
    return [(q * step, step) for q in range(k)]


def _mod_sharded(c_row, w_own_bf, b_ada):
    n_own = w_own_bf.shape[1]
    assert N_CHIPS * n_own == b_ada.shape[1] and n_own % LANES == 0, (n_own, b_ada.shape)

    def kern(c_ref, w_ref, b_ref, m_ref, s_ref, rows, prods, got, send_sems, recv_sems):
        x, y, c, chips = _position()
        me = 2 * x + y

        def copy(k, src, dst, to):
            return pltpu.make_async_remote_copy(src_ref=src, dst_ref=dst, send_sem=send_sems.at[k],
                                                recv_sem=recv_sems.at[k], device_id=(*to, c), device_id_type=MESH_ID)

        number = [2 * cx + cy for cx, cy in chips]
        sc = _silu(c_ref[...])
        s_ref[...] = sc
        rows[me] = jnp.broadcast_to(sc, (SUBLANES, D_MODEL))
        out_rows = [copy(j, rows.at[me], rows.at[me], chip) for j, chip in enumerate(chips)]
        for cp in out_rows:
            cp.start()
        for j, chip in enumerate(chips):
            copy(j, rows.at[number[j]], rows.at[number[j]], chip).wait_recv()
        lhs = rows[...].reshape(N_CHIPS * SUBLANES, D_MODEL).astype(BF16)
        prods[...] = _dot(lhs, w_ref[...]).reshape(N_CHIPS, SUBLANES, n_own)
        got[me] = prods[me]
        out_prods = [copy(3 + j, prods.at[number[j]], got.at[me], chip) for j, chip in enumerate(chips)]
        for cp in out_prods:
            cp.start()
        for j, chip in enumerate(chips):
            copy(3 + j, got.at[number[j]], got.at[number[j]], chip).wait_recv()
        for cp in out_rows + out_prods:
            cp.wait_send()
        for k in range(N_CHIPS):
            cols = slice(k * n_own, (k + 1) * n_own)
            m_ref[:, cols] = got[k] + b_ref[:, cols]

    return pl.pallas_call(
        kern, name="ada_mod_sharded",
        out_shape=(jax.ShapeDtypeStruct((SUBLANES, 3 * D_MODEL), F32), jax.ShapeDtypeStruct((1, D_MODEL), F32)),
        scratch_shapes=[pltpu.VMEM((N_CHIPS, SUBLANES, D_MODEL), F32), pltpu.VMEM((N_CHIPS, SUBLANES, n_own), F32),
                        pltpu.VMEM((N_CHIPS, SUBLANES, n_own), F32), pltpu.SemaphoreType.DMA((6,)),
                        pltpu.SemaphoreType.DMA((6,))],
        compiler_params=_cparams())(c_row, w_own_bf, b_ada)


def _ag_weights_ride(packed, n_chunks=COMM_CHUNKS):
    rows, width = packed.shape
    half = rows // 2
    chunks = _row_chunks(half, n_chunks)
    nq = len(chunks)

    def parts(p_ref, out_ref, send_sems, recv_sems):
        x, y, c, chips = _position()
        sibling = (x, y, 1 - c)

        def copy(k, chip, h, q, to, src=None):
            start, size = chunks[q]
            rows_q = pl.ds(h * half + start, size)
            dst = out_ref.at[2 * chip[0] + chip[1], rows_q, :]
            return pltpu.make_async_remote_copy(
                src_ref=dst if src is None else src.at[rows_q, :], dst_ref=dst, send_sem=send_sems.at[k * nq + q],
                recv_sem=recv_sems.at[k * nq + q], device_id=to, device_id_type=MESH_ID)

        mine = [copy(6 + h, (x, y), h, q, sibling, src=p_ref) for h in range(2) for q in range(nq)]
        first = [copy(j, (x, y), c, q, (*chip, c), src=p_ref) for q in range(nq) for j, chip in enumerate(chips)]
        return (x, y, c), chips, sibling, copy, mine, first

    def start(ins, outs, sems):
        _, _, _, _, mine, first = parts(ins[0], outs[0], sems[0], sems[1])
        for cp in first + mine:
            cp.start()

    def wait(ins, outs, sems):
        (x, y, c), chips, sibling, copy, mine, first = parts(ins[0], outs[0], sems[0], sems[1])
        passed = []
        for q in range(nq):
            for j, chip in enumerate(chips):
                copy(j, chip, c, q, (x, y, c)).wait_recv()
                fwd = copy(3 + j, chip, c, q, sibling)
                fwd.start()
                passed.append(fwd)
        for q in range(nq):
            for j, chip in enumerate(chips):
                copy(3 + j, chip, 1 - c, q, (x, y, c)).wait_recv()
        for cp in mine:
            cp.wait_recv()
        for cp in first + passed + mine:
            cp.wait_send()

    return _Ride([packed], [jax.ShapeDtypeStruct((N_CHIPS, rows, width), packed.dtype)],
                 [pltpu.SemaphoreType.DMA((8 * nq,)), pltpu.SemaphoreType.DMA((8 * nq,))], start, wait)


def _join_rides(rides):
    def split(seq, counts):
        out, at = [], 0
        for n in counts:
            out.append(seq[at:at + n])
            at += n
        return out

    n_in = [len(r.inputs) for r in rides]
    n_out = [len(r.out_shapes) for r in rides]
    n_sem = [len(r.scratch) for r in rides]

    def start(ins, outs, sems):
        for r, i, o, s in zip(rides, split(ins, n_in), split(outs, n_out), split(sems, n_sem)):
            r.start(i, o, s)

    def wait(ins, outs, sems):
        for r, i, o, s in zip(rides, split(ins, n_in), split(outs, n_out), split(sems, n_sem)):
            r.wait(i, o, s)

    return _Ride([a for r in rides for a in r.inputs], [a for r in rides for a in r.out_shapes],
                 [a for r in rides for a in r.scratch], start, wait)


def _run_ride(ride, name):
    n_in, n_out = len(ride.inputs), len(ride.out_shapes)

    def body(*refs):
        ins, outs, sems = refs[:n_in], refs[n_in:n_in + n_out], refs[n_in + n_out:]
        ride.start(ins, outs, sems)
        ride.wait(ins, outs, sems)

    return pl.pallas_call(
        body, name=name, in_specs=[_ANY] * n_in, out_specs=(_ANY,) * n_out, out_shape=tuple(ride.out_shapes),
        scratch_shapes=list(ride.scratch))(*ride.inputs)


def _small_allgather_ride(buf):
    rows, width = buf.shape
    chunks = _row_chunks(rows, COMM_CHUNKS)
    nq = len(chunks)

    def parts(b_ref, all_ref, send_sems, recv_sems, local_sem):
        x, y, c, chips = _position()
        me, sibling = (x, y, c), (x, y, 1 - c)

        def copy(k, block, q, to, src=None):
            rows_q = pl.ds(chunks[q][0], chunks[q][1])
            dst = all_ref.at[4 * block[0] + 2 * block[1] + block[2], rows_q, :]
            return pltpu.make_async_remote_copy(
                src_ref=dst if src is None else src.at[rows_q, :], dst_ref=dst, send_sem=send_sems.at[k * nq + q],
                recv_sem=recv_sems.at[k * nq + q], device_id=to, device_id_type=MESH_ID)

        mine = pltpu.make_async_copy(b_ref, all_ref.at[4 * x + 2 * y + c], local_sem)
        first = []
        for q in range(nq):
            first += [copy(1 + j, me, q, (*chip, c), src=b_ref) for j, chip in enumerate(chips)]
            first.append(copy(0, me, q, sibling, src=b_ref))
        return me, sibling, c, chips, copy, mine, first

    def start(ins, outs, sems):
        _, _, _, _, _, mine, first = parts(ins[0], outs[0], *sems)
        mine.start()
        for cp in first:
            cp.start()

    def wait(ins, outs, sems):
        me, sibling, c, chips, copy, mine, first = parts(ins[0], outs[0], *sems)
        passed = []
        for q in range(nq):
            for j, chip in enumerate(chips):
                copy(1 + j, (*chip, c), q, me).wait_recv()
                fwd = copy(4 + j, (*chip, c), q, sibling)
                fwd.start()
                passed.append(fwd)
        for q in range(nq):
            copy(0, sibling, q, me).wait_recv()
            for j, chip in enumerate(chips):
                copy(4 + j, (*chip, 1 - c), q, me).wait_recv()
        for cp in first + passed:
            cp.wait_send()
        mine.wait()

    return _Ride([buf], [jax.ShapeDtypeStruct((N_DEV, rows, width), F32)],
                 [pltpu.SemaphoreType.DMA((7 * nq,)), pltpu.SemaphoreType.DMA((7 * nq,)), pltpu.SemaphoreType.DMA],
                 start, wait)


def _sum_devices(blocks):
    n, rows, width = blocks.shape
    rb = rows // 2 if (rows // 2) % SUBLANES == 0 else rows

    def kern(b_ref, o_ref):
        total = b_ref[0]
        for d in range(1, n):
            total = total + b_ref[d]
        o_ref[...] = total

    return pl.pallas_call(
        kern, name="small_sum", grid=(rows // rb,), in_specs=[pl.BlockSpec((n, rb, width), lambda i: (0, i, 0))],
        out_specs=pl.BlockSpec((rb, width), lambda i: (i, 0)), out_shape=jax.ShapeDtypeStruct((rows, width), F32),
        compiler_params=_cparams("parallel"))(blocks)


def _small_allgather_sum(buf, head_rows, n_chunks=COMM_CHUNKS):
    rows, width = buf.shape
    chunks = _row_chunks(rows, n_chunks)
    nq = len(chunks)

    def body(b_ref, head_ref, sum_ref, all_ref, send_sems, recv_sems, local_sem):
        x, y, c, chips = _position()
        me, sibling = (x, y, c), (x, y, 1 - c)

        def slot(px, py, pc):
            return all_ref.at[4 * px + 2 * py + pc]

        def copy(k, block, q, to, src=None):
            rows_q = pl.ds(chunks[q][0], chunks[q][1])
            dst = slot(*block).at[rows_q, :]
            return pltpu.make_async_remote_copy(
                src_ref=dst if src is None else src.at[rows_q, :], dst_ref=dst, send_sem=send_sems.at[k * nq + q],
                recv_sem=recv_sems.at[k * nq + q], device_id=to, device_id_type=MESH_ID)

        mine = pltpu.make_async_copy(b_ref, slot(*me), local_sem)
        mine.start()
        first = []
        for q in range(nq):
            first += [copy(1 + j, me, q, (*chip, c), src=b_ref) for j, chip in enumerate(chips)]
            first.append(copy(0, me, q, sibling, src=b_ref))
        for cp in first:
            cp.start()
        passed = []
        for q in range(nq):
            for j, chip in enumerate(chips):
                copy(1 + j, (*chip, c), q, me).wait_recv()
                fwd = copy(4 + j, (*chip, c), q, sibling)
                fwd.start()
                passed.append(fwd)
        for q in range(nq):
            copy(0, sibling, q, me).wait_recv()
            for j, chip in enumerate(chips):
                copy(4 + j, (*chip, 1 - c), q, me).wait_recv()
        for cp in first + passed:
            cp.wait_send()
        mine.wait()
        total = all_ref[0]
        for d in range(1, N_DEV):
            total = total + all_ref[d]
        sum_ref[...] = total
        head_ref[...] = all_ref[:, 0:head_rows, :]

    vm = pl.BlockSpec(memory_space=pltpu.VMEM)
    return pl.pallas_call(
        body, name="small_allgather_sum", in_specs=[vm], out_specs=(vm, vm),
        out_shape=(jax.ShapeDtypeStruct((N_DEV, head_rows, width), F32), jax.ShapeDtypeStruct((rows, width), F32)),
        scratch_shapes=[pltpu.VMEM((N_DEV, rows, width), F32), pltpu.SemaphoreType.DMA((7 * nq,)),
                        pltpu.SemaphoreType.DMA((7 * nq,)), pltpu.SemaphoreType.DMA],
        compiler_params=_cparams(),
    )(buf)


def _rs_pair(g):
    n, rows, width = g.shape
    half = rows // 2
    chunks = _row_chunks(half, COMM_CHUNKS)
    nq = len(chunks)

    def body(g_ref, got_ref, send_sems, recv_sems):
        x, y, c, _ = _position()
        swaps = []
        for k in range(n):
            for q, (start, size) in enumerate(chunks):
                swaps.append(pltpu.make_async_remote_copy(
                    src_ref=g_ref.at[k, pl.ds((1 - c) * half + start, size), :], dst_ref=got_ref.at[k, pl.ds(start, size), :],
                    send_sem=send_sems.at[k * nq + q], recv_sem=recv_sems.at[k * nq + q],
                    device_id=(x, y, 1 - c), device_id_type=MESH_ID))
        for cp in swaps:
            cp.start()
        for cp in swaps:
            cp.wait()

    return pl.pallas_call(
        body, name="rs_pair", in_specs=[_ANY], out_specs=_ANY, out_shape=jax.ShapeDtypeStruct((n, half, width), g.dtype),
        scratch_shapes=[pltpu.SemaphoreType.DMA((n * nq,)), pltpu.SemaphoreType.DMA((n * nq,))],
    )(g)


def _rs_chips_ride(part_bf):
    n, rows, width = part_bf.shape
    chunks = _row_chunks(rows, COMM_CHUNKS)
    nq = len(chunks)

    def sends(pb_ref, got_ref, send_sems, recv_sems):
        x, y, c, chips = _position()
        out = []
        for q, (start, size) in enumerate(chunks):
            for j, chip in enumerate(chips):
                out.append(pltpu.make_async_remote_copy(
                    src_ref=pb_ref.at[2 * chip[0] + chip[1], pl.ds(start, size), :], dst_ref=got_ref.at[j, pl.ds(start, size), :],
                    send_sem=send_sems.at[j * nq + q], recv_sem=recv_sems.at[j * nq + q],
                    device_id=(*chip, c), device_id_type=MESH_ID))
        return out

    def start(ins, outs, sems):
        for cp in sends(ins[0], outs[0], sems[0], sems[1]):
            cp.start()

    def wait(ins, outs, sems):
        for cp in sends(ins[0], outs[0], sems[0], sems[1]):
            cp.wait()

    return _Ride([part_bf], [jax.ShapeDtypeStruct((N_CHIPS - 1, rows, width), BF16)],
                 [pltpu.SemaphoreType.DMA((3 * nq,)), pltpu.SemaphoreType.DMA((3 * nq,))], start, wait)


def _rs_join(shard):
    rows, width = shard.shape
    half = rows // 2
    chunks = _row_chunks(half, COMM_CHUNKS)
    nq = len(chunks)

    def body(in_ref, out_ref, send_sems, recv_sems):
        x, y, c, _ = _position()
        def swap(q, h):
            rows_q = pl.ds(h * half + chunks[q][0], chunks[q][1])
            return pltpu.make_async_remote_copy(
                src_ref=in_ref.at[rows_q, :], dst_ref=out_ref.at[rows_q, :], send_sem=send_sems.at[q],
                recv_sem=recv_sems.at[q], device_id=(x, y, 1 - c), device_id_type=MESH_ID)

        for q in range(nq):
            swap(q, c).start()
        for q in range(nq):
            swap(q, 1 - c).wait_recv()
        for q in range(nq):
            swap(q, c).wait_send()

    return pl.pallas_call(
        body, name="rs_join", in_specs=[_ANY], out_specs=_ANY, input_output_aliases={0: 0},
        out_shape=jax.ShapeDtypeStruct(shard.shape, shard.dtype),
        scratch_shapes=[pltpu.SemaphoreType.DMA((nq,)), pltpu.SemaphoreType.DMA((nq,))],
    )(shard)


def _pair_add(g, got, core):
    n, half, width = got.shape
    nb = 2
    rb = half // nb

    def kern(c_ref, a_ref, b_ref, f_ref, h_ref):
        s = a_ref[...] + b_ref[...]
        f_ref[...] = s
        h_ref[...] = s.astype(BF16)

    spec = pl.BlockSpec((1, rb, width), lambda k, i, c_ref: (k, i, 0))
    return pl.pallas_call(
        kern, name="rs_pair_add",
        grid_spec=pltpu.PrefetchScalarGridSpec(
            num_scalar_prefetch=1, grid=(n, nb),
            in_specs=[pl.BlockSpec((1, rb, width), lambda k, i, c_ref: (k, c_ref[0] * nb + i, 0)), spec],
            out_specs=(spec, spec)),
        out_shape=(jax.ShapeDtypeStruct(got.shape, F32), jax.ShapeDtypeStruct(got.shape, BF16)),
        compiler_params=_cparams("parallel", "parallel"))(core, g, got)


def _chip_add(part_f32, got, where):
    _, rows, width = part_f32.shape
    nb = 2
    rb = rows // nb

    def kern(w_ref, a_ref, b_ref, o_ref):
        o_ref[...] = ((a_ref[0] + b_ref[0].astype(F32)) + b_ref[1].astype(F32)) + b_ref[2].astype(F32)

    return pl.pallas_call(
        kern, name="rs_chip_add",
        grid_spec=pltpu.PrefetchScalarGridSpec(
            num_scalar_prefetch=1, grid=(nb,),
            in_specs=[pl.BlockSpec((1, rb, width), lambda i, w_ref: (w_ref[0], i, 0)),
                      pl.BlockSpec((N_CHIPS - 1, rb, width), lambda i, w_ref: (0, i, 0))],
            out_specs=pl.BlockSpec((rb, width), lambda i, w_ref: (w_ref[1] * nb + i, 0))),
        out_shape=jax.ShapeDtypeStruct((2 * rows, width), F32),
        compiler_params=_cparams("parallel"))(where, part_f32, got)


def _adamw(w, g, m, v, name):
    rows, width = w.shape
    rb = rows
    for cand in (512, 256, 128, 64, 32, 16, 8):
        if rows % cand == 0 and cand * width * 4 <= ADAM_BLOCK_BYTES:
            rb = cand
            break
    spec = pl.BlockSpec((rb, width), lambda i: (i, 0))

    def kern(w_ref, g_ref, m_ref, v_ref, d_ref, nm_ref, nv_ref):
        d_ref[...], nm_ref[...], nv_ref[...] = _adamw_update(w_ref[...], g_ref[...], m_ref[...], v_ref[...])

    shp = jax.ShapeDtypeStruct(w.shape, F32)
    return pl.pallas_call(
        kern, name=name, grid=(rows // rb,), in_specs=[spec] * 4, out_specs=(spec, spec, spec),
        out_shape=(shp, shp, shp), compiler_params=_cparams("parallel"))(w, g, m, v)


def _adamw_update(w, g, m, v):
    nm = ADAM_B1 * m + (1.0 - ADAM_B1) * g
    nv = ADAM_B2 * v + (1.0 - ADAM_B2) * (g * g)
    m_hat = nm / (1.0 - ADAM_B1 ** ADAM_STEP)
    v_hat = nv / (1.0 - ADAM_B2 ** ADAM_STEP)
    return -ADAM_LR * (m_hat / (jnp.sqrt(v_hat) + ADAM_EPS) + ADAM_WD * w), nm, nv


def _adamw_small(params):
    n = len(params)

    def kern(*refs):
        ins, outs = refs[:4 * n], refs[4 * n:]
        for p in range(n):
            w_ref, g_ref, m_ref, v_ref = ins[4 * p:4 * p + 4]
            d, nm, nv = _adamw_update(w_ref[...], g_ref[...], m_ref[...], v_ref[...])
            outs[3 * p][...] = d
            outs[3 * p + 1][...] = nm
            outs[3 * p + 2][...] = nv

    flat = [a for group in params for a in group]
    shapes = [jax.ShapeDtypeStruct(group[0].shape, F32) for group in params for _ in range(3)]
    res = pl.pallas_call(kern, name="adamw_small", out_shape=tuple(shapes), compiler_params=_cparams())(*flat)
    return [tuple(res[3 * p:3 * p + 3]) for p in range(n)]


def _wada_grad(silu_t, dmod_cols):
    n = dmod_cols.shape[1]

    def kern(s_ref, d_ref, o_ref):
        acc = s_ref[:, 0:1] * d_ref[0:1, :]
        for b in range(1, N_DEV):
            acc = acc + s_ref[:, b:b + 1] * d_ref[b:b + 1, :]
        o_ref[...] = acc

    return pl.pallas_call(kern, name="wada_grad", out_shape=jax.ShapeDtypeStruct((D_MODEL, n), F32),
                          compiler_params=_cparams())(silu_t, dmod_cols)


def _rows(a, multiple):
    flat = a.reshape(-1)
    pad = (-flat.shape[0]) % (D_MODEL * multiple)
    if pad:
        flat = jnp.concatenate([flat, jnp.zeros((pad,), flat.dtype)])
    return flat.reshape(-1, D_MODEL)


def _part_rows(shape, multiple):
    return -(-int(np.prod(shape)) // (D_MODEL * multiple)) * multiple


def _pack_rows(parts, multiple, total_multiple=1):
    blocks = [_rows(p, multiple) for p in parts]
    pad = (-sum(b.shape[0] for b in blocks)) % total_multiple
    if pad:
        blocks.append(jnp.zeros((pad, D_MODEL), blocks[0].dtype))
    return jnp.concatenate(blocks, axis=0)


def _unpack_rows(buf, shapes, multiple):
    out, r = [], 0
    for shp in shapes:
        n = int(np.prod(shp))
        nr = _part_rows(shp, multiple)
        out.append(buf[r:r + nr].reshape(-1)[:n].reshape(shp))
        r += nr
    return out


def kernel(x, c, w_ada, b_ada, norm_pre, norm_post, w_in, pool_w, pool_scale, ssm_a_re, ssm_a_im, ssm_log_dt, ssm_b_re, ssm_b_im, ssm_c_re, ssm_c_im, ssm_d, glu_w, glu_b, w_branch_pool, w_branch_ssm, w_out, loss_target, m_w_ada, m_b_ada, m_norm_pre, m_norm_post, m_w_in, m_pool_w, m_pool_scale, m_ssm_a_re, m_ssm_a_im, m_ssm_log_dt, m_ssm_b_re, m_ssm_b_im, m_ssm_c_re, m_ssm_c_im, m_ssm_d, m_glu_w, m_glu_b, m_w_branch_pool, m_w_branch_ssm, m_w_out, v_w_ada, v_b_ada, v_norm_pre, v_norm_post, v_w_in, v_pool_w, v_pool_scale, v_ssm_a_re, v_ssm_a_im, v_ssm_log_dt, v_ssm_b_re, v_ssm_b_im, v_ssm_c_re, v_ssm_c_im, v_ssm_d, v_glu_w, v_glu_b, v_w_branch_pool, v_w_branch_ssm, v_w_out):
    n_ada = w_ada.shape[2]
    n_in = w_in.shape[2]
    n_row = glu_w.shape[1]
    n_pool = pool_w.shape[2]
    n_groups = pool_w.shape[1]

    w_ada_own = w_ada[0].astype(BF16)
    w_in_own = w_in[0].astype(BF16)
    w_in_ride = _ag_weights_ride(w_in_own)

    def unpack_w_in(g_in):
        return g_in.transpose(1, 0, 2).reshape(D_MODEL, N_CHIPS * n_in)
    pool_rows = n_groups * n_pool * POOL_GW // D_MODEL
    late_shards = [pool_w[0].reshape(n_groups * n_pool, POOL_GW), glu_w[0], w_branch_pool[0], w_branch_ssm[0], w_out[0]]
    late_ride = _join_rides([_ag_weights_ride(s.astype(BF16), n_chunks=2) for s in late_shards])

    def unpack_late(pool, *squares):
        pool = pool.reshape(N_CHIPS, n_groups, n_pool, POOL_GW).transpose(1, 0, 2, 3)
        return (pool.reshape(n_groups, POOL_GW, POOL_GW), *[s.reshape(D_MODEL, D_MODEL) for s in squares])

    chip = 2 * lax.axis_index("x") + lax.axis_index("y")
    core = lax.axis_index("c").astype(jnp.int32)
    kept = {}

    half_d = D_MODEL // 2
    assert n_in == D_MODEL + half_d, n_in

    def by_cols(a, n):
        blk = a.reshape(D_MODEL, N_CHIPS, n).transpose(1, 0, 2)
        rest = blk[:, :, D_MODEL:]
        return jnp.concatenate(
            [blk[:, :, :D_MODEL], jnp.concatenate([rest[:, :half_d], rest[:, half_d:]], axis=2)], axis=1)

    def from_cols(packed):
        rest = packed[D_MODEL:]
        return jnp.concatenate(
            [packed[:D_MODEL], jnp.concatenate([rest[:, :half_d], rest[:, half_d:]], axis=0)], axis=1)

    def by_rows(a):
        return a.reshape(N_CHIPS, n_row, D_MODEL)

    def exchange_big(g):
        pool_by_chip = g["d_pool_w"].reshape(n_groups, N_CHIPS, n_pool, POOL_GW).transpose(1, 0, 2, 3)
        blocks = [by_cols(g["d_win"], n_in), by_rows(g["d_glu_w"]), by_rows(g["d_wbp"]), by_rows(g["d_wbs"]),
                  by_rows(g["d_wout"]), pool_by_chip.reshape(N_CHIPS, pool_rows, D_MODEL)]
        pad = (-sum(b.shape[1] for b in blocks)) % (2 * COMM_CHUNKS * COMM_ROW_ALIGN)
        if pad:
            blocks.append(jnp.zeros((N_CHIPS, pad, D_MODEL), F32))
        g_packed = jnp.concatenate(blocks, axis=1)
        kept["part_f32"], part_bf = _pair_add(g_packed, _rs_pair(g_packed), core.reshape(1))
        return _rs_chips_ride(part_bf)

    a_re, a_im, log_dt = ssm_a_re[0], ssm_a_im[0], ssm_log_dt[0].reshape(SSM_G, 1)
    b_re_t, b_im_t = ssm_b_re[0].transpose(2, 0, 1), ssm_b_im[0].transpose(2, 0, 1)
    early_names = ["dg2", "d_pscale", "d_glu_b", "d_dskip", "d_abar_re", "d_abar_im", "d_bb_re_t", "d_bb_im_t",
                   "d_c_re", "d_c_im"]

    def exchange_small(s):
        parts = [s[k] for k in early_names]
        kept["early_shapes"] = [p.shape for p in parts]
        return _small_allgather_ride(_pack_rows(parts, SUBLANES, COMM_CHUNKS * COMM_ROW_ALIGN))

    res = _local_step(x[0], c, loss_target[0], None, b_ada, norm_pre, norm_post, None, None, pool_scale,
                      a_re, a_im, log_dt, b_re_t, b_im_t, ssm_c_re[0], ssm_c_im[0], ssm_d[0], None, glu_b[0:1],
                      None, None, None,
                      split_proj=(w_in_own, chip.astype(jnp.int32).reshape(1), w_in_ride, unpack_w_in, late_ride, unpack_late),
                      ride_for_dw_in=exchange_small, ride_for_dh=exchange_big,
                      mod_fn=lambda c_row, bias: _mod_sharded(c_row, w_ada_own, bias))

    (all_early,) = res["rode_dw_in"]
    (g_norm_post, g_pscale, g_glu_b, g_dskip, s_abar_re, s_abar_im, s_bb_re, s_bb_im, g_c_re, g_c_im) = _unpack_rows(
        _sum_devices(all_early), kept["early_shapes"], SUBLANES)
    g_a_re, g_a_im, g_log_dt, g_b_re_t, g_b_im_t = _ssm_params_bwd(
        a_re, a_im, log_dt, b_re_t, b_im_t, s_abar_re.reshape(SSM_G, SSM_P), s_abar_im.reshape(SSM_G, SSM_P),
        s_bb_re, s_bb_im)
    late_parts = [res["dmod"], res["silu_c"], res["dg1"], res["loss"].reshape(1, 1)]
    late_shapes = [p.shape for p in late_parts]
    head_rows = _part_rows(late_shapes[0], SUBLANES) + _part_rows(late_shapes[1], SUBLANES)
    all_late, sum_late = _small_allgather_sum(_pack_rows(late_parts, SUBLANES, COMM_ROW_ALIGN), head_rows, n_chunks=1)
    g_b_ada, _, g_norm_pre, loss = _unpack_rows(sum_late, late_shapes, SUBLANES)
    loss = loss[0, 0]
    dmod_all = all_late[:, 0:3].reshape(N_DEV, 3 * D_MODEL)
    dmod_cols = lax.dynamic_slice_in_dim(dmod_all, chip * n_ada, n_ada, axis=1)
    silu_t = all_late[:, _part_rows(late_shapes[0], SUBLANES)].transpose(1, 0)
    g_w_ada = _wada_grad(silu_t, dmod_cols)

    (got_chips,) = res["rode"]
    shard = _rs_join(_chip_add(kept["part_f32"], got_chips, jnp.stack([chip.astype(jnp.int32), core])))
    r = 0
    g_w_in = from_cols(shard[r:r + n_in])
    r += n_in
    g_squares = []
    for _ in range(4):
        g_squares.append(shard[r:r + n_row])
        r += n_row
    g_glu_w, g_wbp, g_wbs, g_wout = g_squares
    g_pool_w = shard[r:r + pool_rows].reshape(n_groups * n_pool, POOL_GW)

    big = [("w_ada", w_ada[0], g_w_ada, m_w_ada[0], v_w_ada[0]),
           ("w_in", w_in[0], g_w_in, m_w_in[0], v_w_in[0]),
           ("pool_w", pool_w[0].reshape(n_groups * n_pool, POOL_GW), g_pool_w,
            m_pool_w[0].reshape(n_groups * n_pool, POOL_GW), v_pool_w[0].reshape(n_groups * n_pool, POOL_GW)),
           ("glu_w", glu_w[0], g_glu_w, m_glu_w[0], v_glu_w[0]),
           ("w_branch_pool", w_branch_pool[0], g_wbp, m_w_branch_pool[0], v_w_branch_pool[0]),
           ("w_branch_ssm", w_branch_ssm[0], g_wbs, m_w_branch_ssm[0], v_w_branch_ssm[0]),
           ("w_out", w_out[0], g_wout, m_w_out[0], v_w_out[0])]
    out = {}
    for name, w_, g_, m_, v_ in big:
        d_, nm_, nv_ = _adamw(w_, g_, m_, v_, "adamw_" + name)
        out[name] = (g_, d_, nm_, nv_)

    g_b_re = g_b_re_t.transpose(1, 2, 0)
    g_b_im = g_b_im_t.transpose(1, 2, 0)
    small = [("b_ada", b_ada, g_b_ada, m_b_ada, v_b_ada),
             ("norm_pre", norm_pre, g_norm_pre, m_norm_pre, v_norm_pre),
             ("norm_post", norm_post, g_norm_post, m_norm_post, v_norm_post),
             ("pool_scale", pool_scale, g_pscale, m_pool_scale, v_pool_scale),
             ("ssm_a_re", ssm_a_re, g_a_re, m_ssm_a_re, v_ssm_a_re),
             ("ssm_a_im", ssm_a_im, g_a_im, m_ssm_a_im, v_ssm_a_im),
             ("ssm_log_dt", ssm_log_dt, g_log_dt, m_ssm_log_dt, v_ssm_log_dt),
             ("ssm_b_re", ssm_b_re, g_b_re, m_ssm_b_re, v_ssm_b_re),
             ("ssm_b_im", ssm_b_im, g_b_im, m_ssm_b_im, v_ssm_b_im),
             ("ssm_c_re", ssm_c_re, g_c_re, m_ssm_c_re, v_ssm_c_re),
             ("ssm_c_im", ssm_c_im, g_c_im, m_ssm_c_im, v_ssm_c_im),
             ("ssm_d", ssm_d, g_dskip, m_ssm_d, v_ssm_d),
             ("glu_b", glu_b, g_glu_b, m_glu_b, v_glu_b)]
    small = [(name, w_, g_.reshape(w_.shape), m_, v_) for name, w_, g_, m_, v_ in small]
    updates = _adamw_small([t[1:] for t in small])
    for (name, _, g_, _, _), (d_, nm_, nv_) in zip(small, updates):
        out[name] = (g_, d_, nm_, nv_)

    order = ["w_ada", "b_ada", "norm_pre", "norm_post", "w_in", "pool_w", "pool_scale", "ssm_a_re", "ssm_a_im",
             "ssm_log_dt", "ssm_b_re", "ssm_b_im", "ssm_c_re", "ssm_c_im", "ssm_d", "glu_w", "glu_b", "w_branch_pool",
             "w_branch_ssm", "w_out"]
    ref_shape = dict(w_ada=w_ada.shape, w_in=w_in.shape, pool_w=pool_w.shape, glu_w=glu_w.shape,
                     w_branch_pool=w_branch_pool.shape, w_branch_ssm=w_branch_ssm.shape, w_out=w_out.shape)
    for name, w_, _, _, _ in small:
        ref_shape[name] = w_.shape
    results = [loss, res["grad_x"][None]]
    for k in range(4):
        results += [out[name][k].reshape(ref_shape[name]) for name in order]
    return tuple(results)
```

```python
import functools
import math

import numpy as np
import jax
import jax.numpy as jnp
from jax import lax
from jax.experimental import pallas as pl
from jax.experimental.pallas import tpu as pltpu

F32 = jnp.float32
BF16 = jnp.bfloat16
MESH_ID = pl.DeviceIdType.MESH

D_MODEL = 1024
LANES = 128
SUBLANES = 8
SSM_G, SSM_P, SSM_H = 64, 64, 16
LANE_BLOCKS = D_MODEL // LANES
GROUPS_PER_BLOCK = LANES // SSM_H
STATE_W = GROUPS_PER_BLOCK * SSM_P
STATE_ALL = SSM_G * SSM_P
POOL_WINDOWS = (2, 4, 8, 16)
POOL_GW = D_MODEL // len(POOL_WINDOWS)
HALO = 16
RMS_EPS = 1e-6
N_CHIPS = 4
N_DEV = 8

SCAN_CHUNK = 1024
SCAN_BLOCKS = 1
ROW_CHUNK = 512
ROW_CHUNK_WIDE = 512
PROJ_ROWS = 1024
VMEM_LIMIT_BYTES = 56 * 1024 * 1024

ADAM_BLOCK_BYTES = 1 << 20
ADAM_LR, ADAM_B1, ADAM_B2, ADAM_EPS, ADAM_WD, ADAM_STEP = 0.001, 0.9, 0.999, 1e-08, 0.01, 10

_GELU_C0 = math.sqrt(2.0 / math.pi)
_GELU_C1 = 0.044715


def _cparams(*sem):
    if sem:
        return pltpu.CompilerParams(dimension_semantics=sem, vmem_limit_bytes=VMEM_LIMIT_BYTES)
    return pltpu.CompilerParams(vmem_limit_bytes=VMEM_LIMIT_BYTES)


def _sigmoid(v):
    return jax.nn.sigmoid(v)


def _silu(v):
    return v * _sigmoid(v)


def _dsilu(v):
    s = _sigmoid(v)
    return s * (1.0 + v * (1.0 - s))


def _gelu(v):
    return v * (0.5 * (1.0 + jnp.tanh(_GELU_C0 * v * (1.0 + _GELU_C1 * (v * v)))))


def _gelu_and_grad(v):
    v2 = v * v
    t = jnp.tanh(_GELU_C0 * v * (1.0 + _GELU_C1 * v2))
    half = 0.5 * (1.0 + t)
    grad = half + (0.5 * _GELU_C0) * v * (1.0 - t * t) * (1.0 + (3.0 * _GELU_C1) * v2)
    return v * half, grad


def _silu_and_grad(v):
    s = _sigmoid(v)
    return v * s, s * (1.0 + v * (1.0 - s))


def _dot(a, b):
    return lax.dot_general(a, b, (((1,), (0,)), ((), ())), preferred_element_type=F32)


def _dot_nt(a, b):
    return lax.dot_general(a, b, (((1,), (1,)), ((), ())), preferred_element_type=F32)


def _dot_tn(a, b):
    return lax.dot_general(a, b, (((0,), (0,)), ((), ())), preferred_element_type=F32)


def _acc8(v):
    return v.reshape(v.shape[0] // SUBLANES, SUBLANES, v.shape[1]).sum(axis=0)


class _Ride:
    def __init__(self, inputs, out_shapes, scratch, start, wait):
        self.inputs, self.out_shapes, self.scratch, self.start, self.wait = inputs, out_shapes, scratch, start, wait


def _mm(a_parts, b_parts, *, name, ta=False, tb=False, out_dtype=F32, bm=512, bn=512, bk=512, ride=None,
        blocks_first=False):
    a_parts, b_parts = list(a_parts), list(b_parts)
    if ta:
        assert len(a_parts) == 1
        k_dim, m_dim = a_parts[0].shape
    else:
        m_dim = a_parts[0].shape[0]
        k_dim = sum(a.shape[1] for a in a_parts)
    if tb:
        assert len(b_parts) == 1
        n_dim = b_parts[0].shape[0]
    else:
        n_dim = sum(b.shape[1] for b in b_parts)
    bm, bn, bk = min(bm, m_dim), min(bn, n_dim), min(bk, k_dim)
    nm, nn, nk = m_dim // bm, n_dim // bn, k_dim // bk
    a_ranges, off = [], 0
    for a in a_parts:
        cnt = (a.shape[0] if ta else a.shape[1]) // bk
        a_ranges.append((off, cnt))
        off += cnt
    b_ranges, off = [], 0
    for b in b_parts:
        cnt = (b.shape[0] if tb else b.shape[1]) // bn
        b_ranges.append((off, cnt))
        off += cnt

    def a_spec(off, cnt):
        if ta:
            return pl.BlockSpec((bk, bm), lambda i, n, k: (k, i))
        return pl.BlockSpec((bm, bk), lambda i, n, k: (i, jnp.clip(k - off, 0, cnt - 1)))

    def b_spec(off, cnt):
        if tb:
            return pl.BlockSpec((bn, bk), lambda i, n, k: (n, k))
        return pl.BlockSpec((bk, bn), lambda i, n, k: (k, jnp.clip(n - off, 0, cnt - 1)))

    na, nb = len(a_parts), len(b_parts)
    dims = (((0 if ta else 1,), (1 if tb else 0,)), ((), ()))

    def kern_single(a_ref, b_ref, o_ref):
        o_ref[...] = lax.dot_general(a_ref[...].astype(BF16), b_ref[...].astype(BF16), dims,
                                     preferred_element_type=F32).astype(out_dtype)

    if na == 1 and nb == 1 and nk == 1 and not ride:
        assert not blocks_first
        return pl.pallas_call(
            kern_single, name=name, grid=(nm, nn),
            in_specs=[pl.BlockSpec((bk, bm), lambda i, n: (0, i)) if ta else pl.BlockSpec((bm, bk), lambda i, n: (i, 0)),
                      pl.BlockSpec((bn, bk), lambda i, n: (n, 0)) if tb else pl.BlockSpec((bk, bn), lambda i, n: (0, n))],
            out_specs=pl.BlockSpec((bm, bn), lambda i, n: (i, n)),
            out_shape=jax.ShapeDtypeStruct((m_dim, n_dim), out_dtype),
            compiler_params=_cparams("parallel", "parallel"),
        )(a_parts[0], b_parts[0])

    n_rin = len(ride.inputs) if ride else 0
    n_rout = len(ride.out_shapes) if ride else 0

    def kern(*refs):
        a_refs, b_refs = refs[:na], refs[na:na + nb]
        rin = refs[na + nb:na + nb + n_rin]
        o_ref = refs[na + nb + n_rin]
        rout = refs[na + nb + n_rin + 1:na + nb + n_rin + 1 + n_rout]
        acc = refs[na + nb + n_rin + 1 + n_rout]
        rsem = refs[na + nb + n_rin + 2 + n_rout:]
        i, n, k = pl.program_id(0), pl.program_id(1), pl.program_id(2)

        if ride:
            @pl.when((i == 0) & (n == 0) & (k == 0))
            def _():
                ride.start(rin, rout, rsem)

        if nk > 1:
            @pl.when(k == 0)
            def _():
                acc[...] = jnp.zeros_like(acc)

        for ja, (koff, kcnt) in enumerate(a_ranges):
            for jb, (noff, ncnt) in enumerate(b_ranges):
                def step(ja=ja, jb=jb):
                    a = a_refs[ja][...].astype(BF16)
                    b = b_refs[jb][...].astype(BF16)
                    prod = lax.dot_general(a, b, dims, preferred_element_type=F32)
                    if nk > 1:
                        acc[...] += prod
                    else:
                        o_ref[...] = prod.astype(out_dtype)

                if na == 1 and nb == 1:
                    step()
                else:
                    cond = (k >= koff) & (k < koff + kcnt) & (n >= noff) & (n < noff + ncnt)
                    pl.when(cond)(step)

        if nk > 1:
            @pl.when(k == nk - 1)
            def _():
                o_ref[...] = acc[...].astype(out_dtype)

        if ride:
            @pl.when((i == nm - 1) & (n == nn - 1) & (k == nk - 1))
            def _():
                ride.wait(rin, rout, rsem)

    any_spec = pl.BlockSpec(memory_space=pl.ANY)
    if blocks_first:
        out_spec = pl.BlockSpec((None, bm, bn), lambda i, n, k: (n, i, 0))
        out_shape = jax.ShapeDtypeStruct((nn, m_dim, bn), out_dtype)
    else:
        out_spec = pl.BlockSpec((bm, bn), lambda i, n, k: (i, n))
        out_shape = jax.ShapeDtypeStruct((m_dim, n_dim), out_dtype)
    acc_shape = pltpu.VMEM((bm, bn) if nk > 1 else (SUBLANES, LANES), F32)
    if not ride:
        return pl.pallas_call(
            kern, name=name, grid=(nm, nn, nk),
            in_specs=[a_spec(*r) for r in a_ranges] + [b_spec(*r) for r in b_ranges],
            out_specs=out_spec, out_shape=out_shape, scratch_shapes=[acc_shape],
            compiler_params=_cparams("parallel", "parallel", "arbitrary"),
        )(*a_parts, *b_parts)
    return pl.pallas_call(
        kern, name=name, grid=(nm, nn, nk),
        in_specs=[a_spec(*r) for r in a_ranges] + [b_spec(*r) for r in b_ranges] + [any_spec] * n_rin,
        out_specs=(out_spec,) + (any_spec,) * n_rout, out_shape=(out_shape,) + tuple(ride.out_shapes),
        scratch_shapes=[acc_shape] + list(ride.scratch),
        compiler_params=_cparams("arbitrary", "arbitrary", "arbitrary"),
    )(*a_parts, *b_parts, *ride.inputs)


def _ssm_param_fn(a_re, a_im, log_dt, b_re, b_im):
    dt = jnp.exp(log_dt)
    lam_re = jnp.minimum(a_re, -1e-4)
    lam_im = a_im
    mag = jnp.exp(lam_re * dt)
    abar_re = mag * jnp.cos(lam_im * dt)
    abar_im = mag * jnp.sin(lam_im * dt)
    den = lam_re * lam_re + lam_im * lam_im
    num_re = abar_re - 1.0
    f_re = (num_re * lam_re + abar_im * lam_im) / den
    f_im = (abar_im * lam_re - num_re * lam_im) / den
    bb_re = f_re * b_re - f_im * b_im
    bb_im = f_re * b_im + f_im * b_re
    return abar_re, abar_im, bb_re, bb_im


def _ssm_params(a_re, a_im, log_dt, b_re_t, b_im_t):
    def kern(are, aim, ldt, bre, bim, o_ar, o_ai, o_br, o_bi):
        ar, ai, br, bi = _ssm_param_fn(are[...], aim[...], ldt[...], bre[...], bim[...])
        o_ar[...] = ar
        o_ai[...] = ai
        o_br[...] = br
        o_bi[...] = bi

    gp = jax.ShapeDtypeStruct((SSM_G, SSM_P), F32)
    hgp = jax.ShapeDtypeStruct((SSM_H, SSM_G, SSM_P), F32)
    return pl.pallas_call(kern, name="ssm_params", out_shape=(gp, gp, hgp, hgp), compiler_params=_cparams())(
        a_re, a_im, log_dt, b_re_t, b_im_t)


def _ssm_params_bwd(a_re, a_im, log_dt, b_re_t, b_im_t, d_ar, d_ai, d_bbr, d_bbi):
    def kern(are, aim, ldt, bre, bim, dar, dai, dbr, dbi, o_are, o_aim, o_ldt, o_bre, o_bim):
        prim = (are[...], aim[...], ldt[...], bre[...], bim[...])
        _, vjp = jax.vjp(_ssm_param_fn, *prim)
        g = vjp((dar[...], dai[...], dbr[...], dbi[...]))
        o_are[...] = g[0]
        o_aim[...] = g[1]
        o_ldt[...] = g[2]
        o_bre[...] = g[3]
        o_bim[...] = g[4]

    gp = jax.ShapeDtypeStruct((SSM_G, SSM_P), F32)
    g1 = jax.ShapeDtypeStruct((SSM_G, 1), F32)
    hgp = jax.ShapeDtypeStruct((SSM_H, SSM_G, SSM_P), F32)
    return pl.pallas_call(kern, name="ssm_params_bwd", out_shape=(gp, gp, g1, hgp, hgp), compiler_params=_cparams())(
        a_re, a_im, log_dt, b_re_t, b_im_t, d_ar, d_ai, d_bbr, d_bbi)


def _pow_tables(abar_re, abar_im, tc):
    ls = tc // SUBLANES

    def kern(ar_ref, ai_ref, fr_ref, fi_ref, rr_ref, ri_ref):
        a_re = jnp.broadcast_to(ar_ref[...], (SUBLANES, STATE_W))
        a_im = jnp.broadcast_to(ai_ref[...], (SUBLANES, STATE_W))
        p_re, p_im = a_re, a_im
        for i in range(ls):
            fwd = pl.ds(SUBLANES * i, SUBLANES)
            rev = pl.ds(SUBLANES * (ls - 1 - i), SUBLANES)
            fr_ref[fwd, :] = p_re
            fi_ref[fwd, :] = p_im
            rr_ref[rev, :] = p_re
            ri_ref[rev, :] = p_im
            p_re, p_im = p_re * a_re - p_im * a_im, p_re * a_im + p_im * a_re

    vec = pl.BlockSpec((1, STATE_W), lambda b: (0, b))
    tab = pl.BlockSpec((tc, STATE_W), lambda b: (0, b))
    shp = jax.ShapeDtypeStruct((tc, STATE_ALL), F32)
    return pl.pallas_call(
        kern, name="pow_tables", grid=(LANE_BLOCKS,), in_specs=[vec, vec], out_specs=(tab, tab, tab, tab),
        out_shape=(shp, shp, shp, shp), compiler_params=_cparams("parallel"))(abar_re, abar_im)


def _mod_kernel(c_row, w_ada_bf, b_ada):
    def kern(c_ref, w_ref, b_ref, m_ref, s_ref):
        cv = c_ref[...]
        sc = _silu(cv)
        s_ref[...] = sc
        lhs = jnp.broadcast_to(sc, (SUBLANES, D_MODEL)).astype(BF16)
        m_ref[...] = _dot(lhs, w_ref[...]) + b_ref[...]

    return pl.pallas_call(
        kern, name="ada_mod",
        out_shape=(jax.ShapeDtypeStruct((SUBLANES, 3 * D_MODEL), F32), jax.ShapeDtypeStruct((1, D_MODEL), F32)),
        compiler_params=_cparams())(c_row, w_ada_bf, b_ada)


def _row_spec(tr, width=D_MODEL, col=0):
    return pl.BlockSpec((tr, width), lambda c: (c, col))


def _vec_spec(width=D_MODEL):
    return pl.BlockSpec((1, width), lambda c: (0, 0))


def _col_spec(tr):
    return pl.BlockSpec((D_MODEL, tr), lambda c: (0, c))


def _in_norm(x, g1, scale, shift):
    seq = x.shape[0]
    tr = min(ROW_CHUNK_WIDE, seq)

    def kern(x_ref, g_ref, sc_ref, sh_ref, h_ref, ht_ref):
        xv = x_ref[...]
        r = lax.rsqrt(jnp.mean(xv * xv, axis=-1, keepdims=True) + RMS_EPS)
        h = ((xv * r) * g_ref[...]) * (1.0 + sc_ref[...]) + sh_ref[...]
        h_ref[...] = h.astype(BF16)
        ht_ref[...] = h.T.astype(BF16)

    return pl.pallas_call(
        kern, name="in_norm", grid=(seq // tr,),
        in_specs=[_row_spec(tr), _vec_spec(), _vec_spec(), _vec_spec()], out_specs=(_row_spec(tr), _col_spec(tr)),
        out_shape=(jax.ShapeDtypeStruct((seq, D_MODEL), BF16), jax.ShapeDtypeStruct((D_MODEL, seq), BF16)),
        compiler_params=_cparams("parallel"))(x, g1, scale, shift)


PAD = SUBLANES


def _window_sums(src, cols, w, bufs, rows, ahead):
    cur, cur_cols, step, k = src, cols, 1, 0
    data = pl.ds(PAD, rows)
    while step < w:
        dst = bufs[k % 2]
        dst[data, :] = cur[data, cur_cols] + cur[pl.ds(PAD + (step if ahead else -step), rows), cur_cols]
        cur, cur_cols, step, k = dst, slice(None), 2 * step, k + 1
    return cur, cur_cols


def _in_norm_proj_own(x, g1, scale, shift, w_own, chip, ride):
    seq, n_own = x.shape[0], w_own.shape[1]
    tr = min(PROJ_ROWS, seq)
    nc = seq // tr
    n_rin, n_rout = len(ride.inputs), len(ride.out_shapes)

    def kern(chip_ref, x_ref, g_ref, sc_ref, sh_ref, w_ref, *rest):
        rin, (h_ref, ht_ref, p_ref) = rest[:n_rin], rest[n_rin:n_rin + 3]
        rout, rsem = rest[n_rin + 3:n_rin + 3 + n_rout], rest[n_rin + 3 + n_rout:]
        c = pl.program_id(0)

        @pl.when(c == 0)
        def _():
            ride.start(rin, rout, rsem)

        xv = x_ref[...]
        r = lax.rsqrt(jnp.mean(xv * xv, axis=-1, keepdims=True) + RMS_EPS)
        h = ((xv * r) * g_ref[...]) * (1.0 + sc_ref[...]) + sh_ref[...]
        hb = h.astype(BF16)
        h_ref[...] = hb
        ht_ref[...] = h.T.astype(BF16)
        p_ref[...] = _dot(hb, w_ref[...]).astype(BF16)

        @pl.when(c == nc - 1)
        def _():
            ride.wait(rin, rout, rsem)

    vec = pl.BlockSpec((1, D_MODEL), lambda c, k: (0, 0))
    return pl.pallas_call(
        kern, name="in_norm_proj_own",
        grid_spec=pltpu.PrefetchScalarGridSpec(
            num_scalar_prefetch=1, grid=(nc,),
            in_specs=[pl.BlockSpec((tr, D_MODEL), lambda c, k: (c, 0)), vec, vec, vec,
                      pl.BlockSpec((D_MODEL, n_own), lambda c, k: (0, 0))] + [_ANY] * n_rin,
            out_specs=(pl.BlockSpec((tr, D_MODEL), lambda c, k: (c, 0)), pl.BlockSpec((D_MODEL, tr), lambda c, k: (0, c)),
                       pl.BlockSpec((tr, n_own), lambda c, k: (c, k[0]))) + (_ANY,) * n_rout,
            scratch_shapes=list(ride.scratch)),
        out_shape=(jax.ShapeDtypeStruct((seq, D_MODEL), BF16), jax.ShapeDtypeStruct((D_MODEL, seq), BF16),
                   jax.ShapeDtypeStruct((seq, N_CHIPS * n_own), BF16)) + tuple(ride.out_shapes),
        compiler_params=_cparams("arbitrary"))(chip, x, g1, scale, shift, w_own, *ride.inputs)


def _proj_rest(h, w_blocks, proj, chip, ride):
    seq, n_own = h.shape[0], w_blocks.shape[2]
    tr = min(PROJ_ROWS, seq)
    nm, nn = seq // tr, N_CHIPS - 1
    n_rin, n_rout = len(ride.inputs), len(ride.out_shapes)

    def kern(chip_ref, h_ref, w_ref, _, *rest):
        rin, p_ref = rest[:n_rin], rest[n_rin]
        rout, rsem = rest[n_rin + 1:n_rin + 1 + n_rout], rest[n_rin + 1 + n_rout:]
        i, n = pl.program_id(0), pl.program_id(1)

        @pl.when((i == 0) & (n == 0))
        def _():
            ride.start(rin, rout, rsem)

        p_ref[...] = _dot(h_ref[...], w_ref[0]).astype(BF16)

        @pl.when((i == nm - 1) & (n == nn - 1))
        def _():
            ride.wait(rin, rout, rsem)

    def other(n, k):
        return (k[0] + 1 + n) % N_CHIPS

    return pl.pallas_call(
        kern, name="proj_rest",
        grid_spec=pltpu.PrefetchScalarGridSpec(
            num_scalar_prefetch=1, grid=(nm, nn),
            in_specs=[pl.BlockSpec((tr, D_MODEL), lambda i, n, k: (i, 0)),
                      pl.BlockSpec((1, D_MODEL, n_own), lambda i, n, k: (other(n, k), 0, 0)), _ANY] + [_ANY] * n_rin,
            out_specs=(pl.BlockSpec((tr, n_own), lambda i, n, k: (i, other(n, k))),) + (_ANY,) * n_rout,
            scratch_shapes=list(ride.scratch)),
        out_shape=(jax.ShapeDtypeStruct(proj.shape, BF16),) + tuple(ride.out_shapes),
        input_output_aliases={3: 0},
        compiler_params=_cparams("arbitrary", "arbitrary"))(chip, h, w_blocks, proj, *ride.inputs)


def _pool_windows(ext, bufs, pos, g, w, tr):
    cols = pl.ds(g * POOL_GW, POOL_GW)
    chunk = pl.ds(PAD + HALO, tr)
    cur = ext[chunk, cols]
    win, win_cols = _window_sums(ext, cols, w, bufs, HALO + tr, ahead=False)
    cnt = jnp.minimum(pos + 1, w).astype(F32)
    return win[chunk, win_cols] / cnt - cur


def _zero_pads(refs, rows):
    for ref in refs:
        ref[0:PAD, :] = jnp.zeros((PAD, ref.shape[1]), F32)
        ref[PAD + rows:, :] = jnp.zeros((PAD, ref.shape[1]), F32)


def _pool_fwd(proj, pool_w_bf, pscale):
    seq = proj.shape[0]
    tr = min(ROW_CHUNK_WIDE, seq)
    hb = tr // HALO

    def kern(up_ref, halo_ref, zp_ref, pw_ref, ps_ref, y_ref, yt_ref, ext, buf_a, buf_b):
        c = pl.program_id(0)
        _zero_pads((ext, buf_a, buf_b), HALO + tr)
        ext[pl.ds(PAD, HALO), :] = jnp.where(c > 0, halo_ref[...].astype(F32), 0.0)
        ext[pl.ds(PAD + HALO, tr), :] = up_ref[...].astype(F32)
        pos = c * tr + lax.broadcasted_iota(jnp.int32, (tr, POOL_GW), 0)
        for g, w in enumerate(POOL_WINDOWS):
            cols = pl.ds(g * POOL_GW, POOL_GW)
            pooled = _pool_windows(ext, (buf_a, buf_b), pos, g, w, tr)
            mixed = _dot(pooled.astype(BF16), pw_ref[g])
            y = mixed * ps_ref[:, cols] * _silu(zp_ref[:, cols].astype(F32))
            y_ref[:, cols] = y.astype(BF16)
            yt_ref[cols, :] = y.T.astype(BF16)

    return pl.pallas_call(
        kern, name="pool_fwd", grid=(seq // tr,),
        in_specs=[_row_spec(tr, col=0),
                  pl.BlockSpec((HALO, D_MODEL), lambda c: (jnp.maximum(c * hb - 1, 0), 0)),
                  _row_spec(tr, col=1),
                  pl.BlockSpec((len(POOL_WINDOWS), POOL_GW, POOL_GW), lambda c: (0, 0, 0)),
                  _vec_spec()],
        out_specs=(_row_spec(tr), _col_spec(tr)),
        out_shape=(jax.ShapeDtypeStruct((seq, D_MODEL), BF16), jax.ShapeDtypeStruct((D_MODEL, seq), BF16)),
        scratch_shapes=[pltpu.VMEM((tr + HALO + 2 * PAD, D_MODEL), F32), pltpu.VMEM((tr + HALO + 2 * PAD, POOL_GW), F32),
                        pltpu.VMEM((tr + HALO + 2 * PAD, POOL_GW), F32)],
        compiler_params=_cparams("parallel"))(proj, proj, proj, pool_w_bf, pscale)


def _pool_bwd(proj, dyp, pool_w_bf, pscale, dproj):
    seq = proj.shape[0]
    tr = min(ROW_CHUNK_WIDE, seq)
    hb = tr // HALO
    nc = seq // tr
    n_halo = seq // HALO

    def kern(up_ref, halo_ref, zp_ref, zpn_ref, dyp_ref, dypn_ref, pw_ref, ps_ref, _,
             d01_ref, dpw_ref, dps_ref, ext, dpn, buf_a, buf_b, acc_pw, acc_ps):
        c = pl.program_id(0)

        @pl.when(c == 0)
        def _():
            acc_pw[...] = jnp.zeros_like(acc_pw)
            acc_ps[...] = jnp.zeros_like(acc_ps)

        _zero_pads((ext, dpn, buf_a, buf_b), HALO + tr)
        ext[pl.ds(PAD, HALO), :] = jnp.where(c > 0, halo_ref[...].astype(F32), 0.0)
        ext[pl.ds(PAD + HALO, tr), :] = up_ref[...].astype(F32)
        pos = c * tr + lax.broadcasted_iota(jnp.int32, (tr, POOL_GW), 0)
        pos_n = (c + 1) * tr + lax.broadcasted_iota(jnp.int32, (HALO, POOL_GW), 0)
        has_next = c < nc - 1
        for g, w in enumerate(POOL_WINDOWS):
            cols = pl.ds(g * POOL_GW, POOL_GW)
            pooled_bf = _pool_windows(ext, (buf_a, buf_b), pos, g, w, tr).astype(BF16)
            wg = pw_ref[g]
            mixed = _dot(pooled_bf, wg)
            zp = zp_ref[:, cols].astype(F32)
            sz = _silu(zp)
            dyp_g = dyp_ref[:, cols].astype(F32)
            ps = ps_ref[:, cols]
            dmixed = (dyp_g * ps * sz).astype(BF16)
            acc_ps[:, cols] += _acc8(dyp_g * mixed * sz)
            d01_ref[:, pl.ds(D_MODEL + g * POOL_GW, POOL_GW)] = (dyp_g * mixed * ps * _dsilu(zp)).astype(BF16)
            acc_pw[g] += _dot_tn(pooled_bf, dmixed)
            dpooled = _dot_nt(dmixed, wg)
            dmixed_n = (jnp.where(has_next, dypn_ref[:, cols].astype(F32), 0.0) * ps * _silu(zpn_ref[:, cols].astype(F32))).astype(BF16)
            dpooled_n = _dot_nt(dmixed_n, wg)
            dpn[pl.ds(PAD, tr), :] = dpooled / jnp.minimum(pos + 1, w).astype(F32)
            dpn[pl.ds(PAD + tr, HALO), :] = dpooled_n / jnp.minimum(pos_n + 1, w).astype(F32)
            win, _ = _window_sums(dpn, slice(None), w, (buf_a, buf_b), tr + HALO, ahead=True)
            d01_ref[:, cols] = (win[pl.ds(PAD, tr), :] - dpooled).astype(BF16)

        @pl.when(c == nc - 1)
        def _():
            dpw_ref[...] = acc_pw[...]
            dps_ref[...] = jnp.sum(acc_ps[...], axis=0, keepdims=True)

    nxt = lambda c: (jnp.minimum((c + 1) * hb, n_halo - 1), 0)
    nxt1 = lambda c: (jnp.minimum((c + 1) * hb, n_halo - 1), 1)
    return pl.pallas_call(
        kern, name="pool_bwd", grid=(nc,),
        in_specs=[_row_spec(tr, col=0),
                  pl.BlockSpec((HALO, D_MODEL), lambda c: (jnp.maximum(c * hb - 1, 0), 0)),
                  _row_spec(tr, col=1),
                  pl.BlockSpec((HALO, D_MODEL), nxt1),
                  _row_spec(tr),
                  pl.BlockSpec((HALO, D_MODEL), nxt),
                  pl.BlockSpec((len(POOL_WINDOWS), POOL_GW, POOL_GW), lambda c: (0, 0, 0)),
                  _vec_spec(), _ANY],
        out_specs=(pl.BlockSpec((tr, 2 * D_MODEL), lambda c: (c, 0)),
                   pl.BlockSpec((len(POOL_WINDOWS), POOL_GW, POOL_GW), lambda c: (0, 0, 0)),
                   _vec_spec()),
        out_shape=(jax.ShapeDtypeStruct(dproj.shape, BF16),
                   jax.ShapeDtypeStruct((len(POOL_WINDOWS), POOL_GW, POOL_GW), F32),
                   jax.ShapeDtypeStruct((1, D_MODEL), F32)),
        scratch_shapes=[pltpu.VMEM((tr + HALO + 2 * PAD, D_MODEL), F32)]
        + [pltpu.VMEM((tr + HALO + 2 * PAD, POOL_GW), F32)] * 3
        + [pltpu.VMEM((len(POOL_WINDOWS), POOL_GW, POOL_GW), F32), pltpu.VMEM((SUBLANES, D_MODEL), F32)],
        input_output_aliases={8: 0},
        compiler_params=_cparams("arbitrary"))(proj, proj, proj, proj, dyp, dyp, pool_w_bf, pscale, dproj)


def _glu_fwd(ys, proj, glu_w_bf, glu_b):
    seq = ys.shape[0]
    tr = min(ROW_CHUNK_WIDE, seq)

    def kern(ys_ref, zs_ref, w_ref, b_ref, o_ref, ot_ref):
        yg = _gelu(ys_ref[...])
        q = _dot(yg.astype(BF16), w_ref[...]) + b_ref[...]
        y = yg * _sigmoid(q) * _silu(zs_ref[...].astype(F32))
        o_ref[...] = y.astype(BF16)
        ot_ref[...] = y.T.astype(BF16)

    return pl.pallas_call(
        kern, name="glu_fwd", grid=(seq // tr,),
        in_specs=[_row_spec(tr), _row_spec(tr, col=3), pl.BlockSpec((D_MODEL, D_MODEL), lambda c: (0, 0)), _vec_spec()],
        out_specs=(_row_spec(tr), _col_spec(tr)),
        out_shape=(jax.ShapeDtypeStruct((seq, D_MODEL), BF16), jax.ShapeDtypeStruct((D_MODEL, seq), BF16)),
        compiler_params=_cparams("parallel"))(ys, proj, glu_w_bf, glu_b)


def _glu_bwd(ys, proj, dyssm, glu_w_bf, glu_b, dproj):
    seq = ys.shape[0]
    tr = min(ROW_CHUNK_WIDE, seq)
    nc = seq // tr

    def kern(ys_ref, zs_ref, dy_ref, w_ref, b_ref, _, dys_ref, dzs_ref, dq_ref, yg_ref, db_ref, acc_b):
        c = pl.program_id(0)

        @pl.when(c == 0)
        def _():
            acc_b[...] = jnp.zeros_like(acc_b)

        yg, dgelu = _gelu_and_grad(ys_ref[...])
        yg_bf = yg.astype(BF16)
        q = _dot(yg_bf, w_ref[...]) + b_ref[...]
        sg = _sigmoid(q)
        silu_z, dsilu_z = _silu_and_grad(zs_ref[...].astype(F32))
        dyv = dy_ref[...].astype(F32)
        dyglu = dyv * silu_z
        yglu = yg * sg
        dzs_ref[...] = (dyv * yglu * dsilu_z).astype(BF16)
        dq = dyglu * yglu * (1.0 - sg)
        dq_bf = dq.astype(BF16)
        acc_b[...] += _acc8(dq)
        dyg = dyglu * sg + _dot_nt(dq_bf, w_ref[...])
        dys_ref[...] = dyg * dgelu
        dq_ref[...] = dq_bf
        yg_ref[...] = yg.T.astype(BF16)

        @pl.when(c == nc - 1)
        def _():
            db_ref[...] = jnp.sum(acc_b[...], axis=0, keepdims=True)

    bf = jax.ShapeDtypeStruct((seq, D_MODEL), BF16)
    return pl.pallas_call(
        kern, name="glu_bwd", grid=(nc,),
        in_specs=[_row_spec(tr), _row_spec(tr, col=3), _row_spec(tr),
                  pl.BlockSpec((D_MODEL, D_MODEL), lambda c: (0, 0)), _vec_spec(), _ANY],
        out_specs=(_row_spec(tr), _row_spec(tr, col=3), _row_spec(tr), _col_spec(tr), _vec_spec()),
        out_shape=(jax.ShapeDtypeStruct((seq, D_MODEL), F32), jax.ShapeDtypeStruct(dproj.shape, BF16), bf,
                   jax.ShapeDtypeStruct((D_MODEL, seq), BF16), jax.ShapeDtypeStruct((1, D_MODEL), F32)),
        scratch_shapes=[pltpu.VMEM((SUBLANES, D_MODEL), F32)],
        input_output_aliases={5: 1},
        compiler_params=_cparams("arbitrary"))(ys, proj, dyssm, glu_w_bf, glu_b, dproj)


def _out_fwd_bwd(ypool, yssm, proj, x, tgt, gate, g2, wbp_bf, wbs_bf, wout_bf):
    seq = x.shape[0]
    tr = min(ROW_CHUNK, seq)
    nc = seq // tr

    def kern(yp_ref, ysm_ref, gp_ref, gs_ref, x_ref, t_ref, gate_ref, g2_ref, wbp_ref, wbs_ref, wo_ref,
             dy_ref, dyp_ref, dys_ref, d45_ref, mb_ref, dob_ref, dbp_ref, dbs_ref, loss_ref, dgate_ref, dg2_ref,
             acc_l, acc_gate, acc_g2):
        c = pl.program_id(0)

        @pl.when(c == 0)
        def _():
            acc_l[...] = jnp.zeros_like(acc_l)
            acc_gate[...] = jnp.zeros_like(acc_gate)
            acc_g2[...] = jnp.zeros_like(acc_g2)

        bp = _dot(yp_ref[...], wbp_ref[...])
        bs = _dot(ysm_ref[...], wbs_ref[...])
        sp = _sigmoid(gp_ref[...].astype(F32))
        ss = _sigmoid(gs_ref[...].astype(F32))
        merged = sp * bp + ss * bs
        mb = merged.astype(BF16)
        out = _dot(mb, wo_ref[...])
        r2 = lax.rsqrt(jnp.mean(out * out, axis=-1, keepdims=True) + RMS_EPS)
        oh = out * r2
        gate_v, g2_v = gate_ref[...], g2_ref[...]
        ohg = oh * g2_v
        diff = (x_ref[...] + gate_v * ohg) - t_ref[...]
        acc_l[...] += _acc8(diff * diff)
        dyv = diff * (1.0 / D_MODEL)
        dy_ref[...] = dyv
        dy_oh = dyv * oh
        acc_gate[...] += _acc8(dy_oh * g2_v)
        acc_g2[...] += _acc8(dy_oh * gate_v)
        gg = gate_v * g2_v
        doh = dyv * gg
        dout = r2 * (doh - oh * jnp.mean(dy_oh * gg, axis=-1, keepdims=True))
        dob = dout.astype(BF16)
        dmerged = _dot_nt(dob, wo_ref[...])
        dbp_f = dmerged * sp
        dbs_f = dmerged * ss
        dbp = dbp_f.astype(BF16)
        dbs = dbs_f.astype(BF16)
        d45_ref[:, 0:D_MODEL] = (dbp_f * bp * (1.0 - sp)).astype(BF16)
        d45_ref[:, D_MODEL:] = (dbs_f * bs * (1.0 - ss)).astype(BF16)
        dyp_ref[...] = _dot_nt(dbp, wbp_ref[...]).astype(BF16)
        dys_ref[...] = _dot_nt(dbs, wbs_ref[...]).astype(BF16)
        mb_ref[...] = merged.T.astype(BF16)
        dob_ref[...] = dob
        dbp_ref[...] = dbp
        dbs_ref[...] = dbs

        @pl.when(c == nc - 1)
        def _():
            tot = jnp.sum(acc_l[...], axis=0, keepdims=True)
            loss_ref[...] = jnp.sum(tot, axis=1, keepdims=True) * (0.5 / D_MODEL)
            dgate_ref[...] = jnp.sum(acc_gate[...], axis=0, keepdims=True)
            dg2_ref[...] = jnp.sum(acc_g2[...], axis=0, keepdims=True)

    wspec = pl.BlockSpec((D_MODEL, D_MODEL), lambda c: (0, 0))
    f32 = jax.ShapeDtypeStruct((seq, D_MODEL), F32)
    bf = jax.ShapeDtypeStruct((seq, D_MODEL), BF16)
    vec = jax.ShapeDtypeStruct((1, D_MODEL), F32)
    acc = pltpu.VMEM((SUBLANES, D_MODEL), F32)
    return pl.pallas_call(
        kern, name="out_fwd_bwd", grid=(nc,),
        in_specs=[_row_spec(tr), _row_spec(tr), _row_spec(tr, col=4), _row_spec(tr, col=5), _row_spec(tr), _row_spec(tr),
                  _vec_spec(), _vec_spec(), wspec, wspec, wspec],
        out_specs=(_row_spec(tr), _row_spec(tr), _row_spec(tr), pl.BlockSpec((tr, 2 * D_MODEL), lambda c: (c, 2)),
                   _col_spec(tr), _row_spec(tr), _row_spec(tr), _row_spec(tr),
                   pl.BlockSpec((1, 1), lambda c: (0, 0)), _vec_spec(), _vec_spec()),
        out_shape=(f32, bf, bf, jax.ShapeDtypeStruct((seq, proj.shape[1]), BF16),
                   jax.ShapeDtypeStruct((D_MODEL, seq), BF16), bf, bf, bf,
                   jax.ShapeDtypeStruct((1, 1), F32), vec, vec),
        scratch_shapes=[acc, acc, acc],
        compiler_params=_cparams("arbitrary"))(ypool, yssm, proj, proj, x, tgt, gate, g2, wbp_bf, wbs_bf, wout_bf)


def _in_bwd(dh, x, dy, g1, scale):
    seq = x.shape[0]
    tr = min(ROW_CHUNK_WIDE, seq)
    nc = seq // tr

    def kern(dh_ref, x_ref, dy_ref, g_ref, sc_ref, dx_ref, dsh_ref, dsc_ref, dg_ref, a_sh, a_sc, a_g):
        c = pl.program_id(0)

        @pl.when(c == 0)
        def _():
            a_sh[...] = jnp.zeros_like(a_sh)
            a_sc[...] = jnp.zeros_like(a_sc)
            a_g[...] = jnp.zeros_like(a_g)

        xv = x_ref[...]
        r = lax.rsqrt(jnp.mean(xv * xv, axis=-1, keepdims=True) + RMS_EPS)
        xh = xv * r
        g = g_ref[...]
        dhv = dh_ref[...]
        a_sh[...] += _acc8(dhv)
        a_sc[...] += _acc8(dhv * (xh * g))
        dn = dhv * (1.0 + sc_ref[...])
        a_g[...] += _acc8(dn * xh)
        dxh = dn * g
        dx_ref[...] = dy_ref[...] + r * (dxh - xh * jnp.mean(dxh * xh, axis=-1, keepdims=True))

        @pl.when(c == nc - 1)
        def _():
            dsh_ref[...] = jnp.sum(a_sh[...], axis=0, keepdims=True)
            dsc_ref[...] = jnp.sum(a_sc[...], axis=0, keepdims=True)
            dg_ref[...] = jnp.sum(a_g[...], axis=0, keepdims=True)

    vec = jax.ShapeDtypeStruct((1, D_MODEL), F32)
    acc = pltpu.VMEM((SUBLANES, D_MODEL), F32)
    return pl.pallas_call(
        kern, name="in_bwd", grid=(nc,),
        in_specs=[_row_spec(tr), _row_spec(tr), _row_spec(tr), _vec_spec(), _vec_spec()],
        out_specs=(_row_spec(tr), _vec_spec(), _vec_spec(), _vec_spec()),
        out_shape=(jax.ShapeDtypeStruct((seq, D_MODEL), F32), vec, vec, vec),
        scratch_shapes=[acc, acc, acc],
        compiler_params=_cparams("arbitrary"))(dh, x, dy, g1, scale)


SLAB = 2 * SUBLANES


def _local_scan(a_re, a_im, br, bi, xr, xi, row0, ls, reverse, init=None, xb=None):
    if init is None:
        x_re = jnp.zeros((SUBLANES, STATE_W), F32)
        x_im = jnp.zeros((SUBLANES, STATE_W), F32)
    else:
        x_re, x_im = init
    for i in (range(ls - 1, -1, -1) if reverse else range(ls)):
        src = pl.ds(SUBLANES * i, SUBLANES)
        dst = pl.ds(row0 + SUBLANES * i, SUBLANES)
        n_re = a_re * x_re - a_im * x_im + br[src, :]
        n_im = a_re * x_im + a_im * x_re + bi[src, :]
        if xb is not None and i % 2 == 1:
            pair = pl.ds(SUBLANES * (i - 1), SLAB)
            xb[0][pair, :] = jnp.concatenate([x_re, n_re], axis=0).astype(BF16)
            xb[1][pair, :] = jnp.concatenate([x_im, n_im], axis=0).astype(BF16)
        x_re, x_im = n_re, n_im
        xr[dst, :] = x_re
        xi[dst, :] = x_im
    return x_re, x_im


def _two(v):
    return jnp.concatenate([v, v], axis=0)


def _unpermute_rhs(v, sel):
    hi = v.astype(BF16)
    r1 = v - hi.astype(F32)
    mid = r1.astype(BF16)
    lo = (r1 - mid.astype(F32)).astype(BF16)
    return _dot(hi, sel) + _dot(mid, sel) + _dot(lo, sel)


def _scan_specs(tc, nb, rows_of):
    return dict(
        us=pl.BlockSpec((tc, nb * LANES), lambda b, c: (rows_of(c), 2 * D_MODEL // (nb * LANES) + b)),
        tok=pl.BlockSpec((tc, nb * LANES), lambda b, c: (rows_of(c), b)),
        bblk=pl.BlockSpec((nb, LANES, STATE_W), lambda b, c: (b, 0, 0)),
        cblk=pl.BlockSpec((nb, STATE_W, LANES), lambda b, c: (b, 0, 0)),
        vec=pl.BlockSpec((1, nb * STATE_W), lambda b, c: (0, b)),
        tab=pl.BlockSpec((tc, nb * STATE_W), lambda b, c: (0, b)),
        car=pl.BlockSpec((SUBLANES, nb * STATE_W), lambda b, c: (rows_of(c), b)),
        dvec=pl.BlockSpec((1, nb * LANES), lambda b, c: (0, b)))


def _ssm_scan_fwd(proj, bb_re, bb_im, cm_re, cm_im, abar_re, abar_im, pw_re, pw_im, d_skip, tc):
    seq = proj.shape[0]
    nc = seq // tc
    ls = tc // SUBLANES
    nb = SCAN_BLOCKS

    def kern(us_ref, bbr_ref, bbi_ref, cmr_ref, cmi_ref, ar_ref, ai_ref, pwr_ref, pwi_ref, d_ref,
             ys_ref, ecr_ref, eci_ref, bur, bui, car_r, car_i, end_r, end_i, upb, xb_r, xb_i, *nat):
        c = pl.program_id(1)

        @pl.when(c == 0)
        def _():
            car_r[...] = jnp.zeros_like(car_r)
            car_i[...] = jnp.zeros_like(car_i)

        for j in range(nb):
            cols = pl.ds(j * LANES, LANES)
            scols = pl.ds(j * STATE_W, STATE_W)
            nat[j][...] = us_ref[:, cols].astype(F32)
            for i in range(ls):
                upb[j, pl.ds(SUBLANES * i, SUBLANES), :] = nat[j][pl.ds(i, SUBLANES, stride=ls), :]
            u = upb[j]
            up = u.astype(BF16)
            bur[j] = _dot(up, bbr_ref[j])
            bui[j] = _dot(up, bbi_ref[j])
            a_re = jnp.broadcast_to(ar_ref[:, scols], (SUBLANES, STATE_W))
            a_im = jnp.broadcast_to(ai_ref[:, scols], (SUBLANES, STATE_W))
            x_re, x_im = _local_scan(a_re, a_im, bur.at[j], bui.at[j], bur.at[j], bui.at[j], 0, ls, False)
            end_r[j] = x_re
            end_i[j] = x_im
            big_re = pwr_ref[tc - 1:tc, scols]
            big_im = pwi_ref[tc - 1:tc, scols]
            e_re = car_r[j, 0:1, :]
            e_im = car_i[j, 0:1, :]
            for s in range(SUBLANES):
                n_re = end_r[j, s:s + 1, :] + big_re * e_re - big_im * e_im
                n_im = end_i[j, s:s + 1, :] + big_re * e_im + big_im * e_re
                e_re, e_im = n_re, n_im
                if s < SUBLANES - 1:
                    car_r[j, s + 1:s + 2, :] = e_re
                    car_i[j, s + 1:s + 2, :] = e_im
            ec_re = car_r[j]
            ec_im = car_i[j]
            ecr_ref[:, scols] = ec_re
            eci_ref[:, scols] = ec_im
            e2_re, e2_im = _two(ec_re), _two(ec_im)
            for k in range(tc // SLAB):
                rows_k = pl.ds(SLAB * k, SLAB)
                p_re = pwr_ref[rows_k, scols]
                p_im = pwi_ref[rows_k, scols]
                xb_r[j, rows_k, :] = (bur[j, rows_k, :] + p_re * e2_re - p_im * e2_im).astype(BF16)
                xb_i[j, rows_k, :] = (bui[j, rows_k, :] + p_re * e2_im + p_im * e2_re).astype(BF16)
            upb[j] = _dot(xb_r[j], cmr_ref[j]) - _dot(xb_i[j], cmi_ref[j]) + d_ref[:, cols] * u
            for i in range(ls):
                nat[j][pl.ds(i, SUBLANES, stride=ls), :] = upb[j, pl.ds(SUBLANES * i, SUBLANES), :]
            ys_ref[:, cols] = nat[j][...]
            car_r[j, 0:1, :] = e_re
            car_i[j, 0:1, :] = e_im

    sp = _scan_specs(tc, nb, lambda c: c)
    carry_shape = jax.ShapeDtypeStruct((nc * SUBLANES, STATE_ALL), F32)
    small = pltpu.VMEM((nb, SUBLANES, STATE_W), F32)
    big = pltpu.VMEM((nb, tc, STATE_W), F32)
    return pl.pallas_call(
        kern, name="ssm_scan_fwd", grid=(LANE_BLOCKS // nb, nc),
        in_specs=[sp["us"], sp["bblk"], sp["bblk"], sp["cblk"], sp["cblk"], sp["vec"], sp["vec"], sp["tab"], sp["tab"],
                  sp["dvec"]],
        out_specs=(sp["tok"], sp["car"], sp["car"]),
        out_shape=(jax.ShapeDtypeStruct((seq, D_MODEL), F32), carry_shape, carry_shape),
        scratch_shapes=[big, big, small, small, small, small, pltpu.VMEM((nb, tc, LANES), F32),
                        pltpu.VMEM((nb, tc, STATE_W), BF16), pltpu.VMEM((nb, tc, STATE_W), BF16)]
        + [pltpu.VMEM((tc, LANES), F32)] * nb,
        compiler_params=_cparams("parallel", "arbitrary"),
    )(proj, bb_re, bb_im, cm_re, cm_im, abar_re, abar_im, pw_re, pw_im, d_skip)


def _ssm_scan_bwd(proj, dys, ec_re, ec_im, bb_re, bb_im, cm_re, cm_im, abar_re, abar_im,
                  pw_re, pw_im, pv_re, pv_im, d_skip, dproj, tc):
    seq = proj.shape[0]
    nc = seq // tc
    ls = tc // SUBLANES
    nb = SCAN_BLOCKS

    def kern(us_ref, dys_ref, ecr_ref, eci_ref, bbr_ref, bbi_ref, cmr_ref, cmi_ref, ar_ref, ai_ref,
             pwr_ref, pwi_ref, pvr_ref, pvi_ref, d_ref, _,
             dus_ref, dbbr_ref, dbbi_ref, dcmr_ref, dcmi_ref, dar_ref, dai_ref, dd_ref,
             bur, bui, xr, xi, gr, gi, fc_r, fc_i, a_bbr, a_bbi, a_cmr, a_cmi, a_ar, a_ai, a_dd, upb, dpb, hb_r, hb_i,
             *nat):
        c = pl.program_id(1)

        @pl.when(c == 0)
        def _():
            for ref in (fc_r, fc_i, a_bbr, a_bbi, a_cmr, a_cmi, a_ar, a_ai, a_dd):
                ref[...] = jnp.zeros_like(ref)

        for j in range(nb):
            cols = pl.ds(j * LANES, LANES)
            scols = pl.ds(j * STATE_W, STATE_W)
            nat_u, nat_d = nat[2 * j], nat[2 * j + 1]
            nat_u[...] = us_ref[:, cols].astype(F32)
            nat_d[...] = dys_ref[:, cols]
            for i in range(ls):
                rows_i = pl.ds(SUBLANES * i, SUBLANES)
                upb[j, rows_i, :] = nat_u[pl.ds(i, SUBLANES, stride=ls), :]
                dpb[j, rows_i, :] = nat_d[pl.ds(i, SUBLANES, stride=ls), :]
            u = upb[j]
            dysv = dpb[j]
            a_dd[j] += _acc8(dysv * u)
            up = u.astype(BF16)
            bur[j] = _dot(up, bbr_ref[j])
            bui[j] = _dot(up, bbi_ref[j])
            a_re = jnp.broadcast_to(ar_ref[:, scols], (SUBLANES, STATE_W))
            a_im = jnp.broadcast_to(ai_ref[:, scols], (SUBLANES, STATE_W))
            ec_r = ecr_ref[:, scols]
            ec_i = eci_ref[:, scols]
            xr[j, 0:SUBLANES, :] = ec_r
            xi[j, 0:SUBLANES, :] = ec_i
            _local_scan(a_re, a_im, bur.at[j], bui.at[j], xr.at[j], xi.at[j], SUBLANES, ls, False, init=(ec_r, ec_i),
                        xb=(hb_r.at[j], hb_i.at[j]))
            dysp = dysv.astype(BF16)
            a_cmr[j] += _dot_tn(dysp, hb_r[j])
            a_cmi[j] -= _dot_tn(dysp, hb_i[j])
            gr[j] = _dot_nt(dysp, cmr_ref[j])
            gi[j] = -_dot_nt(dysp, cmi_ref[j])
            _local_scan(a_re, -a_im, gr.at[j], gi.at[j], gr.at[j], gi.at[j], 0, ls, True)
            big_re = pwr_ref[tc - 1:tc, scols]
            big_im = -pwi_ref[tc - 1:tc, scols]
            f_re = fc_r[j, SUBLANES - 1:SUBLANES, :]
            f_im = fc_i[j, SUBLANES - 1:SUBLANES, :]
            for s in range(SUBLANES - 1, -1, -1):
                n_re = gr[j, s:s + 1, :] + big_re * f_re - big_im * f_im
                n_im = gi[j, s:s + 1, :] + big_re * f_im + big_im * f_re
                f_re, f_im = n_re, n_im
                if s > 0:
                    fc_r[j, s - 1:s, :] = f_re
                    fc_i[j, s - 1:s, :] = f_im
            f2_r, f2_i = _two(fc_r[j]), _two(fc_i[j])
            acc_r = jnp.zeros((SUBLANES, STATE_W), F32)
            acc_i = jnp.zeros((SUBLANES, STATE_W), F32)
            for k in range(tc // SLAB):
                rows_k = pl.ds(SLAB * k, SLAB)
                q_re = pvr_ref[rows_k, scols]
                q_im = pvi_ref[rows_k, scols]
                lam_re = gr[j, rows_k, :] + q_re * f2_r + q_im * f2_i
                lam_im = gi[j, rows_k, :] + q_re * f2_i - q_im * f2_r
                xp_re = xr[j, rows_k, :]
                xp_im = xi[j, rows_k, :]
                d_r = lam_re * xp_re + lam_im * xp_im
                d_i = lam_im * xp_re - lam_re * xp_im
                acc_r = acc_r + (d_r[0:SUBLANES] + d_r[SUBLANES:])
                acc_i = acc_i + (d_i[0:SUBLANES] + d_i[SUBLANES:])
                hb_r[j, rows_k, :] = lam_re.astype(BF16)
                hb_i[j, rows_k, :] = lam_im.astype(BF16)
            a_ar[j] += acc_r
            a_ai[j] += acc_i
            fc_r[j, SUBLANES - 1:SUBLANES, :] = f_re
            fc_i[j, SUBLANES - 1:SUBLANES, :] = f_im
            lb_re = hb_r[j]
            lb_im = hb_i[j]
            a_bbr[j] += _dot_tn(up, lb_re)
            a_bbi[j] += _dot_tn(up, lb_im)
            dpb[j] = _dot_nt(lb_re, bbr_ref[j]) + _dot_nt(lb_im, bbi_ref[j]) + dysv * d_ref[:, cols]
            for i in range(ls):
                nat_d[pl.ds(i, SUBLANES, stride=ls), :] = dpb[j, pl.ds(SUBLANES * i, SUBLANES), :]
            dus_ref[:, cols] = nat_d[...].astype(BF16)

        @pl.when(c == nc - 1)
        def _():
            row_g = lax.broadcasted_iota(jnp.int32, (LANES, STATE_W), 0) // SSM_H
            col_g = lax.broadcasted_iota(jnp.int32, (LANES, STATE_W), 1) // SSM_P
            fold = (lax.broadcasted_iota(jnp.int32, (STATE_W, SSM_P), 0) % SSM_P
                    == lax.broadcasted_iota(jnp.int32, (STATE_W, SSM_P), 1)).astype(BF16)
            for j in range(nb):
                rows_j = pl.ds(j * LANES, LANES)
                for acc, out in ((a_bbr, dbbr_ref), (a_bbi, dbbi_ref), (a_cmr, dcmr_ref), (a_cmi, dcmi_ref)):
                    out[rows_j, :] = _unpermute_rhs(jnp.where(row_g == col_g, acc[j], 0.0), fold)
                dar_ref[:, pl.ds(j * STATE_W, STATE_W)] = jnp.sum(a_ar[j], axis=0, keepdims=True)
                dai_ref[:, pl.ds(j * STATE_W, STATE_W)] = jnp.sum(a_ai[j], axis=0, keepdims=True)
                dd_ref[:, pl.ds(j * LANES, LANES)] = jnp.sum(a_dd[j], axis=0, keepdims=True)

    sp = _scan_specs(tc, nb, lambda c: nc - 1 - c)
    ghp = pl.BlockSpec((nb * LANES, SSM_P), lambda b, c: (b, 0))
    ghp_shape = jax.ShapeDtypeStruct((SSM_G * SSM_H, SSM_P), F32)
    small = pltpu.VMEM((nb, SUBLANES, STATE_W), F32)
    big = pltpu.VMEM((nb, tc, STATE_W), F32)
    bigp = pltpu.VMEM((nb, tc + SUBLANES, STATE_W), F32)
    blk = pltpu.VMEM((nb, LANES, STATE_W), F32)
    tok = pltpu.VMEM((nb, tc, LANES), F32)
    return pl.pallas_call(
        kern, name="ssm_scan_bwd", grid=(LANE_BLOCKS // nb, nc),
        in_specs=[sp["us"], sp["tok"], sp["car"], sp["car"], sp["bblk"], sp["bblk"], sp["cblk"], sp["cblk"],
                  sp["vec"], sp["vec"], sp["tab"], sp["tab"], sp["tab"], sp["tab"], sp["dvec"], _ANY],
        out_specs=(sp["us"], ghp, ghp, ghp, ghp, sp["vec"], sp["vec"], sp["dvec"]),
        out_shape=(jax.ShapeDtypeStruct(dproj.shape, BF16), ghp_shape, ghp_shape, ghp_shape, ghp_shape,
                   jax.ShapeDtypeStruct((1, STATE_ALL), F32), jax.ShapeDtypeStruct((1, STATE_ALL), F32),
                   jax.ShapeDtypeStruct((1, D_MODEL), F32)),
        scratch_shapes=[big, big, bigp, bigp, big, big, small, small, blk, blk, blk, blk,
                        small, small, pltpu.VMEM((nb, SUBLANES, LANES), F32), tok, tok,
                        pltpu.VMEM((nb, tc, STATE_W), BF16), pltpu.VMEM((nb, tc, STATE_W), BF16)]
        + [pltpu.VMEM((tc, LANES), F32)] * (2 * nb),
        input_output_aliases={15: 0},
        compiler_params=_cparams("parallel", "arbitrary"),
    )(proj, dys, ec_re, ec_im, bb_re, bb_im, cm_re, cm_im, abar_re, abar_im, pw_re, pw_im, pv_re, pv_im, d_skip, dproj)


def _eye5():
    return jnp.asarray(np.eye(GROUPS_PER_BLOCK, dtype=np.float32)[None, :, None, :, None])


def _embed_b(bb_t):
    t = bb_t.transpose(1, 0, 2).reshape(LANE_BLOCKS, GROUPS_PER_BLOCK, SSM_H, 1, SSM_P)
    return (t * _eye5()).reshape(LANE_BLOCKS, LANES, STATE_W)


def _embed_c(c_ghp):
    t = c_ghp.transpose(0, 2, 1).reshape(LANE_BLOCKS, GROUPS_PER_BLOCK, SSM_P, 1, SSM_H)
    return (t * _eye5()).reshape(LANE_BLOCKS, STATE_W, LANES)


def _local_step(x, c_row, tgt, w_ada_bf, b_ada, g1, g2, w_in_bf, pool_w_bf, pscale, a_re, a_im, log_dt,
                b_re_t, b_im_t, c_re, c_im, d_skip, glu_w_bf, glu_b, wbp_bf, wbs_bf, wout_bf,
                split_proj=None, ride_for_dw_in=None, ride_for_dh=None, mod_fn=None):
    seq = x.shape[0]
    tc = min(SCAN_CHUNK, seq)
    mod8, silu_c = _mod_kernel(c_row, w_ada_bf, b_ada) if mod_fn is None else mod_fn(c_row, b_ada)
    mod = mod8[0:1]
    shift, scale, gate = mod[:, 0:D_MODEL], mod[:, D_MODEL:2 * D_MODEL], mod[:, 2 * D_MODEL:]

    abar_re, abar_im, bb_re_t, bb_im_t = _ssm_params(a_re, a_im, log_dt, b_re_t, b_im_t)
    abar_re_f, abar_im_f = abar_re.reshape(1, STATE_ALL), abar_im.reshape(1, STATE_ALL)
    pw_re, pw_im, pv_re, pv_im = _pow_tables(abar_re_f, abar_im_f, tc)
    bbe_re, bbe_im = _embed_b(bb_re_t).astype(BF16), _embed_b(bb_im_t).astype(BF16)
    cme_re, cme_im = _embed_c(c_re).astype(BF16), _embed_c(c_im).astype(BF16)
    d_row = d_skip.reshape(1, D_MODEL)

    if split_proj:
        w_own, chip, w_in_ride, unpack_w_in, late_ride, unpack_late = split_proj
        h, h_t, proj, w_blocks = _in_norm_proj_own(x, g1, scale, shift, w_own, chip, w_in_ride)
        w_in_bf = unpack_w_in(w_blocks)
        proj, *gathered = _proj_rest(h, w_blocks, proj, chip, late_ride)
        pool_w_bf, glu_w_bf, wbp_bf, wbs_bf, wout_bf = unpack_late(*gathered)
    else:
        h, h_t = _in_norm(x, g1, scale, shift)
        proj = _mm([h], [w_in_bf], name="proj", out_dtype=BF16, bm=1024, bn=1536, bk=1024)
    ypool, ypool_t = _pool_fwd(proj, pool_w_bf, pscale)
    ys, ec_re, ec_im = _ssm_scan_fwd(proj, bbe_re, bbe_im, cme_re, cme_im, abar_re_f, abar_im_f,
                                      pw_re, pw_im, d_row, tc)
    yssm, yssm_t = _glu_fwd(ys, proj, glu_w_bf, glu_b)
    (dy, dypool, dyssm, dproj, merged_t, dob, dbp, dbs, loss, dgate, dg2) = _out_fwd_bwd(
        ypool, yssm, proj, x, tgt, gate, g2, wbp_bf, wbs_bf, wout_bf)

    d_wout = _mm([merged_t], [dob], name="dw_out", bm=1024, bn=1024, bk=2048)
    d_wbp = _mm([ypool_t], [dbp], name="dw_bp", bm=1024, bn=1024, bk=2048)
    d_wbs = _mm([yssm_t], [dbs], name="dw_bs", bm=1024, bn=1024, bk=2048)
    dys, dproj, dq, yg_t, d_glu_b = _glu_bwd(ys, proj, dyssm, glu_w_bf, glu_b, dproj)
    d_glu_w = _mm([yg_t], [dq], name="dw_glu", bm=1024, bn=1024, bk=2048)
    (dproj, dbbe_re, dbbe_im, dcme_re, dcme_im, d_abar_re, d_abar_im, d_dskip) = _ssm_scan_bwd(
        proj, dys, ec_re, ec_im, bbe_re, bbe_im, cme_re, cme_im, abar_re_f, abar_im_f,
        pw_re, pw_im, pv_re, pv_im, d_row, dproj, tc)
    dproj, d_pool_w, d_pscale = _pool_bwd(proj, dypool, pool_w_bf, pscale, dproj)
    dparts = [dproj]
    small_ready = dict(
        dg2=dg2, d_pscale=d_pscale, d_glu_b=d_glu_b, d_dskip=d_dskip, d_abar_re=d_abar_re, d_abar_im=d_abar_im,
        d_bb_re_t=dbbe_re.reshape(SSM_G, SSM_H, SSM_P).transpose(1, 0, 2),
        d_bb_im_t=dbbe_im.reshape(SSM_G, SSM_H, SSM_P).transpose(1, 0, 2),
        d_c_re=dcme_re.reshape(SSM_G, SSM_H, SSM_P), d_c_im=dcme_im.reshape(SSM_G, SSM_H, SSM_P))
    ride = ride_for_dw_in(small_ready) if ride_for_dw_in else None
    d_win = _mm([h_t], dparts, name="dw_in", bm=1024, bn=6 * D_MODEL // N_CHIPS, bk=2048, ride=ride,
                blocks_first=True)
    rode_dw_in = ()
    if ride:
        d_win, rode_dw_in = d_win[0], tuple(d_win[1:])
    big_grads = dict(d_win=d_win, d_glu_w=d_glu_w, d_wbp=d_wbp, d_wbs=d_wbs, d_wout=d_wout, d_pool_w=d_pool_w)
    ride = ride_for_dh(big_grads) if ride_for_dh else None
    dh = _mm(dparts, [w_in_bf], tb=True, name="dh", bm=2048, bn=1024, bk=1024, ride=ride)
    rode = ()
    if ride:
        dh, rode = dh[0], tuple(dh[1:])
    grad_x, dshift, dscale, dg1 = _in_bwd(dh, x, dy, g1, scale)
    dmod = jnp.concatenate([dshift, dscale, dgate], axis=1)
    return dict(
        rode=rode, rode_dw_in=rode_dw_in, loss=loss[0, 0], grad_x=grad_x, dmod=dmod, silu_c=silu_c, dg1=dg1,
        **small_ready, **big_grads)


def _position():
    x, y, c = lax.axis_index("x"), lax.axis_index("y"), lax.axis_index("c")
    chips = [(1 - x, y), (x, 1 - y), (1 - x, 1 - y)]
    return x, y, c, chips


_ANY = pl.BlockSpec(memory_space=pl.ANY)
COMM_CHUNKS = 4
COMM_ROW_ALIGN = 16


def _row_chunks(rows, k):
    assert rows % (k * COMM_ROW_ALIGN) == 0, (rows, k)
    step = rows // k
    return [(q * step, step) for q in range(k)]


def _mod_sharded(c_row, w_own_bf, b_ada):
    n_own = w_own_bf.shape[1]
    assert N_CHIPS * n_own == b_ada.shape[1] and n_own % LANES == 0, (n_own, b_ada.shape)

    def kern(c_ref, w_ref, b_ref, m_ref, s_ref, rows, prods, got, send_sems, recv_sems):
        x, y, c, chips = _position()
        me = 2 * x + y

        def copy(k, src, dst, to):
            return pltpu.make_async_remote_copy(src_ref=src, dst_ref=dst, send_sem=send_sems.at[k],
                                                recv_sem=recv_sems.at[k], device_id=(*to, c), device_id_type=MESH_ID)

        number = [2 * cx + cy for cx, cy in chips]
        sc = _silu(c_ref[...])
        s_ref[...] = sc
        rows[me] = jnp.broadcast_to(sc, (SUBLANES, D_MODEL))
        out_rows = [copy(j, rows.at[me], rows.at[me], chip) for j, chip in enumerate(chips)]
        for cp in out_rows:
            cp.start()
        for j, chip in enumerate(chips):
            copy(j, rows.at[number[j]], rows.at[number[j]], chip).wait_recv()
        lhs = rows[...].reshape(N_CHIPS * SUBLANES, D_MODEL).astype(BF16)
        prods[...] = _dot(lhs, w_ref[...]).reshape(N_CHIPS, SUBLANES, n_own)
        got[me] = prods[me]
        out_prods = [copy(3 + j, prods.at[number[j]], got.at[me], chip) for j, chip in enumerate(chips)]
        for cp in out_prods:
            cp.start()
        for j, chip in enumerate(chips):
            copy(3 + j, got.at[number[j]], got.at[number[j]], chip).wait_recv()
        for cp in out_rows + out_prods:
            cp.wait_send()
        for k in range(N_CHIPS):
            cols = slice(k * n_own, (k + 1) * n_own)
            m_ref[:, cols] = got[k] + b_ref[:, cols]

    return pl.pallas_call(
        kern, name="ada_mod_sharded",
        out_shape=(jax.ShapeDtypeStruct((SUBLANES, 3 * D_MODEL), F32), jax.ShapeDtypeStruct((1, D_MODEL), F32)),
        scratch_shapes=[pltpu.VMEM((N_CHIPS, SUBLANES, D_MODEL), F32), pltpu.VMEM((N_CHIPS, SUBLANES, n_own), F32),
                        pltpu.VMEM((N_CHIPS, SUBLANES, n_own), F32), pltpu.SemaphoreType.DMA((6,)),
                        pltpu.SemaphoreType.DMA((6,))],
        compiler_params=_cparams())(c_row, w_own_bf, b_ada)


def _ag_weights_ride(packed, n_chunks=COMM_CHUNKS):
    rows, width = packed.shape
    half = rows // 2
    chunks = _row_chunks(half, n_chunks)
    nq = len(chunks)

    def parts(p_ref, out_ref, send_sems, recv_sems):
        x, y, c, chips = _position()
        sibling = (x, y, 1 - c)

        def copy(k, chip, h, q, to, src=None):
            start, size = chunks[q]
            rows_q = pl.ds(h * half + start, size)
            dst = out_ref.at[2 * chip[0] + chip[1], rows_q, :]
            return pltpu.make_async_remote_copy(
                src_ref=dst if src is None else src.at[rows_q, :], dst_ref=dst, send_sem=send_sems.at[k * nq + q],
                recv_sem=recv_sems.at[k * nq + q], device_id=to, device_id_type=MESH_ID)

        mine = [copy(6 + h, (x, y), h, q, sibling, src=p_ref) for h in range(2) for q in range(nq)]
        first = [copy(j, (x, y), c, q, (*chip, c), src=p_ref) for q in range(nq) for j, chip in enumerate(chips)]
        return (x, y, c), chips, sibling, copy, mine, first

    def start(ins, outs, sems):
        _, _, _, _, mine, first = parts(ins[0], outs[0], sems[0], sems[1])
        for cp in first + mine:
            cp.start()

    def wait(ins, outs, sems):
        (x, y, c), chips, sibling, copy, mine, first = parts(ins[0], outs[0], sems[0], sems[1])
        passed = []
        for q in range(nq):
            for j, chip in enumerate(chips):
                copy(j, chip, c, q, (x, y, c)).wait_recv()
                fwd = copy(3 + j, chip, c, q, sibling)
                fwd.start()
                passed.append(fwd)
        for q in range(nq):
            for j, chip in enumerate(chips):
                copy(3 + j, chip, 1 - c, q, (x, y, c)).wait_recv()
        for cp in mine:
            cp.wait_recv()
        for cp in first + passed + mine:
            cp.wait_send()

    return _Ride([packed], [jax.ShapeDtypeStruct((N_CHIPS, rows, width), packed.dtype)],
                 [pltpu.SemaphoreType.DMA((8 * nq,)), pltpu.SemaphoreType.DMA((8 * nq,))], start, wait)


def _join_rides(rides):
    def split(seq, counts):
        out, at = [], 0
        for n in counts:
            out.append(seq[at:at + n])
            at += n
        return out

    n_in = [len(r.inputs) for r in rides]
    n_out = [len(r.out_shapes) for r in rides]
    n_sem = [len(r.scratch) for r in rides]

    def start(ins, outs, sems):
        for r, i, o, s in zip(rides, split(ins, n_in), split(outs, n_out), split(sems, n_sem)):
            r.start(i, o, s)

    def wait(ins, outs, sems):
        for r, i, o, s in zip(rides, split(ins, n_in), split(outs, n_out), split(sems, n_sem)):
            r.wait(i, o, s)

    return _Ride([a for r in rides for a in r.inputs], [a for r in rides for a in r.out_shapes],
                 [a for r in rides for a in r.scratch], start, wait)


def _run_ride(ride, name):
    n_in, n_out = len(ride.inputs), len(ride.out_shapes)

    def body(*refs):
        ins, outs, sems = refs[:n_in], refs[n_in:n_in + n_out], refs[n_in + n_out:]
        ride.start(ins, outs, sems)
        ride.wait(ins, outs, sems)

    return pl.pallas_call(
        body, name=name, in_specs=[_ANY] * n_in, out_specs=(_ANY,) * n_out, out_shape=tuple(ride.out_shapes),
        scratch_shapes=list(ride.scratch))(*ride.inputs)


def _small_allgather_ride(buf):
    rows, width = buf.shape
    chunks = _row_chunks(rows, COMM_CHUNKS)
    nq = len(chunks)

    def parts(b_ref, all_ref, send_sems, recv_sems, local_sem):
        x, y, c, chips = _position()
        me, sibling = (x, y, c), (x, y, 1 - c)

        def copy(k, block, q, to, src=None):
            rows_q = pl.ds(chunks[q][0], chunks[q][1])
            dst = all_ref.at[4 * block[0] + 2 * block[1] + block[2], rows_q, :]
            return pltpu.make_async_remote_copy(
                src_ref=dst if src is None else src.at[rows_q, :], dst_ref=dst, send_sem=send_sems.at[k * nq + q],
                recv_sem=recv_sems.at[k * nq + q], device_id=to, device_id_type=MESH_ID)

        mine = pltpu.make_async_copy(b_ref, all_ref.at[4 * x + 2 * y + c], local_sem)
        first = []
        for q in range(nq):
            first += [copy(1 + j, me, q, (*chip, c), src=b_ref) for j, chip in enumerate(chips)]
            first.append(copy(0, me, q, sibling, src=b_ref))
        return me, sibling, c, chips, copy, mine, first

    def start(ins, outs, sems):
        _, _, _, _, _, mine, first = parts(ins[0], outs[0], *sems)
        mine.start()
        for cp in first:
            cp.start()

    def wait(ins, outs, sems):
        me, sibling, c, chips, copy, mine, first = parts(ins[0], outs[0], *sems)
        passed = []
        for q in range(nq):
            for j, chip in enumerate(chips):
                copy(1 + j, (*chip, c), q, me).wait_recv()
                fwd = copy(4 + j, (*chip, c), q, sibling)
                fwd.start()
                passed.append(fwd)
        for q in range(nq):
            copy(0, sibling, q, me).wait_recv()
            for j, chip in enumerate(chips):
                copy(4 + j, (*chip, 1 - c), q, me).wait_recv()
        for cp in first + passed:
            cp.wait_send()
        mine.wait()

    return _Ride([buf], [jax.ShapeDtypeStruct((N_DEV, rows, width), F32)],
                 [pltpu.SemaphoreType.DMA((7 * nq,)), pltpu.SemaphoreType.DMA((7 * nq,)), pltpu.SemaphoreType.DMA],
                 start, wait)


def _sum_devices(blocks):
    n, rows, width = blocks.shape
    rb = rows // 2 if (rows // 2) % SUBLANES == 0 else rows

    def kern(b_ref, o_ref):
        total = b_ref[0]
        for d in range(1, n):
            total = total + b_ref[d]
        o_ref[...] = total

    return pl.pallas_call(
        kern, name="small_sum", grid=(rows // rb,), in_specs=[pl.BlockSpec((n, rb, width), lambda i: (0, i, 0))],
        out_specs=pl.BlockSpec((rb, width), lambda i: (i, 0)), out_shape=jax.ShapeDtypeStruct((rows, width), F32),
        compiler_params=_cparams("parallel"))(blocks)


def _small_allgather_sum(buf, head_rows, n_chunks=COMM_CHUNKS):
    rows, width = buf.shape
    chunks = _row_chunks(rows, n_chunks)
    nq = len(chunks)

    def body(b_ref, head_ref, sum_ref, all_ref, send_sems, recv_sems, local_sem):
        x, y, c, chips = _position()
        me, sibling = (x, y, c), (x, y, 1 - c)

        def slot(px, py, pc):
            return all_ref.at[4 * px + 2 * py + pc]

        def copy(k, block, q, to, src=None):
            rows_q = pl.ds(chunks[q][0], chunks[q][1])
            dst = slot(*block).at[rows_q, :]
            return pltpu.make_async_remote_copy(
                src_ref=dst if src is None else src.at[rows_q, :], dst_ref=dst, send_sem=send_sems.at[k * nq + q],
                recv_sem=recv_sems.at[k * nq + q], device_id=to, device_id_type=MESH_ID)

        mine = pltpu.make_async_copy(b_ref, slot(*me), local_sem)
        mine.start()
        first = []
        for q in range(nq):
            first += [copy(1 + j, me, q, (*chip, c), src=b_ref) for j, chip in enumerate(chips)]
            first.append(copy(0, me, q, sibling, src=b_ref))
        for cp in first:
            cp.start()
        passed = []
        for q in range(nq):
            for j, chip in enumerate(chips):
                copy(1 + j, (*chip, c), q, me).wait_recv()
                fwd = copy(4 + j, (*chip, c), q, sibling)
                fwd.start()
                passed.append(fwd)
        for q in range(nq):
            copy(0, sibling, q, me).wait_recv()
            for j, chip in enumerate(chips):
                copy(4 + j, (*chip, 1 - c), q, me).wait_recv()
        for cp in first + passed:
            cp.wait_send()
        mine.wait()
        total = all_ref[0]
        for d in range(1, N_DEV):
            total = total + all_ref[d]
        sum_ref[...] = total
        head_ref[...] = all_ref[:, 0:head_rows, :]

    vm = pl.BlockSpec(memory_space=pltpu.VMEM)
    return pl.pallas_call(
        body, name="small_allgather_sum", in_specs=[vm], out_specs=(vm, vm),
        out_shape=(jax.ShapeDtypeStruct((N_DEV, head_rows, width), F32), jax.ShapeDtypeStruct((rows, width), F32)),
        scratch_shapes=[pltpu.VMEM((N_DEV, rows, width), F32), pltpu.SemaphoreType.DMA((7 * nq,)),
                        pltpu.SemaphoreType.DMA((7 * nq,)), pltpu.SemaphoreType.DMA],
        compiler_params=_cparams(),
    )(buf)


def _rs_pair(g):
    n, rows, width = g.shape
    half = rows // 2
    chunks = _row_chunks(half, COMM_CHUNKS)
    nq = len(chunks)

    def body(g_ref, got_ref, send_sems, recv_sems):
        x, y, c, _ = _position()
        swaps = []
        for k in range(n):
            for q, (start, size) in enumerate(chunks):
                swaps.append(pltpu.make_async_remote_copy(
                    src_ref=g_ref.at[k, pl.ds((1 - c) * half + start, size), :], dst_ref=got_ref.at[k, pl.ds(start, size), :],
                    send_sem=send_sems.at[k * nq + q], recv_sem=recv_sems.at[k * nq + q],
                    device_id=(x, y, 1 - c), device_id_type=MESH_ID))
        for cp in swaps:
            cp.start()
        for cp in swaps:
            cp.wait()

    return pl.pallas_call(
        body, name="rs_pair", in_specs=[_ANY], out_specs=_ANY, out_shape=jax.ShapeDtypeStruct((n, half, width), g.dtype),
        scratch_shapes=[pltpu.SemaphoreType.DMA((n * nq,)), pltpu.SemaphoreType.DMA((n * nq,))],
    )(g)


def _rs_chips_ride(part_bf):
    n, rows, width = part_bf.shape
    chunks = _row_chunks(rows, COMM_CHUNKS)
    nq = len(chunks)

    def sends(pb_ref, got_ref, send_sems, recv_sems):
        x, y, c, chips = _position()
        out = []
        for q, (start, size) in enumerate(chunks):
            for j, chip in enumerate(chips):
                out.append(pltpu.make_async_remote_copy(
                    src_ref=pb_ref.at[2 * chip[0] + chip[1], pl.ds(start, size), :], dst_ref=got_ref.at[j, pl.ds(start, size), :],
                    send_sem=send_sems.at[j * nq + q], recv_sem=recv_sems.at[j * nq + q],
                    device_id=(*chip, c), device_id_type=MESH_ID))
        return out

    def start(ins, outs, sems):
        for cp in sends(ins[0], outs[0], sems[0], sems[1]):
            cp.start()

    def wait(ins, outs, sems):
        for cp in sends(ins[0], outs[0], sems[0], sems[1]):
            cp.wait()

    return _Ride([part_bf], [jax.ShapeDtypeStruct((N_CHIPS - 1, rows, width), BF16)],
                 [pltpu.SemaphoreType.DMA((3 * nq,)), pltpu.SemaphoreType.DMA((3 * nq,))], start, wait)


def _rs_join(shard):
    rows, width = shard.shape
    half = rows // 2
    chunks = _row_chunks(half, COMM_CHUNKS)
    nq = len(chunks)

    def body(in_ref, out_ref, send_sems, recv_sems):
        x, y, c, _ = _position()
        def swap(q, h):
            rows_q = pl.ds(h * half + chunks[q][0], chunks[q][1])
            return pltpu.make_async_remote_copy(
                src_ref=in_ref.at[rows_q, :], dst_ref=out_ref.at[rows_q, :], send_sem=send_sems.at[q],
                recv_sem=recv_sems.at[q], device_id=(x, y, 1 - c), device_id_type=MESH_ID)

        for q in range(nq):
            swap(q, c).start()
        for q in range(nq):
            swap(q, 1 - c).wait_recv()
        for q in range(nq):
            swap(q, c).wait_send()

    return pl.pallas_call(
        body, name="rs_join", in_specs=[_ANY], out_specs=_ANY, input_output_aliases={0: 0},
        out_shape=jax.ShapeDtypeStruct(shard.shape, shard.dtype),
        scratch_shapes=[pltpu.SemaphoreType.DMA((nq,)), pltpu.SemaphoreType.DMA((nq,))],
    )(shard)


def _pair_add(g, got, core):
    n, half, width = got.shape
    nb = 2
    rb = half // nb

    def kern(c_ref, a_ref, b_ref, f_ref, h_ref):
        s = a_ref[...] + b_ref[...]
        f_ref[...] = s
        h_ref[...] = s.astype(BF16)

    spec = pl.BlockSpec((1, rb, width), lambda k, i, c_ref: (k, i, 0))
    return pl.pallas_call(
        kern, name="rs_pair_add",
        grid_spec=pltpu.PrefetchScalarGridSpec(
            num_scalar_prefetch=1, grid=(n, nb),
            in_specs=[pl.BlockSpec((1, rb, width), lambda k, i, c_ref: (k, c_ref[0] * nb + i, 0)), spec],
            out_specs=(spec, spec)),
        out_shape=(jax.ShapeDtypeStruct(got.shape, F32), jax.ShapeDtypeStruct(got.shape, BF16)),
        compiler_params=_cparams("parallel", "parallel"))(core, g, got)


def _chip_add(part_f32, got, where):
    _, rows, width = part_f32.shape
    nb = 2
    rb = rows // nb

    def kern(w_ref, a_ref, b_ref, o_ref):
        o_ref[...] = ((a_ref[0] + b_ref[0].astype(F32)) + b_ref[1].astype(F32)) + b_ref[2].astype(F32)

    return pl.pallas_call(
        kern, name="rs_chip_add",
        grid_spec=pltpu.PrefetchScalarGridSpec(
            num_scalar_prefetch=1, grid=(nb,),
            in_specs=[pl.BlockSpec((1, rb, width), lambda i, w_ref: (w_ref[0], i, 0)),
                      pl.BlockSpec((N_CHIPS - 1, rb, width), lambda i, w_ref: (0, i, 0))],
            out_specs=pl.BlockSpec((rb, width), lambda i, w_ref: (w_ref[1] * nb + i, 0))),
        out_shape=jax.ShapeDtypeStruct((2 * rows, width), F32),
        compiler_params=_cparams("parallel"))(where, part_f32, got)


def _adamw(w, g, m, v, name):
    rows, width = w.shape
    rb = rows
    for cand in (512, 256, 128, 64, 32, 16, 8):
        if rows % cand == 0 and cand * width * 4 <= ADAM_BLOCK_BYTES:
            rb = cand
            break
    spec = pl.BlockSpec((rb, width), lambda i: (i, 0))

    def kern(w_ref, g_ref, m_ref, v_ref, d_ref, nm_ref, nv_ref):
        d_ref[...], nm_ref[...], nv_ref[...] = _adamw_update(w_ref[...], g_ref[...], m_ref[...], v_ref[...])

    shp = jax.ShapeDtypeStruct(w.shape, F32)
    return pl.pallas_call(
        kern, name=name, grid=(rows // rb,), in_specs=[spec] * 4, out_specs=(spec, spec, spec),
        out_shape=(shp, shp, shp), compiler_params=_cparams("parallel"))(w, g, m, v)


def _adamw_update(w, g, m, v):
    nm = ADAM_B1 * m + (1.0 - ADAM_B1) * g
    nv = ADAM_B2 * v + (1.0 - ADAM_B2) * (g * g)
    m_hat = nm / (1.0 - ADAM_B1 ** ADAM_STEP)
    v_hat = nv / (1.0 - ADAM_B2 ** ADAM_STEP)
    return -ADAM_LR * (m_hat / (jnp.sqrt(v_hat) + ADAM_EPS) + ADAM_WD * w), nm, nv


def _adamw_small(params):
    n = len(params)

    def kern(*refs):
        ins, outs = refs[:4 * n], refs[4 * n:]
        for p in range(n):
            w_ref, g_ref, m_ref, v_ref = ins[4 * p:4 * p + 4]
            d, nm, nv = _adamw_update(w_ref[...], g_ref[...], m_ref[...], v_ref[...])
            outs[3 * p][...] = d
            outs[3 * p + 1][...] = nm
            outs[3 * p + 2][...] = nv

    flat = [a for group in params for a in group]
    shapes = [jax.ShapeDtypeStruct(group[0].shape, F32) for group in params for _ in range(3)]
    res = pl.pallas_call(kern, name="adamw_small", out_shape=tuple(shapes), compiler_params=_cparams())(*flat)
    return [tuple(res[3 * p:3 * p + 3]) for p in range(n)]


def _wada_grad(silu_t, dmod_cols):
    n = dmod_cols.shape[1]

    def kern(s_ref, d_ref, o_ref):
        acc = s_ref[:, 0:1] * d_ref[0:1, :]
        for b in range(1, N_DEV):
            acc = acc + s_ref[:, b:b + 1] * d_ref[b:b + 1, :]
        o_ref[...] = acc

    return pl.pallas_call(kern, name="wada_grad", out_shape=jax.ShapeDtypeStruct((D_MODEL, n), F32),
                          compiler_params=_cparams())(silu_t, dmod_cols)


def _rows(a, multiple):
    flat = a.reshape(-1)
    pad = (-flat.shape[0]) % (D_MODEL * multiple)
    if pad:
        flat = jnp.concatenate([flat, jnp.zeros((pad,), flat.dtype)])
    return flat.reshape(-1, D_MODEL)


def _part_rows(shape, multiple):
    return -(-int(np.prod(shape)) // (D_MODEL * multiple)) * multiple


def _pack_rows(parts, multiple, total_multiple=1):
    blocks = [_rows(p, multiple) for p in parts]
    pad = (-sum(b.shape[0] for b in blocks)) % total_multiple
    if pad:
        blocks.append(jnp.zeros((pad, D_MODEL), blocks[0].dtype))
    return jnp.concatenate(blocks, axis=0)


def _unpack_rows(buf, shapes, multiple):
    out, r = [], 0
    for shp in shapes:
        n = int(np.prod(shp))
        nr = _part_rows(shp, multiple)
        out.append(buf[r:r + nr].reshape(-1)[:n].reshape(shp))
        r += nr
    return out


def kernel(x, c, w_ada, b_ada, norm_pre, norm_post, w_in, pool_w, pool_scale, ssm_a_re, ssm_a_im, ssm_log_dt, ssm_b_re, ssm_b_im, ssm_c_re, ssm_c_im, ssm_d, glu_w, glu_b, w_branch_pool, w_branch_ssm, w_out, loss_target, m_w_ada, m_b_ada, m_norm_pre, m_norm_post, m_w_in, m_pool_w, m_pool_scale, m_ssm_a_re, m_ssm_a_im, m_ssm_log_dt, m_ssm_b_re, m_ssm_b_im, m_ssm_c_re, m_ssm_c_im, m_ssm_d, m_glu_w, m_glu_b, m_w_branch_pool, m_w_branch_ssm, m_w_out, v_w_ada, v_b_ada, v_norm_pre, v_norm_post, v_w_in, v_pool_w, v_pool_scale, v_ssm_a_re, v_ssm_a_im, v_ssm_log_dt, v_ssm_b_re, v_ssm_b_im, v_ssm_c_re, v_ssm_c_im, v_ssm_d, v_glu_w, v_glu_b, v_w_branch_pool, v_w_branch_ssm, v_w_out):
    n_ada = w_ada.shape[2]
    n_in = w_in.shape[2]
    n_row = glu_w.shape[1]
    n_pool = pool_w.shape[2]
    n_groups = pool_w.shape[1]

    w_ada_own = w_ada[0].astype(BF16)
    w_in_own = w_in[0].astype(BF16)
    w_in_ride = _ag_weights_ride(w_in_own)

    def unpack_w_in(g_in):
        return g_in.transpose(1, 0, 2).reshape(D_MODEL, N_CHIPS * n_in)
    pool_rows = n_groups * n_pool * POOL_GW // D_MODEL
    late_shards = [pool_w[0].reshape(n_groups * n_pool, POOL_GW), glu_w[0], w_branch_pool[0], w_branch_ssm[0], w_out[0]]
    late_ride = _join_rides([_ag_weights_ride(s.astype(BF16), n_chunks=2) for s in late_shards])

    def unpack_late(pool, *squares):
        pool = pool.reshape(N_CHIPS, n_groups, n_pool, POOL_GW).transpose(1, 0, 2, 3)
        return (pool.reshape(n_groups, POOL_GW, POOL_GW), *[s.reshape(D_MODEL, D_MODEL) for s in squares])

    chip = 2 * lax.axis_index("x") + lax.axis_index("y")
    core = lax.axis_index("c").astype(jnp.int32)
    kept = {}

    half_d = D_MODEL // 2
    assert n_in == D_MODEL + half_d, n_in

    def by_cols(blk):
        rest = blk[:, :, D_MODEL:]
        return jnp.concatenate(
            [blk[:, :, :D_MODEL], jnp.concatenate([rest[:, :half_d], rest[:, half_d:]], axis=2)], axis=1)

    def from_cols(packed):
        rest = packed[D_MODEL:]
        return jnp.concatenate(
            [packed[:D_MODEL], jnp.concatenate([rest[:, :half_d], rest[:, half_d:]], axis=0)], axis=1)

    def by_rows(a):
        return a.reshape(N_CHIPS, n_row, D_MODEL)

    def exchange_big(g):
        pool_by_chip = g["d_pool_w"].reshape(n_groups, N_CHIPS, n_pool, POOL_GW).transpose(1, 0, 2, 3)
        blocks = [by_cols(g["d_win"]), by_rows(g["d_glu_w"]), by_rows(g["d_wbp"]), by_rows(g["d_wbs"]),
                  by_rows(g["d_wout"]), pool_by_chip.reshape(N_CHIPS, pool_rows, D_MODEL)]
        pad = (-sum(b.shape[1] for b in blocks)) % (2 * COMM_CHUNKS * COMM_ROW_ALIGN)
        if pad:
            blocks.append(jnp.zeros((N_CHIPS, pad, D_MODEL), F32))
        g_packed = jnp.concatenate(blocks, axis=1)
        kept["part_f32"], part_bf = _pair_add(g_packed, _rs_pair(g_packed), core.reshape(1))
        return _rs_chips_ride(part_bf)

    a_re, a_im, log_dt = ssm_a_re[0], ssm_a_im[0], ssm_log_dt[0].reshape(SSM_G, 1)
    b_re_t, b_im_t = ssm_b_re[0].transpose(2, 0, 1), ssm_b_im[0].transpose(2, 0, 1)
    early_names = ["dg2", "d_pscale", "d_glu_b", "d_dskip", "d_abar_re", "d_abar_im", "d_bb_re_t", "d_bb_im_t",
                   "d_c_re", "d_c_im"]

    def exchange_small(s):
        parts = [s[k] for k in early_names]
        kept["early_shapes"] = [p.shape for p in parts]
        return _small_allgather_ride(_pack_rows(parts, SUBLANES, COMM_CHUNKS * COMM_ROW_ALIGN))

    res = _local_step(x[0], c, loss_target[0], None, b_ada, norm_pre, norm_post, None, None, pool_scale,
                      a_re, a_im, log_dt, b_re_t, b_im_t, ssm_c_re[0], ssm_c_im[0], ssm_d[0], None, glu_b[0:1],
                      None, None, None,
                      split_proj=(w_in_own, chip.astype(jnp.int32).reshape(1), w_in_ride, unpack_w_in, late_ride, unpack_late),
                      ride_for_dw_in=exchange_small, ride_for_dh=exchange_big,
                      mod_fn=lambda c_row, bias: _mod_sharded(c_row, w_ada_own, bias))

    (all_early,) = res["rode_dw_in"]
    (g_norm_post, g_pscale, g_glu_b, g_dskip, s_abar_re, s_abar_im, s_bb_re, s_bb_im, g_c_re, g_c_im) = _unpack_rows(
        _sum_devices(all_early), kept["early_shapes"], SUBLANES)
    g_a_re, g_a_im, g_log_dt, g_b_re_t, g_b_im_t = _ssm_params_bwd(
        a_re, a_im, log_dt, b_re_t, b_im_t, s_abar_re.reshape(SSM_G, SSM_P), s_abar_im.reshape(SSM_G, SSM_P),
        s_bb_re, s_bb_im)
    late_parts = [res["dmod"], res["silu_c"], res["dg1"], res["loss"].reshape(1, 1)]
    late_shapes = [p.shape for p in late_parts]
    head_rows = _part_rows(late_shapes[0], SUBLANES) + _part_rows(late_shapes[1], SUBLANES)
    all_late, sum_late = _small_allgather_sum(_pack_rows(late_parts, SUBLANES, COMM_ROW_ALIGN), head_rows, n_chunks=1)
    g_b_ada, _, g_norm_pre, loss = _unpack_rows(sum_late, late_shapes, SUBLANES)
    loss = loss[0, 0]
    dmod_all = all_late[:, 0:3].reshape(N_DEV, 3 * D_MODEL)
    dmod_cols = lax.dynamic_slice_in_dim(dmod_all, chip * n_ada, n_ada, axis=1)
    silu_t = all_late[:, _part_rows(late_shapes[0], SUBLANES)].transpose(1, 0)
    g_w_ada = _wada_grad(silu_t, dmod_cols)

    (got_chips,) = res["rode"]
    shard = _rs_join(_chip_add(kept["part_f32"], got_chips, jnp.stack([chip.astype(jnp.int32), core])))
    r = 0
    g_w_in = from_cols(shard[r:r + n_in])
    r += n_in
    g_squares = []
    for _ in range(4):
        g_squares.append(shard[r:r + n_row])
        r += n_row
    g_glu_w, g_wbp, g_wbs, g_wout = g_squares
    g_pool_w = shard[r:r + pool_rows].reshape(n_groups * n_pool, POOL_GW)

    big = [("w_ada", w_ada[0], g_w_ada, m_w_ada[0], v_w_ada[0]),
           ("w_in", w_in[0], g_w_in, m_w_in[0], v_w_in[0]),
           ("pool_w", pool_w[0].reshape(n_groups * n_pool, POOL_GW), g_pool_w,
            m_pool_w[0].reshape(n_groups * n_pool, POOL_GW), v_pool_w[0].reshape(n_groups * n_pool, POOL_GW)),
           ("glu_w", glu_w[0], g_glu_w, m_glu_w[0], v_glu_w[0]),
           ("w_branch_pool", w_branch_pool[0], g_wbp, m_w_branch_pool[0], v_w_branch_pool[0]),
           ("w_branch_ssm", w_branch_ssm[0], g_wbs, m_w_branch_ssm[0], v_w_branch_ssm[0]),
           ("w_out", w_out[0], g_wout, m_w_out[0], v_w_out[0])]
    out = {}
    for name, w_, g_, m_, v_ in big:
        d_, nm_, nv_ = _adamw(w_, g_, m_, v_, "adamw_" + name)
        out[name] = (g_, d_, nm_, nv_)

    g_b_re = g_b_re_t.transpose(1, 2, 0)
    g_b_im = g_b_im_t.transpose(1, 2, 0)
    small = [("b_ada", b_ada, g_b_ada, m_b_ada, v_b_ada),
             ("norm_pre", norm_pre, g_norm_pre, m_norm_pre, v_norm_pre),
             ("norm_post", norm_post, g_norm_post, m_norm_post, v_norm_post),
             ("pool_scale", pool_scale, g_pscale, m_pool_scale, v_pool_scale),
             ("ssm_a_re", ssm_a_re, g_a_re, m_ssm_a_re, v_ssm_a_re),
             ("ssm_a_im", ssm_a_im, g_a_im, m_ssm_a_im, v_ssm_a_im),
             ("ssm_log_dt", ssm_log_dt, g_log_dt, m_ssm_log_dt, v_ssm_log_dt),
             ("ssm_b_re", ssm_b_re, g_b_re, m_ssm_b_re, v_ssm_b_re),
             ("ssm_b_im", ssm_b_im, g_b_im, m_ssm_b_im, v_ssm_b_im),
             ("ssm_c_re", ssm_c_re, g_c_re, m_ssm_c_re, v_ssm_c_re),
             ("ssm_c_im", ssm_c_im, g_c_im, m_ssm_c_im, v_ssm_c_im),
             ("ssm_d", ssm_d, g_dskip, m_ssm_d, v_ssm_d),
             ("glu_b", glu_b, g_glu_b, m_glu_b, v_glu_b)]
    small = [(name, w_, g_.reshape(w_.shape), m_, v_) for name, w_, g_, m_, v_ in small]
    updates = _adamw_small([t[1:] for t in small])
    for (name, _, g_, _, _), (d_, nm_, nv_) in zip(small, updates):
        out[name] = (g_, d_, nm_, nv_)

    order = ["w_ada", "b_ada", "norm_pre", "norm_post", "w_in", "pool_w", "pool_scale", "ssm_a_re", "ssm_a_im",
             "ssm_log_dt", "ssm_b_re", "ssm_b_im", "ssm_c_re", "ssm_c_im", "ssm_d", "glu_w", "glu_b", "w_branch_pool",
             "w_branch_ssm", "w_out"]
    ref_shape = dict(w_ada=w_ada.shape, w_in=w_in.shape, pool_w=pool_w.shape, glu_w=glu_w.shape,
                     w_branch_pool=w_branch_pool.shape, w_branch_ssm=w_branch_ssm.shape, w_out=w_out.shape)
    for name, w_, _, _, _ in small:
        ref_shape[name] = w_.shape
    results = [loss, res["grad_x"][None]]
    for k in range(4):
        results += [out[name][k].reshape(ref_shape[name]) for name in order]
    return tuple(results)
```

```python
import functools
import math

import numpy as np
import jax
import jax.numpy as jnp
from jax import lax
from jax.experimental import pallas as pl
from jax.experimental.pallas import tpu as pltpu

F32 = jnp.float32
BF16 = jnp.bfloat16
MESH_ID = pl.DeviceIdType.MESH

D_MODEL = 1024
LANES = 128
SUBLANES = 8
SSM_G, SSM_P, SSM_H = 64, 64, 16
LANE_BLOCKS = D_MODEL // LANES
GROUPS_PER_BLOCK = LANES // SSM_H
STATE_W = GROUPS_PER_BLOCK * SSM_P
STATE_ALL = SSM_G * SSM_P
POOL_WINDOWS = (2, 4, 8, 16)
POOL_GW = D_MODEL // len(POOL_WINDOWS)
HALO = 16
RMS_EPS = 1e-6
N_CHIPS = 4
N_DEV = 8

SCAN_CHUNK = 1024
SCAN_BLOCKS = 1
ROW_CHUNK = 512
ROW_CHUNK_WIDE = 512
PROJ_ROWS = 1024
VMEM_LIMIT_BYTES = 56 * 1024 * 1024

ADAM_BLOCK_BYTES = 1 << 20
ADAM_LR, ADAM_B1, ADAM_B2, ADAM_EPS, ADAM_WD, ADAM_STEP = 0.001, 0.9, 0.999, 1e-08, 0.01, 10

_GELU_C0 = math.sqrt(2.0 / math.pi)
_GELU_C1 = 0.044715


def _cparams(*sem):
    if sem:
        return pltpu.CompilerParams(dimension_semantics=sem, vmem_limit_bytes=VMEM_LIMIT_BYTES)
    return pltpu.CompilerParams(vmem_limit_bytes=VMEM_LIMIT_BYTES)


def _sigmoid(v):
    return jax.nn.sigmoid(v)


def _silu(v):
    return v * _sigmoid(v)


def _dsilu(v):
    s = _sigmoid(v)
    return s * (1.0 + v * (1.0 - s))


def _gelu(v):
    return v * (0.5 * (1.0 + jnp.tanh(_GELU_C0 * v * (1.0 + _GELU_C1 * (v * v)))))


def _gelu_and_grad(v):
    v2 = v * v
    t = jnp.tanh(_GELU_C0 * v * (1.0 + _GELU_C1 * v2))
    half = 0.5 * (1.0 + t)
    grad = half + (0.5 * _GELU_C0) * v * (1.0 - t * t) * (1.0 + (3.0 * _GELU_C1) * v2)
    return v * half, grad


def _silu_and_grad(v):
    s = _sigmoid(v)
    return v * s, s * (1.0 + v * (1.0 - s))


def _dot(a, b):
    return lax.dot_general(a, b, (((1,), (0,)), ((), ())), preferred_element_type=F32)


def _dot_nt(a, b):
    return lax.dot_general(a, b, (((1,), (1,)), ((), ())), preferred_element_type=F32)


def _dot_tn(a, b):
    return lax.dot_general(a, b, (((0,), (0,)), ((), ())), preferred_element_type=F32)


def _acc8(v):
    return v.reshape(v.shape[0] // SUBLANES, SUBLANES, v.shape[1]).sum(axis=0)


class _Ride:
    def __init__(self, inputs, out_shapes, scratch, start, wait):
        self.inputs, self.out_shapes, self.scratch, self.start, self.wait = inputs, out_shapes, scratch, start, wait


def _mm(a_parts, b_parts, *, name, ta=False, tb=False, out_dtype=F32, bm=512, bn=512, bk=512, ride=None,
        blocks_first=False):
    a_parts, b_parts = list(a_parts), list(b_parts)
    if ta:
        assert len(a_parts) == 1
        k_dim, m_dim = a_parts[0].shape
    else:
        m_dim = a_parts[0].shape[0]
        k_dim = sum(a.shape[1] for a in a_parts)
    if tb:
        assert len(b_parts) == 1
        n_dim = b_parts[0].shape[0]
    else:
        n_dim = sum(b.shape[1] for b in b_parts)
    bm, bn, bk = min(bm, m_dim), min(bn, n_dim), min(bk, k_dim)
    nm, nn, nk = m_dim // bm, n_dim // bn, k_dim // bk
    a_ranges, off = [], 0
    for a in a_parts:
        cnt = (a.shape[0] if ta else a.shape[1]) // bk
        a_ranges.append((off, cnt))
        off += cnt
    b_ranges, off = [], 0
    for b in b_parts:
        cnt = (b.shape[0] if tb else b.shape[1]) // bn
        b_ranges.append((off, cnt))
        off += cnt

    def a_spec(off, cnt):
        if ta:
            return pl.BlockSpec((bk, bm), lambda i, n, k: (k, i))
        return pl.BlockSpec((bm, bk), lambda i, n, k: (i, jnp.clip(k - off, 0, cnt - 1)))

    def b_spec(off, cnt):
        if tb:
            return pl.BlockSpec((bn, bk), lambda i, n, k: (n, k))
        return pl.BlockSpec((bk, bn), lambda i, n, k: (k, jnp.clip(n - off, 0, cnt - 1)))

    na, nb = len(a_parts), len(b_parts)
    dims = (((0 if ta else 1,), (1 if tb else 0,)), ((), ()))

    def kern_single(a_ref, b_ref, o_ref):
        o_ref[...] = lax.dot_general(a_ref[...].astype(BF16), b_ref[...].astype(BF16), dims,
                                     preferred_element_type=F32).astype(out_dtype)

    if na == 1 and nb == 1 and nk == 1 and not ride:
        assert not blocks_first
        return pl.pallas_call(
            kern_single, name=name, grid=(nm, nn),
            in_specs=[pl.BlockSpec((bk, bm), lambda i, n: (0, i)) if ta else pl.BlockSpec((bm, bk), lambda i, n: (i, 0)),
                      pl.BlockSpec((bn, bk), lambda i, n: (n, 0)) if tb else pl.BlockSpec((bk, bn), lambda i, n: (0, n))],
            out_specs=pl.BlockSpec((bm, bn), lambda i, n: (i, n)),
            out_shape=jax.ShapeDtypeStruct((m_dim, n_dim), out_dtype),
            compiler_params=_cparams("parallel", "parallel"),
        )(a_parts[0], b_parts[0])

    n_rin = len(ride.inputs) if ride else 0
    n_rout = len(ride.out_shapes) if ride else 0

    def kern(*refs):
        a_refs, b_refs = refs[:na], refs[na:na + nb]
        rin = refs[na + nb:na + nb + n_rin]
        o_ref = refs[na + nb + n_rin]
        rout = refs[na + nb + n_rin + 1:na + nb + n_rin + 1 + n_rout]
        acc = refs[na + nb + n_rin + 1 + n_rout]
        rsem = refs[na + nb + n_rin + 2 + n_rout:]
        i, n, k = pl.program_id(0), pl.program_id(1), pl.program_id(2)

        if ride:
            @pl.when((i == 0) & (n == 0) & (k == 0))
            def _():
                ride.start(rin, rout, rsem)

        if nk > 1:
            @pl.when(k == 0)
            def _():
                acc[...] = jnp.zeros_like(acc)

        for ja, (koff, kcnt) in enumerate(a_ranges):
            for jb, (noff, ncnt) in enumerate(b_ranges):
                def step(ja=ja, jb=jb):
                    a = a_refs[ja][...].astype(BF16)
                    b = b_refs[jb][...].astype(BF16)
                    prod = lax.dot_general(a, b, dims, preferred_element_type=F32)
                    if nk > 1:
                        acc[...] += prod
                    else:
                        o_ref[...] = prod.astype(out_dtype)

                if na == 1 and nb == 1:
                    step()
                else:
                    cond = (k >= koff) & (k < koff + kcnt) & (n >= noff) & (n < noff + ncnt)
                    pl.when(cond)(step)

        if nk > 1:
            @pl.when(k == nk - 1)
            def _():
                o_ref[...] = acc[...].astype(out_dtype)

        if ride:
            @pl.when((i == nm - 1) & (n == nn - 1) & (k == nk - 1))
            def _():
                ride.wait(rin, rout, rsem)

    any_spec = pl.BlockSpec(memory_space=pl.ANY)
    if blocks_first:
        out_spec = pl.BlockSpec((None, bm, bn), lambda i, n, k: (n, i, 0))
        out_shape = jax.ShapeDtypeStruct((nn, m_dim, bn), out_dtype)
    else:
        out_spec = pl.BlockSpec((bm, bn), lambda i, n, k: (i, n))
        out_shape = jax.ShapeDtypeStruct((m_dim, n_dim), out_dtype)
    acc_shape = pltpu.VMEM((bm, bn) if nk > 1 else (SUBLANES, LANES), F32)
    if not ride:
        return pl.pallas_call(
            kern, name=name, grid=(nm, nn, nk),
            in_specs=[a_spec(*r) for r in a_ranges] + [b_spec(*r) for r in b_ranges],
            out_specs=out_spec, out_shape=out_shape, scratch_shapes=[acc_shape],
            compiler_params=_cparams("parallel", "parallel", "arbitrary"),
        )(*a_parts, *b_parts)
    return pl.pallas_call(
        kern, name=name, grid=(nm, nn, nk),
        in_specs=[a_spec(*r) for r in a_ranges] + [b_spec(*r) for r in b_ranges] + [any_spec] * n_rin,
        out_specs=(out_spec,) + (any_spec,) * n_rout, out_shape=(out_shape,) + tuple(ride.out_shapes),
        scratch_shapes=[acc_shape] + list(ride.scratch),
        compiler_params=_cparams("arbitrary", "arbitrary", "arbitrary"),
    )(*a_parts, *b_parts, *ride.inputs)


def _ssm_param_fn(a_re, a_im, log_dt, b_re, b_im):
    dt = jnp.exp(log_dt)
    lam_re = jnp.minimum(a_re, -1e-4)
    lam_im = a_im
    mag = jnp.exp(lam_re * dt)
    abar_re = mag * jnp.cos(lam_im * dt)
    abar_im = mag * jnp.sin(lam_im * dt)
    den = lam_re * lam_re + lam_im * lam_im
    num_re = abar_re - 1.0
    f_re = (num_re * lam_re + abar_im * lam_im) / den
    f_im = (abar_im * lam_re - num_re * lam_im) / den
    bb_re = f_re * b_re - f_im * b_im
    bb_im = f_re * b_im + f_im * b_re
    return abar_re, abar_im, bb_re, bb_im


def _ssm_params(a_re, a_im, log_dt, b_re_t, b_im_t):
    def kern(are, aim, ldt, bre, bim, o_ar, o_ai, o_br, o_bi):
        ar, ai, br, bi = _ssm_param_fn(are[...], aim[...], ldt[...], bre[...], bim[...])
        o_ar[...] = ar
        o_ai[...] = ai
        o_br[...] = br
        o_bi[...] = bi

    gp = jax.ShapeDtypeStruct((SSM_G, SSM_P), F32)
    hgp = jax.ShapeDtypeStruct((SSM_H, SSM_G, SSM_P), F32)
    return pl.pallas_call(kern, name="ssm_params", out_shape=(gp, gp, hgp, hgp), compiler_params=_cparams())(
        a_re, a_im, log_dt, b_re_t, b_im_t)


def _ssm_params_bwd(a_re, a_im, log_dt, b_re_t, b_im_t, d_ar, d_ai, d_bbr, d_bbi):
    def kern(are, aim, ldt, bre, bim, dar, dai, dbr, dbi, o_are, o_aim, o_ldt, o_bre, o_bim):
        prim = (are[...], aim[...], ldt[...], bre[...], bim[...])
        _, vjp = jax.vjp(_ssm_param_fn, *prim)
        g = vjp((dar[...], dai[...], dbr[...], dbi[...]))
        o_are[...] = g[0]
        o_aim[...] = g[1]
        o_ldt[...] = g[2]
        o_bre[...] = g[3]
        o_bim[...] = g[4]

    gp = jax.ShapeDtypeStruct((SSM_G, SSM_P), F32)
    g1 = jax.ShapeDtypeStruct((SSM_G, 1), F32)
    hgp = jax.ShapeDtypeStruct((SSM_H, SSM_G, SSM_P), F32)
    return pl.pallas_call(kern, name="ssm_params_bwd", out_shape=(gp, gp, g1, hgp, hgp), compiler_params=_cparams())(
        a_re, a_im, log_dt, b_re_t, b_im_t, d_ar, d_ai, d_bbr, d_bbi)


def _pow_tables(abar_re, abar_im, tc):
    ls = tc // SUBLANES

    def kern(ar_ref, ai_ref, fr_ref, fi_ref, rr_ref, ri_ref):
        a_re = jnp.broadcast_to(ar_ref[...], (SUBLANES, STATE_W))
        a_im = jnp.broadcast_to(ai_ref[...], (SUBLANES, STATE_W))
        p_re, p_im = a_re, a_im
        for i in range(ls):
            fwd = pl.ds(SUBLANES * i, SUBLANES)
            rev = pl.ds(SUBLANES * (ls - 1 - i), SUBLANES)
            fr_ref[fwd, :] = p_re
            fi_ref[fwd, :] = p_im
            rr_ref[rev, :] = p_re
            ri_ref[rev, :] = p_im
            p_re, p_im = p_re * a_re - p_im * a_im, p_re * a_im + p_im * a_re

    vec = pl.BlockSpec((1, STATE_W), lambda b: (0, b))
    tab = pl.BlockSpec((tc, STATE_W), lambda b: (0, b))
    shp = jax.ShapeDtypeStruct((tc, STATE_ALL), F32)
    return pl.pallas_call(
        kern, name="pow_tables", grid=(LANE_BLOCKS,), in_specs=[vec, vec], out_specs=(tab, tab, tab, tab),
        out_shape=(shp, shp, shp, shp), compiler_params=_cparams("parallel"))(abar_re, abar_im)


def _mod_kernel(c_row, w_ada_bf, b_ada):
    def kern(c_ref, w_ref, b_ref, m_ref, s_ref):
        cv = c_ref[...]
        sc = _silu(cv)
        s_ref[...] = sc
        lhs = jnp.broadcast_to(sc, (SUBLANES, D_MODEL)).astype(BF16)
        m_ref[...] = _dot(lhs, w_ref[...]) + b_ref[...]

    return pl.pallas_call(
        kern, name="ada_mod",
        out_shape=(jax.ShapeDtypeStruct((SUBLANES, 3 * D_MODEL), F32), jax.ShapeDtypeStruct((1, D_MODEL), F32)),
        compiler_params=_cparams())(c_row, w_ada_bf, b_ada)


def _row_spec(tr, width=D_MODEL, col=0):
    return pl.BlockSpec((tr, width), lambda c: (c, col))


def _vec_spec(width=D_MODEL):
    return pl.BlockSpec((1, width), lambda c: (0, 0))


def _col_spec(tr):
    return pl.BlockSpec((D_MODEL, tr), lambda c: (0, c))


def _in_norm(x, g1, scale, shift):
    seq = x.shape[0]
    tr = min(ROW_CHUNK_WIDE, seq)

    def kern(x_ref, g_ref, sc_ref, sh_ref, h_ref, ht_ref):
        xv = x_ref[...]
        r = lax.rsqrt(jnp.mean(xv * xv, axis=-1, keepdims=True) + RMS_EPS)
        h = ((xv * r) * g_ref[...]) * (1.0 + sc_ref[...]) + sh_ref[...]
        h_ref[...] = h.astype(BF16)
        ht_ref[...] = h.T.astype(BF16)

    return pl.pallas_call(
        kern, name="in_norm", grid=(seq // tr,),
        in_specs=[_row_spec(tr), _vec_spec(), _vec_spec(), _vec_spec()], out_specs=(_row_spec(tr), _col_spec(tr)),
        out_shape=(jax.ShapeDtypeStruct((seq, D_MODEL), BF16), jax.ShapeDtypeStruct((D_MODEL, seq), BF16)),
        compiler_params=_cparams("parallel"))(x, g1, scale, shift)


PAD = SUBLANES


def _window_sums(src, cols, w, bufs, rows, ahead):
    cur, cur_cols, step, k = src, cols, 1, 0
    data = pl.ds(PAD, rows)
    while step < w:
        dst = bufs[k % 2]
        dst[data, :] = cur[data, cur_cols] + cur[pl.ds(PAD + (step if ahead else -step), rows), cur_cols]
        cur, cur_cols, step, k = dst, slice(None), 2 * step, k + 1
    return cur, cur_cols


def _in_norm_proj_own(x, g1, scale, shift, w_own, chip, ride):
    seq, n_own = x.shape[0], w_own.shape[1]
    tr = min(PROJ_ROWS, seq)
    nc = seq // tr
    n_rin, n_rout = len(ride.inputs), len(ride.out_shapes)

    def kern(chip_ref, x_ref, g_ref, sc_ref, sh_ref, w_ref, *rest):
        rin, (h_ref, ht_ref, p_ref) = rest[:n_rin], rest[n_rin:n_rin + 3]
        rout, rsem = rest[n_rin + 3:n_rin + 3 + n_rout], rest[n_rin + 3 + n_rout:]
        c = pl.program_id(0)

        @pl.when(c == 0)
        def _():
            ride.start(rin, rout, rsem)

        xv = x_ref[...]
        r = lax.rsqrt(jnp.mean(xv * xv, axis=-1, keepdims=True) + RMS_EPS)
        h = ((xv * r) * g_ref[...]) * (1.0 + sc_ref[...]) + sh_ref[...]
        hb = h.astype(BF16)
        h_ref[...] = hb
        ht_ref[...] = h.T.astype(BF16)
        p_ref[...] = _dot(hb, w_ref[...]).astype(BF16)

        @pl.when(c == nc - 1)
        def _():
            ride.wait(rin, rout, rsem)

    vec = pl.BlockSpec((1, D_MODEL), lambda c, k: (0, 0))
    return pl.pallas_call(
        kern, name="in_norm_proj_own",
        grid_spec=pltpu.PrefetchScalarGridSpec(
            num_scalar_prefetch=1, grid=(nc,),
            in_specs=[pl.BlockSpec((tr, D_MODEL), lambda c, k: (c, 0)), vec, vec, vec,
                      pl.BlockSpec((D_MODEL, n_own), lambda c, k: (0, 0))] + [_ANY] * n_rin,
            out_specs=(pl.BlockSpec((tr, D_MODEL), lambda c, k: (c, 0)), pl.BlockSpec((D_MODEL, tr), lambda c, k: (0, c)),
                       pl.BlockSpec((tr, n_own), lambda c, k: (c, k[0]))) + (_ANY,) * n_rout,
            scratch_shapes=list(ride.scratch)),
        out_shape=(jax.ShapeDtypeStruct((seq, D_MODEL), BF16), jax.ShapeDtypeStruct((D_MODEL, seq), BF16),
                   jax.ShapeDtypeStruct((seq, N_CHIPS * n_own), BF16)) + tuple(ride.out_shapes),
        compiler_params=_cparams("arbitrary"))(chip, x, g1, scale, shift, w_own, *ride.inputs)


def _proj_rest(h, w_blocks, proj, chip, ride):
    seq, n_own = h.shape[0], w_blocks.shape[2]
    tr = min(PROJ_ROWS, seq)
    nm, nn = seq // tr, N_CHIPS - 1
    n_rin, n_rout = len(ride.inputs), len(ride.out_shapes)

    def kern(chip_ref, h_ref, w_ref, _, *rest):
        rin, p_ref = rest[:n_rin], rest[n_rin]
        rout, rsem = rest[n_rin + 1:n_rin + 1 + n_rout], rest[n_rin + 1 + n_rout:]
        i, n = pl.program_id(0), pl.program_id(1)

        @pl.when((i == 0) & (n == 0))
        def _():
            ride.start(rin, rout, rsem)

        p_ref[...] = _dot(h_ref[...], w_ref[0]).astype(BF16)

        @pl.when((i == nm - 1) & (n == nn - 1))
        def _():
            ride.wait(rin, rout, rsem)

    def other(n, k):
        return (k[0] + 1 + n) % N_CHIPS

    return pl.pallas_call(
        kern, name="proj_rest",
        grid_spec=pltpu.PrefetchScalarGridSpec(
            num_scalar_prefetch=1, grid=(nm, nn),
            in_specs=[pl.BlockSpec((tr, D_MODEL), lambda i, n, k: (i, 0)),
                      pl.BlockSpec((1, D_MODEL, n_own), lambda i, n, k: (other(n, k), 0, 0)), _ANY] + [_ANY] * n_rin,
            out_specs=(pl.BlockSpec((tr, n_own), lambda i, n, k: (i, other(n, k))),) + (_ANY,) * n_rout,
            scratch_shapes=list(ride.scratch)),
        out_shape=(jax.ShapeDtypeStruct(proj.shape, BF16),) + tuple(ride.out_shapes),
        input_output_aliases={3: 0},
        compiler_params=_cparams("arbitrary", "arbitrary"))(chip, h, w_blocks, proj, *ride.inputs)


def _pool_windows(ext, bufs, pos, g, w, tr):
    cols = pl.ds(g * POOL_GW, POOL_GW)
    chunk = pl.ds(PAD + HALO, tr)
    cur = ext[chunk, cols]
    win, win_cols = _window_sums(ext, cols, w, bufs, HALO + tr, ahead=False)
    cnt = jnp.minimum(pos + 1, w).astype(F32)
    return win[chunk, win_cols] / cnt - cur


def _zero_pads(refs, rows):
    for ref in refs:
        ref[0:PAD, :] = jnp.zeros((PAD, ref.shape[1]), F32)
        ref[PAD + rows:, :] = jnp.zeros((PAD, ref.shape[1]), F32)


def _pool_fwd(proj, pool_w_bf, pscale):
    seq = proj.shape[0]
    tr = min(ROW_CHUNK_WIDE, seq)
    hb = tr // HALO

    def kern(up_ref, halo_ref, zp_ref, pw_ref, ps_ref, y_ref, yt_ref, ext, buf_a, buf_b):
        c = pl.program_id(0)
        _zero_pads((ext, buf_a, buf_b), HALO + tr)
        ext[pl.ds(PAD, HALO), :] = jnp.where(c > 0, halo_ref[...].astype(F32), 0.0)
        ext[pl.ds(PAD + HALO, tr), :] = up_ref[...].astype(F32)
        pos = c * tr + lax.broadcasted_iota(jnp.int32, (tr, POOL_GW), 0)
        for g, w in enumerate(POOL_WINDOWS):
            cols = pl.ds(g * POOL_GW, POOL_GW)
            pooled = _pool_windows(ext, (buf_a, buf_b), pos, g, w, tr)
            mixed = _dot(pooled.astype(BF16), pw_ref[g])
            y = mixed * ps_ref[:, cols] * _silu(zp_ref[:, cols].astype(F32))
            y_ref[:, cols] = y.astype(BF16)
            yt_ref[cols, :] = y.T.astype(BF16)

    return pl.pallas_call(
        kern, name="pool_fwd", grid=(seq // tr,),
        in_specs=[_row_spec(tr, col=0),
                  pl.BlockSpec((HALO, D_MODEL), lambda c: (jnp.maximum(c * hb - 1, 0), 0)),
                  _row_spec(tr, col=1),
                  pl.BlockSpec((len(POOL_WINDOWS), POOL_GW, POOL_GW), lambda c: (0, 0, 0)),
                  _vec_spec()],
        out_specs=(_row_spec(tr), _col_spec(tr)),
        out_shape=(jax.ShapeDtypeStruct((seq, D_MODEL), BF16), jax.ShapeDtypeStruct((D_MODEL, seq), BF16)),
        scratch_shapes=[pltpu.VMEM((tr + HALO + 2 * PAD, D_MODEL), F32), pltpu.VMEM((tr + HALO + 2 * PAD, POOL_GW), F32),
                        pltpu.VMEM((tr + HALO + 2 * PAD, POOL_GW), F32)],
        compiler_params=_cparams("parallel"))(proj, proj, proj, pool_w_bf, pscale)


def _pool_bwd(proj, dyp, pool_w_bf, pscale, dproj):
    seq = proj.shape[0]
    tr = min(ROW_CHUNK_WIDE, seq)
    hb = tr // HALO
    nc = seq // tr
    n_halo = seq // HALO

    def kern(up_ref, halo_ref, zp_ref, zpn_ref, dyp_ref, dypn_ref, pw_ref, ps_ref, _,
             d01_ref, dpw_ref, dps_ref, ext, dpn, buf_a, buf_b, acc_pw, acc_ps):
        c = pl.program_id(0)

        @pl.when(c == 0)
        def _():
            acc_pw[...] = jnp.zeros_like(acc_pw)
            acc_ps[...] = jnp.zeros_like(acc_ps)

        _zero_pads((ext, dpn, buf_a, buf_b), HALO + tr)
        ext[pl.ds(PAD, HALO), :] = jnp.where(c > 0, halo_ref[...].astype(F32), 0.0)
        ext[pl.ds(PAD + HALO, tr), :] = up_ref[...].astype(F32)
        pos = c * tr + lax.broadcasted_iota(jnp.int32, (tr, POOL_GW), 0)
        pos_n = (c + 1) * tr + lax.broadcasted_iota(jnp.int32, (HALO, POOL_GW), 0)
        has_next = c < nc - 1
        for g, w in enumerate(POOL_WINDOWS):
            cols = pl.ds(g * POOL_GW, POOL_GW)
            pooled_bf = _pool_windows(ext, (buf_a, buf_b), pos, g, w, tr).astype(BF16)
            wg = pw_ref[g]
            mixed = _dot(pooled_bf, wg)
            zp = zp_ref[:, cols].astype(F32)
            sz = _silu(zp)
            dyp_g = dyp_ref[:, cols].astype(F32)
            ps = ps_ref[:, cols]
            dmixed = (dyp_g * ps * sz).astype(BF16)
            acc_ps[:, cols] += _acc8(dyp_g * mixed * sz)
            d01_ref[:, pl.ds(D_MODEL + g * POOL_GW, POOL_GW)] = (dyp_g * mixed * ps * _dsilu(zp)).astype(BF16)
            acc_pw[g] += _dot_tn(pooled_bf, dmixed)
            dpooled = _dot_nt(dmixed, wg)
            dmixed_n = (jnp.where(has_next, dypn_ref[:, cols].astype(F32), 0.0) * ps * _silu(zpn_ref[:, cols].astype(F32))).astype(BF16)
            dpooled_n = _dot_nt(dmixed_n, wg)
            dpn[pl.ds(PAD, tr), :] = dpooled / jnp.minimum(pos + 1, w).astype(F32)
            dpn[pl.ds(PAD + tr, HALO), :] = dpooled_n / jnp.minimum(pos_n + 1, w).astype(F32)
            win, _ = _window_sums(dpn, slice(None), w, (buf_a, buf_b), tr + HALO, ahead=True)
            d01_ref[:, cols] = (win[pl.ds(PAD, tr), :] - dpooled).astype(BF16)

        @pl.when(c == nc - 1)
        def _():
            dpw_ref[...] = acc_pw[...]
            dps_ref[...] = jnp.sum(acc_ps[...], axis=0, keepdims=True)

    nxt = lambda c: (jnp.minimum((c + 1) * hb, n_halo - 1), 0)
    nxt1 = lambda c: (jnp.minimum((c + 1) * hb, n_halo - 1), 1)
    return pl.pallas_call(
        kern, name="pool_bwd", grid=(nc,),
        in_specs=[_row_spec(tr, col=0),
                  pl.BlockSpec((HALO, D_MODEL), lambda c: (jnp.maximum(c * hb - 1, 0), 0)),
                  _row_spec(tr, col=1),
                  pl.BlockSpec((HALO, D_MODEL), nxt1),
                  _row_spec(tr),
                  pl.BlockSpec((HALO, D_MODEL), nxt),
                  pl.BlockSpec((len(POOL_WINDOWS), POOL_GW, POOL_GW), lambda c: (0, 0, 0)),
                  _vec_spec(), _ANY],
        out_specs=(pl.BlockSpec((tr, 2 * D_MODEL), lambda c: (c, 0)),
                   pl.BlockSpec((len(POOL_WINDOWS), POOL_GW, POOL_GW), lambda c: (0, 0, 0)),
                   _vec_spec()),
        out_shape=(jax.ShapeDtypeStruct(dproj.shape, BF16),
                   jax.ShapeDtypeStruct((len(POOL_WINDOWS), POOL_GW, POOL_GW), F32),
                   jax.ShapeDtypeStruct((1, D_MODEL), F32)),
        scratch_shapes=[pltpu.VMEM((tr + HALO + 2 * PAD, D_MODEL), F32)]
        + [pltpu.VMEM((tr + HALO + 2 * PAD, POOL_GW), F32)] * 3
        + [pltpu.VMEM((len(POOL_WINDOWS), POOL_GW, POOL_GW), F32), pltpu.VMEM((SUBLANES, D_MODEL), F32)],
        input_output_aliases={8: 0},
        compiler_params=_cparams("arbitrary"))(proj, proj, proj, proj, dyp, dyp, pool_w_bf, pscale, dproj)


def _glu_fwd(ys, proj, glu_w_bf, glu_b):
    seq = ys.shape[0]
    tr = min(ROW_CHUNK_WIDE, seq)

    def kern(ys_ref, zs_ref, w_ref, b_ref, o_ref, ot_ref):
        yg = _gelu(ys_ref[...])
        q = _dot(yg.astype(BF16), w_ref[...]) + b_ref[...]
        y = yg * _sigmoid(q) * _silu(zs_ref[...].astype(F32))
        o_ref[...] = y.astype(BF16)
        ot_ref[...] = y.T.astype(BF16)

    return pl.pallas_call(
        kern, name="glu_fwd", grid=(seq // tr,),
        in_specs=[_row_spec(tr), _row_spec(tr, col=3), pl.BlockSpec((D_MODEL, D_MODEL), lambda c: (0, 0)), _vec_spec()],
        out_specs=(_row_spec(tr), _col_spec(tr)),
        out_shape=(jax.ShapeDtypeStruct((seq, D_MODEL), BF16), jax.ShapeDtypeStruct((D_MODEL, seq), BF16)),
        compiler_params=_cparams("parallel"))(ys, proj, glu_w_bf, glu_b)


def _glu_bwd(ys, proj, dyssm, glu_w_bf, glu_b, dproj):
    seq = ys.shape[0]
    tr = min(ROW_CHUNK_WIDE, seq)
    nc = seq // tr

    def kern(ys_ref, zs_ref, dy_ref, w_ref, b_ref, _, dys_ref, dzs_ref, dq_ref, yg_ref, db_ref, acc_b):
        c = pl.program_id(0)

        @pl.when(c == 0)
        def _():
            acc_b[...] = jnp.zeros_like(acc_b)

        yg, dgelu = _gelu_and_grad(ys_ref[...])
        yg_bf = yg.astype(BF16)
        q = _dot(yg_bf, w_ref[...]) + b_ref[...]
        sg = _sigmoid(q)
        silu_z, dsilu_z = _silu_and_grad(zs_ref[...].astype(F32))
        dyv = dy_ref[...].astype(F32)
        dyglu = dyv * silu_z
        yglu = yg * sg
        dzs_ref[...] = (dyv * yglu * dsilu_z).astype(BF16)
        dq = dyglu * yglu * (1.0 - sg)
        dq_bf = dq.astype(BF16)
        acc_b[...] += _acc8(dq)
        dyg = dyglu * sg + _dot_nt(dq_bf, w_ref[...])
        dys_ref[...] = dyg * dgelu
        dq_ref[...] = dq_bf
        yg_ref[...] = yg.T.astype(BF16)

        @pl.when(c == nc - 1)
        def _():
            db_ref[...] = jnp.sum(acc_b[...], axis=0, keepdims=True)

    bf = jax.ShapeDtypeStruct((seq, D_MODEL), BF16)
    return pl.pallas_call(
        kern, name="glu_bwd", grid=(nc,),
        in_specs=[_row_spec(tr), _row_spec(tr, col=3), _row_spec(tr),
                  pl.BlockSpec((D_MODEL, D_MODEL), lambda c: (0, 0)), _vec_spec(), _ANY],
        out_specs=(_row_spec(tr), _row_spec(tr, col=3), _row_spec(tr), _col_spec(tr), _vec_spec()),
        out_shape=(jax.ShapeDtypeStruct((seq, D_MODEL), F32), jax.ShapeDtypeStruct(dproj.shape, BF16), bf,
                   jax.ShapeDtypeStruct((D_MODEL, seq), BF16), jax.ShapeDtypeStruct((1, D_MODEL), F32)),
        scratch_shapes=[pltpu.VMEM((SUBLANES, D_MODEL), F32)],
        input_output_aliases={5: 1},
        compiler_params=_cparams("arbitrary"))(ys, proj, dyssm, glu_w_bf, glu_b, dproj)


def _out_fwd_bwd(ypool, yssm, proj, x, tgt, gate, g2, wbp_bf, wbs_bf, wout_bf):
    seq = x.shape[0]
    tr = min(ROW_CHUNK, seq)
    nc = seq // tr

    def kern(yp_ref, ysm_ref, gp_ref, gs_ref, x_ref, t_ref, gate_ref, g2_ref, wbp_ref, wbs_ref, wo_ref,
             dy_ref, dyp_ref, dys_ref, d45_ref, mb_ref, dob_ref, dbp_ref, dbs_ref, loss_ref, dgate_ref, dg2_ref,
             acc_l, acc_gate, acc_g2):
        c = pl.program_id(0)

        @pl.when(c == 0)
        def _():
            acc_l[...] = jnp.zeros_like(acc_l)
            acc_gate[...] = jnp.zeros_like(acc_gate)
            acc_g2[...] = jnp.zeros_like(acc_g2)

        bp = _dot(yp_ref[...], wbp_ref[...])
        bs = _dot(ysm_ref[...], wbs_ref[...])
        sp = _sigmoid(gp_ref[...].astype(F32))
        ss = _sigmoid(gs_ref[...].astype(F32))
        merged = sp * bp + ss * bs
        mb = merged.astype(BF16)
        out = _dot(mb, wo_ref[...])
        r2 = lax.rsqrt(jnp.mean(out * out, axis=-1, keepdims=True) + RMS_EPS)
        oh = out * r2
        gate_v, g2_v = gate_ref[...], g2_ref[...]
        ohg = oh * g2_v
        diff = (x_ref[...] + gate_v * ohg) - t_ref[...]
        acc_l[...] += _acc8(diff * diff)
        dyv = diff * (1.0 / D_MODEL)
        dy_ref[...] = dyv
        dy_oh = dyv * oh
        acc_gate[...] += _acc8(dy_oh * g2_v)
        acc_g2[...] += _acc8(dy_oh * gate_v)
        gg = gate_v * g2_v
        doh = dyv * gg
        dout = r2 * (doh - oh * jnp.mean(dy_oh * gg, axis=-1, keepdims=True))
        dob = dout.astype(BF16)
        dmerged = _dot_nt(dob, wo_ref[...])
        dbp_f = dmerged * sp
        dbs_f = dmerged * ss
        dbp = dbp_f.astype(BF16)
        dbs = dbs_f.astype(BF16)
        d45_ref[:, 0:D_MODEL] = (dbp_f * bp * (1.0 - sp)).astype(BF16)
        d45_ref[:, D_MODEL:] = (dbs_f * bs * (1.0 - ss)).astype(BF16)
        dyp_ref[...] = _dot_nt(dbp, wbp_ref[...]).astype(BF16)
        dys_ref[...] = _dot_nt(dbs, wbs_ref[...]).astype(BF16)
        mb_ref[...] = merged.T.astype(BF16)
        dob_ref[...] = dob
        dbp_ref[...] = dbp
        dbs_ref[...] = dbs

        @pl.when(c == nc - 1)
        def _():
            tot = jnp.sum(acc_l[...], axis=0, keepdims=True)
            loss_ref[...] = jnp.sum(tot, axis=1, keepdims=True) * (0.5 / D_MODEL)
            dgate_ref[...] = jnp.sum(acc_gate[...], axis=0, keepdims=True)
            dg2_ref[...] = jnp.sum(acc_g2[...], axis=0, keepdims=True)

    wspec = pl.BlockSpec((D_MODEL, D_MODEL), lambda c: (0, 0))
    f32 = jax.ShapeDtypeStruct((seq, D_MODEL), F32)
    bf = jax.ShapeDtypeStruct((seq, D_MODEL), BF16)
    vec = jax.ShapeDtypeStruct((1, D_MODEL), F32)
    acc = pltpu.VMEM((SUBLANES, D_MODEL), F32)
    return pl.pallas_call(
        kern, name="out_fwd_bwd", grid=(nc,),
        in_specs=[_row_spec(tr), _row_spec(tr), _row_spec(tr, col=4), _row_spec(tr, col=5), _row_spec(tr), _row_spec(tr),
                  _vec_spec(), _vec_spec(), wspec, wspec, wspec],
        out_specs=(_row_spec(tr), _row_spec(tr), _row_spec(tr), pl.BlockSpec((tr, 2 * D_MODEL), lambda c: (c, 2)),
                   _col_spec(tr), _row_spec(tr), _row_spec(tr), _row_spec(tr),
                   pl.BlockSpec((1, 1), lambda c: (0, 0)), _vec_spec(), _vec_spec()),
        out_shape=(f32, bf, bf, jax.ShapeDtypeStruct((seq, proj.shape[1]), BF16),
                   jax.ShapeDtypeStruct((D_MODEL, seq), BF16), bf, bf, bf,
                   jax.ShapeDtypeStruct((1, 1), F32), vec, vec),
        scratch_shapes=[acc, acc, acc],
        compiler_params=_cparams("arbitrary"))(ypool, yssm, proj, proj, x, tgt, gate, g2, wbp_bf, wbs_bf, wout_bf)


def _in_bwd(dh, x, dy, g1, scale):
    seq = x.shape[0]
    tr = min(ROW_CHUNK_WIDE, seq)
    nc = seq // tr

    def kern(dh_ref, x_ref, dy_ref, g_ref, sc_ref, dx_ref, dsh_ref, dsc_ref, dg_ref, a_sh, a_sc, a_g):
        c = pl.program_id(0)

        @pl.when(c == 0)
        def _():
            a_sh[...] = jnp.zeros_like(a_sh)
            a_sc[...] = jnp.zeros_like(a_sc)
            a_g[...] = jnp.zeros_like(a_g)

        xv = x_ref[...]
        r = lax.rsqrt(jnp.mean(xv * xv, axis=-1, keepdims=True) + RMS_EPS)
        xh = xv * r
        g = g_ref[...]
        dhv = dh_ref[...]
        a_sh[...] += _acc8(dhv)
        a_sc[...] += _acc8(dhv * (xh * g))
        dn = dhv * (1.0 + sc_ref[...])
        a_g[...] += _acc8(dn * xh)
        dxh = dn * g
        dx_ref[...] = dy_ref[...] + r * (dxh - xh * jnp.mean(dxh * xh, axis=-1, keepdims=True))

        @pl.when(c == nc - 1)
        def _():
            dsh_ref[...] = jnp.sum(a_sh[...], axis=0, keepdims=True)
            dsc_ref[...] = jnp.sum(a_sc[...], axis=0, keepdims=True)
            dg_ref[...] = jnp.sum(a_g[...], axis=0, keepdims=True)

    vec = jax.ShapeDtypeStruct((1, D_MODEL), F32)
    acc = pltpu.VMEM((SUBLANES, D_MODEL), F32)
    return pl.pallas_call(
        kern, name="in_bwd", grid=(nc,),
        in_specs=[_row_spec(tr), _row_spec(tr), _row_spec(tr), _vec_spec(), _vec_spec()],
        out_specs=(_row_spec(tr), _vec_spec(), _vec_spec(), _vec_spec()),
        out_shape=(jax.ShapeDtypeStruct((seq, D_MODEL), F32), vec, vec, vec),
        scratch_shapes=[acc, acc, acc],
        compiler_params=_cparams("arbitrary"))(dh, x, dy, g1, scale)


SLAB = 2 * SUBLANES


def _local_scan(a_re, a_im, br, bi, xr, xi, row0, ls, reverse, init=None, xb=None):
    if init is None:
        x_re = jnp.zeros((SUBLANES, STATE_W), F32)
        x_im = jnp.zeros((SUBLANES, STATE_W), F32)
    else:
        x_re, x_im = init
    for i in (range(ls - 1, -1, -1) if reverse else range(ls)):
        src = pl.ds(SUBLANES * i, SUBLANES)
        dst = pl.ds(row0 + SUBLANES * i, SUBLANES)
        n_re = a_re * x_re - a_im * x_im + br[src, :]
        n_im = a_re * x_im + a_im * x_re + bi[src, :]
        if xb is not None and i % 2 == 1:
            pair = pl.ds(SUBLANES * (i - 1), SLAB)
            xb[0][pair, :] = jnp.concatenate([x_re, n_re], axis=0).astype(BF16)
            xb[1][pair, :] = jnp.concatenate([x_im, n_im], axis=0).astype(BF16)
        x_re, x_im = n_re, n_im
        xr[dst, :] = x_re
        xi[dst, :] = x_im
    return x_re, x_im


def _two(v):
    return jnp.concatenate([v, v], axis=0)


def _unpermute_rhs(v, sel):
    hi = v.astype(BF16)
    r1 = v - hi.astype(F32)
    mid = r1.astype(BF16)
    lo = (r1 - mid.astype(F32)).astype(BF16)
    return _dot(hi, sel) + _dot(mid, sel) + _dot(lo, sel)


def _scan_specs(tc, nb, rows_of):
    return dict(
        us=pl.BlockSpec((tc, nb * LANES), lambda b, c: (rows_of(c), 2 * D_MODEL // (nb * LANES) + b)),
        tok=pl.BlockSpec((tc, nb * LANES), lambda b, c: (rows_of(c), b)),
        bblk=pl.BlockSpec((nb, LANES, STATE_W), lambda b, c: (b, 0, 0)),
        cblk=pl.BlockSpec((nb, STATE_W, LANES), lambda b, c: (b, 0, 0)),
        vec=pl.BlockSpec((1, nb * STATE_W), lambda b, c: (0, b)),
        tab=pl.BlockSpec((tc, nb * STATE_W), lambda b, c: (0, b)),
        car=pl.BlockSpec((SUBLANES, nb * STATE_W), lambda b, c: (rows_of(c), b)),
        dvec=pl.BlockSpec((1, nb * LANES), lambda b, c: (0, b)))


def _ssm_scan_fwd(proj, bb_re, bb_im, cm_re, cm_im, abar_re, abar_im, pw_re, pw_im, d_skip, tc):
    seq = proj.shape[0]
    nc = seq // tc
    ls = tc // SUBLANES
    nb = SCAN_BLOCKS

    def kern(us_ref, bbr_ref, bbi_ref, cmr_ref, cmi_ref, ar_ref, ai_ref, pwr_ref, pwi_ref, d_ref,
             ys_ref, ecr_ref, eci_ref, bur, bui, car_r, car_i, end_r, end_i, upb, xb_r, xb_i, *nat):
        c = pl.program_id(1)

        @pl.when(c == 0)
        def _():
            car_r[...] = jnp.zeros_like(car_r)
            car_i[...] = jnp.zeros_like(car_i)

        for j in range(nb):
            cols = pl.ds(j * LANES, LANES)
            scols = pl.ds(j * STATE_W, STATE_W)
            nat[j][...] = us_ref[:, cols].astype(F32)
            for i in range(ls):
                upb[j, pl.ds(SUBLANES * i, SUBLANES), :] = nat[j][pl.ds(i, SUBLANES, stride=ls), :]
            u = upb[j]
            up = u.astype(BF16)
            bur[j] = _dot(up, bbr_ref[j])
            bui[j] = _dot(up, bbi_ref[j])
            a_re = jnp.broadcast_to(ar_ref[:, scols], (SUBLANES, STATE_W))
            a_im = jnp.broadcast_to(ai_ref[:, scols], (SUBLANES, STATE_W))
            x_re, x_im = _local_scan(a_re, a_im, bur.at[j], bui.at[j], bur.at[j], bui.at[j], 0, ls, False)
            end_r[j] = x_re
            end_i[j] = x_im
            big_re = pwr_ref[tc - 1:tc, scols]
            big_im = pwi_ref[tc - 1:tc, scols]
            e_re = car_r[j, 0:1, :]
            e_im = car_i[j, 0:1, :]
            for s in range(SUBLANES):
                n_re = end_r[j, s:s + 1, :] + big_re * e_re - big_im * e_im
                n_im = end_i[j, s:s + 1, :] + big_re * e_im + big_im * e_re
                e_re, e_im = n_re, n_im
                if s < SUBLANES - 1:
                    car_r[j, s + 1:s + 2, :] = e_re
                    car_i[j, s + 1:s + 2, :] = e_im
            ec_re = car_r[j]
            ec_im = car_i[j]
            ecr_ref[:, scols] = ec_re
            eci_ref[:, scols] = ec_im
            e2_re, e2_im = _two(ec_re), _two(ec_im)
            for k in range(tc // SLAB):
                rows_k = pl.ds(SLAB * k, SLAB)
                p_re = pwr_ref[rows_k, scols]
                p_im = pwi_ref[rows_k, scols]
                xb_r[j, rows_k, :] = (bur[j, rows_k, :] + p_re * e2_re - p_im * e2_im).astype(BF16)
                xb_i[j, rows_k, :] = (bui[j, rows_k, :] + p_re * e2_im + p_im * e2_re).astype(BF16)
            upb[j] = _dot(xb_r[j], cmr_ref[j]) - _dot(xb_i[j], cmi_ref[j]) + d_ref[:, cols] * u
            for i in range(ls):
                nat[j][pl.ds(i, SUBLANES, stride=ls), :] = upb[j, pl.ds(SUBLANES * i, SUBLANES), :]
            ys_ref[:, cols] = nat[j][...]
            car_r[j, 0:1, :] = e_re
            car_i[j, 0:1, :] = e_im

    sp = _scan_specs(tc, nb, lambda c: c)
    carry_shape = jax.ShapeDtypeStruct((nc * SUBLANES, STATE_ALL), F32)
    small = pltpu.VMEM((nb, SUBLANES, STATE_W), F32)
    big = pltpu.VMEM((nb, tc, STATE_W), F32)
    return pl.pallas_call(
        kern, name="ssm_scan_fwd", grid=(LANE_BLOCKS // nb, nc),
        in_specs=[sp["us"], sp["bblk"], sp["bblk"], sp["cblk"], sp["cblk"], sp["vec"], sp["vec"], sp["tab"], sp["tab"],
                  sp["dvec"]],
        out_specs=(sp["tok"], sp["car"], sp["car"]),
        out_shape=(jax.ShapeDtypeStruct((seq, D_MODEL), F32), carry_shape, carry_shape),
        scratch_shapes=[big, big, small, small, small, small, pltpu.VMEM((nb, tc, LANES), F32),
                        pltpu.VMEM((nb, tc, STATE_W), BF16), pltpu.VMEM((nb, tc, STATE_W), BF16)]
        + [pltpu.VMEM((tc, LANES), F32)] * nb,
        compiler_params=_cparams("parallel", "arbitrary"),
    )(proj, bb_re, bb_im, cm_re, cm_im, abar_re, abar_im, pw_re, pw_im, d_skip)


def _ssm_scan_bwd(proj, dys, ec_re, ec_im, bb_re, bb_im, cm_re, cm_im, abar_re, abar_im,
                  pw_re, pw_im, pv_re, pv_im, d_skip, dproj, tc):
    seq = proj.shape[0]
    nc = seq // tc
    ls = tc // SUBLANES
    nb = SCAN_BLOCKS

    def kern(us_ref, dys_ref, ecr_ref, eci_ref, bbr_ref, bbi_ref, cmr_ref, cmi_ref, ar_ref, ai_ref,
             pwr_ref, pwi_ref, pvr_ref, pvi_ref, d_ref, _,
             dus_ref, dbbr_ref, dbbi_ref, dcmr_ref, dcmi_ref, dar_ref, dai_ref, dd_ref,
             bur, bui, xr, xi, gr, gi, fc_r, fc_i, a_bbr, a_bbi, a_cmr, a_cmi, a_ar, a_ai, a_dd, upb, dpb, hb_r, hb_i,
             *nat):
        c = pl.program_id(1)

        @pl.when(c == 0)
        def _():
            for ref in (fc_r, fc_i, a_bbr, a_bbi, a_cmr, a_cmi, a_ar, a_ai, a_dd):
                ref[...] = jnp.zeros_like(ref)

        for j in range(nb):
            cols = pl.ds(j * LANES, LANES)
            scols = pl.ds(j * STATE_W, STATE_W)
            nat_u, nat_d = nat[2 * j], nat[2 * j + 1]
            nat_u[...] = us_ref[:, cols].astype(F32)
            nat_d[...] = dys_ref[:, cols]
            for i in range(ls):
                rows_i = pl.ds(SUBLANES * i, SUBLANES)
                upb[j, rows_i, :] = nat_u[pl.ds(i, SUBLANES, stride=ls), :]
                dpb[j, rows_i, :] = nat_d[pl.ds(i, SUBLANES, stride=ls), :]
            u = upb[j]
            dysv = dpb[j]
            a_dd[j] += _acc8(dysv * u)
            up = u.astype(BF16)
            bur[j] = _dot(up, bbr_ref[j])
            bui[j] = _dot(up, bbi_ref[j])
            a_re = jnp.broadcast_to(ar_ref[:, scols], (SUBLANES, STATE_W))
            a_im = jnp.broadcast_to(ai_ref[:, scols], (SUBLANES, STATE_W))
            ec_r = ecr_ref[:, scols]
            ec_i = eci_ref[:, scols]
            xr[j, 0:SUBLANES, :] = ec_r
            xi[j, 0:SUBLANES, :] = ec_i
            _local_scan(a_re, a_im, bur.at[j], bui.at[j], xr.at[j], xi.at[j], SUBLANES, ls, False, init=(ec_r, ec_i),
                        xb=(hb_r.at[j], hb_i.at[j]))
            dysp = dysv.astype(BF16)
            a_cmr[j] += _dot_tn(dysp, hb_r[j])
            a_cmi[j] -= _dot_tn(dysp, hb_i[j])
            gr[j] = _dot_nt(dysp, cmr_ref[j])
            gi[j] = -_dot_nt(dysp, cmi_ref[j])
            _local_scan(a_re, -a_im, gr.at[j], gi.at[j], gr.at[j], gi.at[j], 0, ls, True)
            big_re = pwr_ref[tc - 1:tc, scols]
            big_im = -pwi_ref[tc - 1:tc, scols]
            f_re = fc_r[j, SUBLANES - 1:SUBLANES, :]
            f_im = fc_i[j, SUBLANES - 1:SUBLANES, :]
            for s in range(SUBLANES - 1, -1, -1):
                n_re = gr[j, s:s + 1, :] + big_re * f_re - big_im * f_im
                n_im = gi[j, s:s + 1, :] + big_re * f_im + big_im * f_re
                f_re, f_im = n_re, n_im
                if s > 0:
                    fc_r[j, s - 1:s, :] = f_re
                    fc_i[j, s - 1:s, :] = f_im
            f2_r, f2_i = _two(fc_r[j]), _two(fc_i[j])
            acc_r = jnp.zeros((SUBLANES, STATE_W), F32)
            acc_i = jnp.zeros((SUBLANES, STATE_W), F32)
            for k in range(tc // SLAB):
                rows_k = pl.ds(SLAB * k, SLAB)
                q_re = pvr_ref[rows_k, scols]
                q_im = pvi_ref[rows_k, scols]
                lam_re = gr[j, rows_k, :] + q_re * f2_r + q_im * f2_i
                lam_im = gi[j, rows_k, :] + q_re * f2_i - q_im * f2_r
                xp_re = xr[j, rows_k, :]
                xp_im = xi[j, rows_k, :]
                d_r = lam_re * xp_re + lam_im * xp_im
                d_i = lam_im * xp_re - lam_re * xp_im
                acc_r = acc_r + (d_r[0:SUBLANES] + d_r[SUBLANES:])
                acc_i = acc_i + (d_i[0:SUBLANES] + d_i[SUBLANES:])
                hb_r[j, rows_k, :] = lam_re.astype(BF16)
                hb_i[j, rows_k, :] = lam_im.astype(BF16)
            a_ar[j] += acc_r
            a_ai[j] += acc_i
            fc_r[j, SUBLANES - 1:SUBLANES, :] = f_re
            fc_i[j, SUBLANES - 1:SUBLANES, :] = f_im
            lb_re = hb_r[j]
            lb_im = hb_i[j]
            a_bbr[j] += _dot_tn(up, lb_re)
            a_bbi[j] += _dot_tn(up, lb_im)
            dpb[j] = _dot_nt(lb_re, bbr_ref[j]) + _dot_nt(lb_im, bbi_ref[j]) + dysv * d_ref[:, cols]
            for i in range(ls):
                nat_d[pl.ds(i, SUBLANES, stride=ls), :] = dpb[j, pl.ds(SUBLANES * i, SUBLANES), :]
            dus_ref[:, cols] = nat_d[...].astype(BF16)

        @pl.when(c == nc - 1)
        def _():
            row_g = lax.broadcasted_iota(jnp.int32, (LANES, STATE_W), 0) // SSM_H
            col_g = lax.broadcasted_iota(jnp.int32, (LANES, STATE_W), 1) // SSM_P
            fold = (lax.broadcasted_iota(jnp.int32, (STATE_W, SSM_P), 0) % SSM_P
                    == lax.broadcasted_iota(jnp.int32, (STATE_W, SSM_P), 1)).astype(BF16)
            for j in range(nb):
                rows_j = pl.ds(j * LANES, LANES)
                for acc, out in ((a_bbr, dbbr_ref), (a_bbi, dbbi_ref), (a_cmr, dcmr_ref), (a_cmi, dcmi_ref)):
                    out[rows_j, :] = _unpermute_rhs(jnp.where(row_g == col_g, acc[j], 0.0), fold)
                dar_ref[:, pl.ds(j * STATE_W, STATE_W)] = jnp.sum(a_ar[j], axis=0, keepdims=True)
                dai_ref[:, pl.ds(j * STATE_W, STATE_W)] = jnp.sum(a_ai[j], axis=0, keepdims=True)
                dd_ref[:, pl.ds(j * LANES, LANES)] = jnp.sum(a_dd[j], axis=0, keepdims=True)

    sp = _scan_specs(tc, nb, lambda c: nc - 1 - c)
    ghp = pl.BlockSpec((nb * LANES, SSM_P), lambda b, c: (b, 0))
    ghp_shape = jax.ShapeDtypeStruct((SSM_G * SSM_H, SSM_P), F32)
    small = pltpu.VMEM((nb, SUBLANES, STATE_W), F32)
    big = pltpu.VMEM((nb, tc, STATE_W), F32)
    bigp = pltpu.VMEM((nb, tc + SUBLANES, STATE_W), F32)
    blk = pltpu.VMEM((nb, LANES, STATE_W), F32)
    tok = pltpu.VMEM((nb, tc, LANES), F32)
    return pl.pallas_call(
        kern, name="ssm_scan_bwd", grid=(LANE_BLOCKS // nb, nc),
        in_specs=[sp["us"], sp["tok"], sp["car"], sp["car"], sp["bblk"], sp["bblk"], sp["cblk"], sp["cblk"],
                  sp["vec"], sp["vec"], sp["tab"], sp["tab"], sp["tab"], sp["tab"], sp["dvec"], _ANY],
        out_specs=(sp["us"], ghp, ghp, ghp, ghp, sp["vec"], sp["vec"], sp["dvec"]),
        out_shape=(jax.ShapeDtypeStruct(dproj.shape, BF16), ghp_shape, ghp_shape, ghp_shape, ghp_shape,
                   jax.ShapeDtypeStruct((1, STATE_ALL), F32), jax.ShapeDtypeStruct((1, STATE_ALL), F32),
                   jax.ShapeDtypeStruct((1, D_MODEL), F32)),
        scratch_shapes=[big, big, bigp, bigp, big, big, small, small, blk, blk, blk, blk,
                        small, small, pltpu.VMEM((nb, SUBLANES, LANES), F32), tok, tok,
                        pltpu.VMEM((nb, tc, STATE_W), BF16), pltpu.VMEM((nb, tc, STATE_W), BF16)]
        + [pltpu.VMEM((tc, LANES), F32)] * (2 * nb),
        input_output_aliases={15: 0},
        compiler_params=_cparams("parallel", "arbitrary"),
    )(proj, dys, ec_re, ec_im, bb_re, bb_im, cm_re, cm_im, abar_re, abar_im, pw_re, pw_im, pv_re, pv_im, d_skip, dproj)


def _eye5():
    return jnp.asarray(np.eye(GROUPS_PER_BLOCK, dtype=np.float32)[None, :, None, :, None])


def _embed_b(bb_t):
    t = bb_t.transpose(1, 0, 2).reshape(LANE_BLOCKS, GROUPS_PER_BLOCK, SSM_H, 1, SSM_P)
    return (t * _eye5()).reshape(LANE_BLOCKS, LANES, STATE_W)


def _embed_c(c_ghp):
    t = c_ghp.transpose(0, 2, 1).reshape(LANE_BLOCKS, GROUPS_PER_BLOCK, SSM_P, 1, SSM_H)
    return (t * _eye5()).reshape(LANE_BLOCKS, STATE_W, LANES)


def _local_step(x, c_row, tgt, w_ada_bf, b_ada, g1, g2, w_in_bf, pool_w_bf, pscale, a_re, a_im, log_dt,
                b_re_t, b_im_t, c_re, c_im, d_skip, glu_w_bf, glu_b, wbp_bf, wbs_bf, wout_bf,
                split_proj=None, ride_for_dw_in=None, ride_for_dh=None, mod_fn=None):
    seq = x.shape[0]
    tc = min(SCAN_CHUNK, seq)
    mod8, silu_c = _mod_kernel(c_row, w_ada_bf, b_ada) if mod_fn is None else mod_fn(c_row, b_ada)
    mod = mod8[0:1]
    shift, scale, gate = mod[:, 0:D_MODEL], mod[:, D_MODEL:2 * D_MODEL], mod[:, 2 * D_MODEL:]

    abar_re, abar_im, bb_re_t, bb_im_t = _ssm_params(a_re, a_im, log_dt, b_re_t, b_im_t)
    abar_re_f, abar_im_f = abar_re.reshape(1, STATE_ALL), abar_im.reshape(1, STATE_ALL)
    pw_re, pw_im, pv_re, pv_im = _pow_tables(abar_re_f, abar_im_f, tc)
    bbe_re, bbe_im = _embed_b(bb_re_t).astype(BF16), _embed_b(bb_im_t).astype(BF16)
    cme_re, cme_im = _embed_c(c_re).astype(BF16), _embed_c(c_im).astype(BF16)
    d_row = d_skip.reshape(1, D_MODEL)

    if split_proj:
        w_own, chip, w_in_ride, unpack_w_in, late_ride, unpack_late = split_proj
        h, h_t, proj, w_blocks = _in_norm_proj_own(x, g1, scale, shift, w_own, chip, w_in_ride)
        w_in_bf = unpack_w_in(w_blocks)
        proj, *gathered = _proj_rest(h, w_blocks, proj, chip, late_ride)
        pool_w_bf, glu_w_bf, wbp_bf, wbs_bf, wout_bf = unpack_late(*gathered)
    else:
        h, h_t = _in_norm(x, g1, scale, shift)
        proj = _mm([h], [w_in_bf], name="proj", out_dtype=BF16, bm=1024, bn=1536, bk=1024)
    ypool, ypool_t = _pool_fwd(proj, pool_w_bf, pscale)
    ys, ec_re, ec_im = _ssm_scan_fwd(proj, bbe_re, bbe_im, cme_re, cme_im, abar_re_f, abar_im_f,
                                      pw_re, pw_im, d_row, tc)
    yssm, yssm_t = _glu_fwd(ys, proj, glu_w_bf, glu_b)
    (dy, dypool, dyssm, dproj, merged_t, dob, dbp, dbs, loss, dgate, dg2) = _out_fwd_bwd(
        ypool, yssm, proj, x, tgt, gate, g2, wbp_bf, wbs_bf, wout_bf)

    d_wout = _mm([merged_t], [dob], name="dw_out", bm=1024, bn=1024, bk=2048)
    d_wbp = _mm([ypool_t], [dbp], name="dw_bp", bm=1024, bn=1024, bk=2048)
    d_wbs = _mm([yssm_t], [dbs], name="dw_bs", bm=1024, bn=1024, bk=2048)
    dys, dproj, dq, yg_t, d_glu_b = _glu_bwd(ys, proj, dyssm, glu_w_bf, glu_b, dproj)
    d_glu_w = _mm([yg_t], [dq], name="dw_glu", bm=1024, bn=1024, bk=2048)
    (dproj, dbbe_re, dbbe_im, dcme_re, dcme_im, d_abar_re, d_abar_im, d_dskip) = _ssm_scan_bwd(
        proj, dys, ec_re, ec_im, bbe_re, bbe_im, cme_re, cme_im, abar_re_f, abar_im_f,
        pw_re, pw_im, pv_re, pv_im, d_row, dproj, tc)
    dproj, d_pool_w, d_pscale = _pool_bwd(proj, dypool, pool_w_bf, pscale, dproj)
    dparts = [dproj]
    small_ready = dict(
        dg2=dg2, d_pscale=d_pscale, d_glu_b=d_glu_b, d_dskip=d_dskip, d_abar_re=d_abar_re, d_abar_im=d_abar_im,
        d_bb_re_t=dbbe_re.reshape(SSM_G, SSM_H, SSM_P).transpose(1, 0, 2),
        d_bb_im_t=dbbe_im.reshape(SSM_G, SSM_H, SSM_P).transpose(1, 0, 2),
        d_c_re=dcme_re.reshape(SSM_G, SSM_H, SSM_P), d_c_im=dcme_im.reshape(SSM_G, SSM_H, SSM_P))
    ride = ride_for_dw_in(small_ready) if ride_for_dw_in else None
    d_win = _mm([h_t], dparts, name="dw_in", bm=1024, bn=6 * D_MODEL // N_CHIPS, bk=2048, ride=ride,
                blocks_first=True)
    rode_dw_in = ()
    if ride:
        d_win, rode_dw_in = d_win[0], tuple(d_win[1:])
    big_grads = dict(d_win=d_win, d_glu_w=d_glu_w, d_wbp=d_wbp, d_wbs=d_wbs, d_wout=d_wout, d_pool_w=d_pool_w)
    ride = ride_for_dh(big_grads) if ride_for_dh else None
    dh = _mm(dparts, [w_in_bf], tb=True, name="dh", bm=2048, bn=1024, bk=1024, ride=ride)
    rode = ()
    if ride:
        dh, rode = dh[0], tuple(dh[1:])
    grad_x, dshift, dscale, dg1 = _in_bwd(dh, x, dy, g1, scale)
    dmod = jnp.concatenate([dshift, dscale, dgate], axis=1)
    return dict(
        rode=rode, rode_dw_in=rode_dw_in, loss=loss[0, 0], grad_x=grad_x, dmod=dmod, silu_c=silu_c, dg1=dg1,
        **small_ready, **big_grads)


def _position():
    x, y, c = lax.axis_index("x"), lax.axis_index("y"), lax.axis_index("c")
    chips = [(1 - x, y), (x, 1 - y), (1 - x, 1 - y)]
    return x, y, c, chips


_ANY = pl.BlockSpec(memory_space=pl.ANY)
COMM_CHUNKS = 4
COMM_ROW_ALIGN = 16


def _row_chunks(rows, k):
    assert rows % (k * COMM_ROW_ALIGN) == 0, (rows, k)
    step = rows // k
    return [(q * step, step) for q in range(k)]


def _mod_sharded(c_row, w_own_bf, b_ada):
    n_own = w_own_bf.shape[1]
    assert N_CHIPS * n_own == b_ada.shape[1] and n_own % LANES == 0, (n_own, b_ada.shape)

    def kern(c_ref, w_ref, b_ref, m_ref, s_ref, rows, prods, got, send_sems, recv_sems):
        x, y, c, chips = _position()
        me = 2 * x + y

        def copy(k, src, dst, to):
            return pltpu.make_async_remote_copy(src_ref=src, dst_ref=dst, send_sem=send_sems.at[k],
                                                recv_sem=recv_sems.at[k], device_id=(*to, c), device_id_type=MESH_ID)

        number = [2 * cx + cy for cx, cy in chips]
        sc = _silu(c_ref[...])
        s_ref[...] = sc
        rows[me] = jnp.broadcast_to(sc, (SUBLANES, D_MODEL))
        out_rows = [copy(j, rows.at[me], rows.at[me], chip) for j, chip in enumerate(chips)]
        for cp in out_rows:
            cp.start()
        for j, chip in enumerate(chips):
            copy(j, rows.at[number[j]], rows.at[number[j]], chip).wait_recv()
        lhs = rows[...].reshape(N_CHIPS * SUBLANES, D_MODEL).astype(BF16)
        prods[...] = _dot(lhs, w_ref[...]).reshape(N_CHIPS, SUBLANES, n_own)
        got[me] = prods[me]
        out_prods = [copy(3 + j, prods.at[number[j]], got.at[me], chip) for j, chip in enumerate(chips)]
        for cp in out_prods:
            cp.start()
        for j, chip in enumerate(chips):
            copy(3 + j, got.at[number[j]], got.at[number[j]], chip).wait_recv()
        for cp in out_rows + out_prods:
            cp.wait_send()
        for k in range(N_CHIPS):
            cols = slice(k * n_own, (k + 1) * n_own)
            m_ref[:, cols] = got[k] + b_ref[:, cols]

    return pl.pallas_call(
        kern, name="ada_mod_sharded",
        out_shape=(jax.ShapeDtypeStruct((SUBLANES, 3 * D_MODEL), F32), jax.ShapeDtypeStruct((1, D_MODEL), F32)),
        scratch_shapes=[pltpu.VMEM((N_CHIPS, SUBLANES, D_MODEL), F32), pltpu.VMEM((N_CHIPS, SUBLANES, n_own), F32),
                        pltpu.VMEM((N_CHIPS, SUBLANES, n_own), F32), pltpu.SemaphoreType.DMA((6,)),
                        pltpu.SemaphoreType.DMA((6,))],
        compiler_params=_cparams())(c_row, w_own_bf, b_ada)


def _ag_weights_ride(packed, n_chunks=COMM_CHUNKS):
    rows, width = packed.shape
    half = rows // 2
    chunks = _row_chunks(half, n_chunks)
    nq = len(chunks)

    def parts(p_ref, out_ref, send_sems, recv_sems):
        x, y, c, chips = _position()
        sibling = (x, y, 1 - c)

        def copy(k, chip, h, q, to, src=None):
            start, size = chunks[q]
            rows_q = pl.ds(h * half + start, size)
            dst = out_ref.at[2 * chip[0] + chip[1], rows_q, :]
            return pltpu.make_async_remote_copy(
                src_ref=dst if src is None else src.at[rows_q, :], dst_ref=dst, send_sem=send_sems.at[k * nq + q],
                recv_sem=recv_sems.at[k * nq + q], device_id=to, device_id_type=MESH_ID)

        mine = [copy(6 + h, (x, y), h, q, sibling, src=p_ref) for h in range(2) for q in range(nq)]
        first = [copy(j, (x, y), c, q, (*chip, c), src=p_ref) for q in range(nq) for j, chip in enumerate(chips)]
        return (x, y, c), chips, sibling, copy, mine, first

    def start(ins, outs, sems):
        _, _, _, _, mine, first = parts(ins[0], outs[0], sems[0], sems[1])
        for cp in first + mine:
            cp.start()

    def wait(ins, outs, sems):
        (x, y, c), chips, sibling, copy, mine, first = parts(ins[0], outs[0], sems[0], sems[1])
        passed = []
        for q in range(nq):
            for j, chip in enumerate(chips):
                copy(j, chip, c, q, (x, y, c)).wait_recv()
                fwd = copy(3 + j, chip, c, q, sibling)
                fwd.start()
                passed.append(fwd)
        for q in range(nq):
            for j, chip in enumerate(chips):
                copy(3 + j, chip, 1 - c, q, (x, y, c)).wait_recv()
        for cp in mine:
            cp.wait_recv()
        for cp in first + passed + mine:
            cp.wait_send()

    return _Ride([packed], [jax.ShapeDtypeStruct((N_CHIPS, rows, width), packed.dtype)],
                 [pltpu.SemaphoreType.DMA((8 * nq,)), pltpu.SemaphoreType.DMA((8 * nq,))], start, wait)


def _join_rides(rides):
    def split(seq, counts):
        out, at = [], 0
        for n in counts:
            out.append(seq[at:at + n])
            at += n
        return out

    n_in = [len(r.inputs) for r in rides]
    n_out = [len(r.out_shapes) for r in rides]
    n_sem = [len(r.scratch) for r in rides]

    def start(ins, outs, sems):
        for r, i, o, s in zip(rides, split(ins, n_in), split(outs, n_out), split(sems, n_sem)):
            r.start(i, o, s)

    def wait(ins, outs, sems):
        for r, i, o, s in zip(rides, split(ins, n_in), split(outs, n_out), split(sems, n_sem)):
            r.wait(i, o, s)

    return _Ride([a for r in rides for a in r.inputs], [a for r in rides for a in r.out_shapes],
                 [a for r in rides for a in r.scratch], start, wait)


def _run_ride(ride, name):
    n_in, n_out = len(ride.inputs), len(ride.out_shapes)

    def body(*refs):
        ins, outs, sems = refs[:n_in], refs[n_in:n_in + n_out], refs[n_in + n_out:]
        ride.start(ins, outs, sems)
        ride.wait(ins, outs, sems)

    return pl.pallas_call(
        body, name=name, in_specs=[_ANY] * n_in, out_specs=(_ANY,) * n_out, out_shape=tuple(ride.out_shapes),
        scratch_shapes=list(ride.scratch))(*ride.inputs)


def _small_allgather_ride(buf):
    rows, width = buf.shape
    chunks = _row_chunks(rows, COMM_CHUNKS)
    nq = len(chunks)

    def parts(b_ref, all_ref, send_sems, recv_sems, local_sem):
        x, y, c, chips = _position()
        me, sibling = (x, y, c), (x, y, 1 - c)

        def copy(k, block, q, to, src=None):
            rows_q = pl.ds(chunks[q][0], chunks[q][1])
            dst = all_ref.at[4 * block[0] + 2 * block[1] + block[2], rows_q, :]
            return pltpu.make_async_remote_copy(
                src_ref=dst if src is None else src.at[rows_q, :], dst_ref=dst, send_sem=send_sems.at[k * nq + q],
                recv_sem=recv_sems.at[k * nq + q], device_id=to, device_id_type=MESH_ID)

        mine = pltpu.make_async_copy(b_ref, all_ref.at[4 * x + 2 * y + c], local_sem)
        first = []
        for q in range(nq):
            first += [copy(1 + j, me, q, (*chip, c), src=b_ref) for j, chip in enumerate(chips)]
            first.append(copy(0, me, q, sibling, src=b_ref))
        return me, sibling, c, chips, copy, mine, first

    def start(ins, outs, sems):
        _, _, _, _, _, mine, first = parts(ins[0], outs[0], *sems)
        mine.start()
        for cp in first:
            cp.start()

    def wait(ins, outs, sems):
        me, sibling, c, chips, copy, mine, first = parts(ins[0], outs[0], *sems)
        passed = []
        for q in range(nq):
            for j, chip in enumerate(chips):
                copy(1 + j, (*chip, c), q, me).wait_recv()
                fwd = copy(4 + j, (*chip, c), q, sibling)
                fwd.start()
                passed.append(fwd)
        for q in range(nq):
            copy(0, sibling, q, me).wait_recv()
            for j, chip in enumerate(chips):
                copy(4 + j, (*chip, 1 - c), q, me).wait_recv()
        for cp in first + passed:
            cp.wait_send()
        mine.wait()

    return _Ride([buf], [jax.ShapeDtypeStruct((N_DEV, rows, width), F32)],
                 [pltpu.SemaphoreType.DMA((7 * nq,)), pltpu.SemaphoreType.DMA((7 * nq,)), pltpu.SemaphoreType.DMA],
                 start, wait)


def _sum_devices(blocks):
    n, rows, width = blocks.shape
    rb = rows // 2 if (rows // 2) % SUBLANES == 0 else rows

    def kern(b_ref, o_ref):
        total = b_ref[0]
        for d in range(1, n):
            total = total + b_ref[d]
        o_ref[...] = total

    return pl.pallas_call(
        kern, name="small_sum", grid=(rows // rb,), in_specs=[pl.BlockSpec((n, rb, width), lambda i: (0, i, 0))],
        out_specs=pl.BlockSpec((rb, width), lambda i: (i, 0)), out_shape=jax.ShapeDtypeStruct((rows, width), F32),
        compiler_params=_cparams("parallel"))(blocks)


def _small_allgather_sum(buf, head_rows, n_chunks=COMM_CHUNKS):
    rows, width = buf.shape
    chunks = _row_chunks(rows, n_chunks)
    nq = len(chunks)

    def body(b_ref, head_ref, sum_ref, all_ref, send_sems, recv_sems, local_sem):
        x, y, c, chips = _position()
        me, sibling = (x, y, c), (x, y, 1 - c)

        def slot(px, py, pc):
            return all_ref.at[4 * px + 2 * py + pc]

        def copy(k, block, q, to, src=None):
            rows_q = pl.ds(chunks[q][0], chunks[q][1])
            dst = slot(*block).at[rows_q, :]
            return pltpu.make_async_remote_copy(
                src_ref=dst if src is None else src.at[rows_q, :], dst_ref=dst, send_sem=send_sems.at[k * nq + q],
                recv_sem=recv_sems.at[k * nq + q], device_id=to, device_id_type=MESH_ID)

        mine = pltpu.make_async_copy(b_ref, slot(*me), local_sem)
        mine.start()
        first = []
        for q in range(nq):
            first += [copy(1 + j, me, q, (*chip, c), src=b_ref) for j, chip in enumerate(chips)]
            first.append(copy(0, me, q, sibling, src=b_ref))
        for cp in first:
            cp.start()
        passed = []
        for q in range(nq):
            for j, chip in enumerate(chips):
                copy(1 + j, (*chip, c), q, me).wait_recv()
                fwd = copy(4 + j, (*chip, c), q, sibling)
                fwd.start()
                passed.append(fwd)
        for q in range(nq):
            copy(0, sibling, q, me).wait_recv()
            for j, chip in enumerate(chips):
                copy(4 + j, (*chip, 1 - c), q, me).wait_recv()
        for cp in first + passed:
            cp.wait_send()
        mine.wait()
        total = all_ref[0]
        for d in range(1, N_DEV):
            total = total + all_ref[d]
        sum_ref[...] = total
        head_ref[...] = all_ref[:, 0:head_rows, :]

    vm = pl.BlockSpec(memory_space=pltpu.VMEM)
    return pl.pallas_call(
        body, name="small_allgather_sum", in_specs=[vm], out_specs=(vm, vm),
        out_shape=(jax.ShapeDtypeStruct((N_DEV, head_rows, width), F32), jax.ShapeDtypeStruct((rows, width), F32)),
        scratch_shapes=[pltpu.VMEM((N_DEV, rows, width), F32), pltpu.SemaphoreType.DMA((7 * nq,)),
                        pltpu.SemaphoreType.DMA((7 * nq,)), pltpu.SemaphoreType.DMA],
        compiler_params=_cparams(),
    )(buf)


def _rs_pair(g):
    n, rows, width = g.shape
    half = rows // 2
    chunks = _row_chunks(half, COMM_CHUNKS)
    nq = len(chunks)

    def body(g_ref, got_ref, send_sems, recv_sems):
        x, y, c, _ = _position()
        swaps = []
        for k in range(n):
            for q, (start, size) in enumerate(chunks):
                swaps.append(pltpu.make_async_remote_copy(
                    src_ref=g_ref.at[k, pl.ds((1 - c) * half + start, size), :], dst_ref=got_ref.at[k, pl.ds(start, size), :],
                    send_sem=send_sems.at[k * nq + q], recv_sem=recv_sems.at[k * nq + q],
                    device_id=(x, y, 1 - c), device_id_type=MESH_ID))
        for cp in swaps:
            cp.start()
        for cp in swaps:
            cp.wait()

    return pl.pallas_call(
        body, name="rs_pair", in_specs=[_ANY], out_specs=_ANY, out_shape=jax.ShapeDtypeStruct((n, half, width), g.dtype),
        scratch_shapes=[pltpu.SemaphoreType.DMA((n * nq,)), pltpu.SemaphoreType.DMA((n * nq,))],
    )(g)


def _rs_chips_ride(part_bf):
    n, rows, width = part_bf.shape
    chunks = _row_chunks(rows, COMM_CHUNKS)
    nq = len(chunks)

    def sends(pb_ref, got_ref, send_sems, recv_sems):
        x, y, c, chips = _position()
        out = []
        for q, (start, size) in enumerate(chunks):
            for j, chip in enumerate(chips):
                out.append(pltpu.make_async_remote_copy(
                    src_ref=pb_ref.at[2 * chip[0] + chip[1], pl.ds(start, size), :], dst_ref=got_ref.at[j, pl.ds(start, size), :],
                    send_sem=send_sems.at[j * nq + q], recv_sem=recv_sems.at[j * nq + q],
                    device_id=(*chip, c), device_id_type=MESH_ID))
        return out

    def start(ins, outs, sems):
        for cp in sends(ins[0], outs[0], sems[0], sems[1]):
            cp.start()

    def wait(ins, outs, sems):
        for cp in sends(ins[0], outs[0], sems[0], sems[1]):
            cp.wait()

    return _Ride([part_bf], [jax.ShapeDtypeStruct((N_CHIPS - 1, rows, width), BF16)],
                 [pltpu.SemaphoreType.DMA((3 * nq,)), pltpu.SemaphoreType.DMA((3 * nq,))], start, wait)


def _rs_join(shard):
    rows, width = shard.shape
    half = rows // 2
    chunks = _row_chunks(half, COMM_CHUNKS)
    nq = len(chunks)

    def body(in_ref, out_ref, send_sems, recv_sems):
        x, y, c, _ = _position()
        def swap(q, h):
            rows_q = pl.ds(h * half + chunks[q][0], chunks[q][1])
            return pltpu.make_async_remote_copy(
                src_ref=in_ref.at[rows_q, :], dst_ref=out_ref.at[rows_q, :], send_sem=send_sems.at[q],
                recv_sem=recv_sems.at[q], device_id=(x, y, 1 - c), device_id_type=MESH_ID)

        for q in range(nq):
            swap(q, c).start()
        for q in range(nq):
            swap(q, 1 - c).wait_recv()
        for q in range(nq):
            swap(q, c).wait_send()

    return pl.pallas_call(
        body, name="rs_join", in_specs=[_ANY], out_specs=_ANY, input_output_aliases={0: 0},
        out_shape=jax.ShapeDtypeStruct(shard.shape, shard.dtype),
        scratch_shapes=[pltpu.SemaphoreType.DMA((nq,)), pltpu.SemaphoreType.DMA((nq,))],
    )(shard)


def _pair_add(g, got, where):
    n, half, width = got.shape
    nb = 2
    rb = half // nb

    def kern(w_ref, a_ref, b_ref, f_ref, h_ref):
        s = a_ref[...] + b_ref[...]
        h_ref[...] = s.astype(BF16)

        @pl.when(pl.program_id(1) == w_ref[0])
        def _():
            f_ref[...] = s[0]

    spec = pl.BlockSpec((1, rb, width), lambda i, k, w_ref: (k, i, 0))
    return pl.pallas_call(
        kern, name="rs_pair_add",
        grid_spec=pltpu.PrefetchScalarGridSpec(
            num_scalar_prefetch=1, grid=(nb, n),
            in_specs=[pl.BlockSpec((1, rb, width), lambda i, k, w_ref: (k, w_ref[1] * nb + i, 0)), spec],
            out_specs=(pl.BlockSpec((rb, width), lambda i, k, w_ref: (i, 0)), spec)),
        out_shape=(jax.ShapeDtypeStruct((half, width), F32), jax.ShapeDtypeStruct(got.shape, BF16)),
        compiler_params=_cparams("parallel", "arbitrary"))(where, g, got)


def _chip_add(part_f32, got, where):
    rows, width = part_f32.shape
    nb = 2
    rb = rows // nb

    def kern(w_ref, a_ref, b_ref, o_ref):
        o_ref[...] = ((a_ref[...] + b_ref[0].astype(F32)) + b_ref[1].astype(F32)) + b_ref[2].astype(F32)

    return pl.pallas_call(
        kern, name="rs_chip_add",
        grid_spec=pltpu.PrefetchScalarGridSpec(
            num_scalar_prefetch=1, grid=(nb,),
            in_specs=[pl.BlockSpec((rb, width), lambda i, w_ref: (i, 0)),
                      pl.BlockSpec((N_CHIPS - 1, rb, width), lambda i, w_ref: (0, i, 0))],
            out_specs=pl.BlockSpec((rb, width), lambda i, w_ref: (w_ref[1] * nb + i, 0))),
        out_shape=jax.ShapeDtypeStruct((2 * rows, width), F32),
        compiler_params=_cparams("parallel"))(where, part_f32, got)


def _adamw(w, g, m, v, name):
    rows, width = w.shape
    rb = rows
    for cand in (512, 256, 128, 64, 32, 16, 8):
        if rows % cand == 0 and cand * width * 4 <= ADAM_BLOCK_BYTES:
            rb = cand
            break
    spec = pl.BlockSpec((rb, width), lambda i: (i, 0))

    def kern(w_ref, g_ref, m_ref, v_ref, d_ref, nm_ref, nv_ref):
        d_ref[...], nm_ref[...], nv_ref[...] = _adamw_update(w_ref[...], g_ref[...], m_ref[...], v_ref[...])

    shp = jax.ShapeDtypeStruct(w.shape, F32)
    return pl.pallas_call(
        kern, name=name, grid=(rows // rb,), in_specs=[spec] * 4, out_specs=(spec, spec, spec),
        out_shape=(shp, shp, shp), compiler_params=_cparams("parallel"))(w, g, m, v)


def _adamw_update(w, g, m, v):
    nm = ADAM_B1 * m + (1.0 - ADAM_B1) * g
    nv = ADAM_B2 * v + (1.0 - ADAM_B2) * (g * g)
    m_hat = nm / (1.0 - ADAM_B1 ** ADAM_STEP)
    v_hat = nv / (1.0 - ADAM_B2 ** ADAM_STEP)
    return -ADAM_LR * (m_hat / (jnp.sqrt(v_hat) + ADAM_EPS) + ADAM_WD * w), nm, nv


def _adamw_small(params):
    n = len(params)

    def kern(*refs):
        ins, outs = refs[:4 * n], refs[4 * n:]
        for p in range(n):
            w_ref, g_ref, m_ref, v_ref = ins[4 * p:4 * p + 4]
            d, nm, nv = _adamw_update(w_ref[...], g_ref[...], m_ref[...], v_ref[...])
            outs[3 * p][...] = d
            outs[3 * p + 1][...] = nm
            outs[3 * p + 2][...] = nv

    flat = [a for group in params for a in group]
    shapes = [jax.ShapeDtypeStruct(group[0].shape, F32) for group in params for _ in range(3)]
    res = pl.pallas_call(kern, name="adamw_small", out_shape=tuple(shapes), compiler_params=_cparams())(*flat)
    return [tuple(res[3 * p:3 * p + 3]) for p in range(n)]


def _wada_grad(silu_t, dmod_cols):
    n = dmod_cols.shape[1]

    def kern(s_ref, d_ref, o_ref):
        acc = s_ref[:, 0:1] * d_ref[0:1, :]
        for b in range(1, N_DEV):
            acc = acc + s_ref[:, b:b + 1] * d_ref[b:b + 1, :]
        o_ref[...] = acc

    return pl.pallas_call(kern, name="wada_grad", out_shape=jax.ShapeDtypeStruct((D_MODEL, n), F32),
                          compiler_params=_cparams())(silu_t, dmod_cols)


def _rows(a, multiple):
    flat = a.reshape(-1)
    pad = (-flat.shape[0]) % (D_MODEL * multiple)
    if pad:
        flat = jnp.concatenate([flat, jnp.zeros((pad,), flat.dtype)])
    return flat.reshape(-1, D_MODEL)


def _part_rows(shape, multiple):
    return -(-int(np.prod(shape)) // (D_MODEL * multiple)) * multiple


def _pack_rows(parts, multiple, total_multiple=1):
    blocks = [_rows(p, multiple) for p in parts]
    pad = (-sum(b.shape[0] for b in blocks)) % total_multiple
    if pad:
        blocks.append(jnp.zeros((pad, D_MODEL), blocks[0].dtype))
    return jnp.concatenate(blocks, axis=0)


def _unpack_rows(buf, shapes, multiple):
    out, r = [], 0
    for shp in shapes:
        n = int(np.prod(shp))
        nr = _part_rows(shp, multiple)
        out.append(buf[r:r + nr].reshape(-1)[:n].reshape(shp))
        r += nr
    return out


def kernel(x, c, w_ada, b_ada, norm_pre, norm_post, w_in, pool_w, pool_scale, ssm_a_re, ssm_a_im, ssm_log_dt, ssm_b_re, ssm_b_im, ssm_c_re, ssm_c_im, ssm_d, glu_w, glu_b, w_branch_pool, w_branch_ssm, w_out, loss_target, m_w_ada, m_b_ada, m_norm_pre, m_norm_post, m_w_in, m_pool_w, m_pool_scale, m_ssm_a_re, m_ssm_a_im, m_ssm_log_dt, m_ssm_b_re, m_ssm_b_im, m_ssm_c_re, m_ssm_c_im, m_ssm_d, m_glu_w, m_glu_b, m_w_branch_pool, m_w_branch_ssm, m_w_out, v_w_ada, v_b_ada, v_norm_pre, v_norm_post, v_w_in, v_pool_w, v_pool_scale, v_ssm_a_re, v_ssm_a_im, v_ssm_log_dt, v_ssm_b_re, v_ssm_b_im, v_ssm_c_re, v_ssm_c_im, v_ssm_d, v_glu_w, v_glu_b, v_w_branch_pool, v_w_branch_ssm, v_w_out):
    n_ada = w_ada.shape[2]
    n_in = w_in.shape[2]
    n_row = glu_w.shape[1]
    n_pool = pool_w.shape[2]
    n_groups = pool_w.shape[1]

    w_ada_own = w_ada[0].astype(BF16)
    w_in_own = w_in[0].astype(BF16)
    w_in_ride = _ag_weights_ride(w_in_own)

    def unpack_w_in(g_in):
        return g_in.transpose(1, 0, 2).reshape(D_MODEL, N_CHIPS * n_in)
    pool_rows = n_groups * n_pool * POOL_GW // D_MODEL
    late_shards = [pool_w[0].reshape(n_groups * n_pool, POOL_GW), glu_w[0], w_branch_pool[0], w_branch_ssm[0], w_out[0]]
    late_ride = _join_rides([_ag_weights_ride(s.astype(BF16), n_chunks=2) for s in late_shards])

    def unpack_late(pool, *squares):
        pool = pool.reshape(N_CHIPS, n_groups, n_pool, POOL_GW).transpose(1, 0, 2, 3)
        return (pool.reshape(n_groups, POOL_GW, POOL_GW), *[s.reshape(D_MODEL, D_MODEL) for s in squares])

    chip = 2 * lax.axis_index("x") + lax.axis_index("y")
    core = lax.axis_index("c").astype(jnp.int32)
    kept = {}

    half_d = D_MODEL // 2
    assert n_in == D_MODEL + half_d, n_in

    def by_cols(blk):
        rest = blk[:, :, D_MODEL:]
        return jnp.concatenate(
            [blk[:, :, :D_MODEL], jnp.concatenate([rest[:, :half_d], rest[:, half_d:]], axis=2)], axis=1)

    def from_cols(packed):
        rest = packed[D_MODEL:]
        return jnp.concatenate(
            [packed[:D_MODEL], jnp.concatenate([rest[:, :half_d], rest[:, half_d:]], axis=0)], axis=1)

    def by_rows(a):
        return a.reshape(N_CHIPS, n_row, D_MODEL)

    def exchange_big(g):
        pool_by_chip = g["d_pool_w"].reshape(n_groups, N_CHIPS, n_pool, POOL_GW).transpose(1, 0, 2, 3)
        blocks = [by_cols(g["d_win"]), by_rows(g["d_glu_w"]), by_rows(g["d_wbp"]), by_rows(g["d_wbs"]),
                  by_rows(g["d_wout"]), pool_by_chip.reshape(N_CHIPS, pool_rows, D_MODEL)]
        pad = (-sum(b.shape[1] for b in blocks)) % (2 * COMM_CHUNKS * COMM_ROW_ALIGN)
        if pad:
            blocks.append(jnp.zeros((N_CHIPS, pad, D_MODEL), F32))
        g_packed = jnp.concatenate(blocks, axis=1)
        kept["part_f32"], part_bf = _pair_add(g_packed, _rs_pair(g_packed), jnp.stack([chip.astype(jnp.int32), core]))
        return _rs_chips_ride(part_bf)

    a_re, a_im, log_dt = ssm_a_re[0], ssm_a_im[0], ssm_log_dt[0].reshape(SSM_G, 1)
    b_re_t, b_im_t = ssm_b_re[0].transpose(2, 0, 1), ssm_b_im[0].transpose(2, 0, 1)
    early_names = ["dg2", "d_pscale", "d_glu_b", "d_dskip", "d_abar_re", "d_abar_im", "d_bb_re_t", "d_bb_im_t",
                   "d_c_re", "d_c_im"]

    def exchange_small(s):
        parts = [s[k] for k in early_names]
        kept["early_shapes"] = [p.shape for p in parts]
        return _small_allgather_ride(_pack_rows(parts, SUBLANES, COMM_CHUNKS * COMM_ROW_ALIGN))

    res = _local_step(x[0], c, loss_target[0], None, b_ada, norm_pre, norm_post, None, None, pool_scale,
                      a_re, a_im, log_dt, b_re_t, b_im_t, ssm_c_re[0], ssm_c_im[0], ssm_d[0], None, glu_b[0:1],
                      None, None, None,
                      split_proj=(w_in_own, chip.astype(jnp.int32).reshape(1), w_in_ride, unpack_w_in, late_ride, unpack_late),
                      ride_for_dw_in=exchange_small, ride_for_dh=exchange_big,
                      mod_fn=lambda c_row, bias: _mod_sharded(c_row, w_ada_own, bias))

    (all_early,) = res["rode_dw_in"]
    (g_norm_post, g_pscale, g_glu_b, g_dskip, s_abar_re, s_abar_im, s_bb_re, s_bb_im, g_c_re, g_c_im) = _unpack_rows(
        _sum_devices(all_early), kept["early_shapes"], SUBLANES)
    g_a_re, g_a_im, g_log_dt, g_b_re_t, g_b_im_t = _ssm_params_bwd(
        a_re, a_im, log_dt, b_re_t, b_im_t, s_abar_re.reshape(SSM_G, SSM_P), s_abar_im.reshape(SSM_G, SSM_P),
        s_bb_re, s_bb_im)
    late_parts = [res["dmod"], res["silu_c"], res["dg1"], res["loss"].reshape(1, 1)]
    late_shapes = [p.shape for p in late_parts]
    head_rows = _part_rows(late_shapes[0], SUBLANES) + _part_rows(late_shapes[1], SUBLANES)
    all_late, sum_late = _small_allgather_sum(_pack_rows(late_parts, SUBLANES, COMM_ROW_ALIGN), head_rows, n_chunks=1)
    g_b_ada, _, g_norm_pre, loss = _unpack_rows(sum_late, late_shapes, SUBLANES)
    loss = loss[0, 0]
    dmod_all = all_late[:, 0:3].reshape(N_DEV, 3 * D_MODEL)
    dmod_cols = lax.dynamic_slice_in_dim(dmod_all, chip * n_ada, n_ada, axis=1)
    silu_t = all_late[:, _part_rows(late_shapes[0], SUBLANES)].transpose(1, 0)
    g_w_ada = _wada_grad(silu_t, dmod_cols)

    (got_chips,) = res["rode"]
    shard = _rs_join(_chip_add(kept["part_f32"], got_chips, jnp.stack([chip.astype(jnp.int32), core])))
    r = 0
    g_w_in = from_cols(shard[r:r + n_in])
    r += n_in
    g_squares = []
    for _ in range(4):
        g_squares.append(shard[r:r + n_row])
        r += n_row
    g_glu_w, g_wbp, g_wbs, g_wout = g_squares
    g_pool_w = shard[r:r + pool_rows].reshape(n_groups * n_pool, POOL_GW)

    big = [("w_ada", w_ada[0], g_w_ada, m_w_ada[0], v_w_ada[0]),
           ("w_in", w_in[0], g_w_in, m_w_in[0], v_w_in[0]),
           ("pool_w", pool_w[0].reshape(n_groups * n_pool, POOL_GW), g_pool_w,
            m_pool_w[0].reshape(n_groups * n_pool, POOL_GW), v_pool_w[0].reshape(n_groups * n_pool, POOL_GW)),
           ("glu_w", glu_w[0], g_glu_w, m_glu_w[0], v_glu_w[0]),
           ("w_branch_pool", w_branch_pool[0], g_wbp, m_w_branch_pool[0], v_w_branch_pool[0]),
           ("w_branch_ssm", w_branch_ssm[0], g_wbs, m_w_branch_ssm[0], v_w_branch_ssm[0]),
           ("w_out", w_out[0], g_wout, m_w_out[0], v_w_out[0])]
    out = {}
    for name, w_, g_, m_, v_ in big:
        d_, nm_, nv_ = _adamw(w_, g_, m_, v_, "adamw_" + name)
        out[name] = (g_, d_, nm_, nv_)

    g_b_re = g_b_re_t.transpose(1, 2, 0)
    g_b_im = g_b_im_t.transpose(1, 2, 0)
    small = [("b_ada", b_ada, g_b_ada, m_b_ada, v_b_ada),
             ("norm_pre", norm_pre, g_norm_pre, m_norm_pre, v_norm_pre),
             ("norm_post", norm_post, g_norm_post, m_norm_post, v_norm_post),
             ("pool_scale", pool_scale, g_pscale, m_pool_scale, v_pool_scale),
             ("ssm_a_re", ssm_a_re, g_a_re, m_ssm_a_re, v_ssm_a_re),
             ("ssm_a_im", ssm_a_im, g_a_im, m_ssm_a_im, v_ssm_a_im),
             ("ssm_log_dt", ssm_log_dt, g_log_dt, m_ssm_log_dt, v_ssm_log_dt),
             ("ssm_b_re", ssm_b_re, g_b_re, m_ssm_b_re, v_ssm_b_re),
             ("ssm_b_im", ssm_b_im, g_b_im, m_ssm_b_im, v_ssm_b_im),
             ("ssm_c_re", ssm_c_re, g_c_re, m_ssm_c_re, v_ssm_c_re),
             ("ssm_c_im", ssm_c_im, g_c_im, m_ssm_c_im, v_ssm_c_im),
             ("ssm_d", ssm_d, g_dskip, m_ssm_d, v_ssm_d),
             ("glu_b", glu_b, g_glu_b, m_glu_b, v_glu_b)]
    small = [(name, w_, g_.reshape(w_.shape), m_, v_) for name, w_, g_, m_, v_ in small]
    updates = _adamw_small([t[1:] for t in small])
    for (name, _, g_, _, _), (d_, nm_, nv_) in zip(small, updates):
        out[name] = (g_, d_, nm_, nv_)

    order = ["w_ada", "b_ada", "norm_pre", "norm_post", "w_in", "pool_w", "pool_scale", "ssm_a_re", "ssm_a_im",
             "ssm_log_dt", "ssm_b_re", "ssm_b_im", "ssm_c_re", "ssm_c_im", "ssm_d", "glu_w", "glu_b", "w_branch_pool",
             "w_branch_ssm", "w_out"]
    ref_shape = dict(w_ada=w_ada.shape, w_in=w_in.shape, pool_w=pool_w.shape, glu_w=glu_w.shape,
                     w_branch_pool=w_branch_pool.shape, w_branch_ssm=w_branch_ssm.shape, w_out=w_out.shape)
    for name, w_, _, _, _ in small:
        ref_shape[name] = w_.shape
    results = [loss, res["grad_x"][None]]
    for k in range(4):
        results += [out[name][k].reshape(ref_shape[name]) for name in order]
    return tuple(results)
```
